```python
import jax, jax.numpy as jnp
from jax import lax
import numpy as np

D_MODEL = 1024
BATCH = 8
SEQ = 4096
DEPTH = 1

HG_HEADS = 8
HG_DK = 128
HG_DV = D_MODEL // HG_HEADS
HG_CHUNK = 64
RET_HEADS = 8
RET_DV = D_MODEL // RET_HEADS
RET_DK = RET_DV // 2
RET_CHUNK = 128
ROPE_BASE = 10000.0
EPS = 1e-6

HG_Q = HG_HEADS * HG_DK
HG_V = HG_HEADS * HG_DV
RET_Q = RET_HEADS * RET_DK
RET_V = RET_HEADS * RET_DV
SPLITS = [HG_Q, HG_Q, HG_V, HG_V, RET_Q, RET_Q, RET_V, RET_V, D_MODEL, D_MODEL]
D_IN = sum(SPLITS)

kernel_name = "hybrid_hgrn2_retention_gated_block"


def rms_norm(x, g):
    xf = x.astype(jnp.float32)
    y = xf * lax.rsqrt(jnp.mean(xf * xf, axis=-1, keepdims=True) + EPS)
    return (y * g.astype(jnp.float32)).astype(x.dtype)


def head_rms_norm(o, g):
    H, d = o.shape[-2], o.shape[-1]
    y = o * lax.rsqrt(jnp.mean(o * o, axis=-1, keepdims=True) + EPS)
    y = y * g.astype(jnp.float32).reshape(H, d)
    return y.reshape(o.shape[0], o.shape[1], H * d)


def rotary(t, pos):
    dk = t.shape[-1]
    inv_freq = 1.0 / (ROPE_BASE ** jnp.linspace(0.0, 1.0, dk // 2, dtype=jnp.float32))
    ang = pos.astype(jnp.float32)[:, None] * inv_freq[None, :]
    cos = jnp.cos(ang)[None, :, None, :]
    sin = jnp.sin(ang)[None, :, None, :]
    t1, t2 = t[..., : dk // 2], t[..., dk // 2:]
    return jnp.concatenate([t1 * cos - t2 * sin, t1 * sin + t2 * cos], axis=-1)


def hgrn2_chunkwise(q, fpre, v, lb):
    B, L, H, dk = q.shape
    dv = v.shape[-1]
    C = HG_CHUNK
    N = L // C
    q = jax.nn.silu(q)
    lbh = lb.astype(jnp.float32).reshape(H, dk)
    f = lbh + (1.0 - lbh) * jax.nn.sigmoid(fpre)
    k = 1.0 - f
    logf = jnp.log(f)

    def to_chunks(t):
        return t.reshape(B, N, C, H, t.shape[-1]).transpose(1, 0, 3, 2, 4)

    mask = jnp.tril(jnp.ones((C, C), dtype=bool))[:, :, None]

    def step(S, inp):
        qc, kc, lfc, vc = inp
        b = jnp.cumsum(lfc, axis=2)
        inter = jnp.einsum('bhtd,bhde->bhte', qc * jnp.exp(b), S)
        diff = b[:, :, :, None, :] - b[:, :, None, :, :]
        decay = jnp.exp(jnp.where(mask, diff, -jnp.inf))
        A = jnp.einsum('bhtsd,bhsd->bhts', qc[:, :, :, None, :] * decay, kc)
        intra = jnp.einsum('bhts,bhse->bhte', A, vc)
        b_last = b[:, :, -1:, :]
        S_new = jnp.exp(b_last[:, :, 0, :])[..., None] * S + jnp.einsum(
            'bhsd,bhse->bhde', kc * jnp.exp(b_last - b), vc)
        return S_new, inter + intra

    S0 = jnp.zeros((B, H, dk, dv), jnp.float32)
    _, o = lax.scan(step, S0, (to_chunks(q), to_chunks(k), to_chunks(logf), to_chunks(v)))
    return o.transpose(1, 0, 3, 2, 4).reshape(B, L, H, dv)


def retention_chunkwise(q, k, v):
    B, L, H, dk = q.shape
    dv = v.shape[-1]
    C = RET_CHUNK
    N = L // C
    log_gamma = jnp.log(1.0 - jnp.exp2(-5.0 - jnp.arange(H, dtype=jnp.float32)))
    k = k * (dk ** -0.5)
    qc = q.reshape(B, N, C, H, dk).transpose(0, 3, 1, 2, 4)
    kc = k.reshape(B, N, C, H, dk).transpose(0, 3, 1, 2, 4)
    vc = v.reshape(B, N, C, H, dv).transpose(0, 3, 1, 2, 4)
    idx = jnp.arange(C, dtype=jnp.float32)
    rel = idx[:, None] - idx[None, :]
    Dm = jnp.where(rel >= 0, jnp.exp(log_gamma[:, None, None] * jnp.maximum(rel, 0.0)), 0.0)
    scores = jnp.einsum('bhnid,bhnjd->bhnij', qc, kc) * Dm[:, None]
    intra = jnp.einsum('bhnij,bhnje->bhnie', scores, vc)
    zeta = jnp.exp(log_gamma[:, None] * (C - 1.0 - idx))
    chunk_state = jnp.einsum('bhnjd,bhnje->bhnde', kc * zeta[:, None, :, None], vc)
    chunk_decay = jnp.exp(log_gamma * C)[None, :, None, None]

    def step(R, s):
        return chunk_decay * R + s, R

    R0 = jnp.zeros((B, H, dk, dv), jnp.float32)
    _, R_prev = lax.scan(step, R0, chunk_state.transpose(2, 0, 1, 3, 4))
    R_prev = R_prev.transpose(1, 2, 0, 3, 4)
    xi = jnp.exp(log_gamma[:, None] * (idx + 1.0))
    cross = jnp.einsum('bhnid,bhnde->bhnie', qc * xi[:, None, :, None], R_prev)
    return (intra + cross).transpose(0, 2, 3, 1, 4).reshape(B, L, H, dv)


def _fwd_setup_inputs(seed: int = 0) -> dict:
    key = jax.random.key(seed)
    ks = jax.random.split(key, 12)
    f32 = jnp.float32
    x = jax.random.normal(ks[0], (BATCH, SEQ, D_MODEL), f32)
    c = jax.random.normal(ks[1], (BATCH, D_MODEL), f32)
    norm_g = 1.0 + 0.05 * jax.random.normal(ks[2], (DEPTH, D_MODEL), f32)
    w_ada = 0.5 * D_MODEL ** -0.5 * jax.random.normal(ks[3], (DEPTH, D_MODEL, 3 * D_MODEL), f32)
    b_ada = 0.02 * jax.random.normal(ks[4], (DEPTH, 3 * D_MODEL), f32)
    w_in = D_MODEL ** -0.5 * jax.random.normal(ks[5], (DEPTH, D_MODEL, D_IN), f32)
    hg_lb_logits = 0.5 * jax.random.normal(ks[6], (DEPTH + 1, HG_Q), f32)
    hg_norm_g = 1.0 + 0.05 * jax.random.normal(ks[7], (DEPTH, HG_V), f32)
    ret_norm_g = 1.0 + 0.05 * jax.random.normal(ks[8], (DEPTH, RET_V), f32)
    w_out = D_MODEL ** -0.5 * jax.random.normal(ks[9], (DEPTH, D_MODEL, D_MODEL), f32)
    final_g = 1.0 + 0.05 * jax.random.normal(ks[10], (D_MODEL,), f32)
    return {"x": x, "c": c, "norm_g": norm_g, "w_ada": w_ada, "b_ada": b_ada,
            "w_in": w_in, "hg_lb_logits": hg_lb_logits, "hg_norm_g": hg_norm_g,
            "ret_norm_g": ret_norm_g, "w_out": w_out, "final_g": final_g}


def _fwd_reference(x, c, norm_g, w_ada, b_ada, w_in, hg_lb_logits, hg_norm_g, ret_norm_g, w_out, final_g):
    B, L, D = x.shape
    pos = jnp.arange(L, dtype=jnp.int32)
    lower_bounds = jnp.cumsum(jax.nn.softmax(hg_lb_logits.astype(jnp.float32), axis=0), axis=0)
    offsets = np.cumsum([0] + SPLITS)[1:-1].tolist()
    for layer in range(DEPTH):
        mod = jax.nn.silu(c) @ w_ada[layer] + b_ada[layer]
        shift, scale, gate = jnp.split(mod[:, None, :], 3, axis=-1)
        h = rms_norm(x, norm_g[layer]) * (1.0 + scale) + shift
        proj = h @ w_in[layer]
        (hq, hf, hi, hz, rq, rk, rv, rz, ga, gb) = jnp.split(proj, offsets, axis=-1)
        f32 = jnp.float32
        oA = hgrn2_chunkwise(hq.astype(f32).reshape(B, L, HG_HEADS, HG_DK),
                             hf.astype(f32).reshape(B, L, HG_HEADS, HG_DK),
                             hi.astype(f32).reshape(B, L, HG_HEADS, HG_DV),
                             lower_bounds[layer])
        uA = head_rms_norm(oA, hg_norm_g[layer]) * jax.nn.silu(hz.astype(f32))
        qB = rotary(rq.astype(f32).reshape(B, L, RET_HEADS, RET_DK), pos)
        kB = rotary(rk.astype(f32).reshape(B, L, RET_HEADS, RET_DK), pos)
        oB = retention_chunkwise(qB, kB, rv.astype(f32).reshape(B, L, RET_HEADS, RET_DV))
        uB = head_rms_norm(oB, ret_norm_g[layer]) * jax.nn.silu(rz.astype(f32))
        m = (jax.nn.sigmoid(ga.astype(f32)) * uA + jax.nn.sigmoid(gb.astype(f32)) * uB).astype(x.dtype)
        x = x + gate * (m @ w_out[layer])
    return rms_norm(x, final_g)


import jax as _jax
import jax.numpy as _jnp

TWIN_FORMAT = 'train_step'
FWD_PARAMS = ['x', 'c', 'norm_g', 'w_ada', 'b_ada', 'w_in', 'hg_lb_logits', 'hg_norm_g', 'ret_norm_g', 'w_out', 'final_g']
TWIN_WEIGHTS = ['norm_g', 'w_ada', 'b_ada', 'w_in', 'hg_lb_logits', 'hg_norm_g', 'ret_norm_g', 'w_out', 'final_g']
TWIN_DIFF_INPUT = 'x'
TWIN_INPUTS = ['x', 'c', 'norm_g', 'w_ada', 'b_ada', 'w_in', 'hg_lb_logits', 'hg_norm_g', 'ret_norm_g', 'w_out', 'final_g', 'loss_target', 'm_norm_g', 'm_w_ada', 'm_b_ada', 'm_w_in', 'm_hg_lb_logits', 'm_hg_norm_g', 'm_ret_norm_g', 'm_w_out', 'm_final_g', 'v_norm_g', 'v_w_ada', 'v_b_ada', 'v_w_in', 'v_hg_lb_logits', 'v_hg_norm_g', 'v_ret_norm_g', 'v_w_out', 'v_final_g']
TWIN_OUTPUTS = ['loss', 'grad_x', 'grad_norm_g', 'grad_w_ada', 'grad_b_ada', 'grad_w_in', 'grad_hg_lb_logits', 'grad_hg_norm_g', 'grad_ret_norm_g', 'grad_w_out', 'grad_final_g', 'delta_norm_g', 'delta_w_ada', 'delta_b_ada', 'delta_w_in', 'delta_hg_lb_logits', 'delta_hg_norm_g', 'delta_ret_norm_g', 'delta_w_out', 'delta_final_g', 'new_m_norm_g', 'new_m_w_ada', 'new_m_b_ada', 'new_m_w_in', 'new_m_hg_lb_logits', 'new_m_hg_norm_g', 'new_m_ret_norm_g', 'new_m_w_out', 'new_m_final_g', 'new_v_norm_g', 'new_v_w_ada', 'new_v_b_ada', 'new_v_w_in', 'new_v_hg_lb_logits', 'new_v_hg_norm_g', 'new_v_ret_norm_g', 'new_v_w_out', 'new_v_final_g']
TWIN_LEAF_KINDS = {'loss': 'loss', 'grad_x': 'grad_x', 'grad_norm_g': 'grad_w', 'grad_w_ada': 'grad_w', 'grad_b_ada': 'grad_w', 'grad_w_in': 'grad_w', 'grad_hg_lb_logits': 'grad_w', 'grad_hg_norm_g': 'grad_w', 'grad_ret_norm_g': 'grad_w', 'grad_w_out': 'grad_w', 'grad_final_g': 'grad_w', 'delta_norm_g': 'delta_w', 'delta_w_ada': 'delta_w', 'delta_b_ada': 'delta_w', 'delta_w_in': 'delta_w', 'delta_hg_lb_logits': 'delta_w', 'delta_hg_norm_g': 'delta_w', 'delta_ret_norm_g': 'delta_w', 'delta_w_out': 'delta_w', 'delta_final_g': 'delta_w', 'new_m_norm_g': 'new_m', 'new_m_w_ada': 'new_m', 'new_m_b_ada': 'new_m', 'new_m_w_in': 'new_m', 'new_m_hg_lb_logits': 'new_m', 'new_m_hg_norm_g': 'new_m', 'new_m_ret_norm_g': 'new_m', 'new_m_w_out': 'new_m', 'new_m_final_g': 'new_m', 'new_v_norm_g': 'new_v', 'new_v_w_ada': 'new_v', 'new_v_b_ada': 'new_v', 'new_v_w_in': 'new_v', 'new_v_hg_lb_logits': 'new_v', 'new_v_hg_norm_g': 'new_v', 'new_v_ret_norm_g': 'new_v', 'new_v_w_out': 'new_v', 'new_v_final_g': 'new_v'}


def _forward(args):
    return _fwd_reference(*[args[k] for k in FWD_PARAMS])


def _output_shape():
    out = _jax.eval_shape(lambda: _forward(_fwd_setup_inputs(0)))
    return out.shape, out.dtype

N_MICROBATCH = 1
ADAM_LR = 0.001
ADAM_B1 = 0.9
ADAM_B2 = 0.999
ADAM_EPS = 1e-08
ADAM_WD = 0.01
ADAM_STEP = 10
PER_EXAMPLE_BATCH_AXIS = {'x': 0, 'c': 0, 'loss_target': 0}
SHARED_INPUTS = []
_WEIGHT_DTYPES = {'norm_g': _jnp.float32, 'w_ada': _jnp.float32, 'b_ada': _jnp.float32, 'w_in': _jnp.float32, 'hg_lb_logits': _jnp.float32, 'hg_norm_g': _jnp.float32, 'ret_norm_g': _jnp.float32, 'w_out': _jnp.float32, 'final_g': _jnp.float32}
MOMENT_SCALE = {'norm_g': 4.677132e-02, 'w_ada': 5.226089e-02, 'b_ada': 8.998368e-02, 'w_in': 1.727800e-02, 'hg_lb_logits': 1.433349e-03, 'hg_norm_g': 2.067494e-02, 'ret_norm_g': 2.016896e-02, 'w_out': 2.799794e-02, 'final_g': 3.203586e+01}


def _to_microbatches(a, axis):
    t = _jnp.moveaxis(a, axis, 0)
    t = t.reshape((N_MICROBATCH, t.shape[0] // N_MICROBATCH) + t.shape[1:])
    return _jnp.moveaxis(t, 1, axis + 1)


def setup_inputs(seed: int = 0) -> dict:
    inp = _fwd_setup_inputs(seed)
    key = _jax.random.fold_in(_jax.random.key(seed), 7919)
    shape, _ = _output_shape()
    out = dict(inp)
    out["loss_target"] = _jax.random.normal(_jax.random.fold_in(key, 0), shape, _jnp.float32)
    for i, name in enumerate(TWIN_WEIGHTS):
        w = inp[name].astype(_jnp.float32)
        if MOMENT_SCALE is None:
            s = _jnp.sqrt(_jnp.mean(_jnp.square(w)) + 1e-30)
        else:
            s = MOMENT_SCALE[name]
        km, kv = _jax.random.split(_jax.random.fold_in(key, i + 1))
        out[name] = w
        out["m_" + name] = s * _jax.random.normal(km, w.shape, _jnp.float32)
        out["v_" + name] = (s * s) * _jax.random.uniform(kv, w.shape, _jnp.float32, 0.5, 1.5)
    if N_MICROBATCH > 1:
        for name, axis in PER_EXAMPLE_BATCH_AXIS.items():
            out[name] = _to_microbatches(out[name], axis)
    return {'x': out['x'], 'c': out['c'], 'norm_g': out['norm_g'], 'w_ada': out['w_ada'], 'b_ada': out['b_ada'], 'w_in': out['w_in'], 'hg_lb_logits': out['hg_lb_logits'], 'hg_norm_g': out['hg_norm_g'], 'ret_norm_g': out['ret_norm_g'], 'w_out': out['w_out'], 'final_g': out['final_g'], 'loss_target': out['loss_target'], 'm_norm_g': out['m_norm_g'], 'm_w_ada': out['m_w_ada'], 'm_b_ada': out['m_b_ada'], 'm_w_in': out['m_w_in'], 'm_hg_lb_logits': out['m_hg_lb_logits'], 'm_hg_norm_g': out['m_hg_norm_g'], 'm_ret_norm_g': out['m_ret_norm_g'], 'm_w_out': out['m_w_out'], 'm_final_g': out['m_final_g'], 'v_norm_g': out['v_norm_g'], 'v_w_ada': out['v_w_ada'], 'v_b_ada': out['v_b_ada'], 'v_w_in': out['v_w_in'], 'v_hg_lb_logits': out['v_hg_lb_logits'], 'v_hg_norm_g': out['v_hg_norm_g'], 'v_ret_norm_g': out['v_ret_norm_g'], 'v_w_out': out['v_w_out'], 'v_final_g': out['v_final_g']}


def _loss(weights, diff, rest, loss_target):
    with _jax.named_scope("forward"):
        args = {**rest, TWIN_DIFF_INPUT: diff, **{k: w.astype(_WEIGHT_DTYPES[k]) for k, w in weights.items()}}
        y = _forward(args)
    with _jax.named_scope("loss_head"):
        err = _jnp.square(y.astype(_jnp.float32) - loss_target)
        return 0.5 * _jnp.sum(_jnp.mean(err, axis=-1)) if err.ndim else 0.5 * err


def _adamw(w, g, m, v):
    m = ADAM_B1 * m + (1.0 - ADAM_B1) * g
    v = ADAM_B2 * v + (1.0 - ADAM_B2) * _jnp.square(g)
    m_hat = m / (1.0 - ADAM_B1 ** ADAM_STEP)
    v_hat = v / (1.0 - ADAM_B2 ** ADAM_STEP)
    delta = -ADAM_LR * (m_hat / (_jnp.sqrt(v_hat) + ADAM_EPS) + ADAM_WD * w)
    return delta, m, v


def reference(x, c, norm_g, w_ada, b_ada, w_in, hg_lb_logits, hg_norm_g, ret_norm_g, w_out, final_g, loss_target, m_norm_g, m_w_ada, m_b_ada, m_w_in, m_hg_lb_logits, m_hg_norm_g, m_ret_norm_g, m_w_out, m_final_g, v_norm_g, v_w_ada, v_b_ada, v_w_in, v_hg_lb_logits, v_hg_norm_g, v_ret_norm_g, v_w_out, v_final_g):
    given = dict(x=x, c=c, norm_g=norm_g, w_ada=w_ada, b_ada=b_ada, w_in=w_in, hg_lb_logits=hg_lb_logits, hg_norm_g=hg_norm_g, ret_norm_g=ret_norm_g, w_out=w_out, final_g=final_g, loss_target=loss_target, m_norm_g=m_norm_g, m_w_ada=m_w_ada, m_b_ada=m_b_ada, m_w_in=m_w_in, m_hg_lb_logits=m_hg_lb_logits, m_hg_norm_g=m_hg_norm_g, m_ret_norm_g=m_ret_norm_g, m_w_out=m_w_out, m_final_g=m_final_g, v_norm_g=v_norm_g, v_w_ada=v_w_ada, v_b_ada=v_b_ada, v_w_in=v_w_in, v_hg_lb_logits=v_hg_lb_logits, v_hg_norm_g=v_hg_norm_g, v_ret_norm_g=v_ret_norm_g, v_w_out=v_w_out, v_final_g=v_final_g)
    weights = {n: given[n] for n in TWIN_WEIGHTS}
    shared = {n: given[n] for n in SHARED_INPUTS}
    per_example = {n: given[n] for n in ['x', 'c']}
    grad_fn = _jax.value_and_grad(_loss, argnums=(0, 1))

    def one_microbatch(ex, loss_target):
        ex = dict(ex)
        diff = ex.pop(TWIN_DIFF_INPUT)
        return grad_fn(weights, diff, {**shared, **ex}, loss_target)

    if N_MICROBATCH == 1:
        loss, (grad_w, grad_x) = one_microbatch(per_example, given["loss_target"])
    else:
        def body(carry, xs):
            loss_sum, grad_sum = carry
            l_k, (gw_k, gx_k) = one_microbatch(xs[0], xs[1])
            with _jax.named_scope("update"):
                return (loss_sum + l_k, _jax.tree.map(_jnp.add, grad_sum, gw_k)), gx_k

        init = (_jnp.zeros((), _jnp.float32), _jax.tree.map(_jnp.zeros_like, weights))
        (loss, grad_w), grad_x = _jax.lax.scan(body, init, (per_example, given["loss_target"]))
    with _jax.named_scope("update"):
        delta_w, new_m, new_v = {}, {}, {}
        for n in TWIN_WEIGHTS:
            delta_w[n], new_m[n], new_v[n] = _adamw(weights[n], grad_w[n], given["m_" + n], given["v_" + n])
    return (loss, grad_x, *[grad_w[n] for n in TWIN_WEIGHTS], *[delta_w[n] for n in TWIN_WEIGHTS],
            *[new_m[n] for n in TWIN_WEIGHTS], *[new_v[n] for n in TWIN_WEIGHTS])
```

```python
import functools

import numpy as np
import jax
import jax.numpy as jnp
from jax import lax
from jax.experimental import pallas as pl
from jax.experimental.pallas import tpu as pltpu

F32 = jnp.float32
_BF = jnp.bfloat16

D_MODEL = 1024
N_HEADS = 8
LANE = 128
RET_DK = 64
D_IN = 9216
N_DEV = 8
SHARD_IN = D_IN // N_DEV
SHARD_ADA = 3 * D_MODEL // N_DEV
SHARD_OUT = D_MODEL // N_DEV
N_CB = D_IN // LANE
CB_PER_SHARD = SHARD_IN // LANE
CHUNK = 128
N_LEVELS = 7
EPS = 1e-6
ROPE_BASE = 10000.0
CB_HQ, CB_HF, CB_HI, CB_HZ, CB_RQ, CB_RK, CB_RV, CB_RZ, CB_GA, CB_GB = 0, 8, 16, 24, 32, 36, 40, 48, 56, 64
VMEM_LIMIT = 56 * 1024 * 1024

ADAM_LR, ADAM_B1, ADAM_B2, ADAM_EPS, ADAM_WD, ADAM_STEP = 0.001, 0.9, 0.999, 1e-08, 0.01, 10

_NN = (((1,), (0,)), ((), ()))
_NT = (((1,), (1,)), ((), ()))
_TN = (((0,), (0,)), ((), ()))
MESH = pl.DeviceIdType.MESH


def _dot(a, b, dims=_NN):
    return lax.dot_general(a.astype(_BF), b.astype(_BF), dims, preferred_element_type=F32)


def _split2(a):
    hi = a.astype(_BF)
    lo = (a - hi.astype(F32)).astype(_BF)
    return jnp.concatenate([hi, lo], axis=1)


def _dot_sel(sel, a):
    n = a.shape[1]
    r = lax.dot_general(sel.astype(_BF), _split2(a), _NN, preferred_element_type=F32)
    return r[:, :n] + r[:, n:]


def _dot_f32(a, b):
    def pieces(v):
        p1 = v.astype(_BF)
        r1 = v - p1.astype(F32)
        p2 = r1.astype(_BF)
        p3 = (r1 - p2.astype(F32)).astype(_BF)
        return p1, p2, p3
    a1, a2, a3 = pieces(a)
    b1, b2, b3 = pieces(b)
    d = lambda u, v: lax.dot_general(u, v, _NN, preferred_element_type=F32)
    return ((d(a1, b3) + d(a2, b2) + d(a3, b1)) + (d(a1, b2) + d(a2, b1))) + d(a1, b1)


def _sigmoid(v):
    return 1.0 / (1.0 + jnp.exp(-v))


def _params(sem=None):
    return pltpu.CompilerParams(dimension_semantics=sem, vmem_limit_bytes=VMEM_LIMIT)


def _hgrn_consts():
    c, nl = CHUNK, N_LEVELS
    t = np.arange(c)[:, None]
    j = np.arange(c)[None, :]
    sel = [j <= t]
    masks = [j == t]
    for l in range(1, nl + 1):
        m = ((t >> l) << l) + (1 << (l - 1)) - 1
        sec = t > m
        sel.append(np.where(sec, (j > m) & (j <= t), (j > t) & (j <= m)))
        same = (t >> l) == (j >> l)
        masks.append(same & sec & (j <= m))
    sel.append(j > t)
    sel = np.concatenate(sel, 0).astype(np.float32)
    masks = np.stack(masks).astype(np.float32)
    return jnp.asarray(sel, _BF), jnp.asarray(masks, F32)


def _hgrn_chunk(hq, hf, hi, lbv, sel_ref, lvl_ref):
    c, nl = CHUNK, N_LEVELS
    sq = _sigmoid(hq)
    q = hq * sq
    sg = _sigmoid(hf)
    omlb = 1.0 - lbv
    f = lbv + omlb * sg
    k = omlb * (1.0 - sg)
    logf = jnp.log(f)
    e_all = _dot_sel(sel_ref[...], logf)
    eb = jnp.exp(e_all[0:c])
    erem = jnp.exp(e_all[(nl + 1) * c:(nl + 2) * c])
    ebc = jnp.exp(jnp.sum(logf, axis=0, keepdims=True))
    lev = [None] + [jnp.exp(e_all[l * c:(l + 1) * c]) for l in range(1, nl + 1)]
    return dict(sq=sq, q=q, sg=sg, omlb=omlb, f=f, k=k, v=hi, eb=eb, erem=erem, ebc=ebc, lev=lev)


def _hgrn_scores(a, lvl_ref):
    q, k = a["q"], a["k"]
    acc = lvl_ref[0] * _dot(q, k, _NT)
    for l in range(1, N_LEVELS + 1):
        e = a["lev"][l]
        acc = acc + lvl_ref[l] * _dot(q * e, k * e, _NT)
    return acc


def hgrn_forward(pb, lb, t_len):
    nc = t_len // CHUNK
    sel, lvl = _hgrn_consts()

    def body(hq_ref, hf_ref, hi_ref, lb_ref, sel_ref, lvl_ref, o_ref, ssave_ref, st_ref):
        st_ref[...] = jnp.zeros_like(st_ref)
        lbv = lb_ref[...]

        def chunk(ci, carry):
            r = pl.ds(pl.multiple_of(ci * CHUNK, CHUNK), CHUNK)
            a = _hgrn_chunk(hq_ref[r, :], hf_ref[r, :], hi_ref[r, :], lbv, sel_ref, lvl_ref)
            st = st_ref[...]
            ssave_ref[ci] = st
            o = _dot(a["q"] * a["eb"], st, _NT) + _dot(_hgrn_scores(a, lvl_ref), a["v"])
            o_ref[r, :] = o
            st_ref[...] = st * a["ebc"] + _dot(a["v"], a["k"] * a["erem"], _TN)
            return carry

        lax.fori_loop(0, nc, chunk, 0)

    blk = lambda base: pl.BlockSpec((None, t_len, LANE), lambda h, base=base: (base + h, 0, 0))
    return pl.pallas_call(
        body, name="hgrn_fwd", grid=(N_HEADS,),
        in_specs=[blk(CB_HQ), blk(CB_HF), blk(CB_HI),
                  pl.BlockSpec((1, LANE), lambda h: (0, h)),
                  pl.BlockSpec(sel.shape, lambda h: (0, 0)),
                  pl.BlockSpec(lvl.shape, lambda h: (0, 0, 0))],
        out_specs=[pl.BlockSpec((None, t_len, LANE), lambda h: (h, 0, 0)),
                   pl.BlockSpec((None, nc, LANE, LANE), lambda h: (h, 0, 0, 0))],
        out_shape=[jax.ShapeDtypeStruct((N_HEADS, t_len, LANE), F32),
                   jax.ShapeDtypeStruct((N_HEADS, nc, LANE, LANE), F32)],
        scratch_shapes=[pltpu.VMEM((LANE, LANE), F32)],
        compiler_params=_params(("arbitrary",)),
    )(pb, pb, pb, lb, sel, lvl)


def hgrn_backward(pb, lb, do, ssave, dpb, t_len):
    nc = t_len // CHUNK
    sel, lvl = _hgrn_consts()

    def body(hq_ref, hf_ref, hi_ref, lb_ref, do_ref, ssave_ref, sel_ref, lvl_ref, dpb_in,
             dpb_ref, dlb_ref, dq_buf, df_buf, di_buf, dst_ref, sems):
        del dpb_in
        h = pl.program_id(0)
        dst_ref[...] = jnp.zeros_like(dst_ref)
        dlb_ref[...] = jnp.zeros_like(dlb_ref)
        lbv = lb_ref[...]

        def chunk(i, carry):
            ci = nc - 1 - i
            r = pl.ds(pl.multiple_of(ci * CHUNK, CHUNK), CHUNK)
            hq = hq_ref[r, :]
            a = _hgrn_chunk(hq, hf_ref[r, :], hi_ref[r, :], lbv, sel_ref, lvl_ref)
            q, k, v = a["q"], a["k"], a["v"]
            g = do_ref[r, :]
            st0 = ssave_ref[ci]
            dst = dst_ref[...]
            scores = _hgrn_scores(a, lvl_ref)
            da = _dot(g, v, _NT)
            kb = k * a["erem"]
            qb = q * a["eb"]
            dv = _dot(scores, g, _TN) + _dot(kb, dst, _NT)
            dq_inter = _dot(g, st0) * a["eb"]
            dk_state = _dot(v, dst) * a["erem"]
            da0 = lvl_ref[0] * da
            dq = dq_inter + _dot(da0, k)
            dk = dk_state + _dot(da0, q, _TN)
            de = [q * dq_inter]
            for l in range(1, N_LEVELS + 1):
                e = a["lev"][l]
                dal = lvl_ref[l] * da
                dql = _dot(dal, k * e) * e
                dkl = _dot(dal, q * e, _TN) * e
                dq = dq + dql
                dk = dk + dkl
                de.append(q * dql + k * dkl)
            de.append(k * dk_state)
            dst_ref[...] = dst * a["ebc"] + _dot(g, qb, _TN)
            dbc = jnp.sum(dst * st0, axis=0, keepdims=True) * a["ebc"]
            de2 = lax.dot_general(sel_ref[...], _split2(jnp.concatenate(de, axis=0)), _TN,
                                  preferred_element_type=F32)
            dlogf = de2[:, :LANE] + de2[:, LANE:] + dbc
            sq, sg = a["sq"], a["sg"]
            df = dlogf / a["f"] - dk
            dq_buf[r, :] = dq * (sq * (1.0 + hq * (1.0 - sq)))
            df_buf[r, :] = df * a["omlb"] * sg * (1.0 - sg)
            di_buf[r, :] = dv
            dlb_ref[...] = dlb_ref[...] + jnp.sum(df * (1.0 - sg), axis=0, keepdims=True)
            return carry

        lax.fori_loop(0, nc, chunk, 0)
        copies = [pltpu.make_async_copy(buf, dpb_ref.at[base + h], sems.at[n])
                  for n, (buf, base) in enumerate(((dq_buf, CB_HQ), (df_buf, CB_HF), (di_buf, CB_HI)))]
        for cp in copies:
            cp.start()
        for cp in copies:
            cp.wait()

    blk = lambda base: pl.BlockSpec((None, t_len, LANE), lambda h, base=base: (base + h, 0, 0))
    any_spec = pl.BlockSpec(memory_space=pl.ANY)
    return pl.pallas_call(
        body, name="hgrn_bwd", grid=(N_HEADS,),
        in_specs=[blk(CB_HQ), blk(CB_HF), blk(CB_HI),
                  pl.BlockSpec((1, LANE), lambda h: (0, h)),
                  pl.BlockSpec((None, t_len, LANE), lambda h: (h, 0, 0)),
                  pl.BlockSpec((None, nc, LANE, LANE), lambda h: (h, 0, 0, 0)),
                  pl.BlockSpec(sel.shape, lambda h: (0, 0)),
                  pl.BlockSpec(lvl.shape, lambda h: (0, 0, 0)),
                  any_spec],
        out_specs=[any_spec, pl.BlockSpec((1, LANE), lambda h: (0, h))],
        out_shape=[jax.ShapeDtypeStruct(dpb.shape, F32), jax.ShapeDtypeStruct((1, D_MODEL), F32)],
        scratch_shapes=[pltpu.VMEM((t_len, LANE), F32)] * 3 + [
            pltpu.VMEM((LANE, LANE), F32), pltpu.SemaphoreType.DMA((3,))],
        input_output_aliases={8: 0},
        compiler_params=_params(("arbitrary",)),
    )(pb, pb, pb, lb, do, ssave, sel, lvl, dpb)


def _rope_tables(t_len):
    half = RET_DK // 2
    inv_freq = 1.0 / (ROPE_BASE ** jnp.linspace(0.0, 1.0, half, dtype=F32))
    ang = jnp.arange(t_len, dtype=jnp.int32).astype(F32)[:, None] * inv_freq[None, :]
    cos, sin = jnp.cos(ang), jnp.sin(ang)
    cos_t = jnp.concatenate([cos, cos, cos, cos], axis=1)
    sin_t = jnp.concatenate([-sin, sin, -sin, sin], axis=1)
    return cos_t, sin_t


def _swap_halves(v):
    half = RET_DK // 2
    lane = lax.broadcasted_iota(jnp.int32, v.shape, 1)
    first = (lane & (RET_DK - 1)) < half
    return jnp.where(first, pltpu.roll(v, LANE - half, 1), pltpu.roll(v, half, 1))


def _ret_head_consts(hidx):
    c = CHUNK
    hf = jnp.full((1, LANE), hidx, jnp.int32).astype(F32)
    lg = jnp.log(1.0 - jnp.exp(-(5.0 + hf) * np.float32(np.log(2.0))))
    row = lax.broadcasted_iota(jnp.int32, (c, c), 0)
    col = lax.broadcasted_iota(jnp.int32, (c, c), 1)
    rel = (row - col).astype(F32)
    dm = jnp.where(rel >= 0, jnp.exp(lg[:, :1] * jnp.maximum(rel, 0.0)), 0.0)
    idx = lax.broadcasted_iota(jnp.int32, (c, LANE), 0).astype(F32)
    zeta = jnp.exp(lg * (c - 1.0 - idx))
    xi = jnp.exp(lg * (idx + 1.0))
    cdec = jnp.exp(lg * float(c))
    return dm, zeta, xi, cdec


def _lane_mask(which):
    lane = lax.broadcasted_iota(jnp.int32, (1, LANE), 1)
    return ((lane // RET_DK) == which).astype(F32)


def retention_forward(pb, cos_t, sin_t, t_len):
    nc = t_len // CHUNK

    def body(rq_ref, rk_ref, rv_ref, cos_ref, sin_ref, o_ref, rsave_ref, st_ref):
        p = pl.program_id(0)
        for a in range(2):
            dm, zeta, xi, cdec = _ret_head_consts(2 * p + a)
            lm = _lane_mask(a)
            st_ref[...] = jnp.zeros_like(st_ref)

            def chunk(ci, carry):
                r = pl.ds(pl.multiple_of(ci * CHUNK, CHUNK), CHUNK)
                cs, sn = cos_ref[r, :], sin_ref[r, :]
                q = rq_ref[r, :]
                k = rk_ref[r, :]
                q = (q * cs + _swap_halves(q) * sn) * lm
                k = (k * cs + _swap_halves(k) * sn) * (lm * RET_DK ** -0.5)
                v = rv_ref[a, r, :]
                st = st_ref[...]
                rsave_ref[a, ci] = st
                scores = _dot(q, k, _NT) * dm
                o_ref[a, r, :] = _dot(scores, v) + _dot(q * xi, st, _NT)
                st_ref[...] = st * cdec + _dot(v, k * zeta, _TN)
                return carry

            lax.fori_loop(0, nc, chunk, 0)

    return pl.pallas_call(
        body, name="ret_fwd", grid=(N_HEADS // 2,),
        in_specs=[pl.BlockSpec((None, t_len, LANE), lambda p: (CB_RQ + p, 0, 0)),
                  pl.BlockSpec((None, t_len, LANE), lambda p: (CB_RK + p, 0, 0)),
                  pl.BlockSpec((2, t_len, LANE), lambda p: (CB_RV // 2 + p, 0, 0)),
                  pl.BlockSpec((t_len, LANE), lambda p: (0, 0)),
                  pl.BlockSpec((t_len, LANE), lambda p: (0, 0))],
        out_specs=[pl.BlockSpec((2, t_len, LANE), lambda p: (p, 0, 0)),
                   pl.BlockSpec((2, nc, LANE, LANE), lambda p: (p, 0, 0, 0))],
        out_shape=[jax.ShapeDtypeStruct((N_HEADS, t_len, LANE), F32),
                   jax.ShapeDtypeStruct((N_HEADS, nc, LANE, LANE), F32)],
        scratch_shapes=[pltpu.VMEM((LANE, LANE), F32)],
        compiler_params=_params(("arbitrary",)),
    )(pb, pb, pb, cos_t, sin_t)


def retention_backward(pb, cos_t, sin_t, do, rsave, dpb, t_len):
    nc = t_len // CHUNK

    def body(rq_ref, rk_ref, rv_ref, cos_ref, sin_ref, do_ref, rsave_ref, dpb_in,
             dpb_ref, dq_buf, dk_buf, dv_buf, dst_ref, sems):
        del dpb_in
        p = pl.program_id(0)
        for a in range(2):
            dm, zeta, xi, cdec = _ret_head_consts(2 * p + a)
            lm = _lane_mask(a)
            dst_ref[...] = jnp.zeros_like(dst_ref)

            def chunk(i, carry):
                ci = nc - 1 - i
                r = pl.ds(pl.multiple_of(ci * CHUNK, CHUNK), CHUNK)
                cs, sn = cos_ref[r, :], sin_ref[r, :]
                q = rq_ref[r, :]
                k = rk_ref[r, :]
                q = (q * cs + _swap_halves(q) * sn) * lm
                k = (k * cs + _swap_halves(k) * sn) * (lm * RET_DK ** -0.5)
                v = rv_ref[a, r, :]
                g = do_ref[a, r, :]
                st0 = rsave_ref[a, ci]
                dst = dst_ref[...]
                scores = _dot(q, k, _NT) * dm
                dsc = _dot(g, v, _NT) * dm
                dq = _dot(dsc, k) + _dot(g, st0) * xi
                dk = _dot(dsc, q, _TN) + _dot(v, dst) * zeta
                dv_buf[a, r, :] = _dot(scores, g, _TN) + _dot(k * zeta, dst, _NT)
                dst_ref[...] = dst * cdec + _dot(g, q * xi, _TN)
                dk = dk * (RET_DK ** -0.5)
                dq = dq * cs - _swap_halves(dq) * sn
                dk = dk * cs - _swap_halves(dk) * sn
                if a == 0:
                    dq_buf[r, :] = dq
                    dk_buf[r, :] = dk
                else:
                    dq_buf[r, :] = dq_buf[r, :] + dq
                    dk_buf[r, :] = dk_buf[r, :] + dk
                return carry

            lax.fori_loop(0, nc, chunk, 0)
        copies = [pltpu.make_async_copy(dq_buf, dpb_ref.at[CB_RQ + p], sems.at[0]),
                  pltpu.make_async_copy(dk_buf, dpb_ref.at[CB_RK + p], sems.at[1]),
                  pltpu.make_async_copy(dv_buf, dpb_ref.at[pl.ds(CB_RV + 2 * p, 2)], sems.at[2])]
        for cp in copies:
            cp.start()
        for cp in copies:
            cp.wait()

    any_spec = pl.BlockSpec(memory_space=pl.ANY)
    return pl.pallas_call(
        body, name="ret_bwd", grid=(N_HEADS // 2,),
        in_specs=[pl.BlockSpec((None, t_len, LANE), lambda p: (CB_RQ + p, 0, 0)),
                  pl.BlockSpec((None, t_len, LANE), lambda p: (CB_RK + p, 0, 0)),
                  pl.BlockSpec((2, t_len, LANE), lambda p: (CB_RV // 2 + p, 0, 0)),
                  pl.BlockSpec((t_len, LANE), lambda p: (0, 0)),
                  pl.BlockSpec((t_len, LANE), lambda p: (0, 0)),
                  pl.BlockSpec((2, t_len, LANE), lambda p: (p, 0, 0)),
                  pl.BlockSpec((2, nc, LANE, LANE), lambda p: (p, 0, 0, 0)),
                  any_spec],
        out_specs=any_spec,
        out_shape=jax.ShapeDtypeStruct(dpb.shape, F32),
        scratch_shapes=[pltpu.VMEM((t_len, LANE), F32), pltpu.VMEM((t_len, LANE), F32),
                        pltpu.VMEM((2, t_len, LANE), F32), pltpu.VMEM((LANE, LANE), F32),
                        pltpu.SemaphoreType.DMA((3,))],
        input_output_aliases={7: 0},
        compiler_params=_params(("arbitrary",)),
    )(pb, pb, pb, cos_t, sin_t, do, rsave, dpb)


def _row_tile(t_len, want):
    return min(want, t_len)


def proj_forward(h, wg, t_len):
    tm = _row_tile(t_len, 512)

    def body(h_ref, w_ref, o_ref):
        acc = _dot(h_ref[...], w_ref[...])
        for jj in range(CB_PER_SHARD):
            o_ref[jj] = acc[:, jj * LANE:(jj + 1) * LANE]

    return pl.pallas_call(
        body, name="proj_fwd", grid=(N_DEV, t_len // tm),
        in_specs=[pl.BlockSpec((tm, D_MODEL), lambda j, i: (i, 0)),
                  pl.BlockSpec((None, D_MODEL, SHARD_IN), lambda j, i: (j, 0, 0))],
        out_specs=pl.BlockSpec((CB_PER_SHARD, tm, LANE), lambda j, i: (j, i, 0)),
        out_shape=jax.ShapeDtypeStruct((N_CB, t_len, LANE), F32),
        compiler_params=_params(("arbitrary", "arbitrary")),
    )(h, wg)


def proj_backward_input(dpb, wg, t_len):
    tm = _row_tile(t_len, 512)

    def body(a_ref, w_ref, o_ref):
        j = pl.program_id(1)
        a = jnp.concatenate([a_ref[jj].astype(_BF) for jj in range(CB_PER_SHARD)], axis=1)
        part = _dot(a, w_ref[...], _NT)

        @pl.when(j == 0)
        def _():
            o_ref[...] = part

        @pl.when(j > 0)
        def _():
            o_ref[...] = o_ref[...] + part

    return pl.pallas_call(
        body, name="proj_bwd_input", grid=(t_len // tm, N_DEV),
        in_specs=[pl.BlockSpec((CB_PER_SHARD, tm, LANE), lambda i, j: (j, i, 0)),
                  pl.BlockSpec((None, D_MODEL, SHARD_IN), lambda i, j: (j, 0, 0))],
        out_specs=pl.BlockSpec((tm, D_MODEL), lambda i, j: (i, 0)),
        out_shape=jax.ShapeDtypeStruct((t_len, D_MODEL), F32),
        compiler_params=_params(("arbitrary", "arbitrary")),
    )(dpb, wg)


def proj_backward_weight(h, dpb, t_len):
    tk = _row_tile(t_len, 512)

    def body(h_ref, b_ref, o_ref):
        k = pl.program_id(1)
        b = jnp.concatenate([b_ref[jj].astype(_BF) for jj in range(CB_PER_SHARD)], axis=1)
        part = _dot(h_ref[...], b, _TN)

        @pl.when(k == 0)
        def _():
            o_ref[...] = part

        @pl.when(k > 0)
        def _():
            o_ref[...] = o_ref[...] + part

    return pl.pallas_call(
        body, name="proj_bwd_weight", grid=(N_DEV, t_len // tk),
        in_specs=[pl.BlockSpec((tk, D_MODEL), lambda j, k: (k, 0)),
                  pl.BlockSpec((CB_PER_SHARD, tk, LANE), lambda j, k: (j, k, 0))],
        out_specs=pl.BlockSpec((None, D_MODEL, SHARD_IN), lambda j, k: (j, 0, 0)),
        out_shape=jax.ShapeDtypeStruct((N_DEV, D_MODEL, SHARD_IN), F32),
        compiler_params=_params(("arbitrary", "arbitrary")),
    )(h, dpb)


def out_forward(m, wout, t_len):
    tm = _row_tile(t_len, 512)

    def body(a_ref, w_ref, o_ref):
        o_ref[...] = _dot(a_ref[...], w_ref[...])

    return pl.pallas_call(
        body, name="out_fwd", grid=(t_len // tm,),
        in_specs=[pl.BlockSpec((tm, D_MODEL), lambda i: (i, 0)),
                  pl.BlockSpec((D_MODEL, D_MODEL), lambda i: (0, 0))],
        out_specs=pl.BlockSpec((tm, D_MODEL), lambda i: (i, 0)),
        out_shape=jax.ShapeDtypeStruct((t_len, D_MODEL), F32),
        compiler_params=_params(("arbitrary",)),
    )(m, wout)


def out_backward(m, dz, wout, t_len):
    tm = _row_tile(t_len, 512)

    def body(m_ref, dz_ref, w_ref, dm_ref, dw_ref):
        i = pl.program_id(0)
        dm_ref[...] = _dot(dz_ref[...], w_ref[...], _NT)
        part = _dot(m_ref[...], dz_ref[...], _TN)

        @pl.when(i == 0)
        def _():
            dw_ref[...] = part

        @pl.when(i > 0)
        def _():
            dw_ref[...] = dw_ref[...] + part

    return pl.pallas_call(
        body, name="out_bwd", grid=(t_len // tm,),
        in_specs=[pl.BlockSpec((tm, D_MODEL), lambda i: (i, 0)),
                  pl.BlockSpec((tm, D_MODEL), lambda i: (i, 0)),
                  pl.BlockSpec((D_MODEL, D_MODEL), lambda i: (0, 0))],
        out_specs=[pl.BlockSpec((tm, D_MODEL), lambda i: (i, 0)),
                   pl.BlockSpec((D_MODEL, D_MODEL), lambda i: (0, 0))],
        out_shape=[jax.ShapeDtypeStruct((t_len, D_MODEL), F32),
                   jax.ShapeDtypeStruct((D_MODEL, D_MODEL), F32)],
        compiler_params=_params(("arbitrary",)),
    )(m, dz, wout)


def _vec_spec():
    return pl.BlockSpec((1, D_MODEL), lambda i: (0, 0))


def _acc_rows(ref, i, rows):
    @pl.when(i == 0)
    def _():
        ref[...] = jnp.zeros_like(ref)

    for n, row in enumerate(rows):
        ref[n:n + 1, :] = ref[n:n + 1, :] + row


def adaln_forward(x, norm_g, scale1p, shift, t_len):
    tm = _row_tile(t_len, 512)

    def body(x_ref, g_ref, sc_ref, sh_ref, h_ref):
        xv = x_ref[...]
        r = lax.rsqrt(jnp.mean(xv * xv, axis=-1, keepdims=True) + EPS)
        h_ref[...] = (xv * r * g_ref[...] * sc_ref[...] + sh_ref[...]).astype(h_ref.dtype)

    return pl.pallas_call(
        body, name="adaln_fwd", grid=(t_len // tm,),
        in_specs=[pl.BlockSpec((tm, D_MODEL), lambda i: (i, 0)), _vec_spec(), _vec_spec(), _vec_spec()],
        out_specs=pl.BlockSpec((tm, D_MODEL), lambda i: (i, 0)),
        out_shape=jax.ShapeDtypeStruct((t_len, D_MODEL), _BF),
        compiler_params=_params(("arbitrary",)),
    )(x, norm_g, scale1p, shift)


def adaln_backward(dh, x, dy, norm_g, scale1p, t_len):
    tm = _row_tile(t_len, 512)

    def body(dh_ref, x_ref, dy_ref, g_ref, sc_ref, gx_ref, vec_ref):
        i = pl.program_id(0)
        xv, dhv = x_ref[...], dh_ref[...]
        g, sc = g_ref[...], sc_ref[...]
        r = lax.rsqrt(jnp.mean(xv * xv, axis=-1, keepdims=True) + EPS)
        xn = xv * r
        dxn = dhv * (g * sc)
        gx_ref[...] = dy_ref[...] + r * dxn - xn * (r * r) * jnp.mean(xv * dxn, axis=-1, keepdims=True)
        t = dhv * xn
        _acc_rows(vec_ref, i, [jnp.sum(t * sc, axis=0, keepdims=True),
                               jnp.sum(t * g, axis=0, keepdims=True),
                               jnp.sum(dhv, axis=0, keepdims=True)])

    row = pl.BlockSpec((tm, D_MODEL), lambda i: (i, 0))
    return pl.pallas_call(
        body, name="adaln_bwd", grid=(t_len // tm,),
        in_specs=[row, row, row, _vec_spec(), _vec_spec()],
        out_specs=[row, pl.BlockSpec((8, D_MODEL), lambda i: (0, 0))],
        out_shape=[jax.ShapeDtypeStruct((t_len, D_MODEL), F32), jax.ShapeDtypeStruct((8, D_MODEL), F32)],
        compiler_params=_params(("arbitrary",)),
    )(dh, x, dy, norm_g, scale1p)


def _head_norm(o, g):
    r = lax.rsqrt(jnp.mean(o * o, axis=-1, keepdims=True) + EPS)
    return r, o * r * g


def _group_spec(tm, cb):
    return pl.BlockSpec((N_HEADS, tm, LANE), lambda i, cb=cb: (cb // N_HEADS, i, 0))


def merge_forward(oa, ob, pb, hg_g, ret_g, t_len):
    tm = _row_tile(t_len, 256)

    def body(oa_ref, ob_ref, hz_ref, rz_ref, ga_ref, gb_ref, hg_ref, rg_ref, m_ref):
        for hh in range(N_HEADS):
            ls = slice(hh * LANE, (hh + 1) * LANE)
            _, na = _head_norm(oa_ref[hh], hg_ref[:, ls])
            _, nb = _head_norm(ob_ref[hh], rg_ref[:, ls])
            hz, rz = hz_ref[hh], rz_ref[hh]
            ua = na * (hz * _sigmoid(hz))
            ub = nb * (rz * _sigmoid(rz))
            m_ref[:, ls] = (_sigmoid(ga_ref[hh]) * ua + _sigmoid(gb_ref[hh]) * ub).astype(m_ref.dtype)

    head = pl.BlockSpec((N_HEADS, tm, LANE), lambda i: (0, i, 0))
    return pl.pallas_call(
        body, name="merge_fwd", grid=(t_len // tm,),
        in_specs=[head, head, _group_spec(tm, CB_HZ), _group_spec(tm, CB_RZ), _group_spec(tm, CB_GA),
                  _group_spec(tm, CB_GB), _vec_spec(), _vec_spec()],
        out_specs=pl.BlockSpec((tm, D_MODEL), lambda i: (i, 0)),
        out_shape=jax.ShapeDtypeStruct((t_len, D_MODEL), _BF),
        compiler_params=_params(("arbitrary",)),
    )(oa, ob, pb, pb, pb, pb, hg_g, ret_g)


def merge_backward(dm, oa, ob, pb, hg_g, ret_g, t_len):
    tm = _row_tile(t_len, 256)

    def body(dm_ref, oa_ref, ob_ref, hz_ref, rz_ref, ga_ref, gb_ref, hg_ref, rg_ref,
             doa_ref, dob_ref, vec_ref, dpb_ref, bufs, sems):
        i = pl.program_id(0)
        dg_a, dg_b = [], []
        for hh in range(N_HEADS):
            ls = slice(hh * LANE, (hh + 1) * LANE)
            dmh = dm_ref[:, ls]
            for side, (o_ref, z_ref, gate_ref, g_ref, do_ref, acc) in enumerate((
                    (oa_ref, hz_ref, ga_ref, hg_ref, doa_ref, dg_a),
                    (ob_ref, rz_ref, gb_ref, rg_ref, dob_ref, dg_b))):
                o, z, gt, g = o_ref[hh], z_ref[hh], gate_ref[hh], g_ref[:, ls]
                r, n = _head_norm(o, g)
                sz = _sigmoid(z)
                silu = z * sz
                sgt = _sigmoid(gt)
                du = dmh * sgt
                bufs[2 + side, hh] = dmh * (n * silu) * (sgt * (1.0 - sgt))
                bufs[side, hh] = du * n * (sz * (1.0 + z * (1.0 - sz)))
                dn = du * silu
                acc.append(jnp.sum(dn * (o * r), axis=0, keepdims=True))
                gdn = dn * g
                do_ref[hh] = r * gdn - o * (r * r * r) * jnp.mean(o * gdn, axis=-1, keepdims=True)
        _acc_rows(vec_ref, i, [jnp.concatenate(dg_a, axis=1), jnp.concatenate(dg_b, axis=1)])
        rows = pl.ds(pl.multiple_of(i * tm, tm), tm)
        copies = [pltpu.make_async_copy(bufs.at[n], dpb_ref.at[pl.ds(cb, N_HEADS), rows], sems.at[n])
                  for n, cb in enumerate((CB_HZ, CB_RZ, CB_GA, CB_GB))]
        for cp in copies:
            cp.start()
        for cp in copies:
            cp.wait()

    head = pl.BlockSpec((N_HEADS, tm, LANE), lambda i: (0, i, 0))
    return pl.pallas_call(
        body, name="merge_bwd", grid=(t_len // tm,),
        in_specs=[pl.BlockSpec((tm, D_MODEL), lambda i: (i, 0)), head, head,
                  _group_spec(tm, CB_HZ), _group_spec(tm, CB_RZ), _group_spec(tm, CB_GA), _group_spec(tm, CB_GB),
                  _vec_spec(), _vec_spec()],
        out_specs=[head, head, pl.BlockSpec((8, D_MODEL), lambda i: (0, 0)), pl.BlockSpec(memory_space=pl.ANY)],
        out_shape=[jax.ShapeDtypeStruct((N_HEADS, t_len, LANE), F32),
                   jax.ShapeDtypeStruct((N_HEADS, t_len, LANE), F32),
                   jax.ShapeDtypeStruct((8, D_MODEL), F32),
                   jax.ShapeDtypeStruct((N_CB, t_len, LANE), F32)],
        scratch_shapes=[pltpu.VMEM((4, N_HEADS, tm, LANE), F32), pltpu.SemaphoreType.DMA((4,))],
        compiler_params=_params(("arbitrary",)),
    )(dm, oa, ob, pb, pb, pb, pb, hg_g, ret_g)


def tail(x, z, target, gate, final_g, t_len):
    tm = _row_tile(t_len, 512)

    def body(x_ref, z_ref, t_ref, gate_ref, fg_ref, dy_ref, dz_ref, vec_ref):
        i = pl.program_id(0)
        zv, gt, fg = z_ref[...], gate_ref[...], fg_ref[...]
        y = x_ref[...] + gt * zv
        r = lax.rsqrt(jnp.mean(y * y, axis=-1, keepdims=True) + EPS)
        yn = y * r
        err = yn * fg - t_ref[...]
        loss = 0.5 * jnp.sum(jnp.mean(err * err, axis=-1, keepdims=True), axis=0, keepdims=True)
        dout = err * (1.0 / D_MODEL)
        gd = dout * fg
        dy = r * gd - yn * (r * r) * jnp.mean(y * gd, axis=-1, keepdims=True)
        dy_ref[...] = dy
        dz_ref[...] = (dy * gt).astype(dz_ref.dtype)
        _acc_rows(vec_ref, i, [jnp.sum(dout * yn, axis=0, keepdims=True),
                               jnp.sum(dy * zv, axis=0, keepdims=True),
                               jnp.broadcast_to(loss, (1, D_MODEL))])

    row = pl.BlockSpec((tm, D_MODEL), lambda i: (i, 0))
    return pl.pallas_call(
        body, name="tail", grid=(t_len // tm,),
        in_specs=[row, row, row, _vec_spec(), _vec_spec()],
        out_specs=[row, row, pl.BlockSpec((8, D_MODEL), lambda i: (0, 0))],
        out_shape=[jax.ShapeDtypeStruct((t_len, D_MODEL), F32), jax.ShapeDtypeStruct((t_len, D_MODEL), _BF),
                   jax.ShapeDtypeStruct((8, D_MODEL), F32)],
        compiler_params=_params(("arbitrary",)),
    )(x, z, target, gate, final_g)


def device_step(x, target, mod, lb, wg, wout, norm_g, hg_g, ret_g, final_g):
    t_len = x.shape[0]
    shift, scale, gate = mod[:, :D_MODEL], mod[:, D_MODEL:2 * D_MODEL], mod[:, 2 * D_MODEL:]
    scale1p = 1.0 + scale
    cos_t, sin_t = _rope_tables(t_len)
    h = adaln_forward(x, norm_g, scale1p, shift, t_len)
    pb = proj_forward(h, wg, t_len)
    oa, ssave = hgrn_forward(pb, lb, t_len)
    ob, rsave = retention_forward(pb, cos_t, sin_t, t_len)
    m = merge_forward(oa, ob, pb, hg_g, ret_g, t_len)
    z = out_forward(m, wout, t_len)
    dy, dz, vec_tail = tail(x, z, target, gate, final_g, t_len)
    dm, dwout = out_backward(m, dz, wout, t_len)
    doa, dob, vec_merge, dpb = merge_backward(dm, oa, ob, pb, hg_g, ret_g, t_len)
    dpb, dlb = hgrn_backward(pb, lb, doa, ssave, dpb, t_len)
    dpb = retention_backward(pb, cos_t, sin_t, dob, rsave, dpb, t_len)
    dh = proj_backward_input(dpb, wg, t_len)
    dwin = proj_backward_weight(h, dpb, t_len)
    grad_x, vec_ada = adaln_backward(dh, x, dy, norm_g, scale1p, t_len)
    return grad_x, dwin, dwout, vec_tail, vec_merge, vec_ada, dlb


PACK_ROWS = 16
ROW_NORM_G, ROW_LB, ROW_HG_G, ROW_RET_G, ROW_FINAL_G, ROW_SHIFT, ROW_SCALE, ROW_GATE, ROW_LOSS = range(9)


def _mesh_pos():
    return lax.axis_index("x"), lax.axis_index("y"), lax.axis_index("c")


def _lin(pos):
    return 4 * pos[0] + 2 * pos[1] + pos[2]


def _xor_peer(pos, k):
    return tuple(1 - p if (k >> s) & 1 else p for p, s in zip(pos, (2, 1, 0)))


def _other_chips(pos):
    x, y, _ = pos
    return [(1 - x, y), (x, 1 - y), (1 - x, 1 - y)]


def _remote(src, dst, send_sem, recv_sem, to):
    return pltpu.make_async_remote_copy(src_ref=src, dst_ref=dst, send_sem=send_sem, recv_sem=recv_sem,
                                        device_id=to, device_id_type=MESH)


def pre_exchange(c, w_ada, b_ada, logits):
    def body(c_ref, wada_ref, bada_ref, logit_ref, mod_ref, scall_ref, lb_ref,
             cg_ref, modall_ref, parts_ref, send1, recv1, send2, recv2):
        pos = _mesh_pos()
        cv = c_ref[...]
        slot = lambda p: pl.ds(pl.multiple_of(8 * _lin(p), 8), 8)
        cg_ref[slot(pos), :] = jnp.broadcast_to(cv * _sigmoid(cv), (8, D_MODEL))
        lb_ref[...] = _sigmoid(logit_ref[0:1, :] - logit_ref[1:2, :])
        peers = [_xor_peer(pos, k) for k in range(1, N_DEV)]
        gather = [_remote(cg_ref.at[slot(pos)], cg_ref.at[slot(pos)], send1.at[n], recv1.at[n], p)
                  for n, p in enumerate(peers)]
        for cp in gather:
            cp.start()
        for n, p in enumerate(peers):
            _remote(cg_ref.at[slot(p)], cg_ref.at[slot(p)], send1.at[n], recv1.at[n], p).wait_recv()
        modall_ref[...] = _dot_f32(cg_ref[...], wada_ref[...])
        scatter = [_remote(modall_ref.at[slot(p)], parts_ref.at[slot(pos)], send2.at[n], recv2.at[n], p)
                   for n, p in enumerate(peers)]
        for cp in scatter:
            cp.start()
        parts_ref[slot(pos), :] = modall_ref[slot(pos), :]
        for n, p in enumerate(peers):
            _remote(modall_ref.at[slot(p)], parts_ref.at[slot(p)], send2.at[n], recv2.at[n], p).wait_recv()
        for cp in gather + scatter:
            cp.wait_send()
        for j in range(N_DEV):
            cols = slice(j * SHARD_ADA, (j + 1) * SHARD_ADA)
            mod_ref[:, cols] = parts_ref[8 * j:8 * j + 1, :] + bada_ref[:, cols]
            scall_ref[j:j + 1, :] = cg_ref[8 * j:8 * j + 1, :]

    vmem = pl.BlockSpec(memory_space=pltpu.VMEM)
    return pl.pallas_call(
        body, name="pre_exchange",
        in_specs=[vmem] * 4, out_specs=[vmem] * 3,
        out_shape=[jax.ShapeDtypeStruct((1, 3 * D_MODEL), F32), jax.ShapeDtypeStruct((N_DEV, D_MODEL), F32),
                   jax.ShapeDtypeStruct((1, D_MODEL), F32)],
        scratch_shapes=[pltpu.VMEM((N_DEV * 8, D_MODEL), F32), pltpu.VMEM((N_DEV * 8, SHARD_ADA), F32),
                        pltpu.VMEM((N_DEV * 8, SHARD_ADA), F32)] + [pltpu.SemaphoreType.DMA((N_DEV - 1,))] * 4,
        compiler_params=pltpu.CompilerParams(vmem_limit_bytes=VMEM_LIMIT),
    )(c, w_ada, b_ada, logits)


def weight_gather(win_sh, wout_sh):
    def body(win_ref, wout_ref, wg_ref, woutg_ref, send, recv, local):
        pos = _mesh_pos()
        x, y, c = pos
        sibling = (x, y, 1 - c)
        chips = _other_chips(pos)
        first, passed, mine = [], [], []
        for a, (src, out) in enumerate(((win_ref, wg_ref), (wout_ref, woutg_ref))):
            def copy(k, block, to, src_ref=None, a=a, out=out):
                dst = out.at[_lin(block)]
                return _remote(dst if src_ref is None else src_ref, dst, send.at[7 * a + k], recv.at[7 * a + k], to)
            mine.append(pltpu.make_async_copy(src, out.at[_lin(pos)], local.at[a]))
            first.append(copy(0, pos, sibling, src))
            first += [copy(1 + j, pos, (*chip, c), src) for j, chip in enumerate(chips)]
            passed.append([copy(4 + j, (*chip, c), sibling) for j, chip in enumerate(chips)])
        for cp in mine + first:
            cp.start()
        for a, out in enumerate((wg_ref, woutg_ref)):
            for j, chip in enumerate(chips):
                dst = out.at[_lin((*chip, c))]
                _remote(dst, dst, send.at[7 * a + 1 + j], recv.at[7 * a + 1 + j], pos).wait_recv()
                passed[a][j].start()
        for a, out in enumerate((wg_ref, woutg_ref)):
            dst = out.at[_lin(sibling)]
            _remote(dst, dst, send.at[7 * a], recv.at[7 * a], pos).wait_recv()
            for j, chip in enumerate(chips):
                dst = out.at[_lin((*chip, 1 - c))]
                _remote(dst, dst, send.at[7 * a + 4 + j], recv.at[7 * a + 4 + j], pos).wait_recv()
        for cp in first + passed[0] + passed[1]:
            cp.wait_send()
        for cp in mine:
            cp.wait()

    any_spec = pl.BlockSpec(memory_space=pl.ANY)
    return pl.pallas_call(
        body, name="weight_gather",
        in_specs=[any_spec, any_spec], out_specs=[any_spec, any_spec],
        out_shape=[jax.ShapeDtypeStruct((N_DEV,) + win_sh.shape, win_sh.dtype),
                   jax.ShapeDtypeStruct((N_DEV,) + wout_sh.shape, wout_sh.dtype)],
        scratch_shapes=[pltpu.SemaphoreType.DMA((14,)), pltpu.SemaphoreType.DMA((14,)),
                        pltpu.SemaphoreType.DMA((2,))],
    )(win_sh, wout_sh)


def grad_pair_exchange(g_in, g_out):
    def body(gin_ref, gout_ref, ra_ref, rb_ref, send, recv):
        pos = _mesh_pos()
        x, y, c = pos
        sibling = (x, y, 1 - c)
        copies = []
        for a, (src, dst) in enumerate(((gin_ref, ra_ref), (gout_ref, rb_ref))):
            for q in range(4):
                copies.append(_remote(src.at[2 * q + (1 - c)], dst.at[q], send.at[4 * a + q], recv.at[4 * a + q],
                                      sibling))
        for cp in copies:
            cp.start()
        for cp in copies:
            cp.wait_recv()
        for cp in copies:
            cp.wait_send()

    any_spec = pl.BlockSpec(memory_space=pl.ANY)
    return pl.pallas_call(
        body, name="grad_pair_exchange",
        in_specs=[any_spec, any_spec], out_specs=[any_spec, any_spec],
        out_shape=[jax.ShapeDtypeStruct((4,) + g_in.shape[1:], F32), jax.ShapeDtypeStruct((4,) + g_out.shape[1:], F32)],
        scratch_shapes=[pltpu.SemaphoreType.DMA((8,)), pltpu.SemaphoreType.DMA((8,))],
    )(g_in, g_out)


def pair_sum(g_in, ra, g_out, rb, c_idx):
    tr = 256

    def body(c_ref, gin_ref, ra_ref, gout_ref, rb_ref, sb_ref, sbo_ref):
        del c_ref
        sb_ref[...] = (gin_ref[...] + ra_ref[...]).astype(sb_ref.dtype)
        sbo_ref[...] = gout_ref[...] + rb_ref[...]

    n_i = D_MODEL // tr
    return pl.pallas_call(
        body, name="pair_sum",
        grid_spec=pltpu.PrefetchScalarGridSpec(
            num_scalar_prefetch=1, grid=(4, n_i),
            in_specs=[pl.BlockSpec((None, tr, SHARD_IN), lambda q, i, c: (2 * q + c[0], i, 0)),
                      pl.BlockSpec((None, tr, SHARD_IN), lambda q, i, c: (q, i, 0)),
                      pl.BlockSpec((None, SHARD_OUT // n_i, D_MODEL), lambda q, i, c: (2 * q + c[0], i, 0)),
                      pl.BlockSpec((None, SHARD_OUT // n_i, D_MODEL), lambda q, i, c: (q, i, 0))],
            out_specs=[pl.BlockSpec((None, tr, SHARD_IN), lambda q, i, c: (q, i, 0)),
                       pl.BlockSpec((None, SHARD_OUT // n_i, D_MODEL), lambda q, i, c: (q, i, 0))]),
        out_shape=[jax.ShapeDtypeStruct(ra.shape, _BF), jax.ShapeDtypeStruct(rb.shape, F32)],
        compiler_params=_params(("arbitrary", "arbitrary")),
    )(c_idx, g_in, ra, g_out, rb)


def grad_chip_exchange(sb, sbo, pack):
    def body(sb_ref, sbo_ref, pack_ref, rc_ref, rco_ref, packs_ref, send, recv, local, psend, precv):
        pos = _mesh_pos()
        x, y, c = pos
        me = _lin(pos)
        my_chip = 2 * x + y
        chips = _other_chips(pos)
        packs_ref[me] = pack_ref[...]
        peers = [_xor_peer(pos, k) for k in range(1, N_DEV)]
        gather = [_remote(packs_ref.at[me], packs_ref.at[me], psend.at[n], precv.at[n], p) for n, p in enumerate(peers)]
        mine, copies = [], []
        for a, (src, dst) in enumerate(((sb_ref, rc_ref), (sbo_ref, rco_ref))):
            mine.append(pltpu.make_async_copy(src.at[my_chip], dst.at[my_chip], local.at[a]))
            for j, chip in enumerate(chips):
                copies.append(_remote(src.at[2 * chip[0] + chip[1]], dst.at[my_chip], send.at[3 * a + j],
                                      recv.at[3 * a + j], (*chip, c)))
        for cp in gather + mine + copies:
            cp.start()
        for n, p in enumerate(peers):
            _remote(packs_ref.at[_lin(p)], packs_ref.at[_lin(p)], psend.at[n], precv.at[n], p).wait_recv()
        for a, (src, dst) in enumerate(((sb_ref, rc_ref), (sbo_ref, rco_ref))):
            for j, chip in enumerate(chips):
                slot = dst.at[2 * chip[0] + chip[1]]
                _remote(slot, slot, send.at[3 * a + j], recv.at[3 * a + j], pos).wait_recv()
        for cp in gather + copies:
            cp.wait_send()
        for cp in mine:
            cp.wait()

    any_spec = pl.BlockSpec(memory_space=pl.ANY)
    vmem = pl.BlockSpec(memory_space=pltpu.VMEM)
    return pl.pallas_call(
        body, name="grad_chip_exchange",
        in_specs=[any_spec, any_spec, vmem], out_specs=[any_spec, any_spec, vmem],
        out_shape=[jax.ShapeDtypeStruct(sb.shape, sb.dtype), jax.ShapeDtypeStruct(sbo.shape, sbo.dtype),
                   jax.ShapeDtypeStruct((N_DEV,) + pack.shape, F32)],
        scratch_shapes=[pltpu.SemaphoreType.DMA((6,)), pltpu.SemaphoreType.DMA((6,)), pltpu.SemaphoreType.DMA((2,)),
                        pltpu.SemaphoreType.DMA((N_DEV - 1,)), pltpu.SemaphoreType.DMA((N_DEV - 1,))],
    )(sb, sbo, pack)


def pack_rows(vec_tail, vec_merge, vec_ada, dlb):
    def body(tail_ref, merge_ref, ada_ref, dlb_ref, o_ref):
        rows = [ada_ref[0:1, :], dlb_ref[...], merge_ref[0:1, :], merge_ref[1:2, :], tail_ref[0:1, :],
                ada_ref[2:3, :], ada_ref[1:2, :], tail_ref[1:2, :], tail_ref[2:3, :]]
        o_ref[...] = jnp.zeros_like(o_ref)
        for n, row in enumerate(rows):
            o_ref[n:n + 1, :] = row

    vmem = pl.BlockSpec(memory_space=pltpu.VMEM)
    return pl.pallas_call(body, name="pack_rows", in_specs=[vmem] * 4, out_specs=vmem,
                          out_shape=jax.ShapeDtypeStruct((PACK_ROWS, D_MODEL), F32))(vec_tail, vec_merge, vec_ada, dlb)


def _adamw(w, g, m, v):
    m = ADAM_B1 * m + (1.0 - ADAM_B1) * g
    v = ADAM_B2 * v + (1.0 - ADAM_B2) * (g * g)
    m_hat = m / (1.0 - ADAM_B1 ** ADAM_STEP)
    v_hat = v / (1.0 - ADAM_B2 ** ADAM_STEP)
    delta = -ADAM_LR * (m_hat / (jnp.sqrt(v_hat) + ADAM_EPS) + ADAM_WD * w)
    return delta, m, v


def adam_shard(parts, w, m, v, name):
    rows, cols = w.shape
    tr = min(rows, 128)

    def body(p0, p1, p2, p3, w_ref, m_ref, v_ref, g_ref, d_ref, nm_ref, nv_ref):
        g = ((p0[...].astype(F32) + p1[...].astype(F32)) + p2[...].astype(F32)) + p3[...].astype(F32)
        g_ref[...] = g
        d_ref[...], nm_ref[...], nv_ref[...] = _adamw(w_ref[...], g, m_ref[...], v_ref[...])

    part = lambda q: pl.BlockSpec((None, tr, cols), lambda i, q=q: (q, i, 0))
    tile = pl.BlockSpec((tr, cols), lambda i: (i, 0))
    return pl.pallas_call(
        body, name=name, grid=(rows // tr,),
        in_specs=[part(0), part(1), part(2), part(3), tile, tile, tile],
        out_specs=[tile] * 4, out_shape=[jax.ShapeDtypeStruct(w.shape, F32)] * 4,
        compiler_params=_params(("arbitrary",)),
    )(parts, parts, parts, parts, w, m, v)


def adam_ada(sc_t, dmod_all, me_idx, w, m, v):
    def body(me_ref, sc_ref, dm_ref, w_ref, m_ref, v_ref, g_ref, d_ref, nm_ref, nv_ref):
        del me_ref
        g = _dot_f32(sc_ref[...], dm_ref[...])
        g_ref[...] = g
        d_ref[...], nm_ref[...], nv_ref[...] = _adamw(w_ref[...], g, m_ref[...], v_ref[...])

    full = pl.BlockSpec(w.shape, lambda i, me: (0, 0))
    return pl.pallas_call(
        body, name="adam_ada",
        grid_spec=pltpu.PrefetchScalarGridSpec(
            num_scalar_prefetch=1, grid=(1,),
            in_specs=[pl.BlockSpec(sc_t.shape, lambda i, me: (0, 0)),
                      pl.BlockSpec((LANE, SHARD_ADA), lambda i, me: (0, me[0])), full, full, full],
            out_specs=[full] * 4),
        out_shape=[jax.ShapeDtypeStruct(w.shape, F32)] * 4,
        compiler_params=_params(("arbitrary",)),
    )(me_idx, sc_t, dmod_all, w, m, v)


def adam_vectors(packs, lb, params, ms, vs):
    n = len(params)

    def body(*refs):
        packs_ref, lb_ref = refs[0], refs[1]
        w_refs, m_refs, v_refs = refs[2:2 + n], refs[2 + n:2 + 2 * n], refs[2 + 2 * n:2 + 3 * n]
        loss_ref = refs[2 + 3 * n]
        outs = refs[3 + 3 * n:3 + 7 * n]
        tot_ref = refs[3 + 7 * n]
        tot = packs_ref[0]
        for d in range(1, N_DEV):
            tot = tot + packs_ref[d]
        tot_ref[...] = tot
        row = lambda r: tot_ref[r:r + 1, :]
        lbv = lb_ref[...]
        dl0 = row(ROW_LB) * lbv * (1.0 - lbv)
        grads = [[row(ROW_NORM_G)],
                 [jnp.concatenate([row(ROW_SHIFT), row(ROW_SCALE), row(ROW_GATE)], axis=1)],
                 [dl0, -dl0],
                 [row(ROW_HG_G)], [row(ROW_RET_G)], [row(ROW_FINAL_G)]]
        loss_ref[...] = tot_ref[ROW_LOSS:ROW_LOSS + 1, 0:LANE]
        for j, g_rows in enumerate(grads):
            for r, g in enumerate(g_rows):
                rs = slice(r, r + 1)
                d, nm, nv = _adamw(w_refs[j][rs, :], g, m_refs[j][rs, :], v_refs[j][rs, :])
                outs[4 * j][rs, :] = g
                outs[4 * j + 1][rs, :] = d
                outs[4 * j + 2][rs, :] = nm
                outs[4 * j + 3][rs, :] = nv

    vmem = pl.BlockSpec(memory_space=pltpu.VMEM)
    out_shape = [jax.ShapeDtypeStruct((1, LANE), F32)]
    for w in params:
        out_shape += [jax.ShapeDtypeStruct(w.shape, F32)] * 4
    return pl.pallas_call(
        body, name="adam_vectors", in_specs=[vmem] * (2 + 3 * n), out_specs=[vmem] * len(out_shape),
        out_shape=out_shape, scratch_shapes=[pltpu.VMEM((PACK_ROWS, D_MODEL), F32)],
    )(packs, lb, *params, *ms, *vs)


def kernel(x, c, norm_g, w_ada, b_ada, w_in, hg_lb_logits, hg_norm_g, ret_norm_g, w_out, final_g, loss_target, m_norm_g, m_w_ada, m_b_ada, m_w_in, m_hg_lb_logits, m_hg_norm_g, m_ret_norm_g, m_w_out, m_final_g, v_norm_g, v_w_ada, v_b_ada, v_w_in, v_hg_lb_logits, v_hg_norm_g, v_ret_norm_g, v_w_out, v_final_g):
    pos = _mesh_pos()
    me_idx = jnp.reshape(_lin(pos), (1,)).astype(jnp.int32)
    c_idx = jnp.reshape(pos[2], (1,)).astype(jnp.int32)
    vec = lambda a: a.reshape(1, D_MODEL)

    mod, scall, lb = pre_exchange(c, w_ada[0], b_ada, hg_lb_logits)
    wg, woutg = weight_gather(w_in[0].astype(_BF), w_out[0].astype(_BF))
    grad_x, dwin, dwout, vec_tail, vec_merge, vec_ada, dlb = device_step(
        x[0], loss_target[0], mod, lb, wg, woutg.reshape(D_MODEL, D_MODEL), norm_g, hg_norm_g, ret_norm_g,
        vec(final_g))

    dwout = dwout.reshape(N_DEV, SHARD_OUT, D_MODEL)
    ra, rb = grad_pair_exchange(dwin, dwout)
    sb, sbo = pair_sum(dwin, ra, dwout, rb, c_idx)
    pack = pack_rows(vec_tail, vec_merge, vec_ada, dlb)
    rc, rco, packs = grad_chip_exchange(sb, sbo, pack)

    g_in, d_in, nm_in, nv_in = adam_shard(rc, w_in[0], m_w_in[0], v_w_in[0], "adam_w_in")
    g_out, d_out, nm_out, nv_out = adam_shard(rco, w_out[0], m_w_out[0], v_w_out[0], "adam_w_out")
    dmod_all = packs[:, ROW_SHIFT:ROW_GATE + 1, :].reshape(N_DEV, 3 * D_MODEL)
    dmod_all = jnp.pad(dmod_all, ((0, LANE - N_DEV), (0, 0)))
    sc_t = jnp.pad(scall.T, ((0, 0), (0, LANE - N_DEV)))
    g_ada, d_ada, nm_ada, nv_ada = adam_ada(sc_t, dmod_all, me_idx, w_ada[0], m_w_ada[0], v_w_ada[0])
    small = adam_vectors(
        packs, lb,
        (norm_g, b_ada, hg_lb_logits, hg_norm_g, ret_norm_g, vec(final_g)),
        (m_norm_g, m_b_ada, m_hg_lb_logits, m_hg_norm_g, m_ret_norm_g, vec(m_final_g)),
        (v_norm_g, v_b_ada, v_hg_lb_logits, v_hg_norm_g, v_ret_norm_g, vec(v_final_g)))
    loss = small[0][0, 0]
    (g_ng, d_ng, nm_ng, nv_ng), (g_b, d_b, nm_b, nv_b), (g_lb, d_lb, nm_lb, nv_lb), (g_hg, d_hg, nm_hg, nv_hg), \
        (g_rg, d_rg, nm_rg, nv_rg), (g_fg, d_fg, nm_fg, nv_fg) = [small[1 + 4 * j:5 + 4 * j] for j in range(6)]
    flat = lambda a: a.reshape(D_MODEL)

    def group(ng, ada, b, win, lbl, hg, rg, wo, fg):
        return (ng, ada[None], b, win[None], lbl, hg, rg, wo[None], flat(fg))

    return (loss, grad_x[None],
            *group(g_ng, g_ada, g_b, g_in, g_lb, g_hg, g_rg, g_out, g_fg),
            *group(d_ng, d_ada, d_b, d_in, d_lb, d_hg, d_rg, d_out, d_fg),
            *group(nm_ng, nm_ada, nm_b, nm_in, nm_lb, nm_hg, nm_rg, nm_out, nm_fg),
            *group(nv_ng, nv_ada, nv_b, nv_in, nv_lb, nv_hg, nv_rg, nv_out, nv_fg))
```

```python
import functools

import numpy as np
import jax
import jax.numpy as jnp
from jax import lax
from jax.experimental import pallas as pl
from jax.experimental.pallas import tpu as pltpu

F32 = jnp.float32
_BF = jnp.bfloat16

D_MODEL = 1024
N_HEADS = 8
LANE = 128
RET_DK = 64
D_IN = 9216
N_DEV = 8
SHARD_IN = D_IN // N_DEV
SHARD_ADA = 3 * D_MODEL // N_DEV
SHARD_OUT = D_MODEL // N_DEV
N_CB = D_IN // LANE
CB_PER_SHARD = SHARD_IN // LANE
CHUNK = 128
N_LEVELS = 7
EPS = 1e-6
ROPE_BASE = 10000.0
CB_HQ, CB_HF, CB_HI, CB_HZ, CB_RQ, CB_RK, CB_RV, CB_RZ, CB_GA, CB_GB = 0, 8, 16, 24, 32, 36, 40, 48, 56, 64
VMEM_LIMIT = 56 * 1024 * 1024

ADAM_LR, ADAM_B1, ADAM_B2, ADAM_EPS, ADAM_WD, ADAM_STEP = 0.001, 0.9, 0.999, 1e-08, 0.01, 10

_NN = (((1,), (0,)), ((), ()))
_NT = (((1,), (1,)), ((), ()))
_TN = (((0,), (0,)), ((), ()))
MESH = pl.DeviceIdType.MESH


def _dot(a, b, dims=_NN):
    return lax.dot_general(a.astype(_BF), b.astype(_BF), dims, preferred_element_type=F32)


def _split2(a):
    hi = a.astype(_BF)
    lo = (a - hi.astype(F32)).astype(_BF)
    return jnp.concatenate([hi, lo], axis=1)


def _dot_sel(sel, a):
    n = a.shape[1]
    r = lax.dot_general(sel.astype(_BF), _split2(a), _NN, preferred_element_type=F32)
    return r[:, :n] + r[:, n:]


def _dot_f32(a, b):
    def pieces(v):
        p1 = v.astype(_BF)
        r1 = v - p1.astype(F32)
        p2 = r1.astype(_BF)
        p3 = (r1 - p2.astype(F32)).astype(_BF)
        return p1, p2, p3
    a1, a2, a3 = pieces(a)
    b1, b2, b3 = pieces(b)
    d = lambda u, v: lax.dot_general(u, v, _NN, preferred_element_type=F32)
    return ((d(a1, b3) + d(a2, b2) + d(a3, b1)) + (d(a1, b2) + d(a2, b1))) + d(a1, b1)


def _sigmoid(v):
    return 1.0 / (1.0 + jnp.exp(-v))


def _params(sem=None):
    return pltpu.CompilerParams(dimension_semantics=sem, vmem_limit_bytes=VMEM_LIMIT)


def _hgrn_consts():
    c, nl = CHUNK, N_LEVELS
    t = np.arange(c)[:, None]
    j = np.arange(c)[None, :]
    sel = [j <= t]
    masks = [j == t]
    for l in range(1, nl + 1):
        m = ((t >> l) << l) + (1 << (l - 1)) - 1
        sec = t > m
        sel.append(np.where(sec, (j > m) & (j <= t), (j > t) & (j <= m)))
        same = (t >> l) == (j >> l)
        masks.append(same & sec & (j <= m))
    sel.append(j > t)
    sel = np.concatenate(sel, 0).astype(np.float32)
    masks = np.stack(masks).astype(np.float32)
    return jnp.asarray(sel, _BF), jnp.asarray(masks, F32)


def _hgrn_chunk(hq, hf, hi, lbv, sel_ref, lvl_ref):
    c, nl = CHUNK, N_LEVELS
    sq = _sigmoid(hq)
    q = hq * sq
    sg = _sigmoid(hf)
    omlb = 1.0 - lbv
    f = lbv + omlb * sg
    k = omlb * (1.0 - sg)
    logf = jnp.log(f)
    e_all = _dot_sel(sel_ref[...], logf)
    eb = jnp.exp(e_all[0:c])
    erem = jnp.exp(e_all[(nl + 1) * c:(nl + 2) * c])
    ebc = jnp.exp(jnp.sum(logf, axis=0, keepdims=True))
    lev = [None] + [jnp.exp(e_all[l * c:(l + 1) * c]) for l in range(1, nl + 1)]
    return dict(sq=sq, q=q, sg=sg, omlb=omlb, f=f, k=k, v=hi, eb=eb, erem=erem, ebc=ebc, lev=lev)


def _hgrn_scores(a, lvl_ref):
    q, k = a["q"], a["k"]
    acc = lvl_ref[0] * _dot(q, k, _NT)
    for l in range(1, N_LEVELS + 1):
        e = a["lev"][l]
        acc = acc + lvl_ref[l] * _dot(q * e, k * e, _NT)
    return acc


def _time_block(t_len):
    return min(t_len, 1024)


def hgrn_forward(pb, lb, t_len):
    nc = t_len // CHUNK
    tb = _time_block(t_len)
    ncb = tb // CHUNK
    sel, lvl = _hgrn_consts()

    def body(hq_ref, hf_ref, hi_ref, lb_ref, sel_ref, lvl_ref, o_ref, ssave_ref, st_ref):
        @pl.when(pl.program_id(1) == 0)
        def _():
            st_ref[...] = jnp.zeros_like(st_ref)

        def chunk(ci, carry):
            r = pl.ds(pl.multiple_of(ci * CHUNK, CHUNK), CHUNK)
            for hd in range(2):
                lbv = lb_ref[:, hd * LANE:(hd + 1) * LANE]
                a = _hgrn_chunk(hq_ref[hd, r, :], hf_ref[hd, r, :], hi_ref[hd, r, :], lbv, sel_ref, lvl_ref)
                st = st_ref[hd]
                ssave_ref[hd, ci] = st
                o_ref[hd, r, :] = _dot(a["q"] * a["eb"], st, _NT) + _dot(_hgrn_scores(a, lvl_ref), a["v"])
                st_ref[hd] = st * a["ebc"] + _dot(a["v"], a["k"] * a["erem"], _TN)
            return carry

        lax.fori_loop(0, ncb, chunk, 0)

    pair = lambda base: pl.BlockSpec((2, tb, LANE), lambda p, t, base=base: (base // 2 + p, t, 0))
    return pl.pallas_call(
        body, name="hgrn_fwd", grid=(N_HEADS // 2, t_len // tb),
        in_specs=[pair(CB_HQ), pair(CB_HF), pair(CB_HI),
                  pl.BlockSpec((1, 2 * LANE), lambda p, t: (0, p)),
                  pl.BlockSpec(sel.shape, lambda p, t: (0, 0)),
                  pl.BlockSpec(lvl.shape, lambda p, t: (0, 0, 0))],
        out_specs=[pl.BlockSpec((2, tb, LANE), lambda p, t: (p, t, 0)),
                   pl.BlockSpec((2, ncb, LANE, LANE), lambda p, t: (p, t, 0, 0))],
        out_shape=[jax.ShapeDtypeStruct((N_HEADS, t_len, LANE), F32),
                   jax.ShapeDtypeStruct((N_HEADS, nc, LANE, LANE), F32)],
        scratch_shapes=[pltpu.VMEM((2, LANE, LANE), F32)],
        compiler_params=_params(("arbitrary", "arbitrary")),
    )(pb, pb, pb, lb, sel, lvl)


def hgrn_backward(pb, lb, do, ssave, dpb, t_len):
    tb = _time_block(t_len)
    ncb, ntb = tb // CHUNK, t_len // tb
    sel, lvl = _hgrn_consts()

    def body(hq_ref, hf_ref, hi_ref, lb_ref, do_ref, ssave_ref, sel_ref, lvl_ref, dpb_in,
             dpb_ref, dlb_ref, dq_buf, df_buf, di_buf, dst_ref, sems):
        del dpb_in
        p, t = pl.program_id(0), pl.program_id(1)

        @pl.when(t == 0)
        def _():
            dst_ref[...] = jnp.zeros_like(dst_ref)
            dlb_ref[...] = jnp.zeros_like(dlb_ref)

        def chunk(i, carry):
            ci = ncb - 1 - i
            r = pl.ds(pl.multiple_of(ci * CHUNK, CHUNK), CHUNK)
            for hd in range(2):
                head_chunk(hd, ci, r)
            return carry

        def head_chunk(hd, ci, r):
            lbv = lb_ref[:, hd * LANE:(hd + 1) * LANE]
            hq = hq_ref[hd, r, :]
            a = _hgrn_chunk(hq, hf_ref[hd, r, :], hi_ref[hd, r, :], lbv, sel_ref, lvl_ref)
            q, k, v = a["q"], a["k"], a["v"]
            g = do_ref[hd, r, :]
            st0 = ssave_ref[hd, ci]
            dst = dst_ref[hd]
            scores = _hgrn_scores(a, lvl_ref)
            da = _dot(g, v, _NT)
            kb = k * a["erem"]
            qb = q * a["eb"]
            dv = _dot(scores, g, _TN) + _dot(kb, dst, _NT)
            dq_inter = _dot(g, st0) * a["eb"]
            dk_state = _dot(v, dst) * a["erem"]
            da0 = lvl_ref[0] * da
            dq = dq_inter + _dot(da0, k)
            dk = dk_state + _dot(da0, q, _TN)
            de = [q * dq_inter]
            for l in range(1, N_LEVELS + 1):
                e = a["lev"][l]
                dal = lvl_ref[l] * da
                dql = _dot(dal, k * e) * e
                dkl = _dot(dal, q * e, _TN) * e
                dq = dq + dql
                dk = dk + dkl
                de.append(q * dql + k * dkl)
            de.append(k * dk_state)
            dst_ref[hd] = dst * a["ebc"] + _dot(g, qb, _TN)
            dbc = jnp.sum(dst * st0, axis=0, keepdims=True) * a["ebc"]
            de2 = lax.dot_general(sel_ref[...], _split2(jnp.concatenate(de, axis=0)), _TN,
                                  preferred_element_type=F32)
            dlogf = de2[:, :LANE] + de2[:, LANE:] + dbc
            sq, sg = a["sq"], a["sg"]
            df = dlogf / a["f"] - dk
            dq_buf[hd, r, :] = dq * (sq * (1.0 + hq * (1.0 - sq)))
            df_buf[hd, r, :] = df * a["omlb"] * sg * (1.0 - sg)
            di_buf[hd, r, :] = dv
            cols = slice(hd * LANE, (hd + 1) * LANE)
            dlb_ref[:, cols] = dlb_ref[:, cols] + jnp.sum(df * (1.0 - sg), axis=0, keepdims=True)

        lax.fori_loop(0, ncb, chunk, 0)
        rows = pl.ds(pl.multiple_of((ntb - 1 - t) * tb, tb), tb)
        copies = [pltpu.make_async_copy(buf, dpb_ref.at[pl.ds(base + 2 * p, 2), rows], sems.at[n])
                  for n, (buf, base) in enumerate(((dq_buf, CB_HQ), (df_buf, CB_HF), (di_buf, CB_HI)))]
        for cp in copies:
            cp.start()
        for cp in copies:
            cp.wait()

    pair = lambda base: pl.BlockSpec((2, tb, LANE), lambda p, t, base=base: (base // 2 + p, ntb - 1 - t, 0))
    any_spec = pl.BlockSpec(memory_space=pl.ANY)
    return pl.pallas_call(
        body, name="hgrn_bwd", grid=(N_HEADS // 2, ntb),
        in_specs=[pair(CB_HQ), pair(CB_HF), pair(CB_HI),
                  pl.BlockSpec((1, 2 * LANE), lambda p, t: (0, p)),
                  pair(0),
                  pl.BlockSpec((2, ncb, LANE, LANE), lambda p, t: (p, ntb - 1 - t, 0, 0)),
                  pl.BlockSpec(sel.shape, lambda p, t: (0, 0)),
                  pl.BlockSpec(lvl.shape, lambda p, t: (0, 0, 0)),
                  any_spec],
        out_specs=[any_spec, pl.BlockSpec((1, 2 * LANE), lambda p, t: (0, p))],
        out_shape=[jax.ShapeDtypeStruct(dpb.shape, F32), jax.ShapeDtypeStruct((1, D_MODEL), F32)],
        scratch_shapes=[pltpu.VMEM((2, tb, LANE), F32)] * 3 + [
            pltpu.VMEM((2, LANE, LANE), F32), pltpu.SemaphoreType.DMA((3,))],
        input_output_aliases={8: 0},
        compiler_params=_params(("arbitrary", "arbitrary")),
    )(pb, pb, pb, lb, do, ssave, sel, lvl, dpb)


def _rope_tables(t_len):
    half = RET_DK // 2
    inv_freq = 1.0 / (ROPE_BASE ** jnp.linspace(0.0, 1.0, half, dtype=F32))
    ang = jnp.arange(t_len, dtype=jnp.int32).astype(F32)[:, None] * inv_freq[None, :]
    cos, sin = jnp.cos(ang), jnp.sin(ang)
    cos_t = jnp.concatenate([cos, cos, cos, cos], axis=1)
    sin_t = jnp.concatenate([-sin, sin, -sin, sin], axis=1)
    return cos_t, sin_t


def _swap_halves(v):
    half = RET_DK // 2
    lane = lax.broadcasted_iota(jnp.int32, v.shape, 1)
    first = (lane & (RET_DK - 1)) < half
    return jnp.where(first, pltpu.roll(v, LANE - half, 1), pltpu.roll(v, half, 1))


def _ret_head_consts(hidx):
    c = CHUNK
    hf = jnp.full((1, LANE), hidx, jnp.int32).astype(F32)
    lg = jnp.log(1.0 - jnp.exp(-(5.0 + hf) * np.float32(np.log(2.0))))
    row = lax.broadcasted_iota(jnp.int32, (c, c), 0)
    col = lax.broadcasted_iota(jnp.int32, (c, c), 1)
    rel = (row - col).astype(F32)
    dm = jnp.where(rel >= 0, jnp.exp(lg[:, :1] * jnp.maximum(rel, 0.0)), 0.0)
    idx = lax.broadcasted_iota(jnp.int32, (c, LANE), 0).astype(F32)
    zeta = jnp.exp(lg * (c - 1.0 - idx))
    xi = jnp.exp(lg * (idx + 1.0))
    cdec = jnp.exp(lg * float(c))
    return dm, zeta, xi, cdec


def _lane_mask(which):
    lane = lax.broadcasted_iota(jnp.int32, (1, LANE), 1)
    return ((lane // RET_DK) == which).astype(F32)


def retention_forward(pb, cos_t, sin_t, t_len):
    nc = t_len // CHUNK

    tb = _time_block(t_len)
    ncb = tb // CHUNK

    def body(rq_ref, rk_ref, rv_ref, cos_ref, sin_ref, o_ref, rsave_ref, st_ref):
        p = pl.program_id(0)

        @pl.when(pl.program_id(1) == 0)
        def _():
            st_ref[...] = jnp.zeros_like(st_ref)

        consts = [_ret_head_consts(2 * p + hd) for hd in range(2)]

        def chunk(ci, carry):
            r = pl.ds(pl.multiple_of(ci * CHUNK, CHUNK), CHUNK)
            cs, sn = cos_ref[r, :], sin_ref[r, :]
            q = rq_ref[r, :]
            k = rk_ref[r, :]
            q = q * cs + _swap_halves(q) * sn
            k = (k * cs + _swap_halves(k) * sn) * RET_DK ** -0.5
            for hd in range(2):
                dm, zeta, xi, cdec = consts[hd]
                lm = _lane_mask(hd)
                qh, kh = q * lm, k * lm
                v = rv_ref[hd, r, :]
                st = st_ref[hd]
                rsave_ref[hd, ci] = st
                scores = _dot(qh, kh, _NT) * dm
                o_ref[hd, r, :] = _dot(scores, v) + _dot(qh * xi, st, _NT)
                st_ref[hd] = st * cdec + _dot(v, kh * zeta, _TN)
            return carry

        lax.fori_loop(0, ncb, chunk, 0)

    return pl.pallas_call(
        body, name="ret_fwd", grid=(N_HEADS // 2, t_len // tb),
        in_specs=[pl.BlockSpec((None, tb, LANE), lambda p, t: (CB_RQ + p, t, 0)),
                  pl.BlockSpec((None, tb, LANE), lambda p, t: (CB_RK + p, t, 0)),
                  pl.BlockSpec((2, tb, LANE), lambda p, t: (CB_RV // 2 + p, t, 0)),
                  pl.BlockSpec((tb, LANE), lambda p, t: (t, 0)),
                  pl.BlockSpec((tb, LANE), lambda p, t: (t, 0))],
        out_specs=[pl.BlockSpec((2, tb, LANE), lambda p, t: (p, t, 0)),
                   pl.BlockSpec((2, ncb, LANE, LANE), lambda p, t: (p, t, 0, 0))],
        out_shape=[jax.ShapeDtypeStruct((N_HEADS, t_len, LANE), F32),
                   jax.ShapeDtypeStruct((N_HEADS, nc, LANE, LANE), F32)],
        scratch_shapes=[pltpu.VMEM((2, LANE, LANE), F32)],
        compiler_params=_params(("arbitrary", "arbitrary")),
    )(pb, pb, pb, cos_t, sin_t)


def retention_backward(pb, cos_t, sin_t, do, rsave, dpb, t_len):
    tb = _time_block(t_len)
    ncb, ntb = tb // CHUNK, t_len // tb

    def body(rq_ref, rk_ref, rv_ref, cos_ref, sin_ref, do_ref, rsave_ref, dpb_in,
             dpb_ref, dq_buf, dk_buf, dv_buf, dst_ref, sems):
        del dpb_in
        p, t = pl.program_id(0), pl.program_id(1)

        @pl.when(t == 0)
        def _():
            dst_ref[...] = jnp.zeros_like(dst_ref)

        consts = [_ret_head_consts(2 * p + hd) for hd in range(2)]

        def chunk(i, carry):
            ci = ncb - 1 - i
            r = pl.ds(pl.multiple_of(ci * CHUNK, CHUNK), CHUNK)
            cs, sn = cos_ref[r, :], sin_ref[r, :]
            q = rq_ref[r, :]
            k = rk_ref[r, :]
            q = q * cs + _swap_halves(q) * sn
            k = (k * cs + _swap_halves(k) * sn) * RET_DK ** -0.5
            dq, dk = None, None
            for hd in range(2):
                dm, zeta, xi, cdec = consts[hd]
                lm = _lane_mask(hd)
                qh, kh = q * lm, k * lm
                v = rv_ref[hd, r, :]
                g = do_ref[hd, r, :]
                st0 = rsave_ref[hd, ci]
                dst = dst_ref[hd]
                scores = _dot(qh, kh, _NT) * dm
                dsc = _dot(g, v, _NT) * dm
                dqh = _dot(dsc, kh) + _dot(g, st0) * xi
                dkh = _dot(dsc, qh, _TN) + _dot(v, dst) * zeta
                dv_buf[hd, r, :] = _dot(scores, g, _TN) + _dot(kh * zeta, dst, _NT)
                dst_ref[hd] = dst * cdec + _dot(g, qh * xi, _TN)
                dq = dqh if dq is None else dq + dqh
                dk = dkh if dk is None else dk + dkh
            dk = dk * (RET_DK ** -0.5)
            dq_buf[r, :] = dq * cs - _swap_halves(dq) * sn
            dk_buf[r, :] = dk * cs - _swap_halves(dk) * sn
            return carry

        lax.fori_loop(0, ncb, chunk, 0)
        rows = pl.ds(pl.multiple_of((ntb - 1 - t) * tb, tb), tb)
        copies = [pltpu.make_async_copy(dq_buf, dpb_ref.at[CB_RQ + p, rows], sems.at[0]),
                  pltpu.make_async_copy(dk_buf, dpb_ref.at[CB_RK + p, rows], sems.at[1]),
                  pltpu.make_async_copy(dv_buf, dpb_ref.at[pl.ds(CB_RV + 2 * p, 2), rows], sems.at[2])]
        for cp in copies:
            cp.start()
        for cp in copies:
            cp.wait()

    any_spec = pl.BlockSpec(memory_space=pl.ANY)
    return pl.pallas_call(
        body, name="ret_bwd", grid=(N_HEADS // 2, ntb),
        in_specs=[pl.BlockSpec((None, tb, LANE), lambda p, t: (CB_RQ + p, ntb - 1 - t, 0)),
                  pl.BlockSpec((None, tb, LANE), lambda p, t: (CB_RK + p, ntb - 1 - t, 0)),
                  pl.BlockSpec((2, tb, LANE), lambda p, t: (CB_RV // 2 + p, ntb - 1 - t, 0)),
                  pl.BlockSpec((tb, LANE), lambda p, t: (ntb - 1 - t, 0)),
                  pl.BlockSpec((tb, LANE), lambda p, t: (ntb - 1 - t, 0)),
                  pl.BlockSpec((2, tb, LANE), lambda p, t: (p, ntb - 1 - t, 0)),
                  pl.BlockSpec((2, ncb, LANE, LANE), lambda p, t: (p, ntb - 1 - t, 0, 0)),
                  any_spec],
        out_specs=any_spec,
        out_shape=jax.ShapeDtypeStruct(dpb.shape, F32),
        scratch_shapes=[pltpu.VMEM((tb, LANE), F32), pltpu.VMEM((tb, LANE), F32),
                        pltpu.VMEM((2, tb, LANE), F32), pltpu.VMEM((2, LANE, LANE), F32),
                        pltpu.SemaphoreType.DMA((3,))],
        input_output_aliases={7: 0},
        compiler_params=_params(("arbitrary", "arbitrary")),
    )(pb, pb, pb, cos_t, sin_t, do, rsave, dpb)


def _row_tile(t_len, want):
    return min(want, t_len)


def proj_forward(h, wg, t_len):
    tm = _row_tile(t_len, 512)

    def body(h_ref, w_ref, o_ref):
        acc = _dot(h_ref[...], w_ref[...])
        for jj in range(CB_PER_SHARD):
            o_ref[jj] = acc[:, jj * LANE:(jj + 1) * LANE]

    return pl.pallas_call(
        body, name="proj_fwd", grid=(N_DEV, t_len // tm),
        in_specs=[pl.BlockSpec((tm, D_MODEL), lambda j, i: (i, 0)),
                  pl.BlockSpec((None, D_MODEL, SHARD_IN), lambda j, i: (j, 0, 0))],
        out_specs=pl.BlockSpec((CB_PER_SHARD, tm, LANE), lambda j, i: (j, i, 0)),
        out_shape=jax.ShapeDtypeStruct((N_CB, t_len, LANE), F32),
        compiler_params=_params(("arbitrary", "arbitrary")),
    )(h, wg)


def proj_backward_input(dpb, wg, t_len):
    tm = _row_tile(t_len, 512)

    def body(a_ref, w_ref, o_ref):
        j = pl.program_id(1)
        a = jnp.concatenate([a_ref[jj].astype(_BF) for jj in range(CB_PER_SHARD)], axis=1)
        part = _dot(a, w_ref[...], _NT)

        @pl.when(j == 0)
        def _():
            o_ref[...] = part

        @pl.when(j > 0)
        def _():
            o_ref[...] = o_ref[...] + part

    return pl.pallas_call(
        body, name="proj_bwd_input", grid=(t_len // tm, N_DEV),
        in_specs=[pl.BlockSpec((CB_PER_SHARD, tm, LANE), lambda i, j: (j, i, 0)),
                  pl.BlockSpec((None, D_MODEL, SHARD_IN), lambda i, j: (j, 0, 0))],
        out_specs=pl.BlockSpec((tm, D_MODEL), lambda i, j: (i, 0)),
        out_shape=jax.ShapeDtypeStruct((t_len, D_MODEL), F32),
        compiler_params=_params(("arbitrary", "arbitrary")),
    )(dpb, wg)


def proj_backward_weight(h, dpb, t_len):
    tk = _row_tile(t_len, 512)

    def body(h_ref, b_ref, o_ref):
        k = pl.program_id(1)
        b = jnp.concatenate([b_ref[jj].astype(_BF) for jj in range(CB_PER_SHARD)], axis=1)
        part = _dot(h_ref[...], b, _TN)

        @pl.when(k == 0)
        def _():
            o_ref[...] = part

        @pl.when(k > 0)
        def _():
            o_ref[...] = o_ref[...] + part

    return pl.pallas_call(
        body, name="proj_bwd_weight", grid=(N_DEV, t_len // tk),
        in_specs=[pl.BlockSpec((tk, D_MODEL), lambda j, k: (k, 0)),
                  pl.BlockSpec((CB_PER_SHARD, tk, LANE), lambda j, k: (j, k, 0))],
        out_specs=pl.BlockSpec((None, D_MODEL, SHARD_IN), lambda j, k: (j, 0, 0)),
        out_shape=jax.ShapeDtypeStruct((N_DEV, D_MODEL, SHARD_IN), F32),
        compiler_params=_params(("arbitrary", "arbitrary")),
    )(h, dpb)


def out_forward(m, wout, t_len):
    tm = _row_tile(t_len, 512)

    def body(a_ref, w_ref, o_ref):
        o_ref[...] = _dot(a_ref[...], w_ref[...])

    return pl.pallas_call(
        body, name="out_fwd", grid=(t_len // tm,),
        in_specs=[pl.BlockSpec((tm, D_MODEL), lambda i: (i, 0)),
                  pl.BlockSpec((D_MODEL, D_MODEL), lambda i: (0, 0))],
        out_specs=pl.BlockSpec((tm, D_MODEL), lambda i: (i, 0)),
        out_shape=jax.ShapeDtypeStruct((t_len, D_MODEL), F32),
        compiler_params=_params(("arbitrary",)),
    )(m, wout)


def out_backward(m, dz, wout, t_len):
    tm = _row_tile(t_len, 512)

    def body(m_ref, dz_ref, w_ref, dm_ref, dw_ref):
        i = pl.program_id(0)
        dm_ref[...] = _dot(dz_ref[...], w_ref[...], _NT)
        part = _dot(m_ref[...], dz_ref[...], _TN)

        @pl.when(i == 0)
        def _():
            dw_ref[...] = part

        @pl.when(i > 0)
        def _():
            dw_ref[...] = dw_ref[...] + part

    return pl.pallas_call(
        body, name="out_bwd", grid=(t_len // tm,),
        in_specs=[pl.BlockSpec((tm, D_MODEL), lambda i: (i, 0)),
                  pl.BlockSpec((tm, D_MODEL), lambda i: (i, 0)),
                  pl.BlockSpec((D_MODEL, D_MODEL), lambda i: (0, 0))],
        out_specs=[pl.BlockSpec((tm, D_MODEL), lambda i: (i, 0)),
                   pl.BlockSpec((D_MODEL, D_MODEL), lambda i: (0, 0))],
        out_shape=[jax.ShapeDtypeStruct((t_len, D_MODEL), F32),
                   jax.ShapeDtypeStruct((D_MODEL, D_MODEL), F32)],
        compiler_params=_params(("arbitrary",)),
    )(m, dz, wout)


def _vec_spec():
    return pl.BlockSpec((1, D_MODEL), lambda i: (0, 0))


def _acc_rows(ref, i, rows):
    @pl.when(i == 0)
    def _():
        ref[...] = jnp.zeros_like(ref)

    for n, row in enumerate(rows):
        ref[n:n + 1, :] = ref[n:n + 1, :] + row


def adaln_forward(x, norm_g, scale1p, shift, t_len):
    tm = _row_tile(t_len, 512)

    def body(x_ref, g_ref, sc_ref, sh_ref, h_ref):
        xv = x_ref[...]
        r = lax.rsqrt(jnp.mean(xv * xv, axis=-1, keepdims=True) + EPS)
        h_ref[...] = (xv * r * g_ref[...] * sc_ref[...] + sh_ref[...]).astype(h_ref.dtype)

    return pl.pallas_call(
        body, name="adaln_fwd", grid=(t_len // tm,),
        in_specs=[pl.BlockSpec((tm, D_MODEL), lambda i: (i, 0)), _vec_spec(), _vec_spec(), _vec_spec()],
        out_specs=pl.BlockSpec((tm, D_MODEL), lambda i: (i, 0)),
        out_shape=jax.ShapeDtypeStruct((t_len, D_MODEL), _BF),
        compiler_params=_params(("arbitrary",)),
    )(x, norm_g, scale1p, shift)


def adaln_backward(dh, x, dy, norm_g, scale1p, t_len):
    tm = _row_tile(t_len, 512)

    def body(dh_ref, x_ref, dy_ref, g_ref, sc_ref, gx_ref, vec_ref):
        i = pl.program_id(0)
        xv, dhv = x_ref[...], dh_ref[...]
        g, sc = g_ref[...], sc_ref[...]
        r = lax.rsqrt(jnp.mean(xv * xv, axis=-1, keepdims=True) + EPS)
        xn = xv * r
        dxn = dhv * (g * sc)
        gx_ref[...] = dy_ref[...] + r * dxn - xn * (r * r) * jnp.mean(xv * dxn, axis=-1, keepdims=True)
        t = dhv * xn
        _acc_rows(vec_ref, i, [jnp.sum(t * sc, axis=0, keepdims=True),
                               jnp.sum(t * g, axis=0, keepdims=True),
                               jnp.sum(dhv, axis=0, keepdims=True)])

    row = pl.BlockSpec((tm, D_MODEL), lambda i: (i, 0))
    return pl.pallas_call(
        body, name="adaln_bwd", grid=(t_len // tm,),
        in_specs=[row, row, row, _vec_spec(), _vec_spec()],
        out_specs=[row, pl.BlockSpec((8, D_MODEL), lambda i: (0, 0))],
        out_shape=[jax.ShapeDtypeStruct((t_len, D_MODEL), F32), jax.ShapeDtypeStruct((8, D_MODEL), F32)],
        compiler_params=_params(("arbitrary",)),
    )(dh, x, dy, norm_g, scale1p)


def _head_norm(o, g):
    r = lax.rsqrt(jnp.mean(o * o, axis=-1, keepdims=True) + EPS)
    return r, o * r * g


def _group_spec(tm, cb):
    return pl.BlockSpec((N_HEADS, tm, LANE), lambda i, cb=cb: (cb // N_HEADS, i, 0))


def merge_forward(oa, ob, pb, hg_g, ret_g, t_len):
    tm = _row_tile(t_len, 256)

    def body(oa_ref, ob_ref, hz_ref, rz_ref, ga_ref, gb_ref, hg_ref, rg_ref, m_ref):
        for hh in range(N_HEADS):
            ls = slice(hh * LANE, (hh + 1) * LANE)
            _, na = _head_norm(oa_ref[hh], hg_ref[:, ls])
            _, nb = _head_norm(ob_ref[hh], rg_ref[:, ls])
            hz, rz = hz_ref[hh], rz_ref[hh]
            ua = na * (hz * _sigmoid(hz))
            ub = nb * (rz * _sigmoid(rz))
            m_ref[:, ls] = (_sigmoid(ga_ref[hh]) * ua + _sigmoid(gb_ref[hh]) * ub).astype(m_ref.dtype)

    head = pl.BlockSpec((N_HEADS, tm, LANE), lambda i: (0, i, 0))
    return pl.pallas_call(
        body, name="merge_fwd", grid=(t_len // tm,),
        in_specs=[head, head, _group_spec(tm, CB_HZ), _group_spec(tm, CB_RZ), _group_spec(tm, CB_GA),
                  _group_spec(tm, CB_GB), _vec_spec(), _vec_spec()],
        out_specs=pl.BlockSpec((tm, D_MODEL), lambda i: (i, 0)),
        out_shape=jax.ShapeDtypeStruct((t_len, D_MODEL), _BF),
        compiler_params=_params(("arbitrary",)),
    )(oa, ob, pb, pb, pb, pb, hg_g, ret_g)


def merge_backward(dm, oa, ob, pb, hg_g, ret_g, t_len):
    tm = _row_tile(t_len, 256)

    def body(dm_ref, oa_ref, ob_ref, hz_ref, rz_ref, ga_ref, gb_ref, hg_ref, rg_ref,
             doa_ref, dob_ref, vec_ref, dpb_ref, bufs, sems):
        i = pl.program_id(0)
        dg_a, dg_b = [], []
        for hh in range(N_HEADS):
            ls = slice(hh * LANE, (hh + 1) * LANE)
            dmh = dm_ref[:, ls]
            for side, (o_ref, z_ref, gate_ref, g_ref, do_ref, acc) in enumerate((
                    (oa_ref, hz_ref, ga_ref, hg_ref, doa_ref, dg_a),
                    (ob_ref, rz_ref, gb_ref, rg_ref, dob_ref, dg_b))):
                o, z, gt, g = o_ref[hh], z_ref[hh], gate_ref[hh], g_ref[:, ls]
                r, n = _head_norm(o, g)
                sz = _sigmoid(z)
                silu = z * sz
                sgt = _sigmoid(gt)
                du = dmh * sgt
                bufs[2 + side, hh] = dmh * (n * silu) * (sgt * (1.0 - sgt))
                bufs[side, hh] = du * n * (sz * (1.0 + z * (1.0 - sz)))
                dn = du * silu
                acc.append(jnp.sum(dn * (o * r), axis=0, keepdims=True))
                gdn = dn * g
                do_ref[hh] = r * gdn - o * (r * r * r) * jnp.mean(o * gdn, axis=-1, keepdims=True)
        _acc_rows(vec_ref, i, [jnp.concatenate(dg_a, axis=1), jnp.concatenate(dg_b, axis=1)])
        rows = pl.ds(pl.multiple_of(i * tm, tm), tm)
        copies = [pltpu.make_async_copy(bufs.at[n], dpb_ref.at[pl.ds(cb, N_HEADS), rows], sems.at[n])
                  for n, cb in enumerate((CB_HZ, CB_RZ, CB_GA, CB_GB))]
        for cp in copies:
            cp.start()
        for cp in copies:
            cp.wait()

    head = pl.BlockSpec((N_HEADS, tm, LANE), lambda i: (0, i, 0))
    return pl.pallas_call(
        body, name="merge_bwd", grid=(t_len // tm,),
        in_specs=[pl.BlockSpec((tm, D_MODEL), lambda i: (i, 0)), head, head,
                  _group_spec(tm, CB_HZ), _group_spec(tm, CB_RZ), _group_spec(tm, CB_GA), _group_spec(tm, CB_GB),
                  _vec_spec(), _vec_spec()],
        out_specs=[head, head, pl.BlockSpec((8, D_MODEL), lambda i: (0, 0)), pl.BlockSpec(memory_space=pl.ANY)],
        out_shape=[jax.ShapeDtypeStruct((N_HEADS, t_len, LANE), F32),
                   jax.ShapeDtypeStruct((N_HEADS, t_len, LANE), F32),
                   jax.ShapeDtypeStruct((8, D_MODEL), F32),
                   jax.ShapeDtypeStruct((N_CB, t_len, LANE), F32)],
        scratch_shapes=[pltpu.VMEM((4, N_HEADS, tm, LANE), F32), pltpu.SemaphoreType.DMA((4,))],
        compiler_params=_params(("arbitrary",)),
    )(dm, oa, ob, pb, pb, pb, pb, hg_g, ret_g)


def tail(x, z, target, gate, final_g, t_len):
    tm = _row_tile(t_len, 512)

    def body(x_ref, z_ref, t_ref, gate_ref, fg_ref, dy_ref, dz_ref, vec_ref):
        i = pl.program_id(0)
        zv, gt, fg = z_ref[...], gate_ref[...], fg_ref[...]
        y = x_ref[...] + gt * zv
        r = lax.rsqrt(jnp.mean(y * y, axis=-1, keepdims=True) + EPS)
        yn = y * r
        err = yn * fg - t_ref[...]
        loss = 0.5 * jnp.sum(jnp.mean(err * err, axis=-1, keepdims=True), axis=0, keepdims=True)
        dout = err * (1.0 / D_MODEL)
        gd = dout * fg
        dy = r * gd - yn * (r * r) * jnp.mean(y * gd, axis=-1, keepdims=True)
        dy_ref[...] = dy
        dz_ref[...] = (dy * gt).astype(dz_ref.dtype)
        _acc_rows(vec_ref, i, [jnp.sum(dout * yn, axis=0, keepdims=True),
                               jnp.sum(dy * zv, axis=0, keepdims=True),
                               jnp.broadcast_to(loss, (1, D_MODEL))])

    row = pl.BlockSpec((tm, D_MODEL), lambda i: (i, 0))
    return pl.pallas_call(
        body, name="tail", grid=(t_len // tm,),
        in_specs=[row, row, row, _vec_spec(), _vec_spec()],
        out_specs=[row, row, pl.BlockSpec((8, D_MODEL), lambda i: (0, 0))],
        out_shape=[jax.ShapeDtypeStruct((t_len, D_MODEL), F32), jax.ShapeDtypeStruct((t_len, D_MODEL), _BF),
                   jax.ShapeDtypeStruct((8, D_MODEL), F32)],
        compiler_params=_params(("arbitrary",)),
    )(x, z, target, gate, final_g)


def device_step(x, target, mod, lb, wg, wout, norm_g, hg_g, ret_g, final_g):
    t_len = x.shape[0]
    shift, scale, gate = mod[:, :D_MODEL], mod[:, D_MODEL:2 * D_MODEL], mod[:, 2 * D_MODEL:]
    scale1p = 1.0 + scale
    cos_t, sin_t = _rope_tables(t_len)
    h = adaln_forward(x, norm_g, scale1p, shift, t_len)
    pb = proj_forward(h, wg, t_len)
    oa, ssave = hgrn_forward(pb, lb, t_len)
    ob, rsave = retention_forward(pb, cos_t, sin_t, t_len)
    m = merge_forward(oa, ob, pb, hg_g, ret_g, t_len)
    z = out_forward(m, wout, t_len)
    dy, dz, vec_tail = tail(x, z, target, gate, final_g, t_len)
    dm, dwout = out_backward(m, dz, wout, t_len)
    doa, dob, vec_merge, dpb = merge_backward(dm, oa, ob, pb, hg_g, ret_g, t_len)
    dpb, dlb = hgrn_backward(pb, lb, doa, ssave, dpb, t_len)
    dpb = retention_backward(pb, cos_t, sin_t, dob, rsave, dpb, t_len)
    dh = proj_backward_input(dpb, wg, t_len)
    dwin = proj_backward_weight(h, dpb, t_len)
    grad_x, vec_ada = adaln_backward(dh, x, dy, norm_g, scale1p, t_len)
    return grad_x, dwin, dwout, vec_tail, vec_merge, vec_ada, dlb


PACK_ROWS = 16
ROW_NORM_G, ROW_LB, ROW_HG_G, ROW_RET_G, ROW_FINAL_G, ROW_SHIFT, ROW_SCALE, ROW_GATE, ROW_LOSS = range(9)


def _mesh_pos():
    return lax.axis_index("x"), lax.axis_index("y"), lax.axis_index("c")


def _lin(pos):
    return 4 * pos[0] + 2 * pos[1] + pos[2]


def _xor_peer(pos, k):
    return tuple(1 - p if (k >> s) & 1 else p for p, s in zip(pos, (2, 1, 0)))


def _other_chips(pos):
    x, y, _ = pos
    return [(1 - x, y), (x, 1 - y), (1 - x, 1 - y)]


def _remote(src, dst, send_sem, recv_sem, to):
    return pltpu.make_async_remote_copy(src_ref=src, dst_ref=dst, send_sem=send_sem, recv_sem=recv_sem,
                                        device_id=to, device_id_type=MESH)


def pre_exchange(c, w_ada, b_ada, logits):
    def body(c_ref, wada_ref, bada_ref, logit_ref, mod_ref, scall_ref, lb_ref,
             cg_ref, modall_ref, parts_ref, send1, recv1, send2, recv2):
        pos = _mesh_pos()
        cv = c_ref[...]
        slot = lambda p: pl.ds(pl.multiple_of(8 * _lin(p), 8), 8)
        cg_ref[slot(pos), :] = jnp.broadcast_to(cv * _sigmoid(cv), (8, D_MODEL))
        lb_ref[...] = _sigmoid(logit_ref[0:1, :] - logit_ref[1:2, :])
        peers = [_xor_peer(pos, k) for k in range(1, N_DEV)]
        gather = [_remote(cg_ref.at[slot(pos)], cg_ref.at[slot(pos)], send1.at[n], recv1.at[n], p)
                  for n, p in enumerate(peers)]
        for cp in gather:
            cp.start()
        for n, p in enumerate(peers):
            _remote(cg_ref.at[slot(p)], cg_ref.at[slot(p)], send1.at[n], recv1.at[n], p).wait_recv()
        modall_ref[...] = _dot_f32(cg_ref[...], wada_ref[...])
        scatter = [_remote(modall_ref.at[slot(p)], parts_ref.at[slot(pos)], send2.at[n], recv2.at[n], p)
                   for n, p in enumerate(peers)]
        for cp in scatter:
            cp.start()
        parts_ref[slot(pos), :] = modall_ref[slot(pos), :]
        for n, p in enumerate(peers):
            _remote(modall_ref.at[slot(p)], parts_ref.at[slot(p)], send2.at[n], recv2.at[n], p).wait_recv()
        for cp in gather + scatter:
            cp.wait_send()
        for j in range(N_DEV):
            cols = slice(j * SHARD_ADA, (j + 1) * SHARD_ADA)
            mod_ref[:, cols] = parts_ref[8 * j:8 * j + 1, :] + bada_ref[:, cols]
            scall_ref[j:j + 1, :] = cg_ref[8 * j:8 * j + 1, :]

    vmem = pl.BlockSpec(memory_space=pltpu.VMEM)
    return pl.pallas_call(
        body, name="pre_exchange",
        in_specs=[vmem] * 4, out_specs=[vmem] * 3,
        out_shape=[jax.ShapeDtypeStruct((1, 3 * D_MODEL), F32), jax.ShapeDtypeStruct((N_DEV, D_MODEL), F32),
                   jax.ShapeDtypeStruct((1, D_MODEL), F32)],
        scratch_shapes=[pltpu.VMEM((N_DEV * 8, D_MODEL), F32), pltpu.VMEM((N_DEV * 8, SHARD_ADA), F32),
                        pltpu.VMEM((N_DEV * 8, SHARD_ADA), F32)] + [pltpu.SemaphoreType.DMA((N_DEV - 1,))] * 4,
        compiler_params=pltpu.CompilerParams(vmem_limit_bytes=VMEM_LIMIT),
    )(c, w_ada, b_ada, logits)


def weight_gather(win_sh, wout_sh):
    def body(win_ref, wout_ref, wg_ref, woutg_ref, send, recv, local):
        pos = _mesh_pos()
        x, y, c = pos
        sibling = (x, y, 1 - c)
        chips = _other_chips(pos)
        first, passed, mine = [], [], []
        for a, (src, out) in enumerate(((win_ref, wg_ref), (wout_ref, woutg_ref))):
            def copy(k, block, to, src_ref=None, a=a, out=out):
                dst = out.at[_lin(block)]
                return _remote(dst if src_ref is None else src_ref, dst, send.at[7 * a + k], recv.at[7 * a + k], to)
            mine.append(pltpu.make_async_copy(src, out.at[_lin(pos)], local.at[a]))
            first.append(copy(0, pos, sibling, src))
            first += [copy(1 + j, pos, (*chip, c), src) for j, chip in enumerate(chips)]
            passed.append([copy(4 + j, (*chip, c), sibling) for j, chip in enumerate(chips)])
        for cp in mine + first:
            cp.start()
        for a, out in enumerate((wg_ref, woutg_ref)):
            for j, chip in enumerate(chips):
                dst = out.at[_lin((*chip, c))]
                _remote(dst, dst, send.at[7 * a + 1 + j], recv.at[7 * a + 1 + j], pos).wait_recv()
                passed[a][j].start()
        for a, out in enumerate((wg_ref, woutg_ref)):
            dst = out.at[_lin(sibling)]
            _remote(dst, dst, send.at[7 * a], recv.at[7 * a], pos).wait_recv()
            for j, chip in enumerate(chips):
                dst = out.at[_lin((*chip, 1 - c))]
                _remote(dst, dst, send.at[7 * a + 4 + j], recv.at[7 * a + 4 + j], pos).wait_recv()
        for cp in first + passed[0] + passed[1]:
            cp.wait_send()
        for cp in mine:
            cp.wait()

    any_spec = pl.BlockSpec(memory_space=pl.ANY)
    return pl.pallas_call(
        body, name="weight_gather",
        in_specs=[any_spec, any_spec], out_specs=[any_spec, any_spec],
        out_shape=[jax.ShapeDtypeStruct((N_DEV,) + win_sh.shape, win_sh.dtype),
                   jax.ShapeDtypeStruct((N_DEV,) + wout_sh.shape, wout_sh.dtype)],
        scratch_shapes=[pltpu.SemaphoreType.DMA((14,)), pltpu.SemaphoreType.DMA((14,)),
                        pltpu.SemaphoreType.DMA((2,))],
    )(win_sh, wout_sh)


def grad_pair_exchange(g_in, g_out):
    def body(gin_ref, gout_ref, ra_ref, rb_ref, send, recv):
        pos = _mesh_pos()
        x, y, c = pos
        sibling = (x, y, 1 - c)
        copies = []
        for a, (src, dst) in enumerate(((gin_ref, ra_ref), (gout_ref, rb_ref))):
            for q in range(4):
                copies.append(_remote(src.at[2 * q + (1 - c)], dst.at[q], send.at[4 * a + q], recv.at[4 * a + q],
                                      sibling))
        for cp in copies:
            cp.start()
        for cp in copies:
            cp.wait_recv()
        for cp in copies:
            cp.wait_send()

    any_spec = pl.BlockSpec(memory_space=pl.ANY)
    return pl.pallas_call(
        body, name="grad_pair_exchange",
        in_specs=[any_spec, any_spec], out_specs=[any_spec, any_spec],
        out_shape=[jax.ShapeDtypeStruct((4,) + g_in.shape[1:], F32), jax.ShapeDtypeStruct((4,) + g_out.shape[1:], F32)],
        scratch_shapes=[pltpu.SemaphoreType.DMA((8,)), pltpu.SemaphoreType.DMA((8,))],
    )(g_in, g_out)


def pair_sum(g_in, ra, g_out, rb, c_idx):
    tr = 256

    def body(c_ref, gin_ref, ra_ref, gout_ref, rb_ref, sb_ref, sbo_ref):
        del c_ref
        sb_ref[...] = (gin_ref[...] + ra_ref[...]).astype(sb_ref.dtype)
        sbo_ref[...] = gout_ref[...] + rb_ref[...]

    n_i = D_MODEL // tr
    return pl.pallas_call(
        body, name="pair_sum",
        grid_spec=pltpu.PrefetchScalarGridSpec(
            num_scalar_prefetch=1, grid=(4, n_i),
            in_specs=[pl.BlockSpec((None, tr, SHARD_IN), lambda q, i, c: (2 * q + c[0], i, 0)),
                      pl.BlockSpec((None, tr, SHARD_IN), lambda q, i, c: (q, i, 0)),
                      pl.BlockSpec((None, SHARD_OUT // n_i, D_MODEL), lambda q, i, c: (2 * q + c[0], i, 0)),
                      pl.BlockSpec((None, SHARD_OUT // n_i, D_MODEL), lambda q, i, c: (q, i, 0))],
            out_specs=[pl.BlockSpec((None, tr, SHARD_IN), lambda q, i, c: (q, i, 0)),
                       pl.BlockSpec((None, SHARD_OUT // n_i, D_MODEL), lambda q, i, c: (q, i, 0))]),
        out_shape=[jax.ShapeDtypeStruct(ra.shape, _BF), jax.ShapeDtypeStruct(rb.shape, F32)],
        compiler_params=_params(("arbitrary", "arbitrary")),
    )(c_idx, g_in, ra, g_out, rb)


def grad_chip_exchange(sb, sbo, pack):
    def body(sb_ref, sbo_ref, pack_ref, rc_ref, rco_ref, packs_ref, send, recv, local, psend, precv):
        pos = _mesh_pos()
        x, y, c = pos
        me = _lin(pos)
        my_chip = 2 * x + y
        chips = _other_chips(pos)
        packs_ref[me] = pack_ref[...]
        peers = [_xor_peer(pos, k) for k in range(1, N_DEV)]
        gather = [_remote(packs_ref.at[me], packs_ref.at[me], psend.at[n], precv.at[n], p) for n, p in enumerate(peers)]
        mine, copies = [], []
        for a, (src, dst) in enumerate(((sb_ref, rc_ref), (sbo_ref, rco_ref))):
            mine.append(pltpu.make_async_copy(src.at[my_chip], dst.at[my_chip], local.at[a]))
            for j, chip in enumerate(chips):
                copies.append(_remote(src.at[2 * chip[0] + chip[1]], dst.at[my_chip], send.at[3 * a + j],
                                      recv.at[3 * a + j], (*chip, c)))
        for cp in gather + mine + copies:
            cp.start()
        for n, p in enumerate(peers):
            _remote(packs_ref.at[_lin(p)], packs_ref.at[_lin(p)], psend.at[n], precv.at[n], p).wait_recv()
        for a, (src, dst) in enumerate(((sb_ref, rc_ref), (sbo_ref, rco_ref))):
            for j, chip in enumerate(chips):
                slot = dst.at[2 * chip[0] + chip[1]]
                _remote(slot, slot, send.at[3 * a + j], recv.at[3 * a + j], pos).wait_recv()
        for cp in gather + copies:
            cp.wait_send()
        for cp in mine:
            cp.wait()

    any_spec = pl.BlockSpec(memory_space=pl.ANY)
    vmem = pl.BlockSpec(memory_space=pltpu.VMEM)
    return pl.pallas_call(
        body, name="grad_chip_exchange",
        in_specs=[any_spec, any_spec, vmem], out_specs=[any_spec, any_spec, vmem],
        out_shape=[jax.ShapeDtypeStruct(sb.shape, sb.dtype), jax.ShapeDtypeStruct(sbo.shape, sbo.dtype),
                   jax.ShapeDtypeStruct((N_DEV,) + pack.shape, F32)],
        scratch_shapes=[pltpu.SemaphoreType.DMA((6,)), pltpu.SemaphoreType.DMA((6,)), pltpu.SemaphoreType.DMA((2,)),
                        pltpu.SemaphoreType.DMA((N_DEV - 1,)), pltpu.SemaphoreType.DMA((N_DEV - 1,))],
    )(sb, sbo, pack)


def pack_rows(vec_tail, vec_merge, vec_ada, dlb):
    def body(tail_ref, merge_ref, ada_ref, dlb_ref, o_ref):
        rows = [ada_ref[0:1, :], dlb_ref[...], merge_ref[0:1, :], merge_ref[1:2, :], tail_ref[0:1, :],
                ada_ref[2:3, :], ada_ref[1:2, :], tail_ref[1:2, :], tail_ref[2:3, :]]
        o_ref[...] = jnp.zeros_like(o_ref)
        for n, row in enumerate(rows):
            o_ref[n:n + 1, :] = row

    vmem = pl.BlockSpec(memory_space=pltpu.VMEM)
    return pl.pallas_call(body, name="pack_rows", in_specs=[vmem] * 4, out_specs=vmem,
                          out_shape=jax.ShapeDtypeStruct((PACK_ROWS, D_MODEL), F32))(vec_tail, vec_merge, vec_ada, dlb)


def _adamw(w, g, m, v):
    m = ADAM_B1 * m + (1.0 - ADAM_B1) * g
    v = ADAM_B2 * v + (1.0 - ADAM_B2) * (g * g)
    m_hat = m / (1.0 - ADAM_B1 ** ADAM_STEP)
    v_hat = v / (1.0 - ADAM_B2 ** ADAM_STEP)
    delta = -ADAM_LR * (m_hat / (jnp.sqrt(v_hat) + ADAM_EPS) + ADAM_WD * w)
    return delta, m, v


def adam_shard(parts, w, m, v, name):
    rows, cols = w.shape
    tr = min(rows, 128)

    def body(p0, p1, p2, p3, w_ref, m_ref, v_ref, g_ref, d_ref, nm_ref, nv_ref):
        g = ((p0[...].astype(F32) + p1[...].astype(F32)) + p2[...].astype(F32)) + p3[...].astype(F32)
        g_ref[...] = g
        d_ref[...], nm_ref[...], nv_ref[...] = _adamw(w_ref[...], g, m_ref[...], v_ref[...])

    part = lambda q: pl.BlockSpec((None, tr, cols), lambda i, q=q: (q, i, 0))
    tile = pl.BlockSpec((tr, cols), lambda i: (i, 0))
    return pl.pallas_call(
        body, name=name, grid=(rows // tr,),
        in_specs=[part(0), part(1), part(2), part(3), tile, tile, tile],
        out_specs=[tile] * 4, out_shape=[jax.ShapeDtypeStruct(w.shape, F32)] * 4,
        compiler_params=_params(("arbitrary",)),
    )(parts, parts, parts, parts, w, m, v)


def adam_ada(sc_t, dmod_all, me_idx, w, m, v):
    def body(me_ref, sc_ref, dm_ref, w_ref, m_ref, v_ref, g_ref, d_ref, nm_ref, nv_ref):
        del me_ref
        g = _dot_f32(sc_ref[...], dm_ref[...])
        g_ref[...] = g
        d_ref[...], nm_ref[...], nv_ref[...] = _adamw(w_ref[...], g, m_ref[...], v_ref[...])

    full = pl.BlockSpec(w.shape, lambda i, me: (0, 0))
    return pl.pallas_call(
        body, name="adam_ada",
        grid_spec=pltpu.PrefetchScalarGridSpec(
            num_scalar_prefetch=1, grid=(1,),
            in_specs=[pl.BlockSpec(sc_t.shape, lambda i, me: (0, 0)),
                      pl.BlockSpec((LANE, SHARD_ADA), lambda i, me: (0, me[0])), full, full, full],
            out_specs=[full] * 4),
        out_shape=[jax.ShapeDtypeStruct(w.shape, F32)] * 4,
        compiler_params=_params(("arbitrary",)),
    )(me_idx, sc_t, dmod_all, w, m, v)


def adam_vectors(packs, lb, params, ms, vs):
    n = len(params)

    def body(*refs):
        packs_ref, lb_ref = refs[0], refs[1]
        w_refs, m_refs, v_refs = refs[2:2 + n], refs[2 + n:2 + 2 * n], refs[2 + 2 * n:2 + 3 * n]
        loss_ref = refs[2 + 3 * n]
        outs = refs[3 + 3 * n:3 + 7 * n]
        tot_ref = refs[3 + 7 * n]
        tot = packs_ref[0]
        for d in range(1, N_DEV):
            tot = tot + packs_ref[d]
        tot_ref[...] = tot
        row = lambda r: tot_ref[r:r + 1, :]
        lbv = lb_ref[...]
        dl0 = row(ROW_LB) * lbv * (1.0 - lbv)
        grads = [[row(ROW_NORM_G)],
                 [jnp.concatenate([row(ROW_SHIFT), row(ROW_SCALE), row(ROW_GATE)], axis=1)],
                 [dl0, -dl0],
                 [row(ROW_HG_G)], [row(ROW_RET_G)], [row(ROW_FINAL_G)]]
        loss_ref[...] = tot_ref[ROW_LOSS:ROW_LOSS + 1, 0:LANE]
        for j, g_rows in enumerate(grads):
            for r, g in enumerate(g_rows):
                rs = slice(r, r + 1)
                d, nm, nv = _adamw(w_refs[j][rs, :], g, m_refs[j][rs, :], v_refs[j][rs, :])
                outs[4 * j][rs, :] = g
                outs[4 * j + 1][rs, :] = d
                outs[4 * j + 2][rs, :] = nm
                outs[4 * j + 3][rs, :] = nv

    vmem = pl.BlockSpec(memory_space=pltpu.VMEM)
    out_shape = [jax.ShapeDtypeStruct((1, LANE), F32)]
    for w in params:
        out_shape += [jax.ShapeDtypeStruct(w.shape, F32)] * 4
    return pl.pallas_call(
        body, name="adam_vectors", in_specs=[vmem] * (2 + 3 * n), out_specs=[vmem] * len(out_shape),
        out_shape=out_shape, scratch_shapes=[pltpu.VMEM((PACK_ROWS, D_MODEL), F32)],
    )(packs, lb, *params, *ms, *vs)


def kernel(x, c, norm_g, w_ada, b_ada, w_in, hg_lb_logits, hg_norm_g, ret_norm_g, w_out, final_g, loss_target, m_norm_g, m_w_ada, m_b_ada, m_w_in, m_hg_lb_logits, m_hg_norm_g, m_ret_norm_g, m_w_out, m_final_g, v_norm_g, v_w_ada, v_b_ada, v_w_in, v_hg_lb_logits, v_hg_norm_g, v_ret_norm_g, v_w_out, v_final_g):
    pos = _mesh_pos()
    me_idx = jnp.reshape(_lin(pos), (1,)).astype(jnp.int32)
    c_idx = jnp.reshape(pos[2], (1,)).astype(jnp.int32)
    vec = lambda a: a.reshape(1, D_MODEL)

    mod, scall, lb = pre_exchange(c, w_ada[0], b_ada, hg_lb_logits)
    wg, woutg = weight_gather(w_in[0].astype(_BF), w_out[0].astype(_BF))
    grad_x, dwin, dwout, vec_tail, vec_merge, vec_ada, dlb = device_step(
        x[0], loss_target[0], mod, lb, wg, woutg.reshape(D_MODEL, D_MODEL), norm_g, hg_norm_g, ret_norm_g,
        vec(final_g))

    dwout = dwout.reshape(N_DEV, SHARD_OUT, D_MODEL)
    ra, rb = grad_pair_exchange(dwin, dwout)
    sb, sbo = pair_sum(dwin, ra, dwout, rb, c_idx)
    pack = pack_rows(vec_tail, vec_merge, vec_ada, dlb)
    rc, rco, packs = grad_chip_exchange(sb, sbo, pack)

    g_in, d_in, nm_in, nv_in = adam_shard(rc, w_in[0], m_w_in[0], v_w_in[0], "adam_w_in")
    g_out, d_out, nm_out, nv_out = adam_shard(rco, w_out[0], m_w_out[0], v_w_out[0], "adam_w_out")
    dmod_all = packs[:, ROW_SHIFT:ROW_GATE + 1, :].reshape(N_DEV, 3 * D_MODEL)
    dmod_all = jnp.pad(dmod_all, ((0, LANE - N_DEV), (0, 0)))
    sc_t = jnp.pad(scall.T, ((0, 0), (0, LANE - N_DEV)))
    g_ada, d_ada, nm_ada, nv_ada = adam_ada(sc_t, dmod_all, me_idx, w_ada[0], m_w_ada[0], v_w_ada[0])
    small = adam_vectors(
        packs, lb,
        (norm_g, b_ada, hg_lb_logits, hg_norm_g, ret_norm_g, vec(final_g)),
        (m_norm_g, m_b_ada, m_hg_lb_logits, m_hg_norm_g, m_ret_norm_g, vec(m_final_g)),
        (v_norm_g, v_b_ada, v_hg_lb_logits, v_hg_norm_g, v_ret_norm_g, vec(v_final_g)))
    loss = small[0][0, 0]
    (g_ng, d_ng, nm_ng, nv_ng), (g_b, d_b, nm_b, nv_b), (g_lb, d_lb, nm_lb, nv_lb), (g_hg, d_hg, nm_hg, nv_hg), \
        (g_rg, d_rg, nm_rg, nv_rg), (g_fg, d_fg, nm_fg, nv_fg) = [small[1 + 4 * j:5 + 4 * j] for j in range(6)]
    flat = lambda a: a.reshape(D_MODEL)

    def group(ng, ada, b, win, lbl, hg, rg, wo, fg):
        return (ng, ada[None], b, win[None], lbl, hg, rg, wo[None], flat(fg))

    return (loss, grad_x[None],
            *group(g_ng, g_ada, g_b, g_in, g_lb, g_hg, g_rg, g_out, g_fg),
            *group(d_ng, d_ada, d_b, d_in, d_lb, d_hg, d_rg, d_out, d_fg),
            *group(nm_ng, nm_ada, nm_b, nm_in, nm_lb, nm_hg, nm_rg, nm_out, nm_fg),
            *group(nv_ng, nv_ada, nv_b, nv_in, nv_lb, nv_hg, nv_rg, nv_out, nv_fg))
```

```python
import functools

import numpy as np
import jax
import jax.numpy as jnp
from jax import lax
from jax.experimental import pallas as pl
from jax.experimental.pallas import tpu as pltpu

F32 = jnp.float32
_BF = jnp.bfloat16

D_MODEL = 1024
N_HEADS = 8
LANE = 128
RET_DK = 64
D_IN = 9216
N_DEV = 8
SHARD_IN = D_IN // N_DEV
SHARD_ADA = 3 * D_MODEL // N_DEV
SHARD_OUT = D_MODEL // N_DEV
N_CB = D_IN // LANE
CB_PER_SHARD = SHARD_IN // LANE
CHUNK = 128
N_LEVELS = 7
EPS = 1e-6
ROPE_BASE = 10000.0
CB_HQ, CB_HF, CB_HI, CB_HZ, CB_RQ, CB_RK, CB_RV, CB_RZ, CB_GA, CB_GB = 0, 8, 16, 24, 32, 36, 40, 48, 56, 64
VMEM_LIMIT = 56 * 1024 * 1024

ADAM_LR, ADAM_B1, ADAM_B2, ADAM_EPS, ADAM_WD, ADAM_STEP = 0.001, 0.9, 0.999, 1e-08, 0.01, 10

_NN = (((1,), (0,)), ((), ()))
_NT = (((1,), (1,)), ((), ()))
_TN = (((0,), (0,)), ((), ()))
MESH = pl.DeviceIdType.MESH


def _dot(a, b, dims=_NN):
    return lax.dot_general(a.astype(_BF), b.astype(_BF), dims, preferred_element_type=F32)


def _split2(a):
    hi = a.astype(_BF)
    lo = (a - hi.astype(F32)).astype(_BF)
    return jnp.concatenate([hi, lo], axis=1)


def _dot_sel(sel, a):
    n = a.shape[1]
    r = lax.dot_general(sel.astype(_BF), _split2(a), _NN, preferred_element_type=F32)
    return r[:, :n] + r[:, n:]


def _dot_f32(a, b):
    def pieces(v):
        p1 = v.astype(_BF)
        r1 = v - p1.astype(F32)
        p2 = r1.astype(_BF)
        p3 = (r1 - p2.astype(F32)).astype(_BF)
        return p1, p2, p3
    a1, a2, a3 = pieces(a)
    b1, b2, b3 = pieces(b)
    d = lambda u, v: lax.dot_general(u, v, _NN, preferred_element_type=F32)
    return ((d(a1, b3) + d(a2, b2) + d(a3, b1)) + (d(a1, b2) + d(a2, b1))) + d(a1, b1)


def _sigmoid(v):
    return 1.0 / (1.0 + jnp.exp(-v))


def _params(sem=None):
    return pltpu.CompilerParams(dimension_semantics=sem, vmem_limit_bytes=VMEM_LIMIT)


def _hgrn_consts():
    c, nl = CHUNK, N_LEVELS
    t = np.arange(c)[:, None]
    j = np.arange(c)[None, :]
    sel = [j <= t]
    masks = [j == t]
    for l in range(1, nl + 1):
        m = ((t >> l) << l) + (1 << (l - 1)) - 1
        sec = t > m
        sel.append(np.where(sec, (j > m) & (j <= t), (j > t) & (j <= m)))
        same = (t >> l) == (j >> l)
        masks.append(same & sec & (j <= m))
    sel.append(j > t)
    sel = np.concatenate(sel, 0).astype(np.float32)
    masks = np.stack(masks).astype(np.float32)
    return jnp.asarray(sel, _BF), jnp.asarray(masks, F32)


def _hgrn_chunk(hq, hf, hi, lbv, sel_ref, lvl_ref):
    c, nl = CHUNK, N_LEVELS
    sq = _sigmoid(hq)
    q = hq * sq
    sg = _sigmoid(hf)
    omlb = 1.0 - lbv
    f = lbv + omlb * sg
    k = omlb * (1.0 - sg)
    logf = jnp.log(f)
    e_all = _dot_sel(sel_ref[...], logf)
    eb = jnp.exp(e_all[0:c])
    erem = jnp.exp(e_all[(nl + 1) * c:(nl + 2) * c])
    ebc = jnp.exp(jnp.sum(logf, axis=0, keepdims=True))
    lev = [None] + [jnp.exp(e_all[l * c:(l + 1) * c]) for l in range(1, nl + 1)]
    return dict(sq=sq, q=q, sg=sg, omlb=omlb, f=f, k=k, v=hi, eb=eb, erem=erem, ebc=ebc, lev=lev)


def _hgrn_scores(a, lvl_ref):
    q, k = a["q"], a["k"]
    acc = lvl_ref[0] * _dot(q, k, _NT)
    for l in range(1, N_LEVELS + 1):
        e = a["lev"][l]
        acc = acc + lvl_ref[l] * _dot(q * e, k * e, _NT)
    return acc


def _time_block(t_len):
    return min(t_len, 1024)


def hgrn_forward(pb, lb, t_len):
    nc = t_len // CHUNK
    tb = _time_block(t_len)
    ncb = tb // CHUNK
    sel, lvl = _hgrn_consts()

    def body(hq_ref, hf_ref, hi_ref, lb_ref, sel_ref, lvl_ref, o_ref, ssave_ref, st_ref):
        @pl.when(pl.program_id(1) == 0)
        def _():
            st_ref[...] = jnp.zeros_like(st_ref)

        def chunk(ci, carry):
            r = pl.ds(pl.multiple_of(ci * CHUNK, CHUNK), CHUNK)
            for hd in range(2):
                lbv = lb_ref[:, hd * LANE:(hd + 1) * LANE]
                a = _hgrn_chunk(hq_ref[hd, r, :], hf_ref[hd, r, :], hi_ref[hd, r, :], lbv, sel_ref, lvl_ref)
                st = st_ref[hd]
                ssave_ref[hd, ci] = st
                o_ref[hd, r, :] = _dot(a["q"] * a["eb"], st, _NT) + _dot(_hgrn_scores(a, lvl_ref), a["v"])
                st_ref[hd] = st * a["ebc"] + _dot(a["v"], a["k"] * a["erem"], _TN)
            return carry

        lax.fori_loop(0, ncb, chunk, 0)

    pair = lambda base: pl.BlockSpec((2, tb, LANE), lambda p, t, base=base: (base // 2 + p, t, 0))
    return pl.pallas_call(
        body, name="hgrn_fwd", grid=(N_HEADS // 2, t_len // tb),
        in_specs=[pair(CB_HQ), pair(CB_HF), pair(CB_HI),
                  pl.BlockSpec((1, 2 * LANE), lambda p, t: (0, p)),
                  pl.BlockSpec(sel.shape, lambda p, t: (0, 0)),
                  pl.BlockSpec(lvl.shape, lambda p, t: (0, 0, 0))],
        out_specs=[pl.BlockSpec((2, tb, LANE), lambda p, t: (p, t, 0)),
                   pl.BlockSpec((2, ncb, LANE, LANE), lambda p, t: (p, t, 0, 0))],
        out_shape=[jax.ShapeDtypeStruct((N_HEADS, t_len, LANE), F32),
                   jax.ShapeDtypeStruct((N_HEADS, nc, LANE, LANE), F32)],
        scratch_shapes=[pltpu.VMEM((2, LANE, LANE), F32)],
        compiler_params=_params(("arbitrary", "arbitrary")),
    )(pb, pb, pb, lb, sel, lvl)


def hgrn_backward(pb, lb, do, ssave, dpb, t_len):
    tb = _time_block(t_len)
    ncb, ntb = tb // CHUNK, t_len // tb
    sel, lvl = _hgrn_consts()

    def body(hq_ref, hf_ref, hi_ref, lb_ref, do_ref, ssave_ref, sel_ref, lvl_ref, dpb_in,
             dpb_ref, dlb_ref, dq_buf, df_buf, di_buf, dst_ref, sems):
        del dpb_in
        p, t = pl.program_id(0), pl.program_id(1)

        @pl.when(t == 0)
        def _():
            dst_ref[...] = jnp.zeros_like(dst_ref)
            dlb_ref[...] = jnp.zeros_like(dlb_ref)

        def chunk(i, carry):
            ci = ncb - 1 - i
            r = pl.ds(pl.multiple_of(ci * CHUNK, CHUNK), CHUNK)
            for hd in range(2):
                head_chunk(hd, ci, r)
            return carry

        def head_chunk(hd, ci, r):
            lbv = lb_ref[:, hd * LANE:(hd + 1) * LANE]
            hq = hq_ref[hd, r, :]
            a = _hgrn_chunk(hq, hf_ref[hd, r, :], hi_ref[hd, r, :], lbv, sel_ref, lvl_ref)
            q, k, v = a["q"], a["k"], a["v"]
            g = do_ref[hd, r, :]
            st0 = ssave_ref[hd, ci]
            dst = dst_ref[hd]
            scores = _hgrn_scores(a, lvl_ref)
            da = _dot(g, v, _NT)
            kb = k * a["erem"]
            qb = q * a["eb"]
            dv = _dot(scores, g, _TN) + _dot(kb, dst, _NT)
            dq_inter = _dot(g, st0) * a["eb"]
            dk_state = _dot(v, dst) * a["erem"]
            da0 = lvl_ref[0] * da
            dq = dq_inter + _dot(da0, k)
            dk = dk_state + _dot(da0, q, _TN)
            de = [q * dq_inter]
            for l in range(1, N_LEVELS + 1):
                e = a["lev"][l]
                dal = lvl_ref[l] * da
                dql = _dot(dal, k * e) * e
                dkl = _dot(dal, q * e, _TN) * e
                dq = dq + dql
                dk = dk + dkl
                de.append(q * dql + k * dkl)
            de.append(k * dk_state)
            dst_ref[hd] = dst * a["ebc"] + _dot(g, qb, _TN)
            dbc = jnp.sum(dst * st0, axis=0, keepdims=True) * a["ebc"]
            de2 = lax.dot_general(sel_ref[...], _split2(jnp.concatenate(de, axis=0)), _TN,
                                  preferred_element_type=F32)
            dlogf = de2[:, :LANE] + de2[:, LANE:] + dbc
            sq, sg = a["sq"], a["sg"]
            df = dlogf / a["f"] - dk
            dq_buf[hd, r, :] = dq * (sq * (1.0 + hq * (1.0 - sq)))
            df_buf[hd, r, :] = df * a["omlb"] * sg * (1.0 - sg)
            di_buf[hd, r, :] = dv
            cols = slice(hd * LANE, (hd + 1) * LANE)
            dlb_ref[:, cols] = dlb_ref[:, cols] + jnp.sum(df * (1.0 - sg), axis=0, keepdims=True)

        lax.fori_loop(0, ncb, chunk, 0)
        rows = pl.ds(pl.multiple_of((ntb - 1 - t) * tb, tb), tb)
        copies = [pltpu.make_async_copy(buf, dpb_ref.at[pl.ds(base + 2 * p, 2), rows], sems.at[n])
                  for n, (buf, base) in enumerate(((dq_buf, CB_HQ), (df_buf, CB_HF), (di_buf, CB_HI)))]
        for cp in copies:
            cp.start()
        for cp in copies:
            cp.wait()

    pair = lambda base: pl.BlockSpec((2, tb, LANE), lambda p, t, base=base: (base // 2 + p, ntb - 1 - t, 0))
    any_spec = pl.BlockSpec(memory_space=pl.ANY)
    return pl.pallas_call(
        body, name="hgrn_bwd", grid=(N_HEADS // 2, ntb),
        in_specs=[pair(CB_HQ), pair(CB_HF), pair(CB_HI),
                  pl.BlockSpec((1, 2 * LANE), lambda p, t: (0, p)),
                  pair(0),
                  pl.BlockSpec((2, ncb, LANE, LANE), lambda p, t: (p, ntb - 1 - t, 0, 0)),
                  pl.BlockSpec(sel.shape, lambda p, t: (0, 0)),
                  pl.BlockSpec(lvl.shape, lambda p, t: (0, 0, 0)),
                  any_spec],
        out_specs=[any_spec, pl.BlockSpec((1, 2 * LANE), lambda p, t: (0, p))],
        out_shape=[jax.ShapeDtypeStruct(dpb.shape, F32), jax.ShapeDtypeStruct((1, D_MODEL), F32)],
        scratch_shapes=[pltpu.VMEM((2, tb, LANE), F32)] * 3 + [
            pltpu.VMEM((2, LANE, LANE), F32), pltpu.SemaphoreType.DMA((3,))],
        input_output_aliases={8: 0},
        compiler_params=_params(("arbitrary", "arbitrary")),
    )(pb, pb, pb, lb, do, ssave, sel, lvl, dpb)


def _rope_tables(t_len):
    half = RET_DK // 2
    inv_freq = 1.0 / (ROPE_BASE ** jnp.linspace(0.0, 1.0, half, dtype=F32))
    ang = jnp.arange(t_len, dtype=jnp.int32).astype(F32)[:, None] * inv_freq[None, :]
    cos, sin = jnp.cos(ang), jnp.sin(ang)
    cos_t = jnp.concatenate([cos, cos, cos, cos], axis=1)
    sin_t = jnp.concatenate([-sin, sin, -sin, sin], axis=1)
    return cos_t, sin_t


def _swap_halves(v):
    half = RET_DK // 2
    lane = lax.broadcasted_iota(jnp.int32, v.shape, 1)
    first = (lane & (RET_DK - 1)) < half
    return jnp.where(first, pltpu.roll(v, LANE - half, 1), pltpu.roll(v, half, 1))


def _ret_head_consts(hidx):
    c = CHUNK
    hf = jnp.full((1, LANE), hidx, jnp.int32).astype(F32)
    lg = jnp.log(1.0 - jnp.exp(-(5.0 + hf) * np.float32(np.log(2.0))))
    row = lax.broadcasted_iota(jnp.int32, (c, c), 0)
    col = lax.broadcasted_iota(jnp.int32, (c, c), 1)
    rel = (row - col).astype(F32)
    dm = jnp.where(rel >= 0, jnp.exp(lg[:, :1] * jnp.maximum(rel, 0.0)), 0.0)
    idx = lax.broadcasted_iota(jnp.int32, (c, LANE), 0).astype(F32)
    zeta = jnp.exp(lg * (c - 1.0 - idx))
    xi = jnp.exp(lg * (idx + 1.0))
    cdec = jnp.exp(lg * float(c))
    return dm, zeta, xi, cdec


def _lane_mask(which):
    lane = lax.broadcasted_iota(jnp.int32, (1, LANE), 1)
    return ((lane // RET_DK) == which).astype(F32)


def retention_forward(pb, cos_t, sin_t, t_len):
    nc = t_len // CHUNK

    tb = _time_block(t_len)
    ncb = tb // CHUNK

    def body(rq_ref, rk_ref, rv_ref, cos_ref, sin_ref, o_ref, rsave_ref, st_ref):
        p = pl.program_id(0)

        @pl.when(pl.program_id(1) == 0)
        def _():
            st_ref[...] = jnp.zeros_like(st_ref)

        consts = [_ret_head_consts(2 * p + hd) for hd in range(2)]

        def chunk(ci, carry):
            r = pl.ds(pl.multiple_of(ci * CHUNK, CHUNK), CHUNK)
            cs, sn = cos_ref[r, :], sin_ref[r, :]
            q = rq_ref[r, :]
            k = rk_ref[r, :]
            q = q * cs + _swap_halves(q) * sn
            k = (k * cs + _swap_halves(k) * sn) * RET_DK ** -0.5
            for hd in range(2):
                dm, zeta, xi, cdec = consts[hd]
                lm = _lane_mask(hd)
                qh, kh = q * lm, k * lm
                v = rv_ref[hd, r, :]
                st = st_ref[hd]
                rsave_ref[hd, ci] = st
                scores = _dot(qh, kh, _NT) * dm
                o_ref[hd, r, :] = _dot(scores, v) + _dot(qh * xi, st, _NT)
                st_ref[hd] = st * cdec + _dot(v, kh * zeta, _TN)
            return carry

        lax.fori_loop(0, ncb, chunk, 0)

    return pl.pallas_call(
        body, name="ret_fwd", grid=(N_HEADS // 2, t_len // tb),
        in_specs=[pl.BlockSpec((None, tb, LANE), lambda p, t: (CB_RQ + p, t, 0)),
                  pl.BlockSpec((None, tb, LANE), lambda p, t: (CB_RK + p, t, 0)),
                  pl.BlockSpec((2, tb, LANE), lambda p, t: (CB_RV // 2 + p, t, 0)),
                  pl.BlockSpec((tb, LANE), lambda p, t: (t, 0)),
                  pl.BlockSpec((tb, LANE), lambda p, t: (t, 0))],
        out_specs=[pl.BlockSpec((2, tb, LANE), lambda p, t: (p, t, 0)),
                   pl.BlockSpec((2, ncb, LANE, LANE), lambda p, t: (p, t, 0, 0))],
        out_shape=[jax.ShapeDtypeStruct((N_HEADS, t_len, LANE), F32),
                   jax.ShapeDtypeStruct((N_HEADS, nc, LANE, LANE), F32)],
        scratch_shapes=[pltpu.VMEM((2, LANE, LANE), F32)],
        compiler_params=_params(("arbitrary", "arbitrary")),
    )(pb, pb, pb, cos_t, sin_t)


def retention_backward(pb, cos_t, sin_t, do, rsave, dpb, t_len):
    tb = _time_block(t_len)
    ncb, ntb = tb // CHUNK, t_len // tb

    def body(rq_ref, rk_ref, rv_ref, cos_ref, sin_ref, do_ref, rsave_ref, dpb_in,
             dpb_ref, dq_buf, dk_buf, dv_buf, dst_ref, sems):
        del dpb_in
        p, t = pl.program_id(0), pl.program_id(1)

        @pl.when(t == 0)
        def _():
            dst_ref[...] = jnp.zeros_like(dst_ref)

        consts = [_ret_head_consts(2 * p + hd) for hd in range(2)]

        def chunk(i, carry):
            ci = ncb - 1 - i
            r = pl.ds(pl.multiple_of(ci * CHUNK, CHUNK), CHUNK)
            cs, sn = cos_ref[r, :], sin_ref[r, :]
            q = rq_ref[r, :]
            k = rk_ref[r, :]
            q = q * cs + _swap_halves(q) * sn
            k = (k * cs + _swap_halves(k) * sn) * RET_DK ** -0.5
            dq, dk = None, None
            for hd in range(2):
                dm, zeta, xi, cdec = consts[hd]
                lm = _lane_mask(hd)
                qh, kh = q * lm, k * lm
                v = rv_ref[hd, r, :]
                g = do_ref[hd, r, :]
                st0 = rsave_ref[hd, ci]
                dst = dst_ref[hd]
                scores = _dot(qh, kh, _NT) * dm
                dsc = _dot(g, v, _NT) * dm
                dqh = _dot(dsc, kh) + _dot(g, st0) * xi
                dkh = _dot(dsc, qh, _TN) + _dot(v, dst) * zeta
                dv_buf[hd, r, :] = _dot(scores, g, _TN) + _dot(kh * zeta, dst, _NT)
                dst_ref[hd] = dst * cdec + _dot(g, qh * xi, _TN)
                dq = dqh if dq is None else dq + dqh
                dk = dkh if dk is None else dk + dkh
            dk = dk * (RET_DK ** -0.5)
            dq_buf[r, :] = dq * cs - _swap_halves(dq) * sn
            dk_buf[r, :] = dk * cs - _swap_halves(dk) * sn
            return carry

        lax.fori_loop(0, ncb, chunk, 0)
        rows = pl.ds(pl.multiple_of((ntb - 1 - t) * tb, tb), tb)
        copies = [pltpu.make_async_copy(dq_buf, dpb_ref.at[CB_RQ + p, rows], sems.at[0]),
                  pltpu.make_async_copy(dk_buf, dpb_ref.at[CB_RK + p, rows], sems.at[1]),
                  pltpu.make_async_copy(dv_buf, dpb_ref.at[pl.ds(CB_RV + 2 * p, 2), rows], sems.at[2])]
        for cp in copies:
            cp.start()
        for cp in copies:
            cp.wait()

    any_spec = pl.BlockSpec(memory_space=pl.ANY)
    return pl.pallas_call(
        body, name="ret_bwd", grid=(N_HEADS // 2, ntb),
        in_specs=[pl.BlockSpec((None, tb, LANE), lambda p, t: (CB_RQ + p, ntb - 1 - t, 0)),
                  pl.BlockSpec((None, tb, LANE), lambda p, t: (CB_RK + p, ntb - 1 - t, 0)),
                  pl.BlockSpec((2, tb, LANE), lambda p, t: (CB_RV // 2 + p, ntb - 1 - t, 0)),
                  pl.BlockSpec((tb, LANE), lambda p, t: (ntb - 1 - t, 0)),
                  pl.BlockSpec((tb, LANE), lambda p, t: (ntb - 1 - t, 0)),
                  pl.BlockSpec((2, tb, LANE), lambda p, t: (p, ntb - 1 - t, 0)),
                  pl.BlockSpec((2, ncb, LANE, LANE), lambda p, t: (p, ntb - 1 - t, 0, 0)),
                  any_spec],
        out_specs=any_spec,
        out_shape=jax.ShapeDtypeStruct(dpb.shape, F32),
        scratch_shapes=[pltpu.VMEM((tb, LANE), F32), pltpu.VMEM((tb, LANE), F32),
                        pltpu.VMEM((2, tb, LANE), F32), pltpu.VMEM((2, LANE, LANE), F32),
                        pltpu.SemaphoreType.DMA((3,))],
        input_output_aliases={7: 0},
        compiler_params=_params(("arbitrary", "arbitrary")),
    )(pb, pb, pb, cos_t, sin_t, do, rsave, dpb)


def _row_tile(t_len, want):
    return min(want, t_len)


PAIR_CB = 2 * CB_PER_SHARD


def proj_forward(h, wt, t_len):
    tm = _row_tile(t_len, 512)

    def body(h_ref, w_ref, o_ref):
        acc = _dot(h_ref[...], w_ref[...], _NT)
        for jj in range(PAIR_CB):
            o_ref[jj] = acc[:, jj * LANE:(jj + 1) * LANE]

    return pl.pallas_call(
        body, name="proj_fwd", grid=(N_DEV // 2, t_len // tm),
        in_specs=[pl.BlockSpec((tm, D_MODEL), lambda j, i: (i, 0)),
                  pl.BlockSpec((PAIR_CB * LANE, D_MODEL), lambda j, i: (j, 0))],
        out_specs=pl.BlockSpec((PAIR_CB, tm, LANE), lambda j, i: (j, i, 0)),
        out_shape=jax.ShapeDtypeStruct((N_CB, t_len, LANE), F32),
        compiler_params=_params(("arbitrary", "arbitrary")),
    )(h, wt)


def proj_backward_input(dpb, wt, t_len):
    tm = _row_tile(t_len, 256)

    def body(a_ref, wt_hbm, o_ref, w_ref, sem):
        @pl.when(pl.program_id(0) == 0)
        def _():
            cp = pltpu.make_async_copy(wt_hbm, w_ref, sem)
            cp.start()
            cp.wait()

        a = jnp.concatenate([a_ref[jj].astype(_BF) for jj in range(N_CB)], axis=1)
        o_ref[...] = _dot(a, w_ref[...])

    return pl.pallas_call(
        body, name="proj_bwd_input", grid=(t_len // tm,),
        in_specs=[pl.BlockSpec((N_CB, tm, LANE), lambda i: (0, i, 0)),
                  pl.BlockSpec(memory_space=pl.ANY)],
        out_specs=pl.BlockSpec((tm, D_MODEL), lambda i: (i, 0)),
        out_shape=jax.ShapeDtypeStruct((t_len, D_MODEL), F32),
        scratch_shapes=[pltpu.VMEM(wt.shape, wt.dtype), pltpu.SemaphoreType.DMA],
        compiler_params=_params(("arbitrary",)),
    )(dpb, wt)


def proj_backward_weight(h_t, dpb, t_len):
    tk = _row_tile(t_len, 1024)

    def body(h_ref, b_ref, o_ref):
        k = pl.program_id(1)
        b = jnp.concatenate([b_ref[jj].astype(_BF) for jj in range(PAIR_CB)], axis=1)
        part = _dot(h_ref[...], b)

        @pl.when(k == 0)
        def _():
            for s in range(2):
                o_ref[s] = part[:, s * SHARD_IN:(s + 1) * SHARD_IN]

        @pl.when(k > 0)
        def _():
            for s in range(2):
                o_ref[s] = o_ref[s] + part[:, s * SHARD_IN:(s + 1) * SHARD_IN]

    return pl.pallas_call(
        body, name="proj_bwd_weight", grid=(N_DEV // 2, t_len // tk),
        in_specs=[pl.BlockSpec((D_MODEL, tk), lambda j, k: (0, k)),
                  pl.BlockSpec((PAIR_CB, tk, LANE), lambda j, k: (j, k, 0))],
        out_specs=pl.BlockSpec((2, D_MODEL, SHARD_IN), lambda j, k: (j, 0, 0)),
        out_shape=jax.ShapeDtypeStruct((N_DEV, D_MODEL, SHARD_IN), F32),
        compiler_params=_params(("arbitrary", "arbitrary")),
    )(h_t, dpb)


def out_forward(m, wout, t_len):
    tm = _row_tile(t_len, 512)

    def body(a_ref, w_ref, o_ref):
        o_ref[...] = _dot(a_ref[...], w_ref[...])

    return pl.pallas_call(
        body, name="out_fwd", grid=(t_len // tm,),
        in_specs=[pl.BlockSpec((tm, D_MODEL), lambda i: (i, 0)),
                  pl.BlockSpec((D_MODEL, D_MODEL), lambda i: (0, 0))],
        out_specs=pl.BlockSpec((tm, D_MODEL), lambda i: (i, 0)),
        out_shape=jax.ShapeDtypeStruct((t_len, D_MODEL), F32),
        compiler_params=_params(("arbitrary",)),
    )(m, wout)


def out_backward(m, dz, wout, t_len):
    tm = _row_tile(t_len, 512)

    def body(m_ref, dz_ref, w_ref, dm_ref, dw_ref):
        i = pl.program_id(0)
        dm_ref[...] = _dot(dz_ref[...], w_ref[...], _NT)
        part = _dot(m_ref[...], dz_ref[...], _TN)

        @pl.when(i == 0)
        def _():
            dw_ref[...] = part

        @pl.when(i > 0)
        def _():
            dw_ref[...] = dw_ref[...] + part

    return pl.pallas_call(
        body, name="out_bwd", grid=(t_len // tm,),
        in_specs=[pl.BlockSpec((tm, D_MODEL), lambda i: (i, 0)),
                  pl.BlockSpec((tm, D_MODEL), lambda i: (i, 0)),
                  pl.BlockSpec((D_MODEL, D_MODEL), lambda i: (0, 0))],
        out_specs=[pl.BlockSpec((tm, D_MODEL), lambda i: (i, 0)),
                   pl.BlockSpec((D_MODEL, D_MODEL), lambda i: (0, 0))],
        out_shape=[jax.ShapeDtypeStruct((t_len, D_MODEL), F32),
                   jax.ShapeDtypeStruct((D_MODEL, D_MODEL), F32)],
        compiler_params=_params(("arbitrary",)),
    )(m, dz, wout)


def _vec_spec():
    return pl.BlockSpec((1, D_MODEL), lambda i: (0, 0))


def _acc_rows(ref, i, rows):
    @pl.when(i == 0)
    def _():
        ref[...] = jnp.zeros_like(ref)

    for n, row in enumerate(rows):
        ref[n:n + 1, :] = ref[n:n + 1, :] + row


def adaln_forward(x, norm_g, scale1p, shift, t_len):
    tm = _row_tile(t_len, 512)

    def body(x_ref, g_ref, sc_ref, sh_ref, h_ref, ht_ref):
        xv = x_ref[...]
        r = lax.rsqrt(jnp.mean(xv * xv, axis=-1, keepdims=True) + EPS)
        h = xv * r * g_ref[...] * sc_ref[...] + sh_ref[...]
        h_ref[...] = h.astype(h_ref.dtype)
        ht_ref[...] = h.T.astype(ht_ref.dtype)

    return pl.pallas_call(
        body, name="adaln_fwd", grid=(t_len // tm,),
        in_specs=[pl.BlockSpec((tm, D_MODEL), lambda i: (i, 0)), _vec_spec(), _vec_spec(), _vec_spec()],
        out_specs=[pl.BlockSpec((tm, D_MODEL), lambda i: (i, 0)), pl.BlockSpec((D_MODEL, tm), lambda i: (0, i))],
        out_shape=[jax.ShapeDtypeStruct((t_len, D_MODEL), _BF), jax.ShapeDtypeStruct((D_MODEL, t_len), _BF)],
        compiler_params=_params(("arbitrary",)),
    )(x, norm_g, scale1p, shift)


def adaln_backward(dh, x, dy, norm_g, scale1p, t_len):
    tm = _row_tile(t_len, 512)

    def body(dh_ref, x_ref, dy_ref, g_ref, sc_ref, gx_ref, vec_ref):
        i = pl.program_id(0)
        xv, dhv = x_ref[...], dh_ref[...]
        g, sc = g_ref[...], sc_ref[...]
        r = lax.rsqrt(jnp.mean(xv * xv, axis=-1, keepdims=True) + EPS)
        xn = xv * r
        dxn = dhv * (g * sc)
        gx_ref[...] = dy_ref[...] + r * dxn - xn * (r * r) * jnp.mean(xv * dxn, axis=-1, keepdims=True)
        t = dhv * xn
        _acc_rows(vec_ref, i, [jnp.sum(t * sc, axis=0, keepdims=True),
                               jnp.sum(t * g, axis=0, keepdims=True),
                               jnp.sum(dhv, axis=0, keepdims=True)])

    row = pl.BlockSpec((tm, D_MODEL), lambda i: (i, 0))
    return pl.pallas_call(
        body, name="adaln_bwd", grid=(t_len // tm,),
        in_specs=[row, row, row, _vec_spec(), _vec_spec()],
        out_specs=[row, pl.BlockSpec((8, D_MODEL), lambda i: (0, 0))],
        out_shape=[jax.ShapeDtypeStruct((t_len, D_MODEL), F32), jax.ShapeDtypeStruct((8, D_MODEL), F32)],
        compiler_params=_params(("arbitrary",)),
    )(dh, x, dy, norm_g, scale1p)


def _head_norm(o, g):
    r = lax.rsqrt(jnp.mean(o * o, axis=-1, keepdims=True) + EPS)
    return r, o * r * g


def _group_spec(tm, cb):
    return pl.BlockSpec((N_HEADS, tm, LANE), lambda i, cb=cb: (cb // N_HEADS, i, 0))


def merge_forward(oa, ob, pb, hg_g, ret_g, t_len):
    tm = _row_tile(t_len, 256)

    def body(oa_ref, ob_ref, hz_ref, rz_ref, ga_ref, gb_ref, hg_ref, rg_ref, m_ref):
        for hh in range(N_HEADS):
            ls = slice(hh * LANE, (hh + 1) * LANE)
            _, na = _head_norm(oa_ref[hh], hg_ref[:, ls])
            _, nb = _head_norm(ob_ref[hh], rg_ref[:, ls])
            hz, rz = hz_ref[hh], rz_ref[hh]
            ua = na * (hz * _sigmoid(hz))
            ub = nb * (rz * _sigmoid(rz))
            m_ref[:, ls] = (_sigmoid(ga_ref[hh]) * ua + _sigmoid(gb_ref[hh]) * ub).astype(m_ref.dtype)

    head = pl.BlockSpec((N_HEADS, tm, LANE), lambda i: (0, i, 0))
    return pl.pallas_call(
        body, name="merge_fwd", grid=(t_len // tm,),
        in_specs=[head, head, _group_spec(tm, CB_HZ), _group_spec(tm, CB_RZ), _group_spec(tm, CB_GA),
                  _group_spec(tm, CB_GB), _vec_spec(), _vec_spec()],
        out_specs=pl.BlockSpec((tm, D_MODEL), lambda i: (i, 0)),
        out_shape=jax.ShapeDtypeStruct((t_len, D_MODEL), _BF),
        compiler_params=_params(("arbitrary",)),
    )(oa, ob, pb, pb, pb, pb, hg_g, ret_g)


def merge_backward(dm, oa, ob, pb, hg_g, ret_g, t_len):
    tm = _row_tile(t_len, 256)

    def body(dm_ref, oa_ref, ob_ref, hz_ref, rz_ref, ga_ref, gb_ref, hg_ref, rg_ref,
             doa_ref, dob_ref, vec_ref, dpb_ref, bufs, sems):
        i = pl.program_id(0)
        dg_a, dg_b = [], []
        for hh in range(N_HEADS):
            ls = slice(hh * LANE, (hh + 1) * LANE)
            dmh = dm_ref[:, ls]
            for side, (o_ref, z_ref, gate_ref, g_ref, do_ref, acc) in enumerate((
                    (oa_ref, hz_ref, ga_ref, hg_ref, doa_ref, dg_a),
                    (ob_ref, rz_ref, gb_ref, rg_ref, dob_ref, dg_b))):
                o, z, gt, g = o_ref[hh], z_ref[hh], gate_ref[hh], g_ref[:, ls]
                r, n = _head_norm(o, g)
                sz = _sigmoid(z)
                silu = z * sz
                sgt = _sigmoid(gt)
                du = dmh * sgt
                bufs[2 + side, hh] = dmh * (n * silu) * (sgt * (1.0 - sgt))
                bufs[side, hh] = du * n * (sz * (1.0 + z * (1.0 - sz)))
                dn = du * silu
                acc.append(jnp.sum(dn * (o * r), axis=0, keepdims=True))
                gdn = dn * g
                do_ref[hh] = r * gdn - o * (r * r * r) * jnp.mean(o * gdn, axis=-1, keepdims=True)
        _acc_rows(vec_ref, i, [jnp.concatenate(dg_a, axis=1), jnp.concatenate(dg_b, axis=1)])
        rows = pl.ds(pl.multiple_of(i * tm, tm), tm)
        copies = [pltpu.make_async_copy(bufs.at[n], dpb_ref.at[pl.ds(cb, N_HEADS), rows], sems.at[n])
                  for n, cb in enumerate((CB_HZ, CB_RZ, CB_GA, CB_GB))]
        for cp in copies:
            cp.start()
        for cp in copies:
            cp.wait()

    head = pl.BlockSpec((N_HEADS, tm, LANE), lambda i: (0, i, 0))
    return pl.pallas_call(
        body, name="merge_bwd", grid=(t_len // tm,),
        in_specs=[pl.BlockSpec((tm, D_MODEL), lambda i: (i, 0)), head, head,
                  _group_spec(tm, CB_HZ), _group_spec(tm, CB_RZ), _group_spec(tm, CB_GA), _group_spec(tm, CB_GB),
                  _vec_spec(), _vec_spec()],
        out_specs=[head, head, pl.BlockSpec((8, D_MODEL), lambda i: (0, 0)), pl.BlockSpec(memory_space=pl.ANY)],
        out_shape=[jax.ShapeDtypeStruct((N_HEADS, t_len, LANE), F32),
                   jax.ShapeDtypeStruct((N_HEADS, t_len, LANE), F32),
                   jax.ShapeDtypeStruct((8, D_MODEL), F32),
                   jax.ShapeDtypeStruct((N_CB, t_len, LANE), F32)],
        scratch_shapes=[pltpu.VMEM((4, N_HEADS, tm, LANE), F32), pltpu.SemaphoreType.DMA((4,))],
        compiler_params=_params(("arbitrary",)),
    )(dm, oa, ob, pb, pb, pb, pb, hg_g, ret_g)


def tail(x, z, target, gate, final_g, t_len):
    tm = _row_tile(t_len, 512)

    def body(x_ref, z_ref, t_ref, gate_ref, fg_ref, dy_ref, dz_ref, vec_ref):
        i = pl.program_id(0)
        zv, gt, fg = z_ref[...], gate_ref[...], fg_ref[...]
        y = x_ref[...] + gt * zv
        r = lax.rsqrt(jnp.mean(y * y, axis=-1, keepdims=True) + EPS)
        yn = y * r
        err = yn * fg - t_ref[...]
        loss = 0.5 * jnp.sum(jnp.mean(err * err, axis=-1, keepdims=True), axis=0, keepdims=True)
        dout = err * (1.0 / D_MODEL)
        gd = dout * fg
        dy = r * gd - yn * (r * r) * jnp.mean(y * gd, axis=-1, keepdims=True)
        dy_ref[...] = dy
        dz_ref[...] = (dy * gt).astype(dz_ref.dtype)
        _acc_rows(vec_ref, i, [jnp.sum(dout * yn, axis=0, keepdims=True),
                               jnp.sum(dy * zv, axis=0, keepdims=True),
                               jnp.broadcast_to(loss, (1, D_MODEL))])

    row = pl.BlockSpec((tm, D_MODEL), lambda i: (i, 0))
    return pl.pallas_call(
        body, name="tail", grid=(t_len // tm,),
        in_specs=[row, row, row, _vec_spec(), _vec_spec()],
        out_specs=[row, row, pl.BlockSpec((8, D_MODEL), lambda i: (0, 0))],
        out_shape=[jax.ShapeDtypeStruct((t_len, D_MODEL), F32), jax.ShapeDtypeStruct((t_len, D_MODEL), _BF),
                   jax.ShapeDtypeStruct((8, D_MODEL), F32)],
        compiler_params=_params(("arbitrary",)),
    )(x, z, target, gate, final_g)


def device_step(x, target, mod, lb, wt, wout, norm_g, hg_g, ret_g, final_g):
    t_len = x.shape[0]
    shift, scale, gate = mod[:, :D_MODEL], mod[:, D_MODEL:2 * D_MODEL], mod[:, 2 * D_MODEL:]
    scale1p = 1.0 + scale
    cos_t, sin_t = _rope_tables(t_len)
    h, h_t = adaln_forward(x, norm_g, scale1p, shift, t_len)
    pb = proj_forward(h, wt, t_len)
    oa, ssave = hgrn_forward(pb, lb, t_len)
    ob, rsave = retention_forward(pb, cos_t, sin_t, t_len)
    m = merge_forward(oa, ob, pb, hg_g, ret_g, t_len)
    z = out_forward(m, wout, t_len)
    dy, dz, vec_tail = tail(x, z, target, gate, final_g, t_len)
    dm, dwout = out_backward(m, dz, wout, t_len)
    doa, dob, vec_merge, dpb = merge_backward(dm, oa, ob, pb, hg_g, ret_g, t_len)
    dpb, dlb = hgrn_backward(pb, lb, doa, ssave, dpb, t_len)
    dpb = retention_backward(pb, cos_t, sin_t, dob, rsave, dpb, t_len)
    dh = proj_backward_input(dpb, wt, t_len)
    dwin = proj_backward_weight(h_t, dpb, t_len)
    grad_x, vec_ada = adaln_backward(dh, x, dy, norm_g, scale1p, t_len)
    return grad_x, dwin, dwout, vec_tail, vec_merge, vec_ada, dlb


PACK_ROWS = 16
ROW_NORM_G, ROW_LB, ROW_HG_G, ROW_RET_G, ROW_FINAL_G, ROW_SHIFT, ROW_SCALE, ROW_GATE, ROW_LOSS = range(9)


def _mesh_pos():
    return lax.axis_index("x"), lax.axis_index("y"), lax.axis_index("c")


def _lin(pos):
    return 4 * pos[0] + 2 * pos[1] + pos[2]


def _xor_peer(pos, k):
    return tuple(1 - p if (k >> s) & 1 else p for p, s in zip(pos, (2, 1, 0)))


def _other_chips(pos):
    x, y, _ = pos
    return [(1 - x, y), (x, 1 - y), (1 - x, 1 - y)]


def _remote(src, dst, send_sem, recv_sem, to):
    return pltpu.make_async_remote_copy(src_ref=src, dst_ref=dst, send_sem=send_sem, recv_sem=recv_sem,
                                        device_id=to, device_id_type=MESH)


def pre_exchange(c, w_ada, b_ada, logits):
    def body(c_ref, wada_ref, bada_ref, logit_ref, mod_ref, scall_ref, lb_ref,
             cg_ref, modall_ref, parts_ref, send1, recv1, send2, recv2):
        pos = _mesh_pos()
        cv = c_ref[...]
        slot = lambda p: pl.ds(pl.multiple_of(8 * _lin(p), 8), 8)
        cg_ref[slot(pos), :] = jnp.broadcast_to(cv * _sigmoid(cv), (8, D_MODEL))
        lb_ref[...] = _sigmoid(logit_ref[0:1, :] - logit_ref[1:2, :])
        peers = [_xor_peer(pos, k) for k in range(1, N_DEV)]
        gather = [_remote(cg_ref.at[slot(pos)], cg_ref.at[slot(pos)], send1.at[n], recv1.at[n], p)
                  for n, p in enumerate(peers)]
        for cp in gather:
            cp.start()
        for n, p in enumerate(peers):
            _remote(cg_ref.at[slot(p)], cg_ref.at[slot(p)], send1.at[n], recv1.at[n], p).wait_recv()
        modall_ref[...] = _dot_f32(cg_ref[...], wada_ref[...])
        scatter = [_remote(modall_ref.at[slot(p)], parts_ref.at[slot(pos)], send2.at[n], recv2.at[n], p)
                   for n, p in enumerate(peers)]
        for cp in scatter:
            cp.start()
        parts_ref[slot(pos), :] = modall_ref[slot(pos), :]
        for n, p in enumerate(peers):
            _remote(modall_ref.at[slot(p)], parts_ref.at[slot(p)], send2.at[n], recv2.at[n], p).wait_recv()
        for cp in gather + scatter:
            cp.wait_send()
        for j in range(N_DEV):
            cols = slice(j * SHARD_ADA, (j + 1) * SHARD_ADA)
            mod_ref[:, cols] = parts_ref[8 * j:8 * j + 1, :] + bada_ref[:, cols]
            scall_ref[j:j + 1, :] = cg_ref[8 * j:8 * j + 1, :]

    vmem = pl.BlockSpec(memory_space=pltpu.VMEM)
    return pl.pallas_call(
        body, name="pre_exchange",
        in_specs=[vmem] * 4, out_specs=[vmem] * 3,
        out_shape=[jax.ShapeDtypeStruct((1, 3 * D_MODEL), F32), jax.ShapeDtypeStruct((N_DEV, D_MODEL), F32),
                   jax.ShapeDtypeStruct((1, D_MODEL), F32)],
        scratch_shapes=[pltpu.VMEM((N_DEV * 8, D_MODEL), F32), pltpu.VMEM((N_DEV * 8, SHARD_ADA), F32),
                        pltpu.VMEM((N_DEV * 8, SHARD_ADA), F32)] + [pltpu.SemaphoreType.DMA((N_DEV - 1,))] * 4,
        compiler_params=pltpu.CompilerParams(vmem_limit_bytes=VMEM_LIMIT),
    )(c, w_ada, b_ada, logits)


def weight_gather(win_sh, wout_sh):
    def body(win_ref, wout_ref, wg_ref, woutg_ref, send, recv, local):
        pos = _mesh_pos()
        x, y, c = pos
        sibling = (x, y, 1 - c)
        chips = _other_chips(pos)
        first, passed, mine = [], [], []
        for a, (src, out) in enumerate(((win_ref, wg_ref), (wout_ref, woutg_ref))):
            def copy(k, block, to, src_ref=None, a=a, out=out):
                dst = out.at[_lin(block)]
                return _remote(dst if src_ref is None else src_ref, dst, send.at[7 * a + k], recv.at[7 * a + k], to)
            mine.append(pltpu.make_async_copy(src, out.at[_lin(pos)], local.at[a]))
            first.append(copy(0, pos, sibling, src))
            first += [copy(1 + j, pos, (*chip, c), src) for j, chip in enumerate(chips)]
            passed.append([copy(4 + j, (*chip, c), sibling) for j, chip in enumerate(chips)])
        for cp in mine + first:
            cp.start()
        for a, out in enumerate((wg_ref, woutg_ref)):
            for j, chip in enumerate(chips):
                dst = out.at[_lin((*chip, c))]
                _remote(dst, dst, send.at[7 * a + 1 + j], recv.at[7 * a + 1 + j], pos).wait_recv()
                passed[a][j].start()
        for a, out in enumerate((wg_ref, woutg_ref)):
            dst = out.at[_lin(sibling)]
            _remote(dst, dst, send.at[7 * a], recv.at[7 * a], pos).wait_recv()
            for j, chip in enumerate(chips):
                dst = out.at[_lin((*chip, 1 - c))]
                _remote(dst, dst, send.at[7 * a + 4 + j], recv.at[7 * a + 4 + j], pos).wait_recv()
        for cp in first + passed[0] + passed[1]:
            cp.wait_send()
        for cp in mine:
            cp.wait()

    any_spec = pl.BlockSpec(memory_space=pl.ANY)
    return pl.pallas_call(
        body, name="weight_gather",
        in_specs=[any_spec, any_spec], out_specs=[any_spec, any_spec],
        out_shape=[jax.ShapeDtypeStruct((N_DEV,) + win_sh.shape, win_sh.dtype),
                   jax.ShapeDtypeStruct((N_DEV,) + wout_sh.shape, wout_sh.dtype)],
        scratch_shapes=[pltpu.SemaphoreType.DMA((14,)), pltpu.SemaphoreType.DMA((14,)),
                        pltpu.SemaphoreType.DMA((2,))],
    )(win_sh, wout_sh)


def grad_pair_exchange(g_in, g_out):
    def body(gin_ref, gout_ref, ra_ref, rb_ref, send, recv):
        pos = _mesh_pos()
        x, y, c = pos
        sibling = (x, y, 1 - c)
        copies = []
        for a, (src, dst) in enumerate(((gin_ref, ra_ref), (gout_ref, rb_ref))):
            for q in range(4):
                copies.append(_remote(src.at[2 * q + (1 - c)], dst.at[q], send.at[4 * a + q], recv.at[4 * a + q],
                                      sibling))
        for cp in copies:
            cp.start()
        for cp in copies:
            cp.wait_recv()
        for cp in copies:
            cp.wait_send()

    any_spec = pl.BlockSpec(memory_space=pl.ANY)
    return pl.pallas_call(
        body, name="grad_pair_exchange",
        in_specs=[any_spec, any_spec], out_specs=[any_spec, any_spec],
        out_shape=[jax.ShapeDtypeStruct((4,) + g_in.shape[1:], F32), jax.ShapeDtypeStruct((4,) + g_out.shape[1:], F32)],
        scratch_shapes=[pltpu.SemaphoreType.DMA((8,)), pltpu.SemaphoreType.DMA((8,))],
    )(g_in, g_out)


def pair_sum(g_in, ra, g_out, rb, c_idx):
    tr = 256

    def body(c_ref, gin_ref, ra_ref, gout_ref, rb_ref, sb_ref, sbo_ref):
        del c_ref
        sb_ref[...] = (gin_ref[...] + ra_ref[...]).astype(sb_ref.dtype)
        sbo_ref[...] = gout_ref[...] + rb_ref[...]

    n_i = D_MODEL // tr
    return pl.pallas_call(
        body, name="pair_sum",
        grid_spec=pltpu.PrefetchScalarGridSpec(
            num_scalar_prefetch=1, grid=(4, n_i),
            in_specs=[pl.BlockSpec((None, tr, SHARD_IN), lambda q, i, c: (2 * q + c[0], i, 0)),
                      pl.BlockSpec((None, tr, SHARD_IN), lambda q, i, c: (q, i, 0)),
                      pl.BlockSpec((None, SHARD_OUT // n_i, D_MODEL), lambda q, i, c: (2 * q + c[0], i, 0)),
                      pl.BlockSpec((None, SHARD_OUT // n_i, D_MODEL), lambda q, i, c: (q, i, 0))],
            out_specs=[pl.BlockSpec((None, tr, SHARD_IN), lambda q, i, c: (q, i, 0)),
                       pl.BlockSpec((None, SHARD_OUT // n_i, D_MODEL), lambda q, i, c: (q, i, 0))]),
        out_shape=[jax.ShapeDtypeStruct(ra.shape, _BF), jax.ShapeDtypeStruct(rb.shape, F32)],
        compiler_params=_params(("arbitrary", "arbitrary")),
    )(c_idx, g_in, ra, g_out, rb)


def grad_chip_exchange(sb, sbo, pack):
    def body(sb_ref, sbo_ref, pack_ref, rc_ref, rco_ref, packs_ref, send, recv, local, psend, precv):
        pos = _mesh_pos()
        x, y, c = pos
        me = _lin(pos)
        my_chip = 2 * x + y
        chips = _other_chips(pos)
        packs_ref[me] = pack_ref[...]
        peers = [_xor_peer(pos, k) for k in range(1, N_DEV)]
        gather = [_remote(packs_ref.at[me], packs_ref.at[me], psend.at[n], precv.at[n], p) for n, p in enumerate(peers)]
        mine, copies = [], []
        for a, (src, dst) in enumerate(((sb_ref, rc_ref), (sbo_ref, rco_ref))):
            mine.append(pltpu.make_async_copy(src.at[my_chip], dst.at[my_chip], local.at[a]))
            for j, chip in enumerate(chips):
                copies.append(_remote(src.at[2 * chip[0] + chip[1]], dst.at[my_chip], send.at[3 * a + j],
                                      recv.at[3 * a + j], (*chip, c)))
        for cp in gather + mine + copies:
            cp.start()
        for n, p in enumerate(peers):
            _remote(packs_ref.at[_lin(p)], packs_ref.at[_lin(p)], psend.at[n], precv.at[n], p).wait_recv()
        for a, (src, dst) in enumerate(((sb_ref, rc_ref), (sbo_ref, rco_ref))):
            for j, chip in enumerate(chips):
                slot = dst.at[2 * chip[0] + chip[1]]
                _remote(slot, slot, send.at[3 * a + j], recv.at[3 * a + j], pos).wait_recv()
        for cp in gather + copies:
            cp.wait_send()
        for cp in mine:
            cp.wait()

    any_spec = pl.BlockSpec(memory_space=pl.ANY)
    vmem = pl.BlockSpec(memory_space=pltpu.VMEM)
    return pl.pallas_call(
        body, name="grad_chip_exchange",
        in_specs=[any_spec, any_spec, vmem], out_specs=[any_spec, any_spec, vmem],
        out_shape=[jax.ShapeDtypeStruct(sb.shape, sb.dtype), jax.ShapeDtypeStruct(sbo.shape, sbo.dtype),
                   jax.ShapeDtypeStruct((N_DEV,) + pack.shape, F32)],
        scratch_shapes=[pltpu.SemaphoreType.DMA((6,)), pltpu.SemaphoreType.DMA((6,)), pltpu.SemaphoreType.DMA((2,)),
                        pltpu.SemaphoreType.DMA((N_DEV - 1,)), pltpu.SemaphoreType.DMA((N_DEV - 1,))],
    )(sb, sbo, pack)


def pack_rows(vec_tail, vec_merge, vec_ada, dlb):
    def body(tail_ref, merge_ref, ada_ref, dlb_ref, o_ref):
        rows = [ada_ref[0:1, :], dlb_ref[...], merge_ref[0:1, :], merge_ref[1:2, :], tail_ref[0:1, :],
                ada_ref[2:3, :], ada_ref[1:2, :], tail_ref[1:2, :], tail_ref[2:3, :]]
        o_ref[...] = jnp.zeros_like(o_ref)
        for n, row in enumerate(rows):
            o_ref[n:n + 1, :] = row

    vmem = pl.BlockSpec(memory_space=pltpu.VMEM)
    return pl.pallas_call(body, name="pack_rows", in_specs=[vmem] * 4, out_specs=vmem,
                          out_shape=jax.ShapeDtypeStruct((PACK_ROWS, D_MODEL), F32))(vec_tail, vec_merge, vec_ada, dlb)


def _adamw(w, g, m, v):
    m = ADAM_B1 * m + (1.0 - ADAM_B1) * g
    v = ADAM_B2 * v + (1.0 - ADAM_B2) * (g * g)
    m_hat = m / (1.0 - ADAM_B1 ** ADAM_STEP)
    v_hat = v / (1.0 - ADAM_B2 ** ADAM_STEP)
    delta = -ADAM_LR * (m_hat / (jnp.sqrt(v_hat) + ADAM_EPS) + ADAM_WD * w)
    return delta, m, v


def adam_shard(parts, w, m, v, name):
    rows, cols = w.shape
    tr = min(rows, 128)

    def body(p0, p1, p2, p3, w_ref, m_ref, v_ref, g_ref, d_ref, nm_ref, nv_ref):
        g = ((p0[...].astype(F32) + p1[...].astype(F32)) + p2[...].astype(F32)) + p3[...].astype(F32)
        g_ref[...] = g
        d_ref[...], nm_ref[...], nv_ref[...] = _adamw(w_ref[...], g, m_ref[...], v_ref[...])

    part = lambda q: pl.BlockSpec((None, tr, cols), lambda i, q=q: (q, i, 0))
    tile = pl.BlockSpec((tr, cols), lambda i: (i, 0))
    return pl.pallas_call(
        body, name=name, grid=(rows // tr,),
        in_specs=[part(0), part(1), part(2), part(3), tile, tile, tile],
        out_specs=[tile] * 4, out_shape=[jax.ShapeDtypeStruct(w.shape, F32)] * 4,
        compiler_params=_params(("arbitrary",)),
    )(parts, parts, parts, parts, w, m, v)


def adam_ada(sc_t, dmod_all, me_idx, w, m, v):
    def body(me_ref, sc_ref, dm_ref, w_ref, m_ref, v_ref, g_ref, d_ref, nm_ref, nv_ref):
        del me_ref
        g = _dot_f32(sc_ref[...], dm_ref[...])
        g_ref[...] = g
        d_ref[...], nm_ref[...], nv_ref[...] = _adamw(w_ref[...], g, m_ref[...], v_ref[...])

    full = pl.BlockSpec(w.shape, lambda i, me: (0, 0))
    return pl.pallas_call(
        body, name="adam_ada",
        grid_spec=pltpu.PrefetchScalarGridSpec(
            num_scalar_prefetch=1, grid=(1,),
            in_specs=[pl.BlockSpec(sc_t.shape, lambda i, me: (0, 0)),
                      pl.BlockSpec((LANE, SHARD_ADA), lambda i, me: (0, me[0])), full, full, full],
            out_specs=[full] * 4),
        out_shape=[jax.ShapeDtypeStruct(w.shape, F32)] * 4,
        compiler_params=_params(("arbitrary",)),
    )(me_idx, sc_t, dmod_all, w, m, v)


def adam_vectors(packs, lb, params, ms, vs):
    n = len(params)

    def body(*refs):
        packs_ref, lb_ref = refs[0], refs[1]
        w_refs, m_refs, v_refs = refs[2:2 + n], refs[2 + n:2 + 2 * n], refs[2 + 2 * n:2 + 3 * n]
        loss_ref = refs[2 + 3 * n]
        outs = refs[3 + 3 * n:3 + 7 * n]
        tot_ref = refs[3 + 7 * n]
        tot = packs_ref[0]
        for d in range(1, N_DEV):
            tot = tot + packs_ref[d]
        tot_ref[...] = tot
        row = lambda r: tot_ref[r:r + 1, :]
        lbv = lb_ref[...]
        dl0 = row(ROW_LB) * lbv * (1.0 - lbv)
        grads = [[row(ROW_NORM_G)],
                 [jnp.concatenate([row(ROW_SHIFT), row(ROW_SCALE), row(ROW_GATE)], axis=1)],
                 [dl0, -dl0],
                 [row(ROW_HG_G)], [row(ROW_RET_G)], [row(ROW_FINAL_G)]]
        loss_ref[...] = tot_ref[ROW_LOSS:ROW_LOSS + 1, 0:LANE]
        for j, g_rows in enumerate(grads):
            for r, g in enumerate(g_rows):
                rs = slice(r, r + 1)
                d, nm, nv = _adamw(w_refs[j][rs, :], g, m_refs[j][rs, :], v_refs[j][rs, :])
                outs[4 * j][rs, :] = g
                outs[4 * j + 1][rs, :] = d
                outs[4 * j + 2][rs, :] = nm
                outs[4 * j + 3][rs, :] = nv

    vmem = pl.BlockSpec(memory_space=pltpu.VMEM)
    out_shape = [jax.ShapeDtypeStruct((1, LANE), F32)]
    for w in params:
        out_shape += [jax.ShapeDtypeStruct(w.shape, F32)] * 4
    return pl.pallas_call(
        body, name="adam_vectors", in_specs=[vmem] * (2 + 3 * n), out_specs=[vmem] * len(out_shape),
        out_shape=out_shape, scratch_shapes=[pltpu.VMEM((PACK_ROWS, D_MODEL), F32)],
    )(packs, lb, *params, *ms, *vs)


def kernel(x, c, norm_g, w_ada, b_ada, w_in, hg_lb_logits, hg_norm_g, ret_norm_g, w_out, final_g, loss_target, m_norm_g, m_w_ada, m_b_ada, m_w_in, m_hg_lb_logits, m_hg_norm_g, m_ret_norm_g, m_w_out, m_final_g, v_norm_g, v_w_ada, v_b_ada, v_w_in, v_hg_lb_logits, v_hg_norm_g, v_ret_norm_g, v_w_out, v_final_g):
    pos = _mesh_pos()
    me_idx = jnp.reshape(_lin(pos), (1,)).astype(jnp.int32)
    c_idx = jnp.reshape(pos[2], (1,)).astype(jnp.int32)
    vec = lambda a: a.reshape(1, D_MODEL)

    mod, scall, lb = pre_exchange(c, w_ada[0], b_ada, hg_lb_logits)
    wtg, woutg = weight_gather(w_in[0].T.astype(_BF), w_out[0].astype(_BF))
    grad_x, dwin, dwout, vec_tail, vec_merge, vec_ada, dlb = device_step(
        x[0], loss_target[0], mod, lb, wtg.reshape(D_IN, D_MODEL), woutg.reshape(D_MODEL, D_MODEL), norm_g,
        hg_norm_g, ret_norm_g, vec(final_g))

    dwout = dwout.reshape(N_DEV, SHARD_OUT, D_MODEL)
    ra, rb = grad_pair_exchange(dwin, dwout)
    sb, sbo = pair_sum(dwin, ra, dwout, rb, c_idx)
    pack = pack_rows(vec_tail, vec_merge, vec_ada, dlb)
    rc, rco, packs = grad_chip_exchange(sb, sbo, pack)

    g_in, d_in, nm_in, nv_in = adam_shard(rc, w_in[0], m_w_in[0], v_w_in[0], "adam_w_in")
    g_out, d_out, nm_out, nv_out = adam_shard(rco, w_out[0], m_w_out[0], v_w_out[0], "adam_w_out")
    dmod_all = packs[:, ROW_SHIFT:ROW_GATE + 1, :].reshape(N_DEV, 3 * D_MODEL)
    dmod_all = jnp.pad(dmod_all, ((0, LANE - N_DEV), (0, 0)))
    sc_t = jnp.pad(scall.T, ((0, 0), (0, LANE - N_DEV)))
    g_ada, d_ada, nm_ada, nv_ada = adam_ada(sc_t, dmod_all, me_idx, w_ada[0], m_w_ada[0], v_w_ada[0])
    small = adam_vectors(
        packs, lb,
        (norm_g, b_ada, hg_lb_logits, hg_norm_g, ret_norm_g, vec(final_g)),
        (m_norm_g, m_b_ada, m_hg_lb_logits, m_hg_norm_g, m_ret_norm_g, vec(m_final_g)),
        (v_norm_g, v_b_ada, v_hg_lb_logits, v_hg_norm_g, v_ret_norm_g, vec(v_final_g)))
    loss = small[0][0, 0]
    (g_ng, d_ng, nm_ng, nv_ng), (g_b, d_b, nm_b, nv_b), (g_lb, d_lb, nm_lb, nv_lb), (g_hg, d_hg, nm_hg, nv_hg), \
        (g_rg, d_rg, nm_rg, nv_rg), (g_fg, d_fg, nm_fg, nv_fg) = [small[1 + 4 * j:5 + 4 * j] for j in range(6)]
    flat = lambda a: a.reshape(D_MODEL)

    def group(ng, ada, b, win, lbl, hg, rg, wo, fg):
        return (ng, ada[None], b, win[None], lbl, hg, rg, wo[None], flat(fg))

    return (loss, grad_x[None],
            *group(g_ng, g_ada, g_b, g_in, g_lb, g_hg, g_rg, g_out, g_fg),
            *group(d_ng, d_ada, d_b, d_in, d_lb, d_hg, d_rg, d_out, d_fg),
            *group(nm_ng, nm_ada, nm_b, nm_in, nm_lb, nm_hg, nm_rg, nm_out, nm_fg),
            *group(nv_ng, nv_ada, nv_b, nv_in, nv_lb, nv_hg, nv_rg, nv_out, nv_fg))
```

```python
import functools

import numpy as np
import jax
import jax.numpy as jnp
from jax import lax
from jax.experimental import pallas as pl
from jax.experimental.pallas import tpu as pltpu

F32 = jnp.float32
_BF = jnp.bfloat16

D_MODEL = 1024
N_HEADS = 8
LANE = 128
RET_DK = 64
D_IN = 9216
N_DEV = 8
SHARD_IN = D_IN // N_DEV
SHARD_ADA = 3 * D_MODEL // N_DEV
SHARD_OUT = D_MODEL // N_DEV
N_CB = D_IN // LANE
CB_PER_SHARD = SHARD_IN // LANE
CHUNK = 128
N_LEVELS = 7
EPS = 1e-6
ROPE_BASE = 10000.0
CB_HQ, CB_HF, CB_HI, CB_HZ, CB_RQ, CB_RK, CB_RV, CB_RZ, CB_GA, CB_GB = 0, 8, 16, 24, 32, 36, 40, 48, 56, 64
VMEM_LIMIT = 56 * 1024 * 1024

ADAM_LR, ADAM_B1, ADAM_B2, ADAM_EPS, ADAM_WD, ADAM_STEP = 0.001, 0.9, 0.999, 1e-08, 0.01, 10

_NN = (((1,), (0,)), ((), ()))
_NT = (((1,), (1,)), ((), ()))
_TN = (((0,), (0,)), ((), ()))
MESH = pl.DeviceIdType.MESH


def _dot(a, b, dims=_NN):
    return lax.dot_general(a.astype(_BF), b.astype(_BF), dims, preferred_element_type=F32)


def _split2(a):
    hi = a.astype(_BF)
    lo = (a - hi.astype(F32)).astype(_BF)
    return jnp.concatenate([hi, lo], axis=1)


def _dot_sel(sel, a):
    n = a.shape[1]
    r = lax.dot_general(sel.astype(_BF), _split2(a), _NN, preferred_element_type=F32)
    return r[:, :n] + r[:, n:]


def _dot_f32(a, b):
    def pieces(v):
        p1 = v.astype(_BF)
        r1 = v - p1.astype(F32)
        p2 = r1.astype(_BF)
        p3 = (r1 - p2.astype(F32)).astype(_BF)
        return p1, p2, p3
    a1, a2, a3 = pieces(a)
    b1, b2, b3 = pieces(b)
    d = lambda u, v: lax.dot_general(u, v, _NN, preferred_element_type=F32)
    return ((d(a1, b3) + d(a2, b2) + d(a3, b1)) + (d(a1, b2) + d(a2, b1))) + d(a1, b1)


def _sigmoid(v):
    return 1.0 / (1.0 + jnp.exp(-v))


def _params(sem=None):
    return pltpu.CompilerParams(dimension_semantics=sem, vmem_limit_bytes=VMEM_LIMIT)


def _hgrn_consts():
    c, nl = CHUNK, N_LEVELS
    t = np.arange(c)[:, None]
    j = np.arange(c)[None, :]
    sel = [j <= t]
    masks = [j == t]
    for l in range(1, nl + 1):
        m = ((t >> l) << l) + (1 << (l - 1)) - 1
        sec = t > m
        sel.append(np.where(sec, (j > m) & (j <= t), (j > t) & (j <= m)))
        same = (t >> l) == (j >> l)
        masks.append(same & sec & (j <= m))
    sel.append(j > t)
    sel = np.concatenate(sel, 0).astype(np.float32)
    masks = np.stack(masks).astype(np.float32)
    return jnp.asarray(sel, _BF), jnp.asarray(masks, F32)


def _hgrn_chunk(hq, hf, hi, lbv, sel_ref, lvl_ref):
    c, nl = CHUNK, N_LEVELS
    sq = _sigmoid(hq)
    q = hq * sq
    sg = _sigmoid(hf)
    omlb = 1.0 - lbv
    f = lbv + omlb * sg
    k = 1.0 - f
    logf = jnp.log(f)
    e_all = _dot_sel(sel_ref[...], logf)
    eb = jnp.exp(e_all[0:c])
    erem = jnp.exp(e_all[(nl + 1) * c:(nl + 2) * c])
    ebc = jnp.exp(jnp.sum(logf, axis=0, keepdims=True))
    lev = [None] + [jnp.exp(e_all[l * c:(l + 1) * c]) for l in range(1, nl + 1)]
    return dict(sq=sq, q=q, sg=sg, omlb=omlb, f=f, k=k, v=hi, eb=eb, erem=erem, ebc=ebc, lev=lev)


def _hgrn_scores(a, lvl_ref):
    q, k = a["q"], a["k"]
    acc = lvl_ref[0] * _dot(q, k, _NT)
    for l in range(1, N_LEVELS + 1):
        e = a["lev"][l]
        acc = acc + lvl_ref[l] * _dot(q * e, k * e, _NT)
    return acc


def _time_block(t_len):
    return min(t_len, 1024)


def hgrn_forward(pb, lb, t_len):
    nc = t_len // CHUNK
    tb = _time_block(t_len)
    ncb = tb // CHUNK
    sel, lvl = _hgrn_consts()

    def body(hq_ref, hf_ref, hi_ref, lb_ref, sel_ref, lvl_ref, o_ref, ssave_ref, st_ref):
        @pl.when(pl.program_id(1) == 0)
        def _():
            st_ref[...] = jnp.zeros_like(st_ref)

        def chunk(ci, carry):
            r = pl.ds(pl.multiple_of(ci * CHUNK, CHUNK), CHUNK)
            for hd in range(2):
                lbv = lb_ref[:, hd * LANE:(hd + 1) * LANE]
                a = _hgrn_chunk(hq_ref[hd, r, :], hf_ref[hd, r, :], hi_ref[hd, r, :], lbv, sel_ref, lvl_ref)
                st = st_ref[hd]
                ssave_ref[hd, ci] = st
                o_ref[hd, r, :] = _dot(a["q"] * a["eb"], st, _NT) + _dot(_hgrn_scores(a, lvl_ref), a["v"])
                st_ref[hd] = st * a["ebc"] + _dot(a["v"], a["k"] * a["erem"], _TN)
            return carry

        lax.fori_loop(0, ncb, chunk, 0)

    pair = lambda base: pl.BlockSpec((2, tb, LANE), lambda p, t, base=base: (base // 2 + p, t, 0))
    return pl.pallas_call(
        body, name="hgrn_fwd", grid=(N_HEADS // 2, t_len // tb),
        in_specs=[pair(CB_HQ), pair(CB_HF), pair(CB_HI),
                  pl.BlockSpec((1, 2 * LANE), lambda p, t: (0, p)),
                  pl.BlockSpec(sel.shape, lambda p, t: (0, 0)),
                  pl.BlockSpec(lvl.shape, lambda p, t: (0, 0, 0))],
        out_specs=[pl.BlockSpec((2, tb, LANE), lambda p, t: (p, t, 0)),
                   pl.BlockSpec((2, ncb, LANE, LANE), lambda p, t: (p, t, 0, 0))],
        out_shape=[jax.ShapeDtypeStruct((N_HEADS, t_len, LANE), F32),
                   jax.ShapeDtypeStruct((N_HEADS, nc, LANE, LANE), F32)],
        scratch_shapes=[pltpu.VMEM((2, LANE, LANE), F32)],
        compiler_params=_params(("arbitrary", "arbitrary")),
    )(pb, pb, pb, lb, sel, lvl)


def hgrn_backward(pb, lb, do, ssave, dpb, t_len):
    tb = _time_block(t_len)
    ncb, ntb = tb // CHUNK, t_len // tb
    sel, lvl = _hgrn_consts()

    def body(hq_ref, hf_ref, hi_ref, lb_ref, do_ref, ssave_ref, sel_ref, lvl_ref, dpb_in,
             dpb_ref, dlb_ref, dq_buf, df_buf, di_buf, dst_ref, sems):
        del dpb_in
        p, t = pl.program_id(0), pl.program_id(1)

        @pl.when(t == 0)
        def _():
            dst_ref[...] = jnp.zeros_like(dst_ref)
            dlb_ref[...] = jnp.zeros_like(dlb_ref)

        def chunk(i, carry):
            ci = ncb - 1 - i
            r = pl.ds(pl.multiple_of(ci * CHUNK, CHUNK), CHUNK)
            for hd in range(2):
                head_chunk(hd, ci, r)
            return carry

        def head_chunk(hd, ci, r):
            lbv = lb_ref[:, hd * LANE:(hd + 1) * LANE]
            hq = hq_ref[hd, r, :]
            a = _hgrn_chunk(hq, hf_ref[hd, r, :], hi_ref[hd, r, :], lbv, sel_ref, lvl_ref)
            q, k, v = a["q"], a["k"], a["v"]
            g = do_ref[hd, r, :]
            st0 = ssave_ref[hd, ci]
            dst = dst_ref[hd]
            scores = _hgrn_scores(a, lvl_ref)
            da = _dot(g, v, _NT)
            kb = k * a["erem"]
            qb = q * a["eb"]
            dv = _dot(scores, g, _TN) + _dot(kb, dst, _NT)
            dq_inter = _dot(g, st0) * a["eb"]
            dk_state = _dot(v, dst) * a["erem"]
            da0 = lvl_ref[0] * da
            dq = dq_inter + _dot(da0, k)
            dk = dk_state + _dot(da0, q, _TN)
            de = [q * dq_inter]
            for l in range(1, N_LEVELS + 1):
                e = a["lev"][l]
                dal = lvl_ref[l] * da
                dql = _dot(dal, k * e) * e
                dkl = _dot(dal, q * e, _TN) * e
                dq = dq + dql
                dk = dk + dkl
                de.append(q * dql + k * dkl)
            de.append(k * dk_state)
            dst_ref[hd] = dst * a["ebc"] + _dot(g, qb, _TN)
            dbc = jnp.sum(dst * st0, axis=0, keepdims=True) * a["ebc"]
            de2 = lax.dot_general(sel_ref[...], _split2(jnp.concatenate(de, axis=0)), _TN,
                                  preferred_element_type=F32)
            dlogf = de2[:, :LANE] + de2[:, LANE:] + dbc
            sq, sg = a["sq"], a["sg"]
            df = dlogf / a["f"] - dk
            dq_buf[hd, r, :] = dq * (sq * (1.0 + hq * (1.0 - sq)))
            df_buf[hd, r, :] = df * a["omlb"] * sg * (1.0 - sg)
            di_buf[hd, r, :] = dv
            cols = slice(hd * LANE, (hd + 1) * LANE)
            dlb_ref[:, cols] = dlb_ref[:, cols] + jnp.sum(df * (1.0 - sg), axis=0, keepdims=True)

        lax.fori_loop(0, ncb, chunk, 0)
        rows = pl.ds(pl.multiple_of((ntb - 1 - t) * tb, tb), tb)
        copies = [pltpu.make_async_copy(buf, dpb_ref.at[pl.ds(base + 2 * p, 2), rows], sems.at[n])
                  for n, (buf, base) in enumerate(((dq_buf, CB_HQ), (df_buf, CB_HF), (di_buf, CB_HI)))]
        for cp in copies:
            cp.start()
        for cp in copies:
            cp.wait()

    pair = lambda base: pl.BlockSpec((2, tb, LANE), lambda p, t, base=base: (base // 2 + p, ntb - 1 - t, 0))
    any_spec = pl.BlockSpec(memory_space=pl.ANY)
    return pl.pallas_call(
        body, name="hgrn_bwd", grid=(N_HEADS // 2, ntb),
        in_specs=[pair(CB_HQ), pair(CB_HF), pair(CB_HI),
                  pl.BlockSpec((1, 2 * LANE), lambda p, t: (0, p)),
                  pair(0),
                  pl.BlockSpec((2, ncb, LANE, LANE), lambda p, t: (p, ntb - 1 - t, 0, 0)),
                  pl.BlockSpec(sel.shape, lambda p, t: (0, 0)),
                  pl.BlockSpec(lvl.shape, lambda p, t: (0, 0, 0)),
                  any_spec],
        out_specs=[any_spec, pl.BlockSpec((1, 2 * LANE), lambda p, t: (0, p))],
        out_shape=[jax.ShapeDtypeStruct(dpb.shape, F32), jax.ShapeDtypeStruct((1, D_MODEL), F32)],
        scratch_shapes=[pltpu.VMEM((2, tb, LANE), F32)] * 3 + [
            pltpu.VMEM((2, LANE, LANE), F32), pltpu.SemaphoreType.DMA((3,))],
        input_output_aliases={8: 0},
        compiler_params=_params(("arbitrary", "arbitrary")),
    )(pb, pb, pb, lb, do, ssave, sel, lvl, dpb)


def _rope_tables(t_len):
    half = RET_DK // 2
    inv_freq = 1.0 / (ROPE_BASE ** jnp.linspace(0.0, 1.0, half, dtype=F32))
    ang = jnp.arange(t_len, dtype=jnp.int32).astype(F32)[:, None] * inv_freq[None, :]
    cos, sin = jnp.cos(ang), jnp.sin(ang)
    cos_t = jnp.concatenate([cos, cos, cos, cos], axis=1)
    sin_t = jnp.concatenate([-sin, sin, -sin, sin], axis=1)
    return cos_t, sin_t


def _swap_halves(v):
    half = RET_DK // 2
    lane = lax.broadcasted_iota(jnp.int32, v.shape, 1)
    first = (lane & (RET_DK - 1)) < half
    return jnp.where(first, pltpu.roll(v, LANE - half, 1), pltpu.roll(v, half, 1))


def _ret_head_consts(hidx):
    c = CHUNK
    hf = jnp.full((1, LANE), hidx, jnp.int32).astype(F32)
    lg = jnp.log(1.0 - jnp.exp(-(5.0 + hf) * np.float32(np.log(2.0))))
    row = lax.broadcasted_iota(jnp.int32, (c, c), 0)
    col = lax.broadcasted_iota(jnp.int32, (c, c), 1)
    rel = (row - col).astype(F32)
    dm = jnp.where(rel >= 0, jnp.exp(lg[:, :1] * jnp.maximum(rel, 0.0)), 0.0)
    idx = lax.broadcasted_iota(jnp.int32, (c, LANE), 0).astype(F32)
    zeta = jnp.exp(lg * (c - 1.0 - idx))
    xi = jnp.exp(lg * (idx + 1.0))
    cdec = jnp.exp(lg * float(c))
    return dm, zeta, xi, cdec


def _lane_mask(which):
    lane = lax.broadcasted_iota(jnp.int32, (1, LANE), 1)
    return ((lane // RET_DK) == which).astype(F32)


def retention_forward(pb, cos_t, sin_t, t_len):
    nc = t_len // CHUNK

    tb = _time_block(t_len)
    ncb = tb // CHUNK

    def body(rq_ref, rk_ref, rv_ref, cos_ref, sin_ref, o_ref, rsave_ref, st_ref):
        p = pl.program_id(0)

        @pl.when(pl.program_id(1) == 0)
        def _():
            st_ref[...] = jnp.zeros_like(st_ref)

        consts = [_ret_head_consts(2 * p + hd) for hd in range(2)]

        def chunk(ci, carry):
            r = pl.ds(pl.multiple_of(ci * CHUNK, CHUNK), CHUNK)
            cs, sn = cos_ref[r, :], sin_ref[r, :]
            q = rq_ref[r, :]
            k = rk_ref[r, :]
            q = q * cs + _swap_halves(q) * sn
            k = (k * cs + _swap_halves(k) * sn) * RET_DK ** -0.5
            for hd in range(2):
                dm, zeta, xi, cdec = consts[hd]
                lm = _lane_mask(hd)
                qh, kh = q * lm, k * lm
                v = rv_ref[hd, r, :]
                st = st_ref[hd]
                rsave_ref[hd, ci] = st
                scores = _dot(qh, kh, _NT) * dm
                o_ref[hd, r, :] = _dot(scores, v) + _dot(qh * xi, st, _NT)
                st_ref[hd] = st * cdec + _dot(v, kh * zeta, _TN)
            return carry

        lax.fori_loop(0, ncb, chunk, 0)

    return pl.pallas_call(
        body, name="ret_fwd", grid=(N_HEADS // 2, t_len // tb),
        in_specs=[pl.BlockSpec((None, tb, LANE), lambda p, t: (CB_RQ + p, t, 0)),
                  pl.BlockSpec((None, tb, LANE), lambda p, t: (CB_RK + p, t, 0)),
                  pl.BlockSpec((2, tb, LANE), lambda p, t: (CB_RV // 2 + p, t, 0)),
                  pl.BlockSpec((tb, LANE), lambda p, t: (t, 0)),
                  pl.BlockSpec((tb, LANE), lambda p, t: (t, 0))],
        out_specs=[pl.BlockSpec((2, tb, LANE), lambda p, t: (p, t, 0)),
                   pl.BlockSpec((2, ncb, LANE, LANE), lambda p, t: (p, t, 0, 0))],
        out_shape=[jax.ShapeDtypeStruct((N_HEADS, t_len, LANE), F32),
                   jax.ShapeDtypeStruct((N_HEADS, nc, LANE, LANE), F32)],
        scratch_shapes=[pltpu.VMEM((2, LANE, LANE), F32)],
        compiler_params=_params(("arbitrary", "arbitrary")),
    )(pb, pb, pb, cos_t, sin_t)


def retention_backward(pb, cos_t, sin_t, do, rsave, dpb, t_len):
    tb = _time_block(t_len)
    ncb, ntb = tb // CHUNK, t_len // tb

    def body(rq_ref, rk_ref, rv_ref, cos_ref, sin_ref, do_ref, rsave_ref, dpb_in,
             dpb_ref, dq_buf, dk_buf, dv_buf, dst_ref, sems):
        del dpb_in
        p, t = pl.program_id(0), pl.program_id(1)

        @pl.when(t == 0)
        def _():
            dst_ref[...] = jnp.zeros_like(dst_ref)

        consts = [_ret_head_consts(2 * p + hd) for hd in range(2)]

        def chunk(i, carry):
            ci = ncb - 1 - i
            r = pl.ds(pl.multiple_of(ci * CHUNK, CHUNK), CHUNK)
            cs, sn = cos_ref[r, :], sin_ref[r, :]
            q = rq_ref[r, :]
            k = rk_ref[r, :]
            q = q * cs + _swap_halves(q) * sn
            k = (k * cs + _swap_halves(k) * sn) * RET_DK ** -0.5
            dq, dk = None, None
            for hd in range(2):
                dm, zeta, xi, cdec = consts[hd]
                lm = _lane_mask(hd)
                qh, kh = q * lm, k * lm
                v = rv_ref[hd, r, :]
                g = do_ref[hd, r, :]
                st0 = rsave_ref[hd, ci]
                dst = dst_ref[hd]
                scores = _dot(qh, kh, _NT) * dm
                dsc = _dot(g, v, _NT) * dm
                dqh = _dot(dsc, kh) + _dot(g, st0) * xi
                dkh = _dot(dsc, qh, _TN) + _dot(v, dst) * zeta
                dv_buf[hd, r, :] = _dot(scores, g, _TN) + _dot(kh * zeta, dst, _NT)
                dst_ref[hd] = dst * cdec + _dot(g, qh * xi, _TN)
                dq = dqh if dq is None else dq + dqh
                dk = dkh if dk is None else dk + dkh
            dk = dk * (RET_DK ** -0.5)
            dq_buf[r, :] = dq * cs - _swap_halves(dq) * sn
            dk_buf[r, :] = dk * cs - _swap_halves(dk) * sn
            return carry

        lax.fori_loop(0, ncb, chunk, 0)
        rows = pl.ds(pl.multiple_of((ntb - 1 - t) * tb, tb), tb)
        copies = [pltpu.make_async_copy(dq_buf, dpb_ref.at[CB_RQ + p, rows], sems.at[0]),
                  pltpu.make_async_copy(dk_buf, dpb_ref.at[CB_RK + p, rows], sems.at[1]),
                  pltpu.make_async_copy(dv_buf, dpb_ref.at[pl.ds(CB_RV + 2 * p, 2), rows], sems.at[2])]
        for cp in copies:
            cp.start()
        for cp in copies:
            cp.wait()

    any_spec = pl.BlockSpec(memory_space=pl.ANY)
    return pl.pallas_call(
        body, name="ret_bwd", grid=(N_HEADS // 2, ntb),
        in_specs=[pl.BlockSpec((None, tb, LANE), lambda p, t: (CB_RQ + p, ntb - 1 - t, 0)),
                  pl.BlockSpec((None, tb, LANE), lambda p, t: (CB_RK + p, ntb - 1 - t, 0)),
                  pl.BlockSpec((2, tb, LANE), lambda p, t: (CB_RV // 2 + p, ntb - 1 - t, 0)),
                  pl.BlockSpec((tb, LANE), lambda p, t: (ntb - 1 - t, 0)),
                  pl.BlockSpec((tb, LANE), lambda p, t: (ntb - 1 - t, 0)),
                  pl.BlockSpec((2, tb, LANE), lambda p, t: (p, ntb - 1 - t, 0)),
                  pl.BlockSpec((2, ncb, LANE, LANE), lambda p, t: (p, ntb - 1 - t, 0, 0)),
                  any_spec],
        out_specs=any_spec,
        out_shape=jax.ShapeDtypeStruct(dpb.shape, F32),
        scratch_shapes=[pltpu.VMEM((tb, LANE), F32), pltpu.VMEM((tb, LANE), F32),
                        pltpu.VMEM((2, tb, LANE), F32), pltpu.VMEM((2, LANE, LANE), F32),
                        pltpu.SemaphoreType.DMA((3,))],
        input_output_aliases={7: 0},
        compiler_params=_params(("arbitrary", "arbitrary")),
    )(pb, pb, pb, cos_t, sin_t, do, rsave, dpb)


def _row_tile(t_len, want):
    return min(want, t_len)


PAIR_CB = 2 * CB_PER_SHARD


def proj_forward(h, wt, t_len):
    tm = _row_tile(t_len, 512)

    def body(h_ref, w_ref, o_ref):
        acc = _dot(h_ref[...], w_ref[...], _NT)
        for jj in range(PAIR_CB):
            o_ref[jj] = acc[:, jj * LANE:(jj + 1) * LANE]

    return pl.pallas_call(
        body, name="proj_fwd", grid=(N_DEV // 2, t_len // tm),
        in_specs=[pl.BlockSpec((tm, D_MODEL), lambda j, i: (i, 0)),
                  pl.BlockSpec((PAIR_CB * LANE, D_MODEL), lambda j, i: (j, 0))],
        out_specs=pl.BlockSpec((PAIR_CB, tm, LANE), lambda j, i: (j, i, 0)),
        out_shape=jax.ShapeDtypeStruct((N_CB, t_len, LANE), F32),
        compiler_params=_params(("arbitrary", "arbitrary")),
    )(h, wt)


def proj_backward_input(dpb, wt, token, t_len):
    tm = _row_tile(t_len, 256)

    def body(a_ref, wt_hbm, token_ref, o_ref, w_ref, sem):
        del token_ref

        @pl.when(pl.program_id(0) == 0)
        def _():
            cp = pltpu.make_async_copy(wt_hbm, w_ref, sem)
            cp.start()
            cp.wait()

        a = jnp.concatenate([a_ref[jj].astype(_BF) for jj in range(N_CB)], axis=1)
        o_ref[...] = _dot(a, w_ref[...])

    return pl.pallas_call(
        body, name="proj_bwd_input", grid=(t_len // tm,),
        in_specs=[pl.BlockSpec((N_CB, tm, LANE), lambda i: (0, i, 0)),
                  pl.BlockSpec(memory_space=pl.ANY),
                  pl.BlockSpec(token.shape, lambda i: (0, 0))],
        out_specs=pl.BlockSpec((tm, D_MODEL), lambda i: (i, 0)),
        out_shape=jax.ShapeDtypeStruct((t_len, D_MODEL), F32),
        scratch_shapes=[pltpu.VMEM(wt.shape, wt.dtype), pltpu.SemaphoreType.DMA],
        compiler_params=_params(("arbitrary",)),
    )(dpb, wt, token)


def proj_backward_weight(h_t, dpb, t_len):
    tk = _row_tile(t_len, 1024)

    def body(h_ref, b_ref, o_ref):
        k = pl.program_id(1)
        b = jnp.concatenate([b_ref[jj].astype(_BF) for jj in range(PAIR_CB)], axis=1)
        part = _dot(h_ref[...], b)

        @pl.when(k == 0)
        def _():
            for s in range(2):
                o_ref[s] = part[:, s * SHARD_IN:(s + 1) * SHARD_IN]

        @pl.when(k > 0)
        def _():
            for s in range(2):
                o_ref[s] = o_ref[s] + part[:, s * SHARD_IN:(s + 1) * SHARD_IN]

    return pl.pallas_call(
        body, name="proj_bwd_weight", grid=(N_DEV // 2, t_len // tk),
        in_specs=[pl.BlockSpec((D_MODEL, tk), lambda j, k: (0, k)),
                  pl.BlockSpec((PAIR_CB, tk, LANE), lambda j, k: (j, k, 0))],
        out_specs=pl.BlockSpec((2, D_MODEL, SHARD_IN), lambda j, k: (j, 0, 0)),
        out_shape=jax.ShapeDtypeStruct((N_DEV, D_MODEL, SHARD_IN), F32),
        compiler_params=_params(("arbitrary", "arbitrary")),
    )(h_t, dpb)


def out_forward(m, wout, t_len):
    tm = _row_tile(t_len, 512)

    def body(a_ref, w_ref, o_ref):
        o_ref[...] = _dot(a_ref[...], w_ref[...])

    return pl.pallas_call(
        body, name="out_fwd", grid=(t_len // tm,),
        in_specs=[pl.BlockSpec((tm, D_MODEL), lambda i: (i, 0)),
                  pl.BlockSpec((D_MODEL, D_MODEL), lambda i: (0, 0))],
        out_specs=pl.BlockSpec((tm, D_MODEL), lambda i: (i, 0)),
        out_shape=jax.ShapeDtypeStruct((t_len, D_MODEL), F32),
        compiler_params=_params(("arbitrary",)),
    )(m, wout)


def out_backward(m, dz, wout, t_len):
    tm = _row_tile(t_len, 512)

    def body(m_ref, dz_ref, w_ref, dm_ref, dw_ref):
        i = pl.program_id(0)
        dm_ref[...] = _dot(dz_ref[...], w_ref[...], _NT)
        part = _dot(m_ref[...], dz_ref[...], _TN)

        @pl.when(i == 0)
        def _():
            dw_ref[...] = part

        @pl.when(i > 0)
        def _():
            dw_ref[...] = dw_ref[...] + part

    return pl.pallas_call(
        body, name="out_bwd", grid=(t_len // tm,),
        in_specs=[pl.BlockSpec((tm, D_MODEL), lambda i: (i, 0)),
                  pl.BlockSpec((tm, D_MODEL), lambda i: (i, 0)),
                  pl.BlockSpec((D_MODEL, D_MODEL), lambda i: (0, 0))],
        out_specs=[pl.BlockSpec((tm, D_MODEL), lambda i: (i, 0)),
                   pl.BlockSpec((D_MODEL, D_MODEL), lambda i: (0, 0))],
        out_shape=[jax.ShapeDtypeStruct((t_len, D_MODEL), F32),
                   jax.ShapeDtypeStruct((D_MODEL, D_MODEL), F32)],
        compiler_params=_params(("arbitrary",)),
    )(m, dz, wout)


def _vec_spec():
    return pl.BlockSpec((1, D_MODEL), lambda i: (0, 0))


def _acc_rows(ref, i, rows):
    @pl.when(i == 0)
    def _():
        ref[...] = jnp.zeros_like(ref)

    for n, row in enumerate(rows):
        ref[n:n + 1, :] = ref[n:n + 1, :] + row


def adaln_forward(x, norm_g, scale1p, shift, t_len):
    tm = _row_tile(t_len, 512)

    def body(x_ref, g_ref, sc_ref, sh_ref, h_ref, ht_ref):
        xv = x_ref[...]
        r = lax.rsqrt(jnp.mean(xv * xv, axis=-1, keepdims=True) + EPS)
        h = xv * r * g_ref[...] * sc_ref[...] + sh_ref[...]
        h_ref[...] = h.astype(h_ref.dtype)
        ht_ref[...] = h.T.astype(ht_ref.dtype)

    return pl.pallas_call(
        body, name="adaln_fwd", grid=(t_len // tm,),
        in_specs=[pl.BlockSpec((tm, D_MODEL), lambda i: (i, 0)), _vec_spec(), _vec_spec(), _vec_spec()],
        out_specs=[pl.BlockSpec((tm, D_MODEL), lambda i: (i, 0)), pl.BlockSpec((D_MODEL, tm), lambda i: (0, i))],
        out_shape=[jax.ShapeDtypeStruct((t_len, D_MODEL), _BF), jax.ShapeDtypeStruct((D_MODEL, t_len), _BF)],
        compiler_params=_params(("arbitrary",)),
    )(x, norm_g, scale1p, shift)


def adaln_backward(dh, x, dy, norm_g, scale1p, t_len):
    tm = _row_tile(t_len, 512)

    def body(dh_ref, x_ref, dy_ref, g_ref, sc_ref, gx_ref, vec_ref):
        i = pl.program_id(0)
        xv, dhv = x_ref[...], dh_ref[...]
        g, sc = g_ref[...], sc_ref[...]
        r = lax.rsqrt(jnp.mean(xv * xv, axis=-1, keepdims=True) + EPS)
        xn = xv * r
        dxn = dhv * (g * sc)
        gx_ref[...] = dy_ref[...] + r * dxn - xn * (r * r) * jnp.mean(xv * dxn, axis=-1, keepdims=True)
        t = dhv * xn
        _acc_rows(vec_ref, i, [jnp.sum(t * sc, axis=0, keepdims=True),
                               jnp.sum(t * g, axis=0, keepdims=True),
                               jnp.sum(dhv, axis=0, keepdims=True)])

    row = pl.BlockSpec((tm, D_MODEL), lambda i: (i, 0))
    return pl.pallas_call(
        body, name="adaln_bwd", grid=(t_len // tm,),
        in_specs=[row, row, row, _vec_spec(), _vec_spec()],
        out_specs=[row, pl.BlockSpec((8, D_MODEL), lambda i: (0, 0))],
        out_shape=[jax.ShapeDtypeStruct((t_len, D_MODEL), F32), jax.ShapeDtypeStruct((8, D_MODEL), F32)],
        compiler_params=_params(("arbitrary",)),
    )(dh, x, dy, norm_g, scale1p)


def _head_norm(o, g):
    r = lax.rsqrt(jnp.mean(o * o, axis=-1, keepdims=True) + EPS)
    return r, o * r * g


def _group_spec(tm, cb):
    return pl.BlockSpec((N_HEADS, tm, LANE), lambda i, cb=cb: (cb // N_HEADS, i, 0))


def merge_forward(oa, ob, pb, hg_g, ret_g, t_len):
    tm = _row_tile(t_len, 256)

    def body(oa_ref, ob_ref, hz_ref, rz_ref, ga_ref, gb_ref, hg_ref, rg_ref, m_ref):
        for hh in range(N_HEADS):
            ls = slice(hh * LANE, (hh + 1) * LANE)
            _, na = _head_norm(oa_ref[hh], hg_ref[:, ls])
            _, nb = _head_norm(ob_ref[hh], rg_ref[:, ls])
            hz, rz = hz_ref[hh], rz_ref[hh]
            ua = na * (hz * _sigmoid(hz))
            ub = nb * (rz * _sigmoid(rz))
            m_ref[:, ls] = (_sigmoid(ga_ref[hh]) * ua + _sigmoid(gb_ref[hh]) * ub).astype(m_ref.dtype)

    head = pl.BlockSpec((N_HEADS, tm, LANE), lambda i: (0, i, 0))
    return pl.pallas_call(
        body, name="merge_fwd", grid=(t_len // tm,),
        in_specs=[head, head, _group_spec(tm, CB_HZ), _group_spec(tm, CB_RZ), _group_spec(tm, CB_GA),
                  _group_spec(tm, CB_GB), _vec_spec(), _vec_spec()],
        out_specs=pl.BlockSpec((tm, D_MODEL), lambda i: (i, 0)),
        out_shape=jax.ShapeDtypeStruct((t_len, D_MODEL), _BF),
        compiler_params=_params(("arbitrary",)),
    )(oa, ob, pb, pb, pb, pb, hg_g, ret_g)


def merge_backward(dm, oa, ob, pb, hg_g, ret_g, t_len):
    tm = _row_tile(t_len, 256)

    def body(dm_ref, oa_ref, ob_ref, hz_ref, rz_ref, ga_ref, gb_ref, hg_ref, rg_ref,
             doa_ref, dob_ref, vec_ref, dpb_ref, bufs, sems):
        i = pl.program_id(0)
        dg_a, dg_b = [], []
        for hh in range(N_HEADS):
            ls = slice(hh * LANE, (hh + 1) * LANE)
            dmh = dm_ref[:, ls]
            for side, (o_ref, z_ref, gate_ref, g_ref, do_ref, acc) in enumerate((
                    (oa_ref, hz_ref, ga_ref, hg_ref, doa_ref, dg_a),
                    (ob_ref, rz_ref, gb_ref, rg_ref, dob_ref, dg_b))):
                o, z, gt, g = o_ref[hh], z_ref[hh], gate_ref[hh], g_ref[:, ls]
                r, n = _head_norm(o, g)
                sz = _sigmoid(z)
                silu = z * sz
                sgt = _sigmoid(gt)
                du = dmh * sgt
                bufs[2 + side, hh] = dmh * (n * silu) * (sgt * (1.0 - sgt))
                bufs[side, hh] = du * n * (sz * (1.0 + z * (1.0 - sz)))
                dn = du * silu
                acc.append(jnp.sum(dn * (o * r), axis=0, keepdims=True))
                gdn = dn * g
                do_ref[hh] = r * gdn - o * (r * r * r) * jnp.mean(o * gdn, axis=-1, keepdims=True)
        _acc_rows(vec_ref, i, [jnp.concatenate(dg_a, axis=1), jnp.concatenate(dg_b, axis=1)])
        rows = pl.ds(pl.multiple_of(i * tm, tm), tm)
        copies = [pltpu.make_async_copy(bufs.at[n], dpb_ref.at[pl.ds(cb, N_HEADS), rows], sems.at[n])
                  for n, cb in enumerate((CB_HZ, CB_RZ, CB_GA, CB_GB))]
        for cp in copies:
            cp.start()
        for cp in copies:
            cp.wait()

    head = pl.BlockSpec((N_HEADS, tm, LANE), lambda i: (0, i, 0))
    return pl.pallas_call(
        body, name="merge_bwd", grid=(t_len // tm,),
        in_specs=[pl.BlockSpec((tm, D_MODEL), lambda i: (i, 0)), head, head,
                  _group_spec(tm, CB_HZ), _group_spec(tm, CB_RZ), _group_spec(tm, CB_GA), _group_spec(tm, CB_GB),
                  _vec_spec(), _vec_spec()],
        out_specs=[head, head, pl.BlockSpec((8, D_MODEL), lambda i: (0, 0)), pl.BlockSpec(memory_space=pl.ANY)],
        out_shape=[jax.ShapeDtypeStruct((N_HEADS, t_len, LANE), F32),
                   jax.ShapeDtypeStruct((N_HEADS, t_len, LANE), F32),
                   jax.ShapeDtypeStruct((8, D_MODEL), F32),
                   jax.ShapeDtypeStruct((N_CB, t_len, LANE), F32)],
        scratch_shapes=[pltpu.VMEM((4, N_HEADS, tm, LANE), F32), pltpu.SemaphoreType.DMA((4,))],
        compiler_params=_params(("arbitrary",)),
    )(dm, oa, ob, pb, pb, pb, pb, hg_g, ret_g)


def tail(x, z, target, gate, final_g, t_len):
    tm = _row_tile(t_len, 512)

    def body(x_ref, z_ref, t_ref, gate_ref, fg_ref, dy_ref, dz_ref, vec_ref):
        i = pl.program_id(0)
        zv, gt, fg = z_ref[...], gate_ref[...], fg_ref[...]
        y = x_ref[...] + gt * zv
        r = lax.rsqrt(jnp.mean(y * y, axis=-1, keepdims=True) + EPS)
        yn = y * r
        err = yn * fg - t_ref[...]
        loss = 0.5 * jnp.sum(jnp.mean(err * err, axis=-1, keepdims=True), axis=0, keepdims=True)
        dout = err * (1.0 / D_MODEL)
        gd = dout * fg
        dy = r * gd - yn * (r * r) * jnp.mean(y * gd, axis=-1, keepdims=True)
        dy_ref[...] = dy
        dz_ref[...] = (dy * gt).astype(dz_ref.dtype)
        _acc_rows(vec_ref, i, [jnp.sum(dout * yn, axis=0, keepdims=True),
                               jnp.sum(dy * zv, axis=0, keepdims=True),
                               jnp.broadcast_to(loss, (1, D_MODEL))])

    row = pl.BlockSpec((tm, D_MODEL), lambda i: (i, 0))
    return pl.pallas_call(
        body, name="tail", grid=(t_len // tm,),
        in_specs=[row, row, row, _vec_spec(), _vec_spec()],
        out_specs=[row, row, pl.BlockSpec((8, D_MODEL), lambda i: (0, 0))],
        out_shape=[jax.ShapeDtypeStruct((t_len, D_MODEL), F32), jax.ShapeDtypeStruct((t_len, D_MODEL), _BF),
                   jax.ShapeDtypeStruct((8, D_MODEL), F32)],
        compiler_params=_params(("arbitrary",)),
    )(x, z, target, gate, final_g)


def device_step(x, target, mod, lb, wt, wout, norm_g, hg_g, ret_g, final_g, start_exchange=None):
    t_len = x.shape[0]
    shift, scale, gate = mod[:, :D_MODEL], mod[:, D_MODEL:2 * D_MODEL], mod[:, 2 * D_MODEL:]
    scale1p = 1.0 + scale
    cos_t, sin_t = _rope_tables(t_len)
    h, h_t = adaln_forward(x, norm_g, scale1p, shift, t_len)
    pb = proj_forward(h, wt, t_len)
    oa, ssave = hgrn_forward(pb, lb, t_len)
    ob, rsave = retention_forward(pb, cos_t, sin_t, t_len)
    m = merge_forward(oa, ob, pb, hg_g, ret_g, t_len)
    z = out_forward(m, wout, t_len)
    dy, dz, vec_tail = tail(x, z, target, gate, final_g, t_len)
    dm, dwout = out_backward(m, dz, wout, t_len)
    doa, dob, vec_merge, dpb = merge_backward(dm, oa, ob, pb, hg_g, ret_g, t_len)
    dpb, dlb = hgrn_backward(pb, lb, doa, ssave, dpb, t_len)
    dpb = retention_backward(pb, cos_t, sin_t, dob, rsave, dpb, t_len)
    dwin = proj_backward_weight(h_t, dpb, t_len)
    token, pending = start_exchange(dwin, dwout) if start_exchange else (jnp.zeros((8, LANE), F32), None)
    dh = proj_backward_input(dpb, wt, token, t_len)
    grad_x, vec_ada = adaln_backward(dh, x, dy, norm_g, scale1p, t_len)
    return grad_x, dwin, dwout, vec_tail, vec_merge, vec_ada, dlb, pending


PACK_ROWS = 16
ROW_NORM_G, ROW_LB, ROW_HG_G, ROW_RET_G, ROW_FINAL_G, ROW_SHIFT, ROW_SCALE, ROW_GATE, ROW_LOSS = range(9)


def _mesh_pos():
    return lax.axis_index("x"), lax.axis_index("y"), lax.axis_index("c")


def _lin(pos):
    return 4 * pos[0] + 2 * pos[1] + pos[2]


def _xor_peer(pos, k):
    return tuple(1 - p if (k >> s) & 1 else p for p, s in zip(pos, (2, 1, 0)))


def _other_chips(pos):
    x, y, _ = pos
    return [(1 - x, y), (x, 1 - y), (1 - x, 1 - y)]


def _remote(src, dst, send_sem, recv_sem, to):
    return pltpu.make_async_remote_copy(src_ref=src, dst_ref=dst, send_sem=send_sem, recv_sem=recv_sem,
                                        device_id=to, device_id_type=MESH)


def pre_exchange(c, w_ada, b_ada, logits):
    def body(c_ref, wada_ref, bada_ref, logit_ref, mod_ref, scall_ref, lb_ref,
             cg_ref, modall_ref, parts_ref, send1, recv1, send2, recv2):
        pos = _mesh_pos()
        cv = c_ref[...]
        slot = lambda p: pl.ds(pl.multiple_of(8 * _lin(p), 8), 8)
        cg_ref[slot(pos), :] = jnp.broadcast_to(cv * _sigmoid(cv), (8, D_MODEL))
        lb_ref[...] = _sigmoid(logit_ref[0:1, :] - logit_ref[1:2, :])
        peers = [_xor_peer(pos, k) for k in range(1, N_DEV)]
        gather = [_remote(cg_ref.at[slot(pos)], cg_ref.at[slot(pos)], send1.at[n], recv1.at[n], p)
                  for n, p in enumerate(peers)]
        for cp in gather:
            cp.start()
        for n, p in enumerate(peers):
            _remote(cg_ref.at[slot(p)], cg_ref.at[slot(p)], send1.at[n], recv1.at[n], p).wait_recv()
        modall_ref[...] = _dot(cg_ref[...], wada_ref[...])
        scatter = [_remote(modall_ref.at[slot(p)], parts_ref.at[slot(pos)], send2.at[n], recv2.at[n], p)
                   for n, p in enumerate(peers)]
        for cp in scatter:
            cp.start()
        parts_ref[slot(pos), :] = modall_ref[slot(pos), :]
        for n, p in enumerate(peers):
            _remote(modall_ref.at[slot(p)], parts_ref.at[slot(p)], send2.at[n], recv2.at[n], p).wait_recv()
        for cp in gather + scatter:
            cp.wait_send()
        for j in range(N_DEV):
            cols = slice(j * SHARD_ADA, (j + 1) * SHARD_ADA)
            mod_ref[:, cols] = parts_ref[8 * j:8 * j + 1, :] + bada_ref[:, cols]
            scall_ref[j:j + 1, :] = cg_ref[8 * j:8 * j + 1, :]

    vmem = pl.BlockSpec(memory_space=pltpu.VMEM)
    return pl.pallas_call(
        body, name="pre_exchange",
        in_specs=[vmem] * 4, out_specs=[vmem] * 3,
        out_shape=[jax.ShapeDtypeStruct((1, 3 * D_MODEL), F32), jax.ShapeDtypeStruct((N_DEV, D_MODEL), F32),
                   jax.ShapeDtypeStruct((1, D_MODEL), F32)],
        scratch_shapes=[pltpu.VMEM((N_DEV * 8, D_MODEL), F32), pltpu.VMEM((N_DEV * 8, SHARD_ADA), F32),
                        pltpu.VMEM((N_DEV * 8, SHARD_ADA), F32)] + [pltpu.SemaphoreType.DMA((N_DEV - 1,))] * 4,
        compiler_params=pltpu.CompilerParams(vmem_limit_bytes=VMEM_LIMIT),
    )(c, w_ada, b_ada, logits)


def weight_gather(win_sh, wout_sh):
    def body(win_ref, wout_ref, wg_ref, woutg_ref, send, recv, local):
        pos = _mesh_pos()
        x, y, c = pos
        sibling = (x, y, 1 - c)
        chips = _other_chips(pos)
        first, passed, mine = [], [], []
        for a, (src, out) in enumerate(((win_ref, wg_ref), (wout_ref, woutg_ref))):
            def copy(k, block, to, src_ref=None, a=a, out=out):
                dst = out.at[_lin(block)]
                return _remote(dst if src_ref is None else src_ref, dst, send.at[7 * a + k], recv.at[7 * a + k], to)
            mine.append(pltpu.make_async_copy(src, out.at[_lin(pos)], local.at[a]))
            first.append(copy(0, pos, sibling, src))
            first += [copy(1 + j, pos, (*chip, c), src) for j, chip in enumerate(chips)]
            passed.append([copy(4 + j, (*chip, c), sibling) for j, chip in enumerate(chips)])
        for cp in mine + first:
            cp.start()
        for a, out in enumerate((wg_ref, woutg_ref)):
            for j, chip in enumerate(chips):
                dst = out.at[_lin((*chip, c))]
                _remote(dst, dst, send.at[7 * a + 1 + j], recv.at[7 * a + 1 + j], pos).wait_recv()
                passed[a][j].start()
        for a, out in enumerate((wg_ref, woutg_ref)):
            dst = out.at[_lin(sibling)]
            _remote(dst, dst, send.at[7 * a], recv.at[7 * a], pos).wait_recv()
            for j, chip in enumerate(chips):
                dst = out.at[_lin((*chip, 1 - c))]
                _remote(dst, dst, send.at[7 * a + 4 + j], recv.at[7 * a + 4 + j], pos).wait_recv()
        for cp in first + passed[0] + passed[1]:
            cp.wait_send()
        for cp in mine:
            cp.wait()

    any_spec = pl.BlockSpec(memory_space=pl.ANY)
    return pl.pallas_call(
        body, name="weight_gather",
        in_specs=[any_spec, any_spec], out_specs=[any_spec, any_spec],
        out_shape=[jax.ShapeDtypeStruct((N_DEV,) + win_sh.shape, win_sh.dtype),
                   jax.ShapeDtypeStruct((N_DEV,) + wout_sh.shape, wout_sh.dtype)],
        scratch_shapes=[pltpu.SemaphoreType.DMA((14,)), pltpu.SemaphoreType.DMA((14,)),
                        pltpu.SemaphoreType.DMA((2,))],
    )(win_sh, wout_sh)


def grad_pair_exchange(g_in, g_out):
    def body(gin_ref, gout_ref, ra_ref, rb_ref, send, recv):
        pos = _mesh_pos()
        x, y, c = pos
        sibling = (x, y, 1 - c)
        copies = []
        for a, (src, dst) in enumerate(((gin_ref, ra_ref), (gout_ref, rb_ref))):
            for q in range(4):
                copies.append(_remote(src.at[2 * q + (1 - c)], dst.at[q], send.at[4 * a + q], recv.at[4 * a + q],
                                      sibling))
        for cp in copies:
            cp.start()
        for cp in copies:
            cp.wait_recv()
        for cp in copies:
            cp.wait_send()

    any_spec = pl.BlockSpec(memory_space=pl.ANY)
    return pl.pallas_call(
        body, name="grad_pair_exchange",
        in_specs=[any_spec, any_spec], out_specs=[any_spec, any_spec],
        out_shape=[jax.ShapeDtypeStruct((4,) + g_in.shape[1:], F32), jax.ShapeDtypeStruct((4,) + g_out.shape[1:], F32)],
        scratch_shapes=[pltpu.SemaphoreType.DMA((8,)), pltpu.SemaphoreType.DMA((8,))],
    )(g_in, g_out)


def pair_sum(g_in, ra, g_out, rb, c_idx):
    tr = 256

    def body(c_ref, gin_ref, ra_ref, gout_ref, rb_ref, sb_ref, sbo_ref):
        del c_ref
        sb_ref[...] = (gin_ref[...] + ra_ref[...]).astype(sb_ref.dtype)
        sbo_ref[...] = gout_ref[...] + rb_ref[...]

    n_i = D_MODEL // tr
    return pl.pallas_call(
        body, name="pair_sum",
        grid_spec=pltpu.PrefetchScalarGridSpec(
            num_scalar_prefetch=1, grid=(4, n_i),
            in_specs=[pl.BlockSpec((None, tr, SHARD_IN), lambda q, i, c: (2 * q + c[0], i, 0)),
                      pl.BlockSpec((None, tr, SHARD_IN), lambda q, i, c: (q, i, 0)),
                      pl.BlockSpec((None, SHARD_OUT // n_i, D_MODEL), lambda q, i, c: (2 * q + c[0], i, 0)),
                      pl.BlockSpec((None, SHARD_OUT // n_i, D_MODEL), lambda q, i, c: (q, i, 0))],
            out_specs=[pl.BlockSpec((None, tr, SHARD_IN), lambda q, i, c: (q, i, 0)),
                       pl.BlockSpec((None, SHARD_OUT // n_i, D_MODEL), lambda q, i, c: (q, i, 0))]),
        out_shape=[jax.ShapeDtypeStruct(ra.shape, _BF), jax.ShapeDtypeStruct(rb.shape, F32)],
        compiler_params=_params(("arbitrary", "arbitrary")),
    )(c_idx, g_in, ra, g_out, rb)


_HBM = pl.BlockSpec(memory_space=pltpu.HBM)
_SEM = pl.BlockSpec(memory_space=pltpu.SEMAPHORE)
_N_CHIP_COPIES = 6


def _chip_copies(sb_ref, sbo_ref, rc_ref, rco_ref, send, recv):
    pos = _mesh_pos()
    copies = []
    for a, (src, dst) in enumerate(((sb_ref, rc_ref), (sbo_ref, rco_ref))):
        for j, chip in enumerate(_other_chips(pos)):
            copies.append(_remote(src.at[2 * chip[0] + chip[1]], dst.at[j], send.at[3 * a + j], recv.at[3 * a + j],
                                  (*chip, pos[2])))
    return copies


def grad_chip_start(sb, sbo):
    def body(sb_ref, sbo_ref, rc_ref, rco_ref, send, recv, sb_thru, sbo_thru, rc_thru, rco_thru, token):
        del sb_thru, sbo_thru, rc_thru, rco_thru
        for cp in _chip_copies(sb_ref, sbo_ref, rc_ref, rco_ref, send, recv):
            cp.start()
        token[...] = jnp.zeros_like(token)

    hbm = lambda a: pltpu.with_memory_space_constraint(a, pltpu.HBM)
    rc = lax.empty((3,) + sb.shape[1:], sb.dtype)
    rco = lax.empty((3,) + sbo.shape[1:], sbo.dtype)
    return pl.pallas_call(
        body, name="grad_chip_start",
        in_specs=[_HBM] * 4,
        out_specs=[_SEM, _SEM, _HBM, _HBM, _HBM, _HBM, pl.BlockSpec(memory_space=pltpu.VMEM)],
        out_shape=[pltpu.SemaphoreType.DMA((_N_CHIP_COPIES,)), pltpu.SemaphoreType.DMA((_N_CHIP_COPIES,)),
                   pltpu.HBM(sb.shape, sb.dtype), pltpu.HBM(sbo.shape, sbo.dtype),
                   pltpu.HBM(rc.shape, rc.dtype), pltpu.HBM(rco.shape, rco.dtype),
                   jax.ShapeDtypeStruct((8, LANE), F32)],
        input_output_aliases={0: 2, 1: 3, 2: 4, 3: 5},
        compiler_params=pltpu.CompilerParams(has_side_effects=pltpu.SideEffectType.DATAFLOW_SIDE_EFFECTING),
    )(hbm(sb), hbm(sbo), hbm(rc), hbm(rco))


def grad_chip_wait(send, recv, sb, sbo, rc, rco, after):
    def body(sb_ref, sbo_ref, rc_ref, rco_ref, send, recv, after_ref, sb_o, sbo_o, rc_o, rco_o):
        del after_ref, sb_o, sbo_o, rc_o, rco_o
        for cp in _chip_copies(sb_ref, sbo_ref, rc_ref, rco_ref, send, recv):
            cp.wait_send()
            cp.wait_recv()

    return pl.pallas_call(
        body, name="grad_chip_wait",
        in_specs=[_HBM] * 4 + [_SEM, _SEM, pl.BlockSpec(memory_space=pl.ANY)],
        out_specs=[_HBM] * 4,
        out_shape=[pltpu.HBM(sb.shape, sb.dtype), pltpu.HBM(sbo.shape, sbo.dtype),
                   pltpu.HBM(rc.shape, rc.dtype), pltpu.HBM(rco.shape, rco.dtype)],
        input_output_aliases={0: 0, 1: 1, 2: 2, 3: 3},
        compiler_params=pltpu.CompilerParams(has_side_effects=pltpu.SideEffectType.DATAFLOW_SIDE_EFFECTING),
    )(sb, sbo, rc, rco, send, recv, after)


def pack_gather(pack):
    def body(pack_ref, packs_ref, psend, precv):
        pos = _mesh_pos()
        me = _lin(pos)
        packs_ref[me] = pack_ref[...]
        peers = [_xor_peer(pos, k) for k in range(1, N_DEV)]
        gather = [_remote(packs_ref.at[me], packs_ref.at[me], psend.at[n], precv.at[n], p) for n, p in enumerate(peers)]
        for cp in gather:
            cp.start()
        for n, p in enumerate(peers):
            _remote(packs_ref.at[_lin(p)], packs_ref.at[_lin(p)], psend.at[n], precv.at[n], p).wait_recv()
        for cp in gather:
            cp.wait_send()

    vmem = pl.BlockSpec(memory_space=pltpu.VMEM)
    return pl.pallas_call(
        body, name="pack_gather", in_specs=[vmem], out_specs=vmem,
        out_shape=jax.ShapeDtypeStruct((N_DEV,) + pack.shape, F32),
        scratch_shapes=[pltpu.SemaphoreType.DMA((N_DEV - 1,)), pltpu.SemaphoreType.DMA((N_DEV - 1,))],
    )(pack)


def pack_rows(vec_tail, vec_merge, vec_ada, dlb):
    def body(tail_ref, merge_ref, ada_ref, dlb_ref, o_ref):
        rows = [ada_ref[0:1, :], dlb_ref[...], merge_ref[0:1, :], merge_ref[1:2, :], tail_ref[0:1, :],
                ada_ref[2:3, :], ada_ref[1:2, :], tail_ref[1:2, :], tail_ref[2:3, :]]
        o_ref[...] = jnp.zeros_like(o_ref)
        for n, row in enumerate(rows):
            o_ref[n:n + 1, :] = row

    vmem = pl.BlockSpec(memory_space=pltpu.VMEM)
    return pl.pallas_call(body, name="pack_rows", in_specs=[vmem] * 4, out_specs=vmem,
                          out_shape=jax.ShapeDtypeStruct((PACK_ROWS, D_MODEL), F32))(vec_tail, vec_merge, vec_ada, dlb)


def _adamw(w, g, m, v):
    m = ADAM_B1 * m + (1.0 - ADAM_B1) * g
    v = ADAM_B2 * v + (1.0 - ADAM_B2) * (g * g)
    m_hat = m / (1.0 - ADAM_B1 ** ADAM_STEP)
    v_hat = v / (1.0 - ADAM_B2 ** ADAM_STEP)
    delta = -ADAM_LR * (m_hat / (jnp.sqrt(v_hat) + ADAM_EPS) + ADAM_WD * w)
    return delta, m, v


def adam_shard(chip_idx, own, parts, w, m, v, name):
    rows, cols = w.shape
    tr = min(rows, 128)

    def body(chip_ref, p0, p1, p2, p3, w_ref, m_ref, v_ref, g_ref, d_ref, nm_ref, nv_ref):
        del chip_ref
        g = ((p0[...].astype(F32) + p1[...].astype(F32)) + p2[...].astype(F32)) + p3[...].astype(F32)
        g_ref[...] = g
        d_ref[...], nm_ref[...], nv_ref[...] = _adamw(w_ref[...], g, m_ref[...], v_ref[...])

    part = lambda q: pl.BlockSpec((None, tr, cols), lambda i, chip, q=q: (q, i, 0))
    tile = pl.BlockSpec((tr, cols), lambda i, chip: (i, 0))
    return pl.pallas_call(
        body, name=name,
        grid_spec=pltpu.PrefetchScalarGridSpec(
            num_scalar_prefetch=1, grid=(rows // tr,),
            in_specs=[pl.BlockSpec((None, tr, cols), lambda i, chip: (chip[0], i, 0)), part(0), part(1), part(2),
                      tile, tile, tile],
            out_specs=[tile] * 4),
        out_shape=[jax.ShapeDtypeStruct(w.shape, F32)] * 4,
        compiler_params=_params(("arbitrary",)),
    )(chip_idx, own, parts, parts, parts, w, m, v)


def adam_ada(sc_t, dmod_all, me_idx, w, m, v):
    def body(me_ref, sc_ref, dm_ref, w_ref, m_ref, v_ref, g_ref, d_ref, nm_ref, nv_ref):
        del me_ref
        g = _dot_f32(sc_ref[...], dm_ref[...])
        g_ref[...] = g
        d_ref[...], nm_ref[...], nv_ref[...] = _adamw(w_ref[...], g, m_ref[...], v_ref[...])

    full = pl.BlockSpec(w.shape, lambda i, me: (0, 0))
    return pl.pallas_call(
        body, name="adam_ada",
        grid_spec=pltpu.PrefetchScalarGridSpec(
            num_scalar_prefetch=1, grid=(1,),
            in_specs=[pl.BlockSpec(sc_t.shape, lambda i, me: (0, 0)),
                      pl.BlockSpec((LANE, SHARD_ADA), lambda i, me: (0, me[0])), full, full, full],
            out_specs=[full] * 4),
        out_shape=[jax.ShapeDtypeStruct(w.shape, F32)] * 4,
        compiler_params=_params(("arbitrary",)),
    )(me_idx, sc_t, dmod_all, w, m, v)


def adam_vectors(packs, lb, params, ms, vs):
    n = len(params)

    def body(*refs):
        packs_ref, lb_ref = refs[0], refs[1]
        w_refs, m_refs, v_refs = refs[2:2 + n], refs[2 + n:2 + 2 * n], refs[2 + 2 * n:2 + 3 * n]
        loss_ref = refs[2 + 3 * n]
        outs = refs[3 + 3 * n:3 + 7 * n]
        tot_ref = refs[3 + 7 * n]
        tot = packs_ref[0]
        for d in range(1, N_DEV):
            tot = tot + packs_ref[d]
        tot_ref[...] = tot
        row = lambda r: tot_ref[r:r + 1, :]
        lbv = lb_ref[...]
        dl0 = row(ROW_LB) * lbv * (1.0 - lbv)
        grads = [[row(ROW_NORM_G)],
                 [jnp.concatenate([row(ROW_SHIFT), row(ROW_SCALE), row(ROW_GATE)], axis=1)],
                 [dl0, -dl0],
                 [row(ROW_HG_G)], [row(ROW_RET_G)], [row(ROW_FINAL_G)]]
        loss_ref[...] = tot_ref[ROW_LOSS:ROW_LOSS + 1, 0:LANE]
        for j, g_rows in enumerate(grads):
            for r, g in enumerate(g_rows):
                rs = slice(r, r + 1)
                d, nm, nv = _adamw(w_refs[j][rs, :], g, m_refs[j][rs, :], v_refs[j][rs, :])
                outs[4 * j][rs, :] = g
                outs[4 * j + 1][rs, :] = d
                outs[4 * j + 2][rs, :] = nm
                outs[4 * j + 3][rs, :] = nv

    vmem = pl.BlockSpec(memory_space=pltpu.VMEM)
    out_shape = [jax.ShapeDtypeStruct((1, LANE), F32)]
    for w in params:
        out_shape += [jax.ShapeDtypeStruct(w.shape, F32)] * 4
    return pl.pallas_call(
        body, name="adam_vectors", in_specs=[vmem] * (2 + 3 * n), out_specs=[vmem] * len(out_shape),
        out_shape=out_shape, scratch_shapes=[pltpu.VMEM((PACK_ROWS, D_MODEL), F32)],
    )(packs, lb, *params, *ms, *vs)


def kernel(x, c, norm_g, w_ada, b_ada, w_in, hg_lb_logits, hg_norm_g, ret_norm_g, w_out, final_g, loss_target, m_norm_g, m_w_ada, m_b_ada, m_w_in, m_hg_lb_logits, m_hg_norm_g, m_ret_norm_g, m_w_out, m_final_g, v_norm_g, v_w_ada, v_b_ada, v_w_in, v_hg_lb_logits, v_hg_norm_g, v_ret_norm_g, v_w_out, v_final_g):
    pos = _mesh_pos()
    me_idx = jnp.reshape(_lin(pos), (1,)).astype(jnp.int32)
    c_idx = jnp.reshape(pos[2], (1,)).astype(jnp.int32)
    vec = lambda a: a.reshape(1, D_MODEL)

    mod, scall, lb = pre_exchange(c, w_ada[0], b_ada, hg_lb_logits)
    wtg, woutg = weight_gather(w_in[0].T.astype(_BF), w_out[0].astype(_BF))
    chip_idx = jnp.reshape(2 * pos[0] + pos[1], (1,)).astype(jnp.int32)

    def start_exchange(dwin, dwout):
        dwout = dwout.reshape(N_DEV, SHARD_OUT, D_MODEL)
        ra, rb = grad_pair_exchange(dwin, dwout)
        sb, sbo = pair_sum(dwin, ra, dwout, rb, c_idx)
        send, recv, sb, sbo, rc, rco, token = grad_chip_start(sb, sbo)
        return token, (send, recv, sb, sbo, rc, rco)

    grad_x, _, _, vec_tail, vec_merge, vec_ada, dlb, pending = device_step(
        x[0], loss_target[0], mod, lb, wtg.reshape(D_IN, D_MODEL), woutg.reshape(D_MODEL, D_MODEL), norm_g,
        hg_norm_g, ret_norm_g, vec(final_g), start_exchange)
    sb, sbo, rc, rco = grad_chip_wait(*pending, grad_x)
    packs = pack_gather(pack_rows(vec_tail, vec_merge, vec_ada, dlb))

    g_in, d_in, nm_in, nv_in = adam_shard(chip_idx, sb, rc, w_in[0], m_w_in[0], v_w_in[0], "adam_w_in")
    g_out, d_out, nm_out, nv_out = adam_shard(chip_idx, sbo, rco, w_out[0], m_w_out[0], v_w_out[0], "adam_w_out")
    dmod_all = packs[:, ROW_SHIFT:ROW_GATE + 1, :].reshape(N_DEV, 3 * D_MODEL)
    dmod_all = jnp.pad(dmod_all, ((0, LANE - N_DEV), (0, 0)))
    sc_t = jnp.pad(scall.T, ((0, 0), (0, LANE - N_DEV)))
    g_ada, d_ada, nm_ada, nv_ada = adam_ada(sc_t, dmod_all, me_idx, w_ada[0], m_w_ada[0], v_w_ada[0])
    small = adam_vectors(
        packs, lb,
        (norm_g, b_ada, hg_lb_logits, hg_norm_g, ret_norm_g, vec(final_g)),
        (m_norm_g, m_b_ada, m_hg_lb_logits, m_hg_norm_g, m_ret_norm_g, vec(m_final_g)),
        (v_norm_g, v_b_ada, v_hg_lb_logits, v_hg_norm_g, v_ret_norm_g, vec(v_final_g)))
    loss = small[0][0, 0]
    (g_ng, d_ng, nm_ng, nv_ng), (g_b, d_b, nm_b, nv_b), (g_lb, d_lb, nm_lb, nv_lb), (g_hg, d_hg, nm_hg, nv_hg), \
        (g_rg, d_rg, nm_rg, nv_rg), (g_fg, d_fg, nm_fg, nv_fg) = [small[1 + 4 * j:5 + 4 * j] for j in range(6)]
    flat = lambda a: a.reshape(D_MODEL)

    def group(ng, ada, b, win, lbl, hg, rg, wo, fg):
        return (ng, ada[None], b, win[None], lbl, hg, rg, wo[None], flat(fg))

    return (loss, grad_x[None],
            *group(g_ng, g_ada, g_b, g_in, g_lb, g_hg, g_rg, g_out, g_fg),
            *group(d_ng, d_ada, d_b, d_in, d_lb, d_hg, d_rg, d_out, d_fg),
            *group(nm_ng, nm_ada, nm_b, nm_in, nm_lb, nm_hg, nm_rg, nm_out, nm_fg),
            *group(nv_ng, nv_ada, nv_b, nv_in, nv_lb, nv_hg, nv_rg, nv_out, nv_fg))
```

```python
import functools

import numpy as np
import jax
import jax.numpy as jnp
from jax import lax
from jax.experimental import pallas as pl
from jax.experimental.pallas import tpu as pltpu

F32 = jnp.float32
_BF = jnp.bfloat16

D_MODEL = 1024
N_HEADS = 8
LANE = 128
RET_DK = 64
D_IN = 9216
N_DEV = 8
SHARD_IN = D_IN // N_DEV
SHARD_ADA = 3 * D_MODEL // N_DEV
SHARD_OUT = D_MODEL // N_DEV
N_CB = D_IN // LANE
CB_PER_SHARD = SHARD_IN // LANE
CHUNK = 128
N_LEVELS = 7
EPS = 1e-6
ROPE_BASE = 10000.0
CB_HQ, CB_HF, CB_HI, CB_HZ, CB_RQ, CB_RK, CB_RV, CB_RZ, CB_GA, CB_GB = 0, 8, 16, 24, 32, 36, 40, 48, 56, 64
VMEM_LIMIT = 56 * 1024 * 1024

ADAM_LR, ADAM_B1, ADAM_B2, ADAM_EPS, ADAM_WD, ADAM_STEP = 0.001, 0.9, 0.999, 1e-08, 0.01, 10

_NN = (((1,), (0,)), ((), ()))
_NT = (((1,), (1,)), ((), ()))
_TN = (((0,), (0,)), ((), ()))
MESH = pl.DeviceIdType.MESH


def _dot(a, b, dims=_NN):
    return lax.dot_general(a.astype(_BF), b.astype(_BF), dims, preferred_element_type=F32)


def _split2(a):
    hi = a.astype(_BF)
    lo = (a - hi.astype(F32)).astype(_BF)
    return jnp.concatenate([hi, lo], axis=1)


def _dot_sel(sel, a):
    n = a.shape[1]
    r = lax.dot_general(sel.astype(_BF), _split2(a), _NN, preferred_element_type=F32)
    return r[:, :n] + r[:, n:]


def _dot_f32(a, b):
    def pieces(v):
        p1 = v.astype(_BF)
        r1 = v - p1.astype(F32)
        p2 = r1.astype(_BF)
        p3 = (r1 - p2.astype(F32)).astype(_BF)
        return p1, p2, p3
    a1, a2, a3 = pieces(a)
    b1, b2, b3 = pieces(b)
    d = lambda u, v: lax.dot_general(u, v, _NN, preferred_element_type=F32)
    return ((d(a1, b3) + d(a2, b2) + d(a3, b1)) + (d(a1, b2) + d(a2, b1))) + d(a1, b1)


def _sigmoid(v):
    return 1.0 / (1.0 + jnp.exp(-v))


def _params(sem=None):
    return pltpu.CompilerParams(dimension_semantics=sem, vmem_limit_bytes=VMEM_LIMIT)


def _hgrn_consts():
    c, nl = CHUNK, N_LEVELS
    t = np.arange(c)[:, None]
    j = np.arange(c)[None, :]
    sel = [j <= t]
    masks = [j == t]
    for l in range(1, nl + 1):
        m = ((t >> l) << l) + (1 << (l - 1)) - 1
        sec = t > m
        sel.append(np.where(sec, (j > m) & (j <= t), (j > t) & (j <= m)))
        same = (t >> l) == (j >> l)
        masks.append(same & sec & (j <= m))
    sel.append(j > t)
    sel = np.concatenate(sel, 0).astype(np.float32)
    masks = np.stack(masks).astype(np.float32)
    return jnp.asarray(sel, _BF), jnp.asarray(masks, F32)


def _hgrn_chunk(hq, hf, hi, lbv, sel_ref, lvl_ref):
    c, nl = CHUNK, N_LEVELS
    sq = _sigmoid(hq)
    q = hq * sq
    sg = _sigmoid(hf)
    omlb = 1.0 - lbv
    f = lbv + omlb * sg
    k = 1.0 - f
    logf = jnp.log(f)
    e_all = _dot_sel(sel_ref[...], logf)
    eb = jnp.exp(e_all[0:c])
    erem = jnp.exp(e_all[(nl + 1) * c:(nl + 2) * c])
    ebc = jnp.exp(jnp.sum(logf, axis=0, keepdims=True))
    lev = [None] + [jnp.exp(e_all[l * c:(l + 1) * c]) for l in range(1, nl + 1)]
    return dict(sq=sq, q=q, sg=sg, omlb=omlb, f=f, k=k, v=hi, eb=eb, erem=erem, ebc=ebc, lev=lev)


def _hgrn_scores(a, lvl_ref):
    q, k = a["q"], a["k"]
    acc = lvl_ref[0] * _dot(q, k, _NT)
    for l in range(1, N_LEVELS + 1):
        e = a["lev"][l]
        acc = acc + lvl_ref[l] * _dot(q * e, k * e, _NT)
    return acc


def _time_block(t_len):
    return min(t_len, 1024)


def hgrn_forward(pb, lb, t_len):
    nc = t_len // CHUNK
    tb = _time_block(t_len)
    ncb = tb // CHUNK
    sel, lvl = _hgrn_consts()

    def body(hq_ref, hf_ref, hi_ref, lb_ref, sel_ref, lvl_ref, o_ref, ssave_ref, st_ref):
        @pl.when(pl.program_id(1) == 0)
        def _():
            st_ref[...] = jnp.zeros_like(st_ref)

        def chunk(ci, carry):
            r = pl.ds(pl.multiple_of(ci * CHUNK, CHUNK), CHUNK)
            for hd in range(2):
                lbv = lb_ref[:, hd * LANE:(hd + 1) * LANE]
                a = _hgrn_chunk(hq_ref[hd, r, :], hf_ref[hd, r, :], hi_ref[hd, r, :], lbv, sel_ref, lvl_ref)
                st = st_ref[hd]
                ssave_ref[hd, ci] = st
                o_ref[hd, r, :] = _dot(a["q"] * a["eb"], st, _NT) + _dot(_hgrn_scores(a, lvl_ref), a["v"])
                st_ref[hd] = st * a["ebc"] + _dot(a["v"], a["k"] * a["erem"], _TN)
            return carry

        lax.fori_loop(0, ncb, chunk, 0)

    pair = lambda base: pl.BlockSpec((2, tb, LANE), lambda p, t, base=base: (base // 2 + p, t, 0))
    return pl.pallas_call(
        body, name="hgrn_fwd", grid=(N_HEADS // 2, t_len // tb),
        in_specs=[pair(CB_HQ), pair(CB_HF), pair(CB_HI),
                  pl.BlockSpec((1, 2 * LANE), lambda p, t: (0, p)),
                  pl.BlockSpec(sel.shape, lambda p, t: (0, 0)),
                  pl.BlockSpec(lvl.shape, lambda p, t: (0, 0, 0))],
        out_specs=[pl.BlockSpec((2, tb, LANE), lambda p, t: (p, t, 0)),
                   pl.BlockSpec((2, ncb, LANE, LANE), lambda p, t: (p, t, 0, 0))],
        out_shape=[jax.ShapeDtypeStruct((N_HEADS, t_len, LANE), F32),
                   jax.ShapeDtypeStruct((N_HEADS, nc, LANE, LANE), F32)],
        scratch_shapes=[pltpu.VMEM((2, LANE, LANE), F32)],
        compiler_params=_params(("arbitrary", "arbitrary")),
    )(pb, pb, pb, lb, sel, lvl)


def hgrn_backward(pb, lb, do, ssave, dpb, t_len):
    tb = _time_block(t_len)
    ncb, ntb = tb // CHUNK, t_len // tb
    sel, lvl = _hgrn_consts()

    def body(hq_ref, hf_ref, hi_ref, lb_ref, do_ref, ssave_ref, sel_ref, lvl_ref, dpb_in,
             dpb_ref, dlb_ref, dq_buf, df_buf, di_buf, dst_ref, sems):
        del dpb_in
        p, t = pl.program_id(0), pl.program_id(1)

        @pl.when(t == 0)
        def _():
            dst_ref[...] = jnp.zeros_like(dst_ref)
            dlb_ref[...] = jnp.zeros_like(dlb_ref)

        def chunk(i, carry):
            ci = ncb - 1 - i
            r = pl.ds(pl.multiple_of(ci * CHUNK, CHUNK), CHUNK)
            for hd in range(2):
                head_chunk(hd, ci, r)
            return carry

        def head_chunk(hd, ci, r):
            lbv = lb_ref[:, hd * LANE:(hd + 1) * LANE]
            hq = hq_ref[hd, r, :]
            a = _hgrn_chunk(hq, hf_ref[hd, r, :], hi_ref[hd, r, :], lbv, sel_ref, lvl_ref)
            q, k, v = a["q"], a["k"], a["v"]
            g = do_ref[hd, r, :]
            st0 = ssave_ref[hd, ci]
            dst = dst_ref[hd]
            scores = _hgrn_scores(a, lvl_ref)
            da = _dot(g, v, _NT)
            kb = k * a["erem"]
            qb = q * a["eb"]
            dv = _dot(scores, g, _TN) + _dot(kb, dst, _NT)
            dq_inter = _dot(g, st0) * a["eb"]
            dk_state = _dot(v, dst) * a["erem"]
            da0 = lvl_ref[0] * da
            dq = dq_inter + _dot(da0, k)
            dk = dk_state + _dot(da0, q, _TN)
            de = [q * dq_inter]
            for l in range(1, N_LEVELS + 1):
                e = a["lev"][l]
                dal = lvl_ref[l] * da
                dql = _dot(dal, k * e) * e
                dkl = _dot(dal, q * e, _TN) * e
                dq = dq + dql
                dk = dk + dkl
                de.append(q * dql + k * dkl)
            de.append(k * dk_state)
            dst_ref[hd] = dst * a["ebc"] + _dot(g, qb, _TN)
            dbc = jnp.sum(dst * st0, axis=0, keepdims=True) * a["ebc"]
            de2 = lax.dot_general(sel_ref[...], _split2(jnp.concatenate(de, axis=0)), _TN,
                                  preferred_element_type=F32)
            dlogf = de2[:, :LANE] + de2[:, LANE:] + dbc
            sq, sg = a["sq"], a["sg"]
            df = dlogf / a["f"] - dk
            dq_buf[hd, r, :] = dq * (sq * (1.0 + hq * (1.0 - sq)))
            df_buf[hd, r, :] = df * a["omlb"] * sg * (1.0 - sg)
            di_buf[hd, r, :] = dv
            cols = slice(hd * LANE, (hd + 1) * LANE)
            dlb_ref[:, cols] = dlb_ref[:, cols] + jnp.sum(df * (1.0 - sg), axis=0, keepdims=True)

        lax.fori_loop(0, ncb, chunk, 0)
        rows = pl.ds(pl.multiple_of((ntb - 1 - t) * tb, tb), tb)
        copies = [pltpu.make_async_copy(buf, dpb_ref.at[pl.ds(base + 2 * p, 2), rows], sems.at[n])
                  for n, (buf, base) in enumerate(((dq_buf, CB_HQ), (df_buf, CB_HF), (di_buf, CB_HI)))]
        for cp in copies:
            cp.start()
        for cp in copies:
            cp.wait()

    pair = lambda base: pl.BlockSpec((2, tb, LANE), lambda p, t, base=base: (base // 2 + p, ntb - 1 - t, 0))
    any_spec = pl.BlockSpec(memory_space=pl.ANY)
    return pl.pallas_call(
        body, name="hgrn_bwd", grid=(N_HEADS // 2, ntb),
        in_specs=[pair(CB_HQ), pair(CB_HF), pair(CB_HI),
                  pl.BlockSpec((1, 2 * LANE), lambda p, t: (0, p)),
                  pair(0),
                  pl.BlockSpec((2, ncb, LANE, LANE), lambda p, t: (p, ntb - 1 - t, 0, 0)),
                  pl.BlockSpec(sel.shape, lambda p, t: (0, 0)),
                  pl.BlockSpec(lvl.shape, lambda p, t: (0, 0, 0)),
                  any_spec],
        out_specs=[any_spec, pl.BlockSpec((1, 2 * LANE), lambda p, t: (0, p))],
        out_shape=[jax.ShapeDtypeStruct(dpb.shape, F32), jax.ShapeDtypeStruct((1, D_MODEL), F32)],
        scratch_shapes=[pltpu.VMEM((2, tb, LANE), F32)] * 3 + [
            pltpu.VMEM((2, LANE, LANE), F32), pltpu.SemaphoreType.DMA((3,))],
        input_output_aliases={8: 0},
        compiler_params=_params(("arbitrary", "arbitrary")),
    )(pb, pb, pb, lb, do, ssave, sel, lvl, dpb)


def _rope_tables(t_len):
    half = RET_DK // 2
    inv_freq = 1.0 / (ROPE_BASE ** jnp.linspace(0.0, 1.0, half, dtype=F32))
    ang = jnp.arange(t_len, dtype=jnp.int32).astype(F32)[:, None] * inv_freq[None, :]
    cos, sin = jnp.cos(ang), jnp.sin(ang)
    cos_t = jnp.concatenate([cos, cos, cos, cos], axis=1)
    sin_t = jnp.concatenate([-sin, sin, -sin, sin], axis=1)
    return cos_t, sin_t


def _swap_halves(v):
    half = RET_DK // 2
    lane = lax.broadcasted_iota(jnp.int32, v.shape, 1)
    first = (lane & (RET_DK - 1)) < half
    return jnp.where(first, pltpu.roll(v, LANE - half, 1), pltpu.roll(v, half, 1))


def _ret_head_consts(hidx):
    c = CHUNK
    hf = jnp.full((1, LANE), hidx, jnp.int32).astype(F32)
    lg = jnp.log(1.0 - jnp.exp(-(5.0 + hf) * np.float32(np.log(2.0))))
    row = lax.broadcasted_iota(jnp.int32, (c, c), 0)
    col = lax.broadcasted_iota(jnp.int32, (c, c), 1)
    rel = (row - col).astype(F32)
    dm = jnp.where(rel >= 0, jnp.exp(lg[:, :1] * jnp.maximum(rel, 0.0)), 0.0)
    idx = lax.broadcasted_iota(jnp.int32, (c, LANE), 0).astype(F32)
    zeta = jnp.exp(lg * (c - 1.0 - idx))
    xi = jnp.exp(lg * (idx + 1.0))
    cdec = jnp.exp(lg * float(c))
    return dm, zeta, xi, cdec


def _lane_mask(which):
    lane = lax.broadcasted_iota(jnp.int32, (1, LANE), 1)
    return ((lane // RET_DK) == which).astype(F32)


def retention_forward(pb, cos_t, sin_t, t_len):
    nc = t_len // CHUNK

    tb = _time_block(t_len)
    ncb = tb // CHUNK

    def body(rq_ref, rk_ref, rv_ref, cos_ref, sin_ref, o_ref, rsave_ref, st_ref):
        p = pl.program_id(0)

        @pl.when(pl.program_id(1) == 0)
        def _():
            st_ref[...] = jnp.zeros_like(st_ref)

        consts = [_ret_head_consts(2 * p + hd) for hd in range(2)]

        def chunk(ci, carry):
            r = pl.ds(pl.multiple_of(ci * CHUNK, CHUNK), CHUNK)
            cs, sn = cos_ref[r, :], sin_ref[r, :]
            q = rq_ref[r, :]
            k = rk_ref[r, :]
            q = q * cs + _swap_halves(q) * sn
            k = (k * cs + _swap_halves(k) * sn) * RET_DK ** -0.5
            for hd in range(2):
                dm, zeta, xi, cdec = consts[hd]
                lm = _lane_mask(hd)
                qh, kh = q * lm, k * lm
                v = rv_ref[hd, r, :]
                st = st_ref[hd]
                rsave_ref[hd, ci] = st
                scores = _dot(qh, kh, _NT) * dm
                o_ref[hd, r, :] = _dot(scores, v) + _dot(qh * xi, st, _NT)
                st_ref[hd] = st * cdec + _dot(v, kh * zeta, _TN)
            return carry

        lax.fori_loop(0, ncb, chunk, 0)

    return pl.pallas_call(
        body, name="ret_fwd", grid=(N_HEADS // 2, t_len // tb),
        in_specs=[pl.BlockSpec((None, tb, LANE), lambda p, t: (CB_RQ + p, t, 0)),
                  pl.BlockSpec((None, tb, LANE), lambda p, t: (CB_RK + p, t, 0)),
                  pl.BlockSpec((2, tb, LANE), lambda p, t: (CB_RV // 2 + p, t, 0)),
                  pl.BlockSpec((tb, LANE), lambda p, t: (t, 0)),
                  pl.BlockSpec((tb, LANE), lambda p, t: (t, 0))],
        out_specs=[pl.BlockSpec((2, tb, LANE), lambda p, t: (p, t, 0)),
                   pl.BlockSpec((2, ncb, LANE, LANE), lambda p, t: (p, t, 0, 0))],
        out_shape=[jax.ShapeDtypeStruct((N_HEADS, t_len, LANE), F32),
                   jax.ShapeDtypeStruct((N_HEADS, nc, LANE, LANE), F32)],
        scratch_shapes=[pltpu.VMEM((2, LANE, LANE), F32)],
        compiler_params=_params(("arbitrary", "arbitrary")),
    )(pb, pb, pb, cos_t, sin_t)


def retention_backward(pb, cos_t, sin_t, do, rsave, dpb, t_len):
    tb = _time_block(t_len)
    ncb, ntb = tb // CHUNK, t_len // tb

    def body(rq_ref, rk_ref, rv_ref, cos_ref, sin_ref, do_ref, rsave_ref, dpb_in,
             dpb_ref, dq_buf, dk_buf, dv_buf, dst_ref, sems):
        del dpb_in
        p, t = pl.program_id(0), pl.program_id(1)

        @pl.when(t == 0)
        def _():
            dst_ref[...] = jnp.zeros_like(dst_ref)

        consts = [_ret_head_consts(2 * p + hd) for hd in range(2)]

        def chunk(i, carry):
            ci = ncb - 1 - i
            r = pl.ds(pl.multiple_of(ci * CHUNK, CHUNK), CHUNK)
            cs, sn = cos_ref[r, :], sin_ref[r, :]
            q = rq_ref[r, :]
            k = rk_ref[r, :]
            q = q * cs + _swap_halves(q) * sn
            k = (k * cs + _swap_halves(k) * sn) * RET_DK ** -0.5
            dq, dk = None, None
            for hd in range(2):
                dm, zeta, xi, cdec = consts[hd]
                lm = _lane_mask(hd)
                qh, kh = q * lm, k * lm
                v = rv_ref[hd, r, :]
                g = do_ref[hd, r, :]
                st0 = rsave_ref[hd, ci]
                dst = dst_ref[hd]
                scores = _dot(qh, kh, _NT) * dm
                dsc = _dot(g, v, _NT) * dm
                dqh = _dot(dsc, kh) + _dot(g, st0) * xi
                dkh = _dot(dsc, qh, _TN) + _dot(v, dst) * zeta
                dv_buf[hd, r, :] = _dot(scores, g, _TN) + _dot(kh * zeta, dst, _NT)
                dst_ref[hd] = dst * cdec + _dot(g, qh * xi, _TN)
                dq = dqh if dq is None else dq + dqh
                dk = dkh if dk is None else dk + dkh
            dk = dk * (RET_DK ** -0.5)
            dq_buf[r, :] = dq * cs - _swap_halves(dq) * sn
            dk_buf[r, :] = dk * cs - _swap_halves(dk) * sn
            return carry

        lax.fori_loop(0, ncb, chunk, 0)
        rows = pl.ds(pl.multiple_of((ntb - 1 - t) * tb, tb), tb)
        copies = [pltpu.make_async_copy(dq_buf, dpb_ref.at[CB_RQ + p, rows], sems.at[0]),
                  pltpu.make_async_copy(dk_buf, dpb_ref.at[CB_RK + p, rows], sems.at[1]),
                  pltpu.make_async_copy(dv_buf, dpb_ref.at[pl.ds(CB_RV + 2 * p, 2), rows], sems.at[2])]
        for cp in copies:
            cp.start()
        for cp in copies:
            cp.wait()

    any_spec = pl.BlockSpec(memory_space=pl.ANY)
    return pl.pallas_call(
        body, name="ret_bwd", grid=(N_HEADS // 2, ntb),
        in_specs=[pl.BlockSpec((None, tb, LANE), lambda p, t: (CB_RQ + p, ntb - 1 - t, 0)),
                  pl.BlockSpec((None, tb, LANE), lambda p, t: (CB_RK + p, ntb - 1 - t, 0)),
                  pl.BlockSpec((2, tb, LANE), lambda p, t: (CB_RV // 2 + p, ntb - 1 - t, 0)),
                  pl.BlockSpec((tb, LANE), lambda p, t: (ntb - 1 - t, 0)),
                  pl.BlockSpec((tb, LANE), lambda p, t: (ntb - 1 - t, 0)),
                  pl.BlockSpec((2, tb, LANE), lambda p, t: (p, ntb - 1 - t, 0)),
                  pl.BlockSpec((2, ncb, LANE, LANE), lambda p, t: (p, ntb - 1 - t, 0, 0)),
                  any_spec],
        out_specs=any_spec,
        out_shape=jax.ShapeDtypeStruct(dpb.shape, F32),
        scratch_shapes=[pltpu.VMEM((tb, LANE), F32), pltpu.VMEM((tb, LANE), F32),
                        pltpu.VMEM((2, tb, LANE), F32), pltpu.VMEM((2, LANE, LANE), F32),
                        pltpu.SemaphoreType.DMA((3,))],
        input_output_aliases={7: 0},
        compiler_params=_params(("arbitrary", "arbitrary")),
    )(pb, pb, pb, cos_t, sin_t, do, rsave, dpb)


def _row_tile(t_len, want):
    return min(want, t_len)


PAIR_CB = 2 * CB_PER_SHARD


def proj_forward(h, wt, t_len):
    tm = _row_tile(t_len, 512)

    def body(h_ref, w_ref, o_ref):
        acc = _dot(h_ref[...], w_ref[...], _NT)
        for jj in range(PAIR_CB):
            o_ref[jj] = acc[:, jj * LANE:(jj + 1) * LANE]

    return pl.pallas_call(
        body, name="proj_fwd", grid=(N_DEV // 2, t_len // tm),
        in_specs=[pl.BlockSpec((tm, D_MODEL), lambda j, i: (i, 0)),
                  pl.BlockSpec((PAIR_CB * LANE, D_MODEL), lambda j, i: (j, 0))],
        out_specs=pl.BlockSpec((PAIR_CB, tm, LANE), lambda j, i: (j, i, 0)),
        out_shape=jax.ShapeDtypeStruct((N_CB, t_len, LANE), F32),
        compiler_params=_params(("arbitrary", "arbitrary")),
    )(h, wt)


def proj_backward_input(dpb, wt, token, t_len):
    tm = _row_tile(t_len, 256)

    def body(a_ref, wt_hbm, token_ref, o_ref, w_ref, sem):
        del token_ref

        @pl.when(pl.program_id(0) == 0)
        def _():
            cp = pltpu.make_async_copy(wt_hbm, w_ref, sem)
            cp.start()
            cp.wait()

        a = jnp.concatenate([a_ref[jj].astype(_BF) for jj in range(N_CB)], axis=1)
        o_ref[...] = _dot(a, w_ref[...])

    return pl.pallas_call(
        body, name="proj_bwd_input", grid=(t_len // tm,),
        in_specs=[pl.BlockSpec((N_CB, tm, LANE), lambda i: (0, i, 0)),
                  pl.BlockSpec(memory_space=pl.ANY),
                  pl.BlockSpec(token.shape, lambda i: (0, 0))],
        out_specs=pl.BlockSpec((tm, D_MODEL), lambda i: (i, 0)),
        out_shape=jax.ShapeDtypeStruct((t_len, D_MODEL), F32),
        scratch_shapes=[pltpu.VMEM(wt.shape, wt.dtype), pltpu.SemaphoreType.DMA],
        compiler_params=_params(("arbitrary",)),
    )(dpb, wt, token)


def proj_backward_weight(h_t, dpb, t_len):
    tk = _row_tile(t_len, 1024)

    def body(h_ref, b_ref, o_ref):
        k = pl.program_id(1)
        b = jnp.concatenate([b_ref[jj].astype(_BF) for jj in range(PAIR_CB)], axis=1)
        part = _dot(h_ref[...], b)

        @pl.when(k == 0)
        def _():
            for s in range(2):
                o_ref[s] = part[:, s * SHARD_IN:(s + 1) * SHARD_IN]

        @pl.when(k > 0)
        def _():
            for s in range(2):
                o_ref[s] = o_ref[s] + part[:, s * SHARD_IN:(s + 1) * SHARD_IN]

    return pl.pallas_call(
        body, name="proj_bwd_weight", grid=(N_DEV // 2, t_len // tk),
        in_specs=[pl.BlockSpec((D_MODEL, tk), lambda j, k: (0, k)),
                  pl.BlockSpec((PAIR_CB, tk, LANE), lambda j, k: (j, k, 0))],
        out_specs=pl.BlockSpec((2, D_MODEL, SHARD_IN), lambda j, k: (j, 0, 0)),
        out_shape=jax.ShapeDtypeStruct((N_DEV, D_MODEL, SHARD_IN), F32),
        compiler_params=_params(("arbitrary", "arbitrary")),
    )(h_t, dpb)


def _vec_spec():
    return pl.BlockSpec((1, D_MODEL), lambda i: (0, 0))


def _acc_rows(ref, i, rows):
    @pl.when(i == 0)
    def _():
        ref[...] = jnp.zeros_like(ref)

    for n, row in enumerate(rows):
        ref[n:n + 1, :] = ref[n:n + 1, :] + row


def adaln_forward(x, norm_g, scale1p, shift, t_len):
    tm = _row_tile(t_len, 512)

    def body(x_ref, g_ref, sc_ref, sh_ref, h_ref, ht_ref):
        xv = x_ref[...]
        r = lax.rsqrt(jnp.mean(xv * xv, axis=-1, keepdims=True) + EPS)
        h = xv * r * g_ref[...] * sc_ref[...] + sh_ref[...]
        h_ref[...] = h.astype(h_ref.dtype)
        ht_ref[...] = h.T.astype(ht_ref.dtype)

    return pl.pallas_call(
        body, name="adaln_fwd", grid=(t_len // tm,),
        in_specs=[pl.BlockSpec((tm, D_MODEL), lambda i: (i, 0)), _vec_spec(), _vec_spec(), _vec_spec()],
        out_specs=[pl.BlockSpec((tm, D_MODEL), lambda i: (i, 0)), pl.BlockSpec((D_MODEL, tm), lambda i: (0, i))],
        out_shape=[jax.ShapeDtypeStruct((t_len, D_MODEL), _BF), jax.ShapeDtypeStruct((D_MODEL, t_len), _BF)],
        compiler_params=_params(("arbitrary",)),
    )(x, norm_g, scale1p, shift)


def adaln_backward(dh, x, dy, norm_g, scale1p, t_len):
    tm = _row_tile(t_len, 512)

    def body(dh_ref, x_ref, dy_ref, g_ref, sc_ref, gx_ref, vec_ref):
        i = pl.program_id(0)
        xv, dhv = x_ref[...], dh_ref[...]
        g, sc = g_ref[...], sc_ref[...]
        r = lax.rsqrt(jnp.mean(xv * xv, axis=-1, keepdims=True) + EPS)
        xn = xv * r
        dxn = dhv * (g * sc)
        gx_ref[...] = dy_ref[...] + r * dxn - xn * (r * r) * jnp.mean(xv * dxn, axis=-1, keepdims=True)
        t = dhv * xn
        _acc_rows(vec_ref, i, [jnp.sum(t * sc, axis=0, keepdims=True),
                               jnp.sum(t * g, axis=0, keepdims=True),
                               jnp.sum(dhv, axis=0, keepdims=True)])

    row = pl.BlockSpec((tm, D_MODEL), lambda i: (i, 0))
    return pl.pallas_call(
        body, name="adaln_bwd", grid=(t_len // tm,),
        in_specs=[row, row, row, _vec_spec(), _vec_spec()],
        out_specs=[row, pl.BlockSpec((8, D_MODEL), lambda i: (0, 0))],
        out_shape=[jax.ShapeDtypeStruct((t_len, D_MODEL), F32), jax.ShapeDtypeStruct((8, D_MODEL), F32)],
        compiler_params=_params(("arbitrary",)),
    )(dh, x, dy, norm_g, scale1p)


def _head_norm(o, g):
    r = lax.rsqrt(jnp.mean(o * o, axis=-1, keepdims=True) + EPS)
    return r, o * r * g


def _group_spec(tm, cb):
    return pl.BlockSpec((N_HEADS, tm, LANE), lambda i, cb=cb: (cb // N_HEADS, i, 0))


MID_FINAL_G, MID_GATE, MID_LOSS, MID_HG_G, MID_RET_G = range(5)


def middle(x, target, oa, ob, pb, wout, gate, final_g, hg_g, ret_g, t_len):
    tm = _row_tile(t_len, 256)

    def body(x_ref, t_ref, oa_ref, ob_ref, hz_ref, rz_ref, ga_ref, gb_ref, w_ref, gate_ref, fg_ref, hg_ref, rg_ref,
             dy_ref, doa_ref, dob_ref, dw_ref, vec_ref, dpb_ref, m_scr, dm_scr, bufs, sems):
        i = pl.program_id(0)
        sides = ((oa_ref, hz_ref, ga_ref, hg_ref, doa_ref), (ob_ref, rz_ref, gb_ref, rg_ref, dob_ref))
        for hh in range(N_HEADS):
            ls = slice(hh * LANE, (hh + 1) * LANE)
            acc = None
            for o_ref, z_ref, gt_ref, g_ref, _ in sides:
                _, n = _head_norm(o_ref[hh], g_ref[:, ls])
                zz = z_ref[hh]
                u = _sigmoid(gt_ref[hh]) * (n * (zz * _sigmoid(zz)))
                acc = u if acc is None else acc + u
            m_scr[:, ls] = acc.astype(m_scr.dtype)
        zv = _dot(m_scr[...], w_ref[...])
        gt, fg = gate_ref[...], fg_ref[...]
        y = x_ref[...] + gt * zv
        r = lax.rsqrt(jnp.mean(y * y, axis=-1, keepdims=True) + EPS)
        yn = y * r
        err = yn * fg - t_ref[...]
        loss = 0.5 * jnp.sum(jnp.mean(err * err, axis=-1, keepdims=True), axis=0, keepdims=True)
        dout = err * (1.0 / D_MODEL)
        gd = dout * fg
        dy = r * gd - yn * (r * r) * jnp.mean(y * gd, axis=-1, keepdims=True)
        dy_ref[...] = dy
        dz = (dy * gt).astype(_BF)
        dm_scr[...] = _dot(dz, w_ref[...], _NT)
        part = _dot(m_scr[...], dz, _TN)

        @pl.when(i == 0)
        def _():
            dw_ref[...] = part

        @pl.when(i > 0)
        def _():
            dw_ref[...] = dw_ref[...] + part

        dg = [[], []]
        for hh in range(N_HEADS):
            ls = slice(hh * LANE, (hh + 1) * LANE)
            dmh = dm_scr[:, ls]
            for side, (o_ref, z_ref, gt_ref, g_ref, do_ref) in enumerate(sides):
                o, zz, gl, g = o_ref[hh], z_ref[hh], gt_ref[hh], g_ref[:, ls]
                rr, n = _head_norm(o, g)
                sz = _sigmoid(zz)
                silu = zz * sz
                sgt = _sigmoid(gl)
                du = dmh * sgt
                bufs[2 + side, hh] = dmh * (n * silu) * (sgt * (1.0 - sgt))
                bufs[side, hh] = du * n * (sz * (1.0 + zz * (1.0 - sz)))
                dn = du * silu
                dg[side].append(jnp.sum(dn * (o * rr), axis=0, keepdims=True))
                gdn = dn * g
                do_ref[hh] = rr * gdn - o * (rr * rr * rr) * jnp.mean(o * gdn, axis=-1, keepdims=True)
        _acc_rows(vec_ref, i, [jnp.sum(dout * yn, axis=0, keepdims=True),
                               jnp.sum(dy * zv, axis=0, keepdims=True),
                               jnp.broadcast_to(loss, (1, D_MODEL)),
                               jnp.concatenate(dg[0], axis=1), jnp.concatenate(dg[1], axis=1)])
        rows = pl.ds(pl.multiple_of(i * tm, tm), tm)
        copies = [pltpu.make_async_copy(bufs.at[n], dpb_ref.at[pl.ds(cb, N_HEADS), rows], sems.at[n])
                  for n, cb in enumerate((CB_HZ, CB_RZ, CB_GA, CB_GB))]
        for cp in copies:
            cp.start()
        for cp in copies:
            cp.wait()

    row = pl.BlockSpec((tm, D_MODEL), lambda i: (i, 0))
    head = pl.BlockSpec((N_HEADS, tm, LANE), lambda i: (0, i, 0))
    full = pl.BlockSpec((D_MODEL, D_MODEL), lambda i: (0, 0))
    return pl.pallas_call(
        body, name="middle", grid=(t_len // tm,),
        in_specs=[row, row, head, head, _group_spec(tm, CB_HZ), _group_spec(tm, CB_RZ), _group_spec(tm, CB_GA),
                  _group_spec(tm, CB_GB), full, _vec_spec(), _vec_spec(), _vec_spec(), _vec_spec()],
        out_specs=[row, head, head, full, pl.BlockSpec((8, D_MODEL), lambda i: (0, 0)),
                   pl.BlockSpec(memory_space=pl.ANY)],
        out_shape=[jax.ShapeDtypeStruct((t_len, D_MODEL), F32),
                   jax.ShapeDtypeStruct((N_HEADS, t_len, LANE), F32),
                   jax.ShapeDtypeStruct((N_HEADS, t_len, LANE), F32),
                   jax.ShapeDtypeStruct((D_MODEL, D_MODEL), F32),
                   jax.ShapeDtypeStruct((8, D_MODEL), F32),
                   jax.ShapeDtypeStruct((N_CB, t_len, LANE), F32)],
        scratch_shapes=[pltpu.VMEM((tm, D_MODEL), _BF), pltpu.VMEM((tm, D_MODEL), F32),
                        pltpu.VMEM((4, N_HEADS, tm, LANE), F32), pltpu.SemaphoreType.DMA((4,))],
        compiler_params=_params(("arbitrary",)),
    )(x, target, oa, ob, pb, pb, pb, pb, wout, gate, final_g, hg_g, ret_g)


def device_step(x, target, mod, lb, wt, wout, norm_g, hg_g, ret_g, final_g, start_exchange=None):
    t_len = x.shape[0]
    shift, scale, gate = mod[:, :D_MODEL], mod[:, D_MODEL:2 * D_MODEL], mod[:, 2 * D_MODEL:]
    scale1p = 1.0 + scale
    cos_t, sin_t = _rope_tables(t_len)
    h, h_t = adaln_forward(x, norm_g, scale1p, shift, t_len)
    pb = proj_forward(h, wt, t_len)
    oa, ssave = hgrn_forward(pb, lb, t_len)
    ob, rsave = retention_forward(pb, cos_t, sin_t, t_len)
    dy, doa, dob, dwout, vec_mid, dpb = middle(x, target, oa, ob, pb, wout, gate, final_g, hg_g, ret_g, t_len)
    dpb, dlb = hgrn_backward(pb, lb, doa, ssave, dpb, t_len)
    dpb = retention_backward(pb, cos_t, sin_t, dob, rsave, dpb, t_len)
    dwin = proj_backward_weight(h_t, dpb, t_len)
    token, pending = start_exchange(dwin, dwout) if start_exchange else (jnp.zeros((8, LANE), F32), None)
    dh = proj_backward_input(dpb, wt, token, t_len)
    grad_x, vec_ada = adaln_backward(dh, x, dy, norm_g, scale1p, t_len)
    return grad_x, dwin, dwout, vec_mid, vec_ada, dlb, pending


PACK_ROWS = 16
ROW_NORM_G, ROW_LB, ROW_HG_G, ROW_RET_G, ROW_FINAL_G, ROW_SHIFT, ROW_SCALE, ROW_GATE, ROW_LOSS = range(9)


def _mesh_pos():
    return lax.axis_index("x"), lax.axis_index("y"), lax.axis_index("c")


def _lin(pos):
    return 4 * pos[0] + 2 * pos[1] + pos[2]


def _xor_peer(pos, k):
    return tuple(1 - p if (k >> s) & 1 else p for p, s in zip(pos, (2, 1, 0)))


def _other_chips(pos):
    x, y, _ = pos
    return [(1 - x, y), (x, 1 - y), (1 - x, 1 - y)]


def _remote(src, dst, send_sem, recv_sem, to):
    return pltpu.make_async_remote_copy(src_ref=src, dst_ref=dst, send_sem=send_sem, recv_sem=recv_sem,
                                        device_id=to, device_id_type=MESH)


def pre_exchange(c, w_ada, b_ada, logits):
    def body(c_ref, wada_ref, bada_ref, logit_ref, mod_ref, scall_ref, lb_ref,
             cg_ref, modall_ref, parts_ref, send1, recv1, send2, recv2):
        pos = _mesh_pos()
        cv = c_ref[...]
        slot = lambda p: pl.ds(pl.multiple_of(8 * _lin(p), 8), 8)
        cg_ref[slot(pos), :] = jnp.broadcast_to(cv * _sigmoid(cv), (8, D_MODEL))
        lb_ref[...] = _sigmoid(logit_ref[0:1, :] - logit_ref[1:2, :])
        peers = [_xor_peer(pos, k) for k in range(1, N_DEV)]
        gather = [_remote(cg_ref.at[slot(pos)], cg_ref.at[slot(pos)], send1.at[n], recv1.at[n], p)
                  for n, p in enumerate(peers)]
        for cp in gather:
            cp.start()
        for n, p in enumerate(peers):
            _remote(cg_ref.at[slot(p)], cg_ref.at[slot(p)], send1.at[n], recv1.at[n], p).wait_recv()
        modall_ref[...] = _dot(cg_ref[...], wada_ref[...])
        scatter = [_remote(modall_ref.at[slot(p)], parts_ref.at[slot(pos)], send2.at[n], recv2.at[n], p)
                   for n, p in enumerate(peers)]
        for cp in scatter:
            cp.start()
        parts_ref[slot(pos), :] = modall_ref[slot(pos), :]
        for n, p in enumerate(peers):
            _remote(modall_ref.at[slot(p)], parts_ref.at[slot(p)], send2.at[n], recv2.at[n], p).wait_recv()
        for cp in gather + scatter:
            cp.wait_send()
        for j in range(N_DEV):
            cols = slice(j * SHARD_ADA, (j + 1) * SHARD_ADA)
            mod_ref[:, cols] = parts_ref[8 * j:8 * j + 1, :] + bada_ref[:, cols]
            scall_ref[j:j + 1, :] = cg_ref[8 * j:8 * j + 1, :]

    vmem = pl.BlockSpec(memory_space=pltpu.VMEM)
    return pl.pallas_call(
        body, name="pre_exchange",
        in_specs=[vmem] * 4, out_specs=[vmem] * 3,
        out_shape=[jax.ShapeDtypeStruct((1, 3 * D_MODEL), F32), jax.ShapeDtypeStruct((N_DEV, D_MODEL), F32),
                   jax.ShapeDtypeStruct((1, D_MODEL), F32)],
        scratch_shapes=[pltpu.VMEM((N_DEV * 8, D_MODEL), F32), pltpu.VMEM((N_DEV * 8, SHARD_ADA), F32),
                        pltpu.VMEM((N_DEV * 8, SHARD_ADA), F32)] + [pltpu.SemaphoreType.DMA((N_DEV - 1,))] * 4,
        compiler_params=pltpu.CompilerParams(vmem_limit_bytes=VMEM_LIMIT),
    )(c, w_ada, b_ada, logits)


def weight_gather(win_sh, wout_sh):
    def body(win_ref, wout_ref, wg_ref, woutg_ref, send, recv, local):
        pos = _mesh_pos()
        x, y, c = pos
        sibling = (x, y, 1 - c)
        chips = _other_chips(pos)
        first, passed, mine = [], [], []
        for a, (src, out) in enumerate(((win_ref, wg_ref), (wout_ref, woutg_ref))):
            def copy(k, block, to, src_ref=None, a=a, out=out):
                dst = out.at[_lin(block)]
                return _remote(dst if src_ref is None else src_ref, dst, send.at[7 * a + k], recv.at[7 * a + k], to)
            mine.append(pltpu.make_async_copy(src, out.at[_lin(pos)], local.at[a]))
            first.append(copy(0, pos, sibling, src))
            first += [copy(1 + j, pos, (*chip, c), src) for j, chip in enumerate(chips)]
            passed.append([copy(4 + j, (*chip, c), sibling) for j, chip in enumerate(chips)])
        for cp in mine + first:
            cp.start()
        for a, out in enumerate((wg_ref, woutg_ref)):
            for j, chip in enumerate(chips):
                dst = out.at[_lin((*chip, c))]
                _remote(dst, dst, send.at[7 * a + 1 + j], recv.at[7 * a + 1 + j], pos).wait_recv()
                passed[a][j].start()
        for a, out in enumerate((wg_ref, woutg_ref)):
            dst = out.at[_lin(sibling)]
            _remote(dst, dst, send.at[7 * a], recv.at[7 * a], pos).wait_recv()
            for j, chip in enumerate(chips):
                dst = out.at[_lin((*chip, 1 - c))]
                _remote(dst, dst, send.at[7 * a + 4 + j], recv.at[7 * a + 4 + j], pos).wait_recv()
        for cp in first + passed[0] + passed[1]:
            cp.wait_send()
        for cp in mine:
            cp.wait()

    any_spec = pl.BlockSpec(memory_space=pl.ANY)
    return pl.pallas_call(
        body, name="weight_gather",
        in_specs=[any_spec, any_spec], out_specs=[any_spec, any_spec],
        out_shape=[jax.ShapeDtypeStruct((N_DEV,) + win_sh.shape, win_sh.dtype),
                   jax.ShapeDtypeStruct((N_DEV,) + wout_sh.shape, wout_sh.dtype)],
        scratch_shapes=[pltpu.SemaphoreType.DMA((14,)), pltpu.SemaphoreType.DMA((14,)),
                        pltpu.SemaphoreType.DMA((2,))],
    )(win_sh, wout_sh)


def grad_pair_exchange(g_in, g_out):
    def body(gin_ref, gout_ref, ra_ref, rb_ref, send, recv):
        pos = _mesh_pos()
        x, y, c = pos
        sibling = (x, y, 1 - c)
        copies = []
        for a, (src, dst) in enumerate(((gin_ref, ra_ref), (gout_ref, rb_ref))):
            for q in range(4):
                copies.append(_remote(src.at[2 * q + (1 - c)], dst.at[q], send.at[4 * a + q], recv.at[4 * a + q],
                                      sibling))
        for cp in copies:
            cp.start()
        for cp in copies:
            cp.wait_recv()
        for cp in copies:
            cp.wait_send()

    any_spec = pl.BlockSpec(memory_space=pl.ANY)
    return pl.pallas_call(
        body, name="grad_pair_exchange",
        in_specs=[any_spec, any_spec], out_specs=[any_spec, any_spec],
        out_shape=[jax.ShapeDtypeStruct((4,) + g_in.shape[1:], F32), jax.ShapeDtypeStruct((4,) + g_out.shape[1:], F32)],
        scratch_shapes=[pltpu.SemaphoreType.DMA((8,)), pltpu.SemaphoreType.DMA((8,))],
    )(g_in, g_out)


def pair_sum(g_in, ra, g_out, rb, c_idx):
    tr = 256

    def body(c_ref, gin_ref, ra_ref, gout_ref, rb_ref, sb_ref, sbo_ref):
        del c_ref
        sb_ref[...] = (gin_ref[...] + ra_ref[...]).astype(sb_ref.dtype)
        sbo_ref[...] = gout_ref[...] + rb_ref[...]

    n_i = D_MODEL // tr
    return pl.pallas_call(
        body, name="pair_sum",
        grid_spec=pltpu.PrefetchScalarGridSpec(
            num_scalar_prefetch=1, grid=(4, n_i),
            in_specs=[pl.BlockSpec((None, tr, SHARD_IN), lambda q, i, c: (2 * q + c[0], i, 0)),
                      pl.BlockSpec((None, tr, SHARD_IN), lambda q, i, c: (q, i, 0)),
                      pl.BlockSpec((None, SHARD_OUT // n_i, D_MODEL), lambda q, i, c: (2 * q + c[0], i, 0)),
                      pl.BlockSpec((None, SHARD_OUT // n_i, D_MODEL), lambda q, i, c: (q, i, 0))],
            out_specs=[pl.BlockSpec((None, tr, SHARD_IN), lambda q, i, c: (q, i, 0)),
                       pl.BlockSpec((None, SHARD_OUT // n_i, D_MODEL), lambda q, i, c: (q, i, 0))]),
        out_shape=[jax.ShapeDtypeStruct(ra.shape, _BF), jax.ShapeDtypeStruct(rb.shape, F32)],
        compiler_params=_params(("arbitrary", "arbitrary")),
    )(c_idx, g_in, ra, g_out, rb)


_HBM = pl.BlockSpec(memory_space=pltpu.HBM)
_SEM = pl.BlockSpec(memory_space=pltpu.SEMAPHORE)
_N_CHIP_COPIES = 6


def _chip_copies(sb_ref, sbo_ref, rc_ref, rco_ref, send, recv):
    pos = _mesh_pos()
    copies = []
    for a, (src, dst) in enumerate(((sb_ref, rc_ref), (sbo_ref, rco_ref))):
        for j, chip in enumerate(_other_chips(pos)):
            copies.append(_remote(src.at[2 * chip[0] + chip[1]], dst.at[j], send.at[3 * a + j], recv.at[3 * a + j],
                                  (*chip, pos[2])))
    return copies


def grad_chip_start(sb, sbo):
    def body(sb_ref, sbo_ref, rc_ref, rco_ref, send, recv, sb_thru, sbo_thru, rc_thru, rco_thru, token):
        del sb_thru, sbo_thru, rc_thru, rco_thru
        for cp in _chip_copies(sb_ref, sbo_ref, rc_ref, rco_ref, send, recv):
            cp.start()
        token[...] = jnp.zeros_like(token)

    hbm = lambda a: pltpu.with_memory_space_constraint(a, pltpu.HBM)
    rc = lax.empty((3,) + sb.shape[1:], sb.dtype)
    rco = lax.empty((3,) + sbo.shape[1:], sbo.dtype)
    return pl.pallas_call(
        body, name="grad_chip_start",
        in_specs=[_HBM] * 4,
        out_specs=[_SEM, _SEM, _HBM, _HBM, _HBM, _HBM, pl.BlockSpec(memory_space=pltpu.VMEM)],
        out_shape=[pltpu.SemaphoreType.DMA((_N_CHIP_COPIES,)), pltpu.SemaphoreType.DMA((_N_CHIP_COPIES,)),
                   pltpu.HBM(sb.shape, sb.dtype), pltpu.HBM(sbo.shape, sbo.dtype),
                   pltpu.HBM(rc.shape, rc.dtype), pltpu.HBM(rco.shape, rco.dtype),
                   jax.ShapeDtypeStruct((8, LANE), F32)],
        input_output_aliases={0: 2, 1: 3, 2: 4, 3: 5},
        compiler_params=pltpu.CompilerParams(has_side_effects=pltpu.SideEffectType.DATAFLOW_SIDE_EFFECTING),
    )(hbm(sb), hbm(sbo), hbm(rc), hbm(rco))


def grad_chip_wait(send, recv, sb, sbo, rc, rco, after):
    def body(sb_ref, sbo_ref, rc_ref, rco_ref, send, recv, after_ref, sb_o, sbo_o, rc_o, rco_o):
        del after_ref, sb_o, sbo_o, rc_o, rco_o
        for cp in _chip_copies(sb_ref, sbo_ref, rc_ref, rco_ref, send, recv):
            cp.wait_send()
            cp.wait_recv()

    return pl.pallas_call(
        body, name="grad_chip_wait",
        in_specs=[_HBM] * 4 + [_SEM, _SEM, pl.BlockSpec(memory_space=pl.ANY)],
        out_specs=[_HBM] * 4,
        out_shape=[pltpu.HBM(sb.shape, sb.dtype), pltpu.HBM(sbo.shape, sbo.dtype),
                   pltpu.HBM(rc.shape, rc.dtype), pltpu.HBM(rco.shape, rco.dtype)],
        input_output_aliases={0: 0, 1: 1, 2: 2, 3: 3},
        compiler_params=pltpu.CompilerParams(has_side_effects=pltpu.SideEffectType.DATAFLOW_SIDE_EFFECTING),
    )(sb, sbo, rc, rco, send, recv, after)


def pack_gather(pack):
    def body(pack_ref, packs_ref, psend, precv):
        pos = _mesh_pos()
        me = _lin(pos)
        packs_ref[me] = pack_ref[...]
        peers = [_xor_peer(pos, k) for k in range(1, N_DEV)]
        gather = [_remote(packs_ref.at[me], packs_ref.at[me], psend.at[n], precv.at[n], p) for n, p in enumerate(peers)]
        for cp in gather:
            cp.start()
        for n, p in enumerate(peers):
            _remote(packs_ref.at[_lin(p)], packs_ref.at[_lin(p)], psend.at[n], precv.at[n], p).wait_recv()
        for cp in gather:
            cp.wait_send()

    vmem = pl.BlockSpec(memory_space=pltpu.VMEM)
    return pl.pallas_call(
        body, name="pack_gather", in_specs=[vmem], out_specs=vmem,
        out_shape=jax.ShapeDtypeStruct((N_DEV,) + pack.shape, F32),
        scratch_shapes=[pltpu.SemaphoreType.DMA((N_DEV - 1,)), pltpu.SemaphoreType.DMA((N_DEV - 1,))],
    )(pack)


def pack_rows(vec_mid, vec_ada, dlb):
    def body(mid_ref, ada_ref, dlb_ref, o_ref):
        mid = lambda r: mid_ref[r:r + 1, :]
        rows = [ada_ref[0:1, :], dlb_ref[...], mid(MID_HG_G), mid(MID_RET_G), mid(MID_FINAL_G),
                ada_ref[2:3, :], ada_ref[1:2, :], mid(MID_GATE), mid(MID_LOSS)]
        o_ref[...] = jnp.zeros_like(o_ref)
        for n, row in enumerate(rows):
            o_ref[n:n + 1, :] = row

    vmem = pl.BlockSpec(memory_space=pltpu.VMEM)
    return pl.pallas_call(body, name="pack_rows", in_specs=[vmem] * 3, out_specs=vmem,
                          out_shape=jax.ShapeDtypeStruct((PACK_ROWS, D_MODEL), F32))(vec_mid, vec_ada, dlb)


def _adamw(w, g, m, v):
    m = ADAM_B1 * m + (1.0 - ADAM_B1) * g
    v = ADAM_B2 * v + (1.0 - ADAM_B2) * (g * g)
    m_hat = m / (1.0 - ADAM_B1 ** ADAM_STEP)
    v_hat = v / (1.0 - ADAM_B2 ** ADAM_STEP)
    delta = -ADAM_LR * (m_hat / (jnp.sqrt(v_hat) + ADAM_EPS) + ADAM_WD * w)
    return delta, m, v


def adam_shard(chip_idx, own, parts, w, m, v, name):
    rows, cols = w.shape
    tr = min(rows, 128)

    def body(chip_ref, p0, p1, p2, p3, w_ref, m_ref, v_ref, g_ref, d_ref, nm_ref, nv_ref):
        del chip_ref
        g = ((p0[...].astype(F32) + p1[...].astype(F32)) + p2[...].astype(F32)) + p3[...].astype(F32)
        g_ref[...] = g
        d_ref[...], nm_ref[...], nv_ref[...] = _adamw(w_ref[...], g, m_ref[...], v_ref[...])

    part = lambda q: pl.BlockSpec((None, tr, cols), lambda i, chip, q=q: (q, i, 0))
    tile = pl.BlockSpec((tr, cols), lambda i, chip: (i, 0))
    return pl.pallas_call(
        body, name=name,
        grid_spec=pltpu.PrefetchScalarGridSpec(
            num_scalar_prefetch=1, grid=(rows // tr,),
            in_specs=[pl.BlockSpec((None, tr, cols), lambda i, chip: (chip[0], i, 0)), part(0), part(1), part(2),
                      tile, tile, tile],
            out_specs=[tile] * 4),
        out_shape=[jax.ShapeDtypeStruct(w.shape, F32)] * 4,
        compiler_params=_params(("arbitrary",)),
    )(chip_idx, own, parts, parts, parts, w, m, v)


def adam_ada(sc_t, dmod_all, me_idx, w, m, v):
    def body(me_ref, sc_ref, dm_ref, w_ref, m_ref, v_ref, g_ref, d_ref, nm_ref, nv_ref):
        del me_ref
        g = _dot_f32(sc_ref[...], dm_ref[...])
        g_ref[...] = g
        d_ref[...], nm_ref[...], nv_ref[...] = _adamw(w_ref[...], g, m_ref[...], v_ref[...])

    full = pl.BlockSpec(w.shape, lambda i, me: (0, 0))
    return pl.pallas_call(
        body, name="adam_ada",
        grid_spec=pltpu.PrefetchScalarGridSpec(
            num_scalar_prefetch=1, grid=(1,),
            in_specs=[pl.BlockSpec(sc_t.shape, lambda i, me: (0, 0)),
                      pl.BlockSpec((LANE, SHARD_ADA), lambda i, me: (0, me[0])), full, full, full],
            out_specs=[full] * 4),
        out_shape=[jax.ShapeDtypeStruct(w.shape, F32)] * 4,
        compiler_params=_params(("arbitrary",)),
    )(me_idx, sc_t, dmod_all, w, m, v)


def adam_vectors(packs, lb, params, ms, vs):
    n = len(params)

    def body(*refs):
        packs_ref, lb_ref = refs[0], refs[1]
        w_refs, m_refs, v_refs = refs[2:2 + n], refs[2 + n:2 + 2 * n], refs[2 + 2 * n:2 + 3 * n]
        loss_ref = refs[2 + 3 * n]
        outs = refs[3 + 3 * n:3 + 7 * n]
        tot_ref = refs[3 + 7 * n]
        tot = packs_ref[0]
        for d in range(1, N_DEV):
            tot = tot + packs_ref[d]
        tot_ref[...] = tot
        row = lambda r: tot_ref[r:r + 1, :]
        lbv = lb_ref[...]
        dl0 = row(ROW_LB) * lbv * (1.0 - lbv)
        grads = [[row(ROW_NORM_G)],
                 [jnp.concatenate([row(ROW_SHIFT), row(ROW_SCALE), row(ROW_GATE)], axis=1)],
                 [dl0, -dl0],
                 [row(ROW_HG_G)], [row(ROW_RET_G)], [row(ROW_FINAL_G)]]
        loss_ref[...] = tot_ref[ROW_LOSS:ROW_LOSS + 1, 0:LANE]
        for j, g_rows in enumerate(grads):
            for r, g in enumerate(g_rows):
                rs = slice(r, r + 1)
                d, nm, nv = _adamw(w_refs[j][rs, :], g, m_refs[j][rs, :], v_refs[j][rs, :])
                outs[4 * j][rs, :] = g
                outs[4 * j + 1][rs, :] = d
                outs[4 * j + 2][rs, :] = nm
                outs[4 * j + 3][rs, :] = nv

    vmem = pl.BlockSpec(memory_space=pltpu.VMEM)
    out_shape = [jax.ShapeDtypeStruct((1, LANE), F32)]
    for w in params:
        out_shape += [jax.ShapeDtypeStruct(w.shape, F32)] * 4
    return pl.pallas_call(
        body, name="adam_vectors", in_specs=[vmem] * (2 + 3 * n), out_specs=[vmem] * len(out_shape),
        out_shape=out_shape, scratch_shapes=[pltpu.VMEM((PACK_ROWS, D_MODEL), F32)],
    )(packs, lb, *params, *ms, *vs)


def kernel(x, c, norm_g, w_ada, b_ada, w_in, hg_lb_logits, hg_norm_g, ret_norm_g, w_out, final_g, loss_target, m_norm_g, m_w_ada, m_b_ada, m_w_in, m_hg_lb_logits, m_hg_norm_g, m_ret_norm_g, m_w_out, m_final_g, v_norm_g, v_w_ada, v_b_ada, v_w_in, v_hg_lb_logits, v_hg_norm_g, v_ret_norm_g, v_w_out, v_final_g):
    pos = _mesh_pos()
    me_idx = jnp.reshape(_lin(pos), (1,)).astype(jnp.int32)
    c_idx = jnp.reshape(pos[2], (1,)).astype(jnp.int32)
    vec = lambda a: a.reshape(1, D_MODEL)

    mod, scall, lb = pre_exchange(c, w_ada[0], b_ada, hg_lb_logits)
    wtg, woutg = weight_gather(w_in[0].T.astype(_BF), w_out[0].astype(_BF))
    chip_idx = jnp.reshape(2 * pos[0] + pos[1], (1,)).astype(jnp.int32)

    def start_exchange(dwin, dwout):
        dwout = dwout.reshape(N_DEV, SHARD_OUT, D_MODEL)
        ra, rb = grad_pair_exchange(dwin, dwout)
        sb, sbo = pair_sum(dwin, ra, dwout, rb, c_idx)
        send, recv, sb, sbo, rc, rco, token = grad_chip_start(sb, sbo)
        return token, (send, recv, sb, sbo, rc, rco)

    grad_x, _, _, vec_mid, vec_ada, dlb, pending = device_step(
        x[0], loss_target[0], mod, lb, wtg.reshape(D_IN, D_MODEL), woutg.reshape(D_MODEL, D_MODEL), norm_g,
        hg_norm_g, ret_norm_g, vec(final_g), start_exchange)
    sb, sbo, rc, rco = grad_chip_wait(*pending, grad_x)
    packs = pack_gather(pack_rows(vec_mid, vec_ada, dlb))

    g_in, d_in, nm_in, nv_in = adam_shard(chip_idx, sb, rc, w_in[0], m_w_in[0], v_w_in[0], "adam_w_in")
    g_out, d_out, nm_out, nv_out = adam_shard(chip_idx, sbo, rco, w_out[0], m_w_out[0], v_w_out[0], "adam_w_out")
    dmod_all = packs[:, ROW_SHIFT:ROW_GATE + 1, :].reshape(N_DEV, 3 * D_MODEL)
    dmod_all = jnp.pad(dmod_all, ((0, LANE - N_DEV), (0, 0)))
    sc_t = jnp.pad(scall.T, ((0, 0), (0, LANE - N_DEV)))
    g_ada, d_ada, nm_ada, nv_ada = adam_ada(sc_t, dmod_all, me_idx, w_ada[0], m_w_ada[0], v_w_ada[0])
    small = adam_vectors(
        packs, lb,
        (norm_g, b_ada, hg_lb_logits, hg_norm_g, ret_norm_g, vec(final_g)),
        (m_norm_g, m_b_ada, m_hg_lb_logits, m_hg_norm_g, m_ret_norm_g, vec(m_final_g)),
        (v_norm_g, v_b_ada, v_hg_lb_logits, v_hg_norm_g, v_ret_norm_g, vec(v_final_g)))
    loss = small[0][0, 0]
    (g_ng, d_ng, nm_ng, nv_ng), (g_b, d_b, nm_b, nv_b), (g_lb, d_lb, nm_lb, nv_lb), (g_hg, d_hg, nm_hg, nv_hg), \
        (g_rg, d_rg, nm_rg, nv_rg), (g_fg, d_fg, nm_fg, nv_fg) = [small[1 + 4 * j:5 + 4 * j] for j in range(6)]
    flat = lambda a: a.reshape(D_MODEL)

    def group(ng, ada, b, win, lbl, hg, rg, wo, fg):
        return (ng, ada[None], b, win[None], lbl, hg, rg, wo[None], flat(fg))

    return (loss, grad_x[None],
            *group(g_ng, g_ada, g_b, g_in, g_lb, g_hg, g_rg, g_out, g_fg),
            *group(d_ng, d_ada, d_b, d_in, d_lb, d_hg, d_rg, d_out, d_fg),
            *group(nm_ng, nm_ada, nm_b, nm_in, nm_lb, nm_hg, nm_rg, nm_out, nm_fg),
            *group(nv_ng, nv_ada, nv_b, nv_in, nv_lb, nv_hg, nv_rg, nv_out, nv_fg))
```

```python
import functools

import numpy as np
import jax
import jax.numpy as jnp
from jax import lax
from jax.experimental import pallas as pl
from jax.experimental.pallas import tpu as pltpu

F32 = jnp.float32
_BF = jnp.bfloat16

D_MODEL = 1024
N_HEADS = 8
LANE = 128
RET_DK = 64
D_IN = 9216
N_DEV = 8
SHARD_IN = D_IN // N_DEV
SHARD_ADA = 3 * D_MODEL // N_DEV
SHARD_OUT = D_MODEL // N_DEV
N_CB = D_IN // LANE
CB_PER_SHARD = SHARD_IN // LANE
CHUNK = 128
N_LEVELS = 7
EPS = 1e-6
ROPE_BASE = 10000.0
CB_HQ, CB_HF, CB_HI, CB_HZ, CB_RQ, CB_RK, CB_RV, CB_RZ, CB_GA, CB_GB = 0, 8, 16, 24, 32, 36, 40, 48, 56, 64
VMEM_LIMIT = 56 * 1024 * 1024

ADAM_LR, ADAM_B1, ADAM_B2, ADAM_EPS, ADAM_WD, ADAM_STEP = 0.001, 0.9, 0.999, 1e-08, 0.01, 10

_NN = (((1,), (0,)), ((), ()))
_NT = (((1,), (1,)), ((), ()))
_TN = (((0,), (0,)), ((), ()))
MESH = pl.DeviceIdType.MESH


def _dot(a, b, dims=_NN):
    return lax.dot_general(a.astype(_BF), b.astype(_BF), dims, preferred_element_type=F32)


def _split2(a):
    hi = a.astype(_BF)
    lo = (a - hi.astype(F32)).astype(_BF)
    return jnp.concatenate([hi, lo], axis=1)


def _dot_sel(sel, a):
    n = a.shape[1]
    r = lax.dot_general(sel.astype(_BF), _split2(a), _NN, preferred_element_type=F32)
    return r[:, :n] + r[:, n:]


def _dot_f32(a, b):
    def pieces(v):
        p1 = v.astype(_BF)
        r1 = v - p1.astype(F32)
        p2 = r1.astype(_BF)
        p3 = (r1 - p2.astype(F32)).astype(_BF)
        return p1, p2, p3
    a1, a2, a3 = pieces(a)
    b1, b2, b3 = pieces(b)
    d = lambda u, v: lax.dot_general(u, v, _NN, preferred_element_type=F32)
    return ((d(a1, b3) + d(a2, b2) + d(a3, b1)) + (d(a1, b2) + d(a2, b1))) + d(a1, b1)


def _sigmoid(v):
    return 1.0 / (1.0 + jnp.exp(-v))


def _params(sem=None):
    return pltpu.CompilerParams(dimension_semantics=sem, vmem_limit_bytes=VMEM_LIMIT)


def _hgrn_consts(transposed=False):
    c, nl = CHUNK, N_LEVELS
    t = np.arange(c)[:, None]
    j = np.arange(c)[None, :]
    sel = [j <= t]
    masks = [j == t]
    for l in range(1, nl + 1):
        m = ((t >> l) << l) + (1 << (l - 1)) - 1
        sec = t > m
        sel.append(np.where(sec, (j > m) & (j <= t), (j > t) & (j <= m)))
        same = (t >> l) == (j >> l)
        masks.append(same & sec & (j <= m))
    sel.append(j > t)
    sel = np.concatenate(sel, 0).astype(np.float32)
    masks = np.stack(masks).astype(np.float32)
    if transposed:
        return jnp.asarray(sel.T, _BF), jnp.asarray(np.swapaxes(masks, 1, 2), F32)
    return jnp.asarray(sel, _BF), jnp.asarray(masks, F32)


def _hgrn_chunk(hq, hf, hi, lbv, sel_ref, lvl_ref):
    c, nl = CHUNK, N_LEVELS
    sq = _sigmoid(hq)
    q = hq * sq
    sg = _sigmoid(hf)
    omlb = 1.0 - lbv
    f = lbv + omlb * sg
    k = 1.0 - f
    logf = jnp.log(f)
    e_all = _dot_sel(sel_ref[...], logf)
    eb = jnp.exp(e_all[0:c])
    erem = jnp.exp(e_all[(nl + 1) * c:(nl + 2) * c])
    ebc = jnp.exp(jnp.sum(logf, axis=0, keepdims=True))
    lev = [None] + [jnp.exp(e_all[l * c:(l + 1) * c]) for l in range(1, nl + 1)]
    return dict(sq=sq, q=q, sg=sg, omlb=omlb, f=f, k=k, v=hi, eb=eb, erem=erem, ebc=ebc, lev=lev)


def _hgrn_scores(a, lvl_ref):
    q, k = a["q"], a["k"]
    acc = lvl_ref[0] * _dot(q, k, _NT)
    for l in range(1, N_LEVELS + 1):
        e = a["lev"][l]
        acc = acc + lvl_ref[l] * _dot(q * e, k * e, _NT)
    return acc


SCAN_UNROLL = 2


def _time_block(t_len):
    return min(t_len, 1024)


def hgrn_forward(pb, lb, t_len):
    nc = t_len // CHUNK
    tb = _time_block(t_len)
    ncb = tb // CHUNK
    sel, lvl = _hgrn_consts()

    def body(hq_ref, hf_ref, hi_ref, lb_ref, sel_ref, lvl_ref, o_ref, ssave_ref, st_ref):
        @pl.when(pl.program_id(1) == 0)
        def _():
            st_ref[...] = jnp.zeros_like(st_ref)

        def chunk(ci, carry):
            r = pl.ds(pl.multiple_of(ci * CHUNK, CHUNK), CHUNK)
            for hd in range(2):
                lbv = lb_ref[:, hd * LANE:(hd + 1) * LANE]
                a = _hgrn_chunk(hq_ref[hd, r, :], hf_ref[hd, r, :], hi_ref[hd, r, :], lbv, sel_ref, lvl_ref)
                st = st_ref[hd]
                ssave_ref[hd, ci] = st
                o_ref[hd, r, :] = _dot(a["q"] * a["eb"], st, _NT) + _dot(_hgrn_scores(a, lvl_ref), a["v"])
                st_ref[hd] = st * a["ebc"] + _dot(a["v"], a["k"] * a["erem"], _TN)
            return carry

        lax.fori_loop(0, ncb, chunk, 0)

    pair = lambda base: pl.BlockSpec((2, tb, LANE), lambda p, t, base=base: (base // 2 + p, t, 0))
    return pl.pallas_call(
        body, name="hgrn_fwd", grid=(N_HEADS // 2, t_len // tb),
        in_specs=[pair(CB_HQ), pair(CB_HF), pair(CB_HI),
                  pl.BlockSpec((1, 2 * LANE), lambda p, t: (0, p)),
                  pl.BlockSpec(sel.shape, lambda p, t: (0, 0)),
                  pl.BlockSpec(lvl.shape, lambda p, t: (0, 0, 0))],
        out_specs=[pl.BlockSpec((2, tb, LANE), lambda p, t: (p, t, 0)),
                   pl.BlockSpec((2, ncb, LANE, LANE), lambda p, t: (p, t, 0, 0))],
        out_shape=[jax.ShapeDtypeStruct((N_HEADS, t_len, LANE), F32),
                   jax.ShapeDtypeStruct((N_HEADS, nc, LANE, LANE), F32)],
        scratch_shapes=[pltpu.VMEM((2, LANE, LANE), F32)],
        compiler_params=_params(("arbitrary", "arbitrary")),
    )(pb, pb, pb, lb, sel, lvl)


def hgrn_backward(pb, lb, do, ssave, dpb, t_len):
    tb = _time_block(t_len)
    ncb, ntb = tb // CHUNK, t_len // tb
    sel, lvl = _hgrn_consts()
    sel_t, lvl_t = _hgrn_consts(transposed=True)

    def body(hq_ref, hf_ref, hi_ref, lb_ref, do_ref, ssave_ref, sel_ref, lvl_ref, selt_ref, lvlt_ref, dpb_in,
             dpb_ref, dlb_ref, dq_buf, df_buf, di_buf, dst_ref, sems):
        del dpb_in
        p, t = pl.program_id(0), pl.program_id(1)

        @pl.when(t == 0)
        def _():
            dst_ref[...] = jnp.zeros_like(dst_ref)
            dlb_ref[...] = jnp.zeros_like(dlb_ref)

        def chunk(i, carry):
            ci = ncb - 1 - i
            r = pl.ds(pl.multiple_of(ci * CHUNK, CHUNK), CHUNK)
            for hd in range(2):
                head_chunk(hd, ci, r)
            return carry

        def head_chunk(hd, ci, r):
            lbv = lb_ref[:, hd * LANE:(hd + 1) * LANE]
            hq = hq_ref[hd, r, :]
            a = _hgrn_chunk(hq, hf_ref[hd, r, :], hi_ref[hd, r, :], lbv, sel_ref, lvl_ref)
            q, k, v = a["q"], a["k"], a["v"]
            g = do_ref[hd, r, :]
            st0 = ssave_ref[hd, ci]
            dst = dst_ref[hd]
            scores = _hgrn_scores(a, lvl_ref)
            da = _dot(g, v, _NT)
            da_t = _dot(v, g, _NT)
            kb = k * a["erem"]
            qb = q * a["eb"]
            dv = _dot(scores, g, _TN) + _dot(kb, dst, _NT)
            dq_inter = _dot(g, st0) * a["eb"]
            dk_state = _dot(v, dst) * a["erem"]
            dq = dq_inter + _dot(lvl_ref[0] * da, k)
            dk = dk_state + _dot(lvlt_ref[0] * da_t, q)
            de = [q * dq_inter]
            for l in range(1, N_LEVELS + 1):
                e = a["lev"][l]
                dql = _dot(lvl_ref[l] * da, k * e) * e
                dkl = _dot(lvlt_ref[l] * da_t, q * e) * e
                dq = dq + dql
                dk = dk + dkl
                de.append(q * dql + k * dkl)
            de.append(k * dk_state)
            dst_ref[hd] = dst * a["ebc"] + _dot(g, qb, _TN)
            dbc = jnp.sum(dst * st0, axis=0, keepdims=True) * a["ebc"]
            de2 = lax.dot_general(selt_ref[...], _split2(jnp.concatenate(de, axis=0)), _NN,
                                  preferred_element_type=F32)
            dlogf = de2[:, :LANE] + de2[:, LANE:] + dbc
            sq, sg = a["sq"], a["sg"]
            df = dlogf / a["f"] - dk
            dq_buf[hd, r, :] = dq * (sq * (1.0 + hq * (1.0 - sq)))
            df_buf[hd, r, :] = df * a["omlb"] * sg * (1.0 - sg)
            di_buf[hd, r, :] = dv
            cols = slice(hd * LANE, (hd + 1) * LANE)
            dlb_ref[:, cols] = dlb_ref[:, cols] + jnp.sum(df * (1.0 - sg), axis=0, keepdims=True)

        lax.fori_loop(0, ncb, chunk, 0)
        rows = pl.ds(pl.multiple_of((ntb - 1 - t) * tb, tb), tb)
        copies = [pltpu.make_async_copy(buf, dpb_ref.at[pl.ds(base + 2 * p, 2), rows], sems.at[n])
                  for n, (buf, base) in enumerate(((dq_buf, CB_HQ), (df_buf, CB_HF), (di_buf, CB_HI)))]
        for cp in copies:
            cp.start()
        for cp in copies:
            cp.wait()

    pair = lambda base: pl.BlockSpec((2, tb, LANE), lambda p, t, base=base: (base // 2 + p, ntb - 1 - t, 0))
    any_spec = pl.BlockSpec(memory_space=pl.ANY)
    return pl.pallas_call(
        body, name="hgrn_bwd", grid=(N_HEADS // 2, ntb),
        in_specs=[pair(CB_HQ), pair(CB_HF), pair(CB_HI),
                  pl.BlockSpec((1, 2 * LANE), lambda p, t: (0, p)),
                  pair(0),
                  pl.BlockSpec((2, ncb, LANE, LANE), lambda p, t: (p, ntb - 1 - t, 0, 0)),
                  pl.BlockSpec(sel.shape, lambda p, t: (0, 0)),
                  pl.BlockSpec(lvl.shape, lambda p, t: (0, 0, 0)),
                  pl.BlockSpec(sel_t.shape, lambda p, t: (0, 0)),
                  pl.BlockSpec(lvl_t.shape, lambda p, t: (0, 0, 0)),
                  any_spec],
        out_specs=[any_spec, pl.BlockSpec((1, 2 * LANE), lambda p, t: (0, p))],
        out_shape=[jax.ShapeDtypeStruct(dpb.shape, F32), jax.ShapeDtypeStruct((1, D_MODEL), F32)],
        scratch_shapes=[pltpu.VMEM((2, tb, LANE), F32)] * 3 + [
            pltpu.VMEM((2, LANE, LANE), F32), pltpu.SemaphoreType.DMA((3,))],
        input_output_aliases={10: 0},
        compiler_params=_params(("arbitrary", "arbitrary")),
    )(pb, pb, pb, lb, do, ssave, sel, lvl, sel_t, lvl_t, dpb)


def _rope_tables(t_len):
    half = RET_DK // 2
    inv_freq = 1.0 / (ROPE_BASE ** jnp.linspace(0.0, 1.0, half, dtype=F32))
    ang = jnp.arange(t_len, dtype=jnp.int32).astype(F32)[:, None] * inv_freq[None, :]
    cos, sin = jnp.cos(ang), jnp.sin(ang)
    cos_t = jnp.concatenate([cos, cos, cos, cos], axis=1)
    sin_t = jnp.concatenate([-sin, sin, -sin, sin], axis=1)
    return cos_t, sin_t


def _swap_halves(v):
    half = RET_DK // 2
    lane = lax.broadcasted_iota(jnp.int32, v.shape, 1)
    first = (lane & (RET_DK - 1)) < half
    return jnp.where(first, pltpu.roll(v, LANE - half, 1), pltpu.roll(v, half, 1))


def _ret_head_consts(hidx):
    c = CHUNK
    hf = jnp.full((1, LANE), hidx, jnp.int32).astype(F32)
    lg = jnp.log(1.0 - jnp.exp(-(5.0 + hf) * np.float32(np.log(2.0))))
    row = lax.broadcasted_iota(jnp.int32, (c, c), 0)
    col = lax.broadcasted_iota(jnp.int32, (c, c), 1)
    rel = (row - col).astype(F32)
    dm = jnp.where(rel >= 0, jnp.exp(lg[:, :1] * jnp.maximum(rel, 0.0)), 0.0)
    dm_t = jnp.where(rel <= 0, jnp.exp(lg[:, :1] * jnp.maximum(-rel, 0.0)), 0.0)
    idx = lax.broadcasted_iota(jnp.int32, (c, LANE), 0).astype(F32)
    zeta = jnp.exp(lg * (c - 1.0 - idx))
    xi = jnp.exp(lg * (idx + 1.0))
    cdec = jnp.exp(lg * float(c))
    return dm, zeta, xi, cdec, dm_t


def _lane_mask(which):
    lane = lax.broadcasted_iota(jnp.int32, (1, LANE), 1)
    return ((lane // RET_DK) == which).astype(F32)


def retention_forward(pb, cos_t, sin_t, t_len):
    nc = t_len // CHUNK

    tb = _time_block(t_len)
    ncb = tb // CHUNK

    def body(rq_ref, rk_ref, rv_ref, cos_ref, sin_ref, o_ref, rsave_ref, st_ref):
        p = pl.program_id(0)

        @pl.when(pl.program_id(1) == 0)
        def _():
            st_ref[...] = jnp.zeros_like(st_ref)

        consts = [_ret_head_consts(2 * p + hd) for hd in range(2)]

        def chunk(ci, carry):
            r = pl.ds(pl.multiple_of(ci * CHUNK, CHUNK), CHUNK)
            cs, sn = cos_ref[r, :], sin_ref[r, :]
            q = rq_ref[r, :]
            k = rk_ref[r, :]
            q = q * cs + _swap_halves(q) * sn
            k = (k * cs + _swap_halves(k) * sn) * RET_DK ** -0.5
            for hd in range(2):
                dm, zeta, xi, cdec, _ = consts[hd]
                lm = _lane_mask(hd)
                qh, kh = q * lm, k * lm
                v = rv_ref[hd, r, :]
                st = st_ref[hd]
                rsave_ref[hd, ci] = st
                scores = _dot(qh, kh, _NT) * dm
                o_ref[hd, r, :] = _dot(scores, v) + _dot(qh * xi, st, _NT)
                st_ref[hd] = st * cdec + _dot(v, kh * zeta, _TN)
            return carry

        lax.fori_loop(0, ncb, chunk, 0, unroll=SCAN_UNROLL)

    return pl.pallas_call(
        body, name="ret_fwd", grid=(N_HEADS // 2, t_len // tb),
        in_specs=[pl.BlockSpec((None, tb, LANE), lambda p, t: (CB_RQ + p, t, 0)),
                  pl.BlockSpec((None, tb, LANE), lambda p, t: (CB_RK + p, t, 0)),
                  pl.BlockSpec((2, tb, LANE), lambda p, t: (CB_RV // 2 + p, t, 0)),
                  pl.BlockSpec((tb, LANE), lambda p, t: (t, 0)),
                  pl.BlockSpec((tb, LANE), lambda p, t: (t, 0))],
        out_specs=[pl.BlockSpec((2, tb, LANE), lambda p, t: (p, t, 0)),
                   pl.BlockSpec((2, ncb, LANE, LANE), lambda p, t: (p, t, 0, 0))],
        out_shape=[jax.ShapeDtypeStruct((N_HEADS, t_len, LANE), F32),
                   jax.ShapeDtypeStruct((N_HEADS, nc, LANE, LANE), F32)],
        scratch_shapes=[pltpu.VMEM((2, LANE, LANE), F32)],
        compiler_params=_params(("arbitrary", "arbitrary")),
    )(pb, pb, pb, cos_t, sin_t)


def retention_backward(pb, cos_t, sin_t, do, rsave, dpb, t_len):
    tb = _time_block(t_len)
    ncb, ntb = tb // CHUNK, t_len // tb

    def body(rq_ref, rk_ref, rv_ref, cos_ref, sin_ref, do_ref, rsave_ref, dpb_in,
             dpb_ref, dq_buf, dk_buf, dv_buf, dst_ref, sems):
        del dpb_in
        p, t = pl.program_id(0), pl.program_id(1)

        @pl.when(t == 0)
        def _():
            dst_ref[...] = jnp.zeros_like(dst_ref)

        consts = [_ret_head_consts(2 * p + hd) for hd in range(2)]

        def chunk(i, carry):
            ci = ncb - 1 - i
            r = pl.ds(pl.multiple_of(ci * CHUNK, CHUNK), CHUNK)
            cs, sn = cos_ref[r, :], sin_ref[r, :]
            q = rq_ref[r, :]
            k = rk_ref[r, :]
            q = q * cs + _swap_halves(q) * sn
            k = (k * cs + _swap_halves(k) * sn) * RET_DK ** -0.5
            dq, dk = None, None
            for hd in range(2):
                dm, zeta, xi, cdec, dm_t = consts[hd]
                lm = _lane_mask(hd)
                qh, kh = q * lm, k * lm
                v = rv_ref[hd, r, :]
                g = do_ref[hd, r, :]
                st0 = rsave_ref[hd, ci]
                dst = dst_ref[hd]
                scores_t = _dot(kh, qh, _NT) * dm_t
                dsc = _dot(g, v, _NT) * dm
                dsc_t = _dot(v, g, _NT) * dm_t
                dqh = _dot(dsc, kh) + _dot(g, st0) * xi
                dkh = _dot(dsc_t, qh) + _dot(v, dst) * zeta
                dv_buf[hd, r, :] = _dot(scores_t, g) + _dot(kh * zeta, dst, _NT)
                dst_ref[hd] = dst * cdec + _dot(g, qh * xi, _TN)
                dq = dqh if dq is None else dq + dqh
                dk = dkh if dk is None else dk + dkh
            dk = dk * (RET_DK ** -0.5)
            dq_buf[r, :] = dq * cs - _swap_halves(dq) * sn
            dk_buf[r, :] = dk * cs - _swap_halves(dk) * sn
            return carry

        lax.fori_loop(0, ncb, chunk, 0, unroll=SCAN_UNROLL)
        rows = pl.ds(pl.multiple_of((ntb - 1 - t) * tb, tb), tb)
        copies = [pltpu.make_async_copy(dq_buf, dpb_ref.at[CB_RQ + p, rows], sems.at[0]),
                  pltpu.make_async_copy(dk_buf, dpb_ref.at[CB_RK + p, rows], sems.at[1]),
                  pltpu.make_async_copy(dv_buf, dpb_ref.at[pl.ds(CB_RV + 2 * p, 2), rows], sems.at[2])]
        for cp in copies:
            cp.start()
        for cp in copies:
            cp.wait()

    any_spec = pl.BlockSpec(memory_space=pl.ANY)
    return pl.pallas_call(
        body, name="ret_bwd", grid=(N_HEADS // 2, ntb),
        in_specs=[pl.BlockSpec((None, tb, LANE), lambda p, t: (CB_RQ + p, ntb - 1 - t, 0)),
                  pl.BlockSpec((None, tb, LANE), lambda p, t: (CB_RK + p, ntb - 1 - t, 0)),
                  pl.BlockSpec((2, tb, LANE), lambda p, t: (CB_RV // 2 + p, ntb - 1 - t, 0)),
                  pl.BlockSpec((tb, LANE), lambda p, t: (ntb - 1 - t, 0)),
                  pl.BlockSpec((tb, LANE), lambda p, t: (ntb - 1 - t, 0)),
                  pl.BlockSpec((2, tb, LANE), lambda p, t: (p, ntb - 1 - t, 0)),
                  pl.BlockSpec((2, ncb, LANE, LANE), lambda p, t: (p, ntb - 1 - t, 0, 0)),
                  any_spec],
        out_specs=any_spec,
        out_shape=jax.ShapeDtypeStruct(dpb.shape, F32),
        scratch_shapes=[pltpu.VMEM((tb, LANE), F32), pltpu.VMEM((tb, LANE), F32),
                        pltpu.VMEM((2, tb, LANE), F32), pltpu.VMEM((2, LANE, LANE), F32),
                        pltpu.SemaphoreType.DMA((3,))],
        input_output_aliases={7: 0},
        compiler_params=_params(("arbitrary", "arbitrary")),
    )(pb, pb, pb, cos_t, sin_t, do, rsave, dpb)


def _row_tile(t_len, want):
    return min(want, t_len)


PAIR_CB = 2 * CB_PER_SHARD


def proj_forward(h, wt, t_len):
    tm = _row_tile(t_len, 512)

    def body(h_ref, w_ref, o_ref):
        acc = _dot(h_ref[...], w_ref[...], _NT)
        for jj in range(PAIR_CB):
            o_ref[jj] = acc[:, jj * LANE:(jj + 1) * LANE]

    return pl.pallas_call(
        body, name="proj_fwd", grid=(N_DEV // 2, t_len // tm),
        in_specs=[pl.BlockSpec((tm, D_MODEL), lambda j, i: (i, 0)),
                  pl.BlockSpec((PAIR_CB * LANE, D_MODEL), lambda j, i: (j, 0))],
        out_specs=pl.BlockSpec((PAIR_CB, tm, LANE), lambda j, i: (j, i, 0)),
        out_shape=jax.ShapeDtypeStruct((N_CB, t_len, LANE), F32),
        compiler_params=_params(("arbitrary", "arbitrary")),
    )(h, wt)


def proj_backward_input(dpb, wt, token, t_len):
    tm = _row_tile(t_len, 256)

    def body(a_ref, wt_hbm, token_ref, o_ref, w_ref, sem):
        del token_ref

        @pl.when(pl.program_id(0) == 0)
        def _():
            cp = pltpu.make_async_copy(wt_hbm, w_ref, sem)
            cp.start()
            cp.wait()

        a = jnp.concatenate([a_ref[jj].astype(_BF) for jj in range(N_CB)], axis=1)
        o_ref[...] = _dot(a, w_ref[...])

    return pl.pallas_call(
        body, name="proj_bwd_input", grid=(t_len // tm,),
        in_specs=[pl.BlockSpec((N_CB, tm, LANE), lambda i: (0, i, 0)),
                  pl.BlockSpec(memory_space=pl.ANY),
                  pl.BlockSpec(token.shape, lambda i: (0, 0))],
        out_specs=pl.BlockSpec((tm, D_MODEL), lambda i: (i, 0)),
        out_shape=jax.ShapeDtypeStruct((t_len, D_MODEL), F32),
        scratch_shapes=[pltpu.VMEM(wt.shape, wt.dtype), pltpu.SemaphoreType.DMA],
        compiler_params=_params(("arbitrary",)),
    )(dpb, wt, token)


def proj_backward_weight(h_t, dpb, t_len):
    tk = _row_tile(t_len, 1024)

    def body(h_ref, b_ref, o_ref):
        k = pl.program_id(1)
        b = jnp.concatenate([b_ref[jj].astype(_BF) for jj in range(PAIR_CB)], axis=1)
        part = _dot(h_ref[...], b)

        @pl.when(k == 0)
        def _():
            for s in range(2):
                o_ref[s] = part[:, s * SHARD_IN:(s + 1) * SHARD_IN]

        @pl.when(k > 0)
        def _():
            for s in range(2):
                o_ref[s] = o_ref[s] + part[:, s * SHARD_IN:(s + 1) * SHARD_IN]

    return pl.pallas_call(
        body, name="proj_bwd_weight", grid=(N_DEV // 2, t_len // tk),
        in_specs=[pl.BlockSpec((D_MODEL, tk), lambda j, k: (0, k)),
                  pl.BlockSpec((PAIR_CB, tk, LANE), lambda j, k: (j, k, 0))],
        out_specs=pl.BlockSpec((2, D_MODEL, SHARD_IN), lambda j, k: (j, 0, 0)),
        out_shape=jax.ShapeDtypeStruct((N_DEV, D_MODEL, SHARD_IN), F32),
        compiler_params=_params(("arbitrary", "arbitrary")),
    )(h_t, dpb)


def _vec_spec():
    return pl.BlockSpec((1, D_MODEL), lambda i: (0, 0))


def _acc_rows(ref, i, rows):
    @pl.when(i == 0)
    def _():
        ref[...] = jnp.zeros_like(ref)

    for n, row in enumerate(rows):
        ref[n:n + 1, :] = ref[n:n + 1, :] + row


def adaln_forward(x, norm_g, scale1p, shift, t_len):
    tm = _row_tile(t_len, 512)

    def body(x_ref, g_ref, sc_ref, sh_ref, h_ref, ht_ref):
        xv = x_ref[...]
        r = lax.rsqrt(jnp.mean(xv * xv, axis=-1, keepdims=True) + EPS)
        h = xv * r * g_ref[...] * sc_ref[...] + sh_ref[...]
        h_ref[...] = h.astype(h_ref.dtype)
        ht_ref[...] = h.T.astype(ht_ref.dtype)

    return pl.pallas_call(
        body, name="adaln_fwd", grid=(t_len // tm,),
        in_specs=[pl.BlockSpec((tm, D_MODEL), lambda i: (i, 0)), _vec_spec(), _vec_spec(), _vec_spec()],
        out_specs=[pl.BlockSpec((tm, D_MODEL), lambda i: (i, 0)), pl.BlockSpec((D_MODEL, tm), lambda i: (0, i))],
        out_shape=[jax.ShapeDtypeStruct((t_len, D_MODEL), _BF), jax.ShapeDtypeStruct((D_MODEL, t_len), _BF)],
        compiler_params=_params(("arbitrary",)),
    )(x, norm_g, scale1p, shift)


def adaln_backward(dh, x, dy, norm_g, scale1p, t_len):
    tm = _row_tile(t_len, 512)

    def body(dh_ref, x_ref, dy_ref, g_ref, sc_ref, gx_ref, vec_ref):
        i = pl.program_id(0)
        xv, dhv = x_ref[...], dh_ref[...]
        g, sc = g_ref[...], sc_ref[...]
        r = lax.rsqrt(jnp.mean(xv * xv, axis=-1, keepdims=True) + EPS)
        xn = xv * r
        dxn = dhv * (g * sc)
        gx_ref[...] = dy_ref[...] + r * dxn - xn * (r * r) * jnp.mean(xv * dxn, axis=-1, keepdims=True)
        t = dhv * xn
        _acc_rows(vec_ref, i, [jnp.sum(t * sc, axis=0, keepdims=True),
                               jnp.sum(t * g, axis=0, keepdims=True),
                               jnp.sum(dhv, axis=0, keepdims=True)])

    row = pl.BlockSpec((tm, D_MODEL), lambda i: (i, 0))
    return pl.pallas_call(
        body, name="adaln_bwd", grid=(t_len // tm,),
        in_specs=[row, row, row, _vec_spec(), _vec_spec()],
        out_specs=[row, pl.BlockSpec((8, D_MODEL), lambda i: (0, 0))],
        out_shape=[jax.ShapeDtypeStruct((t_len, D_MODEL), F32), jax.ShapeDtypeStruct((8, D_MODEL), F32)],
        compiler_params=_params(("arbitrary",)),
    )(dh, x, dy, norm_g, scale1p)


def _head_norm(o, g):
    r = lax.rsqrt(jnp.mean(o * o, axis=-1, keepdims=True) + EPS)
    return r, o * r * g


def _group_spec(tm, cb):
    return pl.BlockSpec((N_HEADS, tm, LANE), lambda i, cb=cb: (cb // N_HEADS, i, 0))


MID_FINAL_G, MID_GATE, MID_LOSS, MID_HG_G, MID_RET_G = range(5)


def middle(x, target, oa, ob, pb, wout, gate, final_g, hg_g, ret_g, t_len):
    tm = _row_tile(t_len, 256)

    def body(x_ref, t_ref, oa_ref, ob_ref, hz_ref, rz_ref, ga_ref, gb_ref, w_ref, gate_ref, fg_ref, hg_ref, rg_ref,
             dy_ref, doa_ref, dob_ref, dw_ref, vec_ref, dpb_ref, m_scr, dm_scr, keep, bufs, sems):
        i = pl.program_id(0)
        sides = ((oa_ref, hz_ref, ga_ref, hg_ref, doa_ref), (ob_ref, rz_ref, gb_ref, rg_ref, dob_ref))
        for hh in range(N_HEADS):
            ls = slice(hh * LANE, (hh + 1) * LANE)
            acc = None
            for side, (o_ref, z_ref, gt_ref, g_ref, _) in enumerate(sides):
                o = o_ref[hh]
                rr = lax.rsqrt(jnp.mean(o * o, axis=-1, keepdims=True) + EPS)
                orr = o * rr
                zz = z_ref[hh]
                sz = _sigmoid(zz)
                sgt = _sigmoid(gt_ref[hh])
                keep[side, hh, 0] = orr
                keep[side, hh, 1] = sz
                keep[side, hh, 2] = sgt
                keep[side, hh, 3] = jnp.broadcast_to(rr, orr.shape)
                u = sgt * ((orr * g_ref[:, ls]) * (zz * sz))
                acc = u if acc is None else acc + u
            m_scr[:, ls] = acc.astype(m_scr.dtype)
        zv = _dot(m_scr[...], w_ref[...])
        gt, fg = gate_ref[...], fg_ref[...]
        y = x_ref[...] + gt * zv
        r = lax.rsqrt(jnp.mean(y * y, axis=-1, keepdims=True) + EPS)
        yn = y * r
        err = yn * fg - t_ref[...]
        loss = 0.5 * jnp.sum(jnp.mean(err * err, axis=-1, keepdims=True), axis=0, keepdims=True)
        dout = err * (1.0 / D_MODEL)
        gd = dout * fg
        dy = r * gd - yn * (r * r) * jnp.mean(y * gd, axis=-1, keepdims=True)
        dy_ref[...] = dy
        dz = (dy * gt).astype(_BF)
        dm_scr[...] = _dot(dz, w_ref[...], _NT)
        part = _dot(m_scr[...], dz, _TN)

        @pl.when(i == 0)
        def _():
            dw_ref[...] = part

        @pl.when(i > 0)
        def _():
            dw_ref[...] = dw_ref[...] + part

        dg = [[], []]
        for hh in range(N_HEADS):
            ls = slice(hh * LANE, (hh + 1) * LANE)
            dmh = dm_scr[:, ls]
            for side, (o_ref, z_ref, gt_ref, g_ref, do_ref) in enumerate(sides):
                zz, g = z_ref[hh], g_ref[:, ls]
                orr, sz, sgt, rr = keep[side, hh, 0], keep[side, hh, 1], keep[side, hh, 2], keep[side, hh, 3]
                n = orr * g
                silu = zz * sz
                du = dmh * sgt
                bufs[2 + side, hh] = dmh * (n * silu) * (sgt * (1.0 - sgt))
                bufs[side, hh] = du * n * (sz * (1.0 + zz * (1.0 - sz)))
                dn = du * silu
                dg[side].append(jnp.sum(dn * orr, axis=0, keepdims=True))
                gdn = dn * g
                do_ref[hh] = rr * (gdn - orr * jnp.mean(orr * gdn, axis=-1, keepdims=True))
        _acc_rows(vec_ref, i, [jnp.sum(dout * yn, axis=0, keepdims=True),
                               jnp.sum(dy * zv, axis=0, keepdims=True),
                               jnp.broadcast_to(loss, (1, D_MODEL)),
                               jnp.concatenate(dg[0], axis=1), jnp.concatenate(dg[1], axis=1)])
        rows = pl.ds(pl.multiple_of(i * tm, tm), tm)
        copies = [pltpu.make_async_copy(bufs.at[n], dpb_ref.at[pl.ds(cb, N_HEADS), rows], sems.at[n])
                  for n, cb in enumerate((CB_HZ, CB_RZ, CB_GA, CB_GB))]
        for cp in copies:
            cp.start()
        for cp in copies:
            cp.wait()

    row = pl.BlockSpec((tm, D_MODEL), lambda i: (i, 0))
    head = pl.BlockSpec((N_HEADS, tm, LANE), lambda i: (0, i, 0))
    full = pl.BlockSpec((D_MODEL, D_MODEL), lambda i: (0, 0))
    return pl.pallas_call(
        body, name="middle", grid=(t_len // tm,),
        in_specs=[row, row, head, head, _group_spec(tm, CB_HZ), _group_spec(tm, CB_RZ), _group_spec(tm, CB_GA),
                  _group_spec(tm, CB_GB), full, _vec_spec(), _vec_spec(), _vec_spec(), _vec_spec()],
        out_specs=[row, head, head, full, pl.BlockSpec((8, D_MODEL), lambda i: (0, 0)),
                   pl.BlockSpec(memory_space=pl.ANY)],
        out_shape=[jax.ShapeDtypeStruct((t_len, D_MODEL), F32),
                   jax.ShapeDtypeStruct((N_HEADS, t_len, LANE), F32),
                   jax.ShapeDtypeStruct((N_HEADS, t_len, LANE), F32),
                   jax.ShapeDtypeStruct((D_MODEL, D_MODEL), F32),
                   jax.ShapeDtypeStruct((8, D_MODEL), F32),
                   jax.ShapeDtypeStruct((N_CB, t_len, LANE), F32)],
        scratch_shapes=[pltpu.VMEM((tm, D_MODEL), _BF), pltpu.VMEM((tm, D_MODEL), F32),
                        pltpu.VMEM((2, N_HEADS, 4, tm, LANE), F32),
                        pltpu.VMEM((4, N_HEADS, tm, LANE), F32), pltpu.SemaphoreType.DMA((4,))],
        compiler_params=_params(("arbitrary",)),
    )(x, target, oa, ob, pb, pb, pb, pb, wout, gate, final_g, hg_g, ret_g)


def device_step(x, target, mod, lb, wt, wout, norm_g, hg_g, ret_g, final_g, start_exchange=None):
    t_len = x.shape[0]
    shift, scale, gate = mod[:, :D_MODEL], mod[:, D_MODEL:2 * D_MODEL], mod[:, 2 * D_MODEL:]
    scale1p = 1.0 + scale
    cos_t, sin_t = _rope_tables(t_len)
    h, h_t = adaln_forward(x, norm_g, scale1p, shift, t_len)
    pb = proj_forward(h, wt, t_len)
    oa, ssave = hgrn_forward(pb, lb, t_len)
    ob, rsave = retention_forward(pb, cos_t, sin_t, t_len)
    dy, doa, dob, dwout, vec_mid, dpb = middle(x, target, oa, ob, pb, wout, gate, final_g, hg_g, ret_g, t_len)
    dpb, dlb = hgrn_backward(pb, lb, doa, ssave, dpb, t_len)
    dpb = retention_backward(pb, cos_t, sin_t, dob, rsave, dpb, t_len)
    dwin = proj_backward_weight(h_t, dpb, t_len)
    token, pending = start_exchange(dwin, dwout) if start_exchange else (jnp.zeros((8, LANE), F32), None)
    dh = proj_backward_input(dpb, wt, token, t_len)
    grad_x, vec_ada = adaln_backward(dh, x, dy, norm_g, scale1p, t_len)
    return grad_x, dwin, dwout, vec_mid, vec_ada, dlb, pending


PACK_ROWS = 16
ROW_NORM_G, ROW_LB, ROW_HG_G, ROW_RET_G, ROW_FINAL_G, ROW_SHIFT, ROW_SCALE, ROW_GATE, ROW_LOSS = range(9)


def _mesh_pos():
    return lax.axis_index("x"), lax.axis_index("y"), lax.axis_index("c")


def _lin(pos):
    return 4 * pos[0] + 2 * pos[1] + pos[2]


def _xor_peer(pos, k):
    return tuple(1 - p if (k >> s) & 1 else p for p, s in zip(pos, (2, 1, 0)))


def _other_chips(pos):
    x, y, _ = pos
    return [(1 - x, y), (x, 1 - y), (1 - x, 1 - y)]


def _remote(src, dst, send_sem, recv_sem, to):
    return pltpu.make_async_remote_copy(src_ref=src, dst_ref=dst, send_sem=send_sem, recv_sem=recv_sem,
                                        device_id=to, device_id_type=MESH)


def pre_exchange(c, w_ada, b_ada, logits):
    def body(c_ref, wada_ref, bada_ref, logit_ref, mod_ref, scall_ref, lb_ref,
             cg_ref, modall_ref, parts_ref, send1, recv1, send2, recv2):
        pos = _mesh_pos()
        cv = c_ref[...]
        slot = lambda p: pl.ds(pl.multiple_of(8 * _lin(p), 8), 8)
        cg_ref[slot(pos), :] = jnp.broadcast_to(cv * _sigmoid(cv), (8, D_MODEL))
        lb_ref[...] = _sigmoid(logit_ref[0:1, :] - logit_ref[1:2, :])
        peers = [_xor_peer(pos, k) for k in range(1, N_DEV)]
        gather = [_remote(cg_ref.at[slot(pos)], cg_ref.at[slot(pos)], send1.at[n], recv1.at[n], p)
                  for n, p in enumerate(peers)]
        for cp in gather:
            cp.start()
        for n, p in enumerate(peers):
            _remote(cg_ref.at[slot(p)], cg_ref.at[slot(p)], send1.at[n], recv1.at[n], p).wait_recv()
        modall_ref[...] = _dot(cg_ref[...], wada_ref[...])
        scatter = [_remote(modall_ref.at[slot(p)], parts_ref.at[slot(pos)], send2.at[n], recv2.at[n], p)
                   for n, p in enumerate(peers)]
        for cp in scatter:
            cp.start()
        parts_ref[slot(pos), :] = modall_ref[slot(pos), :]
        for n, p in enumerate(peers):
            _remote(modall_ref.at[slot(p)], parts_ref.at[slot(p)], send2.at[n], recv2.at[n], p).wait_recv()
        for cp in gather + scatter:
            cp.wait_send()
        for j in range(N_DEV):
            cols = slice(j * SHARD_ADA, (j + 1) * SHARD_ADA)
            mod_ref[:, cols] = parts_ref[8 * j:8 * j + 1, :] + bada_ref[:, cols]
            scall_ref[j:j + 1, :] = cg_ref[8 * j:8 * j + 1, :]

    vmem = pl.BlockSpec(memory_space=pltpu.VMEM)
    return pl.pallas_call(
        body, name="pre_exchange",
        in_specs=[vmem] * 4, out_specs=[vmem] * 3,
        out_shape=[jax.ShapeDtypeStruct((1, 3 * D_MODEL), F32), jax.ShapeDtypeStruct((N_DEV, D_MODEL), F32),
                   jax.ShapeDtypeStruct((1, D_MODEL), F32)],
        scratch_shapes=[pltpu.VMEM((N_DEV * 8, D_MODEL), F32), pltpu.VMEM((N_DEV * 8, SHARD_ADA), F32),
                        pltpu.VMEM((N_DEV * 8, SHARD_ADA), F32)] + [pltpu.SemaphoreType.DMA((N_DEV - 1,))] * 4,
        compiler_params=pltpu.CompilerParams(vmem_limit_bytes=VMEM_LIMIT),
    )(c, w_ada, b_ada, logits)


def weight_gather(win_sh, wout_sh):
    def body(win_ref, wout_ref, wg_ref, woutg_ref, send, recv, local):
        pos = _mesh_pos()
        x, y, c = pos
        sibling = (x, y, 1 - c)
        chips = _other_chips(pos)
        first, passed, mine = [], [], []
        for a, (src, out) in enumerate(((win_ref, wg_ref), (wout_ref, woutg_ref))):
            def copy(k, block, to, src_ref=None, a=a, out=out):
                dst = out.at[_lin(block)]
                return _remote(dst if src_ref is None else src_ref, dst, send.at[7 * a + k], recv.at[7 * a + k], to)
            mine.append(pltpu.make_async_copy(src, out.at[_lin(pos)], local.at[a]))
            first.append(copy(0, pos, sibling, src))
            first += [copy(1 + j, pos, (*chip, c), src) for j, chip in enumerate(chips)]
            passed.append([copy(4 + j, (*chip, c), sibling) for j, chip in enumerate(chips)])
        for cp in mine + first:
            cp.start()
        for a, out in enumerate((wg_ref, woutg_ref)):
            for j, chip in enumerate(chips):
                dst = out.at[_lin((*chip, c))]
                _remote(dst, dst, send.at[7 * a + 1 + j], recv.at[7 * a + 1 + j], pos).wait_recv()
                passed[a][j].start()
        for a, out in enumerate((wg_ref, woutg_ref)):
            dst = out.at[_lin(sibling)]
            _remote(dst, dst, send.at[7 * a], recv.at[7 * a], pos).wait_recv()
            for j, chip in enumerate(chips):
                dst = out.at[_lin((*chip, 1 - c))]
                _remote(dst, dst, send.at[7 * a + 4 + j], recv.at[7 * a + 4 + j], pos).wait_recv()
        for cp in first + passed[0] + passed[1]:
            cp.wait_send()
        for cp in mine:
            cp.wait()

    any_spec = pl.BlockSpec(memory_space=pl.ANY)
    return pl.pallas_call(
        body, name="weight_gather",
        in_specs=[any_spec, any_spec], out_specs=[any_spec, any_spec],
        out_shape=[jax.ShapeDtypeStruct((N_DEV,) + win_sh.shape, win_sh.dtype),
                   jax.ShapeDtypeStruct((N_DEV,) + wout_sh.shape, wout_sh.dtype)],
        scratch_shapes=[pltpu.SemaphoreType.DMA((14,)), pltpu.SemaphoreType.DMA((14,)),
                        pltpu.SemaphoreType.DMA((2,))],
    )(win_sh, wout_sh)


def grad_pair_exchange(g_in, g_out):
    def body(gin_ref, gout_ref, ra_ref, rb_ref, send, recv):
        pos = _mesh_pos()
        x, y, c = pos
        sibling = (x, y, 1 - c)
        copies = []
        for a, (src, dst) in enumerate(((gin_ref, ra_ref), (gout_ref, rb_ref))):
            for q in range(4):
                copies.append(_remote(src.at[2 * q + (1 - c)], dst.at[q], send.at[4 * a + q], recv.at[4 * a + q],
                                      sibling))
        for cp in copies:
            cp.start()
        for cp in copies:
            cp.wait_recv()
        for cp in copies:
            cp.wait_send()

    any_spec = pl.BlockSpec(memory_space=pl.ANY)
    return pl.pallas_call(
        body, name="grad_pair_exchange",
        in_specs=[any_spec, any_spec], out_specs=[any_spec, any_spec],
        out_shape=[jax.ShapeDtypeStruct((4,) + g_in.shape[1:], F32), jax.ShapeDtypeStruct((4,) + g_out.shape[1:], F32)],
        scratch_shapes=[pltpu.SemaphoreType.DMA((8,)), pltpu.SemaphoreType.DMA((8,))],
    )(g_in, g_out)


def pair_sum(g_in, ra, g_out, rb, c_idx):
    tr = 256

    def body(c_ref, gin_ref, ra_ref, gout_ref, rb_ref, sb_ref, sbo_ref):
        del c_ref
        sb_ref[...] = (gin_ref[...] + ra_ref[...]).astype(sb_ref.dtype)
        sbo_ref[...] = gout_ref[...] + rb_ref[...]

    n_i = D_MODEL // tr
    return pl.pallas_call(
        body, name="pair_sum",
        grid_spec=pltpu.PrefetchScalarGridSpec(
            num_scalar_prefetch=1, grid=(4, n_i),
            in_specs=[pl.BlockSpec((None, tr, SHARD_IN), lambda q, i, c: (2 * q + c[0], i, 0)),
                      pl.BlockSpec((None, tr, SHARD_IN), lambda q, i, c: (q, i, 0)),
                      pl.BlockSpec((None, SHARD_OUT // n_i, D_MODEL), lambda q, i, c: (2 * q + c[0], i, 0)),
                      pl.BlockSpec((None, SHARD_OUT // n_i, D_MODEL), lambda q, i, c: (q, i, 0))],
            out_specs=[pl.BlockSpec((None, tr, SHARD_IN), lambda q, i, c: (q, i, 0)),
                       pl.BlockSpec((None, SHARD_OUT // n_i, D_MODEL), lambda q, i, c: (q, i, 0))]),
        out_shape=[jax.ShapeDtypeStruct(ra.shape, _BF), jax.ShapeDtypeStruct(rb.shape, F32)],
        compiler_params=_params(("arbitrary", "arbitrary")),
    )(c_idx, g_in, ra, g_out, rb)


_HBM = pl.BlockSpec(memory_space=pltpu.HBM)
_SEM = pl.BlockSpec(memory_space=pltpu.SEMAPHORE)
_N_CHIP_COPIES = 6


def _chip_copies(sb_ref, sbo_ref, rc_ref, rco_ref, send, recv):
    pos = _mesh_pos()
    copies = []
    for a, (src, dst) in enumerate(((sb_ref, rc_ref), (sbo_ref, rco_ref))):
        for j, chip in enumerate(_other_chips(pos)):
            copies.append(_remote(src.at[2 * chip[0] + chip[1]], dst.at[j], send.at[3 * a + j], recv.at[3 * a + j],
                                  (*chip, pos[2])))
    return copies


def grad_chip_start(sb, sbo):
    def body(sb_ref, sbo_ref, rc_ref, rco_ref, send, recv, sb_thru, sbo_thru, rc_thru, rco_thru, token):
        del sb_thru, sbo_thru, rc_thru, rco_thru
        for cp in _chip_copies(sb_ref, sbo_ref, rc_ref, rco_ref, send, recv):
            cp.start()
        token[...] = jnp.zeros_like(token)

    hbm = lambda a: pltpu.with_memory_space_constraint(a, pltpu.HBM)
    rc = lax.empty((3,) + sb.shape[1:], sb.dtype)
    rco = lax.empty((3,) + sbo.shape[1:], sbo.dtype)
    return pl.pallas_call(
        body, name="grad_chip_start",
        in_specs=[_HBM] * 4,
        out_specs=[_SEM, _SEM, _HBM, _HBM, _HBM, _HBM, pl.BlockSpec(memory_space=pltpu.VMEM)],
        out_shape=[pltpu.SemaphoreType.DMA((_N_CHIP_COPIES,)), pltpu.SemaphoreType.DMA((_N_CHIP_COPIES,)),
                   pltpu.HBM(sb.shape, sb.dtype), pltpu.HBM(sbo.shape, sbo.dtype),
                   pltpu.HBM(rc.shape, rc.dtype), pltpu.HBM(rco.shape, rco.dtype),
                   jax.ShapeDtypeStruct((8, LANE), F32)],
        input_output_aliases={0: 2, 1: 3, 2: 4, 3: 5},
        compiler_params=pltpu.CompilerParams(has_side_effects=pltpu.SideEffectType.DATAFLOW_SIDE_EFFECTING),
    )(hbm(sb), hbm(sbo), hbm(rc), hbm(rco))


def grad_chip_wait(send, recv, sb, sbo, rc, rco, after):
    def body(sb_ref, sbo_ref, rc_ref, rco_ref, send, recv, after_ref, sb_o, sbo_o, rc_o, rco_o):
        del after_ref, sb_o, sbo_o, rc_o, rco_o
        for cp in _chip_copies(sb_ref, sbo_ref, rc_ref, rco_ref, send, recv):
            cp.wait_send()
            cp.wait_recv()

    return pl.pallas_call(
        body, name="grad_chip_wait",
        in_specs=[_HBM] * 4 + [_SEM, _SEM, pl.BlockSpec(memory_space=pl.ANY)],
        out_specs=[_HBM] * 4,
        out_shape=[pltpu.HBM(sb.shape, sb.dtype), pltpu.HBM(sbo.shape, sbo.dtype),
                   pltpu.HBM(rc.shape, rc.dtype), pltpu.HBM(rco.shape, rco.dtype)],
        input_output_aliases={0: 0, 1: 1, 2: 2, 3: 3},
        compiler_params=pltpu.CompilerParams(has_side_effects=pltpu.SideEffectType.DATAFLOW_SIDE_EFFECTING),
    )(sb, sbo, rc, rco, send, recv, after)


def pack_gather(pack):
    def body(pack_ref, packs_ref, psend, precv):
        pos = _mesh_pos()
        me = _lin(pos)
        packs_ref[me] = pack_ref[...]
        peers = [_xor_peer(pos, k) for k in range(1, N_DEV)]
        gather = [_remote(packs_ref.at[me], packs_ref.at[me], psend.at[n], precv.at[n], p) for n, p in enumerate(peers)]
        for cp in gather:
            cp.start()
        for n, p in enumerate(peers):
            _remote(packs_ref.at[_lin(p)], packs_ref.at[_lin(p)], psend.at[n], precv.at[n], p).wait_recv()
        for cp in gather:
            cp.wait_send()

    vmem = pl.BlockSpec(memory_space=pltpu.VMEM)
    return pl.pallas_call(
        body, name="pack_gather", in_specs=[vmem], out_specs=vmem,
        out_shape=jax.ShapeDtypeStruct((N_DEV,) + pack.shape, F32),
        scratch_shapes=[pltpu.SemaphoreType.DMA((N_DEV - 1,)), pltpu.SemaphoreType.DMA((N_DEV - 1,))],
    )(pack)


def pack_rows(vec_mid, vec_ada, dlb):
    def body(mid_ref, ada_ref, dlb_ref, o_ref):
        mid = lambda r: mid_ref[r:r + 1, :]
        rows = [ada_ref[0:1, :], dlb_ref[...], mid(MID_HG_G), mid(MID_RET_G), mid(MID_FINAL_G),
                ada_ref[2:3, :], ada_ref[1:2, :], mid(MID_GATE), mid(MID_LOSS)]
        o_ref[...] = jnp.zeros_like(o_ref)
        for n, row in enumerate(rows):
            o_ref[n:n + 1, :] = row

    vmem = pl.BlockSpec(memory_space=pltpu.VMEM)
    return pl.pallas_call(body, name="pack_rows", in_specs=[vmem] * 3, out_specs=vmem,
                          out_shape=jax.ShapeDtypeStruct((PACK_ROWS, D_MODEL), F32))(vec_mid, vec_ada, dlb)


def _adamw(w, g, m, v):
    m = ADAM_B1 * m + (1.0 - ADAM_B1) * g
    v = ADAM_B2 * v + (1.0 - ADAM_B2) * (g * g)
    m_hat = m / (1.0 - ADAM_B1 ** ADAM_STEP)
    v_hat = v / (1.0 - ADAM_B2 ** ADAM_STEP)
    delta = -ADAM_LR * (m_hat / (jnp.sqrt(v_hat) + ADAM_EPS) + ADAM_WD * w)
    return delta, m, v


def adam_shard(chip_idx, own, parts, w, m, v, name):
    rows, cols = w.shape
    tr = min(rows, 128)

    def body(chip_ref, p0, p1, p2, p3, w_ref, m_ref, v_ref, g_ref, d_ref, nm_ref, nv_ref):
        del chip_ref
        g = ((p0[...].astype(F32) + p1[...].astype(F32)) + p2[...].astype(F32)) + p3[...].astype(F32)
        g_ref[...] = g
        d_ref[...], nm_ref[...], nv_ref[...] = _adamw(w_ref[...], g, m_ref[...], v_ref[...])

    part = lambda q: pl.BlockSpec((None, tr, cols), lambda i, chip, q=q: (q, i, 0))
    tile = pl.BlockSpec((tr, cols), lambda i, chip: (i, 0))
    return pl.pallas_call(
        body, name=name,
        grid_spec=pltpu.PrefetchScalarGridSpec(
            num_scalar_prefetch=1, grid=(rows // tr,),
            in_specs=[pl.BlockSpec((None, tr, cols), lambda i, chip: (chip[0], i, 0)), part(0), part(1), part(2),
                      tile, tile, tile],
            out_specs=[tile] * 4),
        out_shape=[jax.ShapeDtypeStruct(w.shape, F32)] * 4,
        compiler_params=_params(("arbitrary",)),
    )(chip_idx, own, parts, parts, parts, w, m, v)


def adam_ada(sc_t, dmod_all, me_idx, w, m, v):
    def body(me_ref, sc_ref, dm_ref, w_ref, m_ref, v_ref, g_ref, d_ref, nm_ref, nv_ref):
        del me_ref
        g = _dot_f32(sc_ref[...], dm_ref[...])
        g_ref[...] = g
        d_ref[...], nm_ref[...], nv_ref[...] = _adamw(w_ref[...], g, m_ref[...], v_ref[...])

    full = pl.BlockSpec(w.shape, lambda i, me: (0, 0))
    return pl.pallas_call(
        body, name="adam_ada",
        grid_spec=pltpu.PrefetchScalarGridSpec(
            num_scalar_prefetch=1, grid=(1,),
            in_specs=[pl.BlockSpec(sc_t.shape, lambda i, me: (0, 0)),
                      pl.BlockSpec((LANE, SHARD_ADA), lambda i, me: (0, me[0])), full, full, full],
            out_specs=[full] * 4),
        out_shape=[jax.ShapeDtypeStruct(w.shape, F32)] * 4,
        compiler_params=_params(("arbitrary",)),
    )(me_idx, sc_t, dmod_all, w, m, v)


def adam_vectors(packs, lb, params, ms, vs):
    n = len(params)

    def body(*refs):
        packs_ref, lb_ref = refs[0], refs[1]
        w_refs, m_refs, v_refs = refs[2:2 + n], refs[2 + n:2 + 2 * n], refs[2 + 2 * n:2 + 3 * n]
        loss_ref = refs[2 + 3 * n]
        outs = refs[3 + 3 * n:3 + 7 * n]
        tot_ref = refs[3 + 7 * n]
        tot = packs_ref[0]
        for d in range(1, N_DEV):
            tot = tot + packs_ref[d]
        tot_ref[...] = tot
        row = lambda r: tot_ref[r:r + 1, :]
        lbv = lb_ref[...]
        dl0 = row(ROW_LB) * lbv * (1.0 - lbv)
        grads = [[row(ROW_NORM_G)],
                 [jnp.concatenate([row(ROW_SHIFT), row(ROW_SCALE), row(ROW_GATE)], axis=1)],
                 [dl0, -dl0],
                 [row(ROW_HG_G)], [row(ROW_RET_G)], [row(ROW_FINAL_G)]]
        loss_ref[...] = tot_ref[ROW_LOSS:ROW_LOSS + 1, 0:LANE]
        for j, g_rows in enumerate(grads):
            for r, g in enumerate(g_rows):
                rs = slice(r, r + 1)
                d, nm, nv = _adamw(w_refs[j][rs, :], g, m_refs[j][rs, :], v_refs[j][rs, :])
                outs[4 * j][rs, :] = g
                outs[4 * j + 1][rs, :] = d
                outs[4 * j + 2][rs, :] = nm
                outs[4 * j + 3][rs, :] = nv

    vmem = pl.BlockSpec(memory_space=pltpu.VMEM)
    out_shape = [jax.ShapeDtypeStruct((1, LANE), F32)]
    for w in params:
        out_shape += [jax.ShapeDtypeStruct(w.shape, F32)] * 4
    return pl.pallas_call(
        body, name="adam_vectors", in_specs=[vmem] * (2 + 3 * n), out_specs=[vmem] * len(out_shape),
        out_shape=out_shape, scratch_shapes=[pltpu.VMEM((PACK_ROWS, D_MODEL), F32)],
    )(packs, lb, *params, *ms, *vs)


def kernel(x, c, norm_g, w_ada, b_ada, w_in, hg_lb_logits, hg_norm_g, ret_norm_g, w_out, final_g, loss_target, m_norm_g, m_w_ada, m_b_ada, m_w_in, m_hg_lb_logits, m_hg_norm_g, m_ret_norm_g, m_w_out, m_final_g, v_norm_g, v_w_ada, v_b_ada, v_w_in, v_hg_lb_logits, v_hg_norm_g, v_ret_norm_g, v_w_out, v_final_g):
    pos = _mesh_pos()
    me_idx = jnp.reshape(_lin(pos), (1,)).astype(jnp.int32)
    c_idx = jnp.reshape(pos[2], (1,)).astype(jnp.int32)
    vec = lambda a: a.reshape(1, D_MODEL)

    mod, scall, lb = pre_exchange(c, w_ada[0], b_ada, hg_lb_logits)
    wtg, woutg = weight_gather(w_in[0].T.astype(_BF), w_out[0].astype(_BF))
    chip_idx = jnp.reshape(2 * pos[0] + pos[1], (1,)).astype(jnp.int32)

    def start_exchange(dwin, dwout):
        dwout = dwout.reshape(N_DEV, SHARD_OUT, D_MODEL)
        ra, rb = grad_pair_exchange(dwin, dwout)
        sb, sbo = pair_sum(dwin, ra, dwout, rb, c_idx)
        send, recv, sb, sbo, rc, rco, token = grad_chip_start(sb, sbo)
        return token, (send, recv, sb, sbo, rc, rco)

    grad_x, _, _, vec_mid, vec_ada, dlb, pending = device_step(
        x[0], loss_target[0], mod, lb, wtg.reshape(D_IN, D_MODEL), woutg.reshape(D_MODEL, D_MODEL), norm_g,
        hg_norm_g, ret_norm_g, vec(final_g), start_exchange)
    sb, sbo, rc, rco = grad_chip_wait(*pending, grad_x)
    packs = pack_gather(pack_rows(vec_mid, vec_ada, dlb))

    g_in, d_in, nm_in, nv_in = adam_shard(chip_idx, sb, rc, w_in[0], m_w_in[0], v_w_in[0], "adam_w_in")
    g_out, d_out, nm_out, nv_out = adam_shard(chip_idx, sbo, rco, w_out[0], m_w_out[0], v_w_out[0], "adam_w_out")
    dmod_all = packs[:, ROW_SHIFT:ROW_GATE + 1, :].reshape(N_DEV, 3 * D_MODEL)
    dmod_all = jnp.pad(dmod_all, ((0, LANE - N_DEV), (0, 0)))
    sc_t = jnp.pad(scall.T, ((0, 0), (0, LANE - N_DEV)))
    g_ada, d_ada, nm_ada, nv_ada = adam_ada(sc_t, dmod_all, me_idx, w_ada[0], m_w_ada[0], v_w_ada[0])
    small = adam_vectors(
        packs, lb,
        (norm_g, b_ada, hg_lb_logits, hg_norm_g, ret_norm_g, vec(final_g)),
        (m_norm_g, m_b_ada, m_hg_lb_logits, m_hg_norm_g, m_ret_norm_g, vec(m_final_g)),
        (v_norm_g, v_b_ada, v_hg_lb_logits, v_hg_norm_g, v_ret_norm_g, vec(v_final_g)))
    loss = small[0][0, 0]
    (g_ng, d_ng, nm_ng, nv_ng), (g_b, d_b, nm_b, nv_b), (g_lb, d_lb, nm_lb, nv_lb), (g_hg, d_hg, nm_hg, nv_hg), \
        (g_rg, d_rg, nm_rg, nv_rg), (g_fg, d_fg, nm_fg, nv_fg) = [small[1 + 4 * j:5 + 4 * j] for j in range(6)]
    flat = lambda a: a.reshape(D_MODEL)

    def group(ng, ada, b, win, lbl, hg, rg, wo, fg):
        return (ng, ada[None], b, win[None], lbl, hg, rg, wo[None], flat(fg))

    return (loss, grad_x[None],
            *group(g_ng, g_ada, g_b, g_in, g_lb, g_hg, g_rg, g_out, g_fg),
            *group(d_ng, d_ada, d_b, d_in, d_lb, d_hg, d_rg, d_out, d_fg),
            *group(nm_ng, nm_ada, nm_b, nm_in, nm_lb, nm_hg, nm_rg, nm_out, nm_fg),
            *group(nv_ng, nv_ada, nv_b, nv_in, nv_lb, nv_hg, nv_rg, nv_out, nv_fg))
```

```python
import functools

import numpy as np
import jax
import jax.numpy as jnp
from jax import lax
from jax.experimental import pallas as pl
from jax.experimental.pallas import tpu as pltpu

F32 = jnp.float32
_BF = jnp.bfloat16

D_MODEL = 1024
N_HEADS = 8
LANE = 128
RET_DK = 64
D_IN = 9216
N_DEV = 8
SHARD_IN = D_IN // N_DEV
SHARD_ADA = 3 * D_MODEL // N_DEV
SHARD_OUT = D_MODEL // N_DEV
N_CB = D_IN // LANE
CB_PER_SHARD = SHARD_IN // LANE
CHUNK = 128
N_LEVELS = 7
EPS = 1e-6
ROPE_BASE = 10000.0
CB_HQ, CB_HF, CB_HI, CB_HZ, CB_RQ, CB_RK, CB_RV, CB_RZ, CB_GA, CB_GB = 0, 8, 16, 24, 32, 36, 40, 48, 56, 64
VMEM_LIMIT = 56 * 1024 * 1024

ADAM_LR, ADAM_B1, ADAM_B2, ADAM_EPS, ADAM_WD, ADAM_STEP = 0.001, 0.9, 0.999, 1e-08, 0.01, 10

_NN = (((1,), (0,)), ((), ()))
_NT = (((1,), (1,)), ((), ()))
_TN = (((0,), (0,)), ((), ()))
MESH = pl.DeviceIdType.MESH


def _dot(a, b, dims=_NN):
    return lax.dot_general(a.astype(_BF), b.astype(_BF), dims, preferred_element_type=F32)


def _split2(a):
    hi = a.astype(_BF)
    lo = (a - hi.astype(F32)).astype(_BF)
    return jnp.concatenate([hi, lo], axis=1)


def _dot_sel(sel, a):
    n = a.shape[1]
    r = lax.dot_general(sel.astype(_BF), _split2(a), _NN, preferred_element_type=F32)
    return r[:, :n] + r[:, n:]


def _dot_f32(a, b):
    def pieces(v):
        p1 = v.astype(_BF)
        r1 = v - p1.astype(F32)
        p2 = r1.astype(_BF)
        p3 = (r1 - p2.astype(F32)).astype(_BF)
        return p1, p2, p3
    a1, a2, a3 = pieces(a)
    b1, b2, b3 = pieces(b)
    d = lambda u, v: lax.dot_general(u, v, _NN, preferred_element_type=F32)
    return ((d(a1, b3) + d(a2, b2) + d(a3, b1)) + (d(a1, b2) + d(a2, b1))) + d(a1, b1)


def _sigmoid(v):
    return 1.0 / (1.0 + jnp.exp(-v))


def _params(sem=None):
    return pltpu.CompilerParams(dimension_semantics=sem, vmem_limit_bytes=VMEM_LIMIT)


def _hgrn_consts(transposed=False):
    c, nl = CHUNK, N_LEVELS
    t = np.arange(c)[:, None]
    j = np.arange(c)[None, :]
    sel = [j <= t]
    masks = [j == t]
    for l in range(1, nl + 1):
        m = ((t >> l) << l) + (1 << (l - 1)) - 1
        sec = t > m
        sel.append(np.where(sec, (j > m) & (j <= t), (j > t) & (j <= m)))
        same = (t >> l) == (j >> l)
        masks.append(same & sec & (j <= m))
    sel.append(j > t)
    sel = np.concatenate(sel, 0).astype(np.float32)
    masks = np.stack(masks).astype(np.float32)
    if transposed:
        return jnp.asarray(sel.T, _BF), jnp.asarray(np.swapaxes(masks, 1, 2), F32)
    return jnp.asarray(sel[:c], _BF), jnp.asarray(masks, F32)


def _level_exponents(b, logf, b_scr):
    c = CHUNK
    b_scr[...] = b
    row = lax.broadcasted_iota(jnp.int32, (c, LANE), 0)
    nxt = pltpu.roll(logf, c - 1, 0)
    prv = pltpu.roll(logf, 1, 0)
    r4 = row & 3
    out = [jnp.where((row & 1) == 1, logf, 0.0),
           jnp.where(r4 == 0, nxt, jnp.where(r4 == 1, 0.0, jnp.where(r4 == 2, logf, logf + prv)))]
    for l in range(3, N_LEVELS + 1):
        size, half = 1 << l, 1 << (l - 1)
        ref = jnp.concatenate([jnp.broadcast_to(b_scr[i * size + half - 1:i * size + half, :], (size, LANE))
                               for i in range(c // size)], axis=0)
        out.append(jnp.where((row & half) != 0, b - ref, ref - b))
    return out


def _hgrn_chunk(hq, hf, hi, lbv, tri_ref, b_scr):
    sq = _sigmoid(hq)
    q = hq * sq
    sg = _sigmoid(hf)
    omlb = 1.0 - lbv
    f = lbv + omlb * sg
    k = 1.0 - f
    logf = jnp.log(f)
    b = _dot_sel(tri_ref[...], logf)
    bc = jnp.sum(logf, axis=0, keepdims=True)
    lev = [None] + [jnp.exp(e) for e in _level_exponents(b, logf, b_scr)]
    return dict(sq=sq, q=q, sg=sg, omlb=omlb, f=f, k=k, v=hi, eb=jnp.exp(b), erem=jnp.exp(bc - b), ebc=jnp.exp(bc),
                lev=lev)


def _blockdiag(a, b):
    z = jnp.zeros_like(a)
    return jnp.concatenate([jnp.concatenate([a, z], axis=1), jnp.concatenate([z, b], axis=1)], axis=0)


def _level_operands(a):
    q, k = a["q"], a["k"]
    ql = [q.astype(_BF)] + [(q * a["lev"][l]).astype(_BF) for l in range(1, N_LEVELS + 1)]
    kl = [k.astype(_BF)] + [(k * a["lev"][l]).astype(_BF) for l in range(1, N_LEVELS + 1)]
    pairs = range(0, N_LEVELS + 1, 2)
    return ([jnp.concatenate([ql[l], ql[l + 1]], axis=1) for l in pairs], [_blockdiag(kl[l], kl[l + 1]) for l in pairs],
            ql, kl)


def _hgrn_scores(a, lvl_ref, q_pairs, k_diags):
    acc = None
    for n, (qp, kd) in enumerate(zip(q_pairs, k_diags)):
        both = lax.dot_general(qp, kd, _NT, preferred_element_type=F32)
        part = lvl_ref[2 * n] * both[:, :CHUNK] + lvl_ref[2 * n + 1] * both[:, CHUNK:]
        acc = part if acc is None else acc + part
    return acc


SCAN_UNROLL = 2


def _time_block(t_len):
    return min(t_len, 1024)


def hgrn_forward(pb, lb, t_len):
    nc = t_len // CHUNK
    tb = _time_block(t_len)
    ncb = tb // CHUNK
    sel, lvl = _hgrn_consts()

    def body(hq_ref, hf_ref, hi_ref, lb_ref, sel_ref, lvl_ref, o_ref, ssave_ref, st_ref, b_scr):
        @pl.when(pl.program_id(1) == 0)
        def _():
            st_ref[...] = jnp.zeros_like(st_ref)

        def chunk(ci, carry):
            r = pl.ds(pl.multiple_of(ci * CHUNK, CHUNK), CHUNK)
            for hd in range(2):
                lbv = lb_ref[:, hd * LANE:(hd + 1) * LANE]
                a = _hgrn_chunk(hq_ref[hd, r, :], hf_ref[hd, r, :], hi_ref[hd, r, :], lbv, sel_ref, b_scr.at[hd])
                q_pairs, k_diags, _, _ = _level_operands(a)
                st = st_ref[hd]
                ssave_ref[hd, ci] = st
                scores = _hgrn_scores(a, lvl_ref, q_pairs, k_diags)
                o_ref[hd, r, :] = _dot(a["q"] * a["eb"], st, _NT) + _dot(scores, a["v"])
                st_ref[hd] = st * a["ebc"] + _dot(a["v"], a["k"] * a["erem"], _TN)
            return carry

        lax.fori_loop(0, ncb, chunk, 0)

    pair = lambda base: pl.BlockSpec((2, tb, LANE), lambda p, t, base=base: (base // 2 + p, t, 0))
    return pl.pallas_call(
        body, name="hgrn_fwd", grid=(N_HEADS // 2, t_len // tb),
        in_specs=[pair(CB_HQ), pair(CB_HF), pair(CB_HI),
                  pl.BlockSpec((1, 2 * LANE), lambda p, t: (0, p)),
                  pl.BlockSpec(sel.shape, lambda p, t: (0, 0)),
                  pl.BlockSpec(lvl.shape, lambda p, t: (0, 0, 0))],
        out_specs=[pl.BlockSpec((2, tb, LANE), lambda p, t: (p, t, 0)),
                   pl.BlockSpec((2, ncb, LANE, LANE), lambda p, t: (p, t, 0, 0))],
        out_shape=[jax.ShapeDtypeStruct((N_HEADS, t_len, LANE), F32),
                   jax.ShapeDtypeStruct((N_HEADS, nc, LANE, LANE), F32)],
        scratch_shapes=[pltpu.VMEM((2, LANE, LANE), F32), pltpu.VMEM((2, CHUNK, LANE), F32)],
        compiler_params=_params(("arbitrary", "arbitrary")),
    )(pb, pb, pb, lb, sel, lvl)


def hgrn_backward(pb, lb, do, ssave, dpb, t_len):
    tb = _time_block(t_len)
    ncb, ntb = tb // CHUNK, t_len // tb
    sel, lvl = _hgrn_consts()
    sel_t, lvl_t = _hgrn_consts(transposed=True)

    def body(hq_ref, hf_ref, hi_ref, lb_ref, do_ref, ssave_ref, sel_ref, lvl_ref, selt_ref, lvlt_ref, dpb_in,
             dpb_ref, dlb_ref, dq_buf, df_buf, di_buf, dst_ref, b_scr, sems):
        del dpb_in
        p, t = pl.program_id(0), pl.program_id(1)

        @pl.when(t == 0)
        def _():
            dst_ref[...] = jnp.zeros_like(dst_ref)
            dlb_ref[...] = jnp.zeros_like(dlb_ref)

        def chunk(i, carry):
            ci = ncb - 1 - i
            r = pl.ds(pl.multiple_of(ci * CHUNK, CHUNK), CHUNK)
            for hd in range(2):
                head_chunk(hd, ci, r)
            return carry

        def head_chunk(hd, ci, r):
            lbv = lb_ref[:, hd * LANE:(hd + 1) * LANE]
            hq = hq_ref[hd, r, :]
            a = _hgrn_chunk(hq, hf_ref[hd, r, :], hi_ref[hd, r, :], lbv, sel_ref, b_scr.at[hd])
            q_pairs, k_diags, ql, kl = _level_operands(a)
            q, k, v = a["q"], a["k"], a["v"]
            g = do_ref[hd, r, :]
            st0 = ssave_ref[hd, ci]
            dst = dst_ref[hd]
            scores = _hgrn_scores(a, lvl_ref, q_pairs, k_diags)
            da = _dot(g, v, _NT)
            da_t = _dot(v, g, _NT)
            kb = k * a["erem"]
            qb = q * a["eb"]
            dv = _dot(scores, g, _TN) + _dot(kb, dst, _NT)
            dq_inter = _dot(g, st0) * a["eb"]
            dk_state = _dot(v, dst) * a["erem"]
            dq, dk = dq_inter, dk_state
            de = [q * dq_inter]
            for n in range(len(k_diags)):
                l0, l1 = 2 * n, 2 * n + 1
                da_pair = jnp.concatenate([(lvl_ref[l0] * da).astype(_BF), (lvl_ref[l1] * da).astype(_BF)], axis=1)
                dat_pair = jnp.concatenate([(lvlt_ref[l0] * da_t).astype(_BF), (lvlt_ref[l1] * da_t).astype(_BF)], axis=1)
                dq_both = lax.dot_general(da_pair, k_diags[n], _NN, preferred_element_type=F32)
                dk_both = lax.dot_general(dat_pair, _blockdiag(ql[l0], ql[l1]), _NN, preferred_element_type=F32)
                for l, cols in ((l0, slice(0, LANE)), (l1, slice(LANE, 2 * LANE))):
                    dql, dkl = dq_both[:, cols], dk_both[:, cols]
                    if l > 0:
                        e = a["lev"][l]
                        dql, dkl = dql * e, dkl * e
                        de.append(q * dql + k * dkl)
                    dq = dq + dql
                    dk = dk + dkl
            de.append(k * dk_state)
            dst_ref[hd] = dst * a["ebc"] + _dot(g, qb, _TN)
            dbc = jnp.sum(dst * st0, axis=0, keepdims=True) * a["ebc"]
            de2 = lax.dot_general(selt_ref[...], _split2(jnp.concatenate(de, axis=0)), _NN,
                                  preferred_element_type=F32)
            dlogf = de2[:, :LANE] + de2[:, LANE:] + dbc
            sq, sg = a["sq"], a["sg"]
            df = dlogf / a["f"] - dk
            dq_buf[hd, r, :] = dq * (sq * (1.0 + hq * (1.0 - sq)))
            df_buf[hd, r, :] = df * a["omlb"] * sg * (1.0 - sg)
            di_buf[hd, r, :] = dv
            cols = slice(hd * LANE, (hd + 1) * LANE)
            dlb_ref[:, cols] = dlb_ref[:, cols] + jnp.sum(df * (1.0 - sg), axis=0, keepdims=True)

        lax.fori_loop(0, ncb, chunk, 0)
        rows = pl.ds(pl.multiple_of((ntb - 1 - t) * tb, tb), tb)
        copies = [pltpu.make_async_copy(buf, dpb_ref.at[pl.ds(base + 2 * p, 2), rows], sems.at[n])
                  for n, (buf, base) in enumerate(((dq_buf, CB_HQ), (df_buf, CB_HF), (di_buf, CB_HI)))]
        for cp in copies:
            cp.start()
        for cp in copies:
            cp.wait()

    pair = lambda base: pl.BlockSpec((2, tb, LANE), lambda p, t, base=base: (base // 2 + p, ntb - 1 - t, 0))
    any_spec = pl.BlockSpec(memory_space=pl.ANY)
    return pl.pallas_call(
        body, name="hgrn_bwd", grid=(N_HEADS // 2, ntb),
        in_specs=[pair(CB_HQ), pair(CB_HF), pair(CB_HI),
                  pl.BlockSpec((1, 2 * LANE), lambda p, t: (0, p)),
                  pair(0),
                  pl.BlockSpec((2, ncb, LANE, LANE), lambda p, t: (p, ntb - 1 - t, 0, 0)),
                  pl.BlockSpec(sel.shape, lambda p, t: (0, 0)),
                  pl.BlockSpec(lvl.shape, lambda p, t: (0, 0, 0)),
                  pl.BlockSpec(sel_t.shape, lambda p, t: (0, 0)),
                  pl.BlockSpec(lvl_t.shape, lambda p, t: (0, 0, 0)),
                  any_spec],
        out_specs=[any_spec, pl.BlockSpec((1, 2 * LANE), lambda p, t: (0, p))],
        out_shape=[jax.ShapeDtypeStruct(dpb.shape, F32), jax.ShapeDtypeStruct((1, D_MODEL), F32)],
        scratch_shapes=[pltpu.VMEM((2, tb, LANE), F32)] * 3 + [
            pltpu.VMEM((2, LANE, LANE), F32), pltpu.VMEM((2, CHUNK, LANE), F32), pltpu.SemaphoreType.DMA((3,))],
        input_output_aliases={10: 0},
        compiler_params=_params(("arbitrary", "arbitrary")),
    )(pb, pb, pb, lb, do, ssave, sel, lvl, sel_t, lvl_t, dpb)


def _rope_tables(t_len):
    half = RET_DK // 2
    inv_freq = 1.0 / (ROPE_BASE ** jnp.linspace(0.0, 1.0, half, dtype=F32))
    ang = jnp.arange(t_len, dtype=jnp.int32).astype(F32)[:, None] * inv_freq[None, :]
    cos, sin = jnp.cos(ang), jnp.sin(ang)
    cos_t = jnp.concatenate([cos, cos, cos, cos], axis=1)
    sin_t = jnp.concatenate([-sin, sin, -sin, sin], axis=1)
    return cos_t, sin_t


def _swap_halves(v):
    half = RET_DK // 2
    lane = lax.broadcasted_iota(jnp.int32, v.shape, 1)
    first = (lane & (RET_DK - 1)) < half
    return jnp.where(first, pltpu.roll(v, LANE - half, 1), pltpu.roll(v, half, 1))


def _ret_head_consts(hidx):
    c = CHUNK
    hf = jnp.full((1, LANE), hidx, jnp.int32).astype(F32)
    lg = jnp.log(1.0 - jnp.exp(-(5.0 + hf) * np.float32(np.log(2.0))))
    row = lax.broadcasted_iota(jnp.int32, (c, c), 0)
    col = lax.broadcasted_iota(jnp.int32, (c, c), 1)
    rel = (row - col).astype(F32)
    dm = jnp.where(rel >= 0, jnp.exp(lg[:, :1] * jnp.maximum(rel, 0.0)), 0.0)
    dm_t = jnp.where(rel <= 0, jnp.exp(lg[:, :1] * jnp.maximum(-rel, 0.0)), 0.0)
    idx = lax.broadcasted_iota(jnp.int32, (c, LANE), 0).astype(F32)
    zeta = jnp.exp(lg * (c - 1.0 - idx))
    xi = jnp.exp(lg * (idx + 1.0))
    cdec = jnp.exp(lg * float(c))
    return dm, zeta, xi, cdec, dm_t


def _lane_mask(which):
    lane = lax.broadcasted_iota(jnp.int32, (1, LANE), 1)
    return ((lane // RET_DK) == which).astype(F32)


def retention_forward(pb, cos_t, sin_t, t_len):
    nc = t_len // CHUNK

    tb = _time_block(t_len)
    ncb = tb // CHUNK

    def body(rq_ref, rk_ref, rv_ref, cos_ref, sin_ref, o_ref, rsave_ref, st_ref):
        p = pl.program_id(0)

        @pl.when(pl.program_id(1) == 0)
        def _():
            st_ref[...] = jnp.zeros_like(st_ref)

        consts = [_ret_head_consts(2 * p + hd) for hd in range(2)]

        def chunk(ci, carry):
            r = pl.ds(pl.multiple_of(ci * CHUNK, CHUNK), CHUNK)
            cs, sn = cos_ref[r, :], sin_ref[r, :]
            q = rq_ref[r, :]
            k = rk_ref[r, :]
            q = q * cs + _swap_halves(q) * sn
            k = (k * cs + _swap_halves(k) * sn) * RET_DK ** -0.5
            for hd in range(2):
                dm, zeta, xi, cdec, _ = consts[hd]
                lm = _lane_mask(hd)
                qh, kh = q * lm, k * lm
                v = rv_ref[hd, r, :]
                st = st_ref[hd]
                rsave_ref[hd, ci] = st
                scores = _dot(qh, kh, _NT) * dm
                o_ref[hd, r, :] = _dot(scores, v) + _dot(qh * xi, st, _NT)
                st_ref[hd] = st * cdec + _dot(v, kh * zeta, _TN)
            return carry

        lax.fori_loop(0, ncb, chunk, 0, unroll=SCAN_UNROLL)

    return pl.pallas_call(
        body, name="ret_fwd", grid=(N_HEADS // 2, t_len // tb),
        in_specs=[pl.BlockSpec((None, tb, LANE), lambda p, t: (CB_RQ + p, t, 0)),
                  pl.BlockSpec((None, tb, LANE), lambda p, t: (CB_RK + p, t, 0)),
                  pl.BlockSpec((2, tb, LANE), lambda p, t: (CB_RV // 2 + p, t, 0)),
                  pl.BlockSpec((tb, LANE), lambda p, t: (t, 0)),
                  pl.BlockSpec((tb, LANE), lambda p, t: (t, 0))],
        out_specs=[pl.BlockSpec((2, tb, LANE), lambda p, t: (p, t, 0)),
                   pl.BlockSpec((2, ncb, LANE, LANE), lambda p, t: (p, t, 0, 0))],
        out_shape=[jax.ShapeDtypeStruct((N_HEADS, t_len, LANE), F32),
                   jax.ShapeDtypeStruct((N_HEADS, nc, LANE, LANE), F32)],
        scratch_shapes=[pltpu.VMEM((2, LANE, LANE), F32)],
        compiler_params=_params(("arbitrary", "arbitrary")),
    )(pb, pb, pb, cos_t, sin_t)


def retention_backward(pb, cos_t, sin_t, do, rsave, dpb, t_len):
    tb = _time_block(t_len)
    ncb, ntb = tb // CHUNK, t_len // tb

    def body(rq_ref, rk_ref, rv_ref, cos_ref, sin_ref, do_ref, rsave_ref, dpb_in,
             dpb_ref, dq_buf, dk_buf, dv_buf, dst_ref, sems):
        del dpb_in
        p, t = pl.program_id(0), pl.program_id(1)

        @pl.when(t == 0)
        def _():
            dst_ref[...] = jnp.zeros_like(dst_ref)

        consts = [_ret_head_consts(2 * p + hd) for hd in range(2)]

        def chunk(i, carry):
            ci = ncb - 1 - i
            r = pl.ds(pl.multiple_of(ci * CHUNK, CHUNK), CHUNK)
            cs, sn = cos_ref[r, :], sin_ref[r, :]
            q = rq_ref[r, :]
            k = rk_ref[r, :]
            q = q * cs + _swap_halves(q) * sn
            k = (k * cs + _swap_halves(k) * sn) * RET_DK ** -0.5
            dq, dk = None, None
            for hd in range(2):
                dm, zeta, xi, cdec, dm_t = consts[hd]
                lm = _lane_mask(hd)
                qh, kh = q * lm, k * lm
                v = rv_ref[hd, r, :]
                g = do_ref[hd, r, :]
                st0 = rsave_ref[hd, ci]
                dst = dst_ref[hd]
                scores_t = _dot(kh, qh, _NT) * dm_t
                dsc = _dot(g, v, _NT) * dm
                dsc_t = _dot(v, g, _NT) * dm_t
                dqh = _dot(dsc, kh) + _dot(g, st0) * xi
                dkh = _dot(dsc_t, qh) + _dot(v, dst) * zeta
                dv_buf[hd, r, :] = _dot(scores_t, g) + _dot(kh * zeta, dst, _NT)
                dst_ref[hd] = dst * cdec + _dot(g, qh * xi, _TN)
                dq = dqh if dq is None else dq + dqh
                dk = dkh if dk is None else dk + dkh
            dk = dk * (RET_DK ** -0.5)
            dq_buf[r, :] = dq * cs - _swap_halves(dq) * sn
            dk_buf[r, :] = dk * cs - _swap_halves(dk) * sn
            return carry

        lax.fori_loop(0, ncb, chunk, 0, unroll=SCAN_UNROLL)
        rows = pl.ds(pl.multiple_of((ntb - 1 - t) * tb, tb), tb)
        copies = [pltpu.make_async_copy(dq_buf, dpb_ref.at[CB_RQ + p, rows], sems.at[0]),
                  pltpu.make_async_copy(dk_buf, dpb_ref.at[CB_RK + p, rows], sems.at[1]),
                  pltpu.make_async_copy(dv_buf, dpb_ref.at[pl.ds(CB_RV + 2 * p, 2), rows], sems.at[2])]
        for cp in copies:
            cp.start()
        for cp in copies:
            cp.wait()

    any_spec = pl.BlockSpec(memory_space=pl.ANY)
    return pl.pallas_call(
        body, name="ret_bwd", grid=(N_HEADS // 2, ntb),
        in_specs=[pl.BlockSpec((None, tb, LANE), lambda p, t: (CB_RQ + p, ntb - 1 - t, 0)),
                  pl.BlockSpec((None, tb, LANE), lambda p, t: (CB_RK + p, ntb - 1 - t, 0)),
                  pl.BlockSpec((2, tb, LANE), lambda p, t: (CB_RV // 2 + p, ntb - 1 - t, 0)),
                  pl.BlockSpec((tb, LANE), lambda p, t: (ntb - 1 - t, 0)),
                  pl.BlockSpec((tb, LANE), lambda p, t: (ntb - 1 - t, 0)),
                  pl.BlockSpec((2, tb, LANE), lambda p, t: (p, ntb - 1 - t, 0)),
                  pl.BlockSpec((2, ncb, LANE, LANE), lambda p, t: (p, ntb - 1 - t, 0, 0)),
                  any_spec],
        out_specs=any_spec,
        out_shape=jax.ShapeDtypeStruct(dpb.shape, F32),
        scratch_shapes=[pltpu.VMEM((tb, LANE), F32), pltpu.VMEM((tb, LANE), F32),
                        pltpu.VMEM((2, tb, LANE), F32), pltpu.VMEM((2, LANE, LANE), F32),
                        pltpu.SemaphoreType.DMA((3,))],
        input_output_aliases={7: 0},
        compiler_params=_params(("arbitrary", "arbitrary")),
    )(pb, pb, pb, cos_t, sin_t, do, rsave, dpb)


def _row_tile(t_len, want):
    return min(want, t_len)


PAIR_CB = 2 * CB_PER_SHARD


def proj_forward(h, wt, t_len):
    tm = _row_tile(t_len, 512)

    def body(h_ref, w_ref, o_ref):
        acc = _dot(h_ref[...], w_ref[...], _NT)
        for jj in range(PAIR_CB):
            o_ref[jj] = acc[:, jj * LANE:(jj + 1) * LANE]

    return pl.pallas_call(
        body, name="proj_fwd", grid=(N_DEV // 2, t_len // tm),
        in_specs=[pl.BlockSpec((tm, D_MODEL), lambda j, i: (i, 0)),
                  pl.BlockSpec((PAIR_CB * LANE, D_MODEL), lambda j, i: (j, 0))],
        out_specs=pl.BlockSpec((PAIR_CB, tm, LANE), lambda j, i: (j, i, 0)),
        out_shape=jax.ShapeDtypeStruct((N_CB, t_len, LANE), F32),
        compiler_params=_params(("arbitrary", "arbitrary")),
    )(h, wt)


def proj_backward_input(dpb, wt, token, t_len):
    tm = _row_tile(t_len, 256)

    def body(a_ref, wt_hbm, token_ref, o_ref, w_ref, sem):
        del token_ref

        @pl.when(pl.program_id(0) == 0)
        def _():
            cp = pltpu.make_async_copy(wt_hbm, w_ref, sem)
            cp.start()
            cp.wait()

        a = jnp.concatenate([a_ref[jj].astype(_BF) for jj in range(N_CB)], axis=1)
        o_ref[...] = _dot(a, w_ref[...])

    return pl.pallas_call(
        body, name="proj_bwd_input", grid=(t_len // tm,),
        in_specs=[pl.BlockSpec((N_CB, tm, LANE), lambda i: (0, i, 0)),
                  pl.BlockSpec(memory_space=pl.ANY),
                  pl.BlockSpec(token.shape, lambda i: (0, 0))],
        out_specs=pl.BlockSpec((tm, D_MODEL), lambda i: (i, 0)),
        out_shape=jax.ShapeDtypeStruct((t_len, D_MODEL), F32),
        scratch_shapes=[pltpu.VMEM(wt.shape, wt.dtype), pltpu.SemaphoreType.DMA],
        compiler_params=_params(("arbitrary",)),
    )(dpb, wt, token)


def proj_backward_weight(h_t, dpb, t_len):
    tk = _row_tile(t_len, 1024)

    def body(h_ref, b_ref, o_ref):
        k = pl.program_id(1)
        b = jnp.concatenate([b_ref[jj].astype(_BF) for jj in range(PAIR_CB)], axis=1)
        part = _dot(h_ref[...], b)

        @pl.when(k == 0)
        def _():
            for s in range(2):
                o_ref[s] = part[:, s * SHARD_IN:(s + 1) * SHARD_IN]

        @pl.when(k > 0)
        def _():
            for s in range(2):
                o_ref[s] = o_ref[s] + part[:, s * SHARD_IN:(s + 1) * SHARD_IN]

    return pl.pallas_call(
        body, name="proj_bwd_weight", grid=(N_DEV // 2, t_len // tk),
        in_specs=[pl.BlockSpec((D_MODEL, tk), lambda j, k: (0, k)),
                  pl.BlockSpec((PAIR_CB, tk, LANE), lambda j, k: (j, k, 0))],
        out_specs=pl.BlockSpec((2, D_MODEL, SHARD_IN), lambda j, k: (j, 0, 0)),
        out_shape=jax.ShapeDtypeStruct((N_DEV, D_MODEL, SHARD_IN), F32),
        compiler_params=_params(("arbitrary", "arbitrary")),
    )(h_t, dpb)


def _vec_spec():
    return pl.BlockSpec((1, D_MODEL), lambda i: (0, 0))


def _acc_rows(ref, i, rows):
    @pl.when(i == 0)
    def _():
        ref[...] = jnp.zeros_like(ref)

    for n, row in enumerate(rows):
        ref[n:n + 1, :] = ref[n:n + 1, :] + row


def adaln_forward(x, norm_g, scale1p, shift, t_len):
    tm = _row_tile(t_len, 512)

    def body(x_ref, g_ref, sc_ref, sh_ref, h_ref, ht_ref):
        xv = x_ref[...]
        r = lax.rsqrt(jnp.mean(xv * xv, axis=-1, keepdims=True) + EPS)
        h = xv * r * g_ref[...] * sc_ref[...] + sh_ref[...]
        h_ref[...] = h.astype(h_ref.dtype)
        ht_ref[...] = h.T.astype(ht_ref.dtype)

    return pl.pallas_call(
        body, name="adaln_fwd", grid=(t_len // tm,),
        in_specs=[pl.BlockSpec((tm, D_MODEL), lambda i: (i, 0)), _vec_spec(), _vec_spec(), _vec_spec()],
        out_specs=[pl.BlockSpec((tm, D_MODEL), lambda i: (i, 0)), pl.BlockSpec((D_MODEL, tm), lambda i: (0, i))],
        out_shape=[jax.ShapeDtypeStruct((t_len, D_MODEL), _BF), jax.ShapeDtypeStruct((D_MODEL, t_len), _BF)],
        compiler_params=_params(("arbitrary",)),
    )(x, norm_g, scale1p, shift)


def adaln_backward(dh, x, dy, norm_g, scale1p, t_len):
    tm = _row_tile(t_len, 512)

    def body(dh_ref, x_ref, dy_ref, g_ref, sc_ref, gx_ref, vec_ref):
        i = pl.program_id(0)
        xv, dhv = x_ref[...], dh_ref[...]
        g, sc = g_ref[...], sc_ref[...]
        r = lax.rsqrt(jnp.mean(xv * xv, axis=-1, keepdims=True) + EPS)
        xn = xv * r
        dxn = dhv * (g * sc)
        gx_ref[...] = dy_ref[...] + r * dxn - xn * (r * r) * jnp.mean(xv * dxn, axis=-1, keepdims=True)
        t = dhv * xn
        _acc_rows(vec_ref, i, [jnp.sum(t * sc, axis=0, keepdims=True),
                               jnp.sum(t * g, axis=0, keepdims=True),
                               jnp.sum(dhv, axis=0, keepdims=True)])

    row = pl.BlockSpec((tm, D_MODEL), lambda i: (i, 0))
    return pl.pallas_call(
        body, name="adaln_bwd", grid=(t_len // tm,),
        in_specs=[row, row, row, _vec_spec(), _vec_spec()],
        out_specs=[row, pl.BlockSpec((8, D_MODEL), lambda i: (0, 0))],
        out_shape=[jax.ShapeDtypeStruct((t_len, D_MODEL), F32), jax.ShapeDtypeStruct((8, D_MODEL), F32)],
        compiler_params=_params(("arbitrary",)),
    )(dh, x, dy, norm_g, scale1p)


def _head_norm(o, g):
    r = lax.rsqrt(jnp.mean(o * o, axis=-1, keepdims=True) + EPS)
    return r, o * r * g


def _group_spec(tm, cb):
    return pl.BlockSpec((N_HEADS, tm, LANE), lambda i, cb=cb: (cb // N_HEADS, i, 0))


MID_FINAL_G, MID_GATE, MID_LOSS, MID_HG_G, MID_RET_G = range(5)


def middle(x, target, oa, ob, pb, wout, gate, final_g, hg_g, ret_g, t_len):
    tm = _row_tile(t_len, 256)

    def body(x_ref, t_ref, oa_ref, ob_ref, hz_ref, rz_ref, ga_ref, gb_ref, w_ref, gate_ref, fg_ref, hg_ref, rg_ref,
             dy_ref, doa_ref, dob_ref, dw_ref, vec_ref, dpb_ref, m_scr, dm_scr, keep, bufs, sems):
        i = pl.program_id(0)
        sides = ((oa_ref, hz_ref, ga_ref, hg_ref, doa_ref), (ob_ref, rz_ref, gb_ref, rg_ref, dob_ref))
        for hh in range(N_HEADS):
            ls = slice(hh * LANE, (hh + 1) * LANE)
            acc = None
            for side, (o_ref, z_ref, gt_ref, g_ref, _) in enumerate(sides):
                o = o_ref[hh]
                rr = lax.rsqrt(jnp.mean(o * o, axis=-1, keepdims=True) + EPS)
                orr = o * rr
                zz = z_ref[hh]
                sz = _sigmoid(zz)
                sgt = _sigmoid(gt_ref[hh])
                keep[side, hh, 0] = orr
                keep[side, hh, 1] = sz
                keep[side, hh, 2] = sgt
                keep[side, hh, 3] = jnp.broadcast_to(rr, orr.shape)
                u = sgt * ((orr * g_ref[:, ls]) * (zz * sz))
                acc = u if acc is None else acc + u
            m_scr[:, ls] = acc.astype(m_scr.dtype)
        zv = _dot(m_scr[...], w_ref[...])
        gt, fg = gate_ref[...], fg_ref[...]
        y = x_ref[...] + gt * zv
        r = lax.rsqrt(jnp.mean(y * y, axis=-1, keepdims=True) + EPS)
        yn = y * r
        err = yn * fg - t_ref[...]
        loss = 0.5 * jnp.sum(jnp.mean(err * err, axis=-1, keepdims=True), axis=0, keepdims=True)
        dout = err * (1.0 / D_MODEL)
        gd = dout * fg
        dy = r * gd - yn * (r * r) * jnp.mean(y * gd, axis=-1, keepdims=True)
        dy_ref[...] = dy
        dz = (dy * gt).astype(_BF)
        dm_scr[...] = _dot(dz, w_ref[...], _NT)
        part = _dot(m_scr[...], dz, _TN)

        @pl.when(i == 0)
        def _():
            dw_ref[...] = part

        @pl.when(i > 0)
        def _():
            dw_ref[...] = dw_ref[...] + part

        dg = [[], []]
        for hh in range(N_HEADS):
            ls = slice(hh * LANE, (hh + 1) * LANE)
            dmh = dm_scr[:, ls]
            for side, (o_ref, z_ref, gt_ref, g_ref, do_ref) in enumerate(sides):
                zz, g = z_ref[hh], g_ref[:, ls]
                orr, sz, sgt, rr = keep[side, hh, 0], keep[side, hh, 1], keep[side, hh, 2], keep[side, hh, 3]
                n = orr * g
                silu = zz * sz
                du = dmh * sgt
                bufs[2 + side, hh] = dmh * (n * silu) * (sgt * (1.0 - sgt))
                bufs[side, hh] = du * n * (sz * (1.0 + zz * (1.0 - sz)))
                dn = du * silu
                dg[side].append(jnp.sum(dn * orr, axis=0, keepdims=True))
                gdn = dn * g
                do_ref[hh] = rr * (gdn - orr * jnp.mean(orr * gdn, axis=-1, keepdims=True))
        _acc_rows(vec_ref, i, [jnp.sum(dout * yn, axis=0, keepdims=True),
                               jnp.sum(dy * zv, axis=0, keepdims=True),
                               jnp.broadcast_to(loss, (1, D_MODEL)),
                               jnp.concatenate(dg[0], axis=1), jnp.concatenate(dg[1], axis=1)])
        rows = pl.ds(pl.multiple_of(i * tm, tm), tm)
        copies = [pltpu.make_async_copy(bufs.at[n], dpb_ref.at[pl.ds(cb, N_HEADS), rows], sems.at[n])
                  for n, cb in enumerate((CB_HZ, CB_RZ, CB_GA, CB_GB))]
        for cp in copies:
            cp.start()
        for cp in copies:
            cp.wait()

    row = pl.BlockSpec((tm, D_MODEL), lambda i: (i, 0))
    head = pl.BlockSpec((N_HEADS, tm, LANE), lambda i: (0, i, 0))
    full = pl.BlockSpec((D_MODEL, D_MODEL), lambda i: (0, 0))
    return pl.pallas_call(
        body, name="middle", grid=(t_len // tm,),
        in_specs=[row, row, head, head, _group_spec(tm, CB_HZ), _group_spec(tm, CB_RZ), _group_spec(tm, CB_GA),
                  _group_spec(tm, CB_GB), full, _vec_spec(), _vec_spec(), _vec_spec(), _vec_spec()],
        out_specs=[row, head, head, full, pl.BlockSpec((8, D_MODEL), lambda i: (0, 0)),
                   pl.BlockSpec(memory_space=pl.ANY)],
        out_shape=[jax.ShapeDtypeStruct((t_len, D_MODEL), F32),
                   jax.ShapeDtypeStruct((N_HEADS, t_len, LANE), F32),
                   jax.ShapeDtypeStruct((N_HEADS, t_len, LANE), F32),
                   jax.ShapeDtypeStruct((D_MODEL, D_MODEL), F32),
                   jax.ShapeDtypeStruct((8, D_MODEL), F32),
                   jax.ShapeDtypeStruct((N_CB, t_len, LANE), F32)],
        scratch_shapes=[pltpu.VMEM((tm, D_MODEL), _BF), pltpu.VMEM((tm, D_MODEL), F32),
                        pltpu.VMEM((2, N_HEADS, 4, tm, LANE), F32),
                        pltpu.VMEM((4, N_HEADS, tm, LANE), F32), pltpu.SemaphoreType.DMA((4,))],
        compiler_params=_params(("arbitrary",)),
    )(x, target, oa, ob, pb, pb, pb, pb, wout, gate, final_g, hg_g, ret_g)


def device_step(x, target, mod, lb, wt, wout, norm_g, hg_g, ret_g, final_g, start_exchange=None):
    t_len = x.shape[0]
    shift, scale, gate = mod[:, :D_MODEL], mod[:, D_MODEL:2 * D_MODEL], mod[:, 2 * D_MODEL:]
    scale1p = 1.0 + scale
    cos_t, sin_t = _rope_tables(t_len)
    h, h_t = adaln_forward(x, norm_g, scale1p, shift, t_len)
    pb = proj_forward(h, wt, t_len)
    oa, ssave = hgrn_forward(pb, lb, t_len)
    ob, rsave = retention_forward(pb, cos_t, sin_t, t_len)
    dy, doa, dob, dwout, vec_mid, dpb = middle(x, target, oa, ob, pb, wout, gate, final_g, hg_g, ret_g, t_len)
    dpb, dlb = hgrn_backward(pb, lb, doa, ssave, dpb, t_len)
    dpb = retention_backward(pb, cos_t, sin_t, dob, rsave, dpb, t_len)
    dwin = proj_backward_weight(h_t, dpb, t_len)
    token, pending = start_exchange(dwin, dwout) if start_exchange else (jnp.zeros((8, LANE), F32), None)
    dh = proj_backward_input(dpb, wt, token, t_len)
    grad_x, vec_ada = adaln_backward(dh, x, dy, norm_g, scale1p, t_len)
    return grad_x, dwin, dwout, vec_mid, vec_ada, dlb, pending


PACK_ROWS = 16
ROW_NORM_G, ROW_LB, ROW_HG_G, ROW_RET_G, ROW_FINAL_G, ROW_SHIFT, ROW_SCALE, ROW_GATE, ROW_LOSS = range(9)


def _mesh_pos():
    return lax.axis_index("x"), lax.axis_index("y"), lax.axis_index("c")


def _lin(pos):
    return 4 * pos[0] + 2 * pos[1] + pos[2]


def _xor_peer(pos, k):
    return tuple(1 - p if (k >> s) & 1 else p for p, s in zip(pos, (2, 1, 0)))


def _other_chips(pos):
    x, y, _ = pos
    return [(1 - x, y), (x, 1 - y), (1 - x, 1 - y)]


def _remote(src, dst, send_sem, recv_sem, to):
    return pltpu.make_async_remote_copy(src_ref=src, dst_ref=dst, send_sem=send_sem, recv_sem=recv_sem,
                                        device_id=to, device_id_type=MESH)


def pre_exchange(c, w_ada, b_ada, logits):
    def body(c_ref, wada_ref, bada_ref, logit_ref, mod_ref, scall_ref, lb_ref,
             cg_ref, modall_ref, parts_ref, send1, recv1, send2, recv2):
        pos = _mesh_pos()
        cv = c_ref[...]
        slot = lambda p: pl.ds(pl.multiple_of(8 * _lin(p), 8), 8)
        cg_ref[slot(pos), :] = jnp.broadcast_to(cv * _sigmoid(cv), (8, D_MODEL))
        lb_ref[...] = _sigmoid(logit_ref[0:1, :] - logit_ref[1:2, :])
        peers = [_xor_peer(pos, k) for k in range(1, N_DEV)]
        gather = [_remote(cg_ref.at[slot(pos)], cg_ref.at[slot(pos)], send1.at[n], recv1.at[n], p)
                  for n, p in enumerate(peers)]
        for cp in gather:
            cp.start()
        for n, p in enumerate(peers):
            _remote(cg_ref.at[slot(p)], cg_ref.at[slot(p)], send1.at[n], recv1.at[n], p).wait_recv()
        modall_ref[...] = _dot(cg_ref[...], wada_ref[...])
        scatter = [_remote(modall_ref.at[slot(p)], parts_ref.at[slot(pos)], send2.at[n], recv2.at[n], p)
                   for n, p in enumerate(peers)]
        for cp in scatter:
            cp.start()
        parts_ref[slot(pos), :] = modall_ref[slot(pos), :]
        for n, p in enumerate(peers):
            _remote(modall_ref.at[slot(p)], parts_ref.at[slot(p)], send2.at[n], recv2.at[n], p).wait_recv()
        for cp in gather + scatter:
            cp.wait_send()
        for j in range(N_DEV):
            cols = slice(j * SHARD_ADA, (j + 1) * SHARD_ADA)
            mod_ref[:, cols] = parts_ref[8 * j:8 * j + 1, :] + bada_ref[:, cols]
            scall_ref[j:j + 1, :] = cg_ref[8 * j:8 * j + 1, :]

    vmem = pl.BlockSpec(memory_space=pltpu.VMEM)
    return pl.pallas_call(
        body, name="pre_exchange",
        in_specs=[vmem] * 4, out_specs=[vmem] * 3,
        out_shape=[jax.ShapeDtypeStruct((1, 3 * D_MODEL), F32), jax.ShapeDtypeStruct((N_DEV, D_MODEL), F32),
                   jax.ShapeDtypeStruct((1, D_MODEL), F32)],
        scratch_shapes=[pltpu.VMEM((N_DEV * 8, D_MODEL), F32), pltpu.VMEM((N_DEV * 8, SHARD_ADA), F32),
                        pltpu.VMEM((N_DEV * 8, SHARD_ADA), F32)] + [pltpu.SemaphoreType.DMA((N_DEV - 1,))] * 4,
        compiler_params=pltpu.CompilerParams(vmem_limit_bytes=VMEM_LIMIT),
    )(c, w_ada, b_ada, logits)


def weight_gather(win_sh, wout_sh):
    def body(win_ref, wout_ref, wg_ref, woutg_ref, send, recv, local):
        pos = _mesh_pos()
        x, y, c = pos
        sibling = (x, y, 1 - c)
        chips = _other_chips(pos)
        first, passed, mine = [], [], []
        for a, (src, out) in enumerate(((win_ref, wg_ref), (wout_ref, woutg_ref))):
            def copy(k, block, to, src_ref=None, a=a, out=out):
                dst = out.at[_lin(block)]
                return _remote(dst if src_ref is None else src_ref, dst, send.at[7 * a + k], recv.at[7 * a + k], to)
            mine.append(pltpu.make_async_copy(src, out.at[_lin(pos)], local.at[a]))
            first.append(copy(0, pos, sibling, src))
            first += [copy(1 + j, pos, (*chip, c), src) for j, chip in enumerate(chips)]
            passed.append([copy(4 + j, (*chip, c), sibling) for j, chip in enumerate(chips)])
        for cp in mine + first:
            cp.start()
        for a, out in enumerate((wg_ref, woutg_ref)):
            for j, chip in enumerate(chips):
                dst = out.at[_lin((*chip, c))]
                _remote(dst, dst, send.at[7 * a + 1 + j], recv.at[7 * a + 1 + j], pos).wait_recv()
                passed[a][j].start()
        for a, out in enumerate((wg_ref, woutg_ref)):
            dst = out.at[_lin(sibling)]
            _remote(dst, dst, send.at[7 * a], recv.at[7 * a], pos).wait_recv()
            for j, chip in enumerate(chips):
                dst = out.at[_lin((*chip, 1 - c))]
                _remote(dst, dst, send.at[7 * a + 4 + j], recv.at[7 * a + 4 + j], pos).wait_recv()
        for cp in first + passed[0] + passed[1]:
            cp.wait_send()
        for cp in mine:
            cp.wait()

    any_spec = pl.BlockSpec(memory_space=pl.ANY)
    return pl.pallas_call(
        body, name="weight_gather",
        in_specs=[any_spec, any_spec], out_specs=[any_spec, any_spec],
        out_shape=[jax.ShapeDtypeStruct((N_DEV,) + win_sh.shape, win_sh.dtype),
                   jax.ShapeDtypeStruct((N_DEV,) + wout_sh.shape, wout_sh.dtype)],
        scratch_shapes=[pltpu.SemaphoreType.DMA((14,)), pltpu.SemaphoreType.DMA((14,)),
                        pltpu.SemaphoreType.DMA((2,))],
    )(win_sh, wout_sh)


def grad_pair_exchange(g_in, g_out):
    def body(gin_ref, gout_ref, ra_ref, rb_ref, send, recv):
        pos = _mesh_pos()
        x, y, c = pos
        sibling = (x, y, 1 - c)
        copies = []
        for a, (src, dst) in enumerate(((gin_ref, ra_ref), (gout_ref, rb_ref))):
            for q in range(4):
                copies.append(_remote(src.at[2 * q + (1 - c)], dst.at[q], send.at[4 * a + q], recv.at[4 * a + q],
                                      sibling))
        for cp in copies:
            cp.start()
        for cp in copies:
            cp.wait_recv()
        for cp in copies:
            cp.wait_send()

    any_spec = pl.BlockSpec(memory_space=pl.ANY)
    return pl.pallas_call(
        body, name="grad_pair_exchange",
        in_specs=[any_spec, any_spec], out_specs=[any_spec, any_spec],
        out_shape=[jax.ShapeDtypeStruct((4,) + g_in.shape[1:], F32), jax.ShapeDtypeStruct((4,) + g_out.shape[1:], F32)],
        scratch_shapes=[pltpu.SemaphoreType.DMA((8,)), pltpu.SemaphoreType.DMA((8,))],
    )(g_in, g_out)


def pair_sum(g_in, ra, g_out, rb, c_idx):
    tr = 256

    def body(c_ref, gin_ref, ra_ref, gout_ref, rb_ref, sb_ref, sbo_ref):
        del c_ref
        sb_ref[...] = (gin_ref[...] + ra_ref[...]).astype(sb_ref.dtype)
        sbo_ref[...] = gout_ref[...] + rb_ref[...]

    n_i = D_MODEL // tr
    return pl.pallas_call(
        body, name="pair_sum",
        grid_spec=pltpu.PrefetchScalarGridSpec(
            num_scalar_prefetch=1, grid=(4, n_i),
            in_specs=[pl.BlockSpec((None, tr, SHARD_IN), lambda q, i, c: (2 * q + c[0], i, 0)),
                      pl.BlockSpec((None, tr, SHARD_IN), lambda q, i, c: (q, i, 0)),
                      pl.BlockSpec((None, SHARD_OUT // n_i, D_MODEL), lambda q, i, c: (2 * q + c[0], i, 0)),
                      pl.BlockSpec((None, SHARD_OUT // n_i, D_MODEL), lambda q, i, c: (q, i, 0))],
            out_specs=[pl.BlockSpec((None, tr, SHARD_IN), lambda q, i, c: (q, i, 0)),
                       pl.BlockSpec((None, SHARD_OUT // n_i, D_MODEL), lambda q, i, c: (q, i, 0))]),
        out_shape=[jax.ShapeDtypeStruct(ra.shape, _BF), jax.ShapeDtypeStruct(rb.shape, F32)],
        compiler_params=_params(("arbitrary", "arbitrary")),
    )(c_idx, g_in, ra, g_out, rb)


_HBM = pl.BlockSpec(memory_space=pltpu.HBM)
_SEM = pl.BlockSpec(memory_space=pltpu.SEMAPHORE)
_N_CHIP_COPIES = 6


def _chip_copies(sb_ref, sbo_ref, rc_ref, rco_ref, send, recv):
    pos = _mesh_pos()
    copies = []
    for a, (src, dst) in enumerate(((sb_ref, rc_ref), (sbo_ref, rco_ref))):
        for j, chip in enumerate(_other_chips(pos)):
            copies.append(_remote(src.at[2 * chip[0] + chip[1]], dst.at[j], send.at[3 * a + j], recv.at[3 * a + j],
                                  (*chip, pos[2])))
    return copies


def grad_chip_start(sb, sbo):
    def body(sb_ref, sbo_ref, rc_ref, rco_ref, send, recv, sb_thru, sbo_thru, rc_thru, rco_thru, token):
        del sb_thru, sbo_thru, rc_thru, rco_thru
        for cp in _chip_copies(sb_ref, sbo_ref, rc_ref, rco_ref, send, recv):
            cp.start()
        token[...] = jnp.zeros_like(token)

    hbm = lambda a: pltpu.with_memory_space_constraint(a, pltpu.HBM)
    rc = lax.empty((3,) + sb.shape[1:], sb.dtype)
    rco = lax.empty((3,) + sbo.shape[1:], sbo.dtype)
    return pl.pallas_call(
        body, name="grad_chip_start",
        in_specs=[_HBM] * 4,
        out_specs=[_SEM, _SEM, _HBM, _HBM, _HBM, _HBM, pl.BlockSpec(memory_space=pltpu.VMEM)],
        out_shape=[pltpu.SemaphoreType.DMA((_N_CHIP_COPIES,)), pltpu.SemaphoreType.DMA((_N_CHIP_COPIES,)),
                   pltpu.HBM(sb.shape, sb.dtype), pltpu.HBM(sbo.shape, sbo.dtype),
                   pltpu.HBM(rc.shape, rc.dtype), pltpu.HBM(rco.shape, rco.dtype),
                   jax.ShapeDtypeStruct((8, LANE), F32)],
        input_output_aliases={0: 2, 1: 3, 2: 4, 3: 5},
        compiler_params=pltpu.CompilerParams(has_side_effects=pltpu.SideEffectType.DATAFLOW_SIDE_EFFECTING),
    )(hbm(sb), hbm(sbo), hbm(rc), hbm(rco))


def grad_chip_wait(send, recv, sb, sbo, rc, rco, after):
    def body(sb_ref, sbo_ref, rc_ref, rco_ref, send, recv, after_ref, sb_o, sbo_o, rc_o, rco_o):
        del after_ref, sb_o, sbo_o, rc_o, rco_o
        for cp in _chip_copies(sb_ref, sbo_ref, rc_ref, rco_ref, send, recv):
            cp.wait_send()
            cp.wait_recv()

    return pl.pallas_call(
        body, name="grad_chip_wait",
        in_specs=[_HBM] * 4 + [_SEM, _SEM, pl.BlockSpec(memory_space=pl.ANY)],
        out_specs=[_HBM] * 4,
        out_shape=[pltpu.HBM(sb.shape, sb.dtype), pltpu.HBM(sbo.shape, sbo.dtype),
                   pltpu.HBM(rc.shape, rc.dtype), pltpu.HBM(rco.shape, rco.dtype)],
        input_output_aliases={0: 0, 1: 1, 2: 2, 3: 3},
        compiler_params=pltpu.CompilerParams(has_side_effects=pltpu.SideEffectType.DATAFLOW_SIDE_EFFECTING),
    )(sb, sbo, rc, rco, send, recv, after)


def pack_gather(pack):
    def body(pack_ref, packs_ref, psend, precv):
        pos = _mesh_pos()
        me = _lin(pos)
        packs_ref[me] = pack_ref[...]
        peers = [_xor_peer(pos, k) for k in range(1, N_DEV)]
        gather = [_remote(packs_ref.at[me], packs_ref.at[me], psend.at[n], precv.at[n], p) for n, p in enumerate(peers)]
        for cp in gather:
            cp.start()
        for n, p in enumerate(peers):
            _remote(packs_ref.at[_lin(p)], packs_ref.at[_lin(p)], psend.at[n], precv.at[n], p).wait_recv()
        for cp in gather:
            cp.wait_send()

    vmem = pl.BlockSpec(memory_space=pltpu.VMEM)
    return pl.pallas_call(
        body, name="pack_gather", in_specs=[vmem], out_specs=vmem,
        out_shape=jax.ShapeDtypeStruct((N_DEV,) + pack.shape, F32),
        scratch_shapes=[pltpu.SemaphoreType.DMA((N_DEV - 1,)), pltpu.SemaphoreType.DMA((N_DEV - 1,))],
    )(pack)


def pack_rows(vec_mid, vec_ada, dlb):
    def body(mid_ref, ada_ref, dlb_ref, o_ref):
        mid = lambda r: mid_ref[r:r + 1, :]
        rows = [ada_ref[0:1, :], dlb_ref[...], mid(MID_HG_G), mid(MID_RET_G), mid(MID_FINAL_G),
                ada_ref[2:3, :], ada_ref[1:2, :], mid(MID_GATE), mid(MID_LOSS)]
        o_ref[...] = jnp.zeros_like(o_ref)
        for n, row in enumerate(rows):
            o_ref[n:n + 1, :] = row

    vmem = pl.BlockSpec(memory_space=pltpu.VMEM)
    return pl.pallas_call(body, name="pack_rows", in_specs=[vmem] * 3, out_specs=vmem,
                          out_shape=jax.ShapeDtypeStruct((PACK_ROWS, D_MODEL), F32))(vec_mid, vec_ada, dlb)


def _adamw(w, g, m, v):
    m = ADAM_B1 * m + (1.0 - ADAM_B1) * g
    v = ADAM_B2 * v + (1.0 - ADAM_B2) * (g * g)
    m_hat = m / (1.0 - ADAM_B1 ** ADAM_STEP)
    v_hat = v / (1.0 - ADAM_B2 ** ADAM_STEP)
    delta = -ADAM_LR * (m_hat / (jnp.sqrt(v_hat) + ADAM_EPS) + ADAM_WD * w)
    return delta, m, v


def adam_shard(chip_idx, own, parts, w, m, v, name):
    rows, cols = w.shape
    tr = min(rows, 128)

    def body(chip_ref, p0, p1, p2, p3, w_ref, m_ref, v_ref, g_ref, d_ref, nm_ref, nv_ref):
        del chip_ref
        g = ((p0[...].astype(F32) + p1[...].astype(F32)) + p2[...].astype(F32)) + p3[...].astype(F32)
        g_ref[...] = g
        d_ref[...], nm_ref[...], nv_ref[...] = _adamw(w_ref[...], g, m_ref[...], v_ref[...])

    part = lambda q: pl.BlockSpec((None, tr, cols), lambda i, chip, q=q: (q, i, 0))
    tile = pl.BlockSpec((tr, cols), lambda i, chip: (i, 0))
    return pl.pallas_call(
        body, name=name,
        grid_spec=pltpu.PrefetchScalarGridSpec(
            num_scalar_prefetch=1, grid=(rows // tr,),
            in_specs=[pl.BlockSpec((None, tr, cols), lambda i, chip: (chip[0], i, 0)), part(0), part(1), part(2),
                      tile, tile, tile],
            out_specs=[tile] * 4),
        out_shape=[jax.ShapeDtypeStruct(w.shape, F32)] * 4,
        compiler_params=_params(("arbitrary",)),
    )(chip_idx, own, parts, parts, parts, w, m, v)


def adam_ada(sc_t, dmod_all, me_idx, w, m, v):
    def body(me_ref, sc_ref, dm_ref, w_ref, m_ref, v_ref, g_ref, d_ref, nm_ref, nv_ref):
        del me_ref
        g = _dot_f32(sc_ref[...], dm_ref[...])
        g_ref[...] = g
        d_ref[...], nm_ref[...], nv_ref[...] = _adamw(w_ref[...], g, m_ref[...], v_ref[...])

    full = pl.BlockSpec(w.shape, lambda i, me: (0, 0))
    return pl.pallas_call(
        body, name="adam_ada",
        grid_spec=pltpu.PrefetchScalarGridSpec(
            num_scalar_prefetch=1, grid=(1,),
            in_specs=[pl.BlockSpec(sc_t.shape, lambda i, me: (0, 0)),
                      pl.BlockSpec((LANE, SHARD_ADA), lambda i, me: (0, me[0])), full, full, full],
            out_specs=[full] * 4),
        out_shape=[jax.ShapeDtypeStruct(w.shape, F32)] * 4,
        compiler_params=_params(("arbitrary",)),
    )(me_idx, sc_t, dmod_all, w, m, v)


def adam_vectors(packs, lb, params, ms, vs):
    n = len(params)

    def body(*refs):
        packs_ref, lb_ref = refs[0], refs[1]
        w_refs, m_refs, v_refs = refs[2:2 + n], refs[2 + n:2 + 2 * n], refs[2 + 2 * n:2 + 3 * n]
        loss_ref = refs[2 + 3 * n]
        outs = refs[3 + 3 * n:3 + 7 * n]
        tot_ref = refs[3 + 7 * n]
        tot = packs_ref[0]
        for d in range(1, N_DEV):
            tot = tot + packs_ref[d]
        tot_ref[...] = tot
        row = lambda r: tot_ref[r:r + 1, :]
        lbv = lb_ref[...]
        dl0 = row(ROW_LB) * lbv * (1.0 - lbv)
        grads = [[row(ROW_NORM_G)],
                 [jnp.concatenate([row(ROW_SHIFT), row(ROW_SCALE), row(ROW_GATE)], axis=1)],
                 [dl0, -dl0],
                 [row(ROW_HG_G)], [row(ROW_RET_G)], [row(ROW_FINAL_G)]]
        loss_ref[...] = tot_ref[ROW_LOSS:ROW_LOSS + 1, 0:LANE]
        for j, g_rows in enumerate(grads):
            for r, g in enumerate(g_rows):
                rs = slice(r, r + 1)
                d, nm, nv = _adamw(w_refs[j][rs, :], g, m_refs[j][rs, :], v_refs[j][rs, :])
                outs[4 * j][rs, :] = g
                outs[4 * j + 1][rs, :] = d
                outs[4 * j + 2][rs, :] = nm
                outs[4 * j + 3][rs, :] = nv

    vmem = pl.BlockSpec(memory_space=pltpu.VMEM)
    out_shape = [jax.ShapeDtypeStruct((1, LANE), F32)]
    for w in params:
        out_shape += [jax.ShapeDtypeStruct(w.shape, F32)] * 4
    return pl.pallas_call(
        body, name="adam_vectors", in_specs=[vmem] * (2 + 3 * n), out_specs=[vmem] * len(out_shape),
        out_shape=out_shape, scratch_shapes=[pltpu.VMEM((PACK_ROWS, D_MODEL), F32)],
    )(packs, lb, *params, *ms, *vs)


def kernel(x, c, norm_g, w_ada, b_ada, w_in, hg_lb_logits, hg_norm_g, ret_norm_g, w_out, final_g, loss_target, m_norm_g, m_w_ada, m_b_ada, m_w_in, m_hg_lb_logits, m_hg_norm_g, m_ret_norm_g, m_w_out, m_final_g, v_norm_g, v_w_ada, v_b_ada, v_w_in, v_hg_lb_logits, v_hg_norm_g, v_ret_norm_g, v_w_out, v_final_g):
    pos = _mesh_pos()
    me_idx = jnp.reshape(_lin(pos), (1,)).astype(jnp.int32)
    c_idx = jnp.reshape(pos[2], (1,)).astype(jnp.int32)
    vec = lambda a: a.reshape(1, D_MODEL)

    mod, scall, lb = pre_exchange(c, w_ada[0], b_ada, hg_lb_logits)
    wtg, woutg = weight_gather(w_in[0].T.astype(_BF), w_out[0].astype(_BF))
    chip_idx = jnp.reshape(2 * pos[0] + pos[1], (1,)).astype(jnp.int32)

    def start_exchange(dwin, dwout):
        dwout = dwout.reshape(N_DEV, SHARD_OUT, D_MODEL)
        ra, rb = grad_pair_exchange(dwin, dwout)
        sb, sbo = pair_sum(dwin, ra, dwout, rb, c_idx)
        send, recv, sb, sbo, rc, rco, token = grad_chip_start(sb, sbo)
        return token, (send, recv, sb, sbo, rc, rco)

    grad_x, _, _, vec_mid, vec_ada, dlb, pending = device_step(
        x[0], loss_target[0], mod, lb, wtg.reshape(D_IN, D_MODEL), woutg.reshape(D_MODEL, D_MODEL), norm_g,
        hg_norm_g, ret_norm_g, vec(final_g), start_exchange)
    sb, sbo, rc, rco = grad_chip_wait(*pending, grad_x)
    packs = pack_gather(pack_rows(vec_mid, vec_ada, dlb))

    g_in, d_in, nm_in, nv_in = adam_shard(chip_idx, sb, rc, w_in[0], m_w_in[0], v_w_in[0], "adam_w_in")
    g_out, d_out, nm_out, nv_out = adam_shard(chip_idx, sbo, rco, w_out[0], m_w_out[0], v_w_out[0], "adam_w_out")
    dmod_all = packs[:, ROW_SHIFT:ROW_GATE + 1, :].reshape(N_DEV, 3 * D_MODEL)
    dmod_all = jnp.pad(dmod_all, ((0, LANE - N_DEV), (0, 0)))
    sc_t = jnp.pad(scall.T, ((0, 0), (0, LANE - N_DEV)))
    g_ada, d_ada, nm_ada, nv_ada = adam_ada(sc_t, dmod_all, me_idx, w_ada[0], m_w_ada[0], v_w_ada[0])
    small = adam_vectors(
        packs, lb,
        (norm_g, b_ada, hg_lb_logits, hg_norm_g, ret_norm_g, vec(final_g)),
        (m_norm_g, m_b_ada, m_hg_lb_logits, m_hg_norm_g, m_ret_norm_g, vec(m_final_g)),
        (v_norm_g, v_b_ada, v_hg_lb_logits, v_hg_norm_g, v_ret_norm_g, vec(v_final_g)))
    loss = small[0][0, 0]
    (g_ng, d_ng, nm_ng, nv_ng), (g_b, d_b, nm_b, nv_b), (g_lb, d_lb, nm_lb, nv_lb), (g_hg, d_hg, nm_hg, nv_hg), \
        (g_rg, d_rg, nm_rg, nv_rg), (g_fg, d_fg, nm_fg, nv_fg) = [small[1 + 4 * j:5 + 4 * j] for j in range(6)]
    flat = lambda a: a.reshape(D_MODEL)

    def group(ng, ada, b, win, lbl, hg, rg, wo, fg):
        return (ng, ada[None], b, win[None], lbl, hg, rg, wo[None], flat(fg))

    return (loss, grad_x[None],
            *group(g_ng, g_ada, g_b, g_in, g_lb, g_hg, g_rg, g_out, g_fg),
            *group(d_ng, d_ada, d_b, d_in, d_lb, d_hg, d_rg, d_out, d_fg),
            *group(nm_ng, nm_ada, nm_b, nm_in, nm_lb, nm_hg, nm_rg, nm_out, nm_fg),
            *group(nv_ng, nv_ada, nv_b, nv_in, nv_lb, nv_hg, nv_rg, nv_out, nv_fg))
```

```python
import functools

import numpy as np
import jax
import jax.numpy as jnp
from jax import lax
from jax.experimental import pallas as pl
from jax.experimental.pallas import tpu as pltpu

F32 = jnp.float32
_BF = jnp.bfloat16

D_MODEL = 1024
N_HEADS = 8
LANE = 128
RET_DK = 64
D_IN = 9216
N_DEV = 8
SHARD_IN = D_IN // N_DEV
SHARD_ADA = 3 * D_MODEL // N_DEV
SHARD_OUT = D_MODEL // N_DEV
N_CB = D_IN // LANE
CB_PER_SHARD = SHARD_IN // LANE
CHUNK = 128
N_LEVELS = 7
EPS = 1e-6
ROPE_BASE = 10000.0
CB_HQ, CB_HF, CB_HI, CB_HZ, CB_RQ, CB_RK, CB_RV, CB_RZ, CB_GA, CB_GB = 0, 8, 16, 24, 32, 36, 40, 48, 56, 64
VMEM_LIMIT = 56 * 1024 * 1024

ADAM_LR, ADAM_B1, ADAM_B2, ADAM_EPS, ADAM_WD, ADAM_STEP = 0.001, 0.9, 0.999, 1e-08, 0.01, 10

_NN = (((1,), (0,)), ((), ()))
_NT = (((1,), (1,)), ((), ()))
_TN = (((0,), (0,)), ((), ()))
MESH = pl.DeviceIdType.MESH


def _dot(a, b, dims=_NN):
    return lax.dot_general(a.astype(_BF), b.astype(_BF), dims, preferred_element_type=F32)


def _split2(a):
    hi = a.astype(_BF)
    lo = (a - hi.astype(F32)).astype(_BF)
    return jnp.concatenate([hi, lo], axis=1)


def _dot_sel(sel, a):
    n = a.shape[1]
    r = lax.dot_general(sel.astype(_BF), _split2(a), _NN, preferred_element_type=F32)
    return r[:, :n] + r[:, n:]


def _dot_f32(a, b):
    def pieces(v):
        p1 = v.astype(_BF)
        r1 = v - p1.astype(F32)
        p2 = r1.astype(_BF)
        p3 = (r1 - p2.astype(F32)).astype(_BF)
        return p1, p2, p3
    a1, a2, a3 = pieces(a)
    b1, b2, b3 = pieces(b)
    d = lambda u, v: lax.dot_general(u, v, _NN, preferred_element_type=F32)
    return ((d(a1, b3) + d(a2, b2) + d(a3, b1)) + (d(a1, b2) + d(a2, b1))) + d(a1, b1)


def _sigmoid(v):
    return 1.0 / (1.0 + jnp.exp(-v))


def _params(sem=None):
    return pltpu.CompilerParams(dimension_semantics=sem, vmem_limit_bytes=VMEM_LIMIT)


def _hgrn_consts():
    c, nl = CHUNK, N_LEVELS
    t = np.arange(c)[:, None]
    j = np.arange(c)[None, :]
    sel = [j <= t]
    masks = [j == t]
    for l in range(1, nl + 1):
        m = ((t >> l) << l) + (1 << (l - 1)) - 1
        sec = t > m
        sel.append(np.where(sec, (j > m) & (j <= t), (j > t) & (j <= m)))
        same = (t >> l) == (j >> l)
        masks.append(same & sec & (j <= m))
    sel.append(j > t)
    sel = np.concatenate(sel, 0).astype(np.float32)
    masks = np.stack(masks).astype(np.float32)
    sgn = np.stack([np.where((t & (1 << (l - 1))) != 0, 1.0, -1.0) * np.ones((1, LANE)) for l in range(3, nl + 1)])
    return dict(tri=jnp.asarray(sel[:c], _BF),
                lvl=jnp.asarray(masks, F32),
                sgn=jnp.asarray(sgn, F32),
                sel_t=jnp.asarray(sel.T, _BF),
                lvl_b=jnp.asarray(masks, _BF),
                lvlt_b=jnp.asarray(np.swapaxes(masks, 1, 2), _BF))


def _level_exponents(b, logf, b_scr, sgn_ref):
    c = CHUNK
    b_scr[...] = b
    row = lax.broadcasted_iota(jnp.int32, (c, LANE), 0)
    nxt = pltpu.roll(logf, c - 1, 0)
    prv = pltpu.roll(logf, 1, 0)
    r4 = row & 3
    out = [jnp.where((row & 1) == 1, logf, 0.0),
           jnp.where(r4 == 0, nxt, jnp.where(r4 == 1, 0.0, jnp.where(r4 == 2, logf, logf + prv)))]
    for l in range(3, N_LEVELS + 1):
        size, half = 1 << l, 1 << (l - 1)
        ref = jnp.concatenate([jnp.broadcast_to(b_scr[i * size + half - 1:i * size + half, :], (size, LANE))
                               for i in range(c // size)], axis=0)
        out.append((b - ref) * sgn_ref[l - 3])
    return out


def _hgrn_chunk(hq, hf, hi, lbv, tri_ref, sgn_ref, b_scr):
    sq = _sigmoid(hq)
    q = hq * sq
    sg = _sigmoid(hf)
    omlb = 1.0 - lbv
    f = lbv + omlb * sg
    k = 1.0 - f
    logf = jnp.log(f)
    b = _dot_sel(tri_ref[...], logf)
    bc = jnp.sum(logf, axis=0, keepdims=True)
    lev = [None] + [jnp.exp(e) for e in _level_exponents(b, logf, b_scr, sgn_ref)]
    return dict(sq=sq, q=q, sg=sg, omlb=omlb, f=f, k=k, v=hi, eb=jnp.exp(b), erem=jnp.exp(bc - b), ebc=jnp.exp(bc),
                lev=lev)


def _blockdiag(a, b):
    z = jnp.zeros_like(a)
    return jnp.concatenate([jnp.concatenate([a, z], axis=1), jnp.concatenate([z, b], axis=1)], axis=0)


def _level_operands(a):
    q, k = a["q"], a["k"]
    ql = [q.astype(_BF)] + [(q * a["lev"][l]).astype(_BF) for l in range(1, N_LEVELS + 1)]
    kl = [k.astype(_BF)] + [(k * a["lev"][l]).astype(_BF) for l in range(1, N_LEVELS + 1)]
    pairs = range(0, N_LEVELS + 1, 2)
    return ([jnp.concatenate([ql[l], ql[l + 1]], axis=1) for l in pairs], [_blockdiag(kl[l], kl[l + 1]) for l in pairs],
            ql, kl)


def _hgrn_scores(a, lvl_ref, q_pairs, k_diags):
    acc = None
    for n, (qp, kd) in enumerate(zip(q_pairs, k_diags)):
        both = lax.dot_general(qp, kd, _NT, preferred_element_type=F32)
        part = lvl_ref[2 * n] * both[:, :CHUNK] + lvl_ref[2 * n + 1] * both[:, CHUNK:]
        acc = part if acc is None else acc + part
    return acc


SCAN_UNROLL = 2


def _resident(const):
    zeros = (0,) * const.ndim
    return pl.BlockSpec(const.shape, lambda p, t: zeros)


def _time_block(t_len):
    return min(t_len, 1024)


def hgrn_forward(pb, lb, t_len):
    nc = t_len // CHUNK
    tb = _time_block(t_len)
    ncb = tb // CHUNK
    consts = _hgrn_consts()
    operands = [consts[n] for n in ("tri", "lvl", "sgn")]

    def body(hq_ref, hf_ref, hi_ref, lb_ref, tri_ref, lvl_ref, sgn_ref, o_ref, ssave_ref, st_ref, b_scr):
        @pl.when(pl.program_id(1) == 0)
        def _():
            st_ref[...] = jnp.zeros_like(st_ref)

        def chunk(ci, carry):
            r = pl.ds(pl.multiple_of(ci * CHUNK, CHUNK), CHUNK)
            for hd in range(2):
                lbv = lb_ref[:, hd * LANE:(hd + 1) * LANE]
                a = _hgrn_chunk(hq_ref[hd, r, :], hf_ref[hd, r, :], hi_ref[hd, r, :], lbv, tri_ref, sgn_ref,
                                b_scr.at[hd])
                q_pairs, k_diags, _, _ = _level_operands(a)
                st = st_ref[hd]
                ssave_ref[hd, ci] = st
                scores = _hgrn_scores(a, lvl_ref, q_pairs, k_diags)
                o_ref[hd, r, :] = _dot(a["q"] * a["eb"], st, _NT) + _dot(scores, a["v"])
                st_ref[hd] = st * a["ebc"] + _dot(a["v"], a["k"] * a["erem"], _TN)
            return carry

        lax.fori_loop(0, ncb, chunk, 0)

    pair = lambda base: pl.BlockSpec((2, tb, LANE), lambda p, t, base=base: (base // 2 + p, t, 0))
    return pl.pallas_call(
        body, name="hgrn_fwd", grid=(N_HEADS // 2, t_len // tb),
        in_specs=[pair(CB_HQ), pair(CB_HF), pair(CB_HI),
                  pl.BlockSpec((1, 2 * LANE), lambda p, t: (0, p))] + [_resident(c) for c in operands],
        out_specs=[pl.BlockSpec((2, tb, LANE), lambda p, t: (p, t, 0)),
                   pl.BlockSpec((2, ncb, LANE, LANE), lambda p, t: (p, t, 0, 0))],
        out_shape=[jax.ShapeDtypeStruct((N_HEADS, t_len, LANE), F32),
                   jax.ShapeDtypeStruct((N_HEADS, nc, LANE, LANE), F32)],
        scratch_shapes=[pltpu.VMEM((2, LANE, LANE), F32), pltpu.VMEM((2, CHUNK, LANE), F32)],
        compiler_params=_params(("arbitrary", "arbitrary")),
    )(pb, pb, pb, lb, *operands)


def hgrn_backward(pb, lb, do, ssave, dpb, t_len):
    tb = _time_block(t_len)
    ncb, ntb = tb // CHUNK, t_len // tb
    consts = _hgrn_consts()
    operands = [consts[n] for n in ("tri", "lvl", "sgn", "sel_t", "lvl_b", "lvlt_b")]

    def body(hq_ref, hf_ref, hi_ref, lb_ref, do_ref, ssave_ref, tri_ref, lvl_ref, sgn_ref, selt_ref, lvlb_ref,
             lvltb_ref, dpb_in, dpb_ref, dlb_ref, dq_buf, df_buf, di_buf, dst_ref, b_scr, sems):
        del dpb_in
        p, t = pl.program_id(0), pl.program_id(1)

        @pl.when(t == 0)
        def _():
            dst_ref[...] = jnp.zeros_like(dst_ref)
            dlb_ref[...] = jnp.zeros_like(dlb_ref)

        def chunk(i, carry):
            ci = ncb - 1 - i
            r = pl.ds(pl.multiple_of(ci * CHUNK, CHUNK), CHUNK)
            for hd in range(2):
                head_chunk(hd, ci, r)
            return carry

        def head_chunk(hd, ci, r):
            lbv = lb_ref[:, hd * LANE:(hd + 1) * LANE]
            hq = hq_ref[hd, r, :]
            a = _hgrn_chunk(hq, hf_ref[hd, r, :], hi_ref[hd, r, :], lbv, tri_ref, sgn_ref, b_scr.at[hd])
            q_pairs, k_diags, ql, kl = _level_operands(a)
            q, k, v = a["q"], a["k"], a["v"]
            g = do_ref[hd, r, :]
            st0 = ssave_ref[hd, ci]
            dst = dst_ref[hd]
            scores = _hgrn_scores(a, lvl_ref, q_pairs, k_diags)
            da = _dot(g, v, _NT)
            da_t = _dot(v, g, _NT)
            kb = k * a["erem"]
            qb = q * a["eb"]
            dv = _dot(scores, g, _TN) + _dot(kb, dst, _NT)
            dq_inter = _dot(g, st0) * a["eb"]
            dk_state = _dot(v, dst) * a["erem"]
            dq, dk = dq_inter, dk_state
            de = [q * dq_inter]
            da_b, dat_b = da.astype(_BF), da_t.astype(_BF)
            for n in range(len(k_diags)):
                l0, l1 = 2 * n, 2 * n + 1
                da_pair = jnp.concatenate([lvlb_ref[l0] * da_b, lvlb_ref[l1] * da_b], axis=1)
                dat_pair = jnp.concatenate([lvltb_ref[l0] * dat_b, lvltb_ref[l1] * dat_b], axis=1)
                dq_both = lax.dot_general(da_pair, k_diags[n], _NN, preferred_element_type=F32)
                dk_both = lax.dot_general(dat_pair, _blockdiag(ql[l0], ql[l1]), _NN, preferred_element_type=F32)
                for l, cols in ((l0, slice(0, LANE)), (l1, slice(LANE, 2 * LANE))):
                    dql, dkl = dq_both[:, cols], dk_both[:, cols]
                    if l > 0:
                        e = a["lev"][l]
                        dql, dkl = dql * e, dkl * e
                        de.append(q * dql + k * dkl)
                    dq = dq + dql
                    dk = dk + dkl
            de.append(k * dk_state)
            dst_ref[hd] = dst * a["ebc"] + _dot(g, qb, _TN)
            dbc = jnp.sum(dst * st0, axis=0, keepdims=True) * a["ebc"]
            de2 = lax.dot_general(selt_ref[...], _split2(jnp.concatenate(de, axis=0)), _NN,
                                  preferred_element_type=F32)
            dlogf = de2[:, :LANE] + de2[:, LANE:] + dbc
            sq, sg = a["sq"], a["sg"]
            df = dlogf / a["f"] - dk
            dq_buf[hd, r, :] = dq * (sq * (1.0 + hq * (1.0 - sq)))
            df_buf[hd, r, :] = df * a["omlb"] * sg * (1.0 - sg)
            di_buf[hd, r, :] = dv
            cols = slice(hd * LANE, (hd + 1) * LANE)
            dlb_ref[:, cols] = dlb_ref[:, cols] + jnp.sum(df * (1.0 - sg), axis=0, keepdims=True)

        lax.fori_loop(0, ncb, chunk, 0)
        rows = pl.ds(pl.multiple_of((ntb - 1 - t) * tb, tb), tb)
        copies = [pltpu.make_async_copy(buf, dpb_ref.at[pl.ds(base + 2 * p, 2), rows], sems.at[n])
                  for n, (buf, base) in enumerate(((dq_buf, CB_HQ), (df_buf, CB_HF), (di_buf, CB_HI)))]
        for cp in copies:
            cp.start()
        for cp in copies:
            cp.wait()

    pair = lambda base: pl.BlockSpec((2, tb, LANE), lambda p, t, base=base: (base // 2 + p, ntb - 1 - t, 0))
    any_spec = pl.BlockSpec(memory_space=pl.ANY)
    return pl.pallas_call(
        body, name="hgrn_bwd", grid=(N_HEADS // 2, ntb),
        in_specs=[pair(CB_HQ), pair(CB_HF), pair(CB_HI),
                  pl.BlockSpec((1, 2 * LANE), lambda p, t: (0, p)),
                  pair(0),
                  pl.BlockSpec((2, ncb, LANE, LANE), lambda p, t: (p, ntb - 1 - t, 0, 0))]
        + [_resident(c) for c in operands] + [any_spec],
        out_specs=[any_spec, pl.BlockSpec((1, 2 * LANE), lambda p, t: (0, p))],
        out_shape=[jax.ShapeDtypeStruct(dpb.shape, F32), jax.ShapeDtypeStruct((1, D_MODEL), F32)],
        scratch_shapes=[pltpu.VMEM((2, tb, LANE), F32)] * 3 + [
            pltpu.VMEM((2, LANE, LANE), F32), pltpu.VMEM((2, CHUNK, LANE), F32), pltpu.SemaphoreType.DMA((3,))],
        input_output_aliases={6 + len(operands): 0},
        compiler_params=_params(("arbitrary", "arbitrary")),
    )(pb, pb, pb, lb, do, ssave, *operands, dpb)


def _rope_tables(t_len):
    half = RET_DK // 2
    inv_freq = 1.0 / (ROPE_BASE ** jnp.linspace(0.0, 1.0, half, dtype=F32))
    ang = jnp.arange(t_len, dtype=jnp.int32).astype(F32)[:, None] * inv_freq[None, :]
    cos, sin = jnp.cos(ang), jnp.sin(ang)
    cos_t = jnp.concatenate([cos, cos, cos, cos], axis=1)
    sin_t = jnp.concatenate([-sin, sin, -sin, sin], axis=1)
    return cos_t, sin_t


def _swap_halves(v):
    half = RET_DK // 2
    lane = lax.broadcasted_iota(jnp.int32, v.shape, 1)
    first = (lane & (RET_DK - 1)) < half
    return jnp.where(first, pltpu.roll(v, LANE - half, 1), pltpu.roll(v, half, 1))


def _ret_head_consts(hidx):
    c = CHUNK
    hf = jnp.full((1, LANE), hidx, jnp.int32).astype(F32)
    lg = jnp.log(1.0 - jnp.exp(-(5.0 + hf) * np.float32(np.log(2.0))))
    row = lax.broadcasted_iota(jnp.int32, (c, c), 0)
    col = lax.broadcasted_iota(jnp.int32, (c, c), 1)
    rel = (row - col).astype(F32)
    dm = jnp.where(rel >= 0, jnp.exp(lg[:, :1] * jnp.maximum(rel, 0.0)), 0.0)
    dm_t = jnp.where(rel <= 0, jnp.exp(lg[:, :1] * jnp.maximum(-rel, 0.0)), 0.0)
    idx = lax.broadcasted_iota(jnp.int32, (c, LANE), 0).astype(F32)
    zeta = jnp.exp(lg * (c - 1.0 - idx))
    xi = jnp.exp(lg * (idx + 1.0))
    cdec = jnp.exp(lg * float(c))
    return dm, zeta, xi, cdec, dm_t


def _lane_mask(which):
    lane = lax.broadcasted_iota(jnp.int32, (1, LANE), 1)
    return ((lane // RET_DK) == which).astype(F32)


def retention_forward(pb, cos_t, sin_t, t_len):
    nc = t_len // CHUNK

    tb = _time_block(t_len)
    ncb = tb // CHUNK

    def body(rq_ref, rk_ref, rv_ref, cos_ref, sin_ref, o_ref, rsave_ref, st_ref):
        p = pl.program_id(0)

        @pl.when(pl.program_id(1) == 0)
        def _():
            st_ref[...] = jnp.zeros_like(st_ref)

        consts = [_ret_head_consts(2 * p + hd) for hd in range(2)]

        def chunk(ci, carry):
            r = pl.ds(pl.multiple_of(ci * CHUNK, CHUNK), CHUNK)
            cs, sn = cos_ref[r, :], sin_ref[r, :]
            q = rq_ref[r, :]
            k = rk_ref[r, :]
            q = q * cs + _swap_halves(q) * sn
            k = (k * cs + _swap_halves(k) * sn) * RET_DK ** -0.5
            for hd in range(2):
                dm, zeta, xi, cdec, _ = consts[hd]
                lm = _lane_mask(hd)
                qh, kh = q * lm, k * lm
                v = rv_ref[hd, r, :]
                st = st_ref[hd]
                rsave_ref[hd, ci] = st
                scores = _dot(qh, kh, _NT) * dm
                o_ref[hd, r, :] = _dot(scores, v) + _dot(qh * xi, st, _NT)
                st_ref[hd] = st * cdec + _dot(v, kh * zeta, _TN)
            return carry

        lax.fori_loop(0, ncb, chunk, 0, unroll=SCAN_UNROLL)

    return pl.pallas_call(
        body, name="ret_fwd", grid=(N_HEADS // 2, t_len // tb),
        in_specs=[pl.BlockSpec((None, tb, LANE), lambda p, t: (CB_RQ + p, t, 0)),
                  pl.BlockSpec((None, tb, LANE), lambda p, t: (CB_RK + p, t, 0)),
                  pl.BlockSpec((2, tb, LANE), lambda p, t: (CB_RV // 2 + p, t, 0)),
                  pl.BlockSpec((tb, LANE), lambda p, t: (t, 0)),
                  pl.BlockSpec((tb, LANE), lambda p, t: (t, 0))],
        out_specs=[pl.BlockSpec((2, tb, LANE), lambda p, t: (p, t, 0)),
                   pl.BlockSpec((2, ncb, LANE, LANE), lambda p, t: (p, t, 0, 0))],
        out_shape=[jax.ShapeDtypeStruct((N_HEADS, t_len, LANE), F32),
                   jax.ShapeDtypeStruct((N_HEADS, nc, LANE, LANE), F32)],
        scratch_shapes=[pltpu.VMEM((2, LANE, LANE), F32)],
        compiler_params=_params(("arbitrary", "arbitrary")),
    )(pb, pb, pb, cos_t, sin_t)


def retention_backward(pb, cos_t, sin_t, do, rsave, dpb, t_len):
    tb = _time_block(t_len)
    ncb, ntb = tb // CHUNK, t_len // tb

    def body(rq_ref, rk_ref, rv_ref, cos_ref, sin_ref, do_ref, rsave_ref, dpb_in,
             dpb_ref, dq_buf, dk_buf, dv_buf, dst_ref, sems):
        del dpb_in
        p, t = pl.program_id(0), pl.program_id(1)

        @pl.when(t == 0)
        def _():
            dst_ref[...] = jnp.zeros_like(dst_ref)

        consts = [_ret_head_consts(2 * p + hd) for hd in range(2)]

        def chunk(i, carry):
            ci = ncb - 1 - i
            r = pl.ds(pl.multiple_of(ci * CHUNK, CHUNK), CHUNK)
            cs, sn = cos_ref[r, :], sin_ref[r, :]
            q = rq_ref[r, :]
            k = rk_ref[r, :]
            q = q * cs + _swap_halves(q) * sn
            k = (k * cs + _swap_halves(k) * sn) * RET_DK ** -0.5
            dq, dk = None, None
            for hd in range(2):
                dm, zeta, xi, cdec, dm_t = consts[hd]
                lm = _lane_mask(hd)
                qh, kh = q * lm, k * lm
                v = rv_ref[hd, r, :]
                g = do_ref[hd, r, :]
                st0 = rsave_ref[hd, ci]
                dst = dst_ref[hd]
                scores_t = _dot(kh, qh, _NT) * dm_t
                dsc = _dot(g, v, _NT) * dm
                dsc_t = _dot(v, g, _NT) * dm_t
                dqh = _dot(dsc, kh) + _dot(g, st0) * xi
                dkh = _dot(dsc_t, qh) + _dot(v, dst) * zeta
                dv_buf[hd, r, :] = _dot(scores_t, g) + _dot(kh * zeta, dst, _NT)
                dst_ref[hd] = dst * cdec + _dot(g, qh * xi, _TN)
                dq = dqh if dq is None else dq + dqh
                dk = dkh if dk is None else dk + dkh
            dk = dk * (RET_DK ** -0.5)
            dq_buf[r, :] = dq * cs - _swap_halves(dq) * sn
            dk_buf[r, :] = dk * cs - _swap_halves(dk) * sn
            return carry

        lax.fori_loop(0, ncb, chunk, 0, unroll=SCAN_UNROLL)
        rows = pl.ds(pl.multiple_of((ntb - 1 - t) * tb, tb), tb)
        copies = [pltpu.make_async_copy(dq_buf, dpb_ref.at[CB_RQ + p, rows], sems.at[0]),
                  pltpu.make_async_copy(dk_buf, dpb_ref.at[CB_RK + p, rows], sems.at[1]),
                  pltpu.make_async_copy(dv_buf, dpb_ref.at[pl.ds(CB_RV + 2 * p, 2), rows], sems.at[2])]
        for cp in copies:
            cp.start()
        for cp in copies:
            cp.wait()

    any_spec = pl.BlockSpec(memory_space=pl.ANY)
    return pl.pallas_call(
        body, name="ret_bwd", grid=(N_HEADS // 2, ntb),
        in_specs=[pl.BlockSpec((None, tb, LANE), lambda p, t: (CB_RQ + p, ntb - 1 - t, 0)),
                  pl.BlockSpec((None, tb, LANE), lambda p, t: (CB_RK + p, ntb - 1 - t, 0)),
                  pl.BlockSpec((2, tb, LANE), lambda p, t: (CB_RV // 2 + p, ntb - 1 - t, 0)),
                  pl.BlockSpec((tb, LANE), lambda p, t: (ntb - 1 - t, 0)),
                  pl.BlockSpec((tb, LANE), lambda p, t: (ntb - 1 - t, 0)),
                  pl.BlockSpec((2, tb, LANE), lambda p, t: (p, ntb - 1 - t, 0)),
                  pl.BlockSpec((2, ncb, LANE, LANE), lambda p, t: (p, ntb - 1 - t, 0, 0)),
                  any_spec],
        out_specs=any_spec,
        out_shape=jax.ShapeDtypeStruct(dpb.shape, F32),
        scratch_shapes=[pltpu.VMEM((tb, LANE), F32), pltpu.VMEM((tb, LANE), F32),
                        pltpu.VMEM((2, tb, LANE), F32), pltpu.VMEM((2, LANE, LANE), F32),
                        pltpu.SemaphoreType.DMA((3,))],
        input_output_aliases={7: 0},
        compiler_params=_params(("arbitrary", "arbitrary")),
    )(pb, pb, pb, cos_t, sin_t, do, rsave, dpb)


def _row_tile(t_len, want):
    return min(want, t_len)


PAIR_CB = 2 * CB_PER_SHARD


def proj_forward(h, wt, t_len):
    tm = _row_tile(t_len, 512)

    def body(h_ref, w_ref, o_ref):
        acc = _dot(h_ref[...], w_ref[...], _NT)
        for jj in range(PAIR_CB):
            o_ref[jj] = acc[:, jj * LANE:(jj + 1) * LANE]

    return pl.pallas_call(
        body, name="proj_fwd", grid=(N_DEV // 2, t_len // tm),
        in_specs=[pl.BlockSpec((tm, D_MODEL), lambda j, i: (i, 0)),
                  pl.BlockSpec((PAIR_CB * LANE, D_MODEL), lambda j, i: (j, 0))],
        out_specs=pl.BlockSpec((PAIR_CB, tm, LANE), lambda j, i: (j, i, 0)),
        out_shape=jax.ShapeDtypeStruct((N_CB, t_len, LANE), F32),
        compiler_params=_params(("arbitrary", "arbitrary")),
    )(h, wt)


def proj_backward_input(dpb, wt, token, x, dy, norm_g, scale1p, t_len):
    tm = _row_tile(t_len, 256)

    def body(a_ref, wt_hbm, token_ref, x_ref, dy_ref, g_ref, sc_ref, gx_ref, vec_ref, w_ref, sem):
        del token_ref
        i = pl.program_id(0)

        @pl.when(i == 0)
        def _():
            cp = pltpu.make_async_copy(wt_hbm, w_ref, sem)
            cp.start()
            cp.wait()

        a = jnp.concatenate([a_ref[jj].astype(_BF) for jj in range(N_CB)], axis=1)
        dhv = _dot(a, w_ref[...])
        xv, g, sc = x_ref[...], g_ref[...], sc_ref[...]
        r = lax.rsqrt(jnp.mean(xv * xv, axis=-1, keepdims=True) + EPS)
        xn = xv * r
        dxn = dhv * (g * sc)
        gx_ref[...] = dy_ref[...] + r * dxn - xn * (r * r) * jnp.mean(xv * dxn, axis=-1, keepdims=True)
        t = dhv * xn
        _acc_rows(vec_ref, i, [jnp.sum(t * sc, axis=0, keepdims=True),
                               jnp.sum(t * g, axis=0, keepdims=True),
                               jnp.sum(dhv, axis=0, keepdims=True)])

    row = pl.BlockSpec((tm, D_MODEL), lambda i: (i, 0))
    return pl.pallas_call(
        body, name="proj_bwd_input", grid=(t_len // tm,),
        in_specs=[pl.BlockSpec((N_CB, tm, LANE), lambda i: (0, i, 0)),
                  pl.BlockSpec(memory_space=pl.ANY),
                  pl.BlockSpec(token.shape, lambda i: (0, 0)),
                  row, row, _vec_spec(), _vec_spec()],
        out_specs=[row, pl.BlockSpec((8, D_MODEL), lambda i: (0, 0))],
        out_shape=[jax.ShapeDtypeStruct((t_len, D_MODEL), F32), jax.ShapeDtypeStruct((8, D_MODEL), F32)],
        scratch_shapes=[pltpu.VMEM(wt.shape, wt.dtype), pltpu.SemaphoreType.DMA],
        compiler_params=_params(("arbitrary",)),
    )(dpb, wt, token, x, dy, norm_g, scale1p)


def proj_backward_weight(h_t, dpb, t_len):
    tk = _row_tile(t_len, 1024)

    def body(h_ref, b_ref, o_ref):
        k = pl.program_id(1)
        b = jnp.concatenate([b_ref[jj].astype(_BF) for jj in range(PAIR_CB)], axis=1)
        part = _dot(h_ref[...], b)

        @pl.when(k == 0)
        def _():
            for s in range(2):
                o_ref[s] = part[:, s * SHARD_IN:(s + 1) * SHARD_IN]

        @pl.when(k > 0)
        def _():
            for s in range(2):
                o_ref[s] = o_ref[s] + part[:, s * SHARD_IN:(s + 1) * SHARD_IN]

    return pl.pallas_call(
        body, name="proj_bwd_weight", grid=(N_DEV // 2, t_len // tk),
        in_specs=[pl.BlockSpec((D_MODEL, tk), lambda j, k: (0, k)),
                  pl.BlockSpec((PAIR_CB, tk, LANE), lambda j, k: (j, k, 0))],
        out_specs=pl.BlockSpec((2, D_MODEL, SHARD_IN), lambda j, k: (j, 0, 0)),
        out_shape=jax.ShapeDtypeStruct((N_DEV, D_MODEL, SHARD_IN), F32),
        compiler_params=_params(("arbitrary", "arbitrary")),
    )(h_t, dpb)


def _vec_spec():
    return pl.BlockSpec((1, D_MODEL), lambda i: (0, 0))


def _acc_rows(ref, i, rows):
    @pl.when(i == 0)
    def _():
        ref[...] = jnp.zeros_like(ref)

    for n, row in enumerate(rows):
        ref[n:n + 1, :] = ref[n:n + 1, :] + row


def adaln_forward(x, norm_g, scale1p, shift, t_len):
    tm = _row_tile(t_len, 512)

    def body(x_ref, g_ref, sc_ref, sh_ref, h_ref, ht_ref):
        xv = x_ref[...]
        r = lax.rsqrt(jnp.mean(xv * xv, axis=-1, keepdims=True) + EPS)
        h = xv * r * g_ref[...] * sc_ref[...] + sh_ref[...]
        h_ref[...] = h.astype(h_ref.dtype)
        ht_ref[...] = h.T.astype(ht_ref.dtype)

    return pl.pallas_call(
        body, name="adaln_fwd", grid=(t_len // tm,),
        in_specs=[pl.BlockSpec((tm, D_MODEL), lambda i: (i, 0)), _vec_spec(), _vec_spec(), _vec_spec()],
        out_specs=[pl.BlockSpec((tm, D_MODEL), lambda i: (i, 0)), pl.BlockSpec((D_MODEL, tm), lambda i: (0, i))],
        out_shape=[jax.ShapeDtypeStruct((t_len, D_MODEL), _BF), jax.ShapeDtypeStruct((D_MODEL, t_len), _BF)],
        compiler_params=_params(("arbitrary",)),
    )(x, norm_g, scale1p, shift)


def _head_norm(o, g):
    r = lax.rsqrt(jnp.mean(o * o, axis=-1, keepdims=True) + EPS)
    return r, o * r * g


def _group_spec(tm, cb):
    return pl.BlockSpec((N_HEADS, tm, LANE), lambda i, cb=cb: (cb // N_HEADS, i, 0))


MID_FINAL_G, MID_GATE, MID_LOSS, MID_HG_G, MID_RET_G = range(5)


def middle(x, target, oa, ob, pb, wout, gate, final_g, hg_g, ret_g, t_len):
    tm = _row_tile(t_len, 256)

    def body(x_ref, t_ref, oa_ref, ob_ref, hz_ref, rz_ref, ga_ref, gb_ref, w_ref, gate_ref, fg_ref, hg_ref, rg_ref,
             dy_ref, doa_ref, dob_ref, dw_ref, vec_ref, dpb_ref, m_scr, dm_scr, keep, bufs, sems):
        i = pl.program_id(0)
        sides = ((oa_ref, hz_ref, ga_ref, hg_ref, doa_ref), (ob_ref, rz_ref, gb_ref, rg_ref, dob_ref))
        for hh in range(N_HEADS):
            ls = slice(hh * LANE, (hh + 1) * LANE)
            acc = None
            for side, (o_ref, z_ref, gt_ref, g_ref, _) in enumerate(sides):
                o = o_ref[hh]
                rr = lax.rsqrt(jnp.mean(o * o, axis=-1, keepdims=True) + EPS)
                orr = o * rr
                zz = z_ref[hh]
                sz = _sigmoid(zz)
                sgt = _sigmoid(gt_ref[hh])
                keep[side, hh, 0] = orr
                keep[side, hh, 1] = sz
                keep[side, hh, 2] = sgt
                keep[side, hh, 3] = jnp.broadcast_to(rr, orr.shape)
                u = sgt * ((orr * g_ref[:, ls]) * (zz * sz))
                acc = u if acc is None else acc + u
            m_scr[:, ls] = acc.astype(m_scr.dtype)
        zv = _dot(m_scr[...], w_ref[...])
        gt, fg = gate_ref[...], fg_ref[...]
        y = x_ref[...] + gt * zv
        r = lax.rsqrt(jnp.mean(y * y, axis=-1, keepdims=True) + EPS)
        yn = y * r
        err = yn * fg - t_ref[...]
        loss = 0.5 * jnp.sum(jnp.mean(err * err, axis=-1, keepdims=True), axis=0, keepdims=True)
        dout = err * (1.0 / D_MODEL)
        gd = dout * fg
        dy = r * gd - yn * (r * r) * jnp.mean(y * gd, axis=-1, keepdims=True)
        dy_ref[...] = dy
        dz = (dy * gt).astype(_BF)
        dm_scr[...] = _dot(dz, w_ref[...], _NT)
        part = _dot(m_scr[...], dz, _TN)

        @pl.when(i == 0)
        def _():
            dw_ref[...] = part

        @pl.when(i > 0)
        def _():
            dw_ref[...] = dw_ref[...] + part

        dg = [[], []]
        for hh in range(N_HEADS):
            ls = slice(hh * LANE, (hh + 1) * LANE)
            dmh = dm_scr[:, ls]
            for side, (o_ref, z_ref, gt_ref, g_ref, do_ref) in enumerate(sides):
                zz, g = z_ref[hh], g_ref[:, ls]
                orr, sz, sgt, rr = keep[side, hh, 0], keep[side, hh, 1], keep[side, hh, 2], keep[side, hh, 3]
                n = orr * g
                silu = zz * sz
                du = dmh * sgt
                bufs[2 + side, hh] = dmh * (n * silu) * (sgt * (1.0 - sgt))
                bufs[side, hh] = du * n * (sz * (1.0 + zz * (1.0 - sz)))
                dn = du * silu
                dg[side].append(jnp.sum(dn * orr, axis=0, keepdims=True))
                gdn = dn * g
                do_ref[hh] = rr * (gdn - orr * jnp.mean(orr * gdn, axis=-1, keepdims=True))
        _acc_rows(vec_ref, i, [jnp.sum(dout * yn, axis=0, keepdims=True),
                               jnp.sum(dy * zv, axis=0, keepdims=True),
                               jnp.broadcast_to(loss, (1, D_MODEL)),
                               jnp.concatenate(dg[0], axis=1), jnp.concatenate(dg[1], axis=1)])
        rows = pl.ds(pl.multiple_of(i * tm, tm), tm)
        copies = [pltpu.make_async_copy(bufs.at[n], dpb_ref.at[pl.ds(cb, N_HEADS), rows], sems.at[n])
                  for n, cb in enumerate((CB_HZ, CB_RZ, CB_GA, CB_GB))]
        for cp in copies:
            cp.start()
        for cp in copies:
            cp.wait()

    row = pl.BlockSpec((tm, D_MODEL), lambda i: (i, 0))
    head = pl.BlockSpec((N_HEADS, tm, LANE), lambda i: (0, i, 0))
    full = pl.BlockSpec((D_MODEL, D_MODEL), lambda i: (0, 0))
    return pl.pallas_call(
        body, name="middle", grid=(t_len // tm,),
        in_specs=[row, row, head, head, _group_spec(tm, CB_HZ), _group_spec(tm, CB_RZ), _group_spec(tm, CB_GA),
                  _group_spec(tm, CB_GB), full, _vec_spec(), _vec_spec(), _vec_spec(), _vec_spec()],
        out_specs=[row, head, head, full, pl.BlockSpec((8, D_MODEL), lambda i: (0, 0)),
                   pl.BlockSpec(memory_space=pl.ANY)],
        out_shape=[jax.ShapeDtypeStruct((t_len, D_MODEL), F32),
                   jax.ShapeDtypeStruct((N_HEADS, t_len, LANE), F32),
                   jax.ShapeDtypeStruct((N_HEADS, t_len, LANE), F32),
                   jax.ShapeDtypeStruct((D_MODEL, D_MODEL), F32),
                   jax.ShapeDtypeStruct((8, D_MODEL), F32),
                   jax.ShapeDtypeStruct((N_CB, t_len, LANE), F32)],
        scratch_shapes=[pltpu.VMEM((tm, D_MODEL), _BF), pltpu.VMEM((tm, D_MODEL), F32),
                        pltpu.VMEM((2, N_HEADS, 4, tm, LANE), F32),
                        pltpu.VMEM((4, N_HEADS, tm, LANE), F32), pltpu.SemaphoreType.DMA((4,))],
        compiler_params=_params(("arbitrary",)),
    )(x, target, oa, ob, pb, pb, pb, pb, wout, gate, final_g, hg_g, ret_g)


def device_step(x, target, mod, lb, wt, wout, norm_g, hg_g, ret_g, final_g, start_exchange=None):
    t_len = x.shape[0]
    shift, scale, gate = mod[:, :D_MODEL], mod[:, D_MODEL:2 * D_MODEL], mod[:, 2 * D_MODEL:]
    scale1p = 1.0 + scale
    cos_t, sin_t = _rope_tables(t_len)
    h, h_t = adaln_forward(x, norm_g, scale1p, shift, t_len)
    pb = proj_forward(h, wt, t_len)
    oa, ssave = hgrn_forward(pb, lb, t_len)
    ob, rsave = retention_forward(pb, cos_t, sin_t, t_len)
    dy, doa, dob, dwout, vec_mid, dpb = middle(x, target, oa, ob, pb, wout, gate, final_g, hg_g, ret_g, t_len)
    dpb, dlb = hgrn_backward(pb, lb, doa, ssave, dpb, t_len)
    dpb = retention_backward(pb, cos_t, sin_t, dob, rsave, dpb, t_len)
    dwin = proj_backward_weight(h_t, dpb, t_len)
    token, pending = start_exchange(dwin, dwout) if start_exchange else (jnp.zeros((8, LANE), F32), None)
    grad_x, vec_ada = proj_backward_input(dpb, wt, token, x, dy, norm_g, scale1p, t_len)
    return grad_x, dwin, dwout, vec_mid, vec_ada, dlb, pending


PACK_ROWS = 16
ROW_NORM_G, ROW_LB, ROW_HG_G, ROW_RET_G, ROW_FINAL_G, ROW_SHIFT, ROW_SCALE, ROW_GATE, ROW_LOSS = range(9)


def _mesh_pos():
    return lax.axis_index("x"), lax.axis_index("y"), lax.axis_index("c")


def _lin(pos):
    return 4 * pos[0] + 2 * pos[1] + pos[2]


def _xor_peer(pos, k):
    return tuple(1 - p if (k >> s) & 1 else p for p, s in zip(pos, (2, 1, 0)))


def _other_chips(pos):
    x, y, _ = pos
    return [(1 - x, y), (x, 1 - y), (1 - x, 1 - y)]


def _remote(src, dst, send_sem, recv_sem, to):
    return pltpu.make_async_remote_copy(src_ref=src, dst_ref=dst, send_sem=send_sem, recv_sem=recv_sem,
                                        device_id=to, device_id_type=MESH)


def pre_exchange(c, w_ada, b_ada, logits):
    def body(c_ref, wada_ref, bada_ref, logit_ref, mod_ref, scall_ref, lb_ref,
             cg_ref, modall_ref, parts_ref, send1, recv1, send2, recv2):
        pos = _mesh_pos()
        cv = c_ref[...]
        slot = lambda p: pl.ds(pl.multiple_of(8 * _lin(p), 8), 8)
        cg_ref[slot(pos), :] = jnp.broadcast_to(cv * _sigmoid(cv), (8, D_MODEL))
        lb_ref[...] = _sigmoid(logit_ref[0:1, :] - logit_ref[1:2, :])
        peers = [_xor_peer(pos, k) for k in range(1, N_DEV)]
        gather = [_remote(cg_ref.at[slot(pos)], cg_ref.at[slot(pos)], send1.at[n], recv1.at[n], p)
                  for n, p in enumerate(peers)]
        for cp in gather:
            cp.start()
        for n, p in enumerate(peers):
            _remote(cg_ref.at[slot(p)], cg_ref.at[slot(p)], send1.at[n], recv1.at[n], p).wait_recv()
        modall_ref[...] = _dot(cg_ref[...], wada_ref[...])
        scatter = [_remote(modall_ref.at[slot(p)], parts_ref.at[slot(pos)], send2.at[n], recv2.at[n], p)
                   for n, p in enumerate(peers)]
        for cp in scatter:
            cp.start()
        parts_ref[slot(pos), :] = modall_ref[slot(pos), :]
        for n, p in enumerate(peers):
            _remote(modall_ref.at[slot(p)], parts_ref.at[slot(p)], send2.at[n], recv2.at[n], p).wait_recv()
        for cp in gather + scatter:
            cp.wait_send()
        for j in range(N_DEV):
            cols = slice(j * SHARD_ADA, (j + 1) * SHARD_ADA)
            mod_ref[:, cols] = parts_ref[8 * j:8 * j + 1, :] + bada_ref[:, cols]
            scall_ref[j:j + 1, :] = cg_ref[8 * j:8 * j + 1, :]

    vmem = pl.BlockSpec(memory_space=pltpu.VMEM)
    return pl.pallas_call(
        body, name="pre_exchange",
        in_specs=[vmem] * 4, out_specs=[vmem] * 3,
        out_shape=[jax.ShapeDtypeStruct((1, 3 * D_MODEL), F32), jax.ShapeDtypeStruct((N_DEV, D_MODEL), F32),
                   jax.ShapeDtypeStruct((1, D_MODEL), F32)],
        scratch_shapes=[pltpu.VMEM((N_DEV * 8, D_MODEL), F32), pltpu.VMEM((N_DEV * 8, SHARD_ADA), F32),
                        pltpu.VMEM((N_DEV * 8, SHARD_ADA), F32)] + [pltpu.SemaphoreType.DMA((N_DEV - 1,))] * 4,
        compiler_params=pltpu.CompilerParams(vmem_limit_bytes=VMEM_LIMIT),
    )(c, w_ada, b_ada, logits)


def weight_gather(win_sh, wout_sh):
    def body(win_ref, wout_ref, wg_ref, woutg_ref, send, recv, local):
        pos = _mesh_pos()
        x, y, c = pos
        sibling = (x, y, 1 - c)
        chips = _other_chips(pos)
        first, passed, mine = [], [], []
        for a, (src, out) in enumerate(((win_ref, wg_ref), (wout_ref, woutg_ref))):
            def copy(k, block, to, src_ref=None, a=a, out=out):
                dst = out.at[_lin(block)]
                return _remote(dst if src_ref is None else src_ref, dst, send.at[7 * a + k], recv.at[7 * a + k], to)
            mine.append(pltpu.make_async_copy(src, out.at[_lin(pos)], local.at[a]))
            first.append(copy(0, pos, sibling, src))
            first += [copy(1 + j, pos, (*chip, c), src) for j, chip in enumerate(chips)]
            passed.append([copy(4 + j, (*chip, c), sibling) for j, chip in enumerate(chips)])
        for cp in mine + first:
            cp.start()
        for a, out in enumerate((wg_ref, woutg_ref)):
            for j, chip in enumerate(chips):
                dst = out.at[_lin((*chip, c))]
                _remote(dst, dst, send.at[7 * a + 1 + j], recv.at[7 * a + 1 + j], pos).wait_recv()
                passed[a][j].start()
        for a, out in enumerate((wg_ref, woutg_ref)):
            dst = out.at[_lin(sibling)]
            _remote(dst, dst, send.at[7 * a], recv.at[7 * a], pos).wait_recv()
            for j, chip in enumerate(chips):
                dst = out.at[_lin((*chip, 1 - c))]
                _remote(dst, dst, send.at[7 * a + 4 + j], recv.at[7 * a + 4 + j], pos).wait_recv()
        for cp in first + passed[0] + passed[1]:
            cp.wait_send()
        for cp in mine:
            cp.wait()

    any_spec = pl.BlockSpec(memory_space=pl.ANY)
    return pl.pallas_call(
        body, name="weight_gather",
        in_specs=[any_spec, any_spec], out_specs=[any_spec, any_spec],
        out_shape=[jax.ShapeDtypeStruct((N_DEV,) + win_sh.shape, win_sh.dtype),
                   jax.ShapeDtypeStruct((N_DEV,) + wout_sh.shape, wout_sh.dtype)],
        scratch_shapes=[pltpu.SemaphoreType.DMA((14,)), pltpu.SemaphoreType.DMA((14,)),
                        pltpu.SemaphoreType.DMA((2,))],
    )(win_sh, wout_sh)


def grad_pair_exchange(g_in, g_out):
    def body(gin_ref, gout_ref, ra_ref, rb_ref, send, recv):
        pos = _mesh_pos()
        x, y, c = pos
        sibling = (x, y, 1 - c)
        copies = []
        for a, (src, dst) in enumerate(((gin_ref, ra_ref), (gout_ref, rb_ref))):
            for q in range(4):
                copies.append(_remote(src.at[2 * q + (1 - c)], dst.at[q], send.at[4 * a + q], recv.at[4 * a + q],
                                      sibling))
        for cp in copies:
            cp.start()
        for cp in copies:
            cp.wait_recv()
        for cp in copies:
            cp.wait_send()

    any_spec = pl.BlockSpec(memory_space=pl.ANY)
    return pl.pallas_call(
        body, name="grad_pair_exchange",
        in_specs=[any_spec, any_spec], out_specs=[any_spec, any_spec],
        out_shape=[jax.ShapeDtypeStruct((4,) + g_in.shape[1:], F32), jax.ShapeDtypeStruct((4,) + g_out.shape[1:], F32)],
        scratch_shapes=[pltpu.SemaphoreType.DMA((8,)), pltpu.SemaphoreType.DMA((8,))],
    )(g_in, g_out)


def pair_sum(g_in, ra, g_out, rb, c_idx):
    tr = 256

    def body(c_ref, gin_ref, ra_ref, gout_ref, rb_ref, sb_ref, sbo_ref):
        del c_ref
        sb_ref[...] = (gin_ref[...] + ra_ref[...]).astype(sb_ref.dtype)
        sbo_ref[...] = gout_ref[...] + rb_ref[...]

    n_i = D_MODEL // tr
    return pl.pallas_call(
        body, name="pair_sum",
        grid_spec=pltpu.PrefetchScalarGridSpec(
            num_scalar_prefetch=1, grid=(4, n_i),
            in_specs=[pl.BlockSpec((None, tr, SHARD_IN), lambda q, i, c: (2 * q + c[0], i, 0)),
                      pl.BlockSpec((None, tr, SHARD_IN), lambda q, i, c: (q, i, 0)),
                      pl.BlockSpec((None, SHARD_OUT // n_i, D_MODEL), lambda q, i, c: (2 * q + c[0], i, 0)),
                      pl.BlockSpec((None, SHARD_OUT // n_i, D_MODEL), lambda q, i, c: (q, i, 0))],
            out_specs=[pl.BlockSpec((None, tr, SHARD_IN), lambda q, i, c: (q, i, 0)),
                       pl.BlockSpec((None, SHARD_OUT // n_i, D_MODEL), lambda q, i, c: (q, i, 0))]),
        out_shape=[jax.ShapeDtypeStruct(ra.shape, _BF), jax.ShapeDtypeStruct(rb.shape, F32)],
        compiler_params=_params(("arbitrary", "arbitrary")),
    )(c_idx, g_in, ra, g_out, rb)


_HBM = pl.BlockSpec(memory_space=pltpu.HBM)
_SEM = pl.BlockSpec(memory_space=pltpu.SEMAPHORE)
_N_CHIP_COPIES = 6


def _chip_copies(sb_ref, sbo_ref, rc_ref, rco_ref, send, recv):
    pos = _mesh_pos()
    copies = []
    for a, (src, dst) in enumerate(((sb_ref, rc_ref), (sbo_ref, rco_ref))):
        for j, chip in enumerate(_other_chips(pos)):
            copies.append(_remote(src.at[2 * chip[0] + chip[1]], dst.at[j], send.at[3 * a + j], recv.at[3 * a + j],
                                  (*chip, pos[2])))
    return copies


def grad_chip_start(sb, sbo):
    def body(sb_ref, sbo_ref, rc_ref, rco_ref, send, recv, sb_thru, sbo_thru, rc_thru, rco_thru, token):
        del sb_thru, sbo_thru, rc_thru, rco_thru
        for cp in _chip_copies(sb_ref, sbo_ref, rc_ref, rco_ref, send, recv):
            cp.start()
        token[...] = jnp.zeros_like(token)

    hbm = lambda a: pltpu.with_memory_space_constraint(a, pltpu.HBM)
    rc = lax.empty((3,) + sb.shape[1:], sb.dtype)
    rco = lax.empty((3,) + sbo.shape[1:], sbo.dtype)
    return pl.pallas_call(
        body, name="grad_chip_start",
        in_specs=[_HBM] * 4,
        out_specs=[_SEM, _SEM, _HBM, _HBM, _HBM, _HBM, pl.BlockSpec(memory_space=pltpu.VMEM)],
        out_shape=[pltpu.SemaphoreType.DMA((_N_CHIP_COPIES,)), pltpu.SemaphoreType.DMA((_N_CHIP_COPIES,)),
                   pltpu.HBM(sb.shape, sb.dtype), pltpu.HBM(sbo.shape, sbo.dtype),
                   pltpu.HBM(rc.shape, rc.dtype), pltpu.HBM(rco.shape, rco.dtype),
                   jax.ShapeDtypeStruct((8, LANE), F32)],
        input_output_aliases={0: 2, 1: 3, 2: 4, 3: 5},
        compiler_params=pltpu.CompilerParams(has_side_effects=pltpu.SideEffectType.DATAFLOW_SIDE_EFFECTING),
    )(hbm(sb), hbm(sbo), hbm(rc), hbm(rco))


def grad_chip_wait(send, recv, sb, sbo, rc, rco, after):
    def body(sb_ref, sbo_ref, rc_ref, rco_ref, send, recv, after_ref, sb_o, sbo_o, rc_o, rco_o):
        del after_ref, sb_o, sbo_o, rc_o, rco_o
        for cp in _chip_copies(sb_ref, sbo_ref, rc_ref, rco_ref, send, recv):
            cp.wait_send()
            cp.wait_recv()

    return pl.pallas_call(
        body, name="grad_chip_wait",
        in_specs=[_HBM] * 4 + [_SEM, _SEM, pl.BlockSpec(memory_space=pl.ANY)],
        out_specs=[_HBM] * 4,
        out_shape=[pltpu.HBM(sb.shape, sb.dtype), pltpu.HBM(sbo.shape, sbo.dtype),
                   pltpu.HBM(rc.shape, rc.dtype), pltpu.HBM(rco.shape, rco.dtype)],
        input_output_aliases={0: 0, 1: 1, 2: 2, 3: 3},
        compiler_params=pltpu.CompilerParams(has_side_effects=pltpu.SideEffectType.DATAFLOW_SIDE_EFFECTING),
    )(sb, sbo, rc, rco, send, recv, after)


def pack_gather(pack):
    def body(pack_ref, packs_ref, psend, precv):
        pos = _mesh_pos()
        me = _lin(pos)
        packs_ref[me] = pack_ref[...]
        peers = [_xor_peer(pos, k) for k in range(1, N_DEV)]
        gather = [_remote(packs_ref.at[me], packs_ref.at[me], psend.at[n], precv.at[n], p) for n, p in enumerate(peers)]
        for cp in gather:
            cp.start()
        for n, p in enumerate(peers):
            _remote(packs_ref.at[_lin(p)], packs_ref.at[_lin(p)], psend.at[n], precv.at[n], p).wait_recv()
        for cp in gather:
            cp.wait_send()

    vmem = pl.BlockSpec(memory_space=pltpu.VMEM)
    return pl.pallas_call(
        body, name="pack_gather", in_specs=[vmem], out_specs=vmem,
        out_shape=jax.ShapeDtypeStruct((N_DEV,) + pack.shape, F32),
        scratch_shapes=[pltpu.SemaphoreType.DMA((N_DEV - 1,)), pltpu.SemaphoreType.DMA((N_DEV - 1,))],
    )(pack)


def pack_rows(vec_mid, vec_ada, dlb):
    def body(mid_ref, ada_ref, dlb_ref, o_ref):
        mid = lambda r: mid_ref[r:r + 1, :]
        rows = [ada_ref[0:1, :], dlb_ref[...], mid(MID_HG_G), mid(MID_RET_G), mid(MID_FINAL_G),
                ada_ref[2:3, :], ada_ref[1:2, :], mid(MID_GATE), mid(MID_LOSS)]
        o_ref[...] = jnp.zeros_like(o_ref)
        for n, row in enumerate(rows):
            o_ref[n:n + 1, :] = row

    vmem = pl.BlockSpec(memory_space=pltpu.VMEM)
    return pl.pallas_call(body, name="pack_rows", in_specs=[vmem] * 3, out_specs=vmem,
                          out_shape=jax.ShapeDtypeStruct((PACK_ROWS, D_MODEL), F32))(vec_mid, vec_ada, dlb)


def _adamw(w, g, m, v):
    m = ADAM_B1 * m + (1.0 - ADAM_B1) * g
    v = ADAM_B2 * v + (1.0 - ADAM_B2) * (g * g)
    m_hat = m / (1.0 - ADAM_B1 ** ADAM_STEP)
    v_hat = v / (1.0 - ADAM_B2 ** ADAM_STEP)
    delta = -ADAM_LR * (m_hat / (jnp.sqrt(v_hat) + ADAM_EPS) + ADAM_WD * w)
    return delta, m, v


def adam_shard(chip_idx, own, parts, w, m, v, name):
    rows, cols = w.shape
    tr = min(rows, 128)

    def body(chip_ref, p0, p1, p2, p3, w_ref, m_ref, v_ref, g_ref, d_ref, nm_ref, nv_ref):
        del chip_ref
        g = ((p0[...].astype(F32) + p1[...].astype(F32)) + p2[...].astype(F32)) + p3[...].astype(F32)
        g_ref[...] = g
        d_ref[...], nm_ref[...], nv_ref[...] = _adamw(w_ref[...], g, m_ref[...], v_ref[...])

    part = lambda q: pl.BlockSpec((None, tr, cols), lambda i, chip, q=q: (q, i, 0))
    tile = pl.BlockSpec((tr, cols), lambda i, chip: (i, 0))
    return pl.pallas_call(
        body, name=name,
        grid_spec=pltpu.PrefetchScalarGridSpec(
            num_scalar_prefetch=1, grid=(rows // tr,),
            in_specs=[pl.BlockSpec((None, tr, cols), lambda i, chip: (chip[0], i, 0)), part(0), part(1), part(2),
                      tile, tile, tile],
            out_specs=[tile] * 4),
        out_shape=[jax.ShapeDtypeStruct(w.shape, F32)] * 4,
        compiler_params=_params(("arbitrary",)),
    )(chip_idx, own, parts, parts, parts, w, m, v)


def adam_ada(sc_t, dmod_all, me_idx, w, m, v):
    def body(me_ref, sc_ref, dm_ref, w_ref, m_ref, v_ref, g_ref, d_ref, nm_ref, nv_ref):
        del me_ref
        g = _dot_f32(sc_ref[...], dm_ref[...])
        g_ref[...] = g
        d_ref[...], nm_ref[...], nv_ref[...] = _adamw(w_ref[...], g, m_ref[...], v_ref[...])

    full = pl.BlockSpec(w.shape, lambda i, me: (0, 0))
    return pl.pallas_call(
        body, name="adam_ada",
        grid_spec=pltpu.PrefetchScalarGridSpec(
            num_scalar_prefetch=1, grid=(1,),
            in_specs=[pl.BlockSpec(sc_t.shape, lambda i, me: (0, 0)),
                      pl.BlockSpec((LANE, SHARD_ADA), lambda i, me: (0, me[0])), full, full, full],
            out_specs=[full] * 4),
        out_shape=[jax.ShapeDtypeStruct(w.shape, F32)] * 4,
        compiler_params=_params(("arbitrary",)),
    )(me_idx, sc_t, dmod_all, w, m, v)


def adam_vectors(packs, lb, params, ms, vs):
    n = len(params)

    def body(*refs):
        packs_ref, lb_ref = refs[0], refs[1]
        w_refs, m_refs, v_refs = refs[2:2 + n], refs[2 + n:2 + 2 * n], refs[2 + 2 * n:2 + 3 * n]
        loss_ref = refs[2 + 3 * n]
        outs = refs[3 + 3 * n:3 + 7 * n]
        tot_ref = refs[3 + 7 * n]
        tot = packs_ref[0]
        for d in range(1, N_DEV):
            tot = tot + packs_ref[d]
        tot_ref[...] = tot
        row = lambda r: tot_ref[r:r + 1, :]
        lbv = lb_ref[...]
        dl0 = row(ROW_LB) * lbv * (1.0 - lbv)
        grads = [[row(ROW_NORM_G)],
                 [jnp.concatenate([row(ROW_SHIFT), row(ROW_SCALE), row(ROW_GATE)], axis=1)],
                 [dl0, -dl0],
                 [row(ROW_HG_G)], [row(ROW_RET_G)], [row(ROW_FINAL_G)]]
        loss_ref[...] = tot_ref[ROW_LOSS:ROW_LOSS + 1, 0:LANE]
        for j, g_rows in enumerate(grads):
            for r, g in enumerate(g_rows):
                rs = slice(r, r + 1)
                d, nm, nv = _adamw(w_refs[j][rs, :], g, m_refs[j][rs, :], v_refs[j][rs, :])
                outs[4 * j][rs, :] = g
                outs[4 * j + 1][rs, :] = d
                outs[4 * j + 2][rs, :] = nm
                outs[4 * j + 3][rs, :] = nv

    vmem = pl.BlockSpec(memory_space=pltpu.VMEM)
    out_shape = [jax.ShapeDtypeStruct((1, LANE), F32)]
    for w in params:
        out_shape += [jax.ShapeDtypeStruct(w.shape, F32)] * 4
    return pl.pallas_call(
        body, name="adam_vectors", in_specs=[vmem] * (2 + 3 * n), out_specs=[vmem] * len(out_shape),
        out_shape=out_shape, scratch_shapes=[pltpu.VMEM((PACK_ROWS, D_MODEL), F32)],
    )(packs, lb, *params, *ms, *vs)


def kernel(x, c, norm_g, w_ada, b_ada, w_in, hg_lb_logits, hg_norm_g, ret_norm_g, w_out, final_g, loss_target, m_norm_g, m_w_ada, m_b_ada, m_w_in, m_hg_lb_logits, m_hg_norm_g, m_ret_norm_g, m_w_out, m_final_g, v_norm_g, v_w_ada, v_b_ada, v_w_in, v_hg_lb_logits, v_hg_norm_g, v_ret_norm_g, v_w_out, v_final_g):
    pos = _mesh_pos()
    me_idx = jnp.reshape(_lin(pos), (1,)).astype(jnp.int32)
    c_idx = jnp.reshape(pos[2], (1,)).astype(jnp.int32)
    vec = lambda a: a.reshape(1, D_MODEL)

    mod, scall, lb = pre_exchange(c, w_ada[0], b_ada, hg_lb_logits)
    wtg, woutg = weight_gather(w_in[0].T.astype(_BF), w_out[0].astype(_BF))
    chip_idx = jnp.reshape(2 * pos[0] + pos[1], (1,)).astype(jnp.int32)

    def start_exchange(dwin, dwout):
        dwout = dwout.reshape(N_DEV, SHARD_OUT, D_MODEL)
        ra, rb = grad_pair_exchange(dwin, dwout)
        sb, sbo = pair_sum(dwin, ra, dwout, rb, c_idx)
        send, recv, sb, sbo, rc, rco, token = grad_chip_start(sb, sbo)
        return token, (send, recv, sb, sbo, rc, rco)

    grad_x, _, _, vec_mid, vec_ada, dlb, pending = device_step(
        x[0], loss_target[0], mod, lb, wtg.reshape(D_IN, D_MODEL), woutg.reshape(D_MODEL, D_MODEL), norm_g,
        hg_norm_g, ret_norm_g, vec(final_g), start_exchange)
    packs = pack_gather(pack_rows(vec_mid, vec_ada, dlb))
    dmod_all = packs[:, ROW_SHIFT:ROW_GATE + 1, :].reshape(N_DEV, 3 * D_MODEL)
    dmod_all = jnp.pad(dmod_all, ((0, LANE - N_DEV), (0, 0)))
    sc_t = jnp.pad(scall.T, ((0, 0), (0, LANE - N_DEV)))
    g_ada, d_ada, nm_ada, nv_ada = adam_ada(sc_t, dmod_all, me_idx, w_ada[0], m_w_ada[0], v_w_ada[0])
    small = adam_vectors(
        packs, lb,
        (norm_g, b_ada, hg_lb_logits, hg_norm_g, ret_norm_g, vec(final_g)),
        (m_norm_g, m_b_ada, m_hg_lb_logits, m_hg_norm_g, m_ret_norm_g, vec(m_final_g)),
        (v_norm_g, v_b_ada, v_hg_lb_logits, v_hg_norm_g, v_ret_norm_g, vec(v_final_g)))
    loss = small[0][0, 0]
    sb, sbo, rc, rco = grad_chip_wait(*pending, small[0])
    g_in, d_in, nm_in, nv_in = adam_shard(chip_idx, sb, rc, w_in[0], m_w_in[0], v_w_in[0], "adam_w_in")
    g_out, d_out, nm_out, nv_out = adam_shard(chip_idx, sbo, rco, w_out[0], m_w_out[0], v_w_out[0], "adam_w_out")
    (g_ng, d_ng, nm_ng, nv_ng), (g_b, d_b, nm_b, nv_b), (g_lb, d_lb, nm_lb, nv_lb), (g_hg, d_hg, nm_hg, nv_hg), \
        (g_rg, d_rg, nm_rg, nv_rg), (g_fg, d_fg, nm_fg, nv_fg) = [small[1 + 4 * j:5 + 4 * j] for j in range(6)]
    flat = lambda a: a.reshape(D_MODEL)

    def group(ng, ada, b, win, lbl, hg, rg, wo, fg):
        return (ng, ada[None], b, win[None], lbl, hg, rg, wo[None], flat(fg))

    return (loss, grad_x[None],
            *group(g_ng, g_ada, g_b, g_in, g_lb, g_hg, g_rg, g_out, g_fg),
            *group(d_ng, d_ada, d_b, d_in, d_lb, d_hg, d_rg, d_out, d_fg),
            *group(nm_ng, nm_ada, nm_b, nm_in, nm_lb, nm_hg, nm_rg, nm_out, nm_fg),
            *group(nv_ng, nv_ada, nv_b, nv_in, nv_lb, nv_hg, nv_rg, nv_out, nv_fg))
```

```python
import functools

import numpy as np
import jax
import jax.numpy as jnp
from jax import lax
from jax.experimental import pallas as pl
from jax.experimental.pallas import tpu as pltpu

F32 = jnp.float32
_BF = jnp.bfloat16

D_MODEL = 1024
N_HEADS = 8
LANE = 128
RET_DK = 64
D_IN = 9216
N_DEV = 8
SHARD_IN = D_IN // N_DEV
SHARD_ADA = 3 * D_MODEL // N_DEV
SHARD_OUT = D_MODEL // N_DEV
N_CB = D_IN // LANE
CB_PER_SHARD = SHARD_IN // LANE
CHUNK = 128
N_LEVELS = 7
EPS = 1e-6
ROPE_BASE = 10000.0
CB_HQ, CB_HF, CB_HI, CB_HZ, CB_RQ, CB_RK, CB_RV, CB_RZ, CB_GA, CB_GB = 0, 8, 16, 24, 32, 36, 40, 48, 56, 64
VMEM_LIMIT = 56 * 1024 * 1024

ADAM_LR, ADAM_B1, ADAM_B2, ADAM_EPS, ADAM_WD, ADAM_STEP = 0.001, 0.9, 0.999, 1e-08, 0.01, 10

_NN = (((1,), (0,)), ((), ()))
_NT = (((1,), (1,)), ((), ()))
_TN = (((0,), (0,)), ((), ()))
MESH = pl.DeviceIdType.MESH


def _dot(a, b, dims=_NN):
    return lax.dot_general(a.astype(_BF), b.astype(_BF), dims, preferred_element_type=F32)


def _split2(a):
    hi = a.astype(_BF)
    lo = (a - hi.astype(F32)).astype(_BF)
    return jnp.concatenate([hi, lo], axis=1)


def _dot_sel(sel, a):
    n = a.shape[1]
    r = lax.dot_general(sel.astype(_BF), _split2(a), _NN, preferred_element_type=F32)
    return r[:, :n] + r[:, n:]


def _dot_f32(a, b):
    def pieces(v):
        p1 = v.astype(_BF)
        r1 = v - p1.astype(F32)
        p2 = r1.astype(_BF)
        p3 = (r1 - p2.astype(F32)).astype(_BF)
        return p1, p2, p3
    a1, a2, a3 = pieces(a)
    b1, b2, b3 = pieces(b)
    d = lambda u, v: lax.dot_general(u, v, _NN, preferred_element_type=F32)
    return ((d(a1, b3) + d(a2, b2) + d(a3, b1)) + (d(a1, b2) + d(a2, b1))) + d(a1, b1)


def _sigmoid(v):
    return 1.0 / (1.0 + jnp.exp(-v))


def _params(sem=None):
    return pltpu.CompilerParams(dimension_semantics=sem, vmem_limit_bytes=VMEM_LIMIT)


def _hgrn_consts():
    c, nl = CHUNK, N_LEVELS
    t = np.arange(c)[:, None]
    j = np.arange(c)[None, :]
    sel = [j <= t]
    masks = [j == t]
    for l in range(1, nl + 1):
        m = ((t >> l) << l) + (1 << (l - 1)) - 1
        sec = t > m
        sel.append(np.where(sec, (j > m) & (j <= t), (j > t) & (j <= m)))
        same = (t >> l) == (j >> l)
        masks.append(same & sec & (j <= m))
    sel.append(j > t)
    sel = np.concatenate(sel, 0).astype(np.float32)
    masks = np.stack(masks).astype(np.float32)
    sgn = np.stack([np.where((t & (1 << (l - 1))) != 0, 1.0, -1.0) * np.ones((1, LANE)) for l in range(3, nl + 1)])
    return dict(tri=jnp.asarray(sel[:c], _BF),
                lvl=jnp.asarray(masks, F32),
                sgn=jnp.asarray(sgn, F32),
                sel_t=jnp.asarray(sel.T, _BF),
                lvl_b=jnp.asarray(masks, _BF),
                lvlt_b=jnp.asarray(np.swapaxes(masks, 1, 2), _BF))


def _level_exponents(b, logf, b_scr, sgn_ref):
    c = CHUNK
    b_scr[...] = b
    row = lax.broadcasted_iota(jnp.int32, (c, LANE), 0)
    nxt = pltpu.roll(logf, c - 1, 0)
    prv = pltpu.roll(logf, 1, 0)
    r4 = row & 3
    out = [jnp.where((row & 1) == 1, logf, 0.0),
           jnp.where(r4 == 0, nxt, jnp.where(r4 == 1, 0.0, jnp.where(r4 == 2, logf, logf + prv)))]
    for l in range(3, N_LEVELS + 1):
        size, half = 1 << l, 1 << (l - 1)
        ref = jnp.concatenate([jnp.broadcast_to(b_scr[i * size + half - 1:i * size + half, :], (size, LANE))
                               for i in range(c // size)], axis=0)
        out.append((b - ref) * sgn_ref[l - 3])
    return out


def _hgrn_chunk(hq, hf, hi, lbv, tri_ref, sgn_ref, b_scr):
    sq = _sigmoid(hq)
    q = hq * sq
    sg = _sigmoid(hf)
    omlb = 1.0 - lbv
    f = lbv + omlb * sg
    k = 1.0 - f
    logf = jnp.log(f)
    b = _dot_sel(tri_ref[...], logf)
    bc = jnp.sum(logf, axis=0, keepdims=True)
    lev = [None] + [jnp.exp(e) for e in _level_exponents(b, logf, b_scr, sgn_ref)]
    return dict(sq=sq, q=q, sg=sg, omlb=omlb, f=f, k=k, v=hi, eb=jnp.exp(b), erem=jnp.exp(bc - b), ebc=jnp.exp(bc),
                lev=lev)


def _blockdiag(a, b):
    z = jnp.zeros_like(a)
    return jnp.concatenate([jnp.concatenate([a, z], axis=1), jnp.concatenate([z, b], axis=1)], axis=0)


def _level_operands(a):
    q, k = a["q"], a["k"]
    ql = [q.astype(_BF)] + [(q * a["lev"][l]).astype(_BF) for l in range(1, N_LEVELS + 1)]
    kl = [k.astype(_BF)] + [(k * a["lev"][l]).astype(_BF) for l in range(1, N_LEVELS + 1)]
    pairs = range(0, N_LEVELS + 1, 2)
    return ([jnp.concatenate([ql[l], ql[l + 1]], axis=1) for l in pairs], [_blockdiag(kl[l], kl[l + 1]) for l in pairs],
            ql, kl)


def _hgrn_scores(a, lvl_ref, q_pairs, k_diags):
    acc = None
    for n, (qp, kd) in enumerate(zip(q_pairs, k_diags)):
        both = lax.dot_general(qp, kd, _NT, preferred_element_type=F32)
        part = lvl_ref[2 * n] * both[:, :CHUNK] + lvl_ref[2 * n + 1] * both[:, CHUNK:]
        acc = part if acc is None else acc + part
    return acc


SCAN_UNROLL = 2


def _writeback_reserve(step, make_copies):
    slot = step % 2

    @pl.when(step >= 2)
    def _():
        for cp in make_copies(slot):
            cp.wait()

    return slot


def _writeback_commit(step, n_steps, slot, make_copies):
    for cp in make_copies(slot):
        cp.start()

    @pl.when(step == n_steps - 1)
    def _():
        for cp in make_copies(slot):
            cp.wait()
        if n_steps > 1:
            for cp in make_copies(1 - slot):
                cp.wait()


def _resident(const):
    zeros = (0,) * const.ndim
    return pl.BlockSpec(const.shape, lambda p, t: zeros)


def _time_block(t_len):
    return min(t_len, 1024)


def hgrn_forward(pb, lb, t_len):
    nc = t_len // CHUNK
    tb = _time_block(t_len)
    ncb = tb // CHUNK
    consts = _hgrn_consts()
    operands = [consts[n] for n in ("tri", "lvl", "sgn")]

    def body(hq_ref, hf_ref, hi_ref, lb_ref, tri_ref, lvl_ref, sgn_ref, o_ref, ssave_ref, st_ref, b_scr):
        @pl.when(pl.program_id(1) == 0)
        def _():
            st_ref[...] = jnp.zeros_like(st_ref)

        def chunk(ci, carry):
            r = pl.ds(pl.multiple_of(ci * CHUNK, CHUNK), CHUNK)
            for hd in range(2):
                lbv = lb_ref[:, hd * LANE:(hd + 1) * LANE]
                a = _hgrn_chunk(hq_ref[hd, r, :], hf_ref[hd, r, :], hi_ref[hd, r, :], lbv, tri_ref, sgn_ref,
                                b_scr.at[hd])
                q_pairs, k_diags, _, _ = _level_operands(a)
                st = st_ref[hd]
                ssave_ref[hd, ci] = st
                scores = _hgrn_scores(a, lvl_ref, q_pairs, k_diags)
                o_ref[hd, r, :] = _dot(a["q"] * a["eb"], st, _NT) + _dot(scores, a["v"])
                st_ref[hd] = st * a["ebc"] + _dot(a["v"], a["k"] * a["erem"], _TN)
            return carry

        lax.fori_loop(0, ncb, chunk, 0)

    pair = lambda base: pl.BlockSpec((2, tb, LANE), lambda p, t, base=base: (base // 2 + p, t, 0))
    return pl.pallas_call(
        body, name="hgrn_fwd", grid=(N_HEADS // 2, t_len // tb),
        in_specs=[pair(CB_HQ), pair(CB_HF), pair(CB_HI),
                  pl.BlockSpec((1, 2 * LANE), lambda p, t: (0, p))] + [_resident(c) for c in operands],
        out_specs=[pl.BlockSpec((2, tb, LANE), lambda p, t: (p, t, 0)),
                   pl.BlockSpec((2, ncb, LANE, LANE), lambda p, t: (p, t, 0, 0))],
        out_shape=[jax.ShapeDtypeStruct((N_HEADS, t_len, LANE), F32),
                   jax.ShapeDtypeStruct((N_HEADS, nc, LANE, LANE), F32)],
        scratch_shapes=[pltpu.VMEM((2, LANE, LANE), F32), pltpu.VMEM((2, CHUNK, LANE), F32)],
        compiler_params=_params(("arbitrary", "arbitrary")),
    )(pb, pb, pb, lb, *operands)


def hgrn_backward(pb, lb, do, ssave, dpb, t_len):
    tb = _time_block(t_len)
    ncb, ntb = tb // CHUNK, t_len // tb
    consts = _hgrn_consts()
    operands = [consts[n] for n in ("tri", "lvl", "sgn", "sel_t", "lvl_b", "lvlt_b")]

    def body(hq_ref, hf_ref, hi_ref, lb_ref, do_ref, ssave_ref, tri_ref, lvl_ref, sgn_ref, selt_ref, lvlb_ref,
             lvltb_ref, dpb_in, dpb_ref, dlb_ref, dq_buf, df_buf, di_buf, dst_ref, b_scr, sems):
        del dpb_in
        p, t = pl.program_id(0), pl.program_id(1)
        step = p * ntb + t
        rows = pl.ds(pl.multiple_of((ntb - 1 - t) * tb, tb), tb)

        def out_copies(sl):
            return [pltpu.make_async_copy(buf.at[sl], dpb_ref.at[pl.ds(base + 2 * p, 2), rows], sems.at[sl, n])
                    for n, (buf, base) in enumerate(((dq_buf, CB_HQ), (df_buf, CB_HF), (di_buf, CB_HI)))]

        slot = _writeback_reserve(step, out_copies)

        @pl.when(t == 0)
        def _():
            dst_ref[...] = jnp.zeros_like(dst_ref)
            dlb_ref[...] = jnp.zeros_like(dlb_ref)

        def chunk(i, carry):
            ci = ncb - 1 - i
            r = pl.ds(pl.multiple_of(ci * CHUNK, CHUNK), CHUNK)
            for hd in range(2):
                head_chunk(hd, ci, r)
            return carry

        def head_chunk(hd, ci, r):
            lbv = lb_ref[:, hd * LANE:(hd + 1) * LANE]
            hq = hq_ref[hd, r, :]
            a = _hgrn_chunk(hq, hf_ref[hd, r, :], hi_ref[hd, r, :], lbv, tri_ref, sgn_ref, b_scr.at[hd])
            q_pairs, k_diags, ql, kl = _level_operands(a)
            q, k, v = a["q"], a["k"], a["v"]
            g = do_ref[hd, r, :]
            st0 = ssave_ref[hd, ci]
            dst = dst_ref[hd]
            scores = _hgrn_scores(a, lvl_ref, q_pairs, k_diags)
            da = _dot(g, v, _NT)
            da_t = _dot(v, g, _NT)
            kb = k * a["erem"]
            qb = q * a["eb"]
            dv = _dot(scores, g, _TN) + _dot(kb, dst, _NT)
            dq_inter = _dot(g, st0) * a["eb"]
            dk_state = _dot(v, dst) * a["erem"]
            dq, dk = dq_inter, dk_state
            de = [q * dq_inter]
            da_b, dat_b = da.astype(_BF), da_t.astype(_BF)
            for n in range(len(k_diags)):
                l0, l1 = 2 * n, 2 * n + 1
                da_pair = jnp.concatenate([lvlb_ref[l0] * da_b, lvlb_ref[l1] * da_b], axis=1)
                dat_pair = jnp.concatenate([lvltb_ref[l0] * dat_b, lvltb_ref[l1] * dat_b], axis=1)
                dq_both = lax.dot_general(da_pair, k_diags[n], _NN, preferred_element_type=F32)
                dk_both = lax.dot_general(dat_pair, _blockdiag(ql[l0], ql[l1]), _NN, preferred_element_type=F32)
                for l, cols in ((l0, slice(0, LANE)), (l1, slice(LANE, 2 * LANE))):
                    dql, dkl = dq_both[:, cols], dk_both[:, cols]
                    if l > 0:
                        e = a["lev"][l]
                        dql, dkl = dql * e, dkl * e
                        de.append(q * dql + k * dkl)
                    dq = dq + dql
                    dk = dk + dkl
            de.append(k * dk_state)
            dst_ref[hd] = dst * a["ebc"] + _dot(g, qb, _TN)
            dbc = jnp.sum(dst * st0, axis=0, keepdims=True) * a["ebc"]
            de2 = lax.dot_general(selt_ref[...], _split2(jnp.concatenate(de, axis=0)), _NN,
                                  preferred_element_type=F32)
            dlogf = de2[:, :LANE] + de2[:, LANE:] + dbc
            sq, sg = a["sq"], a["sg"]
            df = dlogf / a["f"] - dk
            dq_buf[slot, hd, r, :] = dq * (sq * (1.0 + hq * (1.0 - sq)))
            df_buf[slot, hd, r, :] = df * a["omlb"] * sg * (1.0 - sg)
            di_buf[slot, hd, r, :] = dv
            cols = slice(hd * LANE, (hd + 1) * LANE)
            dlb_ref[:, cols] = dlb_ref[:, cols] + jnp.sum(df * (1.0 - sg), axis=0, keepdims=True)

        lax.fori_loop(0, ncb, chunk, 0)
        _writeback_commit(step, (N_HEADS // 2) * ntb, slot, out_copies)

    pair = lambda base: pl.BlockSpec((2, tb, LANE), lambda p, t, base=base: (base // 2 + p, ntb - 1 - t, 0))
    any_spec = pl.BlockSpec(memory_space=pl.ANY)
    return pl.pallas_call(
        body, name="hgrn_bwd", grid=(N_HEADS // 2, ntb),
        in_specs=[pair(CB_HQ), pair(CB_HF), pair(CB_HI),
                  pl.BlockSpec((1, 2 * LANE), lambda p, t: (0, p)),
                  pair(0),
                  pl.BlockSpec((2, ncb, LANE, LANE), lambda p, t: (p, ntb - 1 - t, 0, 0))]
        + [_resident(c) for c in operands] + [any_spec],
        out_specs=[any_spec, pl.BlockSpec((1, 2 * LANE), lambda p, t: (0, p))],
        out_shape=[jax.ShapeDtypeStruct(dpb.shape, F32), jax.ShapeDtypeStruct((1, D_MODEL), F32)],
        scratch_shapes=[pltpu.VMEM((2, 2, tb, LANE), F32)] * 3 + [
            pltpu.VMEM((2, LANE, LANE), F32), pltpu.VMEM((2, CHUNK, LANE), F32), pltpu.SemaphoreType.DMA((2, 3))],
        input_output_aliases={6 + len(operands): 0},
        compiler_params=_params(("arbitrary", "arbitrary")),
    )(pb, pb, pb, lb, do, ssave, *operands, dpb)


def _rope_tables(t_len):
    half = RET_DK // 2
    inv_freq = 1.0 / (ROPE_BASE ** jnp.linspace(0.0, 1.0, half, dtype=F32))
    ang = jnp.arange(t_len, dtype=jnp.int32).astype(F32)[:, None] * inv_freq[None, :]
    cos, sin = jnp.cos(ang), jnp.sin(ang)
    cos_t = jnp.concatenate([cos, cos, cos, cos], axis=1)
    sin_t = jnp.concatenate([-sin, sin, -sin, sin], axis=1)
    return cos_t, sin_t


def _swap_halves(v):
    half = RET_DK // 2
    lane = lax.broadcasted_iota(jnp.int32, v.shape, 1)
    first = (lane & (RET_DK - 1)) < half
    return jnp.where(first, pltpu.roll(v, LANE - half, 1), pltpu.roll(v, half, 1))


def _ret_head_consts(hidx):
    c = CHUNK
    hf = jnp.full((1, LANE), hidx, jnp.int32).astype(F32)
    lg = jnp.log(1.0 - jnp.exp(-(5.0 + hf) * np.float32(np.log(2.0))))
    row = lax.broadcasted_iota(jnp.int32, (c, c), 0)
    col = lax.broadcasted_iota(jnp.int32, (c, c), 1)
    rel = (row - col).astype(F32)
    dm = jnp.where(rel >= 0, jnp.exp(lg[:, :1] * jnp.maximum(rel, 0.0)), 0.0)
    dm_t = jnp.where(rel <= 0, jnp.exp(lg[:, :1] * jnp.maximum(-rel, 0.0)), 0.0)
    idx = lax.broadcasted_iota(jnp.int32, (c, LANE), 0).astype(F32)
    zeta = jnp.exp(lg * (c - 1.0 - idx))
    xi = jnp.exp(lg * (idx + 1.0))
    cdec = jnp.exp(lg * float(c))
    return dm, zeta, xi, cdec, dm_t


def _lane_mask(which):
    lane = lax.broadcasted_iota(jnp.int32, (1, LANE), 1)
    return ((lane // RET_DK) == which).astype(F32)


def retention_forward(pb, cos_t, sin_t, t_len):
    nc = t_len // CHUNK

    tb = _time_block(t_len)
    ncb = tb // CHUNK

    def body(rq_ref, rk_ref, rv_ref, cos_ref, sin_ref, o_ref, rsave_ref, st_ref):
        p = pl.program_id(0)

        @pl.when(pl.program_id(1) == 0)
        def _():
            st_ref[...] = jnp.zeros_like(st_ref)

        consts = [_ret_head_consts(2 * p + hd) for hd in range(2)]

        def chunk(ci, carry):
            r = pl.ds(pl.multiple_of(ci * CHUNK, CHUNK), CHUNK)
            cs, sn = cos_ref[r, :], sin_ref[r, :]
            q = rq_ref[r, :]
            k = rk_ref[r, :]
            q = q * cs + _swap_halves(q) * sn
            k = (k * cs + _swap_halves(k) * sn) * RET_DK ** -0.5
            for hd in range(2):
                dm, zeta, xi, cdec, _ = consts[hd]
                lm = _lane_mask(hd)
                qh, kh = q * lm, k * lm
                v = rv_ref[hd, r, :]
                st = st_ref[hd]
                rsave_ref[hd, ci] = st
                scores = _dot(qh, kh, _NT) * dm
                o_ref[hd, r, :] = _dot(scores, v) + _dot(qh * xi, st, _NT)
                st_ref[hd] = st * cdec + _dot(v, kh * zeta, _TN)
            return carry

        lax.fori_loop(0, ncb, chunk, 0, unroll=SCAN_UNROLL)

    return pl.pallas_call(
        body, name="ret_fwd", grid=(N_HEADS // 2, t_len // tb),
        in_specs=[pl.BlockSpec((None, tb, LANE), lambda p, t: (CB_RQ + p, t, 0)),
                  pl.BlockSpec((None, tb, LANE), lambda p, t: (CB_RK + p, t, 0)),
                  pl.BlockSpec((2, tb, LANE), lambda p, t: (CB_RV // 2 + p, t, 0)),
                  pl.BlockSpec((tb, LANE), lambda p, t: (t, 0)),
                  pl.BlockSpec((tb, LANE), lambda p, t: (t, 0))],
        out_specs=[pl.BlockSpec((2, tb, LANE), lambda p, t: (p, t, 0)),
                   pl.BlockSpec((2, ncb, LANE, LANE), lambda p, t: (p, t, 0, 0))],
        out_shape=[jax.ShapeDtypeStruct((N_HEADS, t_len, LANE), F32),
                   jax.ShapeDtypeStruct((N_HEADS, nc, LANE, LANE), F32)],
        scratch_shapes=[pltpu.VMEM((2, LANE, LANE), F32)],
        compiler_params=_params(("arbitrary", "arbitrary")),
    )(pb, pb, pb, cos_t, sin_t)


def retention_backward(pb, cos_t, sin_t, do, rsave, dpb, t_len):
    tb = _time_block(t_len)
    ncb, ntb = tb // CHUNK, t_len // tb

    def body(rq_ref, rk_ref, rv_ref, cos_ref, sin_ref, do_ref, rsave_ref, dpb_in,
             dpb_ref, dq_buf, dk_buf, dv_buf, dst_ref, sems):
        del dpb_in
        p, t = pl.program_id(0), pl.program_id(1)
        step = p * ntb + t
        rows = pl.ds(pl.multiple_of((ntb - 1 - t) * tb, tb), tb)

        def out_copies(sl):
            return [pltpu.make_async_copy(dq_buf.at[sl], dpb_ref.at[CB_RQ + p, rows], sems.at[sl, 0]),
                    pltpu.make_async_copy(dk_buf.at[sl], dpb_ref.at[CB_RK + p, rows], sems.at[sl, 1]),
                    pltpu.make_async_copy(dv_buf.at[sl], dpb_ref.at[pl.ds(CB_RV + 2 * p, 2), rows], sems.at[sl, 2])]

        slot = _writeback_reserve(step, out_copies)

        @pl.when(t == 0)
        def _():
            dst_ref[...] = jnp.zeros_like(dst_ref)

        consts = [_ret_head_consts(2 * p + hd) for hd in range(2)]

        def chunk(i, carry):
            ci = ncb - 1 - i
            r = pl.ds(pl.multiple_of(ci * CHUNK, CHUNK), CHUNK)
            cs, sn = cos_ref[r, :], sin_ref[r, :]
            q = rq_ref[r, :]
            k = rk_ref[r, :]
            q = q * cs + _swap_halves(q) * sn
            k = (k * cs + _swap_halves(k) * sn) * RET_DK ** -0.5
            dq, dk = None, None
            for hd in range(2):
                dm, zeta, xi, cdec, dm_t = consts[hd]
                lm = _lane_mask(hd)
                qh, kh = q * lm, k * lm
                v = rv_ref[hd, r, :]
                g = do_ref[hd, r, :]
                st0 = rsave_ref[hd, ci]
                dst = dst_ref[hd]
                scores_t = _dot(kh, qh, _NT) * dm_t
                dsc = _dot(g, v, _NT) * dm
                dsc_t = _dot(v, g, _NT) * dm_t
                dqh = _dot(dsc, kh) + _dot(g, st0) * xi
                dkh = _dot(dsc_t, qh) + _dot(v, dst) * zeta
                dv_buf[slot, hd, r, :] = _dot(scores_t, g) + _dot(kh * zeta, dst, _NT)
                dst_ref[hd] = dst * cdec + _dot(g, qh * xi, _TN)
                dq = dqh if dq is None else dq + dqh
                dk = dkh if dk is None else dk + dkh
            dk = dk * (RET_DK ** -0.5)
            dq_buf[slot, r, :] = dq * cs - _swap_halves(dq) * sn
            dk_buf[slot, r, :] = dk * cs - _swap_halves(dk) * sn
            return carry

        lax.fori_loop(0, ncb, chunk, 0, unroll=SCAN_UNROLL)
        _writeback_commit(step, (N_HEADS // 2) * ntb, slot, out_copies)

    any_spec = pl.BlockSpec(memory_space=pl.ANY)
    return pl.pallas_call(
        body, name="ret_bwd", grid=(N_HEADS // 2, ntb),
        in_specs=[pl.BlockSpec((None, tb, LANE), lambda p, t: (CB_RQ + p, ntb - 1 - t, 0)),
                  pl.BlockSpec((None, tb, LANE), lambda p, t: (CB_RK + p, ntb - 1 - t, 0)),
                  pl.BlockSpec((2, tb, LANE), lambda p, t: (CB_RV // 2 + p, ntb - 1 - t, 0)),
                  pl.BlockSpec((tb, LANE), lambda p, t: (ntb - 1 - t, 0)),
                  pl.BlockSpec((tb, LANE), lambda p, t: (ntb - 1 - t, 0)),
                  pl.BlockSpec((2, tb, LANE), lambda p, t: (p, ntb - 1 - t, 0)),
                  pl.BlockSpec((2, ncb, LANE, LANE), lambda p, t: (p, ntb - 1 - t, 0, 0)),
                  any_spec],
        out_specs=any_spec,
        out_shape=jax.ShapeDtypeStruct(dpb.shape, F32),
        scratch_shapes=[pltpu.VMEM((2, tb, LANE), F32), pltpu.VMEM((2, tb, LANE), F32),
                        pltpu.VMEM((2, 2, tb, LANE), F32), pltpu.VMEM((2, LANE, LANE), F32),
                        pltpu.SemaphoreType.DMA((2, 3))],
        input_output_aliases={7: 0},
        compiler_params=_params(("arbitrary", "arbitrary")),
    )(pb, pb, pb, cos_t, sin_t, do, rsave, dpb)


def _row_tile(t_len, want):
    return min(want, t_len)


PAIR_CB = 2 * CB_PER_SHARD


def proj_forward(h, wt, t_len):
    tm = _row_tile(t_len, 512)

    def body(h_ref, w_ref, o_ref):
        acc = _dot(h_ref[...], w_ref[...], _NT)
        for jj in range(PAIR_CB):
            o_ref[jj] = acc[:, jj * LANE:(jj + 1) * LANE]

    return pl.pallas_call(
        body, name="proj_fwd", grid=(N_DEV // 2, t_len // tm),
        in_specs=[pl.BlockSpec((tm, D_MODEL), lambda j, i: (i, 0)),
                  pl.BlockSpec((PAIR_CB * LANE, D_MODEL), lambda j, i: (j, 0))],
        out_specs=pl.BlockSpec((PAIR_CB, tm, LANE), lambda j, i: (j, i, 0)),
        out_shape=jax.ShapeDtypeStruct((N_CB, t_len, LANE), F32),
        compiler_params=_params(("arbitrary", "arbitrary")),
    )(h, wt)


def proj_backward_input(dpb, wt, token, x, dy, norm_g, scale1p, t_len):
    tm = _row_tile(t_len, 256)

    def body(a_ref, wt_hbm, token_ref, x_ref, dy_ref, g_ref, sc_ref, gx_ref, vec_ref, w_ref, sem):
        del token_ref
        i = pl.program_id(0)

        @pl.when(i == 0)
        def _():
            cp = pltpu.make_async_copy(wt_hbm, w_ref, sem)
            cp.start()
            cp.wait()

        a = jnp.concatenate([a_ref[jj].astype(_BF) for jj in range(N_CB)], axis=1)
        dhv = _dot(a, w_ref[...])
        xv, g, sc = x_ref[...], g_ref[...], sc_ref[...]
        r = lax.rsqrt(jnp.mean(xv * xv, axis=-1, keepdims=True) + EPS)
        xn = xv * r
        dxn = dhv * (g * sc)
        gx_ref[...] = dy_ref[...] + r * dxn - xn * (r * r) * jnp.mean(xv * dxn, axis=-1, keepdims=True)
        t = dhv * xn
        _acc_rows(vec_ref, i, [jnp.sum(t * sc, axis=0, keepdims=True),
                               jnp.sum(t * g, axis=0, keepdims=True),
                               jnp.sum(dhv, axis=0, keepdims=True)])

    row = pl.BlockSpec((tm, D_MODEL), lambda i: (i, 0))
    return pl.pallas_call(
        body, name="proj_bwd_input", grid=(t_len // tm,),
        in_specs=[pl.BlockSpec((N_CB, tm, LANE), lambda i: (0, i, 0)),
                  pl.BlockSpec(memory_space=pl.ANY),
                  pl.BlockSpec(token.shape, lambda i: (0, 0)),
                  row, row, _vec_spec(), _vec_spec()],
        out_specs=[row, pl.BlockSpec((8, D_MODEL), lambda i: (0, 0))],
        out_shape=[jax.ShapeDtypeStruct((t_len, D_MODEL), F32), jax.ShapeDtypeStruct((8, D_MODEL), F32)],
        scratch_shapes=[pltpu.VMEM(wt.shape, wt.dtype), pltpu.SemaphoreType.DMA],
        compiler_params=_params(("arbitrary",)),
    )(dpb, wt, token, x, dy, norm_g, scale1p)


def proj_backward_weight(h_t, dpb, t_len):
    tk = _row_tile(t_len, 1024)

    def body(h_ref, b_ref, o_ref):
        k = pl.program_id(1)
        b = jnp.concatenate([b_ref[jj].astype(_BF) for jj in range(PAIR_CB)], axis=1)
        part = _dot(h_ref[...], b)

        @pl.when(k == 0)
        def _():
            for s in range(2):
                o_ref[s] = part[:, s * SHARD_IN:(s + 1) * SHARD_IN]

        @pl.when(k > 0)
        def _():
            for s in range(2):
                o_ref[s] = o_ref[s] + part[:, s * SHARD_IN:(s + 1) * SHARD_IN]

    return pl.pallas_call(
        body, name="proj_bwd_weight", grid=(N_DEV // 2, t_len // tk),
        in_specs=[pl.BlockSpec((D_MODEL, tk), lambda j, k: (0, k)),
                  pl.BlockSpec((PAIR_CB, tk, LANE), lambda j, k: (j, k, 0))],
        out_specs=pl.BlockSpec((2, D_MODEL, SHARD_IN), lambda j, k: (j, 0, 0)),
        out_shape=jax.ShapeDtypeStruct((N_DEV, D_MODEL, SHARD_IN), F32),
        compiler_params=_params(("arbitrary", "arbitrary")),
    )(h_t, dpb)


def _vec_spec():
    return pl.BlockSpec((1, D_MODEL), lambda i: (0, 0))


def _acc_rows(ref, i, rows):
    @pl.when(i == 0)
    def _():
        ref[...] = jnp.zeros_like(ref)

    for n, row in enumerate(rows):
        ref[n:n + 1, :] = ref[n:n + 1, :] + row


def adaln_forward(x, norm_g, scale1p, shift, t_len):
    tm = _row_tile(t_len, 512)

    def body(x_ref, g_ref, sc_ref, sh_ref, h_ref, ht_ref):
        xv = x_ref[...]
        r = lax.rsqrt(jnp.mean(xv * xv, axis=-1, keepdims=True) + EPS)
        h = xv * r * g_ref[...] * sc_ref[...] + sh_ref[...]
        h_ref[...] = h.astype(h_ref.dtype)
        ht_ref[...] = h.T.astype(ht_ref.dtype)

    return pl.pallas_call(
        body, name="adaln_fwd", grid=(t_len // tm,),
        in_specs=[pl.BlockSpec((tm, D_MODEL), lambda i: (i, 0)), _vec_spec(), _vec_spec(), _vec_spec()],
        out_specs=[pl.BlockSpec((tm, D_MODEL), lambda i: (i, 0)), pl.BlockSpec((D_MODEL, tm), lambda i: (0, i))],
        out_shape=[jax.ShapeDtypeStruct((t_len, D_MODEL), _BF), jax.ShapeDtypeStruct((D_MODEL, t_len), _BF)],
        compiler_params=_params(("arbitrary",)),
    )(x, norm_g, scale1p, shift)


def _head_norm(o, g):
    r = lax.rsqrt(jnp.mean(o * o, axis=-1, keepdims=True) + EPS)
    return r, o * r * g


def _group_spec(tm, cb):
    return pl.BlockSpec((N_HEADS, tm, LANE), lambda i, cb=cb: (cb // N_HEADS, i, 0))


MID_FINAL_G, MID_GATE, MID_LOSS, MID_HG_G, MID_RET_G = range(5)


def middle(x, target, oa, ob, pb, wout, gate, final_g, hg_g, ret_g, t_len):
    tm = _row_tile(t_len, 128)
    n_steps = t_len // tm

    def body(x_ref, t_ref, oa_ref, ob_ref, hz_ref, rz_ref, ga_ref, gb_ref, w_ref, gate_ref, fg_ref, hg_ref, rg_ref,
             dy_ref, doa_ref, dob_ref, dw_ref, vec_ref, dpb_ref, m_scr, dm_scr, keep, bufs, sems):
        i = pl.program_id(0)
        rows = pl.ds(pl.multiple_of(i * tm, tm), tm)

        def group_copies(sl):
            return [pltpu.make_async_copy(bufs.at[sl, n], dpb_ref.at[pl.ds(cb, N_HEADS), rows], sems.at[sl, n])
                    for n, cb in enumerate((CB_HZ, CB_RZ, CB_GA, CB_GB))]
        sides = ((oa_ref, hz_ref, ga_ref, hg_ref, doa_ref), (ob_ref, rz_ref, gb_ref, rg_ref, dob_ref))
        for hh in range(N_HEADS):
            ls = slice(hh * LANE, (hh + 1) * LANE)
            acc = None
            for side, (o_ref, z_ref, gt_ref, g_ref, _) in enumerate(sides):
                o = o_ref[hh]
                rr = lax.rsqrt(jnp.mean(o * o, axis=-1, keepdims=True) + EPS)
                orr = o * rr
                zz = z_ref[hh]
                sz = _sigmoid(zz)
                sgt = _sigmoid(gt_ref[hh])
                keep[side, hh, 0] = orr
                keep[side, hh, 1] = sz
                keep[side, hh, 2] = sgt
                keep[side, hh, 3] = jnp.broadcast_to(rr, orr.shape)
                u = sgt * ((orr * g_ref[:, ls]) * (zz * sz))
                acc = u if acc is None else acc + u
            m_scr[:, ls] = acc.astype(m_scr.dtype)
        zv = _dot(m_scr[...], w_ref[...])
        gt, fg = gate_ref[...], fg_ref[...]
        y = x_ref[...] + gt * zv
        r = lax.rsqrt(jnp.mean(y * y, axis=-1, keepdims=True) + EPS)
        yn = y * r
        err = yn * fg - t_ref[...]
        loss = 0.5 * jnp.sum(jnp.mean(err * err, axis=-1, keepdims=True), axis=0, keepdims=True)
        dout = err * (1.0 / D_MODEL)
        gd = dout * fg
        dy = r * gd - yn * (r * r) * jnp.mean(y * gd, axis=-1, keepdims=True)
        dy_ref[...] = dy
        dz = (dy * gt).astype(_BF)
        dm_scr[...] = _dot(dz, w_ref[...], _NT)
        part = _dot(m_scr[...], dz, _TN)

        @pl.when(i == 0)
        def _():
            dw_ref[...] = part

        @pl.when(i > 0)
        def _():
            dw_ref[...] = dw_ref[...] + part

        slot = _writeback_reserve(i, group_copies)
        dg = [[], []]
        for hh in range(N_HEADS):
            ls = slice(hh * LANE, (hh + 1) * LANE)
            dmh = dm_scr[:, ls]
            for side, (o_ref, z_ref, gt_ref, g_ref, do_ref) in enumerate(sides):
                zz, g = z_ref[hh], g_ref[:, ls]
                orr, sz, sgt, rr = keep[side, hh, 0], keep[side, hh, 1], keep[side, hh, 2], keep[side, hh, 3]
                n = orr * g
                silu = zz * sz
                du = dmh * sgt
                bufs[slot, 2 + side, hh] = dmh * (n * silu) * (sgt * (1.0 - sgt))
                bufs[slot, side, hh] = du * n * (sz * (1.0 + zz * (1.0 - sz)))
                dn = du * silu
                dg[side].append(jnp.sum(dn * orr, axis=0, keepdims=True))
                gdn = dn * g
                do_ref[hh] = rr * (gdn - orr * jnp.mean(orr * gdn, axis=-1, keepdims=True))
        _acc_rows(vec_ref, i, [jnp.sum(dout * yn, axis=0, keepdims=True),
                               jnp.sum(dy * zv, axis=0, keepdims=True),
                               jnp.broadcast_to(loss, (1, D_MODEL)),
                               jnp.concatenate(dg[0], axis=1), jnp.concatenate(dg[1], axis=1)])
        _writeback_commit(i, n_steps, slot, group_copies)

    row = pl.BlockSpec((tm, D_MODEL), lambda i: (i, 0))
    head = pl.BlockSpec((N_HEADS, tm, LANE), lambda i: (0, i, 0))
    full = pl.BlockSpec((D_MODEL, D_MODEL), lambda i: (0, 0))
    return pl.pallas_call(
        body, name="middle", grid=(n_steps,),
        in_specs=[row, row, head, head, _group_spec(tm, CB_HZ), _group_spec(tm, CB_RZ), _group_spec(tm, CB_GA),
                  _group_spec(tm, CB_GB), full, _vec_spec(), _vec_spec(), _vec_spec(), _vec_spec()],
        out_specs=[row, head, head, full, pl.BlockSpec((8, D_MODEL), lambda i: (0, 0)),
                   pl.BlockSpec(memory_space=pl.ANY)],
        out_shape=[jax.ShapeDtypeStruct((t_len, D_MODEL), F32),
                   jax.ShapeDtypeStruct((N_HEADS, t_len, LANE), F32),
                   jax.ShapeDtypeStruct((N_HEADS, t_len, LANE), F32),
                   jax.ShapeDtypeStruct((D_MODEL, D_MODEL), F32),
                   jax.ShapeDtypeStruct((8, D_MODEL), F32),
                   jax.ShapeDtypeStruct((N_CB, t_len, LANE), F32)],
        scratch_shapes=[pltpu.VMEM((tm, D_MODEL), _BF), pltpu.VMEM((tm, D_MODEL), F32),
                        pltpu.VMEM((2, N_HEADS, 4, tm, LANE), F32),
                        pltpu.VMEM((2, 4, N_HEADS, tm, LANE), F32), pltpu.SemaphoreType.DMA((2, 4))],
        compiler_params=_params(("arbitrary",)),
    )(x, target, oa, ob, pb, pb, pb, pb, wout, gate, final_g, hg_g, ret_g)


def device_step(x, target, mod, lb, wt, wout, norm_g, hg_g, ret_g, final_g, start_exchange=None):
    t_len = x.shape[0]
    shift, scale, gate = mod[:, :D_MODEL], mod[:, D_MODEL:2 * D_MODEL], mod[:, 2 * D_MODEL:]
    scale1p = 1.0 + scale
    cos_t, sin_t = _rope_tables(t_len)
    h, h_t = adaln_forward(x, norm_g, scale1p, shift, t_len)
    pb = proj_forward(h, wt, t_len)
    oa, ssave = hgrn_forward(pb, lb, t_len)
    ob, rsave = retention_forward(pb, cos_t, sin_t, t_len)
    dy, doa, dob, dwout, vec_mid, dpb = middle(x, target, oa, ob, pb, wout, gate, final_g, hg_g, ret_g, t_len)
    dpb, dlb = hgrn_backward(pb, lb, doa, ssave, dpb, t_len)
    dpb = retention_backward(pb, cos_t, sin_t, dob, rsave, dpb, t_len)
    dwin = proj_backward_weight(h_t, dpb, t_len)
    token, pending = start_exchange(dwin, dwout) if start_exchange else (jnp.zeros((8, LANE), F32), None)
    grad_x, vec_ada = proj_backward_input(dpb, wt, token, x, dy, norm_g, scale1p, t_len)
    return grad_x, dwin, dwout, vec_mid, vec_ada, dlb, pending


PACK_ROWS = 16
ROW_NORM_G, ROW_LB, ROW_HG_G, ROW_RET_G, ROW_FINAL_G, ROW_SHIFT, ROW_SCALE, ROW_GATE, ROW_LOSS = range(9)


def _mesh_pos():
    return lax.axis_index("x"), lax.axis_index("y"), lax.axis_index("c")


def _lin(pos):
    return 4 * pos[0] + 2 * pos[1] + pos[2]


def _xor_peer(pos, k):
    return tuple(1 - p if (k >> s) & 1 else p for p, s in zip(pos, (2, 1, 0)))


def _other_chips(pos):
    x, y, _ = pos
    return [(1 - x, y), (x, 1 - y), (1 - x, 1 - y)]


def _remote(src, dst, send_sem, recv_sem, to):
    return pltpu.make_async_remote_copy(src_ref=src, dst_ref=dst, send_sem=send_sem, recv_sem=recv_sem,
                                        device_id=to, device_id_type=MESH)


def pre_exchange(c, w_ada, b_ada, logits):
    def body(c_ref, wada_ref, bada_ref, logit_ref, mod_ref, scall_ref, lb_ref,
             cg_ref, modall_ref, parts_ref, send1, recv1, send2, recv2):
        pos = _mesh_pos()
        cv = c_ref[...]
        slot = lambda p: pl.ds(pl.multiple_of(8 * _lin(p), 8), 8)
        cg_ref[slot(pos), :] = jnp.broadcast_to(cv * _sigmoid(cv), (8, D_MODEL))
        lb_ref[...] = _sigmoid(logit_ref[0:1, :] - logit_ref[1:2, :])
        peers = [_xor_peer(pos, k) for k in range(1, N_DEV)]
        gather = [_remote(cg_ref.at[slot(pos)], cg_ref.at[slot(pos)], send1.at[n], recv1.at[n], p)
                  for n, p in enumerate(peers)]
        for cp in gather:
            cp.start()
        for n, p in enumerate(peers):
            _remote(cg_ref.at[slot(p)], cg_ref.at[slot(p)], send1.at[n], recv1.at[n], p).wait_recv()
        modall_ref[...] = _dot(cg_ref[...], wada_ref[...])
        scatter = [_remote(modall_ref.at[slot(p)], parts_ref.at[slot(pos)], send2.at[n], recv2.at[n], p)
                   for n, p in enumerate(peers)]
        for cp in scatter:
            cp.start()
        parts_ref[slot(pos), :] = modall_ref[slot(pos), :]
        for n, p in enumerate(peers):
            _remote(modall_ref.at[slot(p)], parts_ref.at[slot(p)], send2.at[n], recv2.at[n], p).wait_recv()
        for cp in gather + scatter:
            cp.wait_send()
        for j in range(N_DEV):
            cols = slice(j * SHARD_ADA, (j + 1) * SHARD_ADA)
            mod_ref[:, cols] = parts_ref[8 * j:8 * j + 1, :] + bada_ref[:, cols]
            scall_ref[j:j + 1, :] = cg_ref[8 * j:8 * j + 1, :]

    vmem = pl.BlockSpec(memory_space=pltpu.VMEM)
    return pl.pallas_call(
        body, name="pre_exchange",
        in_specs=[vmem] * 4, out_specs=[vmem] * 3,
        out_shape=[jax.ShapeDtypeStruct((1, 3 * D_MODEL), F32), jax.ShapeDtypeStruct((N_DEV, D_MODEL), F32),
                   jax.ShapeDtypeStruct((1, D_MODEL), F32)],
        scratch_shapes=[pltpu.VMEM((N_DEV * 8, D_MODEL), F32), pltpu.VMEM((N_DEV * 8, SHARD_ADA), F32),
                        pltpu.VMEM((N_DEV * 8, SHARD_ADA), F32)] + [pltpu.SemaphoreType.DMA((N_DEV - 1,))] * 4,
        compiler_params=pltpu.CompilerParams(vmem_limit_bytes=VMEM_LIMIT),
    )(c, w_ada, b_ada, logits)


def weight_gather(win_sh, wout_sh):
    def body(win_ref, wout_ref, wg_ref, woutg_ref, send, recv, local):
        pos = _mesh_pos()
        x, y, c = pos
        sibling = (x, y, 1 - c)
        chips = _other_chips(pos)
        first, passed, mine = [], [], []
        for a, (src, out) in enumerate(((win_ref, wg_ref), (wout_ref, woutg_ref))):
            def copy(k, block, to, src_ref=None, a=a, out=out):
                dst = out.at[_lin(block)]
                return _remote(dst if src_ref is None else src_ref, dst, send.at[7 * a + k], recv.at[7 * a + k], to)
            mine.append(pltpu.make_async_copy(src, out.at[_lin(pos)], local.at[a]))
            first.append(copy(0, pos, sibling, src))
            first += [copy(1 + j, pos, (*chip, c), src) for j, chip in enumerate(chips)]
            passed.append([copy(4 + j, (*chip, c), sibling) for j, chip in enumerate(chips)])
        for cp in mine + first:
            cp.start()
        for a, out in enumerate((wg_ref, woutg_ref)):
            for j, chip in enumerate(chips):
                dst = out.at[_lin((*chip, c))]
                _remote(dst, dst, send.at[7 * a + 1 + j], recv.at[7 * a + 1 + j], pos).wait_recv()
                passed[a][j].start()
        for a, out in enumerate((wg_ref, woutg_ref)):
            dst = out.at[_lin(sibling)]
            _remote(dst, dst, send.at[7 * a], recv.at[7 * a], pos).wait_recv()
            for j, chip in enumerate(chips):
                dst = out.at[_lin((*chip, 1 - c))]
                _remote(dst, dst, send.at[7 * a + 4 + j], recv.at[7 * a + 4 + j], pos).wait_recv()
        for cp in first + passed[0] + passed[1]:
            cp.wait_send()
        for cp in mine:
            cp.wait()

    any_spec = pl.BlockSpec(memory_space=pl.ANY)
    return pl.pallas_call(
        body, name="weight_gather",
        in_specs=[any_spec, any_spec], out_specs=[any_spec, any_spec],
        out_shape=[jax.ShapeDtypeStruct((N_DEV,) + win_sh.shape, win_sh.dtype),
                   jax.ShapeDtypeStruct((N_DEV,) + wout_sh.shape, wout_sh.dtype)],
        scratch_shapes=[pltpu.SemaphoreType.DMA((14,)), pltpu.SemaphoreType.DMA((14,)),
                        pltpu.SemaphoreType.DMA((2,))],
    )(win_sh, wout_sh)


def grad_pair_exchange(g_in, g_out):
    def body(gin_ref, gout_ref, ra_ref, rb_ref, send, recv):
        pos = _mesh_pos()
        x, y, c = pos
        sibling = (x, y, 1 - c)
        copies = []
        for a, (src, dst) in enumerate(((gin_ref, ra_ref), (gout_ref, rb_ref))):
            for q in range(4):
                copies.append(_remote(src.at[2 * q + (1 - c)], dst.at[q], send.at[4 * a + q], recv.at[4 * a + q],
                                      sibling))
        for cp in copies:
            cp.start()
        for cp in copies:
            cp.wait_recv()
        for cp in copies:
            cp.wait_send()

    any_spec = pl.BlockSpec(memory_space=pl.ANY)
    return pl.pallas_call(
        body, name="grad_pair_exchange",
        in_specs=[any_spec, any_spec], out_specs=[any_spec, any_spec],
        out_shape=[jax.ShapeDtypeStruct((4,) + g_in.shape[1:], F32), jax.ShapeDtypeStruct((4,) + g_out.shape[1:], F32)],
        scratch_shapes=[pltpu.SemaphoreType.DMA((8,)), pltpu.SemaphoreType.DMA((8,))],
    )(g_in, g_out)


def pair_sum(g_in, ra, g_out, rb, c_idx):
    tr = 256

    def body(c_ref, gin_ref, ra_ref, gout_ref, rb_ref, sb_ref, sbo_ref):
        del c_ref
        sb_ref[...] = (gin_ref[...] + ra_ref[...]).astype(sb_ref.dtype)
        sbo_ref[...] = gout_ref[...] + rb_ref[...]

    n_i = D_MODEL // tr
    return pl.pallas_call(
        body, name="pair_sum",
        grid_spec=pltpu.PrefetchScalarGridSpec(
            num_scalar_prefetch=1, grid=(4, n_i),
            in_specs=[pl.BlockSpec((None, tr, SHARD_IN), lambda q, i, c: (2 * q + c[0], i, 0)),
                      pl.BlockSpec((None, tr, SHARD_IN), lambda q, i, c: (q, i, 0)),
                      pl.BlockSpec((None, SHARD_OUT // n_i, D_MODEL), lambda q, i, c: (2 * q + c[0], i, 0)),
                      pl.BlockSpec((None, SHARD_OUT // n_i, D_MODEL), lambda q, i, c: (q, i, 0))],
            out_specs=[pl.BlockSpec((None, tr, SHARD_IN), lambda q, i, c: (q, i, 0)),
                       pl.BlockSpec((None, SHARD_OUT // n_i, D_MODEL), lambda q, i, c: (q, i, 0))]),
        out_shape=[jax.ShapeDtypeStruct(ra.shape, _BF), jax.ShapeDtypeStruct(rb.shape, F32)],
        compiler_params=_params(("arbitrary", "arbitrary")),
    )(c_idx, g_in, ra, g_out, rb)


_HBM = pl.BlockSpec(memory_space=pltpu.HBM)
_SEM = pl.BlockSpec(memory_space=pltpu.SEMAPHORE)
_N_CHIP_COPIES = 6


def _chip_copies(sb_ref, sbo_ref, rc_ref, rco_ref, send, recv):
    pos = _mesh_pos()
    copies = []
    for a, (src, dst) in enumerate(((sb_ref, rc_ref), (sbo_ref, rco_ref))):
        for j, chip in enumerate(_other_chips(pos)):
            copies.append(_remote(src.at[2 * chip[0] + chip[1]], dst.at[j], send.at[3 * a + j], recv.at[3 * a + j],
                                  (*chip, pos[2])))
    return copies


def grad_chip_start(sb, sbo):
    def body(sb_ref, sbo_ref, rc_ref, rco_ref, send, recv, sb_thru, sbo_thru, rc_thru, rco_thru, token):
        del sb_thru, sbo_thru, rc_thru, rco_thru
        for cp in _chip_copies(sb_ref, sbo_ref, rc_ref, rco_ref, send, recv):
            cp.start()
        token[...] = jnp.zeros_like(token)

    hbm = lambda a: pltpu.with_memory_space_constraint(a, pltpu.HBM)
    rc = lax.empty((3,) + sb.shape[1:], sb.dtype)
    rco = lax.empty((3,) + sbo.shape[1:], sbo.dtype)
    return pl.pallas_call(
        body, name="grad_chip_start",
        in_specs=[_HBM] * 4,
        out_specs=[_SEM, _SEM, _HBM, _HBM, _HBM, _HBM, pl.BlockSpec(memory_space=pltpu.VMEM)],
        out_shape=[pltpu.SemaphoreType.DMA((_N_CHIP_COPIES,)), pltpu.SemaphoreType.DMA((_N_CHIP_COPIES,)),
                   pltpu.HBM(sb.shape, sb.dtype), pltpu.HBM(sbo.shape, sbo.dtype),
                   pltpu.HBM(rc.shape, rc.dtype), pltpu.HBM(rco.shape, rco.dtype),
                   jax.ShapeDtypeStruct((8, LANE), F32)],
        input_output_aliases={0: 2, 1: 3, 2: 4, 3: 5},
        compiler_params=pltpu.CompilerParams(has_side_effects=pltpu.SideEffectType.DATAFLOW_SIDE_EFFECTING),
    )(hbm(sb), hbm(sbo), hbm(rc), hbm(rco))


def grad_chip_wait(send, recv, sb, sbo, rc, rco, after):
    def body(sb_ref, sbo_ref, rc_ref, rco_ref, send, recv, after_ref, sb_o, sbo_o, rc_o, rco_o):
        del after_ref, sb_o, sbo_o, rc_o, rco_o
        for cp in _chip_copies(sb_ref, sbo_ref, rc_ref, rco_ref, send, recv):
            cp.wait_send()
            cp.wait_recv()

    return pl.pallas_call(
        body, name="grad_chip_wait",
        in_specs=[_HBM] * 4 + [_SEM, _SEM, pl.BlockSpec(memory_space=pl.ANY)],
        out_specs=[_HBM] * 4,
        out_shape=[pltpu.HBM(sb.shape, sb.dtype), pltpu.HBM(sbo.shape, sbo.dtype),
                   pltpu.HBM(rc.shape, rc.dtype), pltpu.HBM(rco.shape, rco.dtype)],
        input_output_aliases={0: 0, 1: 1, 2: 2, 3: 3},
        compiler_params=pltpu.CompilerParams(has_side_effects=pltpu.SideEffectType.DATAFLOW_SIDE_EFFECTING),
    )(sb, sbo, rc, rco, send, recv, after)


def pack_gather(pack):
    def body(pack_ref, packs_ref, psend, precv):
        pos = _mesh_pos()
        me = _lin(pos)
        packs_ref[me] = pack_ref[...]
        peers = [_xor_peer(pos, k) for k in range(1, N_DEV)]
        gather = [_remote(packs_ref.at[me], packs_ref.at[me], psend.at[n], precv.at[n], p) for n, p in enumerate(peers)]
        for cp in gather:
            cp.start()
        for n, p in enumerate(peers):
            _remote(packs_ref.at[_lin(p)], packs_ref.at[_lin(p)], psend.at[n], precv.at[n], p).wait_recv()
        for cp in gather:
            cp.wait_send()

    vmem = pl.BlockSpec(memory_space=pltpu.VMEM)
    return pl.pallas_call(
        body, name="pack_gather", in_specs=[vmem], out_specs=vmem,
        out_shape=jax.ShapeDtypeStruct((N_DEV,) + pack.shape, F32),
        scratch_shapes=[pltpu.SemaphoreType.DMA((N_DEV - 1,)), pltpu.SemaphoreType.DMA((N_DEV - 1,))],
    )(pack)


def pack_rows(vec_mid, vec_ada, dlb):
    def body(mid_ref, ada_ref, dlb_ref, o_ref):
        mid = lambda r: mid_ref[r:r + 1, :]
        rows = [ada_ref[0:1, :], dlb_ref[...], mid(MID_HG_G), mid(MID_RET_G), mid(MID_FINAL_G),
                ada_ref[2:3, :], ada_ref[1:2, :], mid(MID_GATE), mid(MID_LOSS)]
        o_ref[...] = jnp.zeros_like(o_ref)
        for n, row in enumerate(rows):
            o_ref[n:n + 1, :] = row

    vmem = pl.BlockSpec(memory_space=pltpu.VMEM)
    return pl.pallas_call(body, name="pack_rows", in_specs=[vmem] * 3, out_specs=vmem,
                          out_shape=jax.ShapeDtypeStruct((PACK_ROWS, D_MODEL), F32))(vec_mid, vec_ada, dlb)


def _adamw(w, g, m, v):
    m = ADAM_B1 * m + (1.0 - ADAM_B1) * g
    v = ADAM_B2 * v + (1.0 - ADAM_B2) * (g * g)
    m_hat = m / (1.0 - ADAM_B1 ** ADAM_STEP)
    v_hat = v / (1.0 - ADAM_B2 ** ADAM_STEP)
    delta = -ADAM_LR * (m_hat / (jnp.sqrt(v_hat) + ADAM_EPS) + ADAM_WD * w)
    return delta, m, v


def adam_shard(chip_idx, own, parts, w, m, v, name):
    rows, cols = w.shape
    tr = min(rows, 128)

    def body(chip_ref, p0, p1, p2, p3, w_ref, m_ref, v_ref, g_ref, d_ref, nm_ref, nv_ref):
        del chip_ref
        g = ((p0[...].astype(F32) + p1[...].astype(F32)) + p2[...].astype(F32)) + p3[...].astype(F32)
        g_ref[...] = g
        d_ref[...], nm_ref[...], nv_ref[...] = _adamw(w_ref[...], g, m_ref[...], v_ref[...])

    part = lambda q: pl.BlockSpec((None, tr, cols), lambda i, chip, q=q: (q, i, 0))
    tile = pl.BlockSpec((tr, cols), lambda i, chip: (i, 0))
    return pl.pallas_call(
        body, name=name,
        grid_spec=pltpu.PrefetchScalarGridSpec(
            num_scalar_prefetch=1, grid=(rows // tr,),
            in_specs=[pl.BlockSpec((None, tr, cols), lambda i, chip: (chip[0], i, 0)), part(0), part(1), part(2),
                      tile, tile, tile],
            out_specs=[tile] * 4),
        out_shape=[jax.ShapeDtypeStruct(w.shape, F32)] * 4,
        compiler_params=_params(("arbitrary",)),
    )(chip_idx, own, parts, parts, parts, w, m, v)


def adam_ada(sc_t, dmod_all, me_idx, w, m, v):
    def body(me_ref, sc_ref, dm_ref, w_ref, m_ref, v_ref, g_ref, d_ref, nm_ref, nv_ref):
        del me_ref
        g = _dot_f32(sc_ref[...], dm_ref[...])
        g_ref[...] = g
        d_ref[...], nm_ref[...], nv_ref[...] = _adamw(w_ref[...], g, m_ref[...], v_ref[...])

    full = pl.BlockSpec(w.shape, lambda i, me: (0, 0))
    return pl.pallas_call(
        body, name="adam_ada",
        grid_spec=pltpu.PrefetchScalarGridSpec(
            num_scalar_prefetch=1, grid=(1,),
            in_specs=[pl.BlockSpec(sc_t.shape, lambda i, me: (0, 0)),
                      pl.BlockSpec((LANE, SHARD_ADA), lambda i, me: (0, me[0])), full, full, full],
            out_specs=[full] * 4),
        out_shape=[jax.ShapeDtypeStruct(w.shape, F32)] * 4,
        compiler_params=_params(("arbitrary",)),
    )(me_idx, sc_t, dmod_all, w, m, v)


def adam_vectors(packs, lb, params, ms, vs):
    n = len(params)

    def body(*refs):
        packs_ref, lb_ref = refs[0], refs[1]
        w_refs, m_refs, v_refs = refs[2:2 + n], refs[2 + n:2 + 2 * n], refs[2 + 2 * n:2 + 3 * n]
        loss_ref = refs[2 + 3 * n]
        outs = refs[3 + 3 * n:3 + 7 * n]
        tot_ref = refs[3 + 7 * n]
        tot = packs_ref[0]
        for d in range(1, N_DEV):
            tot = tot + packs_ref[d]
        tot_ref[...] = tot
        row = lambda r: tot_ref[r:r + 1, :]
        lbv = lb_ref[...]
        dl0 = row(ROW_LB) * lbv * (1.0 - lbv)
        grads = [[row(ROW_NORM_G)],
                 [jnp.concatenate([row(ROW_SHIFT), row(ROW_SCALE), row(ROW_GATE)], axis=1)],
                 [dl0, -dl0],
                 [row(ROW_HG_G)], [row(ROW_RET_G)], [row(ROW_FINAL_G)]]
        loss_ref[...] = tot_ref[ROW_LOSS:ROW_LOSS + 1, 0:LANE]
        for j, g_rows in enumerate(grads):
            for r, g in enumerate(g_rows):
                rs = slice(r, r + 1)
                d, nm, nv = _adamw(w_refs[j][rs, :], g, m_refs[j][rs, :], v_refs[j][rs, :])
                outs[4 * j][rs, :] = g
                outs[4 * j + 1][rs, :] = d
                outs[4 * j + 2][rs, :] = nm
                outs[4 * j + 3][rs, :] = nv

    vmem = pl.BlockSpec(memory_space=pltpu.VMEM)
    out_shape = [jax.ShapeDtypeStruct((1, LANE), F32)]
    for w in params:
        out_shape += [jax.ShapeDtypeStruct(w.shape, F32)] * 4
    return pl.pallas_call(
        body, name="adam_vectors", in_specs=[vmem] * (2 + 3 * n), out_specs=[vmem] * len(out_shape),
        out_shape=out_shape, scratch_shapes=[pltpu.VMEM((PACK_ROWS, D_MODEL), F32)],
    )(packs, lb, *params, *ms, *vs)


def kernel(x, c, norm_g, w_ada, b_ada, w_in, hg_lb_logits, hg_norm_g, ret_norm_g, w_out, final_g, loss_target, m_norm_g, m_w_ada, m_b_ada, m_w_in, m_hg_lb_logits, m_hg_norm_g, m_ret_norm_g, m_w_out, m_final_g, v_norm_g, v_w_ada, v_b_ada, v_w_in, v_hg_lb_logits, v_hg_norm_g, v_ret_norm_g, v_w_out, v_final_g):
    pos = _mesh_pos()
    me_idx = jnp.reshape(_lin(pos), (1,)).astype(jnp.int32)
    c_idx = jnp.reshape(pos[2], (1,)).astype(jnp.int32)
    vec = lambda a: a.reshape(1, D_MODEL)

    mod, scall, lb = pre_exchange(c, w_ada[0], b_ada, hg_lb_logits)
    wtg, woutg = weight_gather(w_in[0].T.astype(_BF), w_out[0].astype(_BF))
    chip_idx = jnp.reshape(2 * pos[0] + pos[1], (1,)).astype(jnp.int32)

    def start_exchange(dwin, dwout):
        dwout = dwout.reshape(N_DEV, SHARD_OUT, D_MODEL)
        ra, rb = grad_pair_exchange(dwin, dwout)
        sb, sbo = pair_sum(dwin, ra, dwout, rb, c_idx)
        send, recv, sb, sbo, rc, rco, token = grad_chip_start(sb, sbo)
        return token, (send, recv, sb, sbo, rc, rco)

    grad_x, _, _, vec_mid, vec_ada, dlb, pending = device_step(
        x[0], loss_target[0], mod, lb, wtg.reshape(D_IN, D_MODEL), woutg.reshape(D_MODEL, D_MODEL), norm_g,
        hg_norm_g, ret_norm_g, vec(final_g), start_exchange)
    packs = pack_gather(pack_rows(vec_mid, vec_ada, dlb))
    dmod_all = packs[:, ROW_SHIFT:ROW_GATE + 1, :].reshape(N_DEV, 3 * D_MODEL)
    dmod_all = jnp.pad(dmod_all, ((0, LANE - N_DEV), (0, 0)))
    sc_t = jnp.pad(scall.T, ((0, 0), (0, LANE - N_DEV)))
    g_ada, d_ada, nm_ada, nv_ada = adam_ada(sc_t, dmod_all, me_idx, w_ada[0], m_w_ada[0], v_w_ada[0])
    small = adam_vectors(
        packs, lb,
        (norm_g, b_ada, hg_lb_logits, hg_norm_g, ret_norm_g, vec(final_g)),
        (m_norm_g, m_b_ada, m_hg_lb_logits, m_hg_norm_g, m_ret_norm_g, vec(m_final_g)),
        (v_norm_g, v_b_ada, v_hg_lb_logits, v_hg_norm_g, v_ret_norm_g, vec(v_final_g)))
    loss = small[0][0, 0]
    sb, sbo, rc, rco = grad_chip_wait(*pending, small[0])
    g_in, d_in, nm_in, nv_in = adam_shard(chip_idx, sb, rc, w_in[0], m_w_in[0], v_w_in[0], "adam_w_in")
    g_out, d_out, nm_out, nv_out = adam_shard(chip_idx, sbo, rco, w_out[0], m_w_out[0], v_w_out[0], "adam_w_out")
    (g_ng, d_ng, nm_ng, nv_ng), (g_b, d_b, nm_b, nv_b), (g_lb, d_lb, nm_lb, nv_lb), (g_hg, d_hg, nm_hg, nv_hg), \
        (g_rg, d_rg, nm_rg, nv_rg), (g_fg, d_fg, nm_fg, nv_fg) = [small[1 + 4 * j:5 + 4 * j] for j in range(6)]
    flat = lambda a: a.reshape(D_MODEL)

    def group(ng, ada, b, win, lbl, hg, rg, wo, fg):
        return (ng, ada[None], b, win[None], lbl, hg, rg, wo[None], flat(fg))

    return (loss, grad_x[None],
            *group(g_ng, g_ada, g_b, g_in, g_lb, g_hg, g_rg, g_out, g_fg),
            *group(d_ng, d_ada, d_b, d_in, d_lb, d_hg, d_rg, d_out, d_fg),
            *group(nm_ng, nm_ada, nm_b, nm_in, nm_lb, nm_hg, nm_rg, nm_out, nm_fg),
            *group(nv_ng, nv_ada, nv_b, nv_in, nv_lb, nv_hg, nv_rg, nv_out, nv_fg))
```

```python
import functools

import numpy as np
import jax
import jax.numpy as jnp
from jax import lax
from jax.experimental import pallas as pl
from jax.experimental.pallas import tpu as pltpu

F32 = jnp.float32
_BF = jnp.bfloat16

D_MODEL = 1024
N_HEADS = 8
LANE = 128
RET_DK = 64
D_IN = 9216
N_DEV = 8
SHARD_IN = D_IN // N_DEV
SHARD_ADA = 3 * D_MODEL // N_DEV
SHARD_OUT = D_MODEL // N_DEV
N_CB = D_IN // LANE
CB_PER_SHARD = SHARD_IN // LANE
CHUNK = 128
N_LEVELS = 7
EPS = 1e-6
ROPE_BASE = 10000.0
CB_HQ, CB_HF, CB_HI, CB_HZ, CB_RQ, CB_RK, CB_RV, CB_RZ, CB_GA, CB_GB = 0, 8, 16, 24, 32, 36, 40, 48, 56, 64
VMEM_LIMIT = 56 * 1024 * 1024

ADAM_LR, ADAM_B1, ADAM_B2, ADAM_EPS, ADAM_WD, ADAM_STEP = 0.001, 0.9, 0.999, 1e-08, 0.01, 10

_NN = (((1,), (0,)), ((), ()))
_NT = (((1,), (1,)), ((), ()))
_TN = (((0,), (0,)), ((), ()))
MESH = pl.DeviceIdType.MESH


def _dot(a, b, dims=_NN):
    return lax.dot_general(a.astype(_BF), b.astype(_BF), dims, preferred_element_type=F32)


def _split2(a):
    hi = a.astype(_BF)
    lo = (a - hi.astype(F32)).astype(_BF)
    return jnp.concatenate([hi, lo], axis=1)


def _dot_sel(sel, a):
    n = a.shape[1]
    r = lax.dot_general(sel.astype(_BF), _split2(a), _NN, preferred_element_type=F32)
    return r[:, :n] + r[:, n:]


def _dot_f32(a, b):
    def pieces(v):
        p1 = v.astype(_BF)
        r1 = v - p1.astype(F32)
        p2 = r1.astype(_BF)
        p3 = (r1 - p2.astype(F32)).astype(_BF)
        return p1, p2, p3
    a1, a2, a3 = pieces(a)
    b1, b2, b3 = pieces(b)
    d = lambda u, v: lax.dot_general(u, v, _NN, preferred_element_type=F32)
    return ((d(a1, b3) + d(a2, b2) + d(a3, b1)) + (d(a1, b2) + d(a2, b1))) + d(a1, b1)


def _sigmoid(v):
    return 1.0 / (1.0 + jnp.exp(-v))


def _params(sem=None):
    return pltpu.CompilerParams(dimension_semantics=sem, vmem_limit_bytes=VMEM_LIMIT)


def _hgrn_consts():
    c, nl = CHUNK, N_LEVELS
    t = np.arange(c)[:, None]
    j = np.arange(c)[None, :]
    sel = [j <= t]
    masks = [j == t]
    for l in range(1, nl + 1):
        m = ((t >> l) << l) + (1 << (l - 1)) - 1
        sec = t > m
        sel.append(np.where(sec, (j > m) & (j <= t), (j > t) & (j <= m)))
        same = (t >> l) == (j >> l)
        masks.append(same & sec & (j <= m))
    sel.append(j > t)
    sel = np.concatenate(sel, 0).astype(np.float32)
    masks = np.stack(masks).astype(np.float32)
    sgn = np.stack([np.where((t & (1 << (l - 1))) != 0, 1.0, -1.0) * np.ones((1, LANE)) for l in range(3, nl + 1)])
    return dict(tri=jnp.asarray(sel[:c], _BF),
                lvl=jnp.asarray(masks, F32),
                sgn=jnp.asarray(sgn, F32),
                sel_t=jnp.asarray(sel.T, _BF),
                lvl_b=jnp.asarray(masks, _BF),
                lvlt_b=jnp.asarray(np.swapaxes(masks, 1, 2), _BF))


def _level_exponents(b, logf, b_scr, sgn_ref):
    c = CHUNK
    b_scr[...] = b
    row = lax.broadcasted_iota(jnp.int32, (c, LANE), 0)
    nxt = pltpu.roll(logf, c - 1, 0)
    prv = pltpu.roll(logf, 1, 0)
    r4 = row & 3
    out = [jnp.where((row & 1) == 1, logf, 0.0),
           jnp.where(r4 == 0, nxt, jnp.where(r4 == 1, 0.0, jnp.where(r4 == 2, logf, logf + prv)))]
    for l in range(3, N_LEVELS + 1):
        size, half = 1 << l, 1 << (l - 1)
        ref = jnp.concatenate([jnp.broadcast_to(b_scr[i * size + half - 1:i * size + half, :], (size, LANE))
                               for i in range(c // size)], axis=0)
        out.append((b - ref) * sgn_ref[l - 3])
    return out


def _hgrn_chunk(hq, hf, hi, lbv, tri_ref, sgn_ref, b_scr):
    sq = _sigmoid(hq)
    q = hq * sq
    sg = _sigmoid(hf)
    omlb = 1.0 - lbv
    f = lbv + omlb * sg
    k = 1.0 - f
    logf = jnp.log(f)
    b = _dot_sel(tri_ref[...], logf)
    bc = jnp.sum(logf, axis=0, keepdims=True)
    lev = [None] + [jnp.exp(e) for e in _level_exponents(b, logf, b_scr, sgn_ref)]
    return dict(sq=sq, q=q, sg=sg, omlb=omlb, f=f, k=k, v=hi, eb=jnp.exp(b), erem=jnp.exp(bc - b), ebc=jnp.exp(bc),
                lev=lev)


def _blockdiag(a, b):
    z = jnp.zeros_like(a)
    return jnp.concatenate([jnp.concatenate([a, z], axis=1), jnp.concatenate([z, b], axis=1)], axis=0)


def _level_operands(a):
    q, k = a["q"], a["k"]
    ql = [q.astype(_BF)] + [(q * a["lev"][l]).astype(_BF) for l in range(1, N_LEVELS + 1)]
    kl = [k.astype(_BF)] + [(k * a["lev"][l]).astype(_BF) for l in range(1, N_LEVELS + 1)]
    pairs = range(0, N_LEVELS + 1, 2)
    return ([jnp.concatenate([ql[l], ql[l + 1]], axis=1) for l in pairs], [_blockdiag(kl[l], kl[l + 1]) for l in pairs],
            ql, kl)


def _hgrn_scores(a, lvl_ref, q_pairs, k_diags):
    acc = None
    for n, (qp, kd) in enumerate(zip(q_pairs, k_diags)):
        both = lax.dot_general(qp, kd, _NT, preferred_element_type=F32)
        part = lvl_ref[2 * n] * both[:, :CHUNK] + lvl_ref[2 * n + 1] * both[:, CHUNK:]
        acc = part if acc is None else acc + part
    return acc


SCAN_UNROLL = 2


def _writeback_reserve(step, make_copies):
    slot = step % 2

    @pl.when(step >= 2)
    def _():
        for cp in make_copies(slot):
            cp.wait()

    return slot


def _writeback_commit(step, n_steps, slot, make_copies):
    for cp in make_copies(slot):
        cp.start()

    @pl.when(step == n_steps - 1)
    def _():
        for cp in make_copies(slot):
            cp.wait()
        if n_steps > 1:
            for cp in make_copies(1 - slot):
                cp.wait()


def _resident(const):
    zeros = (0,) * const.ndim
    return pl.BlockSpec(const.shape, lambda p, t: zeros)


def _time_block(t_len):
    return min(t_len, 1024)


def hgrn_forward(pb, lb, t_len):
    nc = t_len // CHUNK
    tb = _time_block(t_len)
    ncb = tb // CHUNK
    consts = _hgrn_consts()
    operands = [consts[n] for n in ("tri", "lvl", "sgn")]

    def body(hq_ref, hf_ref, hi_ref, lb_ref, tri_ref, lvl_ref, sgn_ref, o_ref, ssave_ref, st_ref, b_scr):
        @pl.when(pl.program_id(1) == 0)
        def _():
            st_ref[...] = jnp.zeros_like(st_ref)

        def chunk(ci, carry):
            r = pl.ds(pl.multiple_of(ci * CHUNK, CHUNK), CHUNK)
            for hd in range(2):
                lbv = lb_ref[:, hd * LANE:(hd + 1) * LANE]
                a = _hgrn_chunk(hq_ref[hd, r, :], hf_ref[hd, r, :], hi_ref[hd, r, :], lbv, tri_ref, sgn_ref,
                                b_scr.at[hd])
                q_pairs, k_diags, _, _ = _level_operands(a)
                st = st_ref[hd]
                ssave_ref[hd, ci] = st
                scores = _hgrn_scores(a, lvl_ref, q_pairs, k_diags)
                o_ref[hd, r, :] = _dot(a["q"] * a["eb"], st, _NT) + _dot(scores, a["v"])
                st_ref[hd] = st * a["ebc"] + _dot(a["v"], a["k"] * a["erem"], _TN)
            return carry

        lax.fori_loop(0, ncb, chunk, 0, unroll=SCAN_UNROLL)

    pair = lambda base: pl.BlockSpec((2, tb, LANE), lambda p, t, base=base: (base // 2 + p, t, 0))
    return pl.pallas_call(
        body, name="hgrn_fwd", grid=(N_HEADS // 2, t_len // tb),
        in_specs=[pair(CB_HQ), pair(CB_HF), pair(CB_HI),
                  pl.BlockSpec((1, 2 * LANE), lambda p, t: (0, p))] + [_resident(c) for c in operands],
        out_specs=[pl.BlockSpec((2, tb, LANE), lambda p, t: (p, t, 0)),
                   pl.BlockSpec((2, ncb, LANE, LANE), lambda p, t: (p, t, 0, 0))],
        out_shape=[jax.ShapeDtypeStruct((N_HEADS, t_len, LANE), F32),
                   jax.ShapeDtypeStruct((N_HEADS, nc, LANE, LANE), F32)],
        scratch_shapes=[pltpu.VMEM((2, LANE, LANE), F32), pltpu.VMEM((2, CHUNK, LANE), F32)],
        compiler_params=_params(("arbitrary", "arbitrary")),
    )(pb, pb, pb, lb, *operands)


def hgrn_backward(pb, lb, do, ssave, dpb, t_len):
    tb = _time_block(t_len)
    ncb, ntb = tb // CHUNK, t_len // tb
    consts = _hgrn_consts()
    operands = [consts[n] for n in ("tri", "lvl", "sgn", "sel_t", "lvl_b", "lvlt_b")]

    def body(hq_ref, hf_ref, hi_ref, lb_ref, do_ref, ssave_ref, tri_ref, lvl_ref, sgn_ref, selt_ref, lvlb_ref,
             lvltb_ref, dpb_in, dpb_ref, dlb_ref, dq_buf, df_buf, di_buf, dst_ref, b_scr, sems):
        del dpb_in
        p, t = pl.program_id(0), pl.program_id(1)
        step = p * ntb + t
        rows = pl.ds(pl.multiple_of((ntb - 1 - t) * tb, tb), tb)

        def out_copies(sl):
            return [pltpu.make_async_copy(buf.at[sl], dpb_ref.at[pl.ds(base + 2 * p, 2), rows], sems.at[sl, n])
                    for n, (buf, base) in enumerate(((dq_buf, CB_HQ), (df_buf, CB_HF), (di_buf, CB_HI)))]

        slot = _writeback_reserve(step, out_copies)

        @pl.when(t == 0)
        def _():
            dst_ref[...] = jnp.zeros_like(dst_ref)
            dlb_ref[...] = jnp.zeros_like(dlb_ref)

        def chunk(i, carry):
            ci = ncb - 1 - i
            r = pl.ds(pl.multiple_of(ci * CHUNK, CHUNK), CHUNK)
            for hd in range(2):
                head_chunk(hd, ci, r)
            return carry

        def head_chunk(hd, ci, r):
            lbv = lb_ref[:, hd * LANE:(hd + 1) * LANE]
            hq = hq_ref[hd, r, :]
            a = _hgrn_chunk(hq, hf_ref[hd, r, :], hi_ref[hd, r, :], lbv, tri_ref, sgn_ref, b_scr.at[hd])
            q_pairs, k_diags, ql, kl = _level_operands(a)
            q, k, v = a["q"], a["k"], a["v"]
            g = do_ref[hd, r, :]
            st0 = ssave_ref[hd, ci]
            dst = dst_ref[hd]
            scores = _hgrn_scores(a, lvl_ref, q_pairs, k_diags)
            da = _dot(g, v, _NT)
            da_t = _dot(v, g, _NT)
            kb = k * a["erem"]
            qb = q * a["eb"]
            dv = _dot(scores, g, _TN) + _dot(kb, dst, _NT)
            dq_inter = _dot(g, st0) * a["eb"]
            dk_state = _dot(v, dst) * a["erem"]
            dq, dk = dq_inter, dk_state
            de = [q * dq_inter]
            da_b, dat_b = da.astype(_BF), da_t.astype(_BF)
            for n in range(len(k_diags)):
                l0, l1 = 2 * n, 2 * n + 1
                da_pair = jnp.concatenate([lvlb_ref[l0] * da_b, lvlb_ref[l1] * da_b], axis=1)
                dat_pair = jnp.concatenate([lvltb_ref[l0] * dat_b, lvltb_ref[l1] * dat_b], axis=1)
                dq_both = lax.dot_general(da_pair, k_diags[n], _NN, preferred_element_type=F32)
                dk_both = lax.dot_general(dat_pair, _blockdiag(ql[l0], ql[l1]), _NN, preferred_element_type=F32)
                for l, cols in ((l0, slice(0, LANE)), (l1, slice(LANE, 2 * LANE))):
                    dql, dkl = dq_both[:, cols], dk_both[:, cols]
                    if l > 0:
                        e = a["lev"][l]
                        dql, dkl = dql * e, dkl * e
                        de.append(q * dql + k * dkl)
                    dq = dq + dql
                    dk = dk + dkl
            de.append(k * dk_state)
            dst_ref[hd] = dst * a["ebc"] + _dot(g, qb, _TN)
            dbc = jnp.sum(dst * st0, axis=0, keepdims=True) * a["ebc"]
            de2 = lax.dot_general(selt_ref[...], _split2(jnp.concatenate(de, axis=0)), _NN,
                                  preferred_element_type=F32)
            dlogf = de2[:, :LANE] + de2[:, LANE:] + dbc
            sq, sg = a["sq"], a["sg"]
            df = dlogf / a["f"] - dk
            dq_buf[slot, hd, r, :] = dq * (sq * (1.0 + hq * (1.0 - sq)))
            df_buf[slot, hd, r, :] = df * a["omlb"] * sg * (1.0 - sg)
            di_buf[slot, hd, r, :] = dv
            cols = slice(hd * LANE, (hd + 1) * LANE)
            dlb_ref[:, cols] = dlb_ref[:, cols] + jnp.sum(df * (1.0 - sg), axis=0, keepdims=True)

        lax.fori_loop(0, ncb, chunk, 0, unroll=SCAN_UNROLL)
        _writeback_commit(step, (N_HEADS // 2) * ntb, slot, out_copies)

    pair = lambda base: pl.BlockSpec((2, tb, LANE), lambda p, t, base=base: (base // 2 + p, ntb - 1 - t, 0))
    any_spec = pl.BlockSpec(memory_space=pl.ANY)
    return pl.pallas_call(
        body, name="hgrn_bwd", grid=(N_HEADS // 2, ntb),
        in_specs=[pair(CB_HQ), pair(CB_HF), pair(CB_HI),
                  pl.BlockSpec((1, 2 * LANE), lambda p, t: (0, p)),
                  pair(0),
                  pl.BlockSpec((2, ncb, LANE, LANE), lambda p, t: (p, ntb - 1 - t, 0, 0))]
        + [_resident(c) for c in operands] + [any_spec],
        out_specs=[any_spec, pl.BlockSpec((1, 2 * LANE), lambda p, t: (0, p))],
        out_shape=[jax.ShapeDtypeStruct(dpb.shape, F32), jax.ShapeDtypeStruct((1, D_MODEL), F32)],
        scratch_shapes=[pltpu.VMEM((2, 2, tb, LANE), F32)] * 3 + [
            pltpu.VMEM((2, LANE, LANE), F32), pltpu.VMEM((2, CHUNK, LANE), F32), pltpu.SemaphoreType.DMA((2, 3))],
        input_output_aliases={6 + len(operands): 0},
        compiler_params=_params(("arbitrary", "arbitrary")),
    )(pb, pb, pb, lb, do, ssave, *operands, dpb)


def _rope_tables(t_len):
    half = RET_DK // 2
    inv_freq = 1.0 / (ROPE_BASE ** jnp.linspace(0.0, 1.0, half, dtype=F32))
    ang = jnp.arange(t_len, dtype=jnp.int32).astype(F32)[:, None] * inv_freq[None, :]
    cos, sin = jnp.cos(ang), jnp.sin(ang)
    cos_t = jnp.concatenate([cos, cos, cos, cos], axis=1)
    sin_t = jnp.concatenate([-sin, sin, -sin, sin], axis=1)
    return cos_t, sin_t


def _swap_halves(v):
    half = RET_DK // 2
    lane = lax.broadcasted_iota(jnp.int32, v.shape, 1)
    first = (lane & (RET_DK - 1)) < half
    return jnp.where(first, pltpu.roll(v, LANE - half, 1), pltpu.roll(v, half, 1))


def _ret_head_consts(hidx):
    c = CHUNK
    hf = jnp.full((1, LANE), hidx, jnp.int32).astype(F32)
    lg = jnp.log(1.0 - jnp.exp(-(5.0 + hf) * np.float32(np.log(2.0))))
    row = lax.broadcasted_iota(jnp.int32, (c, c), 0)
    col = lax.broadcasted_iota(jnp.int32, (c, c), 1)
    rel = (row - col).astype(F32)
    dm = jnp.where(rel >= 0, jnp.exp(lg[:, :1] * jnp.maximum(rel, 0.0)), 0.0)
    dm_t = jnp.where(rel <= 0, jnp.exp(lg[:, :1] * jnp.maximum(-rel, 0.0)), 0.0)
    idx = lax.broadcasted_iota(jnp.int32, (c, LANE), 0).astype(F32)
    zeta = jnp.exp(lg * (c - 1.0 - idx))
    xi = jnp.exp(lg * (idx + 1.0))
    cdec = jnp.exp(lg * float(c))
    return dm, zeta, xi, cdec, dm_t


def _lane_mask(which):
    lane = lax.broadcasted_iota(jnp.int32, (1, LANE), 1)
    return ((lane // RET_DK) == which).astype(F32)


def retention_forward(pb, cos_t, sin_t, t_len):
    nc = t_len // CHUNK

    tb = _time_block(t_len)
    ncb = tb // CHUNK

    def body(rq_ref, rk_ref, rv_ref, cos_ref, sin_ref, o_ref, rsave_ref, st_ref):
        p = pl.program_id(0)

        @pl.when(pl.program_id(1) == 0)
        def _():
            st_ref[...] = jnp.zeros_like(st_ref)

        consts = [_ret_head_consts(2 * p + hd) for hd in range(2)]

        def chunk(ci, carry):
            r = pl.ds(pl.multiple_of(ci * CHUNK, CHUNK), CHUNK)
            cs, sn = cos_ref[r, :], sin_ref[r, :]
            q = rq_ref[r, :]
            k = rk_ref[r, :]
            q = q * cs + _swap_halves(q) * sn
            k = (k * cs + _swap_halves(k) * sn) * RET_DK ** -0.5
            for hd in range(2):
                dm, zeta, xi, cdec, _ = consts[hd]
                lm = _lane_mask(hd)
                qh, kh = q * lm, k * lm
                v = rv_ref[hd, r, :]
                st = st_ref[hd]
                rsave_ref[hd, ci] = st
                scores = _dot(qh, kh, _NT) * dm
                o_ref[hd, r, :] = _dot(scores, v) + _dot(qh * xi, st, _NT)
                st_ref[hd] = st * cdec + _dot(v, kh * zeta, _TN)
            return carry

        lax.fori_loop(0, ncb, chunk, 0, unroll=SCAN_UNROLL)

    return pl.pallas_call(
        body, name="ret_fwd", grid=(N_HEADS // 2, t_len // tb),
        in_specs=[pl.BlockSpec((None, tb, LANE), lambda p, t: (CB_RQ + p, t, 0)),
                  pl.BlockSpec((None, tb, LANE), lambda p, t: (CB_RK + p, t, 0)),
                  pl.BlockSpec((2, tb, LANE), lambda p, t: (CB_RV // 2 + p, t, 0)),
                  pl.BlockSpec((tb, LANE), lambda p, t: (t, 0)),
                  pl.BlockSpec((tb, LANE), lambda p, t: (t, 0))],
        out_specs=[pl.BlockSpec((2, tb, LANE), lambda p, t: (p, t, 0)),
                   pl.BlockSpec((2, ncb, LANE, LANE), lambda p, t: (p, t, 0, 0))],
        out_shape=[jax.ShapeDtypeStruct((N_HEADS, t_len, LANE), F32),
                   jax.ShapeDtypeStruct((N_HEADS, nc, LANE, LANE), F32)],
        scratch_shapes=[pltpu.VMEM((2, LANE, LANE), F32)],
        compiler_params=_params(("arbitrary", "arbitrary")),
    )(pb, pb, pb, cos_t, sin_t)


def retention_backward(pb, cos_t, sin_t, do, rsave, dpb, t_len):
    tb = _time_block(t_len)
    ncb, ntb = tb // CHUNK, t_len // tb

    def body(rq_ref, rk_ref, rv_ref, cos_ref, sin_ref, do_ref, rsave_ref, dpb_in,
             dpb_ref, dq_buf, dk_buf, dv_buf, dst_ref, sems):
        del dpb_in
        p, t = pl.program_id(0), pl.program_id(1)
        step = p * ntb + t
        rows = pl.ds(pl.multiple_of((ntb - 1 - t) * tb, tb), tb)

        def out_copies(sl):
            return [pltpu.make_async_copy(dq_buf.at[sl], dpb_ref.at[CB_RQ + p, rows], sems.at[sl, 0]),
                    pltpu.make_async_copy(dk_buf.at[sl], dpb_ref.at[CB_RK + p, rows], sems.at[sl, 1]),
                    pltpu.make_async_copy(dv_buf.at[sl], dpb_ref.at[pl.ds(CB_RV + 2 * p, 2), rows], sems.at[sl, 2])]

        slot = _writeback_reserve(step, out_copies)

        @pl.when(t == 0)
        def _():
            dst_ref[...] = jnp.zeros_like(dst_ref)

        consts = [_ret_head_consts(2 * p + hd) for hd in range(2)]

        def chunk(i, carry):
            ci = ncb - 1 - i
            r = pl.ds(pl.multiple_of(ci * CHUNK, CHUNK), CHUNK)
            cs, sn = cos_ref[r, :], sin_ref[r, :]
            q = rq_ref[r, :]
            k = rk_ref[r, :]
            q = q * cs + _swap_halves(q) * sn
            k = (k * cs + _swap_halves(k) * sn) * RET_DK ** -0.5
            dq, dk = None, None
            for hd in range(2):
                dm, zeta, xi, cdec, dm_t = consts[hd]
                lm = _lane_mask(hd)
                qh, kh = q * lm, k * lm
                v = rv_ref[hd, r, :]
                g = do_ref[hd, r, :]
                st0 = rsave_ref[hd, ci]
                dst = dst_ref[hd]
                scores_t = _dot(kh, qh, _NT) * dm_t
                dsc = _dot(g, v, _NT) * dm
                dsc_t = _dot(v, g, _NT) * dm_t
                dqh = _dot(dsc, kh) + _dot(g, st0) * xi
                dkh = _dot(dsc_t, qh) + _dot(v, dst) * zeta
                dv_buf[slot, hd, r, :] = _dot(scores_t, g) + _dot(kh * zeta, dst, _NT)
                dst_ref[hd] = dst * cdec + _dot(g, qh * xi, _TN)
                dq = dqh if dq is None else dq + dqh
                dk = dkh if dk is None else dk + dkh
            dk = dk * (RET_DK ** -0.5)
            dq_buf[slot, r, :] = dq * cs - _swap_halves(dq) * sn
            dk_buf[slot, r, :] = dk * cs - _swap_halves(dk) * sn
            return carry

        lax.fori_loop(0, ncb, chunk, 0, unroll=SCAN_UNROLL)
        _writeback_commit(step, (N_HEADS // 2) * ntb, slot, out_copies)

    any_spec = pl.BlockSpec(memory_space=pl.ANY)
    return pl.pallas_call(
        body, name="ret_bwd", grid=(N_HEADS // 2, ntb),
        in_specs=[pl.BlockSpec((None, tb, LANE), lambda p, t: (CB_RQ + p, ntb - 1 - t, 0)),
                  pl.BlockSpec((None, tb, LANE), lambda p, t: (CB_RK + p, ntb - 1 - t, 0)),
                  pl.BlockSpec((2, tb, LANE), lambda p, t: (CB_RV // 2 + p, ntb - 1 - t, 0)),
                  pl.BlockSpec((tb, LANE), lambda p, t: (ntb - 1 - t, 0)),
                  pl.BlockSpec((tb, LANE), lambda p, t: (ntb - 1 - t, 0)),
                  pl.BlockSpec((2, tb, LANE), lambda p, t: (p, ntb - 1 - t, 0)),
                  pl.BlockSpec((2, ncb, LANE, LANE), lambda p, t: (p, ntb - 1 - t, 0, 0)),
                  any_spec],
        out_specs=any_spec,
        out_shape=jax.ShapeDtypeStruct(dpb.shape, F32),
        scratch_shapes=[pltpu.VMEM((2, tb, LANE), F32), pltpu.VMEM((2, tb, LANE), F32),
                        pltpu.VMEM((2, 2, tb, LANE), F32), pltpu.VMEM((2, LANE, LANE), F32),
                        pltpu.SemaphoreType.DMA((2, 3))],
        input_output_aliases={7: 0},
        compiler_params=_params(("arbitrary", "arbitrary")),
    )(pb, pb, pb, cos_t, sin_t, do, rsave, dpb)


def _row_tile(t_len, want):
    return min(want, t_len)


PAIR_CB = 2 * CB_PER_SHARD


def proj_forward(h, wt, t_len):
    tm = _row_tile(t_len, 512)

    def body(h_ref, w_ref, o_ref):
        acc = _dot(h_ref[...], w_ref[...], _NT)
        for jj in range(PAIR_CB):
            o_ref[jj] = acc[:, jj * LANE:(jj + 1) * LANE]

    return pl.pallas_call(
        body, name="proj_fwd", grid=(N_DEV // 2, t_len // tm),
        in_specs=[pl.BlockSpec((tm, D_MODEL), lambda j, i: (i, 0)),
                  pl.BlockSpec((PAIR_CB * LANE, D_MODEL), lambda j, i: (j, 0))],
        out_specs=pl.BlockSpec((PAIR_CB, tm, LANE), lambda j, i: (j, i, 0)),
        out_shape=jax.ShapeDtypeStruct((N_CB, t_len, LANE), F32),
        compiler_params=_params(("arbitrary", "arbitrary")),
    )(h, wt)


def proj_backward_input(dpb, wt, token, x, dy, norm_g, scale1p, t_len):
    tm = _row_tile(t_len, 256)

    def body(a_ref, wt_hbm, token_ref, x_ref, dy_ref, g_ref, sc_ref, gx_ref, vec_ref, w_ref, sem):
        del token_ref
        i = pl.program_id(0)

        @pl.when(i == 0)
        def _():
            cp = pltpu.make_async_copy(wt_hbm, w_ref, sem)
            cp.start()
            cp.wait()

        a = jnp.concatenate([a_ref[jj].astype(_BF) for jj in range(N_CB)], axis=1)
        dhv = _dot(a, w_ref[...])
        xv, g, sc = x_ref[...], g_ref[...], sc_ref[...]
        r = lax.rsqrt(jnp.mean(xv * xv, axis=-1, keepdims=True) + EPS)
        xn = xv * r
        dxn = dhv * (g * sc)
        gx_ref[...] = dy_ref[...] + r * dxn - xn * (r * r) * jnp.mean(xv * dxn, axis=-1, keepdims=True)
        t = dhv * xn
        _acc_rows(vec_ref, i, [jnp.sum(t * sc, axis=0, keepdims=True),
                               jnp.sum(t * g, axis=0, keepdims=True),
                               jnp.sum(dhv, axis=0, keepdims=True)])

    row = pl.BlockSpec((tm, D_MODEL), lambda i: (i, 0))
    return pl.pallas_call(
        body, name="proj_bwd_input", grid=(t_len // tm,),
        in_specs=[pl.BlockSpec((N_CB, tm, LANE), lambda i: (0, i, 0)),
                  pl.BlockSpec(memory_space=pl.ANY),
                  pl.BlockSpec(token.shape, lambda i: (0, 0)),
                  row, row, _vec_spec(), _vec_spec()],
        out_specs=[row, pl.BlockSpec((8, D_MODEL), lambda i: (0, 0))],
        out_shape=[jax.ShapeDtypeStruct((t_len, D_MODEL), F32), jax.ShapeDtypeStruct((8, D_MODEL), F32)],
        scratch_shapes=[pltpu.VMEM(wt.shape, wt.dtype), pltpu.SemaphoreType.DMA],
        compiler_params=_params(("arbitrary",)),
    )(dpb, wt, token, x, dy, norm_g, scale1p)


def proj_backward_weight(h_t, dpb, t_len):
    tk = _row_tile(t_len, 1024)

    def body(h_ref, b_ref, o_ref):
        k = pl.program_id(1)
        b = jnp.concatenate([b_ref[jj].astype(_BF) for jj in range(PAIR_CB)], axis=1)
        part = _dot(h_ref[...], b)

        @pl.when(k == 0)
        def _():
            for s in range(2):
                o_ref[s] = part[:, s * SHARD_IN:(s + 1) * SHARD_IN]

        @pl.when(k > 0)
        def _():
            for s in range(2):
                o_ref[s] = o_ref[s] + part[:, s * SHARD_IN:(s + 1) * SHARD_IN]

    return pl.pallas_call(
        body, name="proj_bwd_weight", grid=(N_DEV // 2, t_len // tk),
        in_specs=[pl.BlockSpec((D_MODEL, tk), lambda j, k: (0, k)),
                  pl.BlockSpec((PAIR_CB, tk, LANE), lambda j, k: (j, k, 0))],
        out_specs=pl.BlockSpec((2, D_MODEL, SHARD_IN), lambda j, k: (j, 0, 0)),
        out_shape=jax.ShapeDtypeStruct((N_DEV, D_MODEL, SHARD_IN), F32),
        compiler_params=_params(("arbitrary", "arbitrary")),
    )(h_t, dpb)


def _vec_spec():
    return pl.BlockSpec((1, D_MODEL), lambda i: (0, 0))


def _acc_rows(ref, i, rows):
    @pl.when(i == 0)
    def _():
        ref[...] = jnp.zeros_like(ref)

    for n, row in enumerate(rows):
        ref[n:n + 1, :] = ref[n:n + 1, :] + row


def adaln_forward(x, norm_g, scale1p, shift, t_len):
    tm = _row_tile(t_len, 512)

    def body(x_ref, g_ref, sc_ref, sh_ref, h_ref, ht_ref):
        xv = x_ref[...]
        r = lax.rsqrt(jnp.mean(xv * xv, axis=-1, keepdims=True) + EPS)
        h = xv * r * g_ref[...] * sc_ref[...] + sh_ref[...]
        h_ref[...] = h.astype(h_ref.dtype)
        ht_ref[...] = h.T.astype(ht_ref.dtype)

    return pl.pallas_call(
        body, name="adaln_fwd", grid=(t_len // tm,),
        in_specs=[pl.BlockSpec((tm, D_MODEL), lambda i: (i, 0)), _vec_spec(), _vec_spec(), _vec_spec()],
        out_specs=[pl.BlockSpec((tm, D_MODEL), lambda i: (i, 0)), pl.BlockSpec((D_MODEL, tm), lambda i: (0, i))],
        out_shape=[jax.ShapeDtypeStruct((t_len, D_MODEL), _BF), jax.ShapeDtypeStruct((D_MODEL, t_len), _BF)],
        compiler_params=_params(("arbitrary",)),
    )(x, norm_g, scale1p, shift)


def _head_norm(o, g):
    r = lax.rsqrt(jnp.mean(o * o, axis=-1, keepdims=True) + EPS)
    return r, o * r * g


def _group_spec(tm, cb):
    return pl.BlockSpec((N_HEADS, tm, LANE), lambda i, cb=cb: (cb // N_HEADS, i, 0))


MID_FINAL_G, MID_GATE, MID_LOSS, MID_HG_G, MID_RET_G = range(5)


def middle(x, target, oa, ob, pb, wout, gate, final_g, hg_g, ret_g, t_len):
    tm = _row_tile(t_len, 256)
    n_steps = t_len // tm

    def body(x_ref, t_ref, oa_ref, ob_ref, hz_ref, rz_ref, ga_ref, gb_ref, w_ref, gate_ref, fg_ref, hg_ref, rg_ref,
             dy_ref, doa_ref, dob_ref, dw_ref, vec_ref, dpb_ref, m_scr, dm_scr, keep, bufs, sems):
        i = pl.program_id(0)
        rows = pl.ds(pl.multiple_of(i * tm, tm), tm)

        def group_copies(sl):
            return [pltpu.make_async_copy(bufs.at[sl, n], dpb_ref.at[pl.ds(cb, N_HEADS), rows], sems.at[sl, n])
                    for n, cb in enumerate((CB_HZ, CB_RZ, CB_GA, CB_GB))]
        sides = ((oa_ref, hz_ref, ga_ref, hg_ref, doa_ref), (ob_ref, rz_ref, gb_ref, rg_ref, dob_ref))
        for hh in range(N_HEADS):
            ls = slice(hh * LANE, (hh + 1) * LANE)
            acc = None
            for side, (o_ref, z_ref, gt_ref, g_ref, _) in enumerate(sides):
                o = o_ref[hh]
                rr = lax.rsqrt(jnp.mean(o * o, axis=-1, keepdims=True) + EPS)
                orr = o * rr
                zz = z_ref[hh]
                sz = _sigmoid(zz)
                sgt = _sigmoid(gt_ref[hh])
                keep[side, hh, 0] = orr
                keep[side, hh, 1] = sz
                keep[side, hh, 2] = sgt
                keep[side, hh, 3] = jnp.broadcast_to(rr, orr.shape)
                u = sgt * ((orr * g_ref[:, ls]) * (zz * sz))
                acc = u if acc is None else acc + u
            m_scr[:, ls] = acc.astype(m_scr.dtype)
        zv = _dot(m_scr[...], w_ref[...])
        gt, fg = gate_ref[...], fg_ref[...]
        y = x_ref[...] + gt * zv
        r = lax.rsqrt(jnp.mean(y * y, axis=-1, keepdims=True) + EPS)
        yn = y * r
        err = yn * fg - t_ref[...]
        loss = 0.5 * jnp.sum(jnp.mean(err * err, axis=-1, keepdims=True), axis=0, keepdims=True)
        dout = err * (1.0 / D_MODEL)
        gd = dout * fg
        dy = r * gd - yn * (r * r) * jnp.mean(y * gd, axis=-1, keepdims=True)
        dy_ref[...] = dy
        dz = (dy * gt).astype(_BF)
        dm_scr[...] = _dot(dz, w_ref[...], _NT)
        part = _dot(m_scr[...], dz, _TN)

        @pl.when(i == 0)
        def _():
            dw_ref[...] = part

        @pl.when(i > 0)
        def _():
            dw_ref[...] = dw_ref[...] + part

        slot = _writeback_reserve(i, group_copies)
        dg = [[], []]
        for hh in range(N_HEADS):
            ls = slice(hh * LANE, (hh + 1) * LANE)
            dmh = dm_scr[:, ls]
            for side, (o_ref, z_ref, gt_ref, g_ref, do_ref) in enumerate(sides):
                zz, g = z_ref[hh], g_ref[:, ls]
                orr, sz, sgt, rr = keep[side, hh, 0], keep[side, hh, 1], keep[side, hh, 2], keep[side, hh, 3]
                n = orr * g
                silu = zz * sz
                du = dmh * sgt
                bufs[slot, 2 + side, hh] = dmh * (n * silu) * (sgt * (1.0 - sgt))
                bufs[slot, side, hh] = du * n * (sz * (1.0 + zz * (1.0 - sz)))
                dn = du * silu
                dg[side].append(jnp.sum(dn * orr, axis=0, keepdims=True))
                gdn = dn * g
                do_ref[hh] = rr * (gdn - orr * jnp.mean(orr * gdn, axis=-1, keepdims=True))
        _acc_rows(vec_ref, i, [jnp.sum(dout * yn, axis=0, keepdims=True),
                               jnp.sum(dy * zv, axis=0, keepdims=True),
                               jnp.broadcast_to(loss, (1, D_MODEL)),
                               jnp.concatenate(dg[0], axis=1), jnp.concatenate(dg[1], axis=1)])
        _writeback_commit(i, n_steps, slot, group_copies)

    row = pl.BlockSpec((tm, D_MODEL), lambda i: (i, 0))
    head = pl.BlockSpec((N_HEADS, tm, LANE), lambda i: (0, i, 0))
    full = pl.BlockSpec((D_MODEL, D_MODEL), lambda i: (0, 0))
    return pl.pallas_call(
        body, name="middle", grid=(n_steps,),
        in_specs=[row, row, head, head, _group_spec(tm, CB_HZ), _group_spec(tm, CB_RZ), _group_spec(tm, CB_GA),
                  _group_spec(tm, CB_GB), full, _vec_spec(), _vec_spec(), _vec_spec(), _vec_spec()],
        out_specs=[row, head, head, full, pl.BlockSpec((8, D_MODEL), lambda i: (0, 0)),
                   pl.BlockSpec(memory_space=pl.ANY)],
        out_shape=[jax.ShapeDtypeStruct((t_len, D_MODEL), F32),
                   jax.ShapeDtypeStruct((N_HEADS, t_len, LANE), F32),
                   jax.ShapeDtypeStruct((N_HEADS, t_len, LANE), F32),
                   jax.ShapeDtypeStruct((D_MODEL, D_MODEL), F32),
                   jax.ShapeDtypeStruct((8, D_MODEL), F32),
                   jax.ShapeDtypeStruct((N_CB, t_len, LANE), F32)],
        scratch_shapes=[pltpu.VMEM((tm, D_MODEL), _BF), pltpu.VMEM((tm, D_MODEL), F32),
                        pltpu.VMEM((2, N_HEADS, 4, tm, LANE), F32),
                        pltpu.VMEM((2, 4, N_HEADS, tm, LANE), F32), pltpu.SemaphoreType.DMA((2, 4))],
        compiler_params=_params(("arbitrary",)),
    )(x, target, oa, ob, pb, pb, pb, pb, wout, gate, final_g, hg_g, ret_g)


def device_step(x, target, mod, lb, wt, wout, norm_g, hg_g, ret_g, final_g, start_exchange=None):
    t_len = x.shape[0]
    shift, scale, gate = mod[:, :D_MODEL], mod[:, D_MODEL:2 * D_MODEL], mod[:, 2 * D_MODEL:]
    scale1p = 1.0 + scale
    cos_t, sin_t = _rope_tables(t_len)
    h, h_t = adaln_forward(x, norm_g, scale1p, shift, t_len)
    pb = proj_forward(h, wt, t_len)
    oa, ssave = hgrn_forward(pb, lb, t_len)
    ob, rsave = retention_forward(pb, cos_t, sin_t, t_len)
    dy, doa, dob, dwout, vec_mid, dpb = middle(x, target, oa, ob, pb, wout, gate, final_g, hg_g, ret_g, t_len)
    dpb, dlb = hgrn_backward(pb, lb, doa, ssave, dpb, t_len)
    dpb = retention_backward(pb, cos_t, sin_t, dob, rsave, dpb, t_len)
    dwin = proj_backward_weight(h_t, dpb, t_len)
    token, pending = start_exchange(dwin, dwout) if start_exchange else (jnp.zeros((8, LANE), F32), None)
    grad_x, vec_ada = proj_backward_input(dpb, wt, token, x, dy, norm_g, scale1p, t_len)
    return grad_x, dwin, dwout, vec_mid, vec_ada, dlb, pending


PACK_ROWS = 16
ROW_NORM_G, ROW_LB, ROW_HG_G, ROW_RET_G, ROW_FINAL_G, ROW_SHIFT, ROW_SCALE, ROW_GATE, ROW_LOSS = range(9)


def _mesh_pos():
    return lax.axis_index("x"), lax.axis_index("y"), lax.axis_index("c")


def _lin(pos):
    return 4 * pos[0] + 2 * pos[1] + pos[2]


def _xor_peer(pos, k):
    return tuple(1 - p if (k >> s) & 1 else p for p, s in zip(pos, (2, 1, 0)))


def _other_chips(pos):
    x, y, _ = pos
    return [(1 - x, y), (x, 1 - y), (1 - x, 1 - y)]


def _remote(src, dst, send_sem, recv_sem, to):
    return pltpu.make_async_remote_copy(src_ref=src, dst_ref=dst, send_sem=send_sem, recv_sem=recv_sem,
                                        device_id=to, device_id_type=MESH)


def pre_exchange(c, w_ada, b_ada, logits):
    def body(c_ref, wada_ref, bada_ref, logit_ref, mod_ref, scall_ref, lb_ref,
             cg_ref, modall_ref, parts_ref, send1, recv1, send2, recv2):
        pos = _mesh_pos()
        cv = c_ref[...]
        slot = lambda p: pl.ds(pl.multiple_of(8 * _lin(p), 8), 8)
        cg_ref[slot(pos), :] = jnp.broadcast_to(cv * _sigmoid(cv), (8, D_MODEL))
        lb_ref[...] = _sigmoid(logit_ref[0:1, :] - logit_ref[1:2, :])
        peers = [_xor_peer(pos, k) for k in range(1, N_DEV)]
        gather = [_remote(cg_ref.at[slot(pos)], cg_ref.at[slot(pos)], send1.at[n], recv1.at[n], p)
                  for n, p in enumerate(peers)]
        for cp in gather:
            cp.start()
        for n, p in enumerate(peers):
            _remote(cg_ref.at[slot(p)], cg_ref.at[slot(p)], send1.at[n], recv1.at[n], p).wait_recv()
        modall_ref[...] = _dot(cg_ref[...], wada_ref[...])
        scatter = [_remote(modall_ref.at[slot(p)], parts_ref.at[slot(pos)], send2.at[n], recv2.at[n], p)
                   for n, p in enumerate(peers)]
        for cp in scatter:
            cp.start()
        parts_ref[slot(pos), :] = modall_ref[slot(pos), :]
        for n, p in enumerate(peers):
            _remote(modall_ref.at[slot(p)], parts_ref.at[slot(p)], send2.at[n], recv2.at[n], p).wait_recv()
        for cp in gather + scatter:
            cp.wait_send()
        for j in range(N_DEV):
            cols = slice(j * SHARD_ADA, (j + 1) * SHARD_ADA)
            mod_ref[:, cols] = parts_ref[8 * j:8 * j + 1, :] + bada_ref[:, cols]
            scall_ref[j:j + 1, :] = cg_ref[8 * j:8 * j + 1, :]

    vmem = pl.BlockSpec(memory_space=pltpu.VMEM)
    return pl.pallas_call(
        body, name="pre_exchange",
        in_specs=[vmem] * 4, out_specs=[vmem] * 3,
        out_shape=[jax.ShapeDtypeStruct((1, 3 * D_MODEL), F32), jax.ShapeDtypeStruct((N_DEV, D_MODEL), F32),
                   jax.ShapeDtypeStruct((1, D_MODEL), F32)],
        scratch_shapes=[pltpu.VMEM((N_DEV * 8, D_MODEL), F32), pltpu.VMEM((N_DEV * 8, SHARD_ADA), F32),
                        pltpu.VMEM((N_DEV * 8, SHARD_ADA), F32)] + [pltpu.SemaphoreType.DMA((N_DEV - 1,))] * 4,
        compiler_params=pltpu.CompilerParams(vmem_limit_bytes=VMEM_LIMIT),
    )(c, w_ada, b_ada, logits)


def weight_gather(win_sh, wout_sh):
    def body(win_ref, wout_ref, wg_ref, woutg_ref, send, recv, local):
        pos = _mesh_pos()
        x, y, c = pos
        sibling = (x, y, 1 - c)
        chips = _other_chips(pos)
        first, passed, mine = [], [], []
        for a, (src, out) in enumerate(((win_ref, wg_ref), (wout_ref, woutg_ref))):
            def copy(k, block, to, src_ref=None, a=a, out=out):
                dst = out.at[_lin(block)]
                return _remote(dst if src_ref is None else src_ref, dst, send.at[7 * a + k], recv.at[7 * a + k], to)
            mine.append(pltpu.make_async_copy(src, out.at[_lin(pos)], local.at[a]))
            first.append(copy(0, pos, sibling, src))
            first += [copy(1 + j, pos, (*chip, c), src) for j, chip in enumerate(chips)]
            passed.append([copy(4 + j, (*chip, c), sibling) for j, chip in enumerate(chips)])
        for cp in mine + first:
            cp.start()
        for a, out in enumerate((wg_ref, woutg_ref)):
            for j, chip in enumerate(chips):
                dst = out.at[_lin((*chip, c))]
                _remote(dst, dst, send.at[7 * a + 1 + j], recv.at[7 * a + 1 + j], pos).wait_recv()
                passed[a][j].start()
        for a, out in enumerate((wg_ref, woutg_ref)):
            dst = out.at[_lin(sibling)]
            _remote(dst, dst, send.at[7 * a], recv.at[7 * a], pos).wait_recv()
            for j, chip in enumerate(chips):
                dst = out.at[_lin((*chip, 1 - c))]
                _remote(dst, dst, send.at[7 * a + 4 + j], recv.at[7 * a + 4 + j], pos).wait_recv()
        for cp in first + passed[0] + passed[1]:
            cp.wait_send()
        for cp in mine:
            cp.wait()

    any_spec = pl.BlockSpec(memory_space=pl.ANY)
    return pl.pallas_call(
        body, name="weight_gather",
        in_specs=[any_spec, any_spec], out_specs=[any_spec, any_spec],
        out_shape=[jax.ShapeDtypeStruct((N_DEV,) + win_sh.shape, win_sh.dtype),
                   jax.ShapeDtypeStruct((N_DEV,) + wout_sh.shape, wout_sh.dtype)],
        scratch_shapes=[pltpu.SemaphoreType.DMA((14,)), pltpu.SemaphoreType.DMA((14,)),
                        pltpu.SemaphoreType.DMA((2,))],
    )(win_sh, wout_sh)


def grad_pair_exchange(g_in, g_out):
    def body(gin_ref, gout_ref, ra_ref, rb_ref, send, recv):
        pos = _mesh_pos()
        x, y, c = pos
        sibling = (x, y, 1 - c)
        copies = []
        for a, (src, dst) in enumerate(((gin_ref, ra_ref), (gout_ref, rb_ref))):
            for q in range(4):
                copies.append(_remote(src.at[2 * q + (1 - c)], dst.at[q], send.at[4 * a + q], recv.at[4 * a + q],
                                      sibling))
        for cp in copies:
            cp.start()
        for cp in copies:
            cp.wait_recv()
        for cp in copies:
            cp.wait_send()

    any_spec = pl.BlockSpec(memory_space=pl.ANY)
    return pl.pallas_call(
        body, name="grad_pair_exchange",
        in_specs=[any_spec, any_spec], out_specs=[any_spec, any_spec],
        out_shape=[jax.ShapeDtypeStruct((4,) + g_in.shape[1:], F32), jax.ShapeDtypeStruct((4,) + g_out.shape[1:], F32)],
        scratch_shapes=[pltpu.SemaphoreType.DMA((8,)), pltpu.SemaphoreType.DMA((8,))],
    )(g_in, g_out)


def pair_sum(g_in, ra, g_out, rb, c_idx):
    tr = 256

    def body(c_ref, gin_ref, ra_ref, gout_ref, rb_ref, sb_ref, sbo_ref):
        del c_ref
        sb_ref[...] = (gin_ref[...] + ra_ref[...]).astype(sb_ref.dtype)
        sbo_ref[...] = gout_ref[...] + rb_ref[...]

    n_i = D_MODEL // tr
    return pl.pallas_call(
        body, name="pair_sum",
        grid_spec=pltpu.PrefetchScalarGridSpec(
            num_scalar_prefetch=1, grid=(4, n_i),
            in_specs=[pl.BlockSpec((None, tr, SHARD_IN), lambda q, i, c: (2 * q + c[0], i, 0)),
                      pl.BlockSpec((None, tr, SHARD_IN), lambda q, i, c: (q, i, 0)),
                      pl.BlockSpec((None, SHARD_OUT // n_i, D_MODEL), lambda q, i, c: (2 * q + c[0], i, 0)),
                      pl.BlockSpec((None, SHARD_OUT // n_i, D_MODEL), lambda q, i, c: (q, i, 0))],
            out_specs=[pl.BlockSpec((None, tr, SHARD_IN), lambda q, i, c: (q, i, 0)),
                       pl.BlockSpec((None, SHARD_OUT // n_i, D_MODEL), lambda q, i, c: (q, i, 0))]),
        out_shape=[jax.ShapeDtypeStruct(ra.shape, _BF), jax.ShapeDtypeStruct(rb.shape, F32)],
        compiler_params=_params(("arbitrary", "arbitrary")),
    )(c_idx, g_in, ra, g_out, rb)


_HBM = pl.BlockSpec(memory_space=pltpu.HBM)
_SEM = pl.BlockSpec(memory_space=pltpu.SEMAPHORE)
_N_CHIP_COPIES = 6


def _chip_copies(sb_ref, sbo_ref, rc_ref, rco_ref, send, recv):
    pos = _mesh_pos()
    copies = []
    for a, (src, dst) in enumerate(((sb_ref, rc_ref), (sbo_ref, rco_ref))):
        for j, chip in enumerate(_other_chips(pos)):
            copies.append(_remote(src.at[2 * chip[0] + chip[1]], dst.at[j], send.at[3 * a + j], recv.at[3 * a + j],
                                  (*chip, pos[2])))
    return copies


def grad_chip_start(sb, sbo):
    def body(sb_ref, sbo_ref, rc_ref, rco_ref, send, recv, sb_thru, sbo_thru, rc_thru, rco_thru, token):
        del sb_thru, sbo_thru, rc_thru, rco_thru
        for cp in _chip_copies(sb_ref, sbo_ref, rc_ref, rco_ref, send, recv):
            cp.start()
        token[...] = jnp.zeros_like(token)

    hbm = lambda a: pltpu.with_memory_space_constraint(a, pltpu.HBM)
    rc = lax.empty((3,) + sb.shape[1:], sb.dtype)
    rco = lax.empty((3,) + sbo.shape[1:], sbo.dtype)
    return pl.pallas_call(
        body, name="grad_chip_start",
        in_specs=[_HBM] * 4,
        out_specs=[_SEM, _SEM, _HBM, _HBM, _HBM, _HBM, pl.BlockSpec(memory_space=pltpu.VMEM)],
        out_shape=[pltpu.SemaphoreType.DMA((_N_CHIP_COPIES,)), pltpu.SemaphoreType.DMA((_N_CHIP_COPIES,)),
                   pltpu.HBM(sb.shape, sb.dtype), pltpu.HBM(sbo.shape, sbo.dtype),
                   pltpu.HBM(rc.shape, rc.dtype), pltpu.HBM(rco.shape, rco.dtype),
                   jax.ShapeDtypeStruct((8, LANE), F32)],
        input_output_aliases={0: 2, 1: 3, 2: 4, 3: 5},
        compiler_params=pltpu.CompilerParams(has_side_effects=pltpu.SideEffectType.DATAFLOW_SIDE_EFFECTING),
    )(hbm(sb), hbm(sbo), hbm(rc), hbm(rco))


def grad_chip_wait(send, recv, sb, sbo, rc, rco, after):
    def body(sb_ref, sbo_ref, rc_ref, rco_ref, send, recv, after_ref, sb_o, sbo_o, rc_o, rco_o):
        del after_ref, sb_o, sbo_o, rc_o, rco_o
        for cp in _chip_copies(sb_ref, sbo_ref, rc_ref, rco_ref, send, recv):
            cp.wait_send()
            cp.wait_recv()

    return pl.pallas_call(
        body, name="grad_chip_wait",
        in_specs=[_HBM] * 4 + [_SEM, _SEM, pl.BlockSpec(memory_space=pl.ANY)],
        out_specs=[_HBM] * 4,
        out_shape=[pltpu.HBM(sb.shape, sb.dtype), pltpu.HBM(sbo.shape, sbo.dtype),
                   pltpu.HBM(rc.shape, rc.dtype), pltpu.HBM(rco.shape, rco.dtype)],
        input_output_aliases={0: 0, 1: 1, 2: 2, 3: 3},
        compiler_params=pltpu.CompilerParams(has_side_effects=pltpu.SideEffectType.DATAFLOW_SIDE_EFFECTING),
    )(sb, sbo, rc, rco, send, recv, after)


def pack_gather(pack):
    def body(pack_ref, packs_ref, psend, precv):
        pos = _mesh_pos()
        me = _lin(pos)
        packs_ref[me] = pack_ref[...]
        peers = [_xor_peer(pos, k) for k in range(1, N_DEV)]
        gather = [_remote(packs_ref.at[me], packs_ref.at[me], psend.at[n], precv.at[n], p) for n, p in enumerate(peers)]
        for cp in gather:
            cp.start()
        for n, p in enumerate(peers):
            _remote(packs_ref.at[_lin(p)], packs_ref.at[_lin(p)], psend.at[n], precv.at[n], p).wait_recv()
        for cp in gather:
            cp.wait_send()

    vmem = pl.BlockSpec(memory_space=pltpu.VMEM)
    return pl.pallas_call(
        body, name="pack_gather", in_specs=[vmem], out_specs=vmem,
        out_shape=jax.ShapeDtypeStruct((N_DEV,) + pack.shape, F32),
        scratch_shapes=[pltpu.SemaphoreType.DMA((N_DEV - 1,)), pltpu.SemaphoreType.DMA((N_DEV - 1,))],
    )(pack)


def pack_rows(vec_mid, vec_ada, dlb):
    def body(mid_ref, ada_ref, dlb_ref, o_ref):
        mid = lambda r: mid_ref[r:r + 1, :]
        rows = [ada_ref[0:1, :], dlb_ref[...], mid(MID_HG_G), mid(MID_RET_G), mid(MID_FINAL_G),
                ada_ref[2:3, :], ada_ref[1:2, :], mid(MID_GATE), mid(MID_LOSS)]
        o_ref[...] = jnp.zeros_like(o_ref)
        for n, row in enumerate(rows):
            o_ref[n:n + 1, :] = row

    vmem = pl.BlockSpec(memory_space=pltpu.VMEM)
    return pl.pallas_call(body, name="pack_rows", in_specs=[vmem] * 3, out_specs=vmem,
                          out_shape=jax.ShapeDtypeStruct((PACK_ROWS, D_MODEL), F32))(vec_mid, vec_ada, dlb)


def _adamw(w, g, m, v):
    m = ADAM_B1 * m + (1.0 - ADAM_B1) * g
    v = ADAM_B2 * v + (1.0 - ADAM_B2) * (g * g)
    m_hat = m / (1.0 - ADAM_B1 ** ADAM_STEP)
    v_hat = v / (1.0 - ADAM_B2 ** ADAM_STEP)
    delta = -ADAM_LR * (m_hat / (jnp.sqrt(v_hat) + ADAM_EPS) + ADAM_WD * w)
    return delta, m, v


def adam_shard(chip_idx, own, parts, w, m, v, name):
    rows, cols = w.shape
    tr = min(rows, 128)

    def body(chip_ref, p0, p1, p2, p3, w_ref, m_ref, v_ref, g_ref, d_ref, nm_ref, nv_ref):
        del chip_ref
        g = ((p0[...].astype(F32) + p1[...].astype(F32)) + p2[...].astype(F32)) + p3[...].astype(F32)
        g_ref[...] = g
        d_ref[...], nm_ref[...], nv_ref[...] = _adamw(w_ref[...], g, m_ref[...], v_ref[...])

    part = lambda q: pl.BlockSpec((None, tr, cols), lambda i, chip, q=q: (q, i, 0))
    tile = pl.BlockSpec((tr, cols), lambda i, chip: (i, 0))
    return pl.pallas_call(
        body, name=name,
        grid_spec=pltpu.PrefetchScalarGridSpec(
            num_scalar_prefetch=1, grid=(rows // tr,),
            in_specs=[pl.BlockSpec((None, tr, cols), lambda i, chip: (chip[0], i, 0)), part(0), part(1), part(2),
                      tile, tile, tile],
            out_specs=[tile] * 4),
        out_shape=[jax.ShapeDtypeStruct(w.shape, F32)] * 4,
        compiler_params=_params(("arbitrary",)),
    )(chip_idx, own, parts, parts, parts, w, m, v)


def adam_ada(sc_t, dmod_all, me_idx, w, m, v):
    def body(me_ref, sc_ref, dm_ref, w_ref, m_ref, v_ref, g_ref, d_ref, nm_ref, nv_ref):
        del me_ref
        g = _dot_f32(sc_ref[...], dm_ref[...])
        g_ref[...] = g
        d_ref[...], nm_ref[...], nv_ref[...] = _adamw(w_ref[...], g, m_ref[...], v_ref[...])

    full = pl.BlockSpec(w.shape, lambda i, me: (0, 0))
    return pl.pallas_call(
        body, name="adam_ada",
        grid_spec=pltpu.PrefetchScalarGridSpec(
            num_scalar_prefetch=1, grid=(1,),
            in_specs=[pl.BlockSpec(sc_t.shape, lambda i, me: (0, 0)),
                      pl.BlockSpec((LANE, SHARD_ADA), lambda i, me: (0, me[0])), full, full, full],
            out_specs=[full] * 4),
        out_shape=[jax.ShapeDtypeStruct(w.shape, F32)] * 4,
        compiler_params=_params(("arbitrary",)),
    )(me_idx, sc_t, dmod_all, w, m, v)


def adam_vectors(packs, lb, params, ms, vs):
    n = len(params)

    def body(*refs):
        packs_ref, lb_ref = refs[0], refs[1]
        w_refs, m_refs, v_refs = refs[2:2 + n], refs[2 + n:2 + 2 * n], refs[2 + 2 * n:2 + 3 * n]
        loss_ref = refs[2 + 3 * n]
        outs = refs[3 + 3 * n:3 + 7 * n]
        tot_ref = refs[3 + 7 * n]
        tot = packs_ref[0]
        for d in range(1, N_DEV):
            tot = tot + packs_ref[d]
        tot_ref[...] = tot
        row = lambda r: tot_ref[r:r + 1, :]
        lbv = lb_ref[...]
        dl0 = row(ROW_LB) * lbv * (1.0 - lbv)
        grads = [[row(ROW_NORM_G)],
                 [jnp.concatenate([row(ROW_SHIFT), row(ROW_SCALE), row(ROW_GATE)], axis=1)],
                 [dl0, -dl0],
                 [row(ROW_HG_G)], [row(ROW_RET_G)], [row(ROW_FINAL_G)]]
        loss_ref[...] = tot_ref[ROW_LOSS:ROW_LOSS + 1, 0:LANE]
        for j, g_rows in enumerate(grads):
            for r, g in enumerate(g_rows):
                rs = slice(r, r + 1)
                d, nm, nv = _adamw(w_refs[j][rs, :], g, m_refs[j][rs, :], v_refs[j][rs, :])
                outs[4 * j][rs, :] = g
                outs[4 * j + 1][rs, :] = d
                outs[4 * j + 2][rs, :] = nm
                outs[4 * j + 3][rs, :] = nv

    vmem = pl.BlockSpec(memory_space=pltpu.VMEM)
    out_shape = [jax.ShapeDtypeStruct((1, LANE), F32)]
    for w in params:
        out_shape += [jax.ShapeDtypeStruct(w.shape, F32)] * 4
    return pl.pallas_call(
        body, name="adam_vectors", in_specs=[vmem] * (2 + 3 * n), out_specs=[vmem] * len(out_shape),
        out_shape=out_shape, scratch_shapes=[pltpu.VMEM((PACK_ROWS, D_MODEL), F32)],
    )(packs, lb, *params, *ms, *vs)


def kernel(x, c, norm_g, w_ada, b_ada, w_in, hg_lb_logits, hg_norm_g, ret_norm_g, w_out, final_g, loss_target, m_norm_g, m_w_ada, m_b_ada, m_w_in, m_hg_lb_logits, m_hg_norm_g, m_ret_norm_g, m_w_out, m_final_g, v_norm_g, v_w_ada, v_b_ada, v_w_in, v_hg_lb_logits, v_hg_norm_g, v_ret_norm_g, v_w_out, v_final_g):
    pos = _mesh_pos()
    me_idx = jnp.reshape(_lin(pos), (1,)).astype(jnp.int32)
    c_idx = jnp.reshape(pos[2], (1,)).astype(jnp.int32)
    vec = lambda a: a.reshape(1, D_MODEL)

    mod, scall, lb = pre_exchange(c, w_ada[0], b_ada, hg_lb_logits)
    wtg, woutg = weight_gather(w_in[0].T.astype(_BF), w_out[0].astype(_BF))
    chip_idx = jnp.reshape(2 * pos[0] + pos[1], (1,)).astype(jnp.int32)

    def start_exchange(dwin, dwout):
        dwout = dwout.reshape(N_DEV, SHARD_OUT, D_MODEL)
        ra, rb = grad_pair_exchange(dwin, dwout)
        sb, sbo = pair_sum(dwin, ra, dwout, rb, c_idx)
        send, recv, sb, sbo, rc, rco, token = grad_chip_start(sb, sbo)
        return token, (send, recv, sb, sbo, rc, rco)

    grad_x, _, _, vec_mid, vec_ada, dlb, pending = device_step(
        x[0], loss_target[0], mod, lb, wtg.reshape(D_IN, D_MODEL), woutg.reshape(D_MODEL, D_MODEL), norm_g,
        hg_norm_g, ret_norm_g, vec(final_g), start_exchange)
    packs = pack_gather(pack_rows(vec_mid, vec_ada, dlb))
    dmod_all = packs[:, ROW_SHIFT:ROW_GATE + 1, :].reshape(N_DEV, 3 * D_MODEL)
    dmod_all = jnp.pad(dmod_all, ((0, LANE - N_DEV), (0, 0)))
    sc_t = jnp.pad(scall.T, ((0, 0), (0, LANE - N_DEV)))
    g_ada, d_ada, nm_ada, nv_ada = adam_ada(sc_t, dmod_all, me_idx, w_ada[0], m_w_ada[0], v_w_ada[0])
    small = adam_vectors(
        packs, lb,
        (norm_g, b_ada, hg_lb_logits, hg_norm_g, ret_norm_g, vec(final_g)),
        (m_norm_g, m_b_ada, m_hg_lb_logits, m_hg_norm_g, m_ret_norm_g, vec(m_final_g)),
        (v_norm_g, v_b_ada, v_hg_lb_logits, v_hg_norm_g, v_ret_norm_g, vec(v_final_g)))
    loss = small[0][0, 0]
    sb, sbo, rc, rco = grad_chip_wait(*pending, small[0])
    g_in, d_in, nm_in, nv_in = adam_shard(chip_idx, sb, rc, w_in[0], m_w_in[0], v_w_in[0], "adam_w_in")
    g_out, d_out, nm_out, nv_out = adam_shard(chip_idx, sbo, rco, w_out[0], m_w_out[0], v_w_out[0], "adam_w_out")
    (g_ng, d_ng, nm_ng, nv_ng), (g_b, d_b, nm_b, nv_b), (g_lb, d_lb, nm_lb, nv_lb), (g_hg, d_hg, nm_hg, nv_hg), \
        (g_rg, d_rg, nm_rg, nv_rg), (g_fg, d_fg, nm_fg, nv_fg) = [small[1 + 4 * j:5 + 4 * j] for j in range(6)]
    flat = lambda a: a.reshape(D_MODEL)

    def group(ng, ada, b, win, lbl, hg, rg, wo, fg):
        return (ng, ada[None], b, win[None], lbl, hg, rg, wo[None], flat(fg))

    return (loss, grad_x[None],
            *group(g_ng, g_ada, g_b, g_in, g_lb, g_hg, g_rg, g_out, g_fg),
            *group(d_ng, d_ada, d_b, d_in, d_lb, d_hg, d_rg, d_out, d_fg),
            *group(nm_ng, nm_ada, nm_b, nm_in, nm_lb, nm_hg, nm_rg, nm_out, nm_fg),
            *group(nv_ng, nv_ada, nv_b, nv_in, nv_lb, nv_hg, nv_rg, nv_out, nv_fg))
```

```python
import functools

import numpy as np
import jax
import jax.numpy as jnp
from jax import lax
from jax.experimental import pallas as pl
from jax.experimental.pallas import tpu as pltpu

F32 = jnp.float32
_BF = jnp.bfloat16

D_MODEL = 1024
N_HEADS = 8
LANE = 128
RET_DK = 64
D_IN = 9216
N_DEV = 8
SHARD_IN = D_IN // N_DEV
SHARD_ADA = 3 * D_MODEL // N_DEV
SHARD_OUT = D_MODEL // N_DEV
N_CB = D_IN // LANE
CB_PER_SHARD = SHARD_IN // LANE
CHUNK = 128
N_LEVELS = 7
EPS = 1e-6
ROPE_BASE = 10000.0
CB_HQ, CB_HF, CB_HI, CB_HZ, CB_RQ, CB_RK, CB_RV, CB_RZ, CB_GA, CB_GB = 0, 8, 16, 24, 32, 36, 40, 48, 56, 64
VMEM_LIMIT = 56 * 1024 * 1024

ADAM_LR, ADAM_B1, ADAM_B2, ADAM_EPS, ADAM_WD, ADAM_STEP = 0.001, 0.9, 0.999, 1e-08, 0.01, 10

_NN = (((1,), (0,)), ((), ()))
_NT = (((1,), (1,)), ((), ()))
_TN = (((0,), (0,)), ((), ()))
MESH = pl.DeviceIdType.MESH


def _dot(a, b, dims=_NN):
    return lax.dot_general(a.astype(_BF), b.astype(_BF), dims, preferred_element_type=F32)


def _split2(a):
    hi = a.astype(_BF)
    lo = (a - hi.astype(F32)).astype(_BF)
    return jnp.concatenate([hi, lo], axis=1)


def _dot_sel(sel, a):
    n = a.shape[1]
    r = lax.dot_general(sel.astype(_BF), _split2(a), _NN, preferred_element_type=F32)
    return r[:, :n] + r[:, n:]


def _dot_f32(a, b):
    def pieces(v):
        p1 = v.astype(_BF)
        r1 = v - p1.astype(F32)
        p2 = r1.astype(_BF)
        p3 = (r1 - p2.astype(F32)).astype(_BF)
        return p1, p2, p3
    a1, a2, a3 = pieces(a)
    b1, b2, b3 = pieces(b)
    d = lambda u, v: lax.dot_general(u, v, _NN, preferred_element_type=F32)
    return ((d(a1, b3) + d(a2, b2) + d(a3, b1)) + (d(a1, b2) + d(a2, b1))) + d(a1, b1)


def _sigmoid(v):
    return 1.0 / (1.0 + jnp.exp(-v))


def _params(sem=None):
    return pltpu.CompilerParams(dimension_semantics=sem, vmem_limit_bytes=VMEM_LIMIT)


def _hgrn_consts():
    c, nl = CHUNK, N_LEVELS
    t = np.arange(c)[:, None]
    j = np.arange(c)[None, :]
    sel = [j <= t]
    masks = [j == t]
    for l in range(1, nl + 1):
        m = ((t >> l) << l) + (1 << (l - 1)) - 1
        sec = t > m
        sel.append(np.where(sec, (j > m) & (j <= t), (j > t) & (j <= m)))
        same = (t >> l) == (j >> l)
        masks.append(same & sec & (j <= m))
    sel.append(j > t)
    sel = np.concatenate(sel, 0).astype(np.float32)
    masks = np.stack(masks).astype(np.float32)
    sgn = np.stack([np.where((t & (1 << (l - 1))) != 0, 1.0, -1.0) * np.ones((1, LANE)) for l in range(3, nl + 1)])
    return dict(tri=jnp.asarray(sel[:c], _BF),
                lvl=jnp.asarray(masks, F32),
                sgn=jnp.asarray(sgn, F32),
                sel_t=jnp.asarray(sel.T, _BF),
                lvl_b=jnp.asarray(masks, _BF),
                lvlt_b=jnp.asarray(np.swapaxes(masks, 1, 2), _BF))


def _level_exponents(b, logf, b_scr, sgn_ref):
    c = CHUNK
    b_scr[...] = b
    row = lax.broadcasted_iota(jnp.int32, (c, LANE), 0)
    nxt = pltpu.roll(logf, c - 1, 0)
    prv = pltpu.roll(logf, 1, 0)
    r4 = row & 3
    out = [jnp.where((row & 1) == 1, logf, 0.0),
           jnp.where(r4 == 0, nxt, jnp.where(r4 == 1, 0.0, jnp.where(r4 == 2, logf, logf + prv)))]
    for l in range(3, N_LEVELS + 1):
        size, half = 1 << l, 1 << (l - 1)
        ref = jnp.concatenate([jnp.broadcast_to(b_scr[i * size + half - 1:i * size + half, :], (size, LANE))
                               for i in range(c // size)], axis=0)
        out.append((b - ref) * sgn_ref[l - 3])
    return out


def _hgrn_chunk(hq, hf, hi, lbv, tri_ref, sgn_ref, b_scr):
    sq = _sigmoid(hq)
    q = hq * sq
    sg = _sigmoid(hf)
    omlb = 1.0 - lbv
    f = lbv + omlb * sg
    k = 1.0 - f
    logf = jnp.log(f)
    b = _dot_sel(tri_ref[...], logf)
    bc = jnp.sum(logf, axis=0, keepdims=True)
    lev = [None] + [jnp.exp(e) for e in _level_exponents(b, logf, b_scr, sgn_ref)]
    return dict(sq=sq, q=q, sg=sg, omlb=omlb, f=f, k=k, v=hi, eb=jnp.exp(b), erem=jnp.exp(bc - b), ebc=jnp.exp(bc),
                lev=lev)


def _blockdiag(a, b):
    z = jnp.zeros_like(a)
    return jnp.concatenate([jnp.concatenate([a, z], axis=1), jnp.concatenate([z, b], axis=1)], axis=0)


def _level_operands(a):
    q, k = a["q"], a["k"]
    ql = [q.astype(_BF)] + [(q * a["lev"][l]).astype(_BF) for l in range(1, N_LEVELS + 1)]
    kl = [k.astype(_BF)] + [(k * a["lev"][l]).astype(_BF) for l in range(1, N_LEVELS + 1)]
    pairs = range(0, N_LEVELS + 1, 2)
    return ([jnp.concatenate([ql[l], ql[l + 1]], axis=1) for l in pairs], [_blockdiag(kl[l], kl[l + 1]) for l in pairs],
            ql, kl)


def _hgrn_scores(a, lvl_ref, q_pairs, k_diags):
    acc = None
    for n, (qp, kd) in enumerate(zip(q_pairs, k_diags)):
        both = lax.dot_general(qp, kd, _NT, preferred_element_type=F32)
        part = lvl_ref[2 * n] * both[:, :CHUNK] + lvl_ref[2 * n + 1] * both[:, CHUNK:]
        acc = part if acc is None else acc + part
    return acc


SCAN_UNROLL = 4
RET_UNROLL = 8


def _writeback_reserve(step, make_copies):
    slot = step % 2

    @pl.when(step >= 2)
    def _():
        for cp in make_copies(slot):
            cp.wait()

    return slot


def _writeback_commit(step, n_steps, slot, make_copies):
    for cp in make_copies(slot):
        cp.start()

    @pl.when(step == n_steps - 1)
    def _():
        for cp in make_copies(slot):
            cp.wait()
        if n_steps > 1:
            for cp in make_copies(1 - slot):
                cp.wait()


def _resident(const):
    zeros = (0,) * const.ndim
    return pl.BlockSpec(const.shape, lambda p, t: zeros)


def _time_block(t_len):
    return min(t_len, 1024)


def hgrn_forward(pb, lb, t_len):
    nc = t_len // CHUNK
    tb = _time_block(t_len)
    ncb = tb // CHUNK
    consts = _hgrn_consts()
    operands = [consts[n] for n in ("tri", "lvl", "sgn")]

    def body(hq_ref, hf_ref, hi_ref, lb_ref, tri_ref, lvl_ref, sgn_ref, o_ref, ssave_ref, st_ref, b_scr):
        @pl.when(pl.program_id(1) == 0)
        def _():
            st_ref[...] = jnp.zeros_like(st_ref)

        def chunk(ci, carry):
            r = pl.ds(pl.multiple_of(ci * CHUNK, CHUNK), CHUNK)
            for hd in range(2):
                lbv = lb_ref[:, hd * LANE:(hd + 1) * LANE]
                a = _hgrn_chunk(hq_ref[hd, r, :], hf_ref[hd, r, :], hi_ref[hd, r, :], lbv, tri_ref, sgn_ref,
                                b_scr.at[hd])
                q_pairs, k_diags, _, _ = _level_operands(a)
                st = st_ref[hd]
                ssave_ref[hd, ci] = st
                scores = _hgrn_scores(a, lvl_ref, q_pairs, k_diags)
                o_ref[hd, r, :] = _dot(a["q"] * a["eb"], st, _NT) + _dot(scores, a["v"])
                st_ref[hd] = st * a["ebc"] + _dot(a["v"], a["k"] * a["erem"], _TN)
            return carry

        lax.fori_loop(0, ncb, chunk, 0, unroll=SCAN_UNROLL)

    pair = lambda base: pl.BlockSpec((2, tb, LANE), lambda p, t, base=base: (base // 2 + p, t, 0))
    return pl.pallas_call(
        body, name="hgrn_fwd", grid=(N_HEADS // 2, t_len // tb),
        in_specs=[pair(CB_HQ), pair(CB_HF), pair(CB_HI),
                  pl.BlockSpec((1, 2 * LANE), lambda p, t: (0, p))] + [_resident(c) for c in operands],
        out_specs=[pl.BlockSpec((2, tb, LANE), lambda p, t: (p, t, 0)),
                   pl.BlockSpec((2, ncb, LANE, LANE), lambda p, t: (p, t, 0, 0))],
        out_shape=[jax.ShapeDtypeStruct((N_HEADS, t_len, LANE), F32),
                   jax.ShapeDtypeStruct((N_HEADS, nc, LANE, LANE), F32)],
        scratch_shapes=[pltpu.VMEM((2, LANE, LANE), F32), pltpu.VMEM((2, CHUNK, LANE), F32)],
        compiler_params=_params(("arbitrary", "arbitrary")),
    )(pb, pb, pb, lb, *operands)


def hgrn_backward(pb, lb, do, ssave, dpb, t_len):
    tb = _time_block(t_len)
    ncb, ntb = tb // CHUNK, t_len // tb
    consts = _hgrn_consts()
    operands = [consts[n] for n in ("tri", "lvl", "sgn", "sel_t", "lvl_b", "lvlt_b")]

    def body(hq_ref, hf_ref, hi_ref, lb_ref, do_ref, ssave_ref, tri_ref, lvl_ref, sgn_ref, selt_ref, lvlb_ref,
             lvltb_ref, dpb_in, dpb_ref, dlb_ref, dq_buf, df_buf, di_buf, dst_ref, b_scr, sems):
        del dpb_in
        p, t = pl.program_id(0), pl.program_id(1)
        step = p * ntb + t
        rows = pl.ds(pl.multiple_of((ntb - 1 - t) * tb, tb), tb)

        def out_copies(sl):
            return [pltpu.make_async_copy(buf.at[sl], dpb_ref.at[pl.ds(base + 2 * p, 2), rows], sems.at[sl, n])
                    for n, (buf, base) in enumerate(((dq_buf, CB_HQ), (df_buf, CB_HF), (di_buf, CB_HI)))]

        slot = _writeback_reserve(step, out_copies)

        @pl.when(t == 0)
        def _():
            dst_ref[...] = jnp.zeros_like(dst_ref)
            dlb_ref[...] = jnp.zeros_like(dlb_ref)

        def chunk(i, carry):
            ci = ncb - 1 - i
            r = pl.ds(pl.multiple_of(ci * CHUNK, CHUNK), CHUNK)
            for hd in range(2):
                head_chunk(hd, ci, r)
            return carry

        def head_chunk(hd, ci, r):
            lbv = lb_ref[:, hd * LANE:(hd + 1) * LANE]
            hq = hq_ref[hd, r, :]
            a = _hgrn_chunk(hq, hf_ref[hd, r, :], hi_ref[hd, r, :], lbv, tri_ref, sgn_ref, b_scr.at[hd])
            q_pairs, k_diags, ql, kl = _level_operands(a)
            q, k, v = a["q"], a["k"], a["v"]
            g = do_ref[hd, r, :]
            st0 = ssave_ref[hd, ci]
            dst = dst_ref[hd]
            scores = _hgrn_scores(a, lvl_ref, q_pairs, k_diags)
            da = _dot(g, v, _NT)
            da_t = _dot(v, g, _NT)
            kb = k * a["erem"]
            qb = q * a["eb"]
            dv = _dot(scores, g, _TN) + _dot(kb, dst, _NT)
            dq_inter = _dot(g, st0) * a["eb"]
            dk_state = _dot(v, dst) * a["erem"]
            dq, dk = dq_inter, dk_state
            de = [q * dq_inter]
            da_b, dat_b = da.astype(_BF), da_t.astype(_BF)
            for n in range(len(k_diags)):
                l0, l1 = 2 * n, 2 * n + 1
                da_pair = jnp.concatenate([lvlb_ref[l0] * da_b, lvlb_ref[l1] * da_b], axis=1)
                dat_pair = jnp.concatenate([lvltb_ref[l0] * dat_b, lvltb_ref[l1] * dat_b], axis=1)
                dq_both = lax.dot_general(da_pair, k_diags[n], _NN, preferred_element_type=F32)
                dk_both = lax.dot_general(dat_pair, _blockdiag(ql[l0], ql[l1]), _NN, preferred_element_type=F32)
                for l, cols in ((l0, slice(0, LANE)), (l1, slice(LANE, 2 * LANE))):
                    dql, dkl = dq_both[:, cols], dk_both[:, cols]
                    if l > 0:
                        e = a["lev"][l]
                        dql, dkl = dql * e, dkl * e
                        de.append(q * dql + k * dkl)
                    dq = dq + dql
                    dk = dk + dkl
            de.append(k * dk_state)
            dst_ref[hd] = dst * a["ebc"] + _dot(g, qb, _TN)
            dbc = jnp.sum(dst * st0, axis=0, keepdims=True) * a["ebc"]
            de2 = lax.dot_general(selt_ref[...], _split2(jnp.concatenate(de, axis=0)), _NN,
                                  preferred_element_type=F32)
            dlogf = de2[:, :LANE] + de2[:, LANE:] + dbc
            sq, sg = a["sq"], a["sg"]
            df = dlogf / a["f"] - dk
            dq_buf[slot, hd, r, :] = dq * (sq * (1.0 + hq * (1.0 - sq)))
            df_buf[slot, hd, r, :] = df * a["omlb"] * sg * (1.0 - sg)
            di_buf[slot, hd, r, :] = dv
            cols = slice(hd * LANE, (hd + 1) * LANE)
            dlb_ref[:, cols] = dlb_ref[:, cols] + jnp.sum(df * (1.0 - sg), axis=0, keepdims=True)

        lax.fori_loop(0, ncb, chunk, 0, unroll=SCAN_UNROLL)
        _writeback_commit(step, (N_HEADS // 2) * ntb, slot, out_copies)

    pair = lambda base: pl.BlockSpec((2, tb, LANE), lambda p, t, base=base: (base // 2 + p, ntb - 1 - t, 0))
    any_spec = pl.BlockSpec(memory_space=pl.ANY)
    return pl.pallas_call(
        body, name="hgrn_bwd", grid=(N_HEADS // 2, ntb),
        in_specs=[pair(CB_HQ), pair(CB_HF), pair(CB_HI),
                  pl.BlockSpec((1, 2 * LANE), lambda p, t: (0, p)),
                  pair(0),
                  pl.BlockSpec((2, ncb, LANE, LANE), lambda p, t: (p, ntb - 1 - t, 0, 0))]
        + [_resident(c) for c in operands] + [any_spec],
        out_specs=[any_spec, pl.BlockSpec((1, 2 * LANE), lambda p, t: (0, p))],
        out_shape=[jax.ShapeDtypeStruct(dpb.shape, F32), jax.ShapeDtypeStruct((1, D_MODEL), F32)],
        scratch_shapes=[pltpu.VMEM((2, 2, tb, LANE), F32)] * 3 + [
            pltpu.VMEM((2, LANE, LANE), F32), pltpu.VMEM((2, CHUNK, LANE), F32), pltpu.SemaphoreType.DMA((2, 3))],
        input_output_aliases={6 + len(operands): 0},
        compiler_params=_params(("arbitrary", "arbitrary")),
    )(pb, pb, pb, lb, do, ssave, *operands, dpb)


def _rope_tables(t_len):
    half = RET_DK // 2
    inv_freq = 1.0 / (ROPE_BASE ** jnp.linspace(0.0, 1.0, half, dtype=F32))
    ang = jnp.arange(t_len, dtype=jnp.int32).astype(F32)[:, None] * inv_freq[None, :]
    cos, sin = jnp.cos(ang), jnp.sin(ang)
    cos_t = jnp.concatenate([cos, cos, cos, cos], axis=1)
    sin_t = jnp.concatenate([-sin, sin, -sin, sin], axis=1)
    return cos_t, sin_t


def _swap_halves(v):
    half = RET_DK // 2
    lane = lax.broadcasted_iota(jnp.int32, v.shape, 1)
    first = (lane & (RET_DK - 1)) < half
    return jnp.where(first, pltpu.roll(v, LANE - half, 1), pltpu.roll(v, half, 1))


def _ret_head_consts(hidx):
    c = CHUNK
    hf = jnp.full((1, LANE), hidx, jnp.int32).astype(F32)
    lg = jnp.log(1.0 - jnp.exp(-(5.0 + hf) * np.float32(np.log(2.0))))
    row = lax.broadcasted_iota(jnp.int32, (c, c), 0)
    col = lax.broadcasted_iota(jnp.int32, (c, c), 1)
    rel = (row - col).astype(F32)
    dm = jnp.where(rel >= 0, jnp.exp(lg[:, :1] * jnp.maximum(rel, 0.0)), 0.0)
    dm_t = jnp.where(rel <= 0, jnp.exp(lg[:, :1] * jnp.maximum(-rel, 0.0)), 0.0)
    idx = lax.broadcasted_iota(jnp.int32, (c, LANE), 0).astype(F32)
    zeta = jnp.exp(lg * (c - 1.0 - idx))
    xi = jnp.exp(lg * (idx + 1.0))
    cdec = jnp.exp(lg * float(c))
    return dm, zeta, xi, cdec, dm_t


def _lane_mask(which):
    lane = lax.broadcasted_iota(jnp.int32, (1, LANE), 1)
    return ((lane // RET_DK) == which).astype(F32)


def retention_forward(pb, cos_t, sin_t, t_len):
    nc = t_len // CHUNK

    tb = _time_block(t_len)
    ncb = tb // CHUNK

    def body(rq_ref, rk_ref, rv_ref, cos_ref, sin_ref, o_ref, rsave_ref, st_ref):
        p = pl.program_id(0)

        @pl.when(pl.program_id(1) == 0)
        def _():
            st_ref[...] = jnp.zeros_like(st_ref)

        consts = [_ret_head_consts(2 * p + hd) for hd in range(2)]

        def chunk(ci, carry):
            r = pl.ds(pl.multiple_of(ci * CHUNK, CHUNK), CHUNK)
            cs, sn = cos_ref[r, :], sin_ref[r, :]
            q = rq_ref[r, :]
            k = rk_ref[r, :]
            q = q * cs + _swap_halves(q) * sn
            k = (k * cs + _swap_halves(k) * sn) * RET_DK ** -0.5
            for hd in range(2):
                dm, zeta, xi, cdec, _ = consts[hd]
                lm = _lane_mask(hd)
                qh, kh = q * lm, k * lm
                v = rv_ref[hd, r, :]
                st = st_ref[hd]
                rsave_ref[hd, ci] = st
                scores = _dot(qh, kh, _NT) * dm
                o_ref[hd, r, :] = _dot(scores, v) + _dot(qh * xi, st, _NT)
                st_ref[hd] = st * cdec + _dot(v, kh * zeta, _TN)
            return carry

        lax.fori_loop(0, ncb, chunk, 0, unroll=RET_UNROLL)

    return pl.pallas_call(
        body, name="ret_fwd", grid=(N_HEADS // 2, t_len // tb),
        in_specs=[pl.BlockSpec((None, tb, LANE), lambda p, t: (CB_RQ + p, t, 0)),
                  pl.BlockSpec((None, tb, LANE), lambda p, t: (CB_RK + p, t, 0)),
                  pl.BlockSpec((2, tb, LANE), lambda p, t: (CB_RV // 2 + p, t, 0)),
                  pl.BlockSpec((tb, LANE), lambda p, t: (t, 0)),
                  pl.BlockSpec((tb, LANE), lambda p, t: (t, 0))],
        out_specs=[pl.BlockSpec((2, tb, LANE), lambda p, t: (p, t, 0)),
                   pl.BlockSpec((2, ncb, LANE, LANE), lambda p, t: (p, t, 0, 0))],
        out_shape=[jax.ShapeDtypeStruct((N_HEADS, t_len, LANE), F32),
                   jax.ShapeDtypeStruct((N_HEADS, nc, LANE, LANE), F32)],
        scratch_shapes=[pltpu.VMEM((2, LANE, LANE), F32)],
        compiler_params=_params(("arbitrary", "arbitrary")),
    )(pb, pb, pb, cos_t, sin_t)


def retention_backward(pb, cos_t, sin_t, do, rsave, dpb, t_len):
    tb = _time_block(t_len)
    ncb, ntb = tb // CHUNK, t_len // tb

    def body(rq_ref, rk_ref, rv_ref, cos_ref, sin_ref, do_ref, rsave_ref, dpb_in,
             dpb_ref, dq_buf, dk_buf, dv_buf, dst_ref, sems):
        del dpb_in
        p, t = pl.program_id(0), pl.program_id(1)
        step = p * ntb + t
        rows = pl.ds(pl.multiple_of((ntb - 1 - t) * tb, tb), tb)

        def out_copies(sl):
            return [pltpu.make_async_copy(dq_buf.at[sl], dpb_ref.at[CB_RQ + p, rows], sems.at[sl, 0]),
                    pltpu.make_async_copy(dk_buf.at[sl], dpb_ref.at[CB_RK + p, rows], sems.at[sl, 1]),
                    pltpu.make_async_copy(dv_buf.at[sl], dpb_ref.at[pl.ds(CB_RV + 2 * p, 2), rows], sems.at[sl, 2])]

        slot = _writeback_reserve(step, out_copies)

        @pl.when(t == 0)
        def _():
            dst_ref[...] = jnp.zeros_like(dst_ref)

        consts = [_ret_head_consts(2 * p + hd) for hd in range(2)]

        def chunk(i, carry):
            ci = ncb - 1 - i
            r = pl.ds(pl.multiple_of(ci * CHUNK, CHUNK), CHUNK)
            cs, sn = cos_ref[r, :], sin_ref[r, :]
            q = rq_ref[r, :]
            k = rk_ref[r, :]
            q = q * cs + _swap_halves(q) * sn
            k = (k * cs + _swap_halves(k) * sn) * RET_DK ** -0.5
            dq, dk = None, None
            for hd in range(2):
                dm, zeta, xi, cdec, dm_t = consts[hd]
                lm = _lane_mask(hd)
                qh, kh = q * lm, k * lm
                v = rv_ref[hd, r, :]
                g = do_ref[hd, r, :]
                st0 = rsave_ref[hd, ci]
                dst = dst_ref[hd]
                scores_t = _dot(kh, qh, _NT) * dm_t
                dsc = _dot(g, v, _NT) * dm
                dsc_t = _dot(v, g, _NT) * dm_t
                dqh = _dot(dsc, kh) + _dot(g, st0) * xi
                dkh = _dot(dsc_t, qh) + _dot(v, dst) * zeta
                dv_buf[slot, hd, r, :] = _dot(scores_t, g) + _dot(kh * zeta, dst, _NT)
                dst_ref[hd] = dst * cdec + _dot(g, qh * xi, _TN)
                dq = dqh if dq is None else dq + dqh
                dk = dkh if dk is None else dk + dkh
            dk = dk * (RET_DK ** -0.5)
            dq_buf[slot, r, :] = dq * cs - _swap_halves(dq) * sn
            dk_buf[slot, r, :] = dk * cs - _swap_halves(dk) * sn
            return carry

        lax.fori_loop(0, ncb, chunk, 0, unroll=RET_UNROLL)
        _writeback_commit(step, (N_HEADS // 2) * ntb, slot, out_copies)

    any_spec = pl.BlockSpec(memory_space=pl.ANY)
    return pl.pallas_call(
        body, name="ret_bwd", grid=(N_HEADS // 2, ntb),
        in_specs=[pl.BlockSpec((None, tb, LANE), lambda p, t: (CB_RQ + p, ntb - 1 - t, 0)),
                  pl.BlockSpec((None, tb, LANE), lambda p, t: (CB_RK + p, ntb - 1 - t, 0)),
                  pl.BlockSpec((2, tb, LANE), lambda p, t: (CB_RV // 2 + p, ntb - 1 - t, 0)),
                  pl.BlockSpec((tb, LANE), lambda p, t: (ntb - 1 - t, 0)),
                  pl.BlockSpec((tb, LANE), lambda p, t: (ntb - 1 - t, 0)),
                  pl.BlockSpec((2, tb, LANE), lambda p, t: (p, ntb - 1 - t, 0)),
                  pl.BlockSpec((2, ncb, LANE, LANE), lambda p, t: (p, ntb - 1 - t, 0, 0)),
                  any_spec],
        out_specs=any_spec,
        out_shape=jax.ShapeDtypeStruct(dpb.shape, F32),
        scratch_shapes=[pltpu.VMEM((2, tb, LANE), F32), pltpu.VMEM((2, tb, LANE), F32),
                        pltpu.VMEM((2, 2, tb, LANE), F32), pltpu.VMEM((2, LANE, LANE), F32),
                        pltpu.SemaphoreType.DMA((2, 3))],
        input_output_aliases={7: 0},
        compiler_params=_params(("arbitrary", "arbitrary")),
    )(pb, pb, pb, cos_t, sin_t, do, rsave, dpb)


def _row_tile(t_len, want):
    return min(want, t_len)


PAIR_CB = 2 * CB_PER_SHARD


def proj_forward(h, wt, t_len):
    tm = _row_tile(t_len, 1024)

    def body(h_ref, w_ref, o_ref):
        acc = _dot(h_ref[...], w_ref[...], _NT)
        for jj in range(PAIR_CB):
            o_ref[jj] = acc[:, jj * LANE:(jj + 1) * LANE]

    return pl.pallas_call(
        body, name="proj_fwd", grid=(N_DEV // 2, t_len // tm),
        in_specs=[pl.BlockSpec((tm, D_MODEL), lambda j, i: (i, 0)),
                  pl.BlockSpec((PAIR_CB * LANE, D_MODEL), lambda j, i: (j, 0))],
        out_specs=pl.BlockSpec((PAIR_CB, tm, LANE), lambda j, i: (j, i, 0)),
        out_shape=jax.ShapeDtypeStruct((N_CB, t_len, LANE), F32),
        compiler_params=_params(("arbitrary", "arbitrary")),
    )(h, wt)


def proj_backward_input(dpb, wt, token, x, dy, norm_g, scale1p, t_len):
    tm = _row_tile(t_len, 256)

    def body(a_ref, wt_hbm, token_ref, x_ref, dy_ref, g_ref, sc_ref, gx_ref, vec_ref, w_ref, sem):
        del token_ref
        i = pl.program_id(0)

        @pl.when(i == 0)
        def _():
            cp = pltpu.make_async_copy(wt_hbm, w_ref, sem)
            cp.start()
            cp.wait()

        a = jnp.concatenate([a_ref[jj].astype(_BF) for jj in range(N_CB)], axis=1)
        dhv = _dot(a, w_ref[...])
        xv, g, sc = x_ref[...], g_ref[...], sc_ref[...]
        r = lax.rsqrt(jnp.mean(xv * xv, axis=-1, keepdims=True) + EPS)
        xn = xv * r
        dxn = dhv * (g * sc)
        gx_ref[...] = dy_ref[...] + r * dxn - xn * (r * r) * jnp.mean(xv * dxn, axis=-1, keepdims=True)
        t = dhv * xn
        _acc_rows(vec_ref, i, [jnp.sum(t * sc, axis=0, keepdims=True),
                               jnp.sum(t * g, axis=0, keepdims=True),
                               jnp.sum(dhv, axis=0, keepdims=True)])

    row = pl.BlockSpec((tm, D_MODEL), lambda i: (i, 0))
    return pl.pallas_call(
        body, name="proj_bwd_input", grid=(t_len // tm,),
        in_specs=[pl.BlockSpec((N_CB, tm, LANE), lambda i: (0, i, 0)),
                  pl.BlockSpec(memory_space=pl.ANY),
                  pl.BlockSpec(token.shape, lambda i: (0, 0)),
                  row, row, _vec_spec(), _vec_spec()],
        out_specs=[row, pl.BlockSpec((8, D_MODEL), lambda i: (0, 0))],
        out_shape=[jax.ShapeDtypeStruct((t_len, D_MODEL), F32), jax.ShapeDtypeStruct((8, D_MODEL), F32)],
        scratch_shapes=[pltpu.VMEM(wt.shape, wt.dtype), pltpu.SemaphoreType.DMA],
        compiler_params=_params(("arbitrary",)),
    )(dpb, wt, token, x, dy, norm_g, scale1p)


def proj_backward_weight(h_t, dpb, t_len):
    tk = _row_tile(t_len, 1024)

    def body(h_ref, b_ref, o_ref):
        k = pl.program_id(1)
        b = jnp.concatenate([b_ref[jj].astype(_BF) for jj in range(PAIR_CB)], axis=1)
        part = _dot(h_ref[...], b)

        @pl.when(k == 0)
        def _():
            for s in range(2):
                o_ref[s] = part[:, s * SHARD_IN:(s + 1) * SHARD_IN]

        @pl.when(k > 0)
        def _():
            for s in range(2):
                o_ref[s] = o_ref[s] + part[:, s * SHARD_IN:(s + 1) * SHARD_IN]

    return pl.pallas_call(
        body, name="proj_bwd_weight", grid=(N_DEV // 2, t_len // tk),
        in_specs=[pl.BlockSpec((D_MODEL, tk), lambda j, k: (0, k)),
                  pl.BlockSpec((PAIR_CB, tk, LANE), lambda j, k: (j, k, 0))],
        out_specs=pl.BlockSpec((2, D_MODEL, SHARD_IN), lambda j, k: (j, 0, 0)),
        out_shape=jax.ShapeDtypeStruct((N_DEV, D_MODEL, SHARD_IN), F32),
        compiler_params=_params(("arbitrary", "arbitrary")),
    )(h_t, dpb)


def _vec_spec():
    return pl.BlockSpec((1, D_MODEL), lambda i: (0, 0))


def _acc_rows(ref, i, rows):
    @pl.when(i == 0)
    def _():
        ref[...] = jnp.zeros_like(ref)

    for n, row in enumerate(rows):
        ref[n:n + 1, :] = ref[n:n + 1, :] + row


def adaln_forward(x, norm_g, scale1p, shift, t_len):
    tm = _row_tile(t_len, 512)

    def body(x_ref, g_ref, sc_ref, sh_ref, h_ref, ht_ref):
        xv = x_ref[...]
        r = lax.rsqrt(jnp.mean(xv * xv, axis=-1, keepdims=True) + EPS)
        h = xv * r * g_ref[...] * sc_ref[...] + sh_ref[...]
        h_ref[...] = h.astype(h_ref.dtype)
        ht_ref[...] = h.T.astype(ht_ref.dtype)

    return pl.pallas_call(
        body, name="adaln_fwd", grid=(t_len // tm,),
        in_specs=[pl.BlockSpec((tm, D_MODEL), lambda i: (i, 0)), _vec_spec(), _vec_spec(), _vec_spec()],
        out_specs=[pl.BlockSpec((tm, D_MODEL), lambda i: (i, 0)), pl.BlockSpec((D_MODEL, tm), lambda i: (0, i))],
        out_shape=[jax.ShapeDtypeStruct((t_len, D_MODEL), _BF), jax.ShapeDtypeStruct((D_MODEL, t_len), _BF)],
        compiler_params=_params(("arbitrary",)),
    )(x, norm_g, scale1p, shift)


def _head_norm(o, g):
    r = lax.rsqrt(jnp.mean(o * o, axis=-1, keepdims=True) + EPS)
    return r, o * r * g


def _group_spec(tm, cb):
    return pl.BlockSpec((N_HEADS, tm, LANE), lambda i, cb=cb: (cb // N_HEADS, i, 0))


MID_FINAL_G, MID_GATE, MID_LOSS, MID_HG_G, MID_RET_G = range(5)


def middle(x, target, oa, ob, pb, wout, gate, final_g, hg_g, ret_g, t_len):
    tm = _row_tile(t_len, 256)
    n_steps = t_len // tm

    def body(x_ref, t_ref, oa_ref, ob_ref, hz_ref, rz_ref, ga_ref, gb_ref, w_ref, gate_ref, fg_ref, hg_ref, rg_ref,
             dy_ref, doa_ref, dob_ref, dw_ref, vec_ref, dpb_ref, m_scr, dm_scr, keep, bufs, sems):
        i = pl.program_id(0)
        rows = pl.ds(pl.multiple_of(i * tm, tm), tm)

        def group_copies(sl):
            return [pltpu.make_async_copy(bufs.at[sl, n], dpb_ref.at[pl.ds(cb, N_HEADS), rows], sems.at[sl, n])
                    for n, cb in enumerate((CB_HZ, CB_RZ, CB_GA, CB_GB))]
        sides = ((oa_ref, hz_ref, ga_ref, hg_ref, doa_ref), (ob_ref, rz_ref, gb_ref, rg_ref, dob_ref))
        for hh in range(N_HEADS):
            ls = slice(hh * LANE, (hh + 1) * LANE)
            acc = None
            for side, (o_ref, z_ref, gt_ref, g_ref, _) in enumerate(sides):
                o = o_ref[hh]
                rr = lax.rsqrt(jnp.mean(o * o, axis=-1, keepdims=True) + EPS)
                orr = o * rr
                zz = z_ref[hh]
                sz = _sigmoid(zz)
                sgt = _sigmoid(gt_ref[hh])
                keep[side, hh, 0] = orr
                keep[side, hh, 1] = sz
                keep[side, hh, 2] = sgt
                keep[side, hh, 3] = jnp.broadcast_to(rr, orr.shape)
                u = sgt * ((orr * g_ref[:, ls]) * (zz * sz))
                acc = u if acc is None else acc + u
            m_scr[:, ls] = acc.astype(m_scr.dtype)
        zv = _dot(m_scr[...], w_ref[...])
        gt, fg = gate_ref[...], fg_ref[...]
        y = x_ref[...] + gt * zv
        r = lax.rsqrt(jnp.mean(y * y, axis=-1, keepdims=True) + EPS)
        yn = y * r
        err = yn * fg - t_ref[...]
        loss = 0.5 * jnp.sum(jnp.mean(err * err, axis=-1, keepdims=True), axis=0, keepdims=True)
        dout = err * (1.0 / D_MODEL)
        gd = dout * fg
        dy = r * gd - yn * (r * r) * jnp.mean(y * gd, axis=-1, keepdims=True)
        dy_ref[...] = dy
        dz = (dy * gt).astype(_BF)
        dm_scr[...] = _dot(dz, w_ref[...], _NT)
        part = _dot(m_scr[...], dz, _TN)

        @pl.when(i == 0)
        def _():
            dw_ref[...] = part

        @pl.when(i > 0)
        def _():
            dw_ref[...] = dw_ref[...] + part

        slot = _writeback_reserve(i, group_copies)
        dg = [[], []]
        for hh in range(N_HEADS):
            ls = slice(hh * LANE, (hh + 1) * LANE)
            dmh = dm_scr[:, ls]
            for side, (o_ref, z_ref, gt_ref, g_ref, do_ref) in enumerate(sides):
                zz, g = z_ref[hh], g_ref[:, ls]
                orr, sz, sgt, rr = keep[side, hh, 0], keep[side, hh, 1], keep[side, hh, 2], keep[side, hh, 3]
                n = orr * g
                silu = zz * sz
                du = dmh * sgt
                bufs[slot, 2 + side, hh] = dmh * (n * silu) * (sgt * (1.0 - sgt))
                bufs[slot, side, hh] = du * n * (sz * (1.0 + zz * (1.0 - sz)))
                dn = du * silu
                dg[side].append(jnp.sum(dn * orr, axis=0, keepdims=True))
                gdn = dn * g
                do_ref[hh] = rr * (gdn - orr * jnp.mean(orr * gdn, axis=-1, keepdims=True))
        _acc_rows(vec_ref, i, [jnp.sum(dout * yn, axis=0, keepdims=True),
                               jnp.sum(dy * zv, axis=0, keepdims=True),
                               jnp.broadcast_to(loss, (1, D_MODEL)),
                               jnp.concatenate(dg[0], axis=1), jnp.concatenate(dg[1], axis=1)])
        _writeback_commit(i, n_steps, slot, group_copies)

    row = pl.BlockSpec((tm, D_MODEL), lambda i: (i, 0))
    head = pl.BlockSpec((N_HEADS, tm, LANE), lambda i: (0, i, 0))
    full = pl.BlockSpec((D_MODEL, D_MODEL), lambda i: (0, 0))
    return pl.pallas_call(
        body, name="middle", grid=(n_steps,),
        in_specs=[row, row, head, head, _group_spec(tm, CB_HZ), _group_spec(tm, CB_RZ), _group_spec(tm, CB_GA),
                  _group_spec(tm, CB_GB), full, _vec_spec(), _vec_spec(), _vec_spec(), _vec_spec()],
        out_specs=[row, head, head, full, pl.BlockSpec((8, D_MODEL), lambda i: (0, 0)),
                   pl.BlockSpec(memory_space=pl.ANY)],
        out_shape=[jax.ShapeDtypeStruct((t_len, D_MODEL), F32),
                   jax.ShapeDtypeStruct((N_HEADS, t_len, LANE), F32),
                   jax.ShapeDtypeStruct((N_HEADS, t_len, LANE), F32),
                   jax.ShapeDtypeStruct((D_MODEL, D_MODEL), F32),
                   jax.ShapeDtypeStruct((8, D_MODEL), F32),
                   jax.ShapeDtypeStruct((N_CB, t_len, LANE), F32)],
        scratch_shapes=[pltpu.VMEM((tm, D_MODEL), _BF), pltpu.VMEM((tm, D_MODEL), F32),
                        pltpu.VMEM((2, N_HEADS, 4, tm, LANE), F32),
                        pltpu.VMEM((2, 4, N_HEADS, tm, LANE), F32), pltpu.SemaphoreType.DMA((2, 4))],
        compiler_params=_params(("arbitrary",)),
    )(x, target, oa, ob, pb, pb, pb, pb, wout, gate, final_g, hg_g, ret_g)


def device_step(x, target, mod, lb, wt, wout, norm_g, hg_g, ret_g, final_g, start_exchange=None):
    t_len = x.shape[0]
    shift, scale, gate = mod[:, :D_MODEL], mod[:, D_MODEL:2 * D_MODEL], mod[:, 2 * D_MODEL:]
    scale1p = 1.0 + scale
    cos_t, sin_t = _rope_tables(t_len)
    h, h_t = adaln_forward(x, norm_g, scale1p, shift, t_len)
    pb = proj_forward(h, wt, t_len)
    oa, ssave = hgrn_forward(pb, lb, t_len)
    ob, rsave = retention_forward(pb, cos_t, sin_t, t_len)
    dy, doa, dob, dwout, vec_mid, dpb = middle(x, target, oa, ob, pb, wout, gate, final_g, hg_g, ret_g, t_len)
    dpb, dlb = hgrn_backward(pb, lb, doa, ssave, dpb, t_len)
    dpb = retention_backward(pb, cos_t, sin_t, dob, rsave, dpb, t_len)
    dwin = proj_backward_weight(h_t, dpb, t_len)
    token, pending = start_exchange(dwin, dwout) if start_exchange else (jnp.zeros((8, LANE), F32), None)
    grad_x, vec_ada = proj_backward_input(dpb, wt, token, x, dy, norm_g, scale1p, t_len)
    return grad_x, dwin, dwout, vec_mid, vec_ada, dlb, pending


PACK_ROWS = 16
ROW_NORM_G, ROW_LB, ROW_HG_G, ROW_RET_G, ROW_FINAL_G, ROW_SHIFT, ROW_SCALE, ROW_GATE, ROW_LOSS = range(9)


def _mesh_pos():
    return lax.axis_index("x"), lax.axis_index("y"), lax.axis_index("c")


def _lin(pos):
    return 4 * pos[0] + 2 * pos[1] + pos[2]


def _xor_peer(pos, k):
    return tuple(1 - p if (k >> s) & 1 else p for p, s in zip(pos, (2, 1, 0)))


def _other_chips(pos):
    x, y, _ = pos
    return [(1 - x, y), (x, 1 - y), (1 - x, 1 - y)]


def _remote(src, dst, send_sem, recv_sem, to):
    return pltpu.make_async_remote_copy(src_ref=src, dst_ref=dst, send_sem=send_sem, recv_sem=recv_sem,
                                        device_id=to, device_id_type=MESH)


def pre_exchange(c, w_ada, b_ada, logits):
    def body(c_ref, wada_ref, bada_ref, logit_ref, mod_ref, scall_ref, lb_ref,
             cg_ref, modall_ref, parts_ref, send1, recv1, send2, recv2):
        pos = _mesh_pos()
        cv = c_ref[...]
        slot = lambda p: pl.ds(pl.multiple_of(8 * _lin(p), 8), 8)
        cg_ref[slot(pos), :] = jnp.broadcast_to(cv * _sigmoid(cv), (8, D_MODEL))
        lb_ref[...] = _sigmoid(logit_ref[0:1, :] - logit_ref[1:2, :])
        peers = [_xor_peer(pos, k) for k in range(1, N_DEV)]
        gather = [_remote(cg_ref.at[slot(pos)], cg_ref.at[slot(pos)], send1.at[n], recv1.at[n], p)
                  for n, p in enumerate(peers)]
        for cp in gather:
            cp.start()
        for n, p in enumerate(peers):
            _remote(cg_ref.at[slot(p)], cg_ref.at[slot(p)], send1.at[n], recv1.at[n], p).wait_recv()
        modall_ref[...] = _dot(cg_ref[...], wada_ref[...])
        scatter = [_remote(modall_ref.at[slot(p)], parts_ref.at[slot(pos)], send2.at[n], recv2.at[n], p)
                   for n, p in enumerate(peers)]
        for cp in scatter:
            cp.start()
        parts_ref[slot(pos), :] = modall_ref[slot(pos), :]
        for n, p in enumerate(peers):
            _remote(modall_ref.at[slot(p)], parts_ref.at[slot(p)], send2.at[n], recv2.at[n], p).wait_recv()
        for cp in gather + scatter:
            cp.wait_send()
        for j in range(N_DEV):
            cols = slice(j * SHARD_ADA, (j + 1) * SHARD_ADA)
            mod_ref[:, cols] = parts_ref[8 * j:8 * j + 1, :] + bada_ref[:, cols]
            scall_ref[j:j + 1, :] = cg_ref[8 * j:8 * j + 1, :]

    vmem = pl.BlockSpec(memory_space=pltpu.VMEM)
    return pl.pallas_call(
        body, name="pre_exchange",
        in_specs=[vmem] * 4, out_specs=[vmem] * 3,
        out_shape=[jax.ShapeDtypeStruct((1, 3 * D_MODEL), F32), jax.ShapeDtypeStruct((N_DEV, D_MODEL), F32),
                   jax.ShapeDtypeStruct((1, D_MODEL), F32)],
        scratch_shapes=[pltpu.VMEM((N_DEV * 8, D_MODEL), F32), pltpu.VMEM((N_DEV * 8, SHARD_ADA), F32),
                        pltpu.VMEM((N_DEV * 8, SHARD_ADA), F32)] + [pltpu.SemaphoreType.DMA((N_DEV - 1,))] * 4,
        compiler_params=pltpu.CompilerParams(vmem_limit_bytes=VMEM_LIMIT),
    )(c, w_ada, b_ada, logits)


def weight_gather(win_sh, wout_sh):
    def body(win_ref, wout_ref, wg_ref, woutg_ref, send, recv, local):
        pos = _mesh_pos()
        x, y, c = pos
        sibling = (x, y, 1 - c)

        def route(core):
            return [(x + (1 - core) * (1 - 2 * x), y + core * (1 - 2 * y)),
                    (x + core * (1 - 2 * x), y + (1 - core) * (1 - 2 * y)),
                    (1 - x, 1 - y)]

        chips, sib_chips = route(c), route(1 - c)
        mine, first, later = [], [], []
        for a, (src, out) in enumerate(((win_ref, wg_ref), (wout_ref, woutg_ref))):
            def copy(k, block, to, src_ref=None, a=a, out=out):
                dst = out.at[_lin(block)]
                return _remote(dst if src_ref is None else src_ref, dst, send.at[7 * a + k], recv.at[7 * a + k], to)
            mine.append(pltpu.make_async_copy(src, out.at[_lin(pos)], local.at[a]))
            first += [copy(0, pos, sibling, src), copy(1, pos, (*chips[0], c), src), copy(2, pos, (*chips[1], c), src)]
            later.append([[copy(3, (*chips[0], c), (*chips[1], c)), copy(4, (*chips[0], c), sibling)],
                          [copy(5, (*chips[1], c), sibling)],
                          [copy(6, (*chips[2], c), sibling)]])
        for cp in mine + first:
            cp.start()
        for j in range(3):
            for a, out in enumerate((wg_ref, woutg_ref)):
                dst = out.at[_lin((*chips[j], c))]
                _remote(dst, dst, send.at[7 * a + 1 + j], recv.at[7 * a + 1 + j], pos).wait_recv()
                for cp in later[a][j]:
                    cp.start()
        for a, out in enumerate((wg_ref, woutg_ref)):
            dst = out.at[_lin(sibling)]
            _remote(dst, dst, send.at[7 * a], recv.at[7 * a], pos).wait_recv()
            for j in range(3):
                dst = out.at[_lin((*sib_chips[j], 1 - c))]
                _remote(dst, dst, send.at[7 * a + 4 + j], recv.at[7 * a + 4 + j], pos).wait_recv()
        for cp in first + [cp for per_array in later for group in per_array for cp in group]:
            cp.wait_send()
        for cp in mine:
            cp.wait()

    any_spec = pl.BlockSpec(memory_space=pl.ANY)
    return pl.pallas_call(
        body, name="weight_gather",
        in_specs=[any_spec, any_spec], out_specs=[any_spec, any_spec],
        out_shape=[jax.ShapeDtypeStruct((N_DEV,) + win_sh.shape, win_sh.dtype),
                   jax.ShapeDtypeStruct((N_DEV,) + wout_sh.shape, wout_sh.dtype)],
        scratch_shapes=[pltpu.SemaphoreType.DMA((14,)), pltpu.SemaphoreType.DMA((14,)),
                        pltpu.SemaphoreType.DMA((2,))],
    )(win_sh, wout_sh)


def grad_pair_exchange(g_in, g_out):
    def body(gin_ref, gout_ref, ra_ref, rb_ref, send, recv):
        pos = _mesh_pos()
        x, y, c = pos
        sibling = (x, y, 1 - c)
        copies = []
        for a, (src, dst) in enumerate(((gin_ref, ra_ref), (gout_ref, rb_ref))):
            for q in range(4):
                copies.append(_remote(src.at[2 * q + (1 - c)], dst.at[q], send.at[4 * a + q], recv.at[4 * a + q],
                                      sibling))
        for cp in copies:
            cp.start()
        for cp in copies:
            cp.wait_recv()
        for cp in copies:
            cp.wait_send()

    any_spec = pl.BlockSpec(memory_space=pl.ANY)
    return pl.pallas_call(
        body, name="grad_pair_exchange",
        in_specs=[any_spec, any_spec], out_specs=[any_spec, any_spec],
        out_shape=[jax.ShapeDtypeStruct((4,) + g_in.shape[1:], F32), jax.ShapeDtypeStruct((4,) + g_out.shape[1:], F32)],
        scratch_shapes=[pltpu.SemaphoreType.DMA((8,)), pltpu.SemaphoreType.DMA((8,))],
    )(g_in, g_out)


def pair_sum(g_in, ra, g_out, rb, c_idx):
    tr = 256

    def body(c_ref, gin_ref, ra_ref, gout_ref, rb_ref, sb_ref, sbo_ref):
        del c_ref
        sb_ref[...] = (gin_ref[...] + ra_ref[...]).astype(sb_ref.dtype)
        sbo_ref[...] = gout_ref[...] + rb_ref[...]

    n_i = D_MODEL // tr
    return pl.pallas_call(
        body, name="pair_sum",
        grid_spec=pltpu.PrefetchScalarGridSpec(
            num_scalar_prefetch=1, grid=(4, n_i),
            in_specs=[pl.BlockSpec((None, tr, SHARD_IN), lambda q, i, c: (2 * q + c[0], i, 0)),
                      pl.BlockSpec((None, tr, SHARD_IN), lambda q, i, c: (q, i, 0)),
                      pl.BlockSpec((None, SHARD_OUT // n_i, D_MODEL), lambda q, i, c: (2 * q + c[0], i, 0)),
                      pl.BlockSpec((None, SHARD_OUT // n_i, D_MODEL), lambda q, i, c: (q, i, 0))],
            out_specs=[pl.BlockSpec((None, tr, SHARD_IN), lambda q, i, c: (q, i, 0)),
                       pl.BlockSpec((None, SHARD_OUT // n_i, D_MODEL), lambda q, i, c: (q, i, 0))]),
        out_shape=[jax.ShapeDtypeStruct(ra.shape, _BF), jax.ShapeDtypeStruct(rb.shape, F32)],
        compiler_params=_params(("arbitrary", "arbitrary")),
    )(c_idx, g_in, ra, g_out, rb)


_HBM = pl.BlockSpec(memory_space=pltpu.HBM)
_SEM = pl.BlockSpec(memory_space=pltpu.SEMAPHORE)
_N_CHIP_COPIES = 6


def _chip_copies(sb_ref, sbo_ref, rc_ref, rco_ref, send, recv):
    pos = _mesh_pos()
    copies = []
    for a, (src, dst) in enumerate(((sb_ref, rc_ref), (sbo_ref, rco_ref))):
        for j, chip in enumerate(_other_chips(pos)):
            copies.append(_remote(src.at[2 * chip[0] + chip[1]], dst.at[j], send.at[3 * a + j], recv.at[3 * a + j],
                                  (*chip, pos[2])))
    return copies


def grad_chip_start(sb, sbo):
    def body(sb_ref, sbo_ref, rc_ref, rco_ref, send, recv, sb_thru, sbo_thru, rc_thru, rco_thru, token):
        del sb_thru, sbo_thru, rc_thru, rco_thru
        for cp in _chip_copies(sb_ref, sbo_ref, rc_ref, rco_ref, send, recv):
            cp.start()
        token[...] = jnp.zeros_like(token)

    hbm = lambda a: pltpu.with_memory_space_constraint(a, pltpu.HBM)
    rc = lax.empty((3,) + sb.shape[1:], sb.dtype)
    rco = lax.empty((3,) + sbo.shape[1:], sbo.dtype)
    return pl.pallas_call(
        body, name="grad_chip_start",
        in_specs=[_HBM] * 4,
        out_specs=[_SEM, _SEM, _HBM, _HBM, _HBM, _HBM, pl.BlockSpec(memory_space=pltpu.VMEM)],
        out_shape=[pltpu.SemaphoreType.DMA((_N_CHIP_COPIES,)), pltpu.SemaphoreType.DMA((_N_CHIP_COPIES,)),
                   pltpu.HBM(sb.shape, sb.dtype), pltpu.HBM(sbo.shape, sbo.dtype),
                   pltpu.HBM(rc.shape, rc.dtype), pltpu.HBM(rco.shape, rco.dtype),
                   jax.ShapeDtypeStruct((8, LANE), F32)],
        input_output_aliases={0: 2, 1: 3, 2: 4, 3: 5},
        compiler_params=pltpu.CompilerParams(has_side_effects=pltpu.SideEffectType.DATAFLOW_SIDE_EFFECTING),
    )(hbm(sb), hbm(sbo), hbm(rc), hbm(rco))


def grad_chip_wait(send, recv, sb, sbo, rc, rco, after):
    def body(sb_ref, sbo_ref, rc_ref, rco_ref, send, recv, after_ref, sb_o, sbo_o, rc_o, rco_o):
        del after_ref, sb_o, sbo_o, rc_o, rco_o
        for cp in _chip_copies(sb_ref, sbo_ref, rc_ref, rco_ref, send, recv):
            cp.wait_send()
            cp.wait_recv()

    return pl.pallas_call(
        body, name="grad_chip_wait",
        in_specs=[_HBM] * 4 + [_SEM, _SEM, pl.BlockSpec(memory_space=pl.ANY)],
        out_specs=[_HBM] * 4,
        out_shape=[pltpu.HBM(sb.shape, sb.dtype), pltpu.HBM(sbo.shape, sbo.dtype),
                   pltpu.HBM(rc.shape, rc.dtype), pltpu.HBM(rco.shape, rco.dtype)],
        input_output_aliases={0: 0, 1: 1, 2: 2, 3: 3},
        compiler_params=pltpu.CompilerParams(has_side_effects=pltpu.SideEffectType.DATAFLOW_SIDE_EFFECTING),
    )(sb, sbo, rc, rco, send, recv, after)


def pack_gather(pack):
    def body(pack_ref, packs_ref, psend, precv):
        pos = _mesh_pos()
        me = _lin(pos)
        packs_ref[me] = pack_ref[...]
        peers = [_xor_peer(pos, k) for k in range(1, N_DEV)]
        gather = [_remote(packs_ref.at[me], packs_ref.at[me], psend.at[n], precv.at[n], p) for n, p in enumerate(peers)]
        for cp in gather:
            cp.start()
        for n, p in enumerate(peers):
            _remote(packs_ref.at[_lin(p)], packs_ref.at[_lin(p)], psend.at[n], precv.at[n], p).wait_recv()
        for cp in gather:
            cp.wait_send()

    vmem = pl.BlockSpec(memory_space=pltpu.VMEM)
    return pl.pallas_call(
        body, name="pack_gather", in_specs=[vmem], out_specs=vmem,
        out_shape=jax.ShapeDtypeStruct((N_DEV,) + pack.shape, F32),
        scratch_shapes=[pltpu.SemaphoreType.DMA((N_DEV - 1,)), pltpu.SemaphoreType.DMA((N_DEV - 1,))],
    )(pack)


def pack_rows(vec_mid, vec_ada, dlb):
    def body(mid_ref, ada_ref, dlb_ref, o_ref):
        mid = lambda r: mid_ref[r:r + 1, :]
        rows = [ada_ref[0:1, :], dlb_ref[...], mid(MID_HG_G), mid(MID_RET_G), mid(MID_FINAL_G),
                ada_ref[2:3, :], ada_ref[1:2, :], mid(MID_GATE), mid(MID_LOSS)]
        o_ref[...] = jnp.zeros_like(o_ref)
        for n, row in enumerate(rows):
            o_ref[n:n + 1, :] = row

    vmem = pl.BlockSpec(memory_space=pltpu.VMEM)
    return pl.pallas_call(body, name="pack_rows", in_specs=[vmem] * 3, out_specs=vmem,
                          out_shape=jax.ShapeDtypeStruct((PACK_ROWS, D_MODEL), F32))(vec_mid, vec_ada, dlb)


def _adamw(w, g, m, v):
    m = ADAM_B1 * m + (1.0 - ADAM_B1) * g
    v = ADAM_B2 * v + (1.0 - ADAM_B2) * (g * g)
    m_hat = m / (1.0 - ADAM_B1 ** ADAM_STEP)
    v_hat = v / (1.0 - ADAM_B2 ** ADAM_STEP)
    delta = -ADAM_LR * (m_hat / (jnp.sqrt(v_hat) + ADAM_EPS) + ADAM_WD * w)
    return delta, m, v


def adam_shard(chip_idx, own, parts, w, m, v, name):
    rows, cols = w.shape
    tr = min(rows, 128)

    def body(chip_ref, p0, p1, p2, p3, w_ref, m_ref, v_ref, g_ref, d_ref, nm_ref, nv_ref):
        del chip_ref
        g = ((p0[...].astype(F32) + p1[...].astype(F32)) + p2[...].astype(F32)) + p3[...].astype(F32)
        g_ref[...] = g
        d_ref[...], nm_ref[...], nv_ref[...] = _adamw(w_ref[...], g, m_ref[...], v_ref[...])

    part = lambda q: pl.BlockSpec((None, tr, cols), lambda i, chip, q=q: (q, i, 0))
    tile = pl.BlockSpec((tr, cols), lambda i, chip: (i, 0))
    return pl.pallas_call(
        body, name=name,
        grid_spec=pltpu.PrefetchScalarGridSpec(
            num_scalar_prefetch=1, grid=(rows // tr,),
            in_specs=[pl.BlockSpec((None, tr, cols), lambda i, chip: (chip[0], i, 0)), part(0), part(1), part(2),
                      tile, tile, tile],
            out_specs=[tile] * 4),
        out_shape=[jax.ShapeDtypeStruct(w.shape, F32)] * 4,
        compiler_params=_params(("arbitrary",)),
    )(chip_idx, own, parts, parts, parts, w, m, v)


def adam_ada(sc_t, dmod_all, me_idx, w, m, v):
    def body(me_ref, sc_ref, dm_ref, w_ref, m_ref, v_ref, g_ref, d_ref, nm_ref, nv_ref):
        del me_ref
        g = _dot_f32(sc_ref[...], dm_ref[...])
        g_ref[...] = g
        d_ref[...], nm_ref[...], nv_ref[...] = _adamw(w_ref[...], g, m_ref[...], v_ref[...])

    full = pl.BlockSpec(w.shape, lambda i, me: (0, 0))
    return pl.pallas_call(
        body, name="adam_ada",
        grid_spec=pltpu.PrefetchScalarGridSpec(
            num_scalar_prefetch=1, grid=(1,),
            in_specs=[pl.BlockSpec(sc_t.shape, lambda i, me: (0, 0)),
                      pl.BlockSpec((LANE, SHARD_ADA), lambda i, me: (0, me[0])), full, full, full],
            out_specs=[full] * 4),
        out_shape=[jax.ShapeDtypeStruct(w.shape, F32)] * 4,
        compiler_params=_params(("arbitrary",)),
    )(me_idx, sc_t, dmod_all, w, m, v)


def adam_vectors(packs, lb, params, ms, vs):
    n = len(params)

    def body(*refs):
        packs_ref, lb_ref = refs[0], refs[1]
        w_refs, m_refs, v_refs = refs[2:2 + n], refs[2 + n:2 + 2 * n], refs[2 + 2 * n:2 + 3 * n]
        loss_ref = refs[2 + 3 * n]
        outs = refs[3 + 3 * n:3 + 7 * n]
        tot_ref = refs[3 + 7 * n]
        tot = packs_ref[0]
        for d in range(1, N_DEV):
            tot = tot + packs_ref[d]
        tot_ref[...] = tot
        row = lambda r: tot_ref[r:r + 1, :]
        lbv = lb_ref[...]
        dl0 = row(ROW_LB) * lbv * (1.0 - lbv)
        grads = [[row(ROW_NORM_G)],
                 [jnp.concatenate([row(ROW_SHIFT), row(ROW_SCALE), row(ROW_GATE)], axis=1)],
                 [dl0, -dl0],
                 [row(ROW_HG_G)], [row(ROW_RET_G)], [row(ROW_FINAL_G)]]
        loss_ref[...] = tot_ref[ROW_LOSS:ROW_LOSS + 1, 0:LANE]
        for j, g_rows in enumerate(grads):
            for r, g in enumerate(g_rows):
                rs = slice(r, r + 1)
                d, nm, nv = _adamw(w_refs[j][rs, :], g, m_refs[j][rs, :], v_refs[j][rs, :])
                outs[4 * j][rs, :] = g
                outs[4 * j + 1][rs, :] = d
                outs[4 * j + 2][rs, :] = nm
                outs[4 * j + 3][rs, :] = nv

    vmem = pl.BlockSpec(memory_space=pltpu.VMEM)
    out_shape = [jax.ShapeDtypeStruct((1, LANE), F32)]
    for w in params:
        out_shape += [jax.ShapeDtypeStruct(w.shape, F32)] * 4
    return pl.pallas_call(
        body, name="adam_vectors", in_specs=[vmem] * (2 + 3 * n), out_specs=[vmem] * len(out_shape),
        out_shape=out_shape, scratch_shapes=[pltpu.VMEM((PACK_ROWS, D_MODEL), F32)],
    )(packs, lb, *params, *ms, *vs)


def kernel(x, c, norm_g, w_ada, b_ada, w_in, hg_lb_logits, hg_norm_g, ret_norm_g, w_out, final_g, loss_target, m_norm_g, m_w_ada, m_b_ada, m_w_in, m_hg_lb_logits, m_hg_norm_g, m_ret_norm_g, m_w_out, m_final_g, v_norm_g, v_w_ada, v_b_ada, v_w_in, v_hg_lb_logits, v_hg_norm_g, v_ret_norm_g, v_w_out, v_final_g):
    pos = _mesh_pos()
    me_idx = jnp.reshape(_lin(pos), (1,)).astype(jnp.int32)
    c_idx = jnp.reshape(pos[2], (1,)).astype(jnp.int32)
    vec = lambda a: a.reshape(1, D_MODEL)

    mod, scall, lb = pre_exchange(c, w_ada[0], b_ada, hg_lb_logits)
    wtg, woutg = weight_gather(w_in[0].T.astype(_BF), w_out[0].astype(_BF))
    chip_idx = jnp.reshape(2 * pos[0] + pos[1], (1,)).astype(jnp.int32)

    def start_exchange(dwin, dwout):
        dwout = dwout.reshape(N_DEV, SHARD_OUT, D_MODEL)
        ra, rb = grad_pair_exchange(dwin, dwout)
        sb, sbo = pair_sum(dwin, ra, dwout, rb, c_idx)
        send, recv, sb, sbo, rc, rco, token = grad_chip_start(sb, sbo)
        return token, (send, recv, sb, sbo, rc, rco)

    grad_x, _, _, vec_mid, vec_ada, dlb, pending = device_step(
        x[0], loss_target[0], mod, lb, wtg.reshape(D_IN, D_MODEL), woutg.reshape(D_MODEL, D_MODEL), norm_g,
        hg_norm_g, ret_norm_g, vec(final_g), start_exchange)
    packs = pack_gather(pack_rows(vec_mid, vec_ada, dlb))
    dmod_all = packs[:, ROW_SHIFT:ROW_GATE + 1, :].reshape(N_DEV, 3 * D_MODEL)
    dmod_all = jnp.pad(dmod_all, ((0, LANE - N_DEV), (0, 0)))
    sc_t = jnp.pad(scall.T, ((0, 0), (0, LANE - N_DEV)))
    g_ada, d_ada, nm_ada, nv_ada = adam_ada(sc_t, dmod_all, me_idx, w_ada[0], m_w_ada[0], v_w_ada[0])
    small = adam_vectors(
        packs, lb,
        (norm_g, b_ada, hg_lb_logits, hg_norm_g, ret_norm_g, vec(final_g)),
        (m_norm_g, m_b_ada, m_hg_lb_logits, m_hg_norm_g, m_ret_norm_g, vec(m_final_g)),
        (v_norm_g, v_b_ada, v_hg_lb_logits, v_hg_norm_g, v_ret_norm_g, vec(v_final_g)))
    loss = small[0][0, 0]
    sb, sbo, rc, rco = grad_chip_wait(*pending, small[0])
    g_in, d_in, nm_in, nv_in = adam_shard(chip_idx, sb, rc, w_in[0], m_w_in[0], v_w_in[0], "adam_w_in")
    g_out, d_out, nm_out, nv_out = adam_shard(chip_idx, sbo, rco, w_out[0], m_w_out[0], v_w_out[0], "adam_w_out")
    (g_ng, d_ng, nm_ng, nv_ng), (g_b, d_b, nm_b, nv_b), (g_lb, d_lb, nm_lb, nv_lb), (g_hg, d_hg, nm_hg, nv_hg), \
        (g_rg, d_rg, nm_rg, nv_rg), (g_fg, d_fg, nm_fg, nv_fg) = [small[1 + 4 * j:5 + 4 * j] for j in range(6)]
    flat = lambda a: a.reshape(D_MODEL)

    def group(ng, ada, b, win, lbl, hg, rg, wo, fg):
        return (ng, ada[None], b, win[None], lbl, hg, rg, wo[None], flat(fg))

    return (loss, grad_x[None],
            *group(g_ng, g_ada, g_b, g_in, g_lb, g_hg, g_rg, g_out, g_fg),
            *group(d_ng, d_ada, d_b, d_in, d_lb, d_hg, d_rg, d_out, d_fg),
            *group(nm_ng, nm_ada, nm_b, nm_in, nm_lb, nm_hg, nm_rg, nm_out, nm_fg),
            *group(nv_ng, nv_ada, nv_b, nv_in, nv_lb, nv_hg, nv_rg, nv_out, nv_fg))
```

```python
import functools

import numpy as np
import jax
import jax.numpy as jnp
from jax import lax
from jax.experimental import pallas as pl
from jax.experimental.pallas import tpu as pltpu

F32 = jnp.float32
_BF = jnp.bfloat16

D_MODEL = 1024
N_HEADS = 8
LANE = 128
RET_DK = 64
D_IN = 9216
N_DEV = 8
SHARD_IN = D_IN // N_DEV
SHARD_ADA = 3 * D_MODEL // N_DEV
SHARD_OUT = D_MODEL // N_DEV
N_CB = D_IN // LANE
CB_PER_SHARD = SHARD_IN // LANE
CHUNK = 128
N_LEVELS = 7
EPS = 1e-6
ROPE_BASE = 10000.0
CB_HQ, CB_HF, CB_HI, CB_HZ, CB_RQ, CB_RK, CB_RV, CB_RZ, CB_GA, CB_GB = 0, 8, 16, 24, 32, 36, 40, 48, 56, 64
VMEM_LIMIT = 56 * 1024 * 1024

ADAM_LR, ADAM_B1, ADAM_B2, ADAM_EPS, ADAM_WD, ADAM_STEP = 0.001, 0.9, 0.999, 1e-08, 0.01, 10

_NN = (((1,), (0,)), ((), ()))
_NT = (((1,), (1,)), ((), ()))
_TN = (((0,), (0,)), ((), ()))
MESH = pl.DeviceIdType.MESH


def _dot(a, b, dims=_NN):
    return lax.dot_general(a.astype(_BF), b.astype(_BF), dims, preferred_element_type=F32)


def _split2(a):
    hi = a.astype(_BF)
    lo = (a - hi.astype(F32)).astype(_BF)
    return jnp.concatenate([hi, lo], axis=1)


def _dot_sel(sel, a):
    n = a.shape[1]
    r = lax.dot_general(sel.astype(_BF), _split2(a), _NN, preferred_element_type=F32)
    return r[:, :n] + r[:, n:]


def _dot_f32(a, b):
    def pieces(v):
        p1 = v.astype(_BF)
        r1 = v - p1.astype(F32)
        p2 = r1.astype(_BF)
        p3 = (r1 - p2.astype(F32)).astype(_BF)
        return p1, p2, p3
    a1, a2, a3 = pieces(a)
    b1, b2, b3 = pieces(b)
    d = lambda u, v: lax.dot_general(u, v, _NN, preferred_element_type=F32)
    return ((d(a1, b3) + d(a2, b2) + d(a3, b1)) + (d(a1, b2) + d(a2, b1))) + d(a1, b1)


def _sigmoid(v):
    return 1.0 / (1.0 + jnp.exp(-v))


def _params(sem=None):
    return pltpu.CompilerParams(dimension_semantics=sem, vmem_limit_bytes=VMEM_LIMIT)


def _hgrn_consts():
    c, nl = CHUNK, N_LEVELS
    t = np.arange(c)[:, None]
    j = np.arange(c)[None, :]
    sel = [j <= t]
    masks = [j == t]
    for l in range(1, nl + 1):
        m = ((t >> l) << l) + (1 << (l - 1)) - 1
        sec = t > m
        sel.append(np.where(sec, (j > m) & (j <= t), (j > t) & (j <= m)))
        same = (t >> l) == (j >> l)
        masks.append(same & sec & (j <= m))
    sel.append(j > t)
    sel = np.concatenate(sel, 0).astype(np.float32)
    masks = np.stack(masks).astype(np.float32)
    sgn = np.stack([np.where((t & (1 << (l - 1))) != 0, 1.0, -1.0) * np.ones((1, LANE)) for l in range(3, nl + 1)])
    return dict(tri=jnp.asarray(sel[:c], _BF),
                lvl=jnp.asarray(masks, F32),
                sgn=jnp.asarray(sgn, F32),
                sel_t=jnp.asarray(sel.T, _BF),
                lvl_b=jnp.asarray(masks, _BF),
                lvlt_b=jnp.asarray(np.swapaxes(masks, 1, 2), _BF))


def _level_exponents(b, logf, b_scr, sgn_ref):
    c = CHUNK
    b_scr[...] = b
    row = lax.broadcasted_iota(jnp.int32, (c, LANE), 0)
    nxt = pltpu.roll(logf, c - 1, 0)
    prv = pltpu.roll(logf, 1, 0)
    r4 = row & 3
    out = [jnp.where((row & 1) == 1, logf, 0.0),
           jnp.where(r4 == 0, nxt, jnp.where(r4 == 1, 0.0, jnp.where(r4 == 2, logf, logf + prv)))]
    for l in range(3, N_LEVELS + 1):
        size, half = 1 << l, 1 << (l - 1)
        ref = jnp.concatenate([jnp.broadcast_to(b_scr[i * size + half - 1:i * size + half, :], (size, LANE))
                               for i in range(c // size)], axis=0)
        out.append((b - ref) * sgn_ref[l - 3])
    return out


def _hgrn_chunk(hq, hf, hi, lbv, tri_ref, sgn_ref, b_scr):
    sq = _sigmoid(hq)
    q = hq * sq
    sg = _sigmoid(hf)
    omlb = 1.0 - lbv
    f = lbv + omlb * sg
    k = 1.0 - f
    logf = jnp.log(f)
    b = _dot_sel(tri_ref[...], logf)
    bc = jnp.sum(logf, axis=0, keepdims=True)
    lev = [None] + [jnp.exp(e) for e in _level_exponents(b, logf, b_scr, sgn_ref)]
    return dict(sq=sq, q=q, sg=sg, omlb=omlb, f=f, k=k, v=hi, eb=jnp.exp(b), erem=jnp.exp(bc - b), ebc=jnp.exp(bc),
                lev=lev)


def _blockdiag(a, b):
    z = jnp.zeros_like(a)
    return jnp.concatenate([jnp.concatenate([a, z], axis=1), jnp.concatenate([z, b], axis=1)], axis=0)


def _level_operands(a):
    q, k = a["q"], a["k"]
    ql = [q.astype(_BF)] + [(q * a["lev"][l]).astype(_BF) for l in range(1, N_LEVELS + 1)]
    kl = [k.astype(_BF)] + [(k * a["lev"][l]).astype(_BF) for l in range(1, N_LEVELS + 1)]
    pairs = range(0, N_LEVELS + 1, 2)
    return ([jnp.concatenate([ql[l], ql[l + 1]], axis=1) for l in pairs], [_blockdiag(kl[l], kl[l + 1]) for l in pairs],
            ql, kl)


def _hgrn_scores(a, lvl_ref, q_pairs, k_diags):
    acc = None
    for n, (qp, kd) in enumerate(zip(q_pairs, k_diags)):
        both = lax.dot_general(qp, kd, _NT, preferred_element_type=F32)
        part = lvl_ref[2 * n] * both[:, :CHUNK] + lvl_ref[2 * n + 1] * both[:, CHUNK:]
        acc = part if acc is None else acc + part
    return acc


SCAN_UNROLL = 4
RET_UNROLL = 8


def _writeback_reserve(step, make_copies):
    slot = step % 2

    @pl.when(step >= 2)
    def _():
        for cp in make_copies(slot):
            cp.wait()

    return slot


def _writeback_commit(step, n_steps, slot, make_copies):
    for cp in make_copies(slot):
        cp.start()

    @pl.when(step == n_steps - 1)
    def _():
        for cp in make_copies(slot):
            cp.wait()
        if n_steps > 1:
            for cp in make_copies(1 - slot):
                cp.wait()


def _resident(const):
    zeros = (0,) * const.ndim
    return pl.BlockSpec(const.shape, lambda p, t: zeros)


def _time_block(t_len):
    return min(t_len, 1024)


def hgrn_forward(pb, lb, t_len):
    nc = t_len // CHUNK
    tb = _time_block(t_len)
    ncb = tb // CHUNK
    consts = _hgrn_consts()
    operands = [consts[n] for n in ("tri", "lvl", "sgn")]

    def body(hq_ref, hf_ref, hi_ref, lb_ref, tri_ref, lvl_ref, sgn_ref, o_ref, ssave_ref, st_ref, b_scr):
        @pl.when(pl.program_id(1) == 0)
        def _():
            st_ref[...] = jnp.zeros_like(st_ref)

        def chunk(ci, carry):
            r = pl.ds(pl.multiple_of(ci * CHUNK, CHUNK), CHUNK)
            for hd in range(2):
                lbv = lb_ref[:, hd * LANE:(hd + 1) * LANE]
                a = _hgrn_chunk(hq_ref[hd, r, :], hf_ref[hd, r, :], hi_ref[hd, r, :], lbv, tri_ref, sgn_ref,
                                b_scr.at[hd])
                q_pairs, k_diags, _, _ = _level_operands(a)
                st = st_ref[hd]
                ssave_ref[hd, ci] = st
                scores = _hgrn_scores(a, lvl_ref, q_pairs, k_diags)
                o_ref[hd, r, :] = _dot(a["q"] * a["eb"], st, _NT) + _dot(scores, a["v"])
                st_ref[hd] = st * a["ebc"] + _dot(a["v"], a["k"] * a["erem"], _TN)
            return carry

        lax.fori_loop(0, ncb, chunk, 0, unroll=SCAN_UNROLL)

    pair = lambda base: pl.BlockSpec((2, tb, LANE), lambda p, t, base=base: (base // 2 + p, t, 0))
    return pl.pallas_call(
        body, name="hgrn_fwd", grid=(N_HEADS // 2, t_len // tb),
        in_specs=[pair(CB_HQ), pair(CB_HF), pair(CB_HI),
                  pl.BlockSpec((1, 2 * LANE), lambda p, t: (0, p))] + [_resident(c) for c in operands],
        out_specs=[pl.BlockSpec((2, tb, LANE), lambda p, t: (p, t, 0)),
                   pl.BlockSpec((2, ncb, LANE, LANE), lambda p, t: (p, t, 0, 0))],
        out_shape=[jax.ShapeDtypeStruct((N_HEADS, t_len, LANE), F32),
                   jax.ShapeDtypeStruct((N_HEADS, nc, LANE, LANE), F32)],
        scratch_shapes=[pltpu.VMEM((2, LANE, LANE), F32), pltpu.VMEM((2, CHUNK, LANE), F32)],
        compiler_params=_params(("arbitrary", "arbitrary")),
    )(pb, pb, pb, lb, *operands)


def hgrn_backward(pb, lb, do, ssave, dpb, t_len):
    tb = _time_block(t_len)
    ncb, ntb = tb // CHUNK, t_len // tb
    consts = _hgrn_consts()
    operands = [consts[n] for n in ("tri", "lvl", "sgn", "sel_t", "lvl_b", "lvlt_b")]

    def body(hq_ref, hf_ref, hi_ref, lb_ref, do_ref, ssave_ref, tri_ref, lvl_ref, sgn_ref, selt_ref, lvlb_ref,
             lvltb_ref, dpb_in, dpb_ref, dlb_ref, dq_buf, df_buf, di_buf, dst_ref, b_scr, sems):
        del dpb_in
        p, t = pl.program_id(0), pl.program_id(1)
        step = p * ntb + t
        rows = pl.ds(pl.multiple_of((ntb - 1 - t) * tb, tb), tb)

        def out_copies(sl):
            return [pltpu.make_async_copy(buf.at[sl], dpb_ref.at[pl.ds(base + 2 * p, 2), rows], sems.at[sl, n])
                    for n, (buf, base) in enumerate(((dq_buf, CB_HQ), (df_buf, CB_HF), (di_buf, CB_HI)))]

        slot = _writeback_reserve(step, out_copies)

        @pl.when(t == 0)
        def _():
            dst_ref[...] = jnp.zeros_like(dst_ref)
            dlb_ref[...] = jnp.zeros_like(dlb_ref)

        def chunk(i, carry):
            ci = ncb - 1 - i
            r = pl.ds(pl.multiple_of(ci * CHUNK, CHUNK), CHUNK)
            for hd in range(2):
                head_chunk(hd, ci, r)
            return carry

        def head_chunk(hd, ci, r):
            lbv = lb_ref[:, hd * LANE:(hd + 1) * LANE]
            hq = hq_ref[hd, r, :]
            a = _hgrn_chunk(hq, hf_ref[hd, r, :], hi_ref[hd, r, :], lbv, tri_ref, sgn_ref, b_scr.at[hd])
            q_pairs, k_diags, ql, kl = _level_operands(a)
            q, k, v = a["q"], a["k"], a["v"]
            g = do_ref[hd, r, :]
            st0 = ssave_ref[hd, ci]
            dst = dst_ref[hd]
            scores = _hgrn_scores(a, lvl_ref, q_pairs, k_diags)
            da = _dot(g, v, _NT)
            da_t = _dot(v, g, _NT)
            kb = k * a["erem"]
            qb = q * a["eb"]
            dv = _dot(scores, g, _TN) + _dot(kb, dst, _NT)
            dq_inter = _dot(g, st0) * a["eb"]
            dk_state = _dot(v, dst) * a["erem"]
            dq, dk = dq_inter, dk_state
            de = [q * dq_inter]
            da_b, dat_b = da.astype(_BF), da_t.astype(_BF)
            for n in range(len(k_diags)):
                l0, l1 = 2 * n, 2 * n + 1
                da_pair = jnp.concatenate([lvlb_ref[l0] * da_b, lvlb_ref[l1] * da_b], axis=1)
                dat_pair = jnp.concatenate([lvltb_ref[l0] * dat_b, lvltb_ref[l1] * dat_b], axis=1)
                dq_both = lax.dot_general(da_pair, k_diags[n], _NN, preferred_element_type=F32)
                dk_both = lax.dot_general(dat_pair, _blockdiag(ql[l0], ql[l1]), _NN, preferred_element_type=F32)
                for l, cols in ((l0, slice(0, LANE)), (l1, slice(LANE, 2 * LANE))):
                    dql, dkl = dq_both[:, cols], dk_both[:, cols]
                    if l > 0:
                        e = a["lev"][l]
                        dql, dkl = dql * e, dkl * e
                        de.append(q * dql + k * dkl)
                    dq = dq + dql
                    dk = dk + dkl
            de.append(k * dk_state)
            dst_ref[hd] = dst * a["ebc"] + _dot(g, qb, _TN)
            dbc = jnp.sum(dst * st0, axis=0, keepdims=True) * a["ebc"]
            de2 = lax.dot_general(selt_ref[...], _split2(jnp.concatenate(de, axis=0)), _NN,
                                  preferred_element_type=F32)
            dlogf = de2[:, :LANE] + de2[:, LANE:] + dbc
            sq, sg = a["sq"], a["sg"]
            df = dlogf / a["f"] - dk
            dq_buf[slot, hd, r, :] = (dq * (sq * (1.0 + hq * (1.0 - sq)))).astype(dq_buf.dtype)
            df_buf[slot, hd, r, :] = (df * a["omlb"] * sg * (1.0 - sg)).astype(df_buf.dtype)
            di_buf[slot, hd, r, :] = dv.astype(di_buf.dtype)
            cols = slice(hd * LANE, (hd + 1) * LANE)
            dlb_ref[:, cols] = dlb_ref[:, cols] + jnp.sum(df * (1.0 - sg), axis=0, keepdims=True)

        lax.fori_loop(0, ncb, chunk, 0, unroll=SCAN_UNROLL)
        _writeback_commit(step, (N_HEADS // 2) * ntb, slot, out_copies)

    pair = lambda base: pl.BlockSpec((2, tb, LANE), lambda p, t, base=base: (base // 2 + p, ntb - 1 - t, 0))
    any_spec = pl.BlockSpec(memory_space=pl.ANY)
    return pl.pallas_call(
        body, name="hgrn_bwd", grid=(N_HEADS // 2, ntb),
        in_specs=[pair(CB_HQ), pair(CB_HF), pair(CB_HI),
                  pl.BlockSpec((1, 2 * LANE), lambda p, t: (0, p)),
                  pair(0),
                  pl.BlockSpec((2, ncb, LANE, LANE), lambda p, t: (p, ntb - 1 - t, 0, 0))]
        + [_resident(c) for c in operands] + [any_spec],
        out_specs=[any_spec, pl.BlockSpec((1, 2 * LANE), lambda p, t: (0, p))],
        out_shape=[jax.ShapeDtypeStruct(dpb.shape, dpb.dtype), jax.ShapeDtypeStruct((1, D_MODEL), F32)],
        scratch_shapes=[pltpu.VMEM((2, 2, tb, LANE), dpb.dtype)] * 3 + [
            pltpu.VMEM((2, LANE, LANE), F32), pltpu.VMEM((2, CHUNK, LANE), F32), pltpu.SemaphoreType.DMA((2, 3))],
        input_output_aliases={6 + len(operands): 0},
        compiler_params=_params(("arbitrary", "arbitrary")),
    )(pb, pb, pb, lb, do, ssave, *operands, dpb)


def _rope_tables(t_len):
    half = RET_DK // 2
    inv_freq = 1.0 / (ROPE_BASE ** jnp.linspace(0.0, 1.0, half, dtype=F32))
    ang = jnp.arange(t_len, dtype=jnp.int32).astype(F32)[:, None] * inv_freq[None, :]
    cos, sin = jnp.cos(ang), jnp.sin(ang)
    cos_t = jnp.concatenate([cos, cos, cos, cos], axis=1)
    sin_t = jnp.concatenate([-sin, sin, -sin, sin], axis=1)
    return cos_t, sin_t


def _swap_halves(v):
    half = RET_DK // 2
    lane = lax.broadcasted_iota(jnp.int32, v.shape, 1)
    first = (lane & (RET_DK - 1)) < half
    return jnp.where(first, pltpu.roll(v, LANE - half, 1), pltpu.roll(v, half, 1))


def _ret_head_consts(hidx):
    c = CHUNK
    hf = jnp.full((1, LANE), hidx, jnp.int32).astype(F32)
    lg = jnp.log(1.0 - jnp.exp(-(5.0 + hf) * np.float32(np.log(2.0))))
    row = lax.broadcasted_iota(jnp.int32, (c, c), 0)
    col = lax.broadcasted_iota(jnp.int32, (c, c), 1)
    rel = (row - col).astype(F32)
    dm = jnp.where(rel >= 0, jnp.exp(lg[:, :1] * jnp.maximum(rel, 0.0)), 0.0)
    dm_t = jnp.where(rel <= 0, jnp.exp(lg[:, :1] * jnp.maximum(-rel, 0.0)), 0.0)
    idx = lax.broadcasted_iota(jnp.int32, (c, LANE), 0).astype(F32)
    zeta = jnp.exp(lg * (c - 1.0 - idx))
    xi = jnp.exp(lg * (idx + 1.0))
    cdec = jnp.exp(lg * float(c))
    return dm, zeta, xi, cdec, dm_t


def _lane_mask(which):
    lane = lax.broadcasted_iota(jnp.int32, (1, LANE), 1)
    return ((lane // RET_DK) == which).astype(F32)


def retention_forward(pb, cos_t, sin_t, t_len):
    nc = t_len // CHUNK

    tb = _time_block(t_len)
    ncb = tb // CHUNK

    def body(rq_ref, rk_ref, rv_ref, cos_ref, sin_ref, o_ref, rsave_ref, st_ref):
        p = pl.program_id(0)

        @pl.when(pl.program_id(1) == 0)
        def _():
            st_ref[...] = jnp.zeros_like(st_ref)

        consts = [_ret_head_consts(2 * p + hd) for hd in range(2)]

        def chunk(ci, carry):
            r = pl.ds(pl.multiple_of(ci * CHUNK, CHUNK), CHUNK)
            cs, sn = cos_ref[r, :], sin_ref[r, :]
            q = rq_ref[r, :]
            k = rk_ref[r, :]
            q = q * cs + _swap_halves(q) * sn
            k = (k * cs + _swap_halves(k) * sn) * RET_DK ** -0.5
            for hd in range(2):
                dm, zeta, xi, cdec, _ = consts[hd]
                lm = _lane_mask(hd)
                qh, kh = q * lm, k * lm
                v = rv_ref[hd, r, :]
                st = st_ref[hd]
                rsave_ref[hd, ci] = st
                scores = _dot(qh, kh, _NT) * dm
                o_ref[hd, r, :] = _dot(scores, v) + _dot(qh * xi, st, _NT)
                st_ref[hd] = st * cdec + _dot(v, kh * zeta, _TN)
            return carry

        lax.fori_loop(0, ncb, chunk, 0, unroll=RET_UNROLL)

    return pl.pallas_call(
        body, name="ret_fwd", grid=(N_HEADS // 2, t_len // tb),
        in_specs=[pl.BlockSpec((None, tb, LANE), lambda p, t: (CB_RQ + p, t, 0)),
                  pl.BlockSpec((None, tb, LANE), lambda p, t: (CB_RK + p, t, 0)),
                  pl.BlockSpec((2, tb, LANE), lambda p, t: (CB_RV // 2 + p, t, 0)),
                  pl.BlockSpec((tb, LANE), lambda p, t: (t, 0)),
                  pl.BlockSpec((tb, LANE), lambda p, t: (t, 0))],
        out_specs=[pl.BlockSpec((2, tb, LANE), lambda p, t: (p, t, 0)),
                   pl.BlockSpec((2, ncb, LANE, LANE), lambda p, t: (p, t, 0, 0))],
        out_shape=[jax.ShapeDtypeStruct((N_HEADS, t_len, LANE), F32),
                   jax.ShapeDtypeStruct((N_HEADS, nc, LANE, LANE), F32)],
        scratch_shapes=[pltpu.VMEM((2, LANE, LANE), F32)],
        compiler_params=_params(("arbitrary", "arbitrary")),
    )(pb, pb, pb, cos_t, sin_t)


def retention_backward(pb, cos_t, sin_t, do, rsave, dpb, t_len):
    tb = _time_block(t_len)
    ncb, ntb = tb // CHUNK, t_len // tb

    def body(rq_ref, rk_ref, rv_ref, cos_ref, sin_ref, do_ref, rsave_ref, dpb_in,
             dpb_ref, dq_buf, dk_buf, dv_buf, dst_ref, sems):
        del dpb_in
        p, t = pl.program_id(0), pl.program_id(1)
        step = p * ntb + t
        rows = pl.ds(pl.multiple_of((ntb - 1 - t) * tb, tb), tb)

        def out_copies(sl):
            return [pltpu.make_async_copy(dq_buf.at[sl], dpb_ref.at[CB_RQ + p, rows], sems.at[sl, 0]),
                    pltpu.make_async_copy(dk_buf.at[sl], dpb_ref.at[CB_RK + p, rows], sems.at[sl, 1]),
                    pltpu.make_async_copy(dv_buf.at[sl], dpb_ref.at[pl.ds(CB_RV + 2 * p, 2), rows], sems.at[sl, 2])]

        slot = _writeback_reserve(step, out_copies)

        @pl.when(t == 0)
        def _():
            dst_ref[...] = jnp.zeros_like(dst_ref)

        consts = [_ret_head_consts(2 * p + hd) for hd in range(2)]

        def chunk(i, carry):
            ci = ncb - 1 - i
            r = pl.ds(pl.multiple_of(ci * CHUNK, CHUNK), CHUNK)
            cs, sn = cos_ref[r, :], sin_ref[r, :]
            q = rq_ref[r, :]
            k = rk_ref[r, :]
            q = q * cs + _swap_halves(q) * sn
            k = (k * cs + _swap_halves(k) * sn) * RET_DK ** -0.5
            dq, dk = None, None
            for hd in range(2):
                dm, zeta, xi, cdec, dm_t = consts[hd]
                lm = _lane_mask(hd)
                qh, kh = q * lm, k * lm
                v = rv_ref[hd, r, :]
                g = do_ref[hd, r, :]
                st0 = rsave_ref[hd, ci]
                dst = dst_ref[hd]
                scores_t = _dot(kh, qh, _NT) * dm_t
                dsc = _dot(g, v, _NT) * dm
                dsc_t = _dot(v, g, _NT) * dm_t
                dqh = _dot(dsc, kh) + _dot(g, st0) * xi
                dkh = _dot(dsc_t, qh) + _dot(v, dst) * zeta
                dv_buf[slot, hd, r, :] = (_dot(scores_t, g) + _dot(kh * zeta, dst, _NT)).astype(dv_buf.dtype)
                dst_ref[hd] = dst * cdec + _dot(g, qh * xi, _TN)
                dq = dqh if dq is None else dq + dqh
                dk = dkh if dk is None else dk + dkh
            dk = dk * (RET_DK ** -0.5)
            dq_buf[slot, r, :] = (dq * cs - _swap_halves(dq) * sn).astype(dq_buf.dtype)
            dk_buf[slot, r, :] = (dk * cs - _swap_halves(dk) * sn).astype(dk_buf.dtype)
            return carry

        lax.fori_loop(0, ncb, chunk, 0, unroll=RET_UNROLL)
        _writeback_commit(step, (N_HEADS // 2) * ntb, slot, out_copies)

    any_spec = pl.BlockSpec(memory_space=pl.ANY)
    return pl.pallas_call(
        body, name="ret_bwd", grid=(N_HEADS // 2, ntb),
        in_specs=[pl.BlockSpec((None, tb, LANE), lambda p, t: (CB_RQ + p, ntb - 1 - t, 0)),
                  pl.BlockSpec((None, tb, LANE), lambda p, t: (CB_RK + p, ntb - 1 - t, 0)),
                  pl.BlockSpec((2, tb, LANE), lambda p, t: (CB_RV // 2 + p, ntb - 1 - t, 0)),
                  pl.BlockSpec((tb, LANE), lambda p, t: (ntb - 1 - t, 0)),
                  pl.BlockSpec((tb, LANE), lambda p, t: (ntb - 1 - t, 0)),
                  pl.BlockSpec((2, tb, LANE), lambda p, t: (p, ntb - 1 - t, 0)),
                  pl.BlockSpec((2, ncb, LANE, LANE), lambda p, t: (p, ntb - 1 - t, 0, 0)),
                  any_spec],
        out_specs=any_spec,
        out_shape=jax.ShapeDtypeStruct(dpb.shape, dpb.dtype),
        scratch_shapes=[pltpu.VMEM((2, tb, LANE), dpb.dtype), pltpu.VMEM((2, tb, LANE), dpb.dtype),
                        pltpu.VMEM((2, 2, tb, LANE), dpb.dtype), pltpu.VMEM((2, LANE, LANE), F32),
                        pltpu.SemaphoreType.DMA((2, 3))],
        input_output_aliases={7: 0},
        compiler_params=_params(("arbitrary", "arbitrary")),
    )(pb, pb, pb, cos_t, sin_t, do, rsave, dpb)


def _row_tile(t_len, want):
    return min(want, t_len)


PAIR_CB = 2 * CB_PER_SHARD


def proj_forward(h, wt, t_len):
    tm = _row_tile(t_len, 1024)

    def body(h_ref, w_ref, o_ref):
        acc = _dot(h_ref[...], w_ref[...], _NT)
        for jj in range(PAIR_CB):
            o_ref[jj] = acc[:, jj * LANE:(jj + 1) * LANE]

    return pl.pallas_call(
        body, name="proj_fwd", grid=(N_DEV // 2, t_len // tm),
        in_specs=[pl.BlockSpec((tm, D_MODEL), lambda j, i: (i, 0)),
                  pl.BlockSpec((PAIR_CB * LANE, D_MODEL), lambda j, i: (j, 0))],
        out_specs=pl.BlockSpec((PAIR_CB, tm, LANE), lambda j, i: (j, i, 0)),
        out_shape=jax.ShapeDtypeStruct((N_CB, t_len, LANE), F32),
        compiler_params=_params(("arbitrary", "arbitrary")),
    )(h, wt)


def proj_backward_input(dpb, wt, token, x, dy, norm_g, scale1p, t_len):
    tm = _row_tile(t_len, 256)

    def body(a_ref, wt_hbm, token_ref, x_ref, dy_ref, g_ref, sc_ref, gx_ref, vec_ref, w_ref, sem):
        del token_ref
        i = pl.program_id(0)

        @pl.when(i == 0)
        def _():
            cp = pltpu.make_async_copy(wt_hbm, w_ref, sem)
            cp.start()
            cp.wait()

        a = jnp.concatenate([a_ref[jj].astype(_BF) for jj in range(N_CB)], axis=1)
        dhv = _dot(a, w_ref[...])
        xv, g, sc = x_ref[...], g_ref[...], sc_ref[...]
        r = lax.rsqrt(jnp.mean(xv * xv, axis=-1, keepdims=True) + EPS)
        xn = xv * r
        dxn = dhv * (g * sc)
        gx_ref[...] = dy_ref[...] + r * dxn - xn * (r * r) * jnp.mean(xv * dxn, axis=-1, keepdims=True)
        t = dhv * xn
        _acc_rows(vec_ref, i, [jnp.sum(t * sc, axis=0, keepdims=True),
                               jnp.sum(t * g, axis=0, keepdims=True),
                               jnp.sum(dhv, axis=0, keepdims=True)])

    row = pl.BlockSpec((tm, D_MODEL), lambda i: (i, 0))
    return pl.pallas_call(
        body, name="proj_bwd_input", grid=(t_len // tm,),
        in_specs=[pl.BlockSpec((N_CB, tm, LANE), lambda i: (0, i, 0)),
                  pl.BlockSpec(memory_space=pl.ANY),
                  pl.BlockSpec(token.shape, lambda i: (0, 0)),
                  row, row, _vec_spec(), _vec_spec()],
        out_specs=[row, pl.BlockSpec((8, D_MODEL), lambda i: (0, 0))],
        out_shape=[jax.ShapeDtypeStruct((t_len, D_MODEL), F32), jax.ShapeDtypeStruct((8, D_MODEL), F32)],
        scratch_shapes=[pltpu.VMEM(wt.shape, wt.dtype), pltpu.SemaphoreType.DMA],
        compiler_params=_params(("arbitrary",)),
    )(dpb, wt, token, x, dy, norm_g, scale1p)


def proj_backward_weight(h_t, dpb, t_len):
    tk = _row_tile(t_len, 2048)

    def body(h_ref, b_ref, o_ref):
        k = pl.program_id(1)
        b = jnp.concatenate([b_ref[jj].astype(_BF) for jj in range(PAIR_CB)], axis=1)
        part = _dot(h_ref[...], b)

        @pl.when(k == 0)
        def _():
            for s in range(2):
                o_ref[s] = part[:, s * SHARD_IN:(s + 1) * SHARD_IN]

        @pl.when(k > 0)
        def _():
            for s in range(2):
                o_ref[s] = o_ref[s] + part[:, s * SHARD_IN:(s + 1) * SHARD_IN]

    return pl.pallas_call(
        body, name="proj_bwd_weight", grid=(N_DEV // 2, t_len // tk),
        in_specs=[pl.BlockSpec((D_MODEL, tk), lambda j, k: (0, k)),
                  pl.BlockSpec((PAIR_CB, tk, LANE), lambda j, k: (j, k, 0))],
        out_specs=pl.BlockSpec((2, D_MODEL, SHARD_IN), lambda j, k: (j, 0, 0)),
        out_shape=jax.ShapeDtypeStruct((N_DEV, D_MODEL, SHARD_IN), F32),
        compiler_params=_params(("arbitrary", "arbitrary")),
    )(h_t, dpb)


def _vec_spec():
    return pl.BlockSpec((1, D_MODEL), lambda i: (0, 0))


def _acc_rows(ref, i, rows):
    @pl.when(i == 0)
    def _():
        ref[...] = jnp.zeros_like(ref)

    for n, row in enumerate(rows):
        ref[n:n + 1, :] = ref[n:n + 1, :] + row


def adaln_forward(x, norm_g, scale1p, shift, t_len):
    tm = _row_tile(t_len, 512)

    def body(x_ref, g_ref, sc_ref, sh_ref, h_ref, ht_ref):
        xv = x_ref[...]
        r = lax.rsqrt(jnp.mean(xv * xv, axis=-1, keepdims=True) + EPS)
        h = xv * r * g_ref[...] * sc_ref[...] + sh_ref[...]
        h_ref[...] = h.astype(h_ref.dtype)
        ht_ref[...] = h.T.astype(ht_ref.dtype)

    return pl.pallas_call(
        body, name="adaln_fwd", grid=(t_len // tm,),
        in_specs=[pl.BlockSpec((tm, D_MODEL), lambda i: (i, 0)), _vec_spec(), _vec_spec(), _vec_spec()],
        out_specs=[pl.BlockSpec((tm, D_MODEL), lambda i: (i, 0)), pl.BlockSpec((D_MODEL, tm), lambda i: (0, i))],
        out_shape=[jax.ShapeDtypeStruct((t_len, D_MODEL), _BF), jax.ShapeDtypeStruct((D_MODEL, t_len), _BF)],
        compiler_params=_params(("arbitrary",)),
    )(x, norm_g, scale1p, shift)


def _head_norm(o, g):
    r = lax.rsqrt(jnp.mean(o * o, axis=-1, keepdims=True) + EPS)
    return r, o * r * g


def _group_spec(tm, cb):
    return pl.BlockSpec((N_HEADS, tm, LANE), lambda i, cb=cb: (cb // N_HEADS, i, 0))


MID_FINAL_G, MID_GATE, MID_LOSS, MID_HG_G, MID_RET_G = range(5)


def middle(x, target, oa, ob, pb, wout, gate, final_g, hg_g, ret_g, t_len):
    tm = _row_tile(t_len, 256)
    n_steps = t_len // tm

    def body(x_ref, t_ref, oa_ref, ob_ref, hz_ref, rz_ref, ga_ref, gb_ref, w_ref, gate_ref, fg_ref, hg_ref, rg_ref,
             dy_ref, doa_ref, dob_ref, dw_ref, vec_ref, dpb_ref, m_scr, dm_scr, keep, bufs, sems):
        i = pl.program_id(0)
        rows = pl.ds(pl.multiple_of(i * tm, tm), tm)

        def group_copies(sl):
            return [pltpu.make_async_copy(bufs.at[sl, n], dpb_ref.at[pl.ds(cb, N_HEADS), rows], sems.at[sl, n])
                    for n, cb in enumerate((CB_HZ, CB_RZ, CB_GA, CB_GB))]
        sides = ((oa_ref, hz_ref, ga_ref, hg_ref, doa_ref), (ob_ref, rz_ref, gb_ref, rg_ref, dob_ref))
        for hh in range(N_HEADS):
            ls = slice(hh * LANE, (hh + 1) * LANE)
            acc = None
            for side, (o_ref, z_ref, gt_ref, g_ref, _) in enumerate(sides):
                o = o_ref[hh]
                rr = lax.rsqrt(jnp.mean(o * o, axis=-1, keepdims=True) + EPS)
                orr = o * rr
                zz = z_ref[hh]
                sz = _sigmoid(zz)
                sgt = _sigmoid(gt_ref[hh])
                keep[side, hh, 0] = orr
                keep[side, hh, 1] = sz
                keep[side, hh, 2] = sgt
                keep[side, hh, 3] = jnp.broadcast_to(rr, orr.shape)
                u = sgt * ((orr * g_ref[:, ls]) * (zz * sz))
                acc = u if acc is None else acc + u
            m_scr[:, ls] = acc.astype(m_scr.dtype)
        zv = _dot(m_scr[...], w_ref[...])
        gt, fg = gate_ref[...], fg_ref[...]
        y = x_ref[...] + gt * zv
        r = lax.rsqrt(jnp.mean(y * y, axis=-1, keepdims=True) + EPS)
        yn = y * r
        err = yn * fg - t_ref[...]
        loss = 0.5 * jnp.sum(jnp.mean(err * err, axis=-1, keepdims=True), axis=0, keepdims=True)
        dout = err * (1.0 / D_MODEL)
        gd = dout * fg
        dy = r * gd - yn * (r * r) * jnp.mean(y * gd, axis=-1, keepdims=True)
        dy_ref[...] = dy
        dz = (dy * gt).astype(_BF)
        dm_scr[...] = _dot(dz, w_ref[...], _NT)
        part = _dot(m_scr[...], dz, _TN)

        @pl.when(i == 0)
        def _():
            dw_ref[...] = part

        @pl.when(i > 0)
        def _():
            dw_ref[...] = dw_ref[...] + part

        slot = _writeback_reserve(i, group_copies)
        dg = [[], []]
        for hh in range(N_HEADS):
            ls = slice(hh * LANE, (hh + 1) * LANE)
            dmh = dm_scr[:, ls]
            for side, (o_ref, z_ref, gt_ref, g_ref, do_ref) in enumerate(sides):
                zz, g = z_ref[hh], g_ref[:, ls]
                orr, sz, sgt, rr = keep[side, hh, 0], keep[side, hh, 1], keep[side, hh, 2], keep[side, hh, 3]
                n = orr * g
                silu = zz * sz
                du = dmh * sgt
                bufs[slot, 2 + side, hh] = (dmh * (n * silu) * (sgt * (1.0 - sgt))).astype(bufs.dtype)
                bufs[slot, side, hh] = (du * n * (sz * (1.0 + zz * (1.0 - sz)))).astype(bufs.dtype)
                dn = du * silu
                dg[side].append(jnp.sum(dn * orr, axis=0, keepdims=True))
                gdn = dn * g
                do_ref[hh] = (rr * (gdn - orr * jnp.mean(orr * gdn, axis=-1, keepdims=True))).astype(do_ref.dtype)
        _acc_rows(vec_ref, i, [jnp.sum(dout * yn, axis=0, keepdims=True),
                               jnp.sum(dy * zv, axis=0, keepdims=True),
                               jnp.broadcast_to(loss, (1, D_MODEL)),
                               jnp.concatenate(dg[0], axis=1), jnp.concatenate(dg[1], axis=1)])
        _writeback_commit(i, n_steps, slot, group_copies)

    row = pl.BlockSpec((tm, D_MODEL), lambda i: (i, 0))
    head = pl.BlockSpec((N_HEADS, tm, LANE), lambda i: (0, i, 0))
    full = pl.BlockSpec((D_MODEL, D_MODEL), lambda i: (0, 0))
    return pl.pallas_call(
        body, name="middle", grid=(n_steps,),
        in_specs=[row, row, head, head, _group_spec(tm, CB_HZ), _group_spec(tm, CB_RZ), _group_spec(tm, CB_GA),
                  _group_spec(tm, CB_GB), full, _vec_spec(), _vec_spec(), _vec_spec(), _vec_spec()],
        out_specs=[row, head, head, full, pl.BlockSpec((8, D_MODEL), lambda i: (0, 0)),
                   pl.BlockSpec(memory_space=pl.ANY)],
        out_shape=[jax.ShapeDtypeStruct((t_len, D_MODEL), F32),
                   jax.ShapeDtypeStruct((N_HEADS, t_len, LANE), _BF),
                   jax.ShapeDtypeStruct((N_HEADS, t_len, LANE), _BF),
                   jax.ShapeDtypeStruct((D_MODEL, D_MODEL), F32),
                   jax.ShapeDtypeStruct((8, D_MODEL), F32),
                   jax.ShapeDtypeStruct((N_CB, t_len, LANE), _BF)],
        scratch_shapes=[pltpu.VMEM((tm, D_MODEL), _BF), pltpu.VMEM((tm, D_MODEL), F32),
                        pltpu.VMEM((2, N_HEADS, 4, tm, LANE), F32),
                        pltpu.VMEM((2, 4, N_HEADS, tm, LANE), _BF), pltpu.SemaphoreType.DMA((2, 4))],
        compiler_params=_params(("arbitrary",)),
    )(x, target, oa, ob, pb, pb, pb, pb, wout, gate, final_g, hg_g, ret_g)


def device_step(x, target, mod, lb, wt, wout, norm_g, hg_g, ret_g, final_g, start_exchange=None):
    t_len = x.shape[0]
    shift, scale, gate = mod[:, :D_MODEL], mod[:, D_MODEL:2 * D_MODEL], mod[:, 2 * D_MODEL:]
    scale1p = 1.0 + scale
    cos_t, sin_t = _rope_tables(t_len)
    h, h_t = adaln_forward(x, norm_g, scale1p, shift, t_len)
    pb = proj_forward(h, wt, t_len)
    oa, ssave = hgrn_forward(pb, lb, t_len)
    ob, rsave = retention_forward(pb, cos_t, sin_t, t_len)
    dy, doa, dob, dwout, vec_mid, dpb = middle(x, target, oa, ob, pb, wout, gate, final_g, hg_g, ret_g, t_len)
    dpb, dlb = hgrn_backward(pb, lb, doa, ssave, dpb, t_len)
    dpb = retention_backward(pb, cos_t, sin_t, dob, rsave, dpb, t_len)
    dwin = proj_backward_weight(h_t, dpb, t_len)
    token, pending = start_exchange(dwin, dwout) if start_exchange else (jnp.zeros((8, LANE), F32), None)
    grad_x, vec_ada = proj_backward_input(dpb, wt, token, x, dy, norm_g, scale1p, t_len)
    return grad_x, dwin, dwout, vec_mid, vec_ada, dlb, pending


PACK_ROWS = 16
ROW_NORM_G, ROW_LB, ROW_HG_G, ROW_RET_G, ROW_FINAL_G, ROW_SHIFT, ROW_SCALE, ROW_GATE, ROW_LOSS = range(9)


def _mesh_pos():
    return lax.axis_index("x"), lax.axis_index("y"), lax.axis_index("c")


def _lin(pos):
    return 4 * pos[0] + 2 * pos[1] + pos[2]


def _xor_peer(pos, k):
    return tuple(1 - p if (k >> s) & 1 else p for p, s in zip(pos, (2, 1, 0)))


def _other_chips(pos):
    x, y, _ = pos
    return [(1 - x, y), (x, 1 - y), (1 - x, 1 - y)]


def _remote(src, dst, send_sem, recv_sem, to):
    return pltpu.make_async_remote_copy(src_ref=src, dst_ref=dst, send_sem=send_sem, recv_sem=recv_sem,
                                        device_id=to, device_id_type=MESH)


def pre_exchange(c, w_ada, b_ada, logits):
    def body(c_ref, wada_ref, bada_ref, logit_ref, mod_ref, scall_ref, lb_ref,
             cg_ref, modall_ref, parts_ref, send1, recv1, send2, recv2):
        pos = _mesh_pos()
        cv = c_ref[...]
        slot = lambda p: pl.ds(pl.multiple_of(8 * _lin(p), 8), 8)
        cg_ref[slot(pos), :] = jnp.broadcast_to(cv * _sigmoid(cv), (8, D_MODEL))
        lb_ref[...] = _sigmoid(logit_ref[0:1, :] - logit_ref[1:2, :])
        peers = [_xor_peer(pos, k) for k in range(1, N_DEV)]
        gather = [_remote(cg_ref.at[slot(pos)], cg_ref.at[slot(pos)], send1.at[n], recv1.at[n], p)
                  for n, p in enumerate(peers)]
        for cp in gather:
            cp.start()
        for n, p in enumerate(peers):
            _remote(cg_ref.at[slot(p)], cg_ref.at[slot(p)], send1.at[n], recv1.at[n], p).wait_recv()
        modall_ref[...] = _dot(cg_ref[...], wada_ref[...])
        scatter = [_remote(modall_ref.at[slot(p)], parts_ref.at[slot(pos)], send2.at[n], recv2.at[n], p)
                   for n, p in enumerate(peers)]
        for cp in scatter:
            cp.start()
        parts_ref[slot(pos), :] = modall_ref[slot(pos), :]
        for n, p in enumerate(peers):
            _remote(modall_ref.at[slot(p)], parts_ref.at[slot(p)], send2.at[n], recv2.at[n], p).wait_recv()
        for cp in gather + scatter:
            cp.wait_send()
        for j in range(N_DEV):
            cols = slice(j * SHARD_ADA, (j + 1) * SHARD_ADA)
            mod_ref[:, cols] = parts_ref[8 * j:8 * j + 1, :] + bada_ref[:, cols]
            scall_ref[j:j + 1, :] = cg_ref[8 * j:8 * j + 1, :]

    vmem = pl.BlockSpec(memory_space=pltpu.VMEM)
    return pl.pallas_call(
        body, name="pre_exchange",
        in_specs=[vmem] * 4, out_specs=[vmem] * 3,
        out_shape=[jax.ShapeDtypeStruct((1, 3 * D_MODEL), F32), jax.ShapeDtypeStruct((N_DEV, D_MODEL), F32),
                   jax.ShapeDtypeStruct((1, D_MODEL), F32)],
        scratch_shapes=[pltpu.VMEM((N_DEV * 8, D_MODEL), F32), pltpu.VMEM((N_DEV * 8, SHARD_ADA), F32),
                        pltpu.VMEM((N_DEV * 8, SHARD_ADA), F32)] + [pltpu.SemaphoreType.DMA((N_DEV - 1,))] * 4,
        compiler_params=pltpu.CompilerParams(vmem_limit_bytes=VMEM_LIMIT),
    )(c, w_ada, b_ada, logits)


def weight_gather(win_sh, wout_sh):
    def body(win_ref, wout_ref, wg_ref, woutg_ref, send, recv, local):
        pos = _mesh_pos()
        x, y, c = pos
        sibling = (x, y, 1 - c)

        def route(core):
            return [(x + (1 - core) * (1 - 2 * x), y + core * (1 - 2 * y)),
                    (x + core * (1 - 2 * x), y + (1 - core) * (1 - 2 * y)),
                    (1 - x, 1 - y)]

        chips, sib_chips = route(c), route(1 - c)
        mine, first, later = [], [], []
        for a, (src, out) in enumerate(((win_ref, wg_ref), (wout_ref, woutg_ref))):
            def copy(k, block, to, src_ref=None, a=a, out=out):
                dst = out.at[_lin(block)]
                return _remote(dst if src_ref is None else src_ref, dst, send.at[7 * a + k], recv.at[7 * a + k], to)
            mine.append(pltpu.make_async_copy(src, out.at[_lin(pos)], local.at[a]))
            first += [copy(0, pos, sibling, src), copy(1, pos, (*chips[0], c), src), copy(2, pos, (*chips[1], c), src)]
            later.append([[copy(3, (*chips[0], c), (*chips[1], c)), copy(4, (*chips[0], c), sibling)],
                          [copy(5, (*chips[1], c), sibling)],
                          [copy(6, (*chips[2], c), sibling)]])
        for cp in mine + first:
            cp.start()
        for j in range(3):
            for a, out in enumerate((wg_ref, woutg_ref)):
                dst = out.at[_lin((*chips[j], c))]
                _remote(dst, dst, send.at[7 * a + 1 + j], recv.at[7 * a + 1 + j], pos).wait_recv()
                for cp in later[a][j]:
                    cp.start()
        for a, out in enumerate((wg_ref, woutg_ref)):
            dst = out.at[_lin(sibling)]
            _remote(dst, dst, send.at[7 * a], recv.at[7 * a], pos).wait_recv()
            for j in range(3):
                dst = out.at[_lin((*sib_chips[j], 1 - c))]
                _remote(dst, dst, send.at[7 * a + 4 + j], recv.at[7 * a + 4 + j], pos).wait_recv()
        for cp in first + [cp for per_array in later for group in per_array for cp in group]:
            cp.wait_send()
        for cp in mine:
            cp.wait()

    any_spec = pl.BlockSpec(memory_space=pl.ANY)
    return pl.pallas_call(
        body, name="weight_gather",
        in_specs=[any_spec, any_spec], out_specs=[any_spec, any_spec],
        out_shape=[jax.ShapeDtypeStruct((N_DEV,) + win_sh.shape, win_sh.dtype),
                   jax.ShapeDtypeStruct((N_DEV,) + wout_sh.shape, wout_sh.dtype)],
        scratch_shapes=[pltpu.SemaphoreType.DMA((14,)), pltpu.SemaphoreType.DMA((14,)),
                        pltpu.SemaphoreType.DMA((2,))],
    )(win_sh, wout_sh)


def grad_pair_exchange(g_in, g_out):
    def body(gin_ref, gout_ref, ra_ref, rb_ref, send, recv):
        pos = _mesh_pos()
        x, y, c = pos
        sibling = (x, y, 1 - c)
        copies = []
        for a, (src, dst) in enumerate(((gin_ref, ra_ref), (gout_ref, rb_ref))):
            for q in range(4):
                copies.append(_remote(src.at[2 * q + (1 - c)], dst.at[q], send.at[4 * a + q], recv.at[4 * a + q],
                                      sibling))
        for cp in copies:
            cp.start()
        for cp in copies:
            cp.wait_recv()
        for cp in copies:
            cp.wait_send()

    any_spec = pl.BlockSpec(memory_space=pl.ANY)
    return pl.pallas_call(
        body, name="grad_pair_exchange",
        in_specs=[any_spec, any_spec], out_specs=[any_spec, any_spec],
        out_shape=[jax.ShapeDtypeStruct((4,) + g_in.shape[1:], F32), jax.ShapeDtypeStruct((4,) + g_out.shape[1:], F32)],
        scratch_shapes=[pltpu.SemaphoreType.DMA((8,)), pltpu.SemaphoreType.DMA((8,))],
    )(g_in, g_out)


def pair_sum(g_in, ra, g_out, rb, c_idx):
    tr = 256

    def body(c_ref, gin_ref, ra_ref, gout_ref, rb_ref, sb_ref, sbo_ref):
        del c_ref
        sb_ref[...] = (gin_ref[...] + ra_ref[...]).astype(sb_ref.dtype)
        sbo_ref[...] = gout_ref[...] + rb_ref[...]

    n_i = D_MODEL // tr
    return pl.pallas_call(
        body, name="pair_sum",
        grid_spec=pltpu.PrefetchScalarGridSpec(
            num_scalar_prefetch=1, grid=(4, n_i),
            in_specs=[pl.BlockSpec((None, tr, SHARD_IN), lambda q, i, c: (2 * q + c[0], i, 0)),
                      pl.BlockSpec((None, tr, SHARD_IN), lambda q, i, c: (q, i, 0)),
                      pl.BlockSpec((None, SHARD_OUT // n_i, D_MODEL), lambda q, i, c: (2 * q + c[0], i, 0)),
                      pl.BlockSpec((None, SHARD_OUT // n_i, D_MODEL), lambda q, i, c: (q, i, 0))],
            out_specs=[pl.BlockSpec((None, tr, SHARD_IN), lambda q, i, c: (q, i, 0)),
                       pl.BlockSpec((None, SHARD_OUT // n_i, D_MODEL), lambda q, i, c: (q, i, 0))]),
        out_shape=[jax.ShapeDtypeStruct(ra.shape, _BF), jax.ShapeDtypeStruct(rb.shape, F32)],
        compiler_params=_params(("arbitrary", "arbitrary")),
    )(c_idx, g_in, ra, g_out, rb)


_HBM = pl.BlockSpec(memory_space=pltpu.HBM)
_SEM = pl.BlockSpec(memory_space=pltpu.SEMAPHORE)
_N_CHIP_COPIES = 6


def _chip_copies(sb_ref, sbo_ref, rc_ref, rco_ref, send, recv):
    pos = _mesh_pos()
    copies = []
    for a, (src, dst) in enumerate(((sb_ref, rc_ref), (sbo_ref, rco_ref))):
        for j, chip in enumerate(_other_chips(pos)):
            copies.append(_remote(src.at[2 * chip[0] + chip[1]], dst.at[j], send.at[3 * a + j], recv.at[3 * a + j],
                                  (*chip, pos[2])))
    return copies


def grad_chip_start(sb, sbo):
    def body(sb_ref, sbo_ref, rc_ref, rco_ref, send, recv, sb_thru, sbo_thru, rc_thru, rco_thru, token):
        del sb_thru, sbo_thru, rc_thru, rco_thru
        for cp in _chip_copies(sb_ref, sbo_ref, rc_ref, rco_ref, send, recv):
            cp.start()
        token[...] = jnp.zeros_like(token)

    hbm = lambda a: pltpu.with_memory_space_constraint(a, pltpu.HBM)
    rc = lax.empty((3,) + sb.shape[1:], sb.dtype)
    rco = lax.empty((3,) + sbo.shape[1:], sbo.dtype)
    return pl.pallas_call(
        body, name="grad_chip_start",
        in_specs=[_HBM] * 4,
        out_specs=[_SEM, _SEM, _HBM, _HBM, _HBM, _HBM, pl.BlockSpec(memory_space=pltpu.VMEM)],
        out_shape=[pltpu.SemaphoreType.DMA((_N_CHIP_COPIES,)), pltpu.SemaphoreType.DMA((_N_CHIP_COPIES,)),
                   pltpu.HBM(sb.shape, sb.dtype), pltpu.HBM(sbo.shape, sbo.dtype),
                   pltpu.HBM(rc.shape, rc.dtype), pltpu.HBM(rco.shape, rco.dtype),
                   jax.ShapeDtypeStruct((8, LANE), F32)],
        input_output_aliases={0: 2, 1: 3, 2: 4, 3: 5},
        compiler_params=pltpu.CompilerParams(has_side_effects=pltpu.SideEffectType.DATAFLOW_SIDE_EFFECTING),
    )(hbm(sb), hbm(sbo), hbm(rc), hbm(rco))


def grad_chip_wait(send, recv, sb, sbo, rc, rco, after):
    def body(sb_ref, sbo_ref, rc_ref, rco_ref, send, recv, after_ref, sb_o, sbo_o, rc_o, rco_o):
        del after_ref, sb_o, sbo_o, rc_o, rco_o
        for cp in _chip_copies(sb_ref, sbo_ref, rc_ref, rco_ref, send, recv):
            cp.wait_send()
            cp.wait_recv()

    return pl.pallas_call(
        body, name="grad_chip_wait",
        in_specs=[_HBM] * 4 + [_SEM, _SEM, pl.BlockSpec(memory_space=pl.ANY)],
        out_specs=[_HBM] * 4,
        out_shape=[pltpu.HBM(sb.shape, sb.dtype), pltpu.HBM(sbo.shape, sbo.dtype),
                   pltpu.HBM(rc.shape, rc.dtype), pltpu.HBM(rco.shape, rco.dtype)],
        input_output_aliases={0: 0, 1: 1, 2: 2, 3: 3},
        compiler_params=pltpu.CompilerParams(has_side_effects=pltpu.SideEffectType.DATAFLOW_SIDE_EFFECTING),
    )(sb, sbo, rc, rco, send, recv, after)


def pack_gather(pack):
    def body(pack_ref, packs_ref, psend, precv):
        pos = _mesh_pos()
        me = _lin(pos)
        packs_ref[me] = pack_ref[...]
        peers = [_xor_peer(pos, k) for k in range(1, N_DEV)]
        gather = [_remote(packs_ref.at[me], packs_ref.at[me], psend.at[n], precv.at[n], p) for n, p in enumerate(peers)]
        for cp in gather:
            cp.start()
        for n, p in enumerate(peers):
            _remote(packs_ref.at[_lin(p)], packs_ref.at[_lin(p)], psend.at[n], precv.at[n], p).wait_recv()
        for cp in gather:
            cp.wait_send()

    vmem = pl.BlockSpec(memory_space=pltpu.VMEM)
    return pl.pallas_call(
        body, name="pack_gather", in_specs=[vmem], out_specs=vmem,
        out_shape=jax.ShapeDtypeStruct((N_DEV,) + pack.shape, F32),
        scratch_shapes=[pltpu.SemaphoreType.DMA((N_DEV - 1,)), pltpu.SemaphoreType.DMA((N_DEV - 1,))],
    )(pack)


def pack_rows(vec_mid, vec_ada, dlb):
    def body(mid_ref, ada_ref, dlb_ref, o_ref):
        mid = lambda r: mid_ref[r:r + 1, :]
        rows = [ada_ref[0:1, :], dlb_ref[...], mid(MID_HG_G), mid(MID_RET_G), mid(MID_FINAL_G),
                ada_ref[2:3, :], ada_ref[1:2, :], mid(MID_GATE), mid(MID_LOSS)]
        o_ref[...] = jnp.zeros_like(o_ref)
        for n, row in enumerate(rows):
            o_ref[n:n + 1, :] = row

    vmem = pl.BlockSpec(memory_space=pltpu.VMEM)
    return pl.pallas_call(body, name="pack_rows", in_specs=[vmem] * 3, out_specs=vmem,
                          out_shape=jax.ShapeDtypeStruct((PACK_ROWS, D_MODEL), F32))(vec_mid, vec_ada, dlb)


def _adamw(w, g, m, v):
    m = ADAM_B1 * m + (1.0 - ADAM_B1) * g
    v = ADAM_B2 * v + (1.0 - ADAM_B2) * (g * g)
    m_hat = m / (1.0 - ADAM_B1 ** ADAM_STEP)
    v_hat = v / (1.0 - ADAM_B2 ** ADAM_STEP)
    delta = -ADAM_LR * (m_hat / (jnp.sqrt(v_hat) + ADAM_EPS) + ADAM_WD * w)
    return delta, m, v


def adam_shard(chip_idx, own, parts, w, m, v, name):
    rows, cols = w.shape
    tr = min(rows, 128)

    def body(chip_ref, p0, p1, p2, p3, w_ref, m_ref, v_ref, g_ref, d_ref, nm_ref, nv_ref):
        del chip_ref
        g = ((p0[...].astype(F32) + p1[...].astype(F32)) + p2[...].astype(F32)) + p3[...].astype(F32)
        g_ref[...] = g
        d_ref[...], nm_ref[...], nv_ref[...] = _adamw(w_ref[...], g, m_ref[...], v_ref[...])

    part = lambda q: pl.BlockSpec((None, tr, cols), lambda i, chip, q=q: (q, i, 0))
    tile = pl.BlockSpec((tr, cols), lambda i, chip: (i, 0))
    return pl.pallas_call(
        body, name=name,
        grid_spec=pltpu.PrefetchScalarGridSpec(
            num_scalar_prefetch=1, grid=(rows // tr,),
            in_specs=[pl.BlockSpec((None, tr, cols), lambda i, chip: (chip[0], i, 0)), part(0), part(1), part(2),
                      tile, tile, tile],
            out_specs=[tile] * 4),
        out_shape=[jax.ShapeDtypeStruct(w.shape, F32)] * 4,
        compiler_params=_params(("arbitrary",)),
    )(chip_idx, own, parts, parts, parts, w, m, v)


def adam_ada(sc_t, dmod_all, me_idx, w, m, v):
    def body(me_ref, sc_ref, dm_ref, w_ref, m_ref, v_ref, g_ref, d_ref, nm_ref, nv_ref):
        del me_ref
        g = _dot_f32(sc_ref[...], dm_ref[...])
        g_ref[...] = g
        d_ref[...], nm_ref[...], nv_ref[...] = _adamw(w_ref[...], g, m_ref[...], v_ref[...])

    full = pl.BlockSpec(w.shape, lambda i, me: (0, 0))
    return pl.pallas_call(
        body, name="adam_ada",
        grid_spec=pltpu.PrefetchScalarGridSpec(
            num_scalar_prefetch=1, grid=(1,),
            in_specs=[pl.BlockSpec(sc_t.shape, lambda i, me: (0, 0)),
                      pl.BlockSpec((LANE, SHARD_ADA), lambda i, me: (0, me[0])), full, full, full],
            out_specs=[full] * 4),
        out_shape=[jax.ShapeDtypeStruct(w.shape, F32)] * 4,
        compiler_params=_params(("arbitrary",)),
    )(me_idx, sc_t, dmod_all, w, m, v)


def adam_vectors(packs, lb, params, ms, vs):
    n = len(params)

    def body(*refs):
        packs_ref, lb_ref = refs[0], refs[1]
        w_refs, m_refs, v_refs = refs[2:2 + n], refs[2 + n:2 + 2 * n], refs[2 + 2 * n:2 + 3 * n]
        loss_ref = refs[2 + 3 * n]
        outs = refs[3 + 3 * n:3 + 7 * n]
        tot_ref = refs[3 + 7 * n]
        tot = packs_ref[0]
        for d in range(1, N_DEV):
            tot = tot + packs_ref[d]
        tot_ref[...] = tot
        row = lambda r: tot_ref[r:r + 1, :]
        lbv = lb_ref[...]
        dl0 = row(ROW_LB) * lbv * (1.0 - lbv)
        grads = [[row(ROW_NORM_G)],
                 [jnp.concatenate([row(ROW_SHIFT), row(ROW_SCALE), row(ROW_GATE)], axis=1)],
                 [dl0, -dl0],
                 [row(ROW_HG_G)], [row(ROW_RET_G)], [row(ROW_FINAL_G)]]
        loss_ref[...] = tot_ref[ROW_LOSS:ROW_LOSS + 1, 0:LANE]
        for j, g_rows in enumerate(grads):
            for r, g in enumerate(g_rows):
                rs = slice(r, r + 1)
                d, nm, nv = _adamw(w_refs[j][rs, :], g, m_refs[j][rs, :], v_refs[j][rs, :])
                outs[4 * j][rs, :] = g
                outs[4 * j + 1][rs, :] = d
                outs[4 * j + 2][rs, :] = nm
                outs[4 * j + 3][rs, :] = nv

    vmem = pl.BlockSpec(memory_space=pltpu.VMEM)
    out_shape = [jax.ShapeDtypeStruct((1, LANE), F32)]
    for w in params:
        out_shape += [jax.ShapeDtypeStruct(w.shape, F32)] * 4
    return pl.pallas_call(
        body, name="adam_vectors", in_specs=[vmem] * (2 + 3 * n), out_specs=[vmem] * len(out_shape),
        out_shape=out_shape, scratch_shapes=[pltpu.VMEM((PACK_ROWS, D_MODEL), F32)],
    )(packs, lb, *params, *ms, *vs)


def kernel(x, c, norm_g, w_ada, b_ada, w_in, hg_lb_logits, hg_norm_g, ret_norm_g, w_out, final_g, loss_target, m_norm_g, m_w_ada, m_b_ada, m_w_in, m_hg_lb_logits, m_hg_norm_g, m_ret_norm_g, m_w_out, m_final_g, v_norm_g, v_w_ada, v_b_ada, v_w_in, v_hg_lb_logits, v_hg_norm_g, v_ret_norm_g, v_w_out, v_final_g):
    pos = _mesh_pos()
    me_idx = jnp.reshape(_lin(pos), (1,)).astype(jnp.int32)
    c_idx = jnp.reshape(pos[2], (1,)).astype(jnp.int32)
    vec = lambda a: a.reshape(1, D_MODEL)

    mod, scall, lb = pre_exchange(c, w_ada[0], b_ada, hg_lb_logits)
    wtg, woutg = weight_gather(w_in[0].T.astype(_BF), w_out[0].astype(_BF))
    chip_idx = jnp.reshape(2 * pos[0] + pos[1], (1,)).astype(jnp.int32)

    def start_exchange(dwin, dwout):
        dwout = dwout.reshape(N_DEV, SHARD_OUT, D_MODEL)
        ra, rb = grad_pair_exchange(dwin, dwout)
        sb, sbo = pair_sum(dwin, ra, dwout, rb, c_idx)
        send, recv, sb, sbo, rc, rco, token = grad_chip_start(sb, sbo)
        return token, (send, recv, sb, sbo, rc, rco)

    grad_x, _, _, vec_mid, vec_ada, dlb, pending = device_step(
        x[0], loss_target[0], mod, lb, wtg.reshape(D_IN, D_MODEL), woutg.reshape(D_MODEL, D_MODEL), norm_g,
        hg_norm_g, ret_norm_g, vec(final_g), start_exchange)
    packs = pack_gather(pack_rows(vec_mid, vec_ada, dlb))
    dmod_all = packs[:, ROW_SHIFT:ROW_GATE + 1, :].reshape(N_DEV, 3 * D_MODEL)
    dmod_all = jnp.pad(dmod_all, ((0, LANE - N_DEV), (0, 0)))
    sc_t = jnp.pad(scall.T, ((0, 0), (0, LANE - N_DEV)))
    g_ada, d_ada, nm_ada, nv_ada = adam_ada(sc_t, dmod_all, me_idx, w_ada[0], m_w_ada[0], v_w_ada[0])
    small = adam_vectors(
        packs, lb,
        (norm_g, b_ada, hg_lb_logits, hg_norm_g, ret_norm_g, vec(final_g)),
        (m_norm_g, m_b_ada, m_hg_lb_logits, m_hg_norm_g, m_ret_norm_g, vec(m_final_g)),
        (v_norm_g, v_b_ada, v_hg_lb_logits, v_hg_norm_g, v_ret_norm_g, vec(v_final_g)))
    loss = small[0][0, 0]
    sb, sbo, rc, rco = grad_chip_wait(*pending, small[0])
    g_in, d_in, nm_in, nv_in = adam_shard(chip_idx, sb, rc, w_in[0], m_w_in[0], v_w_in[0], "adam_w_in")
    g_out, d_out, nm_out, nv_out = adam_shard(chip_idx, sbo, rco, w_out[0], m_w_out[0], v_w_out[0], "adam_w_out")
    (g_ng, d_ng, nm_ng, nv_ng), (g_b, d_b, nm_b, nv_b), (g_lb, d_lb, nm_lb, nv_lb), (g_hg, d_hg, nm_hg, nv_hg), \
        (g_rg, d_rg, nm_rg, nv_rg), (g_fg, d_fg, nm_fg, nv_fg) = [small[1 + 4 * j:5 + 4 * j] for j in range(6)]
    flat = lambda a: a.reshape(D_MODEL)

    def group(ng, ada, b, win, lbl, hg, rg, wo, fg):
        return (ng, ada[None], b, win[None], lbl, hg, rg, wo[None], flat(fg))

    return (loss, grad_x[None],
            *group(g_ng, g_ada, g_b, g_in, g_lb, g_hg, g_rg, g_out, g_fg),
            *group(d_ng, d_ada, d_b, d_in, d_lb, d_hg, d_rg, d_out, d_fg),
            *group(nm_ng, nm_ada, nm_b, nm_in, nm_lb, nm_hg, nm_rg, nm_out, nm_fg),
            *group(nv_ng, nv_ada, nv_b, nv_in, nv_lb, nv_hg, nv_rg, nv_out, nv_fg))
```

```python
import functools

import numpy as np
import jax
import jax.numpy as jnp
from jax import lax
from jax.experimental import pallas as pl
from jax.experimental.pallas import tpu as pltpu

F32 = jnp.float32
_BF = jnp.bfloat16

D_MODEL = 1024
N_HEADS = 8
LANE = 128
RET_DK = 64
D_IN = 9216
N_DEV = 8
SHARD_IN = D_IN // N_DEV
SHARD_ADA = 3 * D_MODEL // N_DEV
SHARD_OUT = D_MODEL // N_DEV
N_CB = D_IN // LANE
CB_PER_SHARD = SHARD_IN // LANE
CHUNK = 128
N_LEVELS = 7
EPS = 1e-6
ROPE_BASE = 10000.0
CB_HQ, CB_HF, CB_HI, CB_HZ, CB_RQ, CB_RK, CB_RV, CB_RZ, CB_GA, CB_GB = 0, 8, 16, 24, 32, 36, 40, 48, 56, 64
VMEM_LIMIT = 56 * 1024 * 1024

ADAM_LR, ADAM_B1, ADAM_B2, ADAM_EPS, ADAM_WD, ADAM_STEP = 0.001, 0.9, 0.999, 1e-08, 0.01, 10

_NN = (((1,), (0,)), ((), ()))
_NT = (((1,), (1,)), ((), ()))
_TN = (((0,), (0,)), ((), ()))
MESH = pl.DeviceIdType.MESH


def _dot(a, b, dims=_NN):
    return lax.dot_general(a.astype(_BF), b.astype(_BF), dims, preferred_element_type=F32)


def _split2(a):
    hi = a.astype(_BF)
    lo = (a - hi.astype(F32)).astype(_BF)
    return jnp.concatenate([hi, lo], axis=1)


def _dot_sel(sel, a):
    n = a.shape[1]
    r = lax.dot_general(sel.astype(_BF), _split2(a), _NN, preferred_element_type=F32)
    return r[:, :n] + r[:, n:]


def _dot_f32(a, b):
    def pieces(v):
        p1 = v.astype(_BF)
        r1 = v - p1.astype(F32)
        p2 = r1.astype(_BF)
        p3 = (r1 - p2.astype(F32)).astype(_BF)
        return p1, p2, p3
    a1, a2, a3 = pieces(a)
    b1, b2, b3 = pieces(b)
    d = lambda u, v: lax.dot_general(u, v, _NN, preferred_element_type=F32)
    return ((d(a1, b3) + d(a2, b2) + d(a3, b1)) + (d(a1, b2) + d(a2, b1))) + d(a1, b1)


def _sigmoid(v):
    return 1.0 / (1.0 + jnp.exp(-v))


def _params(sem=None):
    return pltpu.CompilerParams(dimension_semantics=sem, vmem_limit_bytes=VMEM_LIMIT)


def _hgrn_consts():
    c, nl = CHUNK, N_LEVELS
    t = np.arange(c)[:, None]
    j = np.arange(c)[None, :]
    sel = [j <= t]
    masks = [j == t]
    for l in range(1, nl + 1):
        m = ((t >> l) << l) + (1 << (l - 1)) - 1
        sec = t > m
        sel.append(np.where(sec, (j > m) & (j <= t), (j > t) & (j <= m)))
        same = (t >> l) == (j >> l)
        masks.append(same & sec & (j <= m))
    sel.append(j > t)
    sel = np.concatenate(sel, 0).astype(np.float32)
    masks = np.stack(masks).astype(np.float32)
    sgn = np.stack([np.where((t & (1 << (l - 1))) != 0, 1.0, -1.0) * np.ones((1, LANE)) for l in range(3, nl + 1)])
    return dict(tri=jnp.asarray(sel[:c], _BF),
                lvl=jnp.asarray(masks, F32),
                sgn=jnp.asarray(sgn, F32),
                sel_t=jnp.asarray(sel.T, _BF),
                lvl_b=jnp.asarray(masks, _BF),
                lvlt_b=jnp.asarray(np.swapaxes(masks, 1, 2), _BF))


def _level_exponents(b, logf, b_scr, sgn_ref):
    c = CHUNK
    b_scr[...] = b
    row = lax.broadcasted_iota(jnp.int32, (c, LANE), 0)
    nxt = pltpu.roll(logf, c - 1, 0)
    prv = pltpu.roll(logf, 1, 0)
    r4 = row & 3
    out = [jnp.where((row & 1) == 1, logf, 0.0),
           jnp.where(r4 == 0, nxt, jnp.where(r4 == 1, 0.0, jnp.where(r4 == 2, logf, logf + prv)))]
    for l in range(3, N_LEVELS + 1):
        size, half = 1 << l, 1 << (l - 1)
        ref = jnp.concatenate([jnp.broadcast_to(b_scr[i * size + half - 1:i * size + half, :], (size, LANE))
                               for i in range(c // size)], axis=0)
        out.append((b - ref) * sgn_ref[l - 3])
    return out


def _hgrn_chunk(hq, hf, hi, lbv, tri_ref, sgn_ref, b_scr):
    sq = _sigmoid(hq)
    q = hq * sq
    sg = _sigmoid(hf)
    omlb = 1.0 - lbv
    f = lbv + omlb * sg
    k = 1.0 - f
    logf = jnp.log(f)
    b = _dot_sel(tri_ref[...], logf)
    bc = jnp.sum(logf, axis=0, keepdims=True)
    lev = [None] + [jnp.exp(e) for e in _level_exponents(b, logf, b_scr, sgn_ref)]
    return dict(sq=sq, q=q, sg=sg, omlb=omlb, f=f, k=k, v=hi, eb=jnp.exp(b), erem=jnp.exp(bc - b), ebc=jnp.exp(bc),
                lev=lev)


def _blockdiag(a, b):
    z = jnp.zeros_like(a)
    return jnp.concatenate([jnp.concatenate([a, z], axis=1), jnp.concatenate([z, b], axis=1)], axis=0)


def _level_operands(a):
    q, k = a["q"], a["k"]
    ql = [q.astype(_BF)] + [(q * a["lev"][l]).astype(_BF) for l in range(1, N_LEVELS + 1)]
    kl = [k.astype(_BF)] + [(k * a["lev"][l]).astype(_BF) for l in range(1, N_LEVELS + 1)]
    pairs = range(0, N_LEVELS + 1, 2)
    return ([jnp.concatenate([ql[l], ql[l + 1]], axis=1) for l in pairs], [_blockdiag(kl[l], kl[l + 1]) for l in pairs],
            ql, kl)


def _hgrn_scores(a, lvl_ref, q_pairs, k_diags):
    acc = None
    for n, (qp, kd) in enumerate(zip(q_pairs, k_diags)):
        both = lax.dot_general(qp, kd, _NT, preferred_element_type=F32)
        part = lvl_ref[2 * n] * both[:, :CHUNK] + lvl_ref[2 * n + 1] * both[:, CHUNK:]
        acc = part if acc is None else acc + part
    return acc


SCAN_UNROLL = 4
RET_UNROLL = 8


def _writeback_reserve(step, make_copies):
    slot = step % 2

    @pl.when(step >= 2)
    def _():
        for cp in make_copies(slot):
            cp.wait()

    return slot


def _writeback_commit(step, n_steps, slot, make_copies):
    for cp in make_copies(slot):
        cp.start()

    @pl.when(step == n_steps - 1)
    def _():
        for cp in make_copies(slot):
            cp.wait()
        if n_steps > 1:
            for cp in make_copies(1 - slot):
                cp.wait()


def _resident(const):
    zeros = (0,) * const.ndim
    return pl.BlockSpec(const.shape, lambda p, t: zeros)


def _time_block(t_len):
    return min(t_len, 1024)


def hgrn_forward(pb, lb, t_len):
    nc = t_len // CHUNK
    tb = _time_block(t_len)
    ncb = tb // CHUNK
    consts = _hgrn_consts()
    operands = [consts[n] for n in ("tri", "lvl", "sgn")]

    def body(hq_ref, hf_ref, hi_ref, lb_ref, tri_ref, lvl_ref, sgn_ref, o_ref, ssave_ref, asave_ref, st_ref, b_scr):
        @pl.when(pl.program_id(1) == 0)
        def _():
            st_ref[...] = jnp.zeros_like(st_ref)

        def chunk(ci, carry):
            r = pl.ds(pl.multiple_of(ci * CHUNK, CHUNK), CHUNK)
            for hd in range(2):
                lbv = lb_ref[:, hd * LANE:(hd + 1) * LANE]
                a = _hgrn_chunk(hq_ref[hd, r, :], hf_ref[hd, r, :], hi_ref[hd, r, :], lbv, tri_ref, sgn_ref,
                                b_scr.at[hd])
                q_pairs, k_diags, _, _ = _level_operands(a)
                st = st_ref[hd]
                ssave_ref[hd, ci] = st
                scores = _hgrn_scores(a, lvl_ref, q_pairs, k_diags).astype(asave_ref.dtype)
                asave_ref[hd, ci] = scores
                o_ref[hd, r, :] = _dot(a["q"] * a["eb"], st, _NT) + _dot(scores, a["v"])
                st_ref[hd] = st * a["ebc"] + _dot(a["v"], a["k"] * a["erem"], _TN)
            return carry

        lax.fori_loop(0, ncb, chunk, 0, unroll=SCAN_UNROLL)

    pair = lambda base: pl.BlockSpec((2, tb, LANE), lambda p, t, base=base: (base // 2 + p, t, 0))
    per_chunk = pl.BlockSpec((2, ncb, LANE, LANE), lambda p, t: (p, t, 0, 0))
    return pl.pallas_call(
        body, name="hgrn_fwd", grid=(N_HEADS // 2, t_len // tb),
        in_specs=[pair(CB_HQ), pair(CB_HF), pair(CB_HI),
                  pl.BlockSpec((1, 2 * LANE), lambda p, t: (0, p))] + [_resident(c) for c in operands],
        out_specs=[pl.BlockSpec((2, tb, LANE), lambda p, t: (p, t, 0)), per_chunk, per_chunk],
        out_shape=[jax.ShapeDtypeStruct((N_HEADS, t_len, LANE), F32),
                   jax.ShapeDtypeStruct((N_HEADS, nc, LANE, LANE), F32),
                   jax.ShapeDtypeStruct((N_HEADS, nc, CHUNK, CHUNK), _BF)],
        scratch_shapes=[pltpu.VMEM((2, LANE, LANE), F32), pltpu.VMEM((2, CHUNK, LANE), F32)],
        compiler_params=_params(("arbitrary", "arbitrary")),
    )(pb, pb, pb, lb, *operands)


def hgrn_backward(pb, lb, do, ssave, asave, dpb, t_len):
    tb = _time_block(t_len)
    ncb, ntb = tb // CHUNK, t_len // tb
    consts = _hgrn_consts()
    operands = [consts[n] for n in ("tri", "sgn", "sel_t", "lvl_b", "lvlt_b")]

    def body(hq_ref, hf_ref, hi_ref, lb_ref, do_ref, ssave_ref, asave_ref, tri_ref, sgn_ref, selt_ref, lvlb_ref,
             lvltb_ref, dpb_in, dpb_ref, dlb_ref, dq_buf, df_buf, di_buf, dst_ref, b_scr, sems):
        del dpb_in
        p, t = pl.program_id(0), pl.program_id(1)
        step = p * ntb + t
        rows = pl.ds(pl.multiple_of((ntb - 1 - t) * tb, tb), tb)

        def out_copies(sl):
            return [pltpu.make_async_copy(buf.at[sl], dpb_ref.at[pl.ds(base + 2 * p, 2), rows], sems.at[sl, n])
                    for n, (buf, base) in enumerate(((dq_buf, CB_HQ), (df_buf, CB_HF), (di_buf, CB_HI)))]

        slot = _writeback_reserve(step, out_copies)

        @pl.when(t == 0)
        def _():
            dst_ref[...] = jnp.zeros_like(dst_ref)
            dlb_ref[...] = jnp.zeros_like(dlb_ref)

        def chunk(i, carry):
            ci = ncb - 1 - i
            r = pl.ds(pl.multiple_of(ci * CHUNK, CHUNK), CHUNK)
            for hd in range(2):
                head_chunk(hd, ci, r)
            return carry

        def head_chunk(hd, ci, r):
            lbv = lb_ref[:, hd * LANE:(hd + 1) * LANE]
            hq = hq_ref[hd, r, :]
            a = _hgrn_chunk(hq, hf_ref[hd, r, :], hi_ref[hd, r, :], lbv, tri_ref, sgn_ref, b_scr.at[hd])
            _, k_diags, ql, kl = _level_operands(a)
            q, k, v = a["q"], a["k"], a["v"]
            g = do_ref[hd, r, :]
            st0 = ssave_ref[hd, ci]
            dst = dst_ref[hd]
            scores = asave_ref[hd, ci]
            da = _dot(g, v, _NT)
            da_t = _dot(v, g, _NT)
            kb = k * a["erem"]
            qb = q * a["eb"]
            dv = _dot(scores, g, _TN) + _dot(kb, dst, _NT)
            dq_inter = _dot(g, st0) * a["eb"]
            dk_state = _dot(v, dst) * a["erem"]
            dq, dk = dq_inter, dk_state
            de = [q * dq_inter]
            da_b, dat_b = da.astype(_BF), da_t.astype(_BF)
            for n in range(len(k_diags)):
                l0, l1 = 2 * n, 2 * n + 1
                da_pair = jnp.concatenate([lvlb_ref[l0] * da_b, lvlb_ref[l1] * da_b], axis=1)
                dat_pair = jnp.concatenate([lvltb_ref[l0] * dat_b, lvltb_ref[l1] * dat_b], axis=1)
                dq_both = lax.dot_general(da_pair, k_diags[n], _NN, preferred_element_type=F32)
                dk_both = lax.dot_general(dat_pair, _blockdiag(ql[l0], ql[l1]), _NN, preferred_element_type=F32)
                for l, cols in ((l0, slice(0, LANE)), (l1, slice(LANE, 2 * LANE))):
                    dql, dkl = dq_both[:, cols], dk_both[:, cols]
                    if l > 0:
                        e = a["lev"][l]
                        dql, dkl = dql * e, dkl * e
                        de.append(q * dql + k * dkl)
                    dq = dq + dql
                    dk = dk + dkl
            de.append(k * dk_state)
            dst_ref[hd] = dst * a["ebc"] + _dot(g, qb, _TN)
            dbc = jnp.sum(dst * st0, axis=0, keepdims=True) * a["ebc"]
            de2 = lax.dot_general(selt_ref[...], _split2(jnp.concatenate(de, axis=0)), _NN,
                                  preferred_element_type=F32)
            dlogf = de2[:, :LANE] + de2[:, LANE:] + dbc
            sq, sg = a["sq"], a["sg"]
            df = dlogf / a["f"] - dk
            dq_buf[slot, hd, r, :] = (dq * (sq * (1.0 + hq * (1.0 - sq)))).astype(dq_buf.dtype)
            df_buf[slot, hd, r, :] = (df * a["omlb"] * sg * (1.0 - sg)).astype(df_buf.dtype)
            di_buf[slot, hd, r, :] = dv.astype(di_buf.dtype)
            cols = slice(hd * LANE, (hd + 1) * LANE)
            dlb_ref[:, cols] = dlb_ref[:, cols] + jnp.sum(df * (1.0 - sg), axis=0, keepdims=True)

        lax.fori_loop(0, ncb, chunk, 0, unroll=SCAN_UNROLL)
        _writeback_commit(step, (N_HEADS // 2) * ntb, slot, out_copies)

    pair = lambda base: pl.BlockSpec((2, tb, LANE), lambda p, t, base=base: (base // 2 + p, ntb - 1 - t, 0))
    any_spec = pl.BlockSpec(memory_space=pl.ANY)
    per_chunk = pl.BlockSpec((2, ncb, LANE, LANE), lambda p, t: (p, ntb - 1 - t, 0, 0))
    return pl.pallas_call(
        body, name="hgrn_bwd", grid=(N_HEADS // 2, ntb),
        in_specs=[pair(CB_HQ), pair(CB_HF), pair(CB_HI),
                  pl.BlockSpec((1, 2 * LANE), lambda p, t: (0, p)),
                  pair(0), per_chunk, per_chunk]
        + [_resident(c) for c in operands] + [any_spec],
        out_specs=[any_spec, pl.BlockSpec((1, 2 * LANE), lambda p, t: (0, p))],
        out_shape=[jax.ShapeDtypeStruct(dpb.shape, dpb.dtype), jax.ShapeDtypeStruct((1, D_MODEL), F32)],
        scratch_shapes=[pltpu.VMEM((2, 2, tb, LANE), dpb.dtype)] * 3 + [
            pltpu.VMEM((2, LANE, LANE), F32), pltpu.VMEM((2, CHUNK, LANE), F32), pltpu.SemaphoreType.DMA((2, 3))],
        input_output_aliases={7 + len(operands): 0},
        compiler_params=_params(("arbitrary", "arbitrary")),
    )(pb, pb, pb, lb, do, ssave, asave, *operands, dpb)


def _rope_tables(t_len):
    half = RET_DK // 2
    inv_freq = 1.0 / (ROPE_BASE ** jnp.linspace(0.0, 1.0, half, dtype=F32))
    ang = jnp.arange(t_len, dtype=jnp.int32).astype(F32)[:, None] * inv_freq[None, :]
    cos, sin = jnp.cos(ang), jnp.sin(ang)
    cos_t = jnp.concatenate([cos, cos, cos, cos], axis=1)
    sin_t = jnp.concatenate([-sin, sin, -sin, sin], axis=1)
    return cos_t, sin_t


def _swap_halves(v):
    half = RET_DK // 2
    lane = lax.broadcasted_iota(jnp.int32, v.shape, 1)
    first = (lane & (RET_DK - 1)) < half
    return jnp.where(first, pltpu.roll(v, LANE - half, 1), pltpu.roll(v, half, 1))


def _ret_head_consts(hidx):
    c = CHUNK
    hf = jnp.full((1, LANE), hidx, jnp.int32).astype(F32)
    lg = jnp.log(1.0 - jnp.exp(-(5.0 + hf) * np.float32(np.log(2.0))))
    row = lax.broadcasted_iota(jnp.int32, (c, c), 0)
    col = lax.broadcasted_iota(jnp.int32, (c, c), 1)
    rel = (row - col).astype(F32)
    dm = jnp.where(rel >= 0, jnp.exp(lg[:, :1] * jnp.maximum(rel, 0.0)), 0.0)
    dm_t = jnp.where(rel <= 0, jnp.exp(lg[:, :1] * jnp.maximum(-rel, 0.0)), 0.0)
    idx = lax.broadcasted_iota(jnp.int32, (c, LANE), 0).astype(F32)
    zeta = jnp.exp(lg * (c - 1.0 - idx))
    xi = jnp.exp(lg * (idx + 1.0))
    cdec = jnp.exp(lg * float(c))
    return dm, zeta, xi, cdec, dm_t


def _lane_mask(which):
    lane = lax.broadcasted_iota(jnp.int32, (1, LANE), 1)
    return ((lane // RET_DK) == which).astype(F32)


def retention_forward(pb, cos_t, sin_t, t_len):
    nc = t_len // CHUNK

    tb = _time_block(t_len)
    ncb = tb // CHUNK

    def body(rq_ref, rk_ref, rv_ref, cos_ref, sin_ref, o_ref, rsave_ref, st_ref):
        p = pl.program_id(0)

        @pl.when(pl.program_id(1) == 0)
        def _():
            st_ref[...] = jnp.zeros_like(st_ref)

        consts = [_ret_head_consts(2 * p + hd) for hd in range(2)]

        def chunk(ci, carry):
            r = pl.ds(pl.multiple_of(ci * CHUNK, CHUNK), CHUNK)
            cs, sn = cos_ref[r, :], sin_ref[r, :]
            q = rq_ref[r, :]
            k = rk_ref[r, :]
            q = q * cs + _swap_halves(q) * sn
            k = (k * cs + _swap_halves(k) * sn) * RET_DK ** -0.5
            for hd in range(2):
                dm, zeta, xi, cdec, _ = consts[hd]
                lm = _lane_mask(hd)
                qh, kh = q * lm, k * lm
                v = rv_ref[hd, r, :]
                st = st_ref[hd]
                rsave_ref[hd, ci] = st
                scores = _dot(qh, kh, _NT) * dm
                o_ref[hd, r, :] = _dot(scores, v) + _dot(qh * xi, st, _NT)
                st_ref[hd] = st * cdec + _dot(v, kh * zeta, _TN)
            return carry

        lax.fori_loop(0, ncb, chunk, 0, unroll=RET_UNROLL)

    return pl.pallas_call(
        body, name="ret_fwd", grid=(N_HEADS // 2, t_len // tb),
        in_specs=[pl.BlockSpec((None, tb, LANE), lambda p, t: (CB_RQ + p, t, 0)),
                  pl.BlockSpec((None, tb, LANE), lambda p, t: (CB_RK + p, t, 0)),
                  pl.BlockSpec((2, tb, LANE), lambda p, t: (CB_RV // 2 + p, t, 0)),
                  pl.BlockSpec((tb, LANE), lambda p, t: (t, 0)),
                  pl.BlockSpec((tb, LANE), lambda p, t: (t, 0))],
        out_specs=[pl.BlockSpec((2, tb, LANE), lambda p, t: (p, t, 0)),
                   pl.BlockSpec((2, ncb, LANE, LANE), lambda p, t: (p, t, 0, 0))],
        out_shape=[jax.ShapeDtypeStruct((N_HEADS, t_len, LANE), F32),
                   jax.ShapeDtypeStruct((N_HEADS, nc, LANE, LANE), F32)],
        scratch_shapes=[pltpu.VMEM((2, LANE, LANE), F32)],
        compiler_params=_params(("arbitrary", "arbitrary")),
    )(pb, pb, pb, cos_t, sin_t)


def retention_backward(pb, cos_t, sin_t, do, rsave, dpb, t_len):
    tb = _time_block(t_len)
    ncb, ntb = tb // CHUNK, t_len // tb

    def body(rq_ref, rk_ref, rv_ref, cos_ref, sin_ref, do_ref, rsave_ref, dpb_in,
             dpb_ref, dq_buf, dk_buf, dv_buf, dst_ref, sems):
        del dpb_in
        p, t = pl.program_id(0), pl.program_id(1)
        step = p * ntb + t
        rows = pl.ds(pl.multiple_of((ntb - 1 - t) * tb, tb), tb)

        def out_copies(sl):
            return [pltpu.make_async_copy(dq_buf.at[sl], dpb_ref.at[CB_RQ + p, rows], sems.at[sl, 0]),
                    pltpu.make_async_copy(dk_buf.at[sl], dpb_ref.at[CB_RK + p, rows], sems.at[sl, 1]),
                    pltpu.make_async_copy(dv_buf.at[sl], dpb_ref.at[pl.ds(CB_RV + 2 * p, 2), rows], sems.at[sl, 2])]

        slot = _writeback_reserve(step, out_copies)

        @pl.when(t == 0)
        def _():
            dst_ref[...] = jnp.zeros_like(dst_ref)

        consts = [_ret_head_consts(2 * p + hd) for hd in range(2)]

        def chunk(i, carry):
            ci = ncb - 1 - i
            r = pl.ds(pl.multiple_of(ci * CHUNK, CHUNK), CHUNK)
            cs, sn = cos_ref[r, :], sin_ref[r, :]
            q = rq_ref[r, :]
            k = rk_ref[r, :]
            q = q * cs + _swap_halves(q) * sn
            k = (k * cs + _swap_halves(k) * sn) * RET_DK ** -0.5
            dq, dk = None, None
            for hd in range(2):
                dm, zeta, xi, cdec, dm_t = consts[hd]
                lm = _lane_mask(hd)
                qh, kh = q * lm, k * lm
                v = rv_ref[hd, r, :]
                g = do_ref[hd, r, :]
                st0 = rsave_ref[hd, ci]
                dst = dst_ref[hd]
                scores_t = _dot(kh, qh, _NT) * dm_t
                dsc = _dot(g, v, _NT) * dm
                dsc_t = _dot(v, g, _NT) * dm_t
                dqh = _dot(dsc, kh) + _dot(g, st0) * xi
                dkh = _dot(dsc_t, qh) + _dot(v, dst) * zeta
                dv_buf[slot, hd, r, :] = (_dot(scores_t, g) + _dot(kh * zeta, dst, _NT)).astype(dv_buf.dtype)
                dst_ref[hd] = dst * cdec + _dot(g, qh * xi, _TN)
                dq = dqh if dq is None else dq + dqh
                dk = dkh if dk is None else dk + dkh
            dk = dk * (RET_DK ** -0.5)
            dq_buf[slot, r, :] = (dq * cs - _swap_halves(dq) * sn).astype(dq_buf.dtype)
            dk_buf[slot, r, :] = (dk * cs - _swap_halves(dk) * sn).astype(dk_buf.dtype)
            return carry

        lax.fori_loop(0, ncb, chunk, 0, unroll=RET_UNROLL)
        _writeback_commit(step, (N_HEADS // 2) * ntb, slot, out_copies)

    any_spec = pl.BlockSpec(memory_space=pl.ANY)
    return pl.pallas_call(
        body, name="ret_bwd", grid=(N_HEADS // 2, ntb),
        in_specs=[pl.BlockSpec((None, tb, LANE), lambda p, t: (CB_RQ + p, ntb - 1 - t, 0)),
                  pl.BlockSpec((None, tb, LANE), lambda p, t: (CB_RK + p, ntb - 1 - t, 0)),
                  pl.BlockSpec((2, tb, LANE), lambda p, t: (CB_RV // 2 + p, ntb - 1 - t, 0)),
                  pl.BlockSpec((tb, LANE), lambda p, t: (ntb - 1 - t, 0)),
                  pl.BlockSpec((tb, LANE), lambda p, t: (ntb - 1 - t, 0)),
                  pl.BlockSpec((2, tb, LANE), lambda p, t: (p, ntb - 1 - t, 0)),
                  pl.BlockSpec((2, ncb, LANE, LANE), lambda p, t: (p, ntb - 1 - t, 0, 0)),
                  any_spec],
        out_specs=any_spec,
        out_shape=jax.ShapeDtypeStruct(dpb.shape, dpb.dtype),
        scratch_shapes=[pltpu.VMEM((2, tb, LANE), dpb.dtype), pltpu.VMEM((2, tb, LANE), dpb.dtype),
                        pltpu.VMEM((2, 2, tb, LANE), dpb.dtype), pltpu.VMEM((2, LANE, LANE), F32),
                        pltpu.SemaphoreType.DMA((2, 3))],
        input_output_aliases={7: 0},
        compiler_params=_params(("arbitrary", "arbitrary")),
    )(pb, pb, pb, cos_t, sin_t, do, rsave, dpb)


def _row_tile(t_len, want):
    return min(want, t_len)


PAIR_CB = 2 * CB_PER_SHARD


def proj_forward(h, wt, t_len):
    tm = _row_tile(t_len, 1024)

    def body(h_ref, w_ref, o_ref):
        acc = _dot(h_ref[...], w_ref[...], _NT)
        for jj in range(PAIR_CB):
            o_ref[jj] = acc[:, jj * LANE:(jj + 1) * LANE]

    return pl.pallas_call(
        body, name="proj_fwd", grid=(N_DEV // 2, t_len // tm),
        in_specs=[pl.BlockSpec((tm, D_MODEL), lambda j, i: (i, 0)),
                  pl.BlockSpec((PAIR_CB * LANE, D_MODEL), lambda j, i: (j, 0))],
        out_specs=pl.BlockSpec((PAIR_CB, tm, LANE), lambda j, i: (j, i, 0)),
        out_shape=jax.ShapeDtypeStruct((N_CB, t_len, LANE), F32),
        compiler_params=_params(("arbitrary", "arbitrary")),
    )(h, wt)


def proj_backward_input(dpb, wt, token, x, dy, norm_g, scale1p, t_len):
    tm = _row_tile(t_len, 256)

    def body(a_ref, wt_hbm, token_ref, x_ref, dy_ref, g_ref, sc_ref, gx_ref, vec_ref, w_ref, sem):
        del token_ref
        i = pl.program_id(0)

        @pl.when(i == 0)
        def _():
            cp = pltpu.make_async_copy(wt_hbm, w_ref, sem)
            cp.start()
            cp.wait()

        a = jnp.concatenate([a_ref[jj].astype(_BF) for jj in range(N_CB)], axis=1)
        dhv = _dot(a, w_ref[...])
        xv, g, sc = x_ref[...], g_ref[...], sc_ref[...]
        r = lax.rsqrt(jnp.mean(xv * xv, axis=-1, keepdims=True) + EPS)
        xn = xv * r
        dxn = dhv * (g * sc)
        gx_ref[...] = dy_ref[...] + r * dxn - xn * (r * r) * jnp.mean(xv * dxn, axis=-1, keepdims=True)
        t = dhv * xn
        _acc_rows(vec_ref, i, [jnp.sum(t * sc, axis=0, keepdims=True),
                               jnp.sum(t * g, axis=0, keepdims=True),
                               jnp.sum(dhv, axis=0, keepdims=True)])

    row = pl.BlockSpec((tm, D_MODEL), lambda i: (i, 0))
    return pl.pallas_call(
        body, name="proj_bwd_input", grid=(t_len // tm,),
        in_specs=[pl.BlockSpec((N_CB, tm, LANE), lambda i: (0, i, 0)),
                  pl.BlockSpec(memory_space=pl.ANY),
                  pl.BlockSpec(token.shape, lambda i: (0, 0)),
                  row, row, _vec_spec(), _vec_spec()],
        out_specs=[row, pl.BlockSpec((8, D_MODEL), lambda i: (0, 0))],
        out_shape=[jax.ShapeDtypeStruct((t_len, D_MODEL), F32), jax.ShapeDtypeStruct((8, D_MODEL), F32)],
        scratch_shapes=[pltpu.VMEM(wt.shape, wt.dtype), pltpu.SemaphoreType.DMA],
        compiler_params=_params(("arbitrary",)),
    )(dpb, wt, token, x, dy, norm_g, scale1p)


def proj_backward_weight(h_t, dpb, t_len):
    tk = _row_tile(t_len, 2048)

    def body(h_ref, b_ref, o_ref):
        k = pl.program_id(1)
        b = jnp.concatenate([b_ref[jj].astype(_BF) for jj in range(PAIR_CB)], axis=1)
        part = _dot(h_ref[...], b)

        @pl.when(k == 0)
        def _():
            for s in range(2):
                o_ref[s] = part[:, s * SHARD_IN:(s + 1) * SHARD_IN]

        @pl.when(k > 0)
        def _():
            for s in range(2):
                o_ref[s] = o_ref[s] + part[:, s * SHARD_IN:(s + 1) * SHARD_IN]

    return pl.pallas_call(
        body, name="proj_bwd_weight", grid=(N_DEV // 2, t_len // tk),
        in_specs=[pl.BlockSpec((D_MODEL, tk), lambda j, k: (0, k)),
                  pl.BlockSpec((PAIR_CB, tk, LANE), lambda j, k: (j, k, 0))],
        out_specs=pl.BlockSpec((2, D_MODEL, SHARD_IN), lambda j, k: (j, 0, 0)),
        out_shape=jax.ShapeDtypeStruct((N_DEV, D_MODEL, SHARD_IN), F32),
        compiler_params=_params(("arbitrary", "arbitrary")),
    )(h_t, dpb)


def sibling_blocks(g_in, c_idx):
    tr = 256

    def body(c_ref, g_ref, o_ref):
        del c_ref
        o_ref[...] = g_ref[...].astype(o_ref.dtype)

    return pl.pallas_call(
        body, name="sibling_blocks",
        grid_spec=pltpu.PrefetchScalarGridSpec(
            num_scalar_prefetch=1, grid=(N_DEV // 2, D_MODEL // tr),
            in_specs=[pl.BlockSpec((None, tr, SHARD_IN), lambda q, i, c: (2 * q + 1 - c[0], i, 0))],
            out_specs=pl.BlockSpec((None, tr, SHARD_IN), lambda q, i, c: (q, i, 0))),
        out_shape=jax.ShapeDtypeStruct((N_DEV // 2, D_MODEL, SHARD_IN), _BF),
        compiler_params=_params(("arbitrary", "arbitrary")),
    )(c_idx, g_in)


def _vec_spec():
    return pl.BlockSpec((1, D_MODEL), lambda i: (0, 0))


def _acc_rows(ref, i, rows):
    @pl.when(i == 0)
    def _():
        ref[...] = jnp.zeros_like(ref)

    for n, row in enumerate(rows):
        ref[n:n + 1, :] = ref[n:n + 1, :] + row


def adaln_forward(x, norm_g, scale1p, shift, t_len):
    tm = _row_tile(t_len, 512)

    def body(x_ref, g_ref, sc_ref, sh_ref, h_ref, ht_ref):
        xv = x_ref[...]
        r = lax.rsqrt(jnp.mean(xv * xv, axis=-1, keepdims=True) + EPS)
        h = xv * r * g_ref[...] * sc_ref[...] + sh_ref[...]
        h_ref[...] = h.astype(h_ref.dtype)
        ht_ref[...] = h.T.astype(ht_ref.dtype)

    return pl.pallas_call(
        body, name="adaln_fwd", grid=(t_len // tm,),
        in_specs=[pl.BlockSpec((tm, D_MODEL), lambda i: (i, 0)), _vec_spec(), _vec_spec(), _vec_spec()],
        out_specs=[pl.BlockSpec((tm, D_MODEL), lambda i: (i, 0)), pl.BlockSpec((D_MODEL, tm), lambda i: (0, i))],
        out_shape=[jax.ShapeDtypeStruct((t_len, D_MODEL), _BF), jax.ShapeDtypeStruct((D_MODEL, t_len), _BF)],
        compiler_params=_params(("arbitrary",)),
    )(x, norm_g, scale1p, shift)


def _head_norm(o, g):
    r = lax.rsqrt(jnp.mean(o * o, axis=-1, keepdims=True) + EPS)
    return r, o * r * g


def _group_spec(tm, cb):
    return pl.BlockSpec((N_HEADS, tm, LANE), lambda i, cb=cb: (cb // N_HEADS, i, 0))


MID_FINAL_G, MID_GATE, MID_LOSS, MID_HG_G, MID_RET_G = range(5)


def middle(x, target, oa, ob, pb, wout, gate, final_g, hg_g, ret_g, t_len):
    tm = _row_tile(t_len, 256)
    n_steps = t_len // tm

    def body(x_ref, t_ref, oa_ref, ob_ref, hz_ref, rz_ref, ga_ref, gb_ref, w_ref, gate_ref, fg_ref, hg_ref, rg_ref,
             dy_ref, doa_ref, dob_ref, dw_ref, vec_ref, dpb_ref, m_scr, dm_scr, keep, bufs, sems):
        i = pl.program_id(0)
        rows = pl.ds(pl.multiple_of(i * tm, tm), tm)

        def group_copies(sl):
            return [pltpu.make_async_copy(bufs.at[sl, n], dpb_ref.at[pl.ds(cb, N_HEADS), rows], sems.at[sl, n])
                    for n, cb in enumerate((CB_HZ, CB_RZ, CB_GA, CB_GB))]
        sides = ((oa_ref, hz_ref, ga_ref, hg_ref, doa_ref), (ob_ref, rz_ref, gb_ref, rg_ref, dob_ref))
        for hh in range(N_HEADS):
            ls = slice(hh * LANE, (hh + 1) * LANE)
            acc = None
            for side, (o_ref, z_ref, gt_ref, g_ref, _) in enumerate(sides):
                o = o_ref[hh]
                rr = lax.rsqrt(jnp.mean(o * o, axis=-1, keepdims=True) + EPS)
                orr = o * rr
                zz = z_ref[hh]
                sz = _sigmoid(zz)
                sgt = _sigmoid(gt_ref[hh])
                keep[side, hh, 0] = orr
                keep[side, hh, 1] = sz
                keep[side, hh, 2] = sgt
                keep[side, hh, 3] = jnp.broadcast_to(rr, orr.shape)
                u = sgt * ((orr * g_ref[:, ls]) * (zz * sz))
                acc = u if acc is None else acc + u
            m_scr[:, ls] = acc.astype(m_scr.dtype)
        zv = _dot(m_scr[...], w_ref[...])
        gt, fg = gate_ref[...], fg_ref[...]
        y = x_ref[...] + gt * zv
        r = lax.rsqrt(jnp.mean(y * y, axis=-1, keepdims=True) + EPS)
        yn = y * r
        err = yn * fg - t_ref[...]
        loss = 0.5 * jnp.sum(jnp.mean(err * err, axis=-1, keepdims=True), axis=0, keepdims=True)
        dout = err * (1.0 / D_MODEL)
        gd = dout * fg
        dy = r * gd - yn * (r * r) * jnp.mean(y * gd, axis=-1, keepdims=True)
        dy_ref[...] = dy
        dz = (dy * gt).astype(_BF)
        dm_scr[...] = _dot(dz, w_ref[...], _NT)
        part = _dot(m_scr[...], dz, _TN)

        @pl.when(i == 0)
        def _():
            dw_ref[...] = part

        @pl.when(i > 0)
        def _():
            dw_ref[...] = dw_ref[...] + part

        slot = _writeback_reserve(i, group_copies)
        dg = [[], []]
        for hh in range(N_HEADS):
            ls = slice(hh * LANE, (hh + 1) * LANE)
            dmh = dm_scr[:, ls]
            for side, (o_ref, z_ref, gt_ref, g_ref, do_ref) in enumerate(sides):
                zz, g = z_ref[hh], g_ref[:, ls]
                orr, sz, sgt, rr = keep[side, hh, 0], keep[side, hh, 1], keep[side, hh, 2], keep[side, hh, 3]
                n = orr * g
                silu = zz * sz
                du = dmh * sgt
                bufs[slot, 2 + side, hh] = (dmh * (n * silu) * (sgt * (1.0 - sgt))).astype(bufs.dtype)
                bufs[slot, side, hh] = (du * n * (sz * (1.0 + zz * (1.0 - sz)))).astype(bufs.dtype)
                dn = du * silu
                dg[side].append(jnp.sum(dn * orr, axis=0, keepdims=True))
                gdn = dn * g
                do_ref[hh] = (rr * (gdn - orr * jnp.mean(orr * gdn, axis=-1, keepdims=True))).astype(do_ref.dtype)
        _acc_rows(vec_ref, i, [jnp.sum(dout * yn, axis=0, keepdims=True),
                               jnp.sum(dy * zv, axis=0, keepdims=True),
                               jnp.broadcast_to(loss, (1, D_MODEL)),
                               jnp.concatenate(dg[0], axis=1), jnp.concatenate(dg[1], axis=1)])
        _writeback_commit(i, n_steps, slot, group_copies)

    row = pl.BlockSpec((tm, D_MODEL), lambda i: (i, 0))
    head = pl.BlockSpec((N_HEADS, tm, LANE), lambda i: (0, i, 0))
    full = pl.BlockSpec((D_MODEL, D_MODEL), lambda i: (0, 0))
    return pl.pallas_call(
        body, name="middle", grid=(n_steps,),
        in_specs=[row, row, head, head, _group_spec(tm, CB_HZ), _group_spec(tm, CB_RZ), _group_spec(tm, CB_GA),
                  _group_spec(tm, CB_GB), full, _vec_spec(), _vec_spec(), _vec_spec(), _vec_spec()],
        out_specs=[row, head, head, full, pl.BlockSpec((8, D_MODEL), lambda i: (0, 0)),
                   pl.BlockSpec(memory_space=pl.ANY)],
        out_shape=[jax.ShapeDtypeStruct((t_len, D_MODEL), F32),
                   jax.ShapeDtypeStruct((N_HEADS, t_len, LANE), _BF),
                   jax.ShapeDtypeStruct((N_HEADS, t_len, LANE), _BF),
                   jax.ShapeDtypeStruct((D_MODEL, D_MODEL), F32),
                   jax.ShapeDtypeStruct((8, D_MODEL), F32),
                   jax.ShapeDtypeStruct((N_CB, t_len, LANE), _BF)],
        scratch_shapes=[pltpu.VMEM((tm, D_MODEL), _BF), pltpu.VMEM((tm, D_MODEL), F32),
                        pltpu.VMEM((2, N_HEADS, 4, tm, LANE), F32),
                        pltpu.VMEM((2, 4, N_HEADS, tm, LANE), _BF), pltpu.SemaphoreType.DMA((2, 4))],
        compiler_params=_params(("arbitrary",)),
    )(x, target, oa, ob, pb, pb, pb, pb, wout, gate, final_g, hg_g, ret_g)


def device_step(x, target, mod, lb, wt, wout, norm_g, hg_g, ret_g, final_g, c_idx=None, start_exchange=None):
    t_len = x.shape[0]
    shift, scale, gate = mod[:, :D_MODEL], mod[:, D_MODEL:2 * D_MODEL], mod[:, 2 * D_MODEL:]
    scale1p = 1.0 + scale
    cos_t, sin_t = _rope_tables(t_len)
    h, h_t = adaln_forward(x, norm_g, scale1p, shift, t_len)
    pb = proj_forward(h, wt, t_len)
    oa, ssave, asave = hgrn_forward(pb, lb, t_len)
    ob, rsave = retention_forward(pb, cos_t, sin_t, t_len)
    dy, doa, dob, dwout, vec_mid, dpb = middle(x, target, oa, ob, pb, wout, gate, final_g, hg_g, ret_g, t_len)
    dpb, dlb = hgrn_backward(pb, lb, doa, ssave, asave, dpb, t_len)
    dpb = retention_backward(pb, cos_t, sin_t, dob, rsave, dpb, t_len)
    c_idx = jnp.zeros((1,), jnp.int32) if c_idx is None else c_idx
    dwin = proj_backward_weight(h_t, dpb, t_len)
    dwin_sib = sibling_blocks(dwin, c_idx)
    token, pending = (start_exchange(dwin, dwin_sib, dwout) if start_exchange
                      else (jnp.zeros((8, LANE), F32), None))
    grad_x, vec_ada = proj_backward_input(dpb, wt, token, x, dy, norm_g, scale1p, t_len)
    return grad_x, dwin, dwout, vec_mid, vec_ada, dlb, pending


PACK_ROWS = 16
ROW_NORM_G, ROW_LB, ROW_HG_G, ROW_RET_G, ROW_FINAL_G, ROW_SHIFT, ROW_SCALE, ROW_GATE, ROW_LOSS = range(9)


def _mesh_pos():
    return lax.axis_index("x"), lax.axis_index("y"), lax.axis_index("c")


def _lin(pos):
    return 4 * pos[0] + 2 * pos[1] + pos[2]


def _xor_peer(pos, k):
    return tuple(1 - p if (k >> s) & 1 else p for p, s in zip(pos, (2, 1, 0)))


def _other_chips(pos):
    x, y, _ = pos
    return [(1 - x, y), (x, 1 - y), (1 - x, 1 - y)]


def _remote(src, dst, send_sem, recv_sem, to):
    return pltpu.make_async_remote_copy(src_ref=src, dst_ref=dst, send_sem=send_sem, recv_sem=recv_sem,
                                        device_id=to, device_id_type=MESH)


def pre_exchange(c, w_ada, b_ada, logits):
    def body(c_ref, wada_ref, bada_ref, logit_ref, mod_ref, scall_ref, lb_ref,
             cg_ref, modall_ref, parts_ref, send1, recv1, send2, recv2):
        pos = _mesh_pos()
        cv = c_ref[...]
        slot = lambda p: pl.ds(pl.multiple_of(8 * _lin(p), 8), 8)
        cg_ref[slot(pos), :] = jnp.broadcast_to(cv * _sigmoid(cv), (8, D_MODEL))
        lb_ref[...] = _sigmoid(logit_ref[0:1, :] - logit_ref[1:2, :])
        peers = [_xor_peer(pos, k) for k in range(1, N_DEV)]
        gather = [_remote(cg_ref.at[slot(pos)], cg_ref.at[slot(pos)], send1.at[n], recv1.at[n], p)
                  for n, p in enumerate(peers)]
        for cp in gather:
            cp.start()
        for n, p in enumerate(peers):
            _remote(cg_ref.at[slot(p)], cg_ref.at[slot(p)], send1.at[n], recv1.at[n], p).wait_recv()
        modall_ref[...] = _dot(cg_ref[...], wada_ref[...])
        scatter = [_remote(modall_ref.at[slot(p)], parts_ref.at[slot(pos)], send2.at[n], recv2.at[n], p)
                   for n, p in enumerate(peers)]
        for cp in scatter:
            cp.start()
        parts_ref[slot(pos), :] = modall_ref[slot(pos), :]
        for n, p in enumerate(peers):
            _remote(modall_ref.at[slot(p)], parts_ref.at[slot(p)], send2.at[n], recv2.at[n], p).wait_recv()
        for cp in gather + scatter:
            cp.wait_send()
        for j in range(N_DEV):
            cols = slice(j * SHARD_ADA, (j + 1) * SHARD_ADA)
            mod_ref[:, cols] = parts_ref[8 * j:8 * j + 1, :] + bada_ref[:, cols]
            scall_ref[j:j + 1, :] = cg_ref[8 * j:8 * j + 1, :]

    vmem = pl.BlockSpec(memory_space=pltpu.VMEM)
    return pl.pallas_call(
        body, name="pre_exchange",
        in_specs=[vmem] * 4, out_specs=[vmem] * 3,
        out_shape=[jax.ShapeDtypeStruct((1, 3 * D_MODEL), F32), jax.ShapeDtypeStruct((N_DEV, D_MODEL), F32),
                   jax.ShapeDtypeStruct((1, D_MODEL), F32)],
        scratch_shapes=[pltpu.VMEM((N_DEV * 8, D_MODEL), F32), pltpu.VMEM((N_DEV * 8, SHARD_ADA), F32),
                        pltpu.VMEM((N_DEV * 8, SHARD_ADA), F32)] + [pltpu.SemaphoreType.DMA((N_DEV - 1,))] * 4,
        compiler_params=pltpu.CompilerParams(vmem_limit_bytes=VMEM_LIMIT),
    )(c, w_ada, b_ada, logits)


def weight_gather(win_sh, wout_sh):
    def body(win_ref, wout_ref, wg_ref, woutg_ref, send, recv, local):
        pos = _mesh_pos()
        x, y, c = pos
        sibling = (x, y, 1 - c)

        def route(core):
            return [(x + (1 - core) * (1 - 2 * x), y + core * (1 - 2 * y)),
                    (x + core * (1 - 2 * x), y + (1 - core) * (1 - 2 * y)),
                    (1 - x, 1 - y)]

        chips, sib_chips = route(c), route(1 - c)
        mine, first, later = [], [], []
        for a, (src, out) in enumerate(((win_ref, wg_ref), (wout_ref, woutg_ref))):
            def copy(k, block, to, src_ref=None, a=a, out=out):
                dst = out.at[_lin(block)]
                return _remote(dst if src_ref is None else src_ref, dst, send.at[7 * a + k], recv.at[7 * a + k], to)
            mine.append(pltpu.make_async_copy(src, out.at[_lin(pos)], local.at[a]))
            first += [copy(0, pos, sibling, src), copy(1, pos, (*chips[0], c), src), copy(2, pos, (*chips[1], c), src)]
            later.append([[copy(3, (*chips[0], c), (*chips[1], c)), copy(4, (*chips[0], c), sibling)],
                          [copy(5, (*chips[1], c), sibling)],
                          [copy(6, (*chips[2], c), sibling)]])
        for cp in mine + first:
            cp.start()
        for j in range(3):
            for a, out in enumerate((wg_ref, woutg_ref)):
                dst = out.at[_lin((*chips[j], c))]
                _remote(dst, dst, send.at[7 * a + 1 + j], recv.at[7 * a + 1 + j], pos).wait_recv()
                for cp in later[a][j]:
                    cp.start()
        for a, out in enumerate((wg_ref, woutg_ref)):
            dst = out.at[_lin(sibling)]
            _remote(dst, dst, send.at[7 * a], recv.at[7 * a], pos).wait_recv()
            for j in range(3):
                dst = out.at[_lin((*sib_chips[j], 1 - c))]
                _remote(dst, dst, send.at[7 * a + 4 + j], recv.at[7 * a + 4 + j], pos).wait_recv()
        for cp in first + [cp for per_array in later for group in per_array for cp in group]:
            cp.wait_send()
        for cp in mine:
            cp.wait()

    any_spec = pl.BlockSpec(memory_space=pl.ANY)
    return pl.pallas_call(
        body, name="weight_gather",
        in_specs=[any_spec, any_spec], out_specs=[any_spec, any_spec],
        out_shape=[jax.ShapeDtypeStruct((N_DEV,) + win_sh.shape, win_sh.dtype),
                   jax.ShapeDtypeStruct((N_DEV,) + wout_sh.shape, wout_sh.dtype)],
        scratch_shapes=[pltpu.SemaphoreType.DMA((14,)), pltpu.SemaphoreType.DMA((14,)),
                        pltpu.SemaphoreType.DMA((2,))],
    )(win_sh, wout_sh)


def grad_pair_exchange(g_sib, g_out):
    def body(gsib_ref, gout_ref, ra_ref, rb_ref, send, recv):
        pos = _mesh_pos()
        x, y, c = pos
        sibling = (x, y, 1 - c)
        copies = []
        for q in range(4):
            copies.append(_remote(gsib_ref.at[q], ra_ref.at[q], send.at[q], recv.at[q], sibling))
            copies.append(_remote(gout_ref.at[2 * q + (1 - c)], rb_ref.at[q], send.at[4 + q], recv.at[4 + q], sibling))
        for cp in copies:
            cp.start()
        for cp in copies:
            cp.wait_recv()
        for cp in copies:
            cp.wait_send()

    any_spec = pl.BlockSpec(memory_space=pl.ANY)
    return pl.pallas_call(
        body, name="grad_pair_exchange",
        in_specs=[any_spec, any_spec], out_specs=[any_spec, any_spec],
        out_shape=[jax.ShapeDtypeStruct(g_sib.shape, g_sib.dtype), jax.ShapeDtypeStruct((4,) + g_out.shape[1:], F32)],
        scratch_shapes=[pltpu.SemaphoreType.DMA((8,)), pltpu.SemaphoreType.DMA((8,))],
    )(g_sib, g_out)


def pair_sum(g_in, ra, g_out, rb, c_idx):
    tr = 256

    def body(c_ref, gin_ref, ra_ref, gout_ref, rb_ref, sb_ref, sbo_ref):
        del c_ref
        sb_ref[...] = (gin_ref[...] + ra_ref[...].astype(F32)).astype(sb_ref.dtype)
        sbo_ref[...] = gout_ref[...] + rb_ref[...]

    n_i = D_MODEL // tr
    return pl.pallas_call(
        body, name="pair_sum",
        grid_spec=pltpu.PrefetchScalarGridSpec(
            num_scalar_prefetch=1, grid=(4, n_i),
            in_specs=[pl.BlockSpec((None, tr, SHARD_IN), lambda q, i, c: (2 * q + c[0], i, 0)),
                      pl.BlockSpec((None, tr, SHARD_IN), lambda q, i, c: (q, i, 0)),
                      pl.BlockSpec((None, SHARD_OUT // n_i, D_MODEL), lambda q, i, c: (2 * q + c[0], i, 0)),
                      pl.BlockSpec((None, SHARD_OUT // n_i, D_MODEL), lambda q, i, c: (q, i, 0))],
            out_specs=[pl.BlockSpec((None, tr, SHARD_IN), lambda q, i, c: (q, i, 0)),
                       pl.BlockSpec((None, SHARD_OUT // n_i, D_MODEL), lambda q, i, c: (q, i, 0))]),
        out_shape=[jax.ShapeDtypeStruct(ra.shape, _BF), jax.ShapeDtypeStruct(rb.shape, F32)],
        compiler_params=_params(("arbitrary", "arbitrary")),
    )(c_idx, g_in, ra, g_out, rb)


_HBM = pl.BlockSpec(memory_space=pltpu.HBM)
_SEM = pl.BlockSpec(memory_space=pltpu.SEMAPHORE)
_N_CHIP_COPIES = 6


def _chip_copies(sb_ref, sbo_ref, rc_ref, rco_ref, send, recv):
    pos = _mesh_pos()
    copies = []
    for a, (src, dst) in enumerate(((sb_ref, rc_ref), (sbo_ref, rco_ref))):
        for j, chip in enumerate(_other_chips(pos)):
            copies.append(_remote(src.at[2 * chip[0] + chip[1]], dst.at[j], send.at[3 * a + j], recv.at[3 * a + j],
                                  (*chip, pos[2])))
    return copies


def grad_chip_start(sb, sbo):
    def body(sb_ref, sbo_ref, rc_ref, rco_ref, send, recv, sb_thru, sbo_thru, rc_thru, rco_thru, token):
        del sb_thru, sbo_thru, rc_thru, rco_thru
        for cp in _chip_copies(sb_ref, sbo_ref, rc_ref, rco_ref, send, recv):
            cp.start()
        token[...] = jnp.zeros_like(token)

    hbm = lambda a: pltpu.with_memory_space_constraint(a, pltpu.HBM)
    rc = lax.empty((3,) + sb.shape[1:], sb.dtype)
    rco = lax.empty((3,) + sbo.shape[1:], sbo.dtype)
    return pl.pallas_call(
        body, name="grad_chip_start",
        in_specs=[_HBM] * 4,
        out_specs=[_SEM, _SEM, _HBM, _HBM, _HBM, _HBM, pl.BlockSpec(memory_space=pltpu.VMEM)],
        out_shape=[pltpu.SemaphoreType.DMA((_N_CHIP_COPIES,)), pltpu.SemaphoreType.DMA((_N_CHIP_COPIES,)),
                   pltpu.HBM(sb.shape, sb.dtype), pltpu.HBM(sbo.shape, sbo.dtype),
                   pltpu.HBM(rc.shape, rc.dtype), pltpu.HBM(rco.shape, rco.dtype),
                   jax.ShapeDtypeStruct((8, LANE), F32)],
        input_output_aliases={0: 2, 1: 3, 2: 4, 3: 5},
        compiler_params=pltpu.CompilerParams(has_side_effects=pltpu.SideEffectType.DATAFLOW_SIDE_EFFECTING),
    )(hbm(sb), hbm(sbo), hbm(rc), hbm(rco))


def grad_chip_wait(send, recv, sb, sbo, rc, rco, after):
    def body(sb_ref, sbo_ref, rc_ref, rco_ref, send, recv, after_ref, sb_o, sbo_o, rc_o, rco_o):
        del after_ref, sb_o, sbo_o, rc_o, rco_o
        for cp in _chip_copies(sb_ref, sbo_ref, rc_ref, rco_ref, send, recv):
            cp.wait_send()
            cp.wait_recv()

    return pl.pallas_call(
        body, name="grad_chip_wait",
        in_specs=[_HBM] * 4 + [_SEM, _SEM, pl.BlockSpec(memory_space=pl.ANY)],
        out_specs=[_HBM] * 4,
        out_shape=[pltpu.HBM(sb.shape, sb.dtype), pltpu.HBM(sbo.shape, sbo.dtype),
                   pltpu.HBM(rc.shape, rc.dtype), pltpu.HBM(rco.shape, rco.dtype)],
        input_output_aliases={0: 0, 1: 1, 2: 2, 3: 3},
        compiler_params=pltpu.CompilerParams(has_side_effects=pltpu.SideEffectType.DATAFLOW_SIDE_EFFECTING),
    )(sb, sbo, rc, rco, send, recv, after)


def pack_gather(pack):
    def body(pack_ref, packs_ref, psend, precv):
        pos = _mesh_pos()
        me = _lin(pos)
        packs_ref[me] = pack_ref[...]
        peers = [_xor_peer(pos, k) for k in range(1, N_DEV)]
        gather = [_remote(packs_ref.at[me], packs_ref.at[me], psend.at[n], precv.at[n], p) for n, p in enumerate(peers)]
        for cp in gather:
            cp.start()
        for n, p in enumerate(peers):
            _remote(packs_ref.at[_lin(p)], packs_ref.at[_lin(p)], psend.at[n], precv.at[n], p).wait_recv()
        for cp in gather:
            cp.wait_send()

    vmem = pl.BlockSpec(memory_space=pltpu.VMEM)
    return pl.pallas_call(
        body, name="pack_gather", in_specs=[vmem], out_specs=vmem,
        out_shape=jax.ShapeDtypeStruct((N_DEV,) + pack.shape, F32),
        scratch_shapes=[pltpu.SemaphoreType.DMA((N_DEV - 1,)), pltpu.SemaphoreType.DMA((N_DEV - 1,))],
    )(pack)


def pack_rows(vec_mid, vec_ada, dlb):
    def body(mid_ref, ada_ref, dlb_ref, o_ref):
        mid = lambda r: mid_ref[r:r + 1, :]
        rows = [ada_ref[0:1, :], dlb_ref[...], mid(MID_HG_G), mid(MID_RET_G), mid(MID_FINAL_G),
                ada_ref[2:3, :], ada_ref[1:2, :], mid(MID_GATE), mid(MID_LOSS)]
        o_ref[...] = jnp.zeros_like(o_ref)
        for n, row in enumerate(rows):
            o_ref[n:n + 1, :] = row

    vmem = pl.BlockSpec(memory_space=pltpu.VMEM)
    return pl.pallas_call(body, name="pack_rows", in_specs=[vmem] * 3, out_specs=vmem,
                          out_shape=jax.ShapeDtypeStruct((PACK_ROWS, D_MODEL), F32))(vec_mid, vec_ada, dlb)


def _adamw(w, g, m, v):
    m = ADAM_B1 * m + (1.0 - ADAM_B1) * g
    v = ADAM_B2 * v + (1.0 - ADAM_B2) * (g * g)
    m_hat = m / (1.0 - ADAM_B1 ** ADAM_STEP)
    v_hat = v / (1.0 - ADAM_B2 ** ADAM_STEP)
    delta = -ADAM_LR * (m_hat / (jnp.sqrt(v_hat) + ADAM_EPS) + ADAM_WD * w)
    return delta, m, v


def adam_shard(chip_idx, own, parts, w, m, v, name):
    rows, cols = w.shape
    tr = min(rows, 128)

    def body(chip_ref, p0, p1, p2, p3, w_ref, m_ref, v_ref, g_ref, d_ref, nm_ref, nv_ref):
        del chip_ref
        g = ((p0[...].astype(F32) + p1[...].astype(F32)) + p2[...].astype(F32)) + p3[...].astype(F32)
        g_ref[...] = g
        d_ref[...], nm_ref[...], nv_ref[...] = _adamw(w_ref[...], g, m_ref[...], v_ref[...])

    part = lambda q: pl.BlockSpec((None, tr, cols), lambda i, chip, q=q: (q, i, 0))
    tile = pl.BlockSpec((tr, cols), lambda i, chip: (i, 0))
    return pl.pallas_call(
        body, name=name,
        grid_spec=pltpu.PrefetchScalarGridSpec(
            num_scalar_prefetch=1, grid=(rows // tr,),
            in_specs=[pl.BlockSpec((None, tr, cols), lambda i, chip: (chip[0], i, 0)), part(0), part(1), part(2),
                      tile, tile, tile],
            out_specs=[tile] * 4),
        out_shape=[jax.ShapeDtypeStruct(w.shape, F32)] * 4,
        compiler_params=_params(("arbitrary",)),
    )(chip_idx, own, parts, parts, parts, w, m, v)


def adam_ada(sc_t, dmod_all, me_idx, w, m, v):
    def body(me_ref, sc_ref, dm_ref, w_ref, m_ref, v_ref, g_ref, d_ref, nm_ref, nv_ref):
        del me_ref
        g = _dot_f32(sc_ref[...], dm_ref[...])
        g_ref[...] = g
        d_ref[...], nm_ref[...], nv_ref[...] = _adamw(w_ref[...], g, m_ref[...], v_ref[...])

    full = pl.BlockSpec(w.shape, lambda i, me: (0, 0))
    return pl.pallas_call(
        body, name="adam_ada",
        grid_spec=pltpu.PrefetchScalarGridSpec(
            num_scalar_prefetch=1, grid=(1,),
            in_specs=[pl.BlockSpec(sc_t.shape, lambda i, me: (0, 0)),
                      pl.BlockSpec((LANE, SHARD_ADA), lambda i, me: (0, me[0])), full, full, full],
            out_specs=[full] * 4),
        out_shape=[jax.ShapeDtypeStruct(w.shape, F32)] * 4,
        compiler_params=_params(("arbitrary",)),
    )(me_idx, sc_t, dmod_all, w, m, v)


def adam_vectors(packs, lb, params, ms, vs):
    n = len(params)

    def body(*refs):
        packs_ref, lb_ref = refs[0], refs[1]
        w_refs, m_refs, v_refs = refs[2:2 + n], refs[2 + n:2 + 2 * n], refs[2 + 2 * n:2 + 3 * n]
        loss_ref = refs[2 + 3 * n]
        outs = refs[3 + 3 * n:3 + 7 * n]
        tot_ref = refs[3 + 7 * n]
        tot = packs_ref[0]
        for d in range(1, N_DEV):
            tot = tot + packs_ref[d]
        tot_ref[...] = tot
        row = lambda r: tot_ref[r:r + 1, :]
        lbv = lb_ref[...]
        dl0 = row(ROW_LB) * lbv * (1.0 - lbv)
        grads = [[row(ROW_NORM_G)],
                 [jnp.concatenate([row(ROW_SHIFT), row(ROW_SCALE), row(ROW_GATE)], axis=1)],
                 [dl0, -dl0],
                 [row(ROW_HG_G)], [row(ROW_RET_G)], [row(ROW_FINAL_G)]]
        loss_ref[...] = tot_ref[ROW_LOSS:ROW_LOSS + 1, 0:LANE]
        for j, g_rows in enumerate(grads):
            for r, g in enumerate(g_rows):
                rs = slice(r, r + 1)
                d, nm, nv = _adamw(w_refs[j][rs, :], g, m_refs[j][rs, :], v_refs[j][rs, :])
                outs[4 * j][rs, :] = g
                outs[4 * j + 1][rs, :] = d
                outs[4 * j + 2][rs, :] = nm
                outs[4 * j + 3][rs, :] = nv

    vmem = pl.BlockSpec(memory_space=pltpu.VMEM)
    out_shape = [jax.ShapeDtypeStruct((1, LANE), F32)]
    for w in params:
        out_shape += [jax.ShapeDtypeStruct(w.shape, F32)] * 4
    return pl.pallas_call(
        body, name="adam_vectors", in_specs=[vmem] * (2 + 3 * n), out_specs=[vmem] * len(out_shape),
        out_shape=out_shape, scratch_shapes=[pltpu.VMEM((PACK_ROWS, D_MODEL), F32)],
    )(packs, lb, *params, *ms, *vs)


def kernel(x, c, norm_g, w_ada, b_ada, w_in, hg_lb_logits, hg_norm_g, ret_norm_g, w_out, final_g, loss_target, m_norm_g, m_w_ada, m_b_ada, m_w_in, m_hg_lb_logits, m_hg_norm_g, m_ret_norm_g, m_w_out, m_final_g, v_norm_g, v_w_ada, v_b_ada, v_w_in, v_hg_lb_logits, v_hg_norm_g, v_ret_norm_g, v_w_out, v_final_g):
    pos = _mesh_pos()
    me_idx = jnp.reshape(_lin(pos), (1,)).astype(jnp.int32)
    c_idx = jnp.reshape(pos[2], (1,)).astype(jnp.int32)
    vec = lambda a: a.reshape(1, D_MODEL)

    mod, scall, lb = pre_exchange(c, w_ada[0], b_ada, hg_lb_logits)
    wtg, woutg = weight_gather(w_in[0].T.astype(_BF), w_out[0].astype(_BF))
    chip_idx = jnp.reshape(2 * pos[0] + pos[1], (1,)).astype(jnp.int32)

    def start_exchange(dwin, dwin_sib, dwout):
        dwout = dwout.reshape(N_DEV, SHARD_OUT, D_MODEL)
        ra, rb = grad_pair_exchange(dwin_sib, dwout)
        sb, sbo = pair_sum(dwin, ra, dwout, rb, c_idx)
        send, recv, sb, sbo, rc, rco, token = grad_chip_start(sb, sbo)
        return token, (send, recv, sb, sbo, rc, rco)

    grad_x, _, _, vec_mid, vec_ada, dlb, pending = device_step(
        x[0], loss_target[0], mod, lb, wtg.reshape(D_IN, D_MODEL), woutg.reshape(D_MODEL, D_MODEL), norm_g,
        hg_norm_g, ret_norm_g, vec(final_g), c_idx, start_exchange)
    packs = pack_gather(pack_rows(vec_mid, vec_ada, dlb))
    dmod_all = packs[:, ROW_SHIFT:ROW_GATE + 1, :].reshape(N_DEV, 3 * D_MODEL)
    dmod_all = jnp.pad(dmod_all, ((0, LANE - N_DEV), (0, 0)))
    sc_t = jnp.pad(scall.T, ((0, 0), (0, LANE - N_DEV)))
    g_ada, d_ada, nm_ada, nv_ada = adam_ada(sc_t, dmod_all, me_idx, w_ada[0], m_w_ada[0], v_w_ada[0])
    small = adam_vectors(
        packs, lb,
        (norm_g, b_ada, hg_lb_logits, hg_norm_g, ret_norm_g, vec(final_g)),
        (m_norm_g, m_b_ada, m_hg_lb_logits, m_hg_norm_g, m_ret_norm_g, vec(m_final_g)),
        (v_norm_g, v_b_ada, v_hg_lb_logits, v_hg_norm_g, v_ret_norm_g, vec(v_final_g)))
    loss = small[0][0, 0]
    sb, sbo, rc, rco = grad_chip_wait(*pending, small[0])
    g_in, d_in, nm_in, nv_in = adam_shard(chip_idx, sb, rc, w_in[0], m_w_in[0], v_w_in[0], "adam_w_in")
    g_out, d_out, nm_out, nv_out = adam_shard(chip_idx, sbo, rco, w_out[0], m_w_out[0], v_w_out[0], "adam_w_out")
    (g_ng, d_ng, nm_ng, nv_ng), (g_b, d_b, nm_b, nv_b), (g_lb, d_lb, nm_lb, nv_lb), (g_hg, d_hg, nm_hg, nv_hg), \
        (g_rg, d_rg, nm_rg, nv_rg), (g_fg, d_fg, nm_fg, nv_fg) = [small[1 + 4 * j:5 + 4 * j] for j in range(6)]
    flat = lambda a: a.reshape(D_MODEL)

    def group(ng, ada, b, win, lbl, hg, rg, wo, fg):
        return (ng, ada[None], b, win[None], lbl, hg, rg, wo[None], flat(fg))

    return (loss, grad_x[None],
            *group(g_ng, g_ada, g_b, g_in, g_lb, g_hg, g_rg, g_out, g_fg),
            *group(d_ng, d_ada, d_b, d_in, d_lb, d_hg, d_rg, d_out, d_fg),
            *group(nm_ng, nm_ada, nm_b, nm_in, nm_lb, nm_hg, nm_rg, nm_out, nm_fg),
            *group(nv_ng, nv_ada, nv_b, nv_in, nv_lb, nv_hg, nv_rg, nv_out, nv_fg))
```

```python
import functools

import numpy as np
import jax
import jax.numpy as jnp
from jax import lax
from jax.experimental import pallas as pl
from jax.experimental.pallas import tpu as pltpu

F32 = jnp.float32
_BF = jnp.bfloat16

D_MODEL = 1024
N_HEADS = 8
LANE = 128
RET_DK = 64
D_IN = 9216
N_DEV = 8
SHARD_IN = D_IN // N_DEV
SHARD_ADA = 3 * D_MODEL // N_DEV
SHARD_OUT = D_MODEL // N_DEV
N_CB = D_IN // LANE
CB_PER_SHARD = SHARD_IN // LANE
CHUNK = 128
N_LEVELS = 7
EPS = 1e-6
LOG2_E = float(np.log2(np.e))
ROPE_BASE = 10000.0
CB_HQ, CB_HF, CB_HI, CB_HZ, CB_RQ, CB_RK, CB_RV, CB_RZ, CB_GA, CB_GB = 0, 8, 16, 24, 32, 36, 40, 48, 56, 64
VMEM_LIMIT = 56 * 1024 * 1024

ADAM_LR, ADAM_B1, ADAM_B2, ADAM_EPS, ADAM_WD, ADAM_STEP = 0.001, 0.9, 0.999, 1e-08, 0.01, 10

_NN = (((1,), (0,)), ((), ()))
_NT = (((1,), (1,)), ((), ()))
_TN = (((0,), (0,)), ((), ()))
MESH = pl.DeviceIdType.MESH


def _dot(a, b, dims=_NN):
    return lax.dot_general(a.astype(_BF), b.astype(_BF), dims, preferred_element_type=F32)


def _split2(a):
    hi = a.astype(_BF)
    lo = (a - hi.astype(F32)).astype(_BF)
    return jnp.concatenate([hi, lo], axis=1)


def _dot_sel(sel, a):
    n = a.shape[1]
    r = lax.dot_general(sel.astype(_BF), _split2(a), _NN, preferred_element_type=F32)
    return r[:, :n] + r[:, n:]


def _dot_f32(a, b):
    def pieces(v):
        p1 = v.astype(_BF)
        r1 = v - p1.astype(F32)
        p2 = r1.astype(_BF)
        p3 = (r1 - p2.astype(F32)).astype(_BF)
        return p1, p2, p3
    a1, a2, a3 = pieces(a)
    b1, b2, b3 = pieces(b)
    d = lambda u, v: lax.dot_general(u, v, _NN, preferred_element_type=F32)
    return ((d(a1, b3) + d(a2, b2) + d(a3, b1)) + (d(a1, b2) + d(a2, b1))) + d(a1, b1)


def _sigmoid(v):
    return 1.0 / (1.0 + jnp.exp(-v))


def _params(sem=None):
    return pltpu.CompilerParams(dimension_semantics=sem, vmem_limit_bytes=VMEM_LIMIT)


def _hgrn_consts():
    c, nl = CHUNK, N_LEVELS
    t = np.arange(c)[:, None]
    j = np.arange(c)[None, :]
    sel = [j <= t]
    masks = [j == t]
    for l in range(1, nl + 1):
        m = ((t >> l) << l) + (1 << (l - 1)) - 1
        sec = t > m
        sel.append(np.where(sec, (j > m) & (j <= t), (j > t) & (j <= m)))
        same = (t >> l) == (j >> l)
        masks.append(same & sec & (j <= m))
    sel.append(j > t)
    sel = np.concatenate(sel, 0).astype(np.float32)
    masks = np.stack(masks).astype(np.float32)
    sgn = np.stack([np.where((t & (1 << (l - 1))) != 0, 1.0, -1.0) * np.ones((1, LANE)) for l in range(3, nl + 1)])
    return dict(tri=jnp.asarray(sel[:c], _BF),
                lvl=jnp.asarray(masks, F32),
                sgn=jnp.asarray(sgn, F32),
                sel_t=jnp.asarray(sel.T, _BF),
                lvl_b=jnp.asarray(masks, _BF),
                lvlt_b=jnp.asarray(np.swapaxes(masks, 1, 2), _BF))


def _level_exponents(b, logf, b_scr, sgn_ref):
    c = CHUNK
    b_scr[...] = b
    row = lax.broadcasted_iota(jnp.int32, (c, LANE), 0)
    nxt = pltpu.roll(logf, c - 1, 0)
    prv = pltpu.roll(logf, 1, 0)
    r4 = row & 3
    out = [jnp.where((row & 1) == 1, logf, 0.0),
           jnp.where(r4 == 0, nxt, jnp.where(r4 == 1, 0.0, jnp.where(r4 == 2, logf, logf + prv)))]
    for l in range(3, N_LEVELS + 1):
        size, half = 1 << l, 1 << (l - 1)
        ref = jnp.concatenate([jnp.broadcast_to(b_scr[i * size + half - 1:i * size + half, :], (size, LANE))
                               for i in range(c // size)], axis=0)
        out.append((b - ref) * sgn_ref[l - 3])
    return out


def _hgrn_chunk(hq, hf, hi, lbv, tri_ref, sgn_ref, b_scr):
    sq = _sigmoid(hq)
    q = hq * sq
    sg = _sigmoid(hf)
    omlb = 1.0 - lbv
    f = lbv + omlb * sg
    k = 1.0 - f
    logf = jnp.log(f) * LOG2_E
    b = _dot_sel(tri_ref[...], logf)
    bc = jnp.sum(logf, axis=0, keepdims=True)
    lev = [None] + [jnp.exp2(e) for e in _level_exponents(b, logf, b_scr, sgn_ref)]
    return dict(sq=sq, q=q, sg=sg, omlb=omlb, f=f, k=k, v=hi, eb=jnp.exp2(b), erem=jnp.exp2(bc - b),
                ebc=jnp.exp2(bc), lev=lev)


def _blockdiag(a, b):
    z = jnp.zeros_like(a)
    return jnp.concatenate([jnp.concatenate([a, z], axis=1), jnp.concatenate([z, b], axis=1)], axis=0)


def _level_operands(a):
    q, k = a["q"].astype(_BF), a["k"].astype(_BF)
    lev = [None] + [a["lev"][l].astype(_BF) for l in range(1, N_LEVELS + 1)]
    ql = [q] + [q * lev[l] for l in range(1, N_LEVELS + 1)]
    kl = [k] + [k * lev[l] for l in range(1, N_LEVELS + 1)]
    pairs = range(0, N_LEVELS + 1, 2)
    return ([jnp.concatenate([ql[l], ql[l + 1]], axis=1) for l in pairs], [_blockdiag(kl[l], kl[l + 1]) for l in pairs],
            ql, kl)


def _hgrn_scores(a, lvl_ref, q_pairs, k_diags):
    acc = None
    for n, (qp, kd) in enumerate(zip(q_pairs, k_diags)):
        both = lax.dot_general(qp, kd, _NT, preferred_element_type=F32)
        part = lvl_ref[2 * n] * both[:, :CHUNK] + lvl_ref[2 * n + 1] * both[:, CHUNK:]
        acc = part if acc is None else acc + part
    return acc


SCAN_UNROLL = 4
RET_UNROLL = 8


def _writeback_reserve(step, make_copies):
    slot = step % 2

    @pl.when(step >= 2)
    def _():
        for cp in make_copies(slot):
            cp.wait()

    return slot


def _writeback_commit(step, n_steps, slot, make_copies):
    for cp in make_copies(slot):
        cp.start()

    @pl.when(step == n_steps - 1)
    def _():
        for cp in make_copies(slot):
            cp.wait()
        if n_steps > 1:
            for cp in make_copies(1 - slot):
                cp.wait()


def _resident(const):
    zeros = (0,) * const.ndim
    return pl.BlockSpec(const.shape, lambda p, t: zeros)


def _time_block(t_len):
    return min(t_len, 1024)


def hgrn_forward(pb, lb, t_len):
    nc = t_len // CHUNK
    tb = _time_block(t_len)
    ncb = tb // CHUNK
    consts = _hgrn_consts()
    operands = [consts[n] for n in ("tri", "lvl", "sgn")]

    def body(hq_ref, hf_ref, hi_ref, lb_ref, tri_ref, lvl_ref, sgn_ref, o_ref, ssave_ref, asave_ref, st_ref, b_scr):
        @pl.when(pl.program_id(1) == 0)
        def _():
            st_ref[...] = jnp.zeros_like(st_ref)

        def chunk(ci, carry):
            r = pl.ds(pl.multiple_of(ci * CHUNK, CHUNK), CHUNK)
            for hd in range(2):
                lbv = lb_ref[:, hd * LANE:(hd + 1) * LANE]
                a = _hgrn_chunk(hq_ref[hd, r, :], hf_ref[hd, r, :], hi_ref[hd, r, :], lbv, tri_ref, sgn_ref,
                                b_scr.at[hd])
                q_pairs, k_diags, _, _ = _level_operands(a)
                st = st_ref[hd]
                ssave_ref[hd, ci] = st
                scores = _hgrn_scores(a, lvl_ref, q_pairs, k_diags).astype(asave_ref.dtype)
                asave_ref[hd, ci] = scores
                o_ref[hd, r, :] = _dot(a["q"] * a["eb"], st, _NT) + _dot(scores, a["v"])
                st_ref[hd] = st * a["ebc"] + _dot(a["v"], a["k"] * a["erem"], _TN)
            return carry

        lax.fori_loop(0, ncb, chunk, 0, unroll=SCAN_UNROLL)

    pair = lambda base: pl.BlockSpec((2, tb, LANE), lambda p, t, base=base: (base // 2 + p, t, 0))
    per_chunk = pl.BlockSpec((2, ncb, LANE, LANE), lambda p, t: (p, t, 0, 0))
    return pl.pallas_call(
        body, name="hgrn_fwd", grid=(N_HEADS // 2, t_len // tb),
        in_specs=[pair(CB_HQ), pair(CB_HF), pair(CB_HI),
                  pl.BlockSpec((1, 2 * LANE), lambda p, t: (0, p))] + [_resident(c) for c in operands],
        out_specs=[pl.BlockSpec((2, tb, LANE), lambda p, t: (p, t, 0)), per_chunk, per_chunk],
        out_shape=[jax.ShapeDtypeStruct((N_HEADS, t_len, LANE), F32),
                   jax.ShapeDtypeStruct((N_HEADS, nc, LANE, LANE), F32),
                   jax.ShapeDtypeStruct((N_HEADS, nc, CHUNK, CHUNK), _BF)],
        scratch_shapes=[pltpu.VMEM((2, LANE, LANE), F32), pltpu.VMEM((2, CHUNK, LANE), F32)],
        compiler_params=_params(("arbitrary", "arbitrary")),
    )(pb, pb, pb, lb, *operands)


def hgrn_backward(pb, lb, do, ssave, asave, dpb, t_len):
    tb = _time_block(t_len)
    ncb, ntb = tb // CHUNK, t_len // tb
    consts = _hgrn_consts()
    operands = [consts[n] for n in ("tri", "sgn", "sel_t", "lvl_b", "lvlt_b")]

    def body(hq_ref, hf_ref, hi_ref, lb_ref, do_ref, ssave_ref, asave_ref, tri_ref, sgn_ref, selt_ref, lvlb_ref,
             lvltb_ref, dpb_in, dpb_ref, dlb_ref, dq_buf, df_buf, di_buf, dst_ref, b_scr, sems):
        del dpb_in
        p, t = pl.program_id(0), pl.program_id(1)
        step = p * ntb + t
        rows = pl.ds(pl.multiple_of((ntb - 1 - t) * tb, tb), tb)

        def out_copies(sl):
            return [pltpu.make_async_copy(buf.at[sl], dpb_ref.at[pl.ds(base + 2 * p, 2), rows], sems.at[sl, n])
                    for n, (buf, base) in enumerate(((dq_buf, CB_HQ), (df_buf, CB_HF), (di_buf, CB_HI)))]

        slot = _writeback_reserve(step, out_copies)

        @pl.when(t == 0)
        def _():
            dst_ref[...] = jnp.zeros_like(dst_ref)
            dlb_ref[...] = jnp.zeros_like(dlb_ref)

        def chunk(i, carry):
            ci = ncb - 1 - i
            r = pl.ds(pl.multiple_of(ci * CHUNK, CHUNK), CHUNK)
            for hd in range(2):
                head_chunk(hd, ci, r)
            return carry

        def head_chunk(hd, ci, r):
            lbv = lb_ref[:, hd * LANE:(hd + 1) * LANE]
            hq = hq_ref[hd, r, :]
            a = _hgrn_chunk(hq, hf_ref[hd, r, :], hi_ref[hd, r, :], lbv, tri_ref, sgn_ref, b_scr.at[hd])
            _, k_diags, ql, kl = _level_operands(a)
            q, k, v = a["q"], a["k"], a["v"]
            g = do_ref[hd, r, :]
            st0 = ssave_ref[hd, ci]
            dst = dst_ref[hd]
            scores = asave_ref[hd, ci]
            da = _dot(g, v, _NT)
            da_t = _dot(v, g, _NT)
            kb = k * a["erem"]
            qb = q * a["eb"]
            dv = _dot(scores, g, _TN) + _dot(kb, dst, _NT)
            dq_inter = _dot(g, st0) * a["eb"]
            dk_state = _dot(v, dst) * a["erem"]
            dq, dk = dq_inter, dk_state
            de = [q * dq_inter]
            da_b, dat_b = da.astype(_BF), da_t.astype(_BF)
            for n in range(len(k_diags)):
                l0, l1 = 2 * n, 2 * n + 1
                da_pair = jnp.concatenate([lvlb_ref[l0] * da_b, lvlb_ref[l1] * da_b], axis=1)
                dat_pair = jnp.concatenate([lvltb_ref[l0] * dat_b, lvltb_ref[l1] * dat_b], axis=1)
                dq_both = lax.dot_general(da_pair, k_diags[n], _NN, preferred_element_type=F32)
                dk_both = lax.dot_general(dat_pair, _blockdiag(ql[l0], ql[l1]), _NN, preferred_element_type=F32)
                for l, cols in ((l0, slice(0, LANE)), (l1, slice(LANE, 2 * LANE))):
                    dql, dkl = dq_both[:, cols], dk_both[:, cols]
                    if l > 0:
                        e = a["lev"][l]
                        dql, dkl = dql * e, dkl * e
                        de.append(q * dql + k * dkl)
                    dq = dq + dql
                    dk = dk + dkl
            de.append(k * dk_state)
            dst_ref[hd] = dst * a["ebc"] + _dot(g, qb, _TN)
            dbc = jnp.sum(dst * st0, axis=0, keepdims=True) * a["ebc"]
            de2 = lax.dot_general(selt_ref[...], _split2(jnp.concatenate(de, axis=0)), _NN,
                                  preferred_element_type=F32)
            dlogf = de2[:, :LANE] + de2[:, LANE:] + dbc
            sq, sg = a["sq"], a["sg"]
            df = dlogf / a["f"] - dk
            dq_buf[slot, hd, r, :] = (dq * (sq * (1.0 + hq * (1.0 - sq)))).astype(dq_buf.dtype)
            df_buf[slot, hd, r, :] = (df * a["omlb"] * sg * (1.0 - sg)).astype(df_buf.dtype)
            di_buf[slot, hd, r, :] = dv.astype(di_buf.dtype)
            cols = slice(hd * LANE, (hd + 1) * LANE)
            dlb_ref[:, cols] = dlb_ref[:, cols] + jnp.sum(df * (1.0 - sg), axis=0, keepdims=True)

        lax.fori_loop(0, ncb, chunk, 0, unroll=SCAN_UNROLL)
        _writeback_commit(step, (N_HEADS // 2) * ntb, slot, out_copies)

    pair = lambda base: pl.BlockSpec((2, tb, LANE), lambda p, t, base=base: (base // 2 + p, ntb - 1 - t, 0))
    any_spec = pl.BlockSpec(memory_space=pl.ANY)
    per_chunk = pl.BlockSpec((2, ncb, LANE, LANE), lambda p, t: (p, ntb - 1 - t, 0, 0))
    return pl.pallas_call(
        body, name="hgrn_bwd", grid=(N_HEADS // 2, ntb),
        in_specs=[pair(CB_HQ), pair(CB_HF), pair(CB_HI),
                  pl.BlockSpec((1, 2 * LANE), lambda p, t: (0, p)),
                  pair(0), per_chunk, per_chunk]
        + [_resident(c) for c in operands] + [any_spec],
        out_specs=[any_spec, pl.BlockSpec((1, 2 * LANE), lambda p, t: (0, p))],
        out_shape=[jax.ShapeDtypeStruct(dpb.shape, dpb.dtype), jax.ShapeDtypeStruct((1, D_MODEL), F32)],
        scratch_shapes=[pltpu.VMEM((2, 2, tb, LANE), dpb.dtype)] * 3 + [
            pltpu.VMEM((2, LANE, LANE), F32), pltpu.VMEM((2, CHUNK, LANE), F32), pltpu.SemaphoreType.DMA((2, 3))],
        input_output_aliases={7 + len(operands): 0},
        compiler_params=_params(("arbitrary", "arbitrary")),
    )(pb, pb, pb, lb, do, ssave, asave, *operands, dpb)


def _rope_tables(t_len):
    half = RET_DK // 2
    inv_freq = 1.0 / (ROPE_BASE ** jnp.linspace(0.0, 1.0, half, dtype=F32))
    ang = jnp.arange(t_len, dtype=jnp.int32).astype(F32)[:, None] * inv_freq[None, :]
    cos, sin = jnp.cos(ang), jnp.sin(ang)
    cos_t = jnp.concatenate([cos, cos, cos, cos], axis=1)
    sin_t = jnp.concatenate([-sin, sin, -sin, sin], axis=1)
    return cos_t, sin_t


def _swap_halves(v):
    half = RET_DK // 2
    lane = lax.broadcasted_iota(jnp.int32, v.shape, 1)
    first = (lane & (RET_DK - 1)) < half
    return jnp.where(first, pltpu.roll(v, LANE - half, 1), pltpu.roll(v, half, 1))


def _ret_head_consts(hidx):
    c = CHUNK
    hf = jnp.full((1, LANE), hidx, jnp.int32).astype(F32)
    lg = jnp.log(1.0 - jnp.exp(-(5.0 + hf) * np.float32(np.log(2.0))))
    row = lax.broadcasted_iota(jnp.int32, (c, c), 0)
    col = lax.broadcasted_iota(jnp.int32, (c, c), 1)
    rel = (row - col).astype(F32)
    dm = jnp.where(rel >= 0, jnp.exp(lg[:, :1] * jnp.maximum(rel, 0.0)), 0.0)
    dm_t = jnp.where(rel <= 0, jnp.exp(lg[:, :1] * jnp.maximum(-rel, 0.0)), 0.0)
    idx = lax.broadcasted_iota(jnp.int32, (c, LANE), 0).astype(F32)
    zeta = jnp.exp(lg * (c - 1.0 - idx))
    xi = jnp.exp(lg * (idx + 1.0))
    cdec = jnp.exp(lg * float(c))
    return dm, zeta, xi, cdec, dm_t


def _lane_mask(which):
    lane = lax.broadcasted_iota(jnp.int32, (1, LANE), 1)
    return ((lane // RET_DK) == which).astype(F32)


def retention_forward(pb, cos_t, sin_t, t_len):
    nc = t_len // CHUNK

    tb = _time_block(t_len)
    ncb = tb // CHUNK

    def body(rq_ref, rk_ref, rv_ref, cos_ref, sin_ref, o_ref, rsave_ref, st_ref):
        p = pl.program_id(0)

        @pl.when(pl.program_id(1) == 0)
        def _():
            st_ref[...] = jnp.zeros_like(st_ref)

        consts = [_ret_head_consts(2 * p + hd) for hd in range(2)]

        def chunk(ci, carry):
            r = pl.ds(pl.multiple_of(ci * CHUNK, CHUNK), CHUNK)
            cs, sn = cos_ref[r, :], sin_ref[r, :]
            q = rq_ref[r, :]
            k = rk_ref[r, :]
            q = q * cs + _swap_halves(q) * sn
            k = (k * cs + _swap_halves(k) * sn) * RET_DK ** -0.5
            for hd in range(2):
                dm, zeta, xi, cdec, _ = consts[hd]
                lm = _lane_mask(hd)
                qh, kh = q * lm, k * lm
                v = rv_ref[hd, r, :]
                st = st_ref[hd]
                rsave_ref[hd, ci] = st
                scores = _dot(qh, kh, _NT) * dm
                o_ref[hd, r, :] = _dot(scores, v) + _dot(qh * xi, st, _NT)
                st_ref[hd] = st * cdec + _dot(v, kh * zeta, _TN)
            return carry

        lax.fori_loop(0, ncb, chunk, 0, unroll=RET_UNROLL)

    return pl.pallas_call(
        body, name="ret_fwd", grid=(N_HEADS // 2, t_len // tb),
        in_specs=[pl.BlockSpec((None, tb, LANE), lambda p, t: (CB_RQ + p, t, 0)),
                  pl.BlockSpec((None, tb, LANE), lambda p, t: (CB_RK + p, t, 0)),
                  pl.BlockSpec((2, tb, LANE), lambda p, t: (CB_RV // 2 + p, t, 0)),
                  pl.BlockSpec((tb, LANE), lambda p, t: (t, 0)),
                  pl.BlockSpec((tb, LANE), lambda p, t: (t, 0))],
        out_specs=[pl.BlockSpec((2, tb, LANE), lambda p, t: (p, t, 0)),
                   pl.BlockSpec((2, ncb, LANE, LANE), lambda p, t: (p, t, 0, 0))],
        out_shape=[jax.ShapeDtypeStruct((N_HEADS, t_len, LANE), F32),
                   jax.ShapeDtypeStruct((N_HEADS, nc, LANE, LANE), F32)],
        scratch_shapes=[pltpu.VMEM((2, LANE, LANE), F32)],
        compiler_params=_params(("arbitrary", "arbitrary")),
    )(pb, pb, pb, cos_t, sin_t)


def retention_backward(pb, cos_t, sin_t, do, rsave, dpb, t_len):
    tb = _time_block(t_len)
    ncb, ntb = tb // CHUNK, t_len // tb

    def body(rq_ref, rk_ref, rv_ref, cos_ref, sin_ref, do_ref, rsave_ref, dpb_in,
             dpb_ref, dq_buf, dk_buf, dv_buf, dst_ref, sems):
        del dpb_in
        p, t = pl.program_id(0), pl.program_id(1)
        step = p * ntb + t
        rows = pl.ds(pl.multiple_of((ntb - 1 - t) * tb, tb), tb)

        def out_copies(sl):
            return [pltpu.make_async_copy(dq_buf.at[sl], dpb_ref.at[CB_RQ + p, rows], sems.at[sl, 0]),
                    pltpu.make_async_copy(dk_buf.at[sl], dpb_ref.at[CB_RK + p, rows], sems.at[sl, 1]),
                    pltpu.make_async_copy(dv_buf.at[sl], dpb_ref.at[pl.ds(CB_RV + 2 * p, 2), rows], sems.at[sl, 2])]

        slot = _writeback_reserve(step, out_copies)

        @pl.when(t == 0)
        def _():
            dst_ref[...] = jnp.zeros_like(dst_ref)

        consts = [_ret_head_consts(2 * p + hd) for hd in range(2)]

        def chunk(i, carry):
            ci = ncb - 1 - i
            r = pl.ds(pl.multiple_of(ci * CHUNK, CHUNK), CHUNK)
            cs, sn = cos_ref[r, :], sin_ref[r, :]
            q = rq_ref[r, :]
            k = rk_ref[r, :]
            q = q * cs + _swap_halves(q) * sn
            k = (k * cs + _swap_halves(k) * sn) * RET_DK ** -0.5
            dq, dk = None, None
            for hd in range(2):
                dm, zeta, xi, cdec, dm_t = consts[hd]
                lm = _lane_mask(hd)
                qh, kh = q * lm, k * lm
                v = rv_ref[hd, r, :]
                g = do_ref[hd, r, :]
                st0 = rsave_ref[hd, ci]
                dst = dst_ref[hd]
                scores_t = _dot(kh, qh, _NT) * dm_t
                dsc = _dot(g, v, _NT) * dm
                dsc_t = _dot(v, g, _NT) * dm_t
                dqh = _dot(dsc, kh) + _dot(g, st0) * xi
                dkh = _dot(dsc_t, qh) + _dot(v, dst) * zeta
                dv_buf[slot, hd, r, :] = (_dot(scores_t, g) + _dot(kh * zeta, dst, _NT)).astype(dv_buf.dtype)
                dst_ref[hd] = dst * cdec + _dot(g, qh * xi, _TN)
                dq = dqh if dq is None else dq + dqh
                dk = dkh if dk is None else dk + dkh
            dk = dk * (RET_DK ** -0.5)
            dq_buf[slot, r, :] = (dq * cs - _swap_halves(dq) * sn).astype(dq_buf.dtype)
            dk_buf[slot, r, :] = (dk * cs - _swap_halves(dk) * sn).astype(dk_buf.dtype)
            return carry

        lax.fori_loop(0, ncb, chunk, 0, unroll=RET_UNROLL)
        _writeback_commit(step, (N_HEADS // 2) * ntb, slot, out_copies)

    any_spec = pl.BlockSpec(memory_space=pl.ANY)
    return pl.pallas_call(
        body, name="ret_bwd", grid=(N_HEADS // 2, ntb),
        in_specs=[pl.BlockSpec((None, tb, LANE), lambda p, t: (CB_RQ + p, ntb - 1 - t, 0)),
                  pl.BlockSpec((None, tb, LANE), lambda p, t: (CB_RK + p, ntb - 1 - t, 0)),
                  pl.BlockSpec((2, tb, LANE), lambda p, t: (CB_RV // 2 + p, ntb - 1 - t, 0)),
                  pl.BlockSpec((tb, LANE), lambda p, t: (ntb - 1 - t, 0)),
                  pl.BlockSpec((tb, LANE), lambda p, t: (ntb - 1 - t, 0)),
                  pl.BlockSpec((2, tb, LANE), lambda p, t: (p, ntb - 1 - t, 0)),
                  pl.BlockSpec((2, ncb, LANE, LANE), lambda p, t: (p, ntb - 1 - t, 0, 0)),
                  any_spec],
        out_specs=any_spec,
        out_shape=jax.ShapeDtypeStruct(dpb.shape, dpb.dtype),
        scratch_shapes=[pltpu.VMEM((2, tb, LANE), dpb.dtype), pltpu.VMEM((2, tb, LANE), dpb.dtype),
                        pltpu.VMEM((2, 2, tb, LANE), dpb.dtype), pltpu.VMEM((2, LANE, LANE), F32),
                        pltpu.SemaphoreType.DMA((2, 3))],
        input_output_aliases={7: 0},
        compiler_params=_params(("arbitrary", "arbitrary")),
    )(pb, pb, pb, cos_t, sin_t, do, rsave, dpb)


def _row_tile(t_len, want):
    return min(want, t_len)


PAIR_CB = 2 * CB_PER_SHARD


def proj_forward(h, wt, t_len):
    tm = _row_tile(t_len, 1024)

    def body(h_ref, w_ref, o_ref):
        acc = _dot(h_ref[...], w_ref[...], _NT)
        for jj in range(PAIR_CB):
            o_ref[jj] = acc[:, jj * LANE:(jj + 1) * LANE]

    return pl.pallas_call(
        body, name="proj_fwd", grid=(N_DEV // 2, t_len // tm),
        in_specs=[pl.BlockSpec((tm, D_MODEL), lambda j, i: (i, 0)),
                  pl.BlockSpec((PAIR_CB * LANE, D_MODEL), lambda j, i: (j, 0))],
        out_specs=pl.BlockSpec((PAIR_CB, tm, LANE), lambda j, i: (j, i, 0)),
        out_shape=jax.ShapeDtypeStruct((N_CB, t_len, LANE), F32),
        compiler_params=_params(("arbitrary", "arbitrary")),
    )(h, wt)


def proj_backward_input(dpb, wt, token, x, dy, norm_g, scale1p, t_len):
    tm = _row_tile(t_len, 256)

    def body(a_ref, wt_hbm, token_ref, x_ref, dy_ref, g_ref, sc_ref, gx_ref, vec_ref, w_ref, sem):
        del token_ref
        i = pl.program_id(0)

        @pl.when(i == 0)
        def _():
            cp = pltpu.make_async_copy(wt_hbm, w_ref, sem)
            cp.start()
            cp.wait()

        a = jnp.concatenate([a_ref[jj].astype(_BF) for jj in range(N_CB)], axis=1)
        dhv = _dot(a, w_ref[...])
        xv, g, sc = x_ref[...], g_ref[...], sc_ref[...]
        r = lax.rsqrt(jnp.mean(xv * xv, axis=-1, keepdims=True) + EPS)
        xn = xv * r
        dxn = dhv * (g * sc)
        gx_ref[...] = dy_ref[...] + r * dxn - xn * (r * r) * jnp.mean(xv * dxn, axis=-1, keepdims=True)
        t = dhv * xn
        _acc_rows(vec_ref, i, [jnp.sum(t * sc, axis=0, keepdims=True),
                               jnp.sum(t * g, axis=0, keepdims=True),
                               jnp.sum(dhv, axis=0, keepdims=True)])

    row = pl.BlockSpec((tm, D_MODEL), lambda i: (i, 0))
    return pl.pallas_call(
        body, name="proj_bwd_input", grid=(t_len // tm,),
        in_specs=[pl.BlockSpec((N_CB, tm, LANE), lambda i: (0, i, 0)),
                  pl.BlockSpec(memory_space=pl.ANY),
                  pl.BlockSpec(token.shape, lambda i: (0, 0)),
                  row, row, _vec_spec(), _vec_spec()],
        out_specs=[row, pl.BlockSpec((8, D_MODEL), lambda i: (0, 0))],
        out_shape=[jax.ShapeDtypeStruct((t_len, D_MODEL), F32), jax.ShapeDtypeStruct((8, D_MODEL), F32)],
        scratch_shapes=[pltpu.VMEM(wt.shape, wt.dtype), pltpu.SemaphoreType.DMA],
        compiler_params=_params(("arbitrary",)),
    )(dpb, wt, token, x, dy, norm_g, scale1p)


def proj_backward_weight(h_t, dpb, t_len):
    tk = _row_tile(t_len, 2048)

    def body(h_ref, b_ref, o_ref):
        k = pl.program_id(1)
        b = jnp.concatenate([b_ref[jj].astype(_BF) for jj in range(PAIR_CB)], axis=1)
        part = _dot(h_ref[...], b)

        @pl.when(k == 0)
        def _():
            for s in range(2):
                o_ref[s] = part[:, s * SHARD_IN:(s + 1) * SHARD_IN]

        @pl.when(k > 0)
        def _():
            for s in range(2):
                o_ref[s] = o_ref[s] + part[:, s * SHARD_IN:(s + 1) * SHARD_IN]

    return pl.pallas_call(
        body, name="proj_bwd_weight", grid=(N_DEV // 2, t_len // tk),
        in_specs=[pl.BlockSpec((D_MODEL, tk), lambda j, k: (0, k)),
                  pl.BlockSpec((PAIR_CB, tk, LANE), lambda j, k: (j, k, 0))],
        out_specs=pl.BlockSpec((2, D_MODEL, SHARD_IN), lambda j, k: (j, 0, 0)),
        out_shape=jax.ShapeDtypeStruct((N_DEV, D_MODEL, SHARD_IN), F32),
        compiler_params=_params(("arbitrary", "arbitrary")),
    )(h_t, dpb)


def sibling_blocks(g_in, c_idx):
    tr = D_MODEL

    def body(c_ref, g_ref, o_ref):
        del c_ref
        o_ref[...] = g_ref[...].astype(o_ref.dtype)

    return pl.pallas_call(
        body, name="sibling_blocks",
        grid_spec=pltpu.PrefetchScalarGridSpec(
            num_scalar_prefetch=1, grid=(N_DEV // 2, D_MODEL // tr),
            in_specs=[pl.BlockSpec((None, tr, SHARD_IN), lambda q, i, c: (2 * q + 1 - c[0], i, 0))],
            out_specs=pl.BlockSpec((None, tr, SHARD_IN), lambda q, i, c: (q, i, 0))),
        out_shape=jax.ShapeDtypeStruct((N_DEV // 2, D_MODEL, SHARD_IN), _BF),
        compiler_params=_params(("arbitrary", "arbitrary")),
    )(c_idx, g_in)


def _vec_spec():
    return pl.BlockSpec((1, D_MODEL), lambda i: (0, 0))


def _acc_rows(ref, i, rows):
    @pl.when(i == 0)
    def _():
        ref[...] = jnp.zeros_like(ref)

    for n, row in enumerate(rows):
        ref[n:n + 1, :] = ref[n:n + 1, :] + row


def adaln_forward(x, norm_g, scale1p, shift, t_len):
    tm = _row_tile(t_len, 512)

    def body(x_ref, g_ref, sc_ref, sh_ref, h_ref, ht_ref):
        xv = x_ref[...]
        r = lax.rsqrt(jnp.mean(xv * xv, axis=-1, keepdims=True) + EPS)
        h = xv * r * g_ref[...] * sc_ref[...] + sh_ref[...]
        h_ref[...] = h.astype(h_ref.dtype)
        ht_ref[...] = h.T.astype(ht_ref.dtype)

    return pl.pallas_call(
        body, name="adaln_fwd", grid=(t_len // tm,),
        in_specs=[pl.BlockSpec((tm, D_MODEL), lambda i: (i, 0)), _vec_spec(), _vec_spec(), _vec_spec()],
        out_specs=[pl.BlockSpec((tm, D_MODEL), lambda i: (i, 0)), pl.BlockSpec((D_MODEL, tm), lambda i: (0, i))],
        out_shape=[jax.ShapeDtypeStruct((t_len, D_MODEL), _BF), jax.ShapeDtypeStruct((D_MODEL, t_len), _BF)],
        compiler_params=_params(("arbitrary",)),
    )(x, norm_g, scale1p, shift)


def _head_norm(o, g):
    r = lax.rsqrt(jnp.mean(o * o, axis=-1, keepdims=True) + EPS)
    return r, o * r * g


def _group_spec(tm, cb):
    return pl.BlockSpec((N_HEADS, tm, LANE), lambda i, cb=cb: (cb // N_HEADS, i, 0))


MID_FINAL_G, MID_GATE, MID_LOSS, MID_HG_G, MID_RET_G = range(5)


def middle(x, target, oa, ob, pb, wout, gate, final_g, hg_g, ret_g, t_len):
    tm = _row_tile(t_len, 256)
    n_steps = t_len // tm

    def body(x_ref, t_ref, oa_ref, ob_ref, hz_ref, rz_ref, ga_ref, gb_ref, w_ref, gate_ref, fg_ref, hg_ref, rg_ref,
             dy_ref, doa_ref, dob_ref, dw_ref, vec_ref, dpb_ref, m_scr, dm_scr, keep, bufs, sems):
        i = pl.program_id(0)
        rows = pl.ds(pl.multiple_of(i * tm, tm), tm)

        def group_copies(sl):
            return [pltpu.make_async_copy(bufs.at[sl, n], dpb_ref.at[pl.ds(cb, N_HEADS), rows], sems.at[sl, n])
                    for n, cb in enumerate((CB_HZ, CB_RZ, CB_GA, CB_GB))]
        sides = ((oa_ref, hz_ref, ga_ref, hg_ref, doa_ref), (ob_ref, rz_ref, gb_ref, rg_ref, dob_ref))
        for hh in range(N_HEADS):
            ls = slice(hh * LANE, (hh + 1) * LANE)
            acc = None
            for side, (o_ref, z_ref, gt_ref, g_ref, _) in enumerate(sides):
                o = o_ref[hh]
                rr = lax.rsqrt(jnp.mean(o * o, axis=-1, keepdims=True) + EPS)
                orr = o * rr
                zz = z_ref[hh]
                sz = _sigmoid(zz)
                sgt = _sigmoid(gt_ref[hh])
                keep[side, hh, 0] = orr
                keep[side, hh, 1] = sz
                keep[side, hh, 2] = sgt
                keep[side, hh, 3] = jnp.broadcast_to(rr, orr.shape)
                u = sgt * ((orr * g_ref[:, ls]) * (zz * sz))
                acc = u if acc is None else acc + u
            m_scr[:, ls] = acc.astype(m_scr.dtype)
        zv = _dot(m_scr[...], w_ref[...])
        gt, fg = gate_ref[...], fg_ref[...]
        y = x_ref[...] + gt * zv
        r = lax.rsqrt(jnp.mean(y * y, axis=-1, keepdims=True) + EPS)
        yn = y * r
        err = yn * fg - t_ref[...]
        loss = 0.5 * jnp.sum(jnp.mean(err * err, axis=-1, keepdims=True), axis=0, keepdims=True)
        dout = err * (1.0 / D_MODEL)
        gd = dout * fg
        dy = r * gd - yn * (r * r) * jnp.mean(y * gd, axis=-1, keepdims=True)
        dy_ref[...] = dy
        dz = (dy * gt).astype(_BF)
        dm_scr[...] = _dot(dz, w_ref[...], _NT)
        part = _dot(m_scr[...], dz, _TN)

        @pl.when(i == 0)
        def _():
            dw_ref[...] = part

        @pl.when(i > 0)
        def _():
            dw_ref[...] = dw_ref[...] + part

        slot = _writeback_reserve(i, group_copies)
        dg = [[], []]
        for hh in range(N_HEADS):
            ls = slice(hh * LANE, (hh + 1) * LANE)
            dmh = dm_scr[:, ls]
            for side, (o_ref, z_ref, gt_ref, g_ref, do_ref) in enumerate(sides):
                zz, g = z_ref[hh], g_ref[:, ls]
                orr, sz, sgt, rr = keep[side, hh, 0], keep[side, hh, 1], keep[side, hh, 2], keep[side, hh, 3]
                n = orr * g
                silu = zz * sz
                du = dmh * sgt
                bufs[slot, 2 + side, hh] = (dmh * (n * silu) * (sgt * (1.0 - sgt))).astype(bufs.dtype)
                bufs[slot, side, hh] = (du * n * (sz * (1.0 + zz * (1.0 - sz)))).astype(bufs.dtype)
                dn = du * silu
                dg[side].append(jnp.sum(dn * orr, axis=0, keepdims=True))
                gdn = dn * g
                do_ref[hh] = (rr * (gdn - orr * jnp.mean(orr * gdn, axis=-1, keepdims=True))).astype(do_ref.dtype)
        _acc_rows(vec_ref, i, [jnp.sum(dout * yn, axis=0, keepdims=True),
                               jnp.sum(dy * zv, axis=0, keepdims=True),
                               jnp.broadcast_to(loss, (1, D_MODEL)),
                               jnp.concatenate(dg[0], axis=1), jnp.concatenate(dg[1], axis=1)])
        _writeback_commit(i, n_steps, slot, group_copies)

    row = pl.BlockSpec((tm, D_MODEL), lambda i: (i, 0))
    head = pl.BlockSpec((N_HEADS, tm, LANE), lambda i: (0, i, 0))
    full = pl.BlockSpec((D_MODEL, D_MODEL), lambda i: (0, 0))
    return pl.pallas_call(
        body, name="middle", grid=(n_steps,),
        in_specs=[row, row, head, head, _group_spec(tm, CB_HZ), _group_spec(tm, CB_RZ), _group_spec(tm, CB_GA),
                  _group_spec(tm, CB_GB), full, _vec_spec(), _vec_spec(), _vec_spec(), _vec_spec()],
        out_specs=[row, head, head, full, pl.BlockSpec((8, D_MODEL), lambda i: (0, 0)),
                   pl.BlockSpec(memory_space=pl.ANY)],
        out_shape=[jax.ShapeDtypeStruct((t_len, D_MODEL), F32),
                   jax.ShapeDtypeStruct((N_HEADS, t_len, LANE), _BF),
                   jax.ShapeDtypeStruct((N_HEADS, t_len, LANE), _BF),
                   jax.ShapeDtypeStruct((D_MODEL, D_MODEL), F32),
                   jax.ShapeDtypeStruct((8, D_MODEL), F32),
                   jax.ShapeDtypeStruct((N_CB, t_len, LANE), _BF)],
        scratch_shapes=[pltpu.VMEM((tm, D_MODEL), _BF), pltpu.VMEM((tm, D_MODEL), F32),
                        pltpu.VMEM((2, N_HEADS, 4, tm, LANE), F32),
                        pltpu.VMEM((2, 4, N_HEADS, tm, LANE), _BF), pltpu.SemaphoreType.DMA((2, 4))],
        compiler_params=_params(("arbitrary",)),
    )(x, target, oa, ob, pb, pb, pb, pb, wout, gate, final_g, hg_g, ret_g)


def device_step(x, target, mod, lb, wt, wout, norm_g, hg_g, ret_g, final_g, c_idx=None, start_exchange=None):
    t_len = x.shape[0]
    shift, scale, gate = mod[:, :D_MODEL], mod[:, D_MODEL:2 * D_MODEL], mod[:, 2 * D_MODEL:]
    scale1p = 1.0 + scale
    cos_t, sin_t = _rope_tables(t_len)
    h, h_t = adaln_forward(x, norm_g, scale1p, shift, t_len)
    pb = proj_forward(h, wt, t_len)
    oa, ssave, asave = hgrn_forward(pb, lb, t_len)
    ob, rsave = retention_forward(pb, cos_t, sin_t, t_len)
    dy, doa, dob, dwout, vec_mid, dpb = middle(x, target, oa, ob, pb, wout, gate, final_g, hg_g, ret_g, t_len)
    dpb, dlb = hgrn_backward(pb, lb, doa, ssave, asave, dpb, t_len)
    dpb = retention_backward(pb, cos_t, sin_t, dob, rsave, dpb, t_len)
    c_idx = jnp.zeros((1,), jnp.int32) if c_idx is None else c_idx
    dwin = proj_backward_weight(h_t, dpb, t_len)
    dwin_sib = sibling_blocks(dwin, c_idx)
    token, pending = (start_exchange(dwin, dwin_sib, dwout) if start_exchange
                      else (jnp.zeros((8, LANE), F32), None))
    grad_x, vec_ada = proj_backward_input(dpb, wt, token, x, dy, norm_g, scale1p, t_len)
    return grad_x, dwin, dwout, vec_mid, vec_ada, dlb, pending


PACK_ROWS = 16
ROW_NORM_G, ROW_LB, ROW_HG_G, ROW_RET_G, ROW_FINAL_G, ROW_SHIFT, ROW_SCALE, ROW_GATE, ROW_LOSS = range(9)


def _mesh_pos():
    return lax.axis_index("x"), lax.axis_index("y"), lax.axis_index("c")


def _lin(pos):
    return 4 * pos[0] + 2 * pos[1] + pos[2]


def _xor_peer(pos, k):
    return tuple(1 - p if (k >> s) & 1 else p for p, s in zip(pos, (2, 1, 0)))


def _other_chips(pos):
    x, y, _ = pos
    return [(1 - x, y), (x, 1 - y), (1 - x, 1 - y)]


def _remote(src, dst, send_sem, recv_sem, to):
    return pltpu.make_async_remote_copy(src_ref=src, dst_ref=dst, send_sem=send_sem, recv_sem=recv_sem,
                                        device_id=to, device_id_type=MESH)


def pre_exchange(c, w_ada, b_ada, logits):
    def body(c_ref, wada_ref, bada_ref, logit_ref, mod_ref, scall_ref, lb_ref,
             cg_ref, modall_ref, parts_ref, send1, recv1, send2, recv2):
        pos = _mesh_pos()
        cv = c_ref[...]
        slot = lambda p: pl.ds(pl.multiple_of(8 * _lin(p), 8), 8)
        cg_ref[slot(pos), :] = jnp.broadcast_to(cv * _sigmoid(cv), (8, D_MODEL))
        lb_ref[...] = _sigmoid(logit_ref[0:1, :] - logit_ref[1:2, :])
        peers = [_xor_peer(pos, k) for k in range(1, N_DEV)]
        gather = [_remote(cg_ref.at[slot(pos)], cg_ref.at[slot(pos)], send1.at[n], recv1.at[n], p)
                  for n, p in enumerate(peers)]
        for cp in gather:
            cp.start()
        for n, p in enumerate(peers):
            _remote(cg_ref.at[slot(p)], cg_ref.at[slot(p)], send1.at[n], recv1.at[n], p).wait_recv()
        modall_ref[...] = _dot(cg_ref[...], wada_ref[...])
        scatter = [_remote(modall_ref.at[slot(p)], parts_ref.at[slot(pos)], send2.at[n], recv2.at[n], p)
                   for n, p in enumerate(peers)]
        for cp in scatter:
            cp.start()
        parts_ref[slot(pos), :] = modall_ref[slot(pos), :]
        for n, p in enumerate(peers):
            _remote(modall_ref.at[slot(p)], parts_ref.at[slot(p)], send2.at[n], recv2.at[n], p).wait_recv()
        for cp in gather + scatter:
            cp.wait_send()
        for j in range(N_DEV):
            cols = slice(j * SHARD_ADA, (j + 1) * SHARD_ADA)
            mod_ref[:, cols] = parts_ref[8 * j:8 * j + 1, :] + bada_ref[:, cols]
            scall_ref[j:j + 1, :] = cg_ref[8 * j:8 * j + 1, :]

    vmem = pl.BlockSpec(memory_space=pltpu.VMEM)
    return pl.pallas_call(
        body, name="pre_exchange",
        in_specs=[vmem] * 4, out_specs=[vmem] * 3,
        out_shape=[jax.ShapeDtypeStruct((1, 3 * D_MODEL), F32), jax.ShapeDtypeStruct((N_DEV, D_MODEL), F32),
                   jax.ShapeDtypeStruct((1, D_MODEL), F32)],
        scratch_shapes=[pltpu.VMEM((N_DEV * 8, D_MODEL), F32), pltpu.VMEM((N_DEV * 8, SHARD_ADA), F32),
                        pltpu.VMEM((N_DEV * 8, SHARD_ADA), F32)] + [pltpu.SemaphoreType.DMA((N_DEV - 1,))] * 4,
        compiler_params=pltpu.CompilerParams(vmem_limit_bytes=VMEM_LIMIT),
    )(c, w_ada, b_ada, logits)


def weight_gather(win_sh, wout_sh):
    def body(win_ref, wout_ref, wg_ref, woutg_ref, send, recv, local):
        pos = _mesh_pos()
        x, y, c = pos
        sibling = (x, y, 1 - c)

        def route(core):
            return [(x + (1 - core) * (1 - 2 * x), y + core * (1 - 2 * y)),
                    (x + core * (1 - 2 * x), y + (1 - core) * (1 - 2 * y)),
                    (1 - x, 1 - y)]

        chips, sib_chips = route(c), route(1 - c)
        mine, first, later = [], [], []
        for a, (src, out) in enumerate(((win_ref, wg_ref), (wout_ref, woutg_ref))):
            def copy(k, block, to, src_ref=None, a=a, out=out):
                dst = out.at[_lin(block)]
                return _remote(dst if src_ref is None else src_ref, dst, send.at[7 * a + k], recv.at[7 * a + k], to)
            mine.append(pltpu.make_async_copy(src, out.at[_lin(pos)], local.at[a]))
            first += [copy(0, pos, sibling, src), copy(1, pos, (*chips[0], c), src), copy(2, pos, (*chips[1], c), src)]
            later.append([[copy(3, (*chips[0], c), (*chips[1], c)), copy(4, (*chips[0], c), sibling)],
                          [copy(5, (*chips[1], c), sibling)],
                          [copy(6, (*chips[2], c), sibling)]])
        for cp in mine + first:
            cp.start()
        for j in range(3):
            for a, out in enumerate((wg_ref, woutg_ref)):
                dst = out.at[_lin((*chips[j], c))]
                _remote(dst, dst, send.at[7 * a + 1 + j], recv.at[7 * a + 1 + j], pos).wait_recv()
                for cp in later[a][j]:
                    cp.start()
        for a, out in enumerate((wg_ref, woutg_ref)):
            dst = out.at[_lin(sibling)]
            _remote(dst, dst, send.at[7 * a], recv.at[7 * a], pos).wait_recv()
            for j in range(3):
                dst = out.at[_lin((*sib_chips[j], 1 - c))]
                _remote(dst, dst, send.at[7 * a + 4 + j], recv.at[7 * a + 4 + j], pos).wait_recv()
        for cp in first + [cp for per_array in later for group in per_array for cp in group]:
            cp.wait_send()
        for cp in mine:
            cp.wait()

    any_spec = pl.BlockSpec(memory_space=pl.ANY)
    return pl.pallas_call(
        body, name="weight_gather",
        in_specs=[any_spec, any_spec], out_specs=[any_spec, any_spec],
        out_shape=[jax.ShapeDtypeStruct((N_DEV,) + win_sh.shape, win_sh.dtype),
                   jax.ShapeDtypeStruct((N_DEV,) + wout_sh.shape, wout_sh.dtype)],
        scratch_shapes=[pltpu.SemaphoreType.DMA((14,)), pltpu.SemaphoreType.DMA((14,)),
                        pltpu.SemaphoreType.DMA((2,))],
    )(win_sh, wout_sh)


def grad_pair_exchange(g_sib, g_out):
    def body(gsib_ref, gout_ref, ra_ref, rb_ref, send, recv):
        pos = _mesh_pos()
        x, y, c = pos
        sibling = (x, y, 1 - c)
        copies = []
        for q in range(4):
            copies.append(_remote(gsib_ref.at[q], ra_ref.at[q], send.at[q], recv.at[q], sibling))
            copies.append(_remote(gout_ref.at[2 * q + (1 - c)], rb_ref.at[q], send.at[4 + q], recv.at[4 + q], sibling))
        for cp in copies:
            cp.start()
        for cp in copies:
            cp.wait_recv()
        for cp in copies:
            cp.wait_send()

    any_spec = pl.BlockSpec(memory_space=pl.ANY)
    return pl.pallas_call(
        body, name="grad_pair_exchange",
        in_specs=[any_spec, any_spec], out_specs=[any_spec, any_spec],
        out_shape=[jax.ShapeDtypeStruct(g_sib.shape, g_sib.dtype), jax.ShapeDtypeStruct((4,) + g_out.shape[1:], F32)],
        scratch_shapes=[pltpu.SemaphoreType.DMA((8,)), pltpu.SemaphoreType.DMA((8,))],
    )(g_sib, g_out)


def pair_sum(g_in, ra, g_out, rb, c_idx):
    tr = 256

    def body(c_ref, gin_ref, ra_ref, gout_ref, rb_ref, sb_ref, sbo_ref):
        del c_ref
        sb_ref[...] = (gin_ref[...] + ra_ref[...].astype(F32)).astype(sb_ref.dtype)
        sbo_ref[...] = gout_ref[...] + rb_ref[...]

    n_i = D_MODEL // tr
    return pl.pallas_call(
        body, name="pair_sum",
        grid_spec=pltpu.PrefetchScalarGridSpec(
            num_scalar_prefetch=1, grid=(4, n_i),
            in_specs=[pl.BlockSpec((None, tr, SHARD_IN), lambda q, i, c: (2 * q + c[0], i, 0)),
                      pl.BlockSpec((None, tr, SHARD_IN), lambda q, i, c: (q, i, 0)),
                      pl.BlockSpec((None, SHARD_OUT // n_i, D_MODEL), lambda q, i, c: (2 * q + c[0], i, 0)),
                      pl.BlockSpec((None, SHARD_OUT // n_i, D_MODEL), lambda q, i, c: (q, i, 0))],
            out_specs=[pl.BlockSpec((None, tr, SHARD_IN), lambda q, i, c: (q, i, 0)),
                       pl.BlockSpec((None, SHARD_OUT // n_i, D_MODEL), lambda q, i, c: (q, i, 0))]),
        out_shape=[jax.ShapeDtypeStruct(ra.shape, _BF), jax.ShapeDtypeStruct(rb.shape, F32)],
        compiler_params=_params(("arbitrary", "arbitrary")),
    )(c_idx, g_in, ra, g_out, rb)


_HBM = pl.BlockSpec(memory_space=pltpu.HBM)
_SEM = pl.BlockSpec(memory_space=pltpu.SEMAPHORE)
_N_CHIP_COPIES = 6


def _chip_copies(sb_ref, sbo_ref, rc_ref, rco_ref, send, recv):
    pos = _mesh_pos()
    copies = []
    for a, (src, dst) in enumerate(((sb_ref, rc_ref), (sbo_ref, rco_ref))):
        for j, chip in enumerate(_other_chips(pos)):
            copies.append(_remote(src.at[2 * chip[0] + chip[1]], dst.at[j], send.at[3 * a + j], recv.at[3 * a + j],
                                  (*chip, pos[2])))
    return copies


def grad_chip_start(sb, sbo):
    def body(sb_ref, sbo_ref, rc_ref, rco_ref, send, recv, sb_thru, sbo_thru, rc_thru, rco_thru, token):
        del sb_thru, sbo_thru, rc_thru, rco_thru
        for cp in _chip_copies(sb_ref, sbo_ref, rc_ref, rco_ref, send, recv):
            cp.start()
        token[...] = jnp.zeros_like(token)

    hbm = lambda a: pltpu.with_memory_space_constraint(a, pltpu.HBM)
    rc = lax.empty((3,) + sb.shape[1:], sb.dtype)
    rco = lax.empty((3,) + sbo.shape[1:], sbo.dtype)
    return pl.pallas_call(
        body, name="grad_chip_start",
        in_specs=[_HBM] * 4,
        out_specs=[_SEM, _SEM, _HBM, _HBM, _HBM, _HBM, pl.BlockSpec(memory_space=pltpu.VMEM)],
        out_shape=[pltpu.SemaphoreType.DMA((_N_CHIP_COPIES,)), pltpu.SemaphoreType.DMA((_N_CHIP_COPIES,)),
                   pltpu.HBM(sb.shape, sb.dtype), pltpu.HBM(sbo.shape, sbo.dtype),
                   pltpu.HBM(rc.shape, rc.dtype), pltpu.HBM(rco.shape, rco.dtype),
                   jax.ShapeDtypeStruct((8, LANE), F32)],
        input_output_aliases={0: 2, 1: 3, 2: 4, 3: 5},
        compiler_params=pltpu.CompilerParams(has_side_effects=pltpu.SideEffectType.DATAFLOW_SIDE_EFFECTING),
    )(hbm(sb), hbm(sbo), hbm(rc), hbm(rco))


def grad_chip_wait(send, recv, sb, sbo, rc, rco, after):
    def body(sb_ref, sbo_ref, rc_ref, rco_ref, send, recv, after_ref, sb_o, sbo_o, rc_o, rco_o):
        del after_ref, sb_o, sbo_o, rc_o, rco_o
        for cp in _chip_copies(sb_ref, sbo_ref, rc_ref, rco_ref, send, recv):
            cp.wait_send()
            cp.wait_recv()

    return pl.pallas_call(
        body, name="grad_chip_wait",
        in_specs=[_HBM] * 4 + [_SEM, _SEM, pl.BlockSpec(memory_space=pl.ANY)],
        out_specs=[_HBM] * 4,
        out_shape=[pltpu.HBM(sb.shape, sb.dtype), pltpu.HBM(sbo.shape, sbo.dtype),
                   pltpu.HBM(rc.shape, rc.dtype), pltpu.HBM(rco.shape, rco.dtype)],
        input_output_aliases={0: 0, 1: 1, 2: 2, 3: 3},
        compiler_params=pltpu.CompilerParams(has_side_effects=pltpu.SideEffectType.DATAFLOW_SIDE_EFFECTING),
    )(sb, sbo, rc, rco, send, recv, after)


def pack_gather(pack):
    def body(pack_ref, packs_ref, psend, precv):
        pos = _mesh_pos()
        me = _lin(pos)
        packs_ref[me] = pack_ref[...]
        peers = [_xor_peer(pos, k) for k in range(1, N_DEV)]
        gather = [_remote(packs_ref.at[me], packs_ref.at[me], psend.at[n], precv.at[n], p) for n, p in enumerate(peers)]
        for cp in gather:
            cp.start()
        for n, p in enumerate(peers):
            _remote(packs_ref.at[_lin(p)], packs_ref.at[_lin(p)], psend.at[n], precv.at[n], p).wait_recv()
        for cp in gather:
            cp.wait_send()

    vmem = pl.BlockSpec(memory_space=pltpu.VMEM)
    return pl.pallas_call(
        body, name="pack_gather", in_specs=[vmem], out_specs=vmem,
        out_shape=jax.ShapeDtypeStruct((N_DEV,) + pack.shape, F32),
        scratch_shapes=[pltpu.SemaphoreType.DMA((N_DEV - 1,)), pltpu.SemaphoreType.DMA((N_DEV - 1,))],
    )(pack)


def pack_rows(vec_mid, vec_ada, dlb):
    def body(mid_ref, ada_ref, dlb_ref, o_ref):
        mid = lambda r: mid_ref[r:r + 1, :]
        rows = [ada_ref[0:1, :], dlb_ref[...], mid(MID_HG_G), mid(MID_RET_G), mid(MID_FINAL_G),
                ada_ref[2:3, :], ada_ref[1:2, :], mid(MID_GATE), mid(MID_LOSS)]
        o_ref[...] = jnp.zeros_like(o_ref)
        for n, row in enumerate(rows):
            o_ref[n:n + 1, :] = row

    vmem = pl.BlockSpec(memory_space=pltpu.VMEM)
    return pl.pallas_call(body, name="pack_rows", in_specs=[vmem] * 3, out_specs=vmem,
                          out_shape=jax.ShapeDtypeStruct((PACK_ROWS, D_MODEL), F32))(vec_mid, vec_ada, dlb)


def _adamw(w, g, m, v):
    m = ADAM_B1 * m + (1.0 - ADAM_B1) * g
    v = ADAM_B2 * v + (1.0 - ADAM_B2) * (g * g)
    m_hat = m / (1.0 - ADAM_B1 ** ADAM_STEP)
    v_hat = v / (1.0 - ADAM_B2 ** ADAM_STEP)
    delta = -ADAM_LR * (m_hat / (jnp.sqrt(v_hat) + ADAM_EPS) + ADAM_WD * w)
    return delta, m, v


def adam_shard(chip_idx, own, parts, w, m, v, name):
    rows, cols = w.shape
    tr = min(rows, 128)

    def body(chip_ref, p0, p1, p2, p3, w_ref, m_ref, v_ref, g_ref, d_ref, nm_ref, nv_ref):
        del chip_ref
        g = ((p0[...].astype(F32) + p1[...].astype(F32)) + p2[...].astype(F32)) + p3[...].astype(F32)
        g_ref[...] = g
        d_ref[...], nm_ref[...], nv_ref[...] = _adamw(w_ref[...], g, m_ref[...], v_ref[...])

    part = lambda q: pl.BlockSpec((None, tr, cols), lambda i, chip, q=q: (q, i, 0))
    tile = pl.BlockSpec((tr, cols), lambda i, chip: (i, 0))
    return pl.pallas_call(
        body, name=name,
        grid_spec=pltpu.PrefetchScalarGridSpec(
            num_scalar_prefetch=1, grid=(rows // tr,),
            in_specs=[pl.BlockSpec((None, tr, cols), lambda i, chip: (chip[0], i, 0)), part(0), part(1), part(2),
                      tile, tile, tile],
            out_specs=[tile] * 4),
        out_shape=[jax.ShapeDtypeStruct(w.shape, F32)] * 4,
        compiler_params=_params(("arbitrary",)),
    )(chip_idx, own, parts, parts, parts, w, m, v)


def adam_ada(sc_t, dmod_all, me_idx, w, m, v):
    def body(me_ref, sc_ref, dm_ref, w_ref, m_ref, v_ref, g_ref, d_ref, nm_ref, nv_ref):
        del me_ref
        g = _dot_f32(sc_ref[...], dm_ref[...])
        g_ref[...] = g
        d_ref[...], nm_ref[...], nv_ref[...] = _adamw(w_ref[...], g, m_ref[...], v_ref[...])

    full = pl.BlockSpec(w.shape, lambda i, me: (0, 0))
    return pl.pallas_call(
        body, name="adam_ada",
        grid_spec=pltpu.PrefetchScalarGridSpec(
            num_scalar_prefetch=1, grid=(1,),
            in_specs=[pl.BlockSpec(sc_t.shape, lambda i, me: (0, 0)),
                      pl.BlockSpec((LANE, SHARD_ADA), lambda i, me: (0, me[0])), full, full, full],
            out_specs=[full] * 4),
        out_shape=[jax.ShapeDtypeStruct(w.shape, F32)] * 4,
        compiler_params=_params(("arbitrary",)),
    )(me_idx, sc_t, dmod_all, w, m, v)


def adam_vectors(packs, lb, params, ms, vs):
    n = len(params)

    def body(*refs):
        packs_ref, lb_ref = refs[0], refs[1]
        w_refs, m_refs, v_refs = refs[2:2 + n], refs[2 + n:2 + 2 * n], refs[2 + 2 * n:2 + 3 * n]
        loss_ref = refs[2 + 3 * n]
        outs = refs[3 + 3 * n:3 + 7 * n]
        tot_ref = refs[3 + 7 * n]
        tot = packs_ref[0]
        for d in range(1, N_DEV):
            tot = tot + packs_ref[d]
        tot_ref[...] = tot
        row = lambda r: tot_ref[r:r + 1, :]
        lbv = lb_ref[...]
        dl0 = row(ROW_LB) * lbv * (1.0 - lbv)
        grads = [[row(ROW_NORM_G)],
                 [jnp.concatenate([row(ROW_SHIFT), row(ROW_SCALE), row(ROW_GATE)], axis=1)],
                 [dl0, -dl0],
                 [row(ROW_HG_G)], [row(ROW_RET_G)], [row(ROW_FINAL_G)]]
        loss_ref[...] = tot_ref[ROW_LOSS:ROW_LOSS + 1, 0:LANE]
        for j, g_rows in enumerate(grads):
            for r, g in enumerate(g_rows):
                rs = slice(r, r + 1)
                d, nm, nv = _adamw(w_refs[j][rs, :], g, m_refs[j][rs, :], v_refs[j][rs, :])
                outs[4 * j][rs, :] = g
                outs[4 * j + 1][rs, :] = d
                outs[4 * j + 2][rs, :] = nm
                outs[4 * j + 3][rs, :] = nv

    vmem = pl.BlockSpec(memory_space=pltpu.VMEM)
    out_shape = [jax.ShapeDtypeStruct((1, LANE), F32)]
    for w in params:
        out_shape += [jax.ShapeDtypeStruct(w.shape, F32)] * 4
    return pl.pallas_call(
        body, name="adam_vectors", in_specs=[vmem] * (2 + 3 * n), out_specs=[vmem] * len(out_shape),
        out_shape=out_shape, scratch_shapes=[pltpu.VMEM((PACK_ROWS, D_MODEL), F32)],
    )(packs, lb, *params, *ms, *vs)


def kernel(x, c, norm_g, w_ada, b_ada, w_in, hg_lb_logits, hg_norm_g, ret_norm_g, w_out, final_g, loss_target, m_norm_g, m_w_ada, m_b_ada, m_w_in, m_hg_lb_logits, m_hg_norm_g, m_ret_norm_g, m_w_out, m_final_g, v_norm_g, v_w_ada, v_b_ada, v_w_in, v_hg_lb_logits, v_hg_norm_g, v_ret_norm_g, v_w_out, v_final_g):
    pos = _mesh_pos()
    me_idx = jnp.reshape(_lin(pos), (1,)).astype(jnp.int32)
    c_idx = jnp.reshape(pos[2], (1,)).astype(jnp.int32)
    vec = lambda a: a.reshape(1, D_MODEL)

    mod, scall, lb = pre_exchange(c, w_ada[0], b_ada, hg_lb_logits)
    wtg, woutg = weight_gather(w_in[0].T.astype(_BF), w_out[0].astype(_BF))
    chip_idx = jnp.reshape(2 * pos[0] + pos[1], (1,)).astype(jnp.int32)

    def start_exchange(dwin, dwin_sib, dwout):
        dwout = dwout.reshape(N_DEV, SHARD_OUT, D_MODEL)
        ra, rb = grad_pair_exchange(dwin_sib, dwout)
        sb, sbo = pair_sum(dwin, ra, dwout, rb, c_idx)
        send, recv, sb, sbo, rc, rco, token = grad_chip_start(sb, sbo)
        return token, (send, recv, sb, sbo, rc, rco)

    grad_x, _, _, vec_mid, vec_ada, dlb, pending = device_step(
        x[0], loss_target[0], mod, lb, wtg.reshape(D_IN, D_MODEL), woutg.reshape(D_MODEL, D_MODEL), norm_g,
        hg_norm_g, ret_norm_g, vec(final_g), c_idx, start_exchange)
    packs = pack_gather(pack_rows(vec_mid, vec_ada, dlb))
    dmod_all = packs[:, ROW_SHIFT:ROW_GATE + 1, :].reshape(N_DEV, 3 * D_MODEL)
    dmod_all = jnp.pad(dmod_all, ((0, LANE - N_DEV), (0, 0)))
    sc_t = jnp.pad(scall.T, ((0, 0), (0, LANE - N_DEV)))
    g_ada, d_ada, nm_ada, nv_ada = adam_ada(sc_t, dmod_all, me_idx, w_ada[0], m_w_ada[0], v_w_ada[0])
    small = adam_vectors(
        packs, lb,
        (norm_g, b_ada, hg_lb_logits, hg_norm_g, ret_norm_g, vec(final_g)),
        (m_norm_g, m_b_ada, m_hg_lb_logits, m_hg_norm_g, m_ret_norm_g, vec(m_final_g)),
        (v_norm_g, v_b_ada, v_hg_lb_logits, v_hg_norm_g, v_ret_norm_g, vec(v_final_g)))
    loss = small[0][0, 0]
    sb, sbo, rc, rco = grad_chip_wait(*pending, small[0])
    g_in, d_in, nm_in, nv_in = adam_shard(chip_idx, sb, rc, w_in[0], m_w_in[0], v_w_in[0], "adam_w_in")
    g_out, d_out, nm_out, nv_out = adam_shard(chip_idx, sbo, rco, w_out[0], m_w_out[0], v_w_out[0], "adam_w_out")
    (g_ng, d_ng, nm_ng, nv_ng), (g_b, d_b, nm_b, nv_b), (g_lb, d_lb, nm_lb, nv_lb), (g_hg, d_hg, nm_hg, nv_hg), \
        (g_rg, d_rg, nm_rg, nv_rg), (g_fg, d_fg, nm_fg, nv_fg) = [small[1 + 4 * j:5 + 4 * j] for j in range(6)]
    flat = lambda a: a.reshape(D_MODEL)

    def group(ng, ada, b, win, lbl, hg, rg, wo, fg):
        return (ng, ada[None], b, win[None], lbl, hg, rg, wo[None], flat(fg))

    return (loss, grad_x[None],
            *group(g_ng, g_ada, g_b, g_in, g_lb, g_hg, g_rg, g_out, g_fg),
            *group(d_ng, d_ada, d_b, d_in, d_lb, d_hg, d_rg, d_out, d_fg),
            *group(nm_ng, nm_ada, nm_b, nm_in, nm_lb, nm_hg, nm_rg, nm_out, nm_fg),
            *group(nv_ng, nv_ada, nv_b, nv_in, nv_lb, nv_hg, nv_rg, nv_out, nv_fg))
```

```python
import functools

import numpy as np
import jax
import jax.numpy as jnp
from jax import lax
from jax.experimental import pallas as pl
from jax.experimental.pallas import tpu as pltpu

F32 = jnp.float32
_BF = jnp.bfloat16

D_MODEL = 1024
N_HEADS = 8
LANE = 128
RET_DK = 64
D_IN = 9216
N_DEV = 8
SHARD_IN = D_IN // N_DEV
SHARD_ADA = 3 * D_MODEL // N_DEV
SHARD_OUT = D_MODEL // N_DEV
N_CB = D_IN // LANE
CB_PER_SHARD = SHARD_IN // LANE
CHUNK = 128
N_LEVELS = 7
EPS = 1e-6
LOG2_E = float(np.log2(np.e))
ROPE_BASE = 10000.0
CB_HQ, CB_HF, CB_HI, CB_HZ, CB_RQ, CB_RK, CB_RV, CB_RZ, CB_GA, CB_GB = 0, 8, 16, 24, 32, 36, 40, 48, 56, 64
VMEM_LIMIT = 56 * 1024 * 1024

ADAM_LR, ADAM_B1, ADAM_B2, ADAM_EPS, ADAM_WD, ADAM_STEP = 0.001, 0.9, 0.999, 1e-08, 0.01, 10

_NN = (((1,), (0,)), ((), ()))
_NT = (((1,), (1,)), ((), ()))
_TN = (((0,), (0,)), ((), ()))
MESH = pl.DeviceIdType.MESH


def _dot(a, b, dims=_NN):
    return lax.dot_general(a.astype(_BF), b.astype(_BF), dims, preferred_element_type=F32)


def _split2(a):
    hi = a.astype(_BF)
    lo = (a - hi.astype(F32)).astype(_BF)
    return jnp.concatenate([hi, lo], axis=1)


def _dot_sel(sel, a):
    n = a.shape[1]
    r = lax.dot_general(sel.astype(_BF), _split2(a), _NN, preferred_element_type=F32)
    return r[:, :n] + r[:, n:]


def _dot_f32(a, b):
    def pieces(v):
        p1 = v.astype(_BF)
        r1 = v - p1.astype(F32)
        p2 = r1.astype(_BF)
        p3 = (r1 - p2.astype(F32)).astype(_BF)
        return p1, p2, p3
    a1, a2, a3 = pieces(a)
    b1, b2, b3 = pieces(b)
    d = lambda u, v: lax.dot_general(u, v, _NN, preferred_element_type=F32)
    return ((d(a1, b3) + d(a2, b2) + d(a3, b1)) + (d(a1, b2) + d(a2, b1))) + d(a1, b1)


def _sigmoid(v):
    return 1.0 / (1.0 + jnp.exp(-v))


def _params(sem=None):
    return pltpu.CompilerParams(dimension_semantics=sem, vmem_limit_bytes=VMEM_LIMIT)


def _hgrn_consts():
    c, nl = CHUNK, N_LEVELS
    t = np.arange(c)[:, None]
    j = np.arange(c)[None, :]
    sel = [j <= t]
    masks = [j == t]
    for l in range(1, nl + 1):
        m = ((t >> l) << l) + (1 << (l - 1)) - 1
        sec = t > m
        sel.append(np.where(sec, (j > m) & (j <= t), (j > t) & (j <= m)))
        same = (t >> l) == (j >> l)
        masks.append(same & sec & (j <= m))
    sel.append(j > t)
    sel = np.concatenate(sel, 0).astype(np.float32)
    masks = np.stack(masks).astype(np.float32)
    sgn = np.stack([np.where((t & (1 << (l - 1))) != 0, 1.0, -1.0) * np.ones((1, LANE)) for l in range(3, nl + 1)])
    return dict(tri=jnp.asarray(sel[:c], _BF),
                lvl=jnp.asarray(masks, F32),
                sgn=jnp.asarray(sgn, F32),
                sel_t=jnp.asarray(sel.T, _BF),
                lvl_b=jnp.asarray(masks, _BF),
                lvlt_b=jnp.asarray(np.swapaxes(masks, 1, 2), _BF))


def _level_exponents(b, logf, b_scr, sgn_ref):
    c = CHUNK
    b_scr[...] = b
    row = lax.broadcasted_iota(jnp.int32, (c, LANE), 0)
    nxt = pltpu.roll(logf, c - 1, 0)
    prv = pltpu.roll(logf, 1, 0)
    r4 = row & 3
    out = [jnp.where((row & 1) == 1, logf, 0.0),
           jnp.where(r4 == 0, nxt, jnp.where(r4 == 1, 0.0, jnp.where(r4 == 2, logf, logf + prv)))]
    for l in range(3, N_LEVELS + 1):
        size, half = 1 << l, 1 << (l - 1)
        ref = jnp.concatenate([jnp.broadcast_to(b_scr[i * size + half - 1:i * size + half, :], (size, LANE))
                               for i in range(c // size)], axis=0)
        out.append((b - ref) * sgn_ref[l - 3])
    return out


def _hgrn_chunk(hq, hf, hi, lbv, tri_ref, sgn_ref, b_scr):
    sq = _sigmoid(hq)
    q = hq * sq
    sg = _sigmoid(hf)
    omlb = 1.0 - lbv
    f = lbv + omlb * sg
    k = 1.0 - f
    logf = jnp.log(f) * LOG2_E
    b = _dot_sel(tri_ref[...], logf)
    bc = jnp.sum(logf, axis=0, keepdims=True)
    lev = [None] + [jnp.exp2(e) for e in _level_exponents(b, logf, b_scr, sgn_ref)]
    return dict(sq=sq, q=q, sg=sg, omlb=omlb, f=f, k=k, v=hi, eb=jnp.exp2(b), erem=jnp.exp2(bc - b),
                ebc=jnp.exp2(bc), lev=lev)


def _blockdiag(a, b):
    z = jnp.zeros_like(a)
    return jnp.concatenate([jnp.concatenate([a, z], axis=1), jnp.concatenate([z, b], axis=1)], axis=0)


def _level_operands(a):
    q, k = a["q"].astype(_BF), a["k"].astype(_BF)
    lev = [None] + [a["lev"][l].astype(_BF) for l in range(1, N_LEVELS + 1)]
    ql = [q] + [q * lev[l] for l in range(1, N_LEVELS + 1)]
    kl = [k] + [k * lev[l] for l in range(1, N_LEVELS + 1)]
    pairs = range(0, N_LEVELS + 1, 2)
    return ([jnp.concatenate([ql[l], ql[l + 1]], axis=1) for l in pairs], [_blockdiag(kl[l], kl[l + 1]) for l in pairs],
            ql, kl)


def _hgrn_scores(a, lvl_ref, q_pairs, k_diags):
    acc = None
    for n, (qp, kd) in enumerate(zip(q_pairs, k_diags)):
        both = lax.dot_general(qp, kd, _NT, preferred_element_type=F32)
        part = lvl_ref[2 * n] * both[:, :CHUNK] + lvl_ref[2 * n + 1] * both[:, CHUNK:]
        acc = part if acc is None else acc + part
    return acc


SCAN_UNROLL = 4
RET_UNROLL = 8


def _writeback_reserve(step, make_copies):
    slot = step % 2

    @pl.when(step >= 2)
    def _():
        for cp in make_copies(slot):
            cp.wait()

    return slot


def _writeback_commit(step, n_steps, slot, make_copies):
    for cp in make_copies(slot):
        cp.start()

    @pl.when(step == n_steps - 1)
    def _():
        for cp in make_copies(slot):
            cp.wait()
        if n_steps > 1:
            for cp in make_copies(1 - slot):
                cp.wait()


def _resident(const):
    zeros = (0,) * const.ndim
    return pl.BlockSpec(const.shape, lambda p, t: zeros)


def _time_block(t_len):
    return min(t_len, 2048)


def hgrn_forward(pb, lb, t_len):
    nc = t_len // CHUNK
    tb = _time_block(t_len)
    ncb = tb // CHUNK
    consts = _hgrn_consts()
    operands = [consts[n] for n in ("tri", "lvl", "sgn")]

    def body(hq_ref, hf_ref, hi_ref, lb_ref, tri_ref, lvl_ref, sgn_ref, o_ref, ssave_ref, asave_ref, st_ref, b_scr):
        @pl.when(pl.program_id(1) == 0)
        def _():
            st_ref[...] = jnp.zeros_like(st_ref)

        def chunk(ci, carry):
            r = pl.ds(pl.multiple_of(ci * CHUNK, CHUNK), CHUNK)
            for hd in range(2):
                lbv = lb_ref[:, hd * LANE:(hd + 1) * LANE]
                a = _hgrn_chunk(hq_ref[hd, r, :], hf_ref[hd, r, :], hi_ref[hd, r, :], lbv, tri_ref, sgn_ref,
                                b_scr.at[hd])
                q_pairs, k_diags, _, _ = _level_operands(a)
                st = st_ref[hd]
                ssave_ref[hd, ci] = st
                scores = _hgrn_scores(a, lvl_ref, q_pairs, k_diags).astype(asave_ref.dtype)
                asave_ref[hd, ci] = scores
                o_ref[hd, r, :] = _dot(a["q"] * a["eb"], st, _NT) + _dot(scores, a["v"])
                st_ref[hd] = st * a["ebc"] + _dot(a["v"], a["k"] * a["erem"], _TN)
            return carry

        lax.fori_loop(0, ncb, chunk, 0, unroll=SCAN_UNROLL)

    pair = lambda base: pl.BlockSpec((2, tb, LANE), lambda p, t, base=base: (base // 2 + p, t, 0))
    per_chunk = pl.BlockSpec((2, ncb, LANE, LANE), lambda p, t: (p, t, 0, 0))
    return pl.pallas_call(
        body, name="hgrn_fwd", grid=(N_HEADS // 2, t_len // tb),
        in_specs=[pair(CB_HQ), pair(CB_HF), pair(CB_HI),
                  pl.BlockSpec((1, 2 * LANE), lambda p, t: (0, p))] + [_resident(c) for c in operands],
        out_specs=[pl.BlockSpec((2, tb, LANE), lambda p, t: (p, t, 0)), per_chunk, per_chunk],
        out_shape=[jax.ShapeDtypeStruct((N_HEADS, t_len, LANE), F32),
                   jax.ShapeDtypeStruct((N_HEADS, nc, LANE, LANE), F32),
                   jax.ShapeDtypeStruct((N_HEADS, nc, CHUNK, CHUNK), _BF)],
        scratch_shapes=[pltpu.VMEM((2, LANE, LANE), F32), pltpu.VMEM((2, CHUNK, LANE), F32)],
        compiler_params=_params(("arbitrary", "arbitrary")),
    )(pb, pb, pb, lb, *operands)


def hgrn_backward(pb, lb, do, ssave, asave, dpb, t_len):
    tb = _time_block(t_len)
    ncb, ntb = tb // CHUNK, t_len // tb
    consts = _hgrn_consts()
    operands = [consts[n] for n in ("tri", "sgn", "sel_t", "lvl_b", "lvlt_b")]

    def body(hq_ref, hf_ref, hi_ref, lb_ref, do_ref, ssave_ref, asave_ref, tri_ref, sgn_ref, selt_ref, lvlb_ref,
             lvltb_ref, dpb_in, dpb_ref, dlb_ref, dq_buf, df_buf, di_buf, dst_ref, b_scr, sems):
        del dpb_in
        p, t = pl.program_id(0), pl.program_id(1)
        step = p * ntb + t
        rows = pl.ds(pl.multiple_of((ntb - 1 - t) * tb, tb), tb)

        def out_copies(sl):
            return [pltpu.make_async_copy(buf.at[sl], dpb_ref.at[pl.ds(base + 2 * p, 2), rows], sems.at[sl, n])
                    for n, (buf, base) in enumerate(((dq_buf, CB_HQ), (df_buf, CB_HF), (di_buf, CB_HI)))]

        slot = _writeback_reserve(step, out_copies)

        @pl.when(t == 0)
        def _():
            dst_ref[...] = jnp.zeros_like(dst_ref)
            dlb_ref[...] = jnp.zeros_like(dlb_ref)

        def chunk(i, carry):
            ci = ncb - 1 - i
            r = pl.ds(pl.multiple_of(ci * CHUNK, CHUNK), CHUNK)
            for hd in range(2):
                head_chunk(hd, ci, r)
            return carry

        def head_chunk(hd, ci, r):
            lbv = lb_ref[:, hd * LANE:(hd + 1) * LANE]
            hq = hq_ref[hd, r, :]
            a = _hgrn_chunk(hq, hf_ref[hd, r, :], hi_ref[hd, r, :], lbv, tri_ref, sgn_ref, b_scr.at[hd])
            _, k_diags, ql, kl = _level_operands(a)
            q, k, v = a["q"], a["k"], a["v"]
            g = do_ref[hd, r, :]
            st0 = ssave_ref[hd, ci]
            dst = dst_ref[hd]
            scores = asave_ref[hd, ci]
            da = _dot(g, v, _NT)
            da_t = _dot(v, g, _NT)
            kb = k * a["erem"]
            qb = q * a["eb"]
            dv = _dot(scores, g, _TN) + _dot(kb, dst, _NT)
            dq_inter = _dot(g, st0) * a["eb"]
            dk_state = _dot(v, dst) * a["erem"]
            dq, dk = dq_inter, dk_state
            de = [q * dq_inter]
            da_b, dat_b = da.astype(_BF), da_t.astype(_BF)
            for n in range(len(k_diags)):
                l0, l1 = 2 * n, 2 * n + 1
                da_pair = jnp.concatenate([lvlb_ref[l0] * da_b, lvlb_ref[l1] * da_b], axis=1)
                dat_pair = jnp.concatenate([lvltb_ref[l0] * dat_b, lvltb_ref[l1] * dat_b], axis=1)
                dq_both = lax.dot_general(da_pair, k_diags[n], _NN, preferred_element_type=F32)
                dk_both = lax.dot_general(dat_pair, _blockdiag(ql[l0], ql[l1]), _NN, preferred_element_type=F32)
                for l, cols in ((l0, slice(0, LANE)), (l1, slice(LANE, 2 * LANE))):
                    dql, dkl = dq_both[:, cols], dk_both[:, cols]
                    if l > 0:
                        e = a["lev"][l]
                        dql, dkl = dql * e, dkl * e
                        de.append(q * dql + k * dkl)
                    dq = dq + dql
                    dk = dk + dkl
            de.append(k * dk_state)
            dst_ref[hd] = dst * a["ebc"] + _dot(g, qb, _TN)
            dbc = jnp.sum(dst * st0, axis=0, keepdims=True) * a["ebc"]
            de2 = lax.dot_general(selt_ref[...], _split2(jnp.concatenate(de, axis=0)), _NN,
                                  preferred_element_type=F32)
            dlogf = de2[:, :LANE] + de2[:, LANE:] + dbc
            sq, sg = a["sq"], a["sg"]
            df = dlogf / a["f"] - dk
            dq_buf[slot, hd, r, :] = (dq * (sq * (1.0 + hq * (1.0 - sq)))).astype(dq_buf.dtype)
            df_buf[slot, hd, r, :] = (df * a["omlb"] * sg * (1.0 - sg)).astype(df_buf.dtype)
            di_buf[slot, hd, r, :] = dv.astype(di_buf.dtype)
            cols = slice(hd * LANE, (hd + 1) * LANE)
            dlb_ref[:, cols] = dlb_ref[:, cols] + jnp.sum(df * (1.0 - sg), axis=0, keepdims=True)

        lax.fori_loop(0, ncb, chunk, 0, unroll=SCAN_UNROLL)
        _writeback_commit(step, (N_HEADS // 2) * ntb, slot, out_copies)

    pair = lambda base: pl.BlockSpec((2, tb, LANE), lambda p, t, base=base: (base // 2 + p, ntb - 1 - t, 0))
    any_spec = pl.BlockSpec(memory_space=pl.ANY)
    per_chunk = pl.BlockSpec((2, ncb, LANE, LANE), lambda p, t: (p, ntb - 1 - t, 0, 0))
    return pl.pallas_call(
        body, name="hgrn_bwd", grid=(N_HEADS // 2, ntb),
        in_specs=[pair(CB_HQ), pair(CB_HF), pair(CB_HI),
                  pl.BlockSpec((1, 2 * LANE), lambda p, t: (0, p)),
                  pair(0), per_chunk, per_chunk]
        + [_resident(c) for c in operands] + [any_spec],
        out_specs=[any_spec, pl.BlockSpec((1, 2 * LANE), lambda p, t: (0, p))],
        out_shape=[jax.ShapeDtypeStruct(dpb.shape, dpb.dtype), jax.ShapeDtypeStruct((1, D_MODEL), F32)],
        scratch_shapes=[pltpu.VMEM((2, 2, tb, LANE), dpb.dtype)] * 3 + [
            pltpu.VMEM((2, LANE, LANE), F32), pltpu.VMEM((2, CHUNK, LANE), F32), pltpu.SemaphoreType.DMA((2, 3))],
        input_output_aliases={7 + len(operands): 0},
        compiler_params=_params(("arbitrary", "arbitrary")),
    )(pb, pb, pb, lb, do, ssave, asave, *operands, dpb)


def _rope_tables(t_len):
    half = RET_DK // 2
    inv_freq = 1.0 / (ROPE_BASE ** jnp.linspace(0.0, 1.0, half, dtype=F32))
    ang = jnp.arange(t_len, dtype=jnp.int32).astype(F32)[:, None] * inv_freq[None, :]
    cos, sin = jnp.cos(ang), jnp.sin(ang)
    cos_t = jnp.concatenate([cos, cos, cos, cos], axis=1)
    sin_t = jnp.concatenate([-sin, sin, -sin, sin], axis=1)
    return cos_t, sin_t


def _swap_halves(v):
    half = RET_DK // 2
    lane = lax.broadcasted_iota(jnp.int32, v.shape, 1)
    first = (lane & (RET_DK - 1)) < half
    return jnp.where(first, pltpu.roll(v, LANE - half, 1), pltpu.roll(v, half, 1))


def _ret_head_consts(hidx):
    c = CHUNK
    hf = jnp.full((1, LANE), hidx, jnp.int32).astype(F32)
    lg = jnp.log(1.0 - jnp.exp(-(5.0 + hf) * np.float32(np.log(2.0))))
    row = lax.broadcasted_iota(jnp.int32, (c, c), 0)
    col = lax.broadcasted_iota(jnp.int32, (c, c), 1)
    rel = (row - col).astype(F32)
    dm = jnp.where(rel >= 0, jnp.exp(lg[:, :1] * jnp.maximum(rel, 0.0)), 0.0)
    dm_t = jnp.where(rel <= 0, jnp.exp(lg[:, :1] * jnp.maximum(-rel, 0.0)), 0.0)
    idx = lax.broadcasted_iota(jnp.int32, (c, LANE), 0).astype(F32)
    zeta = jnp.exp(lg * (c - 1.0 - idx))
    xi = jnp.exp(lg * (idx + 1.0))
    cdec = jnp.exp(lg * float(c))
    return dm, zeta, xi, cdec, dm_t


def _lane_mask(which):
    lane = lax.broadcasted_iota(jnp.int32, (1, LANE), 1)
    return ((lane // RET_DK) == which).astype(F32)


def retention_forward(pb, cos_t, sin_t, t_len):
    nc = t_len // CHUNK

    tb = _time_block(t_len)
    ncb = tb // CHUNK

    def body(rq_ref, rk_ref, rv_ref, cos_ref, sin_ref, o_ref, rsave_ref, st_ref):
        p = pl.program_id(0)

        @pl.when(pl.program_id(1) == 0)
        def _():
            st_ref[...] = jnp.zeros_like(st_ref)

        consts = [_ret_head_consts(2 * p + hd) for hd in range(2)]

        def chunk(ci, carry):
            r = pl.ds(pl.multiple_of(ci * CHUNK, CHUNK), CHUNK)
            cs, sn = cos_ref[r, :], sin_ref[r, :]
            q = rq_ref[r, :]
            k = rk_ref[r, :]
            q = q * cs + _swap_halves(q) * sn
            k = (k * cs + _swap_halves(k) * sn) * RET_DK ** -0.5
            for hd in range(2):
                dm, zeta, xi, cdec, _ = consts[hd]
                lm = _lane_mask(hd)
                qh, kh = q * lm, k * lm
                v = rv_ref[hd, r, :]
                st = st_ref[hd]
                rsave_ref[hd, ci] = st
                scores = _dot(qh, kh, _NT) * dm
                o_ref[hd, r, :] = _dot(scores, v) + _dot(qh * xi, st, _NT)
                st_ref[hd] = st * cdec + _dot(v, kh * zeta, _TN)
            return carry

        lax.fori_loop(0, ncb, chunk, 0, unroll=RET_UNROLL)

    return pl.pallas_call(
        body, name="ret_fwd", grid=(N_HEADS // 2, t_len // tb),
        in_specs=[pl.BlockSpec((None, tb, LANE), lambda p, t: (CB_RQ + p, t, 0)),
                  pl.BlockSpec((None, tb, LANE), lambda p, t: (CB_RK + p, t, 0)),
                  pl.BlockSpec((2, tb, LANE), lambda p, t: (CB_RV // 2 + p, t, 0)),
                  pl.BlockSpec((tb, LANE), lambda p, t: (t, 0)),
                  pl.BlockSpec((tb, LANE), lambda p, t: (t, 0))],
        out_specs=[pl.BlockSpec((2, tb, LANE), lambda p, t: (p, t, 0)),
                   pl.BlockSpec((2, ncb, LANE, LANE), lambda p, t: (p, t, 0, 0))],
        out_shape=[jax.ShapeDtypeStruct((N_HEADS, t_len, LANE), F32),
                   jax.ShapeDtypeStruct((N_HEADS, nc, LANE, LANE), F32)],
        scratch_shapes=[pltpu.VMEM((2, LANE, LANE), F32)],
        compiler_params=_params(("arbitrary", "arbitrary")),
    )(pb, pb, pb, cos_t, sin_t)


def retention_backward(pb, cos_t, sin_t, do, rsave, dpb, t_len):
    tb = _time_block(t_len)
    ncb, ntb = tb // CHUNK, t_len // tb

    def body(rq_ref, rk_ref, rv_ref, cos_ref, sin_ref, do_ref, rsave_ref, dpb_in,
             dpb_ref, dq_buf, dk_buf, dv_buf, dst_ref, sems):
        del dpb_in
        p, t = pl.program_id(0), pl.program_id(1)
        step = p * ntb + t
        rows = pl.ds(pl.multiple_of((ntb - 1 - t) * tb, tb), tb)

        def out_copies(sl):
            return [pltpu.make_async_copy(dq_buf.at[sl], dpb_ref.at[CB_RQ + p, rows], sems.at[sl, 0]),
                    pltpu.make_async_copy(dk_buf.at[sl], dpb_ref.at[CB_RK + p, rows], sems.at[sl, 1]),
                    pltpu.make_async_copy(dv_buf.at[sl], dpb_ref.at[pl.ds(CB_RV + 2 * p, 2), rows], sems.at[sl, 2])]

        slot = _writeback_reserve(step, out_copies)

        @pl.when(t == 0)
        def _():
            dst_ref[...] = jnp.zeros_like(dst_ref)

        consts = [_ret_head_consts(2 * p + hd) for hd in range(2)]

        def chunk(i, carry):
            ci = ncb - 1 - i
            r = pl.ds(pl.multiple_of(ci * CHUNK, CHUNK), CHUNK)
            cs, sn = cos_ref[r, :], sin_ref[r, :]
            q = rq_ref[r, :]
            k = rk_ref[r, :]
            q = q * cs + _swap_halves(q) * sn
            k = (k * cs + _swap_halves(k) * sn) * RET_DK ** -0.5
            dq, dk = None, None
            for hd in range(2):
                dm, zeta, xi, cdec, dm_t = consts[hd]
                lm = _lane_mask(hd)
                qh, kh = q * lm, k * lm
                v = rv_ref[hd, r, :]
                g = do_ref[hd, r, :]
                st0 = rsave_ref[hd, ci]
                dst = dst_ref[hd]
                scores_t = _dot(kh, qh, _NT) * dm_t
                dsc = _dot(g, v, _NT) * dm
                dsc_t = _dot(v, g, _NT) * dm_t
                dqh = _dot(dsc, kh) + _dot(g, st0) * xi
                dkh = _dot(dsc_t, qh) + _dot(v, dst) * zeta
                dv_buf[slot, hd, r, :] = (_dot(scores_t, g) + _dot(kh * zeta, dst, _NT)).astype(dv_buf.dtype)
                dst_ref[hd] = dst * cdec + _dot(g, qh * xi, _TN)
                dq = dqh if dq is None else dq + dqh
                dk = dkh if dk is None else dk + dkh
            dk = dk * (RET_DK ** -0.5)
            dq_buf[slot, r, :] = (dq * cs - _swap_halves(dq) * sn).astype(dq_buf.dtype)
            dk_buf[slot, r, :] = (dk * cs - _swap_halves(dk) * sn).astype(dk_buf.dtype)
            return carry

        lax.fori_loop(0, ncb, chunk, 0, unroll=RET_UNROLL)
        _writeback_commit(step, (N_HEADS // 2) * ntb, slot, out_copies)

    any_spec = pl.BlockSpec(memory_space=pl.ANY)
    return pl.pallas_call(
        body, name="ret_bwd", grid=(N_HEADS // 2, ntb),
        in_specs=[pl.BlockSpec((None, tb, LANE), lambda p, t: (CB_RQ + p, ntb - 1 - t, 0)),
                  pl.BlockSpec((None, tb, LANE), lambda p, t: (CB_RK + p, ntb - 1 - t, 0)),
                  pl.BlockSpec((2, tb, LANE), lambda p, t: (CB_RV // 2 + p, ntb - 1 - t, 0)),
                  pl.BlockSpec((tb, LANE), lambda p, t: (ntb - 1 - t, 0)),
                  pl.BlockSpec((tb, LANE), lambda p, t: (ntb - 1 - t, 0)),
                  pl.BlockSpec((2, tb, LANE), lambda p, t: (p, ntb - 1 - t, 0)),
                  pl.BlockSpec((2, ncb, LANE, LANE), lambda p, t: (p, ntb - 1 - t, 0, 0)),
                  any_spec],
        out_specs=any_spec,
        out_shape=jax.ShapeDtypeStruct(dpb.shape, dpb.dtype),
        scratch_shapes=[pltpu.VMEM((2, tb, LANE), dpb.dtype), pltpu.VMEM((2, tb, LANE), dpb.dtype),
                        pltpu.VMEM((2, 2, tb, LANE), dpb.dtype), pltpu.VMEM((2, LANE, LANE), F32),
                        pltpu.SemaphoreType.DMA((2, 3))],
        input_output_aliases={7: 0},
        compiler_params=_params(("arbitrary", "arbitrary")),
    )(pb, pb, pb, cos_t, sin_t, do, rsave, dpb)


def _row_tile(t_len, want):
    return min(want, t_len)


PAIR_CB = 2 * CB_PER_SHARD


def proj_forward(h, wt, t_len):
    tm = _row_tile(t_len, 1024)

    def body(h_ref, w_ref, o_ref):
        acc = _dot(h_ref[...], w_ref[...], _NT)
        for jj in range(PAIR_CB):
            o_ref[jj] = acc[:, jj * LANE:(jj + 1) * LANE]

    return pl.pallas_call(
        body, name="proj_fwd", grid=(N_DEV // 2, t_len // tm),
        in_specs=[pl.BlockSpec((tm, D_MODEL), lambda j, i: (i, 0)),
                  pl.BlockSpec((PAIR_CB * LANE, D_MODEL), lambda j, i: (j, 0))],
        out_specs=pl.BlockSpec((PAIR_CB, tm, LANE), lambda j, i: (j, i, 0)),
        out_shape=jax.ShapeDtypeStruct((N_CB, t_len, LANE), F32),
        compiler_params=_params(("arbitrary", "arbitrary")),
    )(h, wt)


def proj_backward_input(dpb, wt, token, x, dy, norm_g, scale1p, t_len):
    tm = _row_tile(t_len, 256)

    def body(a_ref, wt_hbm, token_ref, x_ref, dy_ref, g_ref, sc_ref, gx_ref, vec_ref, w_ref, sem):
        del token_ref
        i = pl.program_id(0)

        @pl.when(i == 0)
        def _():
            cp = pltpu.make_async_copy(wt_hbm, w_ref, sem)
            cp.start()
            cp.wait()

        a = jnp.concatenate([a_ref[jj].astype(_BF) for jj in range(N_CB)], axis=1)
        dhv = _dot(a, w_ref[...])
        xv, g, sc = x_ref[...], g_ref[...], sc_ref[...]
        r = lax.rsqrt(jnp.mean(xv * xv, axis=-1, keepdims=True) + EPS)
        xn = xv * r
        dxn = dhv * (g * sc)
        gx_ref[...] = dy_ref[...] + r * dxn - xn * (r * r) * jnp.mean(xv * dxn, axis=-1, keepdims=True)
        t = dhv * xn
        _acc_rows(vec_ref, i, [jnp.sum(t * sc, axis=0, keepdims=True),
                               jnp.sum(t * g, axis=0, keepdims=True),
                               jnp.sum(dhv, axis=0, keepdims=True)])

    row = pl.BlockSpec((tm, D_MODEL), lambda i: (i, 0))
    return pl.pallas_call(
        body, name="proj_bwd_input", grid=(t_len // tm,),
        in_specs=[pl.BlockSpec((N_CB, tm, LANE), lambda i: (0, i, 0)),
                  pl.BlockSpec(memory_space=pl.ANY),
                  pl.BlockSpec(token.shape, lambda i: (0, 0)),
                  row, row, _vec_spec(), _vec_spec()],
        out_specs=[row, pl.BlockSpec((8, D_MODEL), lambda i: (0, 0))],
        out_shape=[jax.ShapeDtypeStruct((t_len, D_MODEL), F32), jax.ShapeDtypeStruct((8, D_MODEL), F32)],
        scratch_shapes=[pltpu.VMEM(wt.shape, wt.dtype), pltpu.SemaphoreType.DMA],
        compiler_params=_params(("arbitrary",)),
    )(dpb, wt, token, x, dy, norm_g, scale1p)


def proj_backward_weight(h_t, dpb, t_len):
    tk = _row_tile(t_len, 2048)

    def body(h_ref, b_ref, o_ref):
        k = pl.program_id(1)
        b = jnp.concatenate([b_ref[jj].astype(_BF) for jj in range(PAIR_CB)], axis=1)
        part = _dot(h_ref[...], b)

        @pl.when(k == 0)
        def _():
            for s in range(2):
                o_ref[s] = part[:, s * SHARD_IN:(s + 1) * SHARD_IN]

        @pl.when(k > 0)
        def _():
            for s in range(2):
                o_ref[s] = o_ref[s] + part[:, s * SHARD_IN:(s + 1) * SHARD_IN]

    return pl.pallas_call(
        body, name="proj_bwd_weight", grid=(N_DEV // 2, t_len // tk),
        in_specs=[pl.BlockSpec((D_MODEL, tk), lambda j, k: (0, k)),
                  pl.BlockSpec((PAIR_CB, tk, LANE), lambda j, k: (j, k, 0))],
        out_specs=pl.BlockSpec((2, D_MODEL, SHARD_IN), lambda j, k: (j, 0, 0)),
        out_shape=jax.ShapeDtypeStruct((N_DEV, D_MODEL, SHARD_IN), F32),
        compiler_params=_params(("arbitrary", "arbitrary")),
    )(h_t, dpb)


def sibling_blocks(g_in, c_idx):
    tr = D_MODEL

    def body(c_ref, g_ref, o_ref):
        del c_ref
        o_ref[...] = g_ref[...].astype(o_ref.dtype)

    return pl.pallas_call(
        body, name="sibling_blocks",
        grid_spec=pltpu.PrefetchScalarGridSpec(
            num_scalar_prefetch=1, grid=(N_DEV // 2, D_MODEL // tr),
            in_specs=[pl.BlockSpec((None, tr, SHARD_IN), lambda q, i, c: (2 * q + 1 - c[0], i, 0))],
            out_specs=pl.BlockSpec((None, tr, SHARD_IN), lambda q, i, c: (q, i, 0))),
        out_shape=jax.ShapeDtypeStruct((N_DEV // 2, D_MODEL, SHARD_IN), _BF),
        compiler_params=_params(("arbitrary", "arbitrary")),
    )(c_idx, g_in)


def _vec_spec():
    return pl.BlockSpec((1, D_MODEL), lambda i: (0, 0))


def _acc_rows(ref, i, rows):
    @pl.when(i == 0)
    def _():
        ref[...] = jnp.zeros_like(ref)

    for n, row in enumerate(rows):
        ref[n:n + 1, :] = ref[n:n + 1, :] + row


def adaln_forward(x, norm_g, scale1p, shift, t_len):
    tm = _row_tile(t_len, 512)

    def body(x_ref, g_ref, sc_ref, sh_ref, h_ref, ht_ref):
        xv = x_ref[...]
        r = lax.rsqrt(jnp.mean(xv * xv, axis=-1, keepdims=True) + EPS)
        h = xv * r * g_ref[...] * sc_ref[...] + sh_ref[...]
        h_ref[...] = h.astype(h_ref.dtype)
        ht_ref[...] = h.T.astype(ht_ref.dtype)

    return pl.pallas_call(
        body, name="adaln_fwd", grid=(t_len // tm,),
        in_specs=[pl.BlockSpec((tm, D_MODEL), lambda i: (i, 0)), _vec_spec(), _vec_spec(), _vec_spec()],
        out_specs=[pl.BlockSpec((tm, D_MODEL), lambda i: (i, 0)), pl.BlockSpec((D_MODEL, tm), lambda i: (0, i))],
        out_shape=[jax.ShapeDtypeStruct((t_len, D_MODEL), _BF), jax.ShapeDtypeStruct((D_MODEL, t_len), _BF)],
        compiler_params=_params(("arbitrary",)),
    )(x, norm_g, scale1p, shift)


def _head_norm(o, g):
    r = lax.rsqrt(jnp.mean(o * o, axis=-1, keepdims=True) + EPS)
    return r, o * r * g


def _group_spec(tm, cb):
    return pl.BlockSpec((N_HEADS, tm, LANE), lambda i, cb=cb: (cb // N_HEADS, i, 0))


MID_FINAL_G, MID_GATE, MID_LOSS, MID_HG_G, MID_RET_G = range(5)


def middle(x, target, oa, ob, pb, wout, gate, final_g, hg_g, ret_g, t_len):
    tm = _row_tile(t_len, 256)
    n_steps = t_len // tm

    def body(x_ref, t_ref, oa_ref, ob_ref, hz_ref, rz_ref, ga_ref, gb_ref, w_ref, gate_ref, fg_ref, hg_ref, rg_ref,
             dy_ref, doa_ref, dob_ref, dw_ref, vec_ref, dpb_ref, m_scr, dm_scr, keep, bufs, sems):
        i = pl.program_id(0)
        rows = pl.ds(pl.multiple_of(i * tm, tm), tm)

        def group_copies(sl):
            return [pltpu.make_async_copy(bufs.at[sl, n], dpb_ref.at[pl.ds(cb, N_HEADS), rows], sems.at[sl, n])
                    for n, cb in enumerate((CB_HZ, CB_RZ, CB_GA, CB_GB))]
        sides = ((oa_ref, hz_ref, ga_ref, hg_ref, doa_ref), (ob_ref, rz_ref, gb_ref, rg_ref, dob_ref))
        for hh in range(N_HEADS):
            ls = slice(hh * LANE, (hh + 1) * LANE)
            acc = None
            for side, (o_ref, z_ref, gt_ref, g_ref, _) in enumerate(sides):
                o = o_ref[hh]
                rr = lax.rsqrt(jnp.mean(o * o, axis=-1, keepdims=True) + EPS)
                orr = o * rr
                zz = z_ref[hh]
                sz = _sigmoid(zz)
                sgt = _sigmoid(gt_ref[hh])
                keep[side, hh, 0] = orr
                keep[side, hh, 1] = sz
                keep[side, hh, 2] = sgt
                keep[side, hh, 3] = jnp.broadcast_to(rr, orr.shape)
                u = sgt * ((orr * g_ref[:, ls]) * (zz * sz))
                acc = u if acc is None else acc + u
            m_scr[:, ls] = acc.astype(m_scr.dtype)
        zv = _dot(m_scr[...], w_ref[...])
        gt, fg = gate_ref[...], fg_ref[...]
        y = x_ref[...] + gt * zv
        r = lax.rsqrt(jnp.mean(y * y, axis=-1, keepdims=True) + EPS)
        yn = y * r
        err = yn * fg - t_ref[...]
        loss = 0.5 * jnp.sum(jnp.mean(err * err, axis=-1, keepdims=True), axis=0, keepdims=True)
        dout = err * (1.0 / D_MODEL)
        gd = dout * fg
        dy = r * gd - yn * (r * r) * jnp.mean(y * gd, axis=-1, keepdims=True)
        dy_ref[...] = dy
        dz = (dy * gt).astype(_BF)
        dm_scr[...] = _dot(dz, w_ref[...], _NT)
        part = _dot(m_scr[...], dz, _TN)

        @pl.when(i == 0)
        def _():
            dw_ref[...] = part

        @pl.when(i > 0)
        def _():
            dw_ref[...] = dw_ref[...] + part

        slot = _writeback_reserve(i, group_copies)
        dg = [[], []]
        for hh in range(N_HEADS):
            ls = slice(hh * LANE, (hh + 1) * LANE)
            dmh = dm_scr[:, ls]
            for side, (o_ref, z_ref, gt_ref, g_ref, do_ref) in enumerate(sides):
                zz, g = z_ref[hh], g_ref[:, ls]
                orr, sz, sgt, rr = keep[side, hh, 0], keep[side, hh, 1], keep[side, hh, 2], keep[side, hh, 3]
                n = orr * g
                silu = zz * sz
                du = dmh * sgt
                bufs[slot, 2 + side, hh] = (dmh * (n * silu) * (sgt * (1.0 - sgt))).astype(bufs.dtype)
                bufs[slot, side, hh] = (du * n * (sz * (1.0 + zz * (1.0 - sz)))).astype(bufs.dtype)
                dn = du * silu
                dg[side].append(jnp.sum(dn * orr, axis=0, keepdims=True))
                gdn = dn * g
                do_ref[hh] = (rr * (gdn - orr * jnp.mean(orr * gdn, axis=-1, keepdims=True))).astype(do_ref.dtype)
        _acc_rows(vec_ref, i, [jnp.sum(dout * yn, axis=0, keepdims=True),
                               jnp.sum(dy * zv, axis=0, keepdims=True),
                               jnp.broadcast_to(loss, (1, D_MODEL)),
                               jnp.concatenate(dg[0], axis=1), jnp.concatenate(dg[1], axis=1)])
        _writeback_commit(i, n_steps, slot, group_copies)

    row = pl.BlockSpec((tm, D_MODEL), lambda i: (i, 0))
    head = pl.BlockSpec((N_HEADS, tm, LANE), lambda i: (0, i, 0))
    full = pl.BlockSpec((D_MODEL, D_MODEL), lambda i: (0, 0))
    return pl.pallas_call(
        body, name="middle", grid=(n_steps,),
        in_specs=[row, row, head, head, _group_spec(tm, CB_HZ), _group_spec(tm, CB_RZ), _group_spec(tm, CB_GA),
                  _group_spec(tm, CB_GB), full, _vec_spec(), _vec_spec(), _vec_spec(), _vec_spec()],
        out_specs=[row, head, head, full, pl.BlockSpec((8, D_MODEL), lambda i: (0, 0)),
                   pl.BlockSpec(memory_space=pl.ANY)],
        out_shape=[jax.ShapeDtypeStruct((t_len, D_MODEL), F32),
                   jax.ShapeDtypeStruct((N_HEADS, t_len, LANE), _BF),
                   jax.ShapeDtypeStruct((N_HEADS, t_len, LANE), _BF),
                   jax.ShapeDtypeStruct((D_MODEL, D_MODEL), F32),
                   jax.ShapeDtypeStruct((8, D_MODEL), F32),
                   jax.ShapeDtypeStruct((N_CB, t_len, LANE), _BF)],
        scratch_shapes=[pltpu.VMEM((tm, D_MODEL), _BF), pltpu.VMEM((tm, D_MODEL), F32),
                        pltpu.VMEM((2, N_HEADS, 4, tm, LANE), F32),
                        pltpu.VMEM((2, 4, N_HEADS, tm, LANE), _BF), pltpu.SemaphoreType.DMA((2, 4))],
        compiler_params=_params(("arbitrary",)),
    )(x, target, oa, ob, pb, pb, pb, pb, wout, gate, final_g, hg_g, ret_g)


def device_step(x, target, mod, lb, wt, wout, norm_g, hg_g, ret_g, final_g, c_idx=None, start_exchange=None):
    t_len = x.shape[0]
    shift, scale, gate = mod[:, :D_MODEL], mod[:, D_MODEL:2 * D_MODEL], mod[:, 2 * D_MODEL:]
    scale1p = 1.0 + scale
    cos_t, sin_t = _rope_tables(t_len)
    h, h_t = adaln_forward(x, norm_g, scale1p, shift, t_len)
    pb = proj_forward(h, wt, t_len)
    oa, ssave, asave = hgrn_forward(pb, lb, t_len)
    ob, rsave = retention_forward(pb, cos_t, sin_t, t_len)
    dy, doa, dob, dwout, vec_mid, dpb = middle(x, target, oa, ob, pb, wout, gate, final_g, hg_g, ret_g, t_len)
    dpb, dlb = hgrn_backward(pb, lb, doa, ssave, asave, dpb, t_len)
    dpb = retention_backward(pb, cos_t, sin_t, dob, rsave, dpb, t_len)
    c_idx = jnp.zeros((1,), jnp.int32) if c_idx is None else c_idx
    dwin = proj_backward_weight(h_t, dpb, t_len)
    dwin_sib = sibling_blocks(dwin, c_idx)
    token, pending = (start_exchange(dwin, dwin_sib, dwout) if start_exchange
                      else (jnp.zeros((8, LANE), F32), None))
    grad_x, vec_ada = proj_backward_input(dpb, wt, token, x, dy, norm_g, scale1p, t_len)
    return grad_x, dwin, dwout, vec_mid, vec_ada, dlb, pending


PACK_ROWS = 16
ROW_NORM_G, ROW_LB, ROW_HG_G, ROW_RET_G, ROW_FINAL_G, ROW_SHIFT, ROW_SCALE, ROW_GATE, ROW_LOSS = range(9)


def _mesh_pos():
    return lax.axis_index("x"), lax.axis_index("y"), lax.axis_index("c")


def _lin(pos):
    return 4 * pos[0] + 2 * pos[1] + pos[2]


def _xor_peer(pos, k):
    return tuple(1 - p if (k >> s) & 1 else p for p, s in zip(pos, (2, 1, 0)))


def _other_chips(pos):
    x, y, _ = pos
    return [(1 - x, y), (x, 1 - y), (1 - x, 1 - y)]


def _remote(src, dst, send_sem, recv_sem, to):
    return pltpu.make_async_remote_copy(src_ref=src, dst_ref=dst, send_sem=send_sem, recv_sem=recv_sem,
                                        device_id=to, device_id_type=MESH)


def pre_exchange(c, w_ada, b_ada, logits, w_in, w_out):
    def body(c_ref, wada_ref, bada_ref, logit_ref, win_ref, wout_ref, mod_ref, scall_ref, lb_ref, wt_ref, wo_ref,
             cg_ref, modall_ref, parts_ref, send1, recv1, send2, recv2):
        pos = _mesh_pos()
        cv = c_ref[...]
        slot = lambda p: pl.ds(pl.multiple_of(8 * _lin(p), 8), 8)
        cg_ref[slot(pos), :] = jnp.broadcast_to(cv * _sigmoid(cv), (8, D_MODEL))
        lb_ref[...] = _sigmoid(logit_ref[0:1, :] - logit_ref[1:2, :])
        peers = [_xor_peer(pos, k) for k in range(1, N_DEV)]
        gather = [_remote(cg_ref.at[slot(pos)], cg_ref.at[slot(pos)], send1.at[n], recv1.at[n], p)
                  for n, p in enumerate(peers)]
        for cp in gather:
            cp.start()
        wt_ref[...] = win_ref[...].T.astype(wt_ref.dtype)
        wo_ref[...] = wout_ref[...].astype(wo_ref.dtype)
        for n, p in enumerate(peers):
            _remote(cg_ref.at[slot(p)], cg_ref.at[slot(p)], send1.at[n], recv1.at[n], p).wait_recv()
        modall_ref[...] = _dot(cg_ref[...], wada_ref[...])
        scatter = [_remote(modall_ref.at[slot(p)], parts_ref.at[slot(pos)], send2.at[n], recv2.at[n], p)
                   for n, p in enumerate(peers)]
        for cp in scatter:
            cp.start()
        parts_ref[slot(pos), :] = modall_ref[slot(pos), :]
        for n, p in enumerate(peers):
            _remote(modall_ref.at[slot(p)], parts_ref.at[slot(p)], send2.at[n], recv2.at[n], p).wait_recv()
        for cp in gather + scatter:
            cp.wait_send()
        for j in range(N_DEV):
            cols = slice(j * SHARD_ADA, (j + 1) * SHARD_ADA)
            mod_ref[:, cols] = parts_ref[8 * j:8 * j + 1, :] + bada_ref[:, cols]
            scall_ref[j:j + 1, :] = cg_ref[8 * j:8 * j + 1, :]

    vmem = pl.BlockSpec(memory_space=pltpu.VMEM)
    return pl.pallas_call(
        body, name="pre_exchange",
        in_specs=[vmem] * 6, out_specs=[vmem] * 5,
        out_shape=[jax.ShapeDtypeStruct((1, 3 * D_MODEL), F32), jax.ShapeDtypeStruct((N_DEV, D_MODEL), F32),
                   jax.ShapeDtypeStruct((1, D_MODEL), F32),
                   jax.ShapeDtypeStruct(w_in.shape[::-1], _BF), jax.ShapeDtypeStruct(w_out.shape, _BF)],
        scratch_shapes=[pltpu.VMEM((N_DEV * 8, D_MODEL), F32), pltpu.VMEM((N_DEV * 8, SHARD_ADA), F32),
                        pltpu.VMEM((N_DEV * 8, SHARD_ADA), F32)] + [pltpu.SemaphoreType.DMA((N_DEV - 1,))] * 4,
        compiler_params=pltpu.CompilerParams(vmem_limit_bytes=VMEM_LIMIT),
    )(c, w_ada, b_ada, logits, w_in, w_out)


def weight_gather(win_sh, wout_sh):
    def body(win_ref, wout_ref, wg_ref, woutg_ref, send, recv, local):
        pos = _mesh_pos()
        x, y, c = pos
        sibling = (x, y, 1 - c)

        def route(core):
            return [(x + (1 - core) * (1 - 2 * x), y + core * (1 - 2 * y)),
                    (x + core * (1 - 2 * x), y + (1 - core) * (1 - 2 * y)),
                    (1 - x, 1 - y)]

        chips, sib_chips = route(c), route(1 - c)
        mine, first, later = [], [], []
        for a, (src, out) in enumerate(((win_ref, wg_ref), (wout_ref, woutg_ref))):
            def copy(k, block, to, src_ref=None, a=a, out=out):
                dst = out.at[_lin(block)]
                return _remote(dst if src_ref is None else src_ref, dst, send.at[7 * a + k], recv.at[7 * a + k], to)
            mine.append(pltpu.make_async_copy(src, out.at[_lin(pos)], local.at[a]))
            first += [copy(0, pos, sibling, src), copy(1, pos, (*chips[0], c), src), copy(2, pos, (*chips[1], c), src)]
            later.append([[copy(3, (*chips[0], c), (*chips[1], c)), copy(4, (*chips[0], c), sibling)],
                          [copy(5, (*chips[1], c), sibling)],
                          [copy(6, (*chips[2], c), sibling)]])
        for cp in mine + first:
            cp.start()
        for j in range(3):
            for a, out in enumerate((wg_ref, woutg_ref)):
                dst = out.at[_lin((*chips[j], c))]
                _remote(dst, dst, send.at[7 * a + 1 + j], recv.at[7 * a + 1 + j], pos).wait_recv()
                for cp in later[a][j]:
                    cp.start()
        for a, out in enumerate((wg_ref, woutg_ref)):
            dst = out.at[_lin(sibling)]
            _remote(dst, dst, send.at[7 * a], recv.at[7 * a], pos).wait_recv()
            for j in range(3):
                dst = out.at[_lin((*sib_chips[j], 1 - c))]
                _remote(dst, dst, send.at[7 * a + 4 + j], recv.at[7 * a + 4 + j], pos).wait_recv()
        for cp in first + [cp for per_array in later for group in per_array for cp in group]:
            cp.wait_send()
        for cp in mine:
            cp.wait()

    any_spec = pl.BlockSpec(memory_space=pl.ANY)
    return pl.pallas_call(
        body, name="weight_gather",
        in_specs=[any_spec, any_spec], out_specs=[any_spec, any_spec],
        out_shape=[jax.ShapeDtypeStruct((N_DEV,) + win_sh.shape, win_sh.dtype),
                   jax.ShapeDtypeStruct((N_DEV,) + wout_sh.shape, wout_sh.dtype)],
        scratch_shapes=[pltpu.SemaphoreType.DMA((14,)), pltpu.SemaphoreType.DMA((14,)),
                        pltpu.SemaphoreType.DMA((2,))],
    )(win_sh, wout_sh)


def grad_pair_exchange(g_sib, g_out):
    def body(gsib_ref, gout_ref, ra_ref, rb_ref, send, recv):
        pos = _mesh_pos()
        x, y, c = pos
        sibling = (x, y, 1 - c)
        copies = []
        for q in range(4):
            copies.append(_remote(gsib_ref.at[q], ra_ref.at[q], send.at[q], recv.at[q], sibling))
            copies.append(_remote(gout_ref.at[2 * q + (1 - c)], rb_ref.at[q], send.at[4 + q], recv.at[4 + q], sibling))
        for cp in copies:
            cp.start()
        for cp in copies:
            cp.wait_recv()
        for cp in copies:
            cp.wait_send()

    any_spec = pl.BlockSpec(memory_space=pl.ANY)
    return pl.pallas_call(
        body, name="grad_pair_exchange",
        in_specs=[any_spec, any_spec], out_specs=[any_spec, any_spec],
        out_shape=[jax.ShapeDtypeStruct(g_sib.shape, g_sib.dtype), jax.ShapeDtypeStruct((4,) + g_out.shape[1:], F32)],
        scratch_shapes=[pltpu.SemaphoreType.DMA((8,)), pltpu.SemaphoreType.DMA((8,))],
    )(g_sib, g_out)


def pair_sum(g_in, ra, g_out, rb, c_idx):
    tr = 256

    def body(c_ref, gin_ref, ra_ref, gout_ref, rb_ref, sb_ref, sbo_ref):
        del c_ref
        sb_ref[...] = (gin_ref[...] + ra_ref[...].astype(F32)).astype(sb_ref.dtype)
        sbo_ref[...] = gout_ref[...] + rb_ref[...]

    n_i = D_MODEL // tr
    return pl.pallas_call(
        body, name="pair_sum",
        grid_spec=pltpu.PrefetchScalarGridSpec(
            num_scalar_prefetch=1, grid=(4, n_i),
            in_specs=[pl.BlockSpec((None, tr, SHARD_IN), lambda q, i, c: (2 * q + c[0], i, 0)),
                      pl.BlockSpec((None, tr, SHARD_IN), lambda q, i, c: (q, i, 0)),
                      pl.BlockSpec((None, SHARD_OUT // n_i, D_MODEL), lambda q, i, c: (2 * q + c[0], i, 0)),
                      pl.BlockSpec((None, SHARD_OUT // n_i, D_MODEL), lambda q, i, c: (q, i, 0))],
            out_specs=[pl.BlockSpec((None, tr, SHARD_IN), lambda q, i, c: (q, i, 0)),
                       pl.BlockSpec((None, SHARD_OUT // n_i, D_MODEL), lambda q, i, c: (q, i, 0))]),
        out_shape=[jax.ShapeDtypeStruct(ra.shape, _BF), jax.ShapeDtypeStruct(rb.shape, F32)],
        compiler_params=_params(("arbitrary", "arbitrary")),
    )(c_idx, g_in, ra, g_out, rb)


_HBM = pl.BlockSpec(memory_space=pltpu.HBM)
_SEM = pl.BlockSpec(memory_space=pltpu.SEMAPHORE)
_N_CHIP_COPIES = 6


def _chip_copies(sb_ref, sbo_ref, rc_ref, rco_ref, send, recv):
    pos = _mesh_pos()
    copies = []
    for a, (src, dst) in enumerate(((sb_ref, rc_ref), (sbo_ref, rco_ref))):
        for j, chip in enumerate(_other_chips(pos)):
            copies.append(_remote(src.at[2 * chip[0] + chip[1]], dst.at[j], send.at[3 * a + j], recv.at[3 * a + j],
                                  (*chip, pos[2])))
    return copies


def grad_chip_start(sb, sbo):
    def body(sb_ref, sbo_ref, rc_ref, rco_ref, send, recv, sb_thru, sbo_thru, rc_thru, rco_thru, token):
        del sb_thru, sbo_thru, rc_thru, rco_thru
        for cp in _chip_copies(sb_ref, sbo_ref, rc_ref, rco_ref, send, recv):
            cp.start()
        token[...] = jnp.zeros_like(token)

    hbm = lambda a: pltpu.with_memory_space_constraint(a, pltpu.HBM)
    rc = lax.empty((3,) + sb.shape[1:], sb.dtype)
    rco = lax.empty((3,) + sbo.shape[1:], sbo.dtype)
    return pl.pallas_call(
        body, name="grad_chip_start",
        in_specs=[_HBM] * 4,
        out_specs=[_SEM, _SEM, _HBM, _HBM, _HBM, _HBM, pl.BlockSpec(memory_space=pltpu.VMEM)],
        out_shape=[pltpu.SemaphoreType.DMA((_N_CHIP_COPIES,)), pltpu.SemaphoreType.DMA((_N_CHIP_COPIES,)),
                   pltpu.HBM(sb.shape, sb.dtype), pltpu.HBM(sbo.shape, sbo.dtype),
                   pltpu.HBM(rc.shape, rc.dtype), pltpu.HBM(rco.shape, rco.dtype),
                   jax.ShapeDtypeStruct((8, LANE), F32)],
        input_output_aliases={0: 2, 1: 3, 2: 4, 3: 5},
        compiler_params=pltpu.CompilerParams(has_side_effects=pltpu.SideEffectType.DATAFLOW_SIDE_EFFECTING),
    )(hbm(sb), hbm(sbo), hbm(rc), hbm(rco))


def grad_chip_wait(send, recv, sb, sbo, rc, rco, after):
    def body(sb_ref, sbo_ref, rc_ref, rco_ref, send, recv, after_ref, sb_o, sbo_o, rc_o, rco_o):
        del after_ref, sb_o, sbo_o, rc_o, rco_o
        for cp in _chip_copies(sb_ref, sbo_ref, rc_ref, rco_ref, send, recv):
            cp.wait_send()
            cp.wait_recv()

    return pl.pallas_call(
        body, name="grad_chip_wait",
        in_specs=[_HBM] * 4 + [_SEM, _SEM, pl.BlockSpec(memory_space=pl.ANY)],
        out_specs=[_HBM] * 4,
        out_shape=[pltpu.HBM(sb.shape, sb.dtype), pltpu.HBM(sbo.shape, sbo.dtype),
                   pltpu.HBM(rc.shape, rc.dtype), pltpu.HBM(rco.shape, rco.dtype)],
        input_output_aliases={0: 0, 1: 1, 2: 2, 3: 3},
        compiler_params=pltpu.CompilerParams(has_side_effects=pltpu.SideEffectType.DATAFLOW_SIDE_EFFECTING),
    )(sb, sbo, rc, rco, send, recv, after)


def pack_gather(pack):
    def body(pack_ref, packs_ref, psend, precv):
        pos = _mesh_pos()
        me = _lin(pos)
        packs_ref[me] = pack_ref[...]
        peers = [_xor_peer(pos, k) for k in range(1, N_DEV)]
        gather = [_remote(packs_ref.at[me], packs_ref.at[me], psend.at[n], precv.at[n], p) for n, p in enumerate(peers)]
        for cp in gather:
            cp.start()
        for n, p in enumerate(peers):
            _remote(packs_ref.at[_lin(p)], packs_ref.at[_lin(p)], psend.at[n], precv.at[n], p).wait_recv()
        for cp in gather:
            cp.wait_send()

    vmem = pl.BlockSpec(memory_space=pltpu.VMEM)
    return pl.pallas_call(
        body, name="pack_gather", in_specs=[vmem], out_specs=vmem,
        out_shape=jax.ShapeDtypeStruct((N_DEV,) + pack.shape, F32),
        scratch_shapes=[pltpu.SemaphoreType.DMA((N_DEV - 1,)), pltpu.SemaphoreType.DMA((N_DEV - 1,))],
    )(pack)


def pack_rows(vec_mid, vec_ada, dlb):
    def body(mid_ref, ada_ref, dlb_ref, o_ref):
        mid = lambda r: mid_ref[r:r + 1, :]
        rows = [ada_ref[0:1, :], dlb_ref[...], mid(MID_HG_G), mid(MID_RET_G), mid(MID_FINAL_G),
                ada_ref[2:3, :], ada_ref[1:2, :], mid(MID_GATE), mid(MID_LOSS)]
        o_ref[...] = jnp.zeros_like(o_ref)
        for n, row in enumerate(rows):
            o_ref[n:n + 1, :] = row

    vmem = pl.BlockSpec(memory_space=pltpu.VMEM)
    return pl.pallas_call(body, name="pack_rows", in_specs=[vmem] * 3, out_specs=vmem,
                          out_shape=jax.ShapeDtypeStruct((PACK_ROWS, D_MODEL), F32))(vec_mid, vec_ada, dlb)


def _adamw(w, g, m, v):
    m = ADAM_B1 * m + (1.0 - ADAM_B1) * g
    v = ADAM_B2 * v + (1.0 - ADAM_B2) * (g * g)
    m_hat = m / (1.0 - ADAM_B1 ** ADAM_STEP)
    v_hat = v / (1.0 - ADAM_B2 ** ADAM_STEP)
    delta = -ADAM_LR * (m_hat / (jnp.sqrt(v_hat) + ADAM_EPS) + ADAM_WD * w)
    return delta, m, v


def adam_shard(chip_idx, own, parts, w, m, v, name):
    rows, cols = w.shape
    tr = min(rows, 128)

    def body(chip_ref, p0, p1, p2, p3, w_ref, m_ref, v_ref, g_ref, d_ref, nm_ref, nv_ref):
        del chip_ref
        g = ((p0[...].astype(F32) + p1[...].astype(F32)) + p2[...].astype(F32)) + p3[...].astype(F32)
        g_ref[...] = g
        d_ref[...], nm_ref[...], nv_ref[...] = _adamw(w_ref[...], g, m_ref[...], v_ref[...])

    part = lambda q: pl.BlockSpec((None, tr, cols), lambda i, chip, q=q: (q, i, 0))
    tile = pl.BlockSpec((tr, cols), lambda i, chip: (i, 0))
    return pl.pallas_call(
        body, name=name,
        grid_spec=pltpu.PrefetchScalarGridSpec(
            num_scalar_prefetch=1, grid=(rows // tr,),
            in_specs=[pl.BlockSpec((None, tr, cols), lambda i, chip: (chip[0], i, 0)), part(0), part(1), part(2),
                      tile, tile, tile],
            out_specs=[tile] * 4),
        out_shape=[jax.ShapeDtypeStruct(w.shape, F32)] * 4,
        compiler_params=_params(("arbitrary",)),
    )(chip_idx, own, parts, parts, parts, w, m, v)


def adam_ada(sc_t, dmod_all, me_idx, w, m, v):
    def body(me_ref, sc_ref, dm_ref, w_ref, m_ref, v_ref, g_ref, d_ref, nm_ref, nv_ref):
        del me_ref
        g = _dot_f32(sc_ref[...], dm_ref[...])
        g_ref[...] = g
        d_ref[...], nm_ref[...], nv_ref[...] = _adamw(w_ref[...], g, m_ref[...], v_ref[...])

    full = pl.BlockSpec(w.shape, lambda i, me: (0, 0))
    return pl.pallas_call(
        body, name="adam_ada",
        grid_spec=pltpu.PrefetchScalarGridSpec(
            num_scalar_prefetch=1, grid=(1,),
            in_specs=[pl.BlockSpec(sc_t.shape, lambda i, me: (0, 0)),
                      pl.BlockSpec((LANE, SHARD_ADA), lambda i, me: (0, me[0])), full, full, full],
            out_specs=[full] * 4),
        out_shape=[jax.ShapeDtypeStruct(w.shape, F32)] * 4,
        compiler_params=_params(("arbitrary",)),
    )(me_idx, sc_t, dmod_all, w, m, v)


def adam_vectors(packs, lb, params, ms, vs):
    n = len(params)

    def body(*refs):
        packs_ref, lb_ref = refs[0], refs[1]
        w_refs, m_refs, v_refs = refs[2:2 + n], refs[2 + n:2 + 2 * n], refs[2 + 2 * n:2 + 3 * n]
        loss_ref = refs[2 + 3 * n]
        outs = refs[3 + 3 * n:3 + 7 * n]
        tot_ref = refs[3 + 7 * n]
        tot = packs_ref[0]
        for d in range(1, N_DEV):
            tot = tot + packs_ref[d]
        tot_ref[...] = tot
        row = lambda r: tot_ref[r:r + 1, :]
        lbv = lb_ref[...]
        dl0 = row(ROW_LB) * lbv * (1.0 - lbv)
        grads = [[row(ROW_NORM_G)],
                 [jnp.concatenate([row(ROW_SHIFT), row(ROW_SCALE), row(ROW_GATE)], axis=1)],
                 [dl0, -dl0],
                 [row(ROW_HG_G)], [row(ROW_RET_G)], [row(ROW_FINAL_G)]]
        loss_ref[...] = tot_ref[ROW_LOSS:ROW_LOSS + 1, 0:LANE]
        for j, g_rows in enumerate(grads):
            for r, g in enumerate(g_rows):
                rs = slice(r, r + 1)
                d, nm, nv = _adamw(w_refs[j][rs, :], g, m_refs[j][rs, :], v_refs[j][rs, :])
                outs[4 * j][rs, :] = g
                outs[4 * j + 1][rs, :] = d
                outs[4 * j + 2][rs, :] = nm
                outs[4 * j + 3][rs, :] = nv

    vmem = pl.BlockSpec(memory_space=pltpu.VMEM)
    out_shape = [jax.ShapeDtypeStruct((1, LANE), F32)]
    for w in params:
        out_shape += [jax.ShapeDtypeStruct(w.shape, F32)] * 4
    return pl.pallas_call(
        body, name="adam_vectors", in_specs=[vmem] * (2 + 3 * n), out_specs=[vmem] * len(out_shape),
        out_shape=out_shape, scratch_shapes=[pltpu.VMEM((PACK_ROWS, D_MODEL), F32)],
    )(packs, lb, *params, *ms, *vs)


def kernel(x, c, norm_g, w_ada, b_ada, w_in, hg_lb_logits, hg_norm_g, ret_norm_g, w_out, final_g, loss_target, m_norm_g, m_w_ada, m_b_ada, m_w_in, m_hg_lb_logits, m_hg_norm_g, m_ret_norm_g, m_w_out, m_final_g, v_norm_g, v_w_ada, v_b_ada, v_w_in, v_hg_lb_logits, v_hg_norm_g, v_ret_norm_g, v_w_out, v_final_g):
    pos = _mesh_pos()
    me_idx = jnp.reshape(_lin(pos), (1,)).astype(jnp.int32)
    c_idx = jnp.reshape(pos[2], (1,)).astype(jnp.int32)
    vec = lambda a: a.reshape(1, D_MODEL)

    mod, scall, lb, wt_sh, wout_sh = pre_exchange(c, w_ada[0], b_ada, hg_lb_logits, w_in[0], w_out[0])
    wtg, woutg = weight_gather(wt_sh, wout_sh)
    chip_idx = jnp.reshape(2 * pos[0] + pos[1], (1,)).astype(jnp.int32)

    def start_exchange(dwin, dwin_sib, dwout):
        dwout = dwout.reshape(N_DEV, SHARD_OUT, D_MODEL)
        ra, rb = grad_pair_exchange(dwin_sib, dwout)
        sb, sbo = pair_sum(dwin, ra, dwout, rb, c_idx)
        send, recv, sb, sbo, rc, rco, token = grad_chip_start(sb, sbo)
        return token, (send, recv, sb, sbo, rc, rco)

    grad_x, _, _, vec_mid, vec_ada, dlb, pending = device_step(
        x[0], loss_target[0], mod, lb, wtg.reshape(D_IN, D_MODEL), woutg.reshape(D_MODEL, D_MODEL), norm_g,
        hg_norm_g, ret_norm_g, vec(final_g), c_idx, start_exchange)
    packs = pack_gather(pack_rows(vec_mid, vec_ada, dlb))
    dmod_all = packs[:, ROW_SHIFT:ROW_GATE + 1, :].reshape(N_DEV, 3 * D_MODEL)
    dmod_all = jnp.pad(dmod_all, ((0, LANE - N_DEV), (0, 0)))
    sc_t = jnp.pad(scall.T, ((0, 0), (0, LANE - N_DEV)))
    g_ada, d_ada, nm_ada, nv_ada = adam_ada(sc_t, dmod_all, me_idx, w_ada[0], m_w_ada[0], v_w_ada[0])
    small = adam_vectors(
        packs, lb,
        (norm_g, b_ada, hg_lb_logits, hg_norm_g, ret_norm_g, vec(final_g)),
        (m_norm_g, m_b_ada, m_hg_lb_logits, m_hg_norm_g, m_ret_norm_g, vec(m_final_g)),
        (v_norm_g, v_b_ada, v_hg_lb_logits, v_hg_norm_g, v_ret_norm_g, vec(v_final_g)))
    loss = small[0][0, 0]
    sb, sbo, rc, rco = grad_chip_wait(*pending, small[0])
    g_in, d_in, nm_in, nv_in = adam_shard(chip_idx, sb, rc, w_in[0], m_w_in[0], v_w_in[0], "adam_w_in")
    g_out, d_out, nm_out, nv_out = adam_shard(chip_idx, sbo, rco, w_out[0], m_w_out[0], v_w_out[0], "adam_w_out")
    (g_ng, d_ng, nm_ng, nv_ng), (g_b, d_b, nm_b, nv_b), (g_lb, d_lb, nm_lb, nv_lb), (g_hg, d_hg, nm_hg, nv_hg), \
        (g_rg, d_rg, nm_rg, nv_rg), (g_fg, d_fg, nm_fg, nv_fg) = [small[1 + 4 * j:5 + 4 * j] for j in range(6)]
    flat = lambda a: a.reshape(D_MODEL)

    def group(ng, ada, b, win, lbl, hg, rg, wo, fg):
        return (ng, ada[None], b, win[None], lbl, hg, rg, wo[None], flat(fg))

    return (loss, grad_x[None],
            *group(g_ng, g_ada, g_b, g_in, g_lb, g_hg, g_rg, g_out, g_fg),
            *group(d_ng, d_ada, d_b, d_in, d_lb, d_hg, d_rg, d_out, d_fg),
            *group(nm_ng, nm_ada, nm_b, nm_in, nm_lb, nm_hg, nm_rg, nm_out, nm_fg),
            *group(nv_ng, nv_ada, nv_b, nv_in, nv_lb, nv_hg, nv_rg, nv_out, nv_fg))
```

```python
import functools

import numpy as np
import jax
import jax.numpy as jnp
from jax import lax
from jax.experimental import pallas as pl
from jax.experimental.pallas import tpu as pltpu

F32 = jnp.float32
_BF = jnp.bfloat16

D_MODEL = 1024
N_HEADS = 8
LANE = 128
RET_DK = 64
D_IN = 9216
N_DEV = 8
SHARD_IN = D_IN // N_DEV
SHARD_ADA = 3 * D_MODEL // N_DEV
SHARD_OUT = D_MODEL // N_DEV
N_CB = D_IN // LANE
CB_PER_SHARD = SHARD_IN // LANE
CHUNK = 128
N_LEVELS = 7
EPS = 1e-6
LOG2_E = float(np.log2(np.e))
ROPE_BASE = 10000.0
CB_HQ, CB_HF, CB_HI, CB_HZ, CB_RQ, CB_RK, CB_RV, CB_RZ, CB_GA, CB_GB = 0, 8, 16, 24, 32, 36, 40, 48, 56, 64
VMEM_LIMIT = 56 * 1024 * 1024

ADAM_LR, ADAM_B1, ADAM_B2, ADAM_EPS, ADAM_WD, ADAM_STEP = 0.001, 0.9, 0.999, 1e-08, 0.01, 10

_NN = (((1,), (0,)), ((), ()))
_NT = (((1,), (1,)), ((), ()))
_TN = (((0,), (0,)), ((), ()))
MESH = pl.DeviceIdType.MESH


def _dot(a, b, dims=_NN):
    return lax.dot_general(a.astype(_BF), b.astype(_BF), dims, preferred_element_type=F32)


def _split2(a):
    hi = a.astype(_BF)
    lo = (a - hi.astype(F32)).astype(_BF)
    return jnp.concatenate([hi, lo], axis=1)


def _dot_sel(sel, a):
    n = a.shape[1]
    r = lax.dot_general(sel.astype(_BF), _split2(a), _NN, preferred_element_type=F32)
    return r[:, :n] + r[:, n:]


def _dot_f32(a, b):
    def pieces(v):
        p1 = v.astype(_BF)
        r1 = v - p1.astype(F32)
        p2 = r1.astype(_BF)
        p3 = (r1 - p2.astype(F32)).astype(_BF)
        return p1, p2, p3
    a1, a2, a3 = pieces(a)
    b1, b2, b3 = pieces(b)
    d = lambda u, v: lax.dot_general(u, v, _NN, preferred_element_type=F32)
    return ((d(a1, b3) + d(a2, b2) + d(a3, b1)) + (d(a1, b2) + d(a2, b1))) + d(a1, b1)


def _sigmoid(v):
    return 1.0 / (1.0 + jnp.exp(-v))


def _params(sem=None):
    return pltpu.CompilerParams(dimension_semantics=sem, vmem_limit_bytes=VMEM_LIMIT)


def _hgrn_consts():
    c, nl = CHUNK, N_LEVELS
    t = np.arange(c)[:, None]
    j = np.arange(c)[None, :]
    sel = [j <= t]
    masks = [j == t]
    for l in range(1, nl + 1):
        m = ((t >> l) << l) + (1 << (l - 1)) - 1
        sec = t > m
        sel.append(np.where(sec, (j > m) & (j <= t), (j > t) & (j <= m)))
        same = (t >> l) == (j >> l)
        masks.append(same & sec & (j <= m))
    sel.append(j > t)
    sel = np.concatenate(sel, 0).astype(np.float32)
    masks = np.stack(masks).astype(np.float32)
    sgn = np.stack([np.where((t & (1 << (l - 1))) != 0, 1.0, -1.0) * np.ones((1, LANE)) for l in range(3, nl + 1)])
    return dict(tri=jnp.asarray(sel[:c], _BF),
                lvl=jnp.asarray(masks, F32),
                sgn=jnp.asarray(sgn, F32),
                sel_t=jnp.asarray(sel.T, _BF),
                lvl_b=jnp.asarray(masks, _BF),
                lvlt_b=jnp.asarray(np.swapaxes(masks, 1, 2), _BF))


def _level_exponents(b, logf, b_scr, sgn_ref):
    c = CHUNK
    b_scr[...] = b
    row = lax.broadcasted_iota(jnp.int32, (c, LANE), 0)
    nxt = pltpu.roll(logf, c - 1, 0)
    prv = pltpu.roll(logf, 1, 0)
    r4 = row & 3
    out = [jnp.where((row & 1) == 1, logf, 0.0),
           jnp.where(r4 == 0, nxt, jnp.where(r4 == 1, 0.0, jnp.where(r4 == 2, logf, logf + prv)))]
    for l in range(3, N_LEVELS + 1):
        size, half = 1 << l, 1 << (l - 1)
        ref = jnp.concatenate([jnp.broadcast_to(b_scr[i * size + half - 1:i * size + half, :], (size, LANE))
                               for i in range(c // size)], axis=0)
        out.append((b - ref) * sgn_ref[l - 3])
    return out


def _hgrn_chunk(hq, hf, hi, lbv, tri_ref, sgn_ref, b_scr):
    sq = _sigmoid(hq)
    q = hq * sq
    sg = _sigmoid(hf)
    omlb = 1.0 - lbv
    f = lbv + omlb * sg
    k = 1.0 - f
    logf = jnp.log(f) * LOG2_E
    b = _dot_sel(tri_ref[...], logf)
    bc = jnp.sum(logf, axis=0, keepdims=True)
    lev = [None] + [jnp.exp2(e) for e in _level_exponents(b, logf, b_scr, sgn_ref)]
    return dict(sq=sq, q=q, sg=sg, omlb=omlb, f=f, k=k, v=hi, eb=jnp.exp2(b), erem=jnp.exp2(bc - b),
                ebc=jnp.exp2(bc), lev=lev)


def _blockdiag(a, b):
    z = jnp.zeros_like(a)
    return jnp.concatenate([jnp.concatenate([a, z], axis=1), jnp.concatenate([z, b], axis=1)], axis=0)


def _level_operands(a):
    q, k = a["q"].astype(_BF), a["k"].astype(_BF)
    lev = [None] + [a["lev"][l].astype(_BF) for l in range(1, N_LEVELS + 1)]
    ql = [q] + [q * lev[l] for l in range(1, N_LEVELS + 1)]
    kl = [k] + [k * lev[l] for l in range(1, N_LEVELS + 1)]
    pairs = range(0, N_LEVELS + 1, 2)
    return ([jnp.concatenate([ql[l], ql[l + 1]], axis=1) for l in pairs], [_blockdiag(kl[l], kl[l + 1]) for l in pairs],
            ql, kl)


def _hgrn_scores(a, lvl_ref, q_pairs, k_diags):
    acc = None
    for n, (qp, kd) in enumerate(zip(q_pairs, k_diags)):
        both = lax.dot_general(qp, kd, _NT, preferred_element_type=F32)
        part = lvl_ref[2 * n] * both[:, :CHUNK] + lvl_ref[2 * n + 1] * both[:, CHUNK:]
        acc = part if acc is None else acc + part
    return acc


SCAN_UNROLL = 4
RET_UNROLL = 8


def _writeback_reserve(step, make_copies):
    slot = step % 2

    @pl.when(step >= 2)
    def _():
        for cp in make_copies(slot):
            cp.wait()

    return slot


def _writeback_commit(step, n_steps, slot, make_copies):
    for cp in make_copies(slot):
        cp.start()

    @pl.when(step == n_steps - 1)
    def _():
        for cp in make_copies(slot):
            cp.wait()
        if n_steps > 1:
            for cp in make_copies(1 - slot):
                cp.wait()


def _resident(const):
    zeros = (0,) * const.ndim
    return pl.BlockSpec(const.shape, lambda p, t: zeros)


def _time_block(t_len):
    return min(t_len, 2048)


def hgrn_forward(pb, lb, t_len):
    nc = t_len // CHUNK
    tb = _time_block(t_len)
    ncb = tb // CHUNK
    consts = _hgrn_consts()
    operands = [consts[n] for n in ("tri", "lvl", "sgn")]

    def body(hq_ref, hf_ref, hi_ref, lb_ref, tri_ref, lvl_ref, sgn_ref, o_ref, ssave_ref, asave_ref, st_ref, b_scr):
        @pl.when(pl.program_id(1) == 0)
        def _():
            st_ref[...] = jnp.zeros_like(st_ref)

        def chunk(ci, carry):
            r = pl.ds(pl.multiple_of(ci * CHUNK, CHUNK), CHUNK)
            for hd in range(2):
                lbv = lb_ref[:, hd * LANE:(hd + 1) * LANE]
                a = _hgrn_chunk(hq_ref[hd, r, :], hf_ref[hd, r, :], hi_ref[hd, r, :], lbv, tri_ref, sgn_ref,
                                b_scr.at[hd])
                q_pairs, k_diags, _, _ = _level_operands(a)
                st = st_ref[hd]
                ssave_ref[hd, ci] = st
                scores = _hgrn_scores(a, lvl_ref, q_pairs, k_diags).astype(asave_ref.dtype)
                asave_ref[hd, ci] = scores
                o_ref[hd, r, :] = _dot(a["q"] * a["eb"], st, _NT) + _dot(scores, a["v"])
                st_ref[hd] = st * a["ebc"] + _dot(a["v"], a["k"] * a["erem"], _TN)
            return carry

        lax.fori_loop(0, ncb, chunk, 0, unroll=SCAN_UNROLL)

    pair = lambda base: pl.BlockSpec((2, tb, LANE), lambda p, t, base=base: (base // 2 + p, t, 0))
    per_chunk = pl.BlockSpec((2, ncb, LANE, LANE), lambda p, t: (p, t, 0, 0))
    return pl.pallas_call(
        body, name="hgrn_fwd", grid=(N_HEADS // 2, t_len // tb),
        in_specs=[pair(CB_HQ), pair(CB_HF), pair(CB_HI),
                  pl.BlockSpec((1, 2 * LANE), lambda p, t: (0, p))] + [_resident(c) for c in operands],
        out_specs=[pl.BlockSpec((2, tb, LANE), lambda p, t: (p, t, 0)), per_chunk, per_chunk],
        out_shape=[jax.ShapeDtypeStruct((N_HEADS, t_len, LANE), F32),
                   jax.ShapeDtypeStruct((N_HEADS, nc, LANE, LANE), F32),
                   jax.ShapeDtypeStruct((N_HEADS, nc, CHUNK, CHUNK), _BF)],
        scratch_shapes=[pltpu.VMEM((2, LANE, LANE), F32), pltpu.VMEM((2, CHUNK, LANE), F32)],
        compiler_params=_params(("arbitrary", "arbitrary")),
    )(pb, pb, pb, lb, *operands)


def hgrn_backward(pb, lb, do, ssave, asave, dpb, t_len):
    tb = _time_block(t_len)
    ncb, ntb = tb // CHUNK, t_len // tb
    consts = _hgrn_consts()
    operands = [consts[n] for n in ("tri", "sgn", "sel_t", "lvl_b", "lvlt_b")]

    def body(hq_ref, hf_ref, hi_ref, lb_ref, do_ref, ssave_ref, asave_ref, tri_ref, sgn_ref, selt_ref, lvlb_ref,
             lvltb_ref, dpb_in, dpb_ref, dlb_ref, dq_buf, df_buf, di_buf, dst_ref, b_scr, sems):
        del dpb_in
        p, t = pl.program_id(0), pl.program_id(1)
        step = p * ntb + t
        rows = pl.ds(pl.multiple_of((ntb - 1 - t) * tb, tb), tb)

        def out_copies(sl):
            return [pltpu.make_async_copy(buf.at[sl], dpb_ref.at[pl.ds(base + 2 * p, 2), rows], sems.at[sl, n])
                    for n, (buf, base) in enumerate(((dq_buf, CB_HQ), (df_buf, CB_HF), (di_buf, CB_HI)))]

        slot = _writeback_reserve(step, out_copies)

        @pl.when(t == 0)
        def _():
            dst_ref[...] = jnp.zeros_like(dst_ref)
            dlb_ref[...] = jnp.zeros_like(dlb_ref)

        def chunk(i, carry):
            ci = ncb - 1 - i
            r = pl.ds(pl.multiple_of(ci * CHUNK, CHUNK), CHUNK)
            for hd in range(2):
                head_chunk(hd, ci, r)
            return carry

        def head_chunk(hd, ci, r):
            lbv = lb_ref[:, hd * LANE:(hd + 1) * LANE]
            hq = hq_ref[hd, r, :]
            a = _hgrn_chunk(hq, hf_ref[hd, r, :], hi_ref[hd, r, :], lbv, tri_ref, sgn_ref, b_scr.at[hd])
            _, k_diags, ql, kl = _level_operands(a)
            q, k, v = a["q"], a["k"], a["v"]
            g = do_ref[hd, r, :]
            st0 = ssave_ref[hd, ci]
            dst = dst_ref[hd]
            scores = asave_ref[hd, ci]
            da = _dot(g, v, _NT)
            da_t = _dot(v, g, _NT)
            kb = k * a["erem"]
            qb = q * a["eb"]
            dv = _dot(scores, g, _TN) + _dot(kb, dst, _NT)
            dq_inter = _dot(g, st0) * a["eb"]
            dk_state = _dot(v, dst) * a["erem"]
            dq, dk = dq_inter, dk_state
            de = [q * dq_inter]
            da_b, dat_b = da.astype(_BF), da_t.astype(_BF)
            for n in range(len(k_diags)):
                l0, l1 = 2 * n, 2 * n + 1
                da_pair = jnp.concatenate([lvlb_ref[l0] * da_b, lvlb_ref[l1] * da_b], axis=1)
                dat_pair = jnp.concatenate([lvltb_ref[l0] * dat_b, lvltb_ref[l1] * dat_b], axis=1)
                dq_both = lax.dot_general(da_pair, k_diags[n], _NN, preferred_element_type=F32)
                dk_both = lax.dot_general(dat_pair, _blockdiag(ql[l0], ql[l1]), _NN, preferred_element_type=F32)
                for l, cols in ((l0, slice(0, LANE)), (l1, slice(LANE, 2 * LANE))):
                    dql, dkl = dq_both[:, cols], dk_both[:, cols]
                    if l > 0:
                        e = a["lev"][l]
                        dql, dkl = dql * e, dkl * e
                        de.append(q * dql + k * dkl)
                    dq = dq + dql
                    dk = dk + dkl
            de.append(k * dk_state)
            dst_ref[hd] = dst * a["ebc"] + _dot(g, qb, _TN)
            dbc = jnp.sum(dst * st0, axis=0, keepdims=True) * a["ebc"]
            de2 = lax.dot_general(selt_ref[...], _split2(jnp.concatenate(de, axis=0)), _NN,
                                  preferred_element_type=F32)
            dlogf = de2[:, :LANE] + de2[:, LANE:] + dbc
            sq, sg = a["sq"], a["sg"]
            df = dlogf / a["f"] - dk
            dq_buf[slot, hd, r, :] = (dq * (sq * (1.0 + hq * (1.0 - sq)))).astype(dq_buf.dtype)
            df_buf[slot, hd, r, :] = (df * a["omlb"] * sg * (1.0 - sg)).astype(df_buf.dtype)
            di_buf[slot, hd, r, :] = dv.astype(di_buf.dtype)
            cols = slice(hd * LANE, (hd + 1) * LANE)
            dlb_ref[:, cols] = dlb_ref[:, cols] + jnp.sum(df * (1.0 - sg), axis=0, keepdims=True)

        lax.fori_loop(0, ncb, chunk, 0, unroll=SCAN_UNROLL)
        _writeback_commit(step, (N_HEADS // 2) * ntb, slot, out_copies)

    pair = lambda base: pl.BlockSpec((2, tb, LANE), lambda p, t, base=base: (base // 2 + p, ntb - 1 - t, 0))
    any_spec = pl.BlockSpec(memory_space=pl.ANY)
    per_chunk = pl.BlockSpec((2, ncb, LANE, LANE), lambda p, t: (p, ntb - 1 - t, 0, 0))
    return pl.pallas_call(
        body, name="hgrn_bwd", grid=(N_HEADS // 2, ntb),
        in_specs=[pair(CB_HQ), pair(CB_HF), pair(CB_HI),
                  pl.BlockSpec((1, 2 * LANE), lambda p, t: (0, p)),
                  pair(0), per_chunk, per_chunk]
        + [_resident(c) for c in operands] + [any_spec],
        out_specs=[any_spec, pl.BlockSpec((1, 2 * LANE), lambda p, t: (0, p))],
        out_shape=[jax.ShapeDtypeStruct(dpb.shape, dpb.dtype), jax.ShapeDtypeStruct((1, D_MODEL), F32)],
        scratch_shapes=[pltpu.VMEM((2, 2, tb, LANE), dpb.dtype)] * 3 + [
            pltpu.VMEM((2, LANE, LANE), F32), pltpu.VMEM((2, CHUNK, LANE), F32), pltpu.SemaphoreType.DMA((2, 3))],
        input_output_aliases={7 + len(operands): 0},
        compiler_params=_params(("arbitrary", "arbitrary")),
    )(pb, pb, pb, lb, do, ssave, asave, *operands, dpb)


def _rope_tables(t_len):
    half = RET_DK // 2
    inv_freq = (1.0 / (np.float32(ROPE_BASE) ** np.linspace(0.0, 1.0, half, dtype=np.float32))).astype(np.float32)
    ang = (np.arange(t_len, dtype=np.float32)[:, None] * inv_freq[None, :]).astype(np.float64)
    cos, sin = np.cos(ang).astype(np.float32), np.sin(ang).astype(np.float32)
    cos_t = np.concatenate([cos, cos, cos, cos], axis=1)
    sin_t = np.concatenate([-sin, sin, -sin, sin], axis=1)
    return jnp.asarray(cos_t), jnp.asarray(sin_t)


def _swap_halves(v):
    half = RET_DK // 2
    lane = lax.broadcasted_iota(jnp.int32, v.shape, 1)
    first = (lane & (RET_DK - 1)) < half
    return jnp.where(first, pltpu.roll(v, LANE - half, 1), pltpu.roll(v, half, 1))


def _ret_head_consts(hidx):
    c = CHUNK
    hf = jnp.full((1, LANE), hidx, jnp.int32).astype(F32)
    lg = jnp.log(1.0 - jnp.exp(-(5.0 + hf) * np.float32(np.log(2.0))))
    row = lax.broadcasted_iota(jnp.int32, (c, c), 0)
    col = lax.broadcasted_iota(jnp.int32, (c, c), 1)
    rel = (row - col).astype(F32)
    dm = jnp.where(rel >= 0, jnp.exp(lg[:, :1] * jnp.maximum(rel, 0.0)), 0.0)
    dm_t = jnp.where(rel <= 0, jnp.exp(lg[:, :1] * jnp.maximum(-rel, 0.0)), 0.0)
    idx = lax.broadcasted_iota(jnp.int32, (c, LANE), 0).astype(F32)
    zeta = jnp.exp(lg * (c - 1.0 - idx))
    xi = jnp.exp(lg * (idx + 1.0))
    cdec = jnp.exp(lg * float(c))
    return dm, zeta, xi, cdec, dm_t


def _lane_mask(which):
    lane = lax.broadcasted_iota(jnp.int32, (1, LANE), 1)
    return ((lane // RET_DK) == which).astype(F32)


def retention_forward(pb, cos_t, sin_t, t_len):
    nc = t_len // CHUNK

    tb = _time_block(t_len)
    ncb = tb // CHUNK

    def body(rq_ref, rk_ref, rv_ref, cos_ref, sin_ref, o_ref, rsave_ref, st_ref):
        p = pl.program_id(0)

        @pl.when(pl.program_id(1) == 0)
        def _():
            st_ref[...] = jnp.zeros_like(st_ref)

        consts = [_ret_head_consts(2 * p + hd) for hd in range(2)]

        def chunk(ci, carry):
            r = pl.ds(pl.multiple_of(ci * CHUNK, CHUNK), CHUNK)
            cs, sn = cos_ref[r, :], sin_ref[r, :]
            q = rq_ref[r, :]
            k = rk_ref[r, :]
            q = q * cs + _swap_halves(q) * sn
            k = (k * cs + _swap_halves(k) * sn) * RET_DK ** -0.5
            for hd in range(2):
                dm, zeta, xi, cdec, _ = consts[hd]
                lm = _lane_mask(hd)
                qh, kh = q * lm, k * lm
                v = rv_ref[hd, r, :]
                st = st_ref[hd]
                rsave_ref[hd, ci] = st
                scores = _dot(qh, kh, _NT) * dm
                o_ref[hd, r, :] = _dot(scores, v) + _dot(qh * xi, st, _NT)
                st_ref[hd] = st * cdec + _dot(v, kh * zeta, _TN)
            return carry

        lax.fori_loop(0, ncb, chunk, 0, unroll=RET_UNROLL)

    return pl.pallas_call(
        body, name="ret_fwd", grid=(N_HEADS // 2, t_len // tb),
        in_specs=[pl.BlockSpec((None, tb, LANE), lambda p, t: (CB_RQ + p, t, 0)),
                  pl.BlockSpec((None, tb, LANE), lambda p, t: (CB_RK + p, t, 0)),
                  pl.BlockSpec((2, tb, LANE), lambda p, t: (CB_RV // 2 + p, t, 0)),
                  pl.BlockSpec((tb, LANE), lambda p, t: (t, 0)),
                  pl.BlockSpec((tb, LANE), lambda p, t: (t, 0))],
        out_specs=[pl.BlockSpec((2, tb, LANE), lambda p, t: (p, t, 0)),
                   pl.BlockSpec((2, ncb, LANE, LANE), lambda p, t: (p, t, 0, 0))],
        out_shape=[jax.ShapeDtypeStruct((N_HEADS, t_len, LANE), F32),
                   jax.ShapeDtypeStruct((N_HEADS, nc, LANE, LANE), F32)],
        scratch_shapes=[pltpu.VMEM((2, LANE, LANE), F32)],
        compiler_params=_params(("arbitrary", "arbitrary")),
    )(pb, pb, pb, cos_t, sin_t)


def retention_backward(pb, cos_t, sin_t, do, rsave, dpb, t_len):
    tb = _time_block(t_len)
    ncb, ntb = tb // CHUNK, t_len // tb

    def body(rq_ref, rk_ref, rv_ref, cos_ref, sin_ref, do_ref, rsave_ref, dpb_in,
             dpb_ref, dq_buf, dk_buf, dv_buf, dst_ref, sems):
        del dpb_in
        p, t = pl.program_id(0), pl.program_id(1)
        step = p * ntb + t
        rows = pl.ds(pl.multiple_of((ntb - 1 - t) * tb, tb), tb)

        def out_copies(sl):
            return [pltpu.make_async_copy(dq_buf.at[sl], dpb_ref.at[CB_RQ + p, rows], sems.at[sl, 0]),
                    pltpu.make_async_copy(dk_buf.at[sl], dpb_ref.at[CB_RK + p, rows], sems.at[sl, 1]),
                    pltpu.make_async_copy(dv_buf.at[sl], dpb_ref.at[pl.ds(CB_RV + 2 * p, 2), rows], sems.at[sl, 2])]

        slot = _writeback_reserve(step, out_copies)

        @pl.when(t == 0)
        def _():
            dst_ref[...] = jnp.zeros_like(dst_ref)

        consts = [_ret_head_consts(2 * p + hd) for hd in range(2)]

        def chunk(i, carry):
            ci = ncb - 1 - i
            r = pl.ds(pl.multiple_of(ci * CHUNK, CHUNK), CHUNK)
            cs, sn = cos_ref[r, :], sin_ref[r, :]
            q = rq_ref[r, :]
            k = rk_ref[r, :]
            q = q * cs + _swap_halves(q) * sn
            k = (k * cs + _swap_halves(k) * sn) * RET_DK ** -0.5
            dq, dk = None, None
            for hd in range(2):
                dm, zeta, xi, cdec, dm_t = consts[hd]
                lm = _lane_mask(hd)
                qh, kh = q * lm, k * lm
                v = rv_ref[hd, r, :]
                g = do_ref[hd, r, :]
                st0 = rsave_ref[hd, ci]
                dst = dst_ref[hd]
                scores_t = _dot(kh, qh, _NT) * dm_t
                dsc = _dot(g, v, _NT) * dm
                dsc_t = _dot(v, g, _NT) * dm_t
                dqh = _dot(dsc, kh) + _dot(g, st0) * xi
                dkh = _dot(dsc_t, qh) + _dot(v, dst) * zeta
                dv_buf[slot, hd, r, :] = (_dot(scores_t, g) + _dot(kh * zeta, dst, _NT)).astype(dv_buf.dtype)
                dst_ref[hd] = dst * cdec + _dot(g, qh * xi, _TN)
                dq = dqh if dq is None else dq + dqh
                dk = dkh if dk is None else dk + dkh
            dk = dk * (RET_DK ** -0.5)
            dq_buf[slot, r, :] = (dq * cs - _swap_halves(dq) * sn).astype(dq_buf.dtype)
            dk_buf[slot, r, :] = (dk * cs - _swap_halves(dk) * sn).astype(dk_buf.dtype)
            return carry

        lax.fori_loop(0, ncb, chunk, 0, unroll=RET_UNROLL)
        _writeback_commit(step, (N_HEADS // 2) * ntb, slot, out_copies)

    any_spec = pl.BlockSpec(memory_space=pl.ANY)
    return pl.pallas_call(
        body, name="ret_bwd", grid=(N_HEADS // 2, ntb),
        in_specs=[pl.BlockSpec((None, tb, LANE), lambda p, t: (CB_RQ + p, ntb - 1 - t, 0)),
                  pl.BlockSpec((None, tb, LANE), lambda p, t: (CB_RK + p, ntb - 1 - t, 0)),
                  pl.BlockSpec((2, tb, LANE), lambda p, t: (CB_RV // 2 + p, ntb - 1 - t, 0)),
                  pl.BlockSpec((tb, LANE), lambda p, t: (ntb - 1 - t, 0)),
                  pl.BlockSpec((tb, LANE), lambda p, t: (ntb - 1 - t, 0)),
                  pl.BlockSpec((2, tb, LANE), lambda p, t: (p, ntb - 1 - t, 0)),
                  pl.BlockSpec((2, ncb, LANE, LANE), lambda p, t: (p, ntb - 1 - t, 0, 0)),
                  any_spec],
        out_specs=any_spec,
        out_shape=jax.ShapeDtypeStruct(dpb.shape, dpb.dtype),
        scratch_shapes=[pltpu.VMEM((2, tb, LANE), dpb.dtype), pltpu.VMEM((2, tb, LANE), dpb.dtype),
                        pltpu.VMEM((2, 2, tb, LANE), dpb.dtype), pltpu.VMEM((2, LANE, LANE), F32),
                        pltpu.SemaphoreType.DMA((2, 3))],
        input_output_aliases={7: 0},
        compiler_params=_params(("arbitrary", "arbitrary")),
    )(pb, pb, pb, cos_t, sin_t, do, rsave, dpb)


def _row_tile(t_len, want):
    return min(want, t_len)


PAIR_CB = 2 * CB_PER_SHARD


def proj_forward(h, wt, t_len):
    tm = _row_tile(t_len, 1024)

    def body(h_ref, w_ref, o_ref):
        acc = _dot(h_ref[...], w_ref[...], _NT)
        for jj in range(PAIR_CB):
            o_ref[jj] = acc[:, jj * LANE:(jj + 1) * LANE]

    return pl.pallas_call(
        body, name="proj_fwd", grid=(N_DEV // 2, t_len // tm),
        in_specs=[pl.BlockSpec((tm, D_MODEL), lambda j, i: (i, 0)),
                  pl.BlockSpec((PAIR_CB * LANE, D_MODEL), lambda j, i: (j, 0))],
        out_specs=pl.BlockSpec((PAIR_CB, tm, LANE), lambda j, i: (j, i, 0)),
        out_shape=jax.ShapeDtypeStruct((N_CB, t_len, LANE), F32),
        compiler_params=_params(("arbitrary", "arbitrary")),
    )(h, wt)


def proj_backward_input(dpb, wt, token, x, dy, norm_g, scale1p, t_len):
    tm = _row_tile(t_len, 256)

    def body(a_ref, wt_hbm, token_ref, x_ref, dy_ref, g_ref, sc_ref, gx_ref, vec_ref, w_ref, sem):
        del token_ref
        i = pl.program_id(0)

        @pl.when(i == 0)
        def _():
            cp = pltpu.make_async_copy(wt_hbm, w_ref, sem)
            cp.start()
            cp.wait()

        a = jnp.concatenate([a_ref[jj].astype(_BF) for jj in range(N_CB)], axis=1)
        dhv = _dot(a, w_ref[...])
        xv, g, sc = x_ref[...], g_ref[...], sc_ref[...]
        r = lax.rsqrt(jnp.mean(xv * xv, axis=-1, keepdims=True) + EPS)
        xn = xv * r
        dxn = dhv * (g * sc)
        gx_ref[...] = dy_ref[...] + r * dxn - xn * (r * r) * jnp.mean(xv * dxn, axis=-1, keepdims=True)
        t = dhv * xn
        _acc_rows(vec_ref, i, [jnp.sum(t * sc, axis=0, keepdims=True),
                               jnp.sum(t * g, axis=0, keepdims=True),
                               jnp.sum(dhv, axis=0, keepdims=True)])

    row = pl.BlockSpec((tm, D_MODEL), lambda i: (i, 0))
    return pl.pallas_call(
        body, name="proj_bwd_input", grid=(t_len // tm,),
        in_specs=[pl.BlockSpec((N_CB, tm, LANE), lambda i: (0, i, 0)),
                  pl.BlockSpec(memory_space=pl.ANY),
                  pl.BlockSpec(token.shape, lambda i: (0, 0)),
                  row, row, _vec_spec(), _vec_spec()],
        out_specs=[row, pl.BlockSpec((8, D_MODEL), lambda i: (0, 0))],
        out_shape=[jax.ShapeDtypeStruct((t_len, D_MODEL), F32), jax.ShapeDtypeStruct((8, D_MODEL), F32)],
        scratch_shapes=[pltpu.VMEM(wt.shape, wt.dtype), pltpu.SemaphoreType.DMA],
        compiler_params=_params(("arbitrary",)),
    )(dpb, wt, token, x, dy, norm_g, scale1p)


def proj_backward_weight(h_t, dpb, t_len):
    tk = _row_tile(t_len, 2048)

    def body(h_ref, b_ref, o_ref):
        k = pl.program_id(1)
        b = jnp.concatenate([b_ref[jj].astype(_BF) for jj in range(PAIR_CB)], axis=1)
        part = _dot(h_ref[...], b)

        @pl.when(k == 0)
        def _():
            for s in range(2):
                o_ref[s] = part[:, s * SHARD_IN:(s + 1) * SHARD_IN]

        @pl.when(k > 0)
        def _():
            for s in range(2):
                o_ref[s] = o_ref[s] + part[:, s * SHARD_IN:(s + 1) * SHARD_IN]

    return pl.pallas_call(
        body, name="proj_bwd_weight", grid=(N_DEV // 2, t_len // tk),
        in_specs=[pl.BlockSpec((D_MODEL, tk), lambda j, k: (0, k)),
                  pl.BlockSpec((PAIR_CB, tk, LANE), lambda j, k: (j, k, 0))],
        out_specs=pl.BlockSpec((2, D_MODEL, SHARD_IN), lambda j, k: (j, 0, 0)),
        out_shape=jax.ShapeDtypeStruct((N_DEV, D_MODEL, SHARD_IN), F32),
        compiler_params=_params(("arbitrary", "arbitrary")),
    )(h_t, dpb)


def sibling_blocks(g_in, c_idx):
    tr = D_MODEL

    def body(c_ref, g_ref, o_ref):
        del c_ref
        o_ref[...] = g_ref[...].astype(o_ref.dtype)

    return pl.pallas_call(
        body, name="sibling_blocks",
        grid_spec=pltpu.PrefetchScalarGridSpec(
            num_scalar_prefetch=1, grid=(N_DEV // 2, D_MODEL // tr),
            in_specs=[pl.BlockSpec((None, tr, SHARD_IN), lambda q, i, c: (2 * q + 1 - c[0], i, 0))],
            out_specs=pl.BlockSpec((None, tr, SHARD_IN), lambda q, i, c: (q, i, 0))),
        out_shape=jax.ShapeDtypeStruct((N_DEV // 2, D_MODEL, SHARD_IN), _BF),
        compiler_params=_params(("arbitrary", "arbitrary")),
    )(c_idx, g_in)


def _vec_spec():
    return pl.BlockSpec((1, D_MODEL), lambda i: (0, 0))


def _acc_rows(ref, i, rows):
    @pl.when(i == 0)
    def _():
        ref[...] = jnp.zeros_like(ref)

    for n, row in enumerate(rows):
        ref[n:n + 1, :] = ref[n:n + 1, :] + row


def adaln_forward(x, norm_g, scale1p, shift, t_len):
    tm = _row_tile(t_len, 512)

    def body(x_ref, g_ref, sc_ref, sh_ref, h_ref, ht_ref):
        xv = x_ref[...]
        r = lax.rsqrt(jnp.mean(xv * xv, axis=-1, keepdims=True) + EPS)
        h = xv * r * g_ref[...] * sc_ref[...] + sh_ref[...]
        h_ref[...] = h.astype(h_ref.dtype)
        ht_ref[...] = h.T.astype(ht_ref.dtype)

    return pl.pallas_call(
        body, name="adaln_fwd", grid=(t_len // tm,),
        in_specs=[pl.BlockSpec((tm, D_MODEL), lambda i: (i, 0)), _vec_spec(), _vec_spec(), _vec_spec()],
        out_specs=[pl.BlockSpec((tm, D_MODEL), lambda i: (i, 0)), pl.BlockSpec((D_MODEL, tm), lambda i: (0, i))],
        out_shape=[jax.ShapeDtypeStruct((t_len, D_MODEL), _BF), jax.ShapeDtypeStruct((D_MODEL, t_len), _BF)],
        compiler_params=_params(("arbitrary",)),
    )(x, norm_g, scale1p, shift)


def _head_norm(o, g):
    r = lax.rsqrt(jnp.mean(o * o, axis=-1, keepdims=True) + EPS)
    return r, o * r * g


def _group_spec(tm, cb):
    return pl.BlockSpec((N_HEADS, tm, LANE), lambda i, cb=cb: (cb // N_HEADS, i, 0))


MID_FINAL_G, MID_GATE, MID_LOSS, MID_HG_G, MID_RET_G = range(5)


def middle(x, target, oa, ob, pb, wout, gate, final_g, hg_g, ret_g, t_len):
    tm = _row_tile(t_len, 256)
    n_steps = t_len // tm

    def body(x_ref, t_ref, oa_ref, ob_ref, hz_ref, rz_ref, ga_ref, gb_ref, w_ref, gate_ref, fg_ref, hg_ref, rg_ref,
             dy_ref, doa_ref, dob_ref, dw_ref, vec_ref, dpb_ref, m_scr, dm_scr, keep, bufs, sems):
        i = pl.program_id(0)
        rows = pl.ds(pl.multiple_of(i * tm, tm), tm)

        def group_copies(sl):
            return [pltpu.make_async_copy(bufs.at[sl, n], dpb_ref.at[pl.ds(cb, N_HEADS), rows], sems.at[sl, n])
                    for n, cb in enumerate((CB_HZ, CB_RZ, CB_GA, CB_GB))]
        sides = ((oa_ref, hz_ref, ga_ref, hg_ref, doa_ref), (ob_ref, rz_ref, gb_ref, rg_ref, dob_ref))
        for hh in range(N_HEADS):
            ls = slice(hh * LANE, (hh + 1) * LANE)
            acc = None
            for side, (o_ref, z_ref, gt_ref, g_ref, _) in enumerate(sides):
                o = o_ref[hh]
                rr = lax.rsqrt(jnp.mean(o * o, axis=-1, keepdims=True) + EPS)
                orr = o * rr
                zz = z_ref[hh]
                sz = _sigmoid(zz)
                sgt = _sigmoid(gt_ref[hh])
                keep[side, hh, 0] = orr
                keep[side, hh, 1] = sz
                keep[side, hh, 2] = sgt
                keep[side, hh, 3] = jnp.broadcast_to(rr, orr.shape)
                u = sgt * ((orr * g_ref[:, ls]) * (zz * sz))
                acc = u if acc is None else acc + u
            m_scr[:, ls] = acc.astype(m_scr.dtype)
        zv = _dot(m_scr[...], w_ref[...])
        gt, fg = gate_ref[...], fg_ref[...]
        y = x_ref[...] + gt * zv
        r = lax.rsqrt(jnp.mean(y * y, axis=-1, keepdims=True) + EPS)
        yn = y * r
        err = yn * fg - t_ref[...]
        loss = 0.5 * jnp.sum(jnp.mean(err * err, axis=-1, keepdims=True), axis=0, keepdims=True)
        dout = err * (1.0 / D_MODEL)
        gd = dout * fg
        dy = r * gd - yn * (r * r) * jnp.mean(y * gd, axis=-1, keepdims=True)
        dy_ref[...] = dy
        dz = (dy * gt).astype(_BF)
        dm_scr[...] = _dot(dz, w_ref[...], _NT)
        part = _dot(m_scr[...], dz, _TN)

        @pl.when(i == 0)
        def _():
            dw_ref[...] = part

        @pl.when(i > 0)
        def _():
            dw_ref[...] = dw_ref[...] + part

        slot = _writeback_reserve(i, group_copies)
        dg = [[], []]
        for hh in range(N_HEADS):
            ls = slice(hh * LANE, (hh + 1) * LANE)
            dmh = dm_scr[:, ls]
            for side, (o_ref, z_ref, gt_ref, g_ref, do_ref) in enumerate(sides):
                zz, g = z_ref[hh], g_ref[:, ls]
                orr, sz, sgt, rr = keep[side, hh, 0], keep[side, hh, 1], keep[side, hh, 2], keep[side, hh, 3]
                n = orr * g
                silu = zz * sz
                du = dmh * sgt
                bufs[slot, 2 + side, hh] = (dmh * (n * silu) * (sgt * (1.0 - sgt))).astype(bufs.dtype)
                bufs[slot, side, hh] = (du * n * (sz * (1.0 + zz * (1.0 - sz)))).astype(bufs.dtype)
                dn = du * silu
                dg[side].append(jnp.sum(dn * orr, axis=0, keepdims=True))
                gdn = dn * g
                do_ref[hh] = (rr * (gdn - orr * jnp.mean(orr * gdn, axis=-1, keepdims=True))).astype(do_ref.dtype)
        _acc_rows(vec_ref, i, [jnp.sum(dout * yn, axis=0, keepdims=True),
                               jnp.sum(dy * zv, axis=0, keepdims=True),
                               jnp.broadcast_to(loss, (1, D_MODEL)),
                               jnp.concatenate(dg[0], axis=1), jnp.concatenate(dg[1], axis=1)])
        _writeback_commit(i, n_steps, slot, group_copies)

    row = pl.BlockSpec((tm, D_MODEL), lambda i: (i, 0))
    head = pl.BlockSpec((N_HEADS, tm, LANE), lambda i: (0, i, 0))
    full = pl.BlockSpec((D_MODEL, D_MODEL), lambda i: (0, 0))
    return pl.pallas_call(
        body, name="middle", grid=(n_steps,),
        in_specs=[row, row, head, head, _group_spec(tm, CB_HZ), _group_spec(tm, CB_RZ), _group_spec(tm, CB_GA),
                  _group_spec(tm, CB_GB), full, _vec_spec(), _vec_spec(), _vec_spec(), _vec_spec()],
        out_specs=[row, head, head, full, pl.BlockSpec((8, D_MODEL), lambda i: (0, 0)),
                   pl.BlockSpec(memory_space=pl.ANY)],
        out_shape=[jax.ShapeDtypeStruct((t_len, D_MODEL), F32),
                   jax.ShapeDtypeStruct((N_HEADS, t_len, LANE), _BF),
                   jax.ShapeDtypeStruct((N_HEADS, t_len, LANE), _BF),
                   jax.ShapeDtypeStruct((D_MODEL, D_MODEL), F32),
                   jax.ShapeDtypeStruct((8, D_MODEL), F32),
                   jax.ShapeDtypeStruct((N_CB, t_len, LANE), _BF)],
        scratch_shapes=[pltpu.VMEM((tm, D_MODEL), _BF), pltpu.VMEM((tm, D_MODEL), F32),
                        pltpu.VMEM((2, N_HEADS, 4, tm, LANE), F32),
                        pltpu.VMEM((2, 4, N_HEADS, tm, LANE), _BF), pltpu.SemaphoreType.DMA((2, 4))],
        compiler_params=_params(("arbitrary",)),
    )(x, target, oa, ob, pb, pb, pb, pb, wout, gate, final_g, hg_g, ret_g)


def device_step(x, target, mod, lb, wt, wout, norm_g, hg_g, ret_g, final_g, c_idx=None, start_exchange=None):
    t_len = x.shape[0]
    shift, scale, gate = mod[:, :D_MODEL], mod[:, D_MODEL:2 * D_MODEL], mod[:, 2 * D_MODEL:]
    scale1p = 1.0 + scale
    cos_t, sin_t = _rope_tables(t_len)
    h, h_t = adaln_forward(x, norm_g, scale1p, shift, t_len)
    pb = proj_forward(h, wt, t_len)
    oa, ssave, asave = hgrn_forward(pb, lb, t_len)
    ob, rsave = retention_forward(pb, cos_t, sin_t, t_len)
    dy, doa, dob, dwout, vec_mid, dpb = middle(x, target, oa, ob, pb, wout, gate, final_g, hg_g, ret_g, t_len)
    dpb, dlb = hgrn_backward(pb, lb, doa, ssave, asave, dpb, t_len)
    dpb = retention_backward(pb, cos_t, sin_t, dob, rsave, dpb, t_len)
    c_idx = jnp.zeros((1,), jnp.int32) if c_idx is None else c_idx
    dwin = proj_backward_weight(h_t, dpb, t_len)
    dwin_sib = sibling_blocks(dwin, c_idx)
    token, pending = (start_exchange(dwin, dwin_sib, dwout) if start_exchange
                      else (jnp.zeros((8, LANE), F32), None))
    grad_x, vec_ada = proj_backward_input(dpb, wt, token, x, dy, norm_g, scale1p, t_len)
    return grad_x, dwin, dwout, vec_mid, vec_ada, dlb, pending


PACK_ROWS = 16
ROW_NORM_G, ROW_LB, ROW_HG_G, ROW_RET_G, ROW_FINAL_G, ROW_SHIFT, ROW_SCALE, ROW_GATE, ROW_LOSS = range(9)


def _mesh_pos():
    return lax.axis_index("x"), lax.axis_index("y"), lax.axis_index("c")


def _lin(pos):
    return 4 * pos[0] + 2 * pos[1] + pos[2]


def _xor_peer(pos, k):
    return tuple(1 - p if (k >> s) & 1 else p for p, s in zip(pos, (2, 1, 0)))


def _other_chips(pos):
    x, y, _ = pos
    return [(1 - x, y), (x, 1 - y), (1 - x, 1 - y)]


def _remote(src, dst, send_sem, recv_sem, to):
    return pltpu.make_async_remote_copy(src_ref=src, dst_ref=dst, send_sem=send_sem, recv_sem=recv_sem,
                                        device_id=to, device_id_type=MESH)


def pre_exchange(c, w_ada, b_ada, logits, w_in, w_out):
    def body(c_ref, wada_ref, bada_ref, logit_ref, win_ref, wout_ref, mod_ref, scall_ref, lb_ref, wt_ref, wo_ref,
             cg_ref, modall_ref, parts_ref, send1, recv1, send2, recv2):
        pos = _mesh_pos()
        cv = c_ref[...]
        slot = lambda p: pl.ds(pl.multiple_of(8 * _lin(p), 8), 8)
        cg_ref[slot(pos), :] = jnp.broadcast_to(cv * _sigmoid(cv), (8, D_MODEL))
        lb_ref[...] = _sigmoid(logit_ref[0:1, :] - logit_ref[1:2, :])
        peers = [_xor_peer(pos, k) for k in range(1, N_DEV)]
        gather = [_remote(cg_ref.at[slot(pos)], cg_ref.at[slot(pos)], send1.at[n], recv1.at[n], p)
                  for n, p in enumerate(peers)]
        for cp in gather:
            cp.start()
        wt_ref[...] = win_ref[...].T.astype(wt_ref.dtype)
        wo_ref[...] = wout_ref[...].astype(wo_ref.dtype)
        for n, p in enumerate(peers):
            _remote(cg_ref.at[slot(p)], cg_ref.at[slot(p)], send1.at[n], recv1.at[n], p).wait_recv()
        modall_ref[...] = _dot(cg_ref[...], wada_ref[...])
        scatter = [_remote(modall_ref.at[slot(p)], parts_ref.at[slot(pos)], send2.at[n], recv2.at[n], p)
                   for n, p in enumerate(peers)]
        for cp in scatter:
            cp.start()
        parts_ref[slot(pos), :] = modall_ref[slot(pos), :]
        for n, p in enumerate(peers):
            _remote(modall_ref.at[slot(p)], parts_ref.at[slot(p)], send2.at[n], recv2.at[n], p).wait_recv()
        for cp in gather + scatter:
            cp.wait_send()
        for j in range(N_DEV):
            cols = slice(j * SHARD_ADA, (j + 1) * SHARD_ADA)
            mod_ref[:, cols] = parts_ref[8 * j:8 * j + 1, :] + bada_ref[:, cols]
            scall_ref[j:j + 1, :] = cg_ref[8 * j:8 * j + 1, :]

    vmem = pl.BlockSpec(memory_space=pltpu.VMEM)
    return pl.pallas_call(
        body, name="pre_exchange",
        in_specs=[vmem] * 6, out_specs=[vmem] * 5,
        out_shape=[jax.ShapeDtypeStruct((1, 3 * D_MODEL), F32), jax.ShapeDtypeStruct((N_DEV, D_MODEL), F32),
                   jax.ShapeDtypeStruct((1, D_MODEL), F32),
                   jax.ShapeDtypeStruct(w_in.shape[::-1], _BF), jax.ShapeDtypeStruct(w_out.shape, _BF)],
        scratch_shapes=[pltpu.VMEM((N_DEV * 8, D_MODEL), F32), pltpu.VMEM((N_DEV * 8, SHARD_ADA), F32),
                        pltpu.VMEM((N_DEV * 8, SHARD_ADA), F32)] + [pltpu.SemaphoreType.DMA((N_DEV - 1,))] * 4,
        compiler_params=pltpu.CompilerParams(vmem_limit_bytes=VMEM_LIMIT),
    )(c, w_ada, b_ada, logits, w_in, w_out)


def weight_gather(win_sh, wout_sh):
    def body(win_ref, wout_ref, wg_ref, woutg_ref, send, recv, local):
        pos = _mesh_pos()
        x, y, c = pos
        sibling = (x, y, 1 - c)

        def route(core):
            return [(x + (1 - core) * (1 - 2 * x), y + core * (1 - 2 * y)),
                    (x + core * (1 - 2 * x), y + (1 - core) * (1 - 2 * y)),
                    (1 - x, 1 - y)]

        chips, sib_chips = route(c), route(1 - c)
        mine, first, later = [], [], []
        for a, (src, out) in enumerate(((win_ref, wg_ref), (wout_ref, woutg_ref))):
            def copy(k, block, to, src_ref=None, a=a, out=out):
                dst = out.at[_lin(block)]
                return _remote(dst if src_ref is None else src_ref, dst, send.at[7 * a + k], recv.at[7 * a + k], to)
            mine.append(pltpu.make_async_copy(src, out.at[_lin(pos)], local.at[a]))
            first += [copy(0, pos, sibling, src), copy(1, pos, (*chips[0], c), src), copy(2, pos, (*chips[1], c), src)]
            later.append([[copy(3, (*chips[0], c), (*chips[1], c)), copy(4, (*chips[0], c), sibling)],
                          [copy(5, (*chips[1], c), sibling)],
                          [copy(6, (*chips[2], c), sibling)]])
        for cp in mine + first:
            cp.start()
        for j in range(3):
            for a, out in enumerate((wg_ref, woutg_ref)):
                dst = out.at[_lin((*chips[j], c))]
                _remote(dst, dst, send.at[7 * a + 1 + j], recv.at[7 * a + 1 + j], pos).wait_recv()
                for cp in later[a][j]:
                    cp.start()
        for a, out in enumerate((wg_ref, woutg_ref)):
            dst = out.at[_lin(sibling)]
            _remote(dst, dst, send.at[7 * a], recv.at[7 * a], pos).wait_recv()
            for j in range(3):
                dst = out.at[_lin((*sib_chips[j], 1 - c))]
                _remote(dst, dst, send.at[7 * a + 4 + j], recv.at[7 * a + 4 + j], pos).wait_recv()
        for cp in first + [cp for per_array in later for group in per_array for cp in group]:
            cp.wait_send()
        for cp in mine:
            cp.wait()

    any_spec = pl.BlockSpec(memory_space=pl.ANY)
    return pl.pallas_call(
        body, name="weight_gather",
        in_specs=[any_spec, any_spec], out_specs=[any_spec, any_spec],
        out_shape=[jax.ShapeDtypeStruct((N_DEV,) + win_sh.shape, win_sh.dtype),
                   jax.ShapeDtypeStruct((N_DEV,) + wout_sh.shape, wout_sh.dtype)],
        scratch_shapes=[pltpu.SemaphoreType.DMA((14,)), pltpu.SemaphoreType.DMA((14,)),
                        pltpu.SemaphoreType.DMA((2,))],
    )(win_sh, wout_sh)


def grad_pair_exchange(g_sib, g_out):
    def body(gsib_ref, gout_ref, ra_ref, rb_ref, send, recv):
        pos = _mesh_pos()
        x, y, c = pos
        sibling = (x, y, 1 - c)
        copies = []
        for q in range(4):
            copies.append(_remote(gsib_ref.at[q], ra_ref.at[q], send.at[q], recv.at[q], sibling))
            copies.append(_remote(gout_ref.at[2 * q + (1 - c)], rb_ref.at[q], send.at[4 + q], recv.at[4 + q], sibling))
        for cp in copies:
            cp.start()
        for cp in copies:
            cp.wait_recv()
        for cp in copies:
            cp.wait_send()

    any_spec = pl.BlockSpec(memory_space=pl.ANY)
    return pl.pallas_call(
        body, name="grad_pair_exchange",
        in_specs=[any_spec, any_spec], out_specs=[any_spec, any_spec],
        out_shape=[jax.ShapeDtypeStruct(g_sib.shape, g_sib.dtype), jax.ShapeDtypeStruct((4,) + g_out.shape[1:], F32)],
        scratch_shapes=[pltpu.SemaphoreType.DMA((8,)), pltpu.SemaphoreType.DMA((8,))],
    )(g_sib, g_out)


def pair_sum(g_in, ra, g_out, rb, c_idx):
    tr = 256

    def body(c_ref, gin_ref, ra_ref, gout_ref, rb_ref, sb_ref, sbo_ref):
        del c_ref
        sb_ref[...] = (gin_ref[...] + ra_ref[...].astype(F32)).astype(sb_ref.dtype)
        sbo_ref[...] = gout_ref[...] + rb_ref[...]

    n_i = D_MODEL // tr
    return pl.pallas_call(
        body, name="pair_sum",
        grid_spec=pltpu.PrefetchScalarGridSpec(
            num_scalar_prefetch=1, grid=(4, n_i),
            in_specs=[pl.BlockSpec((None, tr, SHARD_IN), lambda q, i, c: (2 * q + c[0], i, 0)),
                      pl.BlockSpec((None, tr, SHARD_IN), lambda q, i, c: (q, i, 0)),
                      pl.BlockSpec((None, SHARD_OUT // n_i, D_MODEL), lambda q, i, c: (2 * q + c[0], i, 0)),
                      pl.BlockSpec((None, SHARD_OUT // n_i, D_MODEL), lambda q, i, c: (q, i, 0))],
            out_specs=[pl.BlockSpec((None, tr, SHARD_IN), lambda q, i, c: (q, i, 0)),
                       pl.BlockSpec((None, SHARD_OUT // n_i, D_MODEL), lambda q, i, c: (q, i, 0))]),
        out_shape=[jax.ShapeDtypeStruct(ra.shape, _BF), jax.ShapeDtypeStruct(rb.shape, F32)],
        compiler_params=_params(("arbitrary", "arbitrary")),
    )(c_idx, g_in, ra, g_out, rb)


_HBM = pl.BlockSpec(memory_space=pltpu.HBM)
_SEM = pl.BlockSpec(memory_space=pltpu.SEMAPHORE)
_N_CHIP_COPIES = 6


def _chip_copies(sb_ref, sbo_ref, rc_ref, rco_ref, send, recv):
    pos = _mesh_pos()
    copies = []
    for a, (src, dst) in enumerate(((sb_ref, rc_ref), (sbo_ref, rco_ref))):
        for j, chip in enumerate(_other_chips(pos)):
            copies.append(_remote(src.at[2 * chip[0] + chip[1]], dst.at[j], send.at[3 * a + j], recv.at[3 * a + j],
                                  (*chip, pos[2])))
    return copies


def grad_chip_start(sb, sbo):
    def body(sb_ref, sbo_ref, rc_ref, rco_ref, send, recv, sb_thru, sbo_thru, rc_thru, rco_thru, token):
        del sb_thru, sbo_thru, rc_thru, rco_thru
        for cp in _chip_copies(sb_ref, sbo_ref, rc_ref, rco_ref, send, recv):
            cp.start()
        token[...] = jnp.zeros_like(token)

    hbm = lambda a: pltpu.with_memory_space_constraint(a, pltpu.HBM)
    rc = lax.empty((3,) + sb.shape[1:], sb.dtype)
    rco = lax.empty((3,) + sbo.shape[1:], sbo.dtype)
    return pl.pallas_call(
        body, name="grad_chip_start",
        in_specs=[_HBM] * 4,
        out_specs=[_SEM, _SEM, _HBM, _HBM, _HBM, _HBM, pl.BlockSpec(memory_space=pltpu.VMEM)],
        out_shape=[pltpu.SemaphoreType.DMA((_N_CHIP_COPIES,)), pltpu.SemaphoreType.DMA((_N_CHIP_COPIES,)),
                   pltpu.HBM(sb.shape, sb.dtype), pltpu.HBM(sbo.shape, sbo.dtype),
                   pltpu.HBM(rc.shape, rc.dtype), pltpu.HBM(rco.shape, rco.dtype),
                   jax.ShapeDtypeStruct((8, LANE), F32)],
        input_output_aliases={0: 2, 1: 3, 2: 4, 3: 5},
        compiler_params=pltpu.CompilerParams(has_side_effects=pltpu.SideEffectType.DATAFLOW_SIDE_EFFECTING),
    )(hbm(sb), hbm(sbo), hbm(rc), hbm(rco))


def grad_chip_wait(send, recv, sb, sbo, rc, rco, after):
    def body(sb_ref, sbo_ref, rc_ref, rco_ref, send, recv, after_ref, sb_o, sbo_o, rc_o, rco_o):
        del after_ref, sb_o, sbo_o, rc_o, rco_o
        for cp in _chip_copies(sb_ref, sbo_ref, rc_ref, rco_ref, send, recv):
            cp.wait_send()
            cp.wait_recv()

    return pl.pallas_call(
        body, name="grad_chip_wait",
        in_specs=[_HBM] * 4 + [_SEM, _SEM, pl.BlockSpec(memory_space=pl.ANY)],
        out_specs=[_HBM] * 4,
        out_shape=[pltpu.HBM(sb.shape, sb.dtype), pltpu.HBM(sbo.shape, sbo.dtype),
                   pltpu.HBM(rc.shape, rc.dtype), pltpu.HBM(rco.shape, rco.dtype)],
        input_output_aliases={0: 0, 1: 1, 2: 2, 3: 3},
        compiler_params=pltpu.CompilerParams(has_side_effects=pltpu.SideEffectType.DATAFLOW_SIDE_EFFECTING),
    )(sb, sbo, rc, rco, send, recv, after)


def pack_gather(pack):
    def body(pack_ref, packs_ref, psend, precv):
        pos = _mesh_pos()
        me = _lin(pos)
        packs_ref[me] = pack_ref[...]
        peers = [_xor_peer(pos, k) for k in range(1, N_DEV)]
        gather = [_remote(packs_ref.at[me], packs_ref.at[me], psend.at[n], precv.at[n], p) for n, p in enumerate(peers)]
        for cp in gather:
            cp.start()
        for n, p in enumerate(peers):
            _remote(packs_ref.at[_lin(p)], packs_ref.at[_lin(p)], psend.at[n], precv.at[n], p).wait_recv()
        for cp in gather:
            cp.wait_send()

    vmem = pl.BlockSpec(memory_space=pltpu.VMEM)
    return pl.pallas_call(
        body, name="pack_gather", in_specs=[vmem], out_specs=vmem,
        out_shape=jax.ShapeDtypeStruct((N_DEV,) + pack.shape, F32),
        scratch_shapes=[pltpu.SemaphoreType.DMA((N_DEV - 1,)), pltpu.SemaphoreType.DMA((N_DEV - 1,))],
    )(pack)


def pack_rows(vec_mid, vec_ada, dlb):
    def body(mid_ref, ada_ref, dlb_ref, o_ref):
        mid = lambda r: mid_ref[r:r + 1, :]
        rows = [ada_ref[0:1, :], dlb_ref[...], mid(MID_HG_G), mid(MID_RET_G), mid(MID_FINAL_G),
                ada_ref[2:3, :], ada_ref[1:2, :], mid(MID_GATE), mid(MID_LOSS)]
        o_ref[...] = jnp.zeros_like(o_ref)
        for n, row in enumerate(rows):
            o_ref[n:n + 1, :] = row

    vmem = pl.BlockSpec(memory_space=pltpu.VMEM)
    return pl.pallas_call(body, name="pack_rows", in_specs=[vmem] * 3, out_specs=vmem,
                          out_shape=jax.ShapeDtypeStruct((PACK_ROWS, D_MODEL), F32))(vec_mid, vec_ada, dlb)


def _adamw(w, g, m, v):
    m = ADAM_B1 * m + (1.0 - ADAM_B1) * g
    v = ADAM_B2 * v + (1.0 - ADAM_B2) * (g * g)
    m_hat = m / (1.0 - ADAM_B1 ** ADAM_STEP)
    v_hat = v / (1.0 - ADAM_B2 ** ADAM_STEP)
    delta = -ADAM_LR * (m_hat / (jnp.sqrt(v_hat) + ADAM_EPS) + ADAM_WD * w)
    return delta, m, v


def adam_shard(chip_idx, own, parts, w, m, v, name):
    rows, cols = w.shape
    tr = min(rows, 128)

    def body(chip_ref, p0, p1, p2, p3, w_ref, m_ref, v_ref, g_ref, d_ref, nm_ref, nv_ref):
        del chip_ref
        g = ((p0[...].astype(F32) + p1[...].astype(F32)) + p2[...].astype(F32)) + p3[...].astype(F32)
        g_ref[...] = g
        d_ref[...], nm_ref[...], nv_ref[...] = _adamw(w_ref[...], g, m_ref[...], v_ref[...])

    part = lambda q: pl.BlockSpec((None, tr, cols), lambda i, chip, q=q: (q, i, 0))
    tile = pl.BlockSpec((tr, cols), lambda i, chip: (i, 0))
    return pl.pallas_call(
        body, name=name,
        grid_spec=pltpu.PrefetchScalarGridSpec(
            num_scalar_prefetch=1, grid=(rows // tr,),
            in_specs=[pl.BlockSpec((None, tr, cols), lambda i, chip: (chip[0], i, 0)), part(0), part(1), part(2),
                      tile, tile, tile],
            out_specs=[tile] * 4),
        out_shape=[jax.ShapeDtypeStruct(w.shape, F32)] * 4,
        compiler_params=_params(("arbitrary",)),
    )(chip_idx, own, parts, parts, parts, w, m, v)


def adam_ada(sc_t, dmod_all, me_idx, w, m, v):
    def body(me_ref, sc_ref, dm_ref, w_ref, m_ref, v_ref, g_ref, d_ref, nm_ref, nv_ref):
        del me_ref
        g = _dot_f32(sc_ref[...], dm_ref[...])
        g_ref[...] = g
        d_ref[...], nm_ref[...], nv_ref[...] = _adamw(w_ref[...], g, m_ref[...], v_ref[...])

    full = pl.BlockSpec(w.shape, lambda i, me: (0, 0))
    return pl.pallas_call(
        body, name="adam_ada",
        grid_spec=pltpu.PrefetchScalarGridSpec(
            num_scalar_prefetch=1, grid=(1,),
            in_specs=[pl.BlockSpec(sc_t.shape, lambda i, me: (0, 0)),
                      pl.BlockSpec((LANE, SHARD_ADA), lambda i, me: (0, me[0])), full, full, full],
            out_specs=[full] * 4),
        out_shape=[jax.ShapeDtypeStruct(w.shape, F32)] * 4,
        compiler_params=_params(("arbitrary",)),
    )(me_idx, sc_t, dmod_all, w, m, v)


def adam_vectors(packs, lb, params, ms, vs):
    n = len(params)

    def body(*refs):
        packs_ref, lb_ref = refs[0], refs[1]
        w_refs, m_refs, v_refs = refs[2:2 + n], refs[2 + n:2 + 2 * n], refs[2 + 2 * n:2 + 3 * n]
        loss_ref = refs[2 + 3 * n]
        outs = refs[3 + 3 * n:3 + 7 * n]
        tot_ref = refs[3 + 7 * n]
        tot = packs_ref[0]
        for d in range(1, N_DEV):
            tot = tot + packs_ref[d]
        tot_ref[...] = tot
        row = lambda r: tot_ref[r:r + 1, :]
        lbv = lb_ref[...]
        dl0 = row(ROW_LB) * lbv * (1.0 - lbv)
        grads = [[row(ROW_NORM_G)],
                 [jnp.concatenate([row(ROW_SHIFT), row(ROW_SCALE), row(ROW_GATE)], axis=1)],
                 [dl0, -dl0],
                 [row(ROW_HG_G)], [row(ROW_RET_G)], [row(ROW_FINAL_G)]]
        loss_ref[...] = tot_ref[ROW_LOSS:ROW_LOSS + 1, 0:LANE]
        for j, g_rows in enumerate(grads):
            for r, g in enumerate(g_rows):
                rs = slice(r, r + 1)
                d, nm, nv = _adamw(w_refs[j][rs, :], g, m_refs[j][rs, :], v_refs[j][rs, :])
                outs[4 * j][rs, :] = g
                outs[4 * j + 1][rs, :] = d
                outs[4 * j + 2][rs, :] = nm
                outs[4 * j + 3][rs, :] = nv

    vmem = pl.BlockSpec(memory_space=pltpu.VMEM)
    out_shape = [jax.ShapeDtypeStruct((1, LANE), F32)]
    for w in params:
        out_shape += [jax.ShapeDtypeStruct(w.shape, F32)] * 4
    return pl.pallas_call(
        body, name="adam_vectors", in_specs=[vmem] * (2 + 3 * n), out_specs=[vmem] * len(out_shape),
        out_shape=out_shape, scratch_shapes=[pltpu.VMEM((PACK_ROWS, D_MODEL), F32)],
    )(packs, lb, *params, *ms, *vs)


def kernel(x, c, norm_g, w_ada, b_ada, w_in, hg_lb_logits, hg_norm_g, ret_norm_g, w_out, final_g, loss_target, m_norm_g, m_w_ada, m_b_ada, m_w_in, m_hg_lb_logits, m_hg_norm_g, m_ret_norm_g, m_w_out, m_final_g, v_norm_g, v_w_ada, v_b_ada, v_w_in, v_hg_lb_logits, v_hg_norm_g, v_ret_norm_g, v_w_out, v_final_g):
    pos = _mesh_pos()
    me_idx = jnp.reshape(_lin(pos), (1,)).astype(jnp.int32)
    c_idx = jnp.reshape(pos[2], (1,)).astype(jnp.int32)
    vec = lambda a: a.reshape(1, D_MODEL)

    mod, scall, lb, wt_sh, wout_sh = pre_exchange(c, w_ada[0], b_ada, hg_lb_logits, w_in[0], w_out[0])
    wtg, woutg = weight_gather(wt_sh, wout_sh)
    chip_idx = jnp.reshape(2 * pos[0] + pos[1], (1,)).astype(jnp.int32)

    def start_exchange(dwin, dwin_sib, dwout):
        dwout = dwout.reshape(N_DEV, SHARD_OUT, D_MODEL)
        ra, rb = grad_pair_exchange(dwin_sib, dwout)
        sb, sbo = pair_sum(dwin, ra, dwout, rb, c_idx)
        send, recv, sb, sbo, rc, rco, token = grad_chip_start(sb, sbo)
        return token, (send, recv, sb, sbo, rc, rco)

    grad_x, _, _, vec_mid, vec_ada, dlb, pending = device_step(
        x[0], loss_target[0], mod, lb, wtg.reshape(D_IN, D_MODEL), woutg.reshape(D_MODEL, D_MODEL), norm_g,
        hg_norm_g, ret_norm_g, vec(final_g), c_idx, start_exchange)
    packs = pack_gather(pack_rows(vec_mid, vec_ada, dlb))
    dmod_all = packs[:, ROW_SHIFT:ROW_GATE + 1, :].reshape(N_DEV, 3 * D_MODEL)
    dmod_all = jnp.pad(dmod_all, ((0, LANE - N_DEV), (0, 0)))
    sc_t = jnp.pad(scall.T, ((0, 0), (0, LANE - N_DEV)))
    g_ada, d_ada, nm_ada, nv_ada = adam_ada(sc_t, dmod_all, me_idx, w_ada[0], m_w_ada[0], v_w_ada[0])
    small = adam_vectors(
        packs, lb,
        (norm_g, b_ada, hg_lb_logits, hg_norm_g, ret_norm_g, vec(final_g)),
        (m_norm_g, m_b_ada, m_hg_lb_logits, m_hg_norm_g, m_ret_norm_g, vec(m_final_g)),
        (v_norm_g, v_b_ada, v_hg_lb_logits, v_hg_norm_g, v_ret_norm_g, vec(v_final_g)))
    loss = small[0][0, 0]
    sb, sbo, rc, rco = grad_chip_wait(*pending, small[0])
    g_in, d_in, nm_in, nv_in = adam_shard(chip_idx, sb, rc, w_in[0], m_w_in[0], v_w_in[0], "adam_w_in")
    g_out, d_out, nm_out, nv_out = adam_shard(chip_idx, sbo, rco, w_out[0], m_w_out[0], v_w_out[0], "adam_w_out")
    (g_ng, d_ng, nm_ng, nv_ng), (g_b, d_b, nm_b, nv_b), (g_lb, d_lb, nm_lb, nv_lb), (g_hg, d_hg, nm_hg, nv_hg), \
        (g_rg, d_rg, nm_rg, nv_rg), (g_fg, d_fg, nm_fg, nv_fg) = [small[1 + 4 * j:5 + 4 * j] for j in range(6)]
    flat = lambda a: a.reshape(D_MODEL)

    def group(ng, ada, b, win, lbl, hg, rg, wo, fg):
        return (ng, ada[None], b, win[None], lbl, hg, rg, wo[None], flat(fg))

    return (loss, grad_x[None],
            *group(g_ng, g_ada, g_b, g_in, g_lb, g_hg, g_rg, g_out, g_fg),
            *group(d_ng, d_ada, d_b, d_in, d_lb, d_hg, d_rg, d_out, d_fg),
            *group(nm_ng, nm_ada, nm_b, nm_in, nm_lb, nm_hg, nm_rg, nm_out, nm_fg),
            *group(nv_ng, nv_ada, nv_b, nv_in, nv_lb, nv_hg, nv_rg, nv_out, nv_fg))
```

```python
import functools

import numpy as np
import jax
import jax.numpy as jnp
from jax import lax
from jax.experimental import pallas as pl
from jax.experimental.pallas import tpu as pltpu

F32 = jnp.float32
_BF = jnp.bfloat16

D_MODEL = 1024
N_HEADS = 8
LANE = 128
RET_DK = 64
D_IN = 9216
N_DEV = 8
SHARD_IN = D_IN // N_DEV
SHARD_ADA = 3 * D_MODEL // N_DEV
SHARD_OUT = D_MODEL // N_DEV
N_CB = D_IN // LANE
CB_PER_SHARD = SHARD_IN // LANE
CHUNK = 128
N_LEVELS = 7
EPS = 1e-6
LOG2_E = float(np.log2(np.e))
ROPE_BASE = 10000.0
CB_HQ, CB_HF, CB_HI, CB_HZ, CB_RQ, CB_RK, CB_RV, CB_RZ, CB_GA, CB_GB = 0, 8, 16, 24, 32, 36, 40, 48, 56, 64
VMEM_LIMIT = 56 * 1024 * 1024

ADAM_LR, ADAM_B1, ADAM_B2, ADAM_EPS, ADAM_WD, ADAM_STEP = 0.001, 0.9, 0.999, 1e-08, 0.01, 10

_NN = (((1,), (0,)), ((), ()))
_NT = (((1,), (1,)), ((), ()))
_TN = (((0,), (0,)), ((), ()))
MESH = pl.DeviceIdType.MESH


def _dot(a, b, dims=_NN):
    return lax.dot_general(a.astype(_BF), b.astype(_BF), dims, preferred_element_type=F32)


def _split2(a):
    hi = a.astype(_BF)
    lo = (a - hi.astype(F32)).astype(_BF)
    return jnp.concatenate([hi, lo], axis=1)


def _dot_sel(sel, a):
    n = a.shape[1]
    r = lax.dot_general(sel.astype(_BF), _split2(a), _NN, preferred_element_type=F32)
    return r[:, :n] + r[:, n:]


def _dot_f32(a, b):
    def pieces(v):
        p1 = v.astype(_BF)
        r1 = v - p1.astype(F32)
        p2 = r1.astype(_BF)
        p3 = (r1 - p2.astype(F32)).astype(_BF)
        return p1, p2, p3
    a1, a2, a3 = pieces(a)
    b1, b2, b3 = pieces(b)
    d = lambda u, v: lax.dot_general(u, v, _NN, preferred_element_type=F32)
    return ((d(a1, b3) + d(a2, b2) + d(a3, b1)) + (d(a1, b2) + d(a2, b1))) + d(a1, b1)


def _sigmoid(v):
    return 1.0 / (1.0 + jnp.exp(-v))


def _params(sem=None):
    return pltpu.CompilerParams(dimension_semantics=sem, vmem_limit_bytes=VMEM_LIMIT)


def _hgrn_consts():
    c, nl = CHUNK, N_LEVELS
    t = np.arange(c)[:, None]
    j = np.arange(c)[None, :]
    sel = [j <= t]
    masks = [j == t]
    for l in range(1, nl + 1):
        m = ((t >> l) << l) + (1 << (l - 1)) - 1
        sec = t > m
        sel.append(np.where(sec, (j > m) & (j <= t), (j > t) & (j <= m)))
        same = (t >> l) == (j >> l)
        masks.append(same & sec & (j <= m))
    sel.append(j > t)
    sel = np.concatenate(sel, 0).astype(np.float32)
    masks = np.stack(masks).astype(np.float32)
    sgn = np.stack([np.where((t & (1 << (l - 1))) != 0, 1.0, -1.0) * np.ones((1, LANE)) for l in range(3, nl + 1)])
    return dict(tri=jnp.asarray(sel[:c], _BF),
                lvl=jnp.asarray(masks, F32),
                sgn=jnp.asarray(sgn, F32),
                sel_t=jnp.asarray(sel.T, _BF),
                lvl_b=jnp.asarray(masks, _BF),
                lvlt_b=jnp.asarray(np.swapaxes(masks, 1, 2), _BF))


def _level_exponents(b, logf, b_scr, sgn_ref):
    c = CHUNK
    b_scr[...] = b
    row = lax.broadcasted_iota(jnp.int32, (c, LANE), 0)
    nxt = pltpu.roll(logf, c - 1, 0)
    prv = pltpu.roll(logf, 1, 0)
    r4 = row & 3
    out = [jnp.where((row & 1) == 1, logf, 0.0),
           jnp.where(r4 == 0, nxt, jnp.where(r4 == 1, 0.0, jnp.where(r4 == 2, logf, logf + prv)))]
    for l in range(3, N_LEVELS + 1):
        size, half = 1 << l, 1 << (l - 1)
        ref = jnp.concatenate([jnp.broadcast_to(b_scr[i * size + half - 1:i * size + half, :], (size, LANE))
                               for i in range(c // size)], axis=0)
        out.append((b - ref) * sgn_ref[l - 3])
    return out


def _hgrn_chunk(hq, hf, hi, lbv, tri_ref, sgn_ref, b_scr):
    sq = _sigmoid(hq)
    q = hq * sq
    sg = _sigmoid(hf)
    omlb = 1.0 - lbv
    f = lbv + omlb * sg
    k = 1.0 - f
    logf = jnp.log(f) * LOG2_E
    b = _dot_sel(tri_ref[...], logf)
    bc = jnp.sum(logf, axis=0, keepdims=True)
    lev = [None] + [jnp.exp2(e) for e in _level_exponents(b, logf, b_scr, sgn_ref)]
    return dict(sq=sq, q=q, sg=sg, omlb=omlb, f=f, k=k, v=hi, eb=jnp.exp2(b), erem=jnp.exp2(bc - b),
                ebc=jnp.exp2(bc), lev=lev)


def _blockdiag(a, b):
    z = jnp.zeros_like(a)
    return jnp.concatenate([jnp.concatenate([a, z], axis=1), jnp.concatenate([z, b], axis=1)], axis=0)


def _level_operands(a):
    q, k = a["q"].astype(_BF), a["k"].astype(_BF)
    lev = [None] + [a["lev"][l].astype(_BF) for l in range(1, N_LEVELS + 1)]
    ql = [q] + [q * lev[l] for l in range(1, N_LEVELS + 1)]
    kl = [k] + [k * lev[l] for l in range(1, N_LEVELS + 1)]
    pairs = range(0, N_LEVELS + 1, 2)
    return ([jnp.concatenate([ql[l], ql[l + 1]], axis=1) for l in pairs], [_blockdiag(kl[l], kl[l + 1]) for l in pairs],
            ql, kl)


def _hgrn_scores(a, lvl_ref, q_pairs, k_diags):
    acc = None
    for n, (qp, kd) in enumerate(zip(q_pairs, k_diags)):
        both = lax.dot_general(qp, kd, _NT, preferred_element_type=F32)
        part = lvl_ref[2 * n] * both[:, :CHUNK] + lvl_ref[2 * n + 1] * both[:, CHUNK:]
        acc = part if acc is None else acc + part
    return acc


SCAN_UNROLL = 4
RET_UNROLL = 8


def _writeback_reserve(step, make_copies):
    slot = step % 2

    @pl.when(step >= 2)
    def _():
        for cp in make_copies(slot):
            cp.wait()

    return slot


def _writeback_commit(step, n_steps, slot, make_copies):
    for cp in make_copies(slot):
        cp.start()

    @pl.when(step == n_steps - 1)
    def _():
        for cp in make_copies(slot):
            cp.wait()
        if n_steps > 1:
            for cp in make_copies(1 - slot):
                cp.wait()


def _resident(const):
    zeros = (0,) * const.ndim
    return pl.BlockSpec(const.shape, lambda p, t: zeros)


def _time_block(t_len):
    return min(t_len, 2048)


def hgrn_forward(pb, lb, t_len):
    nc = t_len // CHUNK
    tb = _time_block(t_len)
    ncb = tb // CHUNK
    consts = _hgrn_consts()
    operands = [consts[n] for n in ("tri", "lvl", "sgn")]

    def body(hq_ref, hf_ref, hi_ref, lb_ref, tri_ref, lvl_ref, sgn_ref, o_ref, ssave_ref, asave_ref, st_ref, b_scr):
        @pl.when(pl.program_id(1) == 0)
        def _():
            st_ref[...] = jnp.zeros_like(st_ref)

        def chunk(ci, carry):
            r = pl.ds(pl.multiple_of(ci * CHUNK, CHUNK), CHUNK)
            for hd in range(2):
                lbv = lb_ref[:, hd * LANE:(hd + 1) * LANE]
                a = _hgrn_chunk(hq_ref[hd, r, :], hf_ref[hd, r, :], hi_ref[hd, r, :], lbv, tri_ref, sgn_ref,
                                b_scr.at[hd])
                q_pairs, k_diags, _, _ = _level_operands(a)
                st = st_ref[hd]
                ssave_ref[hd, ci] = st
                scores = _hgrn_scores(a, lvl_ref, q_pairs, k_diags).astype(asave_ref.dtype)
                asave_ref[hd, ci] = scores
                o_ref[hd, r, :] = _dot(a["q"] * a["eb"], st, _NT) + _dot(scores, a["v"])
                st_ref[hd] = st * a["ebc"] + _dot(a["v"], a["k"] * a["erem"], _TN)
            return carry

        lax.fori_loop(0, ncb, chunk, 0, unroll=SCAN_UNROLL)

    pair = lambda base: pl.BlockSpec((2, tb, LANE), lambda p, t, base=base: (base // 2 + p, t, 0))
    per_chunk = pl.BlockSpec((2, ncb, LANE, LANE), lambda p, t: (p, t, 0, 0))
    return pl.pallas_call(
        body, name="hgrn_fwd", grid=(N_HEADS // 2, t_len // tb),
        in_specs=[pair(CB_HQ), pair(CB_HF), pair(CB_HI),
                  pl.BlockSpec((1, 2 * LANE), lambda p, t: (0, p))] + [_resident(c) for c in operands],
        out_specs=[pl.BlockSpec((2, tb, LANE), lambda p, t: (p, t, 0)), per_chunk, per_chunk],
        out_shape=[jax.ShapeDtypeStruct((N_HEADS, t_len, LANE), F32),
                   jax.ShapeDtypeStruct((N_HEADS, nc, LANE, LANE), F32),
                   jax.ShapeDtypeStruct((N_HEADS, nc, CHUNK, CHUNK), _BF)],
        scratch_shapes=[pltpu.VMEM((2, LANE, LANE), F32), pltpu.VMEM((2, CHUNK, LANE), F32)],
        compiler_params=_params(("arbitrary", "arbitrary")),
    )(pb, pb, pb, lb, *operands)


def hgrn_backward(pb, lb, do, ssave, asave, dpb, t_len):
    tb = _time_block(t_len)
    ncb, ntb = tb // CHUNK, t_len // tb
    consts = _hgrn_consts()
    operands = [consts[n] for n in ("tri", "sgn", "sel_t", "lvl_b", "lvlt_b")]

    def body(hq_ref, hf_ref, hi_ref, lb_ref, do_ref, ssave_ref, asave_ref, tri_ref, sgn_ref, selt_ref, lvlb_ref,
             lvltb_ref, dpb_in, dpb_ref, dlb_ref, dq_buf, df_buf, di_buf, dst_ref, b_scr, sems):
        del dpb_in
        p, t = pl.program_id(0), pl.program_id(1)
        step = p * ntb + t
        rows = pl.ds(pl.multiple_of((ntb - 1 - t) * tb, tb), tb)

        def out_copies(sl):
            return [pltpu.make_async_copy(buf.at[sl], dpb_ref.at[pl.ds(base + 2 * p, 2), rows], sems.at[sl, n])
                    for n, (buf, base) in enumerate(((dq_buf, CB_HQ), (df_buf, CB_HF), (di_buf, CB_HI)))]

        slot = _writeback_reserve(step, out_copies)

        @pl.when(t == 0)
        def _():
            dst_ref[...] = jnp.zeros_like(dst_ref)
            dlb_ref[...] = jnp.zeros_like(dlb_ref)

        def chunk(i, carry):
            ci = ncb - 1 - i
            r = pl.ds(pl.multiple_of(ci * CHUNK, CHUNK), CHUNK)
            for hd in range(2):
                head_chunk(hd, ci, r)
            return carry

        def head_chunk(hd, ci, r):
            lbv = lb_ref[:, hd * LANE:(hd + 1) * LANE]
            hq = hq_ref[hd, r, :]
            a = _hgrn_chunk(hq, hf_ref[hd, r, :], hi_ref[hd, r, :], lbv, tri_ref, sgn_ref, b_scr.at[hd])
            _, k_diags, ql, kl = _level_operands(a)
            q, k, v = a["q"], a["k"], a["v"]
            g = do_ref[hd, r, :]
            st0 = ssave_ref[hd, ci]
            dst = dst_ref[hd]
            scores = asave_ref[hd, ci]
            da = _dot(g, v, _NT)
            da_t = _dot(v, g, _NT)
            kb = k * a["erem"]
            qb = q * a["eb"]
            dv = _dot(scores, g, _TN) + _dot(kb, dst, _NT)
            dq_inter = _dot(g, st0) * a["eb"]
            dk_state = _dot(v, dst) * a["erem"]
            dq, dk = dq_inter, dk_state
            de = [q * dq_inter]
            da_b, dat_b = da.astype(_BF), da_t.astype(_BF)
            for n in range(len(k_diags)):
                l0, l1 = 2 * n, 2 * n + 1
                da_pair = jnp.concatenate([lvlb_ref[l0] * da_b, lvlb_ref[l1] * da_b], axis=1)
                dat_pair = jnp.concatenate([lvltb_ref[l0] * dat_b, lvltb_ref[l1] * dat_b], axis=1)
                dq_both = lax.dot_general(da_pair, k_diags[n], _NN, preferred_element_type=F32)
                dk_both = lax.dot_general(dat_pair, _blockdiag(ql[l0], ql[l1]), _NN, preferred_element_type=F32)
                for l, cols in ((l0, slice(0, LANE)), (l1, slice(LANE, 2 * LANE))):
                    dql, dkl = dq_both[:, cols], dk_both[:, cols]
                    if l > 0:
                        e = a["lev"][l]
                        dql, dkl = dql * e, dkl * e
                        de.append(q * dql + k * dkl)
                    dq = dq + dql
                    dk = dk + dkl
            de.append(k * dk_state)
            dst_ref[hd] = dst * a["ebc"] + _dot(g, qb, _TN)
            dbc = jnp.sum(dst * st0, axis=0, keepdims=True) * a["ebc"]
            de2 = lax.dot_general(selt_ref[...], _split2(jnp.concatenate(de, axis=0)), _NN,
                                  preferred_element_type=F32)
            dlogf = de2[:, :LANE] + de2[:, LANE:] + dbc
            sq, sg = a["sq"], a["sg"]
            df = dlogf / a["f"] - dk
            dq_buf[slot, hd, r, :] = (dq * (sq * (1.0 + hq * (1.0 - sq)))).astype(dq_buf.dtype)
            df_buf[slot, hd, r, :] = (df * a["omlb"] * sg * (1.0 - sg)).astype(df_buf.dtype)
            di_buf[slot, hd, r, :] = dv.astype(di_buf.dtype)
            cols = slice(hd * LANE, (hd + 1) * LANE)
            dlb_ref[:, cols] = dlb_ref[:, cols] + jnp.sum(df * (1.0 - sg), axis=0, keepdims=True)

        lax.fori_loop(0, ncb, chunk, 0, unroll=SCAN_UNROLL)
        _writeback_commit(step, (N_HEADS // 2) * ntb, slot, out_copies)

    pair = lambda base: pl.BlockSpec((2, tb, LANE), lambda p, t, base=base: (base // 2 + p, ntb - 1 - t, 0))
    any_spec = pl.BlockSpec(memory_space=pl.ANY)
    per_chunk = pl.BlockSpec((2, ncb, LANE, LANE), lambda p, t: (p, ntb - 1 - t, 0, 0))
    return pl.pallas_call(
        body, name="hgrn_bwd", grid=(N_HEADS // 2, ntb),
        in_specs=[pair(CB_HQ), pair(CB_HF), pair(CB_HI),
                  pl.BlockSpec((1, 2 * LANE), lambda p, t: (0, p)),
                  pair(0), per_chunk, per_chunk]
        + [_resident(c) for c in operands] + [any_spec],
        out_specs=[any_spec, pl.BlockSpec((1, 2 * LANE), lambda p, t: (0, p))],
        out_shape=[jax.ShapeDtypeStruct(dpb.shape, dpb.dtype), jax.ShapeDtypeStruct((1, D_MODEL), F32)],
        scratch_shapes=[pltpu.VMEM((2, 2, tb, LANE), dpb.dtype)] * 3 + [
            pltpu.VMEM((2, LANE, LANE), F32), pltpu.VMEM((2, CHUNK, LANE), F32), pltpu.SemaphoreType.DMA((2, 3))],
        input_output_aliases={7 + len(operands): 0},
        compiler_params=_params(("arbitrary", "arbitrary")),
    )(pb, pb, pb, lb, do, ssave, asave, *operands, dpb)


def _rope_tables(t_len):
    half = RET_DK // 2
    inv_freq = (1.0 / (np.float32(ROPE_BASE) ** np.linspace(0.0, 1.0, half, dtype=np.float32))).astype(np.float32)
    ang = (np.arange(t_len, dtype=np.float32)[:, None] * inv_freq[None, :]).astype(np.float64)
    cos, sin = np.cos(ang).astype(np.float32), np.sin(ang).astype(np.float32)
    cos_t = np.concatenate([cos, cos, cos, cos], axis=1)
    sin_t = np.concatenate([-sin, sin, -sin, sin], axis=1)
    return jnp.asarray(cos_t), jnp.asarray(sin_t)


def _swap_halves(v):
    half = RET_DK // 2
    lane = lax.broadcasted_iota(jnp.int32, v.shape, 1)
    first = (lane & (RET_DK - 1)) < half
    return jnp.where(first, pltpu.roll(v, LANE - half, 1), pltpu.roll(v, half, 1))


def _ret_head_consts(hidx):
    c = CHUNK
    hf = jnp.full((1, LANE), hidx, jnp.int32).astype(F32)
    lg = jnp.log(1.0 - jnp.exp(-(5.0 + hf) * np.float32(np.log(2.0))))
    row = lax.broadcasted_iota(jnp.int32, (c, c), 0)
    col = lax.broadcasted_iota(jnp.int32, (c, c), 1)
    rel = (row - col).astype(F32)
    dm = jnp.where(rel >= 0, jnp.exp(lg[:, :1] * jnp.maximum(rel, 0.0)), 0.0)
    dm_t = jnp.where(rel <= 0, jnp.exp(lg[:, :1] * jnp.maximum(-rel, 0.0)), 0.0)
    idx = lax.broadcasted_iota(jnp.int32, (c, LANE), 0).astype(F32)
    zeta = jnp.exp(lg * (c - 1.0 - idx))
    xi = jnp.exp(lg * (idx + 1.0))
    cdec = jnp.exp(lg * float(c))
    return dm, zeta, xi, cdec, dm_t


def _lane_mask(which):
    lane = lax.broadcasted_iota(jnp.int32, (1, LANE), 1)
    return ((lane // RET_DK) == which).astype(F32)


def retention_forward(pb, cos_t, sin_t, t_len):
    nc = t_len // CHUNK

    tb = _time_block(t_len)
    ncb = tb // CHUNK

    def body(rq_ref, rk_ref, rv_ref, cos_ref, sin_ref, o_ref, rsave_ref, st_ref):
        p = pl.program_id(0)

        @pl.when(pl.program_id(1) == 0)
        def _():
            st_ref[...] = jnp.zeros_like(st_ref)

        consts = [_ret_head_consts(2 * p + hd) for hd in range(2)]

        def chunk(ci, carry):
            r = pl.ds(pl.multiple_of(ci * CHUNK, CHUNK), CHUNK)
            cs, sn = cos_ref[r, :], sin_ref[r, :]
            q = rq_ref[r, :]
            k = rk_ref[r, :]
            q = q * cs + _swap_halves(q) * sn
            k = (k * cs + _swap_halves(k) * sn) * RET_DK ** -0.5
            for hd in range(2):
                dm, zeta, xi, cdec, _ = consts[hd]
                lm = _lane_mask(hd)
                qh, kh = q * lm, k * lm
                v = rv_ref[hd, r, :]
                st = st_ref[hd]
                rsave_ref[hd, ci] = st
                scores = _dot(qh, kh, _NT) * dm
                o_ref[hd, r, :] = _dot(scores, v) + _dot(qh * xi, st, _NT)
                st_ref[hd] = st * cdec + _dot(v, kh * zeta, _TN)
            return carry

        lax.fori_loop(0, ncb, chunk, 0, unroll=RET_UNROLL)

    return pl.pallas_call(
        body, name="ret_fwd", grid=(N_HEADS // 2, t_len // tb),
        in_specs=[pl.BlockSpec((None, tb, LANE), lambda p, t: (CB_RQ + p, t, 0)),
                  pl.BlockSpec((None, tb, LANE), lambda p, t: (CB_RK + p, t, 0)),
                  pl.BlockSpec((2, tb, LANE), lambda p, t: (CB_RV // 2 + p, t, 0)),
                  pl.BlockSpec((tb, LANE), lambda p, t: (t, 0)),
                  pl.BlockSpec((tb, LANE), lambda p, t: (t, 0))],
        out_specs=[pl.BlockSpec((2, tb, LANE), lambda p, t: (p, t, 0)),
                   pl.BlockSpec((2, ncb, LANE, LANE), lambda p, t: (p, t, 0, 0))],
        out_shape=[jax.ShapeDtypeStruct((N_HEADS, t_len, LANE), F32),
                   jax.ShapeDtypeStruct((N_HEADS, nc, LANE, LANE), F32)],
        scratch_shapes=[pltpu.VMEM((2, LANE, LANE), F32)],
        compiler_params=_params(("arbitrary", "arbitrary")),
    )(pb, pb, pb, cos_t, sin_t)


def retention_backward(pb, cos_t, sin_t, do, rsave, dpb, t_len):
    tb = _time_block(t_len)
    ncb, ntb = tb // CHUNK, t_len // tb

    def body(rq_ref, rk_ref, rv_ref, cos_ref, sin_ref, do_ref, rsave_ref, dpb_in,
             dpb_ref, dq_buf, dk_buf, dv_buf, dst_ref, sems):
        del dpb_in
        p, t = pl.program_id(0), pl.program_id(1)
        step = p * ntb + t
        rows = pl.ds(pl.multiple_of((ntb - 1 - t) * tb, tb), tb)

        def out_copies(sl):
            return [pltpu.make_async_copy(dq_buf.at[sl], dpb_ref.at[CB_RQ + p, rows], sems.at[sl, 0]),
                    pltpu.make_async_copy(dk_buf.at[sl], dpb_ref.at[CB_RK + p, rows], sems.at[sl, 1]),
                    pltpu.make_async_copy(dv_buf.at[sl], dpb_ref.at[pl.ds(CB_RV + 2 * p, 2), rows], sems.at[sl, 2])]

        slot = _writeback_reserve(step, out_copies)

        @pl.when(t == 0)
        def _():
            dst_ref[...] = jnp.zeros_like(dst_ref)

        consts = [_ret_head_consts(2 * p + hd) for hd in range(2)]

        def chunk(i, carry):
            ci = ncb - 1 - i
            r = pl.ds(pl.multiple_of(ci * CHUNK, CHUNK), CHUNK)
            cs, sn = cos_ref[r, :], sin_ref[r, :]
            q = rq_ref[r, :]
            k = rk_ref[r, :]
            q = q * cs + _swap_halves(q) * sn
            k = (k * cs + _swap_halves(k) * sn) * RET_DK ** -0.5
            dq, dk = None, None
            for hd in range(2):
                dm, zeta, xi, cdec, dm_t = consts[hd]
                lm = _lane_mask(hd)
                qh, kh = q * lm, k * lm
                v = rv_ref[hd, r, :]
                g = do_ref[hd, r, :]
                st0 = rsave_ref[hd, ci]
                dst = dst_ref[hd]
                scores_t = _dot(kh, qh, _NT) * dm_t
                dsc = _dot(g, v, _NT) * dm
                dsc_t = _dot(v, g, _NT) * dm_t
                dqh = _dot(dsc, kh) + _dot(g, st0) * xi
                dkh = _dot(dsc_t, qh) + _dot(v, dst) * zeta
                dv_buf[slot, hd, r, :] = (_dot(scores_t, g) + _dot(kh * zeta, dst, _NT)).astype(dv_buf.dtype)
                dst_ref[hd] = dst * cdec + _dot(g, qh * xi, _TN)
                dq = dqh if dq is None else dq + dqh
                dk = dkh if dk is None else dk + dkh
            dk = dk * (RET_DK ** -0.5)
            dq_buf[slot, r, :] = (dq * cs - _swap_halves(dq) * sn).astype(dq_buf.dtype)
            dk_buf[slot, r, :] = (dk * cs - _swap_halves(dk) * sn).astype(dk_buf.dtype)
            return carry

        lax.fori_loop(0, ncb, chunk, 0, unroll=RET_UNROLL)
        _writeback_commit(step, (N_HEADS // 2) * ntb, slot, out_copies)

    any_spec = pl.BlockSpec(memory_space=pl.ANY)
    return pl.pallas_call(
        body, name="ret_bwd", grid=(N_HEADS // 2, ntb),
        in_specs=[pl.BlockSpec((None, tb, LANE), lambda p, t: (CB_RQ + p, ntb - 1 - t, 0)),
                  pl.BlockSpec((None, tb, LANE), lambda p, t: (CB_RK + p, ntb - 1 - t, 0)),
                  pl.BlockSpec((2, tb, LANE), lambda p, t: (CB_RV // 2 + p, ntb - 1 - t, 0)),
                  pl.BlockSpec((tb, LANE), lambda p, t: (ntb - 1 - t, 0)),
                  pl.BlockSpec((tb, LANE), lambda p, t: (ntb - 1 - t, 0)),
                  pl.BlockSpec((2, tb, LANE), lambda p, t: (p, ntb - 1 - t, 0)),
                  pl.BlockSpec((2, ncb, LANE, LANE), lambda p, t: (p, ntb - 1 - t, 0, 0)),
                  any_spec],
        out_specs=any_spec,
        out_shape=jax.ShapeDtypeStruct(dpb.shape, dpb.dtype),
        scratch_shapes=[pltpu.VMEM((2, tb, LANE), dpb.dtype), pltpu.VMEM((2, tb, LANE), dpb.dtype),
                        pltpu.VMEM((2, 2, tb, LANE), dpb.dtype), pltpu.VMEM((2, LANE, LANE), F32),
                        pltpu.SemaphoreType.DMA((2, 3))],
        input_output_aliases={7: 0},
        compiler_params=_params(("arbitrary", "arbitrary")),
    )(pb, pb, pb, cos_t, sin_t, do, rsave, dpb)


def _row_tile(t_len, want):
    return min(want, t_len)


PAIR_CB = 2 * CB_PER_SHARD


def proj_forward_gather(h, wt_sh, wout_sh, order, t_len):
    tm = _row_tile(t_len, 1024)
    n_i = t_len // tm
    last = N_DEV // 2 - 1

    def body(order_ref, h_ref, wsh_ref, wosh_ref, o_ref, wg_ref, wog_ref, wbuf, send, recv, local, wsem):
        j, i = pl.program_id(0), pl.program_id(1)
        pos = _mesh_pos()
        x, y, c = pos
        sibling = (x, y, 1 - c)

        def route(core):
            return [(x + (1 - core) * (1 - 2 * x), y + core * (1 - 2 * y)),
                    (x + core * (1 - 2 * x), y + (1 - core) * (1 - 2 * y)),
                    (1 - x, 1 - y)]

        chips, sib_chips = route(c), route(1 - c)
        arrays = ((wsh_ref, wg_ref), (wosh_ref, wog_ref))

        def copy(a, k, block, to, src_ref=None):
            dst = arrays[a][1].at[_lin(block)]
            return _remote(dst if src_ref is None else src_ref, dst, send.at[7 * a + k], recv.at[7 * a + k], to)

        def arrived(a, k, block):
            dst = arrays[a][1].at[_lin(block)]
            _remote(dst, dst, send.at[7 * a + k], recv.at[7 * a + k], pos).wait_recv()

        mine = [pltpu.make_async_copy(arrays[a][0], arrays[a][1].at[_lin(pos)], local.at[a]) for a in range(2)]
        first = [copy(a, k, pos, to, arrays[a][0]) for a in range(2)
                 for k, to in ((0, sibling), (1, (*chips[0], c)), (2, (*chips[1], c)))]
        onward = [[copy(a, 3, (*chips[0], c), (*chips[1], c)), copy(a, 4, (*chips[0], c), sibling)] for a in range(2)]
        pass_other = [copy(a, 5, (*chips[1], c), sibling) for a in range(2)]
        pass_diag = [copy(a, 6, (*chips[2], c), sibling) for a in range(2)]

        @pl.when((j == 0) & (i == 0))
        def _():
            for cp in mine + first:
                cp.start()
            for cp in mine:
                cp.wait()
            for a in range(2):
                arrived(a, 0, sibling)

        @pl.when((j == 1) & (i == 0))
        def _():
            for a in range(2):
                arrived(a, 1, (*chips[0], c))
                for cp in onward[a]:
                    cp.start()
            for a in range(2):
                arrived(a, 2, (*chips[1], c))
                pass_other[a].start()
            for a in range(2):
                arrived(a, 4, (*sib_chips[0], 1 - c))
                arrived(a, 5, (*sib_chips[1], 1 - c))

        @pl.when((j == last) & (i == 0))
        def _():
            for a in range(2):
                arrived(a, 3, (*chips[2], c))
                pass_diag[a].start()
            for a in range(2):
                arrived(a, 6, (*sib_chips[2], 1 - c))

        @pl.when(i == 0)
        def _():
            cp = pltpu.make_async_copy(wg_ref.at[pl.ds(2 * order_ref[j], 2)], wbuf, wsem)
            cp.start()
            cp.wait()

        acc = _dot(h_ref[...], wbuf[...].reshape(PAIR_CB * LANE, D_MODEL), _NT)
        for jj in range(PAIR_CB):
            o_ref[jj] = acc[:, jj * LANE:(jj + 1) * LANE]

        @pl.when((j == last) & (i == n_i - 1))
        def _():
            for cp in first + onward[0] + onward[1] + pass_other + pass_diag:
                cp.wait_send()

    any_spec = pl.BlockSpec(memory_space=pl.ANY)
    return pl.pallas_call(
        body, name="proj_fwd_gather",
        grid_spec=pltpu.PrefetchScalarGridSpec(
            num_scalar_prefetch=1, grid=(N_DEV // 2, n_i),
            in_specs=[pl.BlockSpec((tm, D_MODEL), lambda j, i, order: (i, 0)), any_spec, any_spec],
            out_specs=[pl.BlockSpec((PAIR_CB, tm, LANE), lambda j, i, order: (order[j], i, 0)), any_spec, any_spec],
            scratch_shapes=[pltpu.VMEM((2,) + wt_sh.shape, wt_sh.dtype), pltpu.SemaphoreType.DMA((14,)),
                            pltpu.SemaphoreType.DMA((14,)), pltpu.SemaphoreType.DMA((2,)), pltpu.SemaphoreType.DMA]),
        out_shape=[jax.ShapeDtypeStruct((N_CB, t_len, LANE), F32),
                   jax.ShapeDtypeStruct((N_DEV,) + wt_sh.shape, wt_sh.dtype),
                   jax.ShapeDtypeStruct((N_DEV,) + wout_sh.shape, wout_sh.dtype)],
        compiler_params=_params(("arbitrary", "arbitrary")),
    )(order, h, wt_sh, wout_sh)


def proj_backward_input(dpb, wt, token, x, dy, norm_g, scale1p, t_len):
    tm = _row_tile(t_len, 256)

    def body(a_ref, wt_hbm, token_ref, x_ref, dy_ref, g_ref, sc_ref, gx_ref, vec_ref, w_ref, sem):
        del token_ref
        i = pl.program_id(0)

        @pl.when(i == 0)
        def _():
            cp = pltpu.make_async_copy(wt_hbm, w_ref, sem)
            cp.start()
            cp.wait()

        a = jnp.concatenate([a_ref[jj].astype(_BF) for jj in range(N_CB)], axis=1)
        dhv = _dot(a, w_ref[...])
        xv, g, sc = x_ref[...], g_ref[...], sc_ref[...]
        r = lax.rsqrt(jnp.mean(xv * xv, axis=-1, keepdims=True) + EPS)
        xn = xv * r
        dxn = dhv * (g * sc)
        gx_ref[...] = dy_ref[...] + r * dxn - xn * (r * r) * jnp.mean(xv * dxn, axis=-1, keepdims=True)
        t = dhv * xn
        _acc_rows(vec_ref, i, [jnp.sum(t * sc, axis=0, keepdims=True),
                               jnp.sum(t * g, axis=0, keepdims=True),
                               jnp.sum(dhv, axis=0, keepdims=True)])

    row = pl.BlockSpec((tm, D_MODEL), lambda i: (i, 0))
    return pl.pallas_call(
        body, name="proj_bwd_input", grid=(t_len // tm,),
        in_specs=[pl.BlockSpec((N_CB, tm, LANE), lambda i: (0, i, 0)),
                  pl.BlockSpec(memory_space=pl.ANY),
                  pl.BlockSpec(token.shape, lambda i: (0, 0)),
                  row, row, _vec_spec(), _vec_spec()],
        out_specs=[row, pl.BlockSpec((8, D_MODEL), lambda i: (0, 0))],
        out_shape=[jax.ShapeDtypeStruct((t_len, D_MODEL), F32), jax.ShapeDtypeStruct((8, D_MODEL), F32)],
        scratch_shapes=[pltpu.VMEM(wt.shape, wt.dtype), pltpu.SemaphoreType.DMA],
        compiler_params=_params(("arbitrary",)),
    )(dpb, wt, token, x, dy, norm_g, scale1p)


def proj_backward_weight(h_t, dpb, t_len):
    tk = _row_tile(t_len, 2048)

    def body(h_ref, b_ref, o_ref):
        k = pl.program_id(1)
        b = jnp.concatenate([b_ref[jj].astype(_BF) for jj in range(PAIR_CB)], axis=1)
        part = _dot(h_ref[...], b)

        @pl.when(k == 0)
        def _():
            for s in range(2):
                o_ref[s] = part[:, s * SHARD_IN:(s + 1) * SHARD_IN]

        @pl.when(k > 0)
        def _():
            for s in range(2):
                o_ref[s] = o_ref[s] + part[:, s * SHARD_IN:(s + 1) * SHARD_IN]

    return pl.pallas_call(
        body, name="proj_bwd_weight", grid=(N_DEV // 2, t_len // tk),
        in_specs=[pl.BlockSpec((D_MODEL, tk), lambda j, k: (0, k)),
                  pl.BlockSpec((PAIR_CB, tk, LANE), lambda j, k: (j, k, 0))],
        out_specs=pl.BlockSpec((2, D_MODEL, SHARD_IN), lambda j, k: (j, 0, 0)),
        out_shape=jax.ShapeDtypeStruct((N_DEV, D_MODEL, SHARD_IN), F32),
        compiler_params=_params(("arbitrary", "arbitrary")),
    )(h_t, dpb)


def sibling_blocks(g_in, c_idx):
    tr = D_MODEL

    def body(c_ref, g_ref, o_ref):
        del c_ref
        o_ref[...] = g_ref[...].astype(o_ref.dtype)

    return pl.pallas_call(
        body, name="sibling_blocks",
        grid_spec=pltpu.PrefetchScalarGridSpec(
            num_scalar_prefetch=1, grid=(N_DEV // 2, D_MODEL // tr),
            in_specs=[pl.BlockSpec((None, tr, SHARD_IN), lambda q, i, c: (2 * q + 1 - c[0], i, 0))],
            out_specs=pl.BlockSpec((None, tr, SHARD_IN), lambda q, i, c: (q, i, 0))),
        out_shape=jax.ShapeDtypeStruct((N_DEV // 2, D_MODEL, SHARD_IN), _BF),
        compiler_params=_params(("arbitrary", "arbitrary")),
    )(c_idx, g_in)


def _vec_spec():
    return pl.BlockSpec((1, D_MODEL), lambda i: (0, 0))


def _acc_rows(ref, i, rows):
    @pl.when(i == 0)
    def _():
        ref[...] = jnp.zeros_like(ref)

    for n, row in enumerate(rows):
        ref[n:n + 1, :] = ref[n:n + 1, :] + row


def adaln_forward(x, norm_g, scale1p, shift, t_len):
    tm = _row_tile(t_len, 512)

    def body(x_ref, g_ref, sc_ref, sh_ref, h_ref, ht_ref):
        xv = x_ref[...]
        r = lax.rsqrt(jnp.mean(xv * xv, axis=-1, keepdims=True) + EPS)
        h = xv * r * g_ref[...] * sc_ref[...] + sh_ref[...]
        h_ref[...] = h.astype(h_ref.dtype)
        ht_ref[...] = h.T.astype(ht_ref.dtype)

    return pl.pallas_call(
        body, name="adaln_fwd", grid=(t_len // tm,),
        in_specs=[pl.BlockSpec((tm, D_MODEL), lambda i: (i, 0)), _vec_spec(), _vec_spec(), _vec_spec()],
        out_specs=[pl.BlockSpec((tm, D_MODEL), lambda i: (i, 0)), pl.BlockSpec((D_MODEL, tm), lambda i: (0, i))],
        out_shape=[jax.ShapeDtypeStruct((t_len, D_MODEL), _BF), jax.ShapeDtypeStruct((D_MODEL, t_len), _BF)],
        compiler_params=_params(("arbitrary",)),
    )(x, norm_g, scale1p, shift)


def _head_norm(o, g):
    r = lax.rsqrt(jnp.mean(o * o, axis=-1, keepdims=True) + EPS)
    return r, o * r * g


def _group_spec(tm, cb):
    return pl.BlockSpec((N_HEADS, tm, LANE), lambda i, cb=cb: (cb // N_HEADS, i, 0))


MID_FINAL_G, MID_GATE, MID_LOSS, MID_HG_G, MID_RET_G = range(5)


def middle(x, target, oa, ob, pb, wout, gate, final_g, hg_g, ret_g, t_len):
    tm = _row_tile(t_len, 256)
    n_steps = t_len // tm

    def body(x_ref, t_ref, oa_ref, ob_ref, hz_ref, rz_ref, ga_ref, gb_ref, w_ref, gate_ref, fg_ref, hg_ref, rg_ref,
             dy_ref, doa_ref, dob_ref, dw_ref, vec_ref, dpb_ref, m_scr, dm_scr, keep, bufs, sems):
        i = pl.program_id(0)
        rows = pl.ds(pl.multiple_of(i * tm, tm), tm)

        def group_copies(sl):
            return [pltpu.make_async_copy(bufs.at[sl, n], dpb_ref.at[pl.ds(cb, N_HEADS), rows], sems.at[sl, n])
                    for n, cb in enumerate((CB_HZ, CB_RZ, CB_GA, CB_GB))]
        sides = ((oa_ref, hz_ref, ga_ref, hg_ref, doa_ref), (ob_ref, rz_ref, gb_ref, rg_ref, dob_ref))
        for hh in range(N_HEADS):
            ls = slice(hh * LANE, (hh + 1) * LANE)
            acc = None
            for side, (o_ref, z_ref, gt_ref, g_ref, _) in enumerate(sides):
                o = o_ref[hh]
                rr = lax.rsqrt(jnp.mean(o * o, axis=-1, keepdims=True) + EPS)
                orr = o * rr
                zz = z_ref[hh]
                sz = _sigmoid(zz)
                sgt = _sigmoid(gt_ref[hh])
                keep[side, hh, 0] = orr
                keep[side, hh, 1] = sz
                keep[side, hh, 2] = sgt
                keep[side, hh, 3] = jnp.broadcast_to(rr, orr.shape)
                u = sgt * ((orr * g_ref[:, ls]) * (zz * sz))
                acc = u if acc is None else acc + u
            m_scr[:, ls] = acc.astype(m_scr.dtype)
        zv = _dot(m_scr[...], w_ref[...])
        gt, fg = gate_ref[...], fg_ref[...]
        y = x_ref[...] + gt * zv
        r = lax.rsqrt(jnp.mean(y * y, axis=-1, keepdims=True) + EPS)
        yn = y * r
        err = yn * fg - t_ref[...]
        loss = 0.5 * jnp.sum(jnp.mean(err * err, axis=-1, keepdims=True), axis=0, keepdims=True)
        dout = err * (1.0 / D_MODEL)
        gd = dout * fg
        dy = r * gd - yn * (r * r) * jnp.mean(y * gd, axis=-1, keepdims=True)
        dy_ref[...] = dy
        dz = (dy * gt).astype(_BF)
        dm_scr[...] = _dot(dz, w_ref[...], _NT)
        part = _dot(m_scr[...], dz, _TN)

        @pl.when(i == 0)
        def _():
            dw_ref[...] = part

        @pl.when(i > 0)
        def _():
            dw_ref[...] = dw_ref[...] + part

        slot = _writeback_reserve(i, group_copies)
        dg = [[], []]
        for hh in range(N_HEADS):
            ls = slice(hh * LANE, (hh + 1) * LANE)
            dmh = dm_scr[:, ls]
            for side, (o_ref, z_ref, gt_ref, g_ref, do_ref) in enumerate(sides):
                zz, g = z_ref[hh], g_ref[:, ls]
                orr, sz, sgt, rr = keep[side, hh, 0], keep[side, hh, 1], keep[side, hh, 2], keep[side, hh, 3]
                n = orr * g
                silu = zz * sz
                du = dmh * sgt
                bufs[slot, 2 + side, hh] = (dmh * (n * silu) * (sgt * (1.0 - sgt))).astype(bufs.dtype)
                bufs[slot, side, hh] = (du * n * (sz * (1.0 + zz * (1.0 - sz)))).astype(bufs.dtype)
                dn = du * silu
                dg[side].append(jnp.sum(dn * orr, axis=0, keepdims=True))
                gdn = dn * g
                do_ref[hh] = (rr * (gdn - orr * jnp.mean(orr * gdn, axis=-1, keepdims=True))).astype(do_ref.dtype)
        _acc_rows(vec_ref, i, [jnp.sum(dout * yn, axis=0, keepdims=True),
                               jnp.sum(dy * zv, axis=0, keepdims=True),
                               jnp.broadcast_to(loss, (1, D_MODEL)),
                               jnp.concatenate(dg[0], axis=1), jnp.concatenate(dg[1], axis=1)])
        _writeback_commit(i, n_steps, slot, group_copies)

    row = pl.BlockSpec((tm, D_MODEL), lambda i: (i, 0))
    head = pl.BlockSpec((N_HEADS, tm, LANE), lambda i: (0, i, 0))
    full = pl.BlockSpec((D_MODEL, D_MODEL), lambda i: (0, 0))
    return pl.pallas_call(
        body, name="middle", grid=(n_steps,),
        in_specs=[row, row, head, head, _group_spec(tm, CB_HZ), _group_spec(tm, CB_RZ), _group_spec(tm, CB_GA),
                  _group_spec(tm, CB_GB), full, _vec_spec(), _vec_spec(), _vec_spec(), _vec_spec()],
        out_specs=[row, head, head, full, pl.BlockSpec((8, D_MODEL), lambda i: (0, 0)),
                   pl.BlockSpec(memory_space=pl.ANY)],
        out_shape=[jax.ShapeDtypeStruct((t_len, D_MODEL), F32),
                   jax.ShapeDtypeStruct((N_HEADS, t_len, LANE), _BF),
                   jax.ShapeDtypeStruct((N_HEADS, t_len, LANE), _BF),
                   jax.ShapeDtypeStruct((D_MODEL, D_MODEL), F32),
                   jax.ShapeDtypeStruct((8, D_MODEL), F32),
                   jax.ShapeDtypeStruct((N_CB, t_len, LANE), _BF)],
        scratch_shapes=[pltpu.VMEM((tm, D_MODEL), _BF), pltpu.VMEM((tm, D_MODEL), F32),
                        pltpu.VMEM((2, N_HEADS, 4, tm, LANE), F32),
                        pltpu.VMEM((2, 4, N_HEADS, tm, LANE), _BF), pltpu.SemaphoreType.DMA((2, 4))],
        compiler_params=_params(("arbitrary",)),
    )(x, target, oa, ob, pb, pb, pb, pb, wout, gate, final_g, hg_g, ret_g)


def device_step(x, target, mod, lb, project, norm_g, hg_g, ret_g, final_g, c_idx=None, start_exchange=None):
    t_len = x.shape[0]
    shift, scale, gate = mod[:, :D_MODEL], mod[:, D_MODEL:2 * D_MODEL], mod[:, 2 * D_MODEL:]
    scale1p = 1.0 + scale
    cos_t, sin_t = _rope_tables(t_len)
    h, h_t = adaln_forward(x, norm_g, scale1p, shift, t_len)
    pb, wt, wout = project(h)
    oa, ssave, asave = hgrn_forward(pb, lb, t_len)
    ob, rsave = retention_forward(pb, cos_t, sin_t, t_len)
    dy, doa, dob, dwout, vec_mid, dpb = middle(x, target, oa, ob, pb, wout, gate, final_g, hg_g, ret_g, t_len)
    dpb, dlb = hgrn_backward(pb, lb, doa, ssave, asave, dpb, t_len)
    dpb = retention_backward(pb, cos_t, sin_t, dob, rsave, dpb, t_len)
    c_idx = jnp.zeros((1,), jnp.int32) if c_idx is None else c_idx
    dwin = proj_backward_weight(h_t, dpb, t_len)
    dwin_sib = sibling_blocks(dwin, c_idx)
    token, pending = (start_exchange(dwin, dwin_sib, dwout) if start_exchange
                      else (jnp.zeros((8, LANE), F32), None))
    grad_x, vec_ada = proj_backward_input(dpb, wt, token, x, dy, norm_g, scale1p, t_len)
    return grad_x, dwin, dwout, vec_mid, vec_ada, dlb, pending


PACK_ROWS = 16
ROW_NORM_G, ROW_LB, ROW_HG_G, ROW_RET_G, ROW_FINAL_G, ROW_SHIFT, ROW_SCALE, ROW_GATE, ROW_LOSS = range(9)


def _mesh_pos():
    return lax.axis_index("x"), lax.axis_index("y"), lax.axis_index("c")


def _lin(pos):
    return 4 * pos[0] + 2 * pos[1] + pos[2]


def _xor_peer(pos, k):
    return tuple(1 - p if (k >> s) & 1 else p for p, s in zip(pos, (2, 1, 0)))


def _other_chips(pos):
    x, y, _ = pos
    return [(1 - x, y), (x, 1 - y), (1 - x, 1 - y)]


def _remote(src, dst, send_sem, recv_sem, to):
    return pltpu.make_async_remote_copy(src_ref=src, dst_ref=dst, send_sem=send_sem, recv_sem=recv_sem,
                                        device_id=to, device_id_type=MESH)


def pre_exchange(c, w_ada, b_ada, logits, w_in, w_out):
    def body(c_ref, wada_ref, bada_ref, logit_ref, win_ref, wout_ref, mod_ref, scall_ref, lb_ref, wt_ref, wo_ref,
             cg_ref, modall_ref, parts_ref, send1, recv1, send2, recv2):
        pos = _mesh_pos()
        cv = c_ref[...]
        slot = lambda p: pl.ds(pl.multiple_of(8 * _lin(p), 8), 8)
        cg_ref[slot(pos), :] = jnp.broadcast_to(cv * _sigmoid(cv), (8, D_MODEL))
        lb_ref[...] = _sigmoid(logit_ref[0:1, :] - logit_ref[1:2, :])
        peers = [_xor_peer(pos, k) for k in range(1, N_DEV)]
        gather = [_remote(cg_ref.at[slot(pos)], cg_ref.at[slot(pos)], send1.at[n], recv1.at[n], p)
                  for n, p in enumerate(peers)]
        for cp in gather:
            cp.start()
        wt_ref[...] = win_ref[...].T.astype(wt_ref.dtype)
        wo_ref[...] = wout_ref[...].astype(wo_ref.dtype)
        for n, p in enumerate(peers):
            _remote(cg_ref.at[slot(p)], cg_ref.at[slot(p)], send1.at[n], recv1.at[n], p).wait_recv()
        modall_ref[...] = _dot(cg_ref[...], wada_ref[...])
        scatter = [_remote(modall_ref.at[slot(p)], parts_ref.at[slot(pos)], send2.at[n], recv2.at[n], p)
                   for n, p in enumerate(peers)]
        for cp in scatter:
            cp.start()
        parts_ref[slot(pos), :] = modall_ref[slot(pos), :]
        for n, p in enumerate(peers):
            _remote(modall_ref.at[slot(p)], parts_ref.at[slot(p)], send2.at[n], recv2.at[n], p).wait_recv()
        for cp in gather + scatter:
            cp.wait_send()
        for j in range(N_DEV):
            cols = slice(j * SHARD_ADA, (j + 1) * SHARD_ADA)
            mod_ref[:, cols] = parts_ref[8 * j:8 * j + 1, :] + bada_ref[:, cols]
            scall_ref[j:j + 1, :] = cg_ref[8 * j:8 * j + 1, :]

    vmem = pl.BlockSpec(memory_space=pltpu.VMEM)
    return pl.pallas_call(
        body, name="pre_exchange",
        in_specs=[vmem] * 6, out_specs=[vmem] * 5,
        out_shape=[jax.ShapeDtypeStruct((1, 3 * D_MODEL), F32), jax.ShapeDtypeStruct((N_DEV, D_MODEL), F32),
                   jax.ShapeDtypeStruct((1, D_MODEL), F32),
                   jax.ShapeDtypeStruct(w_in.shape[::-1], _BF), jax.ShapeDtypeStruct(w_out.shape, _BF)],
        scratch_shapes=[pltpu.VMEM((N_DEV * 8, D_MODEL), F32), pltpu.VMEM((N_DEV * 8, SHARD_ADA), F32),
                        pltpu.VMEM((N_DEV * 8, SHARD_ADA), F32)] + [pltpu.SemaphoreType.DMA((N_DEV - 1,))] * 4,
        compiler_params=pltpu.CompilerParams(vmem_limit_bytes=VMEM_LIMIT),
    )(c, w_ada, b_ada, logits, w_in, w_out)


def grad_pair_exchange(g_sib, g_out):
    def body(gsib_ref, gout_ref, ra_ref, rb_ref, send, recv):
        pos = _mesh_pos()
        x, y, c = pos
        sibling = (x, y, 1 - c)
        copies = []
        for q in range(4):
            copies.append(_remote(gsib_ref.at[q], ra_ref.at[q], send.at[q], recv.at[q], sibling))
            copies.append(_remote(gout_ref.at[2 * q + (1 - c)], rb_ref.at[q], send.at[4 + q], recv.at[4 + q], sibling))
        for cp in copies:
            cp.start()
        for cp in copies:
            cp.wait_recv()
        for cp in copies:
            cp.wait_send()

    any_spec = pl.BlockSpec(memory_space=pl.ANY)
    return pl.pallas_call(
        body, name="grad_pair_exchange",
        in_specs=[any_spec, any_spec], out_specs=[any_spec, any_spec],
        out_shape=[jax.ShapeDtypeStruct(g_sib.shape, g_sib.dtype), jax.ShapeDtypeStruct((4,) + g_out.shape[1:], F32)],
        scratch_shapes=[pltpu.SemaphoreType.DMA((8,)), pltpu.SemaphoreType.DMA((8,))],
    )(g_sib, g_out)


def pair_sum(g_in, ra, g_out, rb, c_idx):
    tr = 256

    def body(c_ref, gin_ref, ra_ref, gout_ref, rb_ref, sb_ref, sbo_ref):
        del c_ref
        sb_ref[...] = (gin_ref[...] + ra_ref[...].astype(F32)).astype(sb_ref.dtype)
        sbo_ref[...] = gout_ref[...] + rb_ref[...]

    n_i = D_MODEL // tr
    return pl.pallas_call(
        body, name="pair_sum",
        grid_spec=pltpu.PrefetchScalarGridSpec(
            num_scalar_prefetch=1, grid=(4, n_i),
            in_specs=[pl.BlockSpec((None, tr, SHARD_IN), lambda q, i, c: (2 * q + c[0], i, 0)),
                      pl.BlockSpec((None, tr, SHARD_IN), lambda q, i, c: (q, i, 0)),
                      pl.BlockSpec((None, SHARD_OUT // n_i, D_MODEL), lambda q, i, c: (2 * q + c[0], i, 0)),
                      pl.BlockSpec((None, SHARD_OUT // n_i, D_MODEL), lambda q, i, c: (q, i, 0))],
            out_specs=[pl.BlockSpec((None, tr, SHARD_IN), lambda q, i, c: (q, i, 0)),
                       pl.BlockSpec((None, SHARD_OUT // n_i, D_MODEL), lambda q, i, c: (q, i, 0))]),
        out_shape=[jax.ShapeDtypeStruct(ra.shape, _BF), jax.ShapeDtypeStruct(rb.shape, F32)],
        compiler_params=_params(("arbitrary", "arbitrary")),
    )(c_idx, g_in, ra, g_out, rb)


_HBM = pl.BlockSpec(memory_space=pltpu.HBM)
_SEM = pl.BlockSpec(memory_space=pltpu.SEMAPHORE)
_N_CHIP_COPIES = 6


def _chip_copies(sb_ref, sbo_ref, rc_ref, rco_ref, send, recv):
    pos = _mesh_pos()
    copies = []
    for a, (src, dst) in enumerate(((sb_ref, rc_ref), (sbo_ref, rco_ref))):
        for j, chip in enumerate(_other_chips(pos)):
            copies.append(_remote(src.at[2 * chip[0] + chip[1]], dst.at[j], send.at[3 * a + j], recv.at[3 * a + j],
                                  (*chip, pos[2])))
    return copies


def grad_chip_start(sb, sbo):
    def body(sb_ref, sbo_ref, rc_ref, rco_ref, send, recv, sb_thru, sbo_thru, rc_thru, rco_thru, token):
        del sb_thru, sbo_thru, rc_thru, rco_thru
        for cp in _chip_copies(sb_ref, sbo_ref, rc_ref, rco_ref, send, recv):
            cp.start()
        token[...] = jnp.zeros_like(token)

    hbm = lambda a: pltpu.with_memory_space_constraint(a, pltpu.HBM)
    rc = lax.empty((3,) + sb.shape[1:], sb.dtype)
    rco = lax.empty((3,) + sbo.shape[1:], sbo.dtype)
    return pl.pallas_call(
        body, name="grad_chip_start",
        in_specs=[_HBM] * 4,
        out_specs=[_SEM, _SEM, _HBM, _HBM, _HBM, _HBM, pl.BlockSpec(memory_space=pltpu.VMEM)],
        out_shape=[pltpu.SemaphoreType.DMA((_N_CHIP_COPIES,)), pltpu.SemaphoreType.DMA((_N_CHIP_COPIES,)),
                   pltpu.HBM(sb.shape, sb.dtype), pltpu.HBM(sbo.shape, sbo.dtype),
                   pltpu.HBM(rc.shape, rc.dtype), pltpu.HBM(rco.shape, rco.dtype),
                   jax.ShapeDtypeStruct((8, LANE), F32)],
        input_output_aliases={0: 2, 1: 3, 2: 4, 3: 5},
        compiler_params=pltpu.CompilerParams(has_side_effects=pltpu.SideEffectType.DATAFLOW_SIDE_EFFECTING),
    )(hbm(sb), hbm(sbo), hbm(rc), hbm(rco))


def grad_chip_wait(send, recv, sb, sbo, rc, rco, after):
    def body(sb_ref, sbo_ref, rc_ref, rco_ref, send, recv, after_ref, sb_o, sbo_o, rc_o, rco_o):
        del after_ref, sb_o, sbo_o, rc_o, rco_o
        for cp in _chip_copies(sb_ref, sbo_ref, rc_ref, rco_ref, send, recv):
            cp.wait_send()
            cp.wait_recv()

    return pl.pallas_call(
        body, name="grad_chip_wait",
        in_specs=[_HBM] * 4 + [_SEM, _SEM, pl.BlockSpec(memory_space=pl.ANY)],
        out_specs=[_HBM] * 4,
        out_shape=[pltpu.HBM(sb.shape, sb.dtype), pltpu.HBM(sbo.shape, sbo.dtype),
                   pltpu.HBM(rc.shape, rc.dtype), pltpu.HBM(rco.shape, rco.dtype)],
        input_output_aliases={0: 0, 1: 1, 2: 2, 3: 3},
        compiler_params=pltpu.CompilerParams(has_side_effects=pltpu.SideEffectType.DATAFLOW_SIDE_EFFECTING),
    )(sb, sbo, rc, rco, send, recv, after)


def pack_gather(pack):
    def body(pack_ref, packs_ref, psend, precv):
        pos = _mesh_pos()
        me = _lin(pos)
        packs_ref[me] = pack_ref[...]
        peers = [_xor_peer(pos, k) for k in range(1, N_DEV)]
        gather = [_remote(packs_ref.at[me], packs_ref.at[me], psend.at[n], precv.at[n], p) for n, p in enumerate(peers)]
        for cp in gather:
            cp.start()
        for n, p in enumerate(peers):
            _remote(packs_ref.at[_lin(p)], packs_ref.at[_lin(p)], psend.at[n], precv.at[n], p).wait_recv()
        for cp in gather:
            cp.wait_send()

    vmem = pl.BlockSpec(memory_space=pltpu.VMEM)
    return pl.pallas_call(
        body, name="pack_gather", in_specs=[vmem], out_specs=vmem,
        out_shape=jax.ShapeDtypeStruct((N_DEV,) + pack.shape, F32),
        scratch_shapes=[pltpu.SemaphoreType.DMA((N_DEV - 1,)), pltpu.SemaphoreType.DMA((N_DEV - 1,))],
    )(pack)


def pack_rows(vec_mid, vec_ada, dlb):
    def body(mid_ref, ada_ref, dlb_ref, o_ref):
        mid = lambda r: mid_ref[r:r + 1, :]
        rows = [ada_ref[0:1, :], dlb_ref[...], mid(MID_HG_G), mid(MID_RET_G), mid(MID_FINAL_G),
                ada_ref[2:3, :], ada_ref[1:2, :], mid(MID_GATE), mid(MID_LOSS)]
        o_ref[...] = jnp.zeros_like(o_ref)
        for n, row in enumerate(rows):
            o_ref[n:n + 1, :] = row

    vmem = pl.BlockSpec(memory_space=pltpu.VMEM)
    return pl.pallas_call(body, name="pack_rows", in_specs=[vmem] * 3, out_specs=vmem,
                          out_shape=jax.ShapeDtypeStruct((PACK_ROWS, D_MODEL), F32))(vec_mid, vec_ada, dlb)


def _adamw(w, g, m, v):
    m = ADAM_B1 * m + (1.0 - ADAM_B1) * g
    v = ADAM_B2 * v + (1.0 - ADAM_B2) * (g * g)
    m_hat = m / (1.0 - ADAM_B1 ** ADAM_STEP)
    v_hat = v / (1.0 - ADAM_B2 ** ADAM_STEP)
    delta = -ADAM_LR * (m_hat / (jnp.sqrt(v_hat) + ADAM_EPS) + ADAM_WD * w)
    return delta, m, v


def adam_shard(chip_idx, own, parts, w, m, v, name):
    rows, cols = w.shape
    tr = min(rows, 128)

    def body(chip_ref, p0, p1, p2, p3, w_ref, m_ref, v_ref, g_ref, d_ref, nm_ref, nv_ref):
        del chip_ref
        g = ((p0[...].astype(F32) + p1[...].astype(F32)) + p2[...].astype(F32)) + p3[...].astype(F32)
        g_ref[...] = g
        d_ref[...], nm_ref[...], nv_ref[...] = _adamw(w_ref[...], g, m_ref[...], v_ref[...])

    part = lambda q: pl.BlockSpec((None, tr, cols), lambda i, chip, q=q: (q, i, 0))
    tile = pl.BlockSpec((tr, cols), lambda i, chip: (i, 0))
    return pl.pallas_call(
        body, name=name,
        grid_spec=pltpu.PrefetchScalarGridSpec(
            num_scalar_prefetch=1, grid=(rows // tr,),
            in_specs=[pl.BlockSpec((None, tr, cols), lambda i, chip: (chip[0], i, 0)), part(0), part(1), part(2),
                      tile, tile, tile],
            out_specs=[tile] * 4),
        out_shape=[jax.ShapeDtypeStruct(w.shape, F32)] * 4,
        compiler_params=_params(("arbitrary",)),
    )(chip_idx, own, parts, parts, parts, w, m, v)


def adam_ada(sc_t, dmod_all, me_idx, w, m, v):
    def body(me_ref, sc_ref, dm_ref, w_ref, m_ref, v_ref, g_ref, d_ref, nm_ref, nv_ref):
        del me_ref
        g = _dot_f32(sc_ref[...], dm_ref[...])
        g_ref[...] = g
        d_ref[...], nm_ref[...], nv_ref[...] = _adamw(w_ref[...], g, m_ref[...], v_ref[...])

    full = pl.BlockSpec(w.shape, lambda i, me: (0, 0))
    return pl.pallas_call(
        body, name="adam_ada",
        grid_spec=pltpu.PrefetchScalarGridSpec(
            num_scalar_prefetch=1, grid=(1,),
            in_specs=[pl.BlockSpec(sc_t.shape, lambda i, me: (0, 0)),
                      pl.BlockSpec((LANE, SHARD_ADA), lambda i, me: (0, me[0])), full, full, full],
            out_specs=[full] * 4),
        out_shape=[jax.ShapeDtypeStruct(w.shape, F32)] * 4,
        compiler_params=_params(("arbitrary",)),
    )(me_idx, sc_t, dmod_all, w, m, v)


def adam_vectors(packs, lb, params, ms, vs):
    n = len(params)

    def body(*refs):
        packs_ref, lb_ref = refs[0], refs[1]
        w_refs, m_refs, v_refs = refs[2:2 + n], refs[2 + n:2 + 2 * n], refs[2 + 2 * n:2 + 3 * n]
        loss_ref = refs[2 + 3 * n]
        outs = refs[3 + 3 * n:3 + 7 * n]
        tot_ref = refs[3 + 7 * n]
        tot = packs_ref[0]
        for d in range(1, N_DEV):
            tot = tot + packs_ref[d]
        tot_ref[...] = tot
        row = lambda r: tot_ref[r:r + 1, :]
        lbv = lb_ref[...]
        dl0 = row(ROW_LB) * lbv * (1.0 - lbv)
        grads = [[row(ROW_NORM_G)],
                 [jnp.concatenate([row(ROW_SHIFT), row(ROW_SCALE), row(ROW_GATE)], axis=1)],
                 [dl0, -dl0],
                 [row(ROW_HG_G)], [row(ROW_RET_G)], [row(ROW_FINAL_G)]]
        loss_ref[...] = tot_ref[ROW_LOSS:ROW_LOSS + 1, 0:LANE]
        for j, g_rows in enumerate(grads):
            for r, g in enumerate(g_rows):
                rs = slice(r, r + 1)
                d, nm, nv = _adamw(w_refs[j][rs, :], g, m_refs[j][rs, :], v_refs[j][rs, :])
                outs[4 * j][rs, :] = g
                outs[4 * j + 1][rs, :] = d
                outs[4 * j + 2][rs, :] = nm
                outs[4 * j + 3][rs, :] = nv

    vmem = pl.BlockSpec(memory_space=pltpu.VMEM)
    out_shape = [jax.ShapeDtypeStruct((1, LANE), F32)]
    for w in params:
        out_shape += [jax.ShapeDtypeStruct(w.shape, F32)] * 4
    return pl.pallas_call(
        body, name="adam_vectors", in_specs=[vmem] * (2 + 3 * n), out_specs=[vmem] * len(out_shape),
        out_shape=out_shape, scratch_shapes=[pltpu.VMEM((PACK_ROWS, D_MODEL), F32)],
    )(packs, lb, *params, *ms, *vs)


def kernel(x, c, norm_g, w_ada, b_ada, w_in, hg_lb_logits, hg_norm_g, ret_norm_g, w_out, final_g, loss_target, m_norm_g, m_w_ada, m_b_ada, m_w_in, m_hg_lb_logits, m_hg_norm_g, m_ret_norm_g, m_w_out, m_final_g, v_norm_g, v_w_ada, v_b_ada, v_w_in, v_hg_lb_logits, v_hg_norm_g, v_ret_norm_g, v_w_out, v_final_g):
    pos = _mesh_pos()
    me_idx = jnp.reshape(_lin(pos), (1,)).astype(jnp.int32)
    c_idx = jnp.reshape(pos[2], (1,)).astype(jnp.int32)
    vec = lambda a: a.reshape(1, D_MODEL)

    mod, scall, lb, wt_sh, wout_sh = pre_exchange(c, w_ada[0], b_ada, hg_lb_logits, w_in[0], w_out[0])
    chip_idx = jnp.reshape(2 * pos[0] + pos[1], (1,)).astype(jnp.int32)
    x_, y_, c_ = pos
    order = jnp.stack([2 * x_ + y_,
                       2 * (x_ + (1 - c_) * (1 - 2 * x_)) + (y_ + c_ * (1 - 2 * y_)),
                       2 * (x_ + c_ * (1 - 2 * x_)) + (y_ + (1 - c_) * (1 - 2 * y_)),
                       2 * (1 - x_) + (1 - y_)]).astype(jnp.int32)

    def project(h):
        pb, wtg, woutg = proj_forward_gather(h, wt_sh, wout_sh, order, h.shape[0])
        return pb, wtg.reshape(D_IN, D_MODEL), woutg.reshape(D_MODEL, D_MODEL)

    def start_exchange(dwin, dwin_sib, dwout):
        dwout = dwout.reshape(N_DEV, SHARD_OUT, D_MODEL)
        ra, rb = grad_pair_exchange(dwin_sib, dwout)
        sb, sbo = pair_sum(dwin, ra, dwout, rb, c_idx)
        send, recv, sb, sbo, rc, rco, token = grad_chip_start(sb, sbo)
        return token, (send, recv, sb, sbo, rc, rco)

    grad_x, _, _, vec_mid, vec_ada, dlb, pending = device_step(
        x[0], loss_target[0], mod, lb, project, norm_g, hg_norm_g, ret_norm_g, vec(final_g), c_idx, start_exchange)
    packs = pack_gather(pack_rows(vec_mid, vec_ada, dlb))
    dmod_all = packs[:, ROW_SHIFT:ROW_GATE + 1, :].reshape(N_DEV, 3 * D_MODEL)
    dmod_all = jnp.pad(dmod_all, ((0, LANE - N_DEV), (0, 0)))
    sc_t = jnp.pad(scall.T, ((0, 0), (0, LANE - N_DEV)))
    g_ada, d_ada, nm_ada, nv_ada = adam_ada(sc_t, dmod_all, me_idx, w_ada[0], m_w_ada[0], v_w_ada[0])
    small = adam_vectors(
        packs, lb,
        (norm_g, b_ada, hg_lb_logits, hg_norm_g, ret_norm_g, vec(final_g)),
        (m_norm_g, m_b_ada, m_hg_lb_logits, m_hg_norm_g, m_ret_norm_g, vec(m_final_g)),
        (v_norm_g, v_b_ada, v_hg_lb_logits, v_hg_norm_g, v_ret_norm_g, vec(v_final_g)))
    loss = small[0][0, 0]
    sb, sbo, rc, rco = grad_chip_wait(*pending, small[0])
    g_in, d_in, nm_in, nv_in = adam_shard(chip_idx, sb, rc, w_in[0], m_w_in[0], v_w_in[0], "adam_w_in")
    g_out, d_out, nm_out, nv_out = adam_shard(chip_idx, sbo, rco, w_out[0], m_w_out[0], v_w_out[0], "adam_w_out")
    (g_ng, d_ng, nm_ng, nv_ng), (g_b, d_b, nm_b, nv_b), (g_lb, d_lb, nm_lb, nv_lb), (g_hg, d_hg, nm_hg, nv_hg), \
        (g_rg, d_rg, nm_rg, nv_rg), (g_fg, d_fg, nm_fg, nv_fg) = [small[1 + 4 * j:5 + 4 * j] for j in range(6)]
    flat = lambda a: a.reshape(D_MODEL)

    def group(ng, ada, b, win, lbl, hg, rg, wo, fg):
        return (ng, ada[None], b, win[None], lbl, hg, rg, wo[None], flat(fg))

    return (loss, grad_x[None],
            *group(g_ng, g_ada, g_b, g_in, g_lb, g_hg, g_rg, g_out, g_fg),
            *group(d_ng, d_ada, d_b, d_in, d_lb, d_hg, d_rg, d_out, d_fg),
            *group(nm_ng, nm_ada, nm_b, nm_in, nm_lb, nm_hg, nm_rg, nm_out, nm_fg),
            *group(nv_ng, nv_ada, nv_b, nv_in, nv_lb, nv_hg, nv_rg, nv_out, nv_fg))
```

```python
import functools

import numpy as np
import jax
import jax.numpy as jnp
from jax import lax
from jax.experimental import pallas as pl
from jax.experimental.pallas import tpu as pltpu

F32 = jnp.float32
_BF = jnp.bfloat16

D_MODEL = 1024
N_HEADS = 8
LANE = 128
RET_DK = 64
D_IN = 9216
N_DEV = 8
SHARD_IN = D_IN // N_DEV
SHARD_ADA = 3 * D_MODEL // N_DEV
SHARD_OUT = D_MODEL // N_DEV
N_CB = D_IN // LANE
CB_PER_SHARD = SHARD_IN // LANE
CHUNK = 128
N_LEVELS = 7
EPS = 1e-6
LOG2_E = float(np.log2(np.e))
ROPE_BASE = 10000.0
CB_HQ, CB_HF, CB_HI, CB_HZ, CB_RQ, CB_RK, CB_RV, CB_RZ, CB_GA, CB_GB = 0, 8, 16, 24, 32, 36, 40, 48, 56, 64
VMEM_LIMIT = 56 * 1024 * 1024

ADAM_LR, ADAM_B1, ADAM_B2, ADAM_EPS, ADAM_WD, ADAM_STEP = 0.001, 0.9, 0.999, 1e-08, 0.01, 10

_NN = (((1,), (0,)), ((), ()))
_NT = (((1,), (1,)), ((), ()))
_TN = (((0,), (0,)), ((), ()))
MESH = pl.DeviceIdType.MESH


def _dot(a, b, dims=_NN):
    return lax.dot_general(a.astype(_BF), b.astype(_BF), dims, preferred_element_type=F32)


def _split2(a):
    hi = a.astype(_BF)
    lo = (a - hi.astype(F32)).astype(_BF)
    return jnp.concatenate([hi, lo], axis=1)


def _dot_sel(sel, a):
    n = a.shape[1]
    r = lax.dot_general(sel.astype(_BF), _split2(a), _NN, preferred_element_type=F32)
    return r[:, :n] + r[:, n:]


def _dot_f32(a, b):
    def pieces(v):
        p1 = v.astype(_BF)
        r1 = v - p1.astype(F32)
        p2 = r1.astype(_BF)
        p3 = (r1 - p2.astype(F32)).astype(_BF)
        return p1, p2, p3
    a1, a2, a3 = pieces(a)
    b1, b2, b3 = pieces(b)
    d = lambda u, v: lax.dot_general(u, v, _NN, preferred_element_type=F32)
    return ((d(a1, b3) + d(a2, b2) + d(a3, b1)) + (d(a1, b2) + d(a2, b1))) + d(a1, b1)


def _sigmoid(v):
    return 1.0 / (1.0 + jnp.exp(-v))


def _params(sem=None):
    return pltpu.CompilerParams(dimension_semantics=sem, vmem_limit_bytes=VMEM_LIMIT)


def _hgrn_consts():
    c, nl = CHUNK, N_LEVELS
    t = np.arange(c)[:, None]
    j = np.arange(c)[None, :]
    sel = [j <= t]
    masks = [j == t]
    for l in range(1, nl + 1):
        m = ((t >> l) << l) + (1 << (l - 1)) - 1
        sec = t > m
        sel.append(np.where(sec, (j > m) & (j <= t), (j > t) & (j <= m)))
        same = (t >> l) == (j >> l)
        masks.append(same & sec & (j <= m))
    sel.append(j > t)
    sel = np.concatenate(sel, 0).astype(np.float32)
    masks = np.stack(masks).astype(np.float32)
    sgn = np.stack([np.where((t & (1 << (l - 1))) != 0, 1.0, -1.0) * np.ones((1, LANE)) for l in range(3, nl + 1)])
    return dict(tri=jnp.asarray(sel[:c], _BF),
                lvl=jnp.asarray(masks, F32),
                sgn=jnp.asarray(sgn, F32),
                sel_t=jnp.asarray(sel.T, _BF),
                lvl_b=jnp.asarray(masks, _BF),
                lvlt_b=jnp.asarray(np.swapaxes(masks, 1, 2), _BF))


def _level_exponents(b, logf, b_scr, sgn_ref):
    c = CHUNK
    b_scr[...] = b
    row = lax.broadcasted_iota(jnp.int32, (c, LANE), 0)
    nxt = pltpu.roll(logf, c - 1, 0)
    prv = pltpu.roll(logf, 1, 0)
    r4 = row & 3
    out = [jnp.where((row & 1) == 1, logf, 0.0),
           jnp.where(r4 == 0, nxt, jnp.where(r4 == 1, 0.0, jnp.where(r4 == 2, logf, logf + prv)))]
    for l in range(3, N_LEVELS + 1):
        size, half = 1 << l, 1 << (l - 1)
        ref = jnp.concatenate([jnp.broadcast_to(b_scr[i * size + half - 1:i * size + half, :], (size, LANE))
                               for i in range(c // size)], axis=0)
        out.append((b - ref) * sgn_ref[l - 3])
    return out


def _hgrn_chunk(hq, hf, hi, lbv, tri_ref, sgn_ref, b_scr):
    sq = _sigmoid(hq)
    q = hq * sq
    sg = _sigmoid(hf)
    omlb = 1.0 - lbv
    f = lbv + omlb * sg
    k = 1.0 - f
    logf = jnp.log(f) * LOG2_E
    b = _dot_sel(tri_ref[...], logf)
    bc = jnp.sum(logf, axis=0, keepdims=True)
    lev = [None] + [jnp.exp2(e) for e in _level_exponents(b, logf, b_scr, sgn_ref)]
    return dict(sq=sq, q=q, sg=sg, omlb=omlb, f=f, k=k, v=hi, eb=jnp.exp2(b), erem=jnp.exp2(bc - b),
                ebc=jnp.exp2(bc), lev=lev)


def _blockdiag(a, b):
    z = jnp.zeros_like(a)
    return jnp.concatenate([jnp.concatenate([a, z], axis=1), jnp.concatenate([z, b], axis=1)], axis=0)


def _level_operands(a):
    q, k = a["q"].astype(_BF), a["k"].astype(_BF)
    lev = [None] + [a["lev"][l].astype(_BF) for l in range(1, N_LEVELS + 1)]
    ql = [q] + [q * lev[l] for l in range(1, N_LEVELS + 1)]
    kl = [k] + [k * lev[l] for l in range(1, N_LEVELS + 1)]
    pairs = range(0, N_LEVELS + 1, 2)
    return ([jnp.concatenate([ql[l], ql[l + 1]], axis=1) for l in pairs], [_blockdiag(kl[l], kl[l + 1]) for l in pairs],
            ql, kl)


def _hgrn_scores(a, lvl_ref, q_pairs, k_diags):
    acc = None
    for n, (qp, kd) in enumerate(zip(q_pairs, k_diags)):
        both = lax.dot_general(qp, kd, _NT, preferred_element_type=F32)
        part = lvl_ref[2 * n] * both[:, :CHUNK] + lvl_ref[2 * n + 1] * both[:, CHUNK:]
        acc = part if acc is None else acc + part
    return acc


SCAN_UNROLL = 4
RET_UNROLL = 8


def _writeback_reserve(step, make_copies):
    slot = step % 2

    @pl.when(step >= 2)
    def _():
        for cp in make_copies(slot):
            cp.wait()

    return slot


def _writeback_commit(step, n_steps, slot, make_copies):
    for cp in make_copies(slot):
        cp.start()

    @pl.when(step == n_steps - 1)
    def _():
        for cp in make_copies(slot):
            cp.wait()
        if n_steps > 1:
            for cp in make_copies(1 - slot):
                cp.wait()


def _resident(const):
    zeros = (0,) * const.ndim
    return pl.BlockSpec(const.shape, lambda p, t: zeros)


def _time_block(t_len):
    return min(t_len, 2048)


def hgrn_forward(pb, lb, t_len):
    nc = t_len // CHUNK
    tb = _time_block(t_len)
    ncb = tb // CHUNK
    consts = _hgrn_consts()
    operands = [consts[n] for n in ("tri", "lvl", "sgn")]

    def body(hq_ref, hf_ref, hi_ref, lb_ref, tri_ref, lvl_ref, sgn_ref, o_ref, ssave_ref, asave_ref, st_ref, b_scr):
        @pl.when(pl.program_id(1) == 0)
        def _():
            st_ref[...] = jnp.zeros_like(st_ref)

        def chunk(ci, carry):
            r = pl.ds(pl.multiple_of(ci * CHUNK, CHUNK), CHUNK)
            for hd in range(2):
                lbv = lb_ref[:, hd * LANE:(hd + 1) * LANE]
                a = _hgrn_chunk(hq_ref[hd, r, :], hf_ref[hd, r, :], hi_ref[hd, r, :], lbv, tri_ref, sgn_ref,
                                b_scr.at[hd])
                q_pairs, k_diags, _, _ = _level_operands(a)
                st = st_ref[hd]
                ssave_ref[hd, ci] = st
                scores = _hgrn_scores(a, lvl_ref, q_pairs, k_diags).astype(asave_ref.dtype)
                asave_ref[hd, ci] = scores
                o_ref[hd, r, :] = _dot(a["q"] * a["eb"], st, _NT) + _dot(scores, a["v"])
                st_ref[hd] = st * a["ebc"] + _dot(a["v"], a["k"] * a["erem"], _TN)
            return carry

        lax.fori_loop(0, ncb, chunk, 0, unroll=SCAN_UNROLL)

    pair = lambda base: pl.BlockSpec((2, tb, LANE), lambda p, t, base=base: (base // 2 + p, t, 0))
    per_chunk = pl.BlockSpec((2, ncb, LANE, LANE), lambda p, t: (p, t, 0, 0))
    return pl.pallas_call(
        body, name="hgrn_fwd", grid=(N_HEADS // 2, t_len // tb),
        in_specs=[pair(CB_HQ), pair(CB_HF), pair(CB_HI),
                  pl.BlockSpec((1, 2 * LANE), lambda p, t: (0, p))] + [_resident(c) for c in operands],
        out_specs=[pl.BlockSpec((2, tb, LANE), lambda p, t: (p, t, 0)), per_chunk, per_chunk],
        out_shape=[jax.ShapeDtypeStruct((N_HEADS, t_len, LANE), F32),
                   jax.ShapeDtypeStruct((N_HEADS, nc, LANE, LANE), F32),
                   jax.ShapeDtypeStruct((N_HEADS, nc, CHUNK, CHUNK), _BF)],
        scratch_shapes=[pltpu.VMEM((2, LANE, LANE), F32), pltpu.VMEM((2, CHUNK, LANE), F32)],
        compiler_params=_params(("arbitrary", "arbitrary")),
    )(pb, pb, pb, lb, *operands)


def hgrn_backward(pb, lb, do, ssave, asave, dpb, t_len):
    tb = _time_block(t_len)
    ncb, ntb = tb // CHUNK, t_len // tb
    consts = _hgrn_consts()
    operands = [consts[n] for n in ("tri", "sgn", "sel_t", "lvl_b", "lvlt_b")]

    def body(hq_ref, hf_ref, hi_ref, lb_ref, do_ref, ssave_ref, asave_ref, tri_ref, sgn_ref, selt_ref, lvlb_ref,
             lvltb_ref, dpb_in, dpb_ref, dlb_ref, dq_buf, df_buf, di_buf, dst_ref, b_scr, sems):
        del dpb_in
        p, t = pl.program_id(0), pl.program_id(1)
        step = p * ntb + t
        rows = pl.ds(pl.multiple_of((ntb - 1 - t) * tb, tb), tb)

        def out_copies(sl):
            return [pltpu.make_async_copy(buf.at[sl], dpb_ref.at[pl.ds(base + 2 * p, 2), rows], sems.at[sl, n])
                    for n, (buf, base) in enumerate(((dq_buf, CB_HQ), (df_buf, CB_HF), (di_buf, CB_HI)))]

        slot = _writeback_reserve(step, out_copies)

        @pl.when(t == 0)
        def _():
            dst_ref[...] = jnp.zeros_like(dst_ref)
            dlb_ref[...] = jnp.zeros_like(dlb_ref)

        def chunk(i, carry):
            ci = ncb - 1 - i
            r = pl.ds(pl.multiple_of(ci * CHUNK, CHUNK), CHUNK)
            for hd in range(2):
                head_chunk(hd, ci, r)
            return carry

        def head_chunk(hd, ci, r):
            lbv = lb_ref[:, hd * LANE:(hd + 1) * LANE]
            hq = hq_ref[hd, r, :]
            a = _hgrn_chunk(hq, hf_ref[hd, r, :], hi_ref[hd, r, :], lbv, tri_ref, sgn_ref, b_scr.at[hd])
            _, k_diags, ql, kl = _level_operands(a)
            q, k, v = a["q"], a["k"], a["v"]
            g = do_ref[hd, r, :]
            st0 = ssave_ref[hd, ci]
            dst = dst_ref[hd]
            scores = asave_ref[hd, ci]
            da = _dot(g, v, _NT)
            da_t = _dot(v, g, _NT)
            kb = k * a["erem"]
            qb = q * a["eb"]
            dv = _dot(scores, g, _TN) + _dot(kb, dst, _NT)
            dq_inter = _dot(g, st0) * a["eb"]
            dk_state = _dot(v, dst) * a["erem"]
            dq, dk = dq_inter, dk_state
            de = [q * dq_inter]
            da_b, dat_b = da.astype(_BF), da_t.astype(_BF)
            for n in range(len(k_diags)):
                l0, l1 = 2 * n, 2 * n + 1
                da_pair = jnp.concatenate([lvlb_ref[l0] * da_b, lvlb_ref[l1] * da_b], axis=1)
                dat_pair = jnp.concatenate([lvltb_ref[l0] * dat_b, lvltb_ref[l1] * dat_b], axis=1)
                dq_both = lax.dot_general(da_pair, k_diags[n], _NN, preferred_element_type=F32)
                dk_both = lax.dot_general(dat_pair, _blockdiag(ql[l0], ql[l1]), _NN, preferred_element_type=F32)
                for l, cols in ((l0, slice(0, LANE)), (l1, slice(LANE, 2 * LANE))):
                    dql, dkl = dq_both[:, cols], dk_both[:, cols]
                    if l > 0:
                        e = a["lev"][l]
                        dql, dkl = dql * e, dkl * e
                        de.append(q * dql + k * dkl)
                    dq = dq + dql
                    dk = dk + dkl
            de.append(k * dk_state)
            dst_ref[hd] = dst * a["ebc"] + _dot(g, qb, _TN)
            dbc = jnp.sum(dst * st0, axis=0, keepdims=True) * a["ebc"]
            de2 = lax.dot_general(selt_ref[...], _split2(jnp.concatenate(de, axis=0)), _NN,
                                  preferred_element_type=F32)
            dlogf = de2[:, :LANE] + de2[:, LANE:] + dbc
            sq, sg = a["sq"], a["sg"]
            df = dlogf / a["f"] - dk
            dq_buf[slot, hd, r, :] = (dq * (sq * (1.0 + hq * (1.0 - sq)))).astype(dq_buf.dtype)
            df_buf[slot, hd, r, :] = (df * a["omlb"] * sg * (1.0 - sg)).astype(df_buf.dtype)
            di_buf[slot, hd, r, :] = dv.astype(di_buf.dtype)
            cols = slice(hd * LANE, (hd + 1) * LANE)
            dlb_ref[:, cols] = dlb_ref[:, cols] + jnp.sum(df * (1.0 - sg), axis=0, keepdims=True)

        lax.fori_loop(0, ncb, chunk, 0, unroll=SCAN_UNROLL)
        _writeback_commit(step, (N_HEADS // 2) * ntb, slot, out_copies)

    pair = lambda base: pl.BlockSpec((2, tb, LANE), lambda p, t, base=base: (base // 2 + p, ntb - 1 - t, 0))
    any_spec = pl.BlockSpec(memory_space=pl.ANY)
    per_chunk = pl.BlockSpec((2, ncb, LANE, LANE), lambda p, t: (p, ntb - 1 - t, 0, 0))
    return pl.pallas_call(
        body, name="hgrn_bwd", grid=(N_HEADS // 2, ntb),
        in_specs=[pair(CB_HQ), pair(CB_HF), pair(CB_HI),
                  pl.BlockSpec((1, 2 * LANE), lambda p, t: (0, p)),
                  pair(0), per_chunk, per_chunk]
        + [_resident(c) for c in operands] + [any_spec],
        out_specs=[any_spec, pl.BlockSpec((1, 2 * LANE), lambda p, t: (0, p))],
        out_shape=[jax.ShapeDtypeStruct(dpb.shape, dpb.dtype), jax.ShapeDtypeStruct((1, D_MODEL), F32)],
        scratch_shapes=[pltpu.VMEM((2, 2, tb, LANE), dpb.dtype)] * 3 + [
            pltpu.VMEM((2, LANE, LANE), F32), pltpu.VMEM((2, CHUNK, LANE), F32), pltpu.SemaphoreType.DMA((2, 3))],
        input_output_aliases={7 + len(operands): 0},
        compiler_params=_params(("arbitrary", "arbitrary")),
    )(pb, pb, pb, lb, do, ssave, asave, *operands, dpb)


def _rope_tables(t_len):
    half = RET_DK // 2
    inv_freq = (1.0 / (np.float32(ROPE_BASE) ** np.linspace(0.0, 1.0, half, dtype=np.float32))).astype(np.float32)
    ang = (np.arange(t_len, dtype=np.float32)[:, None] * inv_freq[None, :]).astype(np.float64)
    cos, sin = np.cos(ang).astype(np.float32), np.sin(ang).astype(np.float32)
    cos_t = np.concatenate([cos, cos, cos, cos], axis=1)
    sin_t = np.concatenate([-sin, sin, -sin, sin], axis=1)
    return jnp.asarray(cos_t), jnp.asarray(sin_t)


def _swap_halves(v):
    half = RET_DK // 2
    lane = lax.broadcasted_iota(jnp.int32, v.shape, 1)
    first = (lane & (RET_DK - 1)) < half
    return jnp.where(first, pltpu.roll(v, LANE - half, 1), pltpu.roll(v, half, 1))


def _ret_head_consts(hidx):
    c = CHUNK
    hf = jnp.full((1, LANE), hidx, jnp.int32).astype(F32)
    lg = jnp.log(1.0 - jnp.exp(-(5.0 + hf) * np.float32(np.log(2.0))))
    row = lax.broadcasted_iota(jnp.int32, (c, c), 0)
    col = lax.broadcasted_iota(jnp.int32, (c, c), 1)
    rel = (row - col).astype(F32)
    dm = jnp.where(rel >= 0, jnp.exp(lg[:, :1] * jnp.maximum(rel, 0.0)), 0.0)
    dm_t = jnp.where(rel <= 0, jnp.exp(lg[:, :1] * jnp.maximum(-rel, 0.0)), 0.0)
    idx = lax.broadcasted_iota(jnp.int32, (c, LANE), 0).astype(F32)
    zeta = jnp.exp(lg * (c - 1.0 - idx))
    xi = jnp.exp(lg * (idx + 1.0))
    cdec = jnp.exp(lg * float(c))
    return dm, zeta, xi, cdec, dm_t


def _lane_mask(which):
    lane = lax.broadcasted_iota(jnp.int32, (1, LANE), 1)
    return ((lane // RET_DK) == which).astype(F32)


def retention_forward(pb, cos_t, sin_t, t_len):
    nc = t_len // CHUNK

    tb = _time_block(t_len)
    ncb = tb // CHUNK

    def body(rq_ref, rk_ref, rv_ref, cos_ref, sin_ref, o_ref, rsave_ref, st_ref):
        p = pl.program_id(0)

        @pl.when(pl.program_id(1) == 0)
        def _():
            st_ref[...] = jnp.zeros_like(st_ref)

        consts = [_ret_head_consts(2 * p + hd) for hd in range(2)]

        def chunk(ci, carry):
            r = pl.ds(pl.multiple_of(ci * CHUNK, CHUNK), CHUNK)
            cs, sn = cos_ref[r, :], sin_ref[r, :]
            q = rq_ref[r, :]
            k = rk_ref[r, :]
            q = q * cs + _swap_halves(q) * sn
            k = (k * cs + _swap_halves(k) * sn) * RET_DK ** -0.5
            for hd in range(2):
                dm, zeta, xi, cdec, _ = consts[hd]
                lm = _lane_mask(hd)
                qh, kh = q * lm, k * lm
                v = rv_ref[hd, r, :]
                st = st_ref[hd]
                rsave_ref[hd, ci] = st
                scores = _dot(qh, kh, _NT) * dm
                o_ref[hd, r, :] = _dot(scores, v) + _dot(qh * xi, st, _NT)
                st_ref[hd] = st * cdec + _dot(v, kh * zeta, _TN)
            return carry

        lax.fori_loop(0, ncb, chunk, 0, unroll=RET_UNROLL)

    return pl.pallas_call(
        body, name="ret_fwd", grid=(N_HEADS // 2, t_len // tb),
        in_specs=[pl.BlockSpec((None, tb, LANE), lambda p, t: (CB_RQ + p, t, 0)),
                  pl.BlockSpec((None, tb, LANE), lambda p, t: (CB_RK + p, t, 0)),
                  pl.BlockSpec((2, tb, LANE), lambda p, t: (CB_RV // 2 + p, t, 0)),
                  pl.BlockSpec((tb, LANE), lambda p, t: (t, 0)),
                  pl.BlockSpec((tb, LANE), lambda p, t: (t, 0))],
        out_specs=[pl.BlockSpec((2, tb, LANE), lambda p, t: (p, t, 0)),
                   pl.BlockSpec((2, ncb, LANE, LANE), lambda p, t: (p, t, 0, 0))],
        out_shape=[jax.ShapeDtypeStruct((N_HEADS, t_len, LANE), F32),
                   jax.ShapeDtypeStruct((N_HEADS, nc, LANE, LANE), F32)],
        scratch_shapes=[pltpu.VMEM((2, LANE, LANE), F32)],
        compiler_params=_params(("arbitrary", "arbitrary")),
    )(pb, pb, pb, cos_t, sin_t)


def retention_backward(pb, cos_t, sin_t, do, rsave, dpb, t_len):
    tb = _time_block(t_len)
    ncb, ntb = tb // CHUNK, t_len // tb

    def body(rq_ref, rk_ref, rv_ref, cos_ref, sin_ref, do_ref, rsave_ref, dpb_in,
             dpb_ref, dq_buf, dk_buf, dv_buf, dst_ref, sems):
        del dpb_in
        p, t = pl.program_id(0), pl.program_id(1)
        step = p * ntb + t
        rows = pl.ds(pl.multiple_of((ntb - 1 - t) * tb, tb), tb)

        def out_copies(sl):
            return [pltpu.make_async_copy(dq_buf.at[sl], dpb_ref.at[CB_RQ + p, rows], sems.at[sl, 0]),
                    pltpu.make_async_copy(dk_buf.at[sl], dpb_ref.at[CB_RK + p, rows], sems.at[sl, 1]),
                    pltpu.make_async_copy(dv_buf.at[sl], dpb_ref.at[pl.ds(CB_RV + 2 * p, 2), rows], sems.at[sl, 2])]

        slot = _writeback_reserve(step, out_copies)

        @pl.when(t == 0)
        def _():
            dst_ref[...] = jnp.zeros_like(dst_ref)

        consts = [_ret_head_consts(2 * p + hd) for hd in range(2)]

        def chunk(i, carry):
            ci = ncb - 1 - i
            r = pl.ds(pl.multiple_of(ci * CHUNK, CHUNK), CHUNK)
            cs, sn = cos_ref[r, :], sin_ref[r, :]
            q = rq_ref[r, :]
            k = rk_ref[r, :]
            q = q * cs + _swap_halves(q) * sn
            k = (k * cs + _swap_halves(k) * sn) * RET_DK ** -0.5
            dq, dk = None, None
            for hd in range(2):
                dm, zeta, xi, cdec, dm_t = consts[hd]
                lm = _lane_mask(hd)
                qh, kh = q * lm, k * lm
                v = rv_ref[hd, r, :]
                g = do_ref[hd, r, :]
                st0 = rsave_ref[hd, ci]
                dst = dst_ref[hd]
                scores_t = _dot(kh, qh, _NT) * dm_t
                dsc = _dot(g, v, _NT) * dm
                dsc_t = _dot(v, g, _NT) * dm_t
                dqh = _dot(dsc, kh) + _dot(g, st0) * xi
                dkh = _dot(dsc_t, qh) + _dot(v, dst) * zeta
                dv_buf[slot, hd, r, :] = (_dot(scores_t, g) + _dot(kh * zeta, dst, _NT)).astype(dv_buf.dtype)
                dst_ref[hd] = dst * cdec + _dot(g, qh * xi, _TN)
                dq = dqh if dq is None else dq + dqh
                dk = dkh if dk is None else dk + dkh
            dk = dk * (RET_DK ** -0.5)
            dq_buf[slot, r, :] = (dq * cs - _swap_halves(dq) * sn).astype(dq_buf.dtype)
            dk_buf[slot, r, :] = (dk * cs - _swap_halves(dk) * sn).astype(dk_buf.dtype)
            return carry

        lax.fori_loop(0, ncb, chunk, 0, unroll=RET_UNROLL)
        _writeback_commit(step, (N_HEADS // 2) * ntb, slot, out_copies)

    any_spec = pl.BlockSpec(memory_space=pl.ANY)
    return pl.pallas_call(
        body, name="ret_bwd", grid=(N_HEADS // 2, ntb),
        in_specs=[pl.BlockSpec((None, tb, LANE), lambda p, t: (CB_RQ + p, ntb - 1 - t, 0)),
                  pl.BlockSpec((None, tb, LANE), lambda p, t: (CB_RK + p, ntb - 1 - t, 0)),
                  pl.BlockSpec((2, tb, LANE), lambda p, t: (CB_RV // 2 + p, ntb - 1 - t, 0)),
                  pl.BlockSpec((tb, LANE), lambda p, t: (ntb - 1 - t, 0)),
                  pl.BlockSpec((tb, LANE), lambda p, t: (ntb - 1 - t, 0)),
                  pl.BlockSpec((2, tb, LANE), lambda p, t: (p, ntb - 1 - t, 0)),
                  pl.BlockSpec((2, ncb, LANE, LANE), lambda p, t: (p, ntb - 1 - t, 0, 0)),
                  any_spec],
        out_specs=any_spec,
        out_shape=jax.ShapeDtypeStruct(dpb.shape, dpb.dtype),
        scratch_shapes=[pltpu.VMEM((2, tb, LANE), dpb.dtype), pltpu.VMEM((2, tb, LANE), dpb.dtype),
                        pltpu.VMEM((2, 2, tb, LANE), dpb.dtype), pltpu.VMEM((2, LANE, LANE), F32),
                        pltpu.SemaphoreType.DMA((2, 3))],
        input_output_aliases={7: 0},
        compiler_params=_params(("arbitrary", "arbitrary")),
    )(pb, pb, pb, cos_t, sin_t, do, rsave, dpb)


def _row_tile(t_len, want):
    return min(want, t_len)


PAIR_CB = 2 * CB_PER_SHARD


def proj_forward(h, wt, t_len):
    tm = _row_tile(t_len, 1024)

    def body(h_ref, w_ref, o_ref):
        acc = _dot(h_ref[...], w_ref[...], _NT)
        for jj in range(PAIR_CB):
            o_ref[jj] = acc[:, jj * LANE:(jj + 1) * LANE]

    return pl.pallas_call(
        body, name="proj_fwd", grid=(N_DEV // 2, t_len // tm),
        in_specs=[pl.BlockSpec((tm, D_MODEL), lambda j, i: (i, 0)),
                  pl.BlockSpec((PAIR_CB * LANE, D_MODEL), lambda j, i: (j, 0))],
        out_specs=pl.BlockSpec((PAIR_CB, tm, LANE), lambda j, i: (j, i, 0)),
        out_shape=jax.ShapeDtypeStruct((N_CB, t_len, LANE), F32),
        compiler_params=_params(("arbitrary", "arbitrary")),
    )(h, wt)


def proj_backward_input(dpb, wt, token, x, dy, norm_g, scale1p, t_len):
    tm = _row_tile(t_len, 512)

    def body(a_ref, wt_hbm, token_ref, x_ref, dy_ref, g_ref, sc_ref, gx_ref, vec_ref, w_ref, sem):
        del token_ref
        i = pl.program_id(0)

        @pl.when(i == 0)
        def _():
            cp = pltpu.make_async_copy(wt_hbm, w_ref, sem)
            cp.start()
            cp.wait()

        a = jnp.concatenate([a_ref[jj].astype(_BF) for jj in range(N_CB)], axis=1)
        dhv = _dot(a, w_ref[...])
        xv, g, sc = x_ref[...], g_ref[...], sc_ref[...]
        r = lax.rsqrt(jnp.mean(xv * xv, axis=-1, keepdims=True) + EPS)
        xn = xv * r
        dxn = dhv * (g * sc)
        gx_ref[...] = dy_ref[...] + r * dxn - xn * (r * r) * jnp.mean(xv * dxn, axis=-1, keepdims=True)
        t = dhv * xn
        _acc_rows(vec_ref, i, [jnp.sum(t * sc, axis=0, keepdims=True),
                               jnp.sum(t * g, axis=0, keepdims=True),
                               jnp.sum(dhv, axis=0, keepdims=True)])

    row = pl.BlockSpec((tm, D_MODEL), lambda i: (i, 0))
    return pl.pallas_call(
        body, name="proj_bwd_input", grid=(t_len // tm,),
        in_specs=[pl.BlockSpec((N_CB, tm, LANE), lambda i: (0, i, 0)),
                  pl.BlockSpec(memory_space=pl.ANY),
                  pl.BlockSpec(token.shape, lambda i: (0, 0)),
                  row, row, _vec_spec(), _vec_spec()],
        out_specs=[row, pl.BlockSpec((8, D_MODEL), lambda i: (0, 0))],
        out_shape=[jax.ShapeDtypeStruct((t_len, D_MODEL), F32), jax.ShapeDtypeStruct((8, D_MODEL), F32)],
        scratch_shapes=[pltpu.VMEM(wt.shape, wt.dtype), pltpu.SemaphoreType.DMA],
        compiler_params=_params(("arbitrary",)),
    )(dpb, wt, token, x, dy, norm_g, scale1p)


def proj_backward_weight(h_t, dpb, t_len):
    tk = _row_tile(t_len, 2048)

    def body(h_ref, b_ref, o_ref):
        k = pl.program_id(1)
        b = jnp.concatenate([b_ref[jj].astype(_BF) for jj in range(PAIR_CB)], axis=1)
        part = _dot(h_ref[...], b)

        @pl.when(k == 0)
        def _():
            for s in range(2):
                o_ref[s] = part[:, s * SHARD_IN:(s + 1) * SHARD_IN]

        @pl.when(k > 0)
        def _():
            for s in range(2):
                o_ref[s] = o_ref[s] + part[:, s * SHARD_IN:(s + 1) * SHARD_IN]

    return pl.pallas_call(
        body, name="proj_bwd_weight", grid=(N_DEV // 2, t_len // tk),
        in_specs=[pl.BlockSpec((D_MODEL, tk), lambda j, k: (0, k)),
                  pl.BlockSpec((PAIR_CB, tk, LANE), lambda j, k: (j, k, 0))],
        out_specs=pl.BlockSpec((2, D_MODEL, SHARD_IN), lambda j, k: (j, 0, 0)),
        out_shape=jax.ShapeDtypeStruct((N_DEV, D_MODEL, SHARD_IN), F32),
        compiler_params=_params(("arbitrary", "arbitrary")),
    )(h_t, dpb)


def sibling_blocks(g_in, c_idx):
    tr = D_MODEL

    def body(c_ref, g_ref, o_ref):
        del c_ref
        o_ref[...] = g_ref[...].astype(o_ref.dtype)

    return pl.pallas_call(
        body, name="sibling_blocks",
        grid_spec=pltpu.PrefetchScalarGridSpec(
            num_scalar_prefetch=1, grid=(N_DEV // 2, D_MODEL // tr),
            in_specs=[pl.BlockSpec((None, tr, SHARD_IN), lambda q, i, c: (2 * q + 1 - c[0], i, 0))],
            out_specs=pl.BlockSpec((None, tr, SHARD_IN), lambda q, i, c: (q, i, 0))),
        out_shape=jax.ShapeDtypeStruct((N_DEV // 2, D_MODEL, SHARD_IN), _BF),
        compiler_params=_params(("arbitrary", "arbitrary")),
    )(c_idx, g_in)


def _vec_spec():
    return pl.BlockSpec((1, D_MODEL), lambda i: (0, 0))


def _acc_rows(ref, i, rows):
    @pl.when(i == 0)
    def _():
        ref[...] = jnp.zeros_like(ref)

    for n, row in enumerate(rows):
        ref[n:n + 1, :] = ref[n:n + 1, :] + row


def adaln_forward(x, norm_g, scale1p, shift, t_len):
    tm = _row_tile(t_len, 512)

    def body(x_ref, g_ref, sc_ref, sh_ref, h_ref, ht_ref):
        xv = x_ref[...]
        r = lax.rsqrt(jnp.mean(xv * xv, axis=-1, keepdims=True) + EPS)
        h = xv * r * g_ref[...] * sc_ref[...] + sh_ref[...]
        h_ref[...] = h.astype(h_ref.dtype)
        ht_ref[...] = h.T.astype(ht_ref.dtype)

    return pl.pallas_call(
        body, name="adaln_fwd", grid=(t_len // tm,),
        in_specs=[pl.BlockSpec((tm, D_MODEL), lambda i: (i, 0)), _vec_spec(), _vec_spec(), _vec_spec()],
        out_specs=[pl.BlockSpec((tm, D_MODEL), lambda i: (i, 0)), pl.BlockSpec((D_MODEL, tm), lambda i: (0, i))],
        out_shape=[jax.ShapeDtypeStruct((t_len, D_MODEL), _BF), jax.ShapeDtypeStruct((D_MODEL, t_len), _BF)],
        compiler_params=_params(("arbitrary",)),
    )(x, norm_g, scale1p, shift)


def _head_norm(o, g):
    r = lax.rsqrt(jnp.mean(o * o, axis=-1, keepdims=True) + EPS)
    return r, o * r * g


def _group_spec(tm, cb):
    return pl.BlockSpec((N_HEADS, tm, LANE), lambda i, cb=cb: (cb // N_HEADS, i, 0))


MID_FINAL_G, MID_GATE, MID_LOSS, MID_HG_G, MID_RET_G = range(5)


def middle(x, target, oa, ob, pb, wout, gate, final_g, hg_g, ret_g, t_len):
    tm = _row_tile(t_len, 256)
    n_steps = t_len // tm

    def body(x_ref, t_ref, oa_ref, ob_ref, hz_ref, rz_ref, ga_ref, gb_ref, w_ref, gate_ref, fg_ref, hg_ref, rg_ref,
             dy_ref, doa_ref, dob_ref, dw_ref, vec_ref, dpb_ref, m_scr, dm_scr, keep, bufs, sems):
        i = pl.program_id(0)
        rows = pl.ds(pl.multiple_of(i * tm, tm), tm)

        def group_copies(sl):
            return [pltpu.make_async_copy(bufs.at[sl, n], dpb_ref.at[pl.ds(cb, N_HEADS), rows], sems.at[sl, n])
                    for n, cb in enumerate((CB_HZ, CB_RZ, CB_GA, CB_GB))]
        sides = ((oa_ref, hz_ref, ga_ref, hg_ref, doa_ref), (ob_ref, rz_ref, gb_ref, rg_ref, dob_ref))
        for hh in range(N_HEADS):
            ls = slice(hh * LANE, (hh + 1) * LANE)
            acc = None
            for side, (o_ref, z_ref, gt_ref, g_ref, _) in enumerate(sides):
                o = o_ref[hh]
                rr = lax.rsqrt(jnp.mean(o * o, axis=-1, keepdims=True) + EPS)
                orr = o * rr
                zz = z_ref[hh]
                sz = _sigmoid(zz)
                sgt = _sigmoid(gt_ref[hh])
                keep[side, hh, 0] = orr
                keep[side, hh, 1] = sz
                keep[side, hh, 2] = sgt
                keep[side, hh, 3] = jnp.broadcast_to(rr, orr.shape)
                u = sgt * ((orr * g_ref[:, ls]) * (zz * sz))
                acc = u if acc is None else acc + u
            m_scr[:, ls] = acc.astype(m_scr.dtype)
        zv = _dot(m_scr[...], w_ref[...])
        gt, fg = gate_ref[...], fg_ref[...]
        y = x_ref[...] + gt * zv
        r = lax.rsqrt(jnp.mean(y * y, axis=-1, keepdims=True) + EPS)
        yn = y * r
        err = yn * fg - t_ref[...]
        loss = 0.5 * jnp.sum(jnp.mean(err * err, axis=-1, keepdims=True), axis=0, keepdims=True)
        dout = err * (1.0 / D_MODEL)
        gd = dout * fg
        dy = r * gd - yn * (r * r) * jnp.mean(y * gd, axis=-1, keepdims=True)
        dy_ref[...] = dy
        dz = (dy * gt).astype(_BF)
        dm_scr[...] = _dot(dz, w_ref[...], _NT)
        part = _dot(m_scr[...], dz, _TN)

        @pl.when(i == 0)
        def _():
            dw_ref[...] = part

        @pl.when(i > 0)
        def _():
            dw_ref[...] = dw_ref[...] + part

        slot = _writeback_reserve(i, group_copies)
        dg = [[], []]
        for hh in range(N_HEADS):
            ls = slice(hh * LANE, (hh + 1) * LANE)
            dmh = dm_scr[:, ls]
            for side, (o_ref, z_ref, gt_ref, g_ref, do_ref) in enumerate(sides):
                zz, g = z_ref[hh], g_ref[:, ls]
                orr, sz, sgt, rr = keep[side, hh, 0], keep[side, hh, 1], keep[side, hh, 2], keep[side, hh, 3]
                n = orr * g
                silu = zz * sz
                du = dmh * sgt
                bufs[slot, 2 + side, hh] = (dmh * (n * silu) * (sgt * (1.0 - sgt))).astype(bufs.dtype)
                bufs[slot, side, hh] = (du * n * (sz * (1.0 + zz * (1.0 - sz)))).astype(bufs.dtype)
                dn = du * silu
                dg[side].append(jnp.sum(dn * orr, axis=0, keepdims=True))
                gdn = dn * g
                do_ref[hh] = (rr * (gdn - orr * jnp.mean(orr * gdn, axis=-1, keepdims=True))).astype(do_ref.dtype)
        _acc_rows(vec_ref, i, [jnp.sum(dout * yn, axis=0, keepdims=True),
                               jnp.sum(dy * zv, axis=0, keepdims=True),
                               jnp.broadcast_to(loss, (1, D_MODEL)),
                               jnp.concatenate(dg[0], axis=1), jnp.concatenate(dg[1], axis=1)])
        _writeback_commit(i, n_steps, slot, group_copies)

    row = pl.BlockSpec((tm, D_MODEL), lambda i: (i, 0))
    head = pl.BlockSpec((N_HEADS, tm, LANE), lambda i: (0, i, 0))
    full = pl.BlockSpec((D_MODEL, D_MODEL), lambda i: (0, 0))
    return pl.pallas_call(
        body, name="middle", grid=(n_steps,),
        in_specs=[row, row, head, head, _group_spec(tm, CB_HZ), _group_spec(tm, CB_RZ), _group_spec(tm, CB_GA),
                  _group_spec(tm, CB_GB), full, _vec_spec(), _vec_spec(), _vec_spec(), _vec_spec()],
        out_specs=[row, head, head, full, pl.BlockSpec((8, D_MODEL), lambda i: (0, 0)),
                   pl.BlockSpec(memory_space=pl.ANY)],
        out_shape=[jax.ShapeDtypeStruct((t_len, D_MODEL), F32),
                   jax.ShapeDtypeStruct((N_HEADS, t_len, LANE), _BF),
                   jax.ShapeDtypeStruct((N_HEADS, t_len, LANE), _BF),
                   jax.ShapeDtypeStruct((D_MODEL, D_MODEL), F32),
                   jax.ShapeDtypeStruct((8, D_MODEL), F32),
                   jax.ShapeDtypeStruct((N_CB, t_len, LANE), _BF)],
        scratch_shapes=[pltpu.VMEM((tm, D_MODEL), _BF), pltpu.VMEM((tm, D_MODEL), F32),
                        pltpu.VMEM((2, N_HEADS, 4, tm, LANE), F32),
                        pltpu.VMEM((2, 4, N_HEADS, tm, LANE), _BF), pltpu.SemaphoreType.DMA((2, 4))],
        compiler_params=_params(("arbitrary",)),
    )(x, target, oa, ob, pb, pb, pb, pb, wout, gate, final_g, hg_g, ret_g)


def device_step(x, target, mod, lb, project, norm_g, hg_g, ret_g, final_g, c_idx=None, start_exchange=None):
    t_len = x.shape[0]
    shift, scale, gate = mod[:, :D_MODEL], mod[:, D_MODEL:2 * D_MODEL], mod[:, 2 * D_MODEL:]
    scale1p = 1.0 + scale
    cos_t, sin_t = _rope_tables(t_len)
    h, h_t = adaln_forward(x, norm_g, scale1p, shift, t_len)
    pb, wt, wout = project(h)
    oa, ssave, asave = hgrn_forward(pb, lb, t_len)
    ob, rsave = retention_forward(pb, cos_t, sin_t, t_len)
    dy, doa, dob, dwout, vec_mid, dpb = middle(x, target, oa, ob, pb, wout, gate, final_g, hg_g, ret_g, t_len)
    dpb, dlb = hgrn_backward(pb, lb, doa, ssave, asave, dpb, t_len)
    dpb = retention_backward(pb, cos_t, sin_t, dob, rsave, dpb, t_len)
    c_idx = jnp.zeros((1,), jnp.int32) if c_idx is None else c_idx
    dwin = proj_backward_weight(h_t, dpb, t_len)
    dwin_sib = sibling_blocks(dwin, c_idx)
    token, pending = (start_exchange(dwin, dwin_sib, dwout) if start_exchange
                      else (jnp.zeros((8, LANE), F32), None))
    grad_x, vec_ada = proj_backward_input(dpb, wt, token, x, dy, norm_g, scale1p, t_len)
    return grad_x, dwin, dwout, vec_mid, vec_ada, dlb, pending


PACK_ROWS = 16
ROW_NORM_G, ROW_LB, ROW_HG_G, ROW_RET_G, ROW_FINAL_G, ROW_SHIFT, ROW_SCALE, ROW_GATE, ROW_LOSS = range(9)


def _mesh_pos():
    return lax.axis_index("x"), lax.axis_index("y"), lax.axis_index("c")


def _lin(pos):
    return 4 * pos[0] + 2 * pos[1] + pos[2]


def _xor_peer(pos, k):
    return tuple(1 - p if (k >> s) & 1 else p for p, s in zip(pos, (2, 1, 0)))


def _other_chips(pos):
    x, y, _ = pos
    return [(1 - x, y), (x, 1 - y), (1 - x, 1 - y)]


def _remote(src, dst, send_sem, recv_sem, to):
    return pltpu.make_async_remote_copy(src_ref=src, dst_ref=dst, send_sem=send_sem, recv_sem=recv_sem,
                                        device_id=to, device_id_type=MESH)


def pre_exchange(c, w_ada, b_ada, logits, w_in, w_out):
    def body(c_ref, wada_ref, bada_ref, logit_ref, win_ref, wout_ref, mod_ref, scall_ref, lb_ref, wt_ref, wo_ref,
             cg_ref, modall_ref, parts_ref, send1, recv1, send2, recv2):
        pos = _mesh_pos()
        cv = c_ref[...]
        slot = lambda p: pl.ds(pl.multiple_of(8 * _lin(p), 8), 8)
        cg_ref[slot(pos), :] = jnp.broadcast_to(cv * _sigmoid(cv), (8, D_MODEL))
        lb_ref[...] = _sigmoid(logit_ref[0:1, :] - logit_ref[1:2, :])
        peers = [_xor_peer(pos, k) for k in range(1, N_DEV)]
        gather = [_remote(cg_ref.at[slot(pos)], cg_ref.at[slot(pos)], send1.at[n], recv1.at[n], p)
                  for n, p in enumerate(peers)]
        for cp in gather:
            cp.start()
        wt_ref[...] = win_ref[...].T.astype(wt_ref.dtype)
        wo_ref[...] = wout_ref[...].astype(wo_ref.dtype)
        for n, p in enumerate(peers):
            _remote(cg_ref.at[slot(p)], cg_ref.at[slot(p)], send1.at[n], recv1.at[n], p).wait_recv()
        modall_ref[...] = _dot(cg_ref[...], wada_ref[...])
        scatter = [_remote(modall_ref.at[slot(p)], parts_ref.at[slot(pos)], send2.at[n], recv2.at[n], p)
                   for n, p in enumerate(peers)]
        for cp in scatter:
            cp.start()
        parts_ref[slot(pos), :] = modall_ref[slot(pos), :]
        for n, p in enumerate(peers):
            _remote(modall_ref.at[slot(p)], parts_ref.at[slot(p)], send2.at[n], recv2.at[n], p).wait_recv()
        for cp in gather + scatter:
            cp.wait_send()
        for j in range(N_DEV):
            cols = slice(j * SHARD_ADA, (j + 1) * SHARD_ADA)
            mod_ref[:, cols] = parts_ref[8 * j:8 * j + 1, :] + bada_ref[:, cols]
            scall_ref[j:j + 1, :] = cg_ref[8 * j:8 * j + 1, :]

    vmem = pl.BlockSpec(memory_space=pltpu.VMEM)
    return pl.pallas_call(
        body, name="pre_exchange",
        in_specs=[vmem] * 6, out_specs=[vmem] * 5,
        out_shape=[jax.ShapeDtypeStruct((1, 3 * D_MODEL), F32), jax.ShapeDtypeStruct((N_DEV, D_MODEL), F32),
                   jax.ShapeDtypeStruct((1, D_MODEL), F32),
                   jax.ShapeDtypeStruct(w_in.shape[::-1], _BF), jax.ShapeDtypeStruct(w_out.shape, _BF)],
        scratch_shapes=[pltpu.VMEM((N_DEV * 8, D_MODEL), F32), pltpu.VMEM((N_DEV * 8, SHARD_ADA), F32),
                        pltpu.VMEM((N_DEV * 8, SHARD_ADA), F32)] + [pltpu.SemaphoreType.DMA((N_DEV - 1,))] * 4,
        compiler_params=pltpu.CompilerParams(vmem_limit_bytes=VMEM_LIMIT),
    )(c, w_ada, b_ada, logits, w_in, w_out)


def weight_gather(win_sh, wout_sh):
    def body(win_ref, wout_ref, wg_ref, woutg_ref, send, recv, local):
        pos = _mesh_pos()
        x, y, c = pos
        sibling = (x, y, 1 - c)

        def route(core):
            return [(x + (1 - core) * (1 - 2 * x), y + core * (1 - 2 * y)),
                    (x + core * (1 - 2 * x), y + (1 - core) * (1 - 2 * y)),
                    (1 - x, 1 - y)]

        chips, sib_chips = route(c), route(1 - c)
        mine, first, later = [], [], []
        for a, (src, out) in enumerate(((win_ref, wg_ref), (wout_ref, woutg_ref))):
            def copy(k, block, to, src_ref=None, a=a, out=out):
                dst = out.at[_lin(block)]
                return _remote(dst if src_ref is None else src_ref, dst, send.at[7 * a + k], recv.at[7 * a + k], to)
            mine.append(pltpu.make_async_copy(src, out.at[_lin(pos)], local.at[a]))
            first += [copy(0, pos, sibling, src), copy(1, pos, (*chips[0], c), src), copy(2, pos, (*chips[1], c), src)]
            later.append([[copy(3, (*chips[0], c), (*chips[1], c)), copy(4, (*chips[0], c), sibling)],
                          [copy(5, (*chips[1], c), sibling)],
                          [copy(6, (*chips[2], c), sibling)]])
        for cp in mine + first:
            cp.start()
        for j in range(3):
            for a, out in enumerate((wg_ref, woutg_ref)):
                dst = out.at[_lin((*chips[j], c))]
                _remote(dst, dst, send.at[7 * a + 1 + j], recv.at[7 * a + 1 + j], pos).wait_recv()
                for cp in later[a][j]:
                    cp.start()
        for a, out in enumerate((wg_ref, woutg_ref)):
            dst = out.at[_lin(sibling)]
            _remote(dst, dst, send.at[7 * a], recv.at[7 * a], pos).wait_recv()
            for j in range(3):
                dst = out.at[_lin((*sib_chips[j], 1 - c))]
                _remote(dst, dst, send.at[7 * a + 4 + j], recv.at[7 * a + 4 + j], pos).wait_recv()
        for cp in first + [cp for per_array in later for group in per_array for cp in group]:
            cp.wait_send()
        for cp in mine:
            cp.wait()

    any_spec = pl.BlockSpec(memory_space=pl.ANY)
    return pl.pallas_call(
        body, name="weight_gather",
        in_specs=[any_spec, any_spec], out_specs=[any_spec, any_spec],
        out_shape=[jax.ShapeDtypeStruct((N_DEV,) + win_sh.shape, win_sh.dtype),
                   jax.ShapeDtypeStruct((N_DEV,) + wout_sh.shape, wout_sh.dtype)],
        scratch_shapes=[pltpu.SemaphoreType.DMA((14,)), pltpu.SemaphoreType.DMA((14,)),
                        pltpu.SemaphoreType.DMA((2,))],
    )(win_sh, wout_sh)


def grad_pair_exchange(g_sib, g_out):
    def body(gsib_ref, gout_ref, ra_ref, rb_ref, send, recv):
        pos = _mesh_pos()
        x, y, c = pos
        sibling = (x, y, 1 - c)
        copies = []
        for q in range(4):
            copies.append(_remote(gsib_ref.at[q], ra_ref.at[q], send.at[q], recv.at[q], sibling))
            copies.append(_remote(gout_ref.at[2 * q + (1 - c)], rb_ref.at[q], send.at[4 + q], recv.at[4 + q], sibling))
        for cp in copies:
            cp.start()
        for cp in copies:
            cp.wait_recv()
        for cp in copies:
            cp.wait_send()

    any_spec = pl.BlockSpec(memory_space=pl.ANY)
    return pl.pallas_call(
        body, name="grad_pair_exchange",
        in_specs=[any_spec, any_spec], out_specs=[any_spec, any_spec],
        out_shape=[jax.ShapeDtypeStruct(g_sib.shape, g_sib.dtype), jax.ShapeDtypeStruct((4,) + g_out.shape[1:], F32)],
        scratch_shapes=[pltpu.SemaphoreType.DMA((8,)), pltpu.SemaphoreType.DMA((8,))],
    )(g_sib, g_out)


def pair_sum(g_in, ra, g_out, rb, c_idx):
    tr = D_MODEL

    def body(c_ref, gin_ref, ra_ref, gout_ref, rb_ref, sb_ref, sbo_ref):
        del c_ref
        sb_ref[...] = (gin_ref[...] + ra_ref[...].astype(F32)).astype(sb_ref.dtype)
        sbo_ref[...] = gout_ref[...] + rb_ref[...]

    n_i = D_MODEL // tr
    return pl.pallas_call(
        body, name="pair_sum",
        grid_spec=pltpu.PrefetchScalarGridSpec(
            num_scalar_prefetch=1, grid=(4, n_i),
            in_specs=[pl.BlockSpec((None, tr, SHARD_IN), lambda q, i, c: (2 * q + c[0], i, 0)),
                      pl.BlockSpec((None, tr, SHARD_IN), lambda q, i, c: (q, i, 0)),
                      pl.BlockSpec((None, SHARD_OUT // n_i, D_MODEL), lambda q, i, c: (2 * q + c[0], i, 0)),
                      pl.BlockSpec((None, SHARD_OUT // n_i, D_MODEL), lambda q, i, c: (q, i, 0))],
            out_specs=[pl.BlockSpec((None, tr, SHARD_IN), lambda q, i, c: (q, i, 0)),
                       pl.BlockSpec((None, SHARD_OUT // n_i, D_MODEL), lambda q, i, c: (q, i, 0))]),
        out_shape=[jax.ShapeDtypeStruct(ra.shape, _BF), jax.ShapeDtypeStruct(rb.shape, F32)],
        compiler_params=_params(("arbitrary", "arbitrary")),
    )(c_idx, g_in, ra, g_out, rb)


_HBM = pl.BlockSpec(memory_space=pltpu.HBM)
_SEM = pl.BlockSpec(memory_space=pltpu.SEMAPHORE)
_N_CHIP_COPIES = 6


def _chip_copies(sb_ref, sbo_ref, rc_ref, rco_ref, send, recv):
    pos = _mesh_pos()
    copies = []
    for a, (src, dst) in enumerate(((sb_ref, rc_ref), (sbo_ref, rco_ref))):
        for j, chip in enumerate(_other_chips(pos)):
            copies.append(_remote(src.at[2 * chip[0] + chip[1]], dst.at[j], send.at[3 * a + j], recv.at[3 * a + j],
                                  (*chip, pos[2])))
    return copies


def grad_chip_start(sb, sbo):
    def body(sb_ref, sbo_ref, rc_ref, rco_ref, send, recv, sb_thru, sbo_thru, rc_thru, rco_thru, token):
        del sb_thru, sbo_thru, rc_thru, rco_thru
        for cp in _chip_copies(sb_ref, sbo_ref, rc_ref, rco_ref, send, recv):
            cp.start()
        token[...] = jnp.zeros_like(token)

    hbm = lambda a: pltpu.with_memory_space_constraint(a, pltpu.HBM)
    rc = lax.empty((3,) + sb.shape[1:], sb.dtype)
    rco = lax.empty((3,) + sbo.shape[1:], sbo.dtype)
    return pl.pallas_call(
        body, name="grad_chip_start",
        in_specs=[_HBM] * 4,
        out_specs=[_SEM, _SEM, _HBM, _HBM, _HBM, _HBM, pl.BlockSpec(memory_space=pltpu.VMEM)],
        out_shape=[pltpu.SemaphoreType.DMA((_N_CHIP_COPIES,)), pltpu.SemaphoreType.DMA((_N_CHIP_COPIES,)),
                   pltpu.HBM(sb.shape, sb.dtype), pltpu.HBM(sbo.shape, sbo.dtype),
                   pltpu.HBM(rc.shape, rc.dtype), pltpu.HBM(rco.shape, rco.dtype),
                   jax.ShapeDtypeStruct((8, LANE), F32)],
        input_output_aliases={0: 2, 1: 3, 2: 4, 3: 5},
        compiler_params=pltpu.CompilerParams(has_side_effects=pltpu.SideEffectType.DATAFLOW_SIDE_EFFECTING),
    )(hbm(sb), hbm(sbo), hbm(rc), hbm(rco))


def grad_chip_wait(send, recv, sb, sbo, rc, rco, after):
    def body(sb_ref, sbo_ref, rc_ref, rco_ref, send, recv, after_ref, sb_o, sbo_o, rc_o, rco_o):
        del after_ref, sb_o, sbo_o, rc_o, rco_o
        for cp in _chip_copies(sb_ref, sbo_ref, rc_ref, rco_ref, send, recv):
            cp.wait_send()
            cp.wait_recv()

    return pl.pallas_call(
        body, name="grad_chip_wait",
        in_specs=[_HBM] * 4 + [_SEM, _SEM, pl.BlockSpec(memory_space=pl.ANY)],
        out_specs=[_HBM] * 4,
        out_shape=[pltpu.HBM(sb.shape, sb.dtype), pltpu.HBM(sbo.shape, sbo.dtype),
                   pltpu.HBM(rc.shape, rc.dtype), pltpu.HBM(rco.shape, rco.dtype)],
        input_output_aliases={0: 0, 1: 1, 2: 2, 3: 3},
        compiler_params=pltpu.CompilerParams(has_side_effects=pltpu.SideEffectType.DATAFLOW_SIDE_EFFECTING),
    )(sb, sbo, rc, rco, send, recv, after)


def pack_gather(pack):
    def body(pack_ref, packs_ref, psend, precv):
        pos = _mesh_pos()
        me = _lin(pos)
        packs_ref[me] = pack_ref[...]
        peers = [_xor_peer(pos, k) for k in range(1, N_DEV)]
        gather = [_remote(packs_ref.at[me], packs_ref.at[me], psend.at[n], precv.at[n], p) for n, p in enumerate(peers)]
        for cp in gather:
            cp.start()
        for n, p in enumerate(peers):
            _remote(packs_ref.at[_lin(p)], packs_ref.at[_lin(p)], psend.at[n], precv.at[n], p).wait_recv()
        for cp in gather:
            cp.wait_send()

    vmem = pl.BlockSpec(memory_space=pltpu.VMEM)
    return pl.pallas_call(
        body, name="pack_gather", in_specs=[vmem], out_specs=vmem,
        out_shape=jax.ShapeDtypeStruct((N_DEV,) + pack.shape, F32),
        scratch_shapes=[pltpu.SemaphoreType.DMA((N_DEV - 1,)), pltpu.SemaphoreType.DMA((N_DEV - 1,))],
    )(pack)


def pack_rows(vec_mid, vec_ada, dlb):
    def body(mid_ref, ada_ref, dlb_ref, o_ref):
        mid = lambda r: mid_ref[r:r + 1, :]
        rows = [ada_ref[0:1, :], dlb_ref[...], mid(MID_HG_G), mid(MID_RET_G), mid(MID_FINAL_G),
                ada_ref[2:3, :], ada_ref[1:2, :], mid(MID_GATE), mid(MID_LOSS)]
        o_ref[...] = jnp.zeros_like(o_ref)
        for n, row in enumerate(rows):
            o_ref[n:n + 1, :] = row

    vmem = pl.BlockSpec(memory_space=pltpu.VMEM)
    return pl.pallas_call(body, name="pack_rows", in_specs=[vmem] * 3, out_specs=vmem,
                          out_shape=jax.ShapeDtypeStruct((PACK_ROWS, D_MODEL), F32))(vec_mid, vec_ada, dlb)


def _adamw(w, g, m, v):
    m = ADAM_B1 * m + (1.0 - ADAM_B1) * g
    v = ADAM_B2 * v + (1.0 - ADAM_B2) * (g * g)
    m_hat = m / (1.0 - ADAM_B1 ** ADAM_STEP)
    v_hat = v / (1.0 - ADAM_B2 ** ADAM_STEP)
    delta = -ADAM_LR * (m_hat / (jnp.sqrt(v_hat) + ADAM_EPS) + ADAM_WD * w)
    return delta, m, v


def adam_shard(chip_idx, own, parts, w, m, v, name):
    rows, cols = w.shape
    tr = min(rows, 256)

    def body(chip_ref, p0, p1, p2, p3, w_ref, m_ref, v_ref, g_ref, d_ref, nm_ref, nv_ref):
        del chip_ref
        g = ((p0[...].astype(F32) + p1[...].astype(F32)) + p2[...].astype(F32)) + p3[...].astype(F32)
        g_ref[...] = g
        d_ref[...], nm_ref[...], nv_ref[...] = _adamw(w_ref[...], g, m_ref[...], v_ref[...])

    part = lambda q: pl.BlockSpec((None, tr, cols), lambda i, chip, q=q: (q, i, 0))
    tile = pl.BlockSpec((tr, cols), lambda i, chip: (i, 0))
    return pl.pallas_call(
        body, name=name,
        grid_spec=pltpu.PrefetchScalarGridSpec(
            num_scalar_prefetch=1, grid=(rows // tr,),
            in_specs=[pl.BlockSpec((None, tr, cols), lambda i, chip: (chip[0], i, 0)), part(0), part(1), part(2),
                      tile, tile, tile],
            out_specs=[tile] * 4),
        out_shape=[jax.ShapeDtypeStruct(w.shape, F32)] * 4,
        compiler_params=_params(("arbitrary",)),
    )(chip_idx, own, parts, parts, parts, w, m, v)


def adam_ada(sc_t, dmod_all, me_idx, w, m, v):
    def body(me_ref, sc_ref, dm_ref, w_ref, m_ref, v_ref, g_ref, d_ref, nm_ref, nv_ref):
        del me_ref
        g = _dot_f32(sc_ref[...], dm_ref[...])
        g_ref[...] = g
        d_ref[...], nm_ref[...], nv_ref[...] = _adamw(w_ref[...], g, m_ref[...], v_ref[...])

    full = pl.BlockSpec(w.shape, lambda i, me: (0, 0))
    return pl.pallas_call(
        body, name="adam_ada",
        grid_spec=pltpu.PrefetchScalarGridSpec(
            num_scalar_prefetch=1, grid=(1,),
            in_specs=[pl.BlockSpec(sc_t.shape, lambda i, me: (0, 0)),
                      pl.BlockSpec((LANE, SHARD_ADA), lambda i, me: (0, me[0])), full, full, full],
            out_specs=[full] * 4),
        out_shape=[jax.ShapeDtypeStruct(w.shape, F32)] * 4,
        compiler_params=_params(("arbitrary",)),
    )(me_idx, sc_t, dmod_all, w, m, v)


def adam_vectors(packs, lb, params, ms, vs):
    n = len(params)

    def body(*refs):
        packs_ref, lb_ref = refs[0], refs[1]
        w_refs, m_refs, v_refs = refs[2:2 + n], refs[2 + n:2 + 2 * n], refs[2 + 2 * n:2 + 3 * n]
        loss_ref = refs[2 + 3 * n]
        outs = refs[3 + 3 * n:3 + 7 * n]
        tot_ref = refs[3 + 7 * n]
        tot = packs_ref[0]
        for d in range(1, N_DEV):
            tot = tot + packs_ref[d]
        tot_ref[...] = tot
        row = lambda r: tot_ref[r:r + 1, :]
        lbv = lb_ref[...]
        dl0 = row(ROW_LB) * lbv * (1.0 - lbv)
        grads = [[row(ROW_NORM_G)],
                 [jnp.concatenate([row(ROW_SHIFT), row(ROW_SCALE), row(ROW_GATE)], axis=1)],
                 [dl0, -dl0],
                 [row(ROW_HG_G)], [row(ROW_RET_G)], [row(ROW_FINAL_G)]]
        loss_ref[...] = tot_ref[ROW_LOSS:ROW_LOSS + 1, 0:LANE]
        for j, g_rows in enumerate(grads):
            for r, g in enumerate(g_rows):
                rs = slice(r, r + 1)
                d, nm, nv = _adamw(w_refs[j][rs, :], g, m_refs[j][rs, :], v_refs[j][rs, :])
                outs[4 * j][rs, :] = g
                outs[4 * j + 1][rs, :] = d
                outs[4 * j + 2][rs, :] = nm
                outs[4 * j + 3][rs, :] = nv

    vmem = pl.BlockSpec(memory_space=pltpu.VMEM)
    out_shape = [jax.ShapeDtypeStruct((1, LANE), F32)]
    for w in params:
        out_shape += [jax.ShapeDtypeStruct(w.shape, F32)] * 4
    return pl.pallas_call(
        body, name="adam_vectors", in_specs=[vmem] * (2 + 3 * n), out_specs=[vmem] * len(out_shape),
        out_shape=out_shape, scratch_shapes=[pltpu.VMEM((PACK_ROWS, D_MODEL), F32)],
    )(packs, lb, *params, *ms, *vs)


def kernel(x, c, norm_g, w_ada, b_ada, w_in, hg_lb_logits, hg_norm_g, ret_norm_g, w_out, final_g, loss_target, m_norm_g, m_w_ada, m_b_ada, m_w_in, m_hg_lb_logits, m_hg_norm_g, m_ret_norm_g, m_w_out, m_final_g, v_norm_g, v_w_ada, v_b_ada, v_w_in, v_hg_lb_logits, v_hg_norm_g, v_ret_norm_g, v_w_out, v_final_g):
    pos = _mesh_pos()
    me_idx = jnp.reshape(_lin(pos), (1,)).astype(jnp.int32)
    c_idx = jnp.reshape(pos[2], (1,)).astype(jnp.int32)
    vec = lambda a: a.reshape(1, D_MODEL)

    mod, scall, lb, wt_sh, wout_sh = pre_exchange(c, w_ada[0], b_ada, hg_lb_logits, w_in[0], w_out[0])
    chip_idx = jnp.reshape(2 * pos[0] + pos[1], (1,)).astype(jnp.int32)

    def project(h):
        wtg, woutg = weight_gather(wt_sh, wout_sh)
        wt = wtg.reshape(D_IN, D_MODEL)
        return proj_forward(h, wt, h.shape[0]), wt, woutg.reshape(D_MODEL, D_MODEL)

    def start_exchange(dwin, dwin_sib, dwout):
        dwout = dwout.reshape(N_DEV, SHARD_OUT, D_MODEL)
        ra, rb = grad_pair_exchange(dwin_sib, dwout)
        sb, sbo = pair_sum(dwin, ra, dwout, rb, c_idx)
        send, recv, sb, sbo, rc, rco, token = grad_chip_start(sb, sbo)
        return token, (send, recv, sb, sbo, rc, rco)

    grad_x, _, _, vec_mid, vec_ada, dlb, pending = device_step(
        x[0], loss_target[0], mod, lb, project, norm_g, hg_norm_g, ret_norm_g, vec(final_g), c_idx, start_exchange)
    packs = pack_gather(pack_rows(vec_mid, vec_ada, dlb))
    dmod_all = packs[:, ROW_SHIFT:ROW_GATE + 1, :].reshape(N_DEV, 3 * D_MODEL)
    dmod_all = jnp.pad(dmod_all, ((0, LANE - N_DEV), (0, 0)))
    sc_t = jnp.pad(scall.T, ((0, 0), (0, LANE - N_DEV)))
    g_ada, d_ada, nm_ada, nv_ada = adam_ada(sc_t, dmod_all, me_idx, w_ada[0], m_w_ada[0], v_w_ada[0])
    small = adam_vectors(
        packs, lb,
        (norm_g, b_ada, hg_lb_logits, hg_norm_g, ret_norm_g, vec(final_g)),
        (m_norm_g, m_b_ada, m_hg_lb_logits, m_hg_norm_g, m_ret_norm_g, vec(m_final_g)),
        (v_norm_g, v_b_ada, v_hg_lb_logits, v_hg_norm_g, v_ret_norm_g, vec(v_final_g)))
    loss = small[0][0, 0]
    sb, sbo, rc, rco = grad_chip_wait(*pending, small[0])
    g_in, d_in, nm_in, nv_in = adam_shard(chip_idx, sb, rc, w_in[0], m_w_in[0], v_w_in[0], "adam_w_in")
    g_out, d_out, nm_out, nv_out = adam_shard(chip_idx, sbo, rco, w_out[0], m_w_out[0], v_w_out[0], "adam_w_out")
    (g_ng, d_ng, nm_ng, nv_ng), (g_b, d_b, nm_b, nv_b), (g_lb, d_lb, nm_lb, nv_lb), (g_hg, d_hg, nm_hg, nv_hg), \
        (g_rg, d_rg, nm_rg, nv_rg), (g_fg, d_fg, nm_fg, nv_fg) = [small[1 + 4 * j:5 + 4 * j] for j in range(6)]
    flat = lambda a: a.reshape(D_MODEL)

    def group(ng, ada, b, win, lbl, hg, rg, wo, fg):
        return (ng, ada[None], b, win[None], lbl, hg, rg, wo[None], flat(fg))

    return (loss, grad_x[None],
            *group(g_ng, g_ada, g_b, g_in, g_lb, g_hg, g_rg, g_out, g_fg),
            *group(d_ng, d_ada, d_b, d_in, d_lb, d_hg, d_rg, d_out, d_fg),
            *group(nm_ng, nm_ada, nm_b, nm_in, nm_lb, nm_hg, nm_rg, nm_out, nm_fg),
            *group(nv_ng, nv_ada, nv_b, nv_in, nv_lb, nv_hg, nv_rg, nv_out, nv_fg))
```

```python
import functools

import numpy as np
import jax
import jax.numpy as jnp
from jax import lax
from jax.experimental import pallas as pl
from jax.experimental.pallas import tpu as pltpu

F32 = jnp.float32
_BF = jnp.bfloat16

D_MODEL = 1024
N_HEADS = 8
LANE = 128
RET_DK = 64
D_IN = 9216
N_DEV = 8
SHARD_IN = D_IN // N_DEV
SHARD_ADA = 3 * D_MODEL // N_DEV
SHARD_OUT = D_MODEL // N_DEV
N_CB = D_IN // LANE
CB_PER_SHARD = SHARD_IN // LANE
CHUNK = 128
N_LEVELS = 7
EPS = 1e-6
LOG2_E = float(np.log2(np.e))
ROPE_BASE = 10000.0
CB_HQ, CB_HF, CB_HI, CB_HZ, CB_RQ, CB_RK, CB_RV, CB_RZ, CB_GA, CB_GB = 0, 8, 16, 24, 32, 36, 40, 48, 56, 64
VMEM_LIMIT = 56 * 1024 * 1024

ADAM_LR, ADAM_B1, ADAM_B2, ADAM_EPS, ADAM_WD, ADAM_STEP = 0.001, 0.9, 0.999, 1e-08, 0.01, 10

_NN = (((1,), (0,)), ((), ()))
_NT = (((1,), (1,)), ((), ()))
_TN = (((0,), (0,)), ((), ()))
MESH = pl.DeviceIdType.MESH


def _dot(a, b, dims=_NN):
    return lax.dot_general(a.astype(_BF), b.astype(_BF), dims, preferred_element_type=F32)


def _split2(a):
    hi = a.astype(_BF)
    lo = (a - hi.astype(F32)).astype(_BF)
    return jnp.concatenate([hi, lo], axis=1)


def _dot_sel(sel, a):
    n = a.shape[1]
    r = lax.dot_general(sel.astype(_BF), _split2(a), _NN, preferred_element_type=F32)
    return r[:, :n] + r[:, n:]


def _dot_f32(a, b):
    def pieces(v):
        p1 = v.astype(_BF)
        r1 = v - p1.astype(F32)
        p2 = r1.astype(_BF)
        p3 = (r1 - p2.astype(F32)).astype(_BF)
        return p1, p2, p3
    a1, a2, a3 = pieces(a)
    b1, b2, b3 = pieces(b)
    d = lambda u, v: lax.dot_general(u, v, _NN, preferred_element_type=F32)
    return ((d(a1, b3) + d(a2, b2) + d(a3, b1)) + (d(a1, b2) + d(a2, b1))) + d(a1, b1)


def _sigmoid(v):
    return 1.0 / (1.0 + jnp.exp(-v))


def _params(sem=None):
    return pltpu.CompilerParams(dimension_semantics=sem, vmem_limit_bytes=VMEM_LIMIT)


def _hgrn_consts():
    c, nl = CHUNK, N_LEVELS
    t = np.arange(c)[:, None]
    j = np.arange(c)[None, :]
    sel = [j <= t]
    masks = [j == t]
    for l in range(1, nl + 1):
        m = ((t >> l) << l) + (1 << (l - 1)) - 1
        sec = t > m
        sel.append(np.where(sec, (j > m) & (j <= t), (j > t) & (j <= m)))
        same = (t >> l) == (j >> l)
        masks.append(same & sec & (j <= m))
    sel.append(j > t)
    sel = np.concatenate(sel, 0).astype(np.float32)
    masks = np.stack(masks).astype(np.float32)
    sgn = np.stack([np.where((t & (1 << (l - 1))) != 0, 1.0, -1.0) * np.ones((1, LANE)) for l in range(3, nl + 1)])
    return dict(tri=jnp.asarray(sel[:c], _BF),
                lvl=jnp.asarray(masks, F32),
                sgn=jnp.asarray(sgn, F32),
                sel_t=jnp.asarray(sel.T, _BF),
                lvl_b=jnp.asarray(masks, _BF),
                lvlt_b=jnp.asarray(np.swapaxes(masks, 1, 2), _BF))


def _level_exponents(b, logf, b_scr, sgn_ref):
    c = CHUNK
    b_scr[...] = b
    row = lax.broadcasted_iota(jnp.int32, (c, LANE), 0)
    nxt = pltpu.roll(logf, c - 1, 0)
    prv = pltpu.roll(logf, 1, 0)
    r4 = row & 3
    out = [jnp.where((row & 1) == 1, logf, 0.0),
           jnp.where(r4 == 0, nxt, jnp.where(r4 == 1, 0.0, jnp.where(r4 == 2, logf, logf + prv)))]
    for l in range(3, N_LEVELS + 1):
        size, half = 1 << l, 1 << (l - 1)
        ref = jnp.concatenate([jnp.broadcast_to(b_scr[i * size + half - 1:i * size + half, :], (size, LANE))
                               for i in range(c // size)], axis=0)
        out.append((b - ref) * sgn_ref[l - 3])
    return out


def _hgrn_chunk(hq, hf, hi, lbv, tri_ref, sgn_ref, b_scr):
    sq = _sigmoid(hq)
    q = hq * sq
    sg = _sigmoid(hf)
    omlb = 1.0 - lbv
    f = lbv + omlb * sg
    k = 1.0 - f
    logf = jnp.log(f) * LOG2_E
    b = _dot_sel(tri_ref[...], logf)
    bc = jnp.sum(logf, axis=0, keepdims=True)
    lev = [None] + [jnp.exp2(e) for e in _level_exponents(b, logf, b_scr, sgn_ref)]
    return dict(sq=sq, q=q, sg=sg, omlb=omlb, f=f, k=k, v=hi, eb=jnp.exp2(b), erem=jnp.exp2(bc - b),
                ebc=jnp.exp2(bc), lev=lev)


def _blockdiag(a, b):
    z = jnp.zeros_like(a)
    return jnp.concatenate([jnp.concatenate([a, z], axis=1), jnp.concatenate([z, b], axis=1)], axis=0)


def _level_operands(a):
    q, k = a["q"].astype(_BF), a["k"].astype(_BF)
    lev = [None] + [a["lev"][l].astype(_BF) for l in range(1, N_LEVELS + 1)]
    ql = [q] + [q * lev[l] for l in range(1, N_LEVELS + 1)]
    kl = [k] + [k * lev[l] for l in range(1, N_LEVELS + 1)]
    pairs = range(0, N_LEVELS + 1, 2)
    return ([jnp.concatenate([ql[l], ql[l + 1]], axis=1) for l in pairs], [_blockdiag(kl[l], kl[l + 1]) for l in pairs],
            ql, kl)


def _hgrn_scores(a, lvl_ref, q_pairs, k_diags):
    acc = None
    for n, (qp, kd) in enumerate(zip(q_pairs, k_diags)):
        both = lax.dot_general(qp, kd, _NT, preferred_element_type=F32)
        part = lvl_ref[2 * n] * both[:, :CHUNK] + lvl_ref[2 * n + 1] * both[:, CHUNK:]
        acc = part if acc is None else acc + part
    return acc


SCAN_UNROLL = 2
RET_UNROLL = 8


def _writeback_reserve(step, make_copies):
    slot = step % 2

    @pl.when(step >= 2)
    def _():
        for cp in make_copies(slot):
            cp.wait()

    return slot


def _writeback_commit(step, n_steps, slot, make_copies):
    for cp in make_copies(slot):
        cp.start()

    @pl.when(step == n_steps - 1)
    def _():
        for cp in make_copies(slot):
            cp.wait()
        if n_steps > 1:
            for cp in make_copies(1 - slot):
                cp.wait()


def _resident(const):
    zeros = (0,) * const.ndim
    return pl.BlockSpec(const.shape, lambda p, t: zeros)


def _time_block(t_len):
    return min(t_len, 2048)


def hgrn_forward(pb, lb, t_len):
    nc = t_len // CHUNK
    tb = _time_block(t_len)
    ncb = tb // CHUNK
    consts = _hgrn_consts()
    operands = [consts[n] for n in ("tri", "lvl", "sgn")]

    def body(hq_ref, hf_ref, hi_ref, lb_ref, tri_ref, lvl_ref, sgn_ref, o_ref, ssave_ref, asave_ref, st_ref, b_scr):
        @pl.when(pl.program_id(1) == 0)
        def _():
            st_ref[...] = jnp.zeros_like(st_ref)

        def chunk(ci, carry):
            r = pl.ds(pl.multiple_of(ci * CHUNK, CHUNK), CHUNK)
            for hd in range(2):
                lbv = lb_ref[:, hd * LANE:(hd + 1) * LANE]
                a = _hgrn_chunk(hq_ref[hd, r, :], hf_ref[hd, r, :], hi_ref[hd, r, :], lbv, tri_ref, sgn_ref,
                                b_scr.at[hd])
                q_pairs, k_diags, _, _ = _level_operands(a)
                st = st_ref[hd]
                ssave_ref[hd, ci] = st
                scores = _hgrn_scores(a, lvl_ref, q_pairs, k_diags).astype(asave_ref.dtype)
                asave_ref[hd, ci] = scores
                o_ref[hd, r, :] = _dot(a["q"] * a["eb"], st, _NT) + _dot(scores, a["v"])
                st_ref[hd] = st * a["ebc"] + _dot(a["v"], a["k"] * a["erem"], _TN)
            return carry

        lax.fori_loop(0, ncb, chunk, 0, unroll=SCAN_UNROLL)

    pair = lambda base: pl.BlockSpec((2, tb, LANE), lambda p, t, base=base: (base // 2 + p, t, 0))
    per_chunk = pl.BlockSpec((2, ncb, LANE, LANE), lambda p, t: (p, t, 0, 0))
    return pl.pallas_call(
        body, name="hgrn_fwd", grid=(N_HEADS // 2, t_len // tb),
        in_specs=[pair(CB_HQ), pair(CB_HF), pair(CB_HI),
                  pl.BlockSpec((1, 2 * LANE), lambda p, t: (0, p))] + [_resident(c) for c in operands],
        out_specs=[pl.BlockSpec((2, tb, LANE), lambda p, t: (p, t, 0)), per_chunk, per_chunk],
        out_shape=[jax.ShapeDtypeStruct((N_HEADS, t_len, LANE), F32),
                   jax.ShapeDtypeStruct((N_HEADS, nc, LANE, LANE), F32),
                   jax.ShapeDtypeStruct((N_HEADS, nc, CHUNK, CHUNK), _BF)],
        scratch_shapes=[pltpu.VMEM((2, LANE, LANE), F32), pltpu.VMEM((2, CHUNK, LANE), F32)],
        compiler_params=_params(("arbitrary", "arbitrary")),
    )(pb, pb, pb, lb, *operands)


def hgrn_backward(pb, lb, do, ssave, asave, dpb, t_len):
    tb = _time_block(t_len)
    ncb, ntb = tb // CHUNK, t_len // tb
    consts = _hgrn_consts()
    operands = [consts[n] for n in ("tri", "sgn", "sel_t", "lvl_b", "lvlt_b")]

    def body(hq_ref, hf_ref, hi_ref, lb_ref, do_ref, ssave_ref, asave_ref, tri_ref, sgn_ref, selt_ref, lvlb_ref,
             lvltb_ref, dpb_in, dpb_ref, dlb_ref, dq_buf, df_buf, di_buf, dst_ref, b_scr, sems):
        del dpb_in
        p, t = pl.program_id(0), pl.program_id(1)
        step = p * ntb + t
        rows = pl.ds(pl.multiple_of((ntb - 1 - t) * tb, tb), tb)

        def out_copies(sl):
            return [pltpu.make_async_copy(buf.at[sl], dpb_ref.at[pl.ds(base + 2 * p, 2), rows], sems.at[sl, n])
                    for n, (buf, base) in enumerate(((dq_buf, CB_HQ), (df_buf, CB_HF), (di_buf, CB_HI)))]

        slot = _writeback_reserve(step, out_copies)

        @pl.when(t == 0)
        def _():
            dst_ref[...] = jnp.zeros_like(dst_ref)
            dlb_ref[...] = jnp.zeros_like(dlb_ref)

        def chunk(i, carry):
            ci = ncb - 1 - i
            r = pl.ds(pl.multiple_of(ci * CHUNK, CHUNK), CHUNK)
            for hd in range(2):
                head_chunk(hd, ci, r)
            return carry

        def head_chunk(hd, ci, r):
            lbv = lb_ref[:, hd * LANE:(hd + 1) * LANE]
            hq = hq_ref[hd, r, :]
            a = _hgrn_chunk(hq, hf_ref[hd, r, :], hi_ref[hd, r, :], lbv, tri_ref, sgn_ref, b_scr.at[hd])
            _, k_diags, ql, kl = _level_operands(a)
            q, k, v = a["q"], a["k"], a["v"]
            g = do_ref[hd, r, :]
            st0 = ssave_ref[hd, ci]
            dst = dst_ref[hd]
            scores = asave_ref[hd, ci]
            da = _dot(g, v, _NT)
            da_t = _dot(v, g, _NT)
            kb = k * a["erem"]
            qb = q * a["eb"]
            dv = _dot(scores, g, _TN) + _dot(kb, dst, _NT)
            dq_inter = _dot(g, st0) * a["eb"]
            dk_state = _dot(v, dst) * a["erem"]
            dq, dk = dq_inter, dk_state
            de = [q * dq_inter]
            da_b, dat_b = da.astype(_BF), da_t.astype(_BF)
            for n in range(len(k_diags)):
                l0, l1 = 2 * n, 2 * n + 1
                da_pair = jnp.concatenate([lvlb_ref[l0] * da_b, lvlb_ref[l1] * da_b], axis=1)
                dat_pair = jnp.concatenate([lvltb_ref[l0] * dat_b, lvltb_ref[l1] * dat_b], axis=1)
                dq_both = lax.dot_general(da_pair, k_diags[n], _NN, preferred_element_type=F32)
                dk_both = lax.dot_general(dat_pair, _blockdiag(ql[l0], ql[l1]), _NN, preferred_element_type=F32)
                for l, cols in ((l0, slice(0, LANE)), (l1, slice(LANE, 2 * LANE))):
                    dql, dkl = dq_both[:, cols], dk_both[:, cols]
                    if l > 0:
                        e = a["lev"][l]
                        dql, dkl = dql * e, dkl * e
                        de.append(q * dql + k * dkl)
                    dq = dq + dql
                    dk = dk + dkl
            de.append(k * dk_state)
            dst_ref[hd] = dst * a["ebc"] + _dot(g, qb, _TN)
            dbc = jnp.sum(dst * st0, axis=0, keepdims=True) * a["ebc"]
            de2 = lax.dot_general(selt_ref[...], _split2(jnp.concatenate(de, axis=0)), _NN,
                                  preferred_element_type=F32)
            dlogf = de2[:, :LANE] + de2[:, LANE:] + dbc
            sq, sg = a["sq"], a["sg"]
            df = dlogf / a["f"] - dk
            dq_buf[slot, hd, r, :] = (dq * (sq * (1.0 + hq * (1.0 - sq)))).astype(dq_buf.dtype)
            df_buf[slot, hd, r, :] = (df * a["omlb"] * sg * (1.0 - sg)).astype(df_buf.dtype)
            di_buf[slot, hd, r, :] = dv.astype(di_buf.dtype)
            cols = slice(hd * LANE, (hd + 1) * LANE)
            dlb_ref[:, cols] = dlb_ref[:, cols] + jnp.sum(df * (1.0 - sg), axis=0, keepdims=True)

        lax.fori_loop(0, ncb, chunk, 0, unroll=SCAN_UNROLL)
        _writeback_commit(step, (N_HEADS // 2) * ntb, slot, out_copies)

    pair = lambda base: pl.BlockSpec((2, tb, LANE), lambda p, t, base=base: (base // 2 + p, ntb - 1 - t, 0))
    any_spec = pl.BlockSpec(memory_space=pl.ANY)
    per_chunk = pl.BlockSpec((2, ncb, LANE, LANE), lambda p, t: (p, ntb - 1 - t, 0, 0))
    return pl.pallas_call(
        body, name="hgrn_bwd", grid=(N_HEADS // 2, ntb),
        in_specs=[pair(CB_HQ), pair(CB_HF), pair(CB_HI),
                  pl.BlockSpec((1, 2 * LANE), lambda p, t: (0, p)),
                  pair(0), per_chunk, per_chunk]
        + [_resident(c) for c in operands] + [any_spec],
        out_specs=[any_spec, pl.BlockSpec((1, 2 * LANE), lambda p, t: (0, p))],
        out_shape=[jax.ShapeDtypeStruct(dpb.shape, dpb.dtype), jax.ShapeDtypeStruct((1, D_MODEL), F32)],
        scratch_shapes=[pltpu.VMEM((2, 2, tb, LANE), dpb.dtype)] * 3 + [
            pltpu.VMEM((2, LANE, LANE), F32), pltpu.VMEM((2, CHUNK, LANE), F32), pltpu.SemaphoreType.DMA((2, 3))],
        input_output_aliases={7 + len(operands): 0},
        compiler_params=_params(("arbitrary", "arbitrary")),
    )(pb, pb, pb, lb, do, ssave, asave, *operands, dpb)


def _rope_tables(t_len):
    half = RET_DK // 2
    inv_freq = (1.0 / (np.float32(ROPE_BASE) ** np.linspace(0.0, 1.0, half, dtype=np.float32))).astype(np.float32)
    ang = (np.arange(t_len, dtype=np.float32)[:, None] * inv_freq[None, :]).astype(np.float64)
    cos, sin = np.cos(ang).astype(np.float32), np.sin(ang).astype(np.float32)
    cos_t = np.concatenate([cos, cos, cos, cos], axis=1)
    sin_t = np.concatenate([-sin, sin, -sin, sin], axis=1)
    return jnp.asarray(cos_t), jnp.asarray(sin_t)


def _swap_halves(v):
    half = RET_DK // 2
    lane = lax.broadcasted_iota(jnp.int32, v.shape, 1)
    first = (lane & (RET_DK - 1)) < half
    return jnp.where(first, pltpu.roll(v, LANE - half, 1), pltpu.roll(v, half, 1))


def _ret_head_consts(hidx):
    c = CHUNK
    hf = jnp.full((1, LANE), hidx, jnp.int32).astype(F32)
    lg = jnp.log(1.0 - jnp.exp(-(5.0 + hf) * np.float32(np.log(2.0))))
    row = lax.broadcasted_iota(jnp.int32, (c, c), 0)
    col = lax.broadcasted_iota(jnp.int32, (c, c), 1)
    rel = (row - col).astype(F32)
    dm = jnp.where(rel >= 0, jnp.exp(lg[:, :1] * jnp.maximum(rel, 0.0)), 0.0)
    dm_t = jnp.where(rel <= 0, jnp.exp(lg[:, :1] * jnp.maximum(-rel, 0.0)), 0.0)
    idx = lax.broadcasted_iota(jnp.int32, (c, LANE), 0).astype(F32)
    zeta = jnp.exp(lg * (c - 1.0 - idx))
    xi = jnp.exp(lg * (idx + 1.0))
    cdec = jnp.exp(lg * float(c))
    return dm, zeta, xi, cdec, dm_t


def _lane_mask(which):
    lane = lax.broadcasted_iota(jnp.int32, (1, LANE), 1)
    return ((lane // RET_DK) == which).astype(F32)


def retention_forward(pb, cos_t, sin_t, t_len):
    nc = t_len // CHUNK

    tb = _time_block(t_len)
    ncb = tb // CHUNK

    def body(rq_ref, rk_ref, rv_ref, cos_ref, sin_ref, o_ref, rsave_ref, st_ref):
        p = pl.program_id(0)

        @pl.when(pl.program_id(1) == 0)
        def _():
            st_ref[...] = jnp.zeros_like(st_ref)

        consts = [_ret_head_consts(2 * p + hd) for hd in range(2)]

        def chunk(ci, carry):
            r = pl.ds(pl.multiple_of(ci * CHUNK, CHUNK), CHUNK)
            cs, sn = cos_ref[r, :], sin_ref[r, :]
            q = rq_ref[r, :]
            k = rk_ref[r, :]
            q = q * cs + _swap_halves(q) * sn
            k = (k * cs + _swap_halves(k) * sn) * RET_DK ** -0.5
            for hd in range(2):
                dm, zeta, xi, cdec, _ = consts[hd]
                lm = _lane_mask(hd)
                qh, kh = q * lm, k * lm
                v = rv_ref[hd, r, :]
                st = st_ref[hd]
                rsave_ref[hd, ci] = st
                scores = _dot(qh, kh, _NT) * dm
                o_ref[hd, r, :] = _dot(scores, v) + _dot(qh * xi, st, _NT)
                st_ref[hd] = st * cdec + _dot(v, kh * zeta, _TN)
            return carry

        lax.fori_loop(0, ncb, chunk, 0, unroll=RET_UNROLL)

    return pl.pallas_call(
        body, name="ret_fwd", grid=(N_HEADS // 2, t_len // tb),
        in_specs=[pl.BlockSpec((None, tb, LANE), lambda p, t: (CB_RQ + p, t, 0)),
                  pl.BlockSpec((None, tb, LANE), lambda p, t: (CB_RK + p, t, 0)),
                  pl.BlockSpec((2, tb, LANE), lambda p, t: (CB_RV // 2 + p, t, 0)),
                  pl.BlockSpec((tb, LANE), lambda p, t: (t, 0)),
                  pl.BlockSpec((tb, LANE), lambda p, t: (t, 0))],
        out_specs=[pl.BlockSpec((2, tb, LANE), lambda p, t: (p, t, 0)),
                   pl.BlockSpec((2, ncb, LANE, LANE), lambda p, t: (p, t, 0, 0))],
        out_shape=[jax.ShapeDtypeStruct((N_HEADS, t_len, LANE), F32),
                   jax.ShapeDtypeStruct((N_HEADS, nc, LANE, LANE), F32)],
        scratch_shapes=[pltpu.VMEM((2, LANE, LANE), F32)],
        compiler_params=_params(("arbitrary", "arbitrary")),
    )(pb, pb, pb, cos_t, sin_t)


def retention_backward(pb, cos_t, sin_t, do, rsave, dpb, t_len):
    tb = _time_block(t_len)
    ncb, ntb = tb // CHUNK, t_len // tb

    def body(rq_ref, rk_ref, rv_ref, cos_ref, sin_ref, do_ref, rsave_ref, dpb_in,
             dpb_ref, dq_buf, dk_buf, dv_buf, dst_ref, sems):
        del dpb_in
        p, t = pl.program_id(0), pl.program_id(1)
        step = p * ntb + t
        rows = pl.ds(pl.multiple_of((ntb - 1 - t) * tb, tb), tb)

        def out_copies(sl):
            return [pltpu.make_async_copy(dq_buf.at[sl], dpb_ref.at[CB_RQ + p, rows], sems.at[sl, 0]),
                    pltpu.make_async_copy(dk_buf.at[sl], dpb_ref.at[CB_RK + p, rows], sems.at[sl, 1]),
                    pltpu.make_async_copy(dv_buf.at[sl], dpb_ref.at[pl.ds(CB_RV + 2 * p, 2), rows], sems.at[sl, 2])]

        slot = _writeback_reserve(step, out_copies)

        @pl.when(t == 0)
        def _():
            dst_ref[...] = jnp.zeros_like(dst_ref)

        consts = [_ret_head_consts(2 * p + hd) for hd in range(2)]

        def chunk(i, carry):
            ci = ncb - 1 - i
            r = pl.ds(pl.multiple_of(ci * CHUNK, CHUNK), CHUNK)
            cs, sn = cos_ref[r, :], sin_ref[r, :]
            q = rq_ref[r, :]
            k = rk_ref[r, :]
            q = q * cs + _swap_halves(q) * sn
            k = (k * cs + _swap_halves(k) * sn) * RET_DK ** -0.5
            dq, dk = None, None
            for hd in range(2):
                dm, zeta, xi, cdec, dm_t = consts[hd]
                lm = _lane_mask(hd)
                qh, kh = q * lm, k * lm
                v = rv_ref[hd, r, :]
                g = do_ref[hd, r, :]
                st0 = rsave_ref[hd, ci]
                dst = dst_ref[hd]
                scores_t = _dot(kh, qh, _NT) * dm_t
                dsc = _dot(g, v, _NT) * dm
                dsc_t = _dot(v, g, _NT) * dm_t
                dqh = _dot(dsc, kh) + _dot(g, st0) * xi
                dkh = _dot(dsc_t, qh) + _dot(v, dst) * zeta
                dv_buf[slot, hd, r, :] = (_dot(scores_t, g) + _dot(kh * zeta, dst, _NT)).astype(dv_buf.dtype)
                dst_ref[hd] = dst * cdec + _dot(g, qh * xi, _TN)
                dq = dqh if dq is None else dq + dqh
                dk = dkh if dk is None else dk + dkh
            dk = dk * (RET_DK ** -0.5)
            dq_buf[slot, r, :] = (dq * cs - _swap_halves(dq) * sn).astype(dq_buf.dtype)
            dk_buf[slot, r, :] = (dk * cs - _swap_halves(dk) * sn).astype(dk_buf.dtype)
            return carry

        lax.fori_loop(0, ncb, chunk, 0, unroll=RET_UNROLL)
        _writeback_commit(step, (N_HEADS // 2) * ntb, slot, out_copies)

    any_spec = pl.BlockSpec(memory_space=pl.ANY)
    return pl.pallas_call(
        body, name="ret_bwd", grid=(N_HEADS // 2, ntb),
        in_specs=[pl.BlockSpec((None, tb, LANE), lambda p, t: (CB_RQ + p, ntb - 1 - t, 0)),
                  pl.BlockSpec((None, tb, LANE), lambda p, t: (CB_RK + p, ntb - 1 - t, 0)),
                  pl.BlockSpec((2, tb, LANE), lambda p, t: (CB_RV // 2 + p, ntb - 1 - t, 0)),
                  pl.BlockSpec((tb, LANE), lambda p, t: (ntb - 1 - t, 0)),
                  pl.BlockSpec((tb, LANE), lambda p, t: (ntb - 1 - t, 0)),
                  pl.BlockSpec((2, tb, LANE), lambda p, t: (p, ntb - 1 - t, 0)),
                  pl.BlockSpec((2, ncb, LANE, LANE), lambda p, t: (p, ntb - 1 - t, 0, 0)),
                  any_spec],
        out_specs=any_spec,
        out_shape=jax.ShapeDtypeStruct(dpb.shape, dpb.dtype),
        scratch_shapes=[pltpu.VMEM((2, tb, LANE), dpb.dtype), pltpu.VMEM((2, tb, LANE), dpb.dtype),
                        pltpu.VMEM((2, 2, tb, LANE), dpb.dtype), pltpu.VMEM((2, LANE, LANE), F32),
                        pltpu.SemaphoreType.DMA((2, 3))],
        input_output_aliases={7: 0},
        compiler_params=_params(("arbitrary", "arbitrary")),
    )(pb, pb, pb, cos_t, sin_t, do, rsave, dpb)


def _row_tile(t_len, want):
    return min(want, t_len)


PAIR_CB = 2 * CB_PER_SHARD


def proj_forward(h, wt, t_len):
    tm = _row_tile(t_len, 1024)

    def body(h_ref, w_ref, o_ref):
        acc = _dot(h_ref[...], w_ref[...], _NT)
        for jj in range(PAIR_CB):
            o_ref[jj] = acc[:, jj * LANE:(jj + 1) * LANE]

    return pl.pallas_call(
        body, name="proj_fwd", grid=(N_DEV // 2, t_len // tm),
        in_specs=[pl.BlockSpec((tm, D_MODEL), lambda j, i: (i, 0)),
                  pl.BlockSpec((PAIR_CB * LANE, D_MODEL), lambda j, i: (j, 0))],
        out_specs=pl.BlockSpec((PAIR_CB, tm, LANE), lambda j, i: (j, i, 0)),
        out_shape=jax.ShapeDtypeStruct((N_CB, t_len, LANE), F32),
        compiler_params=_params(("arbitrary", "arbitrary")),
    )(h, wt)


def proj_backward_input(dpb, wt, token, x, dy, norm_g, scale1p, t_len):
    tm = _row_tile(t_len, 512)

    def body(a_ref, wt_hbm, token_ref, x_ref, dy_ref, g_ref, sc_ref, gx_ref, vec_ref, w_ref, sem):
        del token_ref
        i = pl.program_id(0)

        @pl.when(i == 0)
        def _():
            cp = pltpu.make_async_copy(wt_hbm, w_ref, sem)
            cp.start()
            cp.wait()

        a = jnp.concatenate([a_ref[jj].astype(_BF) for jj in range(N_CB)], axis=1)
        dhv = _dot(a, w_ref[...])
        xv, g, sc = x_ref[...], g_ref[...], sc_ref[...]
        r = lax.rsqrt(jnp.mean(xv * xv, axis=-1, keepdims=True) + EPS)
        xn = xv * r
        dxn = dhv * (g * sc)
        gx_ref[...] = dy_ref[...] + r * dxn - xn * (r * r) * jnp.mean(xv * dxn, axis=-1, keepdims=True)
        t = dhv * xn
        _acc_rows(vec_ref, i, [jnp.sum(t * sc, axis=0, keepdims=True),
                               jnp.sum(t * g, axis=0, keepdims=True),
                               jnp.sum(dhv, axis=0, keepdims=True)])

    row = pl.BlockSpec((tm, D_MODEL), lambda i: (i, 0))
    return pl.pallas_call(
        body, name="proj_bwd_input", grid=(t_len // tm,),
        in_specs=[pl.BlockSpec((N_CB, tm, LANE), lambda i: (0, i, 0)),
                  pl.BlockSpec(memory_space=pl.ANY),
                  pl.BlockSpec(token.shape, lambda i: (0, 0)),
                  row, row, _vec_spec(), _vec_spec()],
        out_specs=[row, pl.BlockSpec((8, D_MODEL), lambda i: (0, 0))],
        out_shape=[jax.ShapeDtypeStruct((t_len, D_MODEL), F32), jax.ShapeDtypeStruct((8, D_MODEL), F32)],
        scratch_shapes=[pltpu.VMEM(wt.shape, wt.dtype), pltpu.SemaphoreType.DMA],
        compiler_params=_params(("arbitrary",)),
    )(dpb, wt, token, x, dy, norm_g, scale1p)


def proj_backward_weight(h_t, dpb, t_len):
    tk = _row_tile(t_len, 2048)

    def body(h_ref, b_ref, o_ref):
        k = pl.program_id(1)
        b = jnp.concatenate([b_ref[jj].astype(_BF) for jj in range(PAIR_CB)], axis=1)
        part = _dot(h_ref[...], b)

        @pl.when(k == 0)
        def _():
            for s in range(2):
                o_ref[s] = part[:, s * SHARD_IN:(s + 1) * SHARD_IN]

        @pl.when(k > 0)
        def _():
            for s in range(2):
                o_ref[s] = o_ref[s] + part[:, s * SHARD_IN:(s + 1) * SHARD_IN]

    return pl.pallas_call(
        body, name="proj_bwd_weight", grid=(N_DEV // 2, t_len // tk),
        in_specs=[pl.BlockSpec((D_MODEL, tk), lambda j, k: (0, k)),
                  pl.BlockSpec((PAIR_CB, tk, LANE), lambda j, k: (j, k, 0))],
        out_specs=pl.BlockSpec((2, D_MODEL, SHARD_IN), lambda j, k: (j, 0, 0)),
        out_shape=jax.ShapeDtypeStruct((N_DEV, D_MODEL, SHARD_IN), F32),
        compiler_params=_params(("arbitrary", "arbitrary")),
    )(h_t, dpb)


def sibling_blocks(g_in, c_idx):
    tr = D_MODEL

    def body(c_ref, g_ref, o_ref):
        del c_ref
        o_ref[...] = g_ref[...].astype(o_ref.dtype)

    return pl.pallas_call(
        body, name="sibling_blocks",
        grid_spec=pltpu.PrefetchScalarGridSpec(
            num_scalar_prefetch=1, grid=(N_DEV // 2, D_MODEL // tr),
            in_specs=[pl.BlockSpec((None, tr, SHARD_IN), lambda q, i, c: (2 * q + 1 - c[0], i, 0))],
            out_specs=pl.BlockSpec((None, tr, SHARD_IN), lambda q, i, c: (q, i, 0))),
        out_shape=jax.ShapeDtypeStruct((N_DEV // 2, D_MODEL, SHARD_IN), _BF),
        compiler_params=_params(("arbitrary", "arbitrary")),
    )(c_idx, g_in)


def _vec_spec():
    return pl.BlockSpec((1, D_MODEL), lambda i: (0, 0))


def _acc_rows(ref, i, rows):
    @pl.when(i == 0)
    def _():
        ref[...] = jnp.zeros_like(ref)

    for n, row in enumerate(rows):
        ref[n:n + 1, :] = ref[n:n + 1, :] + row


def adaln_forward(x, norm_g, scale1p, shift, t_len):
    tm = _row_tile(t_len, 512)

    def body(x_ref, g_ref, sc_ref, sh_ref, h_ref, ht_ref):
        xv = x_ref[...]
        r = lax.rsqrt(jnp.mean(xv * xv, axis=-1, keepdims=True) + EPS)
        h = xv * r * g_ref[...] * sc_ref[...] + sh_ref[...]
        h_ref[...] = h.astype(h_ref.dtype)
        ht_ref[...] = h.T.astype(ht_ref.dtype)

    return pl.pallas_call(
        body, name="adaln_fwd", grid=(t_len // tm,),
        in_specs=[pl.BlockSpec((tm, D_MODEL), lambda i: (i, 0)), _vec_spec(), _vec_spec(), _vec_spec()],
        out_specs=[pl.BlockSpec((tm, D_MODEL), lambda i: (i, 0)), pl.BlockSpec((D_MODEL, tm), lambda i: (0, i))],
        out_shape=[jax.ShapeDtypeStruct((t_len, D_MODEL), _BF), jax.ShapeDtypeStruct((D_MODEL, t_len), _BF)],
        compiler_params=_params(("arbitrary",)),
    )(x, norm_g, scale1p, shift)


def _head_norm(o, g):
    r = lax.rsqrt(jnp.mean(o * o, axis=-1, keepdims=True) + EPS)
    return r, o * r * g


def _group_spec(tm, cb):
    return pl.BlockSpec((N_HEADS, tm, LANE), lambda i, cb=cb: (cb // N_HEADS, i, 0))


MID_FINAL_G, MID_GATE, MID_LOSS, MID_HG_G, MID_RET_G = range(5)


def middle(x, target, oa, ob, pb, wout, gate, final_g, hg_g, ret_g, t_len):
    tm = _row_tile(t_len, 256)
    n_steps = t_len // tm

    def body(x_ref, t_ref, oa_ref, ob_ref, hz_ref, rz_ref, ga_ref, gb_ref, w_ref, gate_ref, fg_ref, hg_ref, rg_ref,
             dy_ref, doa_ref, dob_ref, dw_ref, vec_ref, dpb_ref, m_scr, dm_scr, keep, bufs, sems):
        i = pl.program_id(0)
        rows = pl.ds(pl.multiple_of(i * tm, tm), tm)

        def group_copies(sl):
            return [pltpu.make_async_copy(bufs.at[sl, n], dpb_ref.at[pl.ds(cb, N_HEADS), rows], sems.at[sl, n])
                    for n, cb in enumerate((CB_HZ, CB_RZ, CB_GA, CB_GB))]
        sides = ((oa_ref, hz_ref, ga_ref, hg_ref, doa_ref), (ob_ref, rz_ref, gb_ref, rg_ref, dob_ref))
        for hh in range(N_HEADS):
            ls = slice(hh * LANE, (hh + 1) * LANE)
            acc = None
            for side, (o_ref, z_ref, gt_ref, g_ref, _) in enumerate(sides):
                o = o_ref[hh]
                rr = lax.rsqrt(jnp.mean(o * o, axis=-1, keepdims=True) + EPS)
                orr = o * rr
                zz = z_ref[hh]
                sz = _sigmoid(zz)
                sgt = _sigmoid(gt_ref[hh])
                keep[side, hh, 0] = orr
                keep[side, hh, 1] = sz
                keep[side, hh, 2] = sgt
                keep[side, hh, 3] = jnp.broadcast_to(rr, orr.shape)
                u = sgt * ((orr * g_ref[:, ls]) * (zz * sz))
                acc = u if acc is None else acc + u
            m_scr[:, ls] = acc.astype(m_scr.dtype)
        zv = _dot(m_scr[...], w_ref[...])
        gt, fg = gate_ref[...], fg_ref[...]
        y = x_ref[...] + gt * zv
        r = lax.rsqrt(jnp.mean(y * y, axis=-1, keepdims=True) + EPS)
        yn = y * r
        err = yn * fg - t_ref[...]
        loss = 0.5 * jnp.sum(jnp.mean(err * err, axis=-1, keepdims=True), axis=0, keepdims=True)
        dout = err * (1.0 / D_MODEL)
        gd = dout * fg
        dy = r * gd - yn * (r * r) * jnp.mean(y * gd, axis=-1, keepdims=True)
        dy_ref[...] = dy
        dz = (dy * gt).astype(_BF)
        dm_scr[...] = _dot(dz, w_ref[...], _NT)
        part = _dot(m_scr[...], dz, _TN)

        @pl.when(i == 0)
        def _():
            dw_ref[...] = part

        @pl.when(i > 0)
        def _():
            dw_ref[...] = dw_ref[...] + part

        slot = _writeback_reserve(i, group_copies)
        dg = [[], []]
        for hh in range(N_HEADS):
            ls = slice(hh * LANE, (hh + 1) * LANE)
            dmh = dm_scr[:, ls]
            for side, (o_ref, z_ref, gt_ref, g_ref, do_ref) in enumerate(sides):
                zz, g = z_ref[hh], g_ref[:, ls]
                orr, sz, sgt, rr = keep[side, hh, 0], keep[side, hh, 1], keep[side, hh, 2], keep[side, hh, 3]
                n = orr * g
                silu = zz * sz
                du = dmh * sgt
                bufs[slot, 2 + side, hh] = (dmh * (n * silu) * (sgt * (1.0 - sgt))).astype(bufs.dtype)
                bufs[slot, side, hh] = (du * n * (sz * (1.0 + zz * (1.0 - sz)))).astype(bufs.dtype)
                dn = du * silu
                dg[side].append(jnp.sum(dn * orr, axis=0, keepdims=True))
                gdn = dn * g
                do_ref[hh] = (rr * (gdn - orr * jnp.mean(orr * gdn, axis=-1, keepdims=True))).astype(do_ref.dtype)
        _acc_rows(vec_ref, i, [jnp.sum(dout * yn, axis=0, keepdims=True),
                               jnp.sum(dy * zv, axis=0, keepdims=True),
                               jnp.broadcast_to(loss, (1, D_MODEL)),
                               jnp.concatenate(dg[0], axis=1), jnp.concatenate(dg[1], axis=1)])
        _writeback_commit(i, n_steps, slot, group_copies)

    row = pl.BlockSpec((tm, D_MODEL), lambda i: (i, 0))
    head = pl.BlockSpec((N_HEADS, tm, LANE), lambda i: (0, i, 0))
    full = pl.BlockSpec((D_MODEL, D_MODEL), lambda i: (0, 0))
    return pl.pallas_call(
        body, name="middle", grid=(n_steps,),
        in_specs=[row, row, head, head, _group_spec(tm, CB_HZ), _group_spec(tm, CB_RZ), _group_spec(tm, CB_GA),
                  _group_spec(tm, CB_GB), full, _vec_spec(), _vec_spec(), _vec_spec(), _vec_spec()],
        out_specs=[row, head, head, full, pl.BlockSpec((8, D_MODEL), lambda i: (0, 0)),
                   pl.BlockSpec(memory_space=pl.ANY)],
        out_shape=[jax.ShapeDtypeStruct((t_len, D_MODEL), F32),
                   jax.ShapeDtypeStruct((N_HEADS, t_len, LANE), _BF),
                   jax.ShapeDtypeStruct((N_HEADS, t_len, LANE), _BF),
                   jax.ShapeDtypeStruct((D_MODEL, D_MODEL), F32),
                   jax.ShapeDtypeStruct((8, D_MODEL), F32),
                   jax.ShapeDtypeStruct((N_CB, t_len, LANE), _BF)],
        scratch_shapes=[pltpu.VMEM((tm, D_MODEL), _BF), pltpu.VMEM((tm, D_MODEL), F32),
                        pltpu.VMEM((2, N_HEADS, 4, tm, LANE), F32),
                        pltpu.VMEM((2, 4, N_HEADS, tm, LANE), _BF), pltpu.SemaphoreType.DMA((2, 4))],
        compiler_params=_params(("arbitrary",)),
    )(x, target, oa, ob, pb, pb, pb, pb, wout, gate, final_g, hg_g, ret_g)


def device_step(x, target, mod, lb, project, norm_g, hg_g, ret_g, final_g, c_idx=None, start_exchange=None):
    t_len = x.shape[0]
    shift, scale, gate = mod[:, :D_MODEL], mod[:, D_MODEL:2 * D_MODEL], mod[:, 2 * D_MODEL:]
    scale1p = 1.0 + scale
    cos_t, sin_t = _rope_tables(t_len)
    h, h_t = adaln_forward(x, norm_g, scale1p, shift, t_len)
    pb, wt, wout = project(h)
    oa, ssave, asave = hgrn_forward(pb, lb, t_len)
    ob, rsave = retention_forward(pb, cos_t, sin_t, t_len)
    dy, doa, dob, dwout, vec_mid, dpb = middle(x, target, oa, ob, pb, wout, gate, final_g, hg_g, ret_g, t_len)
    dpb, dlb = hgrn_backward(pb, lb, doa, ssave, asave, dpb, t_len)
    dpb = retention_backward(pb, cos_t, sin_t, dob, rsave, dpb, t_len)
    c_idx = jnp.zeros((1,), jnp.int32) if c_idx is None else c_idx
    dwin = proj_backward_weight(h_t, dpb, t_len)
    dwin_sib = sibling_blocks(dwin, c_idx)
    token, pending = (start_exchange(dwin, dwin_sib, dwout) if start_exchange
                      else (jnp.zeros((8, LANE), F32), None))
    grad_x, vec_ada = proj_backward_input(dpb, wt, token, x, dy, norm_g, scale1p, t_len)
    return grad_x, dwin, dwout, vec_mid, vec_ada, dlb, pending


PACK_ROWS = 16
ROW_NORM_G, ROW_LB, ROW_HG_G, ROW_RET_G, ROW_FINAL_G, ROW_SHIFT, ROW_SCALE, ROW_GATE, ROW_LOSS = range(9)


def _mesh_pos():
    return lax.axis_index("x"), lax.axis_index("y"), lax.axis_index("c")


def _lin(pos):
    return 4 * pos[0] + 2 * pos[1] + pos[2]


def _xor_peer(pos, k):
    return tuple(1 - p if (k >> s) & 1 else p for p, s in zip(pos, (2, 1, 0)))


def _other_chips(pos):
    x, y, _ = pos
    return [(1 - x, y), (x, 1 - y), (1 - x, 1 - y)]


def _remote(src, dst, send_sem, recv_sem, to):
    return pltpu.make_async_remote_copy(src_ref=src, dst_ref=dst, send_sem=send_sem, recv_sem=recv_sem,
                                        device_id=to, device_id_type=MESH)


def pre_exchange(c, w_ada, b_ada, logits, w_in, w_out):
    def body(c_ref, wada_ref, bada_ref, logit_ref, win_ref, wout_ref, mod_ref, scall_ref, lb_ref, wt_ref, wo_ref,
             cg_ref, modall_ref, parts_ref, send1, recv1, send2, recv2):
        pos = _mesh_pos()
        cv = c_ref[...]
        slot = lambda p: pl.ds(pl.multiple_of(8 * _lin(p), 8), 8)
        cg_ref[slot(pos), :] = jnp.broadcast_to(cv * _sigmoid(cv), (8, D_MODEL))
        lb_ref[...] = _sigmoid(logit_ref[0:1, :] - logit_ref[1:2, :])
        peers = [_xor_peer(pos, k) for k in range(1, N_DEV)]
        gather = [_remote(cg_ref.at[slot(pos)], cg_ref.at[slot(pos)], send1.at[n], recv1.at[n], p)
                  for n, p in enumerate(peers)]
        for cp in gather:
            cp.start()
        wt_ref[...] = win_ref[...].T.astype(wt_ref.dtype)
        wo_ref[...] = wout_ref[...].astype(wo_ref.dtype)
        for n, p in enumerate(peers):
            _remote(cg_ref.at[slot(p)], cg_ref.at[slot(p)], send1.at[n], recv1.at[n], p).wait_recv()
        modall_ref[...] = _dot(cg_ref[...], wada_ref[...])
        scatter = [_remote(modall_ref.at[slot(p)], parts_ref.at[slot(pos)], send2.at[n], recv2.at[n], p)
                   for n, p in enumerate(peers)]
        for cp in scatter:
            cp.start()
        parts_ref[slot(pos), :] = modall_ref[slot(pos), :]
        for n, p in enumerate(peers):
            _remote(modall_ref.at[slot(p)], parts_ref.at[slot(p)], send2.at[n], recv2.at[n], p).wait_recv()
        for cp in gather + scatter:
            cp.wait_send()
        for j in range(N_DEV):
            cols = slice(j * SHARD_ADA, (j + 1) * SHARD_ADA)
            mod_ref[:, cols] = parts_ref[8 * j:8 * j + 1, :] + bada_ref[:, cols]
            scall_ref[j:j + 1, :] = cg_ref[8 * j:8 * j + 1, :]

    vmem = pl.BlockSpec(memory_space=pltpu.VMEM)
    return pl.pallas_call(
        body, name="pre_exchange",
        in_specs=[vmem] * 6, out_specs=[vmem] * 5,
        out_shape=[jax.ShapeDtypeStruct((1, 3 * D_MODEL), F32), jax.ShapeDtypeStruct((N_DEV, D_MODEL), F32),
                   jax.ShapeDtypeStruct((1, D_MODEL), F32),
                   jax.ShapeDtypeStruct(w_in.shape[::-1], _BF), jax.ShapeDtypeStruct(w_out.shape, _BF)],
        scratch_shapes=[pltpu.VMEM((N_DEV * 8, D_MODEL), F32), pltpu.VMEM((N_DEV * 8, SHARD_ADA), F32),
                        pltpu.VMEM((N_DEV * 8, SHARD_ADA), F32)] + [pltpu.SemaphoreType.DMA((N_DEV - 1,))] * 4,
        compiler_params=pltpu.CompilerParams(vmem_limit_bytes=VMEM_LIMIT),
    )(c, w_ada, b_ada, logits, w_in, w_out)


def weight_gather(win_sh, wout_sh):
    def body(win_ref, wout_ref, wg_ref, woutg_ref, send, recv, local):
        pos = _mesh_pos()
        x, y, c = pos
        sibling = (x, y, 1 - c)

        def route(core):
            return [(x + (1 - core) * (1 - 2 * x), y + core * (1 - 2 * y)),
                    (x + core * (1 - 2 * x), y + (1 - core) * (1 - 2 * y)),
                    (1 - x, 1 - y)]

        chips, sib_chips = route(c), route(1 - c)
        mine, first, later = [], [], []
        for a, (src, out) in enumerate(((win_ref, wg_ref), (wout_ref, woutg_ref))):
            def copy(k, block, to, src_ref=None, a=a, out=out):
                dst = out.at[_lin(block)]
                return _remote(dst if src_ref is None else src_ref, dst, send.at[7 * a + k], recv.at[7 * a + k], to)
            mine.append(pltpu.make_async_copy(src, out.at[_lin(pos)], local.at[a]))
            first += [copy(0, pos, sibling, src), copy(1, pos, (*chips[0], c), src), copy(2, pos, (*chips[1], c), src)]
            later.append([[copy(3, (*chips[0], c), (*chips[1], c)), copy(4, (*chips[0], c), sibling)],
                          [copy(5, (*chips[1], c), sibling)],
                          [copy(6, (*chips[2], c), sibling)]])
        for cp in mine + first:
            cp.start()
        for j in range(3):
            for a, out in enumerate((wg_ref, woutg_ref)):
                dst = out.at[_lin((*chips[j], c))]
                _remote(dst, dst, send.at[7 * a + 1 + j], recv.at[7 * a + 1 + j], pos).wait_recv()
                for cp in later[a][j]:
                    cp.start()
        for a, out in enumerate((wg_ref, woutg_ref)):
            dst = out.at[_lin(sibling)]
            _remote(dst, dst, send.at[7 * a], recv.at[7 * a], pos).wait_recv()
            for j in range(3):
                dst = out.at[_lin((*sib_chips[j], 1 - c))]
                _remote(dst, dst, send.at[7 * a + 4 + j], recv.at[7 * a + 4 + j], pos).wait_recv()
        for cp in first + [cp for per_array in later for group in per_array for cp in group]:
            cp.wait_send()
        for cp in mine:
            cp.wait()

    any_spec = pl.BlockSpec(memory_space=pl.ANY)
    return pl.pallas_call(
        body, name="weight_gather",
        in_specs=[any_spec, any_spec], out_specs=[any_spec, any_spec],
        out_shape=[jax.ShapeDtypeStruct((N_DEV,) + win_sh.shape, win_sh.dtype),
                   jax.ShapeDtypeStruct((N_DEV,) + wout_sh.shape, wout_sh.dtype)],
        scratch_shapes=[pltpu.SemaphoreType.DMA((14,)), pltpu.SemaphoreType.DMA((14,)),
                        pltpu.SemaphoreType.DMA((2,))],
    )(win_sh, wout_sh)


def grad_pair_exchange(g_sib, g_out):
    def body(gsib_ref, gout_ref, ra_ref, rb_ref, send, recv):
        pos = _mesh_pos()
        x, y, c = pos
        sibling = (x, y, 1 - c)
        copies = []
        for q in range(4):
            copies.append(_remote(gsib_ref.at[q], ra_ref.at[q], send.at[q], recv.at[q], sibling))
            copies.append(_remote(gout_ref.at[2 * q + (1 - c)], rb_ref.at[q], send.at[4 + q], recv.at[4 + q], sibling))
        for cp in copies:
            cp.start()
        for cp in copies:
            cp.wait_recv()
        for cp in copies:
            cp.wait_send()

    any_spec = pl.BlockSpec(memory_space=pl.ANY)
    return pl.pallas_call(
        body, name="grad_pair_exchange",
        in_specs=[any_spec, any_spec], out_specs=[any_spec, any_spec],
        out_shape=[jax.ShapeDtypeStruct(g_sib.shape, g_sib.dtype), jax.ShapeDtypeStruct((4,) + g_out.shape[1:], F32)],
        scratch_shapes=[pltpu.SemaphoreType.DMA((8,)), pltpu.SemaphoreType.DMA((8,))],
    )(g_sib, g_out)


def pair_sum(g_in, ra, g_out, rb, c_idx):
    tr = D_MODEL

    def body(c_ref, gin_ref, ra_ref, gout_ref, rb_ref, sb_ref, sbo_ref):
        del c_ref
        sb_ref[...] = (gin_ref[...] + ra_ref[...].astype(F32)).astype(sb_ref.dtype)
        sbo_ref[...] = gout_ref[...] + rb_ref[...]

    n_i = D_MODEL // tr
    return pl.pallas_call(
        body, name="pair_sum",
        grid_spec=pltpu.PrefetchScalarGridSpec(
            num_scalar_prefetch=1, grid=(4, n_i),
            in_specs=[pl.BlockSpec((None, tr, SHARD_IN), lambda q, i, c: (2 * q + c[0], i, 0)),
                      pl.BlockSpec((None, tr, SHARD_IN), lambda q, i, c: (q, i, 0)),
                      pl.BlockSpec((None, SHARD_OUT // n_i, D_MODEL), lambda q, i, c: (2 * q + c[0], i, 0)),
                      pl.BlockSpec((None, SHARD_OUT // n_i, D_MODEL), lambda q, i, c: (q, i, 0))],
            out_specs=[pl.BlockSpec((None, tr, SHARD_IN), lambda q, i, c: (q, i, 0)),
                       pl.BlockSpec((None, SHARD_OUT // n_i, D_MODEL), lambda q, i, c: (q, i, 0))]),
        out_shape=[jax.ShapeDtypeStruct(ra.shape, _BF), jax.ShapeDtypeStruct(rb.shape, F32)],
        compiler_params=_params(("arbitrary", "arbitrary")),
    )(c_idx, g_in, ra, g_out, rb)


_HBM = pl.BlockSpec(memory_space=pltpu.HBM)
_SEM = pl.BlockSpec(memory_space=pltpu.SEMAPHORE)
_N_CHIP_COPIES = 6


def _chip_copies(sb_ref, sbo_ref, rc_ref, rco_ref, send, recv):
    pos = _mesh_pos()
    copies = []
    for a, (src, dst) in enumerate(((sb_ref, rc_ref), (sbo_ref, rco_ref))):
        for j, chip in enumerate(_other_chips(pos)):
            copies.append(_remote(src.at[2 * chip[0] + chip[1]], dst.at[j], send.at[3 * a + j], recv.at[3 * a + j],
                                  (*chip, pos[2])))
    return copies


def grad_chip_start(sb, sbo):
    def body(sb_ref, sbo_ref, rc_ref, rco_ref, send, recv, sb_thru, sbo_thru, rc_thru, rco_thru, token):
        del sb_thru, sbo_thru, rc_thru, rco_thru
        for cp in _chip_copies(sb_ref, sbo_ref, rc_ref, rco_ref, send, recv):
            cp.start()
        token[...] = jnp.zeros_like(token)

    hbm = lambda a: pltpu.with_memory_space_constraint(a, pltpu.HBM)
    rc = lax.empty((3,) + sb.shape[1:], sb.dtype)
    rco = lax.empty((3,) + sbo.shape[1:], sbo.dtype)
    return pl.pallas_call(
        body, name="grad_chip_start",
        in_specs=[_HBM] * 4,
        out_specs=[_SEM, _SEM, _HBM, _HBM, _HBM, _HBM, pl.BlockSpec(memory_space=pltpu.VMEM)],
        out_shape=[pltpu.SemaphoreType.DMA((_N_CHIP_COPIES,)), pltpu.SemaphoreType.DMA((_N_CHIP_COPIES,)),
                   pltpu.HBM(sb.shape, sb.dtype), pltpu.HBM(sbo.shape, sbo.dtype),
                   pltpu.HBM(rc.shape, rc.dtype), pltpu.HBM(rco.shape, rco.dtype),
                   jax.ShapeDtypeStruct((8, LANE), F32)],
        input_output_aliases={0: 2, 1: 3, 2: 4, 3: 5},
        compiler_params=pltpu.CompilerParams(has_side_effects=pltpu.SideEffectType.DATAFLOW_SIDE_EFFECTING),
    )(hbm(sb), hbm(sbo), hbm(rc), hbm(rco))


def grad_chip_wait(send, recv, sb, sbo, rc, rco, after):
    def body(sb_ref, sbo_ref, rc_ref, rco_ref, send, recv, after_ref, sb_o, sbo_o, rc_o, rco_o):
        del after_ref, sb_o, sbo_o, rc_o, rco_o
        for cp in _chip_copies(sb_ref, sbo_ref, rc_ref, rco_ref, send, recv):
            cp.wait_send()
            cp.wait_recv()

    return pl.pallas_call(
        body, name="grad_chip_wait",
        in_specs=[_HBM] * 4 + [_SEM, _SEM, pl.BlockSpec(memory_space=pl.ANY)],
        out_specs=[_HBM] * 4,
        out_shape=[pltpu.HBM(sb.shape, sb.dtype), pltpu.HBM(sbo.shape, sbo.dtype),
                   pltpu.HBM(rc.shape, rc.dtype), pltpu.HBM(rco.shape, rco.dtype)],
        input_output_aliases={0: 0, 1: 1, 2: 2, 3: 3},
        compiler_params=pltpu.CompilerParams(has_side_effects=pltpu.SideEffectType.DATAFLOW_SIDE_EFFECTING),
    )(sb, sbo, rc, rco, send, recv, after)


def pack_gather(pack):
    def body(pack_ref, packs_ref, psend, precv):
        pos = _mesh_pos()
        me = _lin(pos)
        packs_ref[me] = pack_ref[...]
        peers = [_xor_peer(pos, k) for k in range(1, N_DEV)]
        gather = [_remote(packs_ref.at[me], packs_ref.at[me], psend.at[n], precv.at[n], p) for n, p in enumerate(peers)]
        for cp in gather:
            cp.start()
        for n, p in enumerate(peers):
            _remote(packs_ref.at[_lin(p)], packs_ref.at[_lin(p)], psend.at[n], precv.at[n], p).wait_recv()
        for cp in gather:
            cp.wait_send()

    vmem = pl.BlockSpec(memory_space=pltpu.VMEM)
    return pl.pallas_call(
        body, name="pack_gather", in_specs=[vmem], out_specs=vmem,
        out_shape=jax.ShapeDtypeStruct((N_DEV,) + pack.shape, F32),
        scratch_shapes=[pltpu.SemaphoreType.DMA((N_DEV - 1,)), pltpu.SemaphoreType.DMA((N_DEV - 1,))],
    )(pack)


def pack_rows(vec_mid, vec_ada, dlb):
    def body(mid_ref, ada_ref, dlb_ref, o_ref):
        mid = lambda r: mid_ref[r:r + 1, :]
        rows = [ada_ref[0:1, :], dlb_ref[...], mid(MID_HG_G), mid(MID_RET_G), mid(MID_FINAL_G),
                ada_ref[2:3, :], ada_ref[1:2, :], mid(MID_GATE), mid(MID_LOSS)]
        o_ref[...] = jnp.zeros_like(o_ref)
        for n, row in enumerate(rows):
            o_ref[n:n + 1, :] = row

    vmem = pl.BlockSpec(memory_space=pltpu.VMEM)
    return pl.pallas_call(body, name="pack_rows", in_specs=[vmem] * 3, out_specs=vmem,
                          out_shape=jax.ShapeDtypeStruct((PACK_ROWS, D_MODEL), F32))(vec_mid, vec_ada, dlb)


def _adamw(w, g, m, v):
    m = ADAM_B1 * m + (1.0 - ADAM_B1) * g
    v = ADAM_B2 * v + (1.0 - ADAM_B2) * (g * g)
    m_hat = m / (1.0 - ADAM_B1 ** ADAM_STEP)
    v_hat = v / (1.0 - ADAM_B2 ** ADAM_STEP)
    delta = -ADAM_LR * (m_hat / (jnp.sqrt(v_hat) + ADAM_EPS) + ADAM_WD * w)
    return delta, m, v


def adam_shard(chip_idx, own, parts, w, m, v, name):
    rows, cols = w.shape
    tr = min(rows, 256)

    def body(chip_ref, p0, p1, p2, p3, w_ref, m_ref, v_ref, g_ref, d_ref, nm_ref, nv_ref):
        del chip_ref
        g = ((p0[...].astype(F32) + p1[...].astype(F32)) + p2[...].astype(F32)) + p3[...].astype(F32)
        g_ref[...] = g
        d_ref[...], nm_ref[...], nv_ref[...] = _adamw(w_ref[...], g, m_ref[...], v_ref[...])

    part = lambda q: pl.BlockSpec((None, tr, cols), lambda i, chip, q=q: (q, i, 0))
    tile = pl.BlockSpec((tr, cols), lambda i, chip: (i, 0))
    return pl.pallas_call(
        body, name=name,
        grid_spec=pltpu.PrefetchScalarGridSpec(
            num_scalar_prefetch=1, grid=(rows // tr,),
            in_specs=[pl.BlockSpec((None, tr, cols), lambda i, chip: (chip[0], i, 0)), part(0), part(1), part(2),
                      tile, tile, tile],
            out_specs=[tile] * 4),
        out_shape=[jax.ShapeDtypeStruct(w.shape, F32)] * 4,
        compiler_params=_params(("arbitrary",)),
    )(chip_idx, own, parts, parts, parts, w, m, v)


def adam_ada(sc_t, dmod_all, me_idx, w, m, v):
    def body(me_ref, sc_ref, dm_ref, w_ref, m_ref, v_ref, g_ref, d_ref, nm_ref, nv_ref):
        del me_ref
        g = _dot_f32(sc_ref[...], dm_ref[...])
        g_ref[...] = g
        d_ref[...], nm_ref[...], nv_ref[...] = _adamw(w_ref[...], g, m_ref[...], v_ref[...])

    full = pl.BlockSpec(w.shape, lambda i, me: (0, 0))
    return pl.pallas_call(
        body, name="adam_ada",
        grid_spec=pltpu.PrefetchScalarGridSpec(
            num_scalar_prefetch=1, grid=(1,),
            in_specs=[pl.BlockSpec(sc_t.shape, lambda i, me: (0, 0)),
                      pl.BlockSpec((LANE, SHARD_ADA), lambda i, me: (0, me[0])), full, full, full],
            out_specs=[full] * 4),
        out_shape=[jax.ShapeDtypeStruct(w.shape, F32)] * 4,
        compiler_params=_params(("arbitrary",)),
    )(me_idx, sc_t, dmod_all, w, m, v)


def adam_vectors(packs, lb, params, ms, vs):
    n = len(params)

    def body(*refs):
        packs_ref, lb_ref = refs[0], refs[1]
        w_refs, m_refs, v_refs = refs[2:2 + n], refs[2 + n:2 + 2 * n], refs[2 + 2 * n:2 + 3 * n]
        loss_ref = refs[2 + 3 * n]
        outs = refs[3 + 3 * n:3 + 7 * n]
        tot_ref = refs[3 + 7 * n]
        tot = packs_ref[0]
        for d in range(1, N_DEV):
            tot = tot + packs_ref[d]
        tot_ref[...] = tot
        row = lambda r: tot_ref[r:r + 1, :]
        lbv = lb_ref[...]
        dl0 = row(ROW_LB) * lbv * (1.0 - lbv)
        grads = [[row(ROW_NORM_G)],
                 [jnp.concatenate([row(ROW_SHIFT), row(ROW_SCALE), row(ROW_GATE)], axis=1)],
                 [dl0, -dl0],
                 [row(ROW_HG_G)], [row(ROW_RET_G)], [row(ROW_FINAL_G)]]
        loss_ref[...] = tot_ref[ROW_LOSS:ROW_LOSS + 1, 0:LANE]
        for j, g_rows in enumerate(grads):
            for r, g in enumerate(g_rows):
                rs = slice(r, r + 1)
                d, nm, nv = _adamw(w_refs[j][rs, :], g, m_refs[j][rs, :], v_refs[j][rs, :])
                outs[4 * j][rs, :] = g
                outs[4 * j + 1][rs, :] = d
                outs[4 * j + 2][rs, :] = nm
                outs[4 * j + 3][rs, :] = nv

    vmem = pl.BlockSpec(memory_space=pltpu.VMEM)
    out_shape = [jax.ShapeDtypeStruct((1, LANE), F32)]
    for w in params:
        out_shape += [jax.ShapeDtypeStruct(w.shape, F32)] * 4
    return pl.pallas_call(
        body, name="adam_vectors", in_specs=[vmem] * (2 + 3 * n), out_specs=[vmem] * len(out_shape),
        out_shape=out_shape, scratch_shapes=[pltpu.VMEM((PACK_ROWS, D_MODEL), F32)],
    )(packs, lb, *params, *ms, *vs)


def kernel(x, c, norm_g, w_ada, b_ada, w_in, hg_lb_logits, hg_norm_g, ret_norm_g, w_out, final_g, loss_target, m_norm_g, m_w_ada, m_b_ada, m_w_in, m_hg_lb_logits, m_hg_norm_g, m_ret_norm_g, m_w_out, m_final_g, v_norm_g, v_w_ada, v_b_ada, v_w_in, v_hg_lb_logits, v_hg_norm_g, v_ret_norm_g, v_w_out, v_final_g):
    pos = _mesh_pos()
    me_idx = jnp.reshape(_lin(pos), (1,)).astype(jnp.int32)
    c_idx = jnp.reshape(pos[2], (1,)).astype(jnp.int32)
    vec = lambda a: a.reshape(1, D_MODEL)

    mod, scall, lb, wt_sh, wout_sh = pre_exchange(c, w_ada[0], b_ada, hg_lb_logits, w_in[0], w_out[0])
    chip_idx = jnp.reshape(2 * pos[0] + pos[1], (1,)).astype(jnp.int32)

    def project(h):
        wtg, woutg = weight_gather(wt_sh, wout_sh)
        wt = wtg.reshape(D_IN, D_MODEL)
        return proj_forward(h, wt, h.shape[0]), wt, woutg.reshape(D_MODEL, D_MODEL)

    def start_exchange(dwin, dwin_sib, dwout):
        dwout = dwout.reshape(N_DEV, SHARD_OUT, D_MODEL)
        ra, rb = grad_pair_exchange(dwin_sib, dwout)
        sb, sbo = pair_sum(dwin, ra, dwout, rb, c_idx)
        send, recv, sb, sbo, rc, rco, token = grad_chip_start(sb, sbo)
        return token, (send, recv, sb, sbo, rc, rco)

    grad_x, _, _, vec_mid, vec_ada, dlb, pending = device_step(
        x[0], loss_target[0], mod, lb, project, norm_g, hg_norm_g, ret_norm_g, vec(final_g), c_idx, start_exchange)
    packs = pack_gather(pack_rows(vec_mid, vec_ada, dlb))
    dmod_all = packs[:, ROW_SHIFT:ROW_GATE + 1, :].reshape(N_DEV, 3 * D_MODEL)
    dmod_all = jnp.pad(dmod_all, ((0, LANE - N_DEV), (0, 0)))
    sc_t = jnp.pad(scall.T, ((0, 0), (0, LANE - N_DEV)))
    g_ada, d_ada, nm_ada, nv_ada = adam_ada(sc_t, dmod_all, me_idx, w_ada[0], m_w_ada[0], v_w_ada[0])
    small = adam_vectors(
        packs, lb,
        (norm_g, b_ada, hg_lb_logits, hg_norm_g, ret_norm_g, vec(final_g)),
        (m_norm_g, m_b_ada, m_hg_lb_logits, m_hg_norm_g, m_ret_norm_g, vec(m_final_g)),
        (v_norm_g, v_b_ada, v_hg_lb_logits, v_hg_norm_g, v_ret_norm_g, vec(v_final_g)))
    loss = small[0][0, 0]
    sb, sbo, rc, rco = grad_chip_wait(*pending, small[0])
    g_in, d_in, nm_in, nv_in = adam_shard(chip_idx, sb, rc, w_in[0], m_w_in[0], v_w_in[0], "adam_w_in")
    g_out, d_out, nm_out, nv_out = adam_shard(chip_idx, sbo, rco, w_out[0], m_w_out[0], v_w_out[0], "adam_w_out")
    (g_ng, d_ng, nm_ng, nv_ng), (g_b, d_b, nm_b, nv_b), (g_lb, d_lb, nm_lb, nv_lb), (g_hg, d_hg, nm_hg, nv_hg), \
        (g_rg, d_rg, nm_rg, nv_rg), (g_fg, d_fg, nm_fg, nv_fg) = [small[1 + 4 * j:5 + 4 * j] for j in range(6)]
    flat = lambda a: a.reshape(D_MODEL)

    def group(ng, ada, b, win, lbl, hg, rg, wo, fg):
        return (ng, ada[None], b, win[None], lbl, hg, rg, wo[None], flat(fg))

    return (loss, grad_x[None],
            *group(g_ng, g_ada, g_b, g_in, g_lb, g_hg, g_rg, g_out, g_fg),
            *group(d_ng, d_ada, d_b, d_in, d_lb, d_hg, d_rg, d_out, d_fg),
            *group(nm_ng, nm_ada, nm_b, nm_in, nm_lb, nm_hg, nm_rg, nm_out, nm_fg),
            *group(nv_ng, nv_ada, nv_b, nv_in, nv_lb, nv_hg, nv_rg, nv_out, nv_fg))
```

```python
import functools

import numpy as np
import jax
import jax.numpy as jnp
from jax import lax
from jax.experimental import pallas as pl
from jax.experimental.pallas import tpu as pltpu

F32 = jnp.float32
_BF = jnp.bfloat16

D_MODEL = 1024
N_HEADS = 8
LANE = 128
RET_DK = 64
D_IN = 9216
N_DEV = 8
SHARD_IN = D_IN // N_DEV
SHARD_ADA = 3 * D_MODEL // N_DEV
SHARD_OUT = D_MODEL // N_DEV
N_CB = D_IN // LANE
CB_PER_SHARD = SHARD_IN // LANE
CHUNK = 128
N_LEVELS = 7
EPS = 1e-6
LOG2_E = float(np.log2(np.e))
ROPE_BASE = 10000.0
CB_HQ, CB_HF, CB_HI, CB_HZ, CB_RQ, CB_RK, CB_RV, CB_RZ, CB_GA, CB_GB = 0, 8, 16, 24, 32, 36, 40, 48, 56, 64
VMEM_LIMIT = 56 * 1024 * 1024

ADAM_LR, ADAM_B1, ADAM_B2, ADAM_EPS, ADAM_WD, ADAM_STEP = 0.001, 0.9, 0.999, 1e-08, 0.01, 10

_NN = (((1,), (0,)), ((), ()))
_NT = (((1,), (1,)), ((), ()))
_TN = (((0,), (0,)), ((), ()))
MESH = pl.DeviceIdType.MESH


def _dot(a, b, dims=_NN):
    return lax.dot_general(a.astype(_BF), b.astype(_BF), dims, preferred_element_type=F32)


def _split2(a):
    hi = a.astype(_BF)
    lo = (a - hi.astype(F32)).astype(_BF)
    return jnp.concatenate([hi, lo], axis=1)


def _dot_sel(sel, a):
    n = a.shape[1]
    r = lax.dot_general(sel.astype(_BF), _split2(a), _NN, preferred_element_type=F32)
    return r[:, :n] + r[:, n:]


def _dot_f32(a, b):
    def pieces(v):
        p1 = v.astype(_BF)
        r1 = v - p1.astype(F32)
        p2 = r1.astype(_BF)
        p3 = (r1 - p2.astype(F32)).astype(_BF)
        return p1, p2, p3
    a1, a2, a3 = pieces(a)
    b1, b2, b3 = pieces(b)
    d = lambda u, v: lax.dot_general(u, v, _NN, preferred_element_type=F32)
    return ((d(a1, b3) + d(a2, b2) + d(a3, b1)) + (d(a1, b2) + d(a2, b1))) + d(a1, b1)


def _sigmoid(v):
    return 1.0 / (1.0 + jnp.exp(-v))


def _params(sem=None):
    return pltpu.CompilerParams(dimension_semantics=sem, vmem_limit_bytes=VMEM_LIMIT)


def _hgrn_consts():
    c, nl = CHUNK, N_LEVELS
    t = np.arange(c)[:, None]
    j = np.arange(c)[None, :]
    sel = [j <= t]
    masks = [j == t]
    for l in range(1, nl + 1):
        m = ((t >> l) << l) + (1 << (l - 1)) - 1
        sec = t > m
        sel.append(np.where(sec, (j > m) & (j <= t), (j > t) & (j <= m)))
        same = (t >> l) == (j >> l)
        masks.append(same & sec & (j <= m))
    sel.append(j > t)
    sel = np.concatenate(sel, 0).astype(np.float32)
    masks = np.stack(masks).astype(np.float32)
    sgn = np.stack([np.where((t & (1 << (l - 1))) != 0, 1.0, -1.0) * np.ones((1, LANE)) for l in range(3, nl + 1)])
    return dict(tri=jnp.asarray(sel[:c], _BF),
                lvl=jnp.asarray(masks, F32),
                sgn=jnp.asarray(sgn, F32),
                sel_t=jnp.asarray(sel.T, _BF),
                lvl_b=jnp.asarray(masks, _BF),
                lvlt_b=jnp.asarray(np.swapaxes(masks, 1, 2), _BF))


def _level_exponents(b, logf, b_scr, sgn_ref):
    c = CHUNK
    b_scr[...] = b
    row = lax.broadcasted_iota(jnp.int32, (c, LANE), 0)
    nxt = pltpu.roll(logf, c - 1, 0)
    prv = pltpu.roll(logf, 1, 0)
    r4 = row & 3
    out = [jnp.where((row & 1) == 1, logf, 0.0),
           jnp.where(r4 == 0, nxt, jnp.where(r4 == 1, 0.0, jnp.where(r4 == 2, logf, logf + prv)))]
    for l in range(3, N_LEVELS + 1):
        size, half = 1 << l, 1 << (l - 1)
        ref = jnp.concatenate([jnp.broadcast_to(b_scr[i * size + half - 1:i * size + half, :], (size, LANE))
                               for i in range(c // size)], axis=0)
        out.append((b - ref) * sgn_ref[l - 3])
    return out


def _hgrn_chunk(hq, hf, hi, lbv, tri_ref, sgn_ref, b_scr):
    sq = _sigmoid(hq)
    q = hq * sq
    sg = _sigmoid(hf)
    omlb = 1.0 - lbv
    f = lbv + omlb * sg
    k = 1.0 - f
    logf = jnp.log(f) * LOG2_E
    b = _dot_sel(tri_ref[...], logf)
    bc = jnp.sum(logf, axis=0, keepdims=True)
    lev = [None] + [jnp.exp2(e) for e in _level_exponents(b, logf, b_scr, sgn_ref)]
    return dict(sq=sq, q=q, sg=sg, omlb=omlb, f=f, k=k, v=hi, eb=jnp.exp2(b), erem=jnp.exp2(bc - b),
                ebc=jnp.exp2(bc), lev=lev)


def _blockdiag(a, b):
    z = jnp.zeros_like(a)
    return jnp.concatenate([jnp.concatenate([a, z], axis=1), jnp.concatenate([z, b], axis=1)], axis=0)


def _level_operands(a):
    q, k = a["q"].astype(_BF), a["k"].astype(_BF)
    lev = [None] + [a["lev"][l].astype(_BF) for l in range(1, N_LEVELS + 1)]
    ql = [q] + [q * lev[l] for l in range(1, N_LEVELS + 1)]
    kl = [k] + [k * lev[l] for l in range(1, N_LEVELS + 1)]
    pairs = range(0, N_LEVELS + 1, 2)
    return ([jnp.concatenate([ql[l], ql[l + 1]], axis=1) for l in pairs], [_blockdiag(kl[l], kl[l + 1]) for l in pairs],
            ql, kl)


def _hgrn_scores(a, lvl_ref, q_pairs, k_diags):
    acc = None
    for n, (qp, kd) in enumerate(zip(q_pairs, k_diags)):
        both = lax.dot_general(qp, kd, _NT, preferred_element_type=F32)
        part = lvl_ref[2 * n] * both[:, :CHUNK] + lvl_ref[2 * n + 1] * both[:, CHUNK:]
        acc = part if acc is None else acc + part
    return acc


SCAN_UNROLL = 8
RET_UNROLL = 8


def _writeback_reserve(step, make_copies):
    slot = step % 2

    @pl.when(step >= 2)
    def _():
        for cp in make_copies(slot):
            cp.wait()

    return slot


def _writeback_commit(step, n_steps, slot, make_copies):
    for cp in make_copies(slot):
        cp.start()

    @pl.when(step == n_steps - 1)
    def _():
        for cp in make_copies(slot):
            cp.wait()
        if n_steps > 1:
            for cp in make_copies(1 - slot):
                cp.wait()


def _resident(const):
    zeros = (0,) * const.ndim
    return pl.BlockSpec(const.shape, lambda p, t: zeros)


def _time_block(t_len):
    return min(t_len, 2048)


def hgrn_forward(pb, lb, t_len):
    nc = t_len // CHUNK
    tb = _time_block(t_len)
    ncb = tb // CHUNK
    consts = _hgrn_consts()
    operands = [consts[n] for n in ("tri", "lvl", "sgn")]

    def body(hq_ref, hf_ref, hi_ref, lb_ref, tri_ref, lvl_ref, sgn_ref, o_ref, ssave_ref, asave_ref, st_ref, b_scr):
        @pl.when(pl.program_id(1) == 0)
        def _():
            st_ref[...] = jnp.zeros_like(st_ref)

        def chunk(ci, carry):
            r = pl.ds(pl.multiple_of(ci * CHUNK, CHUNK), CHUNK)
            for hd in range(2):
                lbv = lb_ref[:, hd * LANE:(hd + 1) * LANE]
                a = _hgrn_chunk(hq_ref[hd, r, :], hf_ref[hd, r, :], hi_ref[hd, r, :], lbv, tri_ref, sgn_ref,
                                b_scr.at[hd])
                q_pairs, k_diags, _, _ = _level_operands(a)
                st = st_ref[hd]
                ssave_ref[hd, ci] = st
                scores = _hgrn_scores(a, lvl_ref, q_pairs, k_diags).astype(asave_ref.dtype)
                asave_ref[hd, ci] = scores
                o_ref[hd, r, :] = _dot(a["q"] * a["eb"], st, _NT) + _dot(scores, a["v"])
                st_ref[hd] = st * a["ebc"] + _dot(a["v"], a["k"] * a["erem"], _TN)
            return carry

        lax.fori_loop(0, ncb, chunk, 0, unroll=SCAN_UNROLL)

    pair = lambda base: pl.BlockSpec((2, tb, LANE), lambda p, t, base=base: (base // 2 + p, t, 0))
    per_chunk = pl.BlockSpec((2, ncb, LANE, LANE), lambda p, t: (p, t, 0, 0))
    return pl.pallas_call(
        body, name="hgrn_fwd", grid=(N_HEADS // 2, t_len // tb),
        in_specs=[pair(CB_HQ), pair(CB_HF), pair(CB_HI),
                  pl.BlockSpec((1, 2 * LANE), lambda p, t: (0, p))] + [_resident(c) for c in operands],
        out_specs=[pl.BlockSpec((2, tb, LANE), lambda p, t: (p, t, 0)), per_chunk, per_chunk],
        out_shape=[jax.ShapeDtypeStruct((N_HEADS, t_len, LANE), F32),
                   jax.ShapeDtypeStruct((N_HEADS, nc, LANE, LANE), F32),
                   jax.ShapeDtypeStruct((N_HEADS, nc, CHUNK, CHUNK), _BF)],
        scratch_shapes=[pltpu.VMEM((2, LANE, LANE), F32), pltpu.VMEM((2, CHUNK, LANE), F32)],
        compiler_params=_params(("arbitrary", "arbitrary")),
    )(pb, pb, pb, lb, *operands)


def hgrn_backward(pb, lb, do, ssave, asave, dpb, t_len):
    tb = _time_block(t_len)
    ncb, ntb = tb // CHUNK, t_len // tb
    consts = _hgrn_consts()
    operands = [consts[n] for n in ("tri", "sgn", "sel_t", "lvl_b", "lvlt_b")]

    def body(hq_ref, hf_ref, hi_ref, lb_ref, do_ref, ssave_ref, asave_ref, tri_ref, sgn_ref, selt_ref, lvlb_ref,
             lvltb_ref, dpb_in, dpb_ref, dlb_ref, dq_buf, df_buf, di_buf, dst_ref, b_scr, sems):
        del dpb_in
        p, t = pl.program_id(0), pl.program_id(1)
        step = p * ntb + t
        rows = pl.ds(pl.multiple_of((ntb - 1 - t) * tb, tb), tb)

        def out_copies(sl):
            return [pltpu.make_async_copy(buf.at[sl], dpb_ref.at[pl.ds(base + 2 * p, 2), rows], sems.at[sl, n])
                    for n, (buf, base) in enumerate(((dq_buf, CB_HQ), (df_buf, CB_HF), (di_buf, CB_HI)))]

        slot = _writeback_reserve(step, out_copies)

        @pl.when(t == 0)
        def _():
            dst_ref[...] = jnp.zeros_like(dst_ref)
            dlb_ref[...] = jnp.zeros_like(dlb_ref)

        def chunk(i, carry):
            ci = ncb - 1 - i
            r = pl.ds(pl.multiple_of(ci * CHUNK, CHUNK), CHUNK)
            for hd in range(2):
                head_chunk(hd, ci, r)
            return carry

        def head_chunk(hd, ci, r):
            lbv = lb_ref[:, hd * LANE:(hd + 1) * LANE]
            hq = hq_ref[hd, r, :]
            a = _hgrn_chunk(hq, hf_ref[hd, r, :], hi_ref[hd, r, :], lbv, tri_ref, sgn_ref, b_scr.at[hd])
            _, k_diags, ql, kl = _level_operands(a)
            q, k, v = a["q"], a["k"], a["v"]
            g = do_ref[hd, r, :]
            st0 = ssave_ref[hd, ci]
            dst = dst_ref[hd]
            scores = asave_ref[hd, ci]
            da = _dot(g, v, _NT)
            da_t = _dot(v, g, _NT)
            kb = k * a["erem"]
            qb = q * a["eb"]
            dv = _dot(scores, g, _TN) + _dot(kb, dst, _NT)
            dq_inter = _dot(g, st0) * a["eb"]
            dk_state = _dot(v, dst) * a["erem"]
            dq, dk = dq_inter, dk_state
            de = [q * dq_inter]
            da_b, dat_b = da.astype(_BF), da_t.astype(_BF)
            for n in range(len(k_diags)):
                l0, l1 = 2 * n, 2 * n + 1
                da_pair = jnp.concatenate([lvlb_ref[l0] * da_b, lvlb_ref[l1] * da_b], axis=1)
                dat_pair = jnp.concatenate([lvltb_ref[l0] * dat_b, lvltb_ref[l1] * dat_b], axis=1)
                dq_both = lax.dot_general(da_pair, k_diags[n], _NN, preferred_element_type=F32)
                dk_both = lax.dot_general(dat_pair, _blockdiag(ql[l0], ql[l1]), _NN, preferred_element_type=F32)
                for l, cols in ((l0, slice(0, LANE)), (l1, slice(LANE, 2 * LANE))):
                    dql, dkl = dq_both[:, cols], dk_both[:, cols]
                    if l > 0:
                        e = a["lev"][l]
                        dql, dkl = dql * e, dkl * e
                        de.append(q * dql + k * dkl)
                    dq = dq + dql
                    dk = dk + dkl
            de.append(k * dk_state)
            dst_ref[hd] = dst * a["ebc"] + _dot(g, qb, _TN)
            dbc = jnp.sum(dst * st0, axis=0, keepdims=True) * a["ebc"]
            de2 = lax.dot_general(selt_ref[...], _split2(jnp.concatenate(de, axis=0)), _NN,
                                  preferred_element_type=F32)
            dlogf = de2[:, :LANE] + de2[:, LANE:] + dbc
            sq, sg = a["sq"], a["sg"]
            df = dlogf / a["f"] - dk
            dq_buf[slot, hd, r, :] = (dq * (sq * (1.0 + hq * (1.0 - sq)))).astype(dq_buf.dtype)
            df_buf[slot, hd, r, :] = (df * a["omlb"] * sg * (1.0 - sg)).astype(df_buf.dtype)
            di_buf[slot, hd, r, :] = dv.astype(di_buf.dtype)
            cols = slice(hd * LANE, (hd + 1) * LANE)
            dlb_ref[:, cols] = dlb_ref[:, cols] + jnp.sum(df * (1.0 - sg), axis=0, keepdims=True)

        lax.fori_loop(0, ncb, chunk, 0, unroll=SCAN_UNROLL)
        _writeback_commit(step, (N_HEADS // 2) * ntb, slot, out_copies)

    pair = lambda base: pl.BlockSpec((2, tb, LANE), lambda p, t, base=base: (base // 2 + p, ntb - 1 - t, 0))
    any_spec = pl.BlockSpec(memory_space=pl.ANY)
    per_chunk = pl.BlockSpec((2, ncb, LANE, LANE), lambda p, t: (p, ntb - 1 - t, 0, 0))
    return pl.pallas_call(
        body, name="hgrn_bwd", grid=(N_HEADS // 2, ntb),
        in_specs=[pair(CB_HQ), pair(CB_HF), pair(CB_HI),
                  pl.BlockSpec((1, 2 * LANE), lambda p, t: (0, p)),
                  pair(0), per_chunk, per_chunk]
        + [_resident(c) for c in operands] + [any_spec],
        out_specs=[any_spec, pl.BlockSpec((1, 2 * LANE), lambda p, t: (0, p))],
        out_shape=[jax.ShapeDtypeStruct(dpb.shape, dpb.dtype), jax.ShapeDtypeStruct((1, D_MODEL), F32)],
        scratch_shapes=[pltpu.VMEM((2, 2, tb, LANE), dpb.dtype)] * 3 + [
            pltpu.VMEM((2, LANE, LANE), F32), pltpu.VMEM((2, CHUNK, LANE), F32), pltpu.SemaphoreType.DMA((2, 3))],
        input_output_aliases={7 + len(operands): 0},
        compiler_params=_params(("arbitrary", "arbitrary")),
    )(pb, pb, pb, lb, do, ssave, asave, *operands, dpb)


def _rope_tables(t_len):
    half = RET_DK // 2
    inv_freq = (1.0 / (np.float32(ROPE_BASE) ** np.linspace(0.0, 1.0, half, dtype=np.float32))).astype(np.float32)
    ang = (np.arange(t_len, dtype=np.float32)[:, None] * inv_freq[None, :]).astype(np.float64)
    cos, sin = np.cos(ang).astype(np.float32), np.sin(ang).astype(np.float32)
    cos_t = np.concatenate([cos, cos, cos, cos], axis=1)
    sin_t = np.concatenate([-sin, sin, -sin, sin], axis=1)
    return jnp.asarray(cos_t), jnp.asarray(sin_t)


def _swap_halves(v):
    half = RET_DK // 2
    lane = lax.broadcasted_iota(jnp.int32, v.shape, 1)
    first = (lane & (RET_DK - 1)) < half
    return jnp.where(first, pltpu.roll(v, LANE - half, 1), pltpu.roll(v, half, 1))


def _ret_head_consts(hidx):
    c = CHUNK
    hf = jnp.full((1, LANE), hidx, jnp.int32).astype(F32)
    lg = jnp.log(1.0 - jnp.exp(-(5.0 + hf) * np.float32(np.log(2.0))))
    row = lax.broadcasted_iota(jnp.int32, (c, c), 0)
    col = lax.broadcasted_iota(jnp.int32, (c, c), 1)
    rel = (row - col).astype(F32)
    dm = jnp.where(rel >= 0, jnp.exp(lg[:, :1] * jnp.maximum(rel, 0.0)), 0.0)
    dm_t = jnp.where(rel <= 0, jnp.exp(lg[:, :1] * jnp.maximum(-rel, 0.0)), 0.0)
    idx = lax.broadcasted_iota(jnp.int32, (c, LANE), 0).astype(F32)
    zeta = jnp.exp(lg * (c - 1.0 - idx))
    xi = jnp.exp(lg * (idx + 1.0))
    cdec = jnp.exp(lg * float(c))
    return dm, zeta, xi, cdec, dm_t


def _lane_mask(which):
    lane = lax.broadcasted_iota(jnp.int32, (1, LANE), 1)
    return ((lane // RET_DK) == which).astype(F32)


def retention_forward(pb, cos_t, sin_t, t_len):
    nc = t_len // CHUNK

    tb = _time_block(t_len)
    ncb = tb // CHUNK

    def body(rq_ref, rk_ref, rv_ref, cos_ref, sin_ref, o_ref, rsave_ref, st_ref):
        p = pl.program_id(0)

        @pl.when(pl.program_id(1) == 0)
        def _():
            st_ref[...] = jnp.zeros_like(st_ref)

        consts = [_ret_head_consts(2 * p + hd) for hd in range(2)]

        def chunk(ci, carry):
            r = pl.ds(pl.multiple_of(ci * CHUNK, CHUNK), CHUNK)
            cs, sn = cos_ref[r, :], sin_ref[r, :]
            q = rq_ref[r, :]
            k = rk_ref[r, :]
            q = q * cs + _swap_halves(q) * sn
            k = (k * cs + _swap_halves(k) * sn) * RET_DK ** -0.5
            for hd in range(2):
                dm, zeta, xi, cdec, _ = consts[hd]
                lm = _lane_mask(hd)
                qh, kh = q * lm, k * lm
                v = rv_ref[hd, r, :]
                st = st_ref[hd]
                rsave_ref[hd, ci] = st
                scores = _dot(qh, kh, _NT) * dm
                o_ref[hd, r, :] = _dot(scores, v) + _dot(qh * xi, st, _NT)
                st_ref[hd] = st * cdec + _dot(v, kh * zeta, _TN)
            return carry

        lax.fori_loop(0, ncb, chunk, 0, unroll=RET_UNROLL)

    return pl.pallas_call(
        body, name="ret_fwd", grid=(N_HEADS // 2, t_len // tb),
        in_specs=[pl.BlockSpec((None, tb, LANE), lambda p, t: (CB_RQ + p, t, 0)),
                  pl.BlockSpec((None, tb, LANE), lambda p, t: (CB_RK + p, t, 0)),
                  pl.BlockSpec((2, tb, LANE), lambda p, t: (CB_RV // 2 + p, t, 0)),
                  pl.BlockSpec((tb, LANE), lambda p, t: (t, 0)),
                  pl.BlockSpec((tb, LANE), lambda p, t: (t, 0))],
        out_specs=[pl.BlockSpec((2, tb, LANE), lambda p, t: (p, t, 0)),
                   pl.BlockSpec((2, ncb, LANE, LANE), lambda p, t: (p, t, 0, 0))],
        out_shape=[jax.ShapeDtypeStruct((N_HEADS, t_len, LANE), F32),
                   jax.ShapeDtypeStruct((N_HEADS, nc, LANE, LANE), F32)],
        scratch_shapes=[pltpu.VMEM((2, LANE, LANE), F32)],
        compiler_params=_params(("arbitrary", "arbitrary")),
    )(pb, pb, pb, cos_t, sin_t)


def retention_backward(pb, cos_t, sin_t, do, rsave, dpb, t_len):
    tb = _time_block(t_len)
    ncb, ntb = tb // CHUNK, t_len // tb

    def body(rq_ref, rk_ref, rv_ref, cos_ref, sin_ref, do_ref, rsave_ref, dpb_in,
             dpb_ref, dq_buf, dk_buf, dv_buf, dst_ref, sems):
        del dpb_in
        p, t = pl.program_id(0), pl.program_id(1)
        step = p * ntb + t
        rows = pl.ds(pl.multiple_of((ntb - 1 - t) * tb, tb), tb)

        def out_copies(sl):
            return [pltpu.make_async_copy(dq_buf.at[sl], dpb_ref.at[CB_RQ + p, rows], sems.at[sl, 0]),
                    pltpu.make_async_copy(dk_buf.at[sl], dpb_ref.at[CB_RK + p, rows], sems.at[sl, 1]),
                    pltpu.make_async_copy(dv_buf.at[sl], dpb_ref.at[pl.ds(CB_RV + 2 * p, 2), rows], sems.at[sl, 2])]

        slot = _writeback_reserve(step, out_copies)

        @pl.when(t == 0)
        def _():
            dst_ref[...] = jnp.zeros_like(dst_ref)

        consts = [_ret_head_consts(2 * p + hd) for hd in range(2)]

        def chunk(i, carry):
            ci = ncb - 1 - i
            r = pl.ds(pl.multiple_of(ci * CHUNK, CHUNK), CHUNK)
            cs, sn = cos_ref[r, :], sin_ref[r, :]
            q = rq_ref[r, :]
            k = rk_ref[r, :]
            q = q * cs + _swap_halves(q) * sn
            k = (k * cs + _swap_halves(k) * sn) * RET_DK ** -0.5
            dq, dk = None, None
            for hd in range(2):
                dm, zeta, xi, cdec, dm_t = consts[hd]
                lm = _lane_mask(hd)
                qh, kh = q * lm, k * lm
                v = rv_ref[hd, r, :]
                g = do_ref[hd, r, :]
                st0 = rsave_ref[hd, ci]
                dst = dst_ref[hd]
                scores_t = _dot(kh, qh, _NT) * dm_t
                dsc = _dot(g, v, _NT) * dm
                dsc_t = _dot(v, g, _NT) * dm_t
                dqh = _dot(dsc, kh) + _dot(g, st0) * xi
                dkh = _dot(dsc_t, qh) + _dot(v, dst) * zeta
                dv_buf[slot, hd, r, :] = (_dot(scores_t, g) + _dot(kh * zeta, dst, _NT)).astype(dv_buf.dtype)
                dst_ref[hd] = dst * cdec + _dot(g, qh * xi, _TN)
                dq = dqh if dq is None else dq + dqh
                dk = dkh if dk is None else dk + dkh
            dk = dk * (RET_DK ** -0.5)
            dq_buf[slot, r, :] = (dq * cs - _swap_halves(dq) * sn).astype(dq_buf.dtype)
            dk_buf[slot, r, :] = (dk * cs - _swap_halves(dk) * sn).astype(dk_buf.dtype)
            return carry

        lax.fori_loop(0, ncb, chunk, 0, unroll=RET_UNROLL)
        _writeback_commit(step, (N_HEADS // 2) * ntb, slot, out_copies)

    any_spec = pl.BlockSpec(memory_space=pl.ANY)
    return pl.pallas_call(
        body, name="ret_bwd", grid=(N_HEADS // 2, ntb),
        in_specs=[pl.BlockSpec((None, tb, LANE), lambda p, t: (CB_RQ + p, ntb - 1 - t, 0)),
                  pl.BlockSpec((None, tb, LANE), lambda p, t: (CB_RK + p, ntb - 1 - t, 0)),
                  pl.BlockSpec((2, tb, LANE), lambda p, t: (CB_RV // 2 + p, ntb - 1 - t, 0)),
                  pl.BlockSpec((tb, LANE), lambda p, t: (ntb - 1 - t, 0)),
                  pl.BlockSpec((tb, LANE), lambda p, t: (ntb - 1 - t, 0)),
                  pl.BlockSpec((2, tb, LANE), lambda p, t: (p, ntb - 1 - t, 0)),
                  pl.BlockSpec((2, ncb, LANE, LANE), lambda p, t: (p, ntb - 1 - t, 0, 0)),
                  any_spec],
        out_specs=any_spec,
        out_shape=jax.ShapeDtypeStruct(dpb.shape, dpb.dtype),
        scratch_shapes=[pltpu.VMEM((2, tb, LANE), dpb.dtype), pltpu.VMEM((2, tb, LANE), dpb.dtype),
                        pltpu.VMEM((2, 2, tb, LANE), dpb.dtype), pltpu.VMEM((2, LANE, LANE), F32),
                        pltpu.SemaphoreType.DMA((2, 3))],
        input_output_aliases={7: 0},
        compiler_params=_params(("arbitrary", "arbitrary")),
    )(pb, pb, pb, cos_t, sin_t, do, rsave, dpb)


def _row_tile(t_len, want):
    return min(want, t_len)


PAIR_CB = 2 * CB_PER_SHARD


def proj_forward(h, wt, t_len):
    tm = _row_tile(t_len, 1024)

    def body(h_ref, w_ref, o_ref):
        acc = _dot(h_ref[...], w_ref[...], _NT)
        for jj in range(PAIR_CB):
            o_ref[jj] = acc[:, jj * LANE:(jj + 1) * LANE]

    return pl.pallas_call(
        body, name="proj_fwd", grid=(N_DEV // 2, t_len // tm),
        in_specs=[pl.BlockSpec((tm, D_MODEL), lambda j, i: (i, 0)),
                  pl.BlockSpec((PAIR_CB * LANE, D_MODEL), lambda j, i: (j, 0))],
        out_specs=pl.BlockSpec((PAIR_CB, tm, LANE), lambda j, i: (j, i, 0)),
        out_shape=jax.ShapeDtypeStruct((N_CB, t_len, LANE), F32),
        compiler_params=_params(("arbitrary", "arbitrary")),
    )(h, wt)


def proj_backward_input(dpb, wt, token, x, dy, norm_g, scale1p, t_len):
    tm = _row_tile(t_len, 512)

    def body(a_ref, wt_hbm, token_ref, x_ref, dy_ref, g_ref, sc_ref, gx_ref, vec_ref, w_ref, sem):
        del token_ref
        i = pl.program_id(0)

        @pl.when(i == 0)
        def _():
            cp = pltpu.make_async_copy(wt_hbm, w_ref, sem)
            cp.start()
            cp.wait()

        a = jnp.concatenate([a_ref[jj].astype(_BF) for jj in range(N_CB)], axis=1)
        dhv = _dot(a, w_ref[...])
        xv, g, sc = x_ref[...], g_ref[...], sc_ref[...]
        r = lax.rsqrt(jnp.mean(xv * xv, axis=-1, keepdims=True) + EPS)
        xn = xv * r
        dxn = dhv * (g * sc)
        gx_ref[...] = dy_ref[...] + r * dxn - xn * (r * r) * jnp.mean(xv * dxn, axis=-1, keepdims=True)
        t = dhv * xn
        _acc_rows(vec_ref, i, [jnp.sum(t * sc, axis=0, keepdims=True),
                               jnp.sum(t * g, axis=0, keepdims=True),
                               jnp.sum(dhv, axis=0, keepdims=True)])

    row = pl.BlockSpec((tm, D_MODEL), lambda i: (i, 0))
    return pl.pallas_call(
        body, name="proj_bwd_input", grid=(t_len // tm,),
        in_specs=[pl.BlockSpec((N_CB, tm, LANE), lambda i: (0, i, 0)),
                  pl.BlockSpec(memory_space=pl.ANY),
                  pl.BlockSpec(token.shape, lambda i: (0, 0)),
                  row, row, _vec_spec(), _vec_spec()],
        out_specs=[row, pl.BlockSpec((8, D_MODEL), lambda i: (0, 0))],
        out_shape=[jax.ShapeDtypeStruct((t_len, D_MODEL), F32), jax.ShapeDtypeStruct((8, D_MODEL), F32)],
        scratch_shapes=[pltpu.VMEM(wt.shape, wt.dtype), pltpu.SemaphoreType.DMA],
        compiler_params=_params(("arbitrary",)),
    )(dpb, wt, token, x, dy, norm_g, scale1p)


def proj_backward_weight(h_t, dpb, t_len):
    tk = _row_tile(t_len, 2048)

    def body(h_ref, b_ref, o_ref):
        k = pl.program_id(1)
        b = jnp.concatenate([b_ref[jj].astype(_BF) for jj in range(PAIR_CB)], axis=1)
        part = _dot(h_ref[...], b)

        @pl.when(k == 0)
        def _():
            for s in range(2):
                o_ref[s] = part[:, s * SHARD_IN:(s + 1) * SHARD_IN]

        @pl.when(k > 0)
        def _():
            for s in range(2):
                o_ref[s] = o_ref[s] + part[:, s * SHARD_IN:(s + 1) * SHARD_IN]

    return pl.pallas_call(
        body, name="proj_bwd_weight", grid=(N_DEV // 2, t_len // tk),
        in_specs=[pl.BlockSpec((D_MODEL, tk), lambda j, k: (0, k)),
                  pl.BlockSpec((PAIR_CB, tk, LANE), lambda j, k: (j, k, 0))],
        out_specs=pl.BlockSpec((2, D_MODEL, SHARD_IN), lambda j, k: (j, 0, 0)),
        out_shape=jax.ShapeDtypeStruct((N_DEV, D_MODEL, SHARD_IN), F32),
        compiler_params=_params(("arbitrary", "arbitrary")),
    )(h_t, dpb)


def sibling_blocks(g_in, c_idx):
    tr = D_MODEL

    def body(c_ref, g_ref, o_ref):
        del c_ref
        o_ref[...] = g_ref[...].astype(o_ref.dtype)

    return pl.pallas_call(
        body, name="sibling_blocks",
        grid_spec=pltpu.PrefetchScalarGridSpec(
            num_scalar_prefetch=1, grid=(N_DEV // 2, D_MODEL // tr),
            in_specs=[pl.BlockSpec((None, tr, SHARD_IN), lambda q, i, c: (2 * q + 1 - c[0], i, 0))],
            out_specs=pl.BlockSpec((None, tr, SHARD_IN), lambda q, i, c: (q, i, 0))),
        out_shape=jax.ShapeDtypeStruct((N_DEV // 2, D_MODEL, SHARD_IN), _BF),
        compiler_params=_params(("arbitrary", "arbitrary")),
    )(c_idx, g_in)


def _vec_spec():
    return pl.BlockSpec((1, D_MODEL), lambda i: (0, 0))


def _acc_rows(ref, i, rows):
    @pl.when(i == 0)
    def _():
        ref[...] = jnp.zeros_like(ref)

    for n, row in enumerate(rows):
        ref[n:n + 1, :] = ref[n:n + 1, :] + row


def adaln_forward(x, norm_g, scale1p, shift, t_len):
    tm = _row_tile(t_len, 512)

    def body(x_ref, g_ref, sc_ref, sh_ref, h_ref, ht_ref):
        xv = x_ref[...]
        r = lax.rsqrt(jnp.mean(xv * xv, axis=-1, keepdims=True) + EPS)
        h = xv * r * g_ref[...] * sc_ref[...] + sh_ref[...]
        h_ref[...] = h.astype(h_ref.dtype)
        ht_ref[...] = h.T.astype(ht_ref.dtype)

    return pl.pallas_call(
        body, name="adaln_fwd", grid=(t_len // tm,),
        in_specs=[pl.BlockSpec((tm, D_MODEL), lambda i: (i, 0)), _vec_spec(), _vec_spec(), _vec_spec()],
        out_specs=[pl.BlockSpec((tm, D_MODEL), lambda i: (i, 0)), pl.BlockSpec((D_MODEL, tm), lambda i: (0, i))],
        out_shape=[jax.ShapeDtypeStruct((t_len, D_MODEL), _BF), jax.ShapeDtypeStruct((D_MODEL, t_len), _BF)],
        compiler_params=_params(("arbitrary",)),
    )(x, norm_g, scale1p, shift)


def _head_norm(o, g):
    r = lax.rsqrt(jnp.mean(o * o, axis=-1, keepdims=True) + EPS)
    return r, o * r * g


def _group_spec(tm, cb):
    return pl.BlockSpec((N_HEADS, tm, LANE), lambda i, cb=cb: (cb // N_HEADS, i, 0))


MID_FINAL_G, MID_GATE, MID_LOSS, MID_HG_G, MID_RET_G = range(5)


def middle(x, target, oa, ob, pb, wout, gate, final_g, hg_g, ret_g, t_len):
    tm = _row_tile(t_len, 256)
    n_steps = t_len // tm

    def body(x_ref, t_ref, oa_ref, ob_ref, hz_ref, rz_ref, ga_ref, gb_ref, w_ref, gate_ref, fg_ref, hg_ref, rg_ref,
             dy_ref, doa_ref, dob_ref, dw_ref, vec_ref, dpb_ref, m_scr, dm_scr, keep, bufs, sems):
        i = pl.program_id(0)
        rows = pl.ds(pl.multiple_of(i * tm, tm), tm)

        def group_copies(sl):
            return [pltpu.make_async_copy(bufs.at[sl, n], dpb_ref.at[pl.ds(cb, N_HEADS), rows], sems.at[sl, n])
                    for n, cb in enumerate((CB_HZ, CB_RZ, CB_GA, CB_GB))]
        sides = ((oa_ref, hz_ref, ga_ref, hg_ref, doa_ref), (ob_ref, rz_ref, gb_ref, rg_ref, dob_ref))
        for hh in range(N_HEADS):
            ls = slice(hh * LANE, (hh + 1) * LANE)
            acc = None
            for side, (o_ref, z_ref, gt_ref, g_ref, _) in enumerate(sides):
                o = o_ref[hh]
                rr = lax.rsqrt(jnp.mean(o * o, axis=-1, keepdims=True) + EPS)
                orr = o * rr
                zz = z_ref[hh]
                sz = _sigmoid(zz)
                sgt = _sigmoid(gt_ref[hh])
                keep[side, hh, 0] = orr
                keep[side, hh, 1] = sz
                keep[side, hh, 2] = sgt
                keep[side, hh, 3] = jnp.broadcast_to(rr, orr.shape)
                u = sgt * ((orr * g_ref[:, ls]) * (zz * sz))
                acc = u if acc is None else acc + u
            m_scr[:, ls] = acc.astype(m_scr.dtype)
        zv = _dot(m_scr[...], w_ref[...])
        gt, fg = gate_ref[...], fg_ref[...]
        y = x_ref[...] + gt * zv
        r = lax.rsqrt(jnp.mean(y * y, axis=-1, keepdims=True) + EPS)
        yn = y * r
        err = yn * fg - t_ref[...]
        loss = 0.5 * jnp.sum(jnp.mean(err * err, axis=-1, keepdims=True), axis=0, keepdims=True)
        dout = err * (1.0 / D_MODEL)
        gd = dout * fg
        dy = r * gd - yn * (r * r) * jnp.mean(y * gd, axis=-1, keepdims=True)
        dy_ref[...] = dy
        dz = (dy * gt).astype(_BF)
        dm_scr[...] = _dot(dz, w_ref[...], _NT)
        part = _dot(m_scr[...], dz, _TN)

        @pl.when(i == 0)
        def _():
            dw_ref[...] = part

        @pl.when(i > 0)
        def _():
            dw_ref[...] = dw_ref[...] + part

        slot = _writeback_reserve(i, group_copies)
        dg = [[], []]
        for hh in range(N_HEADS):
            ls = slice(hh * LANE, (hh + 1) * LANE)
            dmh = dm_scr[:, ls]
            for side, (o_ref, z_ref, gt_ref, g_ref, do_ref) in enumerate(sides):
                zz, g = z_ref[hh], g_ref[:, ls]
                orr, sz, sgt, rr = keep[side, hh, 0], keep[side, hh, 1], keep[side, hh, 2], keep[side, hh, 3]
                n = orr * g
                silu = zz * sz
                du = dmh * sgt
                bufs[slot, 2 + side, hh] = (dmh * (n * silu) * (sgt * (1.0 - sgt))).astype(bufs.dtype)
                bufs[slot, side, hh] = (du * n * (sz * (1.0 + zz * (1.0 - sz)))).astype(bufs.dtype)
                dn = du * silu
                dg[side].append(jnp.sum(dn * orr, axis=0, keepdims=True))
                gdn = dn * g
                do_ref[hh] = (rr * (gdn - orr * jnp.mean(orr * gdn, axis=-1, keepdims=True))).astype(do_ref.dtype)
        _acc_rows(vec_ref, i, [jnp.sum(dout * yn, axis=0, keepdims=True),
                               jnp.sum(dy * zv, axis=0, keepdims=True),
                               jnp.broadcast_to(loss, (1, D_MODEL)),
                               jnp.concatenate(dg[0], axis=1), jnp.concatenate(dg[1], axis=1)])
        _writeback_commit(i, n_steps, slot, group_copies)

    row = pl.BlockSpec((tm, D_MODEL), lambda i: (i, 0))
    head = pl.BlockSpec((N_HEADS, tm, LANE), lambda i: (0, i, 0))
    full = pl.BlockSpec((D_MODEL, D_MODEL), lambda i: (0, 0))
    return pl.pallas_call(
        body, name="middle", grid=(n_steps,),
        in_specs=[row, row, head, head, _group_spec(tm, CB_HZ), _group_spec(tm, CB_RZ), _group_spec(tm, CB_GA),
                  _group_spec(tm, CB_GB), full, _vec_spec(), _vec_spec(), _vec_spec(), _vec_spec()],
        out_specs=[row, head, head, full, pl.BlockSpec((8, D_MODEL), lambda i: (0, 0)),
                   pl.BlockSpec(memory_space=pl.ANY)],
        out_shape=[jax.ShapeDtypeStruct((t_len, D_MODEL), F32),
                   jax.ShapeDtypeStruct((N_HEADS, t_len, LANE), _BF),
                   jax.ShapeDtypeStruct((N_HEADS, t_len, LANE), _BF),
                   jax.ShapeDtypeStruct((D_MODEL, D_MODEL), F32),
                   jax.ShapeDtypeStruct((8, D_MODEL), F32),
                   jax.ShapeDtypeStruct((N_CB, t_len, LANE), _BF)],
        scratch_shapes=[pltpu.VMEM((tm, D_MODEL), _BF), pltpu.VMEM((tm, D_MODEL), F32),
                        pltpu.VMEM((2, N_HEADS, 4, tm, LANE), F32),
                        pltpu.VMEM((2, 4, N_HEADS, tm, LANE), _BF), pltpu.SemaphoreType.DMA((2, 4))],
        compiler_params=_params(("arbitrary",)),
    )(x, target, oa, ob, pb, pb, pb, pb, wout, gate, final_g, hg_g, ret_g)


def device_step(x, target, mod, lb, project, norm_g, hg_g, ret_g, final_g, c_idx=None, start_exchange=None):
    t_len = x.shape[0]
    shift, scale, gate = mod[:, :D_MODEL], mod[:, D_MODEL:2 * D_MODEL], mod[:, 2 * D_MODEL:]
    scale1p = 1.0 + scale
    cos_t, sin_t = _rope_tables(t_len)
    h, h_t = adaln_forward(x, norm_g, scale1p, shift, t_len)
    pb, wt, wout = project(h)
    oa, ssave, asave = hgrn_forward(pb, lb, t_len)
    ob, rsave = retention_forward(pb, cos_t, sin_t, t_len)
    dy, doa, dob, dwout, vec_mid, dpb = middle(x, target, oa, ob, pb, wout, gate, final_g, hg_g, ret_g, t_len)
    dpb, dlb = hgrn_backward(pb, lb, doa, ssave, asave, dpb, t_len)
    dpb = retention_backward(pb, cos_t, sin_t, dob, rsave, dpb, t_len)
    c_idx = jnp.zeros((1,), jnp.int32) if c_idx is None else c_idx
    dwin = proj_backward_weight(h_t, dpb, t_len)
    dwin_sib = sibling_blocks(dwin, c_idx)
    token, pending = (start_exchange(dwin, dwin_sib, dwout) if start_exchange
                      else (jnp.zeros((8, LANE), F32), None))
    grad_x, vec_ada = proj_backward_input(dpb, wt, token, x, dy, norm_g, scale1p, t_len)
    return grad_x, dwin, dwout, vec_mid, vec_ada, dlb, pending


PACK_ROWS = 16
ROW_NORM_G, ROW_LB, ROW_HG_G, ROW_RET_G, ROW_FINAL_G, ROW_SHIFT, ROW_SCALE, ROW_GATE, ROW_LOSS = range(9)


def _mesh_pos():
    return lax.axis_index("x"), lax.axis_index("y"), lax.axis_index("c")


def _lin(pos):
    return 4 * pos[0] + 2 * pos[1] + pos[2]


def _xor_peer(pos, k):
    return tuple(1 - p if (k >> s) & 1 else p for p, s in zip(pos, (2, 1, 0)))


def _other_chips(pos):
    x, y, _ = pos
    return [(1 - x, y), (x, 1 - y), (1 - x, 1 - y)]


def _remote(src, dst, send_sem, recv_sem, to):
    return pltpu.make_async_remote_copy(src_ref=src, dst_ref=dst, send_sem=send_sem, recv_sem=recv_sem,
                                        device_id=to, device_id_type=MESH)


def pre_exchange(c, w_ada, b_ada, logits, w_in, w_out):
    def body(c_ref, wada_ref, bada_ref, logit_ref, win_ref, wout_ref, mod_ref, scall_ref, lb_ref, wt_ref, wo_ref,
             cg_ref, modall_ref, parts_ref, send1, recv1, send2, recv2):
        pos = _mesh_pos()
        cv = c_ref[...]
        slot = lambda p: pl.ds(pl.multiple_of(8 * _lin(p), 8), 8)
        cg_ref[slot(pos), :] = jnp.broadcast_to(cv * _sigmoid(cv), (8, D_MODEL))
        lb_ref[...] = _sigmoid(logit_ref[0:1, :] - logit_ref[1:2, :])
        peers = [_xor_peer(pos, k) for k in range(1, N_DEV)]
        gather = [_remote(cg_ref.at[slot(pos)], cg_ref.at[slot(pos)], send1.at[n], recv1.at[n], p)
                  for n, p in enumerate(peers)]
        for cp in gather:
            cp.start()
        wt_ref[...] = win_ref[...].T.astype(wt_ref.dtype)
        wo_ref[...] = wout_ref[...].astype(wo_ref.dtype)
        for n, p in enumerate(peers):
            _remote(cg_ref.at[slot(p)], cg_ref.at[slot(p)], send1.at[n], recv1.at[n], p).wait_recv()
        modall_ref[...] = _dot(cg_ref[...], wada_ref[...])
        scatter = [_remote(modall_ref.at[slot(p)], parts_ref.at[slot(pos)], send2.at[n], recv2.at[n], p)
                   for n, p in enumerate(peers)]
        for cp in scatter:
            cp.start()
        parts_ref[slot(pos), :] = modall_ref[slot(pos), :]
        for n, p in enumerate(peers):
            _remote(modall_ref.at[slot(p)], parts_ref.at[slot(p)], send2.at[n], recv2.at[n], p).wait_recv()
        for cp in gather + scatter:
            cp.wait_send()
        for j in range(N_DEV):
            cols = slice(j * SHARD_ADA, (j + 1) * SHARD_ADA)
            mod_ref[:, cols] = parts_ref[8 * j:8 * j + 1, :] + bada_ref[:, cols]
            scall_ref[j:j + 1, :] = cg_ref[8 * j:8 * j + 1, :]

    vmem = pl.BlockSpec(memory_space=pltpu.VMEM)
    return pl.pallas_call(
        body, name="pre_exchange",
        in_specs=[vmem] * 6, out_specs=[vmem] * 5,
        out_shape=[jax.ShapeDtypeStruct((1, 3 * D_MODEL), F32), jax.ShapeDtypeStruct((N_DEV, D_MODEL), F32),
                   jax.ShapeDtypeStruct((1, D_MODEL), F32),
                   jax.ShapeDtypeStruct(w_in.shape[::-1], _BF), jax.ShapeDtypeStruct(w_out.shape, _BF)],
        scratch_shapes=[pltpu.VMEM((N_DEV * 8, D_MODEL), F32), pltpu.VMEM((N_DEV * 8, SHARD_ADA), F32),
                        pltpu.VMEM((N_DEV * 8, SHARD_ADA), F32)] + [pltpu.SemaphoreType.DMA((N_DEV - 1,))] * 4,
        compiler_params=pltpu.CompilerParams(vmem_limit_bytes=VMEM_LIMIT),
    )(c, w_ada, b_ada, logits, w_in, w_out)


def weight_gather(win_sh, wout_sh):
    def body(win_ref, wout_ref, wg_ref, woutg_ref, send, recv, local):
        pos = _mesh_pos()
        x, y, c = pos
        sibling = (x, y, 1 - c)

        def route(core):
            return [(x + (1 - core) * (1 - 2 * x), y + core * (1 - 2 * y)),
                    (x + core * (1 - 2 * x), y + (1 - core) * (1 - 2 * y)),
                    (1 - x, 1 - y)]

        chips, sib_chips = route(c), route(1 - c)
        mine, first, later = [], [], []
        for a, (src, out) in enumerate(((win_ref, wg_ref), (wout_ref, woutg_ref))):
            def copy(k, block, to, src_ref=None, a=a, out=out):
                dst = out.at[_lin(block)]
                return _remote(dst if src_ref is None else src_ref, dst, send.at[7 * a + k], recv.at[7 * a + k], to)
            mine.append(pltpu.make_async_copy(src, out.at[_lin(pos)], local.at[a]))
            first += [copy(0, pos, sibling, src), copy(1, pos, (*chips[0], c), src), copy(2, pos, (*chips[1], c), src)]
            later.append([[copy(3, (*chips[0], c), (*chips[1], c)), copy(4, (*chips[0], c), sibling)],
                          [copy(5, (*chips[1], c), sibling)],
                          [copy(6, (*chips[2], c), sibling)]])
        for cp in mine + first:
            cp.start()
        for j in range(3):
            for a, out in enumerate((wg_ref, woutg_ref)):
                dst = out.at[_lin((*chips[j], c))]
                _remote(dst, dst, send.at[7 * a + 1 + j], recv.at[7 * a + 1 + j], pos).wait_recv()
                for cp in later[a][j]:
                    cp.start()
        for a, out in enumerate((wg_ref, woutg_ref)):
            dst = out.at[_lin(sibling)]
            _remote(dst, dst, send.at[7 * a], recv.at[7 * a], pos).wait_recv()
            for j in range(3):
                dst = out.at[_lin((*sib_chips[j], 1 - c))]
                _remote(dst, dst, send.at[7 * a + 4 + j], recv.at[7 * a + 4 + j], pos).wait_recv()
        for cp in first + [cp for per_array in later for group in per_array for cp in group]:
            cp.wait_send()
        for cp in mine:
            cp.wait()

    any_spec = pl.BlockSpec(memory_space=pl.ANY)
    return pl.pallas_call(
        body, name="weight_gather",
        in_specs=[any_spec, any_spec], out_specs=[any_spec, any_spec],
        out_shape=[jax.ShapeDtypeStruct((N_DEV,) + win_sh.shape, win_sh.dtype),
                   jax.ShapeDtypeStruct((N_DEV,) + wout_sh.shape, wout_sh.dtype)],
        scratch_shapes=[pltpu.SemaphoreType.DMA((14,)), pltpu.SemaphoreType.DMA((14,)),
                        pltpu.SemaphoreType.DMA((2,))],
    )(win_sh, wout_sh)


def grad_pair_exchange(g_sib, g_out):
    def body(gsib_ref, gout_ref, ra_ref, rb_ref, send, recv):
        pos = _mesh_pos()
        x, y, c = pos
        sibling = (x, y, 1 - c)
        copies = []
        for q in range(4):
            copies.append(_remote(gsib_ref.at[q], ra_ref.at[q], send.at[q], recv.at[q], sibling))
            copies.append(_remote(gout_ref.at[2 * q + (1 - c)], rb_ref.at[q], send.at[4 + q], recv.at[4 + q], sibling))
        for cp in copies:
            cp.start()
        for cp in copies:
            cp.wait_recv()
        for cp in copies:
            cp.wait_send()

    any_spec = pl.BlockSpec(memory_space=pl.ANY)
    return pl.pallas_call(
        body, name="grad_pair_exchange",
        in_specs=[any_spec, any_spec], out_specs=[any_spec, any_spec],
        out_shape=[jax.ShapeDtypeStruct(g_sib.shape, g_sib.dtype), jax.ShapeDtypeStruct((4,) + g_out.shape[1:], F32)],
        scratch_shapes=[pltpu.SemaphoreType.DMA((8,)), pltpu.SemaphoreType.DMA((8,))],
    )(g_sib, g_out)


def pair_sum(g_in, ra, g_out, rb, c_idx):
    tr = D_MODEL

    def body(c_ref, gin_ref, ra_ref, gout_ref, rb_ref, sb_ref, sbo_ref):
        del c_ref
        sb_ref[...] = (gin_ref[...] + ra_ref[...].astype(F32)).astype(sb_ref.dtype)
        sbo_ref[...] = gout_ref[...] + rb_ref[...]

    n_i = D_MODEL // tr
    return pl.pallas_call(
        body, name="pair_sum",
        grid_spec=pltpu.PrefetchScalarGridSpec(
            num_scalar_prefetch=1, grid=(4, n_i),
            in_specs=[pl.BlockSpec((None, tr, SHARD_IN), lambda q, i, c: (2 * q + c[0], i, 0)),
                      pl.BlockSpec((None, tr, SHARD_IN), lambda q, i, c: (q, i, 0)),
                      pl.BlockSpec((None, SHARD_OUT // n_i, D_MODEL), lambda q, i, c: (2 * q + c[0], i, 0)),
                      pl.BlockSpec((None, SHARD_OUT // n_i, D_MODEL), lambda q, i, c: (q, i, 0))],
            out_specs=[pl.BlockSpec((None, tr, SHARD_IN), lambda q, i, c: (q, i, 0)),
                       pl.BlockSpec((None, SHARD_OUT // n_i, D_MODEL), lambda q, i, c: (q, i, 0))]),
        out_shape=[jax.ShapeDtypeStruct(ra.shape, _BF), jax.ShapeDtypeStruct(rb.shape, F32)],
        compiler_params=_params(("arbitrary", "arbitrary")),
    )(c_idx, g_in, ra, g_out, rb)


_HBM = pl.BlockSpec(memory_space=pltpu.HBM)
_SEM = pl.BlockSpec(memory_space=pltpu.SEMAPHORE)
_N_CHIP_COPIES = 6


def _chip_copies(sb_ref, sbo_ref, rc_ref, rco_ref, send, recv):
    pos = _mesh_pos()
    copies = []
    for a, (src, dst) in enumerate(((sb_ref, rc_ref), (sbo_ref, rco_ref))):
        for j, chip in enumerate(_other_chips(pos)):
            copies.append(_remote(src.at[2 * chip[0] + chip[1]], dst.at[j], send.at[3 * a + j], recv.at[3 * a + j],
                                  (*chip, pos[2])))
    return copies


def grad_chip_start(sb, sbo):
    def body(sb_ref, sbo_ref, rc_ref, rco_ref, send, recv, sb_thru, sbo_thru, rc_thru, rco_thru, token):
        del sb_thru, sbo_thru, rc_thru, rco_thru
        for cp in _chip_copies(sb_ref, sbo_ref, rc_ref, rco_ref, send, recv):
            cp.start()
        token[...] = jnp.zeros_like(token)

    hbm = lambda a: pltpu.with_memory_space_constraint(a, pltpu.HBM)
    rc = lax.empty((3,) + sb.shape[1:], sb.dtype)
    rco = lax.empty((3,) + sbo.shape[1:], sbo.dtype)
    return pl.pallas_call(
        body, name="grad_chip_start",
        in_specs=[_HBM] * 4,
        out_specs=[_SEM, _SEM, _HBM, _HBM, _HBM, _HBM, pl.BlockSpec(memory_space=pltpu.VMEM)],
        out_shape=[pltpu.SemaphoreType.DMA((_N_CHIP_COPIES,)), pltpu.SemaphoreType.DMA((_N_CHIP_COPIES,)),
                   pltpu.HBM(sb.shape, sb.dtype), pltpu.HBM(sbo.shape, sbo.dtype),
                   pltpu.HBM(rc.shape, rc.dtype), pltpu.HBM(rco.shape, rco.dtype),
                   jax.ShapeDtypeStruct((8, LANE), F32)],
        input_output_aliases={0: 2, 1: 3, 2: 4, 3: 5},
        compiler_params=pltpu.CompilerParams(has_side_effects=pltpu.SideEffectType.DATAFLOW_SIDE_EFFECTING),
    )(hbm(sb), hbm(sbo), hbm(rc), hbm(rco))


def grad_chip_wait(send, recv, sb, sbo, rc, rco, after):
    def body(sb_ref, sbo_ref, rc_ref, rco_ref, send, recv, after_ref, sb_o, sbo_o, rc_o, rco_o):
        del after_ref, sb_o, sbo_o, rc_o, rco_o
        for cp in _chip_copies(sb_ref, sbo_ref, rc_ref, rco_ref, send, recv):
            cp.wait_send()
            cp.wait_recv()

    return pl.pallas_call(
        body, name="grad_chip_wait",
        in_specs=[_HBM] * 4 + [_SEM, _SEM, pl.BlockSpec(memory_space=pl.ANY)],
        out_specs=[_HBM] * 4,
        out_shape=[pltpu.HBM(sb.shape, sb.dtype), pltpu.HBM(sbo.shape, sbo.dtype),
                   pltpu.HBM(rc.shape, rc.dtype), pltpu.HBM(rco.shape, rco.dtype)],
        input_output_aliases={0: 0, 1: 1, 2: 2, 3: 3},
        compiler_params=pltpu.CompilerParams(has_side_effects=pltpu.SideEffectType.DATAFLOW_SIDE_EFFECTING),
    )(sb, sbo, rc, rco, send, recv, after)


def pack_gather(pack):
    def body(pack_ref, packs_ref, psend, precv):
        pos = _mesh_pos()
        me = _lin(pos)
        packs_ref[me] = pack_ref[...]
        peers = [_xor_peer(pos, k) for k in range(1, N_DEV)]
        gather = [_remote(packs_ref.at[me], packs_ref.at[me], psend.at[n], precv.at[n], p) for n, p in enumerate(peers)]
        for cp in gather:
            cp.start()
        for n, p in enumerate(peers):
            _remote(packs_ref.at[_lin(p)], packs_ref.at[_lin(p)], psend.at[n], precv.at[n], p).wait_recv()
        for cp in gather:
            cp.wait_send()

    vmem = pl.BlockSpec(memory_space=pltpu.VMEM)
    return pl.pallas_call(
        body, name="pack_gather", in_specs=[vmem], out_specs=vmem,
        out_shape=jax.ShapeDtypeStruct((N_DEV,) + pack.shape, F32),
        scratch_shapes=[pltpu.SemaphoreType.DMA((N_DEV - 1,)), pltpu.SemaphoreType.DMA((N_DEV - 1,))],
    )(pack)


def pack_rows(vec_mid, vec_ada, dlb):
    def body(mid_ref, ada_ref, dlb_ref, o_ref):
        mid = lambda r: mid_ref[r:r + 1, :]
        rows = [ada_ref[0:1, :], dlb_ref[...], mid(MID_HG_G), mid(MID_RET_G), mid(MID_FINAL_G),
                ada_ref[2:3, :], ada_ref[1:2, :], mid(MID_GATE), mid(MID_LOSS)]
        o_ref[...] = jnp.zeros_like(o_ref)
        for n, row in enumerate(rows):
            o_ref[n:n + 1, :] = row

    vmem = pl.BlockSpec(memory_space=pltpu.VMEM)
    return pl.pallas_call(body, name="pack_rows", in_specs=[vmem] * 3, out_specs=vmem,
                          out_shape=jax.ShapeDtypeStruct((PACK_ROWS, D_MODEL), F32))(vec_mid, vec_ada, dlb)


def _adamw(w, g, m, v):
    m = ADAM_B1 * m + (1.0 - ADAM_B1) * g
    v = ADAM_B2 * v + (1.0 - ADAM_B2) * (g * g)
    m_hat = m / (1.0 - ADAM_B1 ** ADAM_STEP)
    v_hat = v / (1.0 - ADAM_B2 ** ADAM_STEP)
    delta = -ADAM_LR * (m_hat / (jnp.sqrt(v_hat) + ADAM_EPS) + ADAM_WD * w)
    return delta, m, v


def adam_shard(chip_idx, own, parts, w, m, v, name):
    rows, cols = w.shape
    tr = min(rows, 256)

    def body(chip_ref, p0, p1, p2, p3, w_ref, m_ref, v_ref, g_ref, d_ref, nm_ref, nv_ref):
        del chip_ref
        g = ((p0[...].astype(F32) + p1[...].astype(F32)) + p2[...].astype(F32)) + p3[...].astype(F32)
        g_ref[...] = g
        d_ref[...], nm_ref[...], nv_ref[...] = _adamw(w_ref[...], g, m_ref[...], v_ref[...])

    part = lambda q: pl.BlockSpec((None, tr, cols), lambda i, chip, q=q: (q, i, 0))
    tile = pl.BlockSpec((tr, cols), lambda i, chip: (i, 0))
    return pl.pallas_call(
        body, name=name,
        grid_spec=pltpu.PrefetchScalarGridSpec(
            num_scalar_prefetch=1, grid=(rows // tr,),
            in_specs=[pl.BlockSpec((None, tr, cols), lambda i, chip: (chip[0], i, 0)), part(0), part(1), part(2),
                      tile, tile, tile],
            out_specs=[tile] * 4),
        out_shape=[jax.ShapeDtypeStruct(w.shape, F32)] * 4,
        compiler_params=_params(("arbitrary",)),
    )(chip_idx, own, parts, parts, parts, w, m, v)


def adam_ada(sc_t, dmod_all, me_idx, w, m, v):
    def body(me_ref, sc_ref, dm_ref, w_ref, m_ref, v_ref, g_ref, d_ref, nm_ref, nv_ref):
        del me_ref
        g = _dot_f32(sc_ref[...], dm_ref[...])
        g_ref[...] = g
        d_ref[...], nm_ref[...], nv_ref[...] = _adamw(w_ref[...], g, m_ref[...], v_ref[...])

    full = pl.BlockSpec(w.shape, lambda i, me: (0, 0))
    return pl.pallas_call(
        body, name="adam_ada",
        grid_spec=pltpu.PrefetchScalarGridSpec(
            num_scalar_prefetch=1, grid=(1,),
            in_specs=[pl.BlockSpec(sc_t.shape, lambda i, me: (0, 0)),
                      pl.BlockSpec((LANE, SHARD_ADA), lambda i, me: (0, me[0])), full, full, full],
            out_specs=[full] * 4),
        out_shape=[jax.ShapeDtypeStruct(w.shape, F32)] * 4,
        compiler_params=_params(("arbitrary",)),
    )(me_idx, sc_t, dmod_all, w, m, v)


def adam_vectors(packs, lb, params, ms, vs):
    n = len(params)

    def body(*refs):
        packs_ref, lb_ref = refs[0], refs[1]
        w_refs, m_refs, v_refs = refs[2:2 + n], refs[2 + n:2 + 2 * n], refs[2 + 2 * n:2 + 3 * n]
        loss_ref = refs[2 + 3 * n]
        outs = refs[3 + 3 * n:3 + 7 * n]
        tot_ref = refs[3 + 7 * n]
        tot = packs_ref[0]
        for d in range(1, N_DEV):
            tot = tot + packs_ref[d]
        tot_ref[...] = tot
        row = lambda r: tot_ref[r:r + 1, :]
        lbv = lb_ref[...]
        dl0 = row(ROW_LB) * lbv * (1.0 - lbv)
        grads = [[row(ROW_NORM_G)],
                 [jnp.concatenate([row(ROW_SHIFT), row(ROW_SCALE), row(ROW_GATE)], axis=1)],
                 [dl0, -dl0],
                 [row(ROW_HG_G)], [row(ROW_RET_G)], [row(ROW_FINAL_G)]]
        loss_ref[...] = tot_ref[ROW_LOSS:ROW_LOSS + 1, 0:LANE]
        for j, g_rows in enumerate(grads):
            for r, g in enumerate(g_rows):
                rs = slice(r, r + 1)
                d, nm, nv = _adamw(w_refs[j][rs, :], g, m_refs[j][rs, :], v_refs[j][rs, :])
                outs[4 * j][rs, :] = g
                outs[4 * j + 1][rs, :] = d
                outs[4 * j + 2][rs, :] = nm
                outs[4 * j + 3][rs, :] = nv

    vmem = pl.BlockSpec(memory_space=pltpu.VMEM)
    out_shape = [jax.ShapeDtypeStruct((1, LANE), F32)]
    for w in params:
        out_shape += [jax.ShapeDtypeStruct(w.shape, F32)] * 4
    return pl.pallas_call(
        body, name="adam_vectors", in_specs=[vmem] * (2 + 3 * n), out_specs=[vmem] * len(out_shape),
        out_shape=out_shape, scratch_shapes=[pltpu.VMEM((PACK_ROWS, D_MODEL), F32)],
    )(packs, lb, *params, *ms, *vs)


def kernel(x, c, norm_g, w_ada, b_ada, w_in, hg_lb_logits, hg_norm_g, ret_norm_g, w_out, final_g, loss_target, m_norm_g, m_w_ada, m_b_ada, m_w_in, m_hg_lb_logits, m_hg_norm_g, m_ret_norm_g, m_w_out, m_final_g, v_norm_g, v_w_ada, v_b_ada, v_w_in, v_hg_lb_logits, v_hg_norm_g, v_ret_norm_g, v_w_out, v_final_g):
    pos = _mesh_pos()
    me_idx = jnp.reshape(_lin(pos), (1,)).astype(jnp.int32)
    c_idx = jnp.reshape(pos[2], (1,)).astype(jnp.int32)
    vec = lambda a: a.reshape(1, D_MODEL)

    mod, scall, lb, wt_sh, wout_sh = pre_exchange(c, w_ada[0], b_ada, hg_lb_logits, w_in[0], w_out[0])
    chip_idx = jnp.reshape(2 * pos[0] + pos[1], (1,)).astype(jnp.int32)

    def project(h):
        wtg, woutg = weight_gather(wt_sh, wout_sh)
        wt = wtg.reshape(D_IN, D_MODEL)
        return proj_forward(h, wt, h.shape[0]), wt, woutg.reshape(D_MODEL, D_MODEL)

    def start_exchange(dwin, dwin_sib, dwout):
        dwout = dwout.reshape(N_DEV, SHARD_OUT, D_MODEL)
        ra, rb = grad_pair_exchange(dwin_sib, dwout)
        sb, sbo = pair_sum(dwin, ra, dwout, rb, c_idx)
        send, recv, sb, sbo, rc, rco, token = grad_chip_start(sb, sbo)
        return token, (send, recv, sb, sbo, rc, rco)

    grad_x, _, _, vec_mid, vec_ada, dlb, pending = device_step(
        x[0], loss_target[0], mod, lb, project, norm_g, hg_norm_g, ret_norm_g, vec(final_g), c_idx, start_exchange)
    packs = pack_gather(pack_rows(vec_mid, vec_ada, dlb))
    dmod_all = packs[:, ROW_SHIFT:ROW_GATE + 1, :].reshape(N_DEV, 3 * D_MODEL)
    dmod_all = jnp.pad(dmod_all, ((0, LANE - N_DEV), (0, 0)))
    sc_t = jnp.pad(scall.T, ((0, 0), (0, LANE - N_DEV)))
    g_ada, d_ada, nm_ada, nv_ada = adam_ada(sc_t, dmod_all, me_idx, w_ada[0], m_w_ada[0], v_w_ada[0])
    small = adam_vectors(
        packs, lb,
        (norm_g, b_ada, hg_lb_logits, hg_norm_g, ret_norm_g, vec(final_g)),
        (m_norm_g, m_b_ada, m_hg_lb_logits, m_hg_norm_g, m_ret_norm_g, vec(m_final_g)),
        (v_norm_g, v_b_ada, v_hg_lb_logits, v_hg_norm_g, v_ret_norm_g, vec(v_final_g)))
    loss = small[0][0, 0]
    sb, sbo, rc, rco = grad_chip_wait(*pending, small[0])
    g_in, d_in, nm_in, nv_in = adam_shard(chip_idx, sb, rc, w_in[0], m_w_in[0], v_w_in[0], "adam_w_in")
    g_out, d_out, nm_out, nv_out = adam_shard(chip_idx, sbo, rco, w_out[0], m_w_out[0], v_w_out[0], "adam_w_out")
    (g_ng, d_ng, nm_ng, nv_ng), (g_b, d_b, nm_b, nv_b), (g_lb, d_lb, nm_lb, nv_lb), (g_hg, d_hg, nm_hg, nv_hg), \
        (g_rg, d_rg, nm_rg, nv_rg), (g_fg, d_fg, nm_fg, nv_fg) = [small[1 + 4 * j:5 + 4 * j] for j in range(6)]
    flat = lambda a: a.reshape(D_MODEL)

    def group(ng, ada, b, win, lbl, hg, rg, wo, fg):
        return (ng, ada[None], b, win[None], lbl, hg, rg, wo[None], flat(fg))

    return (loss, grad_x[None],
            *group(g_ng, g_ada, g_b, g_in, g_lb, g_hg, g_rg, g_out, g_fg),
            *group(d_ng, d_ada, d_b, d_in, d_lb, d_hg, d_rg, d_out, d_fg),
            *group(nm_ng, nm_ada, nm_b, nm_in, nm_lb, nm_hg, nm_rg, nm_out, nm_fg),
            *group(nv_ng, nv_ada, nv_b, nv_in, nv_lb, nv_hg, nv_rg, nv_out, nv_fg))
```

```python
import functools

import numpy as np
import jax
import jax.numpy as jnp
from jax import lax
from jax.experimental import pallas as pl
from jax.experimental.pallas import tpu as pltpu

F32 = jnp.float32
_BF = jnp.bfloat16

D_MODEL = 1024
N_HEADS = 8
LANE = 128
RET_DK = 64
D_IN = 9216
N_DEV = 8
SHARD_IN = D_IN // N_DEV
SHARD_ADA = 3 * D_MODEL // N_DEV
SHARD_OUT = D_MODEL // N_DEV
N_CB = D_IN // LANE
CB_PER_SHARD = SHARD_IN // LANE
CHUNK = 128
N_LEVELS = 7
EPS = 1e-6
LOG2_E = float(np.log2(np.e))
ROPE_BASE = 10000.0
CB_HQ, CB_HF, CB_HI, CB_HZ, CB_RQ, CB_RK, CB_RV, CB_RZ, CB_GA, CB_GB = 0, 8, 16, 24, 32, 36, 40, 48, 56, 64
VMEM_LIMIT = 56 * 1024 * 1024

ADAM_LR, ADAM_B1, ADAM_B2, ADAM_EPS, ADAM_WD, ADAM_STEP = 0.001, 0.9, 0.999, 1e-08, 0.01, 10

_NN = (((1,), (0,)), ((), ()))
_NT = (((1,), (1,)), ((), ()))
_TN = (((0,), (0,)), ((), ()))
MESH = pl.DeviceIdType.MESH


def _dot(a, b, dims=_NN):
    return lax.dot_general(a.astype(_BF), b.astype(_BF), dims, preferred_element_type=F32)


def _split2(a):
    hi = a.astype(_BF)
    lo = (a - hi.astype(F32)).astype(_BF)
    return jnp.concatenate([hi, lo], axis=1)


def _dot_sel(sel, a):
    n = a.shape[1]
    r = lax.dot_general(sel.astype(_BF), _split2(a), _NN, preferred_element_type=F32)
    return r[:, :n] + r[:, n:]


def _dot_f32(a, b):
    def pieces(v):
        p1 = v.astype(_BF)
        r1 = v - p1.astype(F32)
        p2 = r1.astype(_BF)
        p3 = (r1 - p2.astype(F32)).astype(_BF)
        return p1, p2, p3
    a1, a2, a3 = pieces(a)
    b1, b2, b3 = pieces(b)
    d = lambda u, v: lax.dot_general(u, v, _NN, preferred_element_type=F32)
    return ((d(a1, b3) + d(a2, b2) + d(a3, b1)) + (d(a1, b2) + d(a2, b1))) + d(a1, b1)


def _sigmoid(v):
    return 1.0 / (1.0 + jnp.exp(-v))


def _params(sem=None):
    return pltpu.CompilerParams(dimension_semantics=sem, vmem_limit_bytes=VMEM_LIMIT)


def _hgrn_consts():
    c, nl = CHUNK, N_LEVELS
    t = np.arange(c)[:, None]
    j = np.arange(c)[None, :]
    sel = [j <= t]
    masks = [j == t]
    for l in range(1, nl + 1):
        m = ((t >> l) << l) + (1 << (l - 1)) - 1
        sec = t > m
        sel.append(np.where(sec, (j > m) & (j <= t), (j > t) & (j <= m)))
        same = (t >> l) == (j >> l)
        masks.append(same & sec & (j <= m))
    sel.append(j > t)
    sel = np.concatenate(sel, 0).astype(np.float32)
    masks = np.stack(masks).astype(np.float32)
    sgn = np.stack([np.where((t & (1 << (l - 1))) != 0, 1.0, -1.0) * np.ones((1, LANE)) for l in range(3, nl + 1)])
    return dict(tri=jnp.asarray(sel[:c], _BF),
                lvl=jnp.asarray(masks, F32),
                sgn=jnp.asarray(sgn, F32),
                sel_t=jnp.asarray(sel.T, _BF),
                lvl_b=jnp.asarray(masks, _BF),
                lvlt_b=jnp.asarray(np.swapaxes(masks, 1, 2), _BF))


def _level_exponents(b, logf, b_scr, sgn_ref):
    c = CHUNK
    b_scr[...] = b
    row = lax.broadcasted_iota(jnp.int32, (c, LANE), 0)
    nxt = pltpu.roll(logf, c - 1, 0)
    prv = pltpu.roll(logf, 1, 0)
    r4 = row & 3
    out = [jnp.where((row & 1) == 1, logf, 0.0),
           jnp.where(r4 == 0, nxt, jnp.where(r4 == 1, 0.0, jnp.where(r4 == 2, logf, logf + prv)))]
    for l in range(3, N_LEVELS + 1):
        size, half = 1 << l, 1 << (l - 1)
        ref = jnp.concatenate([jnp.broadcast_to(b_scr[i * size + half - 1:i * size + half, :], (size, LANE))
                               for i in range(c // size)], axis=0)
        out.append((b - ref) * sgn_ref[l - 3])
    return out


def _hgrn_chunk(hq, hf, hi, lbv, tri_ref, sgn_ref, b_scr):
    sq = _sigmoid(hq)
    q = hq * sq
    sg = _sigmoid(hf)
    omlb = 1.0 - lbv
    f = lbv + omlb * sg
    k = 1.0 - f
    logf = jnp.log(f) * LOG2_E
    b = _dot_sel(tri_ref[...], logf)
    bc = jnp.sum(logf, axis=0, keepdims=True)
    lev = [None] + [jnp.exp2(e) for e in _level_exponents(b, logf, b_scr, sgn_ref)]
    return dict(sq=sq, q=q, sg=sg, omlb=omlb, f=f, k=k, v=hi, eb=jnp.exp2(b), erem=jnp.exp2(bc - b),
                ebc=jnp.exp2(bc), lev=lev)


def _blockdiag(a, b):
    z = jnp.zeros_like(a)
    return jnp.concatenate([jnp.concatenate([a, z], axis=1), jnp.concatenate([z, b], axis=1)], axis=0)


def _level_operands(a):
    q, k = a["q"].astype(_BF), a["k"].astype(_BF)
    lev = [None] + [a["lev"][l].astype(_BF) for l in range(1, N_LEVELS + 1)]
    ql = [q] + [q * lev[l] for l in range(1, N_LEVELS + 1)]
    kl = [k] + [k * lev[l] for l in range(1, N_LEVELS + 1)]
    pairs = range(0, N_LEVELS + 1, 2)
    return ([jnp.concatenate([ql[l], ql[l + 1]], axis=1) for l in pairs], [_blockdiag(kl[l], kl[l + 1]) for l in pairs],
            ql, kl)


def _hgrn_scores(a, lvl_ref, q_pairs, k_diags):
    acc = None
    for n, (qp, kd) in enumerate(zip(q_pairs, k_diags)):
        both = lax.dot_general(qp, kd, _NT, preferred_element_type=F32)
        part = lvl_ref[2 * n] * both[:, :CHUNK] + lvl_ref[2 * n + 1] * both[:, CHUNK:]
        acc = part if acc is None else acc + part
    return acc


SCAN_UNROLL = 16
RET_UNROLL = 8


def _writeback_reserve(step, make_copies):
    slot = step % 2

    @pl.when(step >= 2)
    def _():
        for cp in make_copies(slot):
            cp.wait()

    return slot


def _writeback_commit(step, n_steps, slot, make_copies):
    for cp in make_copies(slot):
        cp.start()

    @pl.when(step == n_steps - 1)
    def _():
        for cp in make_copies(slot):
            cp.wait()
        if n_steps > 1:
            for cp in make_copies(1 - slot):
                cp.wait()


def _resident(const):
    zeros = (0,) * const.ndim
    return pl.BlockSpec(const.shape, lambda p, t: zeros)


def _time_block(t_len):
    return min(t_len, 2048)


def hgrn_forward(pb, lb, t_len):
    nc = t_len // CHUNK
    tb = _time_block(t_len)
    ncb = tb // CHUNK
    consts = _hgrn_consts()
    operands = [consts[n] for n in ("tri", "lvl", "sgn")]

    def body(hq_ref, hf_ref, hi_ref, lb_ref, tri_ref, lvl_ref, sgn_ref, o_ref, ssave_ref, asave_ref, st_ref, b_scr):
        @pl.when(pl.program_id(1) == 0)
        def _():
            st_ref[...] = jnp.zeros_like(st_ref)

        def chunk(ci, carry):
            r = pl.ds(pl.multiple_of(ci * CHUNK, CHUNK), CHUNK)
            for hd in range(2):
                lbv = lb_ref[:, hd * LANE:(hd + 1) * LANE]
                a = _hgrn_chunk(hq_ref[hd, r, :], hf_ref[hd, r, :], hi_ref[hd, r, :], lbv, tri_ref, sgn_ref,
                                b_scr.at[hd])
                q_pairs, k_diags, _, _ = _level_operands(a)
                st = st_ref[hd]
                ssave_ref[hd, ci] = st
                scores = _hgrn_scores(a, lvl_ref, q_pairs, k_diags).astype(asave_ref.dtype)
                asave_ref[hd, ci] = scores
                o_ref[hd, r, :] = _dot(a["q"] * a["eb"], st, _NT) + _dot(scores, a["v"])
                st_ref[hd] = st * a["ebc"] + _dot(a["v"], a["k"] * a["erem"], _TN)
            return carry

        lax.fori_loop(0, ncb, chunk, 0, unroll=SCAN_UNROLL)

    pair = lambda base: pl.BlockSpec((2, tb, LANE), lambda p, t, base=base: (base // 2 + p, t, 0))
    per_chunk = pl.BlockSpec((2, ncb, LANE, LANE), lambda p, t: (p, t, 0, 0))
    return pl.pallas_call(
        body, name="hgrn_fwd", grid=(N_HEADS // 2, t_len // tb),
        in_specs=[pair(CB_HQ), pair(CB_HF), pair(CB_HI),
                  pl.BlockSpec((1, 2 * LANE), lambda p, t: (0, p))] + [_resident(c) for c in operands],
        out_specs=[pl.BlockSpec((2, tb, LANE), lambda p, t: (p, t, 0)), per_chunk, per_chunk],
        out_shape=[jax.ShapeDtypeStruct((N_HEADS, t_len, LANE), F32),
                   jax.ShapeDtypeStruct((N_HEADS, nc, LANE, LANE), F32),
                   jax.ShapeDtypeStruct((N_HEADS, nc, CHUNK, CHUNK), _BF)],
        scratch_shapes=[pltpu.VMEM((2, LANE, LANE), F32), pltpu.VMEM((2, CHUNK, LANE), F32)],
        compiler_params=_params(("arbitrary", "arbitrary")),
    )(pb, pb, pb, lb, *operands)


def hgrn_backward(pb, lb, do, ssave, asave, dpb, t_len):
    tb = _time_block(t_len)
    ncb, ntb = tb // CHUNK, t_len // tb
    consts = _hgrn_consts()
    operands = [consts[n] for n in ("tri", "sgn", "sel_t", "lvl_b", "lvlt_b")]

    def body(hq_ref, hf_ref, hi_ref, lb_ref, do_ref, ssave_ref, asave_ref, tri_ref, sgn_ref, selt_ref, lvlb_ref,
             lvltb_ref, dpb_in, dpb_ref, dlb_ref, dq_buf, df_buf, di_buf, dst_ref, b_scr, sems):
        del dpb_in
        p, t = pl.program_id(0), pl.program_id(1)
        step = p * ntb + t
        rows = pl.ds(pl.multiple_of((ntb - 1 - t) * tb, tb), tb)

        def out_copies(sl):
            return [pltpu.make_async_copy(buf.at[sl], dpb_ref.at[pl.ds(base + 2 * p, 2), rows], sems.at[sl, n])
                    for n, (buf, base) in enumerate(((dq_buf, CB_HQ), (df_buf, CB_HF), (di_buf, CB_HI)))]

        slot = _writeback_reserve(step, out_copies)

        @pl.when(t == 0)
        def _():
            dst_ref[...] = jnp.zeros_like(dst_ref)
            dlb_ref[...] = jnp.zeros_like(dlb_ref)

        def chunk(i, carry):
            ci = ncb - 1 - i
            r = pl.ds(pl.multiple_of(ci * CHUNK, CHUNK), CHUNK)
            for hd in range(2):
                head_chunk(hd, ci, r)
            return carry

        def head_chunk(hd, ci, r):
            lbv = lb_ref[:, hd * LANE:(hd + 1) * LANE]
            hq = hq_ref[hd, r, :]
            a = _hgrn_chunk(hq, hf_ref[hd, r, :], hi_ref[hd, r, :], lbv, tri_ref, sgn_ref, b_scr.at[hd])
            _, k_diags, ql, kl = _level_operands(a)
            q, k, v = a["q"], a["k"], a["v"]
            g = do_ref[hd, r, :]
            st0 = ssave_ref[hd, ci]
            dst = dst_ref[hd]
            scores = asave_ref[hd, ci]
            da = _dot(g, v, _NT)
            da_t = _dot(v, g, _NT)
            kb = k * a["erem"]
            qb = q * a["eb"]
            dv = _dot(scores, g, _TN) + _dot(kb, dst, _NT)
            dq_inter = _dot(g, st0) * a["eb"]
            dk_state = _dot(v, dst) * a["erem"]
            dq, dk = dq_inter, dk_state
            de = [q * dq_inter]
            da_b, dat_b = da.astype(_BF), da_t.astype(_BF)
            for n in range(len(k_diags)):
                l0, l1 = 2 * n, 2 * n + 1
                da_pair = jnp.concatenate([lvlb_ref[l0] * da_b, lvlb_ref[l1] * da_b], axis=1)
                dat_pair = jnp.concatenate([lvltb_ref[l0] * dat_b, lvltb_ref[l1] * dat_b], axis=1)
                dq_both = lax.dot_general(da_pair, k_diags[n], _NN, preferred_element_type=F32)
                dk_both = lax.dot_general(dat_pair, _blockdiag(ql[l0], ql[l1]), _NN, preferred_element_type=F32)
                for l, cols in ((l0, slice(0, LANE)), (l1, slice(LANE, 2 * LANE))):
                    dql, dkl = dq_both[:, cols], dk_both[:, cols]
                    if l > 0:
                        e = a["lev"][l]
                        dql, dkl = dql * e, dkl * e
                        de.append(q * dql + k * dkl)
                    dq = dq + dql
                    dk = dk + dkl
            de.append(k * dk_state)
            dst_ref[hd] = dst * a["ebc"] + _dot(g, qb, _TN)
            dbc = jnp.sum(dst * st0, axis=0, keepdims=True) * a["ebc"]
            de2 = lax.dot_general(selt_ref[...], _split2(jnp.concatenate(de, axis=0)), _NN,
                                  preferred_element_type=F32)
            dlogf = de2[:, :LANE] + de2[:, LANE:] + dbc
            sq, sg = a["sq"], a["sg"]
            df = dlogf / a["f"] - dk
            dq_buf[slot, hd, r, :] = (dq * (sq * (1.0 + hq * (1.0 - sq)))).astype(dq_buf.dtype)
            df_buf[slot, hd, r, :] = (df * a["omlb"] * sg * (1.0 - sg)).astype(df_buf.dtype)
            di_buf[slot, hd, r, :] = dv.astype(di_buf.dtype)
            cols = slice(hd * LANE, (hd + 1) * LANE)
            dlb_ref[:, cols] = dlb_ref[:, cols] + jnp.sum(df * (1.0 - sg), axis=0, keepdims=True)

        lax.fori_loop(0, ncb, chunk, 0, unroll=SCAN_UNROLL)
        _writeback_commit(step, (N_HEADS // 2) * ntb, slot, out_copies)

    pair = lambda base: pl.BlockSpec((2, tb, LANE), lambda p, t, base=base: (base // 2 + p, ntb - 1 - t, 0))
    any_spec = pl.BlockSpec(memory_space=pl.ANY)
    per_chunk = pl.BlockSpec((2, ncb, LANE, LANE), lambda p, t: (p, ntb - 1 - t, 0, 0))
    return pl.pallas_call(
        body, name="hgrn_bwd", grid=(N_HEADS // 2, ntb),
        in_specs=[pair(CB_HQ), pair(CB_HF), pair(CB_HI),
                  pl.BlockSpec((1, 2 * LANE), lambda p, t: (0, p)),
                  pair(0), per_chunk, per_chunk]
        + [_resident(c) for c in operands] + [any_spec],
        out_specs=[any_spec, pl.BlockSpec((1, 2 * LANE), lambda p, t: (0, p))],
        out_shape=[jax.ShapeDtypeStruct(dpb.shape, dpb.dtype), jax.ShapeDtypeStruct((1, D_MODEL), F32)],
        scratch_shapes=[pltpu.VMEM((2, 2, tb, LANE), dpb.dtype)] * 3 + [
            pltpu.VMEM((2, LANE, LANE), F32), pltpu.VMEM((2, CHUNK, LANE), F32), pltpu.SemaphoreType.DMA((2, 3))],
        input_output_aliases={7 + len(operands): 0},
        compiler_params=_params(("arbitrary", "arbitrary")),
    )(pb, pb, pb, lb, do, ssave, asave, *operands, dpb)


def _rope_tables(t_len):
    half = RET_DK // 2
    inv_freq = (1.0 / (np.float32(ROPE_BASE) ** np.linspace(0.0, 1.0, half, dtype=np.float32))).astype(np.float32)
    ang = (np.arange(t_len, dtype=np.float32)[:, None] * inv_freq[None, :]).astype(np.float64)
    cos, sin = np.cos(ang).astype(np.float32), np.sin(ang).astype(np.float32)
    cos_t = np.concatenate([cos, cos, cos, cos], axis=1)
    sin_t = np.concatenate([-sin, sin, -sin, sin], axis=1)
    return jnp.asarray(cos_t), jnp.asarray(sin_t)


def _swap_halves(v):
    half = RET_DK // 2
    lane = lax.broadcasted_iota(jnp.int32, v.shape, 1)
    first = (lane & (RET_DK - 1)) < half
    return jnp.where(first, pltpu.roll(v, LANE - half, 1), pltpu.roll(v, half, 1))


def _ret_head_consts(hidx):
    c = CHUNK
    hf = jnp.full((1, LANE), hidx, jnp.int32).astype(F32)
    lg = jnp.log(1.0 - jnp.exp(-(5.0 + hf) * np.float32(np.log(2.0))))
    row = lax.broadcasted_iota(jnp.int32, (c, c), 0)
    col = lax.broadcasted_iota(jnp.int32, (c, c), 1)
    rel = (row - col).astype(F32)
    dm = jnp.where(rel >= 0, jnp.exp(lg[:, :1] * jnp.maximum(rel, 0.0)), 0.0)
    dm_t = jnp.where(rel <= 0, jnp.exp(lg[:, :1] * jnp.maximum(-rel, 0.0)), 0.0)
    idx = lax.broadcasted_iota(jnp.int32, (c, LANE), 0).astype(F32)
    zeta = jnp.exp(lg * (c - 1.0 - idx))
    xi = jnp.exp(lg * (idx + 1.0))
    cdec = jnp.exp(lg * float(c))
    return dm, zeta, xi, cdec, dm_t


def _lane_mask(which):
    lane = lax.broadcasted_iota(jnp.int32, (1, LANE), 1)
    return ((lane // RET_DK) == which).astype(F32)


def retention_forward(pb, cos_t, sin_t, t_len):
    nc = t_len // CHUNK

    tb = _time_block(t_len)
    ncb = tb // CHUNK

    def body(rq_ref, rk_ref, rv_ref, cos_ref, sin_ref, o_ref, rsave_ref, st_ref):
        p = pl.program_id(0)

        @pl.when(pl.program_id(1) == 0)
        def _():
            st_ref[...] = jnp.zeros_like(st_ref)

        consts = [_ret_head_consts(2 * p + hd) for hd in range(2)]

        def chunk(ci, carry):
            r = pl.ds(pl.multiple_of(ci * CHUNK, CHUNK), CHUNK)
            cs, sn = cos_ref[r, :], sin_ref[r, :]
            q = rq_ref[r, :]
            k = rk_ref[r, :]
            q = q * cs + _swap_halves(q) * sn
            k = (k * cs + _swap_halves(k) * sn) * RET_DK ** -0.5
            for hd in range(2):
                dm, zeta, xi, cdec, _ = consts[hd]
                lm = _lane_mask(hd)
                qh, kh = q * lm, k * lm
                v = rv_ref[hd, r, :]
                st = st_ref[hd]
                rsave_ref[hd, ci] = st
                scores = _dot(qh, kh, _NT) * dm
                o_ref[hd, r, :] = _dot(scores, v) + _dot(qh * xi, st, _NT)
                st_ref[hd] = st * cdec + _dot(v, kh * zeta, _TN)
            return carry

        lax.fori_loop(0, ncb, chunk, 0, unroll=RET_UNROLL)

    return pl.pallas_call(
        body, name="ret_fwd", grid=(N_HEADS // 2, t_len // tb),
        in_specs=[pl.BlockSpec((None, tb, LANE), lambda p, t: (CB_RQ + p, t, 0)),
                  pl.BlockSpec((None, tb, LANE), lambda p, t: (CB_RK + p, t, 0)),
                  pl.BlockSpec((2, tb, LANE), lambda p, t: (CB_RV // 2 + p, t, 0)),
                  pl.BlockSpec((tb, LANE), lambda p, t: (t, 0)),
                  pl.BlockSpec((tb, LANE), lambda p, t: (t, 0))],
        out_specs=[pl.BlockSpec((2, tb, LANE), lambda p, t: (p, t, 0)),
                   pl.BlockSpec((2, ncb, LANE, LANE), lambda p, t: (p, t, 0, 0))],
        out_shape=[jax.ShapeDtypeStruct((N_HEADS, t_len, LANE), F32),
                   jax.ShapeDtypeStruct((N_HEADS, nc, LANE, LANE), F32)],
        scratch_shapes=[pltpu.VMEM((2, LANE, LANE), F32)],
        compiler_params=_params(("arbitrary", "arbitrary")),
    )(pb, pb, pb, cos_t, sin_t)


def retention_backward(pb, cos_t, sin_t, do, rsave, dpb, t_len):
    tb = _time_block(t_len)
    ncb, ntb = tb // CHUNK, t_len // tb

    def body(rq_ref, rk_ref, rv_ref, cos_ref, sin_ref, do_ref, rsave_ref, dpb_in,
             dpb_ref, dq_buf, dk_buf, dv_buf, dst_ref, sems):
        del dpb_in
        p, t = pl.program_id(0), pl.program_id(1)
        step = p * ntb + t
        rows = pl.ds(pl.multiple_of((ntb - 1 - t) * tb, tb), tb)

        def out_copies(sl):
            return [pltpu.make_async_copy(dq_buf.at[sl], dpb_ref.at[CB_RQ + p, rows], sems.at[sl, 0]),
                    pltpu.make_async_copy(dk_buf.at[sl], dpb_ref.at[CB_RK + p, rows], sems.at[sl, 1]),
                    pltpu.make_async_copy(dv_buf.at[sl], dpb_ref.at[pl.ds(CB_RV + 2 * p, 2), rows], sems.at[sl, 2])]

        slot = _writeback_reserve(step, out_copies)

        @pl.when(t == 0)
        def _():
            dst_ref[...] = jnp.zeros_like(dst_ref)

        consts = [_ret_head_consts(2 * p + hd) for hd in range(2)]

        def chunk(i, carry):
            ci = ncb - 1 - i
            r = pl.ds(pl.multiple_of(ci * CHUNK, CHUNK), CHUNK)
            cs, sn = cos_ref[r, :], sin_ref[r, :]
            q = rq_ref[r, :]
            k = rk_ref[r, :]
            q = q * cs + _swap_halves(q) * sn
            k = (k * cs + _swap_halves(k) * sn) * RET_DK ** -0.5
            dq, dk = None, None
            for hd in range(2):
                dm, zeta, xi, cdec, dm_t = consts[hd]
                lm = _lane_mask(hd)
                qh, kh = q * lm, k * lm
                v = rv_ref[hd, r, :]
                g = do_ref[hd, r, :]
                st0 = rsave_ref[hd, ci]
                dst = dst_ref[hd]
                scores_t = _dot(kh, qh, _NT) * dm_t
                dsc = _dot(g, v, _NT) * dm
                dsc_t = _dot(v, g, _NT) * dm_t
                dqh = _dot(dsc, kh) + _dot(g, st0) * xi
                dkh = _dot(dsc_t, qh) + _dot(v, dst) * zeta
                dv_buf[slot, hd, r, :] = (_dot(scores_t, g) + _dot(kh * zeta, dst, _NT)).astype(dv_buf.dtype)
                dst_ref[hd] = dst * cdec + _dot(g, qh * xi, _TN)
                dq = dqh if dq is None else dq + dqh
                dk = dkh if dk is None else dk + dkh
            dk = dk * (RET_DK ** -0.5)
            dq_buf[slot, r, :] = (dq * cs - _swap_halves(dq) * sn).astype(dq_buf.dtype)
            dk_buf[slot, r, :] = (dk * cs - _swap_halves(dk) * sn).astype(dk_buf.dtype)
            return carry

        lax.fori_loop(0, ncb, chunk, 0, unroll=RET_UNROLL)
        _writeback_commit(step, (N_HEADS // 2) * ntb, slot, out_copies)

    any_spec = pl.BlockSpec(memory_space=pl.ANY)
    return pl.pallas_call(
        body, name="ret_bwd", grid=(N_HEADS // 2, ntb),
        in_specs=[pl.BlockSpec((None, tb, LANE), lambda p, t: (CB_RQ + p, ntb - 1 - t, 0)),
                  pl.BlockSpec((None, tb, LANE), lambda p, t: (CB_RK + p, ntb - 1 - t, 0)),
                  pl.BlockSpec((2, tb, LANE), lambda p, t: (CB_RV // 2 + p, ntb - 1 - t, 0)),
                  pl.BlockSpec((tb, LANE), lambda p, t: (ntb - 1 - t, 0)),
                  pl.BlockSpec((tb, LANE), lambda p, t: (ntb - 1 - t, 0)),
                  pl.BlockSpec((2, tb, LANE), lambda p, t: (p, ntb - 1 - t, 0)),
                  pl.BlockSpec((2, ncb, LANE, LANE), lambda p, t: (p, ntb - 1 - t, 0, 0)),
                  any_spec],
        out_specs=any_spec,
        out_shape=jax.ShapeDtypeStruct(dpb.shape, dpb.dtype),
        scratch_shapes=[pltpu.VMEM((2, tb, LANE), dpb.dtype), pltpu.VMEM((2, tb, LANE), dpb.dtype),
                        pltpu.VMEM((2, 2, tb, LANE), dpb.dtype), pltpu.VMEM((2, LANE, LANE), F32),
                        pltpu.SemaphoreType.DMA((2, 3))],
        input_output_aliases={7: 0},
        compiler_params=_params(("arbitrary", "arbitrary")),
    )(pb, pb, pb, cos_t, sin_t, do, rsave, dpb)


def _row_tile(t_len, want):
    return min(want, t_len)


PAIR_CB = 2 * CB_PER_SHARD


def proj_forward(h, wt, t_len):
    tm = _row_tile(t_len, 1024)

    def body(h_ref, w_ref, o_ref):
        acc = _dot(h_ref[...], w_ref[...], _NT)
        for jj in range(PAIR_CB):
            o_ref[jj] = acc[:, jj * LANE:(jj + 1) * LANE]

    return pl.pallas_call(
        body, name="proj_fwd", grid=(N_DEV // 2, t_len // tm),
        in_specs=[pl.BlockSpec((tm, D_MODEL), lambda j, i: (i, 0)),
                  pl.BlockSpec((PAIR_CB * LANE, D_MODEL), lambda j, i: (j, 0))],
        out_specs=pl.BlockSpec((PAIR_CB, tm, LANE), lambda j, i: (j, i, 0)),
        out_shape=jax.ShapeDtypeStruct((N_CB, t_len, LANE), F32),
        compiler_params=_params(("arbitrary", "arbitrary")),
    )(h, wt)


def proj_backward_input(dpb, wt, token, x, dy, norm_g, scale1p, t_len):
    tm = _row_tile(t_len, 512)

    def body(a_ref, wt_hbm, token_ref, x_ref, dy_ref, g_ref, sc_ref, gx_ref, vec_ref, w_ref, sem):
        del token_ref
        i = pl.program_id(0)

        @pl.when(i == 0)
        def _():
            cp = pltpu.make_async_copy(wt_hbm, w_ref, sem)
            cp.start()
            cp.wait()

        a = jnp.concatenate([a_ref[jj].astype(_BF) for jj in range(N_CB)], axis=1)
        dhv = _dot(a, w_ref[...])
        xv, g, sc = x_ref[...], g_ref[...], sc_ref[...]
        r = lax.rsqrt(jnp.mean(xv * xv, axis=-1, keepdims=True) + EPS)
        xn = xv * r
        dxn = dhv * (g * sc)
        gx_ref[...] = dy_ref[...] + r * dxn - xn * (r * r) * jnp.mean(xv * dxn, axis=-1, keepdims=True)
        t = dhv * xn
        _acc_rows(vec_ref, i, [jnp.sum(t * sc, axis=0, keepdims=True),
                               jnp.sum(t * g, axis=0, keepdims=True),
                               jnp.sum(dhv, axis=0, keepdims=True)])

    row = pl.BlockSpec((tm, D_MODEL), lambda i: (i, 0))
    return pl.pallas_call(
        body, name="proj_bwd_input", grid=(t_len // tm,),
        in_specs=[pl.BlockSpec((N_CB, tm, LANE), lambda i: (0, i, 0)),
                  pl.BlockSpec(memory_space=pl.ANY),
                  pl.BlockSpec(token.shape, lambda i: (0, 0)),
                  row, row, _vec_spec(), _vec_spec()],
        out_specs=[row, pl.BlockSpec((8, D_MODEL), lambda i: (0, 0))],
        out_shape=[jax.ShapeDtypeStruct((t_len, D_MODEL), F32), jax.ShapeDtypeStruct((8, D_MODEL), F32)],
        scratch_shapes=[pltpu.VMEM(wt.shape, wt.dtype), pltpu.SemaphoreType.DMA],
        compiler_params=_params(("arbitrary",)),
    )(dpb, wt, token, x, dy, norm_g, scale1p)


def proj_backward_weight(h_t, dpb, t_len):
    tk = _row_tile(t_len, 2048)

    def body(h_ref, b_ref, o_ref):
        k = pl.program_id(1)
        b = jnp.concatenate([b_ref[jj].astype(_BF) for jj in range(PAIR_CB)], axis=1)
        part = _dot(h_ref[...], b)

        @pl.when(k == 0)
        def _():
            for s in range(2):
                o_ref[s] = part[:, s * SHARD_IN:(s + 1) * SHARD_IN]

        @pl.when(k > 0)
        def _():
            for s in range(2):
                o_ref[s] = o_ref[s] + part[:, s * SHARD_IN:(s + 1) * SHARD_IN]

    return pl.pallas_call(
        body, name="proj_bwd_weight", grid=(N_DEV // 2, t_len // tk),
        in_specs=[pl.BlockSpec((D_MODEL, tk), lambda j, k: (0, k)),
                  pl.BlockSpec((PAIR_CB, tk, LANE), lambda j, k: (j, k, 0))],
        out_specs=pl.BlockSpec((2, D_MODEL, SHARD_IN), lambda j, k: (j, 0, 0)),
        out_shape=jax.ShapeDtypeStruct((N_DEV, D_MODEL, SHARD_IN), F32),
        compiler_params=_params(("arbitrary", "arbitrary")),
    )(h_t, dpb)


def sibling_blocks(g_in, c_idx):
    tr = D_MODEL

    def body(c_ref, g_ref, o_ref):
        del c_ref
        o_ref[...] = g_ref[...].astype(o_ref.dtype)

    return pl.pallas_call(
        body, name="sibling_blocks",
        grid_spec=pltpu.PrefetchScalarGridSpec(
            num_scalar_prefetch=1, grid=(N_DEV // 2, D_MODEL // tr),
            in_specs=[pl.BlockSpec((None, tr, SHARD_IN), lambda q, i, c: (2 * q + 1 - c[0], i, 0))],
            out_specs=pl.BlockSpec((None, tr, SHARD_IN), lambda q, i, c: (q, i, 0))),
        out_shape=jax.ShapeDtypeStruct((N_DEV // 2, D_MODEL, SHARD_IN), _BF),
        compiler_params=_params(("arbitrary", "arbitrary")),
    )(c_idx, g_in)


def _vec_spec():
    return pl.BlockSpec((1, D_MODEL), lambda i: (0, 0))


def _acc_rows(ref, i, rows):
    @pl.when(i == 0)
    def _():
        ref[...] = jnp.zeros_like(ref)

    for n, row in enumerate(rows):
        ref[n:n + 1, :] = ref[n:n + 1, :] + row


def adaln_forward(x, norm_g, scale1p, shift, t_len):
    tm = _row_tile(t_len, 512)

    def body(x_ref, g_ref, sc_ref, sh_ref, h_ref, ht_ref):
        xv = x_ref[...]
        r = lax.rsqrt(jnp.mean(xv * xv, axis=-1, keepdims=True) + EPS)
        h = xv * r * g_ref[...] * sc_ref[...] + sh_ref[...]
        h_ref[...] = h.astype(h_ref.dtype)
        ht_ref[...] = h.T.astype(ht_ref.dtype)

    return pl.pallas_call(
        body, name="adaln_fwd", grid=(t_len // tm,),
        in_specs=[pl.BlockSpec((tm, D_MODEL), lambda i: (i, 0)), _vec_spec(), _vec_spec(), _vec_spec()],
        out_specs=[pl.BlockSpec((tm, D_MODEL), lambda i: (i, 0)), pl.BlockSpec((D_MODEL, tm), lambda i: (0, i))],
        out_shape=[jax.ShapeDtypeStruct((t_len, D_MODEL), _BF), jax.ShapeDtypeStruct((D_MODEL, t_len), _BF)],
        compiler_params=_params(("arbitrary",)),
    )(x, norm_g, scale1p, shift)


def _head_norm(o, g):
    r = lax.rsqrt(jnp.mean(o * o, axis=-1, keepdims=True) + EPS)
    return r, o * r * g


def _group_spec(tm, cb):
    return pl.BlockSpec((N_HEADS, tm, LANE), lambda i, cb=cb: (cb // N_HEADS, i, 0))


MID_FINAL_G, MID_GATE, MID_LOSS, MID_HG_G, MID_RET_G = range(5)


def middle(x, target, oa, ob, pb, wout, gate, final_g, hg_g, ret_g, t_len):
    tm = _row_tile(t_len, 256)
    n_steps = t_len // tm

    def body(x_ref, t_ref, oa_ref, ob_ref, hz_ref, rz_ref, ga_ref, gb_ref, w_ref, gate_ref, fg_ref, hg_ref, rg_ref,
             dy_ref, doa_ref, dob_ref, dw_ref, vec_ref, dpb_ref, m_scr, dm_scr, keep, bufs, sems):
        i = pl.program_id(0)
        rows = pl.ds(pl.multiple_of(i * tm, tm), tm)

        def group_copies(sl):
            return [pltpu.make_async_copy(bufs.at[sl, n], dpb_ref.at[pl.ds(cb, N_HEADS), rows], sems.at[sl, n])
                    for n, cb in enumerate((CB_HZ, CB_RZ, CB_GA, CB_GB))]
        sides = ((oa_ref, hz_ref, ga_ref, hg_ref, doa_ref), (ob_ref, rz_ref, gb_ref, rg_ref, dob_ref))
        for hh in range(N_HEADS):
            ls = slice(hh * LANE, (hh + 1) * LANE)
            acc = None
            for side, (o_ref, z_ref, gt_ref, g_ref, _) in enumerate(sides):
                o = o_ref[hh]
                rr = lax.rsqrt(jnp.mean(o * o, axis=-1, keepdims=True) + EPS)
                orr = o * rr
                zz = z_ref[hh]
                sz = _sigmoid(zz)
                sgt = _sigmoid(gt_ref[hh])
                keep[side, hh, 0] = orr
                keep[side, hh, 1] = sz
                keep[side, hh, 2] = sgt
                keep[side, hh, 3] = jnp.broadcast_to(rr, orr.shape)
                u = sgt * ((orr * g_ref[:, ls]) * (zz * sz))
                acc = u if acc is None else acc + u
            m_scr[:, ls] = acc.astype(m_scr.dtype)
        zv = _dot(m_scr[...], w_ref[...])
        gt, fg = gate_ref[...], fg_ref[...]
        y = x_ref[...] + gt * zv
        r = lax.rsqrt(jnp.mean(y * y, axis=-1, keepdims=True) + EPS)
        yn = y * r
        err = yn * fg - t_ref[...]
        loss = 0.5 * jnp.sum(jnp.mean(err * err, axis=-1, keepdims=True), axis=0, keepdims=True)
        dout = err * (1.0 / D_MODEL)
        gd = dout * fg
        dy = r * gd - yn * (r * r) * jnp.mean(y * gd, axis=-1, keepdims=True)
        dy_ref[...] = dy
        dz = (dy * gt).astype(_BF)
        dm_scr[...] = _dot(dz, w_ref[...], _NT)
        part = _dot(m_scr[...], dz, _TN)

        @pl.when(i == 0)
        def _():
            dw_ref[...] = part

        @pl.when(i > 0)
        def _():
            dw_ref[...] = dw_ref[...] + part

        slot = _writeback_reserve(i, group_copies)
        dg = [[], []]
        for hh in range(N_HEADS):
            ls = slice(hh * LANE, (hh + 1) * LANE)
            dmh = dm_scr[:, ls]
            for side, (o_ref, z_ref, gt_ref, g_ref, do_ref) in enumerate(sides):
                zz, g = z_ref[hh], g_ref[:, ls]
                orr, sz, sgt, rr = keep[side, hh, 0], keep[side, hh, 1], keep[side, hh, 2], keep[side, hh, 3]
                n = orr * g
                silu = zz * sz
                du = dmh * sgt
                bufs[slot, 2 + side, hh] = (dmh * (n * silu) * (sgt * (1.0 - sgt))).astype(bufs.dtype)
                bufs[slot, side, hh] = (du * n * (sz * (1.0 + zz * (1.0 - sz)))).astype(bufs.dtype)
                dn = du * silu
                dg[side].append(jnp.sum(dn * orr, axis=0, keepdims=True))
                gdn = dn * g
                do_ref[hh] = (rr * (gdn - orr * jnp.mean(orr * gdn, axis=-1, keepdims=True))).astype(do_ref.dtype)
        _acc_rows(vec_ref, i, [jnp.sum(dout * yn, axis=0, keepdims=True),
                               jnp.sum(dy * zv, axis=0, keepdims=True),
                               jnp.broadcast_to(loss, (1, D_MODEL)),
                               jnp.concatenate(dg[0], axis=1), jnp.concatenate(dg[1], axis=1)])
        _writeback_commit(i, n_steps, slot, group_copies)

    row = pl.BlockSpec((tm, D_MODEL), lambda i: (i, 0))
    head = pl.BlockSpec((N_HEADS, tm, LANE), lambda i: (0, i, 0))
    full = pl.BlockSpec((D_MODEL, D_MODEL), lambda i: (0, 0))
    return pl.pallas_call(
        body, name="middle", grid=(n_steps,),
        in_specs=[row, row, head, head, _group_spec(tm, CB_HZ), _group_spec(tm, CB_RZ), _group_spec(tm, CB_GA),
                  _group_spec(tm, CB_GB), full, _vec_spec(), _vec_spec(), _vec_spec(), _vec_spec()],
        out_specs=[row, head, head, full, pl.BlockSpec((8, D_MODEL), lambda i: (0, 0)),
                   pl.BlockSpec(memory_space=pl.ANY)],
        out_shape=[jax.ShapeDtypeStruct((t_len, D_MODEL), F32),
                   jax.ShapeDtypeStruct((N_HEADS, t_len, LANE), _BF),
                   jax.ShapeDtypeStruct((N_HEADS, t_len, LANE), _BF),
                   jax.ShapeDtypeStruct((D_MODEL, D_MODEL), F32),
                   jax.ShapeDtypeStruct((8, D_MODEL), F32),
                   jax.ShapeDtypeStruct((N_CB, t_len, LANE), _BF)],
        scratch_shapes=[pltpu.VMEM((tm, D_MODEL), _BF), pltpu.VMEM((tm, D_MODEL), F32),
                        pltpu.VMEM((2, N_HEADS, 4, tm, LANE), F32),
                        pltpu.VMEM((2, 4, N_HEADS, tm, LANE), _BF), pltpu.SemaphoreType.DMA((2, 4))],
        compiler_params=_params(("arbitrary",)),
    )(x, target, oa, ob, pb, pb, pb, pb, wout, gate, final_g, hg_g, ret_g)


def device_step(x, target, mod, lb, project, norm_g, hg_g, ret_g, final_g, c_idx=None, start_exchange=None):
    t_len = x.shape[0]
    shift, scale, gate = mod[:, :D_MODEL], mod[:, D_MODEL:2 * D_MODEL], mod[:, 2 * D_MODEL:]
    scale1p = 1.0 + scale
    cos_t, sin_t = _rope_tables(t_len)
    h, h_t = adaln_forward(x, norm_g, scale1p, shift, t_len)
    pb, wt, wout = project(h)
    oa, ssave, asave = hgrn_forward(pb, lb, t_len)
    ob, rsave = retention_forward(pb, cos_t, sin_t, t_len)
    dy, doa, dob, dwout, vec_mid, dpb = middle(x, target, oa, ob, pb, wout, gate, final_g, hg_g, ret_g, t_len)
    dpb, dlb = hgrn_backward(pb, lb, doa, ssave, asave, dpb, t_len)
    dpb = retention_backward(pb, cos_t, sin_t, dob, rsave, dpb, t_len)
    c_idx = jnp.zeros((1,), jnp.int32) if c_idx is None else c_idx
    dwin = proj_backward_weight(h_t, dpb, t_len)
    dwin_sib = sibling_blocks(dwin, c_idx)
    token, pending = (start_exchange(dwin, dwin_sib, dwout) if start_exchange
                      else (jnp.zeros((8, LANE), F32), None))
    grad_x, vec_ada = proj_backward_input(dpb, wt, token, x, dy, norm_g, scale1p, t_len)
    return grad_x, dwin, dwout, vec_mid, vec_ada, dlb, pending


PACK_ROWS = 16
ROW_NORM_G, ROW_LB, ROW_HG_G, ROW_RET_G, ROW_FINAL_G, ROW_SHIFT, ROW_SCALE, ROW_GATE, ROW_LOSS = range(9)


def _mesh_pos():
    return lax.axis_index("x"), lax.axis_index("y"), lax.axis_index("c")


def _lin(pos):
    return 4 * pos[0] + 2 * pos[1] + pos[2]


def _xor_peer(pos, k):
    return tuple(1 - p if (k >> s) & 1 else p for p, s in zip(pos, (2, 1, 0)))


def _other_chips(pos):
    x, y, _ = pos
    return [(1 - x, y), (x, 1 - y), (1 - x, 1 - y)]


def _remote(src, dst, send_sem, recv_sem, to):
    return pltpu.make_async_remote_copy(src_ref=src, dst_ref=dst, send_sem=send_sem, recv_sem=recv_sem,
                                        device_id=to, device_id_type=MESH)


def pre_exchange(c, w_ada, b_ada, logits, w_in, w_out):
    def body(c_ref, wada_ref, bada_ref, logit_ref, win_ref, wout_ref, mod_ref, scall_ref, lb_ref, wt_ref, wo_ref,
             cg_ref, modall_ref, parts_ref, send1, recv1, send2, recv2):
        pos = _mesh_pos()
        cv = c_ref[...]
        slot = lambda p: pl.ds(pl.multiple_of(8 * _lin(p), 8), 8)
        cg_ref[slot(pos), :] = jnp.broadcast_to(cv * _sigmoid(cv), (8, D_MODEL))
        lb_ref[...] = _sigmoid(logit_ref[0:1, :] - logit_ref[1:2, :])
        peers = [_xor_peer(pos, k) for k in range(1, N_DEV)]
        gather = [_remote(cg_ref.at[slot(pos)], cg_ref.at[slot(pos)], send1.at[n], recv1.at[n], p)
                  for n, p in enumerate(peers)]
        for cp in gather:
            cp.start()
        wt_ref[...] = win_ref[...].T.astype(wt_ref.dtype)
        wo_ref[...] = wout_ref[...].astype(wo_ref.dtype)
        for n, p in enumerate(peers):
            _remote(cg_ref.at[slot(p)], cg_ref.at[slot(p)], send1.at[n], recv1.at[n], p).wait_recv()
        modall_ref[...] = _dot(cg_ref[...], wada_ref[...])
        scatter = [_remote(modall_ref.at[slot(p)], parts_ref.at[slot(pos)], send2.at[n], recv2.at[n], p)
                   for n, p in enumerate(peers)]
        for cp in scatter:
            cp.start()
        parts_ref[slot(pos), :] = modall_ref[slot(pos), :]
        for n, p in enumerate(peers):
            _remote(modall_ref.at[slot(p)], parts_ref.at[slot(p)], send2.at[n], recv2.at[n], p).wait_recv()
        for cp in gather + scatter:
            cp.wait_send()
        for j in range(N_DEV):
            cols = slice(j * SHARD_ADA, (j + 1) * SHARD_ADA)
            mod_ref[:, cols] = parts_ref[8 * j:8 * j + 1, :] + bada_ref[:, cols]
            scall_ref[j:j + 1, :] = cg_ref[8 * j:8 * j + 1, :]

    vmem = pl.BlockSpec(memory_space=pltpu.VMEM)
    return pl.pallas_call(
        body, name="pre_exchange",
        in_specs=[vmem] * 6, out_specs=[vmem] * 5,
        out_shape=[jax.ShapeDtypeStruct((1, 3 * D_MODEL), F32), jax.ShapeDtypeStruct((N_DEV, D_MODEL), F32),
                   jax.ShapeDtypeStruct((1, D_MODEL), F32),
                   jax.ShapeDtypeStruct(w_in.shape[::-1], _BF), jax.ShapeDtypeStruct(w_out.shape, _BF)],
        scratch_shapes=[pltpu.VMEM((N_DEV * 8, D_MODEL), F32), pltpu.VMEM((N_DEV * 8, SHARD_ADA), F32),
                        pltpu.VMEM((N_DEV * 8, SHARD_ADA), F32)] + [pltpu.SemaphoreType.DMA((N_DEV - 1,))] * 4,
        compiler_params=pltpu.CompilerParams(vmem_limit_bytes=VMEM_LIMIT),
    )(c, w_ada, b_ada, logits, w_in, w_out)


def weight_gather(win_sh, wout_sh):
    def body(win_ref, wout_ref, wg_ref, woutg_ref, send, recv, local):
        pos = _mesh_pos()
        x, y, c = pos
        sibling = (x, y, 1 - c)

        def route(core):
            return [(x + (1 - core) * (1 - 2 * x), y + core * (1 - 2 * y)),
                    (x + core * (1 - 2 * x), y + (1 - core) * (1 - 2 * y)),
                    (1 - x, 1 - y)]

        chips, sib_chips = route(c), route(1 - c)
        mine, first, later = [], [], []
        for a, (src, out) in enumerate(((win_ref, wg_ref), (wout_ref, woutg_ref))):
            def copy(k, block, to, src_ref=None, a=a, out=out):
                dst = out.at[_lin(block)]
                return _remote(dst if src_ref is None else src_ref, dst, send.at[7 * a + k], recv.at[7 * a + k], to)
            mine.append(pltpu.make_async_copy(src, out.at[_lin(pos)], local.at[a]))
            first += [copy(0, pos, sibling, src), copy(1, pos, (*chips[0], c), src), copy(2, pos, (*chips[1], c), src)]
            later.append([[copy(3, (*chips[0], c), (*chips[1], c)), copy(4, (*chips[0], c), sibling)],
                          [copy(5, (*chips[1], c), sibling)],
                          [copy(6, (*chips[2], c), sibling)]])
        for cp in mine + first:
            cp.start()
        for j in range(3):
            for a, out in enumerate((wg_ref, woutg_ref)):
                dst = out.at[_lin((*chips[j], c))]
                _remote(dst, dst, send.at[7 * a + 1 + j], recv.at[7 * a + 1 + j], pos).wait_recv()
                for cp in later[a][j]:
                    cp.start()
        for a, out in enumerate((wg_ref, woutg_ref)):
            dst = out.at[_lin(sibling)]
            _remote(dst, dst, send.at[7 * a], recv.at[7 * a], pos).wait_recv()
            for j in range(3):
                dst = out.at[_lin((*sib_chips[j], 1 - c))]
                _remote(dst, dst, send.at[7 * a + 4 + j], recv.at[7 * a + 4 + j], pos).wait_recv()
        for cp in first + [cp for per_array in later for group in per_array for cp in group]:
            cp.wait_send()
        for cp in mine:
            cp.wait()

    any_spec = pl.BlockSpec(memory_space=pl.ANY)
    return pl.pallas_call(
        body, name="weight_gather",
        in_specs=[any_spec, any_spec], out_specs=[any_spec, any_spec],
        out_shape=[jax.ShapeDtypeStruct((N_DEV,) + win_sh.shape, win_sh.dtype),
                   jax.ShapeDtypeStruct((N_DEV,) + wout_sh.shape, wout_sh.dtype)],
        scratch_shapes=[pltpu.SemaphoreType.DMA((14,)), pltpu.SemaphoreType.DMA((14,)),
                        pltpu.SemaphoreType.DMA((2,))],
    )(win_sh, wout_sh)


def grad_pair_exchange(g_sib, g_out):
    def body(gsib_ref, gout_ref, ra_ref, rb_ref, send, recv):
        pos = _mesh_pos()
        x, y, c = pos
        sibling = (x, y, 1 - c)
        copies = []
        for q in range(4):
            copies.append(_remote(gsib_ref.at[q], ra_ref.at[q], send.at[q], recv.at[q], sibling))
            copies.append(_remote(gout_ref.at[2 * q + (1 - c)], rb_ref.at[q], send.at[4 + q], recv.at[4 + q], sibling))
        for cp in copies:
            cp.start()
        for cp in copies:
            cp.wait_recv()
        for cp in copies:
            cp.wait_send()

    any_spec = pl.BlockSpec(memory_space=pl.ANY)
    return pl.pallas_call(
        body, name="grad_pair_exchange",
        in_specs=[any_spec, any_spec], out_specs=[any_spec, any_spec],
        out_shape=[jax.ShapeDtypeStruct(g_sib.shape, g_sib.dtype), jax.ShapeDtypeStruct((4,) + g_out.shape[1:], F32)],
        scratch_shapes=[pltpu.SemaphoreType.DMA((8,)), pltpu.SemaphoreType.DMA((8,))],
    )(g_sib, g_out)


def pair_sum(g_in, ra, g_out, rb, c_idx):
    tr = D_MODEL

    def body(c_ref, gin_ref, ra_ref, gout_ref, rb_ref, sb_ref, sbo_ref):
        del c_ref
        sb_ref[...] = (gin_ref[...] + ra_ref[...].astype(F32)).astype(sb_ref.dtype)
        sbo_ref[...] = gout_ref[...] + rb_ref[...]

    n_i = D_MODEL // tr
    return pl.pallas_call(
        body, name="pair_sum",
        grid_spec=pltpu.PrefetchScalarGridSpec(
            num_scalar_prefetch=1, grid=(4, n_i),
            in_specs=[pl.BlockSpec((None, tr, SHARD_IN), lambda q, i, c: (2 * q + c[0], i, 0)),
                      pl.BlockSpec((None, tr, SHARD_IN), lambda q, i, c: (q, i, 0)),
                      pl.BlockSpec((None, SHARD_OUT // n_i, D_MODEL), lambda q, i, c: (2 * q + c[0], i, 0)),
                      pl.BlockSpec((None, SHARD_OUT // n_i, D_MODEL), lambda q, i, c: (q, i, 0))],
            out_specs=[pl.BlockSpec((None, tr, SHARD_IN), lambda q, i, c: (q, i, 0)),
                       pl.BlockSpec((None, SHARD_OUT // n_i, D_MODEL), lambda q, i, c: (q, i, 0))]),
        out_shape=[jax.ShapeDtypeStruct(ra.shape, _BF), jax.ShapeDtypeStruct(rb.shape, F32)],
        compiler_params=_params(("arbitrary", "arbitrary")),
    )(c_idx, g_in, ra, g_out, rb)


_HBM = pl.BlockSpec(memory_space=pltpu.HBM)
_SEM = pl.BlockSpec(memory_space=pltpu.SEMAPHORE)
_N_CHIP_COPIES = 6


def _chip_copies(sb_ref, sbo_ref, rc_ref, rco_ref, send, recv):
    pos = _mesh_pos()
    copies = []
    for a, (src, dst) in enumerate(((sb_ref, rc_ref), (sbo_ref, rco_ref))):
        for j, chip in enumerate(_other_chips(pos)):
            copies.append(_remote(src.at[2 * chip[0] + chip[1]], dst.at[j], send.at[3 * a + j], recv.at[3 * a + j],
                                  (*chip, pos[2])))
    return copies


def grad_chip_start(sb, sbo):
    def body(sb_ref, sbo_ref, rc_ref, rco_ref, send, recv, sb_thru, sbo_thru, rc_thru, rco_thru, token):
        del sb_thru, sbo_thru, rc_thru, rco_thru
        for cp in _chip_copies(sb_ref, sbo_ref, rc_ref, rco_ref, send, recv):
            cp.start()
        token[...] = jnp.zeros_like(token)

    hbm = lambda a: pltpu.with_memory_space_constraint(a, pltpu.HBM)
    rc = lax.empty((3,) + sb.shape[1:], sb.dtype)
    rco = lax.empty((3,) + sbo.shape[1:], sbo.dtype)
    return pl.pallas_call(
        body, name="grad_chip_start",
        in_specs=[_HBM] * 4,
        out_specs=[_SEM, _SEM, _HBM, _HBM, _HBM, _HBM, pl.BlockSpec(memory_space=pltpu.VMEM)],
        out_shape=[pltpu.SemaphoreType.DMA((_N_CHIP_COPIES,)), pltpu.SemaphoreType.DMA((_N_CHIP_COPIES,)),
                   pltpu.HBM(sb.shape, sb.dtype), pltpu.HBM(sbo.shape, sbo.dtype),
                   pltpu.HBM(rc.shape, rc.dtype), pltpu.HBM(rco.shape, rco.dtype),
                   jax.ShapeDtypeStruct((8, LANE), F32)],
        input_output_aliases={0: 2, 1: 3, 2: 4, 3: 5},
        compiler_params=pltpu.CompilerParams(has_side_effects=pltpu.SideEffectType.DATAFLOW_SIDE_EFFECTING),
    )(hbm(sb), hbm(sbo), hbm(rc), hbm(rco))


def grad_chip_wait(send, recv, sb, sbo, rc, rco, after):
    def body(sb_ref, sbo_ref, rc_ref, rco_ref, send, recv, after_ref, sb_o, sbo_o, rc_o, rco_o):
        del after_ref, sb_o, sbo_o, rc_o, rco_o
        for cp in _chip_copies(sb_ref, sbo_ref, rc_ref, rco_ref, send, recv):
            cp.wait_send()
            cp.wait_recv()

    return pl.pallas_call(
        body, name="grad_chip_wait",
        in_specs=[_HBM] * 4 + [_SEM, _SEM, pl.BlockSpec(memory_space=pl.ANY)],
        out_specs=[_HBM] * 4,
        out_shape=[pltpu.HBM(sb.shape, sb.dtype), pltpu.HBM(sbo.shape, sbo.dtype),
                   pltpu.HBM(rc.shape, rc.dtype), pltpu.HBM(rco.shape, rco.dtype)],
        input_output_aliases={0: 0, 1: 1, 2: 2, 3: 3},
        compiler_params=pltpu.CompilerParams(has_side_effects=pltpu.SideEffectType.DATAFLOW_SIDE_EFFECTING),
    )(sb, sbo, rc, rco, send, recv, after)


def pack_gather(pack):
    def body(pack_ref, packs_ref, psend, precv):
        pos = _mesh_pos()
        me = _lin(pos)
        packs_ref[me] = pack_ref[...]
        peers = [_xor_peer(pos, k) for k in range(1, N_DEV)]
        gather = [_remote(packs_ref.at[me], packs_ref.at[me], psend.at[n], precv.at[n], p) for n, p in enumerate(peers)]
        for cp in gather:
            cp.start()
        for n, p in enumerate(peers):
            _remote(packs_ref.at[_lin(p)], packs_ref.at[_lin(p)], psend.at[n], precv.at[n], p).wait_recv()
        for cp in gather:
            cp.wait_send()

    vmem = pl.BlockSpec(memory_space=pltpu.VMEM)
    return pl.pallas_call(
        body, name="pack_gather", in_specs=[vmem], out_specs=vmem,
        out_shape=jax.ShapeDtypeStruct((N_DEV,) + pack.shape, F32),
        scratch_shapes=[pltpu.SemaphoreType.DMA((N_DEV - 1,)), pltpu.SemaphoreType.DMA((N_DEV - 1,))],
    )(pack)


def pack_rows(vec_mid, vec_ada, dlb):
    def body(mid_ref, ada_ref, dlb_ref, o_ref):
        mid = lambda r: mid_ref[r:r + 1, :]
        rows = [ada_ref[0:1, :], dlb_ref[...], mid(MID_HG_G), mid(MID_RET_G), mid(MID_FINAL_G),
                ada_ref[2:3, :], ada_ref[1:2, :], mid(MID_GATE), mid(MID_LOSS)]
        o_ref[...] = jnp.zeros_like(o_ref)
        for n, row in enumerate(rows):
            o_ref[n:n + 1, :] = row

    vmem = pl.BlockSpec(memory_space=pltpu.VMEM)
    return pl.pallas_call(body, name="pack_rows", in_specs=[vmem] * 3, out_specs=vmem,
                          out_shape=jax.ShapeDtypeStruct((PACK_ROWS, D_MODEL), F32))(vec_mid, vec_ada, dlb)


def _adamw(w, g, m, v):
    m = ADAM_B1 * m + (1.0 - ADAM_B1) * g
    v = ADAM_B2 * v + (1.0 - ADAM_B2) * (g * g)
    m_hat = m / (1.0 - ADAM_B1 ** ADAM_STEP)
    v_hat = v / (1.0 - ADAM_B2 ** ADAM_STEP)
    delta = -ADAM_LR * (m_hat / (jnp.sqrt(v_hat) + ADAM_EPS) + ADAM_WD * w)
    return delta, m, v


def adam_shard(chip_idx, own, parts, w, m, v, name):
    rows, cols = w.shape
    tr = min(rows, 256)

    def body(chip_ref, p0, p1, p2, p3, w_ref, m_ref, v_ref, g_ref, d_ref, nm_ref, nv_ref):
        del chip_ref
        g = ((p0[...].astype(F32) + p1[...].astype(F32)) + p2[...].astype(F32)) + p3[...].astype(F32)
        g_ref[...] = g
        d_ref[...], nm_ref[...], nv_ref[...] = _adamw(w_ref[...], g, m_ref[...], v_ref[...])

    part = lambda q: pl.BlockSpec((None, tr, cols), lambda i, chip, q=q: (q, i, 0))
    tile = pl.BlockSpec((tr, cols), lambda i, chip: (i, 0))
    return pl.pallas_call(
        body, name=name,
        grid_spec=pltpu.PrefetchScalarGridSpec(
            num_scalar_prefetch=1, grid=(rows // tr,),
            in_specs=[pl.BlockSpec((None, tr, cols), lambda i, chip: (chip[0], i, 0)), part(0), part(1), part(2),
                      tile, tile, tile],
            out_specs=[tile] * 4),
        out_shape=[jax.ShapeDtypeStruct(w.shape, F32)] * 4,
        compiler_params=_params(("arbitrary",)),
    )(chip_idx, own, parts, parts, parts, w, m, v)


def adam_ada(sc_t, dmod_all, me_idx, w, m, v):
    def body(me_ref, sc_ref, dm_ref, w_ref, m_ref, v_ref, g_ref, d_ref, nm_ref, nv_ref):
        del me_ref
        g = _dot_f32(sc_ref[...], dm_ref[...])
        g_ref[...] = g
        d_ref[...], nm_ref[...], nv_ref[...] = _adamw(w_ref[...], g, m_ref[...], v_ref[...])

    full = pl.BlockSpec(w.shape, lambda i, me: (0, 0))
    return pl.pallas_call(
        body, name="adam_ada",
        grid_spec=pltpu.PrefetchScalarGridSpec(
            num_scalar_prefetch=1, grid=(1,),
            in_specs=[pl.BlockSpec(sc_t.shape, lambda i, me: (0, 0)),
                      pl.BlockSpec((LANE, SHARD_ADA), lambda i, me: (0, me[0])), full, full, full],
            out_specs=[full] * 4),
        out_shape=[jax.ShapeDtypeStruct(w.shape, F32)] * 4,
        compiler_params=_params(("arbitrary",)),
    )(me_idx, sc_t, dmod_all, w, m, v)


def adam_vectors(packs, lb, params, ms, vs):
    n = len(params)

    def body(*refs):
        packs_ref, lb_ref = refs[0], refs[1]
        w_refs, m_refs, v_refs = refs[2:2 + n], refs[2 + n:2 + 2 * n], refs[2 + 2 * n:2 + 3 * n]
        loss_ref = refs[2 + 3 * n]
        outs = refs[3 + 3 * n:3 + 7 * n]
        tot_ref = refs[3 + 7 * n]
        tot = packs_ref[0]
        for d in range(1, N_DEV):
            tot = tot + packs_ref[d]
        tot_ref[...] = tot
        row = lambda r: tot_ref[r:r + 1, :]
        lbv = lb_ref[...]
        dl0 = row(ROW_LB) * lbv * (1.0 - lbv)
        grads = [[row(ROW_NORM_G)],
                 [jnp.concatenate([row(ROW_SHIFT), row(ROW_SCALE), row(ROW_GATE)], axis=1)],
                 [dl0, -dl0],
                 [row(ROW_HG_G)], [row(ROW_RET_G)], [row(ROW_FINAL_G)]]
        loss_ref[...] = tot_ref[ROW_LOSS:ROW_LOSS + 1, 0:LANE]
        for j, g_rows in enumerate(grads):
            for r, g in enumerate(g_rows):
                rs = slice(r, r + 1)
                d, nm, nv = _adamw(w_refs[j][rs, :], g, m_refs[j][rs, :], v_refs[j][rs, :])
                outs[4 * j][rs, :] = g
                outs[4 * j + 1][rs, :] = d
                outs[4 * j + 2][rs, :] = nm
                outs[4 * j + 3][rs, :] = nv

    vmem = pl.BlockSpec(memory_space=pltpu.VMEM)
    out_shape = [jax.ShapeDtypeStruct((1, LANE), F32)]
    for w in params:
        out_shape += [jax.ShapeDtypeStruct(w.shape, F32)] * 4
    return pl.pallas_call(
        body, name="adam_vectors", in_specs=[vmem] * (2 + 3 * n), out_specs=[vmem] * len(out_shape),
        out_shape=out_shape, scratch_shapes=[pltpu.VMEM((PACK_ROWS, D_MODEL), F32)],
    )(packs, lb, *params, *ms, *vs)


def kernel(x, c, norm_g, w_ada, b_ada, w_in, hg_lb_logits, hg_norm_g, ret_norm_g, w_out, final_g, loss_target, m_norm_g, m_w_ada, m_b_ada, m_w_in, m_hg_lb_logits, m_hg_norm_g, m_ret_norm_g, m_w_out, m_final_g, v_norm_g, v_w_ada, v_b_ada, v_w_in, v_hg_lb_logits, v_hg_norm_g, v_ret_norm_g, v_w_out, v_final_g):
    pos = _mesh_pos()
    me_idx = jnp.reshape(_lin(pos), (1,)).astype(jnp.int32)
    c_idx = jnp.reshape(pos[2], (1,)).astype(jnp.int32)
    vec = lambda a: a.reshape(1, D_MODEL)

    mod, scall, lb, wt_sh, wout_sh = pre_exchange(c, w_ada[0], b_ada, hg_lb_logits, w_in[0], w_out[0])
    chip_idx = jnp.reshape(2 * pos[0] + pos[1], (1,)).astype(jnp.int32)

    def project(h):
        wtg, woutg = weight_gather(wt_sh, wout_sh)
        wt = wtg.reshape(D_IN, D_MODEL)
        return proj_forward(h, wt, h.shape[0]), wt, woutg.reshape(D_MODEL, D_MODEL)

    def start_exchange(dwin, dwin_sib, dwout):
        dwout = dwout.reshape(N_DEV, SHARD_OUT, D_MODEL)
        ra, rb = grad_pair_exchange(dwin_sib, dwout)
        sb, sbo = pair_sum(dwin, ra, dwout, rb, c_idx)
        send, recv, sb, sbo, rc, rco, token = grad_chip_start(sb, sbo)
        return token, (send, recv, sb, sbo, rc, rco)

    grad_x, _, _, vec_mid, vec_ada, dlb, pending = device_step(
        x[0], loss_target[0], mod, lb, project, norm_g, hg_norm_g, ret_norm_g, vec(final_g), c_idx, start_exchange)
    packs = pack_gather(pack_rows(vec_mid, vec_ada, dlb))
    dmod_all = packs[:, ROW_SHIFT:ROW_GATE + 1, :].reshape(N_DEV, 3 * D_MODEL)
    dmod_all = jnp.pad(dmod_all, ((0, LANE - N_DEV), (0, 0)))
    sc_t = jnp.pad(scall.T, ((0, 0), (0, LANE - N_DEV)))
    g_ada, d_ada, nm_ada, nv_ada = adam_ada(sc_t, dmod_all, me_idx, w_ada[0], m_w_ada[0], v_w_ada[0])
    small = adam_vectors(
        packs, lb,
        (norm_g, b_ada, hg_lb_logits, hg_norm_g, ret_norm_g, vec(final_g)),
        (m_norm_g, m_b_ada, m_hg_lb_logits, m_hg_norm_g, m_ret_norm_g, vec(m_final_g)),
        (v_norm_g, v_b_ada, v_hg_lb_logits, v_hg_norm_g, v_ret_norm_g, vec(v_final_g)))
    loss = small[0][0, 0]
    sb, sbo, rc, rco = grad_chip_wait(*pending, small[0])
    g_in, d_in, nm_in, nv_in = adam_shard(chip_idx, sb, rc, w_in[0], m_w_in[0], v_w_in[0], "adam_w_in")
    g_out, d_out, nm_out, nv_out = adam_shard(chip_idx, sbo, rco, w_out[0], m_w_out[0], v_w_out[0], "adam_w_out")
    (g_ng, d_ng, nm_ng, nv_ng), (g_b, d_b, nm_b, nv_b), (g_lb, d_lb, nm_lb, nv_lb), (g_hg, d_hg, nm_hg, nv_hg), \
        (g_rg, d_rg, nm_rg, nv_rg), (g_fg, d_fg, nm_fg, nv_fg) = [small[1 + 4 * j:5 + 4 * j] for j in range(6)]
    flat = lambda a: a.reshape(D_MODEL)

    def group(ng, ada, b, win, lbl, hg, rg, wo, fg):
        return (ng, ada[None], b, win[None], lbl, hg, rg, wo[None], flat(fg))

    return (loss, grad_x[None],
            *group(g_ng, g_ada, g_b, g_in, g_lb, g_hg, g_rg, g_out, g_fg),
            *group(d_ng, d_ada, d_b, d_in, d_lb, d_hg, d_rg, d_out, d_fg),
            *group(nm_ng, nm_ada, nm_b, nm_in, nm_lb, nm_hg, nm_rg, nm_out, nm_fg),
            *group(nv_ng, nv_ada, nv_b, nv_in, nv_lb, nv_hg, nv_rg, nv_out, nv_fg))
```

```python
import functools

import numpy as np
import jax
import jax.numpy as jnp
from jax import lax
from jax.experimental import pallas as pl
from jax.experimental.pallas import tpu as pltpu

F32 = jnp.float32
_BF = jnp.bfloat16

D_MODEL = 1024
N_HEADS = 8
LANE = 128
RET_DK = 64
D_IN = 9216
N_DEV = 8
SHARD_IN = D_IN // N_DEV
SHARD_ADA = 3 * D_MODEL // N_DEV
SHARD_OUT = D_MODEL // N_DEV
N_CB = D_IN // LANE
CB_PER_SHARD = SHARD_IN // LANE
CHUNK = 128
N_LEVELS = 7
EPS = 1e-6
LOG2_E = float(np.log2(np.e))
ROPE_BASE = 10000.0
CB_HQ, CB_HF, CB_HI, CB_HZ, CB_RQ, CB_RK, CB_RV, CB_RZ, CB_GA, CB_GB = 0, 8, 16, 24, 32, 36, 40, 48, 56, 64
VMEM_LIMIT = 56 * 1024 * 1024

ADAM_LR, ADAM_B1, ADAM_B2, ADAM_EPS, ADAM_WD, ADAM_STEP = 0.001, 0.9, 0.999, 1e-08, 0.01, 10

_NN = (((1,), (0,)), ((), ()))
_NT = (((1,), (1,)), ((), ()))
_TN = (((0,), (0,)), ((), ()))
MESH = pl.DeviceIdType.MESH


def _dot(a, b, dims=_NN):
    return lax.dot_general(a.astype(_BF), b.astype(_BF), dims, preferred_element_type=F32)


def _split2(a):
    hi = a.astype(_BF)
    lo = (a - hi.astype(F32)).astype(_BF)
    return jnp.concatenate([hi, lo], axis=1)


def _dot_sel(sel, a):
    n = a.shape[1]
    r = lax.dot_general(sel.astype(_BF), _split2(a), _NN, preferred_element_type=F32)
    return r[:, :n] + r[:, n:]


def _dot_f32(a, b):
    def pieces(v):
        p1 = v.astype(_BF)
        r1 = v - p1.astype(F32)
        p2 = r1.astype(_BF)
        p3 = (r1 - p2.astype(F32)).astype(_BF)
        return p1, p2, p3
    a1, a2, a3 = pieces(a)
    b1, b2, b3 = pieces(b)
    d = lambda u, v: lax.dot_general(u, v, _NN, preferred_element_type=F32)
    return ((d(a1, b3) + d(a2, b2) + d(a3, b1)) + (d(a1, b2) + d(a2, b1))) + d(a1, b1)


def _sigmoid(v):
    return 1.0 / (1.0 + jnp.exp(-v))


def _params(sem=None):
    return pltpu.CompilerParams(dimension_semantics=sem, vmem_limit_bytes=VMEM_LIMIT)


def _hgrn_consts():
    c, nl = CHUNK, N_LEVELS
    t = np.arange(c)[:, None]
    j = np.arange(c)[None, :]
    sel = [j <= t]
    masks = [j == t]
    for l in range(1, nl + 1):
        m = ((t >> l) << l) + (1 << (l - 1)) - 1
        sec = t > m
        sel.append(np.where(sec, (j > m) & (j <= t), (j > t) & (j <= m)))
        same = (t >> l) == (j >> l)
        masks.append(same & sec & (j <= m))
    sel.append(j > t)
    sel = np.concatenate(sel, 0).astype(np.float32)
    masks = np.stack(masks).astype(np.float32)
    sgn = np.stack([np.where((t & (1 << (l - 1))) != 0, 1.0, -1.0) * np.ones((1, LANE)) for l in range(3, nl + 1)])
    return dict(tri=jnp.asarray(sel[:c], _BF),
                lvl=jnp.asarray(masks, F32),
                sgn=jnp.asarray(sgn, F32),
                sel_t=jnp.asarray(sel.T, _BF),
                lvl_b=jnp.asarray(masks, _BF),
                lvlt_b=jnp.asarray(np.swapaxes(masks, 1, 2), _BF))


def _level_exponents(b, logf, b_scr, sgn_ref):
    c = CHUNK
    b_scr[...] = b
    row = lax.broadcasted_iota(jnp.int32, (c, LANE), 0)
    nxt = pltpu.roll(logf, c - 1, 0)
    prv = pltpu.roll(logf, 1, 0)
    r4 = row & 3
    out = [jnp.where((row & 1) == 1, logf, 0.0),
           jnp.where(r4 == 0, nxt, jnp.where(r4 == 1, 0.0, jnp.where(r4 == 2, logf, logf + prv)))]
    for l in range(3, N_LEVELS + 1):
        size, half = 1 << l, 1 << (l - 1)
        ref = jnp.concatenate([jnp.broadcast_to(b_scr[i * size + half - 1:i * size + half, :], (size, LANE))
                               for i in range(c // size)], axis=0)
        out.append((b - ref) * sgn_ref[l - 3])
    return out


def _hgrn_chunk(hq, hf, hi, lbv, tri_ref, sgn_ref, b_scr):
    sq = _sigmoid(hq)
    q = hq * sq
    sg = _sigmoid(hf)
    omlb = 1.0 - lbv
    f = lbv + omlb * sg
    k = 1.0 - f
    logf = jnp.log(f) * LOG2_E
    b = _dot_sel(tri_ref[...], logf)
    bc = jnp.sum(logf, axis=0, keepdims=True)
    lev = [None] + [jnp.exp2(e) for e in _level_exponents(b, logf, b_scr, sgn_ref)]
    return dict(sq=sq, q=q, sg=sg, omlb=omlb, f=f, k=k, v=hi, eb=jnp.exp2(b), erem=jnp.exp2(bc - b),
                ebc=jnp.exp2(bc), lev=lev)


def _blockdiag(a, b):
    z = jnp.zeros_like(a)
    return jnp.concatenate([jnp.concatenate([a, z], axis=1), jnp.concatenate([z, b], axis=1)], axis=0)


def _level_operands(a):
    q, k = a["q"].astype(_BF), a["k"].astype(_BF)
    lev = [None] + [a["lev"][l].astype(_BF) for l in range(1, N_LEVELS + 1)]
    ql = [q] + [q * lev[l] for l in range(1, N_LEVELS + 1)]
    kl = [k] + [k * lev[l] for l in range(1, N_LEVELS + 1)]
    pairs = range(0, N_LEVELS + 1, 2)
    return ([jnp.concatenate([ql[l], ql[l + 1]], axis=1) for l in pairs], [_blockdiag(kl[l], kl[l + 1]) for l in pairs],
            ql, kl)


def _hgrn_scores(a, lvl_ref, q_pairs, k_diags):
    acc = None
    for n, (qp, kd) in enumerate(zip(q_pairs, k_diags)):
        both = lax.dot_general(qp, kd, _NT, preferred_element_type=F32)
        part = lvl_ref[2 * n] * both[:, :CHUNK] + lvl_ref[2 * n + 1] * both[:, CHUNK:]
        acc = part if acc is None else acc + part
    return acc


SCAN_UNROLL = 8
RET_UNROLL = 16


def _writeback_reserve(step, make_copies):
    slot = step % 2

    @pl.when(step >= 2)
    def _():
        for cp in make_copies(slot):
            cp.wait()

    return slot


def _writeback_commit(step, n_steps, slot, make_copies):
    for cp in make_copies(slot):
        cp.start()

    @pl.when(step == n_steps - 1)
    def _():
        for cp in make_copies(slot):
            cp.wait()
        if n_steps > 1:
            for cp in make_copies(1 - slot):
                cp.wait()


def _resident(const):
    zeros = (0,) * const.ndim
    return pl.BlockSpec(const.shape, lambda p, t: zeros)


def _time_block(t_len):
    return min(t_len, 2048)


def hgrn_forward(pb, lb, t_len):
    nc = t_len // CHUNK
    tb = _time_block(t_len)
    ncb = tb // CHUNK
    consts = _hgrn_consts()
    operands = [consts[n] for n in ("tri", "lvl", "sgn")]

    def body(hq_ref, hf_ref, hi_ref, lb_ref, tri_ref, lvl_ref, sgn_ref, o_ref, ssave_ref, asave_ref, st_ref, b_scr):
        @pl.when(pl.program_id(1) == 0)
        def _():
            st_ref[...] = jnp.zeros_like(st_ref)

        def chunk(ci, carry):
            r = pl.ds(pl.multiple_of(ci * CHUNK, CHUNK), CHUNK)
            for hd in range(2):
                lbv = lb_ref[:, hd * LANE:(hd + 1) * LANE]
                a = _hgrn_chunk(hq_ref[hd, r, :], hf_ref[hd, r, :], hi_ref[hd, r, :], lbv, tri_ref, sgn_ref,
                                b_scr.at[hd])
                q_pairs, k_diags, _, _ = _level_operands(a)
                st = st_ref[hd]
                ssave_ref[hd, ci] = st
                scores = _hgrn_scores(a, lvl_ref, q_pairs, k_diags).astype(asave_ref.dtype)
                asave_ref[hd, ci] = scores
                o_ref[hd, r, :] = _dot(a["q"] * a["eb"], st, _NT) + _dot(scores, a["v"])
                st_ref[hd] = st * a["ebc"] + _dot(a["v"], a["k"] * a["erem"], _TN)
            return carry

        lax.fori_loop(0, ncb, chunk, 0, unroll=SCAN_UNROLL)

    pair = lambda base: pl.BlockSpec((2, tb, LANE), lambda p, t, base=base: (base // 2 + p, t, 0))
    per_chunk = pl.BlockSpec((2, ncb, LANE, LANE), lambda p, t: (p, t, 0, 0))
    return pl.pallas_call(
        body, name="hgrn_fwd", grid=(N_HEADS // 2, t_len // tb),
        in_specs=[pair(CB_HQ), pair(CB_HF), pair(CB_HI),
                  pl.BlockSpec((1, 2 * LANE), lambda p, t: (0, p))] + [_resident(c) for c in operands],
        out_specs=[pl.BlockSpec((2, tb, LANE), lambda p, t: (p, t, 0)), per_chunk, per_chunk],
        out_shape=[jax.ShapeDtypeStruct((N_HEADS, t_len, LANE), F32),
                   jax.ShapeDtypeStruct((N_HEADS, nc, LANE, LANE), F32),
                   jax.ShapeDtypeStruct((N_HEADS, nc, CHUNK, CHUNK), _BF)],
        scratch_shapes=[pltpu.VMEM((2, LANE, LANE), F32), pltpu.VMEM((2, CHUNK, LANE), F32)],
        compiler_params=_params(("arbitrary", "arbitrary")),
    )(pb, pb, pb, lb, *operands)


def hgrn_backward(pb, lb, do, ssave, asave, dpb, t_len):
    tb = _time_block(t_len)
    ncb, ntb = tb // CHUNK, t_len // tb
    consts = _hgrn_consts()
    operands = [consts[n] for n in ("tri", "sgn", "sel_t", "lvl_b", "lvlt_b")]

    def body(hq_ref, hf_ref, hi_ref, lb_ref, do_ref, ssave_ref, asave_ref, tri_ref, sgn_ref, selt_ref, lvlb_ref,
             lvltb_ref, dpb_in, dpb_ref, dlb_ref, dq_buf, df_buf, di_buf, dst_ref, b_scr, sems):
        del dpb_in
        p, t = pl.program_id(0), pl.program_id(1)
        step = p * ntb + t
        rows = pl.ds(pl.multiple_of((ntb - 1 - t) * tb, tb), tb)

        def out_copies(sl):
            return [pltpu.make_async_copy(buf.at[sl], dpb_ref.at[pl.ds(base + 2 * p, 2), rows], sems.at[sl, n])
                    for n, (buf, base) in enumerate(((dq_buf, CB_HQ), (df_buf, CB_HF), (di_buf, CB_HI)))]

        slot = _writeback_reserve(step, out_copies)

        @pl.when(t == 0)
        def _():
            dst_ref[...] = jnp.zeros_like(dst_ref)
            dlb_ref[...] = jnp.zeros_like(dlb_ref)

        def chunk(i, carry):
            ci = ncb - 1 - i
            r = pl.ds(pl.multiple_of(ci * CHUNK, CHUNK), CHUNK)
            for hd in range(2):
                head_chunk(hd, ci, r)
            return carry

        def head_chunk(hd, ci, r):
            lbv = lb_ref[:, hd * LANE:(hd + 1) * LANE]
            hq = hq_ref[hd, r, :]
            a = _hgrn_chunk(hq, hf_ref[hd, r, :], hi_ref[hd, r, :], lbv, tri_ref, sgn_ref, b_scr.at[hd])
            _, k_diags, ql, kl = _level_operands(a)
            q, k, v = a["q"], a["k"], a["v"]
            g = do_ref[hd, r, :]
            st0 = ssave_ref[hd, ci]
            dst = dst_ref[hd]
            scores = asave_ref[hd, ci]
            da = _dot(g, v, _NT)
            da_t = _dot(v, g, _NT)
            kb = k * a["erem"]
            qb = q * a["eb"]
            dv = _dot(scores, g, _TN) + _dot(kb, dst, _NT)
            dq_inter = _dot(g, st0) * a["eb"]
            dk_state = _dot(v, dst) * a["erem"]
            dq, dk = dq_inter, dk_state
            de = [q * dq_inter]
            da_b, dat_b = da.astype(_BF), da_t.astype(_BF)
            for n in range(len(k_diags)):
                l0, l1 = 2 * n, 2 * n + 1
                da_pair = jnp.concatenate([lvlb_ref[l0] * da_b, lvlb_ref[l1] * da_b], axis=1)
                dat_pair = jnp.concatenate([lvltb_ref[l0] * dat_b, lvltb_ref[l1] * dat_b], axis=1)
                dq_both = lax.dot_general(da_pair, k_diags[n], _NN, preferred_element_type=F32)
                dk_both = lax.dot_general(dat_pair, _blockdiag(ql[l0], ql[l1]), _NN, preferred_element_type=F32)
                for l, cols in ((l0, slice(0, LANE)), (l1, slice(LANE, 2 * LANE))):
                    dql, dkl = dq_both[:, cols], dk_both[:, cols]
                    if l > 0:
                        e = a["lev"][l]
                        dql, dkl = dql * e, dkl * e
                        de.append(q * dql + k * dkl)
                    dq = dq + dql
                    dk = dk + dkl
            de.append(k * dk_state)
            dst_ref[hd] = dst * a["ebc"] + _dot(g, qb, _TN)
            dbc = jnp.sum(dst * st0, axis=0, keepdims=True) * a["ebc"]
            de2 = lax.dot_general(selt_ref[...], _split2(jnp.concatenate(de, axis=0)), _NN,
                                  preferred_element_type=F32)
            dlogf = de2[:, :LANE] + de2[:, LANE:] + dbc
            sq, sg = a["sq"], a["sg"]
            df = dlogf / a["f"] - dk
            dq_buf[slot, hd, r, :] = (dq * (sq * (1.0 + hq * (1.0 - sq)))).astype(dq_buf.dtype)
            df_buf[slot, hd, r, :] = (df * a["omlb"] * sg * (1.0 - sg)).astype(df_buf.dtype)
            di_buf[slot, hd, r, :] = dv.astype(di_buf.dtype)
            cols = slice(hd * LANE, (hd + 1) * LANE)
            dlb_ref[:, cols] = dlb_ref[:, cols] + jnp.sum(df * (1.0 - sg), axis=0, keepdims=True)

        lax.fori_loop(0, ncb, chunk, 0, unroll=SCAN_UNROLL)
        _writeback_commit(step, (N_HEADS // 2) * ntb, slot, out_copies)

    pair = lambda base: pl.BlockSpec((2, tb, LANE), lambda p, t, base=base: (base // 2 + p, ntb - 1 - t, 0))
    any_spec = pl.BlockSpec(memory_space=pl.ANY)
    per_chunk = pl.BlockSpec((2, ncb, LANE, LANE), lambda p, t: (p, ntb - 1 - t, 0, 0))
    return pl.pallas_call(
        body, name="hgrn_bwd", grid=(N_HEADS // 2, ntb),
        in_specs=[pair(CB_HQ), pair(CB_HF), pair(CB_HI),
                  pl.BlockSpec((1, 2 * LANE), lambda p, t: (0, p)),
                  pair(0), per_chunk, per_chunk]
        + [_resident(c) for c in operands] + [any_spec],
        out_specs=[any_spec, pl.BlockSpec((1, 2 * LANE), lambda p, t: (0, p))],
        out_shape=[jax.ShapeDtypeStruct(dpb.shape, dpb.dtype), jax.ShapeDtypeStruct((1, D_MODEL), F32)],
        scratch_shapes=[pltpu.VMEM((2, 2, tb, LANE), dpb.dtype)] * 3 + [
            pltpu.VMEM((2, LANE, LANE), F32), pltpu.VMEM((2, CHUNK, LANE), F32), pltpu.SemaphoreType.DMA((2, 3))],
        input_output_aliases={7 + len(operands): 0},
        compiler_params=_params(("arbitrary", "arbitrary")),
    )(pb, pb, pb, lb, do, ssave, asave, *operands, dpb)


def _rope_tables(t_len):
    half = RET_DK // 2
    inv_freq = (1.0 / (np.float32(ROPE_BASE) ** np.linspace(0.0, 1.0, half, dtype=np.float32))).astype(np.float32)
    ang = (np.arange(t_len, dtype=np.float32)[:, None] * inv_freq[None, :]).astype(np.float64)
    cos, sin = np.cos(ang).astype(np.float32), np.sin(ang).astype(np.float32)
    cos_t = np.concatenate([cos, cos, cos, cos], axis=1)
    sin_t = np.concatenate([-sin, sin, -sin, sin], axis=1)
    return jnp.asarray(cos_t), jnp.asarray(sin_t)


def _swap_halves(v):
    half = RET_DK // 2
    lane = lax.broadcasted_iota(jnp.int32, v.shape, 1)
    first = (lane & (RET_DK - 1)) < half
    return jnp.where(first, pltpu.roll(v, LANE - half, 1), pltpu.roll(v, half, 1))


def _ret_head_consts(hidx):
    c = CHUNK
    hf = jnp.full((1, LANE), hidx, jnp.int32).astype(F32)
    lg = jnp.log(1.0 - jnp.exp(-(5.0 + hf) * np.float32(np.log(2.0))))
    row = lax.broadcasted_iota(jnp.int32, (c, c), 0)
    col = lax.broadcasted_iota(jnp.int32, (c, c), 1)
    rel = (row - col).astype(F32)
    dm = jnp.where(rel >= 0, jnp.exp(lg[:, :1] * jnp.maximum(rel, 0.0)), 0.0)
    dm_t = jnp.where(rel <= 0, jnp.exp(lg[:, :1] * jnp.maximum(-rel, 0.0)), 0.0)
    idx = lax.broadcasted_iota(jnp.int32, (c, LANE), 0).astype(F32)
    zeta = jnp.exp(lg * (c - 1.0 - idx))
    xi = jnp.exp(lg * (idx + 1.0))
    cdec = jnp.exp(lg * float(c))
    return dm, zeta, xi, cdec, dm_t


def _lane_mask(which):
    lane = lax.broadcasted_iota(jnp.int32, (1, LANE), 1)
    return ((lane // RET_DK) == which).astype(F32)


def retention_forward(pb, cos_t, sin_t, t_len):
    nc = t_len // CHUNK

    tb = _time_block(t_len)
    ncb = tb // CHUNK

    def body(rq_ref, rk_ref, rv_ref, cos_ref, sin_ref, o_ref, rsave_ref, st_ref):
        p = pl.program_id(0)

        @pl.when(pl.program_id(1) == 0)
        def _():
            st_ref[...] = jnp.zeros_like(st_ref)

        consts = [_ret_head_consts(2 * p + hd) for hd in range(2)]

        def chunk(ci, carry):
            r = pl.ds(pl.multiple_of(ci * CHUNK, CHUNK), CHUNK)
            cs, sn = cos_ref[r, :], sin_ref[r, :]
            q = rq_ref[r, :]
            k = rk_ref[r, :]
            q = q * cs + _swap_halves(q) * sn
            k = (k * cs + _swap_halves(k) * sn) * RET_DK ** -0.5
            for hd in range(2):
                dm, zeta, xi, cdec, _ = consts[hd]
                lm = _lane_mask(hd)
                qh, kh = q * lm, k * lm
                v = rv_ref[hd, r, :]
                st = st_ref[hd]
                rsave_ref[hd, ci] = st
                scores = _dot(qh, kh, _NT) * dm
                o_ref[hd, r, :] = _dot(scores, v) + _dot(qh * xi, st, _NT)
                st_ref[hd] = st * cdec + _dot(v, kh * zeta, _TN)
            return carry

        lax.fori_loop(0, ncb, chunk, 0, unroll=RET_UNROLL)

    return pl.pallas_call(
        body, name="ret_fwd", grid=(N_HEADS // 2, t_len // tb),
        in_specs=[pl.BlockSpec((None, tb, LANE), lambda p, t: (CB_RQ + p, t, 0)),
                  pl.BlockSpec((None, tb, LANE), lambda p, t: (CB_RK + p, t, 0)),
                  pl.BlockSpec((2, tb, LANE), lambda p, t: (CB_RV // 2 + p, t, 0)),
                  pl.BlockSpec((tb, LANE), lambda p, t: (t, 0)),
                  pl.BlockSpec((tb, LANE), lambda p, t: (t, 0))],
        out_specs=[pl.BlockSpec((2, tb, LANE), lambda p, t: (p, t, 0)),
                   pl.BlockSpec((2, ncb, LANE, LANE), lambda p, t: (p, t, 0, 0))],
        out_shape=[jax.ShapeDtypeStruct((N_HEADS, t_len, LANE), F32),
                   jax.ShapeDtypeStruct((N_HEADS, nc, LANE, LANE), F32)],
        scratch_shapes=[pltpu.VMEM((2, LANE, LANE), F32)],
        compiler_params=_params(("arbitrary", "arbitrary")),
    )(pb, pb, pb, cos_t, sin_t)


def retention_backward(pb, cos_t, sin_t, do, rsave, dpb, t_len):
    tb = _time_block(t_len)
    ncb, ntb = tb // CHUNK, t_len // tb

    def body(rq_ref, rk_ref, rv_ref, cos_ref, sin_ref, do_ref, rsave_ref, dpb_in,
             dpb_ref, dq_buf, dk_buf, dv_buf, dst_ref, sems):
        del dpb_in
        p, t = pl.program_id(0), pl.program_id(1)
        step = p * ntb + t
        rows = pl.ds(pl.multiple_of((ntb - 1 - t) * tb, tb), tb)

        def out_copies(sl):
            return [pltpu.make_async_copy(dq_buf.at[sl], dpb_ref.at[CB_RQ + p, rows], sems.at[sl, 0]),
                    pltpu.make_async_copy(dk_buf.at[sl], dpb_ref.at[CB_RK + p, rows], sems.at[sl, 1]),
                    pltpu.make_async_copy(dv_buf.at[sl], dpb_ref.at[pl.ds(CB_RV + 2 * p, 2), rows], sems.at[sl, 2])]

        slot = _writeback_reserve(step, out_copies)

        @pl.when(t == 0)
        def _():
            dst_ref[...] = jnp.zeros_like(dst_ref)

        consts = [_ret_head_consts(2 * p + hd) for hd in range(2)]

        def chunk(i, carry):
            ci = ncb - 1 - i
            r = pl.ds(pl.multiple_of(ci * CHUNK, CHUNK), CHUNK)
            cs, sn = cos_ref[r, :], sin_ref[r, :]
            q = rq_ref[r, :]
            k = rk_ref[r, :]
            q = q * cs + _swap_halves(q) * sn
            k = (k * cs + _swap_halves(k) * sn) * RET_DK ** -0.5
            dq, dk = None, None
            for hd in range(2):
                dm, zeta, xi, cdec, dm_t = consts[hd]
                lm = _lane_mask(hd)
                qh, kh = q * lm, k * lm
                v = rv_ref[hd, r, :]
                g = do_ref[hd, r, :]
                st0 = rsave_ref[hd, ci]
                dst = dst_ref[hd]
                scores_t = _dot(kh, qh, _NT) * dm_t
                dsc = _dot(g, v, _NT) * dm
                dsc_t = _dot(v, g, _NT) * dm_t
                dqh = _dot(dsc, kh) + _dot(g, st0) * xi
                dkh = _dot(dsc_t, qh) + _dot(v, dst) * zeta
                dv_buf[slot, hd, r, :] = (_dot(scores_t, g) + _dot(kh * zeta, dst, _NT)).astype(dv_buf.dtype)
                dst_ref[hd] = dst * cdec + _dot(g, qh * xi, _TN)
                dq = dqh if dq is None else dq + dqh
                dk = dkh if dk is None else dk + dkh
            dk = dk * (RET_DK ** -0.5)
            dq_buf[slot, r, :] = (dq * cs - _swap_halves(dq) * sn).astype(dq_buf.dtype)
            dk_buf[slot, r, :] = (dk * cs - _swap_halves(dk) * sn).astype(dk_buf.dtype)
            return carry

        lax.fori_loop(0, ncb, chunk, 0, unroll=RET_UNROLL)
        _writeback_commit(step, (N_HEADS // 2) * ntb, slot, out_copies)

    any_spec = pl.BlockSpec(memory_space=pl.ANY)
    return pl.pallas_call(
        body, name="ret_bwd", grid=(N_HEADS // 2, ntb),
        in_specs=[pl.BlockSpec((None, tb, LANE), lambda p, t: (CB_RQ + p, ntb - 1 - t, 0)),
                  pl.BlockSpec((None, tb, LANE), lambda p, t: (CB_RK + p, ntb - 1 - t, 0)),
                  pl.BlockSpec((2, tb, LANE), lambda p, t: (CB_RV // 2 + p, ntb - 1 - t, 0)),
                  pl.BlockSpec((tb, LANE), lambda p, t: (ntb - 1 - t, 0)),
                  pl.BlockSpec((tb, LANE), lambda p, t: (ntb - 1 - t, 0)),
                  pl.BlockSpec((2, tb, LANE), lambda p, t: (p, ntb - 1 - t, 0)),
                  pl.BlockSpec((2, ncb, LANE, LANE), lambda p, t: (p, ntb - 1 - t, 0, 0)),
                  any_spec],
        out_specs=any_spec,
        out_shape=jax.ShapeDtypeStruct(dpb.shape, dpb.dtype),
        scratch_shapes=[pltpu.VMEM((2, tb, LANE), dpb.dtype), pltpu.VMEM((2, tb, LANE), dpb.dtype),
                        pltpu.VMEM((2, 2, tb, LANE), dpb.dtype), pltpu.VMEM((2, LANE, LANE), F32),
                        pltpu.SemaphoreType.DMA((2, 3))],
        input_output_aliases={7: 0},
        compiler_params=_params(("arbitrary", "arbitrary")),
    )(pb, pb, pb, cos_t, sin_t, do, rsave, dpb)


def _row_tile(t_len, want):
    return min(want, t_len)


PAIR_CB = 2 * CB_PER_SHARD


def proj_forward(h, wt, t_len):
    tm = _row_tile(t_len, 1024)

    def body(h_ref, w_ref, o_ref):
        acc = _dot(h_ref[...], w_ref[...], _NT)
        for jj in range(PAIR_CB):
            o_ref[jj] = acc[:, jj * LANE:(jj + 1) * LANE]

    return pl.pallas_call(
        body, name="proj_fwd", grid=(N_DEV // 2, t_len // tm),
        in_specs=[pl.BlockSpec((tm, D_MODEL), lambda j, i: (i, 0)),
                  pl.BlockSpec((PAIR_CB * LANE, D_MODEL), lambda j, i: (j, 0))],
        out_specs=pl.BlockSpec((PAIR_CB, tm, LANE), lambda j, i: (j, i, 0)),
        out_shape=jax.ShapeDtypeStruct((N_CB, t_len, LANE), F32),
        compiler_params=_params(("arbitrary", "arbitrary")),
    )(h, wt)


def proj_backward_input(dpb, wt, token, x, dy, norm_g, scale1p, t_len):
    tm = _row_tile(t_len, 512)

    def body(a_ref, wt_hbm, token_ref, x_ref, dy_ref, g_ref, sc_ref, gx_ref, vec_ref, w_ref, sem):
        del token_ref
        i = pl.program_id(0)

        @pl.when(i == 0)
        def _():
            cp = pltpu.make_async_copy(wt_hbm, w_ref, sem)
            cp.start()
            cp.wait()

        a = jnp.concatenate([a_ref[jj].astype(_BF) for jj in range(N_CB)], axis=1)
        dhv = _dot(a, w_ref[...])
        xv, g, sc = x_ref[...], g_ref[...], sc_ref[...]
        r = lax.rsqrt(jnp.mean(xv * xv, axis=-1, keepdims=True) + EPS)
        xn = xv * r
        dxn = dhv * (g * sc)
        gx_ref[...] = dy_ref[...] + r * dxn - xn * (r * r) * jnp.mean(xv * dxn, axis=-1, keepdims=True)
        t = dhv * xn
        _acc_rows(vec_ref, i, [jnp.sum(t * sc, axis=0, keepdims=True),
                               jnp.sum(t * g, axis=0, keepdims=True),
                               jnp.sum(dhv, axis=0, keepdims=True)])

    row = pl.BlockSpec((tm, D_MODEL), lambda i: (i, 0))
    return pl.pallas_call(
        body, name="proj_bwd_input", grid=(t_len // tm,),
        in_specs=[pl.BlockSpec((N_CB, tm, LANE), lambda i: (0, i, 0)),
                  pl.BlockSpec(memory_space=pl.ANY),
                  pl.BlockSpec(token.shape, lambda i: (0, 0)),
                  row, row, _vec_spec(), _vec_spec()],
        out_specs=[row, pl.BlockSpec((8, D_MODEL), lambda i: (0, 0))],
        out_shape=[jax.ShapeDtypeStruct((t_len, D_MODEL), F32), jax.ShapeDtypeStruct((8, D_MODEL), F32)],
        scratch_shapes=[pltpu.VMEM(wt.shape, wt.dtype), pltpu.SemaphoreType.DMA],
        compiler_params=_params(("arbitrary",)),
    )(dpb, wt, token, x, dy, norm_g, scale1p)


def proj_backward_weight(h_t, dpb, t_len):
    tk = _row_tile(t_len, 2048)

    def body(h_ref, b_ref, o_ref):
        k = pl.program_id(1)
        b = jnp.concatenate([b_ref[jj].astype(_BF) for jj in range(PAIR_CB)], axis=1)
        part = _dot(h_ref[...], b)

        @pl.when(k == 0)
        def _():
            for s in range(2):
                o_ref[s] = part[:, s * SHARD_IN:(s + 1) * SHARD_IN]

        @pl.when(k > 0)
        def _():
            for s in range(2):
                o_ref[s] = o_ref[s] + part[:, s * SHARD_IN:(s + 1) * SHARD_IN]

    return pl.pallas_call(
        body, name="proj_bwd_weight", grid=(N_DEV // 2, t_len // tk),
        in_specs=[pl.BlockSpec((D_MODEL, tk), lambda j, k: (0, k)),
                  pl.BlockSpec((PAIR_CB, tk, LANE), lambda j, k: (j, k, 0))],
        out_specs=pl.BlockSpec((2, D_MODEL, SHARD_IN), lambda j, k: (j, 0, 0)),
        out_shape=jax.ShapeDtypeStruct((N_DEV, D_MODEL, SHARD_IN), F32),
        compiler_params=_params(("arbitrary", "arbitrary")),
    )(h_t, dpb)


def sibling_blocks(g_in, c_idx):
    tr = D_MODEL

    def body(c_ref, g_ref, o_ref):
        del c_ref
        o_ref[...] = g_ref[...].astype(o_ref.dtype)

    return pl.pallas_call(
        body, name="sibling_blocks",
        grid_spec=pltpu.PrefetchScalarGridSpec(
            num_scalar_prefetch=1, grid=(N_DEV // 2, D_MODEL // tr),
            in_specs=[pl.BlockSpec((None, tr, SHARD_IN), lambda q, i, c: (2 * q + 1 - c[0], i, 0))],
            out_specs=pl.BlockSpec((None, tr, SHARD_IN), lambda q, i, c: (q, i, 0))),
        out_shape=jax.ShapeDtypeStruct((N_DEV // 2, D_MODEL, SHARD_IN), _BF),
        compiler_params=_params(("arbitrary", "arbitrary")),
    )(c_idx, g_in)


def _vec_spec():
    return pl.BlockSpec((1, D_MODEL), lambda i: (0, 0))


def _acc_rows(ref, i, rows):
    @pl.when(i == 0)
    def _():
        ref[...] = jnp.zeros_like(ref)

    for n, row in enumerate(rows):
        ref[n:n + 1, :] = ref[n:n + 1, :] + row


def adaln_forward(x, norm_g, scale1p, shift, t_len):
    tm = _row_tile(t_len, 512)

    def body(x_ref, g_ref, sc_ref, sh_ref, h_ref, ht_ref):
        xv = x_ref[...]
        r = lax.rsqrt(jnp.mean(xv * xv, axis=-1, keepdims=True) + EPS)
        h = xv * r * g_ref[...] * sc_ref[...] + sh_ref[...]
        h_ref[...] = h.astype(h_ref.dtype)
        ht_ref[...] = h.T.astype(ht_ref.dtype)

    return pl.pallas_call(
        body, name="adaln_fwd", grid=(t_len // tm,),
        in_specs=[pl.BlockSpec((tm, D_MODEL), lambda i: (i, 0)), _vec_spec(), _vec_spec(), _vec_spec()],
        out_specs=[pl.BlockSpec((tm, D_MODEL), lambda i: (i, 0)), pl.BlockSpec((D_MODEL, tm), lambda i: (0, i))],
        out_shape=[jax.ShapeDtypeStruct((t_len, D_MODEL), _BF), jax.ShapeDtypeStruct((D_MODEL, t_len), _BF)],
        compiler_params=_params(("arbitrary",)),
    )(x, norm_g, scale1p, shift)


def _head_norm(o, g):
    r = lax.rsqrt(jnp.mean(o * o, axis=-1, keepdims=True) + EPS)
    return r, o * r * g


def _group_spec(tm, cb):
    return pl.BlockSpec((N_HEADS, tm, LANE), lambda i, cb=cb: (cb // N_HEADS, i, 0))


MID_FINAL_G, MID_GATE, MID_LOSS, MID_HG_G, MID_RET_G = range(5)


def middle(x, target, oa, ob, pb, wout, gate, final_g, hg_g, ret_g, t_len):
    tm = _row_tile(t_len, 256)
    n_steps = t_len // tm

    def body(x_ref, t_ref, oa_ref, ob_ref, hz_ref, rz_ref, ga_ref, gb_ref, w_ref, gate_ref, fg_ref, hg_ref, rg_ref,
             dy_ref, doa_ref, dob_ref, dw_ref, vec_ref, dpb_ref, m_scr, dm_scr, keep, bufs, sems):
        i = pl.program_id(0)
        rows = pl.ds(pl.multiple_of(i * tm, tm), tm)

        def group_copies(sl):
            return [pltpu.make_async_copy(bufs.at[sl, n], dpb_ref.at[pl.ds(cb, N_HEADS), rows], sems.at[sl, n])
                    for n, cb in enumerate((CB_HZ, CB_RZ, CB_GA, CB_GB))]
        sides = ((oa_ref, hz_ref, ga_ref, hg_ref, doa_ref), (ob_ref, rz_ref, gb_ref, rg_ref, dob_ref))
        for hh in range(N_HEADS):
            ls = slice(hh * LANE, (hh + 1) * LANE)
            acc = None
            for side, (o_ref, z_ref, gt_ref, g_ref, _) in enumerate(sides):
                o = o_ref[hh]
                rr = lax.rsqrt(jnp.mean(o * o, axis=-1, keepdims=True) + EPS)
                orr = o * rr
                zz = z_ref[hh]
                sz = _sigmoid(zz)
                sgt = _sigmoid(gt_ref[hh])
                keep[side, hh, 0] = orr
                keep[side, hh, 1] = sz
                keep[side, hh, 2] = sgt
                keep[side, hh, 3] = jnp.broadcast_to(rr, orr.shape)
                u = sgt * ((orr * g_ref[:, ls]) * (zz * sz))
                acc = u if acc is None else acc + u
            m_scr[:, ls] = acc.astype(m_scr.dtype)
        zv = _dot(m_scr[...], w_ref[...])
        gt, fg = gate_ref[...], fg_ref[...]
        y = x_ref[...] + gt * zv
        r = lax.rsqrt(jnp.mean(y * y, axis=-1, keepdims=True) + EPS)
        yn = y * r
        err = yn * fg - t_ref[...]
        loss = 0.5 * jnp.sum(jnp.mean(err * err, axis=-1, keepdims=True), axis=0, keepdims=True)
        dout = err * (1.0 / D_MODEL)
        gd = dout * fg
        dy = r * gd - yn * (r * r) * jnp.mean(y * gd, axis=-1, keepdims=True)
        dy_ref[...] = dy
        dz = (dy * gt).astype(_BF)
        dm_scr[...] = _dot(dz, w_ref[...], _NT)
        part = _dot(m_scr[...], dz, _TN)

        @pl.when(i == 0)
        def _():
            dw_ref[...] = part

        @pl.when(i > 0)
        def _():
            dw_ref[...] = dw_ref[...] + part

        slot = _writeback_reserve(i, group_copies)
        dg = [[], []]
        for hh in range(N_HEADS):
            ls = slice(hh * LANE, (hh + 1) * LANE)
            dmh = dm_scr[:, ls]
            for side, (o_ref, z_ref, gt_ref, g_ref, do_ref) in enumerate(sides):
                zz, g = z_ref[hh], g_ref[:, ls]
                orr, sz, sgt, rr = keep[side, hh, 0], keep[side, hh, 1], keep[side, hh, 2], keep[side, hh, 3]
                n = orr * g
                silu = zz * sz
                du = dmh * sgt
                bufs[slot, 2 + side, hh] = (dmh * (n * silu) * (sgt * (1.0 - sgt))).astype(bufs.dtype)
                bufs[slot, side, hh] = (du * n * (sz * (1.0 + zz * (1.0 - sz)))).astype(bufs.dtype)
                dn = du * silu
                dg[side].append(jnp.sum(dn * orr, axis=0, keepdims=True))
                gdn = dn * g
                do_ref[hh] = (rr * (gdn - orr * jnp.mean(orr * gdn, axis=-1, keepdims=True))).astype(do_ref.dtype)
        _acc_rows(vec_ref, i, [jnp.sum(dout * yn, axis=0, keepdims=True),
                               jnp.sum(dy * zv, axis=0, keepdims=True),
                               jnp.broadcast_to(loss, (1, D_MODEL)),
                               jnp.concatenate(dg[0], axis=1), jnp.concatenate(dg[1], axis=1)])
        _writeback_commit(i, n_steps, slot, group_copies)

    row = pl.BlockSpec((tm, D_MODEL), lambda i: (i, 0))
    head = pl.BlockSpec((N_HEADS, tm, LANE), lambda i: (0, i, 0))
    full = pl.BlockSpec((D_MODEL, D_MODEL), lambda i: (0, 0))
    return pl.pallas_call(
        body, name="middle", grid=(n_steps,),
        in_specs=[row, row, head, head, _group_spec(tm, CB_HZ), _group_spec(tm, CB_RZ), _group_spec(tm, CB_GA),
                  _group_spec(tm, CB_GB), full, _vec_spec(), _vec_spec(), _vec_spec(), _vec_spec()],
        out_specs=[row, head, head, full, pl.BlockSpec((8, D_MODEL), lambda i: (0, 0)),
                   pl.BlockSpec(memory_space=pl.ANY)],
        out_shape=[jax.ShapeDtypeStruct((t_len, D_MODEL), F32),
                   jax.ShapeDtypeStruct((N_HEADS, t_len, LANE), _BF),
                   jax.ShapeDtypeStruct((N_HEADS, t_len, LANE), _BF),
                   jax.ShapeDtypeStruct((D_MODEL, D_MODEL), F32),
                   jax.ShapeDtypeStruct((8, D_MODEL), F32),
                   jax.ShapeDtypeStruct((N_CB, t_len, LANE), _BF)],
        scratch_shapes=[pltpu.VMEM((tm, D_MODEL), _BF), pltpu.VMEM((tm, D_MODEL), F32),
                        pltpu.VMEM((2, N_HEADS, 4, tm, LANE), F32),
                        pltpu.VMEM((2, 4, N_HEADS, tm, LANE), _BF), pltpu.SemaphoreType.DMA((2, 4))],
        compiler_params=_params(("arbitrary",)),
    )(x, target, oa, ob, pb, pb, pb, pb, wout, gate, final_g, hg_g, ret_g)


def device_step(x, target, mod, lb, project, norm_g, hg_g, ret_g, final_g, c_idx=None, start_exchange=None):
    t_len = x.shape[0]
    shift, scale, gate = mod[:, :D_MODEL], mod[:, D_MODEL:2 * D_MODEL], mod[:, 2 * D_MODEL:]
    scale1p = 1.0 + scale
    cos_t, sin_t = _rope_tables(t_len)
    h, h_t = adaln_forward(x, norm_g, scale1p, shift, t_len)
    pb, wt, wout = project(h)
    oa, ssave, asave = hgrn_forward(pb, lb, t_len)
    ob, rsave = retention_forward(pb, cos_t, sin_t, t_len)
    dy, doa, dob, dwout, vec_mid, dpb = middle(x, target, oa, ob, pb, wout, gate, final_g, hg_g, ret_g, t_len)
    dpb, dlb = hgrn_backward(pb, lb, doa, ssave, asave, dpb, t_len)
    dpb = retention_backward(pb, cos_t, sin_t, dob, rsave, dpb, t_len)
    c_idx = jnp.zeros((1,), jnp.int32) if c_idx is None else c_idx
    dwin = proj_backward_weight(h_t, dpb, t_len)
    dwin_sib = sibling_blocks(dwin, c_idx)
    token, pending = (start_exchange(dwin, dwin_sib, dwout) if start_exchange
                      else (jnp.zeros((8, LANE), F32), None))
    grad_x, vec_ada = proj_backward_input(dpb, wt, token, x, dy, norm_g, scale1p, t_len)
    return grad_x, dwin, dwout, vec_mid, vec_ada, dlb, pending


PACK_ROWS = 16
ROW_NORM_G, ROW_LB, ROW_HG_G, ROW_RET_G, ROW_FINAL_G, ROW_SHIFT, ROW_SCALE, ROW_GATE, ROW_LOSS = range(9)


def _mesh_pos():
    return lax.axis_index("x"), lax.axis_index("y"), lax.axis_index("c")


def _lin(pos):
    return 4 * pos[0] + 2 * pos[1] + pos[2]


def _xor_peer(pos, k):
    return tuple(1 - p if (k >> s) & 1 else p for p, s in zip(pos, (2, 1, 0)))


def _other_chips(pos):
    x, y, _ = pos
    return [(1 - x, y), (x, 1 - y), (1 - x, 1 - y)]


def _remote(src, dst, send_sem, recv_sem, to):
    return pltpu.make_async_remote_copy(src_ref=src, dst_ref=dst, send_sem=send_sem, recv_sem=recv_sem,
                                        device_id=to, device_id_type=MESH)


def pre_exchange(c, w_ada, b_ada, logits, w_in, w_out):
    def body(c_ref, wada_ref, bada_ref, logit_ref, win_ref, wout_ref, mod_ref, scall_ref, lb_ref, wt_ref, wo_ref,
             cg_ref, modall_ref, parts_ref, send1, recv1, send2, recv2):
        pos = _mesh_pos()
        cv = c_ref[...]
        slot = lambda p: pl.ds(pl.multiple_of(8 * _lin(p), 8), 8)
        cg_ref[slot(pos), :] = jnp.broadcast_to(cv * _sigmoid(cv), (8, D_MODEL))
        lb_ref[...] = _sigmoid(logit_ref[0:1, :] - logit_ref[1:2, :])
        peers = [_xor_peer(pos, k) for k in range(1, N_DEV)]
        gather = [_remote(cg_ref.at[slot(pos)], cg_ref.at[slot(pos)], send1.at[n], recv1.at[n], p)
                  for n, p in enumerate(peers)]
        for cp in gather:
            cp.start()
        wt_ref[...] = win_ref[...].T.astype(wt_ref.dtype)
        wo_ref[...] = wout_ref[...].astype(wo_ref.dtype)
        for n, p in enumerate(peers):
            _remote(cg_ref.at[slot(p)], cg_ref.at[slot(p)], send1.at[n], recv1.at[n], p).wait_recv()
        modall_ref[...] = _dot(cg_ref[...], wada_ref[...])
        scatter = [_remote(modall_ref.at[slot(p)], parts_ref.at[slot(pos)], send2.at[n], recv2.at[n], p)
                   for n, p in enumerate(peers)]
        for cp in scatter:
            cp.start()
        parts_ref[slot(pos), :] = modall_ref[slot(pos), :]
        for n, p in enumerate(peers):
            _remote(modall_ref.at[slot(p)], parts_ref.at[slot(p)], send2.at[n], recv2.at[n], p).wait_recv()
        for cp in gather + scatter:
            cp.wait_send()
        for j in range(N_DEV):
            cols = slice(j * SHARD_ADA, (j + 1) * SHARD_ADA)
            mod_ref[:, cols] = parts_ref[8 * j:8 * j + 1, :] + bada_ref[:, cols]
            scall_ref[j:j + 1, :] = cg_ref[8 * j:8 * j + 1, :]

    vmem = pl.BlockSpec(memory_space=pltpu.VMEM)
    return pl.pallas_call(
        body, name="pre_exchange",
        in_specs=[vmem] * 6, out_specs=[vmem] * 5,
        out_shape=[jax.ShapeDtypeStruct((1, 3 * D_MODEL), F32), jax.ShapeDtypeStruct((N_DEV, D_MODEL), F32),
                   jax.ShapeDtypeStruct((1, D_MODEL), F32),
                   jax.ShapeDtypeStruct(w_in.shape[::-1], _BF), jax.ShapeDtypeStruct(w_out.shape, _BF)],
        scratch_shapes=[pltpu.VMEM((N_DEV * 8, D_MODEL), F32), pltpu.VMEM((N_DEV * 8, SHARD_ADA), F32),
                        pltpu.VMEM((N_DEV * 8, SHARD_ADA), F32)] + [pltpu.SemaphoreType.DMA((N_DEV - 1,))] * 4,
        compiler_params=pltpu.CompilerParams(vmem_limit_bytes=VMEM_LIMIT),
    )(c, w_ada, b_ada, logits, w_in, w_out)


def weight_gather(win_sh, wout_sh):
    def body(win_ref, wout_ref, wg_ref, woutg_ref, send, recv, local):
        pos = _mesh_pos()
        x, y, c = pos
        sibling = (x, y, 1 - c)

        def route(core):
            return [(x + (1 - core) * (1 - 2 * x), y + core * (1 - 2 * y)),
                    (x + core * (1 - 2 * x), y + (1 - core) * (1 - 2 * y)),
                    (1 - x, 1 - y)]

        chips, sib_chips = route(c), route(1 - c)
        mine, first, later = [], [], []
        for a, (src, out) in enumerate(((win_ref, wg_ref), (wout_ref, woutg_ref))):
            def copy(k, block, to, src_ref=None, a=a, out=out):
                dst = out.at[_lin(block)]
                return _remote(dst if src_ref is None else src_ref, dst, send.at[7 * a + k], recv.at[7 * a + k], to)
            mine.append(pltpu.make_async_copy(src, out.at[_lin(pos)], local.at[a]))
            first += [copy(0, pos, sibling, src), copy(1, pos, (*chips[0], c), src), copy(2, pos, (*chips[1], c), src)]
            later.append([[copy(3, (*chips[0], c), (*chips[1], c)), copy(4, (*chips[0], c), sibling)],
                          [copy(5, (*chips[1], c), sibling)],
                          [copy(6, (*chips[2], c), sibling)]])
        for cp in mine + first:
            cp.start()
        for j in range(3):
            for a, out in enumerate((wg_ref, woutg_ref)):
                dst = out.at[_lin((*chips[j], c))]
                _remote(dst, dst, send.at[7 * a + 1 + j], recv.at[7 * a + 1 + j], pos).wait_recv()
                for cp in later[a][j]:
                    cp.start()
        for a, out in enumerate((wg_ref, woutg_ref)):
            dst = out.at[_lin(sibling)]
            _remote(dst, dst, send.at[7 * a], recv.at[7 * a], pos).wait_recv()
            for j in range(3):
                dst = out.at[_lin((*sib_chips[j], 1 - c))]
                _remote(dst, dst, send.at[7 * a + 4 + j], recv.at[7 * a + 4 + j], pos).wait_recv()
        for cp in first + [cp for per_array in later for group in per_array for cp in group]:
            cp.wait_send()
        for cp in mine:
            cp.wait()

    any_spec = pl.BlockSpec(memory_space=pl.ANY)
    return pl.pallas_call(
        body, name="weight_gather",
        in_specs=[any_spec, any_spec], out_specs=[any_spec, any_spec],
        out_shape=[jax.ShapeDtypeStruct((N_DEV,) + win_sh.shape, win_sh.dtype),
                   jax.ShapeDtypeStruct((N_DEV,) + wout_sh.shape, wout_sh.dtype)],
        scratch_shapes=[pltpu.SemaphoreType.DMA((14,)), pltpu.SemaphoreType.DMA((14,)),
                        pltpu.SemaphoreType.DMA((2,))],
    )(win_sh, wout_sh)


def grad_pair_exchange(g_sib, g_out):
    def body(gsib_ref, gout_ref, ra_ref, rb_ref, send, recv):
        pos = _mesh_pos()
        x, y, c = pos
        sibling = (x, y, 1 - c)
        copies = []
        for q in range(4):
            copies.append(_remote(gsib_ref.at[q], ra_ref.at[q], send.at[q], recv.at[q], sibling))
            copies.append(_remote(gout_ref.at[2 * q + (1 - c)], rb_ref.at[q], send.at[4 + q], recv.at[4 + q], sibling))
        for cp in copies:
            cp.start()
        for cp in copies:
            cp.wait_recv()
        for cp in copies:
            cp.wait_send()

    any_spec = pl.BlockSpec(memory_space=pl.ANY)
    return pl.pallas_call(
        body, name="grad_pair_exchange",
        in_specs=[any_spec, any_spec], out_specs=[any_spec, any_spec],
        out_shape=[jax.ShapeDtypeStruct(g_sib.shape, g_sib.dtype), jax.ShapeDtypeStruct((4,) + g_out.shape[1:], F32)],
        scratch_shapes=[pltpu.SemaphoreType.DMA((8,)), pltpu.SemaphoreType.DMA((8,))],
    )(g_sib, g_out)


def pair_sum(g_in, ra, g_out, rb, c_idx):
    tr = D_MODEL

    def body(c_ref, gin_ref, ra_ref, gout_ref, rb_ref, sb_ref, sbo_ref):
        del c_ref
        sb_ref[...] = (gin_ref[...] + ra_ref[...].astype(F32)).astype(sb_ref.dtype)
        sbo_ref[...] = gout_ref[...] + rb_ref[...]

    n_i = D_MODEL // tr
    return pl.pallas_call(
        body, name="pair_sum",
        grid_spec=pltpu.PrefetchScalarGridSpec(
            num_scalar_prefetch=1, grid=(4, n_i),
            in_specs=[pl.BlockSpec((None, tr, SHARD_IN), lambda q, i, c: (2 * q + c[0], i, 0)),
                      pl.BlockSpec((None, tr, SHARD_IN), lambda q, i, c: (q, i, 0)),
                      pl.BlockSpec((None, SHARD_OUT // n_i, D_MODEL), lambda q, i, c: (2 * q + c[0], i, 0)),
                      pl.BlockSpec((None, SHARD_OUT // n_i, D_MODEL), lambda q, i, c: (q, i, 0))],
            out_specs=[pl.BlockSpec((None, tr, SHARD_IN), lambda q, i, c: (q, i, 0)),
                       pl.BlockSpec((None, SHARD_OUT // n_i, D_MODEL), lambda q, i, c: (q, i, 0))]),
        out_shape=[jax.ShapeDtypeStruct(ra.shape, _BF), jax.ShapeDtypeStruct(rb.shape, F32)],
        compiler_params=_params(("arbitrary", "arbitrary")),
    )(c_idx, g_in, ra, g_out, rb)


_HBM = pl.BlockSpec(memory_space=pltpu.HBM)
_SEM = pl.BlockSpec(memory_space=pltpu.SEMAPHORE)
_N_CHIP_COPIES = 6


def _chip_copies(sb_ref, sbo_ref, rc_ref, rco_ref, send, recv):
    pos = _mesh_pos()
    copies = []
    for a, (src, dst) in enumerate(((sb_ref, rc_ref), (sbo_ref, rco_ref))):
        for j, chip in enumerate(_other_chips(pos)):
            copies.append(_remote(src.at[2 * chip[0] + chip[1]], dst.at[j], send.at[3 * a + j], recv.at[3 * a + j],
                                  (*chip, pos[2])))
    return copies


def grad_chip_start(sb, sbo):
    def body(sb_ref, sbo_ref, rc_ref, rco_ref, send, recv, sb_thru, sbo_thru, rc_thru, rco_thru, token):
        del sb_thru, sbo_thru, rc_thru, rco_thru
        for cp in _chip_copies(sb_ref, sbo_ref, rc_ref, rco_ref, send, recv):
            cp.start()
        token[...] = jnp.zeros_like(token)

    hbm = lambda a: pltpu.with_memory_space_constraint(a, pltpu.HBM)
    rc = lax.empty((3,) + sb.shape[1:], sb.dtype)
    rco = lax.empty((3,) + sbo.shape[1:], sbo.dtype)
    return pl.pallas_call(
        body, name="grad_chip_start",
        in_specs=[_HBM] * 4,
        out_specs=[_SEM, _SEM, _HBM, _HBM, _HBM, _HBM, pl.BlockSpec(memory_space=pltpu.VMEM)],
        out_shape=[pltpu.SemaphoreType.DMA((_N_CHIP_COPIES,)), pltpu.SemaphoreType.DMA((_N_CHIP_COPIES,)),
                   pltpu.HBM(sb.shape, sb.dtype), pltpu.HBM(sbo.shape, sbo.dtype),
                   pltpu.HBM(rc.shape, rc.dtype), pltpu.HBM(rco.shape, rco.dtype),
                   jax.ShapeDtypeStruct((8, LANE), F32)],
        input_output_aliases={0: 2, 1: 3, 2: 4, 3: 5},
        compiler_params=pltpu.CompilerParams(has_side_effects=pltpu.SideEffectType.DATAFLOW_SIDE_EFFECTING),
    )(hbm(sb), hbm(sbo), hbm(rc), hbm(rco))


def grad_chip_wait(send, recv, sb, sbo, rc, rco, after):
    def body(sb_ref, sbo_ref, rc_ref, rco_ref, send, recv, after_ref, sb_o, sbo_o, rc_o, rco_o):
        del after_ref, sb_o, sbo_o, rc_o, rco_o
        for cp in _chip_copies(sb_ref, sbo_ref, rc_ref, rco_ref, send, recv):
            cp.wait_send()
            cp.wait_recv()

    return pl.pallas_call(
        body, name="grad_chip_wait",
        in_specs=[_HBM] * 4 + [_SEM, _SEM, pl.BlockSpec(memory_space=pl.ANY)],
        out_specs=[_HBM] * 4,
        out_shape=[pltpu.HBM(sb.shape, sb.dtype), pltpu.HBM(sbo.shape, sbo.dtype),
                   pltpu.HBM(rc.shape, rc.dtype), pltpu.HBM(rco.shape, rco.dtype)],
        input_output_aliases={0: 0, 1: 1, 2: 2, 3: 3},
        compiler_params=pltpu.CompilerParams(has_side_effects=pltpu.SideEffectType.DATAFLOW_SIDE_EFFECTING),
    )(sb, sbo, rc, rco, send, recv, after)


def pack_gather(pack):
    def body(pack_ref, packs_ref, psend, precv):
        pos = _mesh_pos()
        me = _lin(pos)
        packs_ref[me] = pack_ref[...]
        peers = [_xor_peer(pos, k) for k in range(1, N_DEV)]
        gather = [_remote(packs_ref.at[me], packs_ref.at[me], psend.at[n], precv.at[n], p) for n, p in enumerate(peers)]
        for cp in gather:
            cp.start()
        for n, p in enumerate(peers):
            _remote(packs_ref.at[_lin(p)], packs_ref.at[_lin(p)], psend.at[n], precv.at[n], p).wait_recv()
        for cp in gather:
            cp.wait_send()

    vmem = pl.BlockSpec(memory_space=pltpu.VMEM)
    return pl.pallas_call(
        body, name="pack_gather", in_specs=[vmem], out_specs=vmem,
        out_shape=jax.ShapeDtypeStruct((N_DEV,) + pack.shape, F32),
        scratch_shapes=[pltpu.SemaphoreType.DMA((N_DEV - 1,)), pltpu.SemaphoreType.DMA((N_DEV - 1,))],
    )(pack)


def pack_rows(vec_mid, vec_ada, dlb):
    def body(mid_ref, ada_ref, dlb_ref, o_ref):
        mid = lambda r: mid_ref[r:r + 1, :]
        rows = [ada_ref[0:1, :], dlb_ref[...], mid(MID_HG_G), mid(MID_RET_G), mid(MID_FINAL_G),
                ada_ref[2:3, :], ada_ref[1:2, :], mid(MID_GATE), mid(MID_LOSS)]
        o_ref[...] = jnp.zeros_like(o_ref)
        for n, row in enumerate(rows):
            o_ref[n:n + 1, :] = row

    vmem = pl.BlockSpec(memory_space=pltpu.VMEM)
    return pl.pallas_call(body, name="pack_rows", in_specs=[vmem] * 3, out_specs=vmem,
                          out_shape=jax.ShapeDtypeStruct((PACK_ROWS, D_MODEL), F32))(vec_mid, vec_ada, dlb)


def _adamw(w, g, m, v):
    m = ADAM_B1 * m + (1.0 - ADAM_B1) * g
    v = ADAM_B2 * v + (1.0 - ADAM_B2) * (g * g)
    m_hat = m / (1.0 - ADAM_B1 ** ADAM_STEP)
    v_hat = v / (1.0 - ADAM_B2 ** ADAM_STEP)
    delta = -ADAM_LR * (m_hat / (jnp.sqrt(v_hat) + ADAM_EPS) + ADAM_WD * w)
    return delta, m, v


def adam_shard(chip_idx, own, parts, w, m, v, name):
    rows, cols = w.shape
    tr = min(rows, 256)

    def body(chip_ref, p0, p1, p2, p3, w_ref, m_ref, v_ref, g_ref, d_ref, nm_ref, nv_ref):
        del chip_ref
        g = ((p0[...].astype(F32) + p1[...].astype(F32)) + p2[...].astype(F32)) + p3[...].astype(F32)
        g_ref[...] = g
        d_ref[...], nm_ref[...], nv_ref[...] = _adamw(w_ref[...], g, m_ref[...], v_ref[...])

    part = lambda q: pl.BlockSpec((None, tr, cols), lambda i, chip, q=q: (q, i, 0))
    tile = pl.BlockSpec((tr, cols), lambda i, chip: (i, 0))
    return pl.pallas_call(
        body, name=name,
        grid_spec=pltpu.PrefetchScalarGridSpec(
            num_scalar_prefetch=1, grid=(rows // tr,),
            in_specs=[pl.BlockSpec((None, tr, cols), lambda i, chip: (chip[0], i, 0)), part(0), part(1), part(2),
                      tile, tile, tile],
            out_specs=[tile] * 4),
        out_shape=[jax.ShapeDtypeStruct(w.shape, F32)] * 4,
        compiler_params=_params(("arbitrary",)),
    )(chip_idx, own, parts, parts, parts, w, m, v)


def adam_ada(sc_t, dmod_all, me_idx, w, m, v):
    def body(me_ref, sc_ref, dm_ref, w_ref, m_ref, v_ref, g_ref, d_ref, nm_ref, nv_ref):
        del me_ref
        g = _dot_f32(sc_ref[...], dm_ref[...])
        g_ref[...] = g
        d_ref[...], nm_ref[...], nv_ref[...] = _adamw(w_ref[...], g, m_ref[...], v_ref[...])

    full = pl.BlockSpec(w.shape, lambda i, me: (0, 0))
    return pl.pallas_call(
        body, name="adam_ada",
        grid_spec=pltpu.PrefetchScalarGridSpec(
            num_scalar_prefetch=1, grid=(1,),
            in_specs=[pl.BlockSpec(sc_t.shape, lambda i, me: (0, 0)),
                      pl.BlockSpec((LANE, SHARD_ADA), lambda i, me: (0, me[0])), full, full, full],
            out_specs=[full] * 4),
        out_shape=[jax.ShapeDtypeStruct(w.shape, F32)] * 4,
        compiler_params=_params(("arbitrary",)),
    )(me_idx, sc_t, dmod_all, w, m, v)


def adam_vectors(packs, lb, params, ms, vs):
    n = len(params)

    def body(*refs):
        packs_ref, lb_ref = refs[0], refs[1]
        w_refs, m_refs, v_refs = refs[2:2 + n], refs[2 + n:2 + 2 * n], refs[2 + 2 * n:2 + 3 * n]
        loss_ref = refs[2 + 3 * n]
        outs = refs[3 + 3 * n:3 + 7 * n]
        tot_ref = refs[3 + 7 * n]
        tot = packs_ref[0]
        for d in range(1, N_DEV):
            tot = tot + packs_ref[d]
        tot_ref[...] = tot
        row = lambda r: tot_ref[r:r + 1, :]
        lbv = lb_ref[...]
        dl0 = row(ROW_LB) * lbv * (1.0 - lbv)
        grads = [[row(ROW_NORM_G)],
                 [jnp.concatenate([row(ROW_SHIFT), row(ROW_SCALE), row(ROW_GATE)], axis=1)],
                 [dl0, -dl0],
                 [row(ROW_HG_G)], [row(ROW_RET_G)], [row(ROW_FINAL_G)]]
        loss_ref[...] = tot_ref[ROW_LOSS:ROW_LOSS + 1, 0:LANE]
        for j, g_rows in enumerate(grads):
            for r, g in enumerate(g_rows):
                rs = slice(r, r + 1)
                d, nm, nv = _adamw(w_refs[j][rs, :], g, m_refs[j][rs, :], v_refs[j][rs, :])
                outs[4 * j][rs, :] = g
                outs[4 * j + 1][rs, :] = d
                outs[4 * j + 2][rs, :] = nm
                outs[4 * j + 3][rs, :] = nv

    vmem = pl.BlockSpec(memory_space=pltpu.VMEM)
    out_shape = [jax.ShapeDtypeStruct((1, LANE), F32)]
    for w in params:
        out_shape += [jax.ShapeDtypeStruct(w.shape, F32)] * 4
    return pl.pallas_call(
        body, name="adam_vectors", in_specs=[vmem] * (2 + 3 * n), out_specs=[vmem] * len(out_shape),
        out_shape=out_shape, scratch_shapes=[pltpu.VMEM((PACK_ROWS, D_MODEL), F32)],
    )(packs, lb, *params, *ms, *vs)


def kernel(x, c, norm_g, w_ada, b_ada, w_in, hg_lb_logits, hg_norm_g, ret_norm_g, w_out, final_g, loss_target, m_norm_g, m_w_ada, m_b_ada, m_w_in, m_hg_lb_logits, m_hg_norm_g, m_ret_norm_g, m_w_out, m_final_g, v_norm_g, v_w_ada, v_b_ada, v_w_in, v_hg_lb_logits, v_hg_norm_g, v_ret_norm_g, v_w_out, v_final_g):
    pos = _mesh_pos()
    me_idx = jnp.reshape(_lin(pos), (1,)).astype(jnp.int32)
    c_idx = jnp.reshape(pos[2], (1,)).astype(jnp.int32)
    vec = lambda a: a.reshape(1, D_MODEL)

    mod, scall, lb, wt_sh, wout_sh = pre_exchange(c, w_ada[0], b_ada, hg_lb_logits, w_in[0], w_out[0])
    chip_idx = jnp.reshape(2 * pos[0] + pos[1], (1,)).astype(jnp.int32)

    def project(h):
        wtg, woutg = weight_gather(wt_sh, wout_sh)
        wt = wtg.reshape(D_IN, D_MODEL)
        return proj_forward(h, wt, h.shape[0]), wt, woutg.reshape(D_MODEL, D_MODEL)

    def start_exchange(dwin, dwin_sib, dwout):
        dwout = dwout.reshape(N_DEV, SHARD_OUT, D_MODEL)
        ra, rb = grad_pair_exchange(dwin_sib, dwout)
        sb, sbo = pair_sum(dwin, ra, dwout, rb, c_idx)
        send, recv, sb, sbo, rc, rco, token = grad_chip_start(sb, sbo)
        return token, (send, recv, sb, sbo, rc, rco)

    grad_x, _, _, vec_mid, vec_ada, dlb, pending = device_step(
        x[0], loss_target[0], mod, lb, project, norm_g, hg_norm_g, ret_norm_g, vec(final_g), c_idx, start_exchange)
    packs = pack_gather(pack_rows(vec_mid, vec_ada, dlb))
    dmod_all = packs[:, ROW_SHIFT:ROW_GATE + 1, :].reshape(N_DEV, 3 * D_MODEL)
    dmod_all = jnp.pad(dmod_all, ((0, LANE - N_DEV), (0, 0)))
    sc_t = jnp.pad(scall.T, ((0, 0), (0, LANE - N_DEV)))
    g_ada, d_ada, nm_ada, nv_ada = adam_ada(sc_t, dmod_all, me_idx, w_ada[0], m_w_ada[0], v_w_ada[0])
    small = adam_vectors(
        packs, lb,
        (norm_g, b_ada, hg_lb_logits, hg_norm_g, ret_norm_g, vec(final_g)),
        (m_norm_g, m_b_ada, m_hg_lb_logits, m_hg_norm_g, m_ret_norm_g, vec(m_final_g)),
        (v_norm_g, v_b_ada, v_hg_lb_logits, v_hg_norm_g, v_ret_norm_g, vec(v_final_g)))
    loss = small[0][0, 0]
    sb, sbo, rc, rco = grad_chip_wait(*pending, small[0])
    g_in, d_in, nm_in, nv_in = adam_shard(chip_idx, sb, rc, w_in[0], m_w_in[0], v_w_in[0], "adam_w_in")
    g_out, d_out, nm_out, nv_out = adam_shard(chip_idx, sbo, rco, w_out[0], m_w_out[0], v_w_out[0], "adam_w_out")
    (g_ng, d_ng, nm_ng, nv_ng), (g_b, d_b, nm_b, nv_b), (g_lb, d_lb, nm_lb, nv_lb), (g_hg, d_hg, nm_hg, nv_hg), \
        (g_rg, d_rg, nm_rg, nv_rg), (g_fg, d_fg, nm_fg, nv_fg) = [small[1 + 4 * j:5 + 4 * j] for j in range(6)]
    flat = lambda a: a.reshape(D_MODEL)

    def group(ng, ada, b, win, lbl, hg, rg, wo, fg):
        return (ng, ada[None], b, win[None], lbl, hg, rg, wo[None], flat(fg))

    return (loss, grad_x[None],
            *group(g_ng, g_ada, g_b, g_in, g_lb, g_hg, g_rg, g_out, g_fg),
            *group(d_ng, d_ada, d_b, d_in, d_lb, d_hg, d_rg, d_out, d_fg),
            *group(nm_ng, nm_ada, nm_b, nm_in, nm_lb, nm_hg, nm_rg, nm_out, nm_fg),
            *group(nv_ng, nv_ada, nv_b, nv_in, nv_lb, nv_hg, nv_rg, nv_out, nv_fg))
```

```python
import numpy as np
import jax
import jax.numpy as jnp
from jax import lax
from jax.experimental import pallas as pl
from jax.experimental.pallas import tpu as pltpu

F32 = jnp.float32
_BF = jnp.bfloat16

D_MODEL = 1024
N_HEADS = 8
LANE = 128
RET_DK = 64
D_IN = 9216
N_DEV = 8
SHARD_IN = D_IN // N_DEV
SHARD_ADA = 3 * D_MODEL // N_DEV
SHARD_OUT = D_MODEL // N_DEV
N_CB = D_IN // LANE
CB_PER_SHARD = SHARD_IN // LANE
CHUNK = 128
N_LEVELS = 7
EPS = 1e-6
LOG2_E = float(np.log2(np.e))
ROPE_BASE = 10000.0
CB_HQ, CB_HF, CB_HI, CB_HZ, CB_RQ, CB_RK, CB_RV, CB_RZ, CB_GA, CB_GB = 0, 8, 16, 24, 32, 36, 40, 48, 56, 64
VMEM_LIMIT = 56 * 1024 * 1024

ADAM_LR, ADAM_B1, ADAM_B2, ADAM_EPS, ADAM_WD, ADAM_STEP = 0.001, 0.9, 0.999, 1e-08, 0.01, 10

_NN = (((1,), (0,)), ((), ()))
_NT = (((1,), (1,)), ((), ()))
_TN = (((0,), (0,)), ((), ()))
MESH = pl.DeviceIdType.MESH


def _dot(a, b, dims=_NN):
    return lax.dot_general(a.astype(_BF), b.astype(_BF), dims, preferred_element_type=F32)


def _split2(a):
    hi = a.astype(_BF)
    lo = (a - hi.astype(F32)).astype(_BF)
    return jnp.concatenate([hi, lo], axis=1)


def _dot_sel(sel, a):
    n = a.shape[1]
    r = lax.dot_general(sel.astype(_BF), _split2(a), _NN, preferred_element_type=F32)
    return r[:, :n] + r[:, n:]


def _dot_f32(a, b):
    def pieces(v):
        p1 = v.astype(_BF)
        r1 = v - p1.astype(F32)
        p2 = r1.astype(_BF)
        p3 = (r1 - p2.astype(F32)).astype(_BF)
        return p1, p2, p3
    a1, a2, a3 = pieces(a)
    b1, b2, b3 = pieces(b)
    d = lambda u, v: lax.dot_general(u, v, _NN, preferred_element_type=F32)
    return ((d(a1, b3) + d(a2, b2) + d(a3, b1)) + (d(a1, b2) + d(a2, b1))) + d(a1, b1)


def _sigmoid(v):
    return 1.0 / (1.0 + jnp.exp(-v))


def _params(sem=None):
    return pltpu.CompilerParams(dimension_semantics=sem, vmem_limit_bytes=VMEM_LIMIT)


def _hgrn_consts():
    c, nl = CHUNK, N_LEVELS
    t = np.arange(c)[:, None]
    j = np.arange(c)[None, :]
    sel = [j <= t]
    masks = [j == t]
    for l in range(1, nl + 1):
        m = ((t >> l) << l) + (1 << (l - 1)) - 1
        sec = t > m
        sel.append(np.where(sec, (j > m) & (j <= t), (j > t) & (j <= m)))
        same = (t >> l) == (j >> l)
        masks.append(same & sec & (j <= m))
    sel.append(j > t)
    sel = np.concatenate(sel, 0).astype(np.float32)
    masks = np.stack(masks).astype(np.float32)
    sgn = np.stack([np.where((t & (1 << (l - 1))) != 0, 1.0, -1.0) * np.ones((1, LANE)) for l in range(3, nl + 1)])
    return dict(tri=jnp.asarray(sel[:c], _BF),
                lvl=jnp.asarray(masks, F32),
                sgn=jnp.asarray(sgn, F32),
                sel_t=jnp.asarray(sel.T, _BF),
                lvl_b=jnp.asarray(masks, _BF),
                lvlt_b=jnp.asarray(np.swapaxes(masks, 1, 2), _BF))


def _level_exponents(b, logf, b_scr, sgn_ref):
    c = CHUNK
    b_scr[...] = b
    row = lax.broadcasted_iota(jnp.int32, (c, LANE), 0)
    nxt = pltpu.roll(logf, c - 1, 0)
    prv = pltpu.roll(logf, 1, 0)
    r4 = row & 3
    out = [jnp.where((row & 1) == 1, logf, 0.0),
           jnp.where(r4 == 0, nxt, jnp.where(r4 == 1, 0.0, jnp.where(r4 == 2, logf, logf + prv)))]
    for l in range(3, N_LEVELS + 1):
        size, half = 1 << l, 1 << (l - 1)
        ref = jnp.concatenate([jnp.broadcast_to(b_scr[i * size + half - 1:i * size + half, :], (size, LANE))
                               for i in range(c // size)], axis=0)
        out.append((b - ref) * sgn_ref[l - 3])
    return out


def _hgrn_chunk(hq, hf, hi, lbv, tri_ref, sgn_ref, b_scr):
    sq = _sigmoid(hq)
    q = hq * sq
    sg = _sigmoid(hf)
    omlb = 1.0 - lbv
    f = lbv + omlb * sg
    k = 1.0 - f
    logf = jnp.log(f) * LOG2_E
    b = _dot_sel(tri_ref[...], logf)
    bc = jnp.sum(logf, axis=0, keepdims=True)
    lev = [None] + [jnp.exp2(e) for e in _level_exponents(b, logf, b_scr, sgn_ref)]
    return dict(sq=sq, q=q, sg=sg, omlb=omlb, f=f, k=k, v=hi, eb=jnp.exp2(b), erem=jnp.exp2(bc - b),
                ebc=jnp.exp2(bc), lev=lev)


def _blockdiag(a, b):
    z = jnp.zeros_like(a)
    return jnp.concatenate([jnp.concatenate([a, z], axis=1), jnp.concatenate([z, b], axis=1)], axis=0)


def _level_operands(a):
    q, k = a["q"].astype(_BF), a["k"].astype(_BF)
    lev = [None] + [a["lev"][l].astype(_BF) for l in range(1, N_LEVELS + 1)]
    ql = [q] + [q * lev[l] for l in range(1, N_LEVELS + 1)]
    kl = [k] + [k * lev[l] for l in range(1, N_LEVELS + 1)]
    pairs = range(0, N_LEVELS + 1, 2)
    return ([jnp.concatenate([ql[l], ql[l + 1]], axis=1) for l in pairs], [_blockdiag(kl[l], kl[l + 1]) for l in pairs],
            ql, kl)


def _hgrn_scores(a, lvl_ref, q_pairs, k_diags):
    acc = None
    for n, (qp, kd) in enumerate(zip(q_pairs, k_diags)):
        both = lax.dot_general(qp, kd, _NT, preferred_element_type=F32)
        part = lvl_ref[2 * n] * both[:, :CHUNK] + lvl_ref[2 * n + 1] * both[:, CHUNK:]
        acc = part if acc is None else acc + part
    return acc


SCAN_UNROLL = 8
FWD_UNROLL = 16
RET_UNROLL = 16


def _writeback_reserve(step, make_copies):
    slot = step % 2

    @pl.when(step >= 2)
    def _():
        for cp in make_copies(slot):
            cp.wait()

    return slot


def _writeback_commit(step, n_steps, slot, make_copies):
    for cp in make_copies(slot):
        cp.start()

    @pl.when(step == n_steps - 1)
    def _():
        for cp in make_copies(slot):
            cp.wait()
        if n_steps > 1:
            for cp in make_copies(1 - slot):
                cp.wait()


def _resident(const):
    zeros = (0,) * const.ndim
    return pl.BlockSpec(const.shape, lambda p, t: zeros)


def _time_block(t_len):
    return min(t_len, 2048)


def hgrn_forward(pb, lb, t_len):
    nc = t_len // CHUNK
    tb = _time_block(t_len)
    ncb = tb // CHUNK
    consts = _hgrn_consts()
    operands = [consts[n] for n in ("tri", "lvl", "sgn")]

    def body(hq_ref, hf_ref, hi_ref, lb_ref, tri_ref, lvl_ref, sgn_ref, o_ref, ssave_ref, asave_ref, st_ref, b_scr):
        @pl.when(pl.program_id(1) == 0)
        def _():
            st_ref[...] = jnp.zeros_like(st_ref)

        def chunk(ci, carry):
            r = pl.ds(pl.multiple_of(ci * CHUNK, CHUNK), CHUNK)
            for hd in range(2):
                lbv = lb_ref[:, hd * LANE:(hd + 1) * LANE]
                a = _hgrn_chunk(hq_ref[hd, r, :], hf_ref[hd, r, :], hi_ref[hd, r, :], lbv, tri_ref, sgn_ref,
                                b_scr.at[hd])
                q_pairs, k_diags, _, _ = _level_operands(a)
                st = st_ref[hd]
                ssave_ref[hd, ci] = st
                scores = _hgrn_scores(a, lvl_ref, q_pairs, k_diags).astype(asave_ref.dtype)
                asave_ref[hd, ci] = scores
                o_ref[hd, r, :] = _dot(a["q"] * a["eb"], st, _NT) + _dot(scores, a["v"])
                st_ref[hd] = st * a["ebc"] + _dot(a["v"], a["k"] * a["erem"], _TN)
            return carry

        lax.fori_loop(0, ncb, chunk, 0, unroll=FWD_UNROLL)

    pair = lambda base: pl.BlockSpec((2, tb, LANE), lambda p, t, base=base: (base // 2 + p, t, 0))
    per_chunk = pl.BlockSpec((2, ncb, LANE, LANE), lambda p, t: (p, t, 0, 0))
    return pl.pallas_call(
        body, name="hgrn_fwd", grid=(N_HEADS // 2, t_len // tb),
        in_specs=[pair(CB_HQ), pair(CB_HF), pair(CB_HI),
                  pl.BlockSpec((1, 2 * LANE), lambda p, t: (0, p))] + [_resident(c) for c in operands],
        out_specs=[pl.BlockSpec((2, tb, LANE), lambda p, t: (p, t, 0)), per_chunk, per_chunk],
        out_shape=[jax.ShapeDtypeStruct((N_HEADS, t_len, LANE), F32),
                   jax.ShapeDtypeStruct((N_HEADS, nc, LANE, LANE), F32),
                   jax.ShapeDtypeStruct((N_HEADS, nc, CHUNK, CHUNK), _BF)],
        scratch_shapes=[pltpu.VMEM((2, LANE, LANE), F32), pltpu.VMEM((2, CHUNK, LANE), F32)],
        compiler_params=_params(("arbitrary", "arbitrary")),
    )(pb, pb, pb, lb, *operands)


def hgrn_backward(pb, lb, do, ssave, asave, dpb, t_len):
    tb = _time_block(t_len)
    ncb, ntb = tb // CHUNK, t_len // tb
    consts = _hgrn_consts()
    operands = [consts[n] for n in ("tri", "sgn", "sel_t", "lvl_b", "lvlt_b")]

    def body(hq_ref, hf_ref, hi_ref, lb_ref, do_ref, ssave_ref, asave_ref, tri_ref, sgn_ref, selt_ref, lvlb_ref,
             lvltb_ref, dpb_in, dpb_ref, dlb_ref, dq_buf, df_buf, di_buf, dst_ref, b_scr, sems):
        del dpb_in
        p, t = pl.program_id(0), pl.program_id(1)
        step = p * ntb + t
        rows = pl.ds(pl.multiple_of((ntb - 1 - t) * tb, tb), tb)

        def out_copies(sl):
            return [pltpu.make_async_copy(buf.at[sl], dpb_ref.at[pl.ds(base + 2 * p, 2), rows], sems.at[sl, n])
                    for n, (buf, base) in enumerate(((dq_buf, CB_HQ), (df_buf, CB_HF), (di_buf, CB_HI)))]

        slot = _writeback_reserve(step, out_copies)

        @pl.when(t == 0)
        def _():
            dst_ref[...] = jnp.zeros_like(dst_ref)
            dlb_ref[...] = jnp.zeros_like(dlb_ref)

        def chunk(i, carry):
            ci = ncb - 1 - i
            r = pl.ds(pl.multiple_of(ci * CHUNK, CHUNK), CHUNK)
            for hd in range(2):
                head_chunk(hd, ci, r)
            return carry

        def head_chunk(hd, ci, r):
            lbv = lb_ref[:, hd * LANE:(hd + 1) * LANE]
            hq = hq_ref[hd, r, :]
            a = _hgrn_chunk(hq, hf_ref[hd, r, :], hi_ref[hd, r, :], lbv, tri_ref, sgn_ref, b_scr.at[hd])
            _, k_diags, ql, kl = _level_operands(a)
            q, k, v = a["q"], a["k"], a["v"]
            g = do_ref[hd, r, :]
            st0 = ssave_ref[hd, ci]
            dst = dst_ref[hd]
            scores = asave_ref[hd, ci]
            da = _dot(g, v, _NT)
            da_t = _dot(v, g, _NT)
            kb = k * a["erem"]
            qb = q * a["eb"]
            dv = _dot(scores, g, _TN) + _dot(kb, dst, _NT)
            dq_inter = _dot(g, st0) * a["eb"]
            dk_state = _dot(v, dst) * a["erem"]
            dq, dk = dq_inter, dk_state
            de = [q * dq_inter]
            da_b, dat_b = da.astype(_BF), da_t.astype(_BF)
            for n in range(len(k_diags)):
                l0, l1 = 2 * n, 2 * n + 1
                da_pair = jnp.concatenate([lvlb_ref[l0] * da_b, lvlb_ref[l1] * da_b], axis=1)
                dat_pair = jnp.concatenate([lvltb_ref[l0] * dat_b, lvltb_ref[l1] * dat_b], axis=1)
                dq_both = lax.dot_general(da_pair, k_diags[n], _NN, preferred_element_type=F32)
                dk_both = lax.dot_general(dat_pair, _blockdiag(ql[l0], ql[l1]), _NN, preferred_element_type=F32)
                for l, cols in ((l0, slice(0, LANE)), (l1, slice(LANE, 2 * LANE))):
                    dql, dkl = dq_both[:, cols], dk_both[:, cols]
                    if l > 0:
                        e = a["lev"][l]
                        dql, dkl = dql * e, dkl * e
                        de.append(q * dql + k * dkl)
                    dq = dq + dql
                    dk = dk + dkl
            de.append(k * dk_state)
            dst_ref[hd] = dst * a["ebc"] + _dot(g, qb, _TN)
            dbc = jnp.sum(dst * st0, axis=0, keepdims=True) * a["ebc"]
            de2 = lax.dot_general(selt_ref[...], _split2(jnp.concatenate(de, axis=0)), _NN,
                                  preferred_element_type=F32)
            dlogf = de2[:, :LANE] + de2[:, LANE:] + dbc
            sq, sg = a["sq"], a["sg"]
            df = dlogf / a["f"] - dk
            dq_buf[slot, hd, r, :] = (dq * (sq * (1.0 + hq * (1.0 - sq)))).astype(dq_buf.dtype)
            df_buf[slot, hd, r, :] = (df * a["omlb"] * sg * (1.0 - sg)).astype(df_buf.dtype)
            di_buf[slot, hd, r, :] = dv.astype(di_buf.dtype)
            cols = slice(hd * LANE, (hd + 1) * LANE)
            dlb_ref[:, cols] = dlb_ref[:, cols] + jnp.sum(df * (1.0 - sg), axis=0, keepdims=True)

        lax.fori_loop(0, ncb, chunk, 0, unroll=SCAN_UNROLL)
        _writeback_commit(step, (N_HEADS // 2) * ntb, slot, out_copies)

    pair = lambda base: pl.BlockSpec((2, tb, LANE), lambda p, t, base=base: (base // 2 + p, ntb - 1 - t, 0))
    any_spec = pl.BlockSpec(memory_space=pl.ANY)
    per_chunk = pl.BlockSpec((2, ncb, LANE, LANE), lambda p, t: (p, ntb - 1 - t, 0, 0))
    return pl.pallas_call(
        body, name="hgrn_bwd", grid=(N_HEADS // 2, ntb),
        in_specs=[pair(CB_HQ), pair(CB_HF), pair(CB_HI),
                  pl.BlockSpec((1, 2 * LANE), lambda p, t: (0, p)),
                  pair(0), per_chunk, per_chunk]
        + [_resident(c) for c in operands] + [any_spec],
        out_specs=[any_spec, pl.BlockSpec((1, 2 * LANE), lambda p, t: (0, p))],
        out_shape=[jax.ShapeDtypeStruct(dpb.shape, dpb.dtype), jax.ShapeDtypeStruct((1, D_MODEL), F32)],
        scratch_shapes=[pltpu.VMEM((2, 2, tb, LANE), dpb.dtype)] * 3 + [
            pltpu.VMEM((2, LANE, LANE), F32), pltpu.VMEM((2, CHUNK, LANE), F32), pltpu.SemaphoreType.DMA((2, 3))],
        input_output_aliases={7 + len(operands): 0},
        compiler_params=_params(("arbitrary", "arbitrary")),
    )(pb, pb, pb, lb, do, ssave, asave, *operands, dpb)


def _rope_tables(t_len):
    half = RET_DK // 2
    inv_freq = (1.0 / (np.float32(ROPE_BASE) ** np.linspace(0.0, 1.0, half, dtype=np.float32))).astype(np.float32)
    ang = (np.arange(t_len, dtype=np.float32)[:, None] * inv_freq[None, :]).astype(np.float64)
    cos, sin = np.cos(ang).astype(np.float32), np.sin(ang).astype(np.float32)
    cos_t = np.concatenate([cos, cos, cos, cos], axis=1)
    sin_t = np.concatenate([-sin, sin, -sin, sin], axis=1)
    return jnp.asarray(cos_t), jnp.asarray(sin_t)


def _swap_halves(v):
    half = RET_DK // 2
    lane = lax.broadcasted_iota(jnp.int32, v.shape, 1)
    first = (lane & (RET_DK - 1)) < half
    return jnp.where(first, pltpu.roll(v, LANE - half, 1), pltpu.roll(v, half, 1))


def _ret_head_consts(hidx):
    c = CHUNK
    hf = jnp.full((1, LANE), hidx, jnp.int32).astype(F32)
    lg = jnp.log(1.0 - jnp.exp(-(5.0 + hf) * np.float32(np.log(2.0))))
    row = lax.broadcasted_iota(jnp.int32, (c, c), 0)
    col = lax.broadcasted_iota(jnp.int32, (c, c), 1)
    rel = (row - col).astype(F32)
    dm = jnp.where(rel >= 0, jnp.exp(lg[:, :1] * jnp.maximum(rel, 0.0)), 0.0)
    dm_t = jnp.where(rel <= 0, jnp.exp(lg[:, :1] * jnp.maximum(-rel, 0.0)), 0.0)
    idx = lax.broadcasted_iota(jnp.int32, (c, LANE), 0).astype(F32)
    zeta = jnp.exp(lg * (c - 1.0 - idx))
    xi = jnp.exp(lg * (idx + 1.0))
    cdec = jnp.exp(lg * float(c))
    return dm, zeta, xi, cdec, dm_t


def _lane_mask(which):
    lane = lax.broadcasted_iota(jnp.int32, (1, LANE), 1)
    return ((lane // RET_DK) == which).astype(F32)


def retention_forward(pb, cos_t, sin_t, t_len):
    nc = t_len // CHUNK

    tb = _time_block(t_len)
    ncb = tb // CHUNK

    def body(rq_ref, rk_ref, rv_ref, cos_ref, sin_ref, o_ref, rsave_ref, st_ref):
        p = pl.program_id(0)

        @pl.when(pl.program_id(1) == 0)
        def _():
            st_ref[...] = jnp.zeros_like(st_ref)

        consts = [_ret_head_consts(2 * p + hd) for hd in range(2)]

        def chunk(ci, carry):
            r = pl.ds(pl.multiple_of(ci * CHUNK, CHUNK), CHUNK)
            cs, sn = cos_ref[r, :], sin_ref[r, :]
            q = rq_ref[r, :]
            k = rk_ref[r, :]
            q = q * cs + _swap_halves(q) * sn
            k = (k * cs + _swap_halves(k) * sn) * RET_DK ** -0.5
            for hd in range(2):
                dm, zeta, xi, cdec, _ = consts[hd]
                lm = _lane_mask(hd)
                qh, kh = q * lm, k * lm
                v = rv_ref[hd, r, :]
                st = st_ref[hd]
                rsave_ref[hd, ci] = st
                scores = _dot(qh, kh, _NT) * dm
                o_ref[hd, r, :] = _dot(scores, v) + _dot(qh * xi, st, _NT)
                st_ref[hd] = st * cdec + _dot(v, kh * zeta, _TN)
            return carry

        lax.fori_loop(0, ncb, chunk, 0, unroll=RET_UNROLL)

    return pl.pallas_call(
        body, name="ret_fwd", grid=(N_HEADS // 2, t_len // tb),
        in_specs=[pl.BlockSpec((None, tb, LANE), lambda p, t: (CB_RQ + p, t, 0)),
                  pl.BlockSpec((None, tb, LANE), lambda p, t: (CB_RK + p, t, 0)),
                  pl.BlockSpec((2, tb, LANE), lambda p, t: (CB_RV // 2 + p, t, 0)),
                  pl.BlockSpec((tb, LANE), lambda p, t: (t, 0)),
                  pl.BlockSpec((tb, LANE), lambda p, t: (t, 0))],
        out_specs=[pl.BlockSpec((2, tb, LANE), lambda p, t: (p, t, 0)),
                   pl.BlockSpec((2, ncb, LANE, LANE), lambda p, t: (p, t, 0, 0))],
        out_shape=[jax.ShapeDtypeStruct((N_HEADS, t_len, LANE), F32),
                   jax.ShapeDtypeStruct((N_HEADS, nc, LANE, LANE), F32)],
        scratch_shapes=[pltpu.VMEM((2, LANE, LANE), F32)],
        compiler_params=_params(("arbitrary", "arbitrary")),
    )(pb, pb, pb, cos_t, sin_t)


def retention_backward(pb, cos_t, sin_t, do, rsave, dpb, t_len):
    tb = _time_block(t_len)
    ncb, ntb = tb // CHUNK, t_len // tb

    def body(rq_ref, rk_ref, rv_ref, cos_ref, sin_ref, do_ref, rsave_ref, dpb_in,
             dpb_ref, dq_buf, dk_buf, dv_buf, dst_ref, sems):
        del dpb_in
        p, t = pl.program_id(0), pl.program_id(1)
        step = p * ntb + t
        rows = pl.ds(pl.multiple_of((ntb - 1 - t) * tb, tb), tb)

        def out_copies(sl):
            return [pltpu.make_async_copy(dq_buf.at[sl], dpb_ref.at[CB_RQ + p, rows], sems.at[sl, 0]),
                    pltpu.make_async_copy(dk_buf.at[sl], dpb_ref.at[CB_RK + p, rows], sems.at[sl, 1]),
                    pltpu.make_async_copy(dv_buf.at[sl], dpb_ref.at[pl.ds(CB_RV + 2 * p, 2), rows], sems.at[sl, 2])]

        slot = _writeback_reserve(step, out_copies)

        @pl.when(t == 0)
        def _():
            dst_ref[...] = jnp.zeros_like(dst_ref)

        consts = [_ret_head_consts(2 * p + hd) for hd in range(2)]

        def chunk(i, carry):
            ci = ncb - 1 - i
            r = pl.ds(pl.multiple_of(ci * CHUNK, CHUNK), CHUNK)
            cs, sn = cos_ref[r, :], sin_ref[r, :]
            q = rq_ref[r, :]
            k = rk_ref[r, :]
            q = q * cs + _swap_halves(q) * sn
            k = (k * cs + _swap_halves(k) * sn) * RET_DK ** -0.5
            dq, dk = None, None
            for hd in range(2):
                dm, zeta, xi, cdec, dm_t = consts[hd]
                lm = _lane_mask(hd)
                qh, kh = q * lm, k * lm
                v = rv_ref[hd, r, :]
                g = do_ref[hd, r, :]
                st0 = rsave_ref[hd, ci]
                dst = dst_ref[hd]
                scores_t = _dot(kh, qh, _NT) * dm_t
                dsc = _dot(g, v, _NT) * dm
                dsc_t = _dot(v, g, _NT) * dm_t
                dqh = _dot(dsc, kh) + _dot(g, st0) * xi
                dkh = _dot(dsc_t, qh) + _dot(v, dst) * zeta
                dv_buf[slot, hd, r, :] = (_dot(scores_t, g) + _dot(kh * zeta, dst, _NT)).astype(dv_buf.dtype)
                dst_ref[hd] = dst * cdec + _dot(g, qh * xi, _TN)
                dq = dqh if dq is None else dq + dqh
                dk = dkh if dk is None else dk + dkh
            dk = dk * (RET_DK ** -0.5)
            dq_buf[slot, r, :] = (dq * cs - _swap_halves(dq) * sn).astype(dq_buf.dtype)
            dk_buf[slot, r, :] = (dk * cs - _swap_halves(dk) * sn).astype(dk_buf.dtype)
            return carry

        lax.fori_loop(0, ncb, chunk, 0, unroll=RET_UNROLL)
        _writeback_commit(step, (N_HEADS // 2) * ntb, slot, out_copies)

    any_spec = pl.BlockSpec(memory_space=pl.ANY)
    return pl.pallas_call(
        body, name="ret_bwd", grid=(N_HEADS // 2, ntb),
        in_specs=[pl.BlockSpec((None, tb, LANE), lambda p, t: (CB_RQ + p, ntb - 1 - t, 0)),
                  pl.BlockSpec((None, tb, LANE), lambda p, t: (CB_RK + p, ntb - 1 - t, 0)),
                  pl.BlockSpec((2, tb, LANE), lambda p, t: (CB_RV // 2 + p, ntb - 1 - t, 0)),
                  pl.BlockSpec((tb, LANE), lambda p, t: (ntb - 1 - t, 0)),
                  pl.BlockSpec((tb, LANE), lambda p, t: (ntb - 1 - t, 0)),
                  pl.BlockSpec((2, tb, LANE), lambda p, t: (p, ntb - 1 - t, 0)),
                  pl.BlockSpec((2, ncb, LANE, LANE), lambda p, t: (p, ntb - 1 - t, 0, 0)),
                  any_spec],
        out_specs=any_spec,
        out_shape=jax.ShapeDtypeStruct(dpb.shape, dpb.dtype),
        scratch_shapes=[pltpu.VMEM((2, tb, LANE), dpb.dtype), pltpu.VMEM((2, tb, LANE), dpb.dtype),
                        pltpu.VMEM((2, 2, tb, LANE), dpb.dtype), pltpu.VMEM((2, LANE, LANE), F32),
                        pltpu.SemaphoreType.DMA((2, 3))],
        input_output_aliases={7: 0},
        compiler_params=_params(("arbitrary", "arbitrary")),
    )(pb, pb, pb, cos_t, sin_t, do, rsave, dpb)


def _row_tile(t_len, want):
    return min(want, t_len)


PAIR_CB = 2 * CB_PER_SHARD


def proj_forward(h, wt, t_len):
    tm = _row_tile(t_len, 1024)

    def body(h_ref, w_ref, o_ref):
        acc = _dot(h_ref[...], w_ref[...], _NT)
        for jj in range(PAIR_CB):
            o_ref[jj] = acc[:, jj * LANE:(jj + 1) * LANE]

    return pl.pallas_call(
        body, name="proj_fwd", grid=(N_DEV // 2, t_len // tm),
        in_specs=[pl.BlockSpec((tm, D_MODEL), lambda j, i: (i, 0)),
                  pl.BlockSpec((PAIR_CB * LANE, D_MODEL), lambda j, i: (j, 0))],
        out_specs=pl.BlockSpec((PAIR_CB, tm, LANE), lambda j, i: (j, i, 0)),
        out_shape=jax.ShapeDtypeStruct((N_CB, t_len, LANE), F32),
        compiler_params=_params(("arbitrary", "arbitrary")),
    )(h, wt)


def proj_backward_input(dpb, wt, token, x, dy, norm_g, scale1p, t_len):
    tm = _row_tile(t_len, 512)

    def body(a_ref, wt_hbm, token_ref, x_ref, dy_ref, g_ref, sc_ref, gx_ref, vec_ref, w_ref, sem):
        del token_ref
        i = pl.program_id(0)

        @pl.when(i == 0)
        def _():
            cp = pltpu.make_async_copy(wt_hbm, w_ref, sem)
            cp.start()
            cp.wait()

        a = jnp.concatenate([a_ref[jj].astype(_BF) for jj in range(N_CB)], axis=1)
        dhv = _dot(a, w_ref[...])
        xv, g, sc = x_ref[...], g_ref[...], sc_ref[...]
        r = lax.rsqrt(jnp.mean(xv * xv, axis=-1, keepdims=True) + EPS)
        xn = xv * r
        dxn = dhv * (g * sc)
        gx_ref[...] = dy_ref[...] + r * dxn - xn * (r * r) * jnp.mean(xv * dxn, axis=-1, keepdims=True)
        t = dhv * xn
        _acc_rows(vec_ref, i, [jnp.sum(t * sc, axis=0, keepdims=True),
                               jnp.sum(t * g, axis=0, keepdims=True),
                               jnp.sum(dhv, axis=0, keepdims=True)])

    row = pl.BlockSpec((tm, D_MODEL), lambda i: (i, 0))
    return pl.pallas_call(
        body, name="proj_bwd_input", grid=(t_len // tm,),
        in_specs=[pl.BlockSpec((N_CB, tm, LANE), lambda i: (0, i, 0)),
                  pl.BlockSpec(memory_space=pl.ANY),
                  pl.BlockSpec(token.shape, lambda i: (0, 0)),
                  row, row, _vec_spec(), _vec_spec()],
        out_specs=[row, pl.BlockSpec((8, D_MODEL), lambda i: (0, 0))],
        out_shape=[jax.ShapeDtypeStruct((t_len, D_MODEL), F32), jax.ShapeDtypeStruct((8, D_MODEL), F32)],
        scratch_shapes=[pltpu.VMEM(wt.shape, wt.dtype), pltpu.SemaphoreType.DMA],
        compiler_params=_params(("arbitrary",)),
    )(dpb, wt, token, x, dy, norm_g, scale1p)


def proj_backward_weight(h_t, dpb, t_len):
    tk = _row_tile(t_len, 2048)

    def body(h_ref, b_ref, o_ref):
        k = pl.program_id(1)
        b = jnp.concatenate([b_ref[jj].astype(_BF) for jj in range(PAIR_CB)], axis=1)
        part = _dot(h_ref[...], b)

        @pl.when(k == 0)
        def _():
            for s in range(2):
                o_ref[s] = part[:, s * SHARD_IN:(s + 1) * SHARD_IN]

        @pl.when(k > 0)
        def _():
            for s in range(2):
                o_ref[s] = o_ref[s] + part[:, s * SHARD_IN:(s + 1) * SHARD_IN]

    return pl.pallas_call(
        body, name="proj_bwd_weight", grid=(N_DEV // 2, t_len // tk),
        in_specs=[pl.BlockSpec((D_MODEL, tk), lambda j, k: (0, k)),
                  pl.BlockSpec((PAIR_CB, tk, LANE), lambda j, k: (j, k, 0))],
        out_specs=pl.BlockSpec((2, D_MODEL, SHARD_IN), lambda j, k: (j, 0, 0)),
        out_shape=jax.ShapeDtypeStruct((N_DEV, D_MODEL, SHARD_IN), F32),
        compiler_params=_params(("arbitrary", "arbitrary")),
    )(h_t, dpb)


def sibling_blocks(g_in, c_idx):
    tr = D_MODEL

    def body(c_ref, g_ref, o_ref):
        del c_ref
        o_ref[...] = g_ref[...].astype(o_ref.dtype)

    return pl.pallas_call(
        body, name="sibling_blocks",
        grid_spec=pltpu.PrefetchScalarGridSpec(
            num_scalar_prefetch=1, grid=(N_DEV // 2, D_MODEL // tr),
            in_specs=[pl.BlockSpec((None, tr, SHARD_IN), lambda q, i, c: (2 * q + 1 - c[0], i, 0))],
            out_specs=pl.BlockSpec((None, tr, SHARD_IN), lambda q, i, c: (q, i, 0))),
        out_shape=jax.ShapeDtypeStruct((N_DEV // 2, D_MODEL, SHARD_IN), _BF),
        compiler_params=_params(("arbitrary", "arbitrary")),
    )(c_idx, g_in)


def _vec_spec():
    return pl.BlockSpec((1, D_MODEL), lambda i: (0, 0))


def _acc_rows(ref, i, rows):
    @pl.when(i == 0)
    def _():
        ref[...] = jnp.zeros_like(ref)

    for n, row in enumerate(rows):
        ref[n:n + 1, :] = ref[n:n + 1, :] + row


def adaln_forward(x, norm_g, scale1p, shift, t_len):
    tm = _row_tile(t_len, 1024)

    def body(x_ref, g_ref, sc_ref, sh_ref, h_ref, ht_ref):
        xv = x_ref[...]
        r = lax.rsqrt(jnp.mean(xv * xv, axis=-1, keepdims=True) + EPS)
        h = xv * r * g_ref[...] * sc_ref[...] + sh_ref[...]
        h_ref[...] = h.astype(h_ref.dtype)
        ht_ref[...] = h.T.astype(ht_ref.dtype)

    return pl.pallas_call(
        body, name="adaln_fwd", grid=(t_len // tm,),
        in_specs=[pl.BlockSpec((tm, D_MODEL), lambda i: (i, 0)), _vec_spec(), _vec_spec(), _vec_spec()],
        out_specs=[pl.BlockSpec((tm, D_MODEL), lambda i: (i, 0)), pl.BlockSpec((D_MODEL, tm), lambda i: (0, i))],
        out_shape=[jax.ShapeDtypeStruct((t_len, D_MODEL), _BF), jax.ShapeDtypeStruct((D_MODEL, t_len), _BF)],
        compiler_params=_params(("arbitrary",)),
    )(x, norm_g, scale1p, shift)


def _head_norm(o, g):
    r = lax.rsqrt(jnp.mean(o * o, axis=-1, keepdims=True) + EPS)
    return r, o * r * g


def _group_spec(tm, cb):
    return pl.BlockSpec((N_HEADS, tm, LANE), lambda i, cb=cb: (cb // N_HEADS, i, 0))


MID_FINAL_G, MID_GATE, MID_LOSS, MID_HG_G, MID_RET_G = range(5)


def middle(x, target, oa, ob, pb, wout, gate, final_g, hg_g, ret_g, t_len):
    tm = _row_tile(t_len, 256)
    n_steps = t_len // tm

    def body(x_ref, t_ref, oa_ref, ob_ref, hz_ref, rz_ref, ga_ref, gb_ref, w_ref, gate_ref, fg_ref, hg_ref, rg_ref,
             dy_ref, doa_ref, dob_ref, dw_ref, vec_ref, dpb_ref, m_scr, dm_scr, keep, bufs, sems):
        i = pl.program_id(0)
        rows = pl.ds(pl.multiple_of(i * tm, tm), tm)

        def group_copies(sl):
            return [pltpu.make_async_copy(bufs.at[sl, n], dpb_ref.at[pl.ds(cb, N_HEADS), rows], sems.at[sl, n])
                    for n, cb in enumerate((CB_HZ, CB_RZ, CB_GA, CB_GB))]
        sides = ((oa_ref, hz_ref, ga_ref, hg_ref, doa_ref), (ob_ref, rz_ref, gb_ref, rg_ref, dob_ref))
        for hh in range(N_HEADS):
            ls = slice(hh * LANE, (hh + 1) * LANE)
            acc = None
            for side, (o_ref, z_ref, gt_ref, g_ref, _) in enumerate(sides):
                o = o_ref[hh]
                rr = lax.rsqrt(jnp.mean(o * o, axis=-1, keepdims=True) + EPS)
                orr = o * rr
                zz = z_ref[hh]
                sz = _sigmoid(zz)
                sgt = _sigmoid(gt_ref[hh])
                keep[side, hh, 0] = orr
                keep[side, hh, 1] = sz
                keep[side, hh, 2] = sgt
                keep[side, hh, 3] = jnp.broadcast_to(rr, orr.shape)
                u = sgt * ((orr * g_ref[:, ls]) * (zz * sz))
                acc = u if acc is None else acc + u
            m_scr[:, ls] = acc.astype(m_scr.dtype)
        zv = _dot(m_scr[...], w_ref[...])
        gt, fg = gate_ref[...], fg_ref[...]
        y = x_ref[...] + gt * zv
        r = lax.rsqrt(jnp.mean(y * y, axis=-1, keepdims=True) + EPS)
        yn = y * r
        err = yn * fg - t_ref[...]
        loss = 0.5 * jnp.sum(jnp.mean(err * err, axis=-1, keepdims=True), axis=0, keepdims=True)
        dout = err * (1.0 / D_MODEL)
        gd = dout * fg
        dy = r * gd - yn * (r * r) * jnp.mean(y * gd, axis=-1, keepdims=True)
        dy_ref[...] = dy
        dz = (dy * gt).astype(_BF)
        dm_scr[...] = _dot(dz, w_ref[...], _NT)
        part = _dot(m_scr[...], dz, _TN)

        @pl.when(i == 0)
        def _():
            dw_ref[...] = part

        @pl.when(i > 0)
        def _():
            dw_ref[...] = dw_ref[...] + part

        slot = _writeback_reserve(i, group_copies)
        dg = [[], []]
        for hh in range(N_HEADS):
            ls = slice(hh * LANE, (hh + 1) * LANE)
            dmh = dm_scr[:, ls]
            for side, (o_ref, z_ref, gt_ref, g_ref, do_ref) in enumerate(sides):
                zz, g = z_ref[hh], g_ref[:, ls]
                orr, sz, sgt, rr = keep[side, hh, 0], keep[side, hh, 1], keep[side, hh, 2], keep[side, hh, 3]
                n = orr * g
                silu = zz * sz
                du = dmh * sgt
                bufs[slot, 2 + side, hh] = (dmh * (n * silu) * (sgt * (1.0 - sgt))).astype(bufs.dtype)
                bufs[slot, side, hh] = (du * n * (sz * (1.0 + zz * (1.0 - sz)))).astype(bufs.dtype)
                dn = du * silu
                dg[side].append(jnp.sum(dn * orr, axis=0, keepdims=True))
                gdn = dn * g
                do_ref[hh] = (rr * (gdn - orr * jnp.mean(orr * gdn, axis=-1, keepdims=True))).astype(do_ref.dtype)
        _acc_rows(vec_ref, i, [jnp.sum(dout * yn, axis=0, keepdims=True),
                               jnp.sum(dy * zv, axis=0, keepdims=True),
                               jnp.broadcast_to(loss, (1, D_MODEL)),
                               jnp.concatenate(dg[0], axis=1), jnp.concatenate(dg[1], axis=1)])
        _writeback_commit(i, n_steps, slot, group_copies)

    row = pl.BlockSpec((tm, D_MODEL), lambda i: (i, 0))
    head = pl.BlockSpec((N_HEADS, tm, LANE), lambda i: (0, i, 0))
    full = pl.BlockSpec((D_MODEL, D_MODEL), lambda i: (0, 0))
    return pl.pallas_call(
        body, name="middle", grid=(n_steps,),
        in_specs=[row, row, head, head, _group_spec(tm, CB_HZ), _group_spec(tm, CB_RZ), _group_spec(tm, CB_GA),
                  _group_spec(tm, CB_GB), full, _vec_spec(), _vec_spec(), _vec_spec(), _vec_spec()],
        out_specs=[row, head, head, full, pl.BlockSpec((8, D_MODEL), lambda i: (0, 0)),
                   pl.BlockSpec(memory_space=pl.ANY)],
        out_shape=[jax.ShapeDtypeStruct((t_len, D_MODEL), F32),
                   jax.ShapeDtypeStruct((N_HEADS, t_len, LANE), _BF),
                   jax.ShapeDtypeStruct((N_HEADS, t_len, LANE), _BF),
                   jax.ShapeDtypeStruct((D_MODEL, D_MODEL), F32),
                   jax.ShapeDtypeStruct((8, D_MODEL), F32),
                   jax.ShapeDtypeStruct((N_CB, t_len, LANE), _BF)],
        scratch_shapes=[pltpu.VMEM((tm, D_MODEL), _BF), pltpu.VMEM((tm, D_MODEL), F32),
                        pltpu.VMEM((2, N_HEADS, 4, tm, LANE), F32),
                        pltpu.VMEM((2, 4, N_HEADS, tm, LANE), _BF), pltpu.SemaphoreType.DMA((2, 4))],
        compiler_params=_params(("arbitrary",)),
    )(x, target, oa, ob, pb, pb, pb, pb, wout, gate, final_g, hg_g, ret_g)


def device_step(x, target, mod, lb, project, norm_g, hg_g, ret_g, final_g, c_idx=None, start_exchange=None):
    t_len = x.shape[0]
    shift, scale, gate = mod[:, :D_MODEL], mod[:, D_MODEL:2 * D_MODEL], mod[:, 2 * D_MODEL:]
    scale1p = 1.0 + scale
    cos_t, sin_t = _rope_tables(t_len)
    h, h_t = adaln_forward(x, norm_g, scale1p, shift, t_len)
    pb, wt, wout = project(h)
    oa, ssave, asave = hgrn_forward(pb, lb, t_len)
    ob, rsave = retention_forward(pb, cos_t, sin_t, t_len)
    dy, doa, dob, dwout, vec_mid, dpb = middle(x, target, oa, ob, pb, wout, gate, final_g, hg_g, ret_g, t_len)
    dpb, dlb = hgrn_backward(pb, lb, doa, ssave, asave, dpb, t_len)
    dpb = retention_backward(pb, cos_t, sin_t, dob, rsave, dpb, t_len)
    c_idx = jnp.zeros((1,), jnp.int32) if c_idx is None else c_idx
    dwin = proj_backward_weight(h_t, dpb, t_len)
    dwin_sib = sibling_blocks(dwin, c_idx)
    token, pending = (start_exchange(dwin, dwin_sib, dwout) if start_exchange
                      else (jnp.zeros((8, LANE), F32), None))
    grad_x, vec_ada = proj_backward_input(dpb, wt, token, x, dy, norm_g, scale1p, t_len)
    return grad_x, dwin, dwout, vec_mid, vec_ada, dlb, pending


PACK_ROWS = 16
ROW_NORM_G, ROW_LB, ROW_HG_G, ROW_RET_G, ROW_FINAL_G, ROW_SHIFT, ROW_SCALE, ROW_GATE, ROW_LOSS = range(9)


def _mesh_pos():
    return lax.axis_index("x"), lax.axis_index("y"), lax.axis_index("c")


def _lin(pos):
    return 4 * pos[0] + 2 * pos[1] + pos[2]


def _xor_peer(pos, k):
    return tuple(1 - p if (k >> s) & 1 else p for p, s in zip(pos, (2, 1, 0)))


def _other_chips(pos):
    x, y, _ = pos
    return [(1 - x, y), (x, 1 - y), (1 - x, 1 - y)]


def _remote(src, dst, send_sem, recv_sem, to):
    return pltpu.make_async_remote_copy(src_ref=src, dst_ref=dst, send_sem=send_sem, recv_sem=recv_sem,
                                        device_id=to, device_id_type=MESH)


def pre_exchange(c, w_ada, b_ada, logits, w_in, w_out):
    def body(c_ref, wada_ref, bada_ref, logit_ref, win_ref, wout_ref, mod_ref, scall_ref, lb_ref, wt_ref, wo_ref,
             cg_ref, modall_ref, parts_ref, send1, recv1, send2, recv2):
        pos = _mesh_pos()
        cv = c_ref[...]
        slot = lambda p: pl.ds(pl.multiple_of(8 * _lin(p), 8), 8)
        cg_ref[slot(pos), :] = jnp.broadcast_to(cv * _sigmoid(cv), (8, D_MODEL))
        lb_ref[...] = _sigmoid(logit_ref[0:1, :] - logit_ref[1:2, :])
        peers = [_xor_peer(pos, k) for k in range(1, N_DEV)]
        gather = [_remote(cg_ref.at[slot(pos)], cg_ref.at[slot(pos)], send1.at[n], recv1.at[n], p)
                  for n, p in enumerate(peers)]
        for cp in gather:
            cp.start()
        wt_ref[...] = win_ref[...].T.astype(wt_ref.dtype)
        wo_ref[...] = wout_ref[...].astype(wo_ref.dtype)
        for n, p in enumerate(peers):
            _remote(cg_ref.at[slot(p)], cg_ref.at[slot(p)], send1.at[n], recv1.at[n], p).wait_recv()
        modall_ref[...] = _dot(cg_ref[...], wada_ref[...])
        scatter = [_remote(modall_ref.at[slot(p)], parts_ref.at[slot(pos)], send2.at[n], recv2.at[n], p)
                   for n, p in enumerate(peers)]
        for cp in scatter:
            cp.start()
        parts_ref[slot(pos), :] = modall_ref[slot(pos), :]
        for n, p in enumerate(peers):
            _remote(modall_ref.at[slot(p)], parts_ref.at[slot(p)], send2.at[n], recv2.at[n], p).wait_recv()
        for cp in gather + scatter:
            cp.wait_send()
        for j in range(N_DEV):
            cols = slice(j * SHARD_ADA, (j + 1) * SHARD_ADA)
            mod_ref[:, cols] = parts_ref[8 * j:8 * j + 1, :] + bada_ref[:, cols]
            scall_ref[j:j + 1, :] = cg_ref[8 * j:8 * j + 1, :]

    vmem = pl.BlockSpec(memory_space=pltpu.VMEM)
    return pl.pallas_call(
        body, name="pre_exchange",
        in_specs=[vmem] * 6, out_specs=[vmem] * 5,
        out_shape=[jax.ShapeDtypeStruct((1, 3 * D_MODEL), F32), jax.ShapeDtypeStruct((N_DEV, D_MODEL), F32),
                   jax.ShapeDtypeStruct((1, D_MODEL), F32),
                   jax.ShapeDtypeStruct(w_in.shape[::-1], _BF), jax.ShapeDtypeStruct(w_out.shape, _BF)],
        scratch_shapes=[pltpu.VMEM((N_DEV * 8, D_MODEL), F32), pltpu.VMEM((N_DEV * 8, SHARD_ADA), F32),
                        pltpu.VMEM((N_DEV * 8, SHARD_ADA), F32)] + [pltpu.SemaphoreType.DMA((N_DEV - 1,))] * 4,
        compiler_params=pltpu.CompilerParams(vmem_limit_bytes=VMEM_LIMIT),
    )(c, w_ada, b_ada, logits, w_in, w_out)


def weight_gather(win_sh, wout_sh):
    def body(win_ref, wout_ref, wg_ref, woutg_ref, send, recv, local):
        pos = _mesh_pos()
        x, y, c = pos
        sibling = (x, y, 1 - c)

        def route(core):
            return [(x + (1 - core) * (1 - 2 * x), y + core * (1 - 2 * y)),
                    (x + core * (1 - 2 * x), y + (1 - core) * (1 - 2 * y)),
                    (1 - x, 1 - y)]

        chips, sib_chips = route(c), route(1 - c)
        mine, first, later = [], [], []
        for a, (src, out) in enumerate(((win_ref, wg_ref), (wout_ref, woutg_ref))):
            def copy(k, block, to, src_ref=None, a=a, out=out):
                dst = out.at[_lin(block)]
                return _remote(dst if src_ref is None else src_ref, dst, send.at[7 * a + k], recv.at[7 * a + k], to)
            mine.append(pltpu.make_async_copy(src, out.at[_lin(pos)], local.at[a]))
            first += [copy(0, pos, sibling, src), copy(1, pos, (*chips[0], c), src), copy(2, pos, (*chips[1], c), src)]
            later.append([[copy(3, (*chips[0], c), (*chips[1], c)), copy(4, (*chips[0], c), sibling)],
                          [copy(5, (*chips[1], c), sibling)],
                          [copy(6, (*chips[2], c), sibling)]])
        for cp in mine + first:
            cp.start()
        for j in range(3):
            for a, out in enumerate((wg_ref, woutg_ref)):
                dst = out.at[_lin((*chips[j], c))]
                _remote(dst, dst, send.at[7 * a + 1 + j], recv.at[7 * a + 1 + j], pos).wait_recv()
                for cp in later[a][j]:
                    cp.start()
        for a, out in enumerate((wg_ref, woutg_ref)):
            dst = out.at[_lin(sibling)]
            _remote(dst, dst, send.at[7 * a], recv.at[7 * a], pos).wait_recv()
            for j in range(3):
                dst = out.at[_lin((*sib_chips[j], 1 - c))]
                _remote(dst, dst, send.at[7 * a + 4 + j], recv.at[7 * a + 4 + j], pos).wait_recv()
        for cp in first + [cp for per_array in later for group in per_array for cp in group]:
            cp.wait_send()
        for cp in mine:
            cp.wait()

    any_spec = pl.BlockSpec(memory_space=pl.ANY)
    return pl.pallas_call(
        body, name="weight_gather",
        in_specs=[any_spec, any_spec], out_specs=[any_spec, any_spec],
        out_shape=[jax.ShapeDtypeStruct((N_DEV,) + win_sh.shape, win_sh.dtype),
                   jax.ShapeDtypeStruct((N_DEV,) + wout_sh.shape, wout_sh.dtype)],
        scratch_shapes=[pltpu.SemaphoreType.DMA((14,)), pltpu.SemaphoreType.DMA((14,)),
                        pltpu.SemaphoreType.DMA((2,))],
    )(win_sh, wout_sh)


def grad_pair_exchange(g_sib, g_out):
    def body(gsib_ref, gout_ref, ra_ref, rb_ref, send, recv):
        pos = _mesh_pos()
        x, y, c = pos
        sibling = (x, y, 1 - c)
        copies = []
        for q in range(4):
            copies.append(_remote(gsib_ref.at[q], ra_ref.at[q], send.at[q], recv.at[q], sibling))
            copies.append(_remote(gout_ref.at[2 * q + (1 - c)], rb_ref.at[q], send.at[4 + q], recv.at[4 + q], sibling))
        for cp in copies:
            cp.start()
        for cp in copies:
            cp.wait_recv()
        for cp in copies:
            cp.wait_send()

    any_spec = pl.BlockSpec(memory_space=pl.ANY)
    return pl.pallas_call(
        body, name="grad_pair_exchange",
        in_specs=[any_spec, any_spec], out_specs=[any_spec, any_spec],
        out_shape=[jax.ShapeDtypeStruct(g_sib.shape, g_sib.dtype), jax.ShapeDtypeStruct((4,) + g_out.shape[1:], F32)],
        scratch_shapes=[pltpu.SemaphoreType.DMA((8,)), pltpu.SemaphoreType.DMA((8,))],
    )(g_sib, g_out)


def pair_sum(g_in, ra, g_out, rb, c_idx):
    tr = D_MODEL

    def body(c_ref, gin_ref, ra_ref, gout_ref, rb_ref, sb_ref, sbo_ref):
        del c_ref
        sb_ref[...] = (gin_ref[...] + ra_ref[...].astype(F32)).astype(sb_ref.dtype)
        sbo_ref[...] = gout_ref[...] + rb_ref[...]

    n_i = D_MODEL // tr
    return pl.pallas_call(
        body, name="pair_sum",
        grid_spec=pltpu.PrefetchScalarGridSpec(
            num_scalar_prefetch=1, grid=(4, n_i),
            in_specs=[pl.BlockSpec((None, tr, SHARD_IN), lambda q, i, c: (2 * q + c[0], i, 0)),
                      pl.BlockSpec((None, tr, SHARD_IN), lambda q, i, c: (q, i, 0)),
                      pl.BlockSpec((None, SHARD_OUT // n_i, D_MODEL), lambda q, i, c: (2 * q + c[0], i, 0)),
                      pl.BlockSpec((None, SHARD_OUT // n_i, D_MODEL), lambda q, i, c: (q, i, 0))],
            out_specs=[pl.BlockSpec((None, tr, SHARD_IN), lambda q, i, c: (q, i, 0)),
                       pl.BlockSpec((None, SHARD_OUT // n_i, D_MODEL), lambda q, i, c: (q, i, 0))]),
        out_shape=[jax.ShapeDtypeStruct(ra.shape, _BF), jax.ShapeDtypeStruct(rb.shape, F32)],
        compiler_params=_params(("arbitrary", "arbitrary")),
    )(c_idx, g_in, ra, g_out, rb)


_HBM = pl.BlockSpec(memory_space=pltpu.HBM)
_SEM = pl.BlockSpec(memory_space=pltpu.SEMAPHORE)
_N_CHIP_COPIES = 6


def _chip_copies(sb_ref, sbo_ref, rc_ref, rco_ref, send, recv):
    pos = _mesh_pos()
    copies = []
    for a, (src, dst) in enumerate(((sb_ref, rc_ref), (sbo_ref, rco_ref))):
        for j, chip in enumerate(_other_chips(pos)):
            copies.append(_remote(src.at[2 * chip[0] + chip[1]], dst.at[j], send.at[3 * a + j], recv.at[3 * a + j],
                                  (*chip, pos[2])))
    return copies


def grad_chip_start(sb, sbo):
    def body(sb_ref, sbo_ref, rc_ref, rco_ref, send, recv, sb_thru, sbo_thru, rc_thru, rco_thru, token):
        del sb_thru, sbo_thru, rc_thru, rco_thru
        for cp in _chip_copies(sb_ref, sbo_ref, rc_ref, rco_ref, send, recv):
            cp.start()
        token[...] = jnp.zeros_like(token)

    hbm = lambda a: pltpu.with_memory_space_constraint(a, pltpu.HBM)
    rc = lax.empty((3,) + sb.shape[1:], sb.dtype)
    rco = lax.empty((3,) + sbo.shape[1:], sbo.dtype)
    return pl.pallas_call(
        body, name="grad_chip_start",
        in_specs=[_HBM] * 4,
        out_specs=[_SEM, _SEM, _HBM, _HBM, _HBM, _HBM, pl.BlockSpec(memory_space=pltpu.VMEM)],
        out_shape=[pltpu.SemaphoreType.DMA((_N_CHIP_COPIES,)), pltpu.SemaphoreType.DMA((_N_CHIP_COPIES,)),
                   pltpu.HBM(sb.shape, sb.dtype), pltpu.HBM(sbo.shape, sbo.dtype),
                   pltpu.HBM(rc.shape, rc.dtype), pltpu.HBM(rco.shape, rco.dtype),
                   jax.ShapeDtypeStruct((8, LANE), F32)],
        input_output_aliases={0: 2, 1: 3, 2: 4, 3: 5},
        compiler_params=pltpu.CompilerParams(has_side_effects=pltpu.SideEffectType.DATAFLOW_SIDE_EFFECTING),
    )(hbm(sb), hbm(sbo), hbm(rc), hbm(rco))


def grad_chip_wait(send, recv, sb, sbo, rc, rco, after):
    def body(sb_ref, sbo_ref, rc_ref, rco_ref, send, recv, after_ref, sb_o, sbo_o, rc_o, rco_o):
        del after_ref, sb_o, sbo_o, rc_o, rco_o
        for cp in _chip_copies(sb_ref, sbo_ref, rc_ref, rco_ref, send, recv):
            cp.wait_send()
            cp.wait_recv()

    return pl.pallas_call(
        body, name="grad_chip_wait",
        in_specs=[_HBM] * 4 + [_SEM, _SEM, pl.BlockSpec(memory_space=pl.ANY)],
        out_specs=[_HBM] * 4,
        out_shape=[pltpu.HBM(sb.shape, sb.dtype), pltpu.HBM(sbo.shape, sbo.dtype),
                   pltpu.HBM(rc.shape, rc.dtype), pltpu.HBM(rco.shape, rco.dtype)],
        input_output_aliases={0: 0, 1: 1, 2: 2, 3: 3},
        compiler_params=pltpu.CompilerParams(has_side_effects=pltpu.SideEffectType.DATAFLOW_SIDE_EFFECTING),
    )(sb, sbo, rc, rco, send, recv, after)


def pack_gather(pack):
    def body(pack_ref, packs_ref, psend, precv):
        pos = _mesh_pos()
        me = _lin(pos)
        packs_ref[me] = pack_ref[...]
        peers = [_xor_peer(pos, k) for k in range(1, N_DEV)]
        gather = [_remote(packs_ref.at[me], packs_ref.at[me], psend.at[n], precv.at[n], p) for n, p in enumerate(peers)]
        for cp in gather:
            cp.start()
        for n, p in enumerate(peers):
            _remote(packs_ref.at[_lin(p)], packs_ref.at[_lin(p)], psend.at[n], precv.at[n], p).wait_recv()
        for cp in gather:
            cp.wait_send()

    vmem = pl.BlockSpec(memory_space=pltpu.VMEM)
    return pl.pallas_call(
        body, name="pack_gather", in_specs=[vmem], out_specs=vmem,
        out_shape=jax.ShapeDtypeStruct((N_DEV,) + pack.shape, F32),
        scratch_shapes=[pltpu.SemaphoreType.DMA((N_DEV - 1,)), pltpu.SemaphoreType.DMA((N_DEV - 1,))],
    )(pack)


def pack_rows(vec_mid, vec_ada, dlb):
    def body(mid_ref, ada_ref, dlb_ref, o_ref):
        mid = lambda r: mid_ref[r:r + 1, :]
        rows = [ada_ref[0:1, :], dlb_ref[...], mid(MID_HG_G), mid(MID_RET_G), mid(MID_FINAL_G),
                ada_ref[2:3, :], ada_ref[1:2, :], mid(MID_GATE), mid(MID_LOSS)]
        o_ref[...] = jnp.zeros_like(o_ref)
        for n, row in enumerate(rows):
            o_ref[n:n + 1, :] = row

    vmem = pl.BlockSpec(memory_space=pltpu.VMEM)
    return pl.pallas_call(body, name="pack_rows", in_specs=[vmem] * 3, out_specs=vmem,
                          out_shape=jax.ShapeDtypeStruct((PACK_ROWS, D_MODEL), F32))(vec_mid, vec_ada, dlb)


def _adamw(w, g, m, v):
    m = ADAM_B1 * m + (1.0 - ADAM_B1) * g
    v = ADAM_B2 * v + (1.0 - ADAM_B2) * (g * g)
    m_hat = m / (1.0 - ADAM_B1 ** ADAM_STEP)
    v_hat = v / (1.0 - ADAM_B2 ** ADAM_STEP)
    delta = -ADAM_LR * (m_hat / (jnp.sqrt(v_hat) + ADAM_EPS) + ADAM_WD * w)
    return delta, m, v


def adam_shard(chip_idx, own, parts, w, m, v, name):
    rows, cols = w.shape
    tr = min(rows, 256)

    def body(chip_ref, p0, p1, p2, p3, w_ref, m_ref, v_ref, g_ref, d_ref, nm_ref, nv_ref):
        del chip_ref
        g = ((p0[...].astype(F32) + p1[...].astype(F32)) + p2[...].astype(F32)) + p3[...].astype(F32)
        g_ref[...] = g
        d_ref[...], nm_ref[...], nv_ref[...] = _adamw(w_ref[...], g, m_ref[...], v_ref[...])

    part = lambda q: pl.BlockSpec((None, tr, cols), lambda i, chip, q=q: (q, i, 0))
    tile = pl.BlockSpec((tr, cols), lambda i, chip: (i, 0))
    return pl.pallas_call(
        body, name=name,
        grid_spec=pltpu.PrefetchScalarGridSpec(
            num_scalar_prefetch=1, grid=(rows // tr,),
            in_specs=[pl.BlockSpec((None, tr, cols), lambda i, chip: (chip[0], i, 0)), part(0), part(1), part(2),
                      tile, tile, tile],
            out_specs=[tile] * 4),
        out_shape=[jax.ShapeDtypeStruct(w.shape, F32)] * 4,
        compiler_params=_params(("arbitrary",)),
    )(chip_idx, own, parts, parts, parts, w, m, v)


def adam_ada(sc_t, dmod_all, me_idx, w, m, v):
    def body(me_ref, sc_ref, dm_ref, w_ref, m_ref, v_ref, g_ref, d_ref, nm_ref, nv_ref):
        del me_ref
        g = _dot_f32(sc_ref[...], dm_ref[...])
        g_ref[...] = g
        d_ref[...], nm_ref[...], nv_ref[...] = _adamw(w_ref[...], g, m_ref[...], v_ref[...])

    full = pl.BlockSpec(w.shape, lambda i, me: (0, 0))
    return pl.pallas_call(
        body, name="adam_ada",
        grid_spec=pltpu.PrefetchScalarGridSpec(
            num_scalar_prefetch=1, grid=(1,),
            in_specs=[pl.BlockSpec(sc_t.shape, lambda i, me: (0, 0)),
                      pl.BlockSpec((LANE, SHARD_ADA), lambda i, me: (0, me[0])), full, full, full],
            out_specs=[full] * 4),
        out_shape=[jax.ShapeDtypeStruct(w.shape, F32)] * 4,
        compiler_params=_params(("arbitrary",)),
    )(me_idx, sc_t, dmod_all, w, m, v)


def adam_vectors(packs, lb, params, ms, vs):
    n = len(params)

    def body(*refs):
        packs_ref, lb_ref = refs[0], refs[1]
        w_refs, m_refs, v_refs = refs[2:2 + n], refs[2 + n:2 + 2 * n], refs[2 + 2 * n:2 + 3 * n]
        loss_ref = refs[2 + 3 * n]
        outs = refs[3 + 3 * n:3 + 7 * n]
        tot_ref = refs[3 + 7 * n]
        tot = packs_ref[0]
        for d in range(1, N_DEV):
            tot = tot + packs_ref[d]
        tot_ref[...] = tot
        row = lambda r: tot_ref[r:r + 1, :]
        lbv = lb_ref[...]
        dl0 = row(ROW_LB) * lbv * (1.0 - lbv)
        grads = [[row(ROW_NORM_G)],
                 [jnp.concatenate([row(ROW_SHIFT), row(ROW_SCALE), row(ROW_GATE)], axis=1)],
                 [dl0, -dl0],
                 [row(ROW_HG_G)], [row(ROW_RET_G)], [row(ROW_FINAL_G)]]
        loss_ref[...] = tot_ref[ROW_LOSS:ROW_LOSS + 1, 0:LANE]
        for j, g_rows in enumerate(grads):
            for r, g in enumerate(g_rows):
                rs = slice(r, r + 1)
                d, nm, nv = _adamw(w_refs[j][rs, :], g, m_refs[j][rs, :], v_refs[j][rs, :])
                outs[4 * j][rs, :] = g
                outs[4 * j + 1][rs, :] = d
                outs[4 * j + 2][rs, :] = nm
                outs[4 * j + 3][rs, :] = nv

    vmem = pl.BlockSpec(memory_space=pltpu.VMEM)
    out_shape = [jax.ShapeDtypeStruct((1, LANE), F32)]
    for w in params:
        out_shape += [jax.ShapeDtypeStruct(w.shape, F32)] * 4
    return pl.pallas_call(
        body, name="adam_vectors", in_specs=[vmem] * (2 + 3 * n), out_specs=[vmem] * len(out_shape),
        out_shape=out_shape, scratch_shapes=[pltpu.VMEM((PACK_ROWS, D_MODEL), F32)],
    )(packs, lb, *params, *ms, *vs)


def kernel(x, c, norm_g, w_ada, b_ada, w_in, hg_lb_logits, hg_norm_g, ret_norm_g, w_out, final_g, loss_target, m_norm_g, m_w_ada, m_b_ada, m_w_in, m_hg_lb_logits, m_hg_norm_g, m_ret_norm_g, m_w_out, m_final_g, v_norm_g, v_w_ada, v_b_ada, v_w_in, v_hg_lb_logits, v_hg_norm_g, v_ret_norm_g, v_w_out, v_final_g):
    pos = _mesh_pos()
    me_idx = jnp.reshape(_lin(pos), (1,)).astype(jnp.int32)
    c_idx = jnp.reshape(pos[2], (1,)).astype(jnp.int32)
    vec = lambda a: a.reshape(1, D_MODEL)

    mod, scall, lb, wt_sh, wout_sh = pre_exchange(c, w_ada[0], b_ada, hg_lb_logits, w_in[0], w_out[0])
    chip_idx = jnp.reshape(2 * pos[0] + pos[1], (1,)).astype(jnp.int32)

    def project(h):
        wtg, woutg = weight_gather(wt_sh, wout_sh)
        wt = wtg.reshape(D_IN, D_MODEL)
        return proj_forward(h, wt, h.shape[0]), wt, woutg.reshape(D_MODEL, D_MODEL)

    def start_exchange(dwin, dwin_sib, dwout):
        dwout = dwout.reshape(N_DEV, SHARD_OUT, D_MODEL)
        ra, rb = grad_pair_exchange(dwin_sib, dwout)
        sb, sbo = pair_sum(dwin, ra, dwout, rb, c_idx)
        send, recv, sb, sbo, rc, rco, token = grad_chip_start(sb, sbo)
        return token, (send, recv, sb, sbo, rc, rco)

    grad_x, _, _, vec_mid, vec_ada, dlb, pending = device_step(
        x[0], loss_target[0], mod, lb, project, norm_g, hg_norm_g, ret_norm_g, vec(final_g), c_idx, start_exchange)
    packs = pack_gather(pack_rows(vec_mid, vec_ada, dlb))
    dmod_all = packs[:, ROW_SHIFT:ROW_GATE + 1, :].reshape(N_DEV, 3 * D_MODEL)
    dmod_all = jnp.pad(dmod_all, ((0, LANE - N_DEV), (0, 0)))
    sc_t = jnp.pad(scall.T, ((0, 0), (0, LANE - N_DEV)))
    g_ada, d_ada, nm_ada, nv_ada = adam_ada(sc_t, dmod_all, me_idx, w_ada[0], m_w_ada[0], v_w_ada[0])
    small = adam_vectors(
        packs, lb,
        (norm_g, b_ada, hg_lb_logits, hg_norm_g, ret_norm_g, vec(final_g)),
        (m_norm_g, m_b_ada, m_hg_lb_logits, m_hg_norm_g, m_ret_norm_g, vec(m_final_g)),
        (v_norm_g, v_b_ada, v_hg_lb_logits, v_hg_norm_g, v_ret_norm_g, vec(v_final_g)))
    loss = small[0][0, 0]
    sb, sbo, rc, rco = grad_chip_wait(*pending, small[0])
    g_in, d_in, nm_in, nv_in = adam_shard(chip_idx, sb, rc, w_in[0], m_w_in[0], v_w_in[0], "adam_w_in")
    g_out, d_out, nm_out, nv_out = adam_shard(chip_idx, sbo, rco, w_out[0], m_w_out[0], v_w_out[0], "adam_w_out")
    (g_ng, d_ng, nm_ng, nv_ng), (g_b, d_b, nm_b, nv_b), (g_lb, d_lb, nm_lb, nv_lb), (g_hg, d_hg, nm_hg, nv_hg), \
        (g_rg, d_rg, nm_rg, nv_rg), (g_fg, d_fg, nm_fg, nv_fg) = [small[1 + 4 * j:5 + 4 * j] for j in range(6)]
    flat = lambda a: a.reshape(D_MODEL)

    def group(ng, ada, b, win, lbl, hg, rg, wo, fg):
        return (ng, ada[None], b, win[None], lbl, hg, rg, wo[None], flat(fg))

    return (loss, grad_x[None],
            *group(g_ng, g_ada, g_b, g_in, g_lb, g_hg, g_rg, g_out, g_fg),
            *group(d_ng, d_ada, d_b, d_in, d_lb, d_hg, d_rg, d_out, d_fg),
            *group(nm_ng, nm_ada, nm_b, nm_in, nm_lb, nm_hg, nm_rg, nm_out, nm_fg),
            *group(nv_ng, nv_ada, nv_b, nv_in, nv_lb, nv_hg, nv_rg, nv_out, nv_fg))
```

```python
import numpy as np
import jax
import jax.numpy as jnp
from jax import lax
from jax.experimental import pallas as pl
from jax.experimental.pallas import tpu as pltpu

F32 = jnp.float32
_BF = jnp.bfloat16

D_MODEL = 1024
N_HEADS = 8
LANE = 128
RET_DK = 64
D_IN = 9216
N_DEV = 8
SHARD_IN = D_IN // N_DEV
SHARD_ADA = 3 * D_MODEL // N_DEV
SHARD_OUT = D_MODEL // N_DEV
N_CB = D_IN // LANE
CB_PER_SHARD = SHARD_IN // LANE
CHUNK = 128
N_LEVELS = 7
EPS = 1e-6
LOG2_E = float(np.log2(np.e))
ROPE_BASE = 10000.0
CB_HQ, CB_HF, CB_HI, CB_HZ, CB_RQ, CB_RK, CB_RV, CB_RZ, CB_GA, CB_GB = 0, 8, 16, 24, 32, 36, 40, 48, 56, 64
VMEM_LIMIT = 56 * 1024 * 1024

ADAM_LR, ADAM_B1, ADAM_B2, ADAM_EPS, ADAM_WD, ADAM_STEP = 0.001, 0.9, 0.999, 1e-08, 0.01, 10

_NN = (((1,), (0,)), ((), ()))
_NT = (((1,), (1,)), ((), ()))
_TN = (((0,), (0,)), ((), ()))
MESH = pl.DeviceIdType.MESH


def _dot(a, b, dims=_NN):
    return lax.dot_general(a.astype(_BF), b.astype(_BF), dims, preferred_element_type=F32)


def _split2(a):
    hi = a.astype(_BF)
    lo = (a - hi.astype(F32)).astype(_BF)
    return jnp.concatenate([hi, lo], axis=1)


def _dot_sel(sel, a):
    n = a.shape[1]
    r = lax.dot_general(sel.astype(_BF), _split2(a), _NN, preferred_element_type=F32)
    return r[:, :n] + r[:, n:]


def _dot_f32(a, b):
    def pieces(v):
        p1 = v.astype(_BF)
        r1 = v - p1.astype(F32)
        p2 = r1.astype(_BF)
        p3 = (r1 - p2.astype(F32)).astype(_BF)
        return p1, p2, p3
    a1, a2, a3 = pieces(a)
    b1, b2, b3 = pieces(b)
    d = lambda u, v: lax.dot_general(u, v, _NN, preferred_element_type=F32)
    return ((d(a1, b3) + d(a2, b2) + d(a3, b1)) + (d(a1, b2) + d(a2, b1))) + d(a1, b1)


def _sigmoid(v):
    return 1.0 / (1.0 + jnp.exp(-v))


def _params(sem=None):
    return pltpu.CompilerParams(dimension_semantics=sem, vmem_limit_bytes=VMEM_LIMIT)


def _hgrn_consts():
    c, nl = CHUNK, N_LEVELS
    t = np.arange(c)[:, None]
    j = np.arange(c)[None, :]
    sel = [j <= t]
    masks = [j == t]
    for l in range(1, nl + 1):
        m = ((t >> l) << l) + (1 << (l - 1)) - 1
        sec = t > m
        sel.append(np.where(sec, (j > m) & (j <= t), (j > t) & (j <= m)))
        same = (t >> l) == (j >> l)
        masks.append(same & sec & (j <= m))
    sel.append(j > t)
    sel = np.concatenate(sel, 0).astype(np.float32)
    masks = np.stack(masks).astype(np.float32)
    sgn = np.stack([np.where((t & (1 << (l - 1))) != 0, 1.0, -1.0) * np.ones((1, LANE)) for l in range(3, nl + 1)])
    return dict(tri=jnp.asarray(sel[:c], _BF),
                lvl=jnp.asarray(masks, F32),
                sgn=jnp.asarray(sgn, F32),
                sel_t=jnp.asarray(sel.T, _BF),
                lvl_b=jnp.asarray(masks, _BF),
                lvlt_b=jnp.asarray(np.swapaxes(masks, 1, 2), _BF))


def _level_exponents(b, logf, b_scr, sgn_ref):
    c = CHUNK
    b_scr[...] = b
    row = lax.broadcasted_iota(jnp.int32, (c, LANE), 0)
    nxt = pltpu.roll(logf, c - 1, 0)
    prv = pltpu.roll(logf, 1, 0)
    r4 = row & 3
    out = [jnp.where((row & 1) == 1, logf, 0.0),
           jnp.where(r4 == 0, nxt, jnp.where(r4 == 1, 0.0, jnp.where(r4 == 2, logf, logf + prv)))]
    for l in range(3, N_LEVELS + 1):
        size, half = 1 << l, 1 << (l - 1)
        ref = jnp.concatenate([jnp.broadcast_to(b_scr[i * size + half - 1:i * size + half, :], (size, LANE))
                               for i in range(c // size)], axis=0)
        out.append((b - ref) * sgn_ref[l - 3])
    return out


def _hgrn_chunk(hq, hf, hi, lbv, tri_ref, sgn_ref, b_scr):
    sq = _sigmoid(hq)
    q = hq * sq
    sg = _sigmoid(hf)
    omlb = 1.0 - lbv
    f = lbv + omlb * sg
    k = 1.0 - f
    logf = jnp.log(f) * LOG2_E
    b = _dot_sel(tri_ref[...], logf)
    bc = jnp.sum(logf, axis=0, keepdims=True)
    lev = [None] + [jnp.exp2(e) for e in _level_exponents(b, logf, b_scr, sgn_ref)]
    return dict(sq=sq, q=q, sg=sg, omlb=omlb, f=f, k=k, v=hi, eb=jnp.exp2(b), erem=jnp.exp2(bc - b),
                ebc=jnp.exp2(bc), lev=lev)


def _blockdiag(a, b):
    z = jnp.zeros_like(a)
    return jnp.concatenate([jnp.concatenate([a, z], axis=1), jnp.concatenate([z, b], axis=1)], axis=0)


def _level_operands(a):
    q, k = a["q"].astype(_BF), a["k"].astype(_BF)
    lev = [None] + [a["lev"][l].astype(_BF) for l in range(1, N_LEVELS + 1)]
    ql = [q] + [q * lev[l] for l in range(1, N_LEVELS + 1)]
    kl = [k] + [k * lev[l] for l in range(1, N_LEVELS + 1)]
    pairs = range(0, N_LEVELS + 1, 2)
    return ([jnp.concatenate([ql[l], ql[l + 1]], axis=1) for l in pairs], [_blockdiag(kl[l], kl[l + 1]) for l in pairs],
            ql, kl)


def _hgrn_scores(a, lvl_ref, q_pairs, k_diags):
    acc = None
    for n, (qp, kd) in enumerate(zip(q_pairs, k_diags)):
        both = lax.dot_general(qp, kd, _NT, preferred_element_type=F32)
        part = lvl_ref[2 * n] * both[:, :CHUNK] + lvl_ref[2 * n + 1] * both[:, CHUNK:]
        acc = part if acc is None else acc + part
    return acc


SCAN_UNROLL = 8
FWD_UNROLL = 16
RET_UNROLL = 16


def _writeback_reserve(step, make_copies):
    slot = step % 2

    @pl.when(step >= 2)
    def _():
        for cp in make_copies(slot):
            cp.wait()

    return slot


def _writeback_commit(step, n_steps, slot, make_copies):
    for cp in make_copies(slot):
        cp.start()

    @pl.when(step == n_steps - 1)
    def _():
        for cp in make_copies(slot):
            cp.wait()
        if n_steps > 1:
            for cp in make_copies(1 - slot):
                cp.wait()


def _resident(const):
    zeros = (0,) * const.ndim
    return pl.BlockSpec(const.shape, lambda p, t: zeros)


def _time_block(t_len):
    return min(t_len, 2048)


def hgrn_forward(pb, lb, t_len):
    nc = t_len // CHUNK
    tb = _time_block(t_len)
    ncb = tb // CHUNK
    consts = _hgrn_consts()
    operands = [consts[n] for n in ("tri", "lvl", "sgn")]

    def body(hq_ref, hf_ref, hi_ref, lb_ref, tri_ref, lvl_ref, sgn_ref, o_ref, ssave_ref, asave_ref, st_ref, b_scr):
        @pl.when(pl.program_id(1) == 0)
        def _():
            st_ref[...] = jnp.zeros_like(st_ref)

        def chunk(ci, carry):
            r = pl.ds(pl.multiple_of(ci * CHUNK, CHUNK), CHUNK)
            for hd in range(2):
                lbv = lb_ref[:, hd * LANE:(hd + 1) * LANE]
                a = _hgrn_chunk(hq_ref[hd, r, :], hf_ref[hd, r, :], hi_ref[hd, r, :], lbv, tri_ref, sgn_ref,
                                b_scr.at[hd])
                q_pairs, k_diags, _, _ = _level_operands(a)
                st = st_ref[hd]
                ssave_ref[hd, ci] = st
                scores = _hgrn_scores(a, lvl_ref, q_pairs, k_diags).astype(asave_ref.dtype)
                asave_ref[hd, ci] = scores
                o_ref[hd, r, :] = _dot(a["q"] * a["eb"], st, _NT) + _dot(scores, a["v"])
                st_ref[hd] = st * a["ebc"] + _dot(a["v"], a["k"] * a["erem"], _TN)
            return carry

        lax.fori_loop(0, ncb, chunk, 0, unroll=FWD_UNROLL)

    pair = lambda base: pl.BlockSpec((2, tb, LANE), lambda p, t, base=base: (base // 2 + p, t, 0))
    per_chunk = pl.BlockSpec((2, ncb, LANE, LANE), lambda p, t: (p, t, 0, 0))
    return pl.pallas_call(
        body, name="hgrn_fwd", grid=(N_HEADS // 2, t_len // tb),
        in_specs=[pair(CB_HQ), pair(CB_HF), pair(CB_HI),
                  pl.BlockSpec((1, 2 * LANE), lambda p, t: (0, p))] + [_resident(c) for c in operands],
        out_specs=[pl.BlockSpec((2, tb, LANE), lambda p, t: (p, t, 0)), per_chunk, per_chunk],
        out_shape=[jax.ShapeDtypeStruct((N_HEADS, t_len, LANE), F32),
                   jax.ShapeDtypeStruct((N_HEADS, nc, LANE, LANE), F32),
                   jax.ShapeDtypeStruct((N_HEADS, nc, CHUNK, CHUNK), _BF)],
        scratch_shapes=[pltpu.VMEM((2, LANE, LANE), F32), pltpu.VMEM((2, CHUNK, LANE), F32)],
        compiler_params=_params(("arbitrary", "arbitrary")),
    )(pb, pb, pb, lb, *operands)


def hgrn_backward(pb, lb, do, ssave, asave, dpb, t_len):
    tb = _time_block(t_len)
    ncb, ntb = tb // CHUNK, t_len // tb
    consts = _hgrn_consts()
    operands = [consts[n] for n in ("tri", "sgn", "sel_t", "lvl_b", "lvlt_b")]

    def body(hq_ref, hf_ref, hi_ref, lb_ref, do_ref, ssave_ref, asave_ref, tri_ref, sgn_ref, selt_ref, lvlb_ref,
             lvltb_ref, dpb_in, dpb_ref, dlb_ref, dq_buf, df_buf, di_buf, dst_ref, b_scr, sems):
        del dpb_in
        p, t = pl.program_id(0), pl.program_id(1)
        step = p * ntb + t
        rows = pl.ds(pl.multiple_of((ntb - 1 - t) * tb, tb), tb)

        def out_copies(sl):
            return [pltpu.make_async_copy(buf.at[sl], dpb_ref.at[pl.ds(base + 2 * p, 2), rows], sems.at[sl, n])
                    for n, (buf, base) in enumerate(((dq_buf, CB_HQ), (df_buf, CB_HF), (di_buf, CB_HI)))]

        slot = _writeback_reserve(step, out_copies)

        @pl.when(t == 0)
        def _():
            dst_ref[...] = jnp.zeros_like(dst_ref)
            dlb_ref[...] = jnp.zeros_like(dlb_ref)

        def chunk(i, carry):
            ci = ncb - 1 - i
            r = pl.ds(pl.multiple_of(ci * CHUNK, CHUNK), CHUNK)
            for hd in range(2):
                head_chunk(hd, ci, r)
            return carry

        def head_chunk(hd, ci, r):
            lbv = lb_ref[:, hd * LANE:(hd + 1) * LANE]
            hq = hq_ref[hd, r, :]
            a = _hgrn_chunk(hq, hf_ref[hd, r, :], hi_ref[hd, r, :], lbv, tri_ref, sgn_ref, b_scr.at[hd])
            _, k_diags, ql, kl = _level_operands(a)
            q, k, v = a["q"], a["k"], a["v"]
            g = do_ref[hd, r, :]
            st0 = ssave_ref[hd, ci]
            dst = dst_ref[hd]
            scores = asave_ref[hd, ci]
            da = _dot(g, v, _NT)
            da_t = _dot(v, g, _NT)
            kb = k * a["erem"]
            qb = q * a["eb"]
            dv = _dot(scores, g, _TN) + _dot(kb, dst, _NT)
            dq_inter = _dot(g, st0) * a["eb"]
            dk_state = _dot(v, dst) * a["erem"]
            dq, dk = dq_inter, dk_state
            de = [q * dq_inter]
            da_b, dat_b = da.astype(_BF), da_t.astype(_BF)
            for n in range(len(k_diags)):
                l0, l1 = 2 * n, 2 * n + 1
                da_pair = jnp.concatenate([lvlb_ref[l0] * da_b, lvlb_ref[l1] * da_b], axis=1)
                dat_pair = jnp.concatenate([lvltb_ref[l0] * dat_b, lvltb_ref[l1] * dat_b], axis=1)
                dq_both = lax.dot_general(da_pair, k_diags[n], _NN, preferred_element_type=F32)
                dk_both = lax.dot_general(dat_pair, _blockdiag(ql[l0], ql[l1]), _NN, preferred_element_type=F32)
                for l, cols in ((l0, slice(0, LANE)), (l1, slice(LANE, 2 * LANE))):
                    dql, dkl = dq_both[:, cols], dk_both[:, cols]
                    if l > 0:
                        e = a["lev"][l]
                        dql, dkl = dql * e, dkl * e
                        de.append(q * dql + k * dkl)
                    dq = dq + dql
                    dk = dk + dkl
            de.append(k * dk_state)
            dst_ref[hd] = dst * a["ebc"] + _dot(g, qb, _TN)
            dbc = jnp.sum(dst * st0, axis=0, keepdims=True) * a["ebc"]
            de2 = lax.dot_general(selt_ref[...], _split2(jnp.concatenate(de, axis=0)), _NN,
                                  preferred_element_type=F32)
            dlogf = de2[:, :LANE] + de2[:, LANE:] + dbc
            sq, sg = a["sq"], a["sg"]
            df = dlogf / a["f"] - dk
            dq_buf[slot, hd, r, :] = (dq * (sq * (1.0 + hq * (1.0 - sq)))).astype(dq_buf.dtype)
            df_buf[slot, hd, r, :] = (df * a["omlb"] * sg * (1.0 - sg)).astype(df_buf.dtype)
            di_buf[slot, hd, r, :] = dv.astype(di_buf.dtype)
            cols = slice(hd * LANE, (hd + 1) * LANE)
            dlb_ref[:, cols] = dlb_ref[:, cols] + jnp.sum(df * (1.0 - sg), axis=0, keepdims=True)

        lax.fori_loop(0, ncb, chunk, 0, unroll=SCAN_UNROLL)
        _writeback_commit(step, (N_HEADS // 2) * ntb, slot, out_copies)

    pair = lambda base: pl.BlockSpec((2, tb, LANE), lambda p, t, base=base: (base // 2 + p, ntb - 1 - t, 0))
    any_spec = pl.BlockSpec(memory_space=pl.ANY)
    per_chunk = pl.BlockSpec((2, ncb, LANE, LANE), lambda p, t: (p, ntb - 1 - t, 0, 0))
    return pl.pallas_call(
        body, name="hgrn_bwd", grid=(N_HEADS // 2, ntb),
        in_specs=[pair(CB_HQ), pair(CB_HF), pair(CB_HI),
                  pl.BlockSpec((1, 2 * LANE), lambda p, t: (0, p)),
                  pair(0), per_chunk, per_chunk]
        + [_resident(c) for c in operands] + [any_spec],
        out_specs=[any_spec, pl.BlockSpec((1, 2 * LANE), lambda p, t: (0, p))],
        out_shape=[jax.ShapeDtypeStruct(dpb.shape, dpb.dtype), jax.ShapeDtypeStruct((1, D_MODEL), F32)],
        scratch_shapes=[pltpu.VMEM((2, 2, tb, LANE), dpb.dtype)] * 3 + [
            pltpu.VMEM((2, LANE, LANE), F32), pltpu.VMEM((2, CHUNK, LANE), F32), pltpu.SemaphoreType.DMA((2, 3))],
        input_output_aliases={7 + len(operands): 0},
        compiler_params=_params(("arbitrary", "arbitrary")),
    )(pb, pb, pb, lb, do, ssave, asave, *operands, dpb)


def _rope_tables(t_len):
    half = RET_DK // 2
    inv_freq = (1.0 / (np.float32(ROPE_BASE) ** np.linspace(0.0, 1.0, half, dtype=np.float32))).astype(np.float32)
    ang = (np.arange(t_len, dtype=np.float32)[:, None] * inv_freq[None, :]).astype(np.float64)
    cos, sin = np.cos(ang).astype(np.float32), np.sin(ang).astype(np.float32)
    cos_t = np.concatenate([cos, cos, cos, cos], axis=1)
    sin_t = np.concatenate([-sin, sin, -sin, sin], axis=1)
    return jnp.asarray(cos_t), jnp.asarray(sin_t)


def _swap_halves(v):
    half = RET_DK // 2
    lane = lax.broadcasted_iota(jnp.int32, v.shape, 1)
    first = (lane & (RET_DK - 1)) < half
    return jnp.where(first, pltpu.roll(v, LANE - half, 1), pltpu.roll(v, half, 1))


def _ret_head_consts(hidx):
    c = CHUNK
    hf = jnp.full((1, LANE), hidx, jnp.int32).astype(F32)
    lg = jnp.log(1.0 - jnp.exp(-(5.0 + hf) * np.float32(np.log(2.0))))
    row = lax.broadcasted_iota(jnp.int32, (c, c), 0)
    col = lax.broadcasted_iota(jnp.int32, (c, c), 1)
    rel = (row - col).astype(F32)
    dm = jnp.where(rel >= 0, jnp.exp(lg[:, :1] * jnp.maximum(rel, 0.0)), 0.0)
    dm_t = jnp.where(rel <= 0, jnp.exp(lg[:, :1] * jnp.maximum(-rel, 0.0)), 0.0)
    idx = lax.broadcasted_iota(jnp.int32, (c, LANE), 0).astype(F32)
    zeta = jnp.exp(lg * (c - 1.0 - idx))
    xi = jnp.exp(lg * (idx + 1.0))
    cdec = jnp.exp(lg * float(c))
    return dm, zeta, xi, cdec, dm_t


def _lane_mask(which):
    lane = lax.broadcasted_iota(jnp.int32, (1, LANE), 1)
    return ((lane // RET_DK) == which).astype(F32)


def retention_forward(pb, cos_t, sin_t, t_len):
    nc = t_len // CHUNK

    tb = _time_block(t_len)
    ncb = tb // CHUNK

    def body(rq_ref, rk_ref, rv_ref, cos_ref, sin_ref, o_ref, rsave_ref, st_ref):
        p = pl.program_id(0)

        @pl.when(pl.program_id(1) == 0)
        def _():
            st_ref[...] = jnp.zeros_like(st_ref)

        consts = [_ret_head_consts(2 * p + hd) for hd in range(2)]

        def chunk(ci, carry):
            r = pl.ds(pl.multiple_of(ci * CHUNK, CHUNK), CHUNK)
            cs, sn = cos_ref[r, :], sin_ref[r, :]
            q = rq_ref[r, :]
            k = rk_ref[r, :]
            q = q * cs + _swap_halves(q) * sn
            k = (k * cs + _swap_halves(k) * sn) * RET_DK ** -0.5
            for hd in range(2):
                dm, zeta, xi, cdec, _ = consts[hd]
                lm = _lane_mask(hd)
                qh, kh = q * lm, k * lm
                v = rv_ref[hd, r, :]
                st = st_ref[hd]
                rsave_ref[hd, ci] = st
                scores = _dot(qh, kh, _NT) * dm
                o_ref[hd, r, :] = _dot(scores, v) + _dot(qh * xi, st, _NT)
                st_ref[hd] = st * cdec + _dot(v, kh * zeta, _TN)
            return carry

        lax.fori_loop(0, ncb, chunk, 0, unroll=RET_UNROLL)

    return pl.pallas_call(
        body, name="ret_fwd", grid=(N_HEADS // 2, t_len // tb),
        in_specs=[pl.BlockSpec((None, tb, LANE), lambda p, t: (CB_RQ + p, t, 0)),
                  pl.BlockSpec((None, tb, LANE), lambda p, t: (CB_RK + p, t, 0)),
                  pl.BlockSpec((2, tb, LANE), lambda p, t: (CB_RV // 2 + p, t, 0)),
                  pl.BlockSpec((tb, LANE), lambda p, t: (t, 0)),
                  pl.BlockSpec((tb, LANE), lambda p, t: (t, 0))],
        out_specs=[pl.BlockSpec((2, tb, LANE), lambda p, t: (p, t, 0)),
                   pl.BlockSpec((2, ncb, LANE, LANE), lambda p, t: (p, t, 0, 0))],
        out_shape=[jax.ShapeDtypeStruct((N_HEADS, t_len, LANE), F32),
                   jax.ShapeDtypeStruct((N_HEADS, nc, LANE, LANE), F32)],
        scratch_shapes=[pltpu.VMEM((2, LANE, LANE), F32)],
        compiler_params=_params(("arbitrary", "arbitrary")),
    )(pb, pb, pb, cos_t, sin_t)


def retention_backward(pb, cos_t, sin_t, do, rsave, dpb, t_len):
    tb = _time_block(t_len)
    ncb, ntb = tb // CHUNK, t_len // tb

    def body(rq_ref, rk_ref, rv_ref, cos_ref, sin_ref, do_ref, rsave_ref, dpb_in,
             dpb_ref, dq_buf, dk_buf, dv_buf, dst_ref, sems):
        del dpb_in
        p, t = pl.program_id(0), pl.program_id(1)
        step = p * ntb + t
        rows = pl.ds(pl.multiple_of((ntb - 1 - t) * tb, tb), tb)

        def out_copies(sl):
            return [pltpu.make_async_copy(dq_buf.at[sl], dpb_ref.at[CB_RQ + p, rows], sems.at[sl, 0]),
                    pltpu.make_async_copy(dk_buf.at[sl], dpb_ref.at[CB_RK + p, rows], sems.at[sl, 1]),
                    pltpu.make_async_copy(dv_buf.at[sl], dpb_ref.at[pl.ds(CB_RV + 2 * p, 2), rows], sems.at[sl, 2])]

        slot = _writeback_reserve(step, out_copies)

        @pl.when(t == 0)
        def _():
            dst_ref[...] = jnp.zeros_like(dst_ref)

        consts = [_ret_head_consts(2 * p + hd) for hd in range(2)]

        def chunk(i, carry):
            ci = ncb - 1 - i
            r = pl.ds(pl.multiple_of(ci * CHUNK, CHUNK), CHUNK)
            cs, sn = cos_ref[r, :], sin_ref[r, :]
            q = rq_ref[r, :]
            k = rk_ref[r, :]
            q = q * cs + _swap_halves(q) * sn
            k = (k * cs + _swap_halves(k) * sn) * RET_DK ** -0.5
            dq, dk = None, None
            for hd in range(2):
                dm, zeta, xi, cdec, dm_t = consts[hd]
                lm = _lane_mask(hd)
                qh, kh = q * lm, k * lm
                v = rv_ref[hd, r, :]
                g = do_ref[hd, r, :]
                st0 = rsave_ref[hd, ci]
                dst = dst_ref[hd]
                scores_t = _dot(kh, qh, _NT) * dm_t
                dsc = _dot(g, v, _NT) * dm
                dsc_t = _dot(v, g, _NT) * dm_t
                dqh = _dot(dsc, kh) + _dot(g, st0) * xi
                dkh = _dot(dsc_t, qh) + _dot(v, dst) * zeta
                dv_buf[slot, hd, r, :] = (_dot(scores_t, g) + _dot(kh * zeta, dst, _NT)).astype(dv_buf.dtype)
                dst_ref[hd] = dst * cdec + _dot(g, qh * xi, _TN)
                dq = dqh if dq is None else dq + dqh
                dk = dkh if dk is None else dk + dkh
            dk = dk * (RET_DK ** -0.5)
            dq_buf[slot, r, :] = (dq * cs - _swap_halves(dq) * sn).astype(dq_buf.dtype)
            dk_buf[slot, r, :] = (dk * cs - _swap_halves(dk) * sn).astype(dk_buf.dtype)
            return carry

        lax.fori_loop(0, ncb, chunk, 0, unroll=RET_UNROLL)
        _writeback_commit(step, (N_HEADS // 2) * ntb, slot, out_copies)

    any_spec = pl.BlockSpec(memory_space=pl.ANY)
    return pl.pallas_call(
        body, name="ret_bwd", grid=(N_HEADS // 2, ntb),
        in_specs=[pl.BlockSpec((None, tb, LANE), lambda p, t: (CB_RQ + p, ntb - 1 - t, 0)),
                  pl.BlockSpec((None, tb, LANE), lambda p, t: (CB_RK + p, ntb - 1 - t, 0)),
                  pl.BlockSpec((2, tb, LANE), lambda p, t: (CB_RV // 2 + p, ntb - 1 - t, 0)),
                  pl.BlockSpec((tb, LANE), lambda p, t: (ntb - 1 - t, 0)),
                  pl.BlockSpec((tb, LANE), lambda p, t: (ntb - 1 - t, 0)),
                  pl.BlockSpec((2, tb, LANE), lambda p, t: (p, ntb - 1 - t, 0)),
                  pl.BlockSpec((2, ncb, LANE, LANE), lambda p, t: (p, ntb - 1 - t, 0, 0)),
                  any_spec],
        out_specs=any_spec,
        out_shape=jax.ShapeDtypeStruct(dpb.shape, dpb.dtype),
        scratch_shapes=[pltpu.VMEM((2, tb, LANE), dpb.dtype), pltpu.VMEM((2, tb, LANE), dpb.dtype),
                        pltpu.VMEM((2, 2, tb, LANE), dpb.dtype), pltpu.VMEM((2, LANE, LANE), F32),
                        pltpu.SemaphoreType.DMA((2, 3))],
        input_output_aliases={7: 0},
        compiler_params=_params(("arbitrary", "arbitrary")),
    )(pb, pb, pb, cos_t, sin_t, do, rsave, dpb)


def _row_tile(t_len, want):
    return min(want, t_len)


PAIR_CB = 2 * CB_PER_SHARD


def proj_forward(h, wt, chips, t_len, name, pb=None):
    tm = _row_tile(t_len, 1024)

    def body(chips_ref, h_ref, w_ref, *rest):
        del chips_ref
        o_ref = rest[-1]
        acc = _dot(h_ref[...], w_ref[...], _NT)
        for jj in range(PAIR_CB):
            o_ref[jj] = acc[:, jj * LANE:(jj + 1) * LANE]

    given = [] if pb is None else [pb]
    return pl.pallas_call(
        body, name=name,
        grid_spec=pltpu.PrefetchScalarGridSpec(
            num_scalar_prefetch=1, grid=(chips.shape[0], t_len // tm),
            in_specs=[pl.BlockSpec((tm, D_MODEL), lambda j, i, ch: (i, 0)),
                      pl.BlockSpec((PAIR_CB * LANE, D_MODEL), lambda j, i, ch: (ch[j], 0))]
            + [pl.BlockSpec(memory_space=pl.ANY)] * len(given),
            out_specs=pl.BlockSpec((PAIR_CB, tm, LANE), lambda j, i, ch: (ch[j], i, 0))),
        out_shape=jax.ShapeDtypeStruct((N_CB, t_len, LANE), F32),
        input_output_aliases={3: 0} if given else {},
        compiler_params=_params(("arbitrary", "arbitrary")),
    )(chips, h, wt, *given)


def proj_backward_input(dpb, wt, token, x, dy, norm_g, scale1p, t_len):
    tm = _row_tile(t_len, 512)

    def body(a_ref, wt_hbm, token_ref, x_ref, dy_ref, g_ref, sc_ref, gx_ref, vec_ref, w_ref, sem):
        del token_ref
        i = pl.program_id(0)

        @pl.when(i == 0)
        def _():
            cp = pltpu.make_async_copy(wt_hbm, w_ref, sem)
            cp.start()
            cp.wait()

        a = jnp.concatenate([a_ref[jj].astype(_BF) for jj in range(N_CB)], axis=1)
        dhv = _dot(a, w_ref[...])
        xv, g, sc = x_ref[...], g_ref[...], sc_ref[...]
        r = lax.rsqrt(jnp.mean(xv * xv, axis=-1, keepdims=True) + EPS)
        xn = xv * r
        dxn = dhv * (g * sc)
        gx_ref[...] = dy_ref[...] + r * dxn - xn * (r * r) * jnp.mean(xv * dxn, axis=-1, keepdims=True)
        t = dhv * xn
        _acc_rows(vec_ref, i, [jnp.sum(t * sc, axis=0, keepdims=True),
                               jnp.sum(t * g, axis=0, keepdims=True),
                               jnp.sum(dhv, axis=0, keepdims=True)])

    row = pl.BlockSpec((tm, D_MODEL), lambda i: (i, 0))
    return pl.pallas_call(
        body, name="proj_bwd_input", grid=(t_len // tm,),
        in_specs=[pl.BlockSpec((N_CB, tm, LANE), lambda i: (0, i, 0)),
                  pl.BlockSpec(memory_space=pl.ANY),
                  pl.BlockSpec(token.shape, lambda i: (0, 0)),
                  row, row, _vec_spec(), _vec_spec()],
        out_specs=[row, pl.BlockSpec((8, D_MODEL), lambda i: (0, 0))],
        out_shape=[jax.ShapeDtypeStruct((t_len, D_MODEL), F32), jax.ShapeDtypeStruct((8, D_MODEL), F32)],
        scratch_shapes=[pltpu.VMEM(wt.shape, wt.dtype), pltpu.SemaphoreType.DMA],
        compiler_params=_params(("arbitrary",)),
    )(dpb, wt, token, x, dy, norm_g, scale1p)


def proj_backward_weight(h_t, dpb, t_len):
    tk = _row_tile(t_len, 2048)

    def body(h_ref, b_ref, o_ref):
        k = pl.program_id(1)
        b = jnp.concatenate([b_ref[jj].astype(_BF) for jj in range(PAIR_CB)], axis=1)
        part = _dot(h_ref[...], b)

        @pl.when(k == 0)
        def _():
            for s in range(2):
                o_ref[s] = part[:, s * SHARD_IN:(s + 1) * SHARD_IN]

        @pl.when(k > 0)
        def _():
            for s in range(2):
                o_ref[s] = o_ref[s] + part[:, s * SHARD_IN:(s + 1) * SHARD_IN]

    return pl.pallas_call(
        body, name="proj_bwd_weight", grid=(N_DEV // 2, t_len // tk),
        in_specs=[pl.BlockSpec((D_MODEL, tk), lambda j, k: (0, k)),
                  pl.BlockSpec((PAIR_CB, tk, LANE), lambda j, k: (j, k, 0))],
        out_specs=pl.BlockSpec((2, D_MODEL, SHARD_IN), lambda j, k: (j, 0, 0)),
        out_shape=jax.ShapeDtypeStruct((N_DEV, D_MODEL, SHARD_IN), F32),
        compiler_params=_params(("arbitrary", "arbitrary")),
    )(h_t, dpb)


def sibling_blocks(g_in, c_idx):
    tr = D_MODEL

    def body(c_ref, g_ref, o_ref):
        del c_ref
        o_ref[...] = g_ref[...].astype(o_ref.dtype)

    return pl.pallas_call(
        body, name="sibling_blocks",
        grid_spec=pltpu.PrefetchScalarGridSpec(
            num_scalar_prefetch=1, grid=(N_DEV // 2, D_MODEL // tr),
            in_specs=[pl.BlockSpec((None, tr, SHARD_IN), lambda q, i, c: (2 * q + 1 - c[0], i, 0))],
            out_specs=pl.BlockSpec((None, tr, SHARD_IN), lambda q, i, c: (q, i, 0))),
        out_shape=jax.ShapeDtypeStruct((N_DEV // 2, D_MODEL, SHARD_IN), _BF),
        compiler_params=_params(("arbitrary", "arbitrary")),
    )(c_idx, g_in)


def _vec_spec():
    return pl.BlockSpec((1, D_MODEL), lambda i: (0, 0))


def _acc_rows(ref, i, rows):
    @pl.when(i == 0)
    def _():
        ref[...] = jnp.zeros_like(ref)

    for n, row in enumerate(rows):
        ref[n:n + 1, :] = ref[n:n + 1, :] + row


def adaln_forward(x, norm_g, scale1p, shift, t_len):
    tm = _row_tile(t_len, 1024)

    def body(x_ref, g_ref, sc_ref, sh_ref, h_ref, ht_ref):
        xv = x_ref[...]
        r = lax.rsqrt(jnp.mean(xv * xv, axis=-1, keepdims=True) + EPS)
        h = xv * r * g_ref[...] * sc_ref[...] + sh_ref[...]
        h_ref[...] = h.astype(h_ref.dtype)
        ht_ref[...] = h.T.astype(ht_ref.dtype)

    return pl.pallas_call(
        body, name="adaln_fwd", grid=(t_len // tm,),
        in_specs=[pl.BlockSpec((tm, D_MODEL), lambda i: (i, 0)), _vec_spec(), _vec_spec(), _vec_spec()],
        out_specs=[pl.BlockSpec((tm, D_MODEL), lambda i: (i, 0)), pl.BlockSpec((D_MODEL, tm), lambda i: (0, i))],
        out_shape=[jax.ShapeDtypeStruct((t_len, D_MODEL), _BF), jax.ShapeDtypeStruct((D_MODEL, t_len), _BF)],
        compiler_params=_params(("arbitrary",)),
    )(x, norm_g, scale1p, shift)


def _head_norm(o, g):
    r = lax.rsqrt(jnp.mean(o * o, axis=-1, keepdims=True) + EPS)
    return r, o * r * g


def _group_spec(tm, cb):
    return pl.BlockSpec((N_HEADS, tm, LANE), lambda i, cb=cb: (cb // N_HEADS, i, 0))


MID_FINAL_G, MID_GATE, MID_LOSS, MID_HG_G, MID_RET_G = range(5)


def middle(x, target, oa, ob, pb, wout, gate, final_g, hg_g, ret_g, t_len):
    tm = _row_tile(t_len, 256)
    n_steps = t_len // tm

    def body(x_ref, t_ref, oa_ref, ob_ref, hz_ref, rz_ref, ga_ref, gb_ref, w_ref, gate_ref, fg_ref, hg_ref, rg_ref,
             dy_ref, doa_ref, dob_ref, dw_ref, vec_ref, dpb_ref, m_scr, dm_scr, keep, bufs, sems):
        i = pl.program_id(0)
        rows = pl.ds(pl.multiple_of(i * tm, tm), tm)

        def group_copies(sl):
            return [pltpu.make_async_copy(bufs.at[sl, n], dpb_ref.at[pl.ds(cb, N_HEADS), rows], sems.at[sl, n])
                    for n, cb in enumerate((CB_HZ, CB_RZ, CB_GA, CB_GB))]
        sides = ((oa_ref, hz_ref, ga_ref, hg_ref, doa_ref), (ob_ref, rz_ref, gb_ref, rg_ref, dob_ref))
        for hh in range(N_HEADS):
            ls = slice(hh * LANE, (hh + 1) * LANE)
            acc = None
            for side, (o_ref, z_ref, gt_ref, g_ref, _) in enumerate(sides):
                o = o_ref[hh]
                rr = lax.rsqrt(jnp.mean(o * o, axis=-1, keepdims=True) + EPS)
                orr = o * rr
                zz = z_ref[hh]
                sz = _sigmoid(zz)
                sgt = _sigmoid(gt_ref[hh])
                keep[side, hh, 0] = orr
                keep[side, hh, 1] = sz
                keep[side, hh, 2] = sgt
                keep[side, hh, 3] = jnp.broadcast_to(rr, orr.shape)
                u = sgt * ((orr * g_ref[:, ls]) * (zz * sz))
                acc = u if acc is None else acc + u
            m_scr[:, ls] = acc.astype(m_scr.dtype)
        zv = _dot(m_scr[...], w_ref[...])
        gt, fg = gate_ref[...], fg_ref[...]
        y = x_ref[...] + gt * zv
        r = lax.rsqrt(jnp.mean(y * y, axis=-1, keepdims=True) + EPS)
        yn = y * r
        err = yn * fg - t_ref[...]
        loss = 0.5 * jnp.sum(jnp.mean(err * err, axis=-1, keepdims=True), axis=0, keepdims=True)
        dout = err * (1.0 / D_MODEL)
        gd = dout * fg
        dy = r * gd - yn * (r * r) * jnp.mean(y * gd, axis=-1, keepdims=True)
        dy_ref[...] = dy
        dz = (dy * gt).astype(_BF)
        dm_scr[...] = _dot(dz, w_ref[...], _NT)
        part = _dot(m_scr[...], dz, _TN)

        @pl.when(i == 0)
        def _():
            dw_ref[...] = part

        @pl.when(i > 0)
        def _():
            dw_ref[...] = dw_ref[...] + part

        slot = _writeback_reserve(i, group_copies)
        dg = [[], []]
        for hh in range(N_HEADS):
            ls = slice(hh * LANE, (hh + 1) * LANE)
            dmh = dm_scr[:, ls]
            for side, (o_ref, z_ref, gt_ref, g_ref, do_ref) in enumerate(sides):
                zz, g = z_ref[hh], g_ref[:, ls]
                orr, sz, sgt, rr = keep[side, hh, 0], keep[side, hh, 1], keep[side, hh, 2], keep[side, hh, 3]
                n = orr * g
                silu = zz * sz
                du = dmh * sgt
                bufs[slot, 2 + side, hh] = (dmh * (n * silu) * (sgt * (1.0 - sgt))).astype(bufs.dtype)
                bufs[slot, side, hh] = (du * n * (sz * (1.0 + zz * (1.0 - sz)))).astype(bufs.dtype)
                dn = du * silu
                dg[side].append(jnp.sum(dn * orr, axis=0, keepdims=True))
                gdn = dn * g
                do_ref[hh] = (rr * (gdn - orr * jnp.mean(orr * gdn, axis=-1, keepdims=True))).astype(do_ref.dtype)
        _acc_rows(vec_ref, i, [jnp.sum(dout * yn, axis=0, keepdims=True),
                               jnp.sum(dy * zv, axis=0, keepdims=True),
                               jnp.broadcast_to(loss, (1, D_MODEL)),
                               jnp.concatenate(dg[0], axis=1), jnp.concatenate(dg[1], axis=1)])
        _writeback_commit(i, n_steps, slot, group_copies)

    row = pl.BlockSpec((tm, D_MODEL), lambda i: (i, 0))
    head = pl.BlockSpec((N_HEADS, tm, LANE), lambda i: (0, i, 0))
    full = pl.BlockSpec((D_MODEL, D_MODEL), lambda i: (0, 0))
    return pl.pallas_call(
        body, name="middle", grid=(n_steps,),
        in_specs=[row, row, head, head, _group_spec(tm, CB_HZ), _group_spec(tm, CB_RZ), _group_spec(tm, CB_GA),
                  _group_spec(tm, CB_GB), full, _vec_spec(), _vec_spec(), _vec_spec(), _vec_spec()],
        out_specs=[row, head, head, full, pl.BlockSpec((8, D_MODEL), lambda i: (0, 0)),
                   pl.BlockSpec(memory_space=pl.ANY)],
        out_shape=[jax.ShapeDtypeStruct((t_len, D_MODEL), F32),
                   jax.ShapeDtypeStruct((N_HEADS, t_len, LANE), _BF),
                   jax.ShapeDtypeStruct((N_HEADS, t_len, LANE), _BF),
                   jax.ShapeDtypeStruct((D_MODEL, D_MODEL), F32),
                   jax.ShapeDtypeStruct((8, D_MODEL), F32),
                   jax.ShapeDtypeStruct((N_CB, t_len, LANE), _BF)],
        scratch_shapes=[pltpu.VMEM((tm, D_MODEL), _BF), pltpu.VMEM((tm, D_MODEL), F32),
                        pltpu.VMEM((2, N_HEADS, 4, tm, LANE), F32),
                        pltpu.VMEM((2, 4, N_HEADS, tm, LANE), _BF), pltpu.SemaphoreType.DMA((2, 4))],
        compiler_params=_params(("arbitrary",)),
    )(x, target, oa, ob, pb, pb, pb, pb, wout, gate, final_g, hg_g, ret_g)


def device_step(x, target, mod, lb, project, norm_g, hg_g, ret_g, final_g, c_idx=None, start_exchange=None):
    t_len = x.shape[0]
    shift, scale, gate = mod[:, :D_MODEL], mod[:, D_MODEL:2 * D_MODEL], mod[:, 2 * D_MODEL:]
    scale1p = 1.0 + scale
    cos_t, sin_t = _rope_tables(t_len)
    h, h_t = adaln_forward(x, norm_g, scale1p, shift, t_len)
    pb, wt, wout = project(h)
    oa, ssave, asave = hgrn_forward(pb, lb, t_len)
    ob, rsave = retention_forward(pb, cos_t, sin_t, t_len)
    dy, doa, dob, dwout, vec_mid, dpb = middle(x, target, oa, ob, pb, wout, gate, final_g, hg_g, ret_g, t_len)
    dpb, dlb = hgrn_backward(pb, lb, doa, ssave, asave, dpb, t_len)
    dpb = retention_backward(pb, cos_t, sin_t, dob, rsave, dpb, t_len)
    c_idx = jnp.zeros((1,), jnp.int32) if c_idx is None else c_idx
    dwin = proj_backward_weight(h_t, dpb, t_len)
    dwin_sib = sibling_blocks(dwin, c_idx)
    token, pending = (start_exchange(dwin, dwin_sib, dwout) if start_exchange
                      else (jnp.zeros((8, LANE), F32), None))
    grad_x, vec_ada = proj_backward_input(dpb, wt, token, x, dy, norm_g, scale1p, t_len)
    return grad_x, dwin, dwout, vec_mid, vec_ada, dlb, pending


PACK_ROWS = 16
ROW_NORM_G, ROW_LB, ROW_HG_G, ROW_RET_G, ROW_FINAL_G, ROW_SHIFT, ROW_SCALE, ROW_GATE, ROW_LOSS = range(9)


def _mesh_pos():
    return lax.axis_index("x"), lax.axis_index("y"), lax.axis_index("c")


def _lin(pos):
    return 4 * pos[0] + 2 * pos[1] + pos[2]


def _xor_peer(pos, k):
    return tuple(1 - p if (k >> s) & 1 else p for p, s in zip(pos, (2, 1, 0)))


def _other_chips(pos):
    x, y, _ = pos
    return [(1 - x, y), (x, 1 - y), (1 - x, 1 - y)]


def _remote(src, dst, send_sem, recv_sem, to):
    return pltpu.make_async_remote_copy(src_ref=src, dst_ref=dst, send_sem=send_sem, recv_sem=recv_sem,
                                        device_id=to, device_id_type=MESH)


def pre_exchange(c, w_ada, b_ada, logits, w_in, w_out):
    def body(c_ref, wada_ref, bada_ref, logit_ref, win_ref, wout_ref, mod_ref, scall_ref, lb_ref, wt_ref, wo_ref,
             cg_ref, modall_ref, parts_ref, send1, recv1, send2, recv2):
        pos = _mesh_pos()
        cv = c_ref[...]
        slot = lambda p: pl.ds(pl.multiple_of(8 * _lin(p), 8), 8)
        cg_ref[slot(pos), :] = jnp.broadcast_to(cv * _sigmoid(cv), (8, D_MODEL))
        lb_ref[...] = _sigmoid(logit_ref[0:1, :] - logit_ref[1:2, :])
        peers = [_xor_peer(pos, k) for k in range(1, N_DEV)]
        gather = [_remote(cg_ref.at[slot(pos)], cg_ref.at[slot(pos)], send1.at[n], recv1.at[n], p)
                  for n, p in enumerate(peers)]
        for cp in gather:
            cp.start()
        wt_ref[...] = win_ref[...].T.astype(wt_ref.dtype)
        wo_ref[...] = wout_ref[...].astype(wo_ref.dtype)
        for n, p in enumerate(peers):
            _remote(cg_ref.at[slot(p)], cg_ref.at[slot(p)], send1.at[n], recv1.at[n], p).wait_recv()
        modall_ref[...] = _dot(cg_ref[...], wada_ref[...])
        scatter = [_remote(modall_ref.at[slot(p)], parts_ref.at[slot(pos)], send2.at[n], recv2.at[n], p)
                   for n, p in enumerate(peers)]
        for cp in scatter:
            cp.start()
        parts_ref[slot(pos), :] = modall_ref[slot(pos), :]
        for n, p in enumerate(peers):
            _remote(modall_ref.at[slot(p)], parts_ref.at[slot(p)], send2.at[n], recv2.at[n], p).wait_recv()
        for cp in gather + scatter:
            cp.wait_send()
        for j in range(N_DEV):
            cols = slice(j * SHARD_ADA, (j + 1) * SHARD_ADA)
            mod_ref[:, cols] = parts_ref[8 * j:8 * j + 1, :] + bada_ref[:, cols]
            scall_ref[j:j + 1, :] = cg_ref[8 * j:8 * j + 1, :]

    vmem = pl.BlockSpec(memory_space=pltpu.VMEM)
    return pl.pallas_call(
        body, name="pre_exchange",
        in_specs=[vmem] * 6, out_specs=[vmem] * 5,
        out_shape=[jax.ShapeDtypeStruct((1, 3 * D_MODEL), F32), jax.ShapeDtypeStruct((N_DEV, D_MODEL), F32),
                   jax.ShapeDtypeStruct((1, D_MODEL), F32),
                   jax.ShapeDtypeStruct(w_in.shape[::-1], _BF), jax.ShapeDtypeStruct(w_out.shape, _BF)],
        scratch_shapes=[pltpu.VMEM((N_DEV * 8, D_MODEL), F32), pltpu.VMEM((N_DEV * 8, SHARD_ADA), F32),
                        pltpu.VMEM((N_DEV * 8, SHARD_ADA), F32)] + [pltpu.SemaphoreType.DMA((N_DEV - 1,))] * 4,
        compiler_params=pltpu.CompilerParams(vmem_limit_bytes=VMEM_LIMIT),
    )(c, w_ada, b_ada, logits, w_in, w_out)


def _gather_copies(srcs, outs, sems):
    pos = _mesh_pos()
    x, y, c = pos
    sibling = (x, y, 1 - c)

    def route(core):
        return [(x + (1 - core) * (1 - 2 * x), y + core * (1 - 2 * y)),
                (x + core * (1 - 2 * x), y + (1 - core) * (1 - 2 * y)),
                (1 - x, 1 - y)]

    mine = [(*chip, c) for chip in route(c)]
    carried = [pos, pos, pos, mine[0], mine[0], mine[1], mine[2]]
    to = [sibling, mine[0], mine[1], mine[1], sibling, sibling, sibling]
    landed = [sibling] + mine + [(*chip, 1 - c) for chip in route(1 - c)]

    def sent(a, k):
        dst = outs[a].at[_lin(carried[k])]
        return _remote(srcs[a] if k < 3 else dst, dst, *sems(a, k), to[k])

    def arrival(a, k):
        dst = outs[a].at[_lin(landed[k])]
        return _remote(dst, dst, *sems(a, k), pos)

    return sent, arrival


_ROUNDS = (range(0, 3), range(3, 6), range(6, 7))


def _round_sems(send, recv, rnd):
    n = len(_ROUNDS[rnd])
    return lambda a, k: (send.at[n * a + k - _ROUNDS[rnd][0]], recv.at[n * a + k - _ROUNDS[rnd][0]])


def weight_gather_start(win_sh, wout_sh):
    def body(win_ref, wout_ref, wg_in, woutg_in, send, recv, wg_ref, woutg_ref, token, local):
        del wg_in, woutg_in
        sent, _ = _gather_copies((win_ref, wout_ref), (wg_ref, woutg_ref), _round_sems(send, recv, 0))
        first = [sent(a, k) for a in range(2) for k in _ROUNDS[0]]
        me = _lin(_mesh_pos())
        mine = [pltpu.make_async_copy(win_ref, wg_ref.at[me], local.at[0]),
                pltpu.make_async_copy(wout_ref, woutg_ref.at[me], local.at[1])]
        for cp in first + mine:
            cp.start()
        for cp in mine:
            cp.wait()
        token[...] = jnp.zeros_like(token)

    hbm = lambda a: pltpu.with_memory_space_constraint(a, pltpu.HBM)
    wg = lax.empty((N_DEV,) + win_sh.shape, win_sh.dtype)
    woutg = lax.empty((N_DEV,) + wout_sh.shape, wout_sh.dtype)
    return pl.pallas_call(
        body, name="weight_gather_start",
        in_specs=[_HBM] * 4,
        out_specs=[_SEM, _SEM, _HBM, _HBM, pl.BlockSpec(memory_space=pltpu.VMEM)],
        out_shape=[pltpu.SemaphoreType.DMA((2 * len(_ROUNDS[0]),)), pltpu.SemaphoreType.DMA((2 * len(_ROUNDS[0]),)),
                   pltpu.HBM(wg.shape, wg.dtype), pltpu.HBM(woutg.shape, woutg.dtype),
                   jax.ShapeDtypeStruct((8, LANE), F32)],
        scratch_shapes=[pltpu.SemaphoreType.DMA((2,))],
        input_output_aliases={2: 2, 3: 3},
        compiler_params=pltpu.CompilerParams(has_side_effects=pltpu.SideEffectType.DATAFLOW_SIDE_EFFECTING),
    )(hbm(win_sh), hbm(wout_sh), hbm(wg), hbm(woutg))


def _wait_round(sent, arrival, rnd):
    for a in range(2):
        for k in _ROUNDS[rnd]:
            arrival(a, k).wait_recv()
            sent(a, k).wait_send()


def weight_gather_wait(send, recv, win_sh, wout_sh, wg, woutg, after):
    def body(win_ref, wout_ref, wg_ref, woutg_ref, send, recv, after_ref, wg_out, woutg_out):
        del after_ref, wg_out, woutg_out
        _wait_round(*_gather_copies((win_ref, wout_ref), (wg_ref, woutg_ref), _round_sems(send, recv, 0)), 0)

    hbm = lambda a: pltpu.with_memory_space_constraint(a, pltpu.HBM)
    return pl.pallas_call(
        body, name="weight_gather_wait",
        in_specs=[_HBM] * 4 + [_SEM, _SEM, pl.BlockSpec(memory_space=pl.ANY)],
        out_specs=[_HBM] * 2,
        out_shape=[pltpu.HBM(wg.shape, wg.dtype), pltpu.HBM(woutg.shape, woutg.dtype)],
        input_output_aliases={2: 0, 3: 1},
        compiler_params=pltpu.CompilerParams(has_side_effects=pltpu.SideEffectType.DATAFLOW_SIDE_EFFECTING),
    )(hbm(win_sh), hbm(wout_sh), wg, woutg, send, recv, after)


def weight_pass_start(wg, woutg):
    def body(wg_ref, woutg_ref, send, recv, wg_out, woutg_out):
        del wg_out, woutg_out
        sent, _ = _gather_copies(None, (wg_ref, woutg_ref), _round_sems(send, recv, 1))
        for a in range(2):
            for k in _ROUNDS[1]:
                sent(a, k).start()

    n = 2 * len(_ROUNDS[1])
    return pl.pallas_call(
        body, name="weight_pass_start",
        in_specs=[_HBM] * 2,
        out_specs=[_SEM, _SEM, _HBM, _HBM],
        out_shape=[pltpu.SemaphoreType.DMA((n,)), pltpu.SemaphoreType.DMA((n,)),
                   pltpu.HBM(wg.shape, wg.dtype), pltpu.HBM(woutg.shape, woutg.dtype)],
        input_output_aliases={0: 2, 1: 3},
        compiler_params=pltpu.CompilerParams(has_side_effects=pltpu.SideEffectType.DATAFLOW_SIDE_EFFECTING),
    )(wg, woutg)


def weight_pass_wait(send, recv, wg, woutg, after):
    def body(wg_ref, woutg_ref, send, recv, after_ref, wg_out, woutg_out):
        del after_ref, wg_out, woutg_out
        _wait_round(*_gather_copies(None, (wg_ref, woutg_ref), _round_sems(send, recv, 1)), 1)

    return pl.pallas_call(
        body, name="weight_pass_wait",
        in_specs=[_HBM] * 2 + [_SEM, _SEM, pl.BlockSpec(memory_space=pl.ANY)],
        out_specs=[_HBM] * 2,
        out_shape=[pltpu.HBM(wg.shape, wg.dtype), pltpu.HBM(woutg.shape, woutg.dtype)],
        input_output_aliases={0: 0, 1: 1},
        compiler_params=pltpu.CompilerParams(has_side_effects=pltpu.SideEffectType.DATAFLOW_SIDE_EFFECTING),
    )(wg, woutg, send, recv, after)


def weight_pass_last(wg, woutg):
    def body(wg_in, woutg_in, wg_ref, woutg_ref, send, recv):
        del wg_in, woutg_in
        sent, arrival = _gather_copies(None, (wg_ref, woutg_ref), _round_sems(send, recv, 2))
        for a in range(2):
            sent(a, _ROUNDS[2][0]).start()
        _wait_round(sent, arrival, 2)

    any_spec = pl.BlockSpec(memory_space=pl.ANY)
    n = 2 * len(_ROUNDS[2])
    return pl.pallas_call(
        body, name="weight_pass_last",
        in_specs=[any_spec, any_spec], out_specs=[any_spec, any_spec],
        out_shape=[jax.ShapeDtypeStruct(wg.shape, wg.dtype), jax.ShapeDtypeStruct(woutg.shape, woutg.dtype)],
        scratch_shapes=[pltpu.SemaphoreType.DMA((n,)), pltpu.SemaphoreType.DMA((n,))],
        input_output_aliases={0: 0, 1: 1},
    )(wg, woutg)


def grad_pair_exchange(g_sib, g_out):
    def body(gsib_ref, gout_ref, ra_ref, rb_ref, send, recv):
        pos = _mesh_pos()
        x, y, c = pos
        sibling = (x, y, 1 - c)
        copies = []
        for q in range(4):
            copies.append(_remote(gsib_ref.at[q], ra_ref.at[q], send.at[q], recv.at[q], sibling))
            copies.append(_remote(gout_ref.at[2 * q + (1 - c)], rb_ref.at[q], send.at[4 + q], recv.at[4 + q], sibling))
        for cp in copies:
            cp.start()
        for cp in copies:
            cp.wait_recv()
        for cp in copies:
            cp.wait_send()

    any_spec = pl.BlockSpec(memory_space=pl.ANY)
    return pl.pallas_call(
        body, name="grad_pair_exchange",
        in_specs=[any_spec, any_spec], out_specs=[any_spec, any_spec],
        out_shape=[jax.ShapeDtypeStruct(g_sib.shape, g_sib.dtype), jax.ShapeDtypeStruct((4,) + g_out.shape[1:], F32)],
        scratch_shapes=[pltpu.SemaphoreType.DMA((8,)), pltpu.SemaphoreType.DMA((8,))],
    )(g_sib, g_out)


def pair_sum(g_in, ra, g_out, rb, c_idx):
    tr = D_MODEL

    def body(c_ref, gin_ref, ra_ref, gout_ref, rb_ref, sb_ref, sbo_ref):
        del c_ref
        sb_ref[...] = (gin_ref[...] + ra_ref[...].astype(F32)).astype(sb_ref.dtype)
        sbo_ref[...] = gout_ref[...] + rb_ref[...]

    n_i = D_MODEL // tr
    return pl.pallas_call(
        body, name="pair_sum",
        grid_spec=pltpu.PrefetchScalarGridSpec(
            num_scalar_prefetch=1, grid=(4, n_i),
            in_specs=[pl.BlockSpec((None, tr, SHARD_IN), lambda q, i, c: (2 * q + c[0], i, 0)),
                      pl.BlockSpec((None, tr, SHARD_IN), lambda q, i, c: (q, i, 0)),
                      pl.BlockSpec((None, SHARD_OUT // n_i, D_MODEL), lambda q, i, c: (2 * q + c[0], i, 0)),
                      pl.BlockSpec((None, SHARD_OUT // n_i, D_MODEL), lambda q, i, c: (q, i, 0))],
            out_specs=[pl.BlockSpec((None, tr, SHARD_IN), lambda q, i, c: (q, i, 0)),
                       pl.BlockSpec((None, SHARD_OUT // n_i, D_MODEL), lambda q, i, c: (q, i, 0))]),
        out_shape=[jax.ShapeDtypeStruct(ra.shape, _BF), jax.ShapeDtypeStruct(rb.shape, F32)],
        compiler_params=_params(("arbitrary", "arbitrary")),
    )(c_idx, g_in, ra, g_out, rb)


_HBM = pl.BlockSpec(memory_space=pltpu.HBM)
_SEM = pl.BlockSpec(memory_space=pltpu.SEMAPHORE)
_N_CHIP_COPIES = 6


def _chip_copies(sb_ref, sbo_ref, rc_ref, rco_ref, send, recv):
    pos = _mesh_pos()
    copies = []
    for a, (src, dst) in enumerate(((sb_ref, rc_ref), (sbo_ref, rco_ref))):
        for j, chip in enumerate(_other_chips(pos)):
            copies.append(_remote(src.at[2 * chip[0] + chip[1]], dst.at[j], send.at[3 * a + j], recv.at[3 * a + j],
                                  (*chip, pos[2])))
    return copies


def grad_chip_start(sb, sbo):
    def body(sb_ref, sbo_ref, rc_ref, rco_ref, send, recv, sb_thru, sbo_thru, rc_thru, rco_thru, token):
        del sb_thru, sbo_thru, rc_thru, rco_thru
        for cp in _chip_copies(sb_ref, sbo_ref, rc_ref, rco_ref, send, recv):
            cp.start()
        token[...] = jnp.zeros_like(token)

    hbm = lambda a: pltpu.with_memory_space_constraint(a, pltpu.HBM)
    rc = lax.empty((3,) + sb.shape[1:], sb.dtype)
    rco = lax.empty((3,) + sbo.shape[1:], sbo.dtype)
    return pl.pallas_call(
        body, name="grad_chip_start",
        in_specs=[_HBM] * 4,
        out_specs=[_SEM, _SEM, _HBM, _HBM, _HBM, _HBM, pl.BlockSpec(memory_space=pltpu.VMEM)],
        out_shape=[pltpu.SemaphoreType.DMA((_N_CHIP_COPIES,)), pltpu.SemaphoreType.DMA((_N_CHIP_COPIES,)),
                   pltpu.HBM(sb.shape, sb.dtype), pltpu.HBM(sbo.shape, sbo.dtype),
                   pltpu.HBM(rc.shape, rc.dtype), pltpu.HBM(rco.shape, rco.dtype),
                   jax.ShapeDtypeStruct((8, LANE), F32)],
        input_output_aliases={0: 2, 1: 3, 2: 4, 3: 5},
        compiler_params=pltpu.CompilerParams(has_side_effects=pltpu.SideEffectType.DATAFLOW_SIDE_EFFECTING),
    )(hbm(sb), hbm(sbo), hbm(rc), hbm(rco))


def grad_chip_wait(send, recv, sb, sbo, rc, rco, after):
    def body(sb_ref, sbo_ref, rc_ref, rco_ref, send, recv, after_ref, sb_o, sbo_o, rc_o, rco_o):
        del after_ref, sb_o, sbo_o, rc_o, rco_o
        for cp in _chip_copies(sb_ref, sbo_ref, rc_ref, rco_ref, send, recv):
            cp.wait_send()
            cp.wait_recv()

    return pl.pallas_call(
        body, name="grad_chip_wait",
        in_specs=[_HBM] * 4 + [_SEM, _SEM, pl.BlockSpec(memory_space=pl.ANY)],
        out_specs=[_HBM] * 4,
        out_shape=[pltpu.HBM(sb.shape, sb.dtype), pltpu.HBM(sbo.shape, sbo.dtype),
                   pltpu.HBM(rc.shape, rc.dtype), pltpu.HBM(rco.shape, rco.dtype)],
        input_output_aliases={0: 0, 1: 1, 2: 2, 3: 3},
        compiler_params=pltpu.CompilerParams(has_side_effects=pltpu.SideEffectType.DATAFLOW_SIDE_EFFECTING),
    )(sb, sbo, rc, rco, send, recv, after)


def pack_gather(pack):
    def body(pack_ref, packs_ref, psend, precv):
        pos = _mesh_pos()
        me = _lin(pos)
        packs_ref[me] = pack_ref[...]
        peers = [_xor_peer(pos, k) for k in range(1, N_DEV)]
        gather = [_remote(packs_ref.at[me], packs_ref.at[me], psend.at[n], precv.at[n], p) for n, p in enumerate(peers)]
        for cp in gather:
            cp.start()
        for n, p in enumerate(peers):
            _remote(packs_ref.at[_lin(p)], packs_ref.at[_lin(p)], psend.at[n], precv.at[n], p).wait_recv()
        for cp in gather:
            cp.wait_send()

    vmem = pl.BlockSpec(memory_space=pltpu.VMEM)
    return pl.pallas_call(
        body, name="pack_gather", in_specs=[vmem], out_specs=vmem,
        out_shape=jax.ShapeDtypeStruct((N_DEV,) + pack.shape, F32),
        scratch_shapes=[pltpu.SemaphoreType.DMA((N_DEV - 1,)), pltpu.SemaphoreType.DMA((N_DEV - 1,))],
    )(pack)


def pack_rows(vec_mid, vec_ada, dlb):
    def body(mid_ref, ada_ref, dlb_ref, o_ref):
        mid = lambda r: mid_ref[r:r + 1, :]
        rows = [ada_ref[0:1, :], dlb_ref[...], mid(MID_HG_G), mid(MID_RET_G), mid(MID_FINAL_G),
                ada_ref[2:3, :], ada_ref[1:2, :], mid(MID_GATE), mid(MID_LOSS)]
        o_ref[...] = jnp.zeros_like(o_ref)
        for n, row in enumerate(rows):
            o_ref[n:n + 1, :] = row

    vmem = pl.BlockSpec(memory_space=pltpu.VMEM)
    return pl.pallas_call(body, name="pack_rows", in_specs=[vmem] * 3, out_specs=vmem,
                          out_shape=jax.ShapeDtypeStruct((PACK_ROWS, D_MODEL), F32))(vec_mid, vec_ada, dlb)


def _adamw(w, g, m, v):
    m = ADAM_B1 * m + (1.0 - ADAM_B1) * g
    v = ADAM_B2 * v + (1.0 - ADAM_B2) * (g * g)
    m_hat = m / (1.0 - ADAM_B1 ** ADAM_STEP)
    v_hat = v / (1.0 - ADAM_B2 ** ADAM_STEP)
    delta = -ADAM_LR * (m_hat / (jnp.sqrt(v_hat) + ADAM_EPS) + ADAM_WD * w)
    return delta, m, v


def adam_shard(chip_idx, own, parts, w, m, v, name):
    rows, cols = w.shape
    tr = min(rows, 256)

    def body(chip_ref, p0, p1, p2, p3, w_ref, m_ref, v_ref, g_ref, d_ref, nm_ref, nv_ref):
        del chip_ref
        g = ((p0[...].astype(F32) + p1[...].astype(F32)) + p2[...].astype(F32)) + p3[...].astype(F32)
        g_ref[...] = g
        d_ref[...], nm_ref[...], nv_ref[...] = _adamw(w_ref[...], g, m_ref[...], v_ref[...])

    part = lambda q: pl.BlockSpec((None, tr, cols), lambda i, chip, q=q: (q, i, 0))
    tile = pl.BlockSpec((tr, cols), lambda i, chip: (i, 0))
    return pl.pallas_call(
        body, name=name,
        grid_spec=pltpu.PrefetchScalarGridSpec(
            num_scalar_prefetch=1, grid=(rows // tr,),
            in_specs=[pl.BlockSpec((None, tr, cols), lambda i, chip: (chip[0], i, 0)), part(0), part(1), part(2),
                      tile, tile, tile],
            out_specs=[tile] * 4),
        out_shape=[jax.ShapeDtypeStruct(w.shape, F32)] * 4,
        compiler_params=_params(("arbitrary",)),
    )(chip_idx, own, parts, parts, parts, w, m, v)


def adam_ada(sc_t, dmod_all, me_idx, w, m, v):
    def body(me_ref, sc_ref, dm_ref, w_ref, m_ref, v_ref, g_ref, d_ref, nm_ref, nv_ref):
        del me_ref
        g = _dot_f32(sc_ref[...], dm_ref[...])
        g_ref[...] = g
        d_ref[...], nm_ref[...], nv_ref[...] = _adamw(w_ref[...], g, m_ref[...], v_ref[...])

    full = pl.BlockSpec(w.shape, lambda i, me: (0, 0))
    return pl.pallas_call(
        body, name="adam_ada",
        grid_spec=pltpu.PrefetchScalarGridSpec(
            num_scalar_prefetch=1, grid=(1,),
            in_specs=[pl.BlockSpec(sc_t.shape, lambda i, me: (0, 0)),
                      pl.BlockSpec((LANE, SHARD_ADA), lambda i, me: (0, me[0])), full, full, full],
            out_specs=[full] * 4),
        out_shape=[jax.ShapeDtypeStruct(w.shape, F32)] * 4,
        compiler_params=_params(("arbitrary",)),
    )(me_idx, sc_t, dmod_all, w, m, v)


def adam_vectors(packs, lb, params, ms, vs):
    n = len(params)

    def body(*refs):
        packs_ref, lb_ref = refs[0], refs[1]
        w_refs, m_refs, v_refs = refs[2:2 + n], refs[2 + n:2 + 2 * n], refs[2 + 2 * n:2 + 3 * n]
        loss_ref = refs[2 + 3 * n]
        outs = refs[3 + 3 * n:3 + 7 * n]
        tot_ref = refs[3 + 7 * n]
        tot = packs_ref[0]
        for d in range(1, N_DEV):
            tot = tot + packs_ref[d]
        tot_ref[...] = tot
        row = lambda r: tot_ref[r:r + 1, :]
        lbv = lb_ref[...]
        dl0 = row(ROW_LB) * lbv * (1.0 - lbv)
        grads = [[row(ROW_NORM_G)],
                 [jnp.concatenate([row(ROW_SHIFT), row(ROW_SCALE), row(ROW_GATE)], axis=1)],
                 [dl0, -dl0],
                 [row(ROW_HG_G)], [row(ROW_RET_G)], [row(ROW_FINAL_G)]]
        loss_ref[...] = tot_ref[ROW_LOSS:ROW_LOSS + 1, 0:LANE]
        for j, g_rows in enumerate(grads):
            for r, g in enumerate(g_rows):
                rs = slice(r, r + 1)
                d, nm, nv = _adamw(w_refs[j][rs, :], g, m_refs[j][rs, :], v_refs[j][rs, :])
                outs[4 * j][rs, :] = g
                outs[4 * j + 1][rs, :] = d
                outs[4 * j + 2][rs, :] = nm
                outs[4 * j + 3][rs, :] = nv

    vmem = pl.BlockSpec(memory_space=pltpu.VMEM)
    out_shape = [jax.ShapeDtypeStruct((1, LANE), F32)]
    for w in params:
        out_shape += [jax.ShapeDtypeStruct(w.shape, F32)] * 4
    return pl.pallas_call(
        body, name="adam_vectors", in_specs=[vmem] * (2 + 3 * n), out_specs=[vmem] * len(out_shape),
        out_shape=out_shape, scratch_shapes=[pltpu.VMEM((PACK_ROWS, D_MODEL), F32)],
    )(packs, lb, *params, *ms, *vs)


def kernel(x, c, norm_g, w_ada, b_ada, w_in, hg_lb_logits, hg_norm_g, ret_norm_g, w_out, final_g, loss_target, m_norm_g, m_w_ada, m_b_ada, m_w_in, m_hg_lb_logits, m_hg_norm_g, m_ret_norm_g, m_w_out, m_final_g, v_norm_g, v_w_ada, v_b_ada, v_w_in, v_hg_lb_logits, v_hg_norm_g, v_ret_norm_g, v_w_out, v_final_g):
    pos = _mesh_pos()
    me_idx = jnp.reshape(_lin(pos), (1,)).astype(jnp.int32)
    c_idx = jnp.reshape(pos[2], (1,)).astype(jnp.int32)
    vec = lambda a: a.reshape(1, D_MODEL)

    mod, scall, lb, wt_sh, wout_sh = pre_exchange(c, w_ada[0], b_ada, hg_lb_logits, w_in[0], w_out[0])
    chip_idx = jnp.reshape(2 * pos[0] + pos[1], (1,)).astype(jnp.int32)
    gather_send, gather_recv, wtg, woutg, gather_token = weight_gather_start(wt_sh, wout_sh)
    mod = mod + gather_token[:1, :1]

    other_chips = jnp.stack([2 * cx + cy for cx, cy in _other_chips(pos)]).astype(jnp.int32)

    def project(h):
        t_len = h.shape[0]
        flat_in = lambda a: a.reshape(D_IN, D_MODEL)
        wg, wog = weight_gather_wait(gather_send, gather_recv, wt_sh, wout_sh, wtg, woutg, h)
        pass_send, pass_recv, wg, wog = weight_pass_start(wg, wog)
        pb = proj_forward(h, flat_in(wg), chip_idx, t_len, "proj_fwd_own")
        wg, wog = weight_pass_wait(pass_send, pass_recv, wg, wog, pb)
        wg, wog = weight_pass_last(wg, wog)
        pb = proj_forward(h, flat_in(wg), other_chips, t_len, "proj_fwd_rest", pb)
        return pb, flat_in(wg), wog.reshape(D_MODEL, D_MODEL)

    def start_exchange(dwin, dwin_sib, dwout):
        dwout = dwout.reshape(N_DEV, SHARD_OUT, D_MODEL)
        ra, rb = grad_pair_exchange(dwin_sib, dwout)
        sb, sbo = pair_sum(dwin, ra, dwout, rb, c_idx)
        send, recv, sb, sbo, rc, rco, token = grad_chip_start(sb, sbo)
        return token, (send, recv, sb, sbo, rc, rco)

    grad_x, _, _, vec_mid, vec_ada, dlb, pending = device_step(
        x[0], loss_target[0], mod, lb, project, norm_g, hg_norm_g, ret_norm_g, vec(final_g), c_idx, start_exchange)
    packs = pack_gather(pack_rows(vec_mid, vec_ada, dlb))
    dmod_all = packs[:, ROW_SHIFT:ROW_GATE + 1, :].reshape(N_DEV, 3 * D_MODEL)
    dmod_all = jnp.pad(dmod_all, ((0, LANE - N_DEV), (0, 0)))
    sc_t = jnp.pad(scall.T, ((0, 0), (0, LANE - N_DEV)))
    g_ada, d_ada, nm_ada, nv_ada = adam_ada(sc_t, dmod_all, me_idx, w_ada[0], m_w_ada[0], v_w_ada[0])
    small = adam_vectors(
        packs, lb,
        (norm_g, b_ada, hg_lb_logits, hg_norm_g, ret_norm_g, vec(final_g)),
        (m_norm_g, m_b_ada, m_hg_lb_logits, m_hg_norm_g, m_ret_norm_g, vec(m_final_g)),
        (v_norm_g, v_b_ada, v_hg_lb_logits, v_hg_norm_g, v_ret_norm_g, vec(v_final_g)))
    loss = small[0][0, 0]
    sb, sbo, rc, rco = grad_chip_wait(*pending, small[0])
    g_in, d_in, nm_in, nv_in = adam_shard(chip_idx, sb, rc, w_in[0], m_w_in[0], v_w_in[0], "adam_w_in")
    g_out, d_out, nm_out, nv_out = adam_shard(chip_idx, sbo, rco, w_out[0], m_w_out[0], v_w_out[0], "adam_w_out")
    (g_ng, d_ng, nm_ng, nv_ng), (g_b, d_b, nm_b, nv_b), (g_lb, d_lb, nm_lb, nv_lb), (g_hg, d_hg, nm_hg, nv_hg), \
        (g_rg, d_rg, nm_rg, nv_rg), (g_fg, d_fg, nm_fg, nv_fg) = [small[1 + 4 * j:5 + 4 * j] for j in range(6)]
    flat = lambda a: a.reshape(D_MODEL)

    def group(ng, ada, b, win, lbl, hg, rg, wo, fg):
        return (ng, ada[None], b, win[None], lbl, hg, rg, wo[None], flat(fg))

    return (loss, grad_x[None],
            *group(g_ng, g_ada, g_b, g_in, g_lb, g_hg, g_rg, g_out, g_fg),
            *group(d_ng, d_ada, d_b, d_in, d_lb, d_hg, d_rg, d_out, d_fg),
            *group(nm_ng, nm_ada, nm_b, nm_in, nm_lb, nm_hg, nm_rg, nm_out, nm_fg),
            *group(nv_ng, nv_ada, nv_b, nv_in, nv_lb, nv_hg, nv_rg, nv_out, nv_fg))
```

```python
import numpy as np
import jax
import jax.numpy as jnp
from jax import lax
from jax.experimental import pallas as pl
from jax.experimental.pallas import tpu as pltpu

F32 = jnp.float32
_BF = jnp.bfloat16

D_MODEL = 1024
N_HEADS = 8
LANE = 128
RET_DK = 64
D_IN = 9216
N_DEV = 8
SHARD_IN = D_IN // N_DEV
SHARD_ADA = 3 * D_MODEL // N_DEV
SHARD_OUT = D_MODEL // N_DEV
N_CB = D_IN // LANE
CB_PER_SHARD = SHARD_IN // LANE
CHUNK = 128
N_LEVELS = 7
EPS = 1e-6
LOG2_E = float(np.log2(np.e))
ROPE_BASE = 10000.0
CB_HQ, CB_HF, CB_HI, CB_HZ, CB_RQ, CB_RK, CB_RV, CB_RZ, CB_GA, CB_GB = 0, 8, 16, 24, 32, 36, 40, 48, 56, 64
VMEM_LIMIT = 56 * 1024 * 1024

ADAM_LR, ADAM_B1, ADAM_B2, ADAM_EPS, ADAM_WD, ADAM_STEP = 0.001, 0.9, 0.999, 1e-08, 0.01, 10

_NN = (((1,), (0,)), ((), ()))
_NT = (((1,), (1,)), ((), ()))
_TN = (((0,), (0,)), ((), ()))
MESH = pl.DeviceIdType.MESH


def _dot(a, b, dims=_NN):
    return lax.dot_general(a.astype(_BF), b.astype(_BF), dims, preferred_element_type=F32)


def _split2(a):
    hi = a.astype(_BF)
    lo = (a - hi.astype(F32)).astype(_BF)
    return jnp.concatenate([hi, lo], axis=1)


def _dot_sel(sel, a):
    n = a.shape[1]
    r = lax.dot_general(sel.astype(_BF), _split2(a), _NN, preferred_element_type=F32)
    return r[:, :n] + r[:, n:]


def _dot_f32(a, b):
    def pieces(v):
        p1 = v.astype(_BF)
        r1 = v - p1.astype(F32)
        p2 = r1.astype(_BF)
        p3 = (r1 - p2.astype(F32)).astype(_BF)
        return p1, p2, p3
    a1, a2, a3 = pieces(a)
    b1, b2, b3 = pieces(b)
    d = lambda u, v: lax.dot_general(u, v, _NN, preferred_element_type=F32)
    return ((d(a1, b3) + d(a2, b2) + d(a3, b1)) + (d(a1, b2) + d(a2, b1))) + d(a1, b1)


def _sigmoid(v):
    return 1.0 / (1.0 + jnp.exp(-v))


def _params(sem=None):
    return pltpu.CompilerParams(dimension_semantics=sem, vmem_limit_bytes=VMEM_LIMIT)


def _hgrn_consts():
    c, nl = CHUNK, N_LEVELS
    t = np.arange(c)[:, None]
    j = np.arange(c)[None, :]
    sel = [j <= t]
    masks = [j == t]
    for l in range(1, nl + 1):
        m = ((t >> l) << l) + (1 << (l - 1)) - 1
        sec = t > m
        sel.append(np.where(sec, (j > m) & (j <= t), (j > t) & (j <= m)))
        same = (t >> l) == (j >> l)
        masks.append(same & sec & (j <= m))
    sel.append(j > t)
    sel = np.concatenate(sel, 0).astype(np.float32)
    masks = np.stack(masks).astype(np.float32)
    sgn = np.stack([np.where((t & (1 << (l - 1))) != 0, 1.0, -1.0) * np.ones((1, LANE)) for l in range(3, nl + 1)])
    return dict(tri=jnp.asarray(sel[:c], _BF),
                lvl=jnp.asarray(masks, F32),
                sgn=jnp.asarray(sgn, F32),
                sel_t=jnp.asarray(sel.T, _BF),
                lvl_b=jnp.asarray(masks, _BF),
                lvlt_b=jnp.asarray(np.swapaxes(masks, 1, 2), _BF))


def _level_exponents(b, logf, b_scr, sgn_ref):
    c = CHUNK
    b_scr[...] = b
    row = lax.broadcasted_iota(jnp.int32, (c, LANE), 0)
    nxt = pltpu.roll(logf, c - 1, 0)
    prv = pltpu.roll(logf, 1, 0)
    r4 = row & 3
    out = [jnp.where((row & 1) == 1, logf, 0.0),
           jnp.where(r4 == 0, nxt, jnp.where(r4 == 1, 0.0, jnp.where(r4 == 2, logf, logf + prv)))]
    for l in range(3, N_LEVELS + 1):
        size, half = 1 << l, 1 << (l - 1)
        ref = jnp.concatenate([jnp.broadcast_to(b_scr[i * size + half - 1:i * size + half, :], (size, LANE))
                               for i in range(c // size)], axis=0)
        out.append((b - ref) * sgn_ref[l - 3])
    return out


def _hgrn_chunk(hq, hf, hi, lbv, tri_ref, sgn_ref, b_scr):
    sq = _sigmoid(hq)
    q = hq * sq
    sg = _sigmoid(hf)
    omlb = 1.0 - lbv
    f = lbv + omlb * sg
    k = 1.0 - f
    logf = jnp.log(f) * LOG2_E
    b = _dot_sel(tri_ref[...], logf)
    bc = jnp.sum(logf, axis=0, keepdims=True)
    lev = [None] + [jnp.exp2(e) for e in _level_exponents(b, logf, b_scr, sgn_ref)]
    return dict(sq=sq, q=q, sg=sg, omlb=omlb, f=f, k=k, v=hi, eb=jnp.exp2(b), erem=jnp.exp2(bc - b),
                ebc=jnp.exp2(bc), lev=lev)


def _blockdiag(a, b):
    z = jnp.zeros_like(a)
    return jnp.concatenate([jnp.concatenate([a, z], axis=1), jnp.concatenate([z, b], axis=1)], axis=0)


def _level_operands(a):
    q, k = a["q"].astype(_BF), a["k"].astype(_BF)
    lev = [None] + [a["lev"][l].astype(_BF) for l in range(1, N_LEVELS + 1)]
    ql = [q] + [q * lev[l] for l in range(1, N_LEVELS + 1)]
    kl = [k] + [k * lev[l] for l in range(1, N_LEVELS + 1)]
    pairs = range(0, N_LEVELS + 1, 2)
    return ([jnp.concatenate([ql[l], ql[l + 1]], axis=1) for l in pairs], [_blockdiag(kl[l], kl[l + 1]) for l in pairs],
            ql, kl)


def _hgrn_scores(a, lvl_ref, q_pairs, k_diags):
    acc = None
    for n, (qp, kd) in enumerate(zip(q_pairs, k_diags)):
        both = lax.dot_general(qp, kd, _NT, preferred_element_type=F32)
        part = lvl_ref[2 * n] * both[:, :CHUNK] + lvl_ref[2 * n + 1] * both[:, CHUNK:]
        acc = part if acc is None else acc + part
    return acc


SCAN_UNROLL = 8
FWD_UNROLL = 16
RET_UNROLL = 16


def _writeback_reserve(step, make_copies):
    slot = step % 2

    @pl.when(step >= 2)
    def _():
        for cp in make_copies(slot):
            cp.wait()

    return slot


def _writeback_commit(step, n_steps, slot, make_copies):
    for cp in make_copies(slot):
        cp.start()

    @pl.when(step == n_steps - 1)
    def _():
        for cp in make_copies(slot):
            cp.wait()
        if n_steps > 1:
            for cp in make_copies(1 - slot):
                cp.wait()


def _resident(const):
    zeros = (0,) * const.ndim
    return pl.BlockSpec(const.shape, lambda p, t: zeros)


def _time_block(t_len):
    return min(t_len, 2048)


def hgrn_forward(pb, lb, t_len):
    nc = t_len // CHUNK
    tb = _time_block(t_len)
    ncb = tb // CHUNK
    consts = _hgrn_consts()
    operands = [consts[n] for n in ("tri", "lvl", "sgn")]

    def body(hq_ref, hf_ref, hi_ref, lb_ref, tri_ref, lvl_ref, sgn_ref, o_ref, ssave_ref, asave_ref, st_ref, b_scr):
        @pl.when(pl.program_id(1) == 0)
        def _():
            st_ref[...] = jnp.zeros_like(st_ref)

        def chunk(ci, carry):
            r = pl.ds(pl.multiple_of(ci * CHUNK, CHUNK), CHUNK)
            for hd in range(2):
                lbv = lb_ref[:, hd * LANE:(hd + 1) * LANE]
                a = _hgrn_chunk(hq_ref[hd, r, :], hf_ref[hd, r, :], hi_ref[hd, r, :], lbv, tri_ref, sgn_ref,
                                b_scr.at[hd])
                q_pairs, k_diags, _, _ = _level_operands(a)
                st = st_ref[hd]
                ssave_ref[hd, ci] = st
                scores = _hgrn_scores(a, lvl_ref, q_pairs, k_diags).astype(asave_ref.dtype)
                asave_ref[hd, ci] = scores
                o_ref[hd, r, :] = _dot(a["q"] * a["eb"], st, _NT) + _dot(scores, a["v"])
                st_ref[hd] = st * a["ebc"] + _dot(a["v"], a["k"] * a["erem"], _TN)
            return carry

        lax.fori_loop(0, ncb, chunk, 0, unroll=FWD_UNROLL)

    pair = lambda base: pl.BlockSpec((2, tb, LANE), lambda p, t, base=base: (base // 2 + p, t, 0))
    per_chunk = pl.BlockSpec((2, ncb, LANE, LANE), lambda p, t: (p, t, 0, 0))
    return pl.pallas_call(
        body, name="hgrn_fwd", grid=(N_HEADS // 2, t_len // tb),
        in_specs=[pair(CB_HQ), pair(CB_HF), pair(CB_HI),
                  pl.BlockSpec((1, 2 * LANE), lambda p, t: (0, p))] + [_resident(c) for c in operands],
        out_specs=[pl.BlockSpec((2, tb, LANE), lambda p, t: (p, t, 0)), per_chunk, per_chunk],
        out_shape=[jax.ShapeDtypeStruct((N_HEADS, t_len, LANE), F32),
                   jax.ShapeDtypeStruct((N_HEADS, nc, LANE, LANE), F32),
                   jax.ShapeDtypeStruct((N_HEADS, nc, CHUNK, CHUNK), _BF)],
        scratch_shapes=[pltpu.VMEM((2, LANE, LANE), F32), pltpu.VMEM((2, CHUNK, LANE), F32)],
        compiler_params=_params(("arbitrary", "arbitrary")),
    )(pb, pb, pb, lb, *operands)


def hgrn_backward(pb, lb, do, ssave, asave, dpb, t_len):
    tb = _time_block(t_len)
    ncb, ntb = tb // CHUNK, t_len // tb
    consts = _hgrn_consts()
    operands = [consts[n] for n in ("tri", "sgn", "sel_t", "lvl_b", "lvlt_b")]

    def body(hq_ref, hf_ref, hi_ref, lb_ref, do_ref, ssave_ref, asave_ref, tri_ref, sgn_ref, selt_ref, lvlb_ref,
             lvltb_ref, dpb_in, dpb_ref, dlb_ref, dq_buf, df_buf, di_buf, dst_ref, b_scr, sems):
        del dpb_in
        p, t = pl.program_id(0), pl.program_id(1)
        step = p * ntb + t
        rows = pl.ds(pl.multiple_of((ntb - 1 - t) * tb, tb), tb)

        def out_copies(sl):
            return [pltpu.make_async_copy(buf.at[sl], dpb_ref.at[pl.ds(base + 2 * p, 2), rows], sems.at[sl, n])
                    for n, (buf, base) in enumerate(((dq_buf, CB_HQ), (df_buf, CB_HF), (di_buf, CB_HI)))]

        slot = _writeback_reserve(step, out_copies)

        @pl.when(t == 0)
        def _():
            dst_ref[...] = jnp.zeros_like(dst_ref)
            dlb_ref[...] = jnp.zeros_like(dlb_ref)

        def chunk(i, carry):
            ci = ncb - 1 - i
            r = pl.ds(pl.multiple_of(ci * CHUNK, CHUNK), CHUNK)
            for hd in range(2):
                head_chunk(hd, ci, r)
            return carry

        def head_chunk(hd, ci, r):
            lbv = lb_ref[:, hd * LANE:(hd + 1) * LANE]
            hq = hq_ref[hd, r, :]
            a = _hgrn_chunk(hq, hf_ref[hd, r, :], hi_ref[hd, r, :], lbv, tri_ref, sgn_ref, b_scr.at[hd])
            _, k_diags, ql, kl = _level_operands(a)
            q, k, v = a["q"], a["k"], a["v"]
            g = do_ref[hd, r, :]
            st0 = ssave_ref[hd, ci]
            dst = dst_ref[hd]
            scores = asave_ref[hd, ci]
            da = _dot(g, v, _NT)
            da_t = _dot(v, g, _NT)
            kb = k * a["erem"]
            qb = q * a["eb"]
            dv = _dot(scores, g, _TN) + _dot(kb, dst, _NT)
            dq_inter = _dot(g, st0) * a["eb"]
            dk_state = _dot(v, dst) * a["erem"]
            dq, dk = dq_inter, dk_state
            de = [q * dq_inter]
            da_b, dat_b = da.astype(_BF), da_t.astype(_BF)
            for n in range(len(k_diags)):
                l0, l1 = 2 * n, 2 * n + 1
                da_pair = jnp.concatenate([lvlb_ref[l0] * da_b, lvlb_ref[l1] * da_b], axis=1)
                dat_pair = jnp.concatenate([lvltb_ref[l0] * dat_b, lvltb_ref[l1] * dat_b], axis=1)
                dq_both = lax.dot_general(da_pair, k_diags[n], _NN, preferred_element_type=F32)
                dk_both = lax.dot_general(dat_pair, _blockdiag(ql[l0], ql[l1]), _NN, preferred_element_type=F32)
                for l, cols in ((l0, slice(0, LANE)), (l1, slice(LANE, 2 * LANE))):
                    dql, dkl = dq_both[:, cols], dk_both[:, cols]
                    if l > 0:
                        e = a["lev"][l]
                        dql, dkl = dql * e, dkl * e
                        de.append(q * dql + k * dkl)
                    dq = dq + dql
                    dk = dk + dkl
            de.append(k * dk_state)
            dst_ref[hd] = dst * a["ebc"] + _dot(g, qb, _TN)
            dbc = jnp.sum(dst * st0, axis=0, keepdims=True) * a["ebc"]
            de2 = lax.dot_general(selt_ref[...], _split2(jnp.concatenate(de, axis=0)), _NN,
                                  preferred_element_type=F32)
            dlogf = de2[:, :LANE] + de2[:, LANE:] + dbc
            sq, sg = a["sq"], a["sg"]
            df = dlogf / a["f"] - dk
            dq_buf[slot, hd, r, :] = (dq * (sq * (1.0 + hq * (1.0 - sq)))).astype(dq_buf.dtype)
            df_buf[slot, hd, r, :] = (df * a["omlb"] * sg * (1.0 - sg)).astype(df_buf.dtype)
            di_buf[slot, hd, r, :] = dv.astype(di_buf.dtype)
            cols = slice(hd * LANE, (hd + 1) * LANE)
            dlb_ref[:, cols] = dlb_ref[:, cols] + jnp.sum(df * (1.0 - sg), axis=0, keepdims=True)

        lax.fori_loop(0, ncb, chunk, 0, unroll=SCAN_UNROLL)
        _writeback_commit(step, (N_HEADS // 2) * ntb, slot, out_copies)

    pair = lambda base: pl.BlockSpec((2, tb, LANE), lambda p, t, base=base: (base // 2 + p, ntb - 1 - t, 0))
    any_spec = pl.BlockSpec(memory_space=pl.ANY)
    per_chunk = pl.BlockSpec((2, ncb, LANE, LANE), lambda p, t: (p, ntb - 1 - t, 0, 0))
    return pl.pallas_call(
        body, name="hgrn_bwd", grid=(N_HEADS // 2, ntb),
        in_specs=[pair(CB_HQ), pair(CB_HF), pair(CB_HI),
                  pl.BlockSpec((1, 2 * LANE), lambda p, t: (0, p)),
                  pair(0), per_chunk, per_chunk]
        + [_resident(c) for c in operands] + [any_spec],
        out_specs=[any_spec, pl.BlockSpec((1, 2 * LANE), lambda p, t: (0, p))],
        out_shape=[jax.ShapeDtypeStruct(dpb.shape, dpb.dtype), jax.ShapeDtypeStruct((1, D_MODEL), F32)],
        scratch_shapes=[pltpu.VMEM((2, 2, tb, LANE), dpb.dtype)] * 3 + [
            pltpu.VMEM((2, LANE, LANE), F32), pltpu.VMEM((2, CHUNK, LANE), F32), pltpu.SemaphoreType.DMA((2, 3))],
        input_output_aliases={7 + len(operands): 0},
        compiler_params=_params(("arbitrary", "arbitrary")),
    )(pb, pb, pb, lb, do, ssave, asave, *operands, dpb)


def _rope_tables(t_len):
    half = RET_DK // 2
    inv_freq = (1.0 / (np.float32(ROPE_BASE) ** np.linspace(0.0, 1.0, half, dtype=np.float32))).astype(np.float32)
    ang = (np.arange(t_len, dtype=np.float32)[:, None] * inv_freq[None, :]).astype(np.float64)
    cos, sin = np.cos(ang).astype(np.float32), np.sin(ang).astype(np.float32)
    cos_t = np.concatenate([cos, cos, cos, cos], axis=1)
    sin_t = np.concatenate([-sin, sin, -sin, sin], axis=1)
    return jnp.asarray(cos_t), jnp.asarray(sin_t)


def _swap_halves(v):
    half = RET_DK // 2
    lane = lax.broadcasted_iota(jnp.int32, v.shape, 1)
    first = (lane & (RET_DK - 1)) < half
    return jnp.where(first, pltpu.roll(v, LANE - half, 1), pltpu.roll(v, half, 1))


def _ret_head_consts(hidx):
    c = CHUNK
    hf = jnp.full((1, LANE), hidx, jnp.int32).astype(F32)
    lg = jnp.log(1.0 - jnp.exp(-(5.0 + hf) * np.float32(np.log(2.0))))
    row = lax.broadcasted_iota(jnp.int32, (c, c), 0)
    col = lax.broadcasted_iota(jnp.int32, (c, c), 1)
    rel = (row - col).astype(F32)
    dm = jnp.where(rel >= 0, jnp.exp(lg[:, :1] * jnp.maximum(rel, 0.0)), 0.0)
    dm_t = jnp.where(rel <= 0, jnp.exp(lg[:, :1] * jnp.maximum(-rel, 0.0)), 0.0)
    idx = lax.broadcasted_iota(jnp.int32, (c, LANE), 0).astype(F32)
    zeta = jnp.exp(lg * (c - 1.0 - idx))
    xi = jnp.exp(lg * (idx + 1.0))
    cdec = jnp.exp(lg * float(c))
    return dm, zeta, xi, cdec, dm_t


def _lane_mask(which):
    lane = lax.broadcasted_iota(jnp.int32, (1, LANE), 1)
    return ((lane // RET_DK) == which).astype(F32)


def retention_forward(pb, cos_t, sin_t, t_len):
    nc = t_len // CHUNK

    tb = _time_block(t_len)
    ncb = tb // CHUNK

    def body(rq_ref, rk_ref, rv_ref, cos_ref, sin_ref, o_ref, rsave_ref, st_ref):
        p = pl.program_id(0)

        @pl.when(pl.program_id(1) == 0)
        def _():
            st_ref[...] = jnp.zeros_like(st_ref)

        consts = [_ret_head_consts(2 * p + hd) for hd in range(2)]

        def chunk(ci, carry):
            r = pl.ds(pl.multiple_of(ci * CHUNK, CHUNK), CHUNK)
            cs, sn = cos_ref[r, :], sin_ref[r, :]
            q = rq_ref[r, :]
            k = rk_ref[r, :]
            q = q * cs + _swap_halves(q) * sn
            k = (k * cs + _swap_halves(k) * sn) * RET_DK ** -0.5
            for hd in range(2):
                dm, zeta, xi, cdec, _ = consts[hd]
                lm = _lane_mask(hd)
                qh, kh = q * lm, k * lm
                v = rv_ref[hd, r, :]
                st = st_ref[hd]
                rsave_ref[hd, ci] = st
                scores = _dot(qh, kh, _NT) * dm
                o_ref[hd, r, :] = _dot(scores, v) + _dot(qh * xi, st, _NT)
                st_ref[hd] = st * cdec + _dot(v, kh * zeta, _TN)
            return carry

        lax.fori_loop(0, ncb, chunk, 0, unroll=RET_UNROLL)

    return pl.pallas_call(
        body, name="ret_fwd", grid=(N_HEADS // 2, t_len // tb),
        in_specs=[pl.BlockSpec((None, tb, LANE), lambda p, t: (CB_RQ + p, t, 0)),
                  pl.BlockSpec((None, tb, LANE), lambda p, t: (CB_RK + p, t, 0)),
                  pl.BlockSpec((2, tb, LANE), lambda p, t: (CB_RV // 2 + p, t, 0)),
                  pl.BlockSpec((tb, LANE), lambda p, t: (t, 0)),
                  pl.BlockSpec((tb, LANE), lambda p, t: (t, 0))],
        out_specs=[pl.BlockSpec((2, tb, LANE), lambda p, t: (p, t, 0)),
                   pl.BlockSpec((2, ncb, LANE, LANE), lambda p, t: (p, t, 0, 0))],
        out_shape=[jax.ShapeDtypeStruct((N_HEADS, t_len, LANE), F32),
                   jax.ShapeDtypeStruct((N_HEADS, nc, LANE, LANE), F32)],
        scratch_shapes=[pltpu.VMEM((2, LANE, LANE), F32)],
        compiler_params=_params(("arbitrary", "arbitrary")),
    )(pb, pb, pb, cos_t, sin_t)


def retention_backward(pb, cos_t, sin_t, do, rsave, dpb, t_len):
    tb = _time_block(t_len)
    ncb, ntb = tb // CHUNK, t_len // tb

    def body(rq_ref, rk_ref, rv_ref, cos_ref, sin_ref, do_ref, rsave_ref, dpb_in,
             dpb_ref, dq_buf, dk_buf, dv_buf, dst_ref, sems):
        del dpb_in
        p, t = pl.program_id(0), pl.program_id(1)
        step = p * ntb + t
        rows = pl.ds(pl.multiple_of((ntb - 1 - t) * tb, tb), tb)

        def out_copies(sl):
            return [pltpu.make_async_copy(dq_buf.at[sl], dpb_ref.at[CB_RQ + p, rows], sems.at[sl, 0]),
                    pltpu.make_async_copy(dk_buf.at[sl], dpb_ref.at[CB_RK + p, rows], sems.at[sl, 1]),
                    pltpu.make_async_copy(dv_buf.at[sl], dpb_ref.at[pl.ds(CB_RV + 2 * p, 2), rows], sems.at[sl, 2])]

        slot = _writeback_reserve(step, out_copies)

        @pl.when(t == 0)
        def _():
            dst_ref[...] = jnp.zeros_like(dst_ref)

        consts = [_ret_head_consts(2 * p + hd) for hd in range(2)]

        def chunk(i, carry):
            ci = ncb - 1 - i
            r = pl.ds(pl.multiple_of(ci * CHUNK, CHUNK), CHUNK)
            cs, sn = cos_ref[r, :], sin_ref[r, :]
            q = rq_ref[r, :]
            k = rk_ref[r, :]
            q = q * cs + _swap_halves(q) * sn
            k = (k * cs + _swap_halves(k) * sn) * RET_DK ** -0.5
            dq, dk = None, None
            for hd in range(2):
                dm, zeta, xi, cdec, dm_t = consts[hd]
                lm = _lane_mask(hd)
                qh, kh = q * lm, k * lm
                v = rv_ref[hd, r, :]
                g = do_ref[hd, r, :]
                st0 = rsave_ref[hd, ci]
                dst = dst_ref[hd]
                scores_t = _dot(kh, qh, _NT) * dm_t
                dsc = _dot(g, v, _NT) * dm
                dsc_t = _dot(v, g, _NT) * dm_t
                dqh = _dot(dsc, kh) + _dot(g, st0) * xi
                dkh = _dot(dsc_t, qh) + _dot(v, dst) * zeta
                dv_buf[slot, hd, r, :] = (_dot(scores_t, g) + _dot(kh * zeta, dst, _NT)).astype(dv_buf.dtype)
                dst_ref[hd] = dst * cdec + _dot(g, qh * xi, _TN)
                dq = dqh if dq is None else dq + dqh
                dk = dkh if dk is None else dk + dkh
            dk = dk * (RET_DK ** -0.5)
            dq_buf[slot, r, :] = (dq * cs - _swap_halves(dq) * sn).astype(dq_buf.dtype)
            dk_buf[slot, r, :] = (dk * cs - _swap_halves(dk) * sn).astype(dk_buf.dtype)
            return carry

        lax.fori_loop(0, ncb, chunk, 0, unroll=RET_UNROLL)
        _writeback_commit(step, (N_HEADS // 2) * ntb, slot, out_copies)

    any_spec = pl.BlockSpec(memory_space=pl.ANY)
    return pl.pallas_call(
        body, name="ret_bwd", grid=(N_HEADS // 2, ntb),
        in_specs=[pl.BlockSpec((None, tb, LANE), lambda p, t: (CB_RQ + p, ntb - 1 - t, 0)),
                  pl.BlockSpec((None, tb, LANE), lambda p, t: (CB_RK + p, ntb - 1 - t, 0)),
                  pl.BlockSpec((2, tb, LANE), lambda p, t: (CB_RV // 2 + p, ntb - 1 - t, 0)),
                  pl.BlockSpec((tb, LANE), lambda p, t: (ntb - 1 - t, 0)),
                  pl.BlockSpec((tb, LANE), lambda p, t: (ntb - 1 - t, 0)),
                  pl.BlockSpec((2, tb, LANE), lambda p, t: (p, ntb - 1 - t, 0)),
                  pl.BlockSpec((2, ncb, LANE, LANE), lambda p, t: (p, ntb - 1 - t, 0, 0)),
                  any_spec],
        out_specs=any_spec,
        out_shape=jax.ShapeDtypeStruct(dpb.shape, dpb.dtype),
        scratch_shapes=[pltpu.VMEM((2, tb, LANE), dpb.dtype), pltpu.VMEM((2, tb, LANE), dpb.dtype),
                        pltpu.VMEM((2, 2, tb, LANE), dpb.dtype), pltpu.VMEM((2, LANE, LANE), F32),
                        pltpu.SemaphoreType.DMA((2, 3))],
        input_output_aliases={7: 0},
        compiler_params=_params(("arbitrary", "arbitrary")),
    )(pb, pb, pb, cos_t, sin_t, do, rsave, dpb)


def _row_tile(t_len, want):
    return min(want, t_len)


PAIR_CB = 2 * CB_PER_SHARD


def proj_forward(h, wt, chips, t_len, name, pb=None):
    tm = _row_tile(t_len, 1024)

    def body(chips_ref, h_ref, w_ref, *rest):
        del chips_ref
        o_ref = rest[-1]
        acc = _dot(h_ref[...], w_ref[...], _NT)
        for jj in range(PAIR_CB):
            o_ref[jj] = acc[:, jj * LANE:(jj + 1) * LANE]

    given = [] if pb is None else [pb]
    return pl.pallas_call(
        body, name=name,
        grid_spec=pltpu.PrefetchScalarGridSpec(
            num_scalar_prefetch=1, grid=(chips.shape[0], t_len // tm),
            in_specs=[pl.BlockSpec((tm, D_MODEL), lambda j, i, ch: (i, 0)),
                      pl.BlockSpec((PAIR_CB * LANE, D_MODEL), lambda j, i, ch: (ch[j], 0))]
            + [pl.BlockSpec(memory_space=pl.ANY)] * len(given),
            out_specs=pl.BlockSpec((PAIR_CB, tm, LANE), lambda j, i, ch: (ch[j], i, 0))),
        out_shape=jax.ShapeDtypeStruct((N_CB, t_len, LANE), F32),
        input_output_aliases={3: 0} if given else {},
        compiler_params=_params(("arbitrary", "arbitrary")),
    )(chips, h, wt, *given)


def proj_backward_input(dpb, wt, token, x, dy, norm_g, scale1p, t_len):
    tm = _row_tile(t_len, 512)

    def body(a_ref, wt_hbm, token_ref, x_ref, dy_ref, g_ref, sc_ref, gx_ref, vec_ref, w_ref, sem):
        del token_ref
        i = pl.program_id(0)

        @pl.when(i == 0)
        def _():
            cp = pltpu.make_async_copy(wt_hbm, w_ref, sem)
            cp.start()
            cp.wait()

        a = jnp.concatenate([a_ref[jj].astype(_BF) for jj in range(N_CB)], axis=1)
        dhv = _dot(a, w_ref[...])
        xv, g, sc = x_ref[...], g_ref[...], sc_ref[...]
        r = lax.rsqrt(jnp.mean(xv * xv, axis=-1, keepdims=True) + EPS)
        xn = xv * r
        dxn = dhv * (g * sc)
        gx_ref[...] = dy_ref[...] + r * dxn - xn * (r * r) * jnp.mean(xv * dxn, axis=-1, keepdims=True)
        t = dhv * xn
        _acc_rows(vec_ref, i, [jnp.sum(t * sc, axis=0, keepdims=True),
                               jnp.sum(t * g, axis=0, keepdims=True),
                               jnp.sum(dhv, axis=0, keepdims=True)])

    row = pl.BlockSpec((tm, D_MODEL), lambda i: (i, 0))
    return pl.pallas_call(
        body, name="proj_bwd_input", grid=(t_len // tm,),
        in_specs=[pl.BlockSpec((N_CB, tm, LANE), lambda i: (0, i, 0)),
                  pl.BlockSpec(memory_space=pl.ANY),
                  pl.BlockSpec(token.shape, lambda i: (0, 0)),
                  row, row, _vec_spec(), _vec_spec()],
        out_specs=[row, pl.BlockSpec((8, D_MODEL), lambda i: (0, 0))],
        out_shape=[jax.ShapeDtypeStruct((t_len, D_MODEL), F32), jax.ShapeDtypeStruct((8, D_MODEL), F32)],
        scratch_shapes=[pltpu.VMEM(wt.shape, wt.dtype), pltpu.SemaphoreType.DMA],
        compiler_params=_params(("arbitrary",)),
    )(dpb, wt, token, x, dy, norm_g, scale1p)


def proj_backward_weight(h_t, dpb, t_len):
    tk = _row_tile(t_len, 2048)

    def body(h_ref, b_ref, o_ref):
        k = pl.program_id(1)
        b = jnp.concatenate([b_ref[jj].astype(_BF) for jj in range(PAIR_CB)], axis=1)
        part = _dot(h_ref[...], b)

        @pl.when(k == 0)
        def _():
            for s in range(2):
                o_ref[s] = part[:, s * SHARD_IN:(s + 1) * SHARD_IN]

        @pl.when(k > 0)
        def _():
            for s in range(2):
                o_ref[s] = o_ref[s] + part[:, s * SHARD_IN:(s + 1) * SHARD_IN]

    return pl.pallas_call(
        body, name="proj_bwd_weight", grid=(N_DEV // 2, t_len // tk),
        in_specs=[pl.BlockSpec((D_MODEL, tk), lambda j, k: (0, k)),
                  pl.BlockSpec((PAIR_CB, tk, LANE), lambda j, k: (j, k, 0))],
        out_specs=pl.BlockSpec((2, D_MODEL, SHARD_IN), lambda j, k: (j, 0, 0)),
        out_shape=jax.ShapeDtypeStruct((N_DEV, D_MODEL, SHARD_IN), F32),
        compiler_params=_params(("arbitrary", "arbitrary")),
    )(h_t, dpb)


def sibling_blocks(g_in, c_idx):
    tr = D_MODEL

    def body(c_ref, g_ref, o_ref):
        del c_ref
        o_ref[...] = g_ref[...].astype(o_ref.dtype)

    return pl.pallas_call(
        body, name="sibling_blocks",
        grid_spec=pltpu.PrefetchScalarGridSpec(
            num_scalar_prefetch=1, grid=(N_DEV // 2, D_MODEL // tr),
            in_specs=[pl.BlockSpec((None, tr, SHARD_IN), lambda q, i, c: (2 * q + 1 - c[0], i, 0))],
            out_specs=pl.BlockSpec((None, tr, SHARD_IN), lambda q, i, c: (q, i, 0))),
        out_shape=jax.ShapeDtypeStruct((N_DEV // 2, D_MODEL, SHARD_IN), _BF),
        compiler_params=_params(("arbitrary", "arbitrary")),
    )(c_idx, g_in)


def _vec_spec():
    return pl.BlockSpec((1, D_MODEL), lambda i: (0, 0))


def _acc_rows(ref, i, rows):
    @pl.when(i == 0)
    def _():
        ref[...] = jnp.zeros_like(ref)

    for n, row in enumerate(rows):
        ref[n:n + 1, :] = ref[n:n + 1, :] + row


def adaln_forward(x, norm_g, scale1p, shift, t_len):
    tm = _row_tile(t_len, 1024)

    def body(x_ref, g_ref, sc_ref, sh_ref, h_ref, ht_ref):
        xv = x_ref[...]
        r = lax.rsqrt(jnp.mean(xv * xv, axis=-1, keepdims=True) + EPS)
        h = xv * r * g_ref[...] * sc_ref[...] + sh_ref[...]
        h_ref[...] = h.astype(h_ref.dtype)
        ht_ref[...] = h.T.astype(ht_ref.dtype)

    return pl.pallas_call(
        body, name="adaln_fwd", grid=(t_len // tm,),
        in_specs=[pl.BlockSpec((tm, D_MODEL), lambda i: (i, 0)), _vec_spec(), _vec_spec(), _vec_spec()],
        out_specs=[pl.BlockSpec((tm, D_MODEL), lambda i: (i, 0)), pl.BlockSpec((D_MODEL, tm), lambda i: (0, i))],
        out_shape=[jax.ShapeDtypeStruct((t_len, D_MODEL), _BF), jax.ShapeDtypeStruct((D_MODEL, t_len), _BF)],
        compiler_params=_params(("arbitrary",)),
    )(x, norm_g, scale1p, shift)


def _head_norm(o, g):
    r = lax.rsqrt(jnp.mean(o * o, axis=-1, keepdims=True) + EPS)
    return r, o * r * g


def _group_spec(tm, cb):
    return pl.BlockSpec((N_HEADS, tm, LANE), lambda i, cb=cb: (cb // N_HEADS, i, 0))


MID_FINAL_G, MID_GATE, MID_LOSS, MID_HG_G, MID_RET_G = range(5)


def middle(x, target, oa, ob, pb, wout, gate, final_g, hg_g, ret_g, t_len):
    tm = _row_tile(t_len, 256)
    n_steps = t_len // tm

    def body(x_ref, t_ref, oa_ref, ob_ref, hz_ref, rz_ref, ga_ref, gb_ref, w_ref, gate_ref, fg_ref, hg_ref, rg_ref,
             dy_ref, doa_ref, dob_ref, dw_ref, vec_ref, dpb_ref, m_scr, dm_scr, keep, bufs, sems):
        i = pl.program_id(0)
        rows = pl.ds(pl.multiple_of(i * tm, tm), tm)

        def group_copies(sl):
            return [pltpu.make_async_copy(bufs.at[sl, n], dpb_ref.at[pl.ds(cb, N_HEADS), rows], sems.at[sl, n])
                    for n, cb in enumerate((CB_HZ, CB_RZ, CB_GA, CB_GB))]
        sides = ((oa_ref, hz_ref, ga_ref, hg_ref, doa_ref), (ob_ref, rz_ref, gb_ref, rg_ref, dob_ref))
        for hh in range(N_HEADS):
            ls = slice(hh * LANE, (hh + 1) * LANE)
            acc = None
            for side, (o_ref, z_ref, gt_ref, g_ref, _) in enumerate(sides):
                o = o_ref[hh]
                rr = lax.rsqrt(jnp.mean(o * o, axis=-1, keepdims=True) + EPS)
                orr = o * rr
                zz = z_ref[hh]
                sz = _sigmoid(zz)
                sgt = _sigmoid(gt_ref[hh])
                keep[side, hh, 0] = orr
                keep[side, hh, 1] = sz
                keep[side, hh, 2] = sgt
                keep[side, hh, 3] = jnp.broadcast_to(rr, orr.shape)
                u = sgt * ((orr * g_ref[:, ls]) * (zz * sz))
                acc = u if acc is None else acc + u
            m_scr[:, ls] = acc.astype(m_scr.dtype)
        zv = _dot(m_scr[...], w_ref[...])
        gt, fg = gate_ref[...], fg_ref[...]
        y = x_ref[...] + gt * zv
        r = lax.rsqrt(jnp.mean(y * y, axis=-1, keepdims=True) + EPS)
        yn = y * r
        err = yn * fg - t_ref[...]
        loss = 0.5 * jnp.sum(jnp.mean(err * err, axis=-1, keepdims=True), axis=0, keepdims=True)
        dout = err * (1.0 / D_MODEL)
        gd = dout * fg
        dy = r * gd - yn * (r * r) * jnp.mean(y * gd, axis=-1, keepdims=True)
        dy_ref[...] = dy
        dz = (dy * gt).astype(_BF)
        dm_scr[...] = _dot(dz, w_ref[...], _NT)
        part = _dot(m_scr[...], dz, _TN)

        @pl.when(i == 0)
        def _():
            dw_ref[...] = part

        @pl.when(i > 0)
        def _():
            dw_ref[...] = dw_ref[...] + part

        slot = _writeback_reserve(i, group_copies)
        dg = [[], []]
        for hh in range(N_HEADS):
            ls = slice(hh * LANE, (hh + 1) * LANE)
            dmh = dm_scr[:, ls]
            for side, (o_ref, z_ref, gt_ref, g_ref, do_ref) in enumerate(sides):
                zz, g = z_ref[hh], g_ref[:, ls]
                orr, sz, sgt, rr = keep[side, hh, 0], keep[side, hh, 1], keep[side, hh, 2], keep[side, hh, 3]
                n = orr * g
                silu = zz * sz
                du = dmh * sgt
                bufs[slot, 2 + side, hh] = (dmh * (n * silu) * (sgt * (1.0 - sgt))).astype(bufs.dtype)
                bufs[slot, side, hh] = (du * n * (sz * (1.0 + zz * (1.0 - sz)))).astype(bufs.dtype)
                dn = du * silu
                dg[side].append(jnp.sum(dn * orr, axis=0, keepdims=True))
                gdn = dn * g
                do_ref[hh] = (rr * (gdn - orr * jnp.mean(orr * gdn, axis=-1, keepdims=True))).astype(do_ref.dtype)
        _acc_rows(vec_ref, i, [jnp.sum(dout * yn, axis=0, keepdims=True),
                               jnp.sum(dy * zv, axis=0, keepdims=True),
                               jnp.broadcast_to(loss, (1, D_MODEL)),
                               jnp.concatenate(dg[0], axis=1), jnp.concatenate(dg[1], axis=1)])
        _writeback_commit(i, n_steps, slot, group_copies)

    row = pl.BlockSpec((tm, D_MODEL), lambda i: (i, 0))
    head = pl.BlockSpec((N_HEADS, tm, LANE), lambda i: (0, i, 0))
    full = pl.BlockSpec((D_MODEL, D_MODEL), lambda i: (0, 0))
    return pl.pallas_call(
        body, name="middle", grid=(n_steps,),
        in_specs=[row, row, head, head, _group_spec(tm, CB_HZ), _group_spec(tm, CB_RZ), _group_spec(tm, CB_GA),
                  _group_spec(tm, CB_GB), full, _vec_spec(), _vec_spec(), _vec_spec(), _vec_spec()],
        out_specs=[row, head, head, full, pl.BlockSpec((8, D_MODEL), lambda i: (0, 0)),
                   pl.BlockSpec(memory_space=pl.ANY)],
        out_shape=[jax.ShapeDtypeStruct((t_len, D_MODEL), F32),
                   jax.ShapeDtypeStruct((N_HEADS, t_len, LANE), _BF),
                   jax.ShapeDtypeStruct((N_HEADS, t_len, LANE), _BF),
                   jax.ShapeDtypeStruct((D_MODEL, D_MODEL), F32),
                   jax.ShapeDtypeStruct((8, D_MODEL), F32),
                   jax.ShapeDtypeStruct((N_CB, t_len, LANE), _BF)],
        scratch_shapes=[pltpu.VMEM((tm, D_MODEL), _BF), pltpu.VMEM((tm, D_MODEL), F32),
                        pltpu.VMEM((2, N_HEADS, 4, tm, LANE), F32),
                        pltpu.VMEM((2, 4, N_HEADS, tm, LANE), _BF), pltpu.SemaphoreType.DMA((2, 4))],
        compiler_params=_params(("arbitrary",)),
    )(x, target, oa, ob, pb, pb, pb, pb, wout, gate, final_g, hg_g, ret_g)


def device_step(x, target, mod, lb, project, norm_g, hg_g, ret_g, final_g, c_idx=None, start_exchange=None):
    t_len = x.shape[0]
    shift, scale, gate = mod[:, :D_MODEL], mod[:, D_MODEL:2 * D_MODEL], mod[:, 2 * D_MODEL:]
    scale1p = 1.0 + scale
    cos_t, sin_t = _rope_tables(t_len)
    h, h_t = adaln_forward(x, norm_g, scale1p, shift, t_len)
    pb, wt, wout = project(h)
    oa, ssave, asave = hgrn_forward(pb, lb, t_len)
    ob, rsave = retention_forward(pb, cos_t, sin_t, t_len)
    dy, doa, dob, dwout, vec_mid, dpb = middle(x, target, oa, ob, pb, wout, gate, final_g, hg_g, ret_g, t_len)
    dpb, dlb = hgrn_backward(pb, lb, doa, ssave, asave, dpb, t_len)
    dpb = retention_backward(pb, cos_t, sin_t, dob, rsave, dpb, t_len)
    c_idx = jnp.zeros((1,), jnp.int32) if c_idx is None else c_idx
    dwin = proj_backward_weight(h_t, dpb, t_len)
    dwin_sib = sibling_blocks(dwin, c_idx)
    token, pending = (start_exchange(dwin, dwin_sib, dwout) if start_exchange
                      else (jnp.zeros((8, LANE), F32), None))
    grad_x, vec_ada = proj_backward_input(dpb, wt, token, x, dy, norm_g, scale1p, t_len)
    return grad_x, dwin, dwout, vec_mid, vec_ada, dlb, pending


PACK_ROWS = 16
ROW_NORM_G, ROW_LB, ROW_HG_G, ROW_RET_G, ROW_FINAL_G, ROW_SHIFT, ROW_SCALE, ROW_GATE, ROW_LOSS = range(9)


def _mesh_pos():
    return lax.axis_index("x"), lax.axis_index("y"), lax.axis_index("c")


def _lin(pos):
    return 4 * pos[0] + 2 * pos[1] + pos[2]


def _xor_peer(pos, k):
    return tuple(1 - p if (k >> s) & 1 else p for p, s in zip(pos, (2, 1, 0)))


def _other_chips(pos):
    x, y, _ = pos
    return [(1 - x, y), (x, 1 - y), (1 - x, 1 - y)]


def _remote(src, dst, send_sem, recv_sem, to):
    return pltpu.make_async_remote_copy(src_ref=src, dst_ref=dst, send_sem=send_sem, recv_sem=recv_sem,
                                        device_id=to, device_id_type=MESH)


def pre_exchange(c, w_ada, b_ada, logits, w_in, w_out):
    def body(c_ref, wada_ref, bada_ref, logit_ref, win_ref, wout_ref, mod_ref, scall_ref, lb_ref, wt_ref, wo_ref,
             cg_ref, modall_ref, parts_ref, send1, recv1, send2, recv2):
        pos = _mesh_pos()
        cv = c_ref[...]
        slot = lambda p: pl.ds(pl.multiple_of(8 * _lin(p), 8), 8)
        cg_ref[slot(pos), :] = jnp.broadcast_to(cv * _sigmoid(cv), (8, D_MODEL))
        lb_ref[...] = _sigmoid(logit_ref[0:1, :] - logit_ref[1:2, :])
        peers = [_xor_peer(pos, k) for k in range(1, N_DEV)]
        gather = [_remote(cg_ref.at[slot(pos)], cg_ref.at[slot(pos)], send1.at[n], recv1.at[n], p)
                  for n, p in enumerate(peers)]
        for cp in gather:
            cp.start()
        wt_ref[...] = win_ref[...].T.astype(wt_ref.dtype)
        wo_ref[...] = wout_ref[...].astype(wo_ref.dtype)
        for n, p in enumerate(peers):
            _remote(cg_ref.at[slot(p)], cg_ref.at[slot(p)], send1.at[n], recv1.at[n], p).wait_recv()
        modall_ref[...] = _dot(cg_ref[...], wada_ref[...])
        scatter = [_remote(modall_ref.at[slot(p)], parts_ref.at[slot(pos)], send2.at[n], recv2.at[n], p)
                   for n, p in enumerate(peers)]
        for cp in scatter:
            cp.start()
        parts_ref[slot(pos), :] = modall_ref[slot(pos), :]
        for n, p in enumerate(peers):
            _remote(modall_ref.at[slot(p)], parts_ref.at[slot(p)], send2.at[n], recv2.at[n], p).wait_recv()
        for cp in gather + scatter:
            cp.wait_send()
        for j in range(N_DEV):
            cols = slice(j * SHARD_ADA, (j + 1) * SHARD_ADA)
            mod_ref[:, cols] = parts_ref[8 * j:8 * j + 1, :] + bada_ref[:, cols]
            scall_ref[j:j + 1, :] = cg_ref[8 * j:8 * j + 1, :]

    vmem = pl.BlockSpec(memory_space=pltpu.VMEM)
    return pl.pallas_call(
        body, name="pre_exchange",
        in_specs=[vmem] * 6, out_specs=[vmem] * 5,
        out_shape=[jax.ShapeDtypeStruct((1, 3 * D_MODEL), F32), jax.ShapeDtypeStruct((N_DEV, D_MODEL), F32),
                   jax.ShapeDtypeStruct((1, D_MODEL), F32),
                   jax.ShapeDtypeStruct(w_in.shape[::-1], _BF), jax.ShapeDtypeStruct(w_out.shape, _BF)],
        scratch_shapes=[pltpu.VMEM((N_DEV * 8, D_MODEL), F32), pltpu.VMEM((N_DEV * 8, SHARD_ADA), F32),
                        pltpu.VMEM((N_DEV * 8, SHARD_ADA), F32)] + [pltpu.SemaphoreType.DMA((N_DEV - 1,))] * 4,
        compiler_params=pltpu.CompilerParams(vmem_limit_bytes=VMEM_LIMIT),
    )(c, w_ada, b_ada, logits, w_in, w_out)


def _gather_copies(srcs, outs, sems):
    pos = _mesh_pos()
    x, y, c = pos
    sibling = (x, y, 1 - c)

    def route(core):
        return [(x + (1 - core) * (1 - 2 * x), y + core * (1 - 2 * y)),
                (x + core * (1 - 2 * x), y + (1 - core) * (1 - 2 * y)),
                (1 - x, 1 - y)]

    mine = [(*chip, c) for chip in route(c)]
    carried = [pos, pos, pos, mine[0], mine[0], mine[1], mine[2]]
    to = [sibling, mine[0], mine[1], mine[1], sibling, sibling, sibling]
    landed = [sibling] + mine + [(*chip, 1 - c) for chip in route(1 - c)]

    def sent(a, k):
        dst = outs[a].at[_lin(carried[k])]
        return _remote(srcs[a] if k < 3 else dst, dst, *sems(a, k), to[k])

    def arrival(a, k):
        dst = outs[a].at[_lin(landed[k])]
        return _remote(dst, dst, *sems(a, k), pos)

    return sent, arrival


_ROUNDS = (range(0, 3), range(3, 6), range(6, 7))


def _round_sems(send, recv, rnd):
    n = len(_ROUNDS[rnd])
    return lambda a, k: (send.at[n * a + k - _ROUNDS[rnd][0]], recv.at[n * a + k - _ROUNDS[rnd][0]])


def weight_gather_start(win_sh, wout_sh):
    def body(win_ref, wout_ref, wg_in, woutg_in, send, recv, wg_ref, woutg_ref, token, local):
        del wg_in, woutg_in
        sent, _ = _gather_copies((win_ref, wout_ref), (wg_ref, woutg_ref), _round_sems(send, recv, 0))
        first = [sent(a, k) for a in range(2) for k in _ROUNDS[0]]
        me = _lin(_mesh_pos())
        mine = [pltpu.make_async_copy(win_ref, wg_ref.at[me], local.at[0]),
                pltpu.make_async_copy(wout_ref, woutg_ref.at[me], local.at[1])]
        for cp in mine + first:
            cp.start()
        for cp in mine:
            cp.wait()
        token[...] = jnp.zeros_like(token)

    hbm = lambda a: pltpu.with_memory_space_constraint(a, pltpu.HBM)
    wg = lax.empty((N_DEV,) + win_sh.shape, win_sh.dtype)
    woutg = lax.empty((N_DEV,) + wout_sh.shape, wout_sh.dtype)
    return pl.pallas_call(
        body, name="weight_gather_start",
        in_specs=[_HBM] * 4,
        out_specs=[_SEM, _SEM, _HBM, _HBM, pl.BlockSpec(memory_space=pltpu.VMEM)],
        out_shape=[pltpu.SemaphoreType.DMA((2 * len(_ROUNDS[0]),)), pltpu.SemaphoreType.DMA((2 * len(_ROUNDS[0]),)),
                   pltpu.HBM(wg.shape, wg.dtype), pltpu.HBM(woutg.shape, woutg.dtype),
                   jax.ShapeDtypeStruct((8, LANE), F32)],
        scratch_shapes=[pltpu.SemaphoreType.DMA((2,))],
        input_output_aliases={2: 2, 3: 3},
        compiler_params=pltpu.CompilerParams(has_side_effects=pltpu.SideEffectType.DATAFLOW_SIDE_EFFECTING),
    )(hbm(win_sh), hbm(wout_sh), hbm(wg), hbm(woutg))


def _wait_round(sent, arrival, rnd):
    for a in range(2):
        for k in _ROUNDS[rnd]:
            arrival(a, k).wait_recv()
            sent(a, k).wait_send()


def weight_gather_wait(send, recv, win_sh, wout_sh, wg, woutg, after):
    def body(win_ref, wout_ref, wg_ref, woutg_ref, send, recv, after_ref, wg_out, woutg_out):
        del after_ref, wg_out, woutg_out
        _wait_round(*_gather_copies((win_ref, wout_ref), (wg_ref, woutg_ref), _round_sems(send, recv, 0)), 0)

    hbm = lambda a: pltpu.with_memory_space_constraint(a, pltpu.HBM)
    return pl.pallas_call(
        body, name="weight_gather_wait",
        in_specs=[_HBM] * 4 + [_SEM, _SEM, pl.BlockSpec(memory_space=pl.ANY)],
        out_specs=[_HBM] * 2,
        out_shape=[pltpu.HBM(wg.shape, wg.dtype), pltpu.HBM(woutg.shape, woutg.dtype)],
        input_output_aliases={2: 0, 3: 1},
        compiler_params=pltpu.CompilerParams(has_side_effects=pltpu.SideEffectType.DATAFLOW_SIDE_EFFECTING),
    )(hbm(win_sh), hbm(wout_sh), wg, woutg, send, recv, after)


def weight_pass_start(wg, woutg):
    def body(wg_ref, woutg_ref, send, recv, wg_out, woutg_out):
        del wg_out, woutg_out
        sent, _ = _gather_copies(None, (wg_ref, woutg_ref), _round_sems(send, recv, 1))
        for a in range(2):
            for k in _ROUNDS[1]:
                sent(a, k).start()

    n = 2 * len(_ROUNDS[1])
    return pl.pallas_call(
        body, name="weight_pass_start",
        in_specs=[_HBM] * 2,
        out_specs=[_SEM, _SEM, _HBM, _HBM],
        out_shape=[pltpu.SemaphoreType.DMA((n,)), pltpu.SemaphoreType.DMA((n,)),
                   pltpu.HBM(wg.shape, wg.dtype), pltpu.HBM(woutg.shape, woutg.dtype)],
        input_output_aliases={0: 2, 1: 3},
        compiler_params=pltpu.CompilerParams(has_side_effects=pltpu.SideEffectType.DATAFLOW_SIDE_EFFECTING),
    )(wg, woutg)


def weight_pass_wait(send, recv, wg, woutg, after):
    def body(wg_ref, woutg_ref, send, recv, after_ref, wg_out, woutg_out):
        del after_ref, wg_out, woutg_out
        _wait_round(*_gather_copies(None, (wg_ref, woutg_ref), _round_sems(send, recv, 1)), 1)

    return pl.pallas_call(
        body, name="weight_pass_wait",
        in_specs=[_HBM] * 2 + [_SEM, _SEM, pl.BlockSpec(memory_space=pl.ANY)],
        out_specs=[_HBM] * 2,
        out_shape=[pltpu.HBM(wg.shape, wg.dtype), pltpu.HBM(woutg.shape, woutg.dtype)],
        input_output_aliases={0: 0, 1: 1},
        compiler_params=pltpu.CompilerParams(has_side_effects=pltpu.SideEffectType.DATAFLOW_SIDE_EFFECTING),
    )(wg, woutg, send, recv, after)


def weight_pass_last(wg, woutg):
    def body(wg_in, woutg_in, wg_ref, woutg_ref, send, recv):
        del wg_in, woutg_in
        sent, arrival = _gather_copies(None, (wg_ref, woutg_ref), _round_sems(send, recv, 2))
        for a in range(2):
            sent(a, _ROUNDS[2][0]).start()
        _wait_round(sent, arrival, 2)

    any_spec = pl.BlockSpec(memory_space=pl.ANY)
    n = 2 * len(_ROUNDS[2])
    return pl.pallas_call(
        body, name="weight_pass_last",
        in_specs=[any_spec, any_spec], out_specs=[any_spec, any_spec],
        out_shape=[jax.ShapeDtypeStruct(wg.shape, wg.dtype), jax.ShapeDtypeStruct(woutg.shape, woutg.dtype)],
        scratch_shapes=[pltpu.SemaphoreType.DMA((n,)), pltpu.SemaphoreType.DMA((n,))],
        input_output_aliases={0: 0, 1: 1},
    )(wg, woutg)


def grad_pair_exchange(g_sib, g_out):
    def body(gsib_ref, gout_ref, ra_ref, rb_ref, send, recv):
        pos = _mesh_pos()
        x, y, c = pos
        sibling = (x, y, 1 - c)
        copies = []
        for q in range(4):
            copies.append(_remote(gsib_ref.at[q], ra_ref.at[q], send.at[q], recv.at[q], sibling))
            copies.append(_remote(gout_ref.at[2 * q + (1 - c)], rb_ref.at[q], send.at[4 + q], recv.at[4 + q], sibling))
        for cp in copies:
            cp.start()
        for cp in copies:
            cp.wait_recv()
        for cp in copies:
            cp.wait_send()

    any_spec = pl.BlockSpec(memory_space=pl.ANY)
    return pl.pallas_call(
        body, name="grad_pair_exchange",
        in_specs=[any_spec, any_spec], out_specs=[any_spec, any_spec],
        out_shape=[jax.ShapeDtypeStruct(g_sib.shape, g_sib.dtype), jax.ShapeDtypeStruct((4,) + g_out.shape[1:], F32)],
        scratch_shapes=[pltpu.SemaphoreType.DMA((8,)), pltpu.SemaphoreType.DMA((8,))],
    )(g_sib, g_out)


def pair_sum(g_in, ra, g_out, rb, c_idx):
    tr = D_MODEL

    def body(c_ref, gin_ref, ra_ref, gout_ref, rb_ref, sb_ref, sbo_ref):
        del c_ref
        sb_ref[...] = (gin_ref[...] + ra_ref[...].astype(F32)).astype(sb_ref.dtype)
        sbo_ref[...] = gout_ref[...] + rb_ref[...]

    n_i = D_MODEL // tr
    return pl.pallas_call(
        body, name="pair_sum",
        grid_spec=pltpu.PrefetchScalarGridSpec(
            num_scalar_prefetch=1, grid=(4, n_i),
            in_specs=[pl.BlockSpec((None, tr, SHARD_IN), lambda q, i, c: (2 * q + c[0], i, 0)),
                      pl.BlockSpec((None, tr, SHARD_IN), lambda q, i, c: (q, i, 0)),
                      pl.BlockSpec((None, SHARD_OUT // n_i, D_MODEL), lambda q, i, c: (2 * q + c[0], i, 0)),
                      pl.BlockSpec((None, SHARD_OUT // n_i, D_MODEL), lambda q, i, c: (q, i, 0))],
            out_specs=[pl.BlockSpec((None, tr, SHARD_IN), lambda q, i, c: (q, i, 0)),
                       pl.BlockSpec((None, SHARD_OUT // n_i, D_MODEL), lambda q, i, c: (q, i, 0))]),
        out_shape=[jax.ShapeDtypeStruct(ra.shape, _BF), jax.ShapeDtypeStruct(rb.shape, F32)],
        compiler_params=_params(("arbitrary", "arbitrary")),
    )(c_idx, g_in, ra, g_out, rb)


_HBM = pl.BlockSpec(memory_space=pltpu.HBM)
_SEM = pl.BlockSpec(memory_space=pltpu.SEMAPHORE)
_N_CHIP_COPIES = 6


def _chip_copies(sb_ref, sbo_ref, rc_ref, rco_ref, send, recv):
    pos = _mesh_pos()
    copies = []
    for a, (src, dst) in enumerate(((sb_ref, rc_ref), (sbo_ref, rco_ref))):
        for j, chip in enumerate(_other_chips(pos)):
            copies.append(_remote(src.at[2 * chip[0] + chip[1]], dst.at[j], send.at[3 * a + j], recv.at[3 * a + j],
                                  (*chip, pos[2])))
    return copies


def grad_chip_start(sb, sbo):
    def body(sb_ref, sbo_ref, rc_ref, rco_ref, send, recv, sb_thru, sbo_thru, rc_thru, rco_thru, token):
        del sb_thru, sbo_thru, rc_thru, rco_thru
        for cp in _chip_copies(sb_ref, sbo_ref, rc_ref, rco_ref, send, recv):
            cp.start()
        token[...] = jnp.zeros_like(token)

    hbm = lambda a: pltpu.with_memory_space_constraint(a, pltpu.HBM)
    rc = lax.empty((3,) + sb.shape[1:], sb.dtype)
    rco = lax.empty((3,) + sbo.shape[1:], sbo.dtype)
    return pl.pallas_call(
        body, name="grad_chip_start",
        in_specs=[_HBM] * 4,
        out_specs=[_SEM, _SEM, _HBM, _HBM, _HBM, _HBM, pl.BlockSpec(memory_space=pltpu.VMEM)],
        out_shape=[pltpu.SemaphoreType.DMA((_N_CHIP_COPIES,)), pltpu.SemaphoreType.DMA((_N_CHIP_COPIES,)),
                   pltpu.HBM(sb.shape, sb.dtype), pltpu.HBM(sbo.shape, sbo.dtype),
                   pltpu.HBM(rc.shape, rc.dtype), pltpu.HBM(rco.shape, rco.dtype),
                   jax.ShapeDtypeStruct((8, LANE), F32)],
        input_output_aliases={0: 2, 1: 3, 2: 4, 3: 5},
        compiler_params=pltpu.CompilerParams(has_side_effects=pltpu.SideEffectType.DATAFLOW_SIDE_EFFECTING),
    )(hbm(sb), hbm(sbo), hbm(rc), hbm(rco))


def grad_chip_wait(send, recv, sb, sbo, rc, rco, after):
    def body(sb_ref, sbo_ref, rc_ref, rco_ref, send, recv, after_ref, sb_o, sbo_o, rc_o, rco_o):
        del after_ref, sb_o, sbo_o, rc_o, rco_o
        for cp in _chip_copies(sb_ref, sbo_ref, rc_ref, rco_ref, send, recv):
            cp.wait_send()
            cp.wait_recv()

    return pl.pallas_call(
        body, name="grad_chip_wait",
        in_specs=[_HBM] * 4 + [_SEM, _SEM, pl.BlockSpec(memory_space=pl.ANY)],
        out_specs=[_HBM] * 4,
        out_shape=[pltpu.HBM(sb.shape, sb.dtype), pltpu.HBM(sbo.shape, sbo.dtype),
                   pltpu.HBM(rc.shape, rc.dtype), pltpu.HBM(rco.shape, rco.dtype)],
        input_output_aliases={0: 0, 1: 1, 2: 2, 3: 3},
        compiler_params=pltpu.CompilerParams(has_side_effects=pltpu.SideEffectType.DATAFLOW_SIDE_EFFECTING),
    )(sb, sbo, rc, rco, send, recv, after)


def pack_gather(pack):
    def body(pack_ref, packs_ref, psend, precv):
        pos = _mesh_pos()
        me = _lin(pos)
        packs_ref[me] = pack_ref[...]
        peers = [_xor_peer(pos, k) for k in range(1, N_DEV)]
        gather = [_remote(packs_ref.at[me], packs_ref.at[me], psend.at[n], precv.at[n], p) for n, p in enumerate(peers)]
        for cp in gather:
            cp.start()
        for n, p in enumerate(peers):
            _remote(packs_ref.at[_lin(p)], packs_ref.at[_lin(p)], psend.at[n], precv.at[n], p).wait_recv()
        for cp in gather:
            cp.wait_send()

    vmem = pl.BlockSpec(memory_space=pltpu.VMEM)
    return pl.pallas_call(
        body, name="pack_gather", in_specs=[vmem], out_specs=vmem,
        out_shape=jax.ShapeDtypeStruct((N_DEV,) + pack.shape, F32),
        scratch_shapes=[pltpu.SemaphoreType.DMA((N_DEV - 1,)), pltpu.SemaphoreType.DMA((N_DEV - 1,))],
    )(pack)


def pack_rows(vec_mid, vec_ada, dlb):
    def body(mid_ref, ada_ref, dlb_ref, o_ref):
        mid = lambda r: mid_ref[r:r + 1, :]
        rows = [ada_ref[0:1, :], dlb_ref[...], mid(MID_HG_G), mid(MID_RET_G), mid(MID_FINAL_G),
                ada_ref[2:3, :], ada_ref[1:2, :], mid(MID_GATE), mid(MID_LOSS)]
        o_ref[...] = jnp.zeros_like(o_ref)
        for n, row in enumerate(rows):
            o_ref[n:n + 1, :] = row

    vmem = pl.BlockSpec(memory_space=pltpu.VMEM)
    return pl.pallas_call(body, name="pack_rows", in_specs=[vmem] * 3, out_specs=vmem,
                          out_shape=jax.ShapeDtypeStruct((PACK_ROWS, D_MODEL), F32))(vec_mid, vec_ada, dlb)


def _adamw(w, g, m, v):
    m = ADAM_B1 * m + (1.0 - ADAM_B1) * g
    v = ADAM_B2 * v + (1.0 - ADAM_B2) * (g * g)
    m_hat = m / (1.0 - ADAM_B1 ** ADAM_STEP)
    v_hat = v / (1.0 - ADAM_B2 ** ADAM_STEP)
    delta = -ADAM_LR * (m_hat / (jnp.sqrt(v_hat) + ADAM_EPS) + ADAM_WD * w)
    return delta, m, v


def adam_shard(chip_idx, own, parts, w, m, v, name):
    rows, cols = w.shape
    tr = min(rows, 256)

    def body(chip_ref, p0, p1, p2, p3, w_ref, m_ref, v_ref, g_ref, d_ref, nm_ref, nv_ref):
        del chip_ref
        g = ((p0[...].astype(F32) + p1[...].astype(F32)) + p2[...].astype(F32)) + p3[...].astype(F32)
        g_ref[...] = g
        d_ref[...], nm_ref[...], nv_ref[...] = _adamw(w_ref[...], g, m_ref[...], v_ref[...])

    part = lambda q: pl.BlockSpec((None, tr, cols), lambda i, chip, q=q: (q, i, 0))
    tile = pl.BlockSpec((tr, cols), lambda i, chip: (i, 0))
    return pl.pallas_call(
        body, name=name,
        grid_spec=pltpu.PrefetchScalarGridSpec(
            num_scalar_prefetch=1, grid=(rows // tr,),
            in_specs=[pl.BlockSpec((None, tr, cols), lambda i, chip: (chip[0], i, 0)), part(0), part(1), part(2),
                      tile, tile, tile],
            out_specs=[tile] * 4),
        out_shape=[jax.ShapeDtypeStruct(w.shape, F32)] * 4,
        compiler_params=_params(("arbitrary",)),
    )(chip_idx, own, parts, parts, parts, w, m, v)


def adam_ada(sc_t, dmod_all, me_idx, w, m, v):
    def body(me_ref, sc_ref, dm_ref, w_ref, m_ref, v_ref, g_ref, d_ref, nm_ref, nv_ref):
        del me_ref
        g = _dot_f32(sc_ref[...], dm_ref[...])
        g_ref[...] = g
        d_ref[...], nm_ref[...], nv_ref[...] = _adamw(w_ref[...], g, m_ref[...], v_ref[...])

    full = pl.BlockSpec(w.shape, lambda i, me: (0, 0))
    return pl.pallas_call(
        body, name="adam_ada",
        grid_spec=pltpu.PrefetchScalarGridSpec(
            num_scalar_prefetch=1, grid=(1,),
            in_specs=[pl.BlockSpec(sc_t.shape, lambda i, me: (0, 0)),
                      pl.BlockSpec((LANE, SHARD_ADA), lambda i, me: (0, me[0])), full, full, full],
            out_specs=[full] * 4),
        out_shape=[jax.ShapeDtypeStruct(w.shape, F32)] * 4,
        compiler_params=_params(("arbitrary",)),
    )(me_idx, sc_t, dmod_all, w, m, v)


def adam_vectors(packs, lb, params, ms, vs):
    n = len(params)

    def body(*refs):
        packs_ref, lb_ref = refs[0], refs[1]
        w_refs, m_refs, v_refs = refs[2:2 + n], refs[2 + n:2 + 2 * n], refs[2 + 2 * n:2 + 3 * n]
        loss_ref = refs[2 + 3 * n]
        outs = refs[3 + 3 * n:3 + 7 * n]
        tot_ref = refs[3 + 7 * n]
        tot = packs_ref[0]
        for d in range(1, N_DEV):
            tot = tot + packs_ref[d]
        tot_ref[...] = tot
        row = lambda r: tot_ref[r:r + 1, :]
        lbv = lb_ref[...]
        dl0 = row(ROW_LB) * lbv * (1.0 - lbv)
        grads = [[row(ROW_NORM_G)],
                 [jnp.concatenate([row(ROW_SHIFT), row(ROW_SCALE), row(ROW_GATE)], axis=1)],
                 [dl0, -dl0],
                 [row(ROW_HG_G)], [row(ROW_RET_G)], [row(ROW_FINAL_G)]]
        loss_ref[...] = tot_ref[ROW_LOSS:ROW_LOSS + 1, 0:LANE]
        for j, g_rows in enumerate(grads):
            for r, g in enumerate(g_rows):
                rs = slice(r, r + 1)
                d, nm, nv = _adamw(w_refs[j][rs, :], g, m_refs[j][rs, :], v_refs[j][rs, :])
                outs[4 * j][rs, :] = g
                outs[4 * j + 1][rs, :] = d
                outs[4 * j + 2][rs, :] = nm
                outs[4 * j + 3][rs, :] = nv

    vmem = pl.BlockSpec(memory_space=pltpu.VMEM)
    out_shape = [jax.ShapeDtypeStruct((1, LANE), F32)]
    for w in params:
        out_shape += [jax.ShapeDtypeStruct(w.shape, F32)] * 4
    return pl.pallas_call(
        body, name="adam_vectors", in_specs=[vmem] * (2 + 3 * n), out_specs=[vmem] * len(out_shape),
        out_shape=out_shape, scratch_shapes=[pltpu.VMEM((PACK_ROWS, D_MODEL), F32)],
    )(packs, lb, *params, *ms, *vs)


def kernel(x, c, norm_g, w_ada, b_ada, w_in, hg_lb_logits, hg_norm_g, ret_norm_g, w_out, final_g, loss_target, m_norm_g, m_w_ada, m_b_ada, m_w_in, m_hg_lb_logits, m_hg_norm_g, m_ret_norm_g, m_w_out, m_final_g, v_norm_g, v_w_ada, v_b_ada, v_w_in, v_hg_lb_logits, v_hg_norm_g, v_ret_norm_g, v_w_out, v_final_g):
    pos = _mesh_pos()
    me_idx = jnp.reshape(_lin(pos), (1,)).astype(jnp.int32)
    c_idx = jnp.reshape(pos[2], (1,)).astype(jnp.int32)
    vec = lambda a: a.reshape(1, D_MODEL)

    mod, scall, lb, wt_sh, wout_sh = pre_exchange(c, w_ada[0], b_ada, hg_lb_logits, w_in[0], w_out[0])
    chip_idx = jnp.reshape(2 * pos[0] + pos[1], (1,)).astype(jnp.int32)
    gather_send, gather_recv, wtg, woutg, gather_token = weight_gather_start(wt_sh, wout_sh)
    mod = mod + gather_token[:1, :1]

    other_chips = jnp.stack([2 * cx + cy for cx, cy in _other_chips(pos)]).astype(jnp.int32)

    def project(h):
        t_len = h.shape[0]
        flat_in = lambda a: a.reshape(D_IN, D_MODEL)
        wg, wog = weight_gather_wait(gather_send, gather_recv, wt_sh, wout_sh, wtg, woutg, h)
        pass_send, pass_recv, wg, wog = weight_pass_start(wg, wog)
        pb = proj_forward(h, flat_in(wg), chip_idx, t_len, "proj_fwd_own")
        wg, wog = weight_pass_wait(pass_send, pass_recv, wg, wog, pb)
        wg, wog = weight_pass_last(wg, wog)
        pb = proj_forward(h, flat_in(wg), other_chips, t_len, "proj_fwd_rest", pb)
        return pb, flat_in(wg), wog.reshape(D_MODEL, D_MODEL)

    def start_exchange(dwin, dwin_sib, dwout):
        dwout = dwout.reshape(N_DEV, SHARD_OUT, D_MODEL)
        ra, rb = grad_pair_exchange(dwin_sib, dwout)
        sb, sbo = pair_sum(dwin, ra, dwout, rb, c_idx)
        send, recv, sb, sbo, rc, rco, token = grad_chip_start(sb, sbo)
        return token, (send, recv, sb, sbo, rc, rco)

    grad_x, _, _, vec_mid, vec_ada, dlb, pending = device_step(
        x[0], loss_target[0], mod, lb, project, norm_g, hg_norm_g, ret_norm_g, vec(final_g), c_idx, start_exchange)
    packs = pack_gather(pack_rows(vec_mid, vec_ada, dlb))
    dmod_all = packs[:, ROW_SHIFT:ROW_GATE + 1, :].reshape(N_DEV, 3 * D_MODEL)
    dmod_all = jnp.pad(dmod_all, ((0, LANE - N_DEV), (0, 0)))
    sc_t = jnp.pad(scall.T, ((0, 0), (0, LANE - N_DEV)))
    g_ada, d_ada, nm_ada, nv_ada = adam_ada(sc_t, dmod_all, me_idx, w_ada[0], m_w_ada[0], v_w_ada[0])
    small = adam_vectors(
        packs, lb,
        (norm_g, b_ada, hg_lb_logits, hg_norm_g, ret_norm_g, vec(final_g)),
        (m_norm_g, m_b_ada, m_hg_lb_logits, m_hg_norm_g, m_ret_norm_g, vec(m_final_g)),
        (v_norm_g, v_b_ada, v_hg_lb_logits, v_hg_norm_g, v_ret_norm_g, vec(v_final_g)))
    loss = small[0][0, 0]
    sb, sbo, rc, rco = grad_chip_wait(*pending, small[0])
    g_in, d_in, nm_in, nv_in = adam_shard(chip_idx, sb, rc, w_in[0], m_w_in[0], v_w_in[0], "adam_w_in")
    g_out, d_out, nm_out, nv_out = adam_shard(chip_idx, sbo, rco, w_out[0], m_w_out[0], v_w_out[0], "adam_w_out")
    (g_ng, d_ng, nm_ng, nv_ng), (g_b, d_b, nm_b, nv_b), (g_lb, d_lb, nm_lb, nv_lb), (g_hg, d_hg, nm_hg, nv_hg), \
        (g_rg, d_rg, nm_rg, nv_rg), (g_fg, d_fg, nm_fg, nv_fg) = [small[1 + 4 * j:5 + 4 * j] for j in range(6)]
    flat = lambda a: a.reshape(D_MODEL)

    def group(ng, ada, b, win, lbl, hg, rg, wo, fg):
        return (ng, ada[None], b, win[None], lbl, hg, rg, wo[None], flat(fg))

    return (loss, grad_x[None],
            *group(g_ng, g_ada, g_b, g_in, g_lb, g_hg, g_rg, g_out, g_fg),
            *group(d_ng, d_ada, d_b, d_in, d_lb, d_hg, d_rg, d_out, d_fg),
            *group(nm_ng, nm_ada, nm_b, nm_in, nm_lb, nm_hg, nm_rg, nm_out, nm_fg),
            *group(nv_ng, nv_ada, nv_b, nv_in, nv_lb, nv_hg, nv_rg, nv_out, nv_fg))
```

```python
import numpy as np
import jax
import jax.numpy as jnp
from jax import lax
from jax.experimental import pallas as pl
from jax.experimental.pallas import tpu as pltpu

F32 = jnp.float32
_BF = jnp.bfloat16

D_MODEL = 1024
N_HEADS = 8
LANE = 128
RET_DK = 64
D_IN = 9216
N_DEV = 8
SHARD_IN = D_IN // N_DEV
SHARD_ADA = 3 * D_MODEL // N_DEV
SHARD_OUT = D_MODEL // N_DEV
N_CB = D_IN // LANE
CB_PER_SHARD = SHARD_IN // LANE
CHUNK = 128
N_LEVELS = 7
EPS = 1e-6
LOG2_E = float(np.log2(np.e))
ROPE_BASE = 10000.0
CB_HQ, CB_HF, CB_HI, CB_HZ, CB_RQ, CB_RK, CB_RV, CB_RZ, CB_GA, CB_GB = 0, 8, 16, 24, 32, 36, 40, 48, 56, 64
VMEM_LIMIT = 56 * 1024 * 1024

ADAM_LR, ADAM_B1, ADAM_B2, ADAM_EPS, ADAM_WD, ADAM_STEP = 0.001, 0.9, 0.999, 1e-08, 0.01, 10

_NN = (((1,), (0,)), ((), ()))
_NT = (((1,), (1,)), ((), ()))
_TN = (((0,), (0,)), ((), ()))
MESH = pl.DeviceIdType.MESH


def _dot(a, b, dims=_NN):
    return lax.dot_general(a.astype(_BF), b.astype(_BF), dims, preferred_element_type=F32)


def _split2(a):
    hi = a.astype(_BF)
    lo = (a - hi.astype(F32)).astype(_BF)
    return jnp.concatenate([hi, lo], axis=1)


def _dot_sel(sel, a):
    n = a.shape[1]
    r = lax.dot_general(sel.astype(_BF), _split2(a), _NN, preferred_element_type=F32)
    return r[:, :n] + r[:, n:]


def _dot_f32(a, b):
    def pieces(v):
        p1 = v.astype(_BF)
        r1 = v - p1.astype(F32)
        p2 = r1.astype(_BF)
        p3 = (r1 - p2.astype(F32)).astype(_BF)
        return p1, p2, p3
    a1, a2, a3 = pieces(a)
    b1, b2, b3 = pieces(b)
    d = lambda u, v: lax.dot_general(u, v, _NN, preferred_element_type=F32)
    return ((d(a1, b3) + d(a2, b2) + d(a3, b1)) + (d(a1, b2) + d(a2, b1))) + d(a1, b1)


def _sigmoid(v):
    return 1.0 / (1.0 + jnp.exp(-v))


def _params(sem=None):
    return pltpu.CompilerParams(dimension_semantics=sem, vmem_limit_bytes=VMEM_LIMIT)


def _hgrn_consts():
    c, nl = CHUNK, N_LEVELS
    t = np.arange(c)[:, None]
    j = np.arange(c)[None, :]
    sel = [j <= t]
    masks = [j == t]
    for l in range(1, nl + 1):
        m = ((t >> l) << l) + (1 << (l - 1)) - 1
        sec = t > m
        sel.append(np.where(sec, (j > m) & (j <= t), (j > t) & (j <= m)))
        same = (t >> l) == (j >> l)
        masks.append(same & sec & (j <= m))
    sel.append(j > t)
    sel = np.concatenate(sel, 0).astype(np.float32)
    masks = np.stack(masks).astype(np.float32)
    sgn = np.stack([np.where((t & (1 << (l - 1))) != 0, 1.0, -1.0) * np.ones((1, LANE)) for l in range(3, nl + 1)])
    return dict(tri=jnp.asarray(sel[:c], _BF),
                lvl=jnp.asarray(masks, F32),
                sgn=jnp.asarray(sgn, F32),
                sel_t=jnp.asarray(sel.T, _BF),
                lvl_b=jnp.asarray(masks, _BF),
                lvlt_b=jnp.asarray(np.swapaxes(masks, 1, 2), _BF))


def _level_exponents(b, logf, b_scr, sgn_ref):
    c = CHUNK
    b_scr[...] = b
    row = lax.broadcasted_iota(jnp.int32, (c, LANE), 0)
    nxt = pltpu.roll(logf, c - 1, 0)
    prv = pltpu.roll(logf, 1, 0)
    r4 = row & 3
    out = [jnp.where((row & 1) == 1, logf, 0.0),
           jnp.where(r4 == 0, nxt, jnp.where(r4 == 1, 0.0, jnp.where(r4 == 2, logf, logf + prv)))]
    for l in range(3, N_LEVELS + 1):
        size, half = 1 << l, 1 << (l - 1)
        ref = jnp.concatenate([jnp.broadcast_to(b_scr[i * size + half - 1:i * size + half, :], (size, LANE))
                               for i in range(c // size)], axis=0)
        out.append((b - ref) * sgn_ref[l - 3])
    return out


def _hgrn_chunk(hq, hf, hi, lbv, tri_ref, sgn_ref, b_scr):
    sq = _sigmoid(hq)
    q = hq * sq
    sg = _sigmoid(hf)
    omlb = 1.0 - lbv
    f = lbv + omlb * sg
    k = 1.0 - f
    logf = jnp.log(f) * LOG2_E
    b = _dot_sel(tri_ref[...], logf)
    bc = jnp.sum(logf, axis=0, keepdims=True)
    lev = [None] + [jnp.exp2(e) for e in _level_exponents(b, logf, b_scr, sgn_ref)]
    return dict(sq=sq, q=q, sg=sg, omlb=omlb, f=f, k=k, v=hi, eb=jnp.exp2(b), erem=jnp.exp2(bc - b),
                ebc=jnp.exp2(bc), lev=lev)


def _blockdiag(a, b):
    z = jnp.zeros_like(a)
    return jnp.concatenate([jnp.concatenate([a, z], axis=1), jnp.concatenate([z, b], axis=1)], axis=0)


def _level_operands(a):
    q, k = a["q"].astype(_BF), a["k"].astype(_BF)
    lev = [None] + [a["lev"][l].astype(_BF) for l in range(1, N_LEVELS + 1)]
    ql = [q] + [q * lev[l] for l in range(1, N_LEVELS + 1)]
    kl = [k] + [k * lev[l] for l in range(1, N_LEVELS + 1)]
    pairs = range(0, N_LEVELS + 1, 2)
    return ([jnp.concatenate([ql[l], ql[l + 1]], axis=1) for l in pairs], [_blockdiag(kl[l], kl[l + 1]) for l in pairs],
            ql, kl)


def _hgrn_scores(a, lvl_ref, q_pairs, k_diags):
    acc = None
    for n, (qp, kd) in enumerate(zip(q_pairs, k_diags)):
        both = lax.dot_general(qp, kd, _NT, preferred_element_type=F32)
        part = lvl_ref[2 * n] * both[:, :CHUNK] + lvl_ref[2 * n + 1] * both[:, CHUNK:]
        acc = part if acc is None else acc + part
    return acc


SCAN_UNROLL = 8
FWD_UNROLL = 16
RET_UNROLL = 16


def _writeback_reserve(step, make_copies):
    slot = step % 2

    @pl.when(step >= 2)
    def _():
        for cp in make_copies(slot):
            cp.wait()

    return slot


def _writeback_commit(step, n_steps, slot, make_copies):
    for cp in make_copies(slot):
        cp.start()

    @pl.when(step == n_steps - 1)
    def _():
        for cp in make_copies(slot):
            cp.wait()
        if n_steps > 1:
            for cp in make_copies(1 - slot):
                cp.wait()


def _resident(const):
    zeros = (0,) * const.ndim
    return pl.BlockSpec(const.shape, lambda p, t: zeros)


def _time_block(t_len):
    return min(t_len, 2048)


def hgrn_forward(pb, lb, t_len):
    nc = t_len // CHUNK
    tb = _time_block(t_len)
    ncb = tb // CHUNK
    consts = _hgrn_consts()
    operands = [consts[n] for n in ("tri", "lvl", "sgn")]

    def body(hq_ref, hf_ref, hi_ref, lb_ref, tri_ref, lvl_ref, sgn_ref, o_ref, ssave_ref, asave_ref, st_ref, b_scr):
        @pl.when(pl.program_id(1) == 0)
        def _():
            st_ref[...] = jnp.zeros_like(st_ref)

        def chunk(ci, carry):
            r = pl.ds(pl.multiple_of(ci * CHUNK, CHUNK), CHUNK)
            for hd in range(2):
                lbv = lb_ref[:, hd * LANE:(hd + 1) * LANE]
                a = _hgrn_chunk(hq_ref[hd, r, :], hf_ref[hd, r, :], hi_ref[hd, r, :], lbv, tri_ref, sgn_ref,
                                b_scr.at[hd])
                q_pairs, k_diags, _, _ = _level_operands(a)
                st = st_ref[hd]
                ssave_ref[hd, ci] = st
                scores = _hgrn_scores(a, lvl_ref, q_pairs, k_diags).astype(asave_ref.dtype)
                asave_ref[hd, ci] = scores
                o_ref[hd, r, :] = _dot(a["q"] * a["eb"], st, _NT) + _dot(scores, a["v"])
                st_ref[hd] = st * a["ebc"] + _dot(a["v"], a["k"] * a["erem"], _TN)
            return carry

        lax.fori_loop(0, ncb, chunk, 0, unroll=FWD_UNROLL)

    pair = lambda base: pl.BlockSpec((2, tb, LANE), lambda p, t, base=base: (base // 2 + p, t, 0))
    per_chunk = pl.BlockSpec((2, ncb, LANE, LANE), lambda p, t: (p, t, 0, 0))
    return pl.pallas_call(
        body, name="hgrn_fwd", grid=(N_HEADS // 2, t_len // tb),
        in_specs=[pair(CB_HQ), pair(CB_HF), pair(CB_HI),
                  pl.BlockSpec((1, 2 * LANE), lambda p, t: (0, p))] + [_resident(c) for c in operands],
        out_specs=[pl.BlockSpec((2, tb, LANE), lambda p, t: (p, t, 0)), per_chunk, per_chunk],
        out_shape=[jax.ShapeDtypeStruct((N_HEADS, t_len, LANE), F32),
                   jax.ShapeDtypeStruct((N_HEADS, nc, LANE, LANE), F32),
                   jax.ShapeDtypeStruct((N_HEADS, nc, CHUNK, CHUNK), _BF)],
        scratch_shapes=[pltpu.VMEM((2, LANE, LANE), F32), pltpu.VMEM((2, CHUNK, LANE), F32)],
        compiler_params=_params(("arbitrary", "arbitrary")),
    )(pb, pb, pb, lb, *operands)


def hgrn_backward(pb, lb, do, ssave, asave, dpb, t_len):
    tb = _time_block(t_len)
    ncb, ntb = tb // CHUNK, t_len // tb
    consts = _hgrn_consts()
    operands = [consts[n] for n in ("tri", "sgn", "sel_t", "lvl_b", "lvlt_b")]

    def body(hq_ref, hf_ref, hi_ref, lb_ref, do_ref, ssave_ref, asave_ref, tri_ref, sgn_ref, selt_ref, lvlb_ref,
             lvltb_ref, dpb_in, dpb_ref, dlb_ref, dq_buf, df_buf, di_buf, dst_ref, b_scr, sems):
        del dpb_in
        p, t = pl.program_id(0), pl.program_id(1)
        step = p * ntb + t
        rows = pl.ds(pl.multiple_of((ntb - 1 - t) * tb, tb), tb)

        def out_copies(sl):
            return [pltpu.make_async_copy(buf.at[sl], dpb_ref.at[pl.ds(base + 2 * p, 2), rows], sems.at[sl, n])
                    for n, (buf, base) in enumerate(((dq_buf, CB_HQ), (df_buf, CB_HF), (di_buf, CB_HI)))]

        slot = _writeback_reserve(step, out_copies)

        @pl.when(t == 0)
        def _():
            dst_ref[...] = jnp.zeros_like(dst_ref)
            dlb_ref[...] = jnp.zeros_like(dlb_ref)

        def chunk(i, carry):
            ci = ncb - 1 - i
            r = pl.ds(pl.multiple_of(ci * CHUNK, CHUNK), CHUNK)
            for hd in range(2):
                head_chunk(hd, ci, r)
            return carry

        def head_chunk(hd, ci, r):
            lbv = lb_ref[:, hd * LANE:(hd + 1) * LANE]
            hq = hq_ref[hd, r, :]
            a = _hgrn_chunk(hq, hf_ref[hd, r, :], hi_ref[hd, r, :], lbv, tri_ref, sgn_ref, b_scr.at[hd])
            _, k_diags, ql, kl = _level_operands(a)
            q, k, v = a["q"], a["k"], a["v"]
            g = do_ref[hd, r, :]
            st0 = ssave_ref[hd, ci]
            dst = dst_ref[hd]
            scores = asave_ref[hd, ci]
            da = _dot(g, v, _NT)
            da_t = _dot(v, g, _NT)
            kb = k * a["erem"]
            qb = q * a["eb"]
            dv = _dot(scores, g, _TN) + _dot(kb, dst, _NT)
            dq_inter = _dot(g, st0) * a["eb"]
            dk_state = _dot(v, dst) * a["erem"]
            dq, dk = dq_inter, dk_state
            de = [q * dq_inter]
            da_b, dat_b = da.astype(_BF), da_t.astype(_BF)
            for n in range(len(k_diags)):
                l0, l1 = 2 * n, 2 * n + 1
                da_pair = jnp.concatenate([lvlb_ref[l0] * da_b, lvlb_ref[l1] * da_b], axis=1)
                dat_pair = jnp.concatenate([lvltb_ref[l0] * dat_b, lvltb_ref[l1] * dat_b], axis=1)
                dq_both = lax.dot_general(da_pair, k_diags[n], _NN, preferred_element_type=F32)
                dk_both = lax.dot_general(dat_pair, _blockdiag(ql[l0], ql[l1]), _NN, preferred_element_type=F32)
                for l, cols in ((l0, slice(0, LANE)), (l1, slice(LANE, 2 * LANE))):
                    dql, dkl = dq_both[:, cols], dk_both[:, cols]
                    if l > 0:
                        e = a["lev"][l]
                        dql, dkl = dql * e, dkl * e
                        de.append(q * dql + k * dkl)
                    dq = dq + dql
                    dk = dk + dkl
            de.append(k * dk_state)
            dst_ref[hd] = dst * a["ebc"] + _dot(g, qb, _TN)
            dbc = jnp.sum(dst * st0, axis=0, keepdims=True) * a["ebc"]
            de2 = lax.dot_general(selt_ref[...], _split2(jnp.concatenate(de, axis=0)), _NN,
                                  preferred_element_type=F32)
            dlogf = de2[:, :LANE] + de2[:, LANE:] + dbc
            sq, sg = a["sq"], a["sg"]
            df = dlogf / a["f"] - dk
            dq_buf[slot, hd, r, :] = (dq * (sq * (1.0 + hq * (1.0 - sq)))).astype(dq_buf.dtype)
            df_buf[slot, hd, r, :] = (df * a["omlb"] * sg * (1.0 - sg)).astype(df_buf.dtype)
            di_buf[slot, hd, r, :] = dv.astype(di_buf.dtype)
            cols = slice(hd * LANE, (hd + 1) * LANE)
            dlb_ref[:, cols] = dlb_ref[:, cols] + jnp.sum(df * (1.0 - sg), axis=0, keepdims=True)

        lax.fori_loop(0, ncb, chunk, 0, unroll=SCAN_UNROLL)
        _writeback_commit(step, (N_HEADS // 2) * ntb, slot, out_copies)

    pair = lambda base: pl.BlockSpec((2, tb, LANE), lambda p, t, base=base: (base // 2 + p, ntb - 1 - t, 0))
    any_spec = pl.BlockSpec(memory_space=pl.ANY)
    per_chunk = pl.BlockSpec((2, ncb, LANE, LANE), lambda p, t: (p, ntb - 1 - t, 0, 0))
    return pl.pallas_call(
        body, name="hgrn_bwd", grid=(N_HEADS // 2, ntb),
        in_specs=[pair(CB_HQ), pair(CB_HF), pair(CB_HI),
                  pl.BlockSpec((1, 2 * LANE), lambda p, t: (0, p)),
                  pair(0), per_chunk, per_chunk]
        + [_resident(c) for c in operands] + [any_spec],
        out_specs=[any_spec, pl.BlockSpec((1, 2 * LANE), lambda p, t: (0, p))],
        out_shape=[jax.ShapeDtypeStruct(dpb.shape, dpb.dtype), jax.ShapeDtypeStruct((1, D_MODEL), F32)],
        scratch_shapes=[pltpu.VMEM((2, 2, tb, LANE), dpb.dtype)] * 3 + [
            pltpu.VMEM((2, LANE, LANE), F32), pltpu.VMEM((2, CHUNK, LANE), F32), pltpu.SemaphoreType.DMA((2, 3))],
        input_output_aliases={7 + len(operands): 0},
        compiler_params=_params(("arbitrary", "arbitrary")),
    )(pb, pb, pb, lb, do, ssave, asave, *operands, dpb)


def _rope_tables(t_len):
    half = RET_DK // 2
    inv_freq = (1.0 / (np.float32(ROPE_BASE) ** np.linspace(0.0, 1.0, half, dtype=np.float32))).astype(np.float32)
    ang = (np.arange(t_len, dtype=np.float32)[:, None] * inv_freq[None, :]).astype(np.float64)
    cos, sin = np.cos(ang).astype(np.float32), np.sin(ang).astype(np.float32)
    cos_t = np.concatenate([cos, cos, cos, cos], axis=1)
    sin_t = np.concatenate([-sin, sin, -sin, sin], axis=1)
    return jnp.asarray(cos_t), jnp.asarray(sin_t)


def _swap_halves(v):
    half = RET_DK // 2
    lane = lax.broadcasted_iota(jnp.int32, v.shape, 1)
    first = (lane & (RET_DK - 1)) < half
    return jnp.where(first, pltpu.roll(v, LANE - half, 1), pltpu.roll(v, half, 1))


def _ret_head_consts(hidx):
    c = CHUNK
    hf = jnp.full((1, LANE), hidx, jnp.int32).astype(F32)
    lg = jnp.log(1.0 - jnp.exp(-(5.0 + hf) * np.float32(np.log(2.0))))
    row = lax.broadcasted_iota(jnp.int32, (c, c), 0)
    col = lax.broadcasted_iota(jnp.int32, (c, c), 1)
    rel = (row - col).astype(F32)
    dm = jnp.where(rel >= 0, jnp.exp(lg[:, :1] * jnp.maximum(rel, 0.0)), 0.0)
    dm_t = jnp.where(rel <= 0, jnp.exp(lg[:, :1] * jnp.maximum(-rel, 0.0)), 0.0)
    idx = lax.broadcasted_iota(jnp.int32, (c, LANE), 0).astype(F32)
    zeta = jnp.exp(lg * (c - 1.0 - idx))
    xi = jnp.exp(lg * (idx + 1.0))
    cdec = jnp.exp(lg * float(c))
    return dm, zeta, xi, cdec, dm_t


def _lane_mask(which):
    lane = lax.broadcasted_iota(jnp.int32, (1, LANE), 1)
    return ((lane // RET_DK) == which).astype(F32)


def retention_forward(pb, cos_t, sin_t, t_len):
    nc = t_len // CHUNK

    tb = _time_block(t_len)
    ncb = tb // CHUNK

    def body(rq_ref, rk_ref, rv_ref, cos_ref, sin_ref, o_ref, rsave_ref, st_ref):
        p = pl.program_id(0)

        @pl.when(pl.program_id(1) == 0)
        def _():
            st_ref[...] = jnp.zeros_like(st_ref)

        consts = [_ret_head_consts(2 * p + hd) for hd in range(2)]

        def chunk(ci, carry):
            r = pl.ds(pl.multiple_of(ci * CHUNK, CHUNK), CHUNK)
            cs, sn = cos_ref[r, :], sin_ref[r, :]
            q = rq_ref[r, :]
            k = rk_ref[r, :]
            q = q * cs + _swap_halves(q) * sn
            k = (k * cs + _swap_halves(k) * sn) * RET_DK ** -0.5
            for hd in range(2):
                dm, zeta, xi, cdec, _ = consts[hd]
                lm = _lane_mask(hd)
                qh, kh = q * lm, k * lm
                v = rv_ref[hd, r, :]
                st = st_ref[hd]
                rsave_ref[hd, ci] = st
                scores = _dot(qh, kh, _NT) * dm
                o_ref[hd, r, :] = _dot(scores, v) + _dot(qh * xi, st, _NT)
                st_ref[hd] = st * cdec + _dot(v, kh * zeta, _TN)
            return carry

        lax.fori_loop(0, ncb, chunk, 0, unroll=RET_UNROLL)

    return pl.pallas_call(
        body, name="ret_fwd", grid=(N_HEADS // 2, t_len // tb),
        in_specs=[pl.BlockSpec((None, tb, LANE), lambda p, t: (CB_RQ + p, t, 0)),
                  pl.BlockSpec((None, tb, LANE), lambda p, t: (CB_RK + p, t, 0)),
                  pl.BlockSpec((2, tb, LANE), lambda p, t: (CB_RV // 2 + p, t, 0)),
                  pl.BlockSpec((tb, LANE), lambda p, t: (t, 0)),
                  pl.BlockSpec((tb, LANE), lambda p, t: (t, 0))],
        out_specs=[pl.BlockSpec((2, tb, LANE), lambda p, t: (p, t, 0)),
                   pl.BlockSpec((2, ncb, LANE, LANE), lambda p, t: (p, t, 0, 0))],
        out_shape=[jax.ShapeDtypeStruct((N_HEADS, t_len, LANE), F32),
                   jax.ShapeDtypeStruct((N_HEADS, nc, LANE, LANE), F32)],
        scratch_shapes=[pltpu.VMEM((2, LANE, LANE), F32)],
        compiler_params=_params(("arbitrary", "arbitrary")),
    )(pb, pb, pb, cos_t, sin_t)


def retention_backward(pb, cos_t, sin_t, do, rsave, dpb, t_len):
    tb = _time_block(t_len)
    ncb, ntb = tb // CHUNK, t_len // tb

    def body(rq_ref, rk_ref, rv_ref, cos_ref, sin_ref, do_ref, rsave_ref, dpb_in,
             dpb_ref, dq_buf, dk_buf, dv_buf, dst_ref, sems):
        del dpb_in
        p, t = pl.program_id(0), pl.program_id(1)
        step = p * ntb + t
        rows = pl.ds(pl.multiple_of((ntb - 1 - t) * tb, tb), tb)

        def out_copies(sl):
            return [pltpu.make_async_copy(dq_buf.at[sl], dpb_ref.at[CB_RQ + p, rows], sems.at[sl, 0]),
                    pltpu.make_async_copy(dk_buf.at[sl], dpb_ref.at[CB_RK + p, rows], sems.at[sl, 1]),
                    pltpu.make_async_copy(dv_buf.at[sl], dpb_ref.at[pl.ds(CB_RV + 2 * p, 2), rows], sems.at[sl, 2])]

        slot = _writeback_reserve(step, out_copies)

        @pl.when(t == 0)
        def _():
            dst_ref[...] = jnp.zeros_like(dst_ref)

        consts = [_ret_head_consts(2 * p + hd) for hd in range(2)]

        def chunk(i, carry):
            ci = ncb - 1 - i
            r = pl.ds(pl.multiple_of(ci * CHUNK, CHUNK), CHUNK)
            cs, sn = cos_ref[r, :], sin_ref[r, :]
            q = rq_ref[r, :]
            k = rk_ref[r, :]
            q = q * cs + _swap_halves(q) * sn
            k = (k * cs + _swap_halves(k) * sn) * RET_DK ** -0.5
            dq, dk = None, None
            for hd in range(2):
                dm, zeta, xi, cdec, dm_t = consts[hd]
                lm = _lane_mask(hd)
                qh, kh = q * lm, k * lm
                v = rv_ref[hd, r, :]
                g = do_ref[hd, r, :]
                st0 = rsave_ref[hd, ci]
                dst = dst_ref[hd]
                scores_t = _dot(kh, qh, _NT) * dm_t
                dsc = _dot(g, v, _NT) * dm
                dsc_t = _dot(v, g, _NT) * dm_t
                dqh = _dot(dsc, kh) + _dot(g, st0) * xi
                dkh = _dot(dsc_t, qh) + _dot(v, dst) * zeta
                dv_buf[slot, hd, r, :] = (_dot(scores_t, g) + _dot(kh * zeta, dst, _NT)).astype(dv_buf.dtype)
                dst_ref[hd] = dst * cdec + _dot(g, qh * xi, _TN)
                dq = dqh if dq is None else dq + dqh
                dk = dkh if dk is None else dk + dkh
            dk = dk * (RET_DK ** -0.5)
            dq_buf[slot, r, :] = (dq * cs - _swap_halves(dq) * sn).astype(dq_buf.dtype)
            dk_buf[slot, r, :] = (dk * cs - _swap_halves(dk) * sn).astype(dk_buf.dtype)
            return carry

        lax.fori_loop(0, ncb, chunk, 0, unroll=RET_UNROLL)
        _writeback_commit(step, (N_HEADS // 2) * ntb, slot, out_copies)

    any_spec = pl.BlockSpec(memory_space=pl.ANY)
    return pl.pallas_call(
        body, name="ret_bwd", grid=(N_HEADS // 2, ntb),
        in_specs=[pl.BlockSpec((None, tb, LANE), lambda p, t: (CB_RQ + p, ntb - 1 - t, 0)),
                  pl.BlockSpec((None, tb, LANE), lambda p, t: (CB_RK + p, ntb - 1 - t, 0)),
                  pl.BlockSpec((2, tb, LANE), lambda p, t: (CB_RV // 2 + p, ntb - 1 - t, 0)),
                  pl.BlockSpec((tb, LANE), lambda p, t: (ntb - 1 - t, 0)),
                  pl.BlockSpec((tb, LANE), lambda p, t: (ntb - 1 - t, 0)),
                  pl.BlockSpec((2, tb, LANE), lambda p, t: (p, ntb - 1 - t, 0)),
                  pl.BlockSpec((2, ncb, LANE, LANE), lambda p, t: (p, ntb - 1 - t, 0, 0)),
                  any_spec],
        out_specs=any_spec,
        out_shape=jax.ShapeDtypeStruct(dpb.shape, dpb.dtype),
        scratch_shapes=[pltpu.VMEM((2, tb, LANE), dpb.dtype), pltpu.VMEM((2, tb, LANE), dpb.dtype),
                        pltpu.VMEM((2, 2, tb, LANE), dpb.dtype), pltpu.VMEM((2, LANE, LANE), F32),
                        pltpu.SemaphoreType.DMA((2, 3))],
        input_output_aliases={7: 0},
        compiler_params=_params(("arbitrary", "arbitrary")),
    )(pb, pb, pb, cos_t, sin_t, do, rsave, dpb)


def _row_tile(t_len, want):
    return min(want, t_len)


PAIR_CB = 2 * CB_PER_SHARD


def proj_forward(h, wt, chips, t_len, name, pb=None):
    tm = _row_tile(t_len, 1024)

    def body(chips_ref, h_ref, w_ref, *rest):
        del chips_ref
        o_ref = rest[-1]
        acc = _dot(h_ref[...], w_ref[...], _NT)
        for jj in range(PAIR_CB):
            o_ref[jj] = acc[:, jj * LANE:(jj + 1) * LANE]

    given = [] if pb is None else [pb]
    return pl.pallas_call(
        body, name=name,
        grid_spec=pltpu.PrefetchScalarGridSpec(
            num_scalar_prefetch=1, grid=(chips.shape[0], t_len // tm),
            in_specs=[pl.BlockSpec((tm, D_MODEL), lambda j, i, ch: (i, 0)),
                      pl.BlockSpec((PAIR_CB * LANE, D_MODEL), lambda j, i, ch: (ch[j], 0))]
            + [pl.BlockSpec(memory_space=pl.ANY)] * len(given),
            out_specs=pl.BlockSpec((PAIR_CB, tm, LANE), lambda j, i, ch: (ch[j], i, 0))),
        out_shape=jax.ShapeDtypeStruct((N_CB, t_len, LANE), F32),
        input_output_aliases={3: 0} if given else {},
        compiler_params=_params(("arbitrary", "arbitrary")),
    )(chips, h, wt, *given)


def proj_backward_input(dpb, wt, token, x, dy, norm_g, scale1p, t_len):
    tm = _row_tile(t_len, 512)

    def body(a_ref, wt_hbm, token_ref, x_ref, dy_ref, g_ref, sc_ref, gx_ref, vec_ref, w_ref, sem):
        del token_ref
        i = pl.program_id(0)

        @pl.when(i == 0)
        def _():
            cp = pltpu.make_async_copy(wt_hbm, w_ref, sem)
            cp.start()
            cp.wait()

        a = jnp.concatenate([a_ref[jj].astype(_BF) for jj in range(N_CB)], axis=1)
        dhv = _dot(a, w_ref[...])
        xv, g, sc = x_ref[...], g_ref[...], sc_ref[...]
        r = lax.rsqrt(jnp.mean(xv * xv, axis=-1, keepdims=True) + EPS)
        xn = xv * r
        dxn = dhv * (g * sc)
        gx_ref[...] = dy_ref[...] + r * dxn - xn * (r * r) * jnp.mean(xv * dxn, axis=-1, keepdims=True)
        t = dhv * xn
        _acc_rows(vec_ref, i, [jnp.sum(t * sc, axis=0, keepdims=True),
                               jnp.sum(t * g, axis=0, keepdims=True),
                               jnp.sum(dhv, axis=0, keepdims=True)])

    row = pl.BlockSpec((tm, D_MODEL), lambda i: (i, 0))
    return pl.pallas_call(
        body, name="proj_bwd_input", grid=(t_len // tm,),
        in_specs=[pl.BlockSpec((N_CB, tm, LANE), lambda i: (0, i, 0)),
                  pl.BlockSpec(memory_space=pl.ANY),
                  pl.BlockSpec(token.shape, lambda i: (0, 0)),
                  row, row, _vec_spec(), _vec_spec()],
        out_specs=[row, pl.BlockSpec((8, D_MODEL), lambda i: (0, 0))],
        out_shape=[jax.ShapeDtypeStruct((t_len, D_MODEL), F32), jax.ShapeDtypeStruct((8, D_MODEL), F32)],
        scratch_shapes=[pltpu.VMEM(wt.shape, wt.dtype), pltpu.SemaphoreType.DMA],
        compiler_params=_params(("arbitrary",)),
    )(dpb, wt, token, x, dy, norm_g, scale1p)


def proj_backward_weight(h_t, dpb, t_len):
    tk = _row_tile(t_len, 2048)

    def body(h_ref, b_ref, o_ref):
        k = pl.program_id(1)
        b = jnp.concatenate([b_ref[jj].astype(_BF) for jj in range(PAIR_CB)], axis=1)
        part = _dot(h_ref[...], b)

        @pl.when(k == 0)
        def _():
            for s in range(2):
                o_ref[s] = part[:, s * SHARD_IN:(s + 1) * SHARD_IN]

        @pl.when(k > 0)
        def _():
            for s in range(2):
                o_ref[s] = o_ref[s] + part[:, s * SHARD_IN:(s + 1) * SHARD_IN]

    return pl.pallas_call(
        body, name="proj_bwd_weight", grid=(N_DEV // 2, t_len // tk),
        in_specs=[pl.BlockSpec((D_MODEL, tk), lambda j, k: (0, k)),
                  pl.BlockSpec((PAIR_CB, tk, LANE), lambda j, k: (j, k, 0))],
        out_specs=pl.BlockSpec((2, D_MODEL, SHARD_IN), lambda j, k: (j, 0, 0)),
        out_shape=jax.ShapeDtypeStruct((N_DEV, D_MODEL, SHARD_IN), F32),
        compiler_params=_params(("arbitrary", "arbitrary")),
    )(h_t, dpb)


def sibling_blocks(g_in, c_idx):
    tr = D_MODEL

    def body(c_ref, g_ref, o_ref):
        del c_ref
        o_ref[...] = g_ref[...].astype(o_ref.dtype)

    return pl.pallas_call(
        body, name="sibling_blocks",
        grid_spec=pltpu.PrefetchScalarGridSpec(
            num_scalar_prefetch=1, grid=(N_DEV // 2, D_MODEL // tr),
            in_specs=[pl.BlockSpec((None, tr, SHARD_IN), lambda q, i, c: (2 * q + 1 - c[0], i, 0))],
            out_specs=pl.BlockSpec((None, tr, SHARD_IN), lambda q, i, c: (q, i, 0))),
        out_shape=jax.ShapeDtypeStruct((N_DEV // 2, D_MODEL, SHARD_IN), _BF),
        compiler_params=_params(("arbitrary", "arbitrary")),
    )(c_idx, g_in)


def _vec_spec():
    return pl.BlockSpec((1, D_MODEL), lambda i: (0, 0))


def _acc_rows(ref, i, rows):
    @pl.when(i == 0)
    def _():
        ref[...] = jnp.zeros_like(ref)

    for n, row in enumerate(rows):
        ref[n:n + 1, :] = ref[n:n + 1, :] + row


def adaln_forward(x, norm_g, scale1p, shift, t_len):
    tm = _row_tile(t_len, 1024)

    def body(x_ref, g_ref, sc_ref, sh_ref, h_ref, ht_ref):
        xv = x_ref[...]
        r = lax.rsqrt(jnp.mean(xv * xv, axis=-1, keepdims=True) + EPS)
        h = xv * r * g_ref[...] * sc_ref[...] + sh_ref[...]
        h_ref[...] = h.astype(h_ref.dtype)
        ht_ref[...] = h.T.astype(ht_ref.dtype)

    return pl.pallas_call(
        body, name="adaln_fwd", grid=(t_len // tm,),
        in_specs=[pl.BlockSpec((tm, D_MODEL), lambda i: (i, 0)), _vec_spec(), _vec_spec(), _vec_spec()],
        out_specs=[pl.BlockSpec((tm, D_MODEL), lambda i: (i, 0)), pl.BlockSpec((D_MODEL, tm), lambda i: (0, i))],
        out_shape=[jax.ShapeDtypeStruct((t_len, D_MODEL), _BF), jax.ShapeDtypeStruct((D_MODEL, t_len), _BF)],
        compiler_params=_params(("arbitrary",)),
    )(x, norm_g, scale1p, shift)


def _head_norm(o, g):
    r = lax.rsqrt(jnp.mean(o * o, axis=-1, keepdims=True) + EPS)
    return r, o * r * g


def _group_spec(tm, cb):
    return pl.BlockSpec((N_HEADS, tm, LANE), lambda i, cb=cb: (cb // N_HEADS, i, 0))


MID_FINAL_G, MID_GATE, MID_LOSS, MID_HG_G, MID_RET_G = range(5)


def middle(x, target, oa, ob, pb, wout, gate, final_g, hg_g, ret_g, t_len):
    tm = _row_tile(t_len, 256)
    n_steps = t_len // tm

    def body(x_ref, t_ref, oa_ref, ob_ref, hz_ref, rz_ref, ga_ref, gb_ref, w_ref, gate_ref, fg_ref, hg_ref, rg_ref,
             dy_ref, doa_ref, dob_ref, dw_ref, vec_ref, dpb_ref, m_scr, dm_scr, keep, bufs, sems):
        i = pl.program_id(0)
        rows = pl.ds(pl.multiple_of(i * tm, tm), tm)

        def group_copies(sl):
            return [pltpu.make_async_copy(bufs.at[sl, n], dpb_ref.at[pl.ds(cb, N_HEADS), rows], sems.at[sl, n])
                    for n, cb in enumerate((CB_HZ, CB_RZ, CB_GA, CB_GB))]
        sides = ((oa_ref, hz_ref, ga_ref, hg_ref, doa_ref), (ob_ref, rz_ref, gb_ref, rg_ref, dob_ref))
        for hh in range(N_HEADS):
            ls = slice(hh * LANE, (hh + 1) * LANE)
            acc = None
            for side, (o_ref, z_ref, gt_ref, g_ref, _) in enumerate(sides):
                o = o_ref[hh]
                rr = lax.rsqrt(jnp.mean(o * o, axis=-1, keepdims=True) + EPS)
                orr = o * rr
                zz = z_ref[hh]
                sz = _sigmoid(zz)
                sgt = _sigmoid(gt_ref[hh])
                keep[side, hh, 0] = orr
                keep[side, hh, 1] = sz
                keep[side, hh, 2] = sgt
                keep[side, hh, 3] = jnp.broadcast_to(rr, orr.shape)
                u = sgt * ((orr * g_ref[:, ls]) * (zz * sz))
                acc = u if acc is None else acc + u
            m_scr[:, ls] = acc.astype(m_scr.dtype)
        zv = _dot(m_scr[...], w_ref[...])
        gt, fg = gate_ref[...], fg_ref[...]
        y = x_ref[...] + gt * zv
        r = lax.rsqrt(jnp.mean(y * y, axis=-1, keepdims=True) + EPS)
        yn = y * r
        err = yn * fg - t_ref[...]
        loss = 0.5 * jnp.sum(jnp.mean(err * err, axis=-1, keepdims=True), axis=0, keepdims=True)
        dout = err * (1.0 / D_MODEL)
        gd = dout * fg
        dy = r * gd - yn * (r * r) * jnp.mean(y * gd, axis=-1, keepdims=True)
        dy_ref[...] = dy
        dz = (dy * gt).astype(_BF)
        dm_scr[...] = _dot(dz, w_ref[...], _NT)
        part = _dot(m_scr[...], dz, _TN)

        @pl.when(i == 0)
        def _():
            dw_ref[...] = part

        @pl.when(i > 0)
        def _():
            dw_ref[...] = dw_ref[...] + part

        slot = _writeback_reserve(i, group_copies)
        dg = [[], []]
        for hh in range(N_HEADS):
            ls = slice(hh * LANE, (hh + 1) * LANE)
            dmh = dm_scr[:, ls]
            for side, (o_ref, z_ref, gt_ref, g_ref, do_ref) in enumerate(sides):
                zz, g = z_ref[hh], g_ref[:, ls]
                orr, sz, sgt, rr = keep[side, hh, 0], keep[side, hh, 1], keep[side, hh, 2], keep[side, hh, 3]
                n = orr * g
                silu = zz * sz
                du = dmh * sgt
                bufs[slot, 2 + side, hh] = (dmh * (n * silu) * (sgt * (1.0 - sgt))).astype(bufs.dtype)
                bufs[slot, side, hh] = (du * n * (sz * (1.0 + zz * (1.0 - sz)))).astype(bufs.dtype)
                dn = du * silu
                dg[side].append(jnp.sum(dn * orr, axis=0, keepdims=True))
                gdn = dn * g
                do_ref[hh] = (rr * (gdn - orr * jnp.mean(orr * gdn, axis=-1, keepdims=True))).astype(do_ref.dtype)
        _acc_rows(vec_ref, i, [jnp.sum(dout * yn, axis=0, keepdims=True),
                               jnp.sum(dy * zv, axis=0, keepdims=True),
                               jnp.broadcast_to(loss, (1, D_MODEL)),
                               jnp.concatenate(dg[0], axis=1), jnp.concatenate(dg[1], axis=1)])
        _writeback_commit(i, n_steps, slot, group_copies)

    row = pl.BlockSpec((tm, D_MODEL), lambda i: (i, 0))
    head = pl.BlockSpec((N_HEADS, tm, LANE), lambda i: (0, i, 0))
    full = pl.BlockSpec((D_MODEL, D_MODEL), lambda i: (0, 0))
    return pl.pallas_call(
        body, name="middle", grid=(n_steps,),
        in_specs=[row, row, head, head, _group_spec(tm, CB_HZ), _group_spec(tm, CB_RZ), _group_spec(tm, CB_GA),
                  _group_spec(tm, CB_GB), full, _vec_spec(), _vec_spec(), _vec_spec(), _vec_spec()],
        out_specs=[row, head, head, full, pl.BlockSpec((8, D_MODEL), lambda i: (0, 0)),
                   pl.BlockSpec(memory_space=pl.ANY)],
        out_shape=[jax.ShapeDtypeStruct((t_len, D_MODEL), F32),
                   jax.ShapeDtypeStruct((N_HEADS, t_len, LANE), _BF),
                   jax.ShapeDtypeStruct((N_HEADS, t_len, LANE), _BF),
                   jax.ShapeDtypeStruct((D_MODEL, D_MODEL), F32),
                   jax.ShapeDtypeStruct((8, D_MODEL), F32),
                   jax.ShapeDtypeStruct((N_CB, t_len, LANE), _BF)],
        scratch_shapes=[pltpu.VMEM((tm, D_MODEL), _BF), pltpu.VMEM((tm, D_MODEL), F32),
                        pltpu.VMEM((2, N_HEADS, 4, tm, LANE), F32),
                        pltpu.VMEM((2, 4, N_HEADS, tm, LANE), _BF), pltpu.SemaphoreType.DMA((2, 4))],
        compiler_params=_params(("arbitrary",)),
    )(x, target, oa, ob, pb, pb, pb, pb, wout, gate, final_g, hg_g, ret_g)


def device_step(x, target, mod, lb, project, norm_g, hg_g, ret_g, final_g, c_idx=None, start_exchange=None):
    t_len = x.shape[0]
    shift, scale, gate = mod[:, :D_MODEL], mod[:, D_MODEL:2 * D_MODEL], mod[:, 2 * D_MODEL:]
    scale1p = 1.0 + scale
    cos_t, sin_t = _rope_tables(t_len)
    h, h_t = adaln_forward(x, norm_g, scale1p, shift, t_len)
    pb, wt, wout = project(h)
    oa, ssave, asave = hgrn_forward(pb, lb, t_len)
    ob, rsave = retention_forward(pb, cos_t, sin_t, t_len)
    dy, doa, dob, dwout, vec_mid, dpb = middle(x, target, oa, ob, pb, wout, gate, final_g, hg_g, ret_g, t_len)
    dpb, dlb = hgrn_backward(pb, lb, doa, ssave, asave, dpb, t_len)
    dpb = retention_backward(pb, cos_t, sin_t, dob, rsave, dpb, t_len)
    c_idx = jnp.zeros((1,), jnp.int32) if c_idx is None else c_idx
    dwin = proj_backward_weight(h_t, dpb, t_len)
    dwin_sib = sibling_blocks(dwin, c_idx)
    token, pending = (start_exchange(dwin, dwin_sib, dwout) if start_exchange
                      else (jnp.zeros((8, LANE), F32), None))
    grad_x, vec_ada = proj_backward_input(dpb, wt, token, x, dy, norm_g, scale1p, t_len)
    return grad_x, dwin, dwout, vec_mid, vec_ada, dlb, pending


PACK_ROWS = 16
ROW_NORM_G, ROW_LB, ROW_HG_G, ROW_RET_G, ROW_FINAL_G, ROW_SHIFT, ROW_SCALE, ROW_GATE, ROW_LOSS = range(9)


def _mesh_pos():
    return lax.axis_index("x"), lax.axis_index("y"), lax.axis_index("c")


def _lin(pos):
    return 4 * pos[0] + 2 * pos[1] + pos[2]


def _xor_peer(pos, k):
    return tuple(1 - p if (k >> s) & 1 else p for p, s in zip(pos, (2, 1, 0)))


def _other_chips(pos):
    x, y, _ = pos
    return [(1 - x, y), (x, 1 - y), (1 - x, 1 - y)]


def _remote(src, dst, send_sem, recv_sem, to):
    return pltpu.make_async_remote_copy(src_ref=src, dst_ref=dst, send_sem=send_sem, recv_sem=recv_sem,
                                        device_id=to, device_id_type=MESH)


def pre_exchange(c, w_ada, b_ada, logits, w_in, w_out):
    def body(c_ref, wada_ref, bada_ref, logit_ref, win_ref, wout_ref, mod_ref, scall_ref, lb_ref, wt_ref, wo_ref,
             cg_ref, modall_ref, parts_ref, send1, recv1, send2, recv2):
        pos = _mesh_pos()
        cv = c_ref[...]
        slot = lambda p: pl.ds(pl.multiple_of(8 * _lin(p), 8), 8)
        cg_ref[slot(pos), :] = jnp.broadcast_to(cv * _sigmoid(cv), (8, D_MODEL))
        lb_ref[...] = _sigmoid(logit_ref[0:1, :] - logit_ref[1:2, :])
        peers = [_xor_peer(pos, k) for k in range(1, N_DEV)]
        gather = [_remote(cg_ref.at[slot(pos)], cg_ref.at[slot(pos)], send1.at[n], recv1.at[n], p)
                  for n, p in enumerate(peers)]
        for cp in gather:
            cp.start()
        wt_ref[...] = win_ref[...].T.astype(wt_ref.dtype)
        wo_ref[...] = wout_ref[...].astype(wo_ref.dtype)
        for n, p in enumerate(peers):
            _remote(cg_ref.at[slot(p)], cg_ref.at[slot(p)], send1.at[n], recv1.at[n], p).wait_recv()
        modall_ref[...] = _dot(cg_ref[...], wada_ref[...])
        scatter = [_remote(modall_ref.at[slot(p)], parts_ref.at[slot(pos)], send2.at[n], recv2.at[n], p)
                   for n, p in enumerate(peers)]
        for cp in scatter:
            cp.start()
        parts_ref[slot(pos), :] = modall_ref[slot(pos), :]
        for n, p in enumerate(peers):
            _remote(modall_ref.at[slot(p)], parts_ref.at[slot(p)], send2.at[n], recv2.at[n], p).wait_recv()
        for cp in gather + scatter:
            cp.wait_send()
        for j in range(N_DEV):
            cols = slice(j * SHARD_ADA, (j + 1) * SHARD_ADA)
            mod_ref[:, cols] = parts_ref[8 * j:8 * j + 1, :] + bada_ref[:, cols]
            scall_ref[j:j + 1, :] = cg_ref[8 * j:8 * j + 1, :]

    vmem = pl.BlockSpec(memory_space=pltpu.VMEM)
    return pl.pallas_call(
        body, name="pre_exchange",
        in_specs=[vmem] * 6, out_specs=[vmem] * 5,
        out_shape=[jax.ShapeDtypeStruct((1, 3 * D_MODEL), F32), jax.ShapeDtypeStruct((N_DEV, D_MODEL), F32),
                   jax.ShapeDtypeStruct((1, D_MODEL), F32),
                   jax.ShapeDtypeStruct(w_in.shape[::-1], _BF), jax.ShapeDtypeStruct(w_out.shape, _BF)],
        scratch_shapes=[pltpu.VMEM((N_DEV * 8, D_MODEL), F32), pltpu.VMEM((N_DEV * 8, SHARD_ADA), F32),
                        pltpu.VMEM((N_DEV * 8, SHARD_ADA), F32)] + [pltpu.SemaphoreType.DMA((N_DEV - 1,))] * 4,
        compiler_params=pltpu.CompilerParams(vmem_limit_bytes=VMEM_LIMIT),
    )(c, w_ada, b_ada, logits, w_in, w_out)


def _gather_copies(srcs, outs, sems):
    pos = _mesh_pos()
    x, y, c = pos
    sibling = (x, y, 1 - c)

    def route(core):
        return [(x + (1 - core) * (1 - 2 * x), y + core * (1 - 2 * y)),
                (x + core * (1 - 2 * x), y + (1 - core) * (1 - 2 * y)),
                (1 - x, 1 - y)]

    mine = [(*chip, c) for chip in route(c)]
    carried = [pos, pos, pos, mine[0], mine[0], mine[1], mine[2]]
    to = [sibling, mine[0], mine[1], mine[1], sibling, sibling, sibling]
    landed = [sibling] + mine + [(*chip, 1 - c) for chip in route(1 - c)]

    def sent(a, k):
        dst = outs[a].at[_lin(carried[k])]
        return _remote(srcs[a] if k < 3 else dst, dst, *sems(a, k), to[k])

    def arrival(a, k):
        dst = outs[a].at[_lin(landed[k])]
        return _remote(dst, dst, *sems(a, k), pos)

    return sent, arrival


_ROUNDS = (range(0, 3), range(3, 6), range(6, 7))


def _round_sems(send, recv, rnd):
    n = len(_ROUNDS[rnd])
    return lambda a, k: (send.at[n * a + k - _ROUNDS[rnd][0]], recv.at[n * a + k - _ROUNDS[rnd][0]])


def weight_gather_start(win_sh, wout_sh):
    def body(win_ref, wout_ref, wg_in, woutg_in, send, recv, wg_ref, woutg_ref, token, local):
        del wg_in, woutg_in
        sent, _ = _gather_copies((win_ref, wout_ref), (wg_ref, woutg_ref), _round_sems(send, recv, 0))
        first = [sent(a, k) for a in range(2) for k in _ROUNDS[0]]
        me = _lin(_mesh_pos())
        mine = [pltpu.make_async_copy(win_ref, wg_ref.at[me], local.at[0]),
                pltpu.make_async_copy(wout_ref, woutg_ref.at[me], local.at[1])]
        for cp in mine:
            cp.start()
        for cp in mine:
            cp.wait()
        for cp in first:
            cp.start()
        token[...] = jnp.zeros_like(token)

    hbm = lambda a: pltpu.with_memory_space_constraint(a, pltpu.HBM)
    wg = lax.empty((N_DEV,) + win_sh.shape, win_sh.dtype)
    woutg = lax.empty((N_DEV,) + wout_sh.shape, wout_sh.dtype)
    return pl.pallas_call(
        body, name="weight_gather_start",
        in_specs=[_HBM] * 4,
        out_specs=[_SEM, _SEM, _HBM, _HBM, pl.BlockSpec(memory_space=pltpu.VMEM)],
        out_shape=[pltpu.SemaphoreType.DMA((2 * len(_ROUNDS[0]),)), pltpu.SemaphoreType.DMA((2 * len(_ROUNDS[0]),)),
                   pltpu.HBM(wg.shape, wg.dtype), pltpu.HBM(woutg.shape, woutg.dtype),
                   jax.ShapeDtypeStruct((8, LANE), F32)],
        scratch_shapes=[pltpu.SemaphoreType.DMA((2,))],
        input_output_aliases={2: 2, 3: 3},
        compiler_params=pltpu.CompilerParams(has_side_effects=pltpu.SideEffectType.DATAFLOW_SIDE_EFFECTING),
    )(hbm(win_sh), hbm(wout_sh), hbm(wg), hbm(woutg))


def _wait_round(sent, arrival, rnd):
    for a in range(2):
        for k in _ROUNDS[rnd]:
            arrival(a, k).wait_recv()
            sent(a, k).wait_send()


def weight_gather_wait(send, recv, win_sh, wout_sh, wg, woutg, after):
    def body(win_ref, wout_ref, wg_ref, woutg_ref, send, recv, after_ref, wg_out, woutg_out):
        del after_ref, wg_out, woutg_out
        _wait_round(*_gather_copies((win_ref, wout_ref), (wg_ref, woutg_ref), _round_sems(send, recv, 0)), 0)

    hbm = lambda a: pltpu.with_memory_space_constraint(a, pltpu.HBM)
    return pl.pallas_call(
        body, name="weight_gather_wait",
        in_specs=[_HBM] * 4 + [_SEM, _SEM, pl.BlockSpec(memory_space=pl.ANY)],
        out_specs=[_HBM] * 2,
        out_shape=[pltpu.HBM(wg.shape, wg.dtype), pltpu.HBM(woutg.shape, woutg.dtype)],
        input_output_aliases={2: 0, 3: 1},
        compiler_params=pltpu.CompilerParams(has_side_effects=pltpu.SideEffectType.DATAFLOW_SIDE_EFFECTING),
    )(hbm(win_sh), hbm(wout_sh), wg, woutg, send, recv, after)


def weight_pass_start(wg, woutg):
    def body(wg_ref, woutg_ref, send, recv, wg_out, woutg_out):
        del wg_out, woutg_out
        sent, _ = _gather_copies(None, (wg_ref, woutg_ref), _round_sems(send, recv, 1))
        for a in range(2):
            for k in _ROUNDS[1]:
                sent(a, k).start()

    n = 2 * len(_ROUNDS[1])
    return pl.pallas_call(
        body, name="weight_pass_start",
        in_specs=[_HBM] * 2,
        out_specs=[_SEM, _SEM, _HBM, _HBM],
        out_shape=[pltpu.SemaphoreType.DMA((n,)), pltpu.SemaphoreType.DMA((n,)),
                   pltpu.HBM(wg.shape, wg.dtype), pltpu.HBM(woutg.shape, woutg.dtype)],
        input_output_aliases={0: 2, 1: 3},
        compiler_params=pltpu.CompilerParams(has_side_effects=pltpu.SideEffectType.DATAFLOW_SIDE_EFFECTING),
    )(wg, woutg)


def weight_pass_wait(send, recv, wg, woutg, after):
    def body(wg_ref, woutg_ref, send, recv, after_ref, wg_out, woutg_out):
        del after_ref, wg_out, woutg_out
        _wait_round(*_gather_copies(None, (wg_ref, woutg_ref), _round_sems(send, recv, 1)), 1)

    return pl.pallas_call(
        body, name="weight_pass_wait",
        in_specs=[_HBM] * 2 + [_SEM, _SEM, pl.BlockSpec(memory_space=pl.ANY)],
        out_specs=[_HBM] * 2,
        out_shape=[pltpu.HBM(wg.shape, wg.dtype), pltpu.HBM(woutg.shape, woutg.dtype)],
        input_output_aliases={0: 0, 1: 1},
        compiler_params=pltpu.CompilerParams(has_side_effects=pltpu.SideEffectType.DATAFLOW_SIDE_EFFECTING),
    )(wg, woutg, send, recv, after)


def weight_pass_last(wg, woutg):
    def body(wg_in, woutg_in, wg_ref, woutg_ref, send, recv):
        del wg_in, woutg_in
        sent, arrival = _gather_copies(None, (wg_ref, woutg_ref), _round_sems(send, recv, 2))
        for a in range(2):
            sent(a, _ROUNDS[2][0]).start()
        _wait_round(sent, arrival, 2)

    any_spec = pl.BlockSpec(memory_space=pl.ANY)
    n = 2 * len(_ROUNDS[2])
    return pl.pallas_call(
        body, name="weight_pass_last",
        in_specs=[any_spec, any_spec], out_specs=[any_spec, any_spec],
        out_shape=[jax.ShapeDtypeStruct(wg.shape, wg.dtype), jax.ShapeDtypeStruct(woutg.shape, woutg.dtype)],
        scratch_shapes=[pltpu.SemaphoreType.DMA((n,)), pltpu.SemaphoreType.DMA((n,))],
        input_output_aliases={0: 0, 1: 1},
    )(wg, woutg)


def grad_pair_exchange(g_sib, g_out):
    def body(gsib_ref, gout_ref, ra_ref, rb_ref, send, recv):
        pos = _mesh_pos()
        x, y, c = pos
        sibling = (x, y, 1 - c)
        copies = []
        for q in range(4):
            copies.append(_remote(gsib_ref.at[q], ra_ref.at[q], send.at[q], recv.at[q], sibling))
            copies.append(_remote(gout_ref.at[2 * q + (1 - c)], rb_ref.at[q], send.at[4 + q], recv.at[4 + q], sibling))
        for cp in copies:
            cp.start()
        for cp in copies:
            cp.wait_recv()
        for cp in copies:
            cp.wait_send()

    any_spec = pl.BlockSpec(memory_space=pl.ANY)
    return pl.pallas_call(
        body, name="grad_pair_exchange",
        in_specs=[any_spec, any_spec], out_specs=[any_spec, any_spec],
        out_shape=[jax.ShapeDtypeStruct(g_sib.shape, g_sib.dtype), jax.ShapeDtypeStruct((4,) + g_out.shape[1:], F32)],
        scratch_shapes=[pltpu.SemaphoreType.DMA((8,)), pltpu.SemaphoreType.DMA((8,))],
    )(g_sib, g_out)


def pair_sum(g_in, ra, g_out, rb, c_idx):
    tr = D_MODEL

    def body(c_ref, gin_ref, ra_ref, gout_ref, rb_ref, sb_ref, sbo_ref):
        del c_ref
        sb_ref[...] = (gin_ref[...] + ra_ref[...].astype(F32)).astype(sb_ref.dtype)
        sbo_ref[...] = gout_ref[...] + rb_ref[...]

    n_i = D_MODEL // tr
    return pl.pallas_call(
        body, name="pair_sum",
        grid_spec=pltpu.PrefetchScalarGridSpec(
            num_scalar_prefetch=1, grid=(4, n_i),
            in_specs=[pl.BlockSpec((None, tr, SHARD_IN), lambda q, i, c: (2 * q + c[0], i, 0)),
                      pl.BlockSpec((None, tr, SHARD_IN), lambda q, i, c: (q, i, 0)),
                      pl.BlockSpec((None, SHARD_OUT // n_i, D_MODEL), lambda q, i, c: (2 * q + c[0], i, 0)),
                      pl.BlockSpec((None, SHARD_OUT // n_i, D_MODEL), lambda q, i, c: (q, i, 0))],
            out_specs=[pl.BlockSpec((None, tr, SHARD_IN), lambda q, i, c: (q, i, 0)),
                       pl.BlockSpec((None, SHARD_OUT // n_i, D_MODEL), lambda q, i, c: (q, i, 0))]),
        out_shape=[jax.ShapeDtypeStruct(ra.shape, _BF), jax.ShapeDtypeStruct(rb.shape, F32)],
        compiler_params=_params(("arbitrary", "arbitrary")),
    )(c_idx, g_in, ra, g_out, rb)


_HBM = pl.BlockSpec(memory_space=pltpu.HBM)
_SEM = pl.BlockSpec(memory_space=pltpu.SEMAPHORE)
_N_CHIP_COPIES = 6


def _chip_copies(sb_ref, sbo_ref, rc_ref, rco_ref, send, recv):
    pos = _mesh_pos()
    copies = []
    for a, (src, dst) in enumerate(((sb_ref, rc_ref), (sbo_ref, rco_ref))):
        for j, chip in enumerate(_other_chips(pos)):
            copies.append(_remote(src.at[2 * chip[0] + chip[1]], dst.at[j], send.at[3 * a + j], recv.at[3 * a + j],
                                  (*chip, pos[2])))
    return copies


def grad_chip_start(sb, sbo):
    def body(sb_ref, sbo_ref, rc_ref, rco_ref, send, recv, sb_thru, sbo_thru, rc_thru, rco_thru, token):
        del sb_thru, sbo_thru, rc_thru, rco_thru
        for cp in _chip_copies(sb_ref, sbo_ref, rc_ref, rco_ref, send, recv):
            cp.start()
        token[...] = jnp.zeros_like(token)

    hbm = lambda a: pltpu.with_memory_space_constraint(a, pltpu.HBM)
    rc = lax.empty((3,) + sb.shape[1:], sb.dtype)
    rco = lax.empty((3,) + sbo.shape[1:], sbo.dtype)
    return pl.pallas_call(
        body, name="grad_chip_start",
        in_specs=[_HBM] * 4,
        out_specs=[_SEM, _SEM, _HBM, _HBM, _HBM, _HBM, pl.BlockSpec(memory_space=pltpu.VMEM)],
        out_shape=[pltpu.SemaphoreType.DMA((_N_CHIP_COPIES,)), pltpu.SemaphoreType.DMA((_N_CHIP_COPIES,)),
                   pltpu.HBM(sb.shape, sb.dtype), pltpu.HBM(sbo.shape, sbo.dtype),
                   pltpu.HBM(rc.shape, rc.dtype), pltpu.HBM(rco.shape, rco.dtype),
                   jax.ShapeDtypeStruct((8, LANE), F32)],
        input_output_aliases={0: 2, 1: 3, 2: 4, 3: 5},
        compiler_params=pltpu.CompilerParams(has_side_effects=pltpu.SideEffectType.DATAFLOW_SIDE_EFFECTING),
    )(hbm(sb), hbm(sbo), hbm(rc), hbm(rco))


def grad_chip_wait(send, recv, sb, sbo, rc, rco, after):
    def body(sb_ref, sbo_ref, rc_ref, rco_ref, send, recv, after_ref, sb_o, sbo_o, rc_o, rco_o):
        del after_ref, sb_o, sbo_o, rc_o, rco_o
        for cp in _chip_copies(sb_ref, sbo_ref, rc_ref, rco_ref, send, recv):
            cp.wait_send()
            cp.wait_recv()

    return pl.pallas_call(
        body, name="grad_chip_wait",
        in_specs=[_HBM] * 4 + [_SEM, _SEM, pl.BlockSpec(memory_space=pl.ANY)],
        out_specs=[_HBM] * 4,
        out_shape=[pltpu.HBM(sb.shape, sb.dtype), pltpu.HBM(sbo.shape, sbo.dtype),
                   pltpu.HBM(rc.shape, rc.dtype), pltpu.HBM(rco.shape, rco.dtype)],
        input_output_aliases={0: 0, 1: 1, 2: 2, 3: 3},
        compiler_params=pltpu.CompilerParams(has_side_effects=pltpu.SideEffectType.DATAFLOW_SIDE_EFFECTING),
    )(sb, sbo, rc, rco, send, recv, after)


def pack_gather(pack):
    def body(pack_ref, packs_ref, psend, precv):
        pos = _mesh_pos()
        me = _lin(pos)
        packs_ref[me] = pack_ref[...]
        peers = [_xor_peer(pos, k) for k in range(1, N_DEV)]
        gather = [_remote(packs_ref.at[me], packs_ref.at[me], psend.at[n], precv.at[n], p) for n, p in enumerate(peers)]
        for cp in gather:
            cp.start()
        for n, p in enumerate(peers):
            _remote(packs_ref.at[_lin(p)], packs_ref.at[_lin(p)], psend.at[n], precv.at[n], p).wait_recv()
        for cp in gather:
            cp.wait_send()

    vmem = pl.BlockSpec(memory_space=pltpu.VMEM)
    return pl.pallas_call(
        body, name="pack_gather", in_specs=[vmem], out_specs=vmem,
        out_shape=jax.ShapeDtypeStruct((N_DEV,) + pack.shape, F32),
        scratch_shapes=[pltpu.SemaphoreType.DMA((N_DEV - 1,)), pltpu.SemaphoreType.DMA((N_DEV - 1,))],
    )(pack)


def pack_rows(vec_mid, vec_ada, dlb):
    def body(mid_ref, ada_ref, dlb_ref, o_ref):
        mid = lambda r: mid_ref[r:r + 1, :]
        rows = [ada_ref[0:1, :], dlb_ref[...], mid(MID_HG_G), mid(MID_RET_G), mid(MID_FINAL_G),
                ada_ref[2:3, :], ada_ref[1:2, :], mid(MID_GATE), mid(MID_LOSS)]
        o_ref[...] = jnp.zeros_like(o_ref)
        for n, row in enumerate(rows):
            o_ref[n:n + 1, :] = row

    vmem = pl.BlockSpec(memory_space=pltpu.VMEM)
    return pl.pallas_call(body, name="pack_rows", in_specs=[vmem] * 3, out_specs=vmem,
                          out_shape=jax.ShapeDtypeStruct((PACK_ROWS, D_MODEL), F32))(vec_mid, vec_ada, dlb)


def _adamw(w, g, m, v):
    m = ADAM_B1 * m + (1.0 - ADAM_B1) * g
    v = ADAM_B2 * v + (1.0 - ADAM_B2) * (g * g)
    m_hat = m / (1.0 - ADAM_B1 ** ADAM_STEP)
    v_hat = v / (1.0 - ADAM_B2 ** ADAM_STEP)
    delta = -ADAM_LR * (m_hat / (jnp.sqrt(v_hat) + ADAM_EPS) + ADAM_WD * w)
    return delta, m, v


def adam_shard(chip_idx, own, parts, w, m, v, name):
    rows, cols = w.shape
    tr = min(rows, 256)

    def body(chip_ref, p0, p1, p2, p3, w_ref, m_ref, v_ref, g_ref, d_ref, nm_ref, nv_ref):
        del chip_ref
        g = ((p0[...].astype(F32) + p1[...].astype(F32)) + p2[...].astype(F32)) + p3[...].astype(F32)
        g_ref[...] = g
        d_ref[...], nm_ref[...], nv_ref[...] = _adamw(w_ref[...], g, m_ref[...], v_ref[...])

    part = lambda q: pl.BlockSpec((None, tr, cols), lambda i, chip, q=q: (q, i, 0))
    tile = pl.BlockSpec((tr, cols), lambda i, chip: (i, 0))
    return pl.pallas_call(
        body, name=name,
        grid_spec=pltpu.PrefetchScalarGridSpec(
            num_scalar_prefetch=1, grid=(rows // tr,),
            in_specs=[pl.BlockSpec((None, tr, cols), lambda i, chip: (chip[0], i, 0)), part(0), part(1), part(2),
                      tile, tile, tile],
            out_specs=[tile] * 4),
        out_shape=[jax.ShapeDtypeStruct(w.shape, F32)] * 4,
        compiler_params=_params(("arbitrary",)),
    )(chip_idx, own, parts, parts, parts, w, m, v)


def adam_ada(sc_t, dmod_all, me_idx, w, m, v):
    def body(me_ref, sc_ref, dm_ref, w_ref, m_ref, v_ref, g_ref, d_ref, nm_ref, nv_ref):
        del me_ref
        g = _dot_f32(sc_ref[...], dm_ref[...])
        g_ref[...] = g
        d_ref[...], nm_ref[...], nv_ref[...] = _adamw(w_ref[...], g, m_ref[...], v_ref[...])

    full = pl.BlockSpec(w.shape, lambda i, me: (0, 0))
    return pl.pallas_call(
        body, name="adam_ada",
        grid_spec=pltpu.PrefetchScalarGridSpec(
            num_scalar_prefetch=1, grid=(1,),
            in_specs=[pl.BlockSpec(sc_t.shape, lambda i, me: (0, 0)),
                      pl.BlockSpec((LANE, SHARD_ADA), lambda i, me: (0, me[0])), full, full, full],
            out_specs=[full] * 4),
        out_shape=[jax.ShapeDtypeStruct(w.shape, F32)] * 4,
        compiler_params=_params(("arbitrary",)),
    )(me_idx, sc_t, dmod_all, w, m, v)


def adam_vectors(packs, lb, params, ms, vs):
    n = len(params)

    def body(*refs):
        packs_ref, lb_ref = refs[0], refs[1]
        w_refs, m_refs, v_refs = refs[2:2 + n], refs[2 + n:2 + 2 * n], refs[2 + 2 * n:2 + 3 * n]
        loss_ref = refs[2 + 3 * n]
        outs = refs[3 + 3 * n:3 + 7 * n]
        tot_ref = refs[3 + 7 * n]
        tot = packs_ref[0]
        for d in range(1, N_DEV):
            tot = tot + packs_ref[d]
        tot_ref[...] = tot
        row = lambda r: tot_ref[r:r + 1, :]
        lbv = lb_ref[...]
        dl0 = row(ROW_LB) * lbv * (1.0 - lbv)
        grads = [[row(ROW_NORM_G)],
                 [jnp.concatenate([row(ROW_SHIFT), row(ROW_SCALE), row(ROW_GATE)], axis=1)],
                 [dl0, -dl0],
                 [row(ROW_HG_G)], [row(ROW_RET_G)], [row(ROW_FINAL_G)]]
        loss_ref[...] = tot_ref[ROW_LOSS:ROW_LOSS + 1, 0:LANE]
        for j, g_rows in enumerate(grads):
            for r, g in enumerate(g_rows):
                rs = slice(r, r + 1)
                d, nm, nv = _adamw(w_refs[j][rs, :], g, m_refs[j][rs, :], v_refs[j][rs, :])
                outs[4 * j][rs, :] = g
                outs[4 * j + 1][rs, :] = d
                outs[4 * j + 2][rs, :] = nm
                outs[4 * j + 3][rs, :] = nv

    vmem = pl.BlockSpec(memory_space=pltpu.VMEM)
    out_shape = [jax.ShapeDtypeStruct((1, LANE), F32)]
    for w in params:
        out_shape += [jax.ShapeDtypeStruct(w.shape, F32)] * 4
    return pl.pallas_call(
        body, name="adam_vectors", in_specs=[vmem] * (2 + 3 * n), out_specs=[vmem] * len(out_shape),
        out_shape=out_shape, scratch_shapes=[pltpu.VMEM((PACK_ROWS, D_MODEL), F32)],
    )(packs, lb, *params, *ms, *vs)


def kernel(x, c, norm_g, w_ada, b_ada, w_in, hg_lb_logits, hg_norm_g, ret_norm_g, w_out, final_g, loss_target, m_norm_g, m_w_ada, m_b_ada, m_w_in, m_hg_lb_logits, m_hg_norm_g, m_ret_norm_g, m_w_out, m_final_g, v_norm_g, v_w_ada, v_b_ada, v_w_in, v_hg_lb_logits, v_hg_norm_g, v_ret_norm_g, v_w_out, v_final_g):
    pos = _mesh_pos()
    me_idx = jnp.reshape(_lin(pos), (1,)).astype(jnp.int32)
    c_idx = jnp.reshape(pos[2], (1,)).astype(jnp.int32)
    vec = lambda a: a.reshape(1, D_MODEL)

    mod, scall, lb, wt_sh, wout_sh = pre_exchange(c, w_ada[0], b_ada, hg_lb_logits, w_in[0], w_out[0])
    chip_idx = jnp.reshape(2 * pos[0] + pos[1], (1,)).astype(jnp.int32)
    gather_send, gather_recv, wtg, woutg, gather_token = weight_gather_start(wt_sh, wout_sh)
    mod = mod + gather_token[:1, :1]

    other_chips = jnp.stack([2 * cx + cy for cx, cy in _other_chips(pos)]).astype(jnp.int32)

    def project(h):
        t_len = h.shape[0]
        flat_in = lambda a: a.reshape(D_IN, D_MODEL)
        wg, wog = weight_gather_wait(gather_send, gather_recv, wt_sh, wout_sh, wtg, woutg, h)
        pass_send, pass_recv, wg, wog = weight_pass_start(wg, wog)
        pb = proj_forward(h, flat_in(wg), chip_idx, t_len, "proj_fwd_own")
        wg, wog = weight_pass_wait(pass_send, pass_recv, wg, wog, pb)
        wg, wog = weight_pass_last(wg, wog)
        pb = proj_forward(h, flat_in(wg), other_chips, t_len, "proj_fwd_rest", pb)
        return pb, flat_in(wg), wog.reshape(D_MODEL, D_MODEL)

    def start_exchange(dwin, dwin_sib, dwout):
        dwout = dwout.reshape(N_DEV, SHARD_OUT, D_MODEL)
        ra, rb = grad_pair_exchange(dwin_sib, dwout)
        sb, sbo = pair_sum(dwin, ra, dwout, rb, c_idx)
        send, recv, sb, sbo, rc, rco, token = grad_chip_start(sb, sbo)
        return token, (send, recv, sb, sbo, rc, rco)

    grad_x, _, _, vec_mid, vec_ada, dlb, pending = device_step(
        x[0], loss_target[0], mod, lb, project, norm_g, hg_norm_g, ret_norm_g, vec(final_g), c_idx, start_exchange)
    packs = pack_gather(pack_rows(vec_mid, vec_ada, dlb))
    dmod_all = packs[:, ROW_SHIFT:ROW_GATE + 1, :].reshape(N_DEV, 3 * D_MODEL)
    dmod_all = jnp.pad(dmod_all, ((0, LANE - N_DEV), (0, 0)))
    sc_t = jnp.pad(scall.T, ((0, 0), (0, LANE - N_DEV)))
    g_ada, d_ada, nm_ada, nv_ada = adam_ada(sc_t, dmod_all, me_idx, w_ada[0], m_w_ada[0], v_w_ada[0])
    small = adam_vectors(
        packs, lb,
        (norm_g, b_ada, hg_lb_logits, hg_norm_g, ret_norm_g, vec(final_g)),
        (m_norm_g, m_b_ada, m_hg_lb_logits, m_hg_norm_g, m_ret_norm_g, vec(m_final_g)),
        (v_norm_g, v_b_ada, v_hg_lb_logits, v_hg_norm_g, v_ret_norm_g, vec(v_final_g)))
    loss = small[0][0, 0]
    sb, sbo, rc, rco = grad_chip_wait(*pending, small[0])
    g_in, d_in, nm_in, nv_in = adam_shard(chip_idx, sb, rc, w_in[0], m_w_in[0], v_w_in[0], "adam_w_in")
    g_out, d_out, nm_out, nv_out = adam_shard(chip_idx, sbo, rco, w_out[0], m_w_out[0], v_w_out[0], "adam_w_out")
    (g_ng, d_ng, nm_ng, nv_ng), (g_b, d_b, nm_b, nv_b), (g_lb, d_lb, nm_lb, nv_lb), (g_hg, d_hg, nm_hg, nv_hg), \
        (g_rg, d_rg, nm_rg, nv_rg), (g_fg, d_fg, nm_fg, nv_fg) = [small[1 + 4 * j:5 + 4 * j] for j in range(6)]
    flat = lambda a: a.reshape(D_MODEL)

    def group(ng, ada, b, win, lbl, hg, rg, wo, fg):
        return (ng, ada[None], b, win[None], lbl, hg, rg, wo[None], flat(fg))

    return (loss, grad_x[None],
            *group(g_ng, g_ada, g_b, g_in, g_lb, g_hg, g_rg, g_out, g_fg),
            *group(d_ng, d_ada, d_b, d_in, d_lb, d_hg, d_rg, d_out, d_fg),
            *group(nm_ng, nm_ada, nm_b, nm_in, nm_lb, nm_hg, nm_rg, nm_out, nm_fg),
            *group(nv_ng, nv_ada, nv_b, nv_in, nv_lb, nv_hg, nv_rg, nv_out, nv_fg))
```

```python
import numpy as np
import jax
import jax.numpy as jnp
from jax import lax
from jax.experimental import pallas as pl
from jax.experimental.pallas import tpu as pltpu

F32 = jnp.float32
_BF = jnp.bfloat16

D_MODEL = 1024
N_HEADS = 8
LANE = 128
RET_DK = 64
D_IN = 9216
N_DEV = 8
SHARD_IN = D_IN // N_DEV
SHARD_ADA = 3 * D_MODEL // N_DEV
SHARD_OUT = D_MODEL // N_DEV
N_CB = D_IN // LANE
CB_PER_SHARD = SHARD_IN // LANE
CHUNK = 128
N_LEVELS = 7
EPS = 1e-6
LOG2_E = float(np.log2(np.e))
ROPE_BASE = 10000.0
CB_HQ, CB_HF, CB_HI, CB_HZ, CB_RQ, CB_RK, CB_RV, CB_RZ, CB_GA, CB_GB = 0, 8, 16, 24, 32, 36, 40, 48, 56, 64
VMEM_LIMIT = 56 * 1024 * 1024

ADAM_LR, ADAM_B1, ADAM_B2, ADAM_EPS, ADAM_WD, ADAM_STEP = 0.001, 0.9, 0.999, 1e-08, 0.01, 10

_NN = (((1,), (0,)), ((), ()))
_NT = (((1,), (1,)), ((), ()))
_TN = (((0,), (0,)), ((), ()))
MESH = pl.DeviceIdType.MESH


def _dot(a, b, dims=_NN):
    return lax.dot_general(a.astype(_BF), b.astype(_BF), dims, preferred_element_type=F32)


def _split2(a):
    hi = a.astype(_BF)
    lo = (a - hi.astype(F32)).astype(_BF)
    return jnp.concatenate([hi, lo], axis=1)


def _dot_sel(sel, a):
    n = a.shape[1]
    r = lax.dot_general(sel.astype(_BF), _split2(a), _NN, preferred_element_type=F32)
    return r[:, :n] + r[:, n:]


def _dot_f32(a, b):
    def pieces(v):
        p1 = v.astype(_BF)
        r1 = v - p1.astype(F32)
        p2 = r1.astype(_BF)
        p3 = (r1 - p2.astype(F32)).astype(_BF)
        return p1, p2, p3
    a1, a2, a3 = pieces(a)
    b1, b2, b3 = pieces(b)
    d = lambda u, v: lax.dot_general(u, v, _NN, preferred_element_type=F32)
    return ((d(a1, b3) + d(a2, b2) + d(a3, b1)) + (d(a1, b2) + d(a2, b1))) + d(a1, b1)


def _sigmoid(v):
    return 1.0 / (1.0 + jnp.exp(-v))


def _params(sem=None):
    return pltpu.CompilerParams(dimension_semantics=sem, vmem_limit_bytes=VMEM_LIMIT)


def _hgrn_consts():
    c, nl = CHUNK, N_LEVELS
    t = np.arange(c)[:, None]
    j = np.arange(c)[None, :]
    sel = [j <= t]
    masks = [j == t]
    for l in range(1, nl + 1):
        m = ((t >> l) << l) + (1 << (l - 1)) - 1
        sec = t > m
        sel.append(np.where(sec, (j > m) & (j <= t), (j > t) & (j <= m)))
        same = (t >> l) == (j >> l)
        masks.append(same & sec & (j <= m))
    sel.append(j > t)
    sel = np.concatenate(sel, 0).astype(np.float32)
    masks = np.stack(masks).astype(np.float32)
    sgn = np.stack([np.where((t & (1 << (l - 1))) != 0, 1.0, -1.0) * np.ones((1, LANE)) for l in range(3, nl + 1)])
    return dict(tri=jnp.asarray(sel[:c], _BF),
                lvl=jnp.asarray(masks, F32),
                sgn=jnp.asarray(sgn, F32),
                sel_t=jnp.asarray(sel.T, _BF),
                lvl_b=jnp.asarray(masks, _BF),
                lvlt_b=jnp.asarray(np.swapaxes(masks, 1, 2), _BF))


def _level_exponents(b, logf, b_scr, sgn_ref):
    c = CHUNK
    b_scr[...] = b
    row = lax.broadcasted_iota(jnp.int32, (c, LANE), 0)
    nxt = pltpu.roll(logf, c - 1, 0)
    prv = pltpu.roll(logf, 1, 0)
    r4 = row & 3
    out = [jnp.where((row & 1) == 1, logf, 0.0),
           jnp.where(r4 == 0, nxt, jnp.where(r4 == 1, 0.0, jnp.where(r4 == 2, logf, logf + prv)))]
    for l in range(3, N_LEVELS + 1):
        size, half = 1 << l, 1 << (l - 1)
        ref = jnp.concatenate([jnp.broadcast_to(b_scr[i * size + half - 1:i * size + half, :], (size, LANE))
                               for i in range(c // size)], axis=0)
        out.append((b - ref) * sgn_ref[l - 3])
    return out


def _hgrn_chunk(hq, hf, hi, lbv, tri_ref, sgn_ref, b_scr):
    sq = _sigmoid(hq)
    q = hq * sq
    sg = _sigmoid(hf)
    omlb = 1.0 - lbv
    f = lbv + omlb * sg
    k = 1.0 - f
    logf = jnp.log(f) * LOG2_E
    b = _dot_sel(tri_ref[...], logf)
    bc = jnp.sum(logf, axis=0, keepdims=True)
    lev = [None] + [jnp.exp2(e) for e in _level_exponents(b, logf, b_scr, sgn_ref)]
    return dict(sq=sq, q=q, sg=sg, omlb=omlb, f=f, k=k, v=hi, eb=jnp.exp2(b), erem=jnp.exp2(bc - b),
                ebc=jnp.exp2(bc), lev=lev)


def _blockdiag(a, b):
    z = jnp.zeros_like(a)
    return jnp.concatenate([jnp.concatenate([a, z], axis=1), jnp.concatenate([z, b], axis=1)], axis=0)


def _level_operands(a):
    q, k = a["q"].astype(_BF), a["k"].astype(_BF)
    lev = [None] + [a["lev"][l].astype(_BF) for l in range(1, N_LEVELS + 1)]
    ql = [q] + [q * lev[l] for l in range(1, N_LEVELS + 1)]
    kl = [k] + [k * lev[l] for l in range(1, N_LEVELS + 1)]
    pairs = range(0, N_LEVELS + 1, 2)
    return ([jnp.concatenate([ql[l], ql[l + 1]], axis=1) for l in pairs], [_blockdiag(kl[l], kl[l + 1]) for l in pairs],
            ql, kl)


def _hgrn_scores(a, lvl_ref, q_pairs, k_diags):
    acc = None
    for n, (qp, kd) in enumerate(zip(q_pairs, k_diags)):
        both = lax.dot_general(qp, kd, _NT, preferred_element_type=F32)
        part = lvl_ref[2 * n] * both[:, :CHUNK] + lvl_ref[2 * n + 1] * both[:, CHUNK:]
        acc = part if acc is None else acc + part
    return acc


SCAN_UNROLL = 8
FWD_UNROLL = 16
RET_UNROLL = 16


def _writeback_reserve(step, make_copies):
    slot = step % 2

    @pl.when(step >= 2)
    def _():
        for cp in make_copies(slot):
            cp.wait()

    return slot


def _writeback_commit(step, n_steps, slot, make_copies):
    for cp in make_copies(slot):
        cp.start()

    @pl.when(step == n_steps - 1)
    def _():
        for cp in make_copies(slot):
            cp.wait()
        if n_steps > 1:
            for cp in make_copies(1 - slot):
                cp.wait()


def _resident(const):
    zeros = (0,) * const.ndim
    return pl.BlockSpec(const.shape, lambda p, t: zeros)


def _time_block(t_len):
    return min(t_len, 2048)


def hgrn_forward(pb, lb, t_len):
    nc = t_len // CHUNK
    tb = _time_block(t_len)
    ncb = tb // CHUNK
    consts = _hgrn_consts()
    operands = [consts[n] for n in ("tri", "lvl", "sgn")]

    def body(hq_ref, hf_ref, hi_ref, lb_ref, tri_ref, lvl_ref, sgn_ref, o_ref, ssave_ref, asave_ref, st_ref, b_scr):
        @pl.when(pl.program_id(1) == 0)
        def _():
            st_ref[...] = jnp.zeros_like(st_ref)

        def chunk(ci, carry):
            r = pl.ds(pl.multiple_of(ci * CHUNK, CHUNK), CHUNK)
            for hd in range(2):
                lbv = lb_ref[:, hd * LANE:(hd + 1) * LANE]
                a = _hgrn_chunk(hq_ref[hd, r, :], hf_ref[hd, r, :], hi_ref[hd, r, :], lbv, tri_ref, sgn_ref,
                                b_scr.at[hd])
                q_pairs, k_diags, _, _ = _level_operands(a)
                st = st_ref[hd]
                ssave_ref[hd, ci] = st
                scores = _hgrn_scores(a, lvl_ref, q_pairs, k_diags).astype(asave_ref.dtype)
                asave_ref[hd, ci] = scores
                o_ref[hd, r, :] = _dot(a["q"] * a["eb"], st, _NT) + _dot(scores, a["v"])
                st_ref[hd] = st * a["ebc"] + _dot(a["v"], a["k"] * a["erem"], _TN)
            return carry

        lax.fori_loop(0, ncb, chunk, 0, unroll=FWD_UNROLL)

    pair = lambda base: pl.BlockSpec((2, tb, LANE), lambda p, t, base=base: (base // 2 + p, t, 0))
    per_chunk = pl.BlockSpec((2, ncb, LANE, LANE), lambda p, t: (p, t, 0, 0))
    return pl.pallas_call(
        body, name="hgrn_fwd", grid=(N_HEADS // 2, t_len // tb),
        in_specs=[pair(CB_HQ), pair(CB_HF), pair(CB_HI),
                  pl.BlockSpec((1, 2 * LANE), lambda p, t: (0, p))] + [_resident(c) for c in operands],
        out_specs=[pl.BlockSpec((2, tb, LANE), lambda p, t: (p, t, 0)), per_chunk, per_chunk],
        out_shape=[jax.ShapeDtypeStruct((N_HEADS, t_len, LANE), F32),
                   jax.ShapeDtypeStruct((N_HEADS, nc, LANE, LANE), F32),
                   jax.ShapeDtypeStruct((N_HEADS, nc, CHUNK, CHUNK), _BF)],
        scratch_shapes=[pltpu.VMEM((2, LANE, LANE), F32), pltpu.VMEM((2, CHUNK, LANE), F32)],
        compiler_params=_params(("arbitrary", "arbitrary")),
    )(pb, pb, pb, lb, *operands)


def hgrn_backward(pb, lb, do, ssave, asave, dpb, t_len):
    tb = _time_block(t_len)
    ncb, ntb = tb // CHUNK, t_len // tb
    consts = _hgrn_consts()
    operands = [consts[n] for n in ("tri", "sgn", "sel_t", "lvl_b", "lvlt_b")]

    def body(hq_ref, hf_ref, hi_ref, lb_ref, do_ref, ssave_ref, asave_ref, tri_ref, sgn_ref, selt_ref, lvlb_ref,
             lvltb_ref, dpb_in, dpb_ref, dlb_ref, dq_buf, df_buf, di_buf, dst_ref, b_scr, sems):
        del dpb_in
        p, t = pl.program_id(0), pl.program_id(1)
        step = p * ntb + t
        rows = pl.ds(pl.multiple_of((ntb - 1 - t) * tb, tb), tb)

        def out_copies(sl):
            return [pltpu.make_async_copy(buf.at[sl], dpb_ref.at[pl.ds(base + 2 * p, 2), rows], sems.at[sl, n])
                    for n, (buf, base) in enumerate(((dq_buf, CB_HQ), (df_buf, CB_HF), (di_buf, CB_HI)))]

        slot = _writeback_reserve(step, out_copies)

        @pl.when(t == 0)
        def _():
            dst_ref[...] = jnp.zeros_like(dst_ref)
            dlb_ref[...] = jnp.zeros_like(dlb_ref)

        def chunk(i, carry):
            ci = ncb - 1 - i
            r = pl.ds(pl.multiple_of(ci * CHUNK, CHUNK), CHUNK)
            for hd in range(2):
                head_chunk(hd, ci, r)
            return carry

        def head_chunk(hd, ci, r):
            lbv = lb_ref[:, hd * LANE:(hd + 1) * LANE]
            hq = hq_ref[hd, r, :]
            a = _hgrn_chunk(hq, hf_ref[hd, r, :], hi_ref[hd, r, :], lbv, tri_ref, sgn_ref, b_scr.at[hd])
            _, k_diags, ql, kl = _level_operands(a)
            q, k, v = a["q"], a["k"], a["v"]
            g = do_ref[hd, r, :]
            st0 = ssave_ref[hd, ci]
            dst = dst_ref[hd]
            scores = asave_ref[hd, ci]
            da = _dot(g, v, _NT)
            da_t = _dot(v, g, _NT)
            kb = k * a["erem"]
            qb = q * a["eb"]
            dv = _dot(scores, g, _TN) + _dot(kb, dst, _NT)
            dq_inter = _dot(g, st0) * a["eb"]
            dk_state = _dot(v, dst) * a["erem"]
            dq, dk = dq_inter, dk_state
            de = [q * dq_inter]
            da_b, dat_b = da.astype(_BF), da_t.astype(_BF)
            for n in range(len(k_diags)):
                l0, l1 = 2 * n, 2 * n + 1
                da_pair = jnp.concatenate([lvlb_ref[l0] * da_b, lvlb_ref[l1] * da_b], axis=1)
                dat_pair = jnp.concatenate([lvltb_ref[l0] * dat_b, lvltb_ref[l1] * dat_b], axis=1)
                dq_both = lax.dot_general(da_pair, k_diags[n], _NN, preferred_element_type=F32)
                dk_both = lax.dot_general(dat_pair, _blockdiag(ql[l0], ql[l1]), _NN, preferred_element_type=F32)
                for l, cols in ((l0, slice(0, LANE)), (l1, slice(LANE, 2 * LANE))):
                    dql, dkl = dq_both[:, cols], dk_both[:, cols]
                    if l > 0:
                        e = a["lev"][l]
                        dql, dkl = dql * e, dkl * e
                        de.append(q * dql + k * dkl)
                    dq = dq + dql
                    dk = dk + dkl
            de.append(k * dk_state)
            dst_ref[hd] = dst * a["ebc"] + _dot(g, qb, _TN)
            dbc = jnp.sum(dst * st0, axis=0, keepdims=True) * a["ebc"]
            de2 = lax.dot_general(selt_ref[...], _split2(jnp.concatenate(de, axis=0)), _NN,
                                  preferred_element_type=F32)
            dlogf = de2[:, :LANE] + de2[:, LANE:] + dbc
            sq, sg = a["sq"], a["sg"]
            df = dlogf / a["f"] - dk
            dq_buf[slot, hd, r, :] = (dq * (sq * (1.0 + hq * (1.0 - sq)))).astype(dq_buf.dtype)
            df_buf[slot, hd, r, :] = (df * a["omlb"] * sg * (1.0 - sg)).astype(df_buf.dtype)
            di_buf[slot, hd, r, :] = dv.astype(di_buf.dtype)
            cols = slice(hd * LANE, (hd + 1) * LANE)
            dlb_ref[:, cols] = dlb_ref[:, cols] + jnp.sum(df * (1.0 - sg), axis=0, keepdims=True)

        lax.fori_loop(0, ncb, chunk, 0, unroll=SCAN_UNROLL)
        _writeback_commit(step, (N_HEADS // 2) * ntb, slot, out_copies)

    pair = lambda base: pl.BlockSpec((2, tb, LANE), lambda p, t, base=base: (base // 2 + p, ntb - 1 - t, 0))
    any_spec = pl.BlockSpec(memory_space=pl.ANY)
    per_chunk = pl.BlockSpec((2, ncb, LANE, LANE), lambda p, t: (p, ntb - 1 - t, 0, 0))
    return pl.pallas_call(
        body, name="hgrn_bwd", grid=(N_HEADS // 2, ntb),
        in_specs=[pair(CB_HQ), pair(CB_HF), pair(CB_HI),
                  pl.BlockSpec((1, 2 * LANE), lambda p, t: (0, p)),
                  pair(0), per_chunk, per_chunk]
        + [_resident(c) for c in operands] + [any_spec],
        out_specs=[any_spec, pl.BlockSpec((1, 2 * LANE), lambda p, t: (0, p))],
        out_shape=[jax.ShapeDtypeStruct(dpb.shape, dpb.dtype), jax.ShapeDtypeStruct((1, D_MODEL), F32)],
        scratch_shapes=[pltpu.VMEM((2, 2, tb, LANE), dpb.dtype)] * 3 + [
            pltpu.VMEM((2, LANE, LANE), F32), pltpu.VMEM((2, CHUNK, LANE), F32), pltpu.SemaphoreType.DMA((2, 3))],
        input_output_aliases={7 + len(operands): 0},
        compiler_params=_params(("arbitrary", "arbitrary")),
    )(pb, pb, pb, lb, do, ssave, asave, *operands, dpb)


def _rope_tables(t_len):
    half = RET_DK // 2
    inv_freq = (1.0 / (np.float32(ROPE_BASE) ** np.linspace(0.0, 1.0, half, dtype=np.float32))).astype(np.float32)
    ang = (np.arange(t_len, dtype=np.float32)[:, None] * inv_freq[None, :]).astype(np.float64)
    cos, sin = np.cos(ang).astype(np.float32), np.sin(ang).astype(np.float32)
    cos_t = np.concatenate([cos, cos, cos, cos], axis=1)
    sin_t = np.concatenate([-sin, sin, -sin, sin], axis=1)
    return jnp.asarray(cos_t), jnp.asarray(sin_t)


def _swap_halves(v):
    half = RET_DK // 2
    lane = lax.broadcasted_iota(jnp.int32, v.shape, 1)
    first = (lane & (RET_DK - 1)) < half
    return jnp.where(first, pltpu.roll(v, LANE - half, 1), pltpu.roll(v, half, 1))


def _ret_head_consts(hidx):
    c = CHUNK
    hf = jnp.full((1, LANE), hidx, jnp.int32).astype(F32)
    lg = jnp.log(1.0 - jnp.exp(-(5.0 + hf) * np.float32(np.log(2.0))))
    row = lax.broadcasted_iota(jnp.int32, (c, c), 0)
    col = lax.broadcasted_iota(jnp.int32, (c, c), 1)
    rel = (row - col).astype(F32)
    dm = jnp.where(rel >= 0, jnp.exp(lg[:, :1] * jnp.maximum(rel, 0.0)), 0.0)
    dm_t = jnp.where(rel <= 0, jnp.exp(lg[:, :1] * jnp.maximum(-rel, 0.0)), 0.0)
    idx = lax.broadcasted_iota(jnp.int32, (c, LANE), 0).astype(F32)
    zeta = jnp.exp(lg * (c - 1.0 - idx))
    xi = jnp.exp(lg * (idx + 1.0))
    cdec = jnp.exp(lg * float(c))
    return dm, zeta, xi, cdec, dm_t


def _lane_mask(which):
    lane = lax.broadcasted_iota(jnp.int32, (1, LANE), 1)
    return ((lane // RET_DK) == which).astype(F32)


def retention_forward(pb, cos_t, sin_t, t_len):
    nc = t_len // CHUNK

    tb = _time_block(t_len)
    ncb = tb // CHUNK

    def body(rq_ref, rk_ref, rv_ref, cos_ref, sin_ref, o_ref, rsave_ref, st_ref):
        p = pl.program_id(0)

        @pl.when(pl.program_id(1) == 0)
        def _():
            st_ref[...] = jnp.zeros_like(st_ref)

        consts = [_ret_head_consts(2 * p + hd) for hd in range(2)]

        def chunk(ci, carry):
            r = pl.ds(pl.multiple_of(ci * CHUNK, CHUNK), CHUNK)
            cs, sn = cos_ref[r, :], sin_ref[r, :]
            q = rq_ref[r, :]
            k = rk_ref[r, :]
            q = q * cs + _swap_halves(q) * sn
            k = (k * cs + _swap_halves(k) * sn) * RET_DK ** -0.5
            for hd in range(2):
                dm, zeta, xi, cdec, _ = consts[hd]
                lm = _lane_mask(hd)
                qh, kh = q * lm, k * lm
                v = rv_ref[hd, r, :]
                st = st_ref[hd]
                rsave_ref[hd, ci] = st
                scores = _dot(qh, kh, _NT) * dm
                o_ref[hd, r, :] = _dot(scores, v) + _dot(qh * xi, st, _NT)
                st_ref[hd] = st * cdec + _dot(v, kh * zeta, _TN)
            return carry

        lax.fori_loop(0, ncb, chunk, 0, unroll=RET_UNROLL)

    return pl.pallas_call(
        body, name="ret_fwd", grid=(N_HEADS // 2, t_len // tb),
        in_specs=[pl.BlockSpec((None, tb, LANE), lambda p, t: (CB_RQ + p, t, 0)),
                  pl.BlockSpec((None, tb, LANE), lambda p, t: (CB_RK + p, t, 0)),
                  pl.BlockSpec((2, tb, LANE), lambda p, t: (CB_RV // 2 + p, t, 0)),
                  pl.BlockSpec((tb, LANE), lambda p, t: (t, 0)),
                  pl.BlockSpec((tb, LANE), lambda p, t: (t, 0))],
        out_specs=[pl.BlockSpec((2, tb, LANE), lambda p, t: (p, t, 0)),
                   pl.BlockSpec((2, ncb, LANE, LANE), lambda p, t: (p, t, 0, 0))],
        out_shape=[jax.ShapeDtypeStruct((N_HEADS, t_len, LANE), F32),
                   jax.ShapeDtypeStruct((N_HEADS, nc, LANE, LANE), F32)],
        scratch_shapes=[pltpu.VMEM((2, LANE, LANE), F32)],
        compiler_params=_params(("arbitrary", "arbitrary")),
    )(pb, pb, pb, cos_t, sin_t)


def retention_backward(pb, cos_t, sin_t, do, rsave, dpb, t_len):
    tb = _time_block(t_len)
    ncb, ntb = tb // CHUNK, t_len // tb

    def body(rq_ref, rk_ref, rv_ref, cos_ref, sin_ref, do_ref, rsave_ref, dpb_in,
             dpb_ref, dq_buf, dk_buf, dv_buf, dst_ref, sems):
        del dpb_in
        p, t = pl.program_id(0), pl.program_id(1)
        step = p * ntb + t
        rows = pl.ds(pl.multiple_of((ntb - 1 - t) * tb, tb), tb)

        def out_copies(sl):
            return [pltpu.make_async_copy(dq_buf.at[sl], dpb_ref.at[CB_RQ + p, rows], sems.at[sl, 0]),
                    pltpu.make_async_copy(dk_buf.at[sl], dpb_ref.at[CB_RK + p, rows], sems.at[sl, 1]),
                    pltpu.make_async_copy(dv_buf.at[sl], dpb_ref.at[pl.ds(CB_RV + 2 * p, 2), rows], sems.at[sl, 2])]

        slot = _writeback_reserve(step, out_copies)

        @pl.when(t == 0)
        def _():
            dst_ref[...] = jnp.zeros_like(dst_ref)

        consts = [_ret_head_consts(2 * p + hd) for hd in range(2)]

        def chunk(i, carry):
            ci = ncb - 1 - i
            r = pl.ds(pl.multiple_of(ci * CHUNK, CHUNK), CHUNK)
            cs, sn = cos_ref[r, :], sin_ref[r, :]
            q = rq_ref[r, :]
            k = rk_ref[r, :]
            q = q * cs + _swap_halves(q) * sn
            k = (k * cs + _swap_halves(k) * sn) * RET_DK ** -0.5
            dq, dk = None, None
            for hd in range(2):
                dm, zeta, xi, cdec, dm_t = consts[hd]
                lm = _lane_mask(hd)
                qh, kh = q * lm, k * lm
                v = rv_ref[hd, r, :]
                g = do_ref[hd, r, :]
                st0 = rsave_ref[hd, ci]
                dst = dst_ref[hd]
                scores_t = _dot(kh, qh, _NT) * dm_t
                dsc = _dot(g, v, _NT) * dm
                dsc_t = _dot(v, g, _NT) * dm_t
                dqh = _dot(dsc, kh) + _dot(g, st0) * xi
                dkh = _dot(dsc_t, qh) + _dot(v, dst) * zeta
                dv_buf[slot, hd, r, :] = (_dot(scores_t, g) + _dot(kh * zeta, dst, _NT)).astype(dv_buf.dtype)
                dst_ref[hd] = dst * cdec + _dot(g, qh * xi, _TN)
                dq = dqh if dq is None else dq + dqh
                dk = dkh if dk is None else dk + dkh
            dk = dk * (RET_DK ** -0.5)
            dq_buf[slot, r, :] = (dq * cs - _swap_halves(dq) * sn).astype(dq_buf.dtype)
            dk_buf[slot, r, :] = (dk * cs - _swap_halves(dk) * sn).astype(dk_buf.dtype)
            return carry

        lax.fori_loop(0, ncb, chunk, 0, unroll=RET_UNROLL)
        _writeback_commit(step, (N_HEADS // 2) * ntb, slot, out_copies)

    any_spec = pl.BlockSpec(memory_space=pl.ANY)
    return pl.pallas_call(
        body, name="ret_bwd", grid=(N_HEADS // 2, ntb),
        in_specs=[pl.BlockSpec((None, tb, LANE), lambda p, t: (CB_RQ + p, ntb - 1 - t, 0)),
                  pl.BlockSpec((None, tb, LANE), lambda p, t: (CB_RK + p, ntb - 1 - t, 0)),
                  pl.BlockSpec((2, tb, LANE), lambda p, t: (CB_RV // 2 + p, ntb - 1 - t, 0)),
                  pl.BlockSpec((tb, LANE), lambda p, t: (ntb - 1 - t, 0)),
                  pl.BlockSpec((tb, LANE), lambda p, t: (ntb - 1 - t, 0)),
                  pl.BlockSpec((2, tb, LANE), lambda p, t: (p, ntb - 1 - t, 0)),
                  pl.BlockSpec((2, ncb, LANE, LANE), lambda p, t: (p, ntb - 1 - t, 0, 0)),
                  any_spec],
        out_specs=any_spec,
        out_shape=jax.ShapeDtypeStruct(dpb.shape, dpb.dtype),
        scratch_shapes=[pltpu.VMEM((2, tb, LANE), dpb.dtype), pltpu.VMEM((2, tb, LANE), dpb.dtype),
                        pltpu.VMEM((2, 2, tb, LANE), dpb.dtype), pltpu.VMEM((2, LANE, LANE), F32),
                        pltpu.SemaphoreType.DMA((2, 3))],
        input_output_aliases={7: 0},
        compiler_params=_params(("arbitrary", "arbitrary")),
    )(pb, pb, pb, cos_t, sin_t, do, rsave, dpb)


def _row_tile(t_len, want):
    return min(want, t_len)


PAIR_CB = 2 * CB_PER_SHARD


def proj_forward(h, wt, chips, t_len, name, pb=None):
    tm = _row_tile(t_len, 1024)

    def body(chips_ref, h_ref, w_ref, *rest):
        del chips_ref
        o_ref = rest[-1]
        acc = _dot(h_ref[...], w_ref[...], _NT)
        for jj in range(PAIR_CB):
            o_ref[jj] = acc[:, jj * LANE:(jj + 1) * LANE]

    given = [] if pb is None else [pb]
    return pl.pallas_call(
        body, name=name,
        grid_spec=pltpu.PrefetchScalarGridSpec(
            num_scalar_prefetch=1, grid=(chips.shape[0], t_len // tm),
            in_specs=[pl.BlockSpec((tm, D_MODEL), lambda j, i, ch: (i, 0)),
                      pl.BlockSpec((PAIR_CB * LANE, D_MODEL), lambda j, i, ch: (ch[j], 0))]
            + [pl.BlockSpec(memory_space=pl.ANY)] * len(given),
            out_specs=pl.BlockSpec((PAIR_CB, tm, LANE), lambda j, i, ch: (ch[j], i, 0))),
        out_shape=jax.ShapeDtypeStruct((N_CB, t_len, LANE), F32),
        input_output_aliases={3: 0} if given else {},
        compiler_params=_params(("arbitrary", "arbitrary")),
    )(chips, h, wt, *given)


def proj_backward_input(dpb, wt, token, x, dy, norm_g, scale1p, t_len):
    tm = _row_tile(t_len, 512)

    def body(a_ref, wt_hbm, token_ref, x_ref, dy_ref, g_ref, sc_ref, gx_ref, vec_ref, w_ref, sem):
        del token_ref
        i = pl.program_id(0)

        @pl.when(i == 0)
        def _():
            cp = pltpu.make_async_copy(wt_hbm, w_ref, sem)
            cp.start()
            cp.wait()

        a = jnp.concatenate([a_ref[jj].astype(_BF) for jj in range(N_CB)], axis=1)
        dhv = _dot(a, w_ref[...])
        xv, g, sc = x_ref[...], g_ref[...], sc_ref[...]
        r = lax.rsqrt(jnp.mean(xv * xv, axis=-1, keepdims=True) + EPS)
        xn = xv * r
        dxn = dhv * (g * sc)
        gx_ref[...] = dy_ref[...] + r * dxn - xn * (r * r) * jnp.mean(xv * dxn, axis=-1, keepdims=True)
        t = dhv * xn
        _acc_rows(vec_ref, i, [jnp.sum(t * sc, axis=0, keepdims=True),
                               jnp.sum(t * g, axis=0, keepdims=True),
                               jnp.sum(dhv, axis=0, keepdims=True)])

    row = pl.BlockSpec((tm, D_MODEL), lambda i: (i, 0))
    return pl.pallas_call(
        body, name="proj_bwd_input", grid=(t_len // tm,),
        in_specs=[pl.BlockSpec((N_CB, tm, LANE), lambda i: (0, i, 0)),
                  pl.BlockSpec(memory_space=pl.ANY),
                  pl.BlockSpec(token.shape, lambda i: (0, 0)),
                  row, row, _vec_spec(), _vec_spec()],
        out_specs=[row, pl.BlockSpec((8, D_MODEL), lambda i: (0, 0))],
        out_shape=[jax.ShapeDtypeStruct((t_len, D_MODEL), F32), jax.ShapeDtypeStruct((8, D_MODEL), F32)],
        scratch_shapes=[pltpu.VMEM(wt.shape, wt.dtype), pltpu.SemaphoreType.DMA],
        compiler_params=_params(("arbitrary",)),
    )(dpb, wt, token, x, dy, norm_g, scale1p)


def proj_backward_weight(h_t, dpb, t_len):
    tk = _row_tile(t_len, 2048)

    def body(h_ref, b_ref, o_ref):
        k = pl.program_id(1)
        b = jnp.concatenate([b_ref[jj].astype(_BF) for jj in range(PAIR_CB)], axis=1)
        part = _dot(h_ref[...], b)

        @pl.when(k == 0)
        def _():
            for s in range(2):
                o_ref[s] = part[:, s * SHARD_IN:(s + 1) * SHARD_IN]

        @pl.when(k > 0)
        def _():
            for s in range(2):
                o_ref[s] = o_ref[s] + part[:, s * SHARD_IN:(s + 1) * SHARD_IN]

    return pl.pallas_call(
        body, name="proj_bwd_weight", grid=(N_DEV // 2, t_len // tk),
        in_specs=[pl.BlockSpec((D_MODEL, tk), lambda j, k: (0, k)),
                  pl.BlockSpec((PAIR_CB, tk, LANE), lambda j, k: (j, k, 0))],
        out_specs=pl.BlockSpec((2, D_MODEL, SHARD_IN), lambda j, k: (j, 0, 0)),
        out_shape=jax.ShapeDtypeStruct((N_DEV, D_MODEL, SHARD_IN), F32),
        compiler_params=_params(("arbitrary", "arbitrary")),
    )(h_t, dpb)


def sibling_blocks(g_in, c_idx):
    tr = D_MODEL

    def body(c_ref, g_ref, o_ref):
        del c_ref
        o_ref[...] = g_ref[...].astype(o_ref.dtype)

    return pl.pallas_call(
        body, name="sibling_blocks",
        grid_spec=pltpu.PrefetchScalarGridSpec(
            num_scalar_prefetch=1, grid=(N_DEV // 2, D_MODEL // tr),
            in_specs=[pl.BlockSpec((None, tr, SHARD_IN), lambda q, i, c: (2 * q + 1 - c[0], i, 0))],
            out_specs=pl.BlockSpec((None, tr, SHARD_IN), lambda q, i, c: (q, i, 0))),
        out_shape=jax.ShapeDtypeStruct((N_DEV // 2, D_MODEL, SHARD_IN), _BF),
        compiler_params=_params(("arbitrary", "arbitrary")),
    )(c_idx, g_in)


def _vec_spec():
    return pl.BlockSpec((1, D_MODEL), lambda i: (0, 0))


def _acc_rows(ref, i, rows):
    @pl.when(i == 0)
    def _():
        ref[...] = jnp.zeros_like(ref)

    for n, row in enumerate(rows):
        ref[n:n + 1, :] = ref[n:n + 1, :] + row


def adaln_forward(x, norm_g, scale1p, shift, t_len):
    tm = _row_tile(t_len, 1024)

    def body(x_ref, g_ref, sc_ref, sh_ref, h_ref, ht_ref):
        xv = x_ref[...]
        r = lax.rsqrt(jnp.mean(xv * xv, axis=-1, keepdims=True) + EPS)
        h = xv * r * g_ref[...] * sc_ref[...] + sh_ref[...]
        h_ref[...] = h.astype(h_ref.dtype)
        ht_ref[...] = h.T.astype(ht_ref.dtype)

    return pl.pallas_call(
        body, name="adaln_fwd", grid=(t_len // tm,),
        in_specs=[pl.BlockSpec((tm, D_MODEL), lambda i: (i, 0)), _vec_spec(), _vec_spec(), _vec_spec()],
        out_specs=[pl.BlockSpec((tm, D_MODEL), lambda i: (i, 0)), pl.BlockSpec((D_MODEL, tm), lambda i: (0, i))],
        out_shape=[jax.ShapeDtypeStruct((t_len, D_MODEL), _BF), jax.ShapeDtypeStruct((D_MODEL, t_len), _BF)],
        compiler_params=_params(("arbitrary",)),
    )(x, norm_g, scale1p, shift)


def _head_norm(o, g):
    r = lax.rsqrt(jnp.mean(o * o, axis=-1, keepdims=True) + EPS)
    return r, o * r * g


def _group_spec(tm, cb):
    return pl.BlockSpec((N_HEADS, tm, LANE), lambda i, cb=cb: (cb // N_HEADS, i, 0))


MID_FINAL_G, MID_GATE, MID_LOSS, MID_HG_G, MID_RET_G = range(5)


def middle(x, target, oa, ob, pb, wout, gate, final_g, hg_g, ret_g, t_len):
    tm = _row_tile(t_len, 256)
    n_steps = t_len // tm

    def body(x_ref, t_ref, oa_ref, ob_ref, hz_ref, rz_ref, ga_ref, gb_ref, w_ref, gate_ref, fg_ref, hg_ref, rg_ref,
             dy_ref, doa_ref, dob_ref, dw_ref, vec_ref, dpb_ref, m_scr, dm_scr, keep, bufs, sems):
        i = pl.program_id(0)
        rows = pl.ds(pl.multiple_of(i * tm, tm), tm)

        def group_copies(sl):
            return [pltpu.make_async_copy(bufs.at[sl, n], dpb_ref.at[pl.ds(cb, N_HEADS), rows], sems.at[sl, n])
                    for n, cb in enumerate((CB_HZ, CB_RZ, CB_GA, CB_GB))]
        sides = ((oa_ref, hz_ref, ga_ref, hg_ref, doa_ref), (ob_ref, rz_ref, gb_ref, rg_ref, dob_ref))
        for hh in range(N_HEADS):
            ls = slice(hh * LANE, (hh + 1) * LANE)
            acc = None
            for side, (o_ref, z_ref, gt_ref, g_ref, _) in enumerate(sides):
                o = o_ref[hh]
                rr = lax.rsqrt(jnp.mean(o * o, axis=-1, keepdims=True) + EPS)
                orr = o * rr
                zz = z_ref[hh]
                sz = _sigmoid(zz)
                sgt = _sigmoid(gt_ref[hh])
                keep[side, hh, 0] = orr
                keep[side, hh, 1] = sz
                keep[side, hh, 2] = sgt
                keep[side, hh, 3] = jnp.broadcast_to(rr, orr.shape)
                u = sgt * ((orr * g_ref[:, ls]) * (zz * sz))
                acc = u if acc is None else acc + u
            m_scr[:, ls] = acc.astype(m_scr.dtype)
        zv = _dot(m_scr[...], w_ref[...])
        gt, fg = gate_ref[...], fg_ref[...]
        y = x_ref[...] + gt * zv
        r = lax.rsqrt(jnp.mean(y * y, axis=-1, keepdims=True) + EPS)
        yn = y * r
        err = yn * fg - t_ref[...]
        loss = 0.5 * jnp.sum(jnp.mean(err * err, axis=-1, keepdims=True), axis=0, keepdims=True)
        dout = err * (1.0 / D_MODEL)
        gd = dout * fg
        dy = r * gd - yn * (r * r) * jnp.mean(y * gd, axis=-1, keepdims=True)
        dy_ref[...] = dy
        dz = (dy * gt).astype(_BF)
        dm_scr[...] = _dot(dz, w_ref[...], _NT)
        part = _dot(m_scr[...], dz, _TN)

        @pl.when(i == 0)
        def _():
            dw_ref[...] = part

        @pl.when(i > 0)
        def _():
            dw_ref[...] = dw_ref[...] + part

        slot = _writeback_reserve(i, group_copies)
        dg = [[], []]
        for hh in range(N_HEADS):
            ls = slice(hh * LANE, (hh + 1) * LANE)
            dmh = dm_scr[:, ls]
            for side, (o_ref, z_ref, gt_ref, g_ref, do_ref) in enumerate(sides):
                zz, g = z_ref[hh], g_ref[:, ls]
                orr, sz, sgt, rr = keep[side, hh, 0], keep[side, hh, 1], keep[side, hh, 2], keep[side, hh, 3]
                n = orr * g
                silu = zz * sz
                du = dmh * sgt
                bufs[slot, 2 + side, hh] = (dmh * (n * silu) * (sgt * (1.0 - sgt))).astype(bufs.dtype)
                bufs[slot, side, hh] = (du * n * (sz * (1.0 + zz * (1.0 - sz)))).astype(bufs.dtype)
                dn = du * silu
                dg[side].append(jnp.sum(dn * orr, axis=0, keepdims=True))
                gdn = dn * g
                do_ref[hh] = (rr * (gdn - orr * jnp.mean(orr * gdn, axis=-1, keepdims=True))).astype(do_ref.dtype)
        _acc_rows(vec_ref, i, [jnp.sum(dout * yn, axis=0, keepdims=True),
                               jnp.sum(dy * zv, axis=0, keepdims=True),
                               jnp.broadcast_to(loss, (1, D_MODEL)),
                               jnp.concatenate(dg[0], axis=1), jnp.concatenate(dg[1], axis=1)])
        _writeback_commit(i, n_steps, slot, group_copies)

    row = pl.BlockSpec((tm, D_MODEL), lambda i: (i, 0))
    head = pl.BlockSpec((N_HEADS, tm, LANE), lambda i: (0, i, 0))
    full = pl.BlockSpec((D_MODEL, D_MODEL), lambda i: (0, 0))
    return pl.pallas_call(
        body, name="middle", grid=(n_steps,),
        in_specs=[row, row, head, head, _group_spec(tm, CB_HZ), _group_spec(tm, CB_RZ), _group_spec(tm, CB_GA),
                  _group_spec(tm, CB_GB), full, _vec_spec(), _vec_spec(), _vec_spec(), _vec_spec()],
        out_specs=[row, head, head, full, pl.BlockSpec((8, D_MODEL), lambda i: (0, 0)),
                   pl.BlockSpec(memory_space=pl.ANY)],
        out_shape=[jax.ShapeDtypeStruct((t_len, D_MODEL), F32),
                   jax.ShapeDtypeStruct((N_HEADS, t_len, LANE), _BF),
                   jax.ShapeDtypeStruct((N_HEADS, t_len, LANE), _BF),
                   jax.ShapeDtypeStruct((D_MODEL, D_MODEL), F32),
                   jax.ShapeDtypeStruct((8, D_MODEL), F32),
                   jax.ShapeDtypeStruct((N_CB, t_len, LANE), _BF)],
        scratch_shapes=[pltpu.VMEM((tm, D_MODEL), _BF), pltpu.VMEM((tm, D_MODEL), F32),
                        pltpu.VMEM((2, N_HEADS, 4, tm, LANE), F32),
                        pltpu.VMEM((2, 4, N_HEADS, tm, LANE), _BF), pltpu.SemaphoreType.DMA((2, 4))],
        compiler_params=_params(("arbitrary",)),
    )(x, target, oa, ob, pb, pb, pb, pb, wout, gate, final_g, hg_g, ret_g)


def device_step(x, target, mod, lb, project, norm_g, hg_g, ret_g, final_g, c_idx=None, start_exchange=None):
    t_len = x.shape[0]
    shift, scale, gate = mod[:, :D_MODEL], mod[:, D_MODEL:2 * D_MODEL], mod[:, 2 * D_MODEL:]
    scale1p = 1.0 + scale
    cos_t, sin_t = _rope_tables(t_len)
    h, h_t = adaln_forward(x, norm_g, scale1p, shift, t_len)
    pb, wt, wout = project(h)
    oa, ssave, asave = hgrn_forward(pb, lb, t_len)
    ob, rsave = retention_forward(pb, cos_t, sin_t, t_len)
    dy, doa, dob, dwout, vec_mid, dpb = middle(x, target, oa, ob, pb, wout, gate, final_g, hg_g, ret_g, t_len)
    dpb, dlb = hgrn_backward(pb, lb, doa, ssave, asave, dpb, t_len)
    dpb = retention_backward(pb, cos_t, sin_t, dob, rsave, dpb, t_len)
    c_idx = jnp.zeros((1,), jnp.int32) if c_idx is None else c_idx
    dwin = proj_backward_weight(h_t, dpb, t_len)
    dwin_sib = sibling_blocks(dwin, c_idx)
    token, pending = (start_exchange(dwin, dwin_sib, dwout) if start_exchange
                      else (jnp.zeros((8, LANE), F32), None))
    grad_x, vec_ada = proj_backward_input(dpb, wt, token, x, dy, norm_g, scale1p, t_len)
    return grad_x, dwin, dwout, vec_mid, vec_ada, dlb, pending


PACK_ROWS = 16
ROW_NORM_G, ROW_LB, ROW_HG_G, ROW_RET_G, ROW_FINAL_G, ROW_SHIFT, ROW_SCALE, ROW_GATE, ROW_LOSS = range(9)


def _mesh_pos():
    return lax.axis_index("x"), lax.axis_index("y"), lax.axis_index("c")


def _lin(pos):
    return 4 * pos[0] + 2 * pos[1] + pos[2]


def _xor_peer(pos, k):
    return tuple(1 - p if (k >> s) & 1 else p for p, s in zip(pos, (2, 1, 0)))


def _other_chips(pos):
    x, y, _ = pos
    return [(1 - x, y), (x, 1 - y), (1 - x, 1 - y)]


def _remote(src, dst, send_sem, recv_sem, to):
    return pltpu.make_async_remote_copy(src_ref=src, dst_ref=dst, send_sem=send_sem, recv_sem=recv_sem,
                                        device_id=to, device_id_type=MESH)


def pre_exchange(c, w_ada, b_ada, logits, w_in, w_out):
    def body(c_ref, wada_ref, bada_ref, logit_ref, win_ref, wout_ref, mod_ref, scall_ref, lb_ref, wg_ref, wog_ref,
             cg_ref, modall_ref, parts_ref, wt_ref, wo_ref, send1, recv1, send2, recv2, local):
        pos = _mesh_pos()
        placed = [pltpu.make_async_copy(wt_ref, wg_ref.at[_lin(pos)], local.at[0]),
                  pltpu.make_async_copy(wo_ref, wog_ref.at[_lin(pos)], local.at[1])]
        cv = c_ref[...]
        slot = lambda p: pl.ds(pl.multiple_of(8 * _lin(p), 8), 8)
        cg_ref[slot(pos), :] = jnp.broadcast_to(cv * _sigmoid(cv), (8, D_MODEL))
        lb_ref[...] = _sigmoid(logit_ref[0:1, :] - logit_ref[1:2, :])
        peers = [_xor_peer(pos, k) for k in range(1, N_DEV)]
        gather = [_remote(cg_ref.at[slot(pos)], cg_ref.at[slot(pos)], send1.at[n], recv1.at[n], p)
                  for n, p in enumerate(peers)]
        for cp in gather:
            cp.start()
        wt_ref[...] = win_ref[...].T.astype(wt_ref.dtype)
        wo_ref[...] = wout_ref[...].astype(wo_ref.dtype)
        for cp in placed:
            cp.start()
        for n, p in enumerate(peers):
            _remote(cg_ref.at[slot(p)], cg_ref.at[slot(p)], send1.at[n], recv1.at[n], p).wait_recv()
        modall_ref[...] = _dot(cg_ref[...], wada_ref[...])
        scatter = [_remote(modall_ref.at[slot(p)], parts_ref.at[slot(pos)], send2.at[n], recv2.at[n], p)
                   for n, p in enumerate(peers)]
        for cp in scatter:
            cp.start()
        parts_ref[slot(pos), :] = modall_ref[slot(pos), :]
        for n, p in enumerate(peers):
            _remote(modall_ref.at[slot(p)], parts_ref.at[slot(p)], send2.at[n], recv2.at[n], p).wait_recv()
        for cp in gather + scatter:
            cp.wait_send()
        for j in range(N_DEV):
            cols = slice(j * SHARD_ADA, (j + 1) * SHARD_ADA)
            mod_ref[:, cols] = parts_ref[8 * j:8 * j + 1, :] + bada_ref[:, cols]
            scall_ref[j:j + 1, :] = cg_ref[8 * j:8 * j + 1, :]
        for cp in placed:
            cp.wait()

    vmem = pl.BlockSpec(memory_space=pltpu.VMEM)
    return pl.pallas_call(
        body, name="pre_exchange",
        in_specs=[vmem] * 6, out_specs=[vmem] * 3 + [_HBM] * 2,
        out_shape=[jax.ShapeDtypeStruct((1, 3 * D_MODEL), F32), jax.ShapeDtypeStruct((N_DEV, D_MODEL), F32),
                   jax.ShapeDtypeStruct((1, D_MODEL), F32),
                   jax.ShapeDtypeStruct((N_DEV,) + w_in.shape[::-1], _BF),
                   jax.ShapeDtypeStruct((N_DEV,) + w_out.shape, _BF)],
        scratch_shapes=[pltpu.VMEM((N_DEV * 8, D_MODEL), F32), pltpu.VMEM((N_DEV * 8, SHARD_ADA), F32),
                        pltpu.VMEM((N_DEV * 8, SHARD_ADA), F32),
                        pltpu.VMEM(w_in.shape[::-1], _BF), pltpu.VMEM(w_out.shape, _BF)]
        + [pltpu.SemaphoreType.DMA((N_DEV - 1,))] * 4 + [pltpu.SemaphoreType.DMA((2,))],
        compiler_params=pltpu.CompilerParams(vmem_limit_bytes=VMEM_LIMIT),
    )(c, w_ada, b_ada, logits, w_in, w_out)


def _gather_copies(outs, sems):
    pos = _mesh_pos()
    x, y, c = pos
    sibling = (x, y, 1 - c)

    def route(core):
        return [(x + (1 - core) * (1 - 2 * x), y + core * (1 - 2 * y)),
                (x + core * (1 - 2 * x), y + (1 - core) * (1 - 2 * y)),
                (1 - x, 1 - y)]

    mine = [(*chip, c) for chip in route(c)]
    carried = [pos, pos, pos, mine[0], mine[0], mine[1], mine[2]]
    to = [sibling, mine[0], mine[1], mine[1], sibling, sibling, sibling]
    landed = [sibling] + mine + [(*chip, 1 - c) for chip in route(1 - c)]

    def sent(a, k):
        dst = outs[a].at[_lin(carried[k])]
        return _remote(dst, dst, *sems(a, k), to[k])

    def arrival(a, k):
        dst = outs[a].at[_lin(landed[k])]
        return _remote(dst, dst, *sems(a, k), pos)

    return sent, arrival


_ROUNDS = (range(0, 3), range(3, 6), range(6, 7))


def _round_sems(send, recv, rnd):
    n = len(_ROUNDS[rnd])
    return lambda a, k: (send.at[n * a + k - _ROUNDS[rnd][0]], recv.at[n * a + k - _ROUNDS[rnd][0]])


def weight_gather_start(wg, woutg):
    def body(wg_ref, woutg_ref, send, recv, wg_out, woutg_out, token):
        del wg_out, woutg_out
        sent, _ = _gather_copies((wg_ref, woutg_ref), _round_sems(send, recv, 0))
        for a in range(2):
            for k in _ROUNDS[0]:
                sent(a, k).start()
        token[...] = jnp.zeros_like(token)

    hbm = lambda a: pltpu.with_memory_space_constraint(a, pltpu.HBM)
    n = 2 * len(_ROUNDS[0])
    return pl.pallas_call(
        body, name="weight_gather_start",
        in_specs=[_HBM] * 2,
        out_specs=[_SEM, _SEM, _HBM, _HBM, pl.BlockSpec(memory_space=pltpu.VMEM)],
        out_shape=[pltpu.SemaphoreType.DMA((n,)), pltpu.SemaphoreType.DMA((n,)),
                   pltpu.HBM(wg.shape, wg.dtype), pltpu.HBM(woutg.shape, woutg.dtype),
                   jax.ShapeDtypeStruct((8, LANE), F32)],
        input_output_aliases={0: 2, 1: 3},
        compiler_params=pltpu.CompilerParams(has_side_effects=pltpu.SideEffectType.DATAFLOW_SIDE_EFFECTING),
    )(hbm(wg), hbm(woutg))


def _wait_round(sent, arrival, rnd):
    for a in range(2):
        for k in _ROUNDS[rnd]:
            arrival(a, k).wait_recv()
            sent(a, k).wait_send()


def weight_gather_wait(send, recv, wg, woutg, after):
    def body(wg_ref, woutg_ref, send, recv, after_ref, wg_out, woutg_out):
        del after_ref, wg_out, woutg_out
        _wait_round(*_gather_copies((wg_ref, woutg_ref), _round_sems(send, recv, 0)), 0)

    return pl.pallas_call(
        body, name="weight_gather_wait",
        in_specs=[_HBM] * 2 + [_SEM, _SEM, pl.BlockSpec(memory_space=pl.ANY)],
        out_specs=[_HBM] * 2,
        out_shape=[pltpu.HBM(wg.shape, wg.dtype), pltpu.HBM(woutg.shape, woutg.dtype)],
        input_output_aliases={0: 0, 1: 1},
        compiler_params=pltpu.CompilerParams(has_side_effects=pltpu.SideEffectType.DATAFLOW_SIDE_EFFECTING),
    )(wg, woutg, send, recv, after)


def weight_pass_start(wg, woutg):
    def body(wg_ref, woutg_ref, send, recv, wg_out, woutg_out):
        del wg_out, woutg_out
        sent, _ = _gather_copies((wg_ref, woutg_ref),_round_sems(send, recv, 1))
        for a in range(2):
            for k in _ROUNDS[1]:
                sent(a, k).start()

    n = 2 * len(_ROUNDS[1])
    return pl.pallas_call(
        body, name="weight_pass_start",
        in_specs=[_HBM] * 2,
        out_specs=[_SEM, _SEM, _HBM, _HBM],
        out_shape=[pltpu.SemaphoreType.DMA((n,)), pltpu.SemaphoreType.DMA((n,)),
                   pltpu.HBM(wg.shape, wg.dtype), pltpu.HBM(woutg.shape, woutg.dtype)],
        input_output_aliases={0: 2, 1: 3},
        compiler_params=pltpu.CompilerParams(has_side_effects=pltpu.SideEffectType.DATAFLOW_SIDE_EFFECTING),
    )(wg, woutg)


def weight_pass_wait(send, recv, wg, woutg, after):
    def body(wg_ref, woutg_ref, send, recv, after_ref, wg_out, woutg_out):
        del after_ref, wg_out, woutg_out
        _wait_round(*_gather_copies((wg_ref, woutg_ref),_round_sems(send, recv, 1)), 1)

    return pl.pallas_call(
        body, name="weight_pass_wait",
        in_specs=[_HBM] * 2 + [_SEM, _SEM, pl.BlockSpec(memory_space=pl.ANY)],
        out_specs=[_HBM] * 2,
        out_shape=[pltpu.HBM(wg.shape, wg.dtype), pltpu.HBM(woutg.shape, woutg.dtype)],
        input_output_aliases={0: 0, 1: 1},
        compiler_params=pltpu.CompilerParams(has_side_effects=pltpu.SideEffectType.DATAFLOW_SIDE_EFFECTING),
    )(wg, woutg, send, recv, after)


def weight_pass_last(wg, woutg):
    def body(wg_in, woutg_in, wg_ref, woutg_ref, send, recv):
        del wg_in, woutg_in
        sent, arrival = _gather_copies((wg_ref, woutg_ref),_round_sems(send, recv, 2))
        for a in range(2):
            sent(a, _ROUNDS[2][0]).start()
        _wait_round(sent, arrival, 2)

    any_spec = pl.BlockSpec(memory_space=pl.ANY)
    n = 2 * len(_ROUNDS[2])
    return pl.pallas_call(
        body, name="weight_pass_last",
        in_specs=[any_spec, any_spec], out_specs=[any_spec, any_spec],
        out_shape=[jax.ShapeDtypeStruct(wg.shape, wg.dtype), jax.ShapeDtypeStruct(woutg.shape, woutg.dtype)],
        scratch_shapes=[pltpu.SemaphoreType.DMA((n,)), pltpu.SemaphoreType.DMA((n,))],
        input_output_aliases={0: 0, 1: 1},
    )(wg, woutg)


def grad_pair_exchange(g_sib, g_out):
    def body(gsib_ref, gout_ref, ra_ref, rb_ref, send, recv):
        pos = _mesh_pos()
        x, y, c = pos
        sibling = (x, y, 1 - c)
        copies = []
        for q in range(4):
            copies.append(_remote(gsib_ref.at[q], ra_ref.at[q], send.at[q], recv.at[q], sibling))
            copies.append(_remote(gout_ref.at[2 * q + (1 - c)], rb_ref.at[q], send.at[4 + q], recv.at[4 + q], sibling))
        for cp in copies:
            cp.start()
        for cp in copies:
            cp.wait_recv()
        for cp in copies:
            cp.wait_send()

    any_spec = pl.BlockSpec(memory_space=pl.ANY)
    return pl.pallas_call(
        body, name="grad_pair_exchange",
        in_specs=[any_spec, any_spec], out_specs=[any_spec, any_spec],
        out_shape=[jax.ShapeDtypeStruct(g_sib.shape, g_sib.dtype), jax.ShapeDtypeStruct((4,) + g_out.shape[1:], F32)],
        scratch_shapes=[pltpu.SemaphoreType.DMA((8,)), pltpu.SemaphoreType.DMA((8,))],
    )(g_sib, g_out)


def pair_sum(g_in, ra, g_out, rb, c_idx):
    tr = D_MODEL

    def body(c_ref, gin_ref, ra_ref, gout_ref, rb_ref, sb_ref, sbo_ref):
        del c_ref
        sb_ref[...] = (gin_ref[...] + ra_ref[...].astype(F32)).astype(sb_ref.dtype)
        sbo_ref[...] = gout_ref[...] + rb_ref[...]

    n_i = D_MODEL // tr
    return pl.pallas_call(
        body, name="pair_sum",
        grid_spec=pltpu.PrefetchScalarGridSpec(
            num_scalar_prefetch=1, grid=(4, n_i),
            in_specs=[pl.BlockSpec((None, tr, SHARD_IN), lambda q, i, c: (2 * q + c[0], i, 0)),
                      pl.BlockSpec((None, tr, SHARD_IN), lambda q, i, c: (q, i, 0)),
                      pl.BlockSpec((None, SHARD_OUT // n_i, D_MODEL), lambda q, i, c: (2 * q + c[0], i, 0)),
                      pl.BlockSpec((None, SHARD_OUT // n_i, D_MODEL), lambda q, i, c: (q, i, 0))],
            out_specs=[pl.BlockSpec((None, tr, SHARD_IN), lambda q, i, c: (q, i, 0)),
                       pl.BlockSpec((None, SHARD_OUT // n_i, D_MODEL), lambda q, i, c: (q, i, 0))]),
        out_shape=[jax.ShapeDtypeStruct(ra.shape, _BF), jax.ShapeDtypeStruct(rb.shape, F32)],
        compiler_params=_params(("arbitrary", "arbitrary")),
    )(c_idx, g_in, ra, g_out, rb)


_HBM = pl.BlockSpec(memory_space=pltpu.HBM)
_SEM = pl.BlockSpec(memory_space=pltpu.SEMAPHORE)
_N_CHIP_COPIES = 6


def _chip_copies(sb_ref, sbo_ref, rc_ref, rco_ref, send, recv):
    pos = _mesh_pos()
    copies = []
    for a, (src, dst) in enumerate(((sb_ref, rc_ref), (sbo_ref, rco_ref))):
        for j, chip in enumerate(_other_chips(pos)):
            copies.append(_remote(src.at[2 * chip[0] + chip[1]], dst.at[j], send.at[3 * a + j], recv.at[3 * a + j],
                                  (*chip, pos[2])))
    return copies


def grad_chip_start(sb, sbo):
    def body(sb_ref, sbo_ref, rc_ref, rco_ref, send, recv, sb_thru, sbo_thru, rc_thru, rco_thru, token):
        del sb_thru, sbo_thru, rc_thru, rco_thru
        for cp in _chip_copies(sb_ref, sbo_ref, rc_ref, rco_ref, send, recv):
            cp.start()
        token[...] = jnp.zeros_like(token)

    hbm = lambda a: pltpu.with_memory_space_constraint(a, pltpu.HBM)
    rc = lax.empty((3,) + sb.shape[1:], sb.dtype)
    rco = lax.empty((3,) + sbo.shape[1:], sbo.dtype)
    return pl.pallas_call(
        body, name="grad_chip_start",
        in_specs=[_HBM] * 4,
        out_specs=[_SEM, _SEM, _HBM, _HBM, _HBM, _HBM, pl.BlockSpec(memory_space=pltpu.VMEM)],
        out_shape=[pltpu.SemaphoreType.DMA((_N_CHIP_COPIES,)), pltpu.SemaphoreType.DMA((_N_CHIP_COPIES,)),
                   pltpu.HBM(sb.shape, sb.dtype), pltpu.HBM(sbo.shape, sbo.dtype),
                   pltpu.HBM(rc.shape, rc.dtype), pltpu.HBM(rco.shape, rco.dtype),
                   jax.ShapeDtypeStruct((8, LANE), F32)],
        input_output_aliases={0: 2, 1: 3, 2: 4, 3: 5},
        compiler_params=pltpu.CompilerParams(has_side_effects=pltpu.SideEffectType.DATAFLOW_SIDE_EFFECTING),
    )(hbm(sb), hbm(sbo), hbm(rc), hbm(rco))


def grad_chip_wait(send, recv, sb, sbo, rc, rco, after):
    def body(sb_ref, sbo_ref, rc_ref, rco_ref, send, recv, after_ref, sb_o, sbo_o, rc_o, rco_o):
        del after_ref, sb_o, sbo_o, rc_o, rco_o
        for cp in _chip_copies(sb_ref, sbo_ref, rc_ref, rco_ref, send, recv):
            cp.wait_send()
            cp.wait_recv()

    return pl.pallas_call(
        body, name="grad_chip_wait",
        in_specs=[_HBM] * 4 + [_SEM, _SEM, pl.BlockSpec(memory_space=pl.ANY)],
        out_specs=[_HBM] * 4,
        out_shape=[pltpu.HBM(sb.shape, sb.dtype), pltpu.HBM(sbo.shape, sbo.dtype),
                   pltpu.HBM(rc.shape, rc.dtype), pltpu.HBM(rco.shape, rco.dtype)],
        input_output_aliases={0: 0, 1: 1, 2: 2, 3: 3},
        compiler_params=pltpu.CompilerParams(has_side_effects=pltpu.SideEffectType.DATAFLOW_SIDE_EFFECTING),
    )(sb, sbo, rc, rco, send, recv, after)


def pack_gather(pack):
    def body(pack_ref, packs_ref, psend, precv):
        pos = _mesh_pos()
        me = _lin(pos)
        packs_ref[me] = pack_ref[...]
        peers = [_xor_peer(pos, k) for k in range(1, N_DEV)]
        gather = [_remote(packs_ref.at[me], packs_ref.at[me], psend.at[n], precv.at[n], p) for n, p in enumerate(peers)]
        for cp in gather:
            cp.start()
        for n, p in enumerate(peers):
            _remote(packs_ref.at[_lin(p)], packs_ref.at[_lin(p)], psend.at[n], precv.at[n], p).wait_recv()
        for cp in gather:
            cp.wait_send()

    vmem = pl.BlockSpec(memory_space=pltpu.VMEM)
    return pl.pallas_call(
        body, name="pack_gather", in_specs=[vmem], out_specs=vmem,
        out_shape=jax.ShapeDtypeStruct((N_DEV,) + pack.shape, F32),
        scratch_shapes=[pltpu.SemaphoreType.DMA((N_DEV - 1,)), pltpu.SemaphoreType.DMA((N_DEV - 1,))],
    )(pack)


def pack_rows(vec_mid, vec_ada, dlb):
    def body(mid_ref, ada_ref, dlb_ref, o_ref):
        mid = lambda r: mid_ref[r:r + 1, :]
        rows = [ada_ref[0:1, :], dlb_ref[...], mid(MID_HG_G), mid(MID_RET_G), mid(MID_FINAL_G),
                ada_ref[2:3, :], ada_ref[1:2, :], mid(MID_GATE), mid(MID_LOSS)]
        o_ref[...] = jnp.zeros_like(o_ref)
        for n, row in enumerate(rows):
            o_ref[n:n + 1, :] = row

    vmem = pl.BlockSpec(memory_space=pltpu.VMEM)
    return pl.pallas_call(body, name="pack_rows", in_specs=[vmem] * 3, out_specs=vmem,
                          out_shape=jax.ShapeDtypeStruct((PACK_ROWS, D_MODEL), F32))(vec_mid, vec_ada, dlb)


def _adamw(w, g, m, v):
    m = ADAM_B1 * m + (1.0 - ADAM_B1) * g
    v = ADAM_B2 * v + (1.0 - ADAM_B2) * (g * g)
    m_hat = m / (1.0 - ADAM_B1 ** ADAM_STEP)
    v_hat = v / (1.0 - ADAM_B2 ** ADAM_STEP)
    delta = -ADAM_LR * (m_hat / (jnp.sqrt(v_hat) + ADAM_EPS) + ADAM_WD * w)
    return delta, m, v


def adam_shard(chip_idx, own, parts, w, m, v, name):
    rows, cols = w.shape
    tr = min(rows, 256)

    def body(chip_ref, p0, p1, p2, p3, w_ref, m_ref, v_ref, g_ref, d_ref, nm_ref, nv_ref):
        del chip_ref
        g = ((p0[...].astype(F32) + p1[...].astype(F32)) + p2[...].astype(F32)) + p3[...].astype(F32)
        g_ref[...] = g
        d_ref[...], nm_ref[...], nv_ref[...] = _adamw(w_ref[...], g, m_ref[...], v_ref[...])

    part = lambda q: pl.BlockSpec((None, tr, cols), lambda i, chip, q=q: (q, i, 0))
    tile = pl.BlockSpec((tr, cols), lambda i, chip: (i, 0))
    return pl.pallas_call(
        body, name=name,
        grid_spec=pltpu.PrefetchScalarGridSpec(
            num_scalar_prefetch=1, grid=(rows // tr,),
            in_specs=[pl.BlockSpec((None, tr, cols), lambda i, chip: (chip[0], i, 0)), part(0), part(1), part(2),
                      tile, tile, tile],
            out_specs=[tile] * 4),
        out_shape=[jax.ShapeDtypeStruct(w.shape, F32)] * 4,
        compiler_params=_params(("arbitrary",)),
    )(chip_idx, own, parts, parts, parts, w, m, v)


def adam_ada(sc_t, dmod_all, me_idx, w, m, v):
    def body(me_ref, sc_ref, dm_ref, w_ref, m_ref, v_ref, g_ref, d_ref, nm_ref, nv_ref):
        del me_ref
        g = _dot_f32(sc_ref[...], dm_ref[...])
        g_ref[...] = g
        d_ref[...], nm_ref[...], nv_ref[...] = _adamw(w_ref[...], g, m_ref[...], v_ref[...])

    full = pl.BlockSpec(w.shape, lambda i, me: (0, 0))
    return pl.pallas_call(
        body, name="adam_ada",
        grid_spec=pltpu.PrefetchScalarGridSpec(
            num_scalar_prefetch=1, grid=(1,),
            in_specs=[pl.BlockSpec(sc_t.shape, lambda i, me: (0, 0)),
                      pl.BlockSpec((LANE, SHARD_ADA), lambda i, me: (0, me[0])), full, full, full],
            out_specs=[full] * 4),
        out_shape=[jax.ShapeDtypeStruct(w.shape, F32)] * 4,
        compiler_params=_params(("arbitrary",)),
    )(me_idx, sc_t, dmod_all, w, m, v)


def adam_vectors(packs, lb, params, ms, vs):
    n = len(params)

    def body(*refs):
        packs_ref, lb_ref = refs[0], refs[1]
        w_refs, m_refs, v_refs = refs[2:2 + n], refs[2 + n:2 + 2 * n], refs[2 + 2 * n:2 + 3 * n]
        loss_ref = refs[2 + 3 * n]
        outs = refs[3 + 3 * n:3 + 7 * n]
        tot_ref = refs[3 + 7 * n]
        tot = packs_ref[0]
        for d in range(1, N_DEV):
            tot = tot + packs_ref[d]
        tot_ref[...] = tot
        row = lambda r: tot_ref[r:r + 1, :]
        lbv = lb_ref[...]
        dl0 = row(ROW_LB) * lbv * (1.0 - lbv)
        grads = [[row(ROW_NORM_G)],
                 [jnp.concatenate([row(ROW_SHIFT), row(ROW_SCALE), row(ROW_GATE)], axis=1)],
                 [dl0, -dl0],
                 [row(ROW_HG_G)], [row(ROW_RET_G)], [row(ROW_FINAL_G)]]
        loss_ref[...] = tot_ref[ROW_LOSS:ROW_LOSS + 1, 0:LANE]
        for j, g_rows in enumerate(grads):
            for r, g in enumerate(g_rows):
                rs = slice(r, r + 1)
                d, nm, nv = _adamw(w_refs[j][rs, :], g, m_refs[j][rs, :], v_refs[j][rs, :])
                outs[4 * j][rs, :] = g
                outs[4 * j + 1][rs, :] = d
                outs[4 * j + 2][rs, :] = nm
                outs[4 * j + 3][rs, :] = nv

    vmem = pl.BlockSpec(memory_space=pltpu.VMEM)
    out_shape = [jax.ShapeDtypeStruct((1, LANE), F32)]
    for w in params:
        out_shape += [jax.ShapeDtypeStruct(w.shape, F32)] * 4
    return pl.pallas_call(
        body, name="adam_vectors", in_specs=[vmem] * (2 + 3 * n), out_specs=[vmem] * len(out_shape),
        out_shape=out_shape, scratch_shapes=[pltpu.VMEM((PACK_ROWS, D_MODEL), F32)],
    )(packs, lb, *params, *ms, *vs)


def kernel(x, c, norm_g, w_ada, b_ada, w_in, hg_lb_logits, hg_norm_g, ret_norm_g, w_out, final_g, loss_target, m_norm_g, m_w_ada, m_b_ada, m_w_in, m_hg_lb_logits, m_hg_norm_g, m_ret_norm_g, m_w_out, m_final_g, v_norm_g, v_w_ada, v_b_ada, v_w_in, v_hg_lb_logits, v_hg_norm_g, v_ret_norm_g, v_w_out, v_final_g):
    pos = _mesh_pos()
    me_idx = jnp.reshape(_lin(pos), (1,)).astype(jnp.int32)
    c_idx = jnp.reshape(pos[2], (1,)).astype(jnp.int32)
    vec = lambda a: a.reshape(1, D_MODEL)

    mod, scall, lb, wtg, woutg = pre_exchange(c, w_ada[0], b_ada, hg_lb_logits, w_in[0], w_out[0])
    chip_idx = jnp.reshape(2 * pos[0] + pos[1], (1,)).astype(jnp.int32)
    gather_send, gather_recv, wtg, woutg, gather_token = weight_gather_start(wtg, woutg)
    mod = mod + gather_token[:1, :1]

    other_chips = jnp.stack([2 * cx + cy for cx, cy in _other_chips(pos)]).astype(jnp.int32)

    def project(h):
        t_len = h.shape[0]
        flat_in = lambda a: a.reshape(D_IN, D_MODEL)
        wg, wog = weight_gather_wait(gather_send, gather_recv, wtg, woutg, h)
        pass_send, pass_recv, wg, wog = weight_pass_start(wg, wog)
        pb = proj_forward(h, flat_in(wg), chip_idx, t_len, "proj_fwd_own")
        wg, wog = weight_pass_wait(pass_send, pass_recv, wg, wog, pb)
        wg, wog = weight_pass_last(wg, wog)
        pb = proj_forward(h, flat_in(wg), other_chips, t_len, "proj_fwd_rest", pb)
        return pb, flat_in(wg), wog.reshape(D_MODEL, D_MODEL)

    def start_exchange(dwin, dwin_sib, dwout):
        dwout = dwout.reshape(N_DEV, SHARD_OUT, D_MODEL)
        ra, rb = grad_pair_exchange(dwin_sib, dwout)
        sb, sbo = pair_sum(dwin, ra, dwout, rb, c_idx)
        send, recv, sb, sbo, rc, rco, token = grad_chip_start(sb, sbo)
        return token, (send, recv, sb, sbo, rc, rco)

    grad_x, _, _, vec_mid, vec_ada, dlb, pending = device_step(
        x[0], loss_target[0], mod, lb, project, norm_g, hg_norm_g, ret_norm_g, vec(final_g), c_idx, start_exchange)
    packs = pack_gather(pack_rows(vec_mid, vec_ada, dlb))
    dmod_all = packs[:, ROW_SHIFT:ROW_GATE + 1, :].reshape(N_DEV, 3 * D_MODEL)
    dmod_all = jnp.pad(dmod_all, ((0, LANE - N_DEV), (0, 0)))
    sc_t = jnp.pad(scall.T, ((0, 0), (0, LANE - N_DEV)))
    g_ada, d_ada, nm_ada, nv_ada = adam_ada(sc_t, dmod_all, me_idx, w_ada[0], m_w_ada[0], v_w_ada[0])
    small = adam_vectors(
        packs, lb,
        (norm_g, b_ada, hg_lb_logits, hg_norm_g, ret_norm_g, vec(final_g)),
        (m_norm_g, m_b_ada, m_hg_lb_logits, m_hg_norm_g, m_ret_norm_g, vec(m_final_g)),
        (v_norm_g, v_b_ada, v_hg_lb_logits, v_hg_norm_g, v_ret_norm_g, vec(v_final_g)))
    loss = small[0][0, 0]
    sb, sbo, rc, rco = grad_chip_wait(*pending, small[0])
    g_in, d_in, nm_in, nv_in = adam_shard(chip_idx, sb, rc, w_in[0], m_w_in[0], v_w_in[0], "adam_w_in")
    g_out, d_out, nm_out, nv_out = adam_shard(chip_idx, sbo, rco, w_out[0], m_w_out[0], v_w_out[0], "adam_w_out")
    (g_ng, d_ng, nm_ng, nv_ng), (g_b, d_b, nm_b, nv_b), (g_lb, d_lb, nm_lb, nv_lb), (g_hg, d_hg, nm_hg, nv_hg), \
        (g_rg, d_rg, nm_rg, nv_rg), (g_fg, d_fg, nm_fg, nv_fg) = [small[1 + 4 * j:5 + 4 * j] for j in range(6)]
    flat = lambda a: a.reshape(D_MODEL)

    def group(ng, ada, b, win, lbl, hg, rg, wo, fg):
        return (ng, ada[None], b, win[None], lbl, hg, rg, wo[None], flat(fg))

    return (loss, grad_x[None],
            *group(g_ng, g_ada, g_b, g_in, g_lb, g_hg, g_rg, g_out, g_fg),
            *group(d_ng, d_ada, d_b, d_in, d_lb, d_hg, d_rg, d_out, d_fg),
            *group(nm_ng, nm_ada, nm_b, nm_in, nm_lb, nm_hg, nm_rg, nm_out, nm_fg),
            *group(nv_ng, nv_ada, nv_b, nv_in, nv_lb, nv_hg, nv_rg, nv_out, nv_fg))
```

```python
import numpy as np
import jax
import jax.numpy as jnp
from jax import lax
from jax.experimental import pallas as pl
from jax.experimental.pallas import tpu as pltpu

F32 = jnp.float32
_BF = jnp.bfloat16

D_MODEL = 1024
N_HEADS = 8
LANE = 128
RET_DK = 64
D_IN = 9216
N_DEV = 8
SHARD_IN = D_IN // N_DEV
SHARD_ADA = 3 * D_MODEL // N_DEV
SHARD_OUT = D_MODEL // N_DEV
N_CB = D_IN // LANE
CB_PER_SHARD = SHARD_IN // LANE
CHUNK = 128
N_LEVELS = 7
EPS = 1e-6
LOG2_E = float(np.log2(np.e))
ROPE_BASE = 10000.0
CB_HQ, CB_HF, CB_HI, CB_HZ, CB_RQ, CB_RK, CB_RV, CB_RZ, CB_GA, CB_GB = 0, 8, 16, 24, 32, 36, 40, 48, 56, 64
VMEM_LIMIT = 56 * 1024 * 1024

ADAM_LR, ADAM_B1, ADAM_B2, ADAM_EPS, ADAM_WD, ADAM_STEP = 0.001, 0.9, 0.999, 1e-08, 0.01, 10

_NN = (((1,), (0,)), ((), ()))
_NT = (((1,), (1,)), ((), ()))
_TN = (((0,), (0,)), ((), ()))
MESH = pl.DeviceIdType.MESH


def _dot(a, b, dims=_NN):
    return lax.dot_general(a.astype(_BF), b.astype(_BF), dims, preferred_element_type=F32)


def _split2(a):
    hi = a.astype(_BF)
    lo = (a - hi.astype(F32)).astype(_BF)
    return jnp.concatenate([hi, lo], axis=1)


def _dot_sel(sel, a):
    n = a.shape[1]
    r = lax.dot_general(sel.astype(_BF), _split2(a), _NN, preferred_element_type=F32)
    return r[:, :n] + r[:, n:]


def _dot_f32(a, b):
    def pieces(v):
        p1 = v.astype(_BF)
        r1 = v - p1.astype(F32)
        p2 = r1.astype(_BF)
        p3 = (r1 - p2.astype(F32)).astype(_BF)
        return p1, p2, p3
    a1, a2, a3 = pieces(a)
    b1, b2, b3 = pieces(b)
    d = lambda u, v: lax.dot_general(u, v, _NN, preferred_element_type=F32)
    return ((d(a1, b3) + d(a2, b2) + d(a3, b1)) + (d(a1, b2) + d(a2, b1))) + d(a1, b1)


def _sigmoid(v):
    return 1.0 / (1.0 + jnp.exp(-v))


def _params(sem=None):
    return pltpu.CompilerParams(dimension_semantics=sem, vmem_limit_bytes=VMEM_LIMIT)


def _hgrn_consts():
    c, nl = CHUNK, N_LEVELS
    t = np.arange(c)[:, None]
    j = np.arange(c)[None, :]
    sel = [j <= t]
    masks = [j == t]
    for l in range(1, nl + 1):
        m = ((t >> l) << l) + (1 << (l - 1)) - 1
        sec = t > m
        sel.append(np.where(sec, (j > m) & (j <= t), (j > t) & (j <= m)))
        same = (t >> l) == (j >> l)
        masks.append(same & sec & (j <= m))
    sel.append(j > t)
    sel = np.concatenate(sel, 0).astype(np.float32)
    masks = np.stack(masks).astype(np.float32)
    sgn = np.stack([np.where((t & (1 << (l - 1))) != 0, 1.0, -1.0) * np.ones((1, LANE)) for l in range(3, nl + 1)])
    return dict(tri=jnp.asarray(sel[:c], _BF),
                lvl=jnp.asarray(masks, F32),
                sgn=jnp.asarray(sgn, F32),
                sel_t=jnp.asarray(sel.T, _BF),
                lvl_b=jnp.asarray(masks, _BF),
                lvlt_b=jnp.asarray(np.swapaxes(masks, 1, 2), _BF))


def _level_exponents(b, logf, b_scr, sgn_ref):
    c = CHUNK
    b_scr[...] = b
    row = lax.broadcasted_iota(jnp.int32, (c, LANE), 0)
    nxt = pltpu.roll(logf, c - 1, 0)
    prv = pltpu.roll(logf, 1, 0)
    r4 = row & 3
    out = [jnp.where((row & 1) == 1, logf, 0.0),
           jnp.where(r4 == 0, nxt, jnp.where(r4 == 1, 0.0, jnp.where(r4 == 2, logf, logf + prv)))]
    for l in range(3, N_LEVELS + 1):
        size, half = 1 << l, 1 << (l - 1)
        ref = jnp.concatenate([jnp.broadcast_to(b_scr[i * size + half - 1:i * size + half, :], (size, LANE))
                               for i in range(c // size)], axis=0)
        out.append((b - ref) * sgn_ref[l - 3])
    return out


def _hgrn_chunk(hq, hf, hi, lbv, tri_ref, sgn_ref, b_scr):
    sq = _sigmoid(hq)
    q = hq * sq
    sg = _sigmoid(hf)
    omlb = 1.0 - lbv
    f = lbv + omlb * sg
    k = 1.0 - f
    logf = jnp.log(f) * LOG2_E
    b = _dot_sel(tri_ref[...], logf)
    bc = jnp.sum(logf, axis=0, keepdims=True)
    lev = [None] + [jnp.exp2(e) for e in _level_exponents(b, logf, b_scr, sgn_ref)]
    return dict(sq=sq, q=q, sg=sg, omlb=omlb, f=f, k=k, v=hi, eb=jnp.exp2(b), erem=jnp.exp2(bc - b),
                ebc=jnp.exp2(bc), lev=lev)


def _blockdiag(a, b):
    z = jnp.zeros_like(a)
    return jnp.concatenate([jnp.concatenate([a, z], axis=1), jnp.concatenate([z, b], axis=1)], axis=0)


def _level_operands(a):
    q, k = a["q"].astype(_BF), a["k"].astype(_BF)
    lev = [None] + [a["lev"][l].astype(_BF) for l in range(1, N_LEVELS + 1)]
    ql = [q] + [q * lev[l] for l in range(1, N_LEVELS + 1)]
    kl = [k] + [k * lev[l] for l in range(1, N_LEVELS + 1)]
    pairs = range(0, N_LEVELS + 1, 2)
    return ([jnp.concatenate([ql[l], ql[l + 1]], axis=1) for l in pairs], [_blockdiag(kl[l], kl[l + 1]) for l in pairs],
            ql, kl)


def _hgrn_scores(a, lvl_ref, q_pairs, k_diags):
    acc = None
    for n, (qp, kd) in enumerate(zip(q_pairs, k_diags)):
        both = lax.dot_general(qp, kd, _NT, preferred_element_type=F32)
        part = lvl_ref[2 * n] * both[:, :CHUNK] + lvl_ref[2 * n + 1] * both[:, CHUNK:]
        acc = part if acc is None else acc + part
    return acc


SCAN_UNROLL = 8
FWD_UNROLL = 16
RET_UNROLL = 16


def _writeback_reserve(step, make_copies):
    slot = step % 2

    @pl.when(step >= 2)
    def _():
        for cp in make_copies(slot):
            cp.wait()

    return slot


def _writeback_commit(step, n_steps, slot, make_copies):
    for cp in make_copies(slot):
        cp.start()

    @pl.when(step == n_steps - 1)
    def _():
        for cp in make_copies(slot):
            cp.wait()
        if n_steps > 1:
            for cp in make_copies(1 - slot):
                cp.wait()


def _resident(const):
    zeros = (0,) * const.ndim
    return pl.BlockSpec(const.shape, lambda p, t: zeros)


def _time_block(t_len):
    return min(t_len, 2048)


def hgrn_forward(pb, lb, t_len):
    nc = t_len // CHUNK
    tb = _time_block(t_len)
    ncb = tb // CHUNK
    consts = _hgrn_consts()
    operands = [consts[n] for n in ("tri", "lvl", "sgn")]

    def body(hq_ref, hf_ref, hi_ref, lb_ref, tri_ref, lvl_ref, sgn_ref, o_ref, ssave_ref, asave_ref, st_ref, b_scr):
        @pl.when(pl.program_id(1) == 0)
        def _():
            st_ref[...] = jnp.zeros_like(st_ref)

        def chunk(ci, carry):
            r = pl.ds(pl.multiple_of(ci * CHUNK, CHUNK), CHUNK)
            for hd in range(2):
                lbv = lb_ref[:, hd * LANE:(hd + 1) * LANE]
                a = _hgrn_chunk(hq_ref[hd, r, :], hf_ref[hd, r, :], hi_ref[hd, r, :], lbv, tri_ref, sgn_ref,
                                b_scr.at[hd])
                q_pairs, k_diags, _, _ = _level_operands(a)
                st = st_ref[hd]
                ssave_ref[hd, ci] = st
                scores = _hgrn_scores(a, lvl_ref, q_pairs, k_diags).astype(asave_ref.dtype)
                asave_ref[hd, ci] = scores
                o_ref[hd, r, :] = _dot(a["q"] * a["eb"], st, _NT) + _dot(scores, a["v"])
                st_ref[hd] = st * a["ebc"] + _dot(a["v"], a["k"] * a["erem"], _TN)
            return carry

        lax.fori_loop(0, ncb, chunk, 0, unroll=FWD_UNROLL)

    pair = lambda base: pl.BlockSpec((2, tb, LANE), lambda p, t, base=base: (base // 2 + p, t, 0))
    per_chunk = pl.BlockSpec((2, ncb, LANE, LANE), lambda p, t: (p, t, 0, 0))
    return pl.pallas_call(
        body, name="hgrn_fwd", grid=(N_HEADS // 2, t_len // tb),
        in_specs=[pair(CB_HQ), pair(CB_HF), pair(CB_HI),
                  pl.BlockSpec((1, 2 * LANE), lambda p, t: (0, p))] + [_resident(c) for c in operands],
        out_specs=[pl.BlockSpec((2, tb, LANE), lambda p, t: (p, t, 0)), per_chunk, per_chunk],
        out_shape=[jax.ShapeDtypeStruct((N_HEADS, t_len, LANE), F32),
                   jax.ShapeDtypeStruct((N_HEADS, nc, LANE, LANE), F32),
                   jax.ShapeDtypeStruct((N_HEADS, nc, CHUNK, CHUNK), _BF)],
        scratch_shapes=[pltpu.VMEM((2, LANE, LANE), F32), pltpu.VMEM((2, CHUNK, LANE), F32)],
        compiler_params=_params(("arbitrary", "arbitrary")),
    )(pb, pb, pb, lb, *operands)


def hgrn_backward(pb, lb, do, ssave, asave, dpb, t_len):
    tb = _time_block(t_len)
    ncb, ntb = tb // CHUNK, t_len // tb
    consts = _hgrn_consts()
    operands = [consts[n] for n in ("tri", "sgn", "sel_t", "lvl_b", "lvlt_b")]

    def body(hq_ref, hf_ref, hi_ref, lb_ref, do_ref, ssave_ref, asave_ref, tri_ref, sgn_ref, selt_ref, lvlb_ref,
             lvltb_ref, dpb_in, dpb_ref, dlb_ref, dq_buf, df_buf, di_buf, dst_ref, b_scr, sems):
        del dpb_in
        p, t = pl.program_id(0), pl.program_id(1)
        step = p * ntb + t
        rows = pl.ds(pl.multiple_of((ntb - 1 - t) * tb, tb), tb)

        def out_copies(sl):
            return [pltpu.make_async_copy(buf.at[sl], dpb_ref.at[pl.ds(base + 2 * p, 2), rows], sems.at[sl, n])
                    for n, (buf, base) in enumerate(((dq_buf, CB_HQ), (df_buf, CB_HF), (di_buf, CB_HI)))]

        slot = _writeback_reserve(step, out_copies)

        @pl.when(t == 0)
        def _():
            dst_ref[...] = jnp.zeros_like(dst_ref)
            dlb_ref[...] = jnp.zeros_like(dlb_ref)

        def chunk(i, carry):
            ci = ncb - 1 - i
            r = pl.ds(pl.multiple_of(ci * CHUNK, CHUNK), CHUNK)
            for hd in range(2):
                head_chunk(hd, ci, r)
            return carry

        def head_chunk(hd, ci, r):
            lbv = lb_ref[:, hd * LANE:(hd + 1) * LANE]
            hq = hq_ref[hd, r, :]
            a = _hgrn_chunk(hq, hf_ref[hd, r, :], hi_ref[hd, r, :], lbv, tri_ref, sgn_ref, b_scr.at[hd])
            _, k_diags, ql, kl = _level_operands(a)
            q, k, v = a["q"], a["k"], a["v"]
            g = do_ref[hd, r, :]
            st0 = ssave_ref[hd, ci]
            dst = dst_ref[hd]
            scores = asave_ref[hd, ci]
            da = _dot(g, v, _NT)
            da_t = _dot(v, g, _NT)
            kb = k * a["erem"]
            qb = q * a["eb"]
            dv = _dot(scores, g, _TN) + _dot(kb, dst, _NT)
            dq_inter = _dot(g, st0) * a["eb"]
            dk_state = _dot(v, dst) * a["erem"]
            dq, dk = dq_inter, dk_state
            de = [q * dq_inter]
            da_b, dat_b = da.astype(_BF), da_t.astype(_BF)
            for n in range(len(k_diags)):
                l0, l1 = 2 * n, 2 * n + 1
                da_pair = jnp.concatenate([lvlb_ref[l0] * da_b, lvlb_ref[l1] * da_b], axis=1)
                dat_pair = jnp.concatenate([lvltb_ref[l0] * dat_b, lvltb_ref[l1] * dat_b], axis=1)
                dq_both = lax.dot_general(da_pair, k_diags[n], _NN, preferred_element_type=F32)
                dk_both = lax.dot_general(dat_pair, _blockdiag(ql[l0], ql[l1]), _NN, preferred_element_type=F32)
                for l, cols in ((l0, slice(0, LANE)), (l1, slice(LANE, 2 * LANE))):
                    dql, dkl = dq_both[:, cols], dk_both[:, cols]
                    if l > 0:
                        e = a["lev"][l]
                        dql, dkl = dql * e, dkl * e
                        de.append(q * dql + k * dkl)
                    dq = dq + dql
                    dk = dk + dkl
            de.append(k * dk_state)
            dst_ref[hd] = dst * a["ebc"] + _dot(g, qb, _TN)
            dbc = jnp.sum(dst * st0, axis=0, keepdims=True) * a["ebc"]
            de2 = lax.dot_general(selt_ref[...], _split2(jnp.concatenate(de, axis=0)), _NN,
                                  preferred_element_type=F32)
            dlogf = de2[:, :LANE] + de2[:, LANE:] + dbc
            sq, sg = a["sq"], a["sg"]
            df = dlogf / a["f"] - dk
            dq_buf[slot, hd, r, :] = (dq * (sq * (1.0 + hq * (1.0 - sq)))).astype(dq_buf.dtype)
            df_buf[slot, hd, r, :] = (df * a["omlb"] * sg * (1.0 - sg)).astype(df_buf.dtype)
            di_buf[slot, hd, r, :] = dv.astype(di_buf.dtype)
            cols = slice(hd * LANE, (hd + 1) * LANE)
            dlb_ref[:, cols] = dlb_ref[:, cols] + jnp.sum(df * (1.0 - sg), axis=0, keepdims=True)

        lax.fori_loop(0, ncb, chunk, 0, unroll=SCAN_UNROLL)
        _writeback_commit(step, (N_HEADS // 2) * ntb, slot, out_copies)

    pair = lambda base: pl.BlockSpec((2, tb, LANE), lambda p, t, base=base: (base // 2 + p, ntb - 1 - t, 0))
    any_spec = pl.BlockSpec(memory_space=pl.ANY)
    per_chunk = pl.BlockSpec((2, ncb, LANE, LANE), lambda p, t: (p, ntb - 1 - t, 0, 0))
    return pl.pallas_call(
        body, name="hgrn_bwd", grid=(N_HEADS // 2, ntb),
        in_specs=[pair(CB_HQ), pair(CB_HF), pair(CB_HI),
                  pl.BlockSpec((1, 2 * LANE), lambda p, t: (0, p)),
                  pair(0), per_chunk, per_chunk]
        + [_resident(c) for c in operands] + [any_spec],
        out_specs=[any_spec, pl.BlockSpec((1, 2 * LANE), lambda p, t: (0, p))],
        out_shape=[jax.ShapeDtypeStruct(dpb.shape, dpb.dtype), jax.ShapeDtypeStruct((1, D_MODEL), F32)],
        scratch_shapes=[pltpu.VMEM((2, 2, tb, LANE), dpb.dtype)] * 3 + [
            pltpu.VMEM((2, LANE, LANE), F32), pltpu.VMEM((2, CHUNK, LANE), F32), pltpu.SemaphoreType.DMA((2, 3))],
        input_output_aliases={7 + len(operands): 0},
        compiler_params=_params(("arbitrary", "arbitrary")),
    )(pb, pb, pb, lb, do, ssave, asave, *operands, dpb)


def _rope_tables(t_len):
    half = RET_DK // 2
    inv_freq = (1.0 / (np.float32(ROPE_BASE) ** np.linspace(0.0, 1.0, half, dtype=np.float32))).astype(np.float32)
    ang = (np.arange(t_len, dtype=np.float32)[:, None] * inv_freq[None, :]).astype(np.float64)
    cos, sin = np.cos(ang).astype(np.float32), np.sin(ang).astype(np.float32)
    cos_t = np.concatenate([cos, cos, cos, cos], axis=1)
    sin_t = np.concatenate([-sin, sin, -sin, sin], axis=1)
    return jnp.asarray(cos_t), jnp.asarray(sin_t)


def _swap_halves(v):
    half = RET_DK // 2
    lane = lax.broadcasted_iota(jnp.int32, v.shape, 1)
    first = (lane & (RET_DK - 1)) < half
    return jnp.where(first, pltpu.roll(v, LANE - half, 1), pltpu.roll(v, half, 1))


def _ret_head_consts(hidx):
    c = CHUNK
    hf = jnp.full((1, LANE), hidx, jnp.int32).astype(F32)
    lg = jnp.log(1.0 - jnp.exp(-(5.0 + hf) * np.float32(np.log(2.0))))
    row = lax.broadcasted_iota(jnp.int32, (c, c), 0)
    col = lax.broadcasted_iota(jnp.int32, (c, c), 1)
    rel = (row - col).astype(F32)
    dm = jnp.where(rel >= 0, jnp.exp(lg[:, :1] * jnp.maximum(rel, 0.0)), 0.0)
    dm_t = jnp.where(rel <= 0, jnp.exp(lg[:, :1] * jnp.maximum(-rel, 0.0)), 0.0)
    idx = lax.broadcasted_iota(jnp.int32, (c, LANE), 0).astype(F32)
    zeta = jnp.exp(lg * (c - 1.0 - idx))
    xi = jnp.exp(lg * (idx + 1.0))
    cdec = jnp.exp(lg * float(c))
    return dm, zeta, xi, cdec, dm_t


def _lane_mask(which):
    lane = lax.broadcasted_iota(jnp.int32, (1, LANE), 1)
    return ((lane // RET_DK) == which).astype(F32)


def retention_forward(pb, cos_t, sin_t, t_len):
    nc = t_len // CHUNK

    tb = _time_block(t_len)
    ncb = tb // CHUNK

    def body(rq_ref, rk_ref, rv_ref, cos_ref, sin_ref, o_ref, rsave_ref, st_ref):
        p = pl.program_id(0)

        @pl.when(pl.program_id(1) == 0)
        def _():
            st_ref[...] = jnp.zeros_like(st_ref)

        consts = [_ret_head_consts(2 * p + hd) for hd in range(2)]

        def chunk(ci, carry):
            r = pl.ds(pl.multiple_of(ci * CHUNK, CHUNK), CHUNK)
            cs, sn = cos_ref[r, :], sin_ref[r, :]
            q = rq_ref[r, :]
            k = rk_ref[r, :]
            q = q * cs + _swap_halves(q) * sn
            k = (k * cs + _swap_halves(k) * sn) * RET_DK ** -0.5
            for hd in range(2):
                dm, zeta, xi, cdec, _ = consts[hd]
                lm = _lane_mask(hd)
                qh, kh = q * lm, k * lm
                v = rv_ref[hd, r, :]
                st = st_ref[hd]
                rsave_ref[hd, ci] = st
                scores = _dot(qh, kh, _NT) * dm
                o_ref[hd, r, :] = _dot(scores, v) + _dot(qh * xi, st, _NT)
                st_ref[hd] = st * cdec + _dot(v, kh * zeta, _TN)
            return carry

        lax.fori_loop(0, ncb, chunk, 0, unroll=RET_UNROLL)

    return pl.pallas_call(
        body, name="ret_fwd", grid=(N_HEADS // 2, t_len // tb),
        in_specs=[pl.BlockSpec((None, tb, LANE), lambda p, t: (CB_RQ + p, t, 0)),
                  pl.BlockSpec((None, tb, LANE), lambda p, t: (CB_RK + p, t, 0)),
                  pl.BlockSpec((2, tb, LANE), lambda p, t: (CB_RV // 2 + p, t, 0)),
                  pl.BlockSpec((tb, LANE), lambda p, t: (t, 0)),
                  pl.BlockSpec((tb, LANE), lambda p, t: (t, 0))],
        out_specs=[pl.BlockSpec((2, tb, LANE), lambda p, t: (p, t, 0)),
                   pl.BlockSpec((2, ncb, LANE, LANE), lambda p, t: (p, t, 0, 0))],
        out_shape=[jax.ShapeDtypeStruct((N_HEADS, t_len, LANE), F32),
                   jax.ShapeDtypeStruct((N_HEADS, nc, LANE, LANE), F32)],
        scratch_shapes=[pltpu.VMEM((2, LANE, LANE), F32)],
        compiler_params=_params(("arbitrary", "arbitrary")),
    )(pb, pb, pb, cos_t, sin_t)


def retention_backward(pb, cos_t, sin_t, do, rsave, dpb, t_len):
    tb = _time_block(t_len)
    ncb, ntb = tb // CHUNK, t_len // tb

    def body(rq_ref, rk_ref, rv_ref, cos_ref, sin_ref, do_ref, rsave_ref, dpb_in,
             dpb_ref, dq_buf, dk_buf, dv_buf, dst_ref, sems):
        del dpb_in
        p, t = pl.program_id(0), pl.program_id(1)
        step = p * ntb + t
        rows = pl.ds(pl.multiple_of((ntb - 1 - t) * tb, tb), tb)

        def out_copies(sl):
            return [pltpu.make_async_copy(dq_buf.at[sl], dpb_ref.at[CB_RQ + p, rows], sems.at[sl, 0]),
                    pltpu.make_async_copy(dk_buf.at[sl], dpb_ref.at[CB_RK + p, rows], sems.at[sl, 1]),
                    pltpu.make_async_copy(dv_buf.at[sl], dpb_ref.at[pl.ds(CB_RV + 2 * p, 2), rows], sems.at[sl, 2])]

        slot = _writeback_reserve(step, out_copies)

        @pl.when(t == 0)
        def _():
            dst_ref[...] = jnp.zeros_like(dst_ref)

        consts = [_ret_head_consts(2 * p + hd) for hd in range(2)]

        def chunk(i, carry):
            ci = ncb - 1 - i
            r = pl.ds(pl.multiple_of(ci * CHUNK, CHUNK), CHUNK)
            cs, sn = cos_ref[r, :], sin_ref[r, :]
            q = rq_ref[r, :]
            k = rk_ref[r, :]
            q = q * cs + _swap_halves(q) * sn
            k = (k * cs + _swap_halves(k) * sn) * RET_DK ** -0.5
            dq, dk = None, None
            for hd in range(2):
                dm, zeta, xi, cdec, dm_t = consts[hd]
                lm = _lane_mask(hd)
                qh, kh = q * lm, k * lm
                v = rv_ref[hd, r, :]
                g = do_ref[hd, r, :]
                st0 = rsave_ref[hd, ci]
                dst = dst_ref[hd]
                scores_t = _dot(kh, qh, _NT) * dm_t
                dsc = _dot(g, v, _NT) * dm
                dsc_t = _dot(v, g, _NT) * dm_t
                dqh = _dot(dsc, kh) + _dot(g, st0) * xi
                dkh = _dot(dsc_t, qh) + _dot(v, dst) * zeta
                dv_buf[slot, hd, r, :] = (_dot(scores_t, g) + _dot(kh * zeta, dst, _NT)).astype(dv_buf.dtype)
                dst_ref[hd] = dst * cdec + _dot(g, qh * xi, _TN)
                dq = dqh if dq is None else dq + dqh
                dk = dkh if dk is None else dk + dkh
            dk = dk * (RET_DK ** -0.5)
            dq_buf[slot, r, :] = (dq * cs - _swap_halves(dq) * sn).astype(dq_buf.dtype)
            dk_buf[slot, r, :] = (dk * cs - _swap_halves(dk) * sn).astype(dk_buf.dtype)
            return carry

        lax.fori_loop(0, ncb, chunk, 0, unroll=RET_UNROLL)
        _writeback_commit(step, (N_HEADS // 2) * ntb, slot, out_copies)

    any_spec = pl.BlockSpec(memory_space=pl.ANY)
    return pl.pallas_call(
        body, name="ret_bwd", grid=(N_HEADS // 2, ntb),
        in_specs=[pl.BlockSpec((None, tb, LANE), lambda p, t: (CB_RQ + p, ntb - 1 - t, 0)),
                  pl.BlockSpec((None, tb, LANE), lambda p, t: (CB_RK + p, ntb - 1 - t, 0)),
                  pl.BlockSpec((2, tb, LANE), lambda p, t: (CB_RV // 2 + p, ntb - 1 - t, 0)),
                  pl.BlockSpec((tb, LANE), lambda p, t: (ntb - 1 - t, 0)),
                  pl.BlockSpec((tb, LANE), lambda p, t: (ntb - 1 - t, 0)),
                  pl.BlockSpec((2, tb, LANE), lambda p, t: (p, ntb - 1 - t, 0)),
                  pl.BlockSpec((2, ncb, LANE, LANE), lambda p, t: (p, ntb - 1 - t, 0, 0)),
                  any_spec],
        out_specs=any_spec,
        out_shape=jax.ShapeDtypeStruct(dpb.shape, dpb.dtype),
        scratch_shapes=[pltpu.VMEM((2, tb, LANE), dpb.dtype), pltpu.VMEM((2, tb, LANE), dpb.dtype),
                        pltpu.VMEM((2, 2, tb, LANE), dpb.dtype), pltpu.VMEM((2, LANE, LANE), F32),
                        pltpu.SemaphoreType.DMA((2, 3))],
        input_output_aliases={7: 0},
        compiler_params=_params(("arbitrary", "arbitrary")),
    )(pb, pb, pb, cos_t, sin_t, do, rsave, dpb)


def _row_tile(t_len, want):
    return min(want, t_len)


PAIR_CB = 2 * CB_PER_SHARD


def proj_forward(h, wt, chips, t_len, name, pb=None):
    tm = _row_tile(t_len, 1024)

    def body(chips_ref, h_ref, w_ref, *rest):
        del chips_ref
        o_ref = rest[-1]
        acc = _dot(h_ref[...], w_ref[...], _NT)
        for jj in range(PAIR_CB):
            o_ref[jj] = acc[:, jj * LANE:(jj + 1) * LANE]

    given = [] if pb is None else [pb]
    return pl.pallas_call(
        body, name=name,
        grid_spec=pltpu.PrefetchScalarGridSpec(
            num_scalar_prefetch=1, grid=(chips.shape[0], t_len // tm),
            in_specs=[pl.BlockSpec((tm, D_MODEL), lambda j, i, ch: (i, 0)),
                      pl.BlockSpec((PAIR_CB * LANE, D_MODEL), lambda j, i, ch: (ch[j], 0))]
            + [pl.BlockSpec(memory_space=pl.ANY)] * len(given),
            out_specs=pl.BlockSpec((PAIR_CB, tm, LANE), lambda j, i, ch: (ch[j], i, 0))),
        out_shape=jax.ShapeDtypeStruct((N_CB, t_len, LANE), F32),
        input_output_aliases={3: 0} if given else {},
        compiler_params=_params(("arbitrary", "arbitrary")),
    )(chips, h, wt, *given)


def proj_backward_input(dpb, wt, token, x, dy, norm_g, scale1p, t_len):
    tm = _row_tile(t_len, 512)

    def body(a_ref, wt_hbm, token_ref, x_ref, dy_ref, g_ref, sc_ref, gx_ref, vec_ref, w_ref, sem):
        del token_ref
        i = pl.program_id(0)

        @pl.when(i == 0)
        def _():
            cp = pltpu.make_async_copy(wt_hbm, w_ref, sem)
            cp.start()
            cp.wait()

        a = jnp.concatenate([a_ref[jj].astype(_BF) for jj in range(N_CB)], axis=1)
        dhv = _dot(a, w_ref[...])
        xv, g, sc = x_ref[...], g_ref[...], sc_ref[...]
        r = lax.rsqrt(jnp.mean(xv * xv, axis=-1, keepdims=True) + EPS)
        xn = xv * r
        dxn = dhv * (g * sc)
        gx_ref[...] = dy_ref[...] + r * dxn - xn * (r * r) * jnp.mean(xv * dxn, axis=-1, keepdims=True)
        t = dhv * xn
        _acc_rows(vec_ref, i, [jnp.sum(t * sc, axis=0, keepdims=True),
                               jnp.sum(t * g, axis=0, keepdims=True),
                               jnp.sum(dhv, axis=0, keepdims=True)])

    row = pl.BlockSpec((tm, D_MODEL), lambda i: (i, 0))
    return pl.pallas_call(
        body, name="proj_bwd_input", grid=(t_len // tm,),
        in_specs=[pl.BlockSpec((N_CB, tm, LANE), lambda i: (0, i, 0)),
                  pl.BlockSpec(memory_space=pl.ANY),
                  pl.BlockSpec(token.shape, lambda i: (0, 0)),
                  row, row, _vec_spec(), _vec_spec()],
        out_specs=[row, pl.BlockSpec((8, D_MODEL), lambda i: (0, 0))],
        out_shape=[jax.ShapeDtypeStruct((t_len, D_MODEL), F32), jax.ShapeDtypeStruct((8, D_MODEL), F32)],
        scratch_shapes=[pltpu.VMEM(wt.shape, wt.dtype), pltpu.SemaphoreType.DMA],
        compiler_params=_params(("arbitrary",)),
    )(dpb, wt, token, x, dy, norm_g, scale1p)


def proj_backward_weight(h_t, dpb, t_len):
    tk = _row_tile(t_len, 2048)

    def body(h_ref, b_ref, o_ref):
        k = pl.program_id(1)
        b = jnp.concatenate([b_ref[jj].astype(_BF) for jj in range(PAIR_CB)], axis=1)
        part = _dot(h_ref[...], b)

        @pl.when(k == 0)
        def _():
            for s in range(2):
                o_ref[s] = part[:, s * SHARD_IN:(s + 1) * SHARD_IN]

        @pl.when(k > 0)
        def _():
            for s in range(2):
                o_ref[s] = o_ref[s] + part[:, s * SHARD_IN:(s + 1) * SHARD_IN]

    return pl.pallas_call(
        body, name="proj_bwd_weight", grid=(N_DEV // 2, t_len // tk),
        in_specs=[pl.BlockSpec((D_MODEL, tk), lambda j, k: (0, k)),
                  pl.BlockSpec((PAIR_CB, tk, LANE), lambda j, k: (j, k, 0))],
        out_specs=pl.BlockSpec((2, D_MODEL, SHARD_IN), lambda j, k: (j, 0, 0)),
        out_shape=jax.ShapeDtypeStruct((N_DEV, D_MODEL, SHARD_IN), F32),
        compiler_params=_params(("arbitrary", "arbitrary")),
    )(h_t, dpb)


def sibling_blocks(g_in, c_idx):
    tr = D_MODEL

    def body(c_ref, g_ref, o_ref):
        del c_ref
        o_ref[...] = g_ref[...].astype(o_ref.dtype)

    return pl.pallas_call(
        body, name="sibling_blocks",
        grid_spec=pltpu.PrefetchScalarGridSpec(
            num_scalar_prefetch=1, grid=(N_DEV // 2, D_MODEL // tr),
            in_specs=[pl.BlockSpec((None, tr, SHARD_IN), lambda q, i, c: (2 * q + 1 - c[0], i, 0))],
            out_specs=pl.BlockSpec((None, tr, SHARD_IN), lambda q, i, c: (q, i, 0))),
        out_shape=jax.ShapeDtypeStruct((N_DEV // 2, D_MODEL, SHARD_IN), _BF),
        compiler_params=_params(("arbitrary", "arbitrary")),
    )(c_idx, g_in)


def _vec_spec():
    return pl.BlockSpec((1, D_MODEL), lambda i: (0, 0))


def _acc_rows(ref, i, rows):
    @pl.when(i == 0)
    def _():
        ref[...] = jnp.zeros_like(ref)

    for n, row in enumerate(rows):
        ref[n:n + 1, :] = ref[n:n + 1, :] + row


def adaln_forward(x, norm_g, scale1p, shift, t_len):
    tm = _row_tile(t_len, 1024)

    def body(x_ref, g_ref, sc_ref, sh_ref, h_ref, ht_ref):
        xv = x_ref[...]
        r = lax.rsqrt(jnp.mean(xv * xv, axis=-1, keepdims=True) + EPS)
        h = xv * r * g_ref[...] * sc_ref[...] + sh_ref[...]
        h_ref[...] = h.astype(h_ref.dtype)
        ht_ref[...] = h.T.astype(ht_ref.dtype)

    return pl.pallas_call(
        body, name="adaln_fwd", grid=(t_len // tm,),
        in_specs=[pl.BlockSpec((tm, D_MODEL), lambda i: (i, 0)), _vec_spec(), _vec_spec(), _vec_spec()],
        out_specs=[pl.BlockSpec((tm, D_MODEL), lambda i: (i, 0)), pl.BlockSpec((D_MODEL, tm), lambda i: (0, i))],
        out_shape=[jax.ShapeDtypeStruct((t_len, D_MODEL), _BF), jax.ShapeDtypeStruct((D_MODEL, t_len), _BF)],
        compiler_params=_params(("arbitrary",)),
    )(x, norm_g, scale1p, shift)


def _head_norm(o, g):
    r = lax.rsqrt(jnp.mean(o * o, axis=-1, keepdims=True) + EPS)
    return r, o * r * g


def _group_spec(tm, cb):
    return pl.BlockSpec((N_HEADS, tm, LANE), lambda i, cb=cb: (cb // N_HEADS, i, 0))


MID_FINAL_G, MID_GATE, MID_LOSS, MID_HG_G, MID_RET_G = range(5)


def middle(x, target, oa, ob, pb, wout, gate, final_g, hg_g, ret_g, t_len):
    tm = _row_tile(t_len, 256)
    n_steps = t_len // tm

    def body(x_ref, t_ref, oa_ref, ob_ref, hz_ref, rz_ref, ga_ref, gb_ref, w_ref, gate_ref, fg_ref, hg_ref, rg_ref,
             dy_ref, doa_ref, dob_ref, dw_ref, vec_ref, dpb_ref, m_scr, dm_scr, keep, bufs, sems):
        i = pl.program_id(0)
        rows = pl.ds(pl.multiple_of(i * tm, tm), tm)

        def group_copies(sl):
            return [pltpu.make_async_copy(bufs.at[sl, n], dpb_ref.at[pl.ds(cb, N_HEADS), rows], sems.at[sl, n])
                    for n, cb in enumerate((CB_HZ, CB_RZ, CB_GA, CB_GB))]
        sides = ((oa_ref, hz_ref, ga_ref, hg_ref, doa_ref), (ob_ref, rz_ref, gb_ref, rg_ref, dob_ref))
        for hh in range(N_HEADS):
            ls = slice(hh * LANE, (hh + 1) * LANE)
            acc = None
            for side, (o_ref, z_ref, gt_ref, g_ref, _) in enumerate(sides):
                o = o_ref[hh]
                rr = lax.rsqrt(jnp.mean(o * o, axis=-1, keepdims=True) + EPS)
                orr = o * rr
                zz = z_ref[hh]
                sz = _sigmoid(zz)
                sgt = _sigmoid(gt_ref[hh])
                keep[side, hh, 0] = orr
                keep[side, hh, 1] = sz
                keep[side, hh, 2] = sgt
                keep[side, hh, 3] = jnp.broadcast_to(rr, orr.shape)
                u = sgt * ((orr * g_ref[:, ls]) * (zz * sz))
                acc = u if acc is None else acc + u
            m_scr[:, ls] = acc.astype(m_scr.dtype)
        zv = _dot(m_scr[...], w_ref[...])
        gt, fg = gate_ref[...], fg_ref[...]
        y = x_ref[...] + gt * zv
        r = lax.rsqrt(jnp.mean(y * y, axis=-1, keepdims=True) + EPS)
        yn = y * r
        err = yn * fg - t_ref[...]
        loss = 0.5 * jnp.sum(jnp.mean(err * err, axis=-1, keepdims=True), axis=0, keepdims=True)
        dout = err * (1.0 / D_MODEL)
        gd = dout * fg
        dy = r * gd - yn * (r * r) * jnp.mean(y * gd, axis=-1, keepdims=True)
        dy_ref[...] = dy
        dz = (dy * gt).astype(_BF)
        dm_scr[...] = _dot(dz, w_ref[...], _NT)
        part = _dot(m_scr[...], dz, _TN)

        @pl.when(i == 0)
        def _():
            dw_ref[...] = part

        @pl.when(i > 0)
        def _():
            dw_ref[...] = dw_ref[...] + part

        slot = _writeback_reserve(i, group_copies)
        dg = [[], []]
        for hh in range(N_HEADS):
            ls = slice(hh * LANE, (hh + 1) * LANE)
            dmh = dm_scr[:, ls]
            for side, (o_ref, z_ref, gt_ref, g_ref, do_ref) in enumerate(sides):
                zz, g = z_ref[hh], g_ref[:, ls]
                orr, sz, sgt, rr = keep[side, hh, 0], keep[side, hh, 1], keep[side, hh, 2], keep[side, hh, 3]
                n = orr * g
                silu = zz * sz
                du = dmh * sgt
                bufs[slot, 2 + side, hh] = (dmh * (n * silu) * (sgt * (1.0 - sgt))).astype(bufs.dtype)
                bufs[slot, side, hh] = (du * n * (sz * (1.0 + zz * (1.0 - sz)))).astype(bufs.dtype)
                dn = du * silu
                dg[side].append(jnp.sum(dn * orr, axis=0, keepdims=True))
                gdn = dn * g
                do_ref[hh] = (rr * (gdn - orr * jnp.mean(orr * gdn, axis=-1, keepdims=True))).astype(do_ref.dtype)
        _acc_rows(vec_ref, i, [jnp.sum(dout * yn, axis=0, keepdims=True),
                               jnp.sum(dy * zv, axis=0, keepdims=True),
                               jnp.broadcast_to(loss, (1, D_MODEL)),
                               jnp.concatenate(dg[0], axis=1), jnp.concatenate(dg[1], axis=1)])
        _writeback_commit(i, n_steps, slot, group_copies)

    row = pl.BlockSpec((tm, D_MODEL), lambda i: (i, 0))
    head = pl.BlockSpec((N_HEADS, tm, LANE), lambda i: (0, i, 0))
    full = pl.BlockSpec((D_MODEL, D_MODEL), lambda i: (0, 0))
    return pl.pallas_call(
        body, name="middle", grid=(n_steps,),
        in_specs=[row, row, head, head, _group_spec(tm, CB_HZ), _group_spec(tm, CB_RZ), _group_spec(tm, CB_GA),
                  _group_spec(tm, CB_GB), full, _vec_spec(), _vec_spec(), _vec_spec(), _vec_spec()],
        out_specs=[row, head, head, full, pl.BlockSpec((8, D_MODEL), lambda i: (0, 0)),
                   pl.BlockSpec(memory_space=pl.ANY)],
        out_shape=[jax.ShapeDtypeStruct((t_len, D_MODEL), F32),
                   jax.ShapeDtypeStruct((N_HEADS, t_len, LANE), _BF),
                   jax.ShapeDtypeStruct((N_HEADS, t_len, LANE), _BF),
                   jax.ShapeDtypeStruct((D_MODEL, D_MODEL), F32),
                   jax.ShapeDtypeStruct((8, D_MODEL), F32),
                   jax.ShapeDtypeStruct((N_CB, t_len, LANE), _BF)],
        scratch_shapes=[pltpu.VMEM((tm, D_MODEL), _BF), pltpu.VMEM((tm, D_MODEL), F32),
                        pltpu.VMEM((2, N_HEADS, 4, tm, LANE), F32),
                        pltpu.VMEM((2, 4, N_HEADS, tm, LANE), _BF), pltpu.SemaphoreType.DMA((2, 4))],
        compiler_params=_params(("arbitrary",)),
    )(x, target, oa, ob, pb, pb, pb, pb, wout, gate, final_g, hg_g, ret_g)


def device_step(x, target, mod, lb, project, norm_g, hg_g, ret_g, final_g, c_idx=None, start_exchange=None):
    t_len = x.shape[0]
    shift, scale, gate = mod[:, :D_MODEL], mod[:, D_MODEL:2 * D_MODEL], mod[:, 2 * D_MODEL:]
    scale1p = 1.0 + scale
    cos_t, sin_t = _rope_tables(t_len)
    h, h_t = adaln_forward(x, norm_g, scale1p, shift, t_len)
    pb, wt, wout = project(h)
    oa, ssave, asave = hgrn_forward(pb, lb, t_len)
    ob, rsave = retention_forward(pb, cos_t, sin_t, t_len)
    dy, doa, dob, dwout, vec_mid, dpb = middle(x, target, oa, ob, pb, wout, gate, final_g, hg_g, ret_g, t_len)
    dpb, dlb = hgrn_backward(pb, lb, doa, ssave, asave, dpb, t_len)
    dpb = retention_backward(pb, cos_t, sin_t, dob, rsave, dpb, t_len)
    c_idx = jnp.zeros((1,), jnp.int32) if c_idx is None else c_idx
    dwin = proj_backward_weight(h_t, dpb, t_len)
    dwin_sib = sibling_blocks(dwin, c_idx)
    token, pending = (start_exchange(dwin, dwin_sib, dwout) if start_exchange
                      else (jnp.zeros((8, LANE), F32), None))
    grad_x, vec_ada = proj_backward_input(dpb, wt, token, x, dy, norm_g, scale1p, t_len)
    return grad_x, dwin, dwout, vec_mid, vec_ada, dlb, pending


PACK_ROWS = 16
ROW_NORM_G, ROW_LB, ROW_HG_G, ROW_RET_G, ROW_FINAL_G, ROW_SHIFT, ROW_SCALE, ROW_GATE, ROW_LOSS = range(9)


def _mesh_pos():
    return lax.axis_index("x"), lax.axis_index("y"), lax.axis_index("c")


def _lin(pos):
    return 4 * pos[0] + 2 * pos[1] + pos[2]


def _xor_peer(pos, k):
    return tuple(1 - p if (k >> s) & 1 else p for p, s in zip(pos, (2, 1, 0)))


def _other_chips(pos):
    x, y, _ = pos
    return [(1 - x, y), (x, 1 - y), (1 - x, 1 - y)]


def _remote(src, dst, send_sem, recv_sem, to):
    return pltpu.make_async_remote_copy(src_ref=src, dst_ref=dst, send_sem=send_sem, recv_sem=recv_sem,
                                        device_id=to, device_id_type=MESH)


def pre_exchange(c, w_ada, b_ada, logits, w_in, w_out):
    def body(c_ref, wada_ref, bada_ref, logit_ref, win_ref, wout_ref, mod_ref, scall_ref, lb_ref, wg_ref, wog_ref,
             cg_ref, modall_ref, parts_ref, wt_ref, wo_ref, send1, recv1, send2, recv2, local):
        pos = _mesh_pos()
        placed = [pltpu.make_async_copy(wt_ref, wg_ref.at[_lin(pos)], local.at[0]),
                  pltpu.make_async_copy(wo_ref, wog_ref.at[_lin(pos)], local.at[1])]
        cv = c_ref[...]
        slot = lambda p: pl.ds(pl.multiple_of(8 * _lin(p), 8), 8)
        cg_ref[slot(pos), :] = jnp.broadcast_to(cv * _sigmoid(cv), (8, D_MODEL))
        lb_ref[...] = _sigmoid(logit_ref[0:1, :] - logit_ref[1:2, :])
        peers = [_xor_peer(pos, k) for k in range(1, N_DEV)]
        gather = [_remote(cg_ref.at[slot(pos)], cg_ref.at[slot(pos)], send1.at[n], recv1.at[n], p)
                  for n, p in enumerate(peers)]
        for cp in gather:
            cp.start()
        wt_ref[...] = win_ref[...].T.astype(wt_ref.dtype)
        wo_ref[...] = wout_ref[...].astype(wo_ref.dtype)
        for cp in placed:
            cp.start()
        for n, p in enumerate(peers):
            _remote(cg_ref.at[slot(p)], cg_ref.at[slot(p)], send1.at[n], recv1.at[n], p).wait_recv()
        modall_ref[...] = _dot(cg_ref[...], wada_ref[...])
        scatter = [_remote(modall_ref.at[slot(p)], parts_ref.at[slot(pos)], send2.at[n], recv2.at[n], p)
                   for n, p in enumerate(peers)]
        for cp in scatter:
            cp.start()
        parts_ref[slot(pos), :] = modall_ref[slot(pos), :]
        for n, p in enumerate(peers):
            _remote(modall_ref.at[slot(p)], parts_ref.at[slot(p)], send2.at[n], recv2.at[n], p).wait_recv()
        for cp in gather + scatter:
            cp.wait_send()
        for j in range(N_DEV):
            cols = slice(j * SHARD_ADA, (j + 1) * SHARD_ADA)
            mod_ref[:, cols] = parts_ref[8 * j:8 * j + 1, :] + bada_ref[:, cols]
            scall_ref[j:j + 1, :] = cg_ref[8 * j:8 * j + 1, :]
        for cp in placed:
            cp.wait()

    vmem = pl.BlockSpec(memory_space=pltpu.VMEM)
    return pl.pallas_call(
        body, name="pre_exchange",
        in_specs=[vmem] * 6, out_specs=[vmem] * 3 + [_HBM] * 2,
        out_shape=[jax.ShapeDtypeStruct((1, 3 * D_MODEL), F32), jax.ShapeDtypeStruct((N_DEV, D_MODEL), F32),
                   jax.ShapeDtypeStruct((1, D_MODEL), F32),
                   jax.ShapeDtypeStruct((N_DEV,) + w_in.shape[::-1], _BF),
                   jax.ShapeDtypeStruct((N_DEV,) + w_out.shape, _BF)],
        scratch_shapes=[pltpu.VMEM((N_DEV * 8, D_MODEL), F32), pltpu.VMEM((N_DEV * 8, SHARD_ADA), F32),
                        pltpu.VMEM((N_DEV * 8, SHARD_ADA), F32),
                        pltpu.VMEM(w_in.shape[::-1], _BF), pltpu.VMEM(w_out.shape, _BF)]
        + [pltpu.SemaphoreType.DMA((N_DEV - 1,))] * 4 + [pltpu.SemaphoreType.DMA((2,))],
        compiler_params=pltpu.CompilerParams(vmem_limit_bytes=VMEM_LIMIT),
    )(c, w_ada, b_ada, logits, w_in, w_out)


def _gather_copies(outs, sems):
    pos = _mesh_pos()
    x, y, c = pos
    sibling = (x, y, 1 - c)

    def route(core):
        return [(x + (1 - core) * (1 - 2 * x), y + core * (1 - 2 * y)),
                (x + core * (1 - 2 * x), y + (1 - core) * (1 - 2 * y)),
                (1 - x, 1 - y)]

    mine = [(*chip, c) for chip in route(c)]
    carried = [pos, pos, pos, mine[0], mine[0], mine[1], mine[2]]
    to = [sibling, mine[0], mine[1], mine[1], sibling, sibling, sibling]
    landed = [sibling] + mine + [(*chip, 1 - c) for chip in route(1 - c)]

    def sent(a, k):
        dst = outs[a].at[_lin(carried[k])]
        return _remote(dst, dst, *sems(a, k), to[k])

    def arrival(a, k):
        dst = outs[a].at[_lin(landed[k])]
        return _remote(dst, dst, *sems(a, k), pos)

    return sent, arrival


_ROUNDS = (range(0, 3), range(3, 6), range(6, 7))


def _round_sems(send, recv, rnd):
    n = len(_ROUNDS[rnd])
    return lambda a, k: (send.at[n * a + k - _ROUNDS[rnd][0]], recv.at[n * a + k - _ROUNDS[rnd][0]])


def weight_gather_start(wg, woutg):
    def body(wg_ref, woutg_ref, send, recv, wg_out, woutg_out, token):
        del wg_out, woutg_out
        sent, _ = _gather_copies((wg_ref, woutg_ref), _round_sems(send, recv, 0))
        for a in range(2):
            for k in _ROUNDS[0]:
                sent(a, k).start()
        token[...] = jnp.zeros_like(token)

    hbm = lambda a: pltpu.with_memory_space_constraint(a, pltpu.HBM)
    n = 2 * len(_ROUNDS[0])
    return pl.pallas_call(
        body, name="weight_gather_start",
        in_specs=[_HBM] * 2,
        out_specs=[_SEM, _SEM, _HBM, _HBM, pl.BlockSpec(memory_space=pltpu.VMEM)],
        out_shape=[pltpu.SemaphoreType.DMA((n,)), pltpu.SemaphoreType.DMA((n,)),
                   pltpu.HBM(wg.shape, wg.dtype), pltpu.HBM(woutg.shape, woutg.dtype),
                   jax.ShapeDtypeStruct((8, LANE), F32)],
        input_output_aliases={0: 2, 1: 3},
        compiler_params=pltpu.CompilerParams(has_side_effects=pltpu.SideEffectType.DATAFLOW_SIDE_EFFECTING),
    )(hbm(wg), hbm(woutg))


def _wait_copies(sent, arrival, ks):
    for a in range(2):
        for k in ks:
            arrival(a, k).wait_recv()
            sent(a, k).wait_send()


def weight_gather_wait(name, rnd, ks, send, recv, wg, woutg, after):
    def body(wg_ref, woutg_ref, send, recv, after_ref, wg_out, woutg_out):
        del after_ref, wg_out, woutg_out
        _wait_copies(*_gather_copies((wg_ref, woutg_ref), _round_sems(send, recv, rnd)), ks)

    return pl.pallas_call(
        body, name=name,
        in_specs=[_HBM] * 2 + [_SEM, _SEM, pl.BlockSpec(memory_space=pl.ANY)],
        out_specs=[_HBM] * 2,
        out_shape=[pltpu.HBM(wg.shape, wg.dtype), pltpu.HBM(woutg.shape, woutg.dtype)],
        input_output_aliases={0: 0, 1: 1},
        compiler_params=pltpu.CompilerParams(has_side_effects=pltpu.SideEffectType.DATAFLOW_SIDE_EFFECTING),
    )(wg, woutg, send, recv, after)


def weight_pass_start(wg, woutg):
    def body(wg_ref, woutg_ref, send, recv, wg_out, woutg_out):
        del wg_out, woutg_out
        sent, _ = _gather_copies((wg_ref, woutg_ref), _round_sems(send, recv, 1))
        for a in range(2):
            for k in _ROUNDS[1]:
                sent(a, k).start()

    n = 2 * len(_ROUNDS[1])
    return pl.pallas_call(
        body, name="weight_pass_start",
        in_specs=[_HBM] * 2,
        out_specs=[_SEM, _SEM, _HBM, _HBM],
        out_shape=[pltpu.SemaphoreType.DMA((n,)), pltpu.SemaphoreType.DMA((n,)),
                   pltpu.HBM(wg.shape, wg.dtype), pltpu.HBM(woutg.shape, woutg.dtype)],
        input_output_aliases={0: 2, 1: 3},
        compiler_params=pltpu.CompilerParams(has_side_effects=pltpu.SideEffectType.DATAFLOW_SIDE_EFFECTING),
    )(wg, woutg)


def weight_pass_last(wg, woutg):
    def body(wg_in, woutg_in, wg_ref, woutg_ref, send, recv):
        del wg_in, woutg_in
        sent, arrival = _gather_copies((wg_ref, woutg_ref), _round_sems(send, recv, 2))
        for a in range(2):
            sent(a, _ROUNDS[2][0]).start()
        _wait_copies(sent, arrival, _ROUNDS[2])

    any_spec = pl.BlockSpec(memory_space=pl.ANY)
    n = 2 * len(_ROUNDS[2])
    return pl.pallas_call(
        body, name="weight_pass_last",
        in_specs=[any_spec, any_spec], out_specs=[any_spec, any_spec],
        out_shape=[jax.ShapeDtypeStruct(wg.shape, wg.dtype), jax.ShapeDtypeStruct(woutg.shape, woutg.dtype)],
        scratch_shapes=[pltpu.SemaphoreType.DMA((n,)), pltpu.SemaphoreType.DMA((n,))],
        input_output_aliases={0: 0, 1: 1},
    )(wg, woutg)


def grad_pair_exchange(g_sib, g_out):
    def body(gsib_ref, gout_ref, ra_ref, rb_ref, send, recv):
        pos = _mesh_pos()
        x, y, c = pos
        sibling = (x, y, 1 - c)
        copies = []
        for q in range(4):
            copies.append(_remote(gsib_ref.at[q], ra_ref.at[q], send.at[q], recv.at[q], sibling))
            copies.append(_remote(gout_ref.at[2 * q + (1 - c)], rb_ref.at[q], send.at[4 + q], recv.at[4 + q], sibling))
        for cp in copies:
            cp.start()
        for cp in copies:
            cp.wait_recv()
        for cp in copies:
            cp.wait_send()

    any_spec = pl.BlockSpec(memory_space=pl.ANY)
    return pl.pallas_call(
        body, name="grad_pair_exchange",
        in_specs=[any_spec, any_spec], out_specs=[any_spec, any_spec],
        out_shape=[jax.ShapeDtypeStruct(g_sib.shape, g_sib.dtype), jax.ShapeDtypeStruct((4,) + g_out.shape[1:], F32)],
        scratch_shapes=[pltpu.SemaphoreType.DMA((8,)), pltpu.SemaphoreType.DMA((8,))],
    )(g_sib, g_out)


def pair_sum(g_in, ra, g_out, rb, c_idx):
    tr = D_MODEL

    def body(c_ref, gin_ref, ra_ref, gout_ref, rb_ref, sb_ref, sbo_ref):
        del c_ref
        sb_ref[...] = (gin_ref[...] + ra_ref[...].astype(F32)).astype(sb_ref.dtype)
        sbo_ref[...] = gout_ref[...] + rb_ref[...]

    n_i = D_MODEL // tr
    return pl.pallas_call(
        body, name="pair_sum",
        grid_spec=pltpu.PrefetchScalarGridSpec(
            num_scalar_prefetch=1, grid=(4, n_i),
            in_specs=[pl.BlockSpec((None, tr, SHARD_IN), lambda q, i, c: (2 * q + c[0], i, 0)),
                      pl.BlockSpec((None, tr, SHARD_IN), lambda q, i, c: (q, i, 0)),
                      pl.BlockSpec((None, SHARD_OUT // n_i, D_MODEL), lambda q, i, c: (2 * q + c[0], i, 0)),
                      pl.BlockSpec((None, SHARD_OUT // n_i, D_MODEL), lambda q, i, c: (q, i, 0))],
            out_specs=[pl.BlockSpec((None, tr, SHARD_IN), lambda q, i, c: (q, i, 0)),
                       pl.BlockSpec((None, SHARD_OUT // n_i, D_MODEL), lambda q, i, c: (q, i, 0))]),
        out_shape=[jax.ShapeDtypeStruct(ra.shape, _BF), jax.ShapeDtypeStruct(rb.shape, F32)],
        compiler_params=_params(("arbitrary", "arbitrary")),
    )(c_idx, g_in, ra, g_out, rb)


_HBM = pl.BlockSpec(memory_space=pltpu.HBM)
_SEM = pl.BlockSpec(memory_space=pltpu.SEMAPHORE)
_N_CHIP_COPIES = 6


def _chip_copies(sb_ref, sbo_ref, rc_ref, rco_ref, send, recv):
    pos = _mesh_pos()
    copies = []
    for a, (src, dst) in enumerate(((sb_ref, rc_ref), (sbo_ref, rco_ref))):
        for j, chip in enumerate(_other_chips(pos)):
            copies.append(_remote(src.at[2 * chip[0] + chip[1]], dst.at[j], send.at[3 * a + j], recv.at[3 * a + j],
                                  (*chip, pos[2])))
    return copies


def grad_chip_start(sb, sbo):
    def body(sb_ref, sbo_ref, rc_ref, rco_ref, send, recv, sb_thru, sbo_thru, rc_thru, rco_thru, token):
        del sb_thru, sbo_thru, rc_thru, rco_thru
        for cp in _chip_copies(sb_ref, sbo_ref, rc_ref, rco_ref, send, recv):
            cp.start()
        token[...] = jnp.zeros_like(token)

    hbm = lambda a: pltpu.with_memory_space_constraint(a, pltpu.HBM)
    rc = lax.empty((3,) + sb.shape[1:], sb.dtype)
    rco = lax.empty((3,) + sbo.shape[1:], sbo.dtype)
    return pl.pallas_call(
        body, name="grad_chip_start",
        in_specs=[_HBM] * 4,
        out_specs=[_SEM, _SEM, _HBM, _HBM, _HBM, _HBM, pl.BlockSpec(memory_space=pltpu.VMEM)],
        out_shape=[pltpu.SemaphoreType.DMA((_N_CHIP_COPIES,)), pltpu.SemaphoreType.DMA((_N_CHIP_COPIES,)),
                   pltpu.HBM(sb.shape, sb.dtype), pltpu.HBM(sbo.shape, sbo.dtype),
                   pltpu.HBM(rc.shape, rc.dtype), pltpu.HBM(rco.shape, rco.dtype),
                   jax.ShapeDtypeStruct((8, LANE), F32)],
        input_output_aliases={0: 2, 1: 3, 2: 4, 3: 5},
        compiler_params=pltpu.CompilerParams(has_side_effects=pltpu.SideEffectType.DATAFLOW_SIDE_EFFECTING),
    )(hbm(sb), hbm(sbo), hbm(rc), hbm(rco))


def grad_chip_wait(send, recv, sb, sbo, rc, rco, after):
    def body(sb_ref, sbo_ref, rc_ref, rco_ref, send, recv, after_ref, sb_o, sbo_o, rc_o, rco_o):
        del after_ref, sb_o, sbo_o, rc_o, rco_o
        for cp in _chip_copies(sb_ref, sbo_ref, rc_ref, rco_ref, send, recv):
            cp.wait_send()
            cp.wait_recv()

    return pl.pallas_call(
        body, name="grad_chip_wait",
        in_specs=[_HBM] * 4 + [_SEM, _SEM, pl.BlockSpec(memory_space=pl.ANY)],
        out_specs=[_HBM] * 4,
        out_shape=[pltpu.HBM(sb.shape, sb.dtype), pltpu.HBM(sbo.shape, sbo.dtype),
                   pltpu.HBM(rc.shape, rc.dtype), pltpu.HBM(rco.shape, rco.dtype)],
        input_output_aliases={0: 0, 1: 1, 2: 2, 3: 3},
        compiler_params=pltpu.CompilerParams(has_side_effects=pltpu.SideEffectType.DATAFLOW_SIDE_EFFECTING),
    )(sb, sbo, rc, rco, send, recv, after)


def pack_gather(pack):
    def body(pack_ref, packs_ref, psend, precv):
        pos = _mesh_pos()
        me = _lin(pos)
        packs_ref[me] = pack_ref[...]
        peers = [_xor_peer(pos, k) for k in range(1, N_DEV)]
        gather = [_remote(packs_ref.at[me], packs_ref.at[me], psend.at[n], precv.at[n], p) for n, p in enumerate(peers)]
        for cp in gather:
            cp.start()
        for n, p in enumerate(peers):
            _remote(packs_ref.at[_lin(p)], packs_ref.at[_lin(p)], psend.at[n], precv.at[n], p).wait_recv()
        for cp in gather:
            cp.wait_send()

    vmem = pl.BlockSpec(memory_space=pltpu.VMEM)
    return pl.pallas_call(
        body, name="pack_gather", in_specs=[vmem], out_specs=vmem,
        out_shape=jax.ShapeDtypeStruct((N_DEV,) + pack.shape, F32),
        scratch_shapes=[pltpu.SemaphoreType.DMA((N_DEV - 1,)), pltpu.SemaphoreType.DMA((N_DEV - 1,))],
    )(pack)


def pack_rows(vec_mid, vec_ada, dlb):
    def body(mid_ref, ada_ref, dlb_ref, o_ref):
        mid = lambda r: mid_ref[r:r + 1, :]
        rows = [ada_ref[0:1, :], dlb_ref[...], mid(MID_HG_G), mid(MID_RET_G), mid(MID_FINAL_G),
                ada_ref[2:3, :], ada_ref[1:2, :], mid(MID_GATE), mid(MID_LOSS)]
        o_ref[...] = jnp.zeros_like(o_ref)
        for n, row in enumerate(rows):
            o_ref[n:n + 1, :] = row

    vmem = pl.BlockSpec(memory_space=pltpu.VMEM)
    return pl.pallas_call(body, name="pack_rows", in_specs=[vmem] * 3, out_specs=vmem,
                          out_shape=jax.ShapeDtypeStruct((PACK_ROWS, D_MODEL), F32))(vec_mid, vec_ada, dlb)


def _adamw(w, g, m, v):
    m = ADAM_B1 * m + (1.0 - ADAM_B1) * g
    v = ADAM_B2 * v + (1.0 - ADAM_B2) * (g * g)
    m_hat = m / (1.0 - ADAM_B1 ** ADAM_STEP)
    v_hat = v / (1.0 - ADAM_B2 ** ADAM_STEP)
    delta = -ADAM_LR * (m_hat / (jnp.sqrt(v_hat) + ADAM_EPS) + ADAM_WD * w)
    return delta, m, v


def adam_shard(chip_idx, own, parts, w, m, v, name):
    rows, cols = w.shape
    tr = min(rows, 256)

    def body(chip_ref, p0, p1, p2, p3, w_ref, m_ref, v_ref, g_ref, d_ref, nm_ref, nv_ref):
        del chip_ref
        g = ((p0[...].astype(F32) + p1[...].astype(F32)) + p2[...].astype(F32)) + p3[...].astype(F32)
        g_ref[...] = g
        d_ref[...], nm_ref[...], nv_ref[...] = _adamw(w_ref[...], g, m_ref[...], v_ref[...])

    part = lambda q: pl.BlockSpec((None, tr, cols), lambda i, chip, q=q: (q, i, 0))
    tile = pl.BlockSpec((tr, cols), lambda i, chip: (i, 0))
    return pl.pallas_call(
        body, name=name,
        grid_spec=pltpu.PrefetchScalarGridSpec(
            num_scalar_prefetch=1, grid=(rows // tr,),
            in_specs=[pl.BlockSpec((None, tr, cols), lambda i, chip: (chip[0], i, 0)), part(0), part(1), part(2),
                      tile, tile, tile],
            out_specs=[tile] * 4),
        out_shape=[jax.ShapeDtypeStruct(w.shape, F32)] * 4,
        compiler_params=_params(("arbitrary",)),
    )(chip_idx, own, parts, parts, parts, w, m, v)


def adam_ada(sc_t, dmod_all, me_idx, w, m, v):
    def body(me_ref, sc_ref, dm_ref, w_ref, m_ref, v_ref, g_ref, d_ref, nm_ref, nv_ref):
        del me_ref
        g = _dot_f32(sc_ref[...], dm_ref[...])
        g_ref[...] = g
        d_ref[...], nm_ref[...], nv_ref[...] = _adamw(w_ref[...], g, m_ref[...], v_ref[...])

    full = pl.BlockSpec(w.shape, lambda i, me: (0, 0))
    return pl.pallas_call(
        body, name="adam_ada",
        grid_spec=pltpu.PrefetchScalarGridSpec(
            num_scalar_prefetch=1, grid=(1,),
            in_specs=[pl.BlockSpec(sc_t.shape, lambda i, me: (0, 0)),
                      pl.BlockSpec((LANE, SHARD_ADA), lambda i, me: (0, me[0])), full, full, full],
            out_specs=[full] * 4),
        out_shape=[jax.ShapeDtypeStruct(w.shape, F32)] * 4,
        compiler_params=_params(("arbitrary",)),
    )(me_idx, sc_t, dmod_all, w, m, v)


def adam_vectors(packs, lb, params, ms, vs):
    n = len(params)

    def body(*refs):
        packs_ref, lb_ref = refs[0], refs[1]
        w_refs, m_refs, v_refs = refs[2:2 + n], refs[2 + n:2 + 2 * n], refs[2 + 2 * n:2 + 3 * n]
        loss_ref = refs[2 + 3 * n]
        outs = refs[3 + 3 * n:3 + 7 * n]
        tot_ref = refs[3 + 7 * n]
        tot = packs_ref[0]
        for d in range(1, N_DEV):
            tot = tot + packs_ref[d]
        tot_ref[...] = tot
        row = lambda r: tot_ref[r:r + 1, :]
        lbv = lb_ref[...]
        dl0 = row(ROW_LB) * lbv * (1.0 - lbv)
        grads = [[row(ROW_NORM_G)],
                 [jnp.concatenate([row(ROW_SHIFT), row(ROW_SCALE), row(ROW_GATE)], axis=1)],
                 [dl0, -dl0],
                 [row(ROW_HG_G)], [row(ROW_RET_G)], [row(ROW_FINAL_G)]]
        loss_ref[...] = tot_ref[ROW_LOSS:ROW_LOSS + 1, 0:LANE]
        for j, g_rows in enumerate(grads):
            for r, g in enumerate(g_rows):
                rs = slice(r, r + 1)
                d, nm, nv = _adamw(w_refs[j][rs, :], g, m_refs[j][rs, :], v_refs[j][rs, :])
                outs[4 * j][rs, :] = g
                outs[4 * j + 1][rs, :] = d
                outs[4 * j + 2][rs, :] = nm
                outs[4 * j + 3][rs, :] = nv

    vmem = pl.BlockSpec(memory_space=pltpu.VMEM)
    out_shape = [jax.ShapeDtypeStruct((1, LANE), F32)]
    for w in params:
        out_shape += [jax.ShapeDtypeStruct(w.shape, F32)] * 4
    return pl.pallas_call(
        body, name="adam_vectors", in_specs=[vmem] * (2 + 3 * n), out_specs=[vmem] * len(out_shape),
        out_shape=out_shape, scratch_shapes=[pltpu.VMEM((PACK_ROWS, D_MODEL), F32)],
    )(packs, lb, *params, *ms, *vs)


def kernel(x, c, norm_g, w_ada, b_ada, w_in, hg_lb_logits, hg_norm_g, ret_norm_g, w_out, final_g, loss_target, m_norm_g, m_w_ada, m_b_ada, m_w_in, m_hg_lb_logits, m_hg_norm_g, m_ret_norm_g, m_w_out, m_final_g, v_norm_g, v_w_ada, v_b_ada, v_w_in, v_hg_lb_logits, v_hg_norm_g, v_ret_norm_g, v_w_out, v_final_g):
    pos = _mesh_pos()
    me_idx = jnp.reshape(_lin(pos), (1,)).astype(jnp.int32)
    c_idx = jnp.reshape(pos[2], (1,)).astype(jnp.int32)
    vec = lambda a: a.reshape(1, D_MODEL)

    mod, scall, lb, wtg, woutg = pre_exchange(c, w_ada[0], b_ada, hg_lb_logits, w_in[0], w_out[0])
    chip_idx = jnp.reshape(2 * pos[0] + pos[1], (1,)).astype(jnp.int32)
    gather_send, gather_recv, wtg, woutg, gather_token = weight_gather_start(wtg, woutg)
    mod = mod + gather_token[:1, :1]

    other_chips = jnp.stack([2 * cx + cy for cx, cy in _other_chips(pos)]).astype(jnp.int32)

    def project(h):
        t_len = h.shape[0]
        flat_in = lambda a: a.reshape(D_IN, D_MODEL)
        first = (gather_send, gather_recv)
        wg, wog = weight_gather_wait("weight_wait_sibling", 0, (0,), *first, wtg, woutg, h)
        pb = proj_forward(h, flat_in(wg), chip_idx, t_len, "proj_fwd_own")
        wg, wog = weight_gather_wait("weight_wait_neighbours", 0, (1, 2), *first, wg, wog, pb)
        *second, wg, wog = weight_pass_start(wg, wog)
        wg, wog = weight_gather_wait("weight_wait_passed_sibling", 1, (4, 5), *second, wg, wog, pb)
        pb = proj_forward(h, flat_in(wg), other_chips[:2], t_len, "proj_fwd_neighbours", pb)
        wg, wog = weight_gather_wait("weight_wait_passed_chip", 1, (3,), *second, wg, wog, pb)
        wg, wog = weight_pass_last(wg, wog)
        pb = proj_forward(h, flat_in(wg), other_chips[2:], t_len, "proj_fwd_diagonal", pb)
        return pb, flat_in(wg), wog.reshape(D_MODEL, D_MODEL)

    def start_exchange(dwin, dwin_sib, dwout):
        dwout = dwout.reshape(N_DEV, SHARD_OUT, D_MODEL)
        ra, rb = grad_pair_exchange(dwin_sib, dwout)
        sb, sbo = pair_sum(dwin, ra, dwout, rb, c_idx)
        send, recv, sb, sbo, rc, rco, token = grad_chip_start(sb, sbo)
        return token, (send, recv, sb, sbo, rc, rco)

    grad_x, _, _, vec_mid, vec_ada, dlb, pending = device_step(
        x[0], loss_target[0], mod, lb, project, norm_g, hg_norm_g, ret_norm_g, vec(final_g), c_idx, start_exchange)
    packs = pack_gather(pack_rows(vec_mid, vec_ada, dlb))
    dmod_all = packs[:, ROW_SHIFT:ROW_GATE + 1, :].reshape(N_DEV, 3 * D_MODEL)
    dmod_all = jnp.pad(dmod_all, ((0, LANE - N_DEV), (0, 0)))
    sc_t = jnp.pad(scall.T, ((0, 0), (0, LANE - N_DEV)))
    g_ada, d_ada, nm_ada, nv_ada = adam_ada(sc_t, dmod_all, me_idx, w_ada[0], m_w_ada[0], v_w_ada[0])
    small = adam_vectors(
        packs, lb,
        (norm_g, b_ada, hg_lb_logits, hg_norm_g, ret_norm_g, vec(final_g)),
        (m_norm_g, m_b_ada, m_hg_lb_logits, m_hg_norm_g, m_ret_norm_g, vec(m_final_g)),
        (v_norm_g, v_b_ada, v_hg_lb_logits, v_hg_norm_g, v_ret_norm_g, vec(v_final_g)))
    loss = small[0][0, 0]
    sb, sbo, rc, rco = grad_chip_wait(*pending, small[0])
    g_in, d_in, nm_in, nv_in = adam_shard(chip_idx, sb, rc, w_in[0], m_w_in[0], v_w_in[0], "adam_w_in")
    g_out, d_out, nm_out, nv_out = adam_shard(chip_idx, sbo, rco, w_out[0], m_w_out[0], v_w_out[0], "adam_w_out")
    (g_ng, d_ng, nm_ng, nv_ng), (g_b, d_b, nm_b, nv_b), (g_lb, d_lb, nm_lb, nv_lb), (g_hg, d_hg, nm_hg, nv_hg), \
        (g_rg, d_rg, nm_rg, nv_rg), (g_fg, d_fg, nm_fg, nv_fg) = [small[1 + 4 * j:5 + 4 * j] for j in range(6)]
    flat = lambda a: a.reshape(D_MODEL)

    def group(ng, ada, b, win, lbl, hg, rg, wo, fg):
        return (ng, ada[None], b, win[None], lbl, hg, rg, wo[None], flat(fg))

    return (loss, grad_x[None],
            *group(g_ng, g_ada, g_b, g_in, g_lb, g_hg, g_rg, g_out, g_fg),
            *group(d_ng, d_ada, d_b, d_in, d_lb, d_hg, d_rg, d_out, d_fg),
            *group(nm_ng, nm_ada, nm_b, nm_in, nm_lb, nm_hg, nm_rg, nm_out, nm_fg),
            *group(nv_ng, nv_ada, nv_b, nv_in, nv_lb, nv_hg, nv_rg, nv_out, nv_fg))
```

```python
import numpy as np
import jax
import jax.numpy as jnp
from jax import lax
from jax.experimental import pallas as pl
from jax.experimental.pallas import tpu as pltpu

F32 = jnp.float32
_BF = jnp.bfloat16

D_MODEL = 1024
N_HEADS = 8
LANE = 128
RET_DK = 64
D_IN = 9216
N_DEV = 8
SHARD_IN = D_IN // N_DEV
SHARD_ADA = 3 * D_MODEL // N_DEV
SHARD_OUT = D_MODEL // N_DEV
N_CB = D_IN // LANE
CB_PER_SHARD = SHARD_IN // LANE
CHUNK = 128
N_LEVELS = 7
EPS = 1e-6
LOG2_E = float(np.log2(np.e))
ROPE_BASE = 10000.0
CB_HQ, CB_HF, CB_HI, CB_HZ, CB_RQ, CB_RK, CB_RV, CB_RZ, CB_GA, CB_GB = 0, 8, 16, 24, 32, 36, 40, 48, 56, 64
VMEM_LIMIT = 56 * 1024 * 1024

ADAM_LR, ADAM_B1, ADAM_B2, ADAM_EPS, ADAM_WD, ADAM_STEP = 0.001, 0.9, 0.999, 1e-08, 0.01, 10

_NN = (((1,), (0,)), ((), ()))
_NT = (((1,), (1,)), ((), ()))
_TN = (((0,), (0,)), ((), ()))
MESH = pl.DeviceIdType.MESH


def _dot(a, b, dims=_NN):
    return lax.dot_general(a.astype(_BF), b.astype(_BF), dims, preferred_element_type=F32)


def _split2(a):
    hi = a.astype(_BF)
    lo = (a - hi.astype(F32)).astype(_BF)
    return jnp.concatenate([hi, lo], axis=1)


def _dot_sel(sel, a):
    n = a.shape[1]
    r = lax.dot_general(sel.astype(_BF), _split2(a), _NN, preferred_element_type=F32)
    return r[:, :n] + r[:, n:]


def _dot_f32(a, b):
    def pieces(v):
        p1 = v.astype(_BF)
        r1 = v - p1.astype(F32)
        p2 = r1.astype(_BF)
        p3 = (r1 - p2.astype(F32)).astype(_BF)
        return p1, p2, p3
    a1, a2, a3 = pieces(a)
    b1, b2, b3 = pieces(b)
    d = lambda u, v: lax.dot_general(u, v, _NN, preferred_element_type=F32)
    return ((d(a1, b3) + d(a2, b2) + d(a3, b1)) + (d(a1, b2) + d(a2, b1))) + d(a1, b1)


def _sigmoid(v):
    return 1.0 / (1.0 + jnp.exp(-v))


def _params(sem=None):
    return pltpu.CompilerParams(dimension_semantics=sem, vmem_limit_bytes=VMEM_LIMIT)


def _hgrn_consts():
    c, nl = CHUNK, N_LEVELS
    t = np.arange(c)[:, None]
    j = np.arange(c)[None, :]
    sel = [j <= t]
    masks = [j == t]
    for l in range(1, nl + 1):
        m = ((t >> l) << l) + (1 << (l - 1)) - 1
        sec = t > m
        sel.append(np.where(sec, (j > m) & (j <= t), (j > t) & (j <= m)))
        same = (t >> l) == (j >> l)
        masks.append(same & sec & (j <= m))
    sel.append(j > t)
    sel = np.concatenate(sel, 0).astype(np.float32)
    masks = np.stack(masks).astype(np.float32)
    sgn = np.stack([np.where((t & (1 << (l - 1))) != 0, 1.0, -1.0) * np.ones((1, LANE)) for l in range(3, nl + 1)])
    return dict(tri=jnp.asarray(sel[:c], _BF),
                lvl=jnp.asarray(masks, F32),
                sgn=jnp.asarray(sgn, F32),
                sel_t=jnp.asarray(sel.T, _BF),
                lvl_b=jnp.asarray(masks, _BF),
                lvlt_b=jnp.asarray(np.swapaxes(masks, 1, 2), _BF))


def _level_exponents(b, logf, b_scr, sgn_ref):
    c = CHUNK
    b_scr[...] = b
    row = lax.broadcasted_iota(jnp.int32, (c, LANE), 0)
    nxt = pltpu.roll(logf, c - 1, 0)
    prv = pltpu.roll(logf, 1, 0)
    r4 = row & 3
    out = [jnp.where((row & 1) == 1, logf, 0.0),
           jnp.where(r4 == 0, nxt, jnp.where(r4 == 1, 0.0, jnp.where(r4 == 2, logf, logf + prv)))]
    for l in range(3, N_LEVELS + 1):
        size, half = 1 << l, 1 << (l - 1)
        ref = jnp.concatenate([jnp.broadcast_to(b_scr[i * size + half - 1:i * size + half, :], (size, LANE))
                               for i in range(c // size)], axis=0)
        out.append((b - ref) * sgn_ref[l - 3])
    return out


def _hgrn_chunk(hq, hf, hi, lbv, tri_ref, sgn_ref, b_scr):
    sq = _sigmoid(hq)
    q = hq * sq
    sg = _sigmoid(hf)
    omlb = 1.0 - lbv
    f = lbv + omlb * sg
    k = 1.0 - f
    logf = jnp.log(f) * LOG2_E
    b = _dot_sel(tri_ref[...], logf)
    bc = jnp.sum(logf, axis=0, keepdims=True)
    lev = [None] + [jnp.exp2(e) for e in _level_exponents(b, logf, b_scr, sgn_ref)]
    return dict(sq=sq, q=q, sg=sg, omlb=omlb, f=f, k=k, v=hi, eb=jnp.exp2(b), erem=jnp.exp2(bc - b),
                ebc=jnp.exp2(bc), lev=lev)


def _blockdiag(a, b):
    z = jnp.zeros_like(a)
    return jnp.concatenate([jnp.concatenate([a, z], axis=1), jnp.concatenate([z, b], axis=1)], axis=0)


def _level_operands(a):
    q, k = a["q"].astype(_BF), a["k"].astype(_BF)
    lev = [None] + [a["lev"][l].astype(_BF) for l in range(1, N_LEVELS + 1)]
    ql = [q] + [q * lev[l] for l in range(1, N_LEVELS + 1)]
    kl = [k] + [k * lev[l] for l in range(1, N_LEVELS + 1)]
    pairs = range(0, N_LEVELS + 1, 2)
    return ([jnp.concatenate([ql[l], ql[l + 1]], axis=1) for l in pairs], [_blockdiag(kl[l], kl[l + 1]) for l in pairs],
            ql, kl)


def _hgrn_scores(a, lvl_ref, q_pairs, k_diags):
    acc = None
    for n, (qp, kd) in enumerate(zip(q_pairs, k_diags)):
        both = lax.dot_general(qp, kd, _NT, preferred_element_type=F32)
        part = lvl_ref[2 * n] * both[:, :CHUNK] + lvl_ref[2 * n + 1] * both[:, CHUNK:]
        acc = part if acc is None else acc + part
    return acc


SCAN_UNROLL = 8
FWD_UNROLL = 16
RET_UNROLL = 16


def _writeback_reserve(step, make_copies):
    slot = step % 2

    @pl.when(step >= 2)
    def _():
        for cp in make_copies(slot):
            cp.wait()

    return slot


def _writeback_commit(step, n_steps, slot, make_copies):
    for cp in make_copies(slot):
        cp.start()

    @pl.when(step == n_steps - 1)
    def _():
        for cp in make_copies(slot):
            cp.wait()
        if n_steps > 1:
            for cp in make_copies(1 - slot):
                cp.wait()


def _resident(const):
    zeros = (0,) * const.ndim
    return pl.BlockSpec(const.shape, lambda p, t: zeros)


def _time_block(t_len):
    return min(t_len, 2048)


def hgrn_forward(pb, lb, t_len):
    nc = t_len // CHUNK
    tb = _time_block(t_len)
    ncb = tb // CHUNK
    consts = _hgrn_consts()
    operands = [consts[n] for n in ("tri", "lvl", "sgn")]

    def body(hq_ref, hf_ref, hi_ref, lb_ref, tri_ref, lvl_ref, sgn_ref, o_ref, ssave_ref, asave_ref, st_ref, b_scr):
        @pl.when(pl.program_id(1) == 0)
        def _():
            st_ref[...] = jnp.zeros_like(st_ref)

        def chunk(ci, carry):
            r = pl.ds(pl.multiple_of(ci * CHUNK, CHUNK), CHUNK)
            for hd in range(2):
                lbv = lb_ref[:, hd * LANE:(hd + 1) * LANE]
                a = _hgrn_chunk(hq_ref[hd, r, :], hf_ref[hd, r, :], hi_ref[hd, r, :], lbv, tri_ref, sgn_ref,
                                b_scr.at[hd])
                q_pairs, k_diags, _, _ = _level_operands(a)
                st = st_ref[hd]
                ssave_ref[hd, ci] = st
                scores = _hgrn_scores(a, lvl_ref, q_pairs, k_diags).astype(asave_ref.dtype)
                asave_ref[hd, ci] = scores
                o_ref[hd, r, :] = _dot(a["q"] * a["eb"], st, _NT) + _dot(scores, a["v"])
                st_ref[hd] = st * a["ebc"] + _dot(a["v"], a["k"] * a["erem"], _TN)
            return carry

        lax.fori_loop(0, ncb, chunk, 0, unroll=FWD_UNROLL)

    pair = lambda base: pl.BlockSpec((2, tb, LANE), lambda p, t, base=base: (base // 2 + p, t, 0))
    per_chunk = pl.BlockSpec((2, ncb, LANE, LANE), lambda p, t: (p, t, 0, 0))
    return pl.pallas_call(
        body, name="hgrn_fwd", grid=(N_HEADS // 2, t_len // tb),
        in_specs=[pair(CB_HQ), pair(CB_HF), pair(CB_HI),
                  pl.BlockSpec((1, 2 * LANE), lambda p, t: (0, p))] + [_resident(c) for c in operands],
        out_specs=[pl.BlockSpec((2, tb, LANE), lambda p, t: (p, t, 0)), per_chunk, per_chunk],
        out_shape=[jax.ShapeDtypeStruct((N_HEADS, t_len, LANE), F32),
                   jax.ShapeDtypeStruct((N_HEADS, nc, LANE, LANE), F32),
                   jax.ShapeDtypeStruct((N_HEADS, nc, CHUNK, CHUNK), _BF)],
        scratch_shapes=[pltpu.VMEM((2, LANE, LANE), F32), pltpu.VMEM((2, CHUNK, LANE), F32)],
        compiler_params=_params(("arbitrary", "arbitrary")),
    )(pb, pb, pb, lb, *operands)


def hgrn_backward(pb, lb, do, ssave, asave, dpb, t_len):
    tb = _time_block(t_len)
    ncb, ntb = tb // CHUNK, t_len // tb
    consts = _hgrn_consts()
    operands = [consts[n] for n in ("tri", "sgn", "sel_t", "lvl_b", "lvlt_b")]

    def body(hq_ref, hf_ref, hi_ref, lb_ref, do_ref, ssave_ref, asave_ref, tri_ref, sgn_ref, selt_ref, lvlb_ref,
             lvltb_ref, dpb_in, dpb_ref, dlb_ref, dq_buf, df_buf, di_buf, dst_ref, b_scr, sems):
        del dpb_in
        p, t = pl.program_id(0), pl.program_id(1)
        step = p * ntb + t
        rows = pl.ds(pl.multiple_of((ntb - 1 - t) * tb, tb), tb)

        def out_copies(sl):
            return [pltpu.make_async_copy(buf.at[sl], dpb_ref.at[pl.ds(base + 2 * p, 2), rows], sems.at[sl, n])
                    for n, (buf, base) in enumerate(((dq_buf, CB_HQ), (df_buf, CB_HF), (di_buf, CB_HI)))]

        slot = _writeback_reserve(step, out_copies)

        @pl.when(t == 0)
        def _():
            dst_ref[...] = jnp.zeros_like(dst_ref)
            dlb_ref[...] = jnp.zeros_like(dlb_ref)

        def chunk(i, carry):
            ci = ncb - 1 - i
            r = pl.ds(pl.multiple_of(ci * CHUNK, CHUNK), CHUNK)
            for hd in range(2):
                head_chunk(hd, ci, r)
            return carry

        def head_chunk(hd, ci, r):
            lbv = lb_ref[:, hd * LANE:(hd + 1) * LANE]
            hq = hq_ref[hd, r, :]
            a = _hgrn_chunk(hq, hf_ref[hd, r, :], hi_ref[hd, r, :], lbv, tri_ref, sgn_ref, b_scr.at[hd])
            _, k_diags, ql, kl = _level_operands(a)
            q, k, v = a["q"], a["k"], a["v"]
            g = do_ref[hd, r, :]
            st0 = ssave_ref[hd, ci]
            dst = dst_ref[hd]
            scores = asave_ref[hd, ci]
            da = _dot(g, v, _NT)
            da_t = _dot(v, g, _NT)
            kb = k * a["erem"]
            qb = q * a["eb"]
            dv = _dot(scores, g, _TN) + _dot(kb, dst, _NT)
            dq_inter = _dot(g, st0) * a["eb"]
            dk_state = _dot(v, dst) * a["erem"]
            dq, dk = dq_inter, dk_state
            de = [q * dq_inter]
            da_b, dat_b = da.astype(_BF), da_t.astype(_BF)
            for n in range(len(k_diags)):
                l0, l1 = 2 * n, 2 * n + 1
                da_pair = jnp.concatenate([lvlb_ref[l0] * da_b, lvlb_ref[l1] * da_b], axis=1)
                dat_pair = jnp.concatenate([lvltb_ref[l0] * dat_b, lvltb_ref[l1] * dat_b], axis=1)
                dq_both = lax.dot_general(da_pair, k_diags[n], _NN, preferred_element_type=F32)
                dk_both = lax.dot_general(dat_pair, _blockdiag(ql[l0], ql[l1]), _NN, preferred_element_type=F32)
                for l, cols in ((l0, slice(0, LANE)), (l1, slice(LANE, 2 * LANE))):
                    dql, dkl = dq_both[:, cols], dk_both[:, cols]
                    if l > 0:
                        e = a["lev"][l]
                        dql, dkl = dql * e, dkl * e
                        de.append(q * dql + k * dkl)
                    dq = dq + dql
                    dk = dk + dkl
            de.append(k * dk_state)
            dst_ref[hd] = dst * a["ebc"] + _dot(g, qb, _TN)
            dbc = jnp.sum(dst * st0, axis=0, keepdims=True) * a["ebc"]
            de2 = lax.dot_general(selt_ref[...], _split2(jnp.concatenate(de, axis=0)), _NN,
                                  preferred_element_type=F32)
            dlogf = de2[:, :LANE] + de2[:, LANE:] + dbc
            sq, sg = a["sq"], a["sg"]
            df = dlogf / a["f"] - dk
            dq_buf[slot, hd, r, :] = (dq * (sq * (1.0 + hq * (1.0 - sq)))).astype(dq_buf.dtype)
            df_buf[slot, hd, r, :] = (df * a["omlb"] * sg * (1.0 - sg)).astype(df_buf.dtype)
            di_buf[slot, hd, r, :] = dv.astype(di_buf.dtype)
            cols = slice(hd * LANE, (hd + 1) * LANE)
            dlb_ref[:, cols] = dlb_ref[:, cols] + jnp.sum(df * (1.0 - sg), axis=0, keepdims=True)

        lax.fori_loop(0, ncb, chunk, 0, unroll=SCAN_UNROLL)
        _writeback_commit(step, (N_HEADS // 2) * ntb, slot, out_copies)

    pair = lambda base: pl.BlockSpec((2, tb, LANE), lambda p, t, base=base: (base // 2 + p, ntb - 1 - t, 0))
    any_spec = pl.BlockSpec(memory_space=pl.ANY)
    per_chunk = pl.BlockSpec((2, ncb, LANE, LANE), lambda p, t: (p, ntb - 1 - t, 0, 0))
    return pl.pallas_call(
        body, name="hgrn_bwd", grid=(N_HEADS // 2, ntb),
        in_specs=[pair(CB_HQ), pair(CB_HF), pair(CB_HI),
                  pl.BlockSpec((1, 2 * LANE), lambda p, t: (0, p)),
                  pair(0), per_chunk, per_chunk]
        + [_resident(c) for c in operands] + [any_spec],
        out_specs=[any_spec, pl.BlockSpec((1, 2 * LANE), lambda p, t: (0, p))],
        out_shape=[jax.ShapeDtypeStruct(dpb.shape, dpb.dtype), jax.ShapeDtypeStruct((1, D_MODEL), F32)],
        scratch_shapes=[pltpu.VMEM((2, 2, tb, LANE), dpb.dtype)] * 3 + [
            pltpu.VMEM((2, LANE, LANE), F32), pltpu.VMEM((2, CHUNK, LANE), F32), pltpu.SemaphoreType.DMA((2, 3))],
        input_output_aliases={7 + len(operands): 0},
        compiler_params=_params(("arbitrary", "arbitrary")),
    )(pb, pb, pb, lb, do, ssave, asave, *operands, dpb)


def _rope_tables(t_len):
    half = RET_DK // 2
    inv_freq = (1.0 / (np.float32(ROPE_BASE) ** np.linspace(0.0, 1.0, half, dtype=np.float32))).astype(np.float32)
    ang = (np.arange(t_len, dtype=np.float32)[:, None] * inv_freq[None, :]).astype(np.float64)
    cos, sin = np.cos(ang).astype(np.float32), np.sin(ang).astype(np.float32)
    cos_t = np.concatenate([cos, cos, cos, cos], axis=1)
    sin_t = np.concatenate([-sin, sin, -sin, sin], axis=1)
    return jnp.asarray(cos_t), jnp.asarray(sin_t)


def _swap_halves(v):
    half = RET_DK // 2
    lane = lax.broadcasted_iota(jnp.int32, v.shape, 1)
    first = (lane & (RET_DK - 1)) < half
    return jnp.where(first, pltpu.roll(v, LANE - half, 1), pltpu.roll(v, half, 1))


def _ret_head_consts(hidx):
    c = CHUNK
    hf = jnp.full((1, LANE), hidx, jnp.int32).astype(F32)
    lg = jnp.log(1.0 - jnp.exp(-(5.0 + hf) * np.float32(np.log(2.0))))
    row = lax.broadcasted_iota(jnp.int32, (c, c), 0)
    col = lax.broadcasted_iota(jnp.int32, (c, c), 1)
    rel = (row - col).astype(F32)
    dm = jnp.where(rel >= 0, jnp.exp(lg[:, :1] * jnp.maximum(rel, 0.0)), 0.0)
    dm_t = jnp.where(rel <= 0, jnp.exp(lg[:, :1] * jnp.maximum(-rel, 0.0)), 0.0)
    idx = lax.broadcasted_iota(jnp.int32, (c, LANE), 0).astype(F32)
    zeta = jnp.exp(lg * (c - 1.0 - idx))
    xi = jnp.exp(lg * (idx + 1.0))
    cdec = jnp.exp(lg * float(c))
    return dm, zeta, xi, cdec, dm_t


def _lane_mask(which):
    lane = lax.broadcasted_iota(jnp.int32, (1, LANE), 1)
    return ((lane // RET_DK) == which).astype(F32)


def retention_forward(pb, cos_t, sin_t, t_len):
    nc = t_len // CHUNK

    tb = _time_block(t_len)
    ncb = tb // CHUNK

    def body(rq_ref, rk_ref, rv_ref, cos_ref, sin_ref, o_ref, rsave_ref, st_ref):
        p = pl.program_id(0)

        @pl.when(pl.program_id(1) == 0)
        def _():
            st_ref[...] = jnp.zeros_like(st_ref)

        consts = [_ret_head_consts(2 * p + hd) for hd in range(2)]

        def chunk(ci, carry):
            r = pl.ds(pl.multiple_of(ci * CHUNK, CHUNK), CHUNK)
            cs, sn = cos_ref[r, :], sin_ref[r, :]
            q = rq_ref[r, :]
            k = rk_ref[r, :]
            q = q * cs + _swap_halves(q) * sn
            k = (k * cs + _swap_halves(k) * sn) * RET_DK ** -0.5
            for hd in range(2):
                dm, zeta, xi, cdec, _ = consts[hd]
                lm = _lane_mask(hd)
                qh, kh = q * lm, k * lm
                v = rv_ref[hd, r, :]
                st = st_ref[hd]
                rsave_ref[hd, ci] = st
                scores = _dot(qh, kh, _NT) * dm
                o_ref[hd, r, :] = _dot(scores, v) + _dot(qh * xi, st, _NT)
                st_ref[hd] = st * cdec + _dot(v, kh * zeta, _TN)
            return carry

        lax.fori_loop(0, ncb, chunk, 0, unroll=RET_UNROLL)

    return pl.pallas_call(
        body, name="ret_fwd", grid=(N_HEADS // 2, t_len // tb),
        in_specs=[pl.BlockSpec((None, tb, LANE), lambda p, t: (CB_RQ + p, t, 0)),
                  pl.BlockSpec((None, tb, LANE), lambda p, t: (CB_RK + p, t, 0)),
                  pl.BlockSpec((2, tb, LANE), lambda p, t: (CB_RV // 2 + p, t, 0)),
                  pl.BlockSpec((tb, LANE), lambda p, t: (t, 0)),
                  pl.BlockSpec((tb, LANE), lambda p, t: (t, 0))],
        out_specs=[pl.BlockSpec((2, tb, LANE), lambda p, t: (p, t, 0)),
                   pl.BlockSpec((2, ncb, LANE, LANE), lambda p, t: (p, t, 0, 0))],
        out_shape=[jax.ShapeDtypeStruct((N_HEADS, t_len, LANE), F32),
                   jax.ShapeDtypeStruct((N_HEADS, nc, LANE, LANE), F32)],
        scratch_shapes=[pltpu.VMEM((2, LANE, LANE), F32)],
        compiler_params=_params(("arbitrary", "arbitrary")),
    )(pb, pb, pb, cos_t, sin_t)


def retention_backward(pb, cos_t, sin_t, do, rsave, dpb, t_len):
    tb = _time_block(t_len)
    ncb, ntb = tb // CHUNK, t_len // tb

    def body(rq_ref, rk_ref, rv_ref, cos_ref, sin_ref, do_ref, rsave_ref, dpb_in,
             dpb_ref, dq_buf, dk_buf, dv_buf, dst_ref, sems):
        del dpb_in
        p, t = pl.program_id(0), pl.program_id(1)
        step = p * ntb + t
        rows = pl.ds(pl.multiple_of((ntb - 1 - t) * tb, tb), tb)

        def out_copies(sl):
            return [pltpu.make_async_copy(dq_buf.at[sl], dpb_ref.at[CB_RQ + p, rows], sems.at[sl, 0]),
                    pltpu.make_async_copy(dk_buf.at[sl], dpb_ref.at[CB_RK + p, rows], sems.at[sl, 1]),
                    pltpu.make_async_copy(dv_buf.at[sl], dpb_ref.at[pl.ds(CB_RV + 2 * p, 2), rows], sems.at[sl, 2])]

        slot = _writeback_reserve(step, out_copies)

        @pl.when(t == 0)
        def _():
            dst_ref[...] = jnp.zeros_like(dst_ref)

        consts = [_ret_head_consts(2 * p + hd) for hd in range(2)]

        def chunk(i, carry):
            ci = ncb - 1 - i
            r = pl.ds(pl.multiple_of(ci * CHUNK, CHUNK), CHUNK)
            cs, sn = cos_ref[r, :], sin_ref[r, :]
            q = rq_ref[r, :]
            k = rk_ref[r, :]
            q = q * cs + _swap_halves(q) * sn
            k = (k * cs + _swap_halves(k) * sn) * RET_DK ** -0.5
            dq, dk = None, None
            for hd in range(2):
                dm, zeta, xi, cdec, dm_t = consts[hd]
                lm = _lane_mask(hd)
                qh, kh = q * lm, k * lm
                v = rv_ref[hd, r, :]
                g = do_ref[hd, r, :]
                st0 = rsave_ref[hd, ci]
                dst = dst_ref[hd]
                scores_t = _dot(kh, qh, _NT) * dm_t
                dsc = _dot(g, v, _NT) * dm
                dsc_t = _dot(v, g, _NT) * dm_t
                dqh = _dot(dsc, kh) + _dot(g, st0) * xi
                dkh = _dot(dsc_t, qh) + _dot(v, dst) * zeta
                dv_buf[slot, hd, r, :] = (_dot(scores_t, g) + _dot(kh * zeta, dst, _NT)).astype(dv_buf.dtype)
                dst_ref[hd] = dst * cdec + _dot(g, qh * xi, _TN)
                dq = dqh if dq is None else dq + dqh
                dk = dkh if dk is None else dk + dkh
            dk = dk * (RET_DK ** -0.5)
            dq_buf[slot, r, :] = (dq * cs - _swap_halves(dq) * sn).astype(dq_buf.dtype)
            dk_buf[slot, r, :] = (dk * cs - _swap_halves(dk) * sn).astype(dk_buf.dtype)
            return carry

        lax.fori_loop(0, ncb, chunk, 0, unroll=RET_UNROLL)
        _writeback_commit(step, (N_HEADS // 2) * ntb, slot, out_copies)

    any_spec = pl.BlockSpec(memory_space=pl.ANY)
    return pl.pallas_call(
        body, name="ret_bwd", grid=(N_HEADS // 2, ntb),
        in_specs=[pl.BlockSpec((None, tb, LANE), lambda p, t: (CB_RQ + p, ntb - 1 - t, 0)),
                  pl.BlockSpec((None, tb, LANE), lambda p, t: (CB_RK + p, ntb - 1 - t, 0)),
                  pl.BlockSpec((2, tb, LANE), lambda p, t: (CB_RV // 2 + p, ntb - 1 - t, 0)),
                  pl.BlockSpec((tb, LANE), lambda p, t: (ntb - 1 - t, 0)),
                  pl.BlockSpec((tb, LANE), lambda p, t: (ntb - 1 - t, 0)),
                  pl.BlockSpec((2, tb, LANE), lambda p, t: (p, ntb - 1 - t, 0)),
                  pl.BlockSpec((2, ncb, LANE, LANE), lambda p, t: (p, ntb - 1 - t, 0, 0)),
                  any_spec],
        out_specs=any_spec,
        out_shape=jax.ShapeDtypeStruct(dpb.shape, dpb.dtype),
        scratch_shapes=[pltpu.VMEM((2, tb, LANE), dpb.dtype), pltpu.VMEM((2, tb, LANE), dpb.dtype),
                        pltpu.VMEM((2, 2, tb, LANE), dpb.dtype), pltpu.VMEM((2, LANE, LANE), F32),
                        pltpu.SemaphoreType.DMA((2, 3))],
        input_output_aliases={7: 0},
        compiler_params=_params(("arbitrary", "arbitrary")),
    )(pb, pb, pb, cos_t, sin_t, do, rsave, dpb)


def _row_tile(t_len, want):
    return min(want, t_len)


PAIR_CB = 2 * CB_PER_SHARD


def proj_forward(h, wt, chips, t_len, name, pb=None):
    tm = _row_tile(t_len, 1024)

    def body(chips_ref, h_ref, w_ref, *rest):
        del chips_ref
        o_ref = rest[-1]
        acc = _dot(h_ref[...], w_ref[...], _NT)
        for jj in range(PAIR_CB):
            o_ref[jj] = acc[:, jj * LANE:(jj + 1) * LANE]

    given = [] if pb is None else [pb]
    return pl.pallas_call(
        body, name=name,
        grid_spec=pltpu.PrefetchScalarGridSpec(
            num_scalar_prefetch=1, grid=(chips.shape[0], t_len // tm),
            in_specs=[pl.BlockSpec((tm, D_MODEL), lambda j, i, ch: (i, 0)),
                      pl.BlockSpec((PAIR_CB * LANE, D_MODEL), lambda j, i, ch: (ch[j], 0))]
            + [pl.BlockSpec(memory_space=pl.ANY)] * len(given),
            out_specs=pl.BlockSpec((PAIR_CB, tm, LANE), lambda j, i, ch: (ch[j], i, 0))),
        out_shape=jax.ShapeDtypeStruct((N_CB, t_len, LANE), F32),
        input_output_aliases={3: 0} if given else {},
        compiler_params=_params(("arbitrary", "arbitrary")),
    )(chips, h, wt, *given)


def proj_backward_input(dpb, wt, token, x, dy, norm_g, scale1p, t_len):
    tm = _row_tile(t_len, 512)

    def body(a_ref, wt_hbm, token_ref, x_ref, dy_ref, g_ref, sc_ref, gx_ref, vec_ref, w_ref, sem):
        del token_ref
        i = pl.program_id(0)

        @pl.when(i == 0)
        def _():
            cp = pltpu.make_async_copy(wt_hbm, w_ref, sem)
            cp.start()
            cp.wait()

        a = jnp.concatenate([a_ref[jj].astype(_BF) for jj in range(N_CB)], axis=1)
        dhv = _dot(a, w_ref[...])
        xv, g, sc = x_ref[...], g_ref[...], sc_ref[...]
        r = lax.rsqrt(jnp.mean(xv * xv, axis=-1, keepdims=True) + EPS)
        xn = xv * r
        dxn = dhv * (g * sc)
        gx_ref[...] = dy_ref[...] + r * dxn - xn * (r * r) * jnp.mean(xv * dxn, axis=-1, keepdims=True)
        t = dhv * xn
        _acc_rows(vec_ref, i, [jnp.sum(t * sc, axis=0, keepdims=True),
                               jnp.sum(t * g, axis=0, keepdims=True),
                               jnp.sum(dhv, axis=0, keepdims=True)])

    row = pl.BlockSpec((tm, D_MODEL), lambda i: (i, 0))
    return pl.pallas_call(
        body, name="proj_bwd_input", grid=(t_len // tm,),
        in_specs=[pl.BlockSpec((N_CB, tm, LANE), lambda i: (0, i, 0)),
                  pl.BlockSpec(memory_space=pl.ANY),
                  pl.BlockSpec(token.shape, lambda i: (0, 0)),
                  row, row, _vec_spec(), _vec_spec()],
        out_specs=[row, pl.BlockSpec((8, D_MODEL), lambda i: (0, 0))],
        out_shape=[jax.ShapeDtypeStruct((t_len, D_MODEL), F32), jax.ShapeDtypeStruct((8, D_MODEL), F32)],
        scratch_shapes=[pltpu.VMEM(wt.shape, wt.dtype), pltpu.SemaphoreType.DMA],
        compiler_params=_params(("arbitrary",)),
    )(dpb, wt, token, x, dy, norm_g, scale1p)


def proj_backward_weight(h_t, dpb, t_len):
    tk = _row_tile(t_len, 2048)

    def body(h_ref, b_ref, o_ref):
        k = pl.program_id(1)
        b = jnp.concatenate([b_ref[jj].astype(_BF) for jj in range(PAIR_CB)], axis=1)
        part = _dot(h_ref[...], b)

        @pl.when(k == 0)
        def _():
            for s in range(2):
                o_ref[s] = part[:, s * SHARD_IN:(s + 1) * SHARD_IN]

        @pl.when(k > 0)
        def _():
            for s in range(2):
                o_ref[s] = o_ref[s] + part[:, s * SHARD_IN:(s + 1) * SHARD_IN]

    return pl.pallas_call(
        body, name="proj_bwd_weight", grid=(N_DEV // 2, t_len // tk),
        in_specs=[pl.BlockSpec((D_MODEL, tk), lambda j, k: (0, k)),
                  pl.BlockSpec((PAIR_CB, tk, LANE), lambda j, k: (j, k, 0))],
        out_specs=pl.BlockSpec((2, D_MODEL, SHARD_IN), lambda j, k: (j, 0, 0)),
        out_shape=jax.ShapeDtypeStruct((N_DEV, D_MODEL, SHARD_IN), F32),
        compiler_params=_params(("arbitrary", "arbitrary")),
    )(h_t, dpb)


def sibling_blocks(g_in, c_idx):
    tr = D_MODEL

    def body(c_ref, g_ref, o_ref):
        del c_ref
        o_ref[...] = g_ref[...].astype(o_ref.dtype)

    return pl.pallas_call(
        body, name="sibling_blocks",
        grid_spec=pltpu.PrefetchScalarGridSpec(
            num_scalar_prefetch=1, grid=(N_DEV // 2, D_MODEL // tr),
            in_specs=[pl.BlockSpec((None, tr, SHARD_IN), lambda q, i, c: (2 * q + 1 - c[0], i, 0))],
            out_specs=pl.BlockSpec((None, tr, SHARD_IN), lambda q, i, c: (q, i, 0))),
        out_shape=jax.ShapeDtypeStruct((N_DEV // 2, D_MODEL, SHARD_IN), _BF),
        compiler_params=_params(("arbitrary", "arbitrary")),
    )(c_idx, g_in)


def _vec_spec():
    return pl.BlockSpec((1, D_MODEL), lambda i: (0, 0))


def _acc_rows(ref, i, rows):
    @pl.when(i == 0)
    def _():
        ref[...] = jnp.zeros_like(ref)

    for n, row in enumerate(rows):
        ref[n:n + 1, :] = ref[n:n + 1, :] + row


def adaln_forward(x, norm_g, scale1p, shift, t_len):
    tm = _row_tile(t_len, 1024)

    def body(x_ref, g_ref, sc_ref, sh_ref, h_ref, ht_ref):
        xv = x_ref[...]
        r = lax.rsqrt(jnp.mean(xv * xv, axis=-1, keepdims=True) + EPS)
        h = xv * r * g_ref[...] * sc_ref[...] + sh_ref[...]
        h_ref[...] = h.astype(h_ref.dtype)
        ht_ref[...] = h.T.astype(ht_ref.dtype)

    return pl.pallas_call(
        body, name="adaln_fwd", grid=(t_len // tm,),
        in_specs=[pl.BlockSpec((tm, D_MODEL), lambda i: (i, 0)), _vec_spec(), _vec_spec(), _vec_spec()],
        out_specs=[pl.BlockSpec((tm, D_MODEL), lambda i: (i, 0)), pl.BlockSpec((D_MODEL, tm), lambda i: (0, i))],
        out_shape=[jax.ShapeDtypeStruct((t_len, D_MODEL), _BF), jax.ShapeDtypeStruct((D_MODEL, t_len), _BF)],
        compiler_params=_params(("arbitrary",)),
    )(x, norm_g, scale1p, shift)


def _head_norm(o, g):
    r = lax.rsqrt(jnp.mean(o * o, axis=-1, keepdims=True) + EPS)
    return r, o * r * g


def _group_spec(tm, cb):
    return pl.BlockSpec((N_HEADS, tm, LANE), lambda i, cb=cb: (cb // N_HEADS, i, 0))


MID_FINAL_G, MID_GATE, MID_LOSS, MID_HG_G, MID_RET_G = range(5)


def middle(x, target, oa, ob, pb, wout, gate, final_g, hg_g, ret_g, t_len):
    tm = _row_tile(t_len, 256)
    n_steps = t_len // tm

    def body(x_ref, t_ref, oa_ref, ob_ref, hz_ref, rz_ref, ga_ref, gb_ref, w_ref, gate_ref, fg_ref, hg_ref, rg_ref,
             dy_ref, doa_ref, dob_ref, dw_ref, vec_ref, dpb_ref, m_scr, dm_scr, keep, bufs, sems):
        i = pl.program_id(0)
        rows = pl.ds(pl.multiple_of(i * tm, tm), tm)

        def group_copies(sl):
            return [pltpu.make_async_copy(bufs.at[sl, n], dpb_ref.at[pl.ds(cb, N_HEADS), rows], sems.at[sl, n])
                    for n, cb in enumerate((CB_HZ, CB_RZ, CB_GA, CB_GB))]
        sides = ((oa_ref, hz_ref, ga_ref, hg_ref, doa_ref), (ob_ref, rz_ref, gb_ref, rg_ref, dob_ref))
        for hh in range(N_HEADS):
            ls = slice(hh * LANE, (hh + 1) * LANE)
            acc = None
            for side, (o_ref, z_ref, gt_ref, g_ref, _) in enumerate(sides):
                o = o_ref[hh]
                rr = lax.rsqrt(jnp.mean(o * o, axis=-1, keepdims=True) + EPS)
                orr = o * rr
                zz = z_ref[hh]
                sz = _sigmoid(zz)
                sgt = _sigmoid(gt_ref[hh])
                keep[side, hh, 0] = orr
                keep[side, hh, 1] = sz
                keep[side, hh, 2] = sgt
                keep[side, hh, 3] = jnp.broadcast_to(rr, orr.shape)
                u = sgt * ((orr * g_ref[:, ls]) * (zz * sz))
                acc = u if acc is None else acc + u
            m_scr[:, ls] = acc.astype(m_scr.dtype)
        zv = _dot(m_scr[...], w_ref[...])
        gt, fg = gate_ref[...], fg_ref[...]
        y = x_ref[...] + gt * zv
        r = lax.rsqrt(jnp.mean(y * y, axis=-1, keepdims=True) + EPS)
        yn = y * r
        err = yn * fg - t_ref[...]
        loss = 0.5 * jnp.sum(jnp.mean(err * err, axis=-1, keepdims=True), axis=0, keepdims=True)
        dout = err * (1.0 / D_MODEL)
        gd = dout * fg
        dy = r * gd - yn * (r * r) * jnp.mean(y * gd, axis=-1, keepdims=True)
        dy_ref[...] = dy
        dz = (dy * gt).astype(_BF)
        dm_scr[...] = _dot(dz, w_ref[...], _NT)
        part = _dot(m_scr[...], dz, _TN)

        @pl.when(i == 0)
        def _():
            dw_ref[...] = part

        @pl.when(i > 0)
        def _():
            dw_ref[...] = dw_ref[...] + part

        slot = _writeback_reserve(i, group_copies)
        dg = [[], []]
        for hh in range(N_HEADS):
            ls = slice(hh * LANE, (hh + 1) * LANE)
            dmh = dm_scr[:, ls]
            for side, (o_ref, z_ref, gt_ref, g_ref, do_ref) in enumerate(sides):
                zz, g = z_ref[hh], g_ref[:, ls]
                orr, sz, sgt, rr = keep[side, hh, 0], keep[side, hh, 1], keep[side, hh, 2], keep[side, hh, 3]
                n = orr * g
                silu = zz * sz
                du = dmh * sgt
                bufs[slot, 2 + side, hh] = (dmh * (n * silu) * (sgt * (1.0 - sgt))).astype(bufs.dtype)
                bufs[slot, side, hh] = (du * n * (sz * (1.0 + zz * (1.0 - sz)))).astype(bufs.dtype)
                dn = du * silu
                dg[side].append(jnp.sum(dn * orr, axis=0, keepdims=True))
                gdn = dn * g
                do_ref[hh] = (rr * (gdn - orr * jnp.mean(orr * gdn, axis=-1, keepdims=True))).astype(do_ref.dtype)
        _acc_rows(vec_ref, i, [jnp.sum(dout * yn, axis=0, keepdims=True),
                               jnp.sum(dy * zv, axis=0, keepdims=True),
                               jnp.broadcast_to(loss, (1, D_MODEL)),
                               jnp.concatenate(dg[0], axis=1), jnp.concatenate(dg[1], axis=1)])
        _writeback_commit(i, n_steps, slot, group_copies)

    row = pl.BlockSpec((tm, D_MODEL), lambda i: (i, 0))
    head = pl.BlockSpec((N_HEADS, tm, LANE), lambda i: (0, i, 0))
    full = pl.BlockSpec((D_MODEL, D_MODEL), lambda i: (0, 0))
    return pl.pallas_call(
        body, name="middle", grid=(n_steps,),
        in_specs=[row, row, head, head, _group_spec(tm, CB_HZ), _group_spec(tm, CB_RZ), _group_spec(tm, CB_GA),
                  _group_spec(tm, CB_GB), full, _vec_spec(), _vec_spec(), _vec_spec(), _vec_spec()],
        out_specs=[row, head, head, full, pl.BlockSpec((8, D_MODEL), lambda i: (0, 0)),
                   pl.BlockSpec(memory_space=pl.ANY)],
        out_shape=[jax.ShapeDtypeStruct((t_len, D_MODEL), F32),
                   jax.ShapeDtypeStruct((N_HEADS, t_len, LANE), _BF),
                   jax.ShapeDtypeStruct((N_HEADS, t_len, LANE), _BF),
                   jax.ShapeDtypeStruct((D_MODEL, D_MODEL), F32),
                   jax.ShapeDtypeStruct((8, D_MODEL), F32),
                   jax.ShapeDtypeStruct((N_CB, t_len, LANE), _BF)],
        scratch_shapes=[pltpu.VMEM((tm, D_MODEL), _BF), pltpu.VMEM((tm, D_MODEL), F32),
                        pltpu.VMEM((2, N_HEADS, 4, tm, LANE), F32),
                        pltpu.VMEM((2, 4, N_HEADS, tm, LANE), _BF), pltpu.SemaphoreType.DMA((2, 4))],
        compiler_params=_params(("arbitrary",)),
    )(x, target, oa, ob, pb, pb, pb, pb, wout, gate, final_g, hg_g, ret_g)


def device_step(x, target, mod, lb, project, norm_g, hg_g, ret_g, final_g, c_idx=None, start_exchange=None):
    t_len = x.shape[0]
    shift, scale, gate = mod[:, :D_MODEL], mod[:, D_MODEL:2 * D_MODEL], mod[:, 2 * D_MODEL:]
    scale1p = 1.0 + scale
    cos_t, sin_t = _rope_tables(t_len)
    h, h_t = adaln_forward(x, norm_g, scale1p, shift, t_len)
    pb, wt, wout = project(h)
    oa, ssave, asave = hgrn_forward(pb, lb, t_len)
    ob, rsave = retention_forward(pb, cos_t, sin_t, t_len)
    dy, doa, dob, dwout, vec_mid, dpb = middle(x, target, oa, ob, pb, wout, gate, final_g, hg_g, ret_g, t_len)
    dpb, dlb = hgrn_backward(pb, lb, doa, ssave, asave, dpb, t_len)
    dpb = retention_backward(pb, cos_t, sin_t, dob, rsave, dpb, t_len)
    c_idx = jnp.zeros((1,), jnp.int32) if c_idx is None else c_idx
    dwin = proj_backward_weight(h_t, dpb, t_len)
    dwin_sib = sibling_blocks(dwin, c_idx)
    token, pending = (start_exchange(dwin, dwin_sib, dwout) if start_exchange
                      else (jnp.zeros((8, LANE), F32), None))
    grad_x, vec_ada = proj_backward_input(dpb, wt, token, x, dy, norm_g, scale1p, t_len)
    return grad_x, dwin, dwout, vec_mid, vec_ada, dlb, pending


PACK_ROWS = 16
ROW_NORM_G, ROW_LB, ROW_HG_G, ROW_RET_G, ROW_FINAL_G, ROW_SHIFT, ROW_SCALE, ROW_GATE, ROW_LOSS = range(9)


def _mesh_pos():
    return lax.axis_index("x"), lax.axis_index("y"), lax.axis_index("c")


def _lin(pos):
    return 4 * pos[0] + 2 * pos[1] + pos[2]


def _xor_peer(pos, k):
    return tuple(1 - p if (k >> s) & 1 else p for p, s in zip(pos, (2, 1, 0)))


def _other_chips(pos):
    x, y, _ = pos
    return [(1 - x, y), (x, 1 - y), (1 - x, 1 - y)]


def _remote(src, dst, send_sem, recv_sem, to):
    return pltpu.make_async_remote_copy(src_ref=src, dst_ref=dst, send_sem=send_sem, recv_sem=recv_sem,
                                        device_id=to, device_id_type=MESH)


def pre_exchange(c, w_ada, b_ada, logits, w_in, w_out):
    def body(c_ref, wada_ref, bada_ref, logit_ref, win_ref, wout_ref, mod_ref, scall_ref, lb_ref, wg_ref, wog_ref,
             cg_ref, modall_ref, parts_ref, wt_ref, wo_ref, send1, recv1, send2, recv2, local):
        pos = _mesh_pos()
        placed = [pltpu.make_async_copy(wt_ref, wg_ref.at[_lin(pos)], local.at[0]),
                  pltpu.make_async_copy(wo_ref, wog_ref.at[_lin(pos)], local.at[1])]
        cv = c_ref[...]
        slot = lambda p: pl.ds(pl.multiple_of(8 * _lin(p), 8), 8)
        cg_ref[slot(pos), :] = jnp.broadcast_to(cv * _sigmoid(cv), (8, D_MODEL))
        lb_ref[...] = _sigmoid(logit_ref[0:1, :] - logit_ref[1:2, :])
        peers = [_xor_peer(pos, k) for k in range(1, N_DEV)]
        gather = [_remote(cg_ref.at[slot(pos)], cg_ref.at[slot(pos)], send1.at[n], recv1.at[n], p)
                  for n, p in enumerate(peers)]
        for cp in gather:
            cp.start()
        wt_ref[...] = win_ref[...].T.astype(wt_ref.dtype)
        wo_ref[...] = wout_ref[...].astype(wo_ref.dtype)
        for cp in placed:
            cp.start()
        for n, p in enumerate(peers):
            _remote(cg_ref.at[slot(p)], cg_ref.at[slot(p)], send1.at[n], recv1.at[n], p).wait_recv()
        modall_ref[...] = _dot(cg_ref[...], wada_ref[...])
        scatter = [_remote(modall_ref.at[slot(p)], parts_ref.at[slot(pos)], send2.at[n], recv2.at[n], p)
                   for n, p in enumerate(peers)]
        for cp in scatter:
            cp.start()
        parts_ref[slot(pos), :] = modall_ref[slot(pos), :]
        for n, p in enumerate(peers):
            _remote(modall_ref.at[slot(p)], parts_ref.at[slot(p)], send2.at[n], recv2.at[n], p).wait_recv()
        for cp in gather + scatter:
            cp.wait_send()
        for j in range(N_DEV):
            cols = slice(j * SHARD_ADA, (j + 1) * SHARD_ADA)
            mod_ref[:, cols] = parts_ref[8 * j:8 * j + 1, :] + bada_ref[:, cols]
            scall_ref[j:j + 1, :] = cg_ref[8 * j:8 * j + 1, :]
        for cp in placed:
            cp.wait()

    vmem = pl.BlockSpec(memory_space=pltpu.VMEM)
    return pl.pallas_call(
        body, name="pre_exchange",
        in_specs=[vmem] * 6, out_specs=[vmem] * 3 + [_HBM] * 2,
        out_shape=[jax.ShapeDtypeStruct((1, 3 * D_MODEL), F32), jax.ShapeDtypeStruct((N_DEV, D_MODEL), F32),
                   jax.ShapeDtypeStruct((1, D_MODEL), F32),
                   jax.ShapeDtypeStruct((N_DEV,) + w_in.shape[::-1], _BF),
                   jax.ShapeDtypeStruct((N_DEV,) + w_out.shape, _BF)],
        scratch_shapes=[pltpu.VMEM((N_DEV * 8, D_MODEL), F32), pltpu.VMEM((N_DEV * 8, SHARD_ADA), F32),
                        pltpu.VMEM((N_DEV * 8, SHARD_ADA), F32),
                        pltpu.VMEM(w_in.shape[::-1], _BF), pltpu.VMEM(w_out.shape, _BF)]
        + [pltpu.SemaphoreType.DMA((N_DEV - 1,))] * 4 + [pltpu.SemaphoreType.DMA((2,))],
        compiler_params=pltpu.CompilerParams(vmem_limit_bytes=VMEM_LIMIT),
    )(c, w_ada, b_ada, logits, w_in, w_out)


def _gather_copies(outs, sems):
    pos = _mesh_pos()
    x, y, c = pos
    sibling = (x, y, 1 - c)

    def route(core):
        return [(x + (1 - core) * (1 - 2 * x), y + core * (1 - 2 * y)),
                (x + core * (1 - 2 * x), y + (1 - core) * (1 - 2 * y)),
                (1 - x, 1 - y)]

    mine = [(*chip, c) for chip in route(c)]
    carried = [pos, pos, pos, mine[0], mine[0], mine[1], mine[2]]
    to = [sibling, mine[0], mine[1], mine[1], sibling, sibling, sibling]
    landed = [sibling] + mine + [(*chip, 1 - c) for chip in route(1 - c)]

    def sent(a, k):
        dst = outs[a].at[_lin(carried[k])]
        return _remote(dst, dst, *sems(a, k), to[k])

    def arrival(a, k):
        dst = outs[a].at[_lin(landed[k])]
        return _remote(dst, dst, *sems(a, k), pos)

    return sent, arrival


_ROUNDS = (range(0, 3), range(3, 6), range(6, 7))


def _round_sems(send, recv, rnd):
    n = len(_ROUNDS[rnd])
    return lambda a, k: (send.at[n * a + k - _ROUNDS[rnd][0]], recv.at[n * a + k - _ROUNDS[rnd][0]])


def weight_gather_start(wg, woutg):
    def body(wg_ref, woutg_ref, send, recv, wg_out, woutg_out, token):
        del wg_out, woutg_out
        sent, _ = _gather_copies((wg_ref, woutg_ref), _round_sems(send, recv, 0))
        for a in range(2):
            for k in _ROUNDS[0]:
                sent(a, k).start()
        token[...] = jnp.zeros_like(token)

    hbm = lambda a: pltpu.with_memory_space_constraint(a, pltpu.HBM)
    n = 2 * len(_ROUNDS[0])
    return pl.pallas_call(
        body, name="weight_gather_start",
        in_specs=[_HBM] * 2,
        out_specs=[_SEM, _SEM, _HBM, _HBM, pl.BlockSpec(memory_space=pltpu.VMEM)],
        out_shape=[pltpu.SemaphoreType.DMA((n,)), pltpu.SemaphoreType.DMA((n,)),
                   pltpu.HBM(wg.shape, wg.dtype), pltpu.HBM(woutg.shape, woutg.dtype),
                   jax.ShapeDtypeStruct((8, LANE), F32)],
        input_output_aliases={0: 2, 1: 3},
        compiler_params=pltpu.CompilerParams(has_side_effects=pltpu.SideEffectType.DATAFLOW_SIDE_EFFECTING),
    )(hbm(wg), hbm(woutg))


def _wait_copies(sent, arrival, ks):
    for a in range(2):
        for k in ks:
            arrival(a, k).wait_recv()
            sent(a, k).wait_send()


def weight_gather_wait(name, rnd, ks, send, recv, wg, woutg, after):
    def body(wg_ref, woutg_ref, send, recv, after_ref, wg_out, woutg_out):
        del after_ref, wg_out, woutg_out
        _wait_copies(*_gather_copies((wg_ref, woutg_ref), _round_sems(send, recv, rnd)), ks)

    return pl.pallas_call(
        body, name=name,
        in_specs=[_HBM] * 2 + [_SEM, _SEM, pl.BlockSpec(memory_space=pl.ANY)],
        out_specs=[_HBM] * 2,
        out_shape=[pltpu.HBM(wg.shape, wg.dtype), pltpu.HBM(woutg.shape, woutg.dtype)],
        input_output_aliases={0: 0, 1: 1},
        compiler_params=pltpu.CompilerParams(has_side_effects=pltpu.SideEffectType.DATAFLOW_SIDE_EFFECTING),
    )(wg, woutg, send, recv, after)


def weight_pass_start(wg, woutg):
    def body(wg_ref, woutg_ref, send, recv, wg_out, woutg_out):
        del wg_out, woutg_out
        sent, _ = _gather_copies((wg_ref, woutg_ref), _round_sems(send, recv, 1))
        for a in range(2):
            for k in _ROUNDS[1]:
                sent(a, k).start()

    n = 2 * len(_ROUNDS[1])
    return pl.pallas_call(
        body, name="weight_pass_start",
        in_specs=[_HBM] * 2,
        out_specs=[_SEM, _SEM, _HBM, _HBM],
        out_shape=[pltpu.SemaphoreType.DMA((n,)), pltpu.SemaphoreType.DMA((n,)),
                   pltpu.HBM(wg.shape, wg.dtype), pltpu.HBM(woutg.shape, woutg.dtype)],
        input_output_aliases={0: 2, 1: 3},
        compiler_params=pltpu.CompilerParams(has_side_effects=pltpu.SideEffectType.DATAFLOW_SIDE_EFFECTING),
    )(wg, woutg)


def weight_pass_last(wg, woutg):
    def body(wg_in, woutg_in, wg_ref, woutg_ref, send, recv):
        del wg_in, woutg_in
        sent, arrival = _gather_copies((wg_ref, woutg_ref), _round_sems(send, recv, 2))
        for a in range(2):
            sent(a, _ROUNDS[2][0]).start()
        _wait_copies(sent, arrival, _ROUNDS[2])

    any_spec = pl.BlockSpec(memory_space=pl.ANY)
    n = 2 * len(_ROUNDS[2])
    return pl.pallas_call(
        body, name="weight_pass_last",
        in_specs=[any_spec, any_spec], out_specs=[any_spec, any_spec],
        out_shape=[jax.ShapeDtypeStruct(wg.shape, wg.dtype), jax.ShapeDtypeStruct(woutg.shape, woutg.dtype)],
        scratch_shapes=[pltpu.SemaphoreType.DMA((n,)), pltpu.SemaphoreType.DMA((n,))],
        input_output_aliases={0: 0, 1: 1},
    )(wg, woutg)


def grad_pair_exchange(g_sib, g_out):
    def body(gsib_ref, gout_ref, ra_ref, rb_ref, send, recv):
        pos = _mesh_pos()
        x, y, c = pos
        sibling = (x, y, 1 - c)
        copies = []
        for q in range(4):
            copies.append(_remote(gsib_ref.at[q], ra_ref.at[q], send.at[q], recv.at[q], sibling))
            copies.append(_remote(gout_ref.at[2 * q + (1 - c)], rb_ref.at[q], send.at[4 + q], recv.at[4 + q], sibling))
        for cp in copies:
            cp.start()
        for cp in copies:
            cp.wait_recv()
        for cp in copies:
            cp.wait_send()

    any_spec = pl.BlockSpec(memory_space=pl.ANY)
    return pl.pallas_call(
        body, name="grad_pair_exchange",
        in_specs=[any_spec, any_spec], out_specs=[any_spec, any_spec],
        out_shape=[jax.ShapeDtypeStruct(g_sib.shape, g_sib.dtype), jax.ShapeDtypeStruct((4,) + g_out.shape[1:], F32)],
        scratch_shapes=[pltpu.SemaphoreType.DMA((8,)), pltpu.SemaphoreType.DMA((8,))],
    )(g_sib, g_out)


def pair_sum(g_in, ra, g_out, rb, c_idx):
    tr = D_MODEL

    def body(c_ref, gin_ref, ra_ref, gout_ref, rb_ref, sb_ref, sbo_ref):
        del c_ref
        sb_ref[...] = (gin_ref[...] + ra_ref[...].astype(F32)).astype(sb_ref.dtype)
        sbo_ref[...] = gout_ref[...] + rb_ref[...]

    n_i = D_MODEL // tr
    return pl.pallas_call(
        body, name="pair_sum",
        grid_spec=pltpu.PrefetchScalarGridSpec(
            num_scalar_prefetch=1, grid=(4, n_i),
            in_specs=[pl.BlockSpec((None, tr, SHARD_IN), lambda q, i, c: (2 * q + c[0], i, 0)),
                      pl.BlockSpec((None, tr, SHARD_IN), lambda q, i, c: (q, i, 0)),
                      pl.BlockSpec((None, SHARD_OUT // n_i, D_MODEL), lambda q, i, c: (2 * q + c[0], i, 0)),
                      pl.BlockSpec((None, SHARD_OUT // n_i, D_MODEL), lambda q, i, c: (q, i, 0))],
            out_specs=[pl.BlockSpec((None, tr, SHARD_IN), lambda q, i, c: (q, i, 0)),
                       pl.BlockSpec((None, SHARD_OUT // n_i, D_MODEL), lambda q, i, c: (q, i, 0))]),
        out_shape=[jax.ShapeDtypeStruct(ra.shape, _BF), jax.ShapeDtypeStruct(rb.shape, F32)],
        compiler_params=_params(("arbitrary", "arbitrary")),
    )(c_idx, g_in, ra, g_out, rb)


_HBM = pl.BlockSpec(memory_space=pltpu.HBM)
_SEM = pl.BlockSpec(memory_space=pltpu.SEMAPHORE)
_N_CHIP_COPIES = 6


def _chip_copies(sb_ref, sbo_ref, rc_ref, rco_ref, send, recv):
    pos = _mesh_pos()
    copies = []
    for a, (src, dst) in enumerate(((sb_ref, rc_ref), (sbo_ref, rco_ref))):
        for j, chip in enumerate(_other_chips(pos)):
            copies.append(_remote(src.at[2 * chip[0] + chip[1]], dst.at[j], send.at[3 * a + j], recv.at[3 * a + j],
                                  (*chip, pos[2])))
    return copies


def grad_chip_start(sb, sbo):
    def body(sb_ref, sbo_ref, rc_ref, rco_ref, send, recv, sb_thru, sbo_thru, rc_thru, rco_thru, token):
        del sb_thru, sbo_thru, rc_thru, rco_thru
        for cp in _chip_copies(sb_ref, sbo_ref, rc_ref, rco_ref, send, recv):
            cp.start()
        token[...] = jnp.zeros_like(token)

    hbm = lambda a: pltpu.with_memory_space_constraint(a, pltpu.HBM)
    rc = lax.empty((3,) + sb.shape[1:], sb.dtype)
    rco = lax.empty((3,) + sbo.shape[1:], sbo.dtype)
    return pl.pallas_call(
        body, name="grad_chip_start",
        in_specs=[_HBM] * 4,
        out_specs=[_SEM, _SEM, _HBM, _HBM, _HBM, _HBM, pl.BlockSpec(memory_space=pltpu.VMEM)],
        out_shape=[pltpu.SemaphoreType.DMA((_N_CHIP_COPIES,)), pltpu.SemaphoreType.DMA((_N_CHIP_COPIES,)),
                   pltpu.HBM(sb.shape, sb.dtype), pltpu.HBM(sbo.shape, sbo.dtype),
                   pltpu.HBM(rc.shape, rc.dtype), pltpu.HBM(rco.shape, rco.dtype),
                   jax.ShapeDtypeStruct((8, LANE), F32)],
        input_output_aliases={0: 2, 1: 3, 2: 4, 3: 5},
        compiler_params=pltpu.CompilerParams(has_side_effects=pltpu.SideEffectType.DATAFLOW_SIDE_EFFECTING),
    )(hbm(sb), hbm(sbo), hbm(rc), hbm(rco))


def grad_chip_wait(send, recv, sb, sbo, rc, rco, after):
    def body(sb_ref, sbo_ref, rc_ref, rco_ref, send, recv, after_ref, sb_o, sbo_o, rc_o, rco_o):
        del after_ref, sb_o, sbo_o, rc_o, rco_o
        for cp in _chip_copies(sb_ref, sbo_ref, rc_ref, rco_ref, send, recv):
            cp.wait_send()
            cp.wait_recv()

    return pl.pallas_call(
        body, name="grad_chip_wait",
        in_specs=[_HBM] * 4 + [_SEM, _SEM, pl.BlockSpec(memory_space=pl.ANY)],
        out_specs=[_HBM] * 4,
        out_shape=[pltpu.HBM(sb.shape, sb.dtype), pltpu.HBM(sbo.shape, sbo.dtype),
                   pltpu.HBM(rc.shape, rc.dtype), pltpu.HBM(rco.shape, rco.dtype)],
        input_output_aliases={0: 0, 1: 1, 2: 2, 3: 3},
        compiler_params=pltpu.CompilerParams(has_side_effects=pltpu.SideEffectType.DATAFLOW_SIDE_EFFECTING),
    )(sb, sbo, rc, rco, send, recv, after)


def pack_gather(pack):
    def body(pack_ref, packs_ref, psend, precv):
        pos = _mesh_pos()
        me = _lin(pos)
        packs_ref[me] = pack_ref[...]
        peers = [_xor_peer(pos, k) for k in range(1, N_DEV)]
        gather = [_remote(packs_ref.at[me], packs_ref.at[me], psend.at[n], precv.at[n], p) for n, p in enumerate(peers)]
        for cp in gather:
            cp.start()
        for n, p in enumerate(peers):
            _remote(packs_ref.at[_lin(p)], packs_ref.at[_lin(p)], psend.at[n], precv.at[n], p).wait_recv()
        for cp in gather:
            cp.wait_send()

    vmem = pl.BlockSpec(memory_space=pltpu.VMEM)
    return pl.pallas_call(
        body, name="pack_gather", in_specs=[vmem], out_specs=vmem,
        out_shape=jax.ShapeDtypeStruct((N_DEV,) + pack.shape, F32),
        scratch_shapes=[pltpu.SemaphoreType.DMA((N_DEV - 1,)), pltpu.SemaphoreType.DMA((N_DEV - 1,))],
    )(pack)


def pack_rows(vec_mid, vec_ada, dlb):
    def body(mid_ref, ada_ref, dlb_ref, o_ref):
        mid = lambda r: mid_ref[r:r + 1, :]
        rows = [ada_ref[0:1, :], dlb_ref[...], mid(MID_HG_G), mid(MID_RET_G), mid(MID_FINAL_G),
                ada_ref[2:3, :], ada_ref[1:2, :], mid(MID_GATE), mid(MID_LOSS)]
        o_ref[...] = jnp.zeros_like(o_ref)
        for n, row in enumerate(rows):
            o_ref[n:n + 1, :] = row

    vmem = pl.BlockSpec(memory_space=pltpu.VMEM)
    return pl.pallas_call(body, name="pack_rows", in_specs=[vmem] * 3, out_specs=vmem,
                          out_shape=jax.ShapeDtypeStruct((PACK_ROWS, D_MODEL), F32))(vec_mid, vec_ada, dlb)


def _adamw(w, g, m, v):
    m = ADAM_B1 * m + (1.0 - ADAM_B1) * g
    v = ADAM_B2 * v + (1.0 - ADAM_B2) * (g * g)
    m_hat = m / (1.0 - ADAM_B1 ** ADAM_STEP)
    v_hat = v / (1.0 - ADAM_B2 ** ADAM_STEP)
    delta = -ADAM_LR * (m_hat / (jnp.sqrt(v_hat) + ADAM_EPS) + ADAM_WD * w)
    return delta, m, v


def adam_shard(chip_idx, own, parts, w, m, v, name):
    rows, cols = w.shape
    tr = min(rows, 256)

    def body(chip_ref, p0, p1, p2, p3, w_ref, m_ref, v_ref, g_ref, d_ref, nm_ref, nv_ref):
        del chip_ref
        g = ((p0[...].astype(F32) + p1[...].astype(F32)) + p2[...].astype(F32)) + p3[...].astype(F32)
        g_ref[...] = g
        d_ref[...], nm_ref[...], nv_ref[...] = _adamw(w_ref[...], g, m_ref[...], v_ref[...])

    part = lambda q: pl.BlockSpec((None, tr, cols), lambda i, chip, q=q: (q, i, 0))
    tile = pl.BlockSpec((tr, cols), lambda i, chip: (i, 0))
    return pl.pallas_call(
        body, name=name,
        grid_spec=pltpu.PrefetchScalarGridSpec(
            num_scalar_prefetch=1, grid=(rows // tr,),
            in_specs=[pl.BlockSpec((None, tr, cols), lambda i, chip: (chip[0], i, 0)), part(0), part(1), part(2),
                      tile, tile, tile],
            out_specs=[tile] * 4),
        out_shape=[jax.ShapeDtypeStruct(w.shape, F32)] * 4,
        compiler_params=_params(("arbitrary",)),
    )(chip_idx, own, parts, parts, parts, w, m, v)


def adam_ada(sc_t, dmod_all, me_idx, w, m, v):
    def body(me_ref, sc_ref, dm_ref, w_ref, m_ref, v_ref, g_ref, d_ref, nm_ref, nv_ref):
        del me_ref
        g = _dot_f32(sc_ref[...], dm_ref[...])
        g_ref[...] = g
        d_ref[...], nm_ref[...], nv_ref[...] = _adamw(w_ref[...], g, m_ref[...], v_ref[...])

    full = pl.BlockSpec(w.shape, lambda i, me: (0, 0))
    return pl.pallas_call(
        body, name="adam_ada",
        grid_spec=pltpu.PrefetchScalarGridSpec(
            num_scalar_prefetch=1, grid=(1,),
            in_specs=[pl.BlockSpec(sc_t.shape, lambda i, me: (0, 0)),
                      pl.BlockSpec((LANE, SHARD_ADA), lambda i, me: (0, me[0])), full, full, full],
            out_specs=[full] * 4),
        out_shape=[jax.ShapeDtypeStruct(w.shape, F32)] * 4,
        compiler_params=_params(("arbitrary",)),
    )(me_idx, sc_t, dmod_all, w, m, v)


def adam_vectors(packs, lb, params, ms, vs):
    n = len(params)

    def body(*refs):
        packs_ref, lb_ref = refs[0], refs[1]
        w_refs, m_refs, v_refs = refs[2:2 + n], refs[2 + n:2 + 2 * n], refs[2 + 2 * n:2 + 3 * n]
        loss_ref = refs[2 + 3 * n]
        outs = refs[3 + 3 * n:3 + 7 * n]
        tot_ref = refs[3 + 7 * n]
        tot = packs_ref[0]
        for d in range(1, N_DEV):
            tot = tot + packs_ref[d]
        tot_ref[...] = tot
        row = lambda r: tot_ref[r:r + 1, :]
        lbv = lb_ref[...]
        dl0 = row(ROW_LB) * lbv * (1.0 - lbv)
        grads = [[row(ROW_NORM_G)],
                 [jnp.concatenate([row(ROW_SHIFT), row(ROW_SCALE), row(ROW_GATE)], axis=1)],
                 [dl0, -dl0],
                 [row(ROW_HG_G)], [row(ROW_RET_G)], [row(ROW_FINAL_G)]]
        loss_ref[...] = tot_ref[ROW_LOSS:ROW_LOSS + 1, 0:LANE]
        for j, g_rows in enumerate(grads):
            for r, g in enumerate(g_rows):
                rs = slice(r, r + 1)
                d, nm, nv = _adamw(w_refs[j][rs, :], g, m_refs[j][rs, :], v_refs[j][rs, :])
                outs[4 * j][rs, :] = g
                outs[4 * j + 1][rs, :] = d
                outs[4 * j + 2][rs, :] = nm
                outs[4 * j + 3][rs, :] = nv

    vmem = pl.BlockSpec(memory_space=pltpu.VMEM)
    out_shape = [jax.ShapeDtypeStruct((1, LANE), F32)]
    for w in params:
        out_shape += [jax.ShapeDtypeStruct(w.shape, F32)] * 4
    return pl.pallas_call(
        body, name="adam_vectors", in_specs=[vmem] * (2 + 3 * n), out_specs=[vmem] * len(out_shape),
        out_shape=out_shape, scratch_shapes=[pltpu.VMEM((PACK_ROWS, D_MODEL), F32)],
    )(packs, lb, *params, *ms, *vs)


def vector_tail(vec_mid, vec_ada, dlb, sc_t, lb, ada, params, ms, vs):
    n = len(params)
    dmod_rows = (ROW_SHIFT, ROW_SCALE, ROW_GATE)
    blocks = D_MODEL // LANE

    def body(*refs):
        mid_ref, ada_ref, dlb_ref, sc_ref, lb_ref, wa_ref, ma_ref, va_ref = refs[:8]
        w_refs, m_refs, v_refs = refs[8:8 + n], refs[8 + n:8 + 2 * n], refs[8 + 2 * n:8 + 3 * n]
        loss_ref = refs[8 + 3 * n]
        ada_outs = refs[9 + 3 * n:13 + 3 * n]
        outs = refs[13 + 3 * n:13 + 7 * n]
        packs_ref, tot_ref, dm_ref, psend, precv = refs[13 + 7 * n:]
        pos = _mesh_pos()
        me = _lin(pos)
        mid = lambda r: mid_ref[r:r + 1, :]
        rows = [ada_ref[0:1, :], dlb_ref[...], mid(MID_HG_G), mid(MID_RET_G), mid(MID_FINAL_G),
                ada_ref[2:3, :], ada_ref[1:2, :], mid(MID_GATE), mid(MID_LOSS)]
        packs_ref[me] = jnp.zeros((PACK_ROWS, D_MODEL), F32)
        for r, row in enumerate(rows):
            packs_ref[me, r:r + 1, :] = row
        peers = [_xor_peer(pos, k) for k in range(1, N_DEV)]
        gather = [_remote(packs_ref.at[me], packs_ref.at[me], psend.at[k], precv.at[k], p) for k, p in enumerate(peers)]
        for cp in gather:
            cp.start()
        dm_ref[...] = jnp.zeros_like(dm_ref)
        for k, p in enumerate(peers):
            _remote(packs_ref.at[_lin(p)], packs_ref.at[_lin(p)], psend.at[k], precv.at[k], p).wait_recv()
        for cp in gather:
            cp.wait_send()

        tot = packs_ref[0]
        for d in range(1, N_DEV):
            tot = tot + packs_ref[d]
        tot_ref[...] = tot
        row = lambda r: tot_ref[r:r + 1, :]
        lbv = lb_ref[...]
        dl0 = row(ROW_LB) * lbv * (1.0 - lbv)
        grads = [[row(ROW_NORM_G)],
                 [jnp.concatenate([row(ROW_SHIFT), row(ROW_SCALE), row(ROW_GATE)], axis=1)],
                 [dl0, -dl0],
                 [row(ROW_HG_G)], [row(ROW_RET_G)], [row(ROW_FINAL_G)]]
        loss_ref[...] = tot_ref[ROW_LOSS:ROW_LOSS + 1, 0:LANE]
        for j, g_rows in enumerate(grads):
            for r, g in enumerate(g_rows):
                rs = slice(r, r + 1)
                d, nm, nv = _adamw(w_refs[j][rs, :], g, m_refs[j][rs, :], v_refs[j][rs, :])
                outs[4 * j][rs, :] = g
                outs[4 * j + 1][rs, :] = d
                outs[4 * j + 2][rs, :] = nm
                outs[4 * j + 3][rs, :] = nv

        for part, r in enumerate(dmod_rows):
            for d in range(N_DEV):
                for b in range(blocks):
                    dm_ref[part * blocks + b, d:d + 1, :] = packs_ref[d, r:r + 1, b * LANE:(b + 1) * LANE]
        g_ref, d_ref, nm_ref, nv_ref = ada_outs
        for j in range(SHARD_ADA // LANE):
            g_ref[:, j * LANE:(j + 1) * LANE] = _dot_f32(sc_ref[...], dm_ref[(SHARD_ADA // LANE) * me + j])
        d_ref[...], nm_ref[...], nv_ref[...] = _adamw(wa_ref[...], g_ref[...], ma_ref[...], va_ref[...])

    vmem = pl.BlockSpec(memory_space=pltpu.VMEM)
    out_shape = [jax.ShapeDtypeStruct((1, LANE), F32)] + [jax.ShapeDtypeStruct(ada[0].shape, F32)] * 4
    for w in params:
        out_shape += [jax.ShapeDtypeStruct(w.shape, F32)] * 4
    return pl.pallas_call(
        body, name="vector_tail", in_specs=[vmem] * (8 + 3 * n), out_specs=[vmem] * len(out_shape),
        out_shape=out_shape,
        scratch_shapes=[pltpu.VMEM((N_DEV, PACK_ROWS, D_MODEL), F32), pltpu.VMEM((PACK_ROWS, D_MODEL), F32),
                        pltpu.VMEM((len(dmod_rows) * blocks, LANE, LANE), F32),
                        pltpu.SemaphoreType.DMA((N_DEV - 1,)), pltpu.SemaphoreType.DMA((N_DEV - 1,))],
        compiler_params=pltpu.CompilerParams(vmem_limit_bytes=VMEM_LIMIT),
    )(vec_mid, vec_ada, dlb, sc_t, lb, *ada, *params, *ms, *vs)


def kernel(x, c, norm_g, w_ada, b_ada, w_in, hg_lb_logits, hg_norm_g, ret_norm_g, w_out, final_g, loss_target, m_norm_g, m_w_ada, m_b_ada, m_w_in, m_hg_lb_logits, m_hg_norm_g, m_ret_norm_g, m_w_out, m_final_g, v_norm_g, v_w_ada, v_b_ada, v_w_in, v_hg_lb_logits, v_hg_norm_g, v_ret_norm_g, v_w_out, v_final_g):
    pos = _mesh_pos()
    me_idx = jnp.reshape(_lin(pos), (1,)).astype(jnp.int32)
    c_idx = jnp.reshape(pos[2], (1,)).astype(jnp.int32)
    vec = lambda a: a.reshape(1, D_MODEL)

    mod, scall, lb, wtg, woutg = pre_exchange(c, w_ada[0], b_ada, hg_lb_logits, w_in[0], w_out[0])
    chip_idx = jnp.reshape(2 * pos[0] + pos[1], (1,)).astype(jnp.int32)
    gather_send, gather_recv, wtg, woutg, gather_token = weight_gather_start(wtg, woutg)
    mod = mod + gather_token[:1, :1]

    other_chips = jnp.stack([2 * cx + cy for cx, cy in _other_chips(pos)]).astype(jnp.int32)

    def project(h):
        t_len = h.shape[0]
        flat_in = lambda a: a.reshape(D_IN, D_MODEL)
        first = (gather_send, gather_recv)
        wg, wog = weight_gather_wait("weight_wait_sibling", 0, (0,), *first, wtg, woutg, h)
        pb = proj_forward(h, flat_in(wg), chip_idx, t_len, "proj_fwd_own")
        wg, wog = weight_gather_wait("weight_wait_neighbours", 0, (1, 2), *first, wg, wog, pb)
        *second, wg, wog = weight_pass_start(wg, wog)
        wg, wog = weight_gather_wait("weight_wait_passed_sibling", 1, (4, 5), *second, wg, wog, pb)
        pb = proj_forward(h, flat_in(wg), other_chips[:2], t_len, "proj_fwd_neighbours", pb)
        wg, wog = weight_gather_wait("weight_wait_passed_chip", 1, (3,), *second, wg, wog, pb)
        wg, wog = weight_pass_last(wg, wog)
        pb = proj_forward(h, flat_in(wg), other_chips[2:], t_len, "proj_fwd_diagonal", pb)
        return pb, flat_in(wg), wog.reshape(D_MODEL, D_MODEL)

    def start_exchange(dwin, dwin_sib, dwout):
        dwout = dwout.reshape(N_DEV, SHARD_OUT, D_MODEL)
        ra, rb = grad_pair_exchange(dwin_sib, dwout)
        sb, sbo = pair_sum(dwin, ra, dwout, rb, c_idx)
        send, recv, sb, sbo, rc, rco, token = grad_chip_start(sb, sbo)
        return token, (send, recv, sb, sbo, rc, rco)

    grad_x, _, _, vec_mid, vec_ada, dlb, pending = device_step(
        x[0], loss_target[0], mod, lb, project, norm_g, hg_norm_g, ret_norm_g, vec(final_g), c_idx, start_exchange)
    sc_t = jnp.pad(scall.T, ((0, 0), (0, LANE - N_DEV)))
    tail = vector_tail(
        vec_mid, vec_ada, dlb, sc_t, lb, (w_ada[0], m_w_ada[0], v_w_ada[0]),
        (norm_g, b_ada, hg_lb_logits, hg_norm_g, ret_norm_g, vec(final_g)),
        (m_norm_g, m_b_ada, m_hg_lb_logits, m_hg_norm_g, m_ret_norm_g, vec(m_final_g)),
        (v_norm_g, v_b_ada, v_hg_lb_logits, v_hg_norm_g, v_ret_norm_g, vec(v_final_g)))
    g_ada, d_ada, nm_ada, nv_ada = tail[1:5]
    small = [tail[0]] + list(tail[5:])
    loss = small[0][0, 0]
    sb, sbo, rc, rco = grad_chip_wait(*pending, small[0])
    g_in, d_in, nm_in, nv_in = adam_shard(chip_idx, sb, rc, w_in[0], m_w_in[0], v_w_in[0], "adam_w_in")
    g_out, d_out, nm_out, nv_out = adam_shard(chip_idx, sbo, rco, w_out[0], m_w_out[0], v_w_out[0], "adam_w_out")
    (g_ng, d_ng, nm_ng, nv_ng), (g_b, d_b, nm_b, nv_b), (g_lb, d_lb, nm_lb, nv_lb), (g_hg, d_hg, nm_hg, nv_hg), \
        (g_rg, d_rg, nm_rg, nv_rg), (g_fg, d_fg, nm_fg, nv_fg) = [small[1 + 4 * j:5 + 4 * j] for j in range(6)]
    flat = lambda a: a.reshape(D_MODEL)

    def group(ng, ada, b, win, lbl, hg, rg, wo, fg):
        return (ng, ada[None], b, win[None], lbl, hg, rg, wo[None], flat(fg))

    return (loss, grad_x[None],
            *group(g_ng, g_ada, g_b, g_in, g_lb, g_hg, g_rg, g_out, g_fg),
            *group(d_ng, d_ada, d_b, d_in, d_lb, d_hg, d_rg, d_out, d_fg),
            *group(nm_ng, nm_ada, nm_b, nm_in, nm_lb, nm_hg, nm_rg, nm_out, nm_fg),
            *group(nv_ng, nv_ada, nv_b, nv_in, nv_lb, nv_hg, nv_rg, nv_out, nv_fg))
```

```python
import numpy as np
import jax
import jax.numpy as jnp
from jax import lax
from jax.experimental import pallas as pl
from jax.experimental.pallas import tpu as pltpu

F32 = jnp.float32
_BF = jnp.bfloat16

D_MODEL = 1024
N_HEADS = 8
LANE = 128
RET_DK = 64
D_IN = 9216
N_DEV = 8
SHARD_IN = D_IN // N_DEV
SHARD_ADA = 3 * D_MODEL // N_DEV
SHARD_OUT = D_MODEL // N_DEV
N_CB = D_IN // LANE
CB_PER_SHARD = SHARD_IN // LANE
CHUNK = 128
N_LEVELS = 7
EPS = 1e-6
LOG2_E = float(np.log2(np.e))
ROPE_BASE = 10000.0
CB_HQ, CB_HF, CB_HI, CB_HZ, CB_RQ, CB_RK, CB_RV, CB_RZ, CB_GA, CB_GB = 0, 8, 16, 24, 32, 36, 40, 48, 56, 64
VMEM_LIMIT = 56 * 1024 * 1024

ADAM_LR, ADAM_B1, ADAM_B2, ADAM_EPS, ADAM_WD, ADAM_STEP = 0.001, 0.9, 0.999, 1e-08, 0.01, 10

_NN = (((1,), (0,)), ((), ()))
_NT = (((1,), (1,)), ((), ()))
_TN = (((0,), (0,)), ((), ()))
MESH = pl.DeviceIdType.MESH


def _dot(a, b, dims=_NN):
    return lax.dot_general(a.astype(_BF), b.astype(_BF), dims, preferred_element_type=F32)


def _split2(a):
    hi = a.astype(_BF)
    lo = (a - hi.astype(F32)).astype(_BF)
    return jnp.concatenate([hi, lo], axis=1)


def _dot_sel(sel, a):
    n = a.shape[1]
    r = lax.dot_general(sel.astype(_BF), _split2(a), _NN, preferred_element_type=F32)
    return r[:, :n] + r[:, n:]


def _dot_f32(a, b):
    def pieces(v):
        p1 = v.astype(_BF)
        r1 = v - p1.astype(F32)
        p2 = r1.astype(_BF)
        p3 = (r1 - p2.astype(F32)).astype(_BF)
        return p1, p2, p3
    a1, a2, a3 = pieces(a)
    b1, b2, b3 = pieces(b)
    d = lambda u, v: lax.dot_general(u, v, _NN, preferred_element_type=F32)
    return ((d(a1, b3) + d(a2, b2) + d(a3, b1)) + (d(a1, b2) + d(a2, b1))) + d(a1, b1)


def _sigmoid(v):
    return 1.0 / (1.0 + jnp.exp(-v))


def _params(sem=None):
    return pltpu.CompilerParams(dimension_semantics=sem, vmem_limit_bytes=VMEM_LIMIT)


def _hgrn_consts():
    c, nl = CHUNK, N_LEVELS
    t = np.arange(c)[:, None]
    j = np.arange(c)[None, :]
    sel = [j <= t]
    masks = [j == t]
    for l in range(1, nl + 1):
        m = ((t >> l) << l) + (1 << (l - 1)) - 1
        sec = t > m
        sel.append(np.where(sec, (j > m) & (j <= t), (j > t) & (j <= m)))
        same = (t >> l) == (j >> l)
        masks.append(same & sec & (j <= m))
    sel.append(j > t)
    sel = np.concatenate(sel, 0).astype(np.float32)
    masks = np.stack(masks).astype(np.float32)
    sgn = np.stack([np.where((t & (1 << (l - 1))) != 0, 1.0, -1.0) * np.ones((1, LANE)) for l in range(3, nl + 1)])
    return dict(tri=jnp.asarray(sel[:c], _BF),
                lvl=jnp.asarray(masks, F32),
                sgn=jnp.asarray(sgn, F32),
                sel_t=jnp.asarray(sel.T, _BF),
                lvl_b=jnp.asarray(masks, _BF),
                lvlt_b=jnp.asarray(np.swapaxes(masks, 1, 2), _BF))


def _level_exponents(b, logf, b_scr, sgn_ref):
    c = CHUNK
    b_scr[...] = b
    row = lax.broadcasted_iota(jnp.int32, (c, LANE), 0)
    nxt = pltpu.roll(logf, c - 1, 0)
    prv = pltpu.roll(logf, 1, 0)
    r4 = row & 3
    out = [jnp.where((row & 1) == 1, logf, 0.0),
           jnp.where(r4 == 0, nxt, jnp.where(r4 == 1, 0.0, jnp.where(r4 == 2, logf, logf + prv)))]
    for l in range(3, N_LEVELS + 1):
        size, half = 1 << l, 1 << (l - 1)
        ref = jnp.concatenate([jnp.broadcast_to(b_scr[i * size + half - 1:i * size + half, :], (size, LANE))
                               for i in range(c // size)], axis=0)
        out.append((b - ref) * sgn_ref[l - 3])
    return out


def _hgrn_chunk(hq, hf, hi, lbv, tri_ref, sgn_ref, b_scr):
    sq = _sigmoid(hq)
    q = hq * sq
    sg = _sigmoid(hf)
    omlb = 1.0 - lbv
    f = lbv + omlb * sg
    k = 1.0 - f
    logf = jnp.log(f) * LOG2_E
    b = _dot_sel(tri_ref[...], logf)
    bc = jnp.sum(logf, axis=0, keepdims=True)
    lev = [None] + [jnp.exp2(e) for e in _level_exponents(b, logf, b_scr, sgn_ref)]
    return dict(sq=sq, q=q, sg=sg, omlb=omlb, f=f, k=k, v=hi, eb=jnp.exp2(b), erem=jnp.exp2(bc - b),
                ebc=jnp.exp2(bc), lev=lev)


def _blockdiag(a, b):
    z = jnp.zeros_like(a)
    return jnp.concatenate([jnp.concatenate([a, z], axis=1), jnp.concatenate([z, b], axis=1)], axis=0)


def _level_operands(a):
    q, k = a["q"].astype(_BF), a["k"].astype(_BF)
    lev = [None] + [a["lev"][l].astype(_BF) for l in range(1, N_LEVELS + 1)]
    ql = [q] + [q * lev[l] for l in range(1, N_LEVELS + 1)]
    kl = [k] + [k * lev[l] for l in range(1, N_LEVELS + 1)]
    pairs = range(0, N_LEVELS + 1, 2)
    return ([jnp.concatenate([ql[l], ql[l + 1]], axis=1) for l in pairs], [_blockdiag(kl[l], kl[l + 1]) for l in pairs],
            ql, kl)


def _hgrn_scores(a, lvl_ref, q_pairs, k_diags):
    acc = None
    for n, (qp, kd) in enumerate(zip(q_pairs, k_diags)):
        both = lax.dot_general(qp, kd, _NT, preferred_element_type=F32)
        part = lvl_ref[2 * n] * both[:, :CHUNK] + lvl_ref[2 * n + 1] * both[:, CHUNK:]
        acc = part if acc is None else acc + part
    return acc


SCAN_UNROLL = 8
FWD_UNROLL = 16
RET_UNROLL = 16


def _writeback_reserve(step, make_copies):
    slot = step % 2

    @pl.when(step >= 2)
    def _():
        for cp in make_copies(slot):
            cp.wait()

    return slot


def _writeback_commit(step, n_steps, slot, make_copies):
    for cp in make_copies(slot):
        cp.start()

    @pl.when(step == n_steps - 1)
    def _():
        for cp in make_copies(slot):
            cp.wait()
        if n_steps > 1:
            for cp in make_copies(1 - slot):
                cp.wait()


def _resident(const):
    zeros = (0,) * const.ndim
    return pl.BlockSpec(const.shape, lambda p, t: zeros)


def _time_block(t_len):
    return min(t_len, 2048)


def hgrn_forward(pb, lb, t_len):
    nc = t_len // CHUNK
    tb = _time_block(t_len)
    ncb = tb // CHUNK
    consts = _hgrn_consts()
    operands = [consts[n] for n in ("tri", "lvl", "sgn")]

    def body(hq_ref, hf_ref, hi_ref, lb_ref, tri_ref, lvl_ref, sgn_ref, o_ref, ssave_ref, asave_ref, st_ref, b_scr):
        @pl.when(pl.program_id(1) == 0)
        def _():
            st_ref[...] = jnp.zeros_like(st_ref)

        def chunk(ci, carry):
            r = pl.ds(pl.multiple_of(ci * CHUNK, CHUNK), CHUNK)
            for hd in range(2):
                lbv = lb_ref[:, hd * LANE:(hd + 1) * LANE]
                a = _hgrn_chunk(hq_ref[hd, r, :], hf_ref[hd, r, :], hi_ref[hd, r, :], lbv, tri_ref, sgn_ref,
                                b_scr.at[hd])
                q_pairs, k_diags, _, _ = _level_operands(a)
                st = st_ref[hd]
                ssave_ref[hd, ci] = st
                scores = _hgrn_scores(a, lvl_ref, q_pairs, k_diags).astype(asave_ref.dtype)
                asave_ref[hd, ci] = scores
                o_ref[hd, r, :] = _dot(a["q"] * a["eb"], st, _NT) + _dot(scores, a["v"])
                st_ref[hd] = st * a["ebc"] + _dot(a["v"], a["k"] * a["erem"], _TN)
            return carry

        lax.fori_loop(0, ncb, chunk, 0, unroll=FWD_UNROLL)

    pair = lambda base: pl.BlockSpec((2, tb, LANE), lambda p, t, base=base: (base // 2 + p, t, 0))
    per_chunk = pl.BlockSpec((2, ncb, LANE, LANE), lambda p, t: (p, t, 0, 0))
    return pl.pallas_call(
        body, name="hgrn_fwd", grid=(N_HEADS // 2, t_len // tb),
        in_specs=[pair(CB_HQ), pair(CB_HF), pair(CB_HI),
                  pl.BlockSpec((1, 2 * LANE), lambda p, t: (0, p))] + [_resident(c) for c in operands],
        out_specs=[pl.BlockSpec((2, tb, LANE), lambda p, t: (p, t, 0)), per_chunk, per_chunk],
        out_shape=[jax.ShapeDtypeStruct((N_HEADS, t_len, LANE), F32),
                   jax.ShapeDtypeStruct((N_HEADS, nc, LANE, LANE), F32),
                   jax.ShapeDtypeStruct((N_HEADS, nc, CHUNK, CHUNK), _BF)],
        scratch_shapes=[pltpu.VMEM((2, LANE, LANE), F32), pltpu.VMEM((2, CHUNK, LANE), F32)],
        compiler_params=_params(("arbitrary", "arbitrary")),
    )(pb, pb, pb, lb, *operands)


def hgrn_backward(pb, lb, do, ssave, asave, dpb, t_len):
    tb = _time_block(t_len)
    ncb, ntb = tb // CHUNK, t_len // tb
    consts = _hgrn_consts()
    operands = [consts[n] for n in ("tri", "sgn", "sel_t", "lvl_b", "lvlt_b")]

    def body(hq_ref, hf_ref, hi_ref, lb_ref, do_ref, ssave_ref, asave_ref, tri_ref, sgn_ref, selt_ref, lvlb_ref,
             lvltb_ref, dpb_in, dpb_ref, dlb_ref, dq_buf, df_buf, di_buf, dst_ref, b_scr, sems):
        del dpb_in
        p, t = pl.program_id(0), pl.program_id(1)
        step = p * ntb + t
        rows = pl.ds(pl.multiple_of((ntb - 1 - t) * tb, tb), tb)

        def out_copies(sl):
            return [pltpu.make_async_copy(buf.at[sl], dpb_ref.at[pl.ds(base + 2 * p, 2), rows], sems.at[sl, n])
                    for n, (buf, base) in enumerate(((dq_buf, CB_HQ), (df_buf, CB_HF), (di_buf, CB_HI)))]

        slot = _writeback_reserve(step, out_copies)

        @pl.when(t == 0)
        def _():
            dst_ref[...] = jnp.zeros_like(dst_ref)
            dlb_ref[...] = jnp.zeros_like(dlb_ref)

        def chunk(i, carry):
            ci = ncb - 1 - i
            r = pl.ds(pl.multiple_of(ci * CHUNK, CHUNK), CHUNK)
            for hd in range(2):
                head_chunk(hd, ci, r)
            return carry

        def head_chunk(hd, ci, r):
            lbv = lb_ref[:, hd * LANE:(hd + 1) * LANE]
            hq = hq_ref[hd, r, :]
            a = _hgrn_chunk(hq, hf_ref[hd, r, :], hi_ref[hd, r, :], lbv, tri_ref, sgn_ref, b_scr.at[hd])
            _, k_diags, ql, kl = _level_operands(a)
            q, k, v = a["q"], a["k"], a["v"]
            g = do_ref[hd, r, :]
            st0 = ssave_ref[hd, ci]
            dst = dst_ref[hd]
            scores = asave_ref[hd, ci]
            da = _dot(g, v, _NT)
            da_t = _dot(v, g, _NT)
            kb = k * a["erem"]
            qb = q * a["eb"]
            dv = _dot(scores, g, _TN) + _dot(kb, dst, _NT)
            dq_inter = _dot(g, st0) * a["eb"]
            dk_state = _dot(v, dst) * a["erem"]
            dq, dk = dq_inter, dk_state
            de = [q * dq_inter]
            da_b, dat_b = da.astype(_BF), da_t.astype(_BF)
            for n in range(len(k_diags)):
                l0, l1 = 2 * n, 2 * n + 1
                da_pair = jnp.concatenate([lvlb_ref[l0] * da_b, lvlb_ref[l1] * da_b], axis=1)
                dat_pair = jnp.concatenate([lvltb_ref[l0] * dat_b, lvltb_ref[l1] * dat_b], axis=1)
                dq_both = lax.dot_general(da_pair, k_diags[n], _NN, preferred_element_type=F32)
                dk_both = lax.dot_general(dat_pair, _blockdiag(ql[l0], ql[l1]), _NN, preferred_element_type=F32)
                for l, cols in ((l0, slice(0, LANE)), (l1, slice(LANE, 2 * LANE))):
                    dql, dkl = dq_both[:, cols], dk_both[:, cols]
                    if l > 0:
                        e = a["lev"][l]
                        dql, dkl = dql * e, dkl * e
                        de.append(q * dql + k * dkl)
                    dq = dq + dql
                    dk = dk + dkl
            de.append(k * dk_state)
            dst_ref[hd] = dst * a["ebc"] + _dot(g, qb, _TN)
            dbc = jnp.sum(dst * st0, axis=0, keepdims=True) * a["ebc"]
            de2 = lax.dot_general(selt_ref[...], _split2(jnp.concatenate(de, axis=0)), _NN,
                                  preferred_element_type=F32)
            dlogf = de2[:, :LANE] + de2[:, LANE:] + dbc
            sq, sg = a["sq"], a["sg"]
            df = dlogf / a["f"] - dk
            dq_buf[slot, hd, r, :] = (dq * (sq * (1.0 + hq * (1.0 - sq)))).astype(dq_buf.dtype)
            df_buf[slot, hd, r, :] = (df * a["omlb"] * sg * (1.0 - sg)).astype(df_buf.dtype)
            di_buf[slot, hd, r, :] = dv.astype(di_buf.dtype)
            cols = slice(hd * LANE, (hd + 1) * LANE)
            dlb_ref[:, cols] = dlb_ref[:, cols] + jnp.sum(df * (1.0 - sg), axis=0, keepdims=True)

        lax.fori_loop(0, ncb, chunk, 0, unroll=SCAN_UNROLL)
        _writeback_commit(step, (N_HEADS // 2) * ntb, slot, out_copies)

    pair = lambda base: pl.BlockSpec((2, tb, LANE), lambda p, t, base=base: (base // 2 + p, ntb - 1 - t, 0))
    any_spec = pl.BlockSpec(memory_space=pl.ANY)
    per_chunk = pl.BlockSpec((2, ncb, LANE, LANE), lambda p, t: (p, ntb - 1 - t, 0, 0))
    return pl.pallas_call(
        body, name="hgrn_bwd", grid=(N_HEADS // 2, ntb),
        in_specs=[pair(CB_HQ), pair(CB_HF), pair(CB_HI),
                  pl.BlockSpec((1, 2 * LANE), lambda p, t: (0, p)),
                  pair(0), per_chunk, per_chunk]
        + [_resident(c) for c in operands] + [any_spec],
        out_specs=[any_spec, pl.BlockSpec((1, 2 * LANE), lambda p, t: (0, p))],
        out_shape=[jax.ShapeDtypeStruct(dpb.shape, dpb.dtype), jax.ShapeDtypeStruct((1, D_MODEL), F32)],
        scratch_shapes=[pltpu.VMEM((2, 2, tb, LANE), dpb.dtype)] * 3 + [
            pltpu.VMEM((2, LANE, LANE), F32), pltpu.VMEM((2, CHUNK, LANE), F32), pltpu.SemaphoreType.DMA((2, 3))],
        input_output_aliases={7 + len(operands): 0},
        compiler_params=_params(("arbitrary", "arbitrary")),
    )(pb, pb, pb, lb, do, ssave, asave, *operands, dpb)


def _rope_tables(t_len):
    half = RET_DK // 2
    inv_freq = (1.0 / (np.float32(ROPE_BASE) ** np.linspace(0.0, 1.0, half, dtype=np.float32))).astype(np.float32)
    ang = (np.arange(t_len, dtype=np.float32)[:, None] * inv_freq[None, :]).astype(np.float64)
    cos, sin = np.cos(ang).astype(np.float32), np.sin(ang).astype(np.float32)
    cos_t = np.concatenate([cos, cos, cos, cos], axis=1)
    sin_t = np.concatenate([-sin, sin, -sin, sin], axis=1)
    return jnp.asarray(cos_t), jnp.asarray(sin_t)


def _swap_halves(v):
    half = RET_DK // 2
    lane = lax.broadcasted_iota(jnp.int32, v.shape, 1)
    first = (lane & (RET_DK - 1)) < half
    return jnp.where(first, pltpu.roll(v, LANE - half, 1), pltpu.roll(v, half, 1))


def _ret_head_consts(hidx):
    c = CHUNK
    hf = jnp.full((1, LANE), hidx, jnp.int32).astype(F32)
    lg = jnp.log(1.0 - jnp.exp(-(5.0 + hf) * np.float32(np.log(2.0))))
    row = lax.broadcasted_iota(jnp.int32, (c, c), 0)
    col = lax.broadcasted_iota(jnp.int32, (c, c), 1)
    rel = (row - col).astype(F32)
    dm = jnp.where(rel >= 0, jnp.exp(lg[:, :1] * jnp.maximum(rel, 0.0)), 0.0)
    dm_t = jnp.where(rel <= 0, jnp.exp(lg[:, :1] * jnp.maximum(-rel, 0.0)), 0.0)
    idx = lax.broadcasted_iota(jnp.int32, (c, LANE), 0).astype(F32)
    zeta = jnp.exp(lg * (c - 1.0 - idx))
    xi = jnp.exp(lg * (idx + 1.0))
    cdec = jnp.exp(lg * float(c))
    return dm, zeta, xi, cdec, dm_t


def _lane_mask(which):
    lane = lax.broadcasted_iota(jnp.int32, (1, LANE), 1)
    return ((lane // RET_DK) == which).astype(F32)


def retention_forward(pb, cos_t, sin_t, t_len):
    nc = t_len // CHUNK

    tb = _time_block(t_len)
    ncb = tb // CHUNK

    def body(rq_ref, rk_ref, rv_ref, cos_ref, sin_ref, o_ref, rsave_ref, st_ref):
        p = pl.program_id(0)

        @pl.when(pl.program_id(1) == 0)
        def _():
            st_ref[...] = jnp.zeros_like(st_ref)

        consts = [_ret_head_consts(2 * p + hd) for hd in range(2)]

        def chunk(ci, carry):
            r = pl.ds(pl.multiple_of(ci * CHUNK, CHUNK), CHUNK)
            cs, sn = cos_ref[r, :], sin_ref[r, :]
            q = rq_ref[r, :]
            k = rk_ref[r, :]
            q = q * cs + _swap_halves(q) * sn
            k = (k * cs + _swap_halves(k) * sn) * RET_DK ** -0.5
            for hd in range(2):
                dm, zeta, xi, cdec, _ = consts[hd]
                lm = _lane_mask(hd)
                qh, kh = q * lm, k * lm
                v = rv_ref[hd, r, :]
                st = st_ref[hd]
                rsave_ref[hd, ci] = st
                scores = _dot(qh, kh, _NT) * dm
                o_ref[hd, r, :] = _dot(scores, v) + _dot(qh * xi, st, _NT)
                st_ref[hd] = st * cdec + _dot(v, kh * zeta, _TN)
            return carry

        lax.fori_loop(0, ncb, chunk, 0, unroll=RET_UNROLL)

    return pl.pallas_call(
        body, name="ret_fwd", grid=(N_HEADS // 2, t_len // tb),
        in_specs=[pl.BlockSpec((None, tb, LANE), lambda p, t: (CB_RQ + p, t, 0)),
                  pl.BlockSpec((None, tb, LANE), lambda p, t: (CB_RK + p, t, 0)),
                  pl.BlockSpec((2, tb, LANE), lambda p, t: (CB_RV // 2 + p, t, 0)),
                  pl.BlockSpec((tb, LANE), lambda p, t: (t, 0)),
                  pl.BlockSpec((tb, LANE), lambda p, t: (t, 0))],
        out_specs=[pl.BlockSpec((2, tb, LANE), lambda p, t: (p, t, 0)),
                   pl.BlockSpec((2, ncb, LANE, LANE), lambda p, t: (p, t, 0, 0))],
        out_shape=[jax.ShapeDtypeStruct((N_HEADS, t_len, LANE), F32),
                   jax.ShapeDtypeStruct((N_HEADS, nc, LANE, LANE), F32)],
        scratch_shapes=[pltpu.VMEM((2, LANE, LANE), F32)],
        compiler_params=_params(("arbitrary", "arbitrary")),
    )(pb, pb, pb, cos_t, sin_t)


def retention_backward(pb, cos_t, sin_t, do, rsave, dpb, t_len):
    tb = _time_block(t_len)
    ncb, ntb = tb // CHUNK, t_len // tb

    def body(rq_ref, rk_ref, rv_ref, cos_ref, sin_ref, do_ref, rsave_ref, dpb_in,
             dpb_ref, dq_buf, dk_buf, dv_buf, dst_ref, sems):
        del dpb_in
        p, t = pl.program_id(0), pl.program_id(1)
        step = p * ntb + t
        rows = pl.ds(pl.multiple_of((ntb - 1 - t) * tb, tb), tb)

        def out_copies(sl):
            return [pltpu.make_async_copy(dq_buf.at[sl], dpb_ref.at[CB_RQ + p, rows], sems.at[sl, 0]),
                    pltpu.make_async_copy(dk_buf.at[sl], dpb_ref.at[CB_RK + p, rows], sems.at[sl, 1]),
                    pltpu.make_async_copy(dv_buf.at[sl], dpb_ref.at[pl.ds(CB_RV + 2 * p, 2), rows], sems.at[sl, 2])]

        slot = _writeback_reserve(step, out_copies)

        @pl.when(t == 0)
        def _():
            dst_ref[...] = jnp.zeros_like(dst_ref)

        consts = [_ret_head_consts(2 * p + hd) for hd in range(2)]

        def chunk(i, carry):
            ci = ncb - 1 - i
            r = pl.ds(pl.multiple_of(ci * CHUNK, CHUNK), CHUNK)
            cs, sn = cos_ref[r, :], sin_ref[r, :]
            q = rq_ref[r, :]
            k = rk_ref[r, :]
            q = q * cs + _swap_halves(q) * sn
            k = (k * cs + _swap_halves(k) * sn) * RET_DK ** -0.5
            dq, dk = None, None
            for hd in range(2):
                dm, zeta, xi, cdec, dm_t = consts[hd]
                lm = _lane_mask(hd)
                qh, kh = q * lm, k * lm
                v = rv_ref[hd, r, :]
                g = do_ref[hd, r, :]
                st0 = rsave_ref[hd, ci]
                dst = dst_ref[hd]
                scores_t = _dot(kh, qh, _NT) * dm_t
                dsc = _dot(g, v, _NT) * dm
                dsc_t = _dot(v, g, _NT) * dm_t
                dqh = _dot(dsc, kh) + _dot(g, st0) * xi
                dkh = _dot(dsc_t, qh) + _dot(v, dst) * zeta
                dv_buf[slot, hd, r, :] = (_dot(scores_t, g) + _dot(kh * zeta, dst, _NT)).astype(dv_buf.dtype)
                dst_ref[hd] = dst * cdec + _dot(g, qh * xi, _TN)
                dq = dqh if dq is None else dq + dqh
                dk = dkh if dk is None else dk + dkh
            dk = dk * (RET_DK ** -0.5)
            dq_buf[slot, r, :] = (dq * cs - _swap_halves(dq) * sn).astype(dq_buf.dtype)
            dk_buf[slot, r, :] = (dk * cs - _swap_halves(dk) * sn).astype(dk_buf.dtype)
            return carry

        lax.fori_loop(0, ncb, chunk, 0, unroll=RET_UNROLL)
        _writeback_commit(step, (N_HEADS // 2) * ntb, slot, out_copies)

    any_spec = pl.BlockSpec(memory_space=pl.ANY)
    return pl.pallas_call(
        body, name="ret_bwd", grid=(N_HEADS // 2, ntb),
        in_specs=[pl.BlockSpec((None, tb, LANE), lambda p, t: (CB_RQ + p, ntb - 1 - t, 0)),
                  pl.BlockSpec((None, tb, LANE), lambda p, t: (CB_RK + p, ntb - 1 - t, 0)),
                  pl.BlockSpec((2, tb, LANE), lambda p, t: (CB_RV // 2 + p, ntb - 1 - t, 0)),
                  pl.BlockSpec((tb, LANE), lambda p, t: (ntb - 1 - t, 0)),
                  pl.BlockSpec((tb, LANE), lambda p, t: (ntb - 1 - t, 0)),
                  pl.BlockSpec((2, tb, LANE), lambda p, t: (p, ntb - 1 - t, 0)),
                  pl.BlockSpec((2, ncb, LANE, LANE), lambda p, t: (p, ntb - 1 - t, 0, 0)),
                  any_spec],
        out_specs=any_spec,
        out_shape=jax.ShapeDtypeStruct(dpb.shape, dpb.dtype),
        scratch_shapes=[pltpu.VMEM((2, tb, LANE), dpb.dtype), pltpu.VMEM((2, tb, LANE), dpb.dtype),
                        pltpu.VMEM((2, 2, tb, LANE), dpb.dtype), pltpu.VMEM((2, LANE, LANE), F32),
                        pltpu.SemaphoreType.DMA((2, 3))],
        input_output_aliases={7: 0},
        compiler_params=_params(("arbitrary", "arbitrary")),
    )(pb, pb, pb, cos_t, sin_t, do, rsave, dpb)


def _row_tile(t_len, want):
    return min(want, t_len)


PAIR_CB = 2 * CB_PER_SHARD


def proj_forward(h, wt, chips, t_len, name, pb=None):
    tm = _row_tile(t_len, 1024)

    def body(chips_ref, h_ref, w_ref, *rest):
        del chips_ref
        o_ref = rest[-1]
        acc = _dot(h_ref[...], w_ref[...], _NT)
        for jj in range(PAIR_CB):
            o_ref[jj] = acc[:, jj * LANE:(jj + 1) * LANE]

    given = [] if pb is None else [pb]
    return pl.pallas_call(
        body, name=name,
        grid_spec=pltpu.PrefetchScalarGridSpec(
            num_scalar_prefetch=1, grid=(chips.shape[0], t_len // tm),
            in_specs=[pl.BlockSpec((tm, D_MODEL), lambda j, i, ch: (i, 0)),
                      pl.BlockSpec((PAIR_CB * LANE, D_MODEL), lambda j, i, ch: (ch[j], 0))]
            + [pl.BlockSpec(memory_space=pl.ANY)] * len(given),
            out_specs=pl.BlockSpec((PAIR_CB, tm, LANE), lambda j, i, ch: (ch[j], i, 0))),
        out_shape=jax.ShapeDtypeStruct((N_CB, t_len, LANE), F32),
        input_output_aliases={3: 0} if given else {},
        compiler_params=_params(("arbitrary", "arbitrary")),
    )(chips, h, wt, *given)


def proj_backward_input(dpb, wt, token, x, dy, norm_g, scale1p, t_len):
    tm = _row_tile(t_len, 512)

    def body(a_ref, wt_hbm, token_ref, x_ref, dy_ref, g_ref, sc_ref, gx_ref, vec_ref, w_ref, sem):
        del token_ref
        i = pl.program_id(0)

        @pl.when(i == 0)
        def _():
            cp = pltpu.make_async_copy(wt_hbm, w_ref, sem)
            cp.start()
            cp.wait()

        a = jnp.concatenate([a_ref[jj].astype(_BF) for jj in range(N_CB)], axis=1)
        dhv = _dot(a, w_ref[...])
        xv, g, sc = x_ref[...], g_ref[...], sc_ref[...]
        r = lax.rsqrt(jnp.mean(xv * xv, axis=-1, keepdims=True) + EPS)
        xn = xv * r
        dxn = dhv * (g * sc)
        gx_ref[...] = dy_ref[...] + r * dxn - xn * (r * r) * jnp.mean(xv * dxn, axis=-1, keepdims=True)
        t = dhv * xn
        _acc_rows(vec_ref, i, [jnp.sum(t * sc, axis=0, keepdims=True),
                               jnp.sum(t * g, axis=0, keepdims=True),
                               jnp.sum(dhv, axis=0, keepdims=True)])

    row = pl.BlockSpec((tm, D_MODEL), lambda i: (i, 0))
    return pl.pallas_call(
        body, name="proj_bwd_input", grid=(t_len // tm,),
        in_specs=[pl.BlockSpec((N_CB, tm, LANE), lambda i: (0, i, 0)),
                  pl.BlockSpec(memory_space=pl.ANY),
                  pl.BlockSpec(token.shape, lambda i: (0, 0)),
                  row, row, _vec_spec(), _vec_spec()],
        out_specs=[row, pl.BlockSpec((8, D_MODEL), lambda i: (0, 0))],
        out_shape=[jax.ShapeDtypeStruct((t_len, D_MODEL), F32), jax.ShapeDtypeStruct((8, D_MODEL), F32)],
        scratch_shapes=[pltpu.VMEM(wt.shape, wt.dtype), pltpu.SemaphoreType.DMA],
        compiler_params=_params(("arbitrary",)),
    )(dpb, wt, token, x, dy, norm_g, scale1p)


def proj_backward_weight(h_t, dpb, t_len):
    tk = _row_tile(t_len, 2048)

    def body(h_ref, b_ref, o_ref):
        k = pl.program_id(1)
        b = jnp.concatenate([b_ref[jj].astype(_BF) for jj in range(PAIR_CB)], axis=1)
        part = _dot(h_ref[...], b)

        @pl.when(k == 0)
        def _():
            for s in range(2):
                o_ref[s] = part[:, s * SHARD_IN:(s + 1) * SHARD_IN]

        @pl.when(k > 0)
        def _():
            for s in range(2):
                o_ref[s] = o_ref[s] + part[:, s * SHARD_IN:(s + 1) * SHARD_IN]

    return pl.pallas_call(
        body, name="proj_bwd_weight", grid=(N_DEV // 2, t_len // tk),
        in_specs=[pl.BlockSpec((D_MODEL, tk), lambda j, k: (0, k)),
                  pl.BlockSpec((PAIR_CB, tk, LANE), lambda j, k: (j, k, 0))],
        out_specs=pl.BlockSpec((2, D_MODEL, SHARD_IN), lambda j, k: (j, 0, 0)),
        out_shape=jax.ShapeDtypeStruct((N_DEV, D_MODEL, SHARD_IN), F32),
        compiler_params=_params(("arbitrary", "arbitrary")),
    )(h_t, dpb)


def sibling_blocks(g_in, c_idx):
    tr = D_MODEL

    def body(c_ref, g_ref, o_ref):
        del c_ref
        o_ref[...] = g_ref[...].astype(o_ref.dtype)

    return pl.pallas_call(
        body, name="sibling_blocks",
        grid_spec=pltpu.PrefetchScalarGridSpec(
            num_scalar_prefetch=1, grid=(N_DEV // 2, D_MODEL // tr),
            in_specs=[pl.BlockSpec((None, tr, SHARD_IN), lambda q, i, c: (2 * q + 1 - c[0], i, 0))],
            out_specs=pl.BlockSpec((None, tr, SHARD_IN), lambda q, i, c: (q, i, 0))),
        out_shape=jax.ShapeDtypeStruct((N_DEV // 2, D_MODEL, SHARD_IN), _BF),
        compiler_params=_params(("arbitrary", "arbitrary")),
    )(c_idx, g_in)


def _vec_spec():
    return pl.BlockSpec((1, D_MODEL), lambda i: (0, 0))


def _acc_rows(ref, i, rows):
    @pl.when(i == 0)
    def _():
        ref[...] = jnp.zeros_like(ref)

    for n, row in enumerate(rows):
        ref[n:n + 1, :] = ref[n:n + 1, :] + row


def adaln_forward(x, norm_g, scale1p, shift, t_len):
    tm = _row_tile(t_len, 1024)

    def body(x_ref, g_ref, sc_ref, sh_ref, h_ref, ht_ref):
        xv = x_ref[...]
        r = lax.rsqrt(jnp.mean(xv * xv, axis=-1, keepdims=True) + EPS)
        h = xv * r * g_ref[...] * sc_ref[...] + sh_ref[...]
        h_ref[...] = h.astype(h_ref.dtype)
        ht_ref[...] = h.T.astype(ht_ref.dtype)

    return pl.pallas_call(
        body, name="adaln_fwd", grid=(t_len // tm,),
        in_specs=[pl.BlockSpec((tm, D_MODEL), lambda i: (i, 0)), _vec_spec(), _vec_spec(), _vec_spec()],
        out_specs=[pl.BlockSpec((tm, D_MODEL), lambda i: (i, 0)), pl.BlockSpec((D_MODEL, tm), lambda i: (0, i))],
        out_shape=[jax.ShapeDtypeStruct((t_len, D_MODEL), _BF), jax.ShapeDtypeStruct((D_MODEL, t_len), _BF)],
        compiler_params=_params(("arbitrary",)),
    )(x, norm_g, scale1p, shift)


def _head_norm(o, g):
    r = lax.rsqrt(jnp.mean(o * o, axis=-1, keepdims=True) + EPS)
    return r, o * r * g


def _group_spec(tm, cb):
    return pl.BlockSpec((N_HEADS, tm, LANE), lambda i, cb=cb: (cb // N_HEADS, i, 0))


MID_FINAL_G, MID_GATE, MID_LOSS, MID_HG_G, MID_RET_G = range(5)


def middle(x, target, oa, ob, pb, wout, gate, final_g, hg_g, ret_g, t_len):
    tm = _row_tile(t_len, 256)
    n_steps = t_len // tm

    def body(x_ref, t_ref, oa_ref, ob_ref, hz_ref, rz_ref, ga_ref, gb_ref, w_ref, gate_ref, fg_ref, hg_ref, rg_ref,
             dy_ref, doa_ref, dob_ref, dw_ref, vec_ref, dpb_ref, m_scr, dm_scr, keep, bufs, sems):
        i = pl.program_id(0)
        rows = pl.ds(pl.multiple_of(i * tm, tm), tm)

        def group_copies(sl):
            return [pltpu.make_async_copy(bufs.at[sl, n], dpb_ref.at[pl.ds(cb, N_HEADS), rows], sems.at[sl, n])
                    for n, cb in enumerate((CB_HZ, CB_RZ, CB_GA, CB_GB))]
        sides = ((oa_ref, hz_ref, ga_ref, hg_ref, doa_ref), (ob_ref, rz_ref, gb_ref, rg_ref, dob_ref))
        for hh in range(N_HEADS):
            ls = slice(hh * LANE, (hh + 1) * LANE)
            acc = None
            for side, (o_ref, z_ref, gt_ref, g_ref, _) in enumerate(sides):
                o = o_ref[hh]
                rr = lax.rsqrt(jnp.mean(o * o, axis=-1, keepdims=True) + EPS)
                orr = o * rr
                zz = z_ref[hh]
                sz = _sigmoid(zz)
                sgt = _sigmoid(gt_ref[hh])
                keep[side, hh, 0] = orr
                keep[side, hh, 1] = sz
                keep[side, hh, 2] = sgt
                keep[side, hh, 3] = jnp.broadcast_to(rr, orr.shape)
                u = sgt * ((orr * g_ref[:, ls]) * (zz * sz))
                acc = u if acc is None else acc + u
            m_scr[:, ls] = acc.astype(m_scr.dtype)
        zv = _dot(m_scr[...], w_ref[...])
        gt, fg = gate_ref[...], fg_ref[...]
        y = x_ref[...] + gt * zv
        r = lax.rsqrt(jnp.mean(y * y, axis=-1, keepdims=True) + EPS)
        yn = y * r
        err = yn * fg - t_ref[...]
        loss = 0.5 * jnp.sum(jnp.mean(err * err, axis=-1, keepdims=True), axis=0, keepdims=True)
        dout = err * (1.0 / D_MODEL)
        gd = dout * fg
        dy = r * gd - yn * (r * r) * jnp.mean(y * gd, axis=-1, keepdims=True)
        dy_ref[...] = dy
        dz = (dy * gt).astype(_BF)
        dm_scr[...] = _dot(dz, w_ref[...], _NT)
        part = _dot(m_scr[...], dz, _TN)

        @pl.when(i == 0)
        def _():
            dw_ref[...] = part

        @pl.when(i > 0)
        def _():
            dw_ref[...] = dw_ref[...] + part

        slot = _writeback_reserve(i, group_copies)
        dg = [[], []]
        for hh in range(N_HEADS):
            ls = slice(hh * LANE, (hh + 1) * LANE)
            dmh = dm_scr[:, ls]
            for side, (o_ref, z_ref, gt_ref, g_ref, do_ref) in enumerate(sides):
                zz, g = z_ref[hh], g_ref[:, ls]
                orr, sz, sgt, rr = keep[side, hh, 0], keep[side, hh, 1], keep[side, hh, 2], keep[side, hh, 3]
                n = orr * g
                silu = zz * sz
                du = dmh * sgt
                bufs[slot, 2 + side, hh] = (dmh * (n * silu) * (sgt * (1.0 - sgt))).astype(bufs.dtype)
                bufs[slot, side, hh] = (du * n * (sz * (1.0 + zz * (1.0 - sz)))).astype(bufs.dtype)
                dn = du * silu
                dg[side].append(jnp.sum(dn * orr, axis=0, keepdims=True))
                gdn = dn * g
                do_ref[hh] = (rr * (gdn - orr * jnp.mean(orr * gdn, axis=-1, keepdims=True))).astype(do_ref.dtype)
        _acc_rows(vec_ref, i, [jnp.sum(dout * yn, axis=0, keepdims=True),
                               jnp.sum(dy * zv, axis=0, keepdims=True),
                               jnp.broadcast_to(loss, (1, D_MODEL)),
                               jnp.concatenate(dg[0], axis=1), jnp.concatenate(dg[1], axis=1)])
        _writeback_commit(i, n_steps, slot, group_copies)

    row = pl.BlockSpec((tm, D_MODEL), lambda i: (i, 0))
    head = pl.BlockSpec((N_HEADS, tm, LANE), lambda i: (0, i, 0))
    full = pl.BlockSpec((D_MODEL, D_MODEL), lambda i: (0, 0))
    return pl.pallas_call(
        body, name="middle", grid=(n_steps,),
        in_specs=[row, row, head, head, _group_spec(tm, CB_HZ), _group_spec(tm, CB_RZ), _group_spec(tm, CB_GA),
                  _group_spec(tm, CB_GB), full, _vec_spec(), _vec_spec(), _vec_spec(), _vec_spec()],
        out_specs=[row, head, head, full, pl.BlockSpec((8, D_MODEL), lambda i: (0, 0)),
                   pl.BlockSpec(memory_space=pl.ANY)],
        out_shape=[jax.ShapeDtypeStruct((t_len, D_MODEL), F32),
                   jax.ShapeDtypeStruct((N_HEADS, t_len, LANE), _BF),
                   jax.ShapeDtypeStruct((N_HEADS, t_len, LANE), _BF),
                   jax.ShapeDtypeStruct((D_MODEL, D_MODEL), F32),
                   jax.ShapeDtypeStruct((8, D_MODEL), F32),
                   jax.ShapeDtypeStruct((N_CB, t_len, LANE), _BF)],
        scratch_shapes=[pltpu.VMEM((tm, D_MODEL), _BF), pltpu.VMEM((tm, D_MODEL), F32),
                        pltpu.VMEM((2, N_HEADS, 4, tm, LANE), F32),
                        pltpu.VMEM((2, 4, N_HEADS, tm, LANE), _BF), pltpu.SemaphoreType.DMA((2, 4))],
        compiler_params=_params(("arbitrary",)),
    )(x, target, oa, ob, pb, pb, pb, pb, wout, gate, final_g, hg_g, ret_g)


def device_step(x, target, mod, lb, project, norm_g, hg_g, ret_g, final_g, c_idx=None, start_exchange=None):
    t_len = x.shape[0]
    shift, scale, gate = mod[:, :D_MODEL], mod[:, D_MODEL:2 * D_MODEL], mod[:, 2 * D_MODEL:]
    scale1p = 1.0 + scale
    cos_t, sin_t = _rope_tables(t_len)
    h, h_t = adaln_forward(x, norm_g, scale1p, shift, t_len)
    pb, wt, wout = project(h)
    oa, ssave, asave = hgrn_forward(pb, lb, t_len)
    ob, rsave = retention_forward(pb, cos_t, sin_t, t_len)
    dy, doa, dob, dwout, vec_mid, dpb = middle(x, target, oa, ob, pb, wout, gate, final_g, hg_g, ret_g, t_len)
    dpb, dlb = hgrn_backward(pb, lb, doa, ssave, asave, dpb, t_len)
    dpb = retention_backward(pb, cos_t, sin_t, dob, rsave, dpb, t_len)
    c_idx = jnp.zeros((1,), jnp.int32) if c_idx is None else c_idx
    dwin = proj_backward_weight(h_t, dpb, t_len)
    dwin_sib = sibling_blocks(dwin, c_idx)
    token, pending = (start_exchange(dwin, dwin_sib, dwout) if start_exchange
                      else (jnp.zeros((8, LANE), F32), None))
    grad_x, vec_ada = proj_backward_input(dpb, wt, token, x, dy, norm_g, scale1p, t_len)
    return grad_x, dwin, dwout, vec_mid, vec_ada, dlb, pending


PACK_ROWS = 16
ROW_NORM_G, ROW_LB, ROW_HG_G, ROW_RET_G, ROW_FINAL_G, ROW_SHIFT, ROW_SCALE, ROW_GATE, ROW_LOSS = range(9)


def _mesh_pos():
    return lax.axis_index("x"), lax.axis_index("y"), lax.axis_index("c")


def _lin(pos):
    return 4 * pos[0] + 2 * pos[1] + pos[2]


def _xor_peer(pos, k):
    return tuple(1 - p if (k >> s) & 1 else p for p, s in zip(pos, (2, 1, 0)))


def _other_chips(pos):
    x, y, _ = pos
    return [(1 - x, y), (x, 1 - y), (1 - x, 1 - y)]


def _remote(src, dst, send_sem, recv_sem, to):
    return pltpu.make_async_remote_copy(src_ref=src, dst_ref=dst, send_sem=send_sem, recv_sem=recv_sem,
                                        device_id=to, device_id_type=MESH)


def pre_exchange(c, w_ada, b_ada, logits, w_in, w_out):
    def body(c_ref, wada_ref, bada_ref, logit_ref, win_ref, wout_ref, mod_ref, scall_ref, lb_ref, wg_ref, wog_ref,
             cg_ref, modall_ref, parts_ref, wt_ref, wo_ref, send1, recv1, send2, recv2, local):
        pos = _mesh_pos()
        placed = [pltpu.make_async_copy(wt_ref, wg_ref.at[_lin(pos)], local.at[0]),
                  pltpu.make_async_copy(wo_ref, wog_ref.at[_lin(pos)], local.at[1])]
        cv = c_ref[...]
        slot = lambda p: pl.ds(pl.multiple_of(8 * _lin(p), 8), 8)
        cg_ref[slot(pos), :] = jnp.broadcast_to(cv * _sigmoid(cv), (8, D_MODEL))
        lb_ref[...] = _sigmoid(logit_ref[0:1, :] - logit_ref[1:2, :])
        peers = [_xor_peer(pos, k) for k in range(1, N_DEV)]
        gather = [_remote(cg_ref.at[slot(pos)], cg_ref.at[slot(pos)], send1.at[n], recv1.at[n], p)
                  for n, p in enumerate(peers)]
        for cp in gather:
            cp.start()
        wt_ref[...] = win_ref[...].T.astype(wt_ref.dtype)
        wo_ref[...] = wout_ref[...].astype(wo_ref.dtype)
        for cp in placed:
            cp.start()
        for n, p in enumerate(peers):
            _remote(cg_ref.at[slot(p)], cg_ref.at[slot(p)], send1.at[n], recv1.at[n], p).wait_recv()
        modall_ref[...] = _dot(cg_ref[...], wada_ref[...])
        scatter = [_remote(modall_ref.at[slot(p)], parts_ref.at[slot(pos)], send2.at[n], recv2.at[n], p)
                   for n, p in enumerate(peers)]
        for cp in scatter:
            cp.start()
        parts_ref[slot(pos), :] = modall_ref[slot(pos), :]
        for n, p in enumerate(peers):
            _remote(modall_ref.at[slot(p)], parts_ref.at[slot(p)], send2.at[n], recv2.at[n], p).wait_recv()
        for cp in gather + scatter:
            cp.wait_send()
        for j in range(N_DEV):
            cols = slice(j * SHARD_ADA, (j + 1) * SHARD_ADA)
            mod_ref[:, cols] = parts_ref[8 * j:8 * j + 1, :] + bada_ref[:, cols]
            scall_ref[j:j + 1, :] = cg_ref[8 * j:8 * j + 1, :]
        for cp in placed:
            cp.wait()

    vmem = pl.BlockSpec(memory_space=pltpu.VMEM)
    return pl.pallas_call(
        body, name="pre_exchange",
        in_specs=[vmem] * 6, out_specs=[vmem] * 3 + [_HBM] * 2,
        out_shape=[jax.ShapeDtypeStruct((1, 3 * D_MODEL), F32), jax.ShapeDtypeStruct((N_DEV, D_MODEL), F32),
                   jax.ShapeDtypeStruct((1, D_MODEL), F32),
                   jax.ShapeDtypeStruct((N_DEV,) + w_in.shape[::-1], _BF),
                   jax.ShapeDtypeStruct((N_DEV,) + w_out.shape, _BF)],
        scratch_shapes=[pltpu.VMEM((N_DEV * 8, D_MODEL), F32), pltpu.VMEM((N_DEV * 8, SHARD_ADA), F32),
                        pltpu.VMEM((N_DEV * 8, SHARD_ADA), F32),
                        pltpu.VMEM(w_in.shape[::-1], _BF), pltpu.VMEM(w_out.shape, _BF)]
        + [pltpu.SemaphoreType.DMA((N_DEV - 1,))] * 4 + [pltpu.SemaphoreType.DMA((2,))],
        compiler_params=pltpu.CompilerParams(vmem_limit_bytes=VMEM_LIMIT),
    )(c, w_ada, b_ada, logits, w_in, w_out)


def _gather_copies(outs, sems):
    pos = _mesh_pos()
    x, y, c = pos
    sibling = (x, y, 1 - c)

    def route(core):
        return [(x + (1 - core) * (1 - 2 * x), y + core * (1 - 2 * y)),
                (x + core * (1 - 2 * x), y + (1 - core) * (1 - 2 * y)),
                (1 - x, 1 - y)]

    mine = [(*chip, c) for chip in route(c)]
    carried = [pos, pos, pos, mine[0], mine[0], mine[1], mine[2]]
    to = [sibling, mine[0], mine[1], mine[1], sibling, sibling, sibling]
    landed = [sibling] + mine + [(*chip, 1 - c) for chip in route(1 - c)]

    def sent(a, k):
        dst = outs[a].at[_lin(carried[k])]
        return _remote(dst, dst, *sems(a, k), to[k])

    def arrival(a, k):
        dst = outs[a].at[_lin(landed[k])]
        return _remote(dst, dst, *sems(a, k), pos)

    return sent, arrival


_ROUNDS = (range(0, 3), range(3, 6), range(6, 7))


def _round_sems(send, recv, rnd):
    n = len(_ROUNDS[rnd])
    return lambda a, k: (send.at[n * a + k - _ROUNDS[rnd][0]], recv.at[n * a + k - _ROUNDS[rnd][0]])


def weight_gather_start(wg, woutg):
    def body(wg_ref, woutg_ref, send, recv, wg_out, woutg_out, token):
        del wg_out, woutg_out
        sent, _ = _gather_copies((wg_ref, woutg_ref), _round_sems(send, recv, 0))
        for a in range(2):
            for k in _ROUNDS[0]:
                sent(a, k).start()
        token[...] = jnp.zeros_like(token)

    hbm = lambda a: pltpu.with_memory_space_constraint(a, pltpu.HBM)
    n = 2 * len(_ROUNDS[0])
    return pl.pallas_call(
        body, name="weight_gather_start",
        in_specs=[_HBM] * 2,
        out_specs=[_SEM, _SEM, _HBM, _HBM, pl.BlockSpec(memory_space=pltpu.VMEM)],
        out_shape=[pltpu.SemaphoreType.DMA((n,)), pltpu.SemaphoreType.DMA((n,)),
                   pltpu.HBM(wg.shape, wg.dtype), pltpu.HBM(woutg.shape, woutg.dtype),
                   jax.ShapeDtypeStruct((8, LANE), F32)],
        input_output_aliases={0: 2, 1: 3},
        compiler_params=pltpu.CompilerParams(has_side_effects=pltpu.SideEffectType.DATAFLOW_SIDE_EFFECTING),
    )(hbm(wg), hbm(woutg))


def _wait_copies(sent, arrival, ks):
    for a in range(2):
        for k in ks:
            arrival(a, k).wait_recv()
            sent(a, k).wait_send()


def weight_gather_wait(name, rnd, ks, send, recv, wg, woutg, after):
    def body(wg_ref, woutg_ref, send, recv, after_ref, wg_out, woutg_out):
        del after_ref, wg_out, woutg_out
        _wait_copies(*_gather_copies((wg_ref, woutg_ref), _round_sems(send, recv, rnd)), ks)

    return pl.pallas_call(
        body, name=name,
        in_specs=[_HBM] * 2 + [_SEM, _SEM, pl.BlockSpec(memory_space=pl.ANY)],
        out_specs=[_HBM] * 2,
        out_shape=[pltpu.HBM(wg.shape, wg.dtype), pltpu.HBM(woutg.shape, woutg.dtype)],
        input_output_aliases={0: 0, 1: 1},
        compiler_params=pltpu.CompilerParams(has_side_effects=pltpu.SideEffectType.DATAFLOW_SIDE_EFFECTING),
    )(wg, woutg, send, recv, after)


def weight_pass_start(wg, woutg):
    def body(wg_ref, woutg_ref, send, recv, wg_out, woutg_out):
        del wg_out, woutg_out
        sent, _ = _gather_copies((wg_ref, woutg_ref), _round_sems(send, recv, 1))
        for a in range(2):
            for k in _ROUNDS[1]:
                sent(a, k).start()

    n = 2 * len(_ROUNDS[1])
    return pl.pallas_call(
        body, name="weight_pass_start",
        in_specs=[_HBM] * 2,
        out_specs=[_SEM, _SEM, _HBM, _HBM],
        out_shape=[pltpu.SemaphoreType.DMA((n,)), pltpu.SemaphoreType.DMA((n,)),
                   pltpu.HBM(wg.shape, wg.dtype), pltpu.HBM(woutg.shape, woutg.dtype)],
        input_output_aliases={0: 2, 1: 3},
        compiler_params=pltpu.CompilerParams(has_side_effects=pltpu.SideEffectType.DATAFLOW_SIDE_EFFECTING),
    )(wg, woutg)


def weight_pass_last(wg, woutg):
    def body(wg_in, woutg_in, wg_ref, woutg_ref, send, recv):
        del wg_in, woutg_in
        sent, arrival = _gather_copies((wg_ref, woutg_ref), _round_sems(send, recv, 2))
        for a in range(2):
            sent(a, _ROUNDS[2][0]).start()
        _wait_copies(sent, arrival, _ROUNDS[2])

    any_spec = pl.BlockSpec(memory_space=pl.ANY)
    n = 2 * len(_ROUNDS[2])
    return pl.pallas_call(
        body, name="weight_pass_last",
        in_specs=[any_spec, any_spec], out_specs=[any_spec, any_spec],
        out_shape=[jax.ShapeDtypeStruct(wg.shape, wg.dtype), jax.ShapeDtypeStruct(woutg.shape, woutg.dtype)],
        scratch_shapes=[pltpu.SemaphoreType.DMA((n,)), pltpu.SemaphoreType.DMA((n,))],
        input_output_aliases={0: 0, 1: 1},
    )(wg, woutg)


def grad_pair_exchange(g_sib, g_out):
    def body(gsib_ref, gout_ref, ra_ref, rb_ref, send, recv):
        pos = _mesh_pos()
        x, y, c = pos
        sibling = (x, y, 1 - c)
        copies = []
        for q in range(4):
            copies.append(_remote(gsib_ref.at[q], ra_ref.at[q], send.at[q], recv.at[q], sibling))
            copies.append(_remote(gout_ref.at[2 * q + (1 - c)], rb_ref.at[q], send.at[4 + q], recv.at[4 + q], sibling))
        for cp in copies:
            cp.start()
        for cp in copies:
            cp.wait_recv()
        for cp in copies:
            cp.wait_send()

    any_spec = pl.BlockSpec(memory_space=pl.ANY)
    return pl.pallas_call(
        body, name="grad_pair_exchange",
        in_specs=[any_spec, any_spec], out_specs=[any_spec, any_spec],
        out_shape=[jax.ShapeDtypeStruct(g_sib.shape, g_sib.dtype), jax.ShapeDtypeStruct((4,) + g_out.shape[1:], F32)],
        scratch_shapes=[pltpu.SemaphoreType.DMA((8,)), pltpu.SemaphoreType.DMA((8,))],
    )(g_sib, g_out)


def pair_sum(g_in, ra, g_out, rb, c_idx):
    tr = D_MODEL

    def body(c_ref, gin_ref, ra_ref, gout_ref, rb_ref, sb_ref, sbo_ref):
        del c_ref
        sb_ref[...] = (gin_ref[...] + ra_ref[...].astype(F32)).astype(sb_ref.dtype)
        sbo_ref[...] = gout_ref[...] + rb_ref[...]

    n_i = D_MODEL // tr
    return pl.pallas_call(
        body, name="pair_sum",
        grid_spec=pltpu.PrefetchScalarGridSpec(
            num_scalar_prefetch=1, grid=(4, n_i),
            in_specs=[pl.BlockSpec((None, tr, SHARD_IN), lambda q, i, c: (2 * q + c[0], i, 0)),
                      pl.BlockSpec((None, tr, SHARD_IN), lambda q, i, c: (q, i, 0)),
                      pl.BlockSpec((None, SHARD_OUT // n_i, D_MODEL), lambda q, i, c: (2 * q + c[0], i, 0)),
                      pl.BlockSpec((None, SHARD_OUT // n_i, D_MODEL), lambda q, i, c: (q, i, 0))],
            out_specs=[pl.BlockSpec((None, tr, SHARD_IN), lambda q, i, c: (q, i, 0)),
                       pl.BlockSpec((None, SHARD_OUT // n_i, D_MODEL), lambda q, i, c: (q, i, 0))]),
        out_shape=[jax.ShapeDtypeStruct(ra.shape, _BF), jax.ShapeDtypeStruct(rb.shape, F32)],
        compiler_params=_params(("arbitrary", "arbitrary")),
    )(c_idx, g_in, ra, g_out, rb)


_HBM = pl.BlockSpec(memory_space=pltpu.HBM)
_SEM = pl.BlockSpec(memory_space=pltpu.SEMAPHORE)
_N_CHIP_COPIES = 6


def _chip_copies(sb_ref, sbo_ref, rc_ref, rco_ref, send, recv):
    pos = _mesh_pos()
    copies = []
    for a, (src, dst) in enumerate(((sb_ref, rc_ref), (sbo_ref, rco_ref))):
        for j, chip in enumerate(_other_chips(pos)):
            copies.append(_remote(src.at[2 * chip[0] + chip[1]], dst.at[j], send.at[3 * a + j], recv.at[3 * a + j],
                                  (*chip, pos[2])))
    return copies


def grad_chip_start(sb, sbo):
    def body(sb_ref, sbo_ref, rc_ref, rco_ref, send, recv, sb_thru, sbo_thru, rc_thru, rco_thru, token):
        del sb_thru, sbo_thru, rc_thru, rco_thru
        for cp in _chip_copies(sb_ref, sbo_ref, rc_ref, rco_ref, send, recv):
            cp.start()
        token[...] = jnp.zeros_like(token)

    hbm = lambda a: pltpu.with_memory_space_constraint(a, pltpu.HBM)
    rc = lax.empty((3,) + sb.shape[1:], sb.dtype)
    rco = lax.empty((3,) + sbo.shape[1:], sbo.dtype)
    return pl.pallas_call(
        body, name="grad_chip_start",
        in_specs=[_HBM] * 4,
        out_specs=[_SEM, _SEM, _HBM, _HBM, _HBM, _HBM, pl.BlockSpec(memory_space=pltpu.VMEM)],
        out_shape=[pltpu.SemaphoreType.DMA((_N_CHIP_COPIES,)), pltpu.SemaphoreType.DMA((_N_CHIP_COPIES,)),
                   pltpu.HBM(sb.shape, sb.dtype), pltpu.HBM(sbo.shape, sbo.dtype),
                   pltpu.HBM(rc.shape, rc.dtype), pltpu.HBM(rco.shape, rco.dtype),
                   jax.ShapeDtypeStruct((8, LANE), F32)],
        input_output_aliases={0: 2, 1: 3, 2: 4, 3: 5},
        compiler_params=pltpu.CompilerParams(has_side_effects=pltpu.SideEffectType.DATAFLOW_SIDE_EFFECTING),
    )(hbm(sb), hbm(sbo), hbm(rc), hbm(rco))


def grad_chip_wait(send, recv, sb, sbo, rc, rco, after):
    def body(sb_ref, sbo_ref, rc_ref, rco_ref, send, recv, after_ref, sb_o, sbo_o, rc_o, rco_o):
        del after_ref, sb_o, sbo_o, rc_o, rco_o
        for cp in _chip_copies(sb_ref, sbo_ref, rc_ref, rco_ref, send, recv):
            cp.wait_send()
            cp.wait_recv()

    return pl.pallas_call(
        body, name="grad_chip_wait",
        in_specs=[_HBM] * 4 + [_SEM, _SEM, pl.BlockSpec(memory_space=pl.ANY)],
        out_specs=[_HBM] * 4,
        out_shape=[pltpu.HBM(sb.shape, sb.dtype), pltpu.HBM(sbo.shape, sbo.dtype),
                   pltpu.HBM(rc.shape, rc.dtype), pltpu.HBM(rco.shape, rco.dtype)],
        input_output_aliases={0: 0, 1: 1, 2: 2, 3: 3},
        compiler_params=pltpu.CompilerParams(has_side_effects=pltpu.SideEffectType.DATAFLOW_SIDE_EFFECTING),
    )(sb, sbo, rc, rco, send, recv, after)


def pack_gather(pack):
    def body(pack_ref, packs_ref, psend, precv):
        pos = _mesh_pos()
        me = _lin(pos)
        packs_ref[me] = pack_ref[...]
        peers = [_xor_peer(pos, k) for k in range(1, N_DEV)]
        gather = [_remote(packs_ref.at[me], packs_ref.at[me], psend.at[n], precv.at[n], p) for n, p in enumerate(peers)]
        for cp in gather:
            cp.start()
        for n, p in enumerate(peers):
            _remote(packs_ref.at[_lin(p)], packs_ref.at[_lin(p)], psend.at[n], precv.at[n], p).wait_recv()
        for cp in gather:
            cp.wait_send()

    vmem = pl.BlockSpec(memory_space=pltpu.VMEM)
    return pl.pallas_call(
        body, name="pack_gather", in_specs=[vmem], out_specs=vmem,
        out_shape=jax.ShapeDtypeStruct((N_DEV,) + pack.shape, F32),
        scratch_shapes=[pltpu.SemaphoreType.DMA((N_DEV - 1,)), pltpu.SemaphoreType.DMA((N_DEV - 1,))],
    )(pack)


def pack_rows(vec_mid, vec_ada, dlb):
    def body(mid_ref, ada_ref, dlb_ref, o_ref):
        mid = lambda r: mid_ref[r:r + 1, :]
        rows = [ada_ref[0:1, :], dlb_ref[...], mid(MID_HG_G), mid(MID_RET_G), mid(MID_FINAL_G),
                ada_ref[2:3, :], ada_ref[1:2, :], mid(MID_GATE), mid(MID_LOSS)]
        o_ref[...] = jnp.zeros_like(o_ref)
        for n, row in enumerate(rows):
            o_ref[n:n + 1, :] = row

    vmem = pl.BlockSpec(memory_space=pltpu.VMEM)
    return pl.pallas_call(body, name="pack_rows", in_specs=[vmem] * 3, out_specs=vmem,
                          out_shape=jax.ShapeDtypeStruct((PACK_ROWS, D_MODEL), F32))(vec_mid, vec_ada, dlb)


def _adamw(w, g, m, v):
    m = ADAM_B1 * m + (1.0 - ADAM_B1) * g
    v = ADAM_B2 * v + (1.0 - ADAM_B2) * (g * g)
    m_hat = m / (1.0 - ADAM_B1 ** ADAM_STEP)
    v_hat = v / (1.0 - ADAM_B2 ** ADAM_STEP)
    delta = -ADAM_LR * (m_hat / (jnp.sqrt(v_hat) + ADAM_EPS) + ADAM_WD * w)
    return delta, m, v


def adam_shard(chip_idx, own, parts, w, m, v, name):
    rows, cols = w.shape
    tr = min(rows, 256)

    def body(chip_ref, p0, p1, p2, p3, w_ref, m_ref, v_ref, g_ref, d_ref, nm_ref, nv_ref):
        del chip_ref
        g = ((p0[...].astype(F32) + p1[...].astype(F32)) + p2[...].astype(F32)) + p3[...].astype(F32)
        g_ref[...] = g
        d_ref[...], nm_ref[...], nv_ref[...] = _adamw(w_ref[...], g, m_ref[...], v_ref[...])

    part = lambda q: pl.BlockSpec((None, tr, cols), lambda i, chip, q=q: (q, i, 0))
    tile = pl.BlockSpec((tr, cols), lambda i, chip: (i, 0))
    return pl.pallas_call(
        body, name=name,
        grid_spec=pltpu.PrefetchScalarGridSpec(
            num_scalar_prefetch=1, grid=(rows // tr,),
            in_specs=[pl.BlockSpec((None, tr, cols), lambda i, chip: (chip[0], i, 0)), part(0), part(1), part(2),
                      tile, tile, tile],
            out_specs=[tile] * 4),
        out_shape=[jax.ShapeDtypeStruct(w.shape, F32)] * 4,
        compiler_params=_params(("arbitrary",)),
    )(chip_idx, own, parts, parts, parts, w, m, v)


def adam_ada(sc_t, dmod_all, me_idx, w, m, v):
    def body(me_ref, sc_ref, dm_ref, w_ref, m_ref, v_ref, g_ref, d_ref, nm_ref, nv_ref):
        del me_ref
        g = _dot_f32(sc_ref[...], dm_ref[...])
        g_ref[...] = g
        d_ref[...], nm_ref[...], nv_ref[...] = _adamw(w_ref[...], g, m_ref[...], v_ref[...])

    full = pl.BlockSpec(w.shape, lambda i, me: (0, 0))
    return pl.pallas_call(
        body, name="adam_ada",
        grid_spec=pltpu.PrefetchScalarGridSpec(
            num_scalar_prefetch=1, grid=(1,),
            in_specs=[pl.BlockSpec(sc_t.shape, lambda i, me: (0, 0)),
                      pl.BlockSpec((LANE, SHARD_ADA), lambda i, me: (0, me[0])), full, full, full],
            out_specs=[full] * 4),
        out_shape=[jax.ShapeDtypeStruct(w.shape, F32)] * 4,
        compiler_params=_params(("arbitrary",)),
    )(me_idx, sc_t, dmod_all, w, m, v)


def adam_vectors(packs, lb, params, ms, vs):
    n = len(params)

    def body(*refs):
        packs_ref, lb_ref = refs[0], refs[1]
        w_refs, m_refs, v_refs = refs[2:2 + n], refs[2 + n:2 + 2 * n], refs[2 + 2 * n:2 + 3 * n]
        loss_ref = refs[2 + 3 * n]
        outs = refs[3 + 3 * n:3 + 7 * n]
        tot_ref = refs[3 + 7 * n]
        tot = packs_ref[0]
        for d in range(1, N_DEV):
            tot = tot + packs_ref[d]
        tot_ref[...] = tot
        row = lambda r: tot_ref[r:r + 1, :]
        lbv = lb_ref[...]
        dl0 = row(ROW_LB) * lbv * (1.0 - lbv)
        grads = [[row(ROW_NORM_G)],
                 [jnp.concatenate([row(ROW_SHIFT), row(ROW_SCALE), row(ROW_GATE)], axis=1)],
                 [dl0, -dl0],
                 [row(ROW_HG_G)], [row(ROW_RET_G)], [row(ROW_FINAL_G)]]
        loss_ref[...] = tot_ref[ROW_LOSS:ROW_LOSS + 1, 0:LANE]
        for j, g_rows in enumerate(grads):
            for r, g in enumerate(g_rows):
                rs = slice(r, r + 1)
                d, nm, nv = _adamw(w_refs[j][rs, :], g, m_refs[j][rs, :], v_refs[j][rs, :])
                outs[4 * j][rs, :] = g
                outs[4 * j + 1][rs, :] = d
                outs[4 * j + 2][rs, :] = nm
                outs[4 * j + 3][rs, :] = nv

    vmem = pl.BlockSpec(memory_space=pltpu.VMEM)
    out_shape = [jax.ShapeDtypeStruct((1, LANE), F32)]
    for w in params:
        out_shape += [jax.ShapeDtypeStruct(w.shape, F32)] * 4
    return pl.pallas_call(
        body, name="adam_vectors", in_specs=[vmem] * (2 + 3 * n), out_specs=[vmem] * len(out_shape),
        out_shape=out_shape, scratch_shapes=[pltpu.VMEM((PACK_ROWS, D_MODEL), F32)],
    )(packs, lb, *params, *ms, *vs)


def vector_tail(vec_mid, vec_ada, dlb, sc_t, lb, ada, params, ms, vs):
    n = len(params)
    dmod_rows = (ROW_SHIFT, ROW_SCALE, ROW_GATE)
    blocks = D_MODEL // LANE

    def body(*refs):
        mid_ref, ada_ref, dlb_ref, sc_ref, lb_ref, wa_ref, ma_ref, va_ref = refs[:8]
        w_refs, m_refs, v_refs = refs[8:8 + n], refs[8 + n:8 + 2 * n], refs[8 + 2 * n:8 + 3 * n]
        loss_ref = refs[8 + 3 * n]
        ada_outs = refs[9 + 3 * n:13 + 3 * n]
        outs = refs[13 + 3 * n:13 + 7 * n]
        packs_ref, tot_ref, dm_ref, psend, precv = refs[13 + 7 * n:]
        pos = _mesh_pos()
        me = _lin(pos)
        mid = lambda r: mid_ref[r:r + 1, :]
        rows = [ada_ref[0:1, :], dlb_ref[...], mid(MID_HG_G), mid(MID_RET_G), mid(MID_FINAL_G),
                ada_ref[2:3, :], ada_ref[1:2, :], mid(MID_GATE), mid(MID_LOSS)]
        packs_ref[me] = jnp.zeros((PACK_ROWS, D_MODEL), F32)
        for r, row in enumerate(rows):
            packs_ref[me, r:r + 1, :] = row
        peers = [_xor_peer(pos, k) for k in range(1, N_DEV)]
        gather = [_remote(packs_ref.at[me], packs_ref.at[me], psend.at[k], precv.at[k], p) for k, p in enumerate(peers)]
        for cp in gather:
            cp.start()
        dm_ref[...] = jnp.zeros_like(dm_ref)
        for k, p in enumerate(peers):
            _remote(packs_ref.at[_lin(p)], packs_ref.at[_lin(p)], psend.at[k], precv.at[k], p).wait_recv()
        for cp in gather:
            cp.wait_send()

        tot = packs_ref[0]
        for d in range(1, N_DEV):
            tot = tot + packs_ref[d]
        tot_ref[...] = tot
        row = lambda r: tot_ref[r:r + 1, :]
        lbv = lb_ref[...]
        dl0 = row(ROW_LB) * lbv * (1.0 - lbv)
        grads = [[row(ROW_NORM_G)],
                 [jnp.concatenate([row(ROW_SHIFT), row(ROW_SCALE), row(ROW_GATE)], axis=1)],
                 [dl0, -dl0],
                 [row(ROW_HG_G)], [row(ROW_RET_G)], [row(ROW_FINAL_G)]]
        loss_ref[...] = tot_ref[ROW_LOSS:ROW_LOSS + 1, 0:LANE]
        for j, g_rows in enumerate(grads):
            for r, g in enumerate(g_rows):
                rs = slice(r, r + 1)
                d, nm, nv = _adamw(w_refs[j][rs, :], g, m_refs[j][rs, :], v_refs[j][rs, :])
                outs[4 * j][rs, :] = g
                outs[4 * j + 1][rs, :] = d
                outs[4 * j + 2][rs, :] = nm
                outs[4 * j + 3][rs, :] = nv

        for part, r in enumerate(dmod_rows):
            for d in range(N_DEV):
                for b in range(blocks):
                    dm_ref[part * blocks + b, d:d + 1, :] = packs_ref[d, r:r + 1, b * LANE:(b + 1) * LANE]
        g_ref, d_ref, nm_ref, nv_ref = ada_outs
        for j in range(SHARD_ADA // LANE):
            g_ref[:, j * LANE:(j + 1) * LANE] = _dot_f32(sc_ref[...], dm_ref[(SHARD_ADA // LANE) * me + j])
        d_ref[...], nm_ref[...], nv_ref[...] = _adamw(wa_ref[...], g_ref[...], ma_ref[...], va_ref[...])

    whole = lambda a: pl.BlockSpec(a.shape, lambda i, nd=len(a.shape): (0,) * nd)
    out_shape = [jax.ShapeDtypeStruct((1, LANE), F32)] + [jax.ShapeDtypeStruct(ada[0].shape, F32)] * 4
    for w in params:
        out_shape += [jax.ShapeDtypeStruct(w.shape, F32)] * 4
    operands = (vec_mid, vec_ada, dlb, sc_t, lb, *ada, *params, *ms, *vs)
    return pl.pallas_call(
        body, name="vector_tail", grid=(1,), in_specs=[whole(a) for a in operands],
        out_specs=[whole(a) for a in out_shape], out_shape=out_shape,
        scratch_shapes=[pltpu.VMEM((N_DEV, PACK_ROWS, D_MODEL), F32), pltpu.VMEM((PACK_ROWS, D_MODEL), F32),
                        pltpu.VMEM((len(dmod_rows) * blocks, LANE, LANE), F32),
                        pltpu.SemaphoreType.DMA((N_DEV - 1,)), pltpu.SemaphoreType.DMA((N_DEV - 1,))],
        compiler_params=_params(("arbitrary",)),
    )(*operands)


def kernel(x, c, norm_g, w_ada, b_ada, w_in, hg_lb_logits, hg_norm_g, ret_norm_g, w_out, final_g, loss_target, m_norm_g, m_w_ada, m_b_ada, m_w_in, m_hg_lb_logits, m_hg_norm_g, m_ret_norm_g, m_w_out, m_final_g, v_norm_g, v_w_ada, v_b_ada, v_w_in, v_hg_lb_logits, v_hg_norm_g, v_ret_norm_g, v_w_out, v_final_g):
    pos = _mesh_pos()
    me_idx = jnp.reshape(_lin(pos), (1,)).astype(jnp.int32)
    c_idx = jnp.reshape(pos[2], (1,)).astype(jnp.int32)
    vec = lambda a: a.reshape(1, D_MODEL)

    mod, scall, lb, wtg, woutg = pre_exchange(c, w_ada[0], b_ada, hg_lb_logits, w_in[0], w_out[0])
    chip_idx = jnp.reshape(2 * pos[0] + pos[1], (1,)).astype(jnp.int32)
    gather_send, gather_recv, wtg, woutg, gather_token = weight_gather_start(wtg, woutg)
    mod = mod + gather_token[:1, :1]

    other_chips = jnp.stack([2 * cx + cy for cx, cy in _other_chips(pos)]).astype(jnp.int32)

    def project(h):
        t_len = h.shape[0]
        flat_in = lambda a: a.reshape(D_IN, D_MODEL)
        first = (gather_send, gather_recv)
        wg, wog = weight_gather_wait("weight_wait_sibling", 0, (0,), *first, wtg, woutg, h)
        pb = proj_forward(h, flat_in(wg), chip_idx, t_len, "proj_fwd_own")
        wg, wog = weight_gather_wait("weight_wait_neighbours", 0, (1, 2), *first, wg, wog, pb)
        *second, wg, wog = weight_pass_start(wg, wog)
        wg, wog = weight_gather_wait("weight_wait_passed_sibling", 1, (4, 5), *second, wg, wog, pb)
        pb = proj_forward(h, flat_in(wg), other_chips[:2], t_len, "proj_fwd_neighbours", pb)
        wg, wog = weight_gather_wait("weight_wait_passed_chip", 1, (3,), *second, wg, wog, pb)
        wg, wog = weight_pass_last(wg, wog)
        pb = proj_forward(h, flat_in(wg), other_chips[2:], t_len, "proj_fwd_diagonal", pb)
        return pb, flat_in(wg), wog.reshape(D_MODEL, D_MODEL)

    def start_exchange(dwin, dwin_sib, dwout):
        dwout = dwout.reshape(N_DEV, SHARD_OUT, D_MODEL)
        ra, rb = grad_pair_exchange(dwin_sib, dwout)
        sb, sbo = pair_sum(dwin, ra, dwout, rb, c_idx)
        send, recv, sb, sbo, rc, rco, token = grad_chip_start(sb, sbo)
        return token, (send, recv, sb, sbo, rc, rco)

    grad_x, _, _, vec_mid, vec_ada, dlb, pending = device_step(
        x[0], loss_target[0], mod, lb, project, norm_g, hg_norm_g, ret_norm_g, vec(final_g), c_idx, start_exchange)
    sc_t = jnp.pad(scall.T, ((0, 0), (0, LANE - N_DEV)))
    tail = vector_tail(
        vec_mid, vec_ada, dlb, sc_t, lb, (w_ada[0], m_w_ada[0], v_w_ada[0]),
        (norm_g, b_ada, hg_lb_logits, hg_norm_g, ret_norm_g, vec(final_g)),
        (m_norm_g, m_b_ada, m_hg_lb_logits, m_hg_norm_g, m_ret_norm_g, vec(m_final_g)),
        (v_norm_g, v_b_ada, v_hg_lb_logits, v_hg_norm_g, v_ret_norm_g, vec(v_final_g)))
    g_ada, d_ada, nm_ada, nv_ada = tail[1:5]
    small = [tail[0]] + list(tail[5:])
    loss = small[0][0, 0]
    sb, sbo, rc, rco = grad_chip_wait(*pending, small[0])
    g_in, d_in, nm_in, nv_in = adam_shard(chip_idx, sb, rc, w_in[0], m_w_in[0], v_w_in[0], "adam_w_in")
    g_out, d_out, nm_out, nv_out = adam_shard(chip_idx, sbo, rco, w_out[0], m_w_out[0], v_w_out[0], "adam_w_out")
    (g_ng, d_ng, nm_ng, nv_ng), (g_b, d_b, nm_b, nv_b), (g_lb, d_lb, nm_lb, nv_lb), (g_hg, d_hg, nm_hg, nv_hg), \
        (g_rg, d_rg, nm_rg, nv_rg), (g_fg, d_fg, nm_fg, nv_fg) = [small[1 + 4 * j:5 + 4 * j] for j in range(6)]
    flat = lambda a: a.reshape(D_MODEL)

    def group(ng, ada, b, win, lbl, hg, rg, wo, fg):
        return (ng, ada[None], b, win[None], lbl, hg, rg, wo[None], flat(fg))

    return (loss, grad_x[None],
            *group(g_ng, g_ada, g_b, g_in, g_lb, g_hg, g_rg, g_out, g_fg),
            *group(d_ng, d_ada, d_b, d_in, d_lb, d_hg, d_rg, d_out, d_fg),
            *group(nm_ng, nm_ada, nm_b, nm_in, nm_lb, nm_hg, nm_rg, nm_out, nm_fg),
            *group(nv_ng, nv_ada, nv_b, nv_in, nv_lb, nv_hg, nv_rg, nv_out, nv_fg))
```

```python
import numpy as np
import jax
import jax.numpy as jnp
from jax import lax
from jax.experimental import pallas as pl
from jax.experimental.pallas import tpu as pltpu

F32 = jnp.float32
_BF = jnp.bfloat16

D_MODEL = 1024
N_HEADS = 8
LANE = 128
RET_DK = 64
D_IN = 9216
N_DEV = 8
SHARD_IN = D_IN // N_DEV
SHARD_ADA = 3 * D_MODEL // N_DEV
SHARD_OUT = D_MODEL // N_DEV
N_CB = D_IN // LANE
CB_PER_SHARD = SHARD_IN // LANE
CHUNK = 128
N_LEVELS = 7
EPS = 1e-6
LOG2_E = float(np.log2(np.e))
ROPE_BASE = 10000.0
CB_HQ, CB_HF, CB_HI, CB_HZ, CB_RQ, CB_RK, CB_RV, CB_RZ, CB_GA, CB_GB = 0, 8, 16, 24, 32, 36, 40, 48, 56, 64
VMEM_LIMIT = 56 * 1024 * 1024

ADAM_LR, ADAM_B1, ADAM_B2, ADAM_EPS, ADAM_WD, ADAM_STEP = 0.001, 0.9, 0.999, 1e-08, 0.01, 10

_NN = (((1,), (0,)), ((), ()))
_NT = (((1,), (1,)), ((), ()))
_TN = (((0,), (0,)), ((), ()))
MESH = pl.DeviceIdType.MESH


def _dot(a, b, dims=_NN):
    return lax.dot_general(a.astype(_BF), b.astype(_BF), dims, preferred_element_type=F32)


def _split2(a):
    hi = a.astype(_BF)
    lo = (a - hi.astype(F32)).astype(_BF)
    return jnp.concatenate([hi, lo], axis=1)


def _dot_sel(sel, a):
    n = a.shape[1]
    r = lax.dot_general(sel.astype(_BF), _split2(a), _NN, preferred_element_type=F32)
    return r[:, :n] + r[:, n:]


def _dot_f32(a, b):
    def pieces(v):
        p1 = v.astype(_BF)
        r1 = v - p1.astype(F32)
        p2 = r1.astype(_BF)
        p3 = (r1 - p2.astype(F32)).astype(_BF)
        return p1, p2, p3
    a1, a2, a3 = pieces(a)
    b1, b2, b3 = pieces(b)
    d = lambda u, v: lax.dot_general(u, v, _NN, preferred_element_type=F32)
    return ((d(a1, b3) + d(a2, b2) + d(a3, b1)) + (d(a1, b2) + d(a2, b1))) + d(a1, b1)


def _sigmoid(v):
    return 1.0 / (1.0 + jnp.exp(-v))


def _params(sem=None):
    return pltpu.CompilerParams(dimension_semantics=sem, vmem_limit_bytes=VMEM_LIMIT)


def _hgrn_consts():
    c, nl = CHUNK, N_LEVELS
    t = np.arange(c)[:, None]
    j = np.arange(c)[None, :]
    sel = [j <= t]
    masks = [j == t]
    for l in range(1, nl + 1):
        m = ((t >> l) << l) + (1 << (l - 1)) - 1
        sec = t > m
        sel.append(np.where(sec, (j > m) & (j <= t), (j > t) & (j <= m)))
        same = (t >> l) == (j >> l)
        masks.append(same & sec & (j <= m))
    sel.append(j > t)
    sel = np.concatenate(sel, 0).astype(np.float32)
    masks = np.stack(masks).astype(np.float32)
    sgn = np.stack([np.where((t & (1 << (l - 1))) != 0, 1.0, -1.0) * np.ones((1, LANE)) for l in range(3, nl + 1)])
    return dict(tri=jnp.asarray(sel[:c], _BF),
                lvl=jnp.asarray(masks, F32),
                sgn=jnp.asarray(sgn, F32),
                sel_t=jnp.asarray(sel.T, _BF),
                lvl_b=jnp.asarray(masks, _BF),
                lvlt_b=jnp.asarray(np.swapaxes(masks, 1, 2), _BF))


def _level_exponents(b, logf, b_scr, sgn_ref):
    c = CHUNK
    b_scr[...] = b
    row = lax.broadcasted_iota(jnp.int32, (c, LANE), 0)
    nxt = pltpu.roll(logf, c - 1, 0)
    prv = pltpu.roll(logf, 1, 0)
    r4 = row & 3
    out = [jnp.where((row & 1) == 1, logf, 0.0),
           jnp.where(r4 == 0, nxt, jnp.where(r4 == 1, 0.0, jnp.where(r4 == 2, logf, logf + prv)))]
    for l in range(3, N_LEVELS + 1):
        size, half = 1 << l, 1 << (l - 1)
        ref = jnp.concatenate([jnp.broadcast_to(b_scr[i * size + half - 1:i * size + half, :], (size, LANE))
                               for i in range(c // size)], axis=0)
        out.append((b - ref) * sgn_ref[l - 3])
    return out


def _hgrn_chunk(hq, hf, hi, lbv, tri_ref, sgn_ref, b_scr):
    sq = _sigmoid(hq)
    q = hq * sq
    sg = _sigmoid(hf)
    omlb = 1.0 - lbv
    f = lbv + omlb * sg
    k = 1.0 - f
    logf = jnp.log(f) * LOG2_E
    b = _dot_sel(tri_ref[...], logf)
    bc = jnp.sum(logf, axis=0, keepdims=True)
    lev = [None] + [jnp.exp2(e) for e in _level_exponents(b, logf, b_scr, sgn_ref)]
    return dict(sq=sq, q=q, sg=sg, omlb=omlb, f=f, k=k, v=hi, eb=jnp.exp2(b), erem=jnp.exp2(bc - b),
                ebc=jnp.exp2(bc), lev=lev)


def _blockdiag(a, b):
    z = jnp.zeros_like(a)
    return jnp.concatenate([jnp.concatenate([a, z], axis=1), jnp.concatenate([z, b], axis=1)], axis=0)


def _level_operands(a):
    q, k = a["q"].astype(_BF), a["k"].astype(_BF)
    lev = [None] + [a["lev"][l].astype(_BF) for l in range(1, N_LEVELS + 1)]
    ql = [q] + [q * lev[l] for l in range(1, N_LEVELS + 1)]
    kl = [k] + [k * lev[l] for l in range(1, N_LEVELS + 1)]
    pairs = range(0, N_LEVELS + 1, 2)
    return ([jnp.concatenate([ql[l], ql[l + 1]], axis=1) for l in pairs], [_blockdiag(kl[l], kl[l + 1]) for l in pairs],
            ql, kl)


def _hgrn_scores(a, lvl_ref, q_pairs, k_diags):
    acc = None
    for n, (qp, kd) in enumerate(zip(q_pairs, k_diags)):
        both = lax.dot_general(qp, kd, _NT, preferred_element_type=F32)
        part = lvl_ref[2 * n] * both[:, :CHUNK] + lvl_ref[2 * n + 1] * both[:, CHUNK:]
        acc = part if acc is None else acc + part
    return acc


SCAN_UNROLL = 8
FWD_UNROLL = 16
RET_UNROLL = 16


def _writeback_reserve(step, make_copies):
    slot = step % 2

    @pl.when(step >= 2)
    def _():
        for cp in make_copies(slot):
            cp.wait()

    return slot


def _writeback_commit(step, n_steps, slot, make_copies):
    for cp in make_copies(slot):
        cp.start()

    @pl.when(step == n_steps - 1)
    def _():
        for cp in make_copies(slot):
            cp.wait()
        if n_steps > 1:
            for cp in make_copies(1 - slot):
                cp.wait()


def _resident(const):
    zeros = (0,) * const.ndim
    return pl.BlockSpec(const.shape, lambda p, t: zeros)


def _time_block(t_len):
    return min(t_len, 2048)


def hgrn_forward(pb, lb, t_len):
    nc = t_len // CHUNK
    tb = _time_block(t_len)
    ncb = tb // CHUNK
    consts = _hgrn_consts()
    operands = [consts[n] for n in ("tri", "lvl", "sgn")]

    def body(hq_ref, hf_ref, hi_ref, lb_ref, tri_ref, lvl_ref, sgn_ref, o_ref, ssave_ref, asave_ref, st_ref, b_scr):
        @pl.when(pl.program_id(1) == 0)
        def _():
            st_ref[...] = jnp.zeros_like(st_ref)

        def chunk(ci, carry):
            r = pl.ds(pl.multiple_of(ci * CHUNK, CHUNK), CHUNK)
            for hd in range(2):
                lbv = lb_ref[:, hd * LANE:(hd + 1) * LANE]
                a = _hgrn_chunk(hq_ref[hd, r, :], hf_ref[hd, r, :], hi_ref[hd, r, :], lbv, tri_ref, sgn_ref,
                                b_scr.at[hd])
                q_pairs, k_diags, _, _ = _level_operands(a)
                st = st_ref[hd]
                ssave_ref[hd, ci] = st
                scores = _hgrn_scores(a, lvl_ref, q_pairs, k_diags).astype(asave_ref.dtype)
                asave_ref[hd, ci] = scores
                o_ref[hd, r, :] = _dot(a["q"] * a["eb"], st, _NT) + _dot(scores, a["v"])
                st_ref[hd] = st * a["ebc"] + _dot(a["v"], a["k"] * a["erem"], _TN)
            return carry

        lax.fori_loop(0, ncb, chunk, 0, unroll=FWD_UNROLL)

    pair = lambda base: pl.BlockSpec((2, tb, LANE), lambda p, t, base=base: (base // 2 + p, t, 0))
    per_chunk = pl.BlockSpec((2, ncb, LANE, LANE), lambda p, t: (p, t, 0, 0))
    return pl.pallas_call(
        body, name="hgrn_fwd", grid=(N_HEADS // 2, t_len // tb),
        in_specs=[pair(CB_HQ), pair(CB_HF), pair(CB_HI),
                  pl.BlockSpec((1, 2 * LANE), lambda p, t: (0, p))] + [_resident(c) for c in operands],
        out_specs=[pl.BlockSpec((2, tb, LANE), lambda p, t: (p, t, 0)), per_chunk, per_chunk],
        out_shape=[jax.ShapeDtypeStruct((N_HEADS, t_len, LANE), F32),
                   jax.ShapeDtypeStruct((N_HEADS, nc, LANE, LANE), F32),
                   jax.ShapeDtypeStruct((N_HEADS, nc, CHUNK, CHUNK), _BF)],
        scratch_shapes=[pltpu.VMEM((2, LANE, LANE), F32), pltpu.VMEM((2, CHUNK, LANE), F32)],
        compiler_params=_params(("arbitrary", "arbitrary")),
    )(pb, pb, pb, lb, *operands)


def hgrn_backward(pb, lb, do, ssave, asave, dpb, t_len):
    tb = _time_block(t_len)
    ncb, ntb = tb // CHUNK, t_len // tb
    consts = _hgrn_consts()
    operands = [consts[n] for n in ("tri", "sgn", "sel_t", "lvl_b", "lvlt_b")]

    def body(hq_ref, hf_ref, hi_ref, lb_ref, do_ref, ssave_ref, asave_ref, tri_ref, sgn_ref, selt_ref, lvlb_ref,
             lvltb_ref, dpb_in, dpb_ref, dlb_ref, dq_buf, df_buf, di_buf, dst_ref, b_scr, sems):
        del dpb_in
        p, t = pl.program_id(0), pl.program_id(1)
        step = p * ntb + t
        rows = pl.ds(pl.multiple_of((ntb - 1 - t) * tb, tb), tb)

        def out_copies(sl):
            return [pltpu.make_async_copy(buf.at[sl], dpb_ref.at[pl.ds(base + 2 * p, 2), rows], sems.at[sl, n])
                    for n, (buf, base) in enumerate(((dq_buf, CB_HQ), (df_buf, CB_HF), (di_buf, CB_HI)))]

        slot = _writeback_reserve(step, out_copies)

        @pl.when(t == 0)
        def _():
            dst_ref[...] = jnp.zeros_like(dst_ref)
            dlb_ref[...] = jnp.zeros_like(dlb_ref)

        def chunk(i, carry):
            ci = ncb - 1 - i
            r = pl.ds(pl.multiple_of(ci * CHUNK, CHUNK), CHUNK)
            for hd in range(2):
                head_chunk(hd, ci, r)
            return carry

        def head_chunk(hd, ci, r):
            lbv = lb_ref[:, hd * LANE:(hd + 1) * LANE]
            hq = hq_ref[hd, r, :]
            a = _hgrn_chunk(hq, hf_ref[hd, r, :], hi_ref[hd, r, :], lbv, tri_ref, sgn_ref, b_scr.at[hd])
            _, k_diags, ql, kl = _level_operands(a)
            q, k, v = a["q"], a["k"], a["v"]
            g = do_ref[hd, r, :]
            st0 = ssave_ref[hd, ci]
            dst = dst_ref[hd]
            scores = asave_ref[hd, ci]
            da = _dot(g, v, _NT)
            da_t = _dot(v, g, _NT)
            kb = k * a["erem"]
            qb = q * a["eb"]
            dv = _dot(scores, g, _TN) + _dot(kb, dst, _NT)
            dq_inter = _dot(g, st0) * a["eb"]
            dk_state = _dot(v, dst) * a["erem"]
            dq, dk = dq_inter, dk_state
            de = [q * dq_inter]
            da_b, dat_b = da.astype(_BF), da_t.astype(_BF)
            for n in range(len(k_diags)):
                l0, l1 = 2 * n, 2 * n + 1
                da_pair = jnp.concatenate([lvlb_ref[l0] * da_b, lvlb_ref[l1] * da_b], axis=1)
                dat_pair = jnp.concatenate([lvltb_ref[l0] * dat_b, lvltb_ref[l1] * dat_b], axis=1)
                dq_both = lax.dot_general(da_pair, k_diags[n], _NN, preferred_element_type=F32)
                dk_both = lax.dot_general(dat_pair, _blockdiag(ql[l0], ql[l1]), _NN, preferred_element_type=F32)
                for l, cols in ((l0, slice(0, LANE)), (l1, slice(LANE, 2 * LANE))):
                    dql, dkl = dq_both[:, cols], dk_both[:, cols]
                    if l > 0:
                        e = a["lev"][l]
                        dql, dkl = dql * e, dkl * e
                        de.append(q * dql + k * dkl)
                    dq = dq + dql
                    dk = dk + dkl
            de.append(k * dk_state)
            dst_ref[hd] = dst * a["ebc"] + _dot(g, qb, _TN)
            dbc = jnp.sum(dst * st0, axis=0, keepdims=True) * a["ebc"]
            de2 = lax.dot_general(selt_ref[...], _split2(jnp.concatenate(de, axis=0)), _NN,
                                  preferred_element_type=F32)
            dlogf = de2[:, :LANE] + de2[:, LANE:] + dbc
            sq, sg = a["sq"], a["sg"]
            df = dlogf / a["f"] - dk
            dq_buf[slot, hd, r, :] = (dq * (sq * (1.0 + hq * (1.0 - sq)))).astype(dq_buf.dtype)
            df_buf[slot, hd, r, :] = (df * a["omlb"] * sg * (1.0 - sg)).astype(df_buf.dtype)
            di_buf[slot, hd, r, :] = dv.astype(di_buf.dtype)
            cols = slice(hd * LANE, (hd + 1) * LANE)
            dlb_ref[:, cols] = dlb_ref[:, cols] + jnp.sum(df * (1.0 - sg), axis=0, keepdims=True)

        lax.fori_loop(0, ncb, chunk, 0, unroll=SCAN_UNROLL)
        _writeback_commit(step, (N_HEADS // 2) * ntb, slot, out_copies)

    pair = lambda base: pl.BlockSpec((2, tb, LANE), lambda p, t, base=base: (base // 2 + p, ntb - 1 - t, 0))
    any_spec = pl.BlockSpec(memory_space=pl.ANY)
    per_chunk = pl.BlockSpec((2, ncb, LANE, LANE), lambda p, t: (p, ntb - 1 - t, 0, 0))
    return pl.pallas_call(
        body, name="hgrn_bwd", grid=(N_HEADS // 2, ntb),
        in_specs=[pair(CB_HQ), pair(CB_HF), pair(CB_HI),
                  pl.BlockSpec((1, 2 * LANE), lambda p, t: (0, p)),
                  pair(0), per_chunk, per_chunk]
        + [_resident(c) for c in operands] + [any_spec],
        out_specs=[any_spec, pl.BlockSpec((1, 2 * LANE), lambda p, t: (0, p))],
        out_shape=[jax.ShapeDtypeStruct(dpb.shape, dpb.dtype), jax.ShapeDtypeStruct((1, D_MODEL), F32)],
        scratch_shapes=[pltpu.VMEM((2, 2, tb, LANE), dpb.dtype)] * 3 + [
            pltpu.VMEM((2, LANE, LANE), F32), pltpu.VMEM((2, CHUNK, LANE), F32), pltpu.SemaphoreType.DMA((2, 3))],
        input_output_aliases={7 + len(operands): 0},
        compiler_params=_params(("arbitrary", "arbitrary")),
    )(pb, pb, pb, lb, do, ssave, asave, *operands, dpb)


def _rope_tables(t_len):
    half = RET_DK // 2
    inv_freq = (1.0 / (np.float32(ROPE_BASE) ** np.linspace(0.0, 1.0, half, dtype=np.float32))).astype(np.float32)
    ang = (np.arange(t_len, dtype=np.float32)[:, None] * inv_freq[None, :]).astype(np.float64)
    cos, sin = np.cos(ang).astype(np.float32), np.sin(ang).astype(np.float32)
    cos_t = np.concatenate([cos, cos, cos, cos], axis=1)
    sin_t = np.concatenate([-sin, sin, -sin, sin], axis=1)
    return jnp.asarray(cos_t), jnp.asarray(sin_t)


def _swap_halves(v):
    half = RET_DK // 2
    lane = lax.broadcasted_iota(jnp.int32, v.shape, 1)
    first = (lane & (RET_DK - 1)) < half
    return jnp.where(first, pltpu.roll(v, LANE - half, 1), pltpu.roll(v, half, 1))


def _ret_head_consts(hidx):
    c = CHUNK
    hf = jnp.full((1, LANE), hidx, jnp.int32).astype(F32)
    lg = jnp.log(1.0 - jnp.exp(-(5.0 + hf) * np.float32(np.log(2.0))))
    row = lax.broadcasted_iota(jnp.int32, (c, c), 0)
    col = lax.broadcasted_iota(jnp.int32, (c, c), 1)
    rel = (row - col).astype(F32)
    dm = jnp.where(rel >= 0, jnp.exp(lg[:, :1] * jnp.maximum(rel, 0.0)), 0.0)
    dm_t = jnp.where(rel <= 0, jnp.exp(lg[:, :1] * jnp.maximum(-rel, 0.0)), 0.0)
    idx = lax.broadcasted_iota(jnp.int32, (c, LANE), 0).astype(F32)
    zeta = jnp.exp(lg * (c - 1.0 - idx))
    xi = jnp.exp(lg * (idx + 1.0))
    cdec = jnp.exp(lg * float(c))
    return dm, zeta, xi, cdec, dm_t


def _lane_mask(which):
    lane = lax.broadcasted_iota(jnp.int32, (1, LANE), 1)
    return ((lane // RET_DK) == which).astype(F32)


def retention_forward(pb, cos_t, sin_t, t_len):
    nc = t_len // CHUNK

    tb = _time_block(t_len)
    ncb = tb // CHUNK

    def body(rq_ref, rk_ref, rv_ref, cos_ref, sin_ref, o_ref, rsave_ref, st_ref):
        p = pl.program_id(0)

        @pl.when(pl.program_id(1) == 0)
        def _():
            st_ref[...] = jnp.zeros_like(st_ref)

        consts = [_ret_head_consts(2 * p + hd) for hd in range(2)]

        def chunk(ci, carry):
            r = pl.ds(pl.multiple_of(ci * CHUNK, CHUNK), CHUNK)
            cs, sn = cos_ref[r, :], sin_ref[r, :]
            q = rq_ref[r, :]
            k = rk_ref[r, :]
            q = q * cs + _swap_halves(q) * sn
            k = (k * cs + _swap_halves(k) * sn) * RET_DK ** -0.5
            for hd in range(2):
                dm, zeta, xi, cdec, _ = consts[hd]
                lm = _lane_mask(hd)
                qh, kh = q * lm, k * lm
                v = rv_ref[hd, r, :]
                st = st_ref[hd]
                rsave_ref[hd, ci] = st
                scores = _dot(qh, kh, _NT) * dm
                o_ref[hd, r, :] = _dot(scores, v) + _dot(qh * xi, st, _NT)
                st_ref[hd] = st * cdec + _dot(v, kh * zeta, _TN)
            return carry

        lax.fori_loop(0, ncb, chunk, 0, unroll=RET_UNROLL)

    return pl.pallas_call(
        body, name="ret_fwd", grid=(N_HEADS // 2, t_len // tb),
        in_specs=[pl.BlockSpec((None, tb, LANE), lambda p, t: (CB_RQ + p, t, 0)),
                  pl.BlockSpec((None, tb, LANE), lambda p, t: (CB_RK + p, t, 0)),
                  pl.BlockSpec((2, tb, LANE), lambda p, t: (CB_RV // 2 + p, t, 0)),
                  pl.BlockSpec((tb, LANE), lambda p, t: (t, 0)),
                  pl.BlockSpec((tb, LANE), lambda p, t: (t, 0))],
        out_specs=[pl.BlockSpec((2, tb, LANE), lambda p, t: (p, t, 0)),
                   pl.BlockSpec((2, ncb, LANE, LANE), lambda p, t: (p, t, 0, 0))],
        out_shape=[jax.ShapeDtypeStruct((N_HEADS, t_len, LANE), F32),
                   jax.ShapeDtypeStruct((N_HEADS, nc, LANE, LANE), F32)],
        scratch_shapes=[pltpu.VMEM((2, LANE, LANE), F32)],
        compiler_params=_params(("arbitrary", "arbitrary")),
    )(pb, pb, pb, cos_t, sin_t)


def retention_backward(pb, cos_t, sin_t, do, rsave, dpb, t_len):
    tb = _time_block(t_len)
    ncb, ntb = tb // CHUNK, t_len // tb

    def body(rq_ref, rk_ref, rv_ref, cos_ref, sin_ref, do_ref, rsave_ref, dpb_in,
             dpb_ref, dq_buf, dk_buf, dv_buf, dst_ref, sems):
        del dpb_in
        p, t = pl.program_id(0), pl.program_id(1)
        step = p * ntb + t
        rows = pl.ds(pl.multiple_of((ntb - 1 - t) * tb, tb), tb)

        def out_copies(sl):
            return [pltpu.make_async_copy(dq_buf.at[sl], dpb_ref.at[CB_RQ + p, rows], sems.at[sl, 0]),
                    pltpu.make_async_copy(dk_buf.at[sl], dpb_ref.at[CB_RK + p, rows], sems.at[sl, 1]),
                    pltpu.make_async_copy(dv_buf.at[sl], dpb_ref.at[pl.ds(CB_RV + 2 * p, 2), rows], sems.at[sl, 2])]

        slot = _writeback_reserve(step, out_copies)

        @pl.when(t == 0)
        def _():
            dst_ref[...] = jnp.zeros_like(dst_ref)

        consts = [_ret_head_consts(2 * p + hd) for hd in range(2)]

        def chunk(i, carry):
            ci = ncb - 1 - i
            r = pl.ds(pl.multiple_of(ci * CHUNK, CHUNK), CHUNK)
            cs, sn = cos_ref[r, :], sin_ref[r, :]
            q = rq_ref[r, :]
            k = rk_ref[r, :]
            q = q * cs + _swap_halves(q) * sn
            k = (k * cs + _swap_halves(k) * sn) * RET_DK ** -0.5
            dq, dk = None, None
            for hd in range(2):
                dm, zeta, xi, cdec, dm_t = consts[hd]
                lm = _lane_mask(hd)
                qh, kh = q * lm, k * lm
                v = rv_ref[hd, r, :]
                g = do_ref[hd, r, :]
                st0 = rsave_ref[hd, ci]
                dst = dst_ref[hd]
                scores_t = _dot(kh, qh, _NT) * dm_t
                dsc = _dot(g, v, _NT) * dm
                dsc_t = _dot(v, g, _NT) * dm_t
                dqh = _dot(dsc, kh) + _dot(g, st0) * xi
                dkh = _dot(dsc_t, qh) + _dot(v, dst) * zeta
                dv_buf[slot, hd, r, :] = (_dot(scores_t, g) + _dot(kh * zeta, dst, _NT)).astype(dv_buf.dtype)
                dst_ref[hd] = dst * cdec + _dot(g, qh * xi, _TN)
                dq = dqh if dq is None else dq + dqh
                dk = dkh if dk is None else dk + dkh
            dk = dk * (RET_DK ** -0.5)
            dq_buf[slot, r, :] = (dq * cs - _swap_halves(dq) * sn).astype(dq_buf.dtype)
            dk_buf[slot, r, :] = (dk * cs - _swap_halves(dk) * sn).astype(dk_buf.dtype)
            return carry

        lax.fori_loop(0, ncb, chunk, 0, unroll=RET_UNROLL)
        _writeback_commit(step, (N_HEADS // 2) * ntb, slot, out_copies)

    any_spec = pl.BlockSpec(memory_space=pl.ANY)
    return pl.pallas_call(
        body, name="ret_bwd", grid=(N_HEADS // 2, ntb),
        in_specs=[pl.BlockSpec((None, tb, LANE), lambda p, t: (CB_RQ + p, ntb - 1 - t, 0)),
                  pl.BlockSpec((None, tb, LANE), lambda p, t: (CB_RK + p, ntb - 1 - t, 0)),
                  pl.BlockSpec((2, tb, LANE), lambda p, t: (CB_RV // 2 + p, ntb - 1 - t, 0)),
                  pl.BlockSpec((tb, LANE), lambda p, t: (ntb - 1 - t, 0)),
                  pl.BlockSpec((tb, LANE), lambda p, t: (ntb - 1 - t, 0)),
                  pl.BlockSpec((2, tb, LANE), lambda p, t: (p, ntb - 1 - t, 0)),
                  pl.BlockSpec((2, ncb, LANE, LANE), lambda p, t: (p, ntb - 1 - t, 0, 0)),
                  any_spec],
        out_specs=any_spec,
        out_shape=jax.ShapeDtypeStruct(dpb.shape, dpb.dtype),
        scratch_shapes=[pltpu.VMEM((2, tb, LANE), dpb.dtype), pltpu.VMEM((2, tb, LANE), dpb.dtype),
                        pltpu.VMEM((2, 2, tb, LANE), dpb.dtype), pltpu.VMEM((2, LANE, LANE), F32),
                        pltpu.SemaphoreType.DMA((2, 3))],
        input_output_aliases={7: 0},
        compiler_params=_params(("arbitrary", "arbitrary")),
    )(pb, pb, pb, cos_t, sin_t, do, rsave, dpb)


def _row_tile(t_len, want):
    return min(want, t_len)


PAIR_CB = 2 * CB_PER_SHARD


def proj_forward(h, wt, chips, t_len, name, pb=None):
    tm = _row_tile(t_len, 1024)

    def body(chips_ref, h_ref, w_ref, *rest):
        del chips_ref
        o_ref = rest[-1]
        acc = _dot(h_ref[...], w_ref[...], _NT)
        for jj in range(PAIR_CB):
            o_ref[jj] = acc[:, jj * LANE:(jj + 1) * LANE]

    given = [] if pb is None else [pb]
    return pl.pallas_call(
        body, name=name,
        grid_spec=pltpu.PrefetchScalarGridSpec(
            num_scalar_prefetch=1, grid=(chips.shape[0], t_len // tm),
            in_specs=[pl.BlockSpec((tm, D_MODEL), lambda j, i, ch: (i, 0)),
                      pl.BlockSpec((PAIR_CB * LANE, D_MODEL), lambda j, i, ch: (ch[j], 0))]
            + [pl.BlockSpec(memory_space=pl.ANY)] * len(given),
            out_specs=pl.BlockSpec((PAIR_CB, tm, LANE), lambda j, i, ch: (ch[j], i, 0))),
        out_shape=jax.ShapeDtypeStruct((N_CB, t_len, LANE), F32),
        input_output_aliases={3: 0} if given else {},
        compiler_params=_params(("arbitrary", "arbitrary")),
    )(chips, h, wt, *given)


def proj_backward_input(dpb, wt, token, x, dy, norm_g, scale1p, t_len):
    tm = _row_tile(t_len, 512)

    def body(a_ref, wt_hbm, token_ref, x_ref, dy_ref, g_ref, sc_ref, gx_ref, vec_ref, w_ref, sem):
        del token_ref
        i = pl.program_id(0)

        @pl.when(i == 0)
        def _():
            cp = pltpu.make_async_copy(wt_hbm, w_ref, sem)
            cp.start()
            cp.wait()

        a = jnp.concatenate([a_ref[jj].astype(_BF) for jj in range(N_CB)], axis=1)
        dhv = _dot(a, w_ref[...])
        xv, g, sc = x_ref[...], g_ref[...], sc_ref[...]
        r = lax.rsqrt(jnp.mean(xv * xv, axis=-1, keepdims=True) + EPS)
        xn = xv * r
        dxn = dhv * (g * sc)
        gx_ref[...] = dy_ref[...] + r * dxn - xn * (r * r) * jnp.mean(xv * dxn, axis=-1, keepdims=True)
        t = dhv * xn
        _acc_rows(vec_ref, i, [jnp.sum(t * sc, axis=0, keepdims=True),
                               jnp.sum(t * g, axis=0, keepdims=True),
                               jnp.sum(dhv, axis=0, keepdims=True)])

    row = pl.BlockSpec((tm, D_MODEL), lambda i: (i, 0))
    return pl.pallas_call(
        body, name="proj_bwd_input", grid=(t_len // tm,),
        in_specs=[pl.BlockSpec((N_CB, tm, LANE), lambda i: (0, i, 0)),
                  pl.BlockSpec(memory_space=pl.ANY),
                  pl.BlockSpec(token.shape, lambda i: (0, 0)),
                  row, row, _vec_spec(), _vec_spec()],
        out_specs=[row, pl.BlockSpec((8, D_MODEL), lambda i: (0, 0))],
        out_shape=[jax.ShapeDtypeStruct((t_len, D_MODEL), F32), jax.ShapeDtypeStruct((8, D_MODEL), F32)],
        scratch_shapes=[pltpu.VMEM(wt.shape, wt.dtype), pltpu.SemaphoreType.DMA],
        compiler_params=_params(("arbitrary",)),
    )(dpb, wt, token, x, dy, norm_g, scale1p)


def proj_backward_weight(h_t, dpb, t_len):
    tk = _row_tile(t_len, 2048)

    def body(h_ref, b_ref, o_ref):
        k = pl.program_id(1)
        b = jnp.concatenate([b_ref[jj].astype(_BF) for jj in range(PAIR_CB)], axis=1)
        part = _dot(h_ref[...], b)

        @pl.when(k == 0)
        def _():
            for s in range(2):
                o_ref[s] = part[:, s * SHARD_IN:(s + 1) * SHARD_IN]

        @pl.when(k > 0)
        def _():
            for s in range(2):
                o_ref[s] = o_ref[s] + part[:, s * SHARD_IN:(s + 1) * SHARD_IN]

    return pl.pallas_call(
        body, name="proj_bwd_weight", grid=(N_DEV // 2, t_len // tk),
        in_specs=[pl.BlockSpec((D_MODEL, tk), lambda j, k: (0, k)),
                  pl.BlockSpec((PAIR_CB, tk, LANE), lambda j, k: (j, k, 0))],
        out_specs=pl.BlockSpec((2, D_MODEL, SHARD_IN), lambda j, k: (j, 0, 0)),
        out_shape=jax.ShapeDtypeStruct((N_DEV, D_MODEL, SHARD_IN), F32),
        compiler_params=_params(("arbitrary", "arbitrary")),
    )(h_t, dpb)


def sibling_blocks(g_in, c_idx):
    tr = D_MODEL

    def body(c_ref, g_ref, o_ref):
        del c_ref
        o_ref[...] = g_ref[...].astype(o_ref.dtype)

    return pl.pallas_call(
        body, name="sibling_blocks",
        grid_spec=pltpu.PrefetchScalarGridSpec(
            num_scalar_prefetch=1, grid=(N_DEV // 2, D_MODEL // tr),
            in_specs=[pl.BlockSpec((None, tr, SHARD_IN), lambda q, i, c: (2 * q + 1 - c[0], i, 0))],
            out_specs=pl.BlockSpec((None, tr, SHARD_IN), lambda q, i, c: (q, i, 0))),
        out_shape=jax.ShapeDtypeStruct((N_DEV // 2, D_MODEL, SHARD_IN), _BF),
        compiler_params=_params(("arbitrary", "arbitrary")),
    )(c_idx, g_in)


def _vec_spec():
    return pl.BlockSpec((1, D_MODEL), lambda i: (0, 0))


def _acc_rows(ref, i, rows):
    @pl.when(i == 0)
    def _():
        ref[...] = jnp.zeros_like(ref)

    for n, row in enumerate(rows):
        ref[n:n + 1, :] = ref[n:n + 1, :] + row


def adaln_forward(x, norm_g, scale1p, shift, t_len):
    tm = _row_tile(t_len, 1024)

    def body(x_ref, g_ref, sc_ref, sh_ref, h_ref, ht_ref):
        xv = x_ref[...]
        r = lax.rsqrt(jnp.mean(xv * xv, axis=-1, keepdims=True) + EPS)
        h = xv * r * g_ref[...] * sc_ref[...] + sh_ref[...]
        h_ref[...] = h.astype(h_ref.dtype)
        ht_ref[...] = h.T.astype(ht_ref.dtype)

    return pl.pallas_call(
        body, name="adaln_fwd", grid=(t_len // tm,),
        in_specs=[pl.BlockSpec((tm, D_MODEL), lambda i: (i, 0)), _vec_spec(), _vec_spec(), _vec_spec()],
        out_specs=[pl.BlockSpec((tm, D_MODEL), lambda i: (i, 0)), pl.BlockSpec((D_MODEL, tm), lambda i: (0, i))],
        out_shape=[jax.ShapeDtypeStruct((t_len, D_MODEL), _BF), jax.ShapeDtypeStruct((D_MODEL, t_len), _BF)],
        compiler_params=_params(("arbitrary",)),
    )(x, norm_g, scale1p, shift)


def _head_norm(o, g):
    r = lax.rsqrt(jnp.mean(o * o, axis=-1, keepdims=True) + EPS)
    return r, o * r * g


def _group_spec(tm, cb):
    return pl.BlockSpec((N_HEADS, tm, LANE), lambda i, cb=cb: (cb // N_HEADS, i, 0))


MID_FINAL_G, MID_GATE, MID_LOSS, MID_HG_G, MID_RET_G = range(5)


def middle(x, target, oa, ob, pb, wout, gate, final_g, hg_g, ret_g, t_len):
    tm = _row_tile(t_len, 256)
    n_steps = t_len // tm

    def body(x_ref, t_ref, oa_ref, ob_ref, hz_ref, rz_ref, ga_ref, gb_ref, w_ref, gate_ref, fg_ref, hg_ref, rg_ref,
             dy_ref, doa_ref, dob_ref, dw_ref, vec_ref, dpb_ref, m_scr, dm_scr, keep, bufs, sems):
        i = pl.program_id(0)
        rows = pl.ds(pl.multiple_of(i * tm, tm), tm)

        def group_copies(sl):
            return [pltpu.make_async_copy(bufs.at[sl, n], dpb_ref.at[pl.ds(cb, N_HEADS), rows], sems.at[sl, n])
                    for n, cb in enumerate((CB_HZ, CB_RZ, CB_GA, CB_GB))]
        sides = ((oa_ref, hz_ref, ga_ref, hg_ref, doa_ref), (ob_ref, rz_ref, gb_ref, rg_ref, dob_ref))
        for hh in range(N_HEADS):
            ls = slice(hh * LANE, (hh + 1) * LANE)
            acc = None
            for side, (o_ref, z_ref, gt_ref, g_ref, _) in enumerate(sides):
                o = o_ref[hh]
                rr = lax.rsqrt(jnp.mean(o * o, axis=-1, keepdims=True) + EPS)
                orr = o * rr
                zz = z_ref[hh]
                sz = _sigmoid(zz)
                sgt = _sigmoid(gt_ref[hh])
                keep[side, hh, 0] = orr
                keep[side, hh, 1] = sz
                keep[side, hh, 2] = sgt
                keep[side, hh, 3] = jnp.broadcast_to(rr, orr.shape)
                u = sgt * ((orr * g_ref[:, ls]) * (zz * sz))
                acc = u if acc is None else acc + u
            m_scr[:, ls] = acc.astype(m_scr.dtype)
        zv = _dot(m_scr[...], w_ref[...])
        gt, fg = gate_ref[...], fg_ref[...]
        y = x_ref[...] + gt * zv
        r = lax.rsqrt(jnp.mean(y * y, axis=-1, keepdims=True) + EPS)
        yn = y * r
        err = yn * fg - t_ref[...]
        loss = 0.5 * jnp.sum(jnp.mean(err * err, axis=-1, keepdims=True), axis=0, keepdims=True)
        dout = err * (1.0 / D_MODEL)
        gd = dout * fg
        dy = r * gd - yn * (r * r) * jnp.mean(y * gd, axis=-1, keepdims=True)
        dy_ref[...] = dy
        dz = (dy * gt).astype(_BF)
        dm_scr[...] = _dot(dz, w_ref[...], _NT)
        part = _dot(m_scr[...], dz, _TN)

        @pl.when(i == 0)
        def _():
            dw_ref[...] = part

        @pl.when(i > 0)
        def _():
            dw_ref[...] = dw_ref[...] + part

        slot = _writeback_reserve(i, group_copies)
        dg = [[], []]
        for hh in range(N_HEADS):
            ls = slice(hh * LANE, (hh + 1) * LANE)
            dmh = dm_scr[:, ls]
            for side, (o_ref, z_ref, gt_ref, g_ref, do_ref) in enumerate(sides):
                zz, g = z_ref[hh], g_ref[:, ls]
                orr, sz, sgt, rr = keep[side, hh, 0], keep[side, hh, 1], keep[side, hh, 2], keep[side, hh, 3]
                n = orr * g
                silu = zz * sz
                du = dmh * sgt
                bufs[slot, 2 + side, hh] = (dmh * (n * silu) * (sgt * (1.0 - sgt))).astype(bufs.dtype)
                bufs[slot, side, hh] = (du * n * (sz * (1.0 + zz * (1.0 - sz)))).astype(bufs.dtype)
                dn = du * silu
                dg[side].append(jnp.sum(dn * orr, axis=0, keepdims=True))
                gdn = dn * g
                do_ref[hh] = (rr * (gdn - orr * jnp.mean(orr * gdn, axis=-1, keepdims=True))).astype(do_ref.dtype)
        _acc_rows(vec_ref, i, [jnp.sum(dout * yn, axis=0, keepdims=True),
                               jnp.sum(dy * zv, axis=0, keepdims=True),
                               jnp.broadcast_to(loss, (1, D_MODEL)),
                               jnp.concatenate(dg[0], axis=1), jnp.concatenate(dg[1], axis=1)])
        _writeback_commit(i, n_steps, slot, group_copies)

    row = pl.BlockSpec((tm, D_MODEL), lambda i: (i, 0))
    head = pl.BlockSpec((N_HEADS, tm, LANE), lambda i: (0, i, 0))
    full = pl.BlockSpec((D_MODEL, D_MODEL), lambda i: (0, 0))
    return pl.pallas_call(
        body, name="middle", grid=(n_steps,),
        in_specs=[row, row, head, head, _group_spec(tm, CB_HZ), _group_spec(tm, CB_RZ), _group_spec(tm, CB_GA),
                  _group_spec(tm, CB_GB), full, _vec_spec(), _vec_spec(), _vec_spec(), _vec_spec()],
        out_specs=[row, head, head, full, pl.BlockSpec((8, D_MODEL), lambda i: (0, 0)),
                   pl.BlockSpec(memory_space=pl.ANY)],
        out_shape=[jax.ShapeDtypeStruct((t_len, D_MODEL), F32),
                   jax.ShapeDtypeStruct((N_HEADS, t_len, LANE), _BF),
                   jax.ShapeDtypeStruct((N_HEADS, t_len, LANE), _BF),
                   jax.ShapeDtypeStruct((D_MODEL, D_MODEL), F32),
                   jax.ShapeDtypeStruct((8, D_MODEL), F32),
                   jax.ShapeDtypeStruct((N_CB, t_len, LANE), _BF)],
        scratch_shapes=[pltpu.VMEM((tm, D_MODEL), _BF), pltpu.VMEM((tm, D_MODEL), F32),
                        pltpu.VMEM((2, N_HEADS, 4, tm, LANE), F32),
                        pltpu.VMEM((2, 4, N_HEADS, tm, LANE), _BF), pltpu.SemaphoreType.DMA((2, 4))],
        compiler_params=_params(("arbitrary",)),
    )(x, target, oa, ob, pb, pb, pb, pb, wout, gate, final_g, hg_g, ret_g)


def device_step(x, target, mod, lb, project, norm_g, hg_g, ret_g, final_g, c_idx=None, start_exchange=None):
    t_len = x.shape[0]
    shift, scale, gate = mod[:, :D_MODEL], mod[:, D_MODEL:2 * D_MODEL], mod[:, 2 * D_MODEL:]
    scale1p = 1.0 + scale
    cos_t, sin_t = _rope_tables(t_len)
    h, h_t = adaln_forward(x, norm_g, scale1p, shift, t_len)
    pb, wt, wout = project(h)
    oa, ssave, asave = hgrn_forward(pb, lb, t_len)
    ob, rsave = retention_forward(pb, cos_t, sin_t, t_len)
    dy, doa, dob, dwout, vec_mid, dpb = middle(x, target, oa, ob, pb, wout, gate, final_g, hg_g, ret_g, t_len)
    dpb, dlb = hgrn_backward(pb, lb, doa, ssave, asave, dpb, t_len)
    dpb = retention_backward(pb, cos_t, sin_t, dob, rsave, dpb, t_len)
    c_idx = jnp.zeros((1,), jnp.int32) if c_idx is None else c_idx
    dwin = proj_backward_weight(h_t, dpb, t_len)
    dwin_sib = sibling_blocks(dwin, c_idx)
    token, pending = (start_exchange(dwin, dwin_sib, dwout) if start_exchange
                      else (jnp.zeros((8, LANE), F32), None))
    grad_x, vec_ada = proj_backward_input(dpb, wt, token, x, dy, norm_g, scale1p, t_len)
    return grad_x, dwin, dwout, vec_mid, vec_ada, dlb, pending


PACK_ROWS = 16
ROW_NORM_G, ROW_LB, ROW_HG_G, ROW_RET_G, ROW_FINAL_G, ROW_SHIFT, ROW_SCALE, ROW_GATE, ROW_LOSS = range(9)


def _mesh_pos():
    return lax.axis_index("x"), lax.axis_index("y"), lax.axis_index("c")


def _lin(pos):
    return 4 * pos[0] + 2 * pos[1] + pos[2]


def _xor_peer(pos, k):
    return tuple(1 - p if (k >> s) & 1 else p for p, s in zip(pos, (2, 1, 0)))


def _other_chips(pos):
    x, y, _ = pos
    return [(1 - x, y), (x, 1 - y), (1 - x, 1 - y)]


def _remote(src, dst, send_sem, recv_sem, to):
    return pltpu.make_async_remote_copy(src_ref=src, dst_ref=dst, send_sem=send_sem, recv_sem=recv_sem,
                                        device_id=to, device_id_type=MESH)


def pre_exchange(c, w_ada, b_ada, logits, w_in, w_out):
    def body(c_ref, wada_ref, bada_ref, logit_ref, win_ref, wout_ref, mod_ref, scall_ref, lb_ref, wg_ref, wog_ref,
             cg_ref, modall_ref, parts_ref, wt_ref, wo_ref, send1, recv1, send2, recv2, local):
        pos = _mesh_pos()
        placed = [pltpu.make_async_copy(wt_ref, wg_ref.at[_lin(pos)], local.at[0]),
                  pltpu.make_async_copy(wo_ref, wog_ref.at[_lin(pos)], local.at[1])]
        cv = c_ref[...]
        slot = lambda p: pl.ds(pl.multiple_of(8 * _lin(p), 8), 8)
        cg_ref[slot(pos), :] = jnp.broadcast_to(cv * _sigmoid(cv), (8, D_MODEL))
        lb_ref[...] = _sigmoid(logit_ref[0:1, :] - logit_ref[1:2, :])
        peers = [_xor_peer(pos, k) for k in range(1, N_DEV)]
        gather = [_remote(cg_ref.at[slot(pos)], cg_ref.at[slot(pos)], send1.at[n], recv1.at[n], p)
                  for n, p in enumerate(peers)]
        for cp in gather:
            cp.start()
        wt_ref[...] = win_ref[...].T.astype(wt_ref.dtype)
        wo_ref[...] = wout_ref[...].astype(wo_ref.dtype)
        for cp in placed:
            cp.start()
        for n, p in enumerate(peers):
            _remote(cg_ref.at[slot(p)], cg_ref.at[slot(p)], send1.at[n], recv1.at[n], p).wait_recv()
        modall_ref[...] = _dot(cg_ref[...], wada_ref[...])
        scatter = [_remote(modall_ref.at[slot(p)], parts_ref.at[slot(pos)], send2.at[n], recv2.at[n], p)
                   for n, p in enumerate(peers)]
        for cp in scatter:
            cp.start()
        parts_ref[slot(pos), :] = modall_ref[slot(pos), :]
        for n, p in enumerate(peers):
            _remote(modall_ref.at[slot(p)], parts_ref.at[slot(p)], send2.at[n], recv2.at[n], p).wait_recv()
        for cp in gather + scatter:
            cp.wait_send()
        for j in range(N_DEV):
            cols = slice(j * SHARD_ADA, (j + 1) * SHARD_ADA)
            mod_ref[:, cols] = parts_ref[8 * j:8 * j + 1, :] + bada_ref[:, cols]
            scall_ref[j:j + 1, :] = cg_ref[8 * j:8 * j + 1, :]
        for cp in placed:
            cp.wait()

    vmem = pl.BlockSpec(memory_space=pltpu.VMEM)
    return pl.pallas_call(
        body, name="pre_exchange",
        in_specs=[vmem] * 6, out_specs=[vmem] * 3 + [_HBM] * 2,
        out_shape=[jax.ShapeDtypeStruct((1, 3 * D_MODEL), F32), jax.ShapeDtypeStruct((N_DEV, D_MODEL), F32),
                   jax.ShapeDtypeStruct((1, D_MODEL), F32),
                   jax.ShapeDtypeStruct((N_DEV,) + w_in.shape[::-1], _BF),
                   jax.ShapeDtypeStruct((N_DEV,) + w_out.shape, _BF)],
        scratch_shapes=[pltpu.VMEM((N_DEV * 8, D_MODEL), F32), pltpu.VMEM((N_DEV * 8, SHARD_ADA), F32),
                        pltpu.VMEM((N_DEV * 8, SHARD_ADA), F32),
                        pltpu.VMEM(w_in.shape[::-1], _BF), pltpu.VMEM(w_out.shape, _BF)]
        + [pltpu.SemaphoreType.DMA((N_DEV - 1,))] * 4 + [pltpu.SemaphoreType.DMA((2,))],
        compiler_params=pltpu.CompilerParams(vmem_limit_bytes=VMEM_LIMIT),
    )(c, w_ada, b_ada, logits, w_in, w_out)


def _gather_copies(outs, sems):
    pos = _mesh_pos()
    x, y, c = pos
    sibling = (x, y, 1 - c)

    def route(core):
        return [(x + (1 - core) * (1 - 2 * x), y + core * (1 - 2 * y)),
                (x + core * (1 - 2 * x), y + (1 - core) * (1 - 2 * y)),
                (1 - x, 1 - y)]

    mine = [(*chip, c) for chip in route(c)]
    carried = [pos, pos, pos, mine[0], mine[0], mine[1], mine[2]]
    to = [sibling, mine[0], mine[1], mine[1], sibling, sibling, sibling]
    landed = [sibling] + mine + [(*chip, 1 - c) for chip in route(1 - c)]

    def sent(a, k):
        dst = outs[a].at[_lin(carried[k])]
        return _remote(dst, dst, *sems(a, k), to[k])

    def arrival(a, k):
        dst = outs[a].at[_lin(landed[k])]
        return _remote(dst, dst, *sems(a, k), pos)

    return sent, arrival


_ROUNDS = (range(0, 3), range(3, 6), range(6, 7))


def _round_sems(send, recv, rnd):
    n = len(_ROUNDS[rnd])
    return lambda a, k: (send.at[n * a + k - _ROUNDS[rnd][0]], recv.at[n * a + k - _ROUNDS[rnd][0]])


def weight_gather_start(wg, woutg):
    def body(wg_ref, woutg_ref, send, recv, wg_out, woutg_out, token):
        del wg_out, woutg_out
        sent, _ = _gather_copies((wg_ref, woutg_ref), _round_sems(send, recv, 0))
        for a in range(2):
            for k in _ROUNDS[0]:
                sent(a, k).start()
        token[...] = jnp.zeros_like(token)

    hbm = lambda a: pltpu.with_memory_space_constraint(a, pltpu.HBM)
    n = 2 * len(_ROUNDS[0])
    return pl.pallas_call(
        body, name="weight_gather_start",
        in_specs=[_HBM] * 2,
        out_specs=[_SEM, _SEM, _HBM, _HBM, pl.BlockSpec(memory_space=pltpu.VMEM)],
        out_shape=[pltpu.SemaphoreType.DMA((n,)), pltpu.SemaphoreType.DMA((n,)),
                   pltpu.HBM(wg.shape, wg.dtype), pltpu.HBM(woutg.shape, woutg.dtype),
                   jax.ShapeDtypeStruct((8, LANE), F32)],
        input_output_aliases={0: 2, 1: 3},
        compiler_params=pltpu.CompilerParams(has_side_effects=pltpu.SideEffectType.DATAFLOW_SIDE_EFFECTING),
    )(hbm(wg), hbm(woutg))


def _wait_copies(sent, arrival, ks):
    for a in range(2):
        for k in ks:
            arrival(a, k).wait_recv()
            sent(a, k).wait_send()


def weight_gather_wait(name, rnd, ks, send, recv, wg, woutg, after):
    def body(wg_ref, woutg_ref, send, recv, after_ref, wg_out, woutg_out):
        del after_ref, wg_out, woutg_out
        _wait_copies(*_gather_copies((wg_ref, woutg_ref), _round_sems(send, recv, rnd)), ks)

    return pl.pallas_call(
        body, name=name,
        in_specs=[_HBM] * 2 + [_SEM, _SEM, pl.BlockSpec(memory_space=pl.ANY)],
        out_specs=[_HBM] * 2,
        out_shape=[pltpu.HBM(wg.shape, wg.dtype), pltpu.HBM(woutg.shape, woutg.dtype)],
        input_output_aliases={0: 0, 1: 1},
        compiler_params=pltpu.CompilerParams(has_side_effects=pltpu.SideEffectType.DATAFLOW_SIDE_EFFECTING),
    )(wg, woutg, send, recv, after)


def weight_pass_start(wg, woutg):
    def body(wg_ref, woutg_ref, send, recv, wg_out, woutg_out):
        del wg_out, woutg_out
        sent, _ = _gather_copies((wg_ref, woutg_ref), _round_sems(send, recv, 1))
        for a in range(2):
            for k in _ROUNDS[1]:
                sent(a, k).start()

    n = 2 * len(_ROUNDS[1])
    return pl.pallas_call(
        body, name="weight_pass_start",
        in_specs=[_HBM] * 2,
        out_specs=[_SEM, _SEM, _HBM, _HBM],
        out_shape=[pltpu.SemaphoreType.DMA((n,)), pltpu.SemaphoreType.DMA((n,)),
                   pltpu.HBM(wg.shape, wg.dtype), pltpu.HBM(woutg.shape, woutg.dtype)],
        input_output_aliases={0: 2, 1: 3},
        compiler_params=pltpu.CompilerParams(has_side_effects=pltpu.SideEffectType.DATAFLOW_SIDE_EFFECTING),
    )(wg, woutg)


def weight_pass_last(wg, woutg):
    def body(wg_in, woutg_in, wg_ref, woutg_ref, send, recv):
        del wg_in, woutg_in
        sent, arrival = _gather_copies((wg_ref, woutg_ref), _round_sems(send, recv, 2))
        for a in range(2):
            sent(a, _ROUNDS[2][0]).start()
        _wait_copies(sent, arrival, _ROUNDS[2])

    any_spec = pl.BlockSpec(memory_space=pl.ANY)
    n = 2 * len(_ROUNDS[2])
    return pl.pallas_call(
        body, name="weight_pass_last",
        in_specs=[any_spec, any_spec], out_specs=[any_spec, any_spec],
        out_shape=[jax.ShapeDtypeStruct(wg.shape, wg.dtype), jax.ShapeDtypeStruct(woutg.shape, woutg.dtype)],
        scratch_shapes=[pltpu.SemaphoreType.DMA((n,)), pltpu.SemaphoreType.DMA((n,))],
        input_output_aliases={0: 0, 1: 1},
    )(wg, woutg)


def grad_pair_exchange(g_sib, g_out):
    def body(gsib_ref, gout_ref, ra_ref, rb_ref, send, recv):
        pos = _mesh_pos()
        x, y, c = pos
        sibling = (x, y, 1 - c)
        copies = []
        for q in range(4):
            copies.append(_remote(gsib_ref.at[q], ra_ref.at[q], send.at[q], recv.at[q], sibling))
            copies.append(_remote(gout_ref.at[2 * q + (1 - c)], rb_ref.at[q], send.at[4 + q], recv.at[4 + q], sibling))
        for cp in copies:
            cp.start()
        for cp in copies:
            cp.wait_recv()
        for cp in copies:
            cp.wait_send()

    any_spec = pl.BlockSpec(memory_space=pl.ANY)
    return pl.pallas_call(
        body, name="grad_pair_exchange",
        in_specs=[any_spec, any_spec], out_specs=[any_spec, any_spec],
        out_shape=[jax.ShapeDtypeStruct(g_sib.shape, g_sib.dtype), jax.ShapeDtypeStruct((4,) + g_out.shape[1:], F32)],
        scratch_shapes=[pltpu.SemaphoreType.DMA((8,)), pltpu.SemaphoreType.DMA((8,))],
    )(g_sib, g_out)


def pair_sum(g_in, ra, g_out, rb, c_idx):
    tr = D_MODEL

    def body(c_ref, gin_ref, ra_ref, gout_ref, rb_ref, sb_ref, sbo_ref):
        del c_ref
        sb_ref[...] = (gin_ref[...] + ra_ref[...].astype(F32)).astype(sb_ref.dtype)
        sbo_ref[...] = gout_ref[...] + rb_ref[...]

    n_i = D_MODEL // tr
    return pl.pallas_call(
        body, name="pair_sum",
        grid_spec=pltpu.PrefetchScalarGridSpec(
            num_scalar_prefetch=1, grid=(4, n_i),
            in_specs=[pl.BlockSpec((None, tr, SHARD_IN), lambda q, i, c: (2 * q + c[0], i, 0)),
                      pl.BlockSpec((None, tr, SHARD_IN), lambda q, i, c: (q, i, 0)),
                      pl.BlockSpec((None, SHARD_OUT // n_i, D_MODEL), lambda q, i, c: (2 * q + c[0], i, 0)),
                      pl.BlockSpec((None, SHARD_OUT // n_i, D_MODEL), lambda q, i, c: (q, i, 0))],
            out_specs=[pl.BlockSpec((None, tr, SHARD_IN), lambda q, i, c: (q, i, 0)),
                       pl.BlockSpec((None, SHARD_OUT // n_i, D_MODEL), lambda q, i, c: (q, i, 0))]),
        out_shape=[jax.ShapeDtypeStruct(ra.shape, _BF), jax.ShapeDtypeStruct(rb.shape, F32)],
        compiler_params=_params(("arbitrary", "arbitrary")),
    )(c_idx, g_in, ra, g_out, rb)


_HBM = pl.BlockSpec(memory_space=pltpu.HBM)
_SEM = pl.BlockSpec(memory_space=pltpu.SEMAPHORE)
_N_CHIP_COPIES = 6


def _chip_copies(sb_ref, sbo_ref, rc_ref, rco_ref, send, recv):
    pos = _mesh_pos()
    copies = []
    for a, (src, dst) in enumerate(((sb_ref, rc_ref), (sbo_ref, rco_ref))):
        for j, chip in enumerate(_other_chips(pos)):
            copies.append(_remote(src.at[2 * chip[0] + chip[1]], dst.at[j], send.at[3 * a + j], recv.at[3 * a + j],
                                  (*chip, pos[2])))
    return copies


def grad_chip_start(sb, sbo):
    def body(sb_ref, sbo_ref, rc_ref, rco_ref, send, recv, sb_thru, sbo_thru, rc_thru, rco_thru, token):
        del sb_thru, sbo_thru, rc_thru, rco_thru
        for cp in _chip_copies(sb_ref, sbo_ref, rc_ref, rco_ref, send, recv):
            cp.start()
        token[...] = jnp.zeros_like(token)

    hbm = lambda a: pltpu.with_memory_space_constraint(a, pltpu.HBM)
    rc = lax.empty((3,) + sb.shape[1:], sb.dtype)
    rco = lax.empty((3,) + sbo.shape[1:], sbo.dtype)
    return pl.pallas_call(
        body, name="grad_chip_start",
        in_specs=[_HBM] * 4,
        out_specs=[_SEM, _SEM, _HBM, _HBM, _HBM, _HBM, pl.BlockSpec(memory_space=pltpu.VMEM)],
        out_shape=[pltpu.SemaphoreType.DMA((_N_CHIP_COPIES,)), pltpu.SemaphoreType.DMA((_N_CHIP_COPIES,)),
                   pltpu.HBM(sb.shape, sb.dtype), pltpu.HBM(sbo.shape, sbo.dtype),
                   pltpu.HBM(rc.shape, rc.dtype), pltpu.HBM(rco.shape, rco.dtype),
                   jax.ShapeDtypeStruct((8, LANE), F32)],
        input_output_aliases={0: 2, 1: 3, 2: 4, 3: 5},
        compiler_params=pltpu.CompilerParams(has_side_effects=pltpu.SideEffectType.DATAFLOW_SIDE_EFFECTING),
    )(hbm(sb), hbm(sbo), hbm(rc), hbm(rco))


def grad_chip_wait(send, recv, sb, sbo, rc, rco, after):
    def body(sb_ref, sbo_ref, rc_ref, rco_ref, send, recv, after_ref, sb_o, sbo_o, rc_o, rco_o):
        del after_ref, sb_o, sbo_o, rc_o, rco_o
        for cp in _chip_copies(sb_ref, sbo_ref, rc_ref, rco_ref, send, recv):
            cp.wait_send()
            cp.wait_recv()

    return pl.pallas_call(
        body, name="grad_chip_wait",
        in_specs=[_HBM] * 4 + [_SEM, _SEM, pl.BlockSpec(memory_space=pl.ANY)],
        out_specs=[_HBM] * 4,
        out_shape=[pltpu.HBM(sb.shape, sb.dtype), pltpu.HBM(sbo.shape, sbo.dtype),
                   pltpu.HBM(rc.shape, rc.dtype), pltpu.HBM(rco.shape, rco.dtype)],
        input_output_aliases={0: 0, 1: 1, 2: 2, 3: 3},
        compiler_params=pltpu.CompilerParams(has_side_effects=pltpu.SideEffectType.DATAFLOW_SIDE_EFFECTING),
    )(sb, sbo, rc, rco, send, recv, after)


def pack_gather(vec_mid, vec_ada, dlb):
    def body(mid_ref, ada_ref, dlb_ref, packs_ref, dmod_ref, psend, precv):
        pos = _mesh_pos()
        me = _lin(pos)
        mid = lambda r: mid_ref[r:r + 1, :]
        rows = [ada_ref[0:1, :], dlb_ref[...], mid(MID_HG_G), mid(MID_RET_G), mid(MID_FINAL_G),
                ada_ref[2:3, :], ada_ref[1:2, :], mid(MID_GATE), mid(MID_LOSS)]
        packs_ref[me] = jnp.zeros((PACK_ROWS, D_MODEL), F32)
        for n, row in enumerate(rows):
            packs_ref[me, n:n + 1, :] = row
        peers = [_xor_peer(pos, k) for k in range(1, N_DEV)]
        gather = [_remote(packs_ref.at[me], packs_ref.at[me], psend.at[n], precv.at[n], p) for n, p in enumerate(peers)]
        for cp in gather:
            cp.start()
        dmod_ref[...] = jnp.zeros_like(dmod_ref)
        for n, p in enumerate(peers):
            _remote(packs_ref.at[_lin(p)], packs_ref.at[_lin(p)], psend.at[n], precv.at[n], p).wait_recv()
        for cp in gather:
            cp.wait_send()
        for part, r in enumerate((ROW_SHIFT, ROW_SCALE, ROW_GATE)):
            for d in range(N_DEV):
                dmod_ref[d:d + 1, part * D_MODEL:(part + 1) * D_MODEL] = packs_ref[d, r:r + 1, :]

    vmem = pl.BlockSpec(memory_space=pltpu.VMEM)
    return pl.pallas_call(
        body, name="pack_gather", in_specs=[vmem] * 3, out_specs=[vmem] * 2,
        out_shape=[jax.ShapeDtypeStruct((N_DEV, PACK_ROWS, D_MODEL), F32),
                   jax.ShapeDtypeStruct((LANE, 3 * D_MODEL), F32)],
        scratch_shapes=[pltpu.SemaphoreType.DMA((N_DEV - 1,)), pltpu.SemaphoreType.DMA((N_DEV - 1,))],
    )(vec_mid, vec_ada, dlb)


def _adamw(w, g, m, v):
    m = ADAM_B1 * m + (1.0 - ADAM_B1) * g
    v = ADAM_B2 * v + (1.0 - ADAM_B2) * (g * g)
    m_hat = m / (1.0 - ADAM_B1 ** ADAM_STEP)
    v_hat = v / (1.0 - ADAM_B2 ** ADAM_STEP)
    delta = -ADAM_LR * (m_hat / (jnp.sqrt(v_hat) + ADAM_EPS) + ADAM_WD * w)
    return delta, m, v


def adam_shard(chip_idx, own, parts, w, m, v, name):
    rows, cols = w.shape
    tr = min(rows, 256)

    def body(chip_ref, p0, p1, p2, p3, w_ref, m_ref, v_ref, g_ref, d_ref, nm_ref, nv_ref):
        del chip_ref
        g = ((p0[...].astype(F32) + p1[...].astype(F32)) + p2[...].astype(F32)) + p3[...].astype(F32)
        g_ref[...] = g
        d_ref[...], nm_ref[...], nv_ref[...] = _adamw(w_ref[...], g, m_ref[...], v_ref[...])

    part = lambda q: pl.BlockSpec((None, tr, cols), lambda i, chip, q=q: (q, i, 0))
    tile = pl.BlockSpec((tr, cols), lambda i, chip: (i, 0))
    return pl.pallas_call(
        body, name=name,
        grid_spec=pltpu.PrefetchScalarGridSpec(
            num_scalar_prefetch=1, grid=(rows // tr,),
            in_specs=[pl.BlockSpec((None, tr, cols), lambda i, chip: (chip[0], i, 0)), part(0), part(1), part(2),
                      tile, tile, tile],
            out_specs=[tile] * 4),
        out_shape=[jax.ShapeDtypeStruct(w.shape, F32)] * 4,
        compiler_params=_params(("arbitrary",)),
    )(chip_idx, own, parts, parts, parts, w, m, v)


def adam_ada(sc_t, dmod_all, me_idx, w, m, v):
    def body(me_ref, sc_ref, dm_ref, w_ref, m_ref, v_ref, g_ref, d_ref, nm_ref, nv_ref):
        del me_ref
        g = _dot_f32(sc_ref[...], dm_ref[...])
        g_ref[...] = g
        d_ref[...], nm_ref[...], nv_ref[...] = _adamw(w_ref[...], g, m_ref[...], v_ref[...])

    full = pl.BlockSpec(w.shape, lambda i, me: (0, 0))
    return pl.pallas_call(
        body, name="adam_ada",
        grid_spec=pltpu.PrefetchScalarGridSpec(
            num_scalar_prefetch=1, grid=(1,),
            in_specs=[pl.BlockSpec(sc_t.shape, lambda i, me: (0, 0)),
                      pl.BlockSpec((LANE, SHARD_ADA), lambda i, me: (0, me[0])), full, full, full],
            out_specs=[full] * 4),
        out_shape=[jax.ShapeDtypeStruct(w.shape, F32)] * 4,
        compiler_params=_params(("arbitrary",)),
    )(me_idx, sc_t, dmod_all, w, m, v)


def adam_vectors(packs, lb, params, ms, vs):
    n = len(params)

    def body(*refs):
        packs_ref, lb_ref = refs[0], refs[1]
        w_refs, m_refs, v_refs = refs[2:2 + n], refs[2 + n:2 + 2 * n], refs[2 + 2 * n:2 + 3 * n]
        loss_ref = refs[2 + 3 * n]
        outs = refs[3 + 3 * n:3 + 7 * n]
        tot_ref = refs[3 + 7 * n]
        tot = packs_ref[0]
        for d in range(1, N_DEV):
            tot = tot + packs_ref[d]
        tot_ref[...] = tot
        row = lambda r: tot_ref[r:r + 1, :]
        lbv = lb_ref[...]
        dl0 = row(ROW_LB) * lbv * (1.0 - lbv)
        grads = [[row(ROW_NORM_G)],
                 [jnp.concatenate([row(ROW_SHIFT), row(ROW_SCALE), row(ROW_GATE)], axis=1)],
                 [dl0, -dl0],
                 [row(ROW_HG_G)], [row(ROW_RET_G)], [row(ROW_FINAL_G)]]
        loss_ref[...] = tot_ref[ROW_LOSS:ROW_LOSS + 1, 0:LANE]
        for j, g_rows in enumerate(grads):
            for r, g in enumerate(g_rows):
                rs = slice(r, r + 1)
                d, nm, nv = _adamw(w_refs[j][rs, :], g, m_refs[j][rs, :], v_refs[j][rs, :])
                outs[4 * j][rs, :] = g
                outs[4 * j + 1][rs, :] = d
                outs[4 * j + 2][rs, :] = nm
                outs[4 * j + 3][rs, :] = nv

    vmem = pl.BlockSpec(memory_space=pltpu.VMEM)
    out_shape = [jax.ShapeDtypeStruct((1, LANE), F32)]
    for w in params:
        out_shape += [jax.ShapeDtypeStruct(w.shape, F32)] * 4
    return pl.pallas_call(
        body, name="adam_vectors", in_specs=[vmem] * (2 + 3 * n), out_specs=[vmem] * len(out_shape),
        out_shape=out_shape, scratch_shapes=[pltpu.VMEM((PACK_ROWS, D_MODEL), F32)],
    )(packs, lb, *params, *ms, *vs)


def kernel(x, c, norm_g, w_ada, b_ada, w_in, hg_lb_logits, hg_norm_g, ret_norm_g, w_out, final_g, loss_target, m_norm_g, m_w_ada, m_b_ada, m_w_in, m_hg_lb_logits, m_hg_norm_g, m_ret_norm_g, m_w_out, m_final_g, v_norm_g, v_w_ada, v_b_ada, v_w_in, v_hg_lb_logits, v_hg_norm_g, v_ret_norm_g, v_w_out, v_final_g):
    pos = _mesh_pos()
    me_idx = jnp.reshape(_lin(pos), (1,)).astype(jnp.int32)
    c_idx = jnp.reshape(pos[2], (1,)).astype(jnp.int32)
    vec = lambda a: a.reshape(1, D_MODEL)

    mod, scall, lb, wtg, woutg = pre_exchange(c, w_ada[0], b_ada, hg_lb_logits, w_in[0], w_out[0])
    chip_idx = jnp.reshape(2 * pos[0] + pos[1], (1,)).astype(jnp.int32)
    gather_send, gather_recv, wtg, woutg, gather_token = weight_gather_start(wtg, woutg)
    mod = mod + gather_token[:1, :1]

    other_chips = jnp.stack([2 * cx + cy for cx, cy in _other_chips(pos)]).astype(jnp.int32)

    def project(h):
        t_len = h.shape[0]
        flat_in = lambda a: a.reshape(D_IN, D_MODEL)
        first = (gather_send, gather_recv)
        wg, wog = weight_gather_wait("weight_wait_sibling", 0, (0,), *first, wtg, woutg, h)
        pb = proj_forward(h, flat_in(wg), chip_idx, t_len, "proj_fwd_own")
        wg, wog = weight_gather_wait("weight_wait_neighbours", 0, (1, 2), *first, wg, wog, pb)
        *second, wg, wog = weight_pass_start(wg, wog)
        wg, wog = weight_gather_wait("weight_wait_passed_sibling", 1, (4, 5), *second, wg, wog, pb)
        pb = proj_forward(h, flat_in(wg), other_chips[:2], t_len, "proj_fwd_neighbours", pb)
        wg, wog = weight_gather_wait("weight_wait_passed_chip", 1, (3,), *second, wg, wog, pb)
        wg, wog = weight_pass_last(wg, wog)
        pb = proj_forward(h, flat_in(wg), other_chips[2:], t_len, "proj_fwd_diagonal", pb)
        return pb, flat_in(wg), wog.reshape(D_MODEL, D_MODEL)

    def start_exchange(dwin, dwin_sib, dwout):
        dwout = dwout.reshape(N_DEV, SHARD_OUT, D_MODEL)
        ra, rb = grad_pair_exchange(dwin_sib, dwout)
        sb, sbo = pair_sum(dwin, ra, dwout, rb, c_idx)
        send, recv, sb, sbo, rc, rco, token = grad_chip_start(sb, sbo)
        return token, (send, recv, sb, sbo, rc, rco)

    grad_x, _, _, vec_mid, vec_ada, dlb, pending = device_step(
        x[0], loss_target[0], mod, lb, project, norm_g, hg_norm_g, ret_norm_g, vec(final_g), c_idx, start_exchange)
    packs, dmod_all = pack_gather(vec_mid, vec_ada, dlb)
    sc_t = jnp.pad(scall.T, ((0, 0), (0, LANE - N_DEV)))
    g_ada, d_ada, nm_ada, nv_ada = adam_ada(sc_t, dmod_all, me_idx, w_ada[0], m_w_ada[0], v_w_ada[0])
    small = adam_vectors(
        packs, lb,
        (norm_g, b_ada, hg_lb_logits, hg_norm_g, ret_norm_g, vec(final_g)),
        (m_norm_g, m_b_ada, m_hg_lb_logits, m_hg_norm_g, m_ret_norm_g, vec(m_final_g)),
        (v_norm_g, v_b_ada, v_hg_lb_logits, v_hg_norm_g, v_ret_norm_g, vec(v_final_g)))
    loss = small[0][0, 0]
    sb, sbo, rc, rco = grad_chip_wait(*pending, small[0])
    g_in, d_in, nm_in, nv_in = adam_shard(chip_idx, sb, rc, w_in[0], m_w_in[0], v_w_in[0], "adam_w_in")
    g_out, d_out, nm_out, nv_out = adam_shard(chip_idx, sbo, rco, w_out[0], m_w_out[0], v_w_out[0], "adam_w_out")
    (g_ng, d_ng, nm_ng, nv_ng), (g_b, d_b, nm_b, nv_b), (g_lb, d_lb, nm_lb, nv_lb), (g_hg, d_hg, nm_hg, nv_hg), \
        (g_rg, d_rg, nm_rg, nv_rg), (g_fg, d_fg, nm_fg, nv_fg) = [small[1 + 4 * j:5 + 4 * j] for j in range(6)]
    flat = lambda a: a.reshape(D_MODEL)

    def group(ng, ada, b, win, lbl, hg, rg, wo, fg):
        return (ng, ada[None], b, win[None], lbl, hg, rg, wo[None], flat(fg))

    return (loss, grad_x[None],
            *group(g_ng, g_ada, g_b, g_in, g_lb, g_hg, g_rg, g_out, g_fg),
            *group(d_ng, d_ada, d_b, d_in, d_lb, d_hg, d_rg, d_out, d_fg),
            *group(nm_ng, nm_ada, nm_b, nm_in, nm_lb, nm_hg, nm_rg, nm_out, nm_fg),
            *group(nv_ng, nv_ada, nv_b, nv_in, nv_lb, nv_hg, nv_rg, nv_out, nv_fg))
```

```python
import numpy as np
import jax
import jax.numpy as jnp
from jax import lax
from jax.experimental import pallas as pl
from jax.experimental.pallas import tpu as pltpu

F32 = jnp.float32
_BF = jnp.bfloat16

D_MODEL = 1024
N_HEADS = 8
LANE = 128
RET_DK = 64
D_IN = 9216
N_DEV = 8
SHARD_IN = D_IN // N_DEV
SHARD_ADA = 3 * D_MODEL // N_DEV
SHARD_OUT = D_MODEL // N_DEV
N_CB = D_IN // LANE
CB_PER_SHARD = SHARD_IN // LANE
CHUNK = 128
N_LEVELS = 7
EPS = 1e-6
LOG2_E = float(np.log2(np.e))
ROPE_BASE = 10000.0
CB_HQ, CB_HF, CB_HI, CB_HZ, CB_RQ, CB_RK, CB_RV, CB_RZ, CB_GA, CB_GB = 0, 8, 16, 24, 32, 36, 40, 48, 56, 64
VMEM_LIMIT = 56 * 1024 * 1024

ADAM_LR, ADAM_B1, ADAM_B2, ADAM_EPS, ADAM_WD, ADAM_STEP = 0.001, 0.9, 0.999, 1e-08, 0.01, 10

_NN = (((1,), (0,)), ((), ()))
_NT = (((1,), (1,)), ((), ()))
_TN = (((0,), (0,)), ((), ()))
MESH = pl.DeviceIdType.MESH


def _dot(a, b, dims=_NN):
    return lax.dot_general(a.astype(_BF), b.astype(_BF), dims, preferred_element_type=F32)


def _split2(a):
    hi = a.astype(_BF)
    lo = (a - hi.astype(F32)).astype(_BF)
    return jnp.concatenate([hi, lo], axis=1)


def _dot_sel(sel, a):
    n = a.shape[1]
    r = lax.dot_general(sel.astype(_BF), _split2(a), _NN, preferred_element_type=F32)
    return r[:, :n] + r[:, n:]


def _dot_f32(a, b):
    def pieces(v):
        p1 = v.astype(_BF)
        r1 = v - p1.astype(F32)
        p2 = r1.astype(_BF)
        p3 = (r1 - p2.astype(F32)).astype(_BF)
        return p1, p2, p3
    a1, a2, a3 = pieces(a)
    b1, b2, b3 = pieces(b)
    d = lambda u, v: lax.dot_general(u, v, _NN, preferred_element_type=F32)
    return ((d(a1, b3) + d(a2, b2) + d(a3, b1)) + (d(a1, b2) + d(a2, b1))) + d(a1, b1)


def _sigmoid(v):
    return 1.0 / (1.0 + jnp.exp(-v))


def _params(sem=None):
    return pltpu.CompilerParams(dimension_semantics=sem, vmem_limit_bytes=VMEM_LIMIT)


def _hgrn_consts():
    c, nl = CHUNK, N_LEVELS
    t = np.arange(c)[:, None]
    j = np.arange(c)[None, :]
    sel = [j <= t]
    masks = [j == t]
    for l in range(1, nl + 1):
        m = ((t >> l) << l) + (1 << (l - 1)) - 1
        sec = t > m
        sel.append(np.where(sec, (j > m) & (j <= t), (j > t) & (j <= m)))
        same = (t >> l) == (j >> l)
        masks.append(same & sec & (j <= m))
    sel.append(j > t)
    sel = np.concatenate(sel, 0).astype(np.float32)
    masks = np.stack(masks).astype(np.float32)
    sgn = np.stack([np.where((t & (1 << (l - 1))) != 0, 1.0, -1.0) * np.ones((1, LANE)) for l in range(3, nl + 1)])
    return dict(tri=jnp.asarray(sel[:c], _BF),
                lvl=jnp.asarray(masks, F32),
                sgn=jnp.asarray(sgn, F32),
                sel_t=jnp.asarray(sel.T, _BF),
                lvl_b=jnp.asarray(masks, _BF),
                lvlt_b=jnp.asarray(np.swapaxes(masks, 1, 2), _BF))


def _level_exponents(b, logf, b_scr, sgn_ref):
    c = CHUNK
    b_scr[...] = b
    row = lax.broadcasted_iota(jnp.int32, (c, LANE), 0)
    nxt = pltpu.roll(logf, c - 1, 0)
    prv = pltpu.roll(logf, 1, 0)
    r4 = row & 3
    out = [jnp.where((row & 1) == 1, logf, 0.0),
           jnp.where(r4 == 0, nxt, jnp.where(r4 == 1, 0.0, jnp.where(r4 == 2, logf, logf + prv)))]
    for l in range(3, N_LEVELS + 1):
        size, half = 1 << l, 1 << (l - 1)
        ref = jnp.concatenate([jnp.broadcast_to(b_scr[i * size + half - 1:i * size + half, :], (size, LANE))
                               for i in range(c // size)], axis=0)
        out.append((b - ref) * sgn_ref[l - 3])
    return out


def _hgrn_chunk(hq, hf, hi, lbv, tri_ref, sgn_ref, b_scr):
    sq = _sigmoid(hq)
    q = hq * sq
    sg = _sigmoid(hf)
    omlb = 1.0 - lbv
    f = lbv + omlb * sg
    k = 1.0 - f
    logf = jnp.log(f) * LOG2_E
    b = _dot_sel(tri_ref[...], logf)
    bc = jnp.sum(logf, axis=0, keepdims=True)
    lev = [None] + [jnp.exp2(e) for e in _level_exponents(b, logf, b_scr, sgn_ref)]
    return dict(sq=sq, q=q, sg=sg, omlb=omlb, f=f, k=k, v=hi, eb=jnp.exp2(b), erem=jnp.exp2(bc - b),
                ebc=jnp.exp2(bc), lev=lev)


def _blockdiag(a, b):
    z = jnp.zeros_like(a)
    return jnp.concatenate([jnp.concatenate([a, z], axis=1), jnp.concatenate([z, b], axis=1)], axis=0)


def _level_operands(a):
    q, k = a["q"].astype(_BF), a["k"].astype(_BF)
    lev = [None] + [a["lev"][l].astype(_BF) for l in range(1, N_LEVELS + 1)]
    ql = [q] + [q * lev[l] for l in range(1, N_LEVELS + 1)]
    kl = [k] + [k * lev[l] for l in range(1, N_LEVELS + 1)]
    pairs = range(0, N_LEVELS + 1, 2)
    return ([jnp.concatenate([ql[l], ql[l + 1]], axis=1) for l in pairs], [_blockdiag(kl[l], kl[l + 1]) for l in pairs],
            ql, kl)


def _hgrn_scores(a, lvl_ref, q_pairs, k_diags):
    acc = None
    for n, (qp, kd) in enumerate(zip(q_pairs, k_diags)):
        both = lax.dot_general(qp, kd, _NT, preferred_element_type=F32)
        part = lvl_ref[2 * n] * both[:, :CHUNK] + lvl_ref[2 * n + 1] * both[:, CHUNK:]
        acc = part if acc is None else acc + part
    return acc


SCAN_UNROLL = 8
FWD_UNROLL = 16
RET_UNROLL = 16


def _writeback_reserve(step, make_copies):
    slot = step % 2

    @pl.when(step >= 2)
    def _():
        for cp in make_copies(slot):
            cp.wait()

    return slot


def _writeback_commit(step, n_steps, slot, make_copies):
    for cp in make_copies(slot):
        cp.start()

    @pl.when(step == n_steps - 1)
    def _():
        for cp in make_copies(slot):
            cp.wait()
        if n_steps > 1:
            for cp in make_copies(1 - slot):
                cp.wait()


def _resident(const):
    zeros = (0,) * const.ndim
    return pl.BlockSpec(const.shape, lambda p, t: zeros)


def _time_block(t_len):
    return min(t_len, 2048)


def hgrn_forward(pb, lb, t_len):
    nc = t_len // CHUNK
    tb = _time_block(t_len)
    ncb = tb // CHUNK
    consts = _hgrn_consts()
    operands = [consts[n] for n in ("tri", "lvl", "sgn")]

    def body(hq_ref, hf_ref, hi_ref, lb_ref, tri_ref, lvl_ref, sgn_ref, o_ref, ssave_ref, asave_ref, st_ref, b_scr):
        @pl.when(pl.program_id(1) == 0)
        def _():
            st_ref[...] = jnp.zeros_like(st_ref)

        def chunk(ci, carry):
            r = pl.ds(pl.multiple_of(ci * CHUNK, CHUNK), CHUNK)
            for hd in range(2):
                lbv = lb_ref[:, hd * LANE:(hd + 1) * LANE]
                a = _hgrn_chunk(hq_ref[hd, r, :], hf_ref[hd, r, :], hi_ref[hd, r, :], lbv, tri_ref, sgn_ref,
                                b_scr.at[hd])
                q_pairs, k_diags, _, _ = _level_operands(a)
                st = st_ref[hd]
                ssave_ref[hd, ci] = st
                scores = _hgrn_scores(a, lvl_ref, q_pairs, k_diags).astype(asave_ref.dtype)
                asave_ref[hd, ci] = scores
                o_ref[hd, r, :] = _dot(a["q"] * a["eb"], st, _NT) + _dot(scores, a["v"])
                st_ref[hd] = st * a["ebc"] + _dot(a["v"], a["k"] * a["erem"], _TN)
            return carry

        lax.fori_loop(0, ncb, chunk, 0, unroll=FWD_UNROLL)

    pair = lambda base: pl.BlockSpec((2, tb, LANE), lambda p, t, base=base: (base // 2 + p, t, 0))
    per_chunk = pl.BlockSpec((2, ncb, LANE, LANE), lambda p, t: (p, t, 0, 0))
    return pl.pallas_call(
        body, name="hgrn_fwd", grid=(N_HEADS // 2, t_len // tb),
        in_specs=[pair(CB_HQ), pair(CB_HF), pair(CB_HI),
                  pl.BlockSpec((1, 2 * LANE), lambda p, t: (0, p))] + [_resident(c) for c in operands],
        out_specs=[pl.BlockSpec((2, tb, LANE), lambda p, t: (p, t, 0)), per_chunk, per_chunk],
        out_shape=[jax.ShapeDtypeStruct((N_HEADS, t_len, LANE), F32),
                   jax.ShapeDtypeStruct((N_HEADS, nc, LANE, LANE), F32),
                   jax.ShapeDtypeStruct((N_HEADS, nc, CHUNK, CHUNK), _BF)],
        scratch_shapes=[pltpu.VMEM((2, LANE, LANE), F32), pltpu.VMEM((2, CHUNK, LANE), F32)],
        compiler_params=_params(("arbitrary", "arbitrary")),
    )(pb, pb, pb, lb, *operands)


def hgrn_backward(pb, lb, do, ssave, asave, dpb, t_len):
    tb = _time_block(t_len)
    ncb, ntb = tb // CHUNK, t_len // tb
    consts = _hgrn_consts()
    operands = [consts[n] for n in ("tri", "sgn", "sel_t", "lvl_b", "lvlt_b")]

    def body(hq_ref, hf_ref, hi_ref, lb_ref, do_ref, ssave_ref, asave_ref, tri_ref, sgn_ref, selt_ref, lvlb_ref,
             lvltb_ref, dpb_in, dpb_ref, dlb_ref, dq_buf, df_buf, di_buf, dst_ref, b_scr, sems):
        del dpb_in
        p, t = pl.program_id(0), pl.program_id(1)
        step = p * ntb + t
        rows = pl.ds(pl.multiple_of((ntb - 1 - t) * tb, tb), tb)

        def out_copies(sl):
            return [pltpu.make_async_copy(buf.at[sl], dpb_ref.at[pl.ds(base + 2 * p, 2), rows], sems.at[sl, n])
                    for n, (buf, base) in enumerate(((dq_buf, CB_HQ), (df_buf, CB_HF), (di_buf, CB_HI)))]

        slot = _writeback_reserve(step, out_copies)

        @pl.when(t == 0)
        def _():
            dst_ref[...] = jnp.zeros_like(dst_ref)
            dlb_ref[...] = jnp.zeros_like(dlb_ref)

        def chunk(i, carry):
            ci = ncb - 1 - i
            r = pl.ds(pl.multiple_of(ci * CHUNK, CHUNK), CHUNK)
            for hd in range(2):
                head_chunk(hd, ci, r)
            return carry

        def head_chunk(hd, ci, r):
            lbv = lb_ref[:, hd * LANE:(hd + 1) * LANE]
            hq = hq_ref[hd, r, :]
            a = _hgrn_chunk(hq, hf_ref[hd, r, :], hi_ref[hd, r, :], lbv, tri_ref, sgn_ref, b_scr.at[hd])
            _, k_diags, ql, kl = _level_operands(a)
            q, k, v = a["q"], a["k"], a["v"]
            g = do_ref[hd, r, :]
            st0 = ssave_ref[hd, ci]
            dst = dst_ref[hd]
            scores = asave_ref[hd, ci]
            da = _dot(g, v, _NT)
            da_t = _dot(v, g, _NT)
            kb = k * a["erem"]
            qb = q * a["eb"]
            dv = _dot(scores, g, _TN) + _dot(kb, dst, _NT)
            dq_inter = _dot(g, st0) * a["eb"]
            dk_state = _dot(v, dst) * a["erem"]
            dq, dk = dq_inter, dk_state
            de = [q * dq_inter]
            da_b, dat_b = da.astype(_BF), da_t.astype(_BF)
            for n in range(len(k_diags)):
                l0, l1 = 2 * n, 2 * n + 1
                da_pair = jnp.concatenate([lvlb_ref[l0] * da_b, lvlb_ref[l1] * da_b], axis=1)
                dat_pair = jnp.concatenate([lvltb_ref[l0] * dat_b, lvltb_ref[l1] * dat_b], axis=1)
                dq_both = lax.dot_general(da_pair, k_diags[n], _NN, preferred_element_type=F32)
                dk_both = lax.dot_general(dat_pair, _blockdiag(ql[l0], ql[l1]), _NN, preferred_element_type=F32)
                for l, cols in ((l0, slice(0, LANE)), (l1, slice(LANE, 2 * LANE))):
                    dql, dkl = dq_both[:, cols], dk_both[:, cols]
                    if l > 0:
                        e = a["lev"][l]
                        dql, dkl = dql * e, dkl * e
                        de.append(q * dql + k * dkl)
                    dq = dq + dql
                    dk = dk + dkl
            de.append(k * dk_state)
            dst_ref[hd] = dst * a["ebc"] + _dot(g, qb, _TN)
            dbc = jnp.sum(dst * st0, axis=0, keepdims=True) * a["ebc"]
            de2 = lax.dot_general(selt_ref[...], _split2(jnp.concatenate(de, axis=0)), _NN,
                                  preferred_element_type=F32)
            dlogf = de2[:, :LANE] + de2[:, LANE:] + dbc
            sq, sg = a["sq"], a["sg"]
            df = dlogf / a["f"] - dk
            dq_buf[slot, hd, r, :] = (dq * (sq * (1.0 + hq * (1.0 - sq)))).astype(dq_buf.dtype)
            df_buf[slot, hd, r, :] = (df * a["omlb"] * sg * (1.0 - sg)).astype(df_buf.dtype)
            di_buf[slot, hd, r, :] = dv.astype(di_buf.dtype)
            cols = slice(hd * LANE, (hd + 1) * LANE)
            dlb_ref[:, cols] = dlb_ref[:, cols] + jnp.sum(df * (1.0 - sg), axis=0, keepdims=True)

        lax.fori_loop(0, ncb, chunk, 0, unroll=SCAN_UNROLL)
        _writeback_commit(step, (N_HEADS // 2) * ntb, slot, out_copies)

    pair = lambda base: pl.BlockSpec((2, tb, LANE), lambda p, t, base=base: (base // 2 + p, ntb - 1 - t, 0))
    any_spec = pl.BlockSpec(memory_space=pl.ANY)
    per_chunk = pl.BlockSpec((2, ncb, LANE, LANE), lambda p, t: (p, ntb - 1 - t, 0, 0))
    return pl.pallas_call(
        body, name="hgrn_bwd", grid=(N_HEADS // 2, ntb),
        in_specs=[pair(CB_HQ), pair(CB_HF), pair(CB_HI),
                  pl.BlockSpec((1, 2 * LANE), lambda p, t: (0, p)),
                  pair(0), per_chunk, per_chunk]
        + [_resident(c) for c in operands] + [any_spec],
        out_specs=[any_spec, pl.BlockSpec((1, 2 * LANE), lambda p, t: (0, p))],
        out_shape=[jax.ShapeDtypeStruct(dpb.shape, dpb.dtype), jax.ShapeDtypeStruct((1, D_MODEL), F32)],
        scratch_shapes=[pltpu.VMEM((2, 2, tb, LANE), dpb.dtype)] * 3 + [
            pltpu.VMEM((2, LANE, LANE), F32), pltpu.VMEM((2, CHUNK, LANE), F32), pltpu.SemaphoreType.DMA((2, 3))],
        input_output_aliases={7 + len(operands): 0},
        compiler_params=_params(("arbitrary", "arbitrary")),
    )(pb, pb, pb, lb, do, ssave, asave, *operands, dpb)


def _rope_tables(t_len):
    half = RET_DK // 2
    inv_freq = (1.0 / (np.float32(ROPE_BASE) ** np.linspace(0.0, 1.0, half, dtype=np.float32))).astype(np.float32)
    ang = (np.arange(t_len, dtype=np.float32)[:, None] * inv_freq[None, :]).astype(np.float64)
    cos, sin = np.cos(ang).astype(np.float32), np.sin(ang).astype(np.float32)
    cos_t = np.concatenate([cos, cos, cos, cos], axis=1)
    sin_t = np.concatenate([-sin, sin, -sin, sin], axis=1)
    return jnp.asarray(cos_t), jnp.asarray(sin_t)


def _swap_halves(v):
    half = RET_DK // 2
    lane = lax.broadcasted_iota(jnp.int32, v.shape, 1)
    first = (lane & (RET_DK - 1)) < half
    return jnp.where(first, pltpu.roll(v, LANE - half, 1), pltpu.roll(v, half, 1))


def _ret_head_consts(hidx):
    c = CHUNK
    hf = jnp.full((1, LANE), hidx, jnp.int32).astype(F32)
    lg = jnp.log(1.0 - jnp.exp(-(5.0 + hf) * np.float32(np.log(2.0))))
    row = lax.broadcasted_iota(jnp.int32, (c, c), 0)
    col = lax.broadcasted_iota(jnp.int32, (c, c), 1)
    rel = (row - col).astype(F32)
    dm = jnp.where(rel >= 0, jnp.exp(lg[:, :1] * jnp.maximum(rel, 0.0)), 0.0)
    dm_t = jnp.where(rel <= 0, jnp.exp(lg[:, :1] * jnp.maximum(-rel, 0.0)), 0.0)
    idx = lax.broadcasted_iota(jnp.int32, (c, LANE), 0).astype(F32)
    zeta = jnp.exp(lg * (c - 1.0 - idx))
    xi = jnp.exp(lg * (idx + 1.0))
    cdec = jnp.exp(lg * float(c))
    return dm, zeta, xi, cdec, dm_t


def _lane_mask(which):
    lane = lax.broadcasted_iota(jnp.int32, (1, LANE), 1)
    return ((lane // RET_DK) == which).astype(F32)


def retention_forward(pb, cos_t, sin_t, t_len):
    nc = t_len // CHUNK

    tb = _time_block(t_len)
    ncb = tb // CHUNK

    def body(rq_ref, rk_ref, rv_ref, cos_ref, sin_ref, o_ref, rsave_ref, st_ref):
        p = pl.program_id(0)

        @pl.when(pl.program_id(1) == 0)
        def _():
            st_ref[...] = jnp.zeros_like(st_ref)

        consts = [_ret_head_consts(2 * p + hd) for hd in range(2)]

        def chunk(ci, carry):
            r = pl.ds(pl.multiple_of(ci * CHUNK, CHUNK), CHUNK)
            cs, sn = cos_ref[r, :], sin_ref[r, :]
            q = rq_ref[r, :]
            k = rk_ref[r, :]
            q = q * cs + _swap_halves(q) * sn
            k = (k * cs + _swap_halves(k) * sn) * RET_DK ** -0.5
            for hd in range(2):
                dm, zeta, xi, cdec, _ = consts[hd]
                lm = _lane_mask(hd)
                qh, kh = q * lm, k * lm
                v = rv_ref[hd, r, :]
                st = st_ref[hd]
                rsave_ref[hd, ci] = st
                scores = _dot(qh, kh, _NT) * dm
                o_ref[hd, r, :] = _dot(scores, v) + _dot(qh * xi, st, _NT)
                st_ref[hd] = st * cdec + _dot(v, kh * zeta, _TN)
            return carry

        lax.fori_loop(0, ncb, chunk, 0, unroll=RET_UNROLL)

    return pl.pallas_call(
        body, name="ret_fwd", grid=(N_HEADS // 2, t_len // tb),
        in_specs=[pl.BlockSpec((None, tb, LANE), lambda p, t: (CB_RQ + p, t, 0)),
                  pl.BlockSpec((None, tb, LANE), lambda p, t: (CB_RK + p, t, 0)),
                  pl.BlockSpec((2, tb, LANE), lambda p, t: (CB_RV // 2 + p, t, 0)),
                  pl.BlockSpec((tb, LANE), lambda p, t: (t, 0)),
                  pl.BlockSpec((tb, LANE), lambda p, t: (t, 0))],
        out_specs=[pl.BlockSpec((2, tb, LANE), lambda p, t: (p, t, 0)),
                   pl.BlockSpec((2, ncb, LANE, LANE), lambda p, t: (p, t, 0, 0))],
        out_shape=[jax.ShapeDtypeStruct((N_HEADS, t_len, LANE), F32),
                   jax.ShapeDtypeStruct((N_HEADS, nc, LANE, LANE), F32)],
        scratch_shapes=[pltpu.VMEM((2, LANE, LANE), F32)],
        compiler_params=_params(("arbitrary", "arbitrary")),
    )(pb, pb, pb, cos_t, sin_t)


def retention_backward(pb, cos_t, sin_t, do, rsave, dpb, t_len):
    tb = _time_block(t_len)
    ncb, ntb = tb // CHUNK, t_len // tb

    def body(rq_ref, rk_ref, rv_ref, cos_ref, sin_ref, do_ref, rsave_ref, dpb_in,
             dpb_ref, dq_buf, dk_buf, dv_buf, dst_ref, sems):
        del dpb_in
        p, t = pl.program_id(0), pl.program_id(1)
        step = p * ntb + t
        rows = pl.ds(pl.multiple_of((ntb - 1 - t) * tb, tb), tb)

        def out_copies(sl):
            return [pltpu.make_async_copy(dq_buf.at[sl], dpb_ref.at[CB_RQ + p, rows], sems.at[sl, 0]),
                    pltpu.make_async_copy(dk_buf.at[sl], dpb_ref.at[CB_RK + p, rows], sems.at[sl, 1]),
                    pltpu.make_async_copy(dv_buf.at[sl], dpb_ref.at[pl.ds(CB_RV + 2 * p, 2), rows], sems.at[sl, 2])]

        slot = _writeback_reserve(step, out_copies)

        @pl.when(t == 0)
        def _():
            dst_ref[...] = jnp.zeros_like(dst_ref)

        consts = [_ret_head_consts(2 * p + hd) for hd in range(2)]

        def chunk(i, carry):
            ci = ncb - 1 - i
            r = pl.ds(pl.multiple_of(ci * CHUNK, CHUNK), CHUNK)
            cs, sn = cos_ref[r, :], sin_ref[r, :]
            q = rq_ref[r, :]
            k = rk_ref[r, :]
            q = q * cs + _swap_halves(q) * sn
            k = (k * cs + _swap_halves(k) * sn) * RET_DK ** -0.5
            dq, dk = None, None
            for hd in range(2):
                dm, zeta, xi, cdec, dm_t = consts[hd]
                lm = _lane_mask(hd)
                qh, kh = q * lm, k * lm
                v = rv_ref[hd, r, :]
                g = do_ref[hd, r, :]
                st0 = rsave_ref[hd, ci]
                dst = dst_ref[hd]
                scores_t = _dot(kh, qh, _NT) * dm_t
                dsc = _dot(g, v, _NT) * dm
                dsc_t = _dot(v, g, _NT) * dm_t
                dqh = _dot(dsc, kh) + _dot(g, st0) * xi
                dkh = _dot(dsc_t, qh) + _dot(v, dst) * zeta
                dv_buf[slot, hd, r, :] = (_dot(scores_t, g) + _dot(kh * zeta, dst, _NT)).astype(dv_buf.dtype)
                dst_ref[hd] = dst * cdec + _dot(g, qh * xi, _TN)
                dq = dqh if dq is None else dq + dqh
                dk = dkh if dk is None else dk + dkh
            dk = dk * (RET_DK ** -0.5)
            dq_buf[slot, r, :] = (dq * cs - _swap_halves(dq) * sn).astype(dq_buf.dtype)
            dk_buf[slot, r, :] = (dk * cs - _swap_halves(dk) * sn).astype(dk_buf.dtype)
            return carry

        lax.fori_loop(0, ncb, chunk, 0, unroll=RET_UNROLL)
        _writeback_commit(step, (N_HEADS // 2) * ntb, slot, out_copies)

    any_spec = pl.BlockSpec(memory_space=pl.ANY)
    return pl.pallas_call(
        body, name="ret_bwd", grid=(N_HEADS // 2, ntb),
        in_specs=[pl.BlockSpec((None, tb, LANE), lambda p, t: (CB_RQ + p, ntb - 1 - t, 0)),
                  pl.BlockSpec((None, tb, LANE), lambda p, t: (CB_RK + p, ntb - 1 - t, 0)),
                  pl.BlockSpec((2, tb, LANE), lambda p, t: (CB_RV // 2 + p, ntb - 1 - t, 0)),
                  pl.BlockSpec((tb, LANE), lambda p, t: (ntb - 1 - t, 0)),
                  pl.BlockSpec((tb, LANE), lambda p, t: (ntb - 1 - t, 0)),
                  pl.BlockSpec((2, tb, LANE), lambda p, t: (p, ntb - 1 - t, 0)),
                  pl.BlockSpec((2, ncb, LANE, LANE), lambda p, t: (p, ntb - 1 - t, 0, 0)),
                  any_spec],
        out_specs=any_spec,
        out_shape=jax.ShapeDtypeStruct(dpb.shape, dpb.dtype),
        scratch_shapes=[pltpu.VMEM((2, tb, LANE), dpb.dtype), pltpu.VMEM((2, tb, LANE), dpb.dtype),
                        pltpu.VMEM((2, 2, tb, LANE), dpb.dtype), pltpu.VMEM((2, LANE, LANE), F32),
                        pltpu.SemaphoreType.DMA((2, 3))],
        input_output_aliases={7: 0},
        compiler_params=_params(("arbitrary", "arbitrary")),
    )(pb, pb, pb, cos_t, sin_t, do, rsave, dpb)


def _row_tile(t_len, want):
    return min(want, t_len)


PAIR_CB = 2 * CB_PER_SHARD


def proj_forward(h, wt, chips, t_len, name, pb=None):
    tm = _row_tile(t_len, 1024)

    def body(chips_ref, h_ref, w_ref, *rest):
        del chips_ref
        o_ref = rest[-1]
        acc = _dot(h_ref[...], w_ref[...], _NT)
        for jj in range(PAIR_CB):
            o_ref[jj] = acc[:, jj * LANE:(jj + 1) * LANE]

    given = [] if pb is None else [pb]
    return pl.pallas_call(
        body, name=name,
        grid_spec=pltpu.PrefetchScalarGridSpec(
            num_scalar_prefetch=1, grid=(chips.shape[0], t_len // tm),
            in_specs=[pl.BlockSpec((tm, D_MODEL), lambda j, i, ch: (i, 0)),
                      pl.BlockSpec((PAIR_CB * LANE, D_MODEL), lambda j, i, ch: (ch[j], 0))]
            + [pl.BlockSpec(memory_space=pl.ANY)] * len(given),
            out_specs=pl.BlockSpec((PAIR_CB, tm, LANE), lambda j, i, ch: (ch[j], i, 0))),
        out_shape=jax.ShapeDtypeStruct((N_CB, t_len, LANE), F32),
        input_output_aliases={3: 0} if given else {},
        compiler_params=_params(("arbitrary", "arbitrary")),
    )(chips, h, wt, *given)


def proj_backward_input(dpb, wt, token, x, dy, norm_g, scale1p, t_len):
    tm = _row_tile(t_len, 512)

    def body(a_ref, wt_hbm, token_ref, x_ref, dy_ref, g_ref, sc_ref, gx_ref, vec_ref, w_ref, sem):
        del token_ref
        i = pl.program_id(0)

        @pl.when(i == 0)
        def _():
            cp = pltpu.make_async_copy(wt_hbm, w_ref, sem)
            cp.start()
            cp.wait()

        a = jnp.concatenate([a_ref[jj].astype(_BF) for jj in range(N_CB)], axis=1)
        dhv = _dot(a, w_ref[...])
        xv, g, sc = x_ref[...], g_ref[...], sc_ref[...]
        r = lax.rsqrt(jnp.mean(xv * xv, axis=-1, keepdims=True) + EPS)
        xn = xv * r
        dxn = dhv * (g * sc)
        gx_ref[...] = dy_ref[...] + r * dxn - xn * (r * r) * jnp.mean(xv * dxn, axis=-1, keepdims=True)
        t = dhv * xn
        _acc_rows(vec_ref, i, [jnp.sum(t * sc, axis=0, keepdims=True),
                               jnp.sum(t * g, axis=0, keepdims=True),
                               jnp.sum(dhv, axis=0, keepdims=True)])

    row = pl.BlockSpec((tm, D_MODEL), lambda i: (i, 0))
    return pl.pallas_call(
        body, name="proj_bwd_input", grid=(t_len // tm,),
        in_specs=[pl.BlockSpec((N_CB, tm, LANE), lambda i: (0, i, 0)),
                  pl.BlockSpec(memory_space=pl.ANY),
                  pl.BlockSpec(token.shape, lambda i: (0, 0)),
                  row, row, _vec_spec(), _vec_spec()],
        out_specs=[row, pl.BlockSpec((8, D_MODEL), lambda i: (0, 0))],
        out_shape=[jax.ShapeDtypeStruct((t_len, D_MODEL), F32), jax.ShapeDtypeStruct((8, D_MODEL), F32)],
        scratch_shapes=[pltpu.VMEM(wt.shape, wt.dtype), pltpu.SemaphoreType.DMA],
        compiler_params=_params(("arbitrary",)),
    )(dpb, wt, token, x, dy, norm_g, scale1p)


def proj_backward_weight(h_t, dpb, t_len):
    tk = _row_tile(t_len, 2048)

    def body(h_ref, b_ref, o_ref):
        k = pl.program_id(1)
        b = jnp.concatenate([b_ref[jj].astype(_BF) for jj in range(PAIR_CB)], axis=1)
        part = _dot(h_ref[...], b)

        @pl.when(k == 0)
        def _():
            for s in range(2):
                o_ref[s] = part[:, s * SHARD_IN:(s + 1) * SHARD_IN]

        @pl.when(k > 0)
        def _():
            for s in range(2):
                o_ref[s] = o_ref[s] + part[:, s * SHARD_IN:(s + 1) * SHARD_IN]

    return pl.pallas_call(
        body, name="proj_bwd_weight", grid=(N_DEV // 2, t_len // tk),
        in_specs=[pl.BlockSpec((D_MODEL, tk), lambda j, k: (0, k)),
                  pl.BlockSpec((PAIR_CB, tk, LANE), lambda j, k: (j, k, 0))],
        out_specs=pl.BlockSpec((2, D_MODEL, SHARD_IN), lambda j, k: (j, 0, 0)),
        out_shape=jax.ShapeDtypeStruct((N_DEV, D_MODEL, SHARD_IN), F32),
        compiler_params=_params(("arbitrary", "arbitrary")),
    )(h_t, dpb)


def sibling_blocks(g_in, c_idx):
    tr = D_MODEL

    def body(c_ref, g_ref, o_ref):
        del c_ref
        o_ref[...] = g_ref[...].astype(o_ref.dtype)

    return pl.pallas_call(
        body, name="sibling_blocks",
        grid_spec=pltpu.PrefetchScalarGridSpec(
            num_scalar_prefetch=1, grid=(N_DEV // 2, D_MODEL // tr),
            in_specs=[pl.BlockSpec((None, tr, SHARD_IN), lambda q, i, c: (2 * q + 1 - c[0], i, 0))],
            out_specs=pl.BlockSpec((None, tr, SHARD_IN), lambda q, i, c: (q, i, 0))),
        out_shape=jax.ShapeDtypeStruct((N_DEV // 2, D_MODEL, SHARD_IN), _BF),
        compiler_params=_params(("arbitrary", "arbitrary")),
    )(c_idx, g_in)


def _vec_spec():
    return pl.BlockSpec((1, D_MODEL), lambda i: (0, 0))


def _acc_rows(ref, i, rows):
    @pl.when(i == 0)
    def _():
        ref[...] = jnp.zeros_like(ref)

    for n, row in enumerate(rows):
        ref[n:n + 1, :] = ref[n:n + 1, :] + row


def adaln_forward(x, norm_g, scale1p, shift, t_len):
    tm = _row_tile(t_len, 1024)

    def body(x_ref, g_ref, sc_ref, sh_ref, h_ref, ht_ref):
        xv = x_ref[...]
        r = lax.rsqrt(jnp.mean(xv * xv, axis=-1, keepdims=True) + EPS)
        h = xv * r * g_ref[...] * sc_ref[...] + sh_ref[...]
        h_ref[...] = h.astype(h_ref.dtype)
        ht_ref[...] = h.T.astype(ht_ref.dtype)

    return pl.pallas_call(
        body, name="adaln_fwd", grid=(t_len // tm,),
        in_specs=[pl.BlockSpec((tm, D_MODEL), lambda i: (i, 0)), _vec_spec(), _vec_spec(), _vec_spec()],
        out_specs=[pl.BlockSpec((tm, D_MODEL), lambda i: (i, 0)), pl.BlockSpec((D_MODEL, tm), lambda i: (0, i))],
        out_shape=[jax.ShapeDtypeStruct((t_len, D_MODEL), _BF), jax.ShapeDtypeStruct((D_MODEL, t_len), _BF)],
        compiler_params=_params(("arbitrary",)),
    )(x, norm_g, scale1p, shift)


def _head_norm(o, g):
    r = lax.rsqrt(jnp.mean(o * o, axis=-1, keepdims=True) + EPS)
    return r, o * r * g


def _group_spec(tm, cb):
    return pl.BlockSpec((N_HEADS, tm, LANE), lambda i, cb=cb: (cb // N_HEADS, i, 0))


MID_FINAL_G, MID_GATE, MID_LOSS, MID_HG_G, MID_RET_G = range(5)


def middle(x, target, oa, ob, pb, wout, gate, final_g, hg_g, ret_g, t_len):
    tm = _row_tile(t_len, 256)
    n_steps = t_len // tm

    def body(x_ref, t_ref, oa_ref, ob_ref, hz_ref, rz_ref, ga_ref, gb_ref, w_ref, gate_ref, fg_ref, hg_ref, rg_ref,
             dy_ref, doa_ref, dob_ref, dw_ref, vec_ref, dpb_ref, m_scr, dm_scr, keep, bufs, sems):
        i = pl.program_id(0)
        rows = pl.ds(pl.multiple_of(i * tm, tm), tm)

        def group_copies(sl):
            return [pltpu.make_async_copy(bufs.at[sl, n], dpb_ref.at[pl.ds(cb, N_HEADS), rows], sems.at[sl, n])
                    for n, cb in enumerate((CB_HZ, CB_RZ, CB_GA, CB_GB))]
        sides = ((oa_ref, hz_ref, ga_ref, hg_ref, doa_ref), (ob_ref, rz_ref, gb_ref, rg_ref, dob_ref))
        for hh in range(N_HEADS):
            ls = slice(hh * LANE, (hh + 1) * LANE)
            acc = None
            for side, (o_ref, z_ref, gt_ref, g_ref, _) in enumerate(sides):
                o = o_ref[hh]
                rr = lax.rsqrt(jnp.mean(o * o, axis=-1, keepdims=True) + EPS)
                orr = o * rr
                zz = z_ref[hh]
                sz = _sigmoid(zz)
                sgt = _sigmoid(gt_ref[hh])
                keep[side, hh, 0] = orr
                keep[side, hh, 1] = sz
                keep[side, hh, 2] = sgt
                keep[side, hh, 3] = jnp.broadcast_to(rr, orr.shape)
                u = sgt * ((orr * g_ref[:, ls]) * (zz * sz))
                acc = u if acc is None else acc + u
            m_scr[:, ls] = acc.astype(m_scr.dtype)
        zv = _dot(m_scr[...], w_ref[...])
        gt, fg = gate_ref[...], fg_ref[...]
        y = x_ref[...] + gt * zv
        r = lax.rsqrt(jnp.mean(y * y, axis=-1, keepdims=True) + EPS)
        yn = y * r
        err = yn * fg - t_ref[...]
        loss = 0.5 * jnp.sum(jnp.mean(err * err, axis=-1, keepdims=True), axis=0, keepdims=True)
        dout = err * (1.0 / D_MODEL)
        gd = dout * fg
        dy = r * gd - yn * (r * r) * jnp.mean(y * gd, axis=-1, keepdims=True)
        dy_ref[...] = dy
        dz = (dy * gt).astype(_BF)
        dm_scr[...] = _dot(dz, w_ref[...], _NT)
        part = _dot(m_scr[...], dz, _TN)

        @pl.when(i == 0)
        def _():
            dw_ref[...] = part

        @pl.when(i > 0)
        def _():
            dw_ref[...] = dw_ref[...] + part

        slot = _writeback_reserve(i, group_copies)
        dg = [[], []]
        for hh in range(N_HEADS):
            ls = slice(hh * LANE, (hh + 1) * LANE)
            dmh = dm_scr[:, ls]
            for side, (o_ref, z_ref, gt_ref, g_ref, do_ref) in enumerate(sides):
                zz, g = z_ref[hh], g_ref[:, ls]
                orr, sz, sgt, rr = keep[side, hh, 0], keep[side, hh, 1], keep[side, hh, 2], keep[side, hh, 3]
                n = orr * g
                silu = zz * sz
                du = dmh * sgt
                bufs[slot, 2 + side, hh] = (dmh * (n * silu) * (sgt * (1.0 - sgt))).astype(bufs.dtype)
                bufs[slot, side, hh] = (du * n * (sz * (1.0 + zz * (1.0 - sz)))).astype(bufs.dtype)
                dn = du * silu
                dg[side].append(jnp.sum(dn * orr, axis=0, keepdims=True))
                gdn = dn * g
                do_ref[hh] = (rr * (gdn - orr * jnp.mean(orr * gdn, axis=-1, keepdims=True))).astype(do_ref.dtype)
        _acc_rows(vec_ref, i, [jnp.sum(dout * yn, axis=0, keepdims=True),
                               jnp.sum(dy * zv, axis=0, keepdims=True),
                               jnp.broadcast_to(loss, (1, D_MODEL)),
                               jnp.concatenate(dg[0], axis=1), jnp.concatenate(dg[1], axis=1)])
        _writeback_commit(i, n_steps, slot, group_copies)

    row = pl.BlockSpec((tm, D_MODEL), lambda i: (i, 0))
    head = pl.BlockSpec((N_HEADS, tm, LANE), lambda i: (0, i, 0))
    full = pl.BlockSpec((D_MODEL, D_MODEL), lambda i: (0, 0))
    return pl.pallas_call(
        body, name="middle", grid=(n_steps,),
        in_specs=[row, row, head, head, _group_spec(tm, CB_HZ), _group_spec(tm, CB_RZ), _group_spec(tm, CB_GA),
                  _group_spec(tm, CB_GB), full, _vec_spec(), _vec_spec(), _vec_spec(), _vec_spec()],
        out_specs=[row, head, head, full, pl.BlockSpec((8, D_MODEL), lambda i: (0, 0)),
                   pl.BlockSpec(memory_space=pl.ANY)],
        out_shape=[jax.ShapeDtypeStruct((t_len, D_MODEL), F32),
                   jax.ShapeDtypeStruct((N_HEADS, t_len, LANE), _BF),
                   jax.ShapeDtypeStruct((N_HEADS, t_len, LANE), _BF),
                   jax.ShapeDtypeStruct((D_MODEL, D_MODEL), F32),
                   jax.ShapeDtypeStruct((8, D_MODEL), F32),
                   jax.ShapeDtypeStruct((N_CB, t_len, LANE), _BF)],
        scratch_shapes=[pltpu.VMEM((tm, D_MODEL), _BF), pltpu.VMEM((tm, D_MODEL), F32),
                        pltpu.VMEM((2, N_HEADS, 4, tm, LANE), F32),
                        pltpu.VMEM((2, 4, N_HEADS, tm, LANE), _BF), pltpu.SemaphoreType.DMA((2, 4))],
        compiler_params=_params(("arbitrary",)),
    )(x, target, oa, ob, pb, pb, pb, pb, wout, gate, final_g, hg_g, ret_g)


def device_step(x, target, mod, lb, project, norm_g, hg_g, ret_g, final_g, c_idx=None, start_exchange=None):
    t_len = x.shape[0]
    shift, scale, gate = mod[:, :D_MODEL], mod[:, D_MODEL:2 * D_MODEL], mod[:, 2 * D_MODEL:]
    scale1p = 1.0 + scale
    cos_t, sin_t = _rope_tables(t_len)
    h, h_t = adaln_forward(x, norm_g, scale1p, shift, t_len)
    pb, wt, wout = project(h)
    oa, ssave, asave = hgrn_forward(pb, lb, t_len)
    ob, rsave = retention_forward(pb, cos_t, sin_t, t_len)
    dy, doa, dob, dwout, vec_mid, dpb = middle(x, target, oa, ob, pb, wout, gate, final_g, hg_g, ret_g, t_len)
    dpb, dlb = hgrn_backward(pb, lb, doa, ssave, asave, dpb, t_len)
    dpb = retention_backward(pb, cos_t, sin_t, dob, rsave, dpb, t_len)
    c_idx = jnp.zeros((1,), jnp.int32) if c_idx is None else c_idx
    dwin = proj_backward_weight(h_t, dpb, t_len)
    dwin_sib = sibling_blocks(dwin, c_idx)
    token, pending = (start_exchange(dwin, dwin_sib, dwout) if start_exchange
                      else (jnp.zeros((8, LANE), F32), None))
    grad_x, vec_ada = proj_backward_input(dpb, wt, token, x, dy, norm_g, scale1p, t_len)
    return grad_x, dwin, dwout, vec_mid, vec_ada, dlb, pending


PACK_ROWS = 16
ROW_NORM_G, ROW_LB, ROW_HG_G, ROW_RET_G, ROW_FINAL_G, ROW_SHIFT, ROW_SCALE, ROW_GATE, ROW_LOSS = range(9)


def _mesh_pos():
    return lax.axis_index("x"), lax.axis_index("y"), lax.axis_index("c")


def _lin(pos):
    return 4 * pos[0] + 2 * pos[1] + pos[2]


def _xor_peer(pos, k):
    return tuple(1 - p if (k >> s) & 1 else p for p, s in zip(pos, (2, 1, 0)))


def _other_chips(pos):
    x, y, _ = pos
    return [(1 - x, y), (x, 1 - y), (1 - x, 1 - y)]


def _remote(src, dst, send_sem, recv_sem, to):
    return pltpu.make_async_remote_copy(src_ref=src, dst_ref=dst, send_sem=send_sem, recv_sem=recv_sem,
                                        device_id=to, device_id_type=MESH)


def pre_exchange(c, w_ada, b_ada, logits, w_in, w_out):
    def body(c_ref, wada_ref, bada_ref, logit_ref, win_ref, wout_ref, mod_ref, scall_ref, lb_ref, wg_ref, wog_ref,
             cg_ref, modall_ref, parts_ref, wt_ref, wo_ref, send1, recv1, send2, recv2, local):
        pos = _mesh_pos()
        placed = [pltpu.make_async_copy(wt_ref, wg_ref.at[_lin(pos)], local.at[0]),
                  pltpu.make_async_copy(wo_ref, wog_ref.at[_lin(pos)], local.at[1])]
        cv = c_ref[...]
        slot = lambda p: pl.ds(pl.multiple_of(8 * _lin(p), 8), 8)
        cg_ref[slot(pos), :] = jnp.broadcast_to(cv * _sigmoid(cv), (8, D_MODEL))
        lb_ref[...] = _sigmoid(logit_ref[0:1, :] - logit_ref[1:2, :])
        peers = [_xor_peer(pos, k) for k in range(1, N_DEV)]
        gather = [_remote(cg_ref.at[slot(pos)], cg_ref.at[slot(pos)], send1.at[n], recv1.at[n], p)
                  for n, p in enumerate(peers)]
        for cp in gather:
            cp.start()
        wt_ref[...] = win_ref[...].T.astype(wt_ref.dtype)
        wo_ref[...] = wout_ref[...].astype(wo_ref.dtype)
        for cp in placed:
            cp.start()
        for n, p in enumerate(peers):
            _remote(cg_ref.at[slot(p)], cg_ref.at[slot(p)], send1.at[n], recv1.at[n], p).wait_recv()
        modall_ref[...] = _dot(cg_ref[...], wada_ref[...])
        scatter = [_remote(modall_ref.at[slot(p)], parts_ref.at[slot(pos)], send2.at[n], recv2.at[n], p)
                   for n, p in enumerate(peers)]
        for cp in scatter:
            cp.start()
        parts_ref[slot(pos), :] = modall_ref[slot(pos), :]
        for n, p in enumerate(peers):
            _remote(modall_ref.at[slot(p)], parts_ref.at[slot(p)], send2.at[n], recv2.at[n], p).wait_recv()
        for cp in gather + scatter:
            cp.wait_send()
        for j in range(N_DEV):
            cols = slice(j * SHARD_ADA, (j + 1) * SHARD_ADA)
            mod_ref[:, cols] = parts_ref[8 * j:8 * j + 1, :] + bada_ref[:, cols]
            scall_ref[j:j + 1, :] = cg_ref[8 * j:8 * j + 1, :]
        for cp in placed:
            cp.wait()

    vmem = pl.BlockSpec(memory_space=pltpu.VMEM)
    return pl.pallas_call(
        body, name="pre_exchange",
        in_specs=[vmem] * 6, out_specs=[vmem] * 3 + [_HBM] * 2,
        out_shape=[jax.ShapeDtypeStruct((1, 3 * D_MODEL), F32), jax.ShapeDtypeStruct((N_DEV, D_MODEL), F32),
                   jax.ShapeDtypeStruct((1, D_MODEL), F32),
                   jax.ShapeDtypeStruct((N_DEV,) + w_in.shape[::-1], _BF),
                   jax.ShapeDtypeStruct((N_DEV,) + w_out.shape, _BF)],
        scratch_shapes=[pltpu.VMEM((N_DEV * 8, D_MODEL), F32), pltpu.VMEM((N_DEV * 8, SHARD_ADA), F32),
                        pltpu.VMEM((N_DEV * 8, SHARD_ADA), F32),
                        pltpu.VMEM(w_in.shape[::-1], _BF), pltpu.VMEM(w_out.shape, _BF)]
        + [pltpu.SemaphoreType.DMA((N_DEV - 1,))] * 4 + [pltpu.SemaphoreType.DMA((2,))],
        compiler_params=pltpu.CompilerParams(vmem_limit_bytes=VMEM_LIMIT),
    )(c, w_ada, b_ada, logits, w_in, w_out)


def _gather_copies(outs, sems):
    pos = _mesh_pos()
    x, y, c = pos
    sibling = (x, y, 1 - c)

    def route(core):
        return [(x + (1 - core) * (1 - 2 * x), y + core * (1 - 2 * y)),
                (x + core * (1 - 2 * x), y + (1 - core) * (1 - 2 * y)),
                (1 - x, 1 - y)]

    mine = [(*chip, c) for chip in route(c)]
    carried = [pos, pos, pos, mine[0], mine[0], mine[1], mine[2]]
    to = [sibling, mine[0], mine[1], mine[1], sibling, sibling, sibling]
    landed = [sibling] + mine + [(*chip, 1 - c) for chip in route(1 - c)]

    def sent(a, k):
        dst = outs[a].at[_lin(carried[k])]
        return _remote(dst, dst, *sems(a, k), to[k])

    def arrival(a, k):
        dst = outs[a].at[_lin(landed[k])]
        return _remote(dst, dst, *sems(a, k), pos)

    return sent, arrival


_ROUNDS = (range(0, 3), range(3, 6), range(6, 7))


def _round_sems(send, recv, rnd):
    n = len(_ROUNDS[rnd])
    return lambda a, k: (send.at[n * a + k - _ROUNDS[rnd][0]], recv.at[n * a + k - _ROUNDS[rnd][0]])


def weight_gather_start(wg, woutg):
    def body(wg_ref, woutg_ref, send, recv, wg_out, woutg_out, token):
        del wg_out, woutg_out
        sent, _ = _gather_copies((wg_ref, woutg_ref), _round_sems(send, recv, 0))
        for a in range(2):
            for k in _ROUNDS[0]:
                sent(a, k).start()
        token[...] = jnp.zeros_like(token)

    hbm = lambda a: pltpu.with_memory_space_constraint(a, pltpu.HBM)
    n = 2 * len(_ROUNDS[0])
    return pl.pallas_call(
        body, name="weight_gather_start",
        in_specs=[_HBM] * 2,
        out_specs=[_SEM, _SEM, _HBM, _HBM, pl.BlockSpec(memory_space=pltpu.VMEM)],
        out_shape=[pltpu.SemaphoreType.DMA((n,)), pltpu.SemaphoreType.DMA((n,)),
                   pltpu.HBM(wg.shape, wg.dtype), pltpu.HBM(woutg.shape, woutg.dtype),
                   jax.ShapeDtypeStruct((8, LANE), F32)],
        input_output_aliases={0: 2, 1: 3},
        compiler_params=pltpu.CompilerParams(has_side_effects=pltpu.SideEffectType.DATAFLOW_SIDE_EFFECTING),
    )(hbm(wg), hbm(woutg))


def _wait_copies(sent, arrival, ks):
    for a in range(2):
        for k in ks:
            arrival(a, k).wait_recv()
            sent(a, k).wait_send()


def weight_gather_wait(name, rnd, ks, send, recv, wg, woutg, after):
    def body(wg_ref, woutg_ref, send, recv, after_ref, wg_out, woutg_out):
        del after_ref, wg_out, woutg_out
        _wait_copies(*_gather_copies((wg_ref, woutg_ref), _round_sems(send, recv, rnd)), ks)

    return pl.pallas_call(
        body, name=name,
        in_specs=[_HBM] * 2 + [_SEM, _SEM, pl.BlockSpec(memory_space=pl.ANY)],
        out_specs=[_HBM] * 2,
        out_shape=[pltpu.HBM(wg.shape, wg.dtype), pltpu.HBM(woutg.shape, woutg.dtype)],
        input_output_aliases={0: 0, 1: 1},
        compiler_params=pltpu.CompilerParams(has_side_effects=pltpu.SideEffectType.DATAFLOW_SIDE_EFFECTING),
    )(wg, woutg, send, recv, after)


def weight_pass_start(wg, woutg):
    def body(wg_ref, woutg_ref, send, recv, wg_out, woutg_out):
        del wg_out, woutg_out
        sent, _ = _gather_copies((wg_ref, woutg_ref), _round_sems(send, recv, 1))
        for a in range(2):
            for k in _ROUNDS[1]:
                sent(a, k).start()

    n = 2 * len(_ROUNDS[1])
    return pl.pallas_call(
        body, name="weight_pass_start",
        in_specs=[_HBM] * 2,
        out_specs=[_SEM, _SEM, _HBM, _HBM],
        out_shape=[pltpu.SemaphoreType.DMA((n,)), pltpu.SemaphoreType.DMA((n,)),
                   pltpu.HBM(wg.shape, wg.dtype), pltpu.HBM(woutg.shape, woutg.dtype)],
        input_output_aliases={0: 2, 1: 3},
        compiler_params=pltpu.CompilerParams(has_side_effects=pltpu.SideEffectType.DATAFLOW_SIDE_EFFECTING),
    )(wg, woutg)


def weight_pass_last(wg, woutg):
    def body(wg_in, woutg_in, wg_ref, woutg_ref, send, recv):
        del wg_in, woutg_in
        sent, arrival = _gather_copies((wg_ref, woutg_ref), _round_sems(send, recv, 2))
        for a in range(2):
            sent(a, _ROUNDS[2][0]).start()
        _wait_copies(sent, arrival, _ROUNDS[2])

    any_spec = pl.BlockSpec(memory_space=pl.ANY)
    n = 2 * len(_ROUNDS[2])
    return pl.pallas_call(
        body, name="weight_pass_last",
        in_specs=[any_spec, any_spec], out_specs=[any_spec, any_spec],
        out_shape=[jax.ShapeDtypeStruct(wg.shape, wg.dtype), jax.ShapeDtypeStruct(woutg.shape, woutg.dtype)],
        scratch_shapes=[pltpu.SemaphoreType.DMA((n,)), pltpu.SemaphoreType.DMA((n,))],
        input_output_aliases={0: 0, 1: 1},
    )(wg, woutg)


def grad_pair_exchange(g_sib, g_out):
    def body(gsib_ref, gout_ref, ra_ref, rb_ref, send, recv):
        pos = _mesh_pos()
        x, y, c = pos
        sibling = (x, y, 1 - c)
        copies = []
        for q in range(4):
            copies.append(_remote(gsib_ref.at[q], ra_ref.at[q], send.at[q], recv.at[q], sibling))
            copies.append(_remote(gout_ref.at[2 * q + (1 - c)], rb_ref.at[q], send.at[4 + q], recv.at[4 + q], sibling))
        for cp in copies:
            cp.start()
        for cp in copies:
            cp.wait_recv()
        for cp in copies:
            cp.wait_send()

    any_spec = pl.BlockSpec(memory_space=pl.ANY)
    return pl.pallas_call(
        body, name="grad_pair_exchange",
        in_specs=[any_spec, any_spec], out_specs=[any_spec, any_spec],
        out_shape=[jax.ShapeDtypeStruct(g_sib.shape, g_sib.dtype), jax.ShapeDtypeStruct((4,) + g_out.shape[1:], F32)],
        scratch_shapes=[pltpu.SemaphoreType.DMA((8,)), pltpu.SemaphoreType.DMA((8,))],
    )(g_sib, g_out)


def pair_exchange_sum(g_in, g_sib, g_out):
    rows_per = LANE

    def body(gin_ref, gsib_ref, gout_ref, sb_ref, sbo_ref, ra, rb, gbuf, obuf, gob, send, recv, lin, lout, lmisc):
        x, y, c = _mesh_pos()
        sibling = (x, y, 1 - c)
        remote = []
        for q in range(4):
            remote.append(_remote(gsib_ref.at[q], ra.at[q], send.at[q], recv.at[q], sibling))
            remote.append(_remote(gout_ref.at[2 * q + (1 - c)], rb.at[q], send.at[4 + q], recv.at[4 + q], sibling))
        for cp in remote:
            cp.start()
        own_out = [pltpu.make_async_copy(gout_ref.at[2 * q + c], gob.at[q], lmisc.at[q]) for q in range(4)]
        for cp in own_out:
            cp.start()
        load = lambda q: pltpu.make_async_copy(gin_ref.at[2 * q + c], gbuf.at[q % 2], lin.at[q % 2])
        store = lambda q: pltpu.make_async_copy(obuf.at[q % 2], sb_ref.at[q], lout.at[q % 2])
        load(0).start()
        for q in range(4):
            load(q).wait()
            if q + 1 < 4:
                load(q + 1).start()
            remote[2 * q].wait_recv()
            if q >= 2:
                store(q - 2).wait()

            def rows(i, carry, q=q):
                r = pl.ds(pl.multiple_of(i * rows_per, rows_per), rows_per)
                obuf[q % 2, r, :] = (gbuf[q % 2, r, :] + ra[q, r, :].astype(F32)).astype(obuf.dtype)
                return carry

            lax.fori_loop(0, D_MODEL // rows_per, rows, 0)
            store(q).start()
        for q in range(4):
            own_out[q].wait()
            remote[2 * q + 1].wait_recv()
            gob[q] = gob[q] + rb[q]
        last = pltpu.make_async_copy(gob, sbo_ref, lmisc.at[4])
        last.start()
        store(2).wait()
        store(3).wait()
        last.wait()
        for cp in remote:
            cp.wait_send()

    any_spec = pl.BlockSpec(memory_space=pl.ANY)
    blk = g_sib.shape[1:]
    return pl.pallas_call(
        body, name="pair_exchange_sum",
        in_specs=[any_spec] * 3, out_specs=[any_spec] * 2,
        out_shape=[jax.ShapeDtypeStruct(g_sib.shape, _BF), jax.ShapeDtypeStruct((4,) + g_out.shape[1:], F32)],
        scratch_shapes=[pltpu.VMEM(g_sib.shape, g_sib.dtype), pltpu.VMEM((4,) + g_out.shape[1:], F32),
                        pltpu.VMEM((2,) + blk, F32), pltpu.VMEM((2,) + blk, _BF),
                        pltpu.VMEM((4,) + g_out.shape[1:], F32),
                        pltpu.SemaphoreType.DMA((8,)), pltpu.SemaphoreType.DMA((8,)),
                        pltpu.SemaphoreType.DMA((2,)), pltpu.SemaphoreType.DMA((2,)), pltpu.SemaphoreType.DMA((5,))],
        compiler_params=pltpu.CompilerParams(vmem_limit_bytes=VMEM_LIMIT),
    )(g_in, g_sib, g_out)


def pair_sum(g_in, ra, g_out, rb, c_idx):
    tr = D_MODEL

    def body(c_ref, gin_ref, ra_ref, gout_ref, rb_ref, sb_ref, sbo_ref):
        del c_ref
        sb_ref[...] = (gin_ref[...] + ra_ref[...].astype(F32)).astype(sb_ref.dtype)
        sbo_ref[...] = gout_ref[...] + rb_ref[...]

    n_i = D_MODEL // tr
    return pl.pallas_call(
        body, name="pair_sum",
        grid_spec=pltpu.PrefetchScalarGridSpec(
            num_scalar_prefetch=1, grid=(4, n_i),
            in_specs=[pl.BlockSpec((None, tr, SHARD_IN), lambda q, i, c: (2 * q + c[0], i, 0)),
                      pl.BlockSpec((None, tr, SHARD_IN), lambda q, i, c: (q, i, 0)),
                      pl.BlockSpec((None, SHARD_OUT // n_i, D_MODEL), lambda q, i, c: (2 * q + c[0], i, 0)),
                      pl.BlockSpec((None, SHARD_OUT // n_i, D_MODEL), lambda q, i, c: (q, i, 0))],
            out_specs=[pl.BlockSpec((None, tr, SHARD_IN), lambda q, i, c: (q, i, 0)),
                       pl.BlockSpec((None, SHARD_OUT // n_i, D_MODEL), lambda q, i, c: (q, i, 0))]),
        out_shape=[jax.ShapeDtypeStruct(ra.shape, _BF), jax.ShapeDtypeStruct(rb.shape, F32)],
        compiler_params=_params(("arbitrary", "arbitrary")),
    )(c_idx, g_in, ra, g_out, rb)


_HBM = pl.BlockSpec(memory_space=pltpu.HBM)
_SEM = pl.BlockSpec(memory_space=pltpu.SEMAPHORE)
_N_CHIP_COPIES = 6


def _chip_copies(sb_ref, sbo_ref, rc_ref, rco_ref, send, recv):
    pos = _mesh_pos()
    copies = []
    for a, (src, dst) in enumerate(((sb_ref, rc_ref), (sbo_ref, rco_ref))):
        for j, chip in enumerate(_other_chips(pos)):
            copies.append(_remote(src.at[2 * chip[0] + chip[1]], dst.at[j], send.at[3 * a + j], recv.at[3 * a + j],
                                  (*chip, pos[2])))
    return copies


def grad_chip_start(sb, sbo):
    def body(sb_ref, sbo_ref, rc_ref, rco_ref, send, recv, sb_thru, sbo_thru, rc_thru, rco_thru, token):
        del sb_thru, sbo_thru, rc_thru, rco_thru
        for cp in _chip_copies(sb_ref, sbo_ref, rc_ref, rco_ref, send, recv):
            cp.start()
        token[...] = jnp.zeros_like(token)

    hbm = lambda a: pltpu.with_memory_space_constraint(a, pltpu.HBM)
    rc = lax.empty((3,) + sb.shape[1:], sb.dtype)
    rco = lax.empty((3,) + sbo.shape[1:], sbo.dtype)
    return pl.pallas_call(
        body, name="grad_chip_start",
        in_specs=[_HBM] * 4,
        out_specs=[_SEM, _SEM, _HBM, _HBM, _HBM, _HBM, pl.BlockSpec(memory_space=pltpu.VMEM)],
        out_shape=[pltpu.SemaphoreType.DMA((_N_CHIP_COPIES,)), pltpu.SemaphoreType.DMA((_N_CHIP_COPIES,)),
                   pltpu.HBM(sb.shape, sb.dtype), pltpu.HBM(sbo.shape, sbo.dtype),
                   pltpu.HBM(rc.shape, rc.dtype), pltpu.HBM(rco.shape, rco.dtype),
                   jax.ShapeDtypeStruct((8, LANE), F32)],
        input_output_aliases={0: 2, 1: 3, 2: 4, 3: 5},
        compiler_params=pltpu.CompilerParams(has_side_effects=pltpu.SideEffectType.DATAFLOW_SIDE_EFFECTING),
    )(hbm(sb), hbm(sbo), hbm(rc), hbm(rco))


def grad_chip_wait(send, recv, sb, sbo, rc, rco, after):
    def body(sb_ref, sbo_ref, rc_ref, rco_ref, send, recv, after_ref, sb_o, sbo_o, rc_o, rco_o):
        del after_ref, sb_o, sbo_o, rc_o, rco_o
        for cp in _chip_copies(sb_ref, sbo_ref, rc_ref, rco_ref, send, recv):
            cp.wait_send()
            cp.wait_recv()

    return pl.pallas_call(
        body, name="grad_chip_wait",
        in_specs=[_HBM] * 4 + [_SEM, _SEM, pl.BlockSpec(memory_space=pl.ANY)],
        out_specs=[_HBM] * 4,
        out_shape=[pltpu.HBM(sb.shape, sb.dtype), pltpu.HBM(sbo.shape, sbo.dtype),
                   pltpu.HBM(rc.shape, rc.dtype), pltpu.HBM(rco.shape, rco.dtype)],
        input_output_aliases={0: 0, 1: 1, 2: 2, 3: 3},
        compiler_params=pltpu.CompilerParams(has_side_effects=pltpu.SideEffectType.DATAFLOW_SIDE_EFFECTING),
    )(sb, sbo, rc, rco, send, recv, after)


def pack_gather(vec_mid, vec_ada, dlb):
    def body(mid_ref, ada_ref, dlb_ref, packs_ref, dmod_ref, psend, precv):
        pos = _mesh_pos()
        me = _lin(pos)
        mid = lambda r: mid_ref[r:r + 1, :]
        rows = [ada_ref[0:1, :], dlb_ref[...], mid(MID_HG_G), mid(MID_RET_G), mid(MID_FINAL_G),
                ada_ref[2:3, :], ada_ref[1:2, :], mid(MID_GATE), mid(MID_LOSS)]
        packs_ref[me] = jnp.zeros((PACK_ROWS, D_MODEL), F32)
        for n, row in enumerate(rows):
            packs_ref[me, n:n + 1, :] = row
        peers = [_xor_peer(pos, k) for k in range(1, N_DEV)]
        gather = [_remote(packs_ref.at[me], packs_ref.at[me], psend.at[n], precv.at[n], p) for n, p in enumerate(peers)]
        for cp in gather:
            cp.start()
        dmod_ref[...] = jnp.zeros_like(dmod_ref)
        for n, p in enumerate(peers):
            _remote(packs_ref.at[_lin(p)], packs_ref.at[_lin(p)], psend.at[n], precv.at[n], p).wait_recv()
        for cp in gather:
            cp.wait_send()
        for part, r in enumerate((ROW_SHIFT, ROW_SCALE, ROW_GATE)):
            for d in range(N_DEV):
                dmod_ref[d:d + 1, part * D_MODEL:(part + 1) * D_MODEL] = packs_ref[d, r:r + 1, :]

    vmem = pl.BlockSpec(memory_space=pltpu.VMEM)
    return pl.pallas_call(
        body, name="pack_gather", in_specs=[vmem] * 3, out_specs=[vmem] * 2,
        out_shape=[jax.ShapeDtypeStruct((N_DEV, PACK_ROWS, D_MODEL), F32),
                   jax.ShapeDtypeStruct((LANE, 3 * D_MODEL), F32)],
        scratch_shapes=[pltpu.SemaphoreType.DMA((N_DEV - 1,)), pltpu.SemaphoreType.DMA((N_DEV - 1,))],
    )(vec_mid, vec_ada, dlb)


def _adamw(w, g, m, v):
    m = ADAM_B1 * m + (1.0 - ADAM_B1) * g
    v = ADAM_B2 * v + (1.0 - ADAM_B2) * (g * g)
    m_hat = m / (1.0 - ADAM_B1 ** ADAM_STEP)
    v_hat = v / (1.0 - ADAM_B2 ** ADAM_STEP)
    delta = -ADAM_LR * (m_hat / (jnp.sqrt(v_hat) + ADAM_EPS) + ADAM_WD * w)
    return delta, m, v


def adam_shard(chip_idx, own, parts, w, m, v, name):
    rows, cols = w.shape
    tr = min(rows, 256)

    def body(chip_ref, p0, p1, p2, p3, w_ref, m_ref, v_ref, g_ref, d_ref, nm_ref, nv_ref):
        del chip_ref
        g = ((p0[...].astype(F32) + p1[...].astype(F32)) + p2[...].astype(F32)) + p3[...].astype(F32)
        g_ref[...] = g
        d_ref[...], nm_ref[...], nv_ref[...] = _adamw(w_ref[...], g, m_ref[...], v_ref[...])

    part = lambda q: pl.BlockSpec((None, tr, cols), lambda i, chip, q=q: (q, i, 0))
    tile = pl.BlockSpec((tr, cols), lambda i, chip: (i, 0))
    return pl.pallas_call(
        body, name=name,
        grid_spec=pltpu.PrefetchScalarGridSpec(
            num_scalar_prefetch=1, grid=(rows // tr,),
            in_specs=[pl.BlockSpec((None, tr, cols), lambda i, chip: (chip[0], i, 0)), part(0), part(1), part(2),
                      tile, tile, tile],
            out_specs=[tile] * 4),
        out_shape=[jax.ShapeDtypeStruct(w.shape, F32)] * 4,
        compiler_params=_params(("arbitrary",)),
    )(chip_idx, own, parts, parts, parts, w, m, v)


def adam_ada(sc_t, dmod_all, me_idx, w, m, v):
    def body(me_ref, sc_ref, dm_ref, w_ref, m_ref, v_ref, g_ref, d_ref, nm_ref, nv_ref):
        del me_ref
        g = _dot_f32(sc_ref[...], dm_ref[...])
        g_ref[...] = g
        d_ref[...], nm_ref[...], nv_ref[...] = _adamw(w_ref[...], g, m_ref[...], v_ref[...])

    full = pl.BlockSpec(w.shape, lambda i, me: (0, 0))
    return pl.pallas_call(
        body, name="adam_ada",
        grid_spec=pltpu.PrefetchScalarGridSpec(
            num_scalar_prefetch=1, grid=(1,),
            in_specs=[pl.BlockSpec(sc_t.shape, lambda i, me: (0, 0)),
                      pl.BlockSpec((LANE, SHARD_ADA), lambda i, me: (0, me[0])), full, full, full],
            out_specs=[full] * 4),
        out_shape=[jax.ShapeDtypeStruct(w.shape, F32)] * 4,
        compiler_params=_params(("arbitrary",)),
    )(me_idx, sc_t, dmod_all, w, m, v)


def adam_vectors(packs, lb, params, ms, vs):
    n = len(params)

    def body(*refs):
        packs_ref, lb_ref = refs[0], refs[1]
        w_refs, m_refs, v_refs = refs[2:2 + n], refs[2 + n:2 + 2 * n], refs[2 + 2 * n:2 + 3 * n]
        loss_ref = refs[2 + 3 * n]
        outs = refs[3 + 3 * n:3 + 7 * n]
        tot_ref = refs[3 + 7 * n]
        tot = packs_ref[0]
        for d in range(1, N_DEV):
            tot = tot + packs_ref[d]
        tot_ref[...] = tot
        row = lambda r: tot_ref[r:r + 1, :]
        lbv = lb_ref[...]
        dl0 = row(ROW_LB) * lbv * (1.0 - lbv)
        grads = [[row(ROW_NORM_G)],
                 [jnp.concatenate([row(ROW_SHIFT), row(ROW_SCALE), row(ROW_GATE)], axis=1)],
                 [dl0, -dl0],
                 [row(ROW_HG_G)], [row(ROW_RET_G)], [row(ROW_FINAL_G)]]
        loss_ref[...] = tot_ref[ROW_LOSS:ROW_LOSS + 1, 0:LANE]
        for j, g_rows in enumerate(grads):
            for r, g in enumerate(g_rows):
                rs = slice(r, r + 1)
                d, nm, nv = _adamw(w_refs[j][rs, :], g, m_refs[j][rs, :], v_refs[j][rs, :])
                outs[4 * j][rs, :] = g
                outs[4 * j + 1][rs, :] = d
                outs[4 * j + 2][rs, :] = nm
                outs[4 * j + 3][rs, :] = nv

    vmem = pl.BlockSpec(memory_space=pltpu.VMEM)
    out_shape = [jax.ShapeDtypeStruct((1, LANE), F32)]
    for w in params:
        out_shape += [jax.ShapeDtypeStruct(w.shape, F32)] * 4
    return pl.pallas_call(
        body, name="adam_vectors", in_specs=[vmem] * (2 + 3 * n), out_specs=[vmem] * len(out_shape),
        out_shape=out_shape, scratch_shapes=[pltpu.VMEM((PACK_ROWS, D_MODEL), F32)],
    )(packs, lb, *params, *ms, *vs)


def kernel(x, c, norm_g, w_ada, b_ada, w_in, hg_lb_logits, hg_norm_g, ret_norm_g, w_out, final_g, loss_target, m_norm_g, m_w_ada, m_b_ada, m_w_in, m_hg_lb_logits, m_hg_norm_g, m_ret_norm_g, m_w_out, m_final_g, v_norm_g, v_w_ada, v_b_ada, v_w_in, v_hg_lb_logits, v_hg_norm_g, v_ret_norm_g, v_w_out, v_final_g):
    pos = _mesh_pos()
    me_idx = jnp.reshape(_lin(pos), (1,)).astype(jnp.int32)
    c_idx = jnp.reshape(pos[2], (1,)).astype(jnp.int32)
    vec = lambda a: a.reshape(1, D_MODEL)

    mod, scall, lb, wtg, woutg = pre_exchange(c, w_ada[0], b_ada, hg_lb_logits, w_in[0], w_out[0])
    chip_idx = jnp.reshape(2 * pos[0] + pos[1], (1,)).astype(jnp.int32)
    gather_send, gather_recv, wtg, woutg, gather_token = weight_gather_start(wtg, woutg)
    mod = mod + gather_token[:1, :1]

    other_chips = jnp.stack([2 * cx + cy for cx, cy in _other_chips(pos)]).astype(jnp.int32)

    def project(h):
        t_len = h.shape[0]
        flat_in = lambda a: a.reshape(D_IN, D_MODEL)
        first = (gather_send, gather_recv)
        wg, wog = weight_gather_wait("weight_wait_sibling", 0, (0,), *first, wtg, woutg, h)
        pb = proj_forward(h, flat_in(wg), chip_idx, t_len, "proj_fwd_own")
        wg, wog = weight_gather_wait("weight_wait_neighbours", 0, (1, 2), *first, wg, wog, pb)
        *second, wg, wog = weight_pass_start(wg, wog)
        wg, wog = weight_gather_wait("weight_wait_passed_sibling", 1, (4, 5), *second, wg, wog, pb)
        pb = proj_forward(h, flat_in(wg), other_chips[:2], t_len, "proj_fwd_neighbours", pb)
        wg, wog = weight_gather_wait("weight_wait_passed_chip", 1, (3,), *second, wg, wog, pb)
        wg, wog = weight_pass_last(wg, wog)
        pb = proj_forward(h, flat_in(wg), other_chips[2:], t_len, "proj_fwd_diagonal", pb)
        return pb, flat_in(wg), wog.reshape(D_MODEL, D_MODEL)

    def start_exchange(dwin, dwin_sib, dwout):
        dwout = dwout.reshape(N_DEV, SHARD_OUT, D_MODEL)
        sb, sbo = pair_exchange_sum(dwin, dwin_sib, dwout)
        send, recv, sb, sbo, rc, rco, token = grad_chip_start(sb, sbo)
        return token, (send, recv, sb, sbo, rc, rco)

    grad_x, _, _, vec_mid, vec_ada, dlb, pending = device_step(
        x[0], loss_target[0], mod, lb, project, norm_g, hg_norm_g, ret_norm_g, vec(final_g), c_idx, start_exchange)
    packs, dmod_all = pack_gather(vec_mid, vec_ada, dlb)
    sc_t = jnp.pad(scall.T, ((0, 0), (0, LANE - N_DEV)))
    g_ada, d_ada, nm_ada, nv_ada = adam_ada(sc_t, dmod_all, me_idx, w_ada[0], m_w_ada[0], v_w_ada[0])
    small = adam_vectors(
        packs, lb,
        (norm_g, b_ada, hg_lb_logits, hg_norm_g, ret_norm_g, vec(final_g)),
        (m_norm_g, m_b_ada, m_hg_lb_logits, m_hg_norm_g, m_ret_norm_g, vec(m_final_g)),
        (v_norm_g, v_b_ada, v_hg_lb_logits, v_hg_norm_g, v_ret_norm_g, vec(v_final_g)))
    loss = small[0][0, 0]
    sb, sbo, rc, rco = grad_chip_wait(*pending, small[0])
    g_in, d_in, nm_in, nv_in = adam_shard(chip_idx, sb, rc, w_in[0], m_w_in[0], v_w_in[0], "adam_w_in")
    g_out, d_out, nm_out, nv_out = adam_shard(chip_idx, sbo, rco, w_out[0], m_w_out[0], v_w_out[0], "adam_w_out")
    (g_ng, d_ng, nm_ng, nv_ng), (g_b, d_b, nm_b, nv_b), (g_lb, d_lb, nm_lb, nv_lb), (g_hg, d_hg, nm_hg, nv_hg), \
        (g_rg, d_rg, nm_rg, nv_rg), (g_fg, d_fg, nm_fg, nv_fg) = [small[1 + 4 * j:5 + 4 * j] for j in range(6)]
    flat = lambda a: a.reshape(D_MODEL)

    def group(ng, ada, b, win, lbl, hg, rg, wo, fg):
        return (ng, ada[None], b, win[None], lbl, hg, rg, wo[None], flat(fg))

    return (loss, grad_x[None],
            *group(g_ng, g_ada, g_b, g_in, g_lb, g_hg, g_rg, g_out, g_fg),
            *group(d_ng, d_ada, d_b, d_in, d_lb, d_hg, d_rg, d_out, d_fg),
            *group(nm_ng, nm_ada, nm_b, nm_in, nm_lb, nm_hg, nm_rg, nm_out, nm_fg),
            *group(nv_ng, nv_ada, nv_b, nv_in, nv_lb, nv_hg, nv_rg, nv_out, nv_fg))
```
